```python
import math
import jax
import jax.numpy as jnp
from jax import lax
import numpy as np

D_MODEL = 1024
BATCH = 4
SEQ = 4096
DEPTH = 2

MEM_LEN = 256
EPS = 1e-6

MLSTM_HEADS = 4
MLSTM_DH = 128
MLSTM_W = MLSTM_HEADS * MLSTM_DH
MLSTM_CHUNK = 64
CONV_WIDTH = 4

ATTN_PATTERNS = ((128, 1), (512, 4), (2048, 16))
N_ATTN_GROUPS = len(ATTN_PATTERNS)
HEADS_PER_GROUP = 4
ATTN_HEADS = N_ATTN_GROUPS * HEADS_PER_GROUP
ATTN_DH = 64
ATTN_W = ATTN_HEADS * ATTN_DH
ATTN_BLOCK = 128

GMLP_GROUPS = 4
GMLP_GC = 128
GMLP_W = GMLP_GROUPS * GMLP_GC
GMLP_CHUNK = 128

REL_BUCKETS = 32
REL_MAX_DIST = 2048

XATTN_HEADS = 4
XATTN_DH = 128
XATTN_W = XATTN_HEADS * XATTN_DH

N_EXPERTS = 16
N_EXPERT_GROUPS = 4
EXPERTS_PER_GROUP = N_EXPERTS // N_EXPERT_GROUPS
TOP_K = 2
D_FF_EXPERT = 512

N_BRANCH = 3
SPLIT_SIZES = (MLSTM_W, MLSTM_W, MLSTM_W, MLSTM_W, MLSTM_HEADS, MLSTM_HEADS,
               ATTN_W, ATTN_W, ATTN_W, GMLP_W, GMLP_W, N_BRANCH * D_MODEL)
SPLIT_POINTS = tuple(int(p) for p in np.cumsum(SPLIT_SIZES)[:-1])
N_IN = int(sum(SPLIT_SIZES))

kernel_name = 'hybrid_mlstm_dilated_gmlp_moe'


def rms_norm(x, gain):
    xf = x.astype(jnp.float32)
    y = xf * lax.rsqrt(jnp.mean(xf * xf, axis=-1, keepdims=True) + EPS)
    return (y * gain.astype(jnp.float32)).astype(x.dtype)


def causal_depthwise_conv(x, w):
    k = w.shape[0]
    xp = jnp.pad(x, ((0, 0), (k - 1, 0), (0, 0)))
    return lax.conv_general_dilated(xp, w[:, None, :].astype(x.dtype), window_strides=(1,), padding='VALID',
                                    dimension_numbers=('NWC', 'WIO', 'NWC'), feature_group_count=x.shape[-1])


def mlstm_chunkwise(q, k, v, i_pre, f_pre):
    b_, h_, s_, dh = q.shape
    L = MLSTM_CHUNK
    nc = s_ // L
    log_f = jax.nn.log_sigmoid(f_pre)

    def chunks(t):
        return jnp.moveaxis(t.reshape(b_, h_, nc, L, *t.shape[3:]), 2, 0)

    causal = jnp.tril(jnp.ones((L, L), dtype=bool))

    def step(carry, inp):
        c_st, n_st, m_st = carry
        qj, kj, vj, ij, fj = inp
        b = jnp.cumsum(fj, axis=-1)
        d_mat = jnp.where(causal, b[..., :, None] - b[..., None, :] + ij[..., None, :], -jnp.inf)
        inter = b + m_st[..., None]
        m_t = jnp.maximum(inter, jnp.max(d_mat, axis=-1))
        w_intra = jnp.exp(d_mat - m_t[..., None])
        w_inter = jnp.exp(inter - m_t)
        s = jnp.einsum('bhtd,bhrd->bhtr', qj, kj) * w_intra
        num = jnp.einsum('bhtr,bhre->bhte', s, vj) + w_inter[..., None] * jnp.einsum('bhtd,bhde->bhte', qj, c_st)
        den = jnp.sum(s, axis=-1) + w_inter * jnp.einsum('bhtd,bhd->bht', qj, n_st)
        h = num / jnp.maximum(jnp.abs(den), jnp.exp(-m_t))[..., None]
        b_last = b[..., -1]
        dec = b_last[..., None] - b + ij
        m_new = jnp.maximum(b_last + m_st, jnp.max(dec, axis=-1))
        w_k = jnp.exp(dec - m_new[..., None])
        w_c = jnp.exp(b_last + m_st - m_new)
        c_new = w_c[..., None, None] * c_st + jnp.einsum('bhr,bhrd,bhre->bhde', w_k, kj, vj)
        n_new = w_c[..., None] * n_st + jnp.einsum('bhr,bhrd->bhd', w_k, kj)
        return (c_new, n_new, m_new), h

    init = (jnp.zeros((b_, h_, dh, dh), jnp.float32), jnp.zeros((b_, h_, dh), jnp.float32),
            jnp.zeros((b_, h_), jnp.float32))
    _, hs = lax.scan(step, init, (chunks(q), chunks(k), chunks(v), chunks(i_pre), chunks(log_f)))
    return jnp.moveaxis(hs, 0, 2).reshape(b_, h_, s_, dh)


def mixer_mlstm(mq, mk, mv, mo, mi, mf, conv_w, gate_b, norm_g):
    b_, s_, _ = mq.shape
    qk = jax.nn.silu(causal_depthwise_conv(jnp.concatenate([mq, mk], axis=-1), conv_w))
    q, k = jnp.split(qk, 2, axis=-1)

    def heads(t):
        return t.reshape(b_, s_, MLSTM_HEADS, MLSTM_DH).transpose(0, 2, 1, 3).astype(jnp.float32)

    i_pre = (mi + gate_b[:MLSTM_HEADS]).astype(jnp.float32).transpose(0, 2, 1)
    f_pre = (mf + gate_b[MLSTM_HEADS:]).astype(jnp.float32).transpose(0, 2, 1)
    h = mlstm_chunkwise(heads(q), heads(k) * (MLSTM_DH ** -0.5), heads(mv), i_pre, f_pre)
    h = rms_norm(h.transpose(0, 2, 1, 3), norm_g.reshape(MLSTM_HEADS, MLSTM_DH))
    return (h.reshape(b_, s_, MLSTM_W) * jax.nn.sigmoid(mo.astype(jnp.float32))).astype(mq.dtype)


def rel_bucket(n):
    max_exact = REL_BUCKETS // 2
    nf = jnp.maximum(n, 1).astype(jnp.float32)
    log_b = max_exact + (jnp.log(nf / max_exact) / math.log(REL_MAX_DIST / max_exact)
                         * (REL_BUCKETS - max_exact)).astype(jnp.int32)
    return jnp.where(n < max_exact, n, jnp.minimum(log_b, REL_BUCKETS - 1))


def dilated_window_attention(q, k, v, bias_steps, dilation, steps):
    b_, hg, s_, dh = q.shape
    ls = s_ // dilation
    nb = -(-ls // ATTN_BLOCK)
    lp = nb * ATTN_BLOCK

    def gather_stride(t):
        t = t.reshape(b_, hg, ls, dilation, dh).transpose(0, 1, 3, 2, 4)
        t = jnp.pad(t, ((0, 0), (0, 0), (0, 0), (0, lp - ls), (0, 0)))
        return t.reshape(b_, hg, dilation, nb, ATTN_BLOCK, dh)

    def with_previous_block(t):
        prev = jnp.pad(t, ((0, 0), (0, 0), (0, 0), (1, 0), (0, 0), (0, 0)))[:, :, :, :-1]
        return jnp.concatenate([prev, t], axis=4)

    qb = gather_stride(q)
    kb = with_previous_block(gather_stride(k))
    vb = with_previous_block(gather_stride(v))
    qi = jnp.arange(ATTN_BLOCK)[:, None]
    ki = jnp.arange(2 * ATTN_BLOCK)[None, :]
    dist = ATTN_BLOCK + qi - ki
    local_ok = (dist >= 0) & (dist <= steps)
    first_ok = (jnp.arange(nb) > 0)[:, None, None] | (ki >= ATTN_BLOCK)[None]
    valid = local_ok[None] & first_ok
    bias = bias_steps[:, jnp.clip(dist, 0, steps)]
    logits = jnp.einsum('bhrnqd,bhrnkd->bhrnqk', qb, kb).astype(jnp.float32) * (ATTN_DH ** -0.5)
    logits = logits + bias.astype(jnp.float32)[None, :, None, None]
    logits = jnp.where(valid[None, None, None], logits, -jnp.inf)
    m = jnp.max(logits, axis=-1, keepdims=True)
    p = jnp.exp(logits - m)
    l = jnp.sum(p, axis=-1, keepdims=True)
    o = jnp.einsum('bhrnqk,bhrnkd->bhrnqd', p, vb.astype(jnp.float32)) / l
    lse = m + jnp.log(l)

    def scatter_back(t):
        c = t.shape[-1]
        t = t.reshape(b_, hg, dilation, lp, c)[:, :, :, :ls]
        return t.transpose(0, 1, 3, 2, 4).reshape(b_, hg, s_, c)

    return scatter_back(o), scatter_back(lse)[..., 0]


def mixer_dilated_attention(aq, ak, av, qk_g, rel_bias):
    b_, s_, _ = aq.shape

    def heads(t):
        return t.reshape(b_, s_, ATTN_HEADS, ATTN_DH)

    q = rms_norm(heads(aq), qk_g[0]).transpose(0, 2, 1, 3)
    k = rms_norm(heads(ak), qk_g[1]).transpose(0, 2, 1, 3)
    v = heads(av).transpose(0, 2, 1, 3)
    outs, lses = [], []
    for g, (window, dilation) in enumerate(ATTN_PATTERNS):
        steps = window // dilation
        hs = slice(g * HEADS_PER_GROUP, (g + 1) * HEADS_PER_GROUP)
        bias_steps = rel_bias[rel_bucket(jnp.arange(steps + 1) * dilation)][:, hs].T
        o, lse = dilated_window_attention(q[:, hs], k[:, hs], v[:, hs], bias_steps, dilation, steps)
        outs.append(o)
        lses.append(lse)
    alpha = jax.nn.softmax(jnp.stack(lses), axis=0)
    o = jnp.stack(outs) * alpha[..., None]
    return o.transpose(1, 3, 0, 2, 4).reshape(b_, s_, ATTN_W).astype(aq.dtype)


def mixer_spatial_gating(gu, gv, norm_g, w_s, b_s):
    b_, s_, _ = gu.shape
    nch = s_ // GMLP_CHUNK
    u = jax.nn.gelu(gu).reshape(b_, nch, GMLP_CHUNK, GMLP_GROUPS, GMLP_GC)
    v = rms_norm(jax.nn.gelu(gv).reshape(b_, nch, GMLP_CHUNK, GMLP_GROUPS, GMLP_GC),
                 norm_g.reshape(GMLP_GROUPS, GMLP_GC))
    w_causal = jnp.where(jnp.tril(jnp.ones((GMLP_CHUNK, GMLP_CHUNK), dtype=bool)), w_s, 0.0)
    mixed = jnp.einsum('gts,bnsgc->bntgc', w_causal.astype(v.dtype), v) + b_s.T.astype(v.dtype)[None, None, :, :, None]
    return (u * mixed).reshape(b_, s_, GMLP_W)


def memory_cross_attention(h, mem_n, w_q, w_kv, qk_g, w_o):
    b_, s_, _ = h.shape
    m_ = mem_n.shape[1]
    q = rms_norm((h @ w_q).reshape(b_, s_, XATTN_HEADS, XATTN_DH), qk_g[0])
    k, v = jnp.split(mem_n @ w_kv, 2, axis=-1)
    k = rms_norm(k.reshape(b_, m_, XATTN_HEADS, XATTN_DH), qk_g[1])
    v = v.reshape(b_, m_, XATTN_HEADS, XATTN_DH)
    logits = jnp.einsum('bshd,bmhd->bhsm', q, k).astype(jnp.float32) * (XATTN_DH ** -0.5)
    p = jax.nn.softmax(logits, axis=-1).astype(v.dtype)
    o = jnp.einsum('bhsm,bmhd->bshd', p, v).reshape(b_, s_, XATTN_W)
    return o @ w_o


def grouped_moe(h, router_w, router_b, w_gate, w_up, w_down):
    logits = (h @ router_w).astype(jnp.float32) + router_b.astype(jnp.float32)
    probs = jax.nn.softmax(logits, axis=-1)
    grouped = probs.reshape(*probs.shape[:-1], N_EXPERT_GROUPS, EXPERTS_PER_GROUP)
    group_score = jnp.sum(lax.top_k(grouped, TOP_K)[0], axis=-1)
    best = jnp.argmax(group_score, axis=-1)
    in_group = (best[..., None] == jnp.arange(N_EXPERT_GROUPS))[..., None]
    masked = jnp.where(in_group, grouped, -1.0).reshape(probs.shape)
    top_p, top_i = lax.top_k(masked, TOP_K)
    top_w = top_p / jnp.sum(top_p, axis=-1, keepdims=True)
    gates = jnp.sum(jax.nn.one_hot(top_i, N_EXPERTS, dtype=jnp.float32) * top_w[..., None], axis=-2).astype(h.dtype)
    out = jnp.zeros_like(h)
    for e in range(N_EXPERTS):
        a = jax.nn.silu(h @ w_gate[e]) * (h @ w_up[e])
        out = out + gates[..., e:e + 1] * (a @ w_down[e])
    return out


def setup_inputs(seed: int = 0) -> dict:
    key = jax.random.key(seed)
    ks = iter(jax.random.split(key, 40))

    def nrm(shape, scale):
        return jax.random.normal(next(ks), shape, jnp.float32) * scale

    def gain(shape):
        return 1.0 + nrm(shape, 0.05)

    L = DEPTH
    forget_b = jnp.broadcast_to(jnp.linspace(3.0, 6.0, MLSTM_HEADS), (L, MLSTM_HEADS))
    mlstm_gate_b = jnp.concatenate([nrm((L, MLSTM_HEADS), 0.5) - 2.0,
                                    forget_b + nrm((L, MLSTM_HEADS), 0.1)], axis=-1)
    return {
        'x': nrm((BATCH, SEQ, D_MODEL), 1.0),
        'mem': nrm((BATCH, MEM_LEN, D_MODEL), 1.0),
        'norm_mix': gain((L, D_MODEL)),
        'w_in': nrm((L, D_MODEL, N_IN), D_MODEL ** -0.5),
        'mlstm_conv': nrm((L, CONV_WIDTH, 2 * MLSTM_W), CONV_WIDTH ** -0.5),
        'mlstm_gate_b': mlstm_gate_b,
        'mlstm_norm': gain((L, MLSTM_W)),
        'attn_qk_norm': gain((L, 2, ATTN_DH)),
        'gmlp_norm': gain((L, GMLP_W)),
        'gmlp_ws': nrm((L, GMLP_GROUPS, GMLP_CHUNK, GMLP_CHUNK), GMLP_CHUNK ** -0.5),
        'gmlp_bs': 1.0 + nrm((L, GMLP_GROUPS, GMLP_CHUNK), 0.1),
        'w_branch_a': nrm((L, MLSTM_W, D_MODEL), MLSTM_W ** -0.5),
        'w_branch_b': nrm((L, ATTN_W, D_MODEL), ATTN_W ** -0.5),
        'w_branch_c': nrm((L, GMLP_W, D_MODEL), GMLP_W ** -0.5),
        'w_out': nrm((L, D_MODEL, D_MODEL), 0.5 * D_MODEL ** -0.5),
        'rel_bias': nrm((REL_BUCKETS, ATTN_HEADS), 0.5),
        'norm_xattn': gain((L, D_MODEL)),
        'norm_mem': gain((L, D_MODEL)),
        'w_xq': nrm((L, D_MODEL, XATTN_W), D_MODEL ** -0.5),
        'w_xkv': nrm((L, D_MODEL, 2 * XATTN_W), D_MODEL ** -0.5),
        'xattn_qk_norm': gain((L, 2, XATTN_DH)),
        'w_xo': nrm((L, XATTN_W, D_MODEL), 0.5 * XATTN_W ** -0.5),
        'norm_ffn': gain((L, D_MODEL)),
        'router_w': nrm((D_MODEL, N_EXPERTS), D_MODEL ** -0.5),
        'router_b': nrm((N_EXPERTS,), 0.01),
        'w_expert_gate': nrm((L, N_EXPERTS, D_MODEL, D_FF_EXPERT), D_MODEL ** -0.5),
        'w_expert_up': nrm((L, N_EXPERTS, D_MODEL, D_FF_EXPERT), D_MODEL ** -0.5),
        'w_expert_down': nrm((L, N_EXPERTS, D_FF_EXPERT, D_MODEL), 0.5 * D_FF_EXPERT ** -0.5),
    }


def reference(x, mem, norm_mix, w_in, mlstm_conv, mlstm_gate_b, mlstm_norm, attn_qk_norm, gmlp_norm,
              gmlp_ws, gmlp_bs, w_branch_a, w_branch_b, w_branch_c, w_out, rel_bias, norm_xattn, norm_mem,
              w_xq, w_xkv, xattn_qk_norm, w_xo, norm_ffn, router_w, router_b, w_expert_gate, w_expert_up,
              w_expert_down):
    b_, s_, _ = x.shape
    for l in range(DEPTH):
        h = rms_norm(x, norm_mix[l])
        (mq, mk, mv, mo, mi, mf, aq, ak, av, gu, gv, gate_pre) = jnp.split(h @ w_in[l], SPLIT_POINTS, axis=-1)
        ya = mixer_mlstm(mq, mk, mv, mo, mi, mf, mlstm_conv[l], mlstm_gate_b[l], mlstm_norm[l])
        yb = mixer_dilated_attention(aq, ak, av, attn_qk_norm[l], rel_bias)
        yc = mixer_spatial_gating(gu, gv, gmlp_norm[l], gmlp_ws[l], gmlp_bs[l])
        g = jax.nn.sigmoid(gate_pre.reshape(b_, s_, N_BRANCH, D_MODEL))
        merged = (g[:, :, 0] * (ya @ w_branch_a[l]) + g[:, :, 1] * (yb @ w_branch_b[l])
                  + g[:, :, 2] * (yc @ w_branch_c[l]))
        x = x + merged @ w_out[l]
        x = x + memory_cross_attention(rms_norm(x, norm_xattn[l]), rms_norm(mem, norm_mem[l]),
                                       w_xq[l], w_xkv[l], xattn_qk_norm[l], w_xo[l])
        x = x + grouped_moe(rms_norm(x, norm_ffn[l]), router_w, router_b,
                            w_expert_gate[l], w_expert_up[l], w_expert_down[l])
    return x
```

```python
import functools
import math

import jax
import jax.numpy as jnp
import numpy as np
from jax import lax
from jax.experimental import pallas as pl
from jax.experimental.pallas import tpu as pltpu

F32 = jnp.float32
BF16 = jnp.bfloat16

EPS = 1e-6
NEG = -1e30

MLSTM_HEADS = 4
MLSTM_DH = 128
MLSTM_W = MLSTM_HEADS * MLSTM_DH
CONV_WIDTH = 4
MLSTM_BLOCK = 128

ATTN_PATTERNS = ((128, 1), (512, 4), (2048, 16))
HEADS_PER_GROUP = 4
ATTN_DH = 64
ATTN_GW = HEADS_PER_GROUP * ATTN_DH
ATTN_W = len(ATTN_PATTERNS) * ATTN_GW
ATTN_BLOCK = 128
REL_BUCKETS = 32
REL_MAX_DIST = 2048

GMLP_GROUPS = 4
GMLP_GC = 128
GMLP_W = GMLP_GROUPS * GMLP_GC
GMLP_CHUNK = 128

XATTN_HEADS = 4
XATTN_DH = 128
XATTN_W = XATTN_HEADS * XATTN_DH

N_EXPERTS = 16
N_EXPERT_GROUPS = 4
EXPERTS_PER_GROUP = 4
ROUTER_ROWS = 8 * N_EXPERT_GROUPS

N_BRANCH = 3

OFF_MQ, OFF_MK, OFF_MV, OFF_MO = 0, 512, 1024, 1536
OFF_GU, OFF_GV = 2048, 2560
OFF_GATE = 3072
OFF_AQ, OFF_AK, OFF_AV = 6144, 6912, 7680
OFF_IF = 8448
IF_PAD = 256
N_PROJ = OFF_IF + IF_PAD

VMEM_LIMIT = 48 * 1024 * 1024


def _cparams(*sem):
    return pltpu.CompilerParams(dimension_semantics=sem, vmem_limit_bytes=VMEM_LIMIT)


def _rms(x, gain):
    return x * lax.rsqrt(jnp.mean(x * x, axis=-1, keepdims=True) + EPS) * gain


def _dot(a, b):
    return jnp.dot(a, b, preferred_element_type=F32)


def _dot_nt(a, b):
    return lax.dot_general(a, b, (((1,), (1,)), ((), ())), preferred_element_type=F32)


def _inproj_kernel(x_ref, g_ref, w_ref, o_ref, h_scr):
    @pl.when(pl.program_id(1) == 0)
    def _():
        h_scr[...] = _rms(x_ref[...], g_ref[...]).astype(BF16)

    o_ref[...] = _dot(h_scr[...], w_ref[...]).astype(o_ref.dtype)


def _inproj(x2d, gain, w, *, tm, tn):
    t, d = x2d.shape
    n = w.shape[1]
    return pl.pallas_call(
        _inproj_kernel,
        grid=(t // tm, n // tn),
        in_specs=[pl.BlockSpec((tm, d), lambda i, j: (i, 0)),
                  pl.BlockSpec((1, d), lambda i, j: (0, 0)),
                  pl.BlockSpec((d, tn), lambda i, j: (0, j))],
        out_specs=pl.BlockSpec((tm, tn), lambda i, j: (i, j)),
        out_shape=jax.ShapeDtypeStruct((t, n), BF16),
        scratch_shapes=[pltpu.VMEM((tm, d), BF16)],
        compiler_params=_cparams("parallel", "arbitrary"),
        name="inproj",
    )(x2d, gain, w)


def _log_sigmoid(x):
    return jnp.minimum(x, 0.0) - jnp.log(1.0 + jnp.exp(-jnp.abs(x)))


def _mlstm_kernel(qk_ref, v_ref, og_ref, gc_ref, gr_ref, cw_ref, gbc_ref, gbr_ref, ng_ref, y_ref,
                  xe_scr, c_scr, n_scr, m_scr, *, blk):
    heads, dh, w = MLSTM_HEADS, MLSTM_DH, MLSTM_W
    hp = lax.Precision.HIGHEST

    @pl.when(pl.program_id(1) == 0)
    def _():
        xe_scr[0:8, :] = jnp.zeros((8, 2 * w), F32)
        c_scr[...] = jnp.zeros_like(c_scr)
        n_scr[...] = jnp.zeros_like(n_scr)
        m_scr[...] = jnp.zeros_like(m_scr)

    xe_scr[8:8 + blk, :] = qk_ref[...].astype(F32)
    cw = cw_ref[...]
    conv = cw[CONV_WIDTH - 1:CONV_WIDTH, :] * xe_scr[8:8 + blk, :]
    for j in range(CONV_WIDTH - 1):
        off = 8 - (CONV_WIDTH - 1) + j
        conv = conv + cw[j:j + 1, :] * xe_scr[off:off + blk, :]
    xe_scr[0:8, :] = xe_scr[blk:blk + 8, :]
    qk = conv * jax.nn.sigmoid(conv)

    gcol = gc_ref[...].astype(F32) + gbc_ref[...]
    grow = gr_ref[0] + gbr_ref[...]
    ri = lax.broadcasted_iota(jnp.int32, (blk, blk), 0)
    ci = lax.broadcasted_iota(jnp.int32, (blk, blk), 1)
    causal = ri >= ci
    tril = causal.astype(F32)
    triu = (ri <= ci).astype(F32)
    bcol = jnp.dot(tril, _log_sigmoid(gcol), precision=hp, preferred_element_type=F32)
    brow = jnp.dot(_log_sigmoid(grow), triu, precision=hp, preferred_element_type=F32)

    for h in range(heads):
        sl = slice(h * dh, (h + 1) * dh)
        b_c = bcol[:, heads + h:heads + h + 1]
        i_c = gcol[:, h:h + 1]
        b_r = brow[heads + h:heads + h + 1, :]
        i_r = grow[h:h + 1, :]
        m_st = m_scr[h:h + 1, 0:1]
        c_st = c_scr[h]
        n_st = n_scr[h:h + 1, :]

        d_mat = jnp.where(causal, b_c - b_r + i_r, NEG)
        inter = b_c + m_st
        m_t = jnp.maximum(inter, jnp.max(d_mat, axis=-1, keepdims=True))
        w_intra = jnp.exp(d_mat - m_t)
        w_inter = jnp.exp(inter - m_t)

        q_f = qk[:, sl]
        k_f = qk[:, w + h * dh:w + (h + 1) * dh] * (dh ** -0.5)
        q_b = q_f.astype(BF16)
        k_b = k_f.astype(BF16)
        v_b = v_ref[:, sl]

        s = _dot_nt(q_b, k_b) * w_intra
        num = _dot(s.astype(BF16), v_b) + w_inter * _dot(q_b, c_st.astype(BF16))
        den = jnp.sum(s, axis=-1, keepdims=True) + w_inter * jnp.sum(q_f * n_st, axis=-1, keepdims=True)
        hh = num / jnp.maximum(jnp.abs(den), jnp.exp(-m_t))
        hn = _rms(hh, ng_ref[:, sl])
        y_ref[:, sl] = (hn * jax.nn.sigmoid(og_ref[:, sl].astype(F32))).astype(y_ref.dtype)

        b_last = b_c[blk - 1:blk, :]
        dec = b_last - b_c + i_c
        m_new = jnp.maximum(b_last + m_st, jnp.max(dec, axis=0, keepdims=True))
        w_k = jnp.exp(dec - m_new)
        w_c = jnp.exp(b_last + m_st - m_new)
        kw = k_f * w_k
        c_scr[h] = w_c * c_st + _dot(kw.T.astype(BF16), v_b)
        n_scr[h:h + 1, :] = w_c * n_st + jnp.sum(kw, axis=0, keepdims=True)
        m_scr[h:h + 1, :] = jnp.broadcast_to(m_new, (1, m_scr.shape[1]))


def _mlstm(proj, gates_row, conv_w, gb_col, gb_row, norm_g, *, batch, seq, blk):
    t = proj.shape[0]
    nblk = seq // blk
    w = MLSTM_W
    row = lambda b, c: b * nblk + c
    return pl.pallas_call(
        functools.partial(_mlstm_kernel, blk=blk),
        grid=(batch, nblk),
        in_specs=[pl.BlockSpec((blk, 2 * w), lambda b, c: (row(b, c), OFF_MQ // (2 * w))),
                  pl.BlockSpec((blk, w), lambda b, c: (row(b, c), OFF_MV // w)),
                  pl.BlockSpec((blk, w), lambda b, c: (row(b, c), OFF_MO // w)),
                  pl.BlockSpec((blk, IF_PAD), lambda b, c: (row(b, c), OFF_IF // IF_PAD)),
                  pl.BlockSpec((1, 8, blk), lambda b, c: (b, 0, c)),
                  pl.BlockSpec((CONV_WIDTH, 2 * w), lambda b, c: (0, 0)),
                  pl.BlockSpec((1, IF_PAD), lambda b, c: (0, 0)),
                  pl.BlockSpec((8, 1), lambda b, c: (0, 0)),
                  pl.BlockSpec((1, w), lambda b, c: (0, 0))],
        out_specs=pl.BlockSpec((blk, w), lambda b, c: (row(b, c), 0)),
        out_shape=jax.ShapeDtypeStruct((t, w), BF16),
        scratch_shapes=[pltpu.VMEM((blk + 8, 2 * w), F32),
                        pltpu.VMEM((MLSTM_HEADS, MLSTM_DH, MLSTM_DH), F32),
                        pltpu.VMEM((8, MLSTM_DH), F32),
                        pltpu.VMEM((8, 128), F32)],
        compiler_params=_cparams("parallel", "arbitrary"),
        name="mlstm",
    )(proj, proj, proj, proj, gates_row, conv_w, gb_col, gb_row, norm_g)


def _dattn_kernel(q_ref, kp_ref, kc_ref, vp_ref, vc_ref, bias_ref, gq_ref, gk_ref, o_ref, lse_ref):
    dh = ATTN_DH
    first = pl.program_id(2) == 0
    q = q_ref[0].astype(F32)
    kp = kp_ref[0].astype(F32)
    kc = kc_ref[0].astype(F32)
    for h in range(HEADS_PER_GROUP):
        sl = slice(h * dh, (h + 1) * dh)
        q_h = (_rms(q[:, sl], gq_ref[:, sl]) * (dh ** -0.5)).astype(BF16)
        kp_h = _rms(kp[:, sl], gk_ref[:, sl]).astype(BF16)
        kc_h = _rms(kc[:, sl], gk_ref[:, sl]).astype(BF16)
        bias = bias_ref[h]
        lp = jnp.where(first, NEG, _dot_nt(q_h, kp_h) + bias[:, :ATTN_BLOCK])
        lc = _dot_nt(q_h, kc_h) + bias[:, ATTN_BLOCK:]
        m = jnp.maximum(jnp.max(lp, axis=-1, keepdims=True), jnp.max(lc, axis=-1, keepdims=True))
        pp = jnp.exp(lp - m)
        pc = jnp.exp(lc - m)
        l = jnp.sum(pp, axis=-1, keepdims=True) + jnp.sum(pc, axis=-1, keepdims=True)
        o = (_dot(pp.astype(BF16), vp_ref[0, :, sl]) + _dot(pc.astype(BF16), vc_ref[0, :, sl])) / l
        o_ref[0, :, sl] = o.astype(o_ref.dtype)
        lse_ref[0, :, sl] = jnp.broadcast_to(m + jnp.log(l), (ATTN_BLOCK, dh))


def _dattn(proj, bias, gq, gk, *, batch, seq, group, dilation):
    t, npj = proj.shape
    ls = seq // dilation
    nb = ls // ATTN_BLOCK
    cb = npj // ATTN_GW
    pv = proj.reshape(batch, ls, dilation * npj)
    cq, ck, cv = (off // ATTN_GW + group for off in (OFF_AQ, OFF_AK, OFF_AV))
    blk = (1, ATTN_BLOCK, ATTN_GW)
    cur = lambda c: (lambda b, r, n: (b, n, r * cb + c))
    prev = lambda c: (lambda b, r, n: (b, jnp.maximum(n - 1, 0), r * cb + c))
    const2 = lambda b, r, n: (0, 0)
    o, lse = pl.pallas_call(
        _dattn_kernel,
        grid=(batch, dilation, nb),
        in_specs=[pl.BlockSpec(blk, cur(cq)),
                  pl.BlockSpec(blk, prev(ck)), pl.BlockSpec(blk, cur(ck)),
                  pl.BlockSpec(blk, prev(cv)), pl.BlockSpec(blk, cur(cv)),
                  pl.BlockSpec((HEADS_PER_GROUP, ATTN_BLOCK, 2 * ATTN_BLOCK), lambda b, r, n: (0, 0, 0)),
                  pl.BlockSpec((1, ATTN_GW), const2), pl.BlockSpec((1, ATTN_GW), const2)],
        out_specs=[pl.BlockSpec(blk, lambda b, r, n: (b, n, r)),
                   pl.BlockSpec(blk, lambda b, r, n: (b, n, r))],
        out_shape=[jax.ShapeDtypeStruct((batch, ls, dilation * ATTN_GW), BF16),
                   jax.ShapeDtypeStruct((batch, ls, dilation * ATTN_GW), F32)],
        compiler_params=_cparams("parallel", "parallel", "arbitrary"),
        name=f"dattn{group}",
    )(pv, pv, pv, pv, pv, bias, gq, gk)
    return o.reshape(t, ATTN_GW), lse.reshape(t, ATTN_GW)


def _rel_bucket(n):
    max_exact = REL_BUCKETS // 2
    nf = jnp.maximum(n, 1).astype(F32)
    log_b = max_exact + (jnp.log(nf / max_exact) / math.log(REL_MAX_DIST / max_exact)
                         * (REL_BUCKETS - max_exact)).astype(jnp.int32)
    return jnp.where(n < max_exact, n, jnp.minimum(log_b, REL_BUCKETS - 1))


def _attn_bias(rel_bias, group):
    window, dilation = ATTN_PATTERNS[group]
    steps = window // dilation
    hs = slice(group * HEADS_PER_GROUP, (group + 1) * HEADS_PER_GROUP)
    bias_steps = rel_bias[_rel_bucket(jnp.arange(steps + 1) * dilation)][:, hs].T
    qi = jnp.arange(ATTN_BLOCK)[:, None]
    ki = jnp.arange(2 * ATTN_BLOCK)[None, :]
    dist = ATTN_BLOCK + qi - ki
    ok = (dist >= 0) & (dist <= steps)
    return jnp.where(ok[None], bias_steps[:, jnp.clip(dist, 0, steps)].astype(F32), NEG)


def _merge_kernel(ya_ref, yb0_ref, yb1_ref, yb2_ref, l0_ref, l1_ref, l2_ref, gu_ref, gv_ref, gate_ref,
                  x_ref, wa_ref, wb_ref, wc_ref, wo_ref, ws_ref, bs_ref, gg_ref, o_ref, yc_scr, *, tm):
    d = x_ref.shape[1]
    l0, l1, l2 = l0_ref[...], l1_ref[...], l2_ref[...]
    mx = jnp.maximum(jnp.maximum(l0, l1), l2)
    e0, e1, e2 = jnp.exp(l0 - mx), jnp.exp(l1 - mx), jnp.exp(l2 - mx)
    inv = 1.0 / (e0 + e1 + e2)
    yb = jnp.concatenate([(yb0_ref[...].astype(F32) * (e0 * inv)).astype(BF16),
                          (yb1_ref[...].astype(F32) * (e1 * inv)).astype(BF16),
                          (yb2_ref[...].astype(F32) * (e2 * inv)).astype(BF16)], axis=-1)

    for j in range(tm // GMLP_CHUNK):
        rows = slice(j * GMLP_CHUNK, (j + 1) * GMLP_CHUNK)
        for g in range(GMLP_GROUPS):
            cols = slice(g * GMLP_GC, (g + 1) * GMLP_GC)
            u = jax.nn.gelu(gu_ref[rows, cols].astype(F32))
            v = _rms(jax.nn.gelu(gv_ref[rows, cols].astype(F32)), gg_ref[:, cols])
            mixed = _dot(ws_ref[g], v.astype(BF16)) + bs_ref[g]
            yc_scr[rows, cols] = (u * mixed).astype(BF16)

    merged = jax.nn.sigmoid(gate_ref[:, 0:d].astype(F32)) * _dot(ya_ref[...], wa_ref[...])
    merged = merged + jax.nn.sigmoid(gate_ref[:, d:2 * d].astype(F32)) * _dot(yb, wb_ref[...])
    merged = merged + jax.nn.sigmoid(gate_ref[:, 2 * d:3 * d].astype(F32)) * _dot(yc_scr[...], wc_ref[...])
    o_ref[...] = x_ref[...] + _dot(merged.astype(BF16), wo_ref[...])


def _merge(ya, ybs, lses, proj, x2d, wa, wb, wc, wo, ws, bsb, gg, *, tm):
    t, d = x2d.shape
    row = lambda c: (lambda i: (i, c))
    full2 = lambda i: (0, 0)
    full3 = lambda i: (0, 0, 0)
    gspec = pl.BlockSpec((tm, ATTN_GW), row(0))
    return pl.pallas_call(
        functools.partial(_merge_kernel, tm=tm),
        grid=(t // tm,),
        in_specs=[pl.BlockSpec((tm, MLSTM_W), row(0)),
                  gspec, gspec, gspec, gspec, gspec, gspec,
                  pl.BlockSpec((tm, GMLP_W), row(OFF_GU // GMLP_W)),
                  pl.BlockSpec((tm, GMLP_W), row(OFF_GV // GMLP_W)),
                  pl.BlockSpec((tm, N_BRANCH * d), row(OFF_GATE // (N_BRANCH * d))),
                  pl.BlockSpec((tm, d), row(0)),
                  pl.BlockSpec(wa.shape, full2), pl.BlockSpec(wb.shape, full2),
                  pl.BlockSpec(wc.shape, full2), pl.BlockSpec(wo.shape, full2),
                  pl.BlockSpec(ws.shape, full3), pl.BlockSpec(bsb.shape, full3),
                  pl.BlockSpec(gg.shape, full2)],
        out_specs=pl.BlockSpec((tm, d), row(0)),
        out_shape=jax.ShapeDtypeStruct((t, d), F32),
        scratch_shapes=[pltpu.VMEM((tm, GMLP_W), BF16)],
        compiler_params=_cparams("parallel"),
        name="merge",
    )(ya, *ybs, *lses, proj, proj, proj, x2d, wa, wb, wc, wo, ws, bsb, gg)


def _memkv_kernel(mem_ref, g_ref, w_ref, gk_ref, k_ref, v_ref):
    dh, w = XATTN_DH, XATTN_W
    kv = _dot(_rms(mem_ref[0], g_ref[...]).astype(BF16), w_ref[...])
    for h in range(XATTN_HEADS):
        sl = slice(h * dh, (h + 1) * dh)
        k_ref[0, :, sl] = _rms(kv[:, sl], gk_ref[...]).astype(k_ref.dtype)
    v_ref[0] = kv[:, w:].astype(v_ref.dtype)


def _memkv(mem, gain, w_kv, gk):
    b, m, d = mem.shape
    full2 = lambda i: (0, 0)
    return pl.pallas_call(
        _memkv_kernel,
        grid=(b,),
        in_specs=[pl.BlockSpec((1, m, d), lambda i: (i, 0, 0)),
                  pl.BlockSpec((1, d), full2),
                  pl.BlockSpec(w_kv.shape, full2),
                  pl.BlockSpec((1, XATTN_DH), full2)],
        out_specs=[pl.BlockSpec((1, m, XATTN_W), lambda i: (i, 0, 0)),
                   pl.BlockSpec((1, m, XATTN_W), lambda i: (i, 0, 0))],
        out_shape=[jax.ShapeDtypeStruct((b, m, XATTN_W), BF16),
                   jax.ShapeDtypeStruct((b, m, XATTN_W), BF16)],
        compiler_params=_cparams("parallel"),
        name="memkv",
    )(mem, gain, w_kv, gk)


def _route(logits):
    tm = logits.shape[1]
    e = jnp.exp(logits - jnp.max(logits, axis=0, keepdims=True))
    probs = e / jnp.sum(e, axis=0, keepdims=True)
    rowi = lax.broadcasted_iota(jnp.int32, (8, tm), 0)
    real = rowi < EXPERTS_PER_GROUP
    tops = []
    for g in range(N_EXPERT_GROUPS):
        pg = jnp.where(real, probs[8 * g:8 * g + 8, :], -0.5)
        m1 = jnp.max(pg, axis=0, keepdims=True)
        i1 = jnp.min(jnp.where(pg == m1, rowi, 8), axis=0, keepdims=True)
        pg2 = jnp.where(rowi == i1, -1.0, pg)
        m2 = jnp.max(pg2, axis=0, keepdims=True)
        i2 = jnp.min(jnp.where(pg2 == m2, rowi, 8), axis=0, keepdims=True)
        tops.append((m1, i1, m2, i2))
    best = jnp.zeros((1, tm), jnp.int32)
    best_score = tops[0][0] + tops[0][2]
    for g in range(1, N_EXPERT_GROUPS):
        score = tops[g][0] + tops[g][2]
        better = score > best_score
        best = jnp.where(better, g, best)
        best_score = jnp.where(better, score, best_score)
    out = []
    for g, (m1, i1, m2, i2) in enumerate(tops):
        tot = m1 + m2
        wg = jnp.where(rowi == i1, m1 / tot, jnp.where(rowi == i2, m2 / tot, 0.0))
        out.append(jnp.where(best == g, wg, 0.0))
    return jnp.concatenate(out, axis=0)


def _xattn_kernel(x_ref, k_ref, v_ref, gx_ref, wq_ref, gq_ref, wo_ref, gf_ref, rw_ref, rb_ref,
                  xo_ref, hf_ref, gates_ref):
    dh = XATTN_DH
    x = x_ref[...]
    q = _dot(_rms(x, gx_ref[...]).astype(BF16), wq_ref[...])
    outs = []
    for h in range(XATTN_HEADS):
        sl = slice(h * dh, (h + 1) * dh)
        q_h = (_rms(q[:, sl], gq_ref[...]) * (dh ** -0.5)).astype(BF16)
        logits = _dot_nt(q_h, k_ref[0, :, sl])
        p = jnp.exp(logits - jnp.max(logits, axis=-1, keepdims=True))
        o = _dot(p.astype(BF16), v_ref[0, :, sl]) / jnp.sum(p, axis=-1, keepdims=True)
        outs.append(o.astype(BF16))
    xn = x + _dot(jnp.concatenate(outs, axis=-1), wo_ref[...])
    xo_ref[...] = xn
    hf = _rms(xn, gf_ref[...])
    hf_ref[...] = hf.astype(hf_ref.dtype)
    logits_t = lax.dot_general(rw_ref[...], hf, (((1,), (1,)), ((), ())),
                               precision=lax.Precision.HIGHEST, preferred_element_type=F32) + rb_ref[...]
    gates_ref[...] = _route(logits_t)


def _xattn(x2d, k, v, gx, wq, gq, wo, gf, rw_t, rb, *, seq, tm):
    t, d = x2d.shape
    per_b = seq // tm
    full2 = lambda i: (0, 0)
    kv_spec = pl.BlockSpec((1,) + k.shape[1:], lambda i: (i // per_b, 0, 0))
    return pl.pallas_call(
        _xattn_kernel,
        grid=(t // tm,),
        in_specs=[pl.BlockSpec((tm, d), lambda i: (i, 0)), kv_spec, kv_spec,
                  pl.BlockSpec((1, d), full2), pl.BlockSpec(wq.shape, full2),
                  pl.BlockSpec((1, XATTN_DH), full2), pl.BlockSpec(wo.shape, full2),
                  pl.BlockSpec((1, d), full2), pl.BlockSpec(rw_t.shape, full2),
                  pl.BlockSpec(rb.shape, full2)],
        out_specs=[pl.BlockSpec((tm, d), lambda i: (i, 0)),
                   pl.BlockSpec((tm, d), lambda i: (i, 0)),
                   pl.BlockSpec((ROUTER_ROWS, tm), lambda i: (0, i))],
        out_shape=[jax.ShapeDtypeStruct((t, d), F32),
                   jax.ShapeDtypeStruct((t, d), BF16),
                   jax.ShapeDtypeStruct((ROUTER_ROWS, t), F32)],
        compiler_params=_cparams("parallel"),
        name="xattn_router",
    )(x2d, k, v, gx, wq, gq, wo, gf, rw_t, rb)


def _moe_kernel(hf_ref, x_ref, g_ref, wg_ref, wu_ref, wd_ref, o_ref, acc_scr):
    e = pl.program_id(1)

    @pl.when(e == 0)
    def _():
        acc_scr[...] = jnp.zeros_like(acc_scr)

    h = hf_ref[...]
    up = _dot(h, wg_ref[0])
    a = up * jax.nn.sigmoid(up) * _dot(h, wu_ref[0])
    lane = lax.broadcasted_iota(jnp.int32, g_ref.shape, 1)
    gate = jnp.sum(jnp.where(lane == e, g_ref[...], 0.0), axis=-1, keepdims=True)
    acc_scr[...] += _dot((a * gate).astype(BF16), wd_ref[0])

    @pl.when(e == pl.num_programs(1) - 1)
    def _():
        o_ref[...] = x_ref[...] + acc_scr[...]


def _moe(hf, x2d, gates, wg, wu, wd, *, tm):
    t, d = x2d.shape
    ne, _, dff = wg.shape
    return pl.pallas_call(
        _moe_kernel,
        grid=(t // tm, ne),
        in_specs=[pl.BlockSpec((tm, d), lambda i, e: (i, 0)),
                  pl.BlockSpec((tm, d), lambda i, e: (i, 0)),
                  pl.BlockSpec((tm, gates.shape[1]), lambda i, e: (i, 0)),
                  pl.BlockSpec((1, d, dff), lambda i, e: (e, 0, 0)),
                  pl.BlockSpec((1, d, dff), lambda i, e: (e, 0, 0)),
                  pl.BlockSpec((1, dff, d), lambda i, e: (e, 0, 0))],
        out_specs=pl.BlockSpec((tm, d), lambda i, e: (i, 0)),
        out_shape=jax.ShapeDtypeStruct((t, d), F32),
        scratch_shapes=[pltpu.VMEM((tm, d), F32)],
        compiler_params=_cparams("parallel", "arbitrary"),
        name="moe",
    )(hf, x2d, gates, wg, wu, wd)


def _layout_w_in(w):
    sizes = (MLSTM_W, MLSTM_W, MLSTM_W, MLSTM_W, MLSTM_HEADS, MLSTM_HEADS,
             ATTN_W, ATTN_W, ATTN_W, GMLP_W, GMLP_W, N_BRANCH * w.shape[0])
    pts = np.cumsum(sizes)[:-1]
    mq, mk, mv, mo, mi, mf, aq, ak, av, gu, gv, gate = jnp.split(w, pts, axis=-1)
    pad = jnp.zeros((w.shape[0], IF_PAD - 2 * MLSTM_HEADS), w.dtype)
    return jnp.concatenate([mq, mk, mv, mo, gu, gv, gate, aq, ak, av, mi, mf, pad], axis=-1).astype(BF16)


def kernel(x, mem, norm_mix, w_in, mlstm_conv, mlstm_gate_b, mlstm_norm, attn_qk_norm, gmlp_norm, gmlp_ws,
           gmlp_bs, w_branch_a, w_branch_b, w_branch_c, w_out, rel_bias, norm_xattn, norm_mem, w_xq, w_xkv,
           xattn_qk_norm, w_xo, norm_ffn, router_w, router_b, w_expert_gate, w_expert_up, w_expert_down):
    b, s, d = x.shape
    t = b * s
    depth = w_in.shape[0]
    x2d = x.reshape(t, d)

    biases = [_attn_bias(rel_bias, g) for g in range(len(ATTN_PATTERNS))]
    rw_t = jnp.zeros((N_EXPERT_GROUPS, 8, d), F32).at[:, :EXPERTS_PER_GROUP].set(
        router_w.T.reshape(N_EXPERT_GROUPS, EXPERTS_PER_GROUP, d)).reshape(ROUTER_ROWS, d)
    rb = jnp.full((N_EXPERT_GROUPS, 8), NEG, F32).at[:, :EXPERTS_PER_GROUP].set(
        router_b.astype(F32).reshape(N_EXPERT_GROUPS, EXPERTS_PER_GROUP)).reshape(ROUTER_ROWS, 1)
    tril = jnp.tril(jnp.ones((GMLP_CHUNK, GMLP_CHUNK), bool))

    for l in range(depth):
        proj = _inproj(x2d, norm_mix[l][None], _layout_w_in(w_in[l]), tm=1024, tn=2176)

        gates_row = proj[:, OFF_IF:OFF_IF + 8].astype(F32).reshape(b, s, 8).transpose(0, 2, 1)
        gb_col = jnp.zeros((1, IF_PAD), F32).at[0, :8].set(mlstm_gate_b[l])
        ya = _mlstm(proj, gates_row, mlstm_conv[l], gb_col, mlstm_gate_b[l].reshape(8, 1),
                    mlstm_norm[l][None], batch=b, seq=s, blk=MLSTM_BLOCK)

        gq = jnp.tile(attn_qk_norm[l, 0], HEADS_PER_GROUP)[None]
        gk = jnp.tile(attn_qk_norm[l, 1], HEADS_PER_GROUP)[None]
        ybs, lses = [], []
        for g, (_, dilation) in enumerate(ATTN_PATTERNS):
            o, lse = _dattn(proj, biases[g], gq, gk, batch=b, seq=s, group=g, dilation=dilation)
            ybs.append(o)
            lses.append(lse)

        ws = jnp.where(tril, gmlp_ws[l], 0.0).astype(BF16)
        bsb = jnp.broadcast_to(gmlp_bs[l][:, :, None], (GMLP_GROUPS, GMLP_CHUNK, GMLP_GC)).astype(F32)
        x2d = _merge(ya, ybs, lses, proj, x2d, w_branch_a[l].astype(BF16), w_branch_b[l].astype(BF16),
                     w_branch_c[l].astype(BF16), w_out[l].astype(BF16), ws, bsb, gmlp_norm[l][None], tm=256)

        k_mem, v_mem = _memkv(mem, norm_mem[l][None], w_xkv[l].astype(BF16), xattn_qk_norm[l, 1][None])
        x2d, hf, gates_t = _xattn(x2d, k_mem, v_mem, norm_xattn[l][None], w_xq[l].astype(BF16),
                                  xattn_qk_norm[l, 0][None], w_xo[l].astype(BF16), norm_ffn[l][None],
                                  rw_t, rb, seq=s, tm=512)

        gates = gates_t.reshape(N_EXPERT_GROUPS, 8, t)[:, :EXPERTS_PER_GROUP].reshape(N_EXPERTS, t).T
        gates = jnp.pad(gates, ((0, 0), (0, 128 - N_EXPERTS)))
        x2d = _moe(hf, x2d, gates, w_expert_gate[l].astype(BF16), w_expert_up[l].astype(BF16),
                   w_expert_down[l].astype(BF16), tm=1024)

    return x2d.reshape(b, s, d)
```

```python
import functools
import math

import jax
import jax.numpy as jnp
import numpy as np
from jax import lax
from jax.experimental import pallas as pl
from jax.experimental.pallas import tpu as pltpu

F32 = jnp.float32
BF16 = jnp.bfloat16

EPS = 1e-6
NEG = -1e30

MLSTM_HEADS = 4
MLSTM_DH = 128
MLSTM_W = MLSTM_HEADS * MLSTM_DH
CONV_WIDTH = 4
MLSTM_BLOCK = 128

ATTN_PATTERNS = ((128, 1), (512, 4), (2048, 16))
HEADS_PER_GROUP = 4
ATTN_DH = 64
ATTN_GW = HEADS_PER_GROUP * ATTN_DH
ATTN_W = len(ATTN_PATTERNS) * ATTN_GW
ATTN_BLOCK = 128
REL_BUCKETS = 32
REL_MAX_DIST = 2048

GMLP_GROUPS = 4
GMLP_GC = 128
GMLP_W = GMLP_GROUPS * GMLP_GC
GMLP_CHUNK = 128

XATTN_HEADS = 4
XATTN_DH = 128
XATTN_W = XATTN_HEADS * XATTN_DH

N_EXPERTS = 16
N_EXPERT_GROUPS = 4
EXPERTS_PER_GROUP = 4
ROUTER_ROWS = 8 * N_EXPERT_GROUPS

N_BRANCH = 3

OFF_MQ, OFF_MK, OFF_MV, OFF_MO = 0, 512, 1024, 1536
OFF_GU, OFF_GV = 2048, 2560
OFF_GATE = 3072
OFF_IF = 6144
IF_PAD = 256
N_PROJ = OFF_IF + IF_PAD

ATTN_TILE = 2048
ATTN_SUB = ATTN_TILE // ATTN_BLOCK

VMEM_LIMIT = 48 * 1024 * 1024


def _cparams(*sem):
    return pltpu.CompilerParams(dimension_semantics=sem, vmem_limit_bytes=VMEM_LIMIT)


def _rms(x, gain):
    return x * lax.rsqrt(jnp.mean(x * x, axis=-1, keepdims=True) + EPS) * gain


def _dot(a, b):
    return jnp.dot(a, b, preferred_element_type=F32)


def _dot_nt(a, b):
    return lax.dot_general(a, b, (((1,), (1,)), ((), ())), preferred_element_type=F32)


def _inproj_kernel(x_ref, g_ref, w_ref, o_ref, h_scr):
    @pl.when(pl.program_id(1) == 0)
    def _():
        h_scr[...] = _rms(x_ref[...], g_ref[...]).astype(BF16)

    o_ref[...] = _dot(h_scr[...], w_ref[...]).astype(o_ref.dtype)


def _inproj(x2d, gain, w, *, tm, tn):
    t, d = x2d.shape
    n = w.shape[1]
    return pl.pallas_call(
        _inproj_kernel,
        grid=(t // tm, n // tn),
        in_specs=[pl.BlockSpec((tm, d), lambda i, j: (i, 0)),
                  pl.BlockSpec((1, d), lambda i, j: (0, 0)),
                  pl.BlockSpec((d, tn), lambda i, j: (0, j))],
        out_specs=pl.BlockSpec((tm, tn), lambda i, j: (i, j)),
        out_shape=jax.ShapeDtypeStruct((t, n), BF16),
        scratch_shapes=[pltpu.VMEM((tm, d), BF16)],
        compiler_params=_cparams("parallel", "arbitrary"),
        name="inproj",
    )(x2d, gain, w)


def _log_sigmoid(x):
    return jnp.minimum(x, 0.0) - jnp.log(1.0 + jnp.exp(-jnp.abs(x)))


def _mlstm_kernel(qk_ref, v_ref, og_ref, gc_ref, gr_ref, cw_ref, gbc_ref, gbr_ref, ng_ref, y_ref,
                  xe_scr, c_scr, n_scr, m_scr, *, blk):
    heads, dh, w = MLSTM_HEADS, MLSTM_DH, MLSTM_W
    hp = lax.Precision.HIGHEST

    @pl.when(pl.program_id(1) == 0)
    def _():
        xe_scr[0:8, :] = jnp.zeros((8, 2 * w), F32)
        c_scr[...] = jnp.zeros_like(c_scr)
        n_scr[...] = jnp.zeros_like(n_scr)
        m_scr[...] = jnp.zeros_like(m_scr)

    xe_scr[8:8 + blk, :] = qk_ref[...].astype(F32)
    cw = cw_ref[...]
    conv = cw[CONV_WIDTH - 1:CONV_WIDTH, :] * xe_scr[8:8 + blk, :]
    for j in range(CONV_WIDTH - 1):
        off = 8 - (CONV_WIDTH - 1) + j
        conv = conv + cw[j:j + 1, :] * xe_scr[off:off + blk, :]
    xe_scr[0:8, :] = xe_scr[blk:blk + 8, :]
    qk = conv * jax.nn.sigmoid(conv)

    gcol = gc_ref[...].astype(F32) + gbc_ref[...]
    grow = gr_ref[0] + gbr_ref[...]
    ri = lax.broadcasted_iota(jnp.int32, (blk, blk), 0)
    ci = lax.broadcasted_iota(jnp.int32, (blk, blk), 1)
    causal = ri >= ci
    tril = causal.astype(F32)
    triu = (ri <= ci).astype(F32)
    bcol = jnp.dot(tril, _log_sigmoid(gcol), precision=hp, preferred_element_type=F32)
    brow = jnp.dot(_log_sigmoid(grow), triu, precision=hp, preferred_element_type=F32)

    for h in range(heads):
        sl = slice(h * dh, (h + 1) * dh)
        b_c = bcol[:, heads + h:heads + h + 1]
        i_c = gcol[:, h:h + 1]
        b_r = brow[heads + h:heads + h + 1, :]
        i_r = grow[h:h + 1, :]
        m_st = m_scr[h:h + 1, 0:1]
        c_st = c_scr[h]
        n_st = n_scr[h:h + 1, :]

        d_mat = jnp.where(causal, b_c - b_r + i_r, NEG)
        inter = b_c + m_st
        m_t = jnp.maximum(inter, jnp.max(d_mat, axis=-1, keepdims=True))
        w_intra = jnp.exp(d_mat - m_t)
        w_inter = jnp.exp(inter - m_t)

        q_f = qk[:, sl]
        k_f = qk[:, w + h * dh:w + (h + 1) * dh] * (dh ** -0.5)
        q_b = q_f.astype(BF16)
        k_b = k_f.astype(BF16)
        v_b = v_ref[:, sl]

        s = _dot_nt(q_b, k_b) * w_intra
        num = _dot(s.astype(BF16), v_b) + w_inter * _dot(q_b, c_st.astype(BF16))
        den = jnp.sum(s, axis=-1, keepdims=True) + w_inter * jnp.sum(q_f * n_st, axis=-1, keepdims=True)
        hh = num / jnp.maximum(jnp.abs(den), jnp.exp(-m_t))
        hn = _rms(hh, ng_ref[:, sl])
        y_ref[:, sl] = (hn * jax.nn.sigmoid(og_ref[:, sl].astype(F32))).astype(y_ref.dtype)

        b_last = b_c[blk - 1:blk, :]
        dec = b_last - b_c + i_c
        m_new = jnp.maximum(b_last + m_st, jnp.max(dec, axis=0, keepdims=True))
        w_k = jnp.exp(dec - m_new)
        w_c = jnp.exp(b_last + m_st - m_new)
        kw = k_f * w_k
        c_scr[h] = w_c * c_st + _dot(kw.T.astype(BF16), v_b)
        n_scr[h:h + 1, :] = w_c * n_st + jnp.sum(kw, axis=0, keepdims=True)
        m_scr[h:h + 1, :] = jnp.broadcast_to(m_new, (1, m_scr.shape[1]))


def _mlstm(proj, gates_row, conv_w, gb_col, gb_row, norm_g, *, batch, seq, blk):
    t = proj.shape[0]
    nblk = seq // blk
    w = MLSTM_W
    row = lambda b, c: b * nblk + c
    return pl.pallas_call(
        functools.partial(_mlstm_kernel, blk=blk),
        grid=(batch, nblk),
        in_specs=[pl.BlockSpec((blk, 2 * w), lambda b, c: (row(b, c), OFF_MQ // (2 * w))),
                  pl.BlockSpec((blk, w), lambda b, c: (row(b, c), OFF_MV // w)),
                  pl.BlockSpec((blk, w), lambda b, c: (row(b, c), OFF_MO // w)),
                  pl.BlockSpec((blk, IF_PAD), lambda b, c: (row(b, c), OFF_IF // IF_PAD)),
                  pl.BlockSpec((1, 8, blk), lambda b, c: (b, 0, c)),
                  pl.BlockSpec((CONV_WIDTH, 2 * w), lambda b, c: (0, 0)),
                  pl.BlockSpec((1, IF_PAD), lambda b, c: (0, 0)),
                  pl.BlockSpec((8, 1), lambda b, c: (0, 0)),
                  pl.BlockSpec((1, w), lambda b, c: (0, 0))],
        out_specs=pl.BlockSpec((blk, w), lambda b, c: (row(b, c), 0)),
        out_shape=jax.ShapeDtypeStruct((t, w), BF16),
        scratch_shapes=[pltpu.VMEM((blk + 8, 2 * w), F32),
                        pltpu.VMEM((MLSTM_HEADS, MLSTM_DH, MLSTM_DH), F32),
                        pltpu.VMEM((8, MLSTM_DH), F32),
                        pltpu.VMEM((8, 128), F32)],
        compiler_params=_cparams("parallel", "arbitrary"),
        name="mlstm",
    )(proj, proj, proj, proj, gates_row, conv_w, gb_col, gb_row, norm_g)


def _attnproj_kernel(x_ref, g_ref, w_ref, o_ref, h_scr, r_scr):
    j = pl.program_id(1)

    @pl.when(j == 0)
    def _():
        h_scr[...] = _rms(x_ref[...], g_ref[...]).astype(BF16)

    res = _dot(h_scr[...], w_ref[...])
    ncb = r_scr.shape[0]
    for g, (_, dil) in enumerate(ATTN_PATTERNS):
        @pl.when(j == g)
        def _(dil=dil):
            if dil == 1:
                o_ref[...] = res.astype(o_ref.dtype)
            else:
                seg = ATTN_TILE // dil
                for c in range(ncb):
                    r_scr[c] = res[:, c * 128:(c + 1) * 128]
                for r in range(dil):
                    for c in range(ncb):
                        o_ref[r * seg:(r + 1) * seg, c * 128:(c + 1) * 128] = (
                            r_scr[c, pl.ds(r, seg, stride=dil), :].astype(o_ref.dtype))


def _attnproj(x2d, gain, w):
    t, d = x2d.shape
    ng = len(ATTN_PATTERNS)
    wcols = 3 * ATTN_GW
    return pl.pallas_call(
        _attnproj_kernel,
        grid=(t // ATTN_TILE, ng),
        in_specs=[pl.BlockSpec((ATTN_TILE, d), lambda i, j: (i, 0)),
                  pl.BlockSpec((1, d), lambda i, j: (0, 0)),
                  pl.BlockSpec((d, wcols), lambda i, j: (0, j))],
        out_specs=pl.BlockSpec((ATTN_TILE, wcols), lambda i, j: (i, j)),
        out_shape=jax.ShapeDtypeStruct((t, ng * wcols), BF16),
        scratch_shapes=[pltpu.VMEM((ATTN_TILE, d), BF16), pltpu.VMEM((wcols // 128, ATTN_TILE, 128), F32)],
        compiler_params=_cparams("parallel", "arbitrary"),
        name="attnproj",
    )(x2d, gain, w)


def _dattn_kernel(q_ref, kc_ref, kp_ref, vc_ref, vp_ref, bias_ref, gq_ref, gk_ref, o_ref, lse_ref,
                  kn_scr, o_scr, l_scr, *, dil):
    dh, blk = ATTN_DH, ATTN_BLOCK
    per = ATTN_SUB // dil
    first_tile = pl.program_id(1) == 0

    def norm_k(k):
        return jnp.concatenate([_rms(k[:, h * dh:(h + 1) * dh], gk_ref[:, h * dh:(h + 1) * dh])
                                for h in range(HEADS_PER_GROUP)], axis=-1).astype(BF16)

    for u in range(ATTN_SUB):
        kn_scr[u * blk:(u + 1) * blk, :] = norm_k(kc_ref[u * blk:(u + 1) * blk, :].astype(F32))

    for r in range(dil):
        for sub in range(per):
            u = r * per + sub
            rows = slice(u * blk, (u + 1) * blk)
            q = q_ref[rows, :].astype(F32)
            if sub > 0:
                prows = slice((u - 1) * blk, u * blk)
                kp, vprev_ref, masked = kn_scr[prows, :], vc_ref, None
            else:
                prows = slice((u + per - 1) * blk, (u + per) * blk)
                kp, vprev_ref, masked = norm_k(kp_ref[prows, :].astype(F32)), vp_ref, first_tile
            kc = kn_scr[rows, :]
            outs, lses = [], []
            for h in range(HEADS_PER_GROUP):
                sl = slice(h * dh, (h + 1) * dh)
                q_h = (_rms(q[:, sl], gq_ref[:, sl]) * (dh ** -0.5)).astype(BF16)
                bias = bias_ref[h]
                lp = _dot_nt(q_h, kp[:, sl]) + bias[:, :blk]
                if masked is not None:
                    lp = jnp.where(masked, NEG, lp)
                lc = _dot_nt(q_h, kc[:, sl]) + bias[:, blk:]
                m = jnp.maximum(jnp.max(lp, axis=-1, keepdims=True), jnp.max(lc, axis=-1, keepdims=True))
                pp = jnp.exp(lp - m)
                pc = jnp.exp(lc - m)
                l = jnp.sum(pp, axis=-1, keepdims=True) + jnp.sum(pc, axis=-1, keepdims=True)
                outs.append((_dot(pp.astype(BF16), vprev_ref[prows, sl])
                             + _dot(pc.astype(BF16), vc_ref[rows, sl])) / l)
                lses.append(jnp.broadcast_to(m + jnp.log(l), (blk, dh)))
            dst = pl.ds(sub * blk * dil + r, blk, stride=dil) if dil > 1 else rows
            for c in range(ATTN_GW // 128):
                o_scr[c, dst, :] = jnp.concatenate(outs[2 * c:2 * c + 2], axis=-1)
                l_scr[c, dst, :] = jnp.concatenate(lses[2 * c:2 * c + 2], axis=-1)
    for c in range(ATTN_GW // 128):
        o_ref[:, c * 128:(c + 1) * 128] = o_scr[c].astype(o_ref.dtype)
        lse_ref[:, c * 128:(c + 1) * 128] = l_scr[c]


def _dattn(aproj, bias, gq, gk, *, seq, group, dilation):
    t = aproj.shape[0]
    tiles = seq // ATTN_TILE
    cq, ck, cv = 3 * group, 3 * group + 1, 3 * group + 2
    blk = (ATTN_TILE, ATTN_GW)
    cur = lambda c: (lambda b, j: (b * tiles + j, c))
    prev = lambda c: (lambda b, j: (b * tiles + jnp.maximum(j - 1, 0), c))
    const2 = lambda b, j: (0, 0)
    return pl.pallas_call(
        functools.partial(_dattn_kernel, dil=dilation),
        grid=(t // seq, tiles),
        in_specs=[pl.BlockSpec(blk, cur(cq)),
                  pl.BlockSpec(blk, cur(ck)), pl.BlockSpec(blk, prev(ck)),
                  pl.BlockSpec(blk, cur(cv)), pl.BlockSpec(blk, prev(cv)),
                  pl.BlockSpec((HEADS_PER_GROUP, ATTN_BLOCK, 2 * ATTN_BLOCK), lambda b, j: (0, 0, 0)),
                  pl.BlockSpec((1, ATTN_GW), const2), pl.BlockSpec((1, ATTN_GW), const2)],
        out_specs=[pl.BlockSpec(blk, cur(0)), pl.BlockSpec(blk, cur(0))],
        out_shape=[jax.ShapeDtypeStruct((t, ATTN_GW), BF16), jax.ShapeDtypeStruct((t, ATTN_GW), F32)],
        scratch_shapes=[pltpu.VMEM(blk, BF16),
                        pltpu.VMEM((ATTN_GW // 128, ATTN_TILE, 128), F32),
                        pltpu.VMEM((ATTN_GW // 128, ATTN_TILE, 128), F32)],
        compiler_params=_cparams("parallel", "arbitrary"),
        name=f"dattn{group}",
    )(aproj, aproj, aproj, aproj, aproj, bias, gq, gk)


def _rel_bucket(n):
    max_exact = REL_BUCKETS // 2
    nf = jnp.maximum(n, 1).astype(F32)
    log_b = max_exact + (jnp.log(nf / max_exact) / math.log(REL_MAX_DIST / max_exact)
                         * (REL_BUCKETS - max_exact)).astype(jnp.int32)
    return jnp.where(n < max_exact, n, jnp.minimum(log_b, REL_BUCKETS - 1))


def _attn_bias(rel_bias, group):
    window, dilation = ATTN_PATTERNS[group]
    steps = window // dilation
    hp = lax.Precision.HIGHEST
    hs = slice(group * HEADS_PER_GROUP, (group + 1) * HEADS_PER_GROUP)
    bucket = _rel_bucket(jnp.arange(steps + 1) * dilation)
    bias_steps = jnp.dot(jax.nn.one_hot(bucket, REL_BUCKETS, dtype=F32), rel_bias[:, hs].astype(F32),
                         precision=hp)
    qi = jnp.arange(ATTN_BLOCK)[:, None]
    ki = jnp.arange(2 * ATTN_BLOCK)[None, :]
    dist = ATTN_BLOCK + qi - ki
    ok = (dist >= 0) & (dist <= steps)
    sel = jax.nn.one_hot(jnp.clip(dist, 0, steps).reshape(-1), steps + 1, dtype=F32)
    bias = jnp.dot(sel, bias_steps, precision=hp).T.reshape(HEADS_PER_GROUP, ATTN_BLOCK, 2 * ATTN_BLOCK)
    return jnp.where(ok[None], bias, NEG)


def _merge_kernel(ya_ref, yb0_ref, yb1_ref, yb2_ref, l0_ref, l1_ref, l2_ref, gu_ref, gv_ref, gate_ref,
                  x_ref, wa_ref, wb_ref, wc_ref, wo_ref, ws_ref, bs_ref, gg_ref, o_ref, yc_scr, *, tm):
    d = x_ref.shape[1]
    l0, l1, l2 = l0_ref[...], l1_ref[...], l2_ref[...]
    mx = jnp.maximum(jnp.maximum(l0, l1), l2)
    e0, e1, e2 = jnp.exp(l0 - mx), jnp.exp(l1 - mx), jnp.exp(l2 - mx)
    inv = 1.0 / (e0 + e1 + e2)
    yb = jnp.concatenate([(yb0_ref[...].astype(F32) * (e0 * inv)).astype(BF16),
                          (yb1_ref[...].astype(F32) * (e1 * inv)).astype(BF16),
                          (yb2_ref[...].astype(F32) * (e2 * inv)).astype(BF16)], axis=-1)

    for j in range(tm // GMLP_CHUNK):
        rows = slice(j * GMLP_CHUNK, (j + 1) * GMLP_CHUNK)
        for g in range(GMLP_GROUPS):
            cols = slice(g * GMLP_GC, (g + 1) * GMLP_GC)
            u = jax.nn.gelu(gu_ref[rows, cols].astype(F32))
            v = _rms(jax.nn.gelu(gv_ref[rows, cols].astype(F32)), gg_ref[:, cols])
            mixed = _dot(ws_ref[g], v.astype(BF16)) + bs_ref[g]
            yc_scr[rows, cols] = (u * mixed).astype(BF16)

    merged = jax.nn.sigmoid(gate_ref[:, 0:d].astype(F32)) * _dot(ya_ref[...], wa_ref[...])
    merged = merged + jax.nn.sigmoid(gate_ref[:, d:2 * d].astype(F32)) * _dot(yb, wb_ref[...])
    merged = merged + jax.nn.sigmoid(gate_ref[:, 2 * d:3 * d].astype(F32)) * _dot(yc_scr[...], wc_ref[...])
    o_ref[...] = x_ref[...] + _dot(merged.astype(BF16), wo_ref[...])


def _merge(ya, ybs, lses, proj, x2d, wa, wb, wc, wo, ws, bsb, gg, *, tm):
    t, d = x2d.shape
    row = lambda c: (lambda i: (i, c))
    full2 = lambda i: (0, 0)
    full3 = lambda i: (0, 0, 0)
    gspec = pl.BlockSpec((tm, ATTN_GW), row(0))
    return pl.pallas_call(
        functools.partial(_merge_kernel, tm=tm),
        grid=(t // tm,),
        in_specs=[pl.BlockSpec((tm, MLSTM_W), row(0)),
                  gspec, gspec, gspec, gspec, gspec, gspec,
                  pl.BlockSpec((tm, GMLP_W), row(OFF_GU // GMLP_W)),
                  pl.BlockSpec((tm, GMLP_W), row(OFF_GV // GMLP_W)),
                  pl.BlockSpec((tm, N_BRANCH * d), row(OFF_GATE // (N_BRANCH * d))),
                  pl.BlockSpec((tm, d), row(0)),
                  pl.BlockSpec(wa.shape, full2), pl.BlockSpec(wb.shape, full2),
                  pl.BlockSpec(wc.shape, full2), pl.BlockSpec(wo.shape, full2),
                  pl.BlockSpec(ws.shape, full3), pl.BlockSpec(bsb.shape, full3),
                  pl.BlockSpec(gg.shape, full2)],
        out_specs=pl.BlockSpec((tm, d), row(0)),
        out_shape=jax.ShapeDtypeStruct((t, d), F32),
        scratch_shapes=[pltpu.VMEM((tm, GMLP_W), BF16)],
        compiler_params=_cparams("parallel"),
        name="merge",
    )(ya, *ybs, *lses, proj, proj, proj, x2d, wa, wb, wc, wo, ws, bsb, gg)


def _memkv_kernel(mem_ref, g_ref, w_ref, gk_ref, k_ref, v_ref):
    dh, w = XATTN_DH, XATTN_W
    kv = _dot(_rms(mem_ref[0], g_ref[...]).astype(BF16), w_ref[...])
    for h in range(XATTN_HEADS):
        sl = slice(h * dh, (h + 1) * dh)
        k_ref[0, :, sl] = _rms(kv[:, sl], gk_ref[...]).astype(k_ref.dtype)
    v_ref[0] = kv[:, w:].astype(v_ref.dtype)


def _memkv(mem, gain, w_kv, gk):
    b, m, d = mem.shape
    full2 = lambda i: (0, 0)
    return pl.pallas_call(
        _memkv_kernel,
        grid=(b,),
        in_specs=[pl.BlockSpec((1, m, d), lambda i: (i, 0, 0)),
                  pl.BlockSpec((1, d), full2),
                  pl.BlockSpec(w_kv.shape, full2),
                  pl.BlockSpec((1, XATTN_DH), full2)],
        out_specs=[pl.BlockSpec((1, m, XATTN_W), lambda i: (i, 0, 0)),
                   pl.BlockSpec((1, m, XATTN_W), lambda i: (i, 0, 0))],
        out_shape=[jax.ShapeDtypeStruct((b, m, XATTN_W), BF16),
                   jax.ShapeDtypeStruct((b, m, XATTN_W), BF16)],
        compiler_params=_cparams("parallel"),
        name="memkv",
    )(mem, gain, w_kv, gk)


def _route(logits):
    tm = logits.shape[1]
    e = jnp.exp(logits - jnp.max(logits, axis=0, keepdims=True))
    probs = e / jnp.sum(e, axis=0, keepdims=True)
    rowi = lax.broadcasted_iota(jnp.int32, (8, tm), 0)
    real = rowi < EXPERTS_PER_GROUP
    tops = []
    for g in range(N_EXPERT_GROUPS):
        pg = jnp.where(real, probs[8 * g:8 * g + 8, :], -0.5)
        m1 = jnp.max(pg, axis=0, keepdims=True)
        i1 = jnp.min(jnp.where(pg == m1, rowi, 8), axis=0, keepdims=True)
        pg2 = jnp.where(rowi == i1, -1.0, pg)
        m2 = jnp.max(pg2, axis=0, keepdims=True)
        i2 = jnp.min(jnp.where(pg2 == m2, rowi, 8), axis=0, keepdims=True)
        tops.append((m1, i1, m2, i2))
    best = jnp.zeros((1, tm), jnp.int32)
    best_score = tops[0][0] + tops[0][2]
    for g in range(1, N_EXPERT_GROUPS):
        score = tops[g][0] + tops[g][2]
        better = score > best_score
        best = jnp.where(better, g, best)
        best_score = jnp.where(better, score, best_score)
    out = []
    for g, (m1, i1, m2, i2) in enumerate(tops):
        tot = m1 + m2
        wg = jnp.where(rowi == i1, m1 / tot, jnp.where(rowi == i2, m2 / tot, 0.0))
        out.append(jnp.where(best == g, wg, 0.0))
    return jnp.concatenate(out, axis=0)


def _xattn_kernel(x_ref, k_ref, v_ref, gx_ref, wq_ref, gq_ref, wo_ref, gf_ref, rw_ref, rb_ref,
                  xo_ref, hf_ref, gates_ref):
    dh = XATTN_DH
    x = x_ref[...]
    q = _dot(_rms(x, gx_ref[...]).astype(BF16), wq_ref[...])
    outs = []
    for h in range(XATTN_HEADS):
        sl = slice(h * dh, (h + 1) * dh)
        q_h = (_rms(q[:, sl], gq_ref[...]) * (dh ** -0.5)).astype(BF16)
        logits = _dot_nt(q_h, k_ref[0, :, sl])
        p = jnp.exp(logits - jnp.max(logits, axis=-1, keepdims=True))
        o = _dot(p.astype(BF16), v_ref[0, :, sl]) / jnp.sum(p, axis=-1, keepdims=True)
        outs.append(o.astype(BF16))
    xn = x + _dot(jnp.concatenate(outs, axis=-1), wo_ref[...])
    xo_ref[...] = xn
    hf = _rms(xn, gf_ref[...])
    hf_ref[...] = hf.astype(hf_ref.dtype)
    logits_t = lax.dot_general(rw_ref[...], hf, (((1,), (1,)), ((), ())),
                               precision=lax.Precision.HIGHEST, preferred_element_type=F32) + rb_ref[...]
    gates_ref[...] = _route(logits_t)


def _xattn(x2d, k, v, gx, wq, gq, wo, gf, rw_t, rb, *, seq, tm):
    t, d = x2d.shape
    per_b = seq // tm
    full2 = lambda i: (0, 0)
    kv_spec = pl.BlockSpec((1,) + k.shape[1:], lambda i: (i // per_b, 0, 0))
    return pl.pallas_call(
        _xattn_kernel,
        grid=(t // tm,),
        in_specs=[pl.BlockSpec((tm, d), lambda i: (i, 0)), kv_spec, kv_spec,
                  pl.BlockSpec((1, d), full2), pl.BlockSpec(wq.shape, full2),
                  pl.BlockSpec((1, XATTN_DH), full2), pl.BlockSpec(wo.shape, full2),
                  pl.BlockSpec((1, d), full2), pl.BlockSpec(rw_t.shape, full2),
                  pl.BlockSpec(rb.shape, full2)],
        out_specs=[pl.BlockSpec((tm, d), lambda i: (i, 0)),
                   pl.BlockSpec((tm, d), lambda i: (i, 0)),
                   pl.BlockSpec((ROUTER_ROWS, tm), lambda i: (0, i))],
        out_shape=[jax.ShapeDtypeStruct((t, d), F32),
                   jax.ShapeDtypeStruct((t, d), BF16),
                   jax.ShapeDtypeStruct((ROUTER_ROWS, t), F32)],
        compiler_params=_cparams("parallel"),
        name="xattn_router",
    )(x2d, k, v, gx, wq, gq, wo, gf, rw_t, rb)


def _moe_kernel(hf_ref, x_ref, g_ref, wg_ref, wu_ref, wd_ref, o_ref, acc_scr):
    e = pl.program_id(1)

    @pl.when(e == 0)
    def _():
        acc_scr[...] = jnp.zeros_like(acc_scr)

    h = hf_ref[...]
    up = _dot(h, wg_ref[0])
    a = up * jax.nn.sigmoid(up) * _dot(h, wu_ref[0])
    lane = lax.broadcasted_iota(jnp.int32, g_ref.shape, 1)
    gate = jnp.sum(jnp.where(lane == e, g_ref[...], 0.0), axis=-1, keepdims=True)
    acc_scr[...] += _dot((a * gate).astype(BF16), wd_ref[0])

    @pl.when(e == pl.num_programs(1) - 1)
    def _():
        o_ref[...] = x_ref[...] + acc_scr[...]


def _moe(hf, x2d, gates, wg, wu, wd, *, tm):
    t, d = x2d.shape
    ne, _, dff = wg.shape
    return pl.pallas_call(
        _moe_kernel,
        grid=(t // tm, ne),
        in_specs=[pl.BlockSpec((tm, d), lambda i, e: (i, 0)),
                  pl.BlockSpec((tm, d), lambda i, e: (i, 0)),
                  pl.BlockSpec((tm, gates.shape[1]), lambda i, e: (i, 0)),
                  pl.BlockSpec((1, d, dff), lambda i, e: (e, 0, 0)),
                  pl.BlockSpec((1, d, dff), lambda i, e: (e, 0, 0)),
                  pl.BlockSpec((1, dff, d), lambda i, e: (e, 0, 0))],
        out_specs=pl.BlockSpec((tm, d), lambda i, e: (i, 0)),
        out_shape=jax.ShapeDtypeStruct((t, d), F32),
        scratch_shapes=[pltpu.VMEM((tm, d), F32)],
        compiler_params=_cparams("parallel", "arbitrary"),
        name="moe",
    )(hf, x2d, gates, wg, wu, wd)


def _layout_w_in(w):
    sizes = (MLSTM_W, MLSTM_W, MLSTM_W, MLSTM_W, MLSTM_HEADS, MLSTM_HEADS,
             ATTN_W, ATTN_W, ATTN_W, GMLP_W, GMLP_W, N_BRANCH * w.shape[0])
    pts = np.cumsum(sizes)[:-1]
    mq, mk, mv, mo, mi, mf, aq, ak, av, gu, gv, gate = jnp.split(w, pts, axis=-1)
    pad = jnp.zeros((w.shape[0], IF_PAD - 2 * MLSTM_HEADS), w.dtype)
    main = jnp.concatenate([mq, mk, mv, mo, gu, gv, gate, mi, mf, pad], axis=-1).astype(BF16)
    attn = jnp.concatenate([a[:, g * ATTN_GW:(g + 1) * ATTN_GW] for g in range(len(ATTN_PATTERNS))
                            for a in (aq, ak, av)], axis=-1).astype(BF16)
    return main, attn


def kernel(x, mem, norm_mix, w_in, mlstm_conv, mlstm_gate_b, mlstm_norm, attn_qk_norm, gmlp_norm, gmlp_ws,
           gmlp_bs, w_branch_a, w_branch_b, w_branch_c, w_out, rel_bias, norm_xattn, norm_mem, w_xq, w_xkv,
           xattn_qk_norm, w_xo, norm_ffn, router_w, router_b, w_expert_gate, w_expert_up, w_expert_down):
    b, s, d = x.shape
    t = b * s
    depth = w_in.shape[0]
    x2d = x.reshape(t, d)

    biases = [_attn_bias(rel_bias, g) for g in range(len(ATTN_PATTERNS))]
    rw_t = jnp.zeros((N_EXPERT_GROUPS, 8, d), F32).at[:, :EXPERTS_PER_GROUP].set(
        router_w.T.reshape(N_EXPERT_GROUPS, EXPERTS_PER_GROUP, d)).reshape(ROUTER_ROWS, d)
    rb = jnp.full((N_EXPERT_GROUPS, 8), NEG, F32).at[:, :EXPERTS_PER_GROUP].set(
        router_b.astype(F32).reshape(N_EXPERT_GROUPS, EXPERTS_PER_GROUP)).reshape(ROUTER_ROWS, 1)
    tril = jnp.tril(jnp.ones((GMLP_CHUNK, GMLP_CHUNK), bool))

    for l in range(depth):
        w_main, w_attn = _layout_w_in(w_in[l])
        proj = _inproj(x2d, norm_mix[l][None], w_main, tm=1024, tn=1280)
        aproj = _attnproj(x2d, norm_mix[l][None], w_attn)

        gates_row = proj[:, OFF_IF:OFF_IF + 8].astype(F32).reshape(b, s, 8).transpose(0, 2, 1)
        gb_col = jnp.zeros((1, IF_PAD), F32).at[0, :8].set(mlstm_gate_b[l])
        ya = _mlstm(proj, gates_row, mlstm_conv[l], gb_col, mlstm_gate_b[l].reshape(8, 1),
                    mlstm_norm[l][None], batch=b, seq=s, blk=MLSTM_BLOCK)

        gq = jnp.tile(attn_qk_norm[l, 0], HEADS_PER_GROUP)[None]
        gk = jnp.tile(attn_qk_norm[l, 1], HEADS_PER_GROUP)[None]
        ybs, lses = [], []
        for g, (_, dilation) in enumerate(ATTN_PATTERNS):
            o, lse = _dattn(aproj, biases[g], gq, gk, seq=s, group=g, dilation=dilation)
            ybs.append(o)
            lses.append(lse)

        ws = jnp.where(tril, gmlp_ws[l], 0.0).astype(BF16)
        bsb = jnp.broadcast_to(gmlp_bs[l][:, :, None], (GMLP_GROUPS, GMLP_CHUNK, GMLP_GC)).astype(F32)
        x2d = _merge(ya, ybs, lses, proj, x2d, w_branch_a[l].astype(BF16), w_branch_b[l].astype(BF16),
                     w_branch_c[l].astype(BF16), w_out[l].astype(BF16), ws, bsb, gmlp_norm[l][None], tm=256)

        k_mem, v_mem = _memkv(mem, norm_mem[l][None], w_xkv[l].astype(BF16), xattn_qk_norm[l, 1][None])
        x2d, hf, gates_t = _xattn(x2d, k_mem, v_mem, norm_xattn[l][None], w_xq[l].astype(BF16),
                                  xattn_qk_norm[l, 0][None], w_xo[l].astype(BF16), norm_ffn[l][None],
                                  rw_t, rb, seq=s, tm=512)

        gates = gates_t.reshape(N_EXPERT_GROUPS, 8, t)[:, :EXPERTS_PER_GROUP].reshape(N_EXPERTS, t).T
        gates = jnp.pad(gates, ((0, 0), (0, 128 - N_EXPERTS)))
        x2d = _moe(hf, x2d, gates, w_expert_gate[l].astype(BF16), w_expert_up[l].astype(BF16),
                   w_expert_down[l].astype(BF16), tm=1024)

    return x2d.reshape(b, s, d)
```

```python
import functools
import math

import jax
import jax.numpy as jnp
import numpy as np
from jax import lax
from jax.experimental import pallas as pl
from jax.experimental.pallas import tpu as pltpu
from jax.experimental.pallas import tpu_sc as plsc

F32 = jnp.float32
BF16 = jnp.bfloat16

EPS = 1e-6
NEG = -1e30

MLSTM_HEADS = 4
MLSTM_DH = 128
MLSTM_W = MLSTM_HEADS * MLSTM_DH
CONV_WIDTH = 4
MLSTM_BLOCK = 128

ATTN_PATTERNS = ((128, 1), (512, 4), (2048, 16))
HEADS_PER_GROUP = 4
ATTN_DH = 64
ATTN_GW = HEADS_PER_GROUP * ATTN_DH
ATTN_W = len(ATTN_PATTERNS) * ATTN_GW
ATTN_BLOCK = 128
REL_BUCKETS = 32
REL_MAX_DIST = 2048

GMLP_GROUPS = 4
GMLP_GC = 128
GMLP_W = GMLP_GROUPS * GMLP_GC
GMLP_CHUNK = 128

XATTN_HEADS = 4
XATTN_DH = 128
XATTN_W = XATTN_HEADS * XATTN_DH

N_EXPERTS = 16
N_EXPERT_GROUPS = 4
EXPERTS_PER_GROUP = 4
ROUTER_ROWS = 8 * N_EXPERT_GROUPS

N_BRANCH = 3

MOE_TM = 256
ROW_CHUNKS = 4
SC_CORES, SC_SUBCORES = 2, 16
SC_WINDOW = 128

OFF_MQ, OFF_MK, OFF_MV, OFF_MO = 0, 512, 1024, 1536
OFF_GU, OFF_GV = 2048, 2560
OFF_GATE = 3072
OFF_IF = 6144
IF_PAD = 256
N_PROJ = OFF_IF + IF_PAD

ATTN_TILE = 2048
ATTN_SUB = ATTN_TILE // ATTN_BLOCK

VMEM_LIMIT = 48 * 1024 * 1024


def _cparams(*sem):
    return pltpu.CompilerParams(dimension_semantics=sem, vmem_limit_bytes=VMEM_LIMIT)


def _rms(x, gain):
    return x * lax.rsqrt(jnp.mean(x * x, axis=-1, keepdims=True) + EPS) * gain


def _dot(a, b):
    return jnp.dot(a, b, preferred_element_type=F32)


def _dot_nt(a, b):
    return lax.dot_general(a, b, (((1,), (1,)), ((), ())), preferred_element_type=F32)


def _inproj_kernel(x_ref, g_ref, w_ref, o_ref, h_scr):
    @pl.when(pl.program_id(1) == 0)
    def _():
        h_scr[...] = _rms(x_ref[...], g_ref[...]).astype(BF16)

    o_ref[...] = _dot(h_scr[...], w_ref[...]).astype(o_ref.dtype)


def _inproj(x2d, gain, w, *, tm, tn):
    t, d = x2d.shape
    n = w.shape[1]
    return pl.pallas_call(
        _inproj_kernel,
        grid=(t // tm, n // tn),
        in_specs=[pl.BlockSpec((tm, d), lambda i, j: (i, 0)),
                  pl.BlockSpec((1, d), lambda i, j: (0, 0)),
                  pl.BlockSpec((d, tn), lambda i, j: (0, j))],
        out_specs=pl.BlockSpec((tm, tn), lambda i, j: (i, j)),
        out_shape=jax.ShapeDtypeStruct((t, n), BF16),
        scratch_shapes=[pltpu.VMEM((tm, d), BF16)],
        compiler_params=_cparams("parallel", "arbitrary"),
        name="inproj",
    )(x2d, gain, w)


def _log_sigmoid(x):
    return jnp.minimum(x, 0.0) - jnp.log(1.0 + jnp.exp(-jnp.abs(x)))


def _mlstm_kernel(qk_ref, v_ref, og_ref, gc_ref, gr_ref, cw_ref, gbc_ref, gbr_ref, ng_ref, y_ref,
                  xe_scr, c_scr, n_scr, m_scr, *, blk):
    heads, dh, w = MLSTM_HEADS, MLSTM_DH, MLSTM_W
    hp = lax.Precision.HIGHEST

    @pl.when(pl.program_id(1) == 0)
    def _():
        xe_scr[0:8, :] = jnp.zeros((8, 2 * w), F32)
        c_scr[...] = jnp.zeros_like(c_scr)
        n_scr[...] = jnp.zeros_like(n_scr)
        m_scr[...] = jnp.zeros_like(m_scr)

    xe_scr[8:8 + blk, :] = qk_ref[...].astype(F32)
    cw = cw_ref[...]
    conv = cw[CONV_WIDTH - 1:CONV_WIDTH, :] * xe_scr[8:8 + blk, :]
    for j in range(CONV_WIDTH - 1):
        off = 8 - (CONV_WIDTH - 1) + j
        conv = conv + cw[j:j + 1, :] * xe_scr[off:off + blk, :]
    xe_scr[0:8, :] = xe_scr[blk:blk + 8, :]
    qk = conv * jax.nn.sigmoid(conv)

    gcol = gc_ref[...].astype(F32) + gbc_ref[...]
    grow = gr_ref[0] + gbr_ref[...]
    ri = lax.broadcasted_iota(jnp.int32, (blk, blk), 0)
    ci = lax.broadcasted_iota(jnp.int32, (blk, blk), 1)
    causal = ri >= ci
    tril = causal.astype(F32)
    triu = (ri <= ci).astype(F32)
    bcol = jnp.dot(tril, _log_sigmoid(gcol), precision=hp, preferred_element_type=F32)
    brow = jnp.dot(_log_sigmoid(grow), triu, precision=hp, preferred_element_type=F32)

    for h in range(heads):
        sl = slice(h * dh, (h + 1) * dh)
        b_c = bcol[:, heads + h:heads + h + 1]
        i_c = gcol[:, h:h + 1]
        b_r = brow[heads + h:heads + h + 1, :]
        i_r = grow[h:h + 1, :]
        m_st = m_scr[h:h + 1, 0:1]
        c_st = c_scr[h]
        n_st = n_scr[h:h + 1, :]

        d_mat = jnp.where(causal, b_c - b_r + i_r, NEG)
        inter = b_c + m_st
        m_t = jnp.maximum(inter, jnp.max(d_mat, axis=-1, keepdims=True))
        w_intra = jnp.exp(d_mat - m_t)
        w_inter = jnp.exp(inter - m_t)

        q_f = qk[:, sl]
        k_f = qk[:, w + h * dh:w + (h + 1) * dh] * (dh ** -0.5)
        q_b = q_f.astype(BF16)
        k_b = k_f.astype(BF16)
        v_b = v_ref[:, sl]

        s = _dot_nt(q_b, k_b) * w_intra
        num = _dot(s.astype(BF16), v_b) + w_inter * _dot(q_b, c_st.astype(BF16))
        den = jnp.sum(s, axis=-1, keepdims=True) + w_inter * jnp.sum(q_f * n_st, axis=-1, keepdims=True)
        hh = num / jnp.maximum(jnp.abs(den), jnp.exp(-m_t))
        hn = _rms(hh, ng_ref[:, sl])
        y_ref[:, sl] = (hn * jax.nn.sigmoid(og_ref[:, sl].astype(F32))).astype(y_ref.dtype)

        b_last = b_c[blk - 1:blk, :]
        dec = b_last - b_c + i_c
        m_new = jnp.maximum(b_last + m_st, jnp.max(dec, axis=0, keepdims=True))
        w_k = jnp.exp(dec - m_new)
        w_c = jnp.exp(b_last + m_st - m_new)
        kw = k_f * w_k
        c_scr[h] = w_c * c_st + _dot(kw.T.astype(BF16), v_b)
        n_scr[h:h + 1, :] = w_c * n_st + jnp.sum(kw, axis=0, keepdims=True)
        m_scr[h:h + 1, :] = jnp.broadcast_to(m_new, (1, m_scr.shape[1]))


def _mlstm(proj, gates_row, conv_w, gb_col, gb_row, norm_g, *, batch, seq, blk):
    t = proj.shape[0]
    nblk = seq // blk
    w = MLSTM_W
    row = lambda b, c: b * nblk + c
    return pl.pallas_call(
        functools.partial(_mlstm_kernel, blk=blk),
        grid=(batch, nblk),
        in_specs=[pl.BlockSpec((blk, 2 * w), lambda b, c: (row(b, c), OFF_MQ // (2 * w))),
                  pl.BlockSpec((blk, w), lambda b, c: (row(b, c), OFF_MV // w)),
                  pl.BlockSpec((blk, w), lambda b, c: (row(b, c), OFF_MO // w)),
                  pl.BlockSpec((blk, IF_PAD), lambda b, c: (row(b, c), OFF_IF // IF_PAD)),
                  pl.BlockSpec((1, 8, blk), lambda b, c: (b, 0, c)),
                  pl.BlockSpec((CONV_WIDTH, 2 * w), lambda b, c: (0, 0)),
                  pl.BlockSpec((1, IF_PAD), lambda b, c: (0, 0)),
                  pl.BlockSpec((8, 1), lambda b, c: (0, 0)),
                  pl.BlockSpec((1, w), lambda b, c: (0, 0))],
        out_specs=pl.BlockSpec((blk, w), lambda b, c: (row(b, c), 0)),
        out_shape=jax.ShapeDtypeStruct((t, w), BF16),
        scratch_shapes=[pltpu.VMEM((blk + 8, 2 * w), F32),
                        pltpu.VMEM((MLSTM_HEADS, MLSTM_DH, MLSTM_DH), F32),
                        pltpu.VMEM((8, MLSTM_DH), F32),
                        pltpu.VMEM((8, 128), F32)],
        compiler_params=_cparams("parallel", "arbitrary"),
        name="mlstm",
    )(proj, proj, proj, proj, gates_row, conv_w, gb_col, gb_row, norm_g)


def _attnproj_kernel(x_ref, g_ref, w_ref, o_ref, h_scr, r_scr):
    j = pl.program_id(1)

    @pl.when(j == 0)
    def _():
        h_scr[...] = _rms(x_ref[...], g_ref[...]).astype(BF16)

    res = _dot(h_scr[...], w_ref[...])
    ncb = r_scr.shape[0]
    for g, (_, dil) in enumerate(ATTN_PATTERNS):
        @pl.when(j == g)
        def _(dil=dil):
            if dil == 1:
                o_ref[...] = res.astype(o_ref.dtype)
            else:
                seg = ATTN_TILE // dil
                for c in range(ncb):
                    r_scr[c] = res[:, c * 128:(c + 1) * 128]
                for r in range(dil):
                    for c in range(ncb):
                        o_ref[r * seg:(r + 1) * seg, c * 128:(c + 1) * 128] = (
                            r_scr[c, pl.ds(r, seg, stride=dil), :].astype(o_ref.dtype))


def _attnproj(x2d, gain, w):
    t, d = x2d.shape
    ng = len(ATTN_PATTERNS)
    wcols = 3 * ATTN_GW
    return pl.pallas_call(
        _attnproj_kernel,
        grid=(t // ATTN_TILE, ng),
        in_specs=[pl.BlockSpec((ATTN_TILE, d), lambda i, j: (i, 0)),
                  pl.BlockSpec((1, d), lambda i, j: (0, 0)),
                  pl.BlockSpec((d, wcols), lambda i, j: (0, j))],
        out_specs=pl.BlockSpec((ATTN_TILE, wcols), lambda i, j: (i, j)),
        out_shape=jax.ShapeDtypeStruct((t, ng * wcols), BF16),
        scratch_shapes=[pltpu.VMEM((ATTN_TILE, d), BF16), pltpu.VMEM((wcols // 128, ATTN_TILE, 128), F32)],
        compiler_params=_cparams("parallel", "arbitrary"),
        name="attnproj",
    )(x2d, gain, w)


def _dattn_kernel(q_ref, kc_ref, kp_ref, vc_ref, vp_ref, bias_ref, gq_ref, gk_ref, o_ref, lse_ref,
                  kn_scr, o_scr, l_scr, *, dil):
    dh, blk = ATTN_DH, ATTN_BLOCK
    per = ATTN_SUB // dil
    first_tile = pl.program_id(1) == 0

    def norm_k(k):
        return jnp.concatenate([_rms(k[:, h * dh:(h + 1) * dh], gk_ref[:, h * dh:(h + 1) * dh])
                                for h in range(HEADS_PER_GROUP)], axis=-1).astype(BF16)

    for u in range(ATTN_SUB):
        kn_scr[u * blk:(u + 1) * blk, :] = norm_k(kc_ref[u * blk:(u + 1) * blk, :].astype(F32))

    for r in range(dil):
        for sub in range(per):
            u = r * per + sub
            rows = slice(u * blk, (u + 1) * blk)
            q = q_ref[rows, :].astype(F32)
            if sub > 0:
                prows = slice((u - 1) * blk, u * blk)
                kp, vprev_ref, masked = kn_scr[prows, :], vc_ref, None
            else:
                prows = slice((u + per - 1) * blk, (u + per) * blk)
                kp, vprev_ref, masked = norm_k(kp_ref[prows, :].astype(F32)), vp_ref, first_tile
            kc = kn_scr[rows, :]
            outs, lses = [], []
            for h in range(HEADS_PER_GROUP):
                sl = slice(h * dh, (h + 1) * dh)
                q_h = (_rms(q[:, sl], gq_ref[:, sl]) * (dh ** -0.5)).astype(BF16)
                bias = bias_ref[h]
                lp = _dot_nt(q_h, kp[:, sl]) + bias[:, :blk]
                if masked is not None:
                    lp = jnp.where(masked, NEG, lp)
                lc = _dot_nt(q_h, kc[:, sl]) + bias[:, blk:]
                m = jnp.maximum(jnp.max(lp, axis=-1, keepdims=True), jnp.max(lc, axis=-1, keepdims=True))
                pp = jnp.exp(lp - m)
                pc = jnp.exp(lc - m)
                l = jnp.sum(pp, axis=-1, keepdims=True) + jnp.sum(pc, axis=-1, keepdims=True)
                outs.append((_dot(pp.astype(BF16), vprev_ref[prows, sl])
                             + _dot(pc.astype(BF16), vc_ref[rows, sl])) / l)
                lses.append(jnp.broadcast_to(m + jnp.log(l), (blk, dh)))
            dst = pl.ds(sub * blk * dil + r, blk, stride=dil) if dil > 1 else rows
            for c in range(ATTN_GW // 128):
                o_scr[c, dst, :] = jnp.concatenate(outs[2 * c:2 * c + 2], axis=-1)
                l_scr[c, dst, :] = jnp.concatenate(lses[2 * c:2 * c + 2], axis=-1)
    for c in range(ATTN_GW // 128):
        o_ref[:, c * 128:(c + 1) * 128] = o_scr[c].astype(o_ref.dtype)
        lse_ref[:, c * 128:(c + 1) * 128] = l_scr[c]


def _dattn(aproj, bias, gq, gk, *, seq, group, dilation):
    t = aproj.shape[0]
    tiles = seq // ATTN_TILE
    cq, ck, cv = 3 * group, 3 * group + 1, 3 * group + 2
    blk = (ATTN_TILE, ATTN_GW)
    cur = lambda c: (lambda b, j: (b * tiles + j, c))
    prev = lambda c: (lambda b, j: (b * tiles + jnp.maximum(j - 1, 0), c))
    const2 = lambda b, j: (0, 0)
    return pl.pallas_call(
        functools.partial(_dattn_kernel, dil=dilation),
        grid=(t // seq, tiles),
        in_specs=[pl.BlockSpec(blk, cur(cq)),
                  pl.BlockSpec(blk, cur(ck)), pl.BlockSpec(blk, prev(ck)),
                  pl.BlockSpec(blk, cur(cv)), pl.BlockSpec(blk, prev(cv)),
                  pl.BlockSpec((HEADS_PER_GROUP, ATTN_BLOCK, 2 * ATTN_BLOCK), lambda b, j: (0, 0, 0)),
                  pl.BlockSpec((1, ATTN_GW), const2), pl.BlockSpec((1, ATTN_GW), const2)],
        out_specs=[pl.BlockSpec(blk, cur(0)), pl.BlockSpec(blk, cur(0))],
        out_shape=[jax.ShapeDtypeStruct((t, ATTN_GW), BF16), jax.ShapeDtypeStruct((t, ATTN_GW), F32)],
        scratch_shapes=[pltpu.VMEM(blk, BF16),
                        pltpu.VMEM((ATTN_GW // 128, ATTN_TILE, 128), F32),
                        pltpu.VMEM((ATTN_GW // 128, ATTN_TILE, 128), F32)],
        compiler_params=_cparams("parallel", "arbitrary"),
        name=f"dattn{group}",
    )(aproj, aproj, aproj, aproj, aproj, bias, gq, gk)


def _rel_bucket(n):
    max_exact = REL_BUCKETS // 2
    nf = jnp.maximum(n, 1).astype(F32)
    log_b = max_exact + (jnp.log(nf / max_exact) / math.log(REL_MAX_DIST / max_exact)
                         * (REL_BUCKETS - max_exact)).astype(jnp.int32)
    return jnp.where(n < max_exact, n, jnp.minimum(log_b, REL_BUCKETS - 1))


def _attn_bias(rel_bias, group):
    window, dilation = ATTN_PATTERNS[group]
    steps = window // dilation
    hp = lax.Precision.HIGHEST
    hs = slice(group * HEADS_PER_GROUP, (group + 1) * HEADS_PER_GROUP)
    bucket = _rel_bucket(jnp.arange(steps + 1) * dilation)
    bias_steps = jnp.dot(jax.nn.one_hot(bucket, REL_BUCKETS, dtype=F32), rel_bias[:, hs].astype(F32),
                         precision=hp)
    qi = jnp.arange(ATTN_BLOCK)[:, None]
    ki = jnp.arange(2 * ATTN_BLOCK)[None, :]
    dist = ATTN_BLOCK + qi - ki
    ok = (dist >= 0) & (dist <= steps)
    sel = jax.nn.one_hot(jnp.clip(dist, 0, steps).reshape(-1), steps + 1, dtype=F32)
    bias = jnp.dot(sel, bias_steps, precision=hp).T.reshape(HEADS_PER_GROUP, ATTN_BLOCK, 2 * ATTN_BLOCK)
    return jnp.where(ok[None], bias, NEG)


def _merge_kernel(ya_ref, yb0_ref, yb1_ref, yb2_ref, l0_ref, l1_ref, l2_ref, gu_ref, gv_ref, gate_ref,
                  x_ref, wa_ref, wb_ref, wc_ref, wo_ref, ws_ref, bs_ref, gg_ref, o_ref, yc_scr, *, tm):
    d = x_ref.shape[1]
    l0, l1, l2 = l0_ref[...], l1_ref[...], l2_ref[...]
    mx = jnp.maximum(jnp.maximum(l0, l1), l2)
    e0, e1, e2 = jnp.exp(l0 - mx), jnp.exp(l1 - mx), jnp.exp(l2 - mx)
    inv = 1.0 / (e0 + e1 + e2)
    yb = jnp.concatenate([(yb0_ref[...].astype(F32) * (e0 * inv)).astype(BF16),
                          (yb1_ref[...].astype(F32) * (e1 * inv)).astype(BF16),
                          (yb2_ref[...].astype(F32) * (e2 * inv)).astype(BF16)], axis=-1)

    for j in range(tm // GMLP_CHUNK):
        rows = slice(j * GMLP_CHUNK, (j + 1) * GMLP_CHUNK)
        for g in range(GMLP_GROUPS):
            cols = slice(g * GMLP_GC, (g + 1) * GMLP_GC)
            u = jax.nn.gelu(gu_ref[rows, cols].astype(F32))
            v = _rms(jax.nn.gelu(gv_ref[rows, cols].astype(F32)), gg_ref[:, cols])
            mixed = _dot(ws_ref[g], v.astype(BF16)) + bs_ref[g]
            yc_scr[rows, cols] = (u * mixed).astype(BF16)

    merged = jax.nn.sigmoid(gate_ref[:, 0:d].astype(F32)) * _dot(ya_ref[...], wa_ref[...])
    merged = merged + jax.nn.sigmoid(gate_ref[:, d:2 * d].astype(F32)) * _dot(yb, wb_ref[...])
    merged = merged + jax.nn.sigmoid(gate_ref[:, 2 * d:3 * d].astype(F32)) * _dot(yc_scr[...], wc_ref[...])
    o_ref[...] = x_ref[...] + _dot(merged.astype(BF16), wo_ref[...])


def _merge(ya, ybs, lses, proj, x2d, wa, wb, wc, wo, ws, bsb, gg, *, tm):
    t, d = x2d.shape
    row = lambda c: (lambda i: (i, c))
    full2 = lambda i: (0, 0)
    full3 = lambda i: (0, 0, 0)
    gspec = pl.BlockSpec((tm, ATTN_GW), row(0))
    return pl.pallas_call(
        functools.partial(_merge_kernel, tm=tm),
        grid=(t // tm,),
        in_specs=[pl.BlockSpec((tm, MLSTM_W), row(0)),
                  gspec, gspec, gspec, gspec, gspec, gspec,
                  pl.BlockSpec((tm, GMLP_W), row(OFF_GU // GMLP_W)),
                  pl.BlockSpec((tm, GMLP_W), row(OFF_GV // GMLP_W)),
                  pl.BlockSpec((tm, N_BRANCH * d), row(OFF_GATE // (N_BRANCH * d))),
                  pl.BlockSpec((tm, d), row(0)),
                  pl.BlockSpec(wa.shape, full2), pl.BlockSpec(wb.shape, full2),
                  pl.BlockSpec(wc.shape, full2), pl.BlockSpec(wo.shape, full2),
                  pl.BlockSpec(ws.shape, full3), pl.BlockSpec(bsb.shape, full3),
                  pl.BlockSpec(gg.shape, full2)],
        out_specs=pl.BlockSpec((tm, d), row(0)),
        out_shape=jax.ShapeDtypeStruct((t, d), F32),
        scratch_shapes=[pltpu.VMEM((tm, GMLP_W), BF16)],
        compiler_params=_cparams("parallel"),
        name="merge",
    )(ya, *ybs, *lses, proj, proj, proj, x2d, wa, wb, wc, wo, ws, bsb, gg)


def _memkv_kernel(mem_ref, g_ref, w_ref, gk_ref, k_ref, v_ref):
    dh, w = XATTN_DH, XATTN_W
    kv = _dot(_rms(mem_ref[0], g_ref[...]).astype(BF16), w_ref[...])
    for h in range(XATTN_HEADS):
        sl = slice(h * dh, (h + 1) * dh)
        k_ref[0, :, sl] = _rms(kv[:, sl], gk_ref[...]).astype(k_ref.dtype)
    v_ref[0] = kv[:, w:].astype(v_ref.dtype)


def _memkv(mem, gain, w_kv, gk):
    b, m, d = mem.shape
    full2 = lambda i: (0, 0)
    return pl.pallas_call(
        _memkv_kernel,
        grid=(b,),
        in_specs=[pl.BlockSpec((1, m, d), lambda i: (i, 0, 0)),
                  pl.BlockSpec((1, d), full2),
                  pl.BlockSpec(w_kv.shape, full2),
                  pl.BlockSpec((1, XATTN_DH), full2)],
        out_specs=[pl.BlockSpec((1, m, XATTN_W), lambda i: (i, 0, 0)),
                   pl.BlockSpec((1, m, XATTN_W), lambda i: (i, 0, 0))],
        out_shape=[jax.ShapeDtypeStruct((b, m, XATTN_W), BF16),
                   jax.ShapeDtypeStruct((b, m, XATTN_W), BF16)],
        compiler_params=_cparams("parallel"),
        name="memkv",
    )(mem, gain, w_kv, gk)


def _route(logits):
    tm = logits.shape[1]
    e = jnp.exp(logits - jnp.max(logits, axis=0, keepdims=True))
    probs = e / jnp.sum(e, axis=0, keepdims=True)
    rowi = lax.broadcasted_iota(jnp.int32, (8, tm), 0)
    real = rowi < EXPERTS_PER_GROUP
    tops = []
    for g in range(N_EXPERT_GROUPS):
        pg = jnp.where(real, probs[8 * g:8 * g + 8, :], -0.5)
        m1 = jnp.max(pg, axis=0, keepdims=True)
        i1 = jnp.min(jnp.where(pg == m1, rowi, 8), axis=0, keepdims=True)
        pg2 = jnp.where(rowi == i1, -1.0, pg)
        m2 = jnp.max(pg2, axis=0, keepdims=True)
        i2 = jnp.min(jnp.where(pg2 == m2, rowi, 8), axis=0, keepdims=True)
        tops.append((m1, i1, m2, i2))
    best = jnp.zeros((1, tm), jnp.int32)
    best_score = tops[0][0] + tops[0][2]
    for g in range(1, N_EXPERT_GROUPS):
        score = tops[g][0] + tops[g][2]
        better = score > best_score
        best = jnp.where(better, g, best)
        best_score = jnp.where(better, score, best_score)
    m1, i1, m2, i2 = tops[0]
    for g in range(1, N_EXPERT_GROUPS):
        m1, i1, m2, i2 = (jnp.where(best == g, new, old) for new, old in zip(tops[g], (m1, i1, m2, i2)))
    tot = m1 + m2
    base = best * EXPERTS_PER_GROUP
    return base + i1, base + i2, m1 / tot, m2 / tot


def _pack_bf16_pairs(x):
    n = x.shape[1] // 2
    hi = lax.bitcast_convert_type(x[:, :n].astype(BF16).astype(F32), jnp.uint32)
    lo = lax.bitcast_convert_type(x[:, n:].astype(BF16).astype(F32), jnp.uint32)
    return hi | (lo >> 16)


def _unpack_bf16_pairs(p):
    hi = lax.bitcast_convert_type(p & jnp.uint32(0xFFFF0000), F32)
    lo = lax.bitcast_convert_type(p << 16, F32)
    return hi, lo


def _store_row_chunks(ref, packed):
    m = packed.shape[0]
    for j in range(ROW_CHUNKS):
        ref[pl.ds(j, m, stride=ROW_CHUNKS), :] = packed[:, j * 128:(j + 1) * 128]


def _load_row_chunks(ref, m):
    return jnp.concatenate([ref[pl.ds(j, m, stride=ROW_CHUNKS), :] for j in range(ROW_CHUNKS)], axis=-1)


def _xattn_kernel(x_ref, k_ref, v_ref, gx_ref, wq_ref, gq_ref, wo_ref, gf_ref, rw_ref, rb_ref,
                  xo_ref, hf_ref, eidx_ref, wts_ref):
    dh = XATTN_DH
    x = x_ref[...]
    q = _dot(_rms(x, gx_ref[...]).astype(BF16), wq_ref[...])
    outs = []
    for h in range(XATTN_HEADS):
        sl = slice(h * dh, (h + 1) * dh)
        q_h = (_rms(q[:, sl], gq_ref[...]) * (dh ** -0.5)).astype(BF16)
        logits = _dot_nt(q_h, k_ref[0, :, sl])
        p = jnp.exp(logits - jnp.max(logits, axis=-1, keepdims=True))
        o = _dot(p.astype(BF16), v_ref[0, :, sl]) / jnp.sum(p, axis=-1, keepdims=True)
        outs.append(o.astype(BF16))
    xn = x + _dot(jnp.concatenate(outs, axis=-1), wo_ref[...])
    xo_ref[...] = xn
    hf = _rms(xn, gf_ref[...])
    _store_row_chunks(hf_ref, _pack_bf16_pairs(hf))
    logits_t = lax.dot_general(rw_ref[...], hf, (((1,), (1,)), ((), ())),
                               precision=lax.Precision.HIGHEST, preferred_element_type=F32) + rb_ref[...]
    e1, e2, w1, w2 = _route(logits_t)
    tm = x.shape[0]
    eidx_ref[...] = jnp.concatenate([e1, e2, jnp.zeros((6, tm), jnp.int32)], axis=0)
    wts_ref[...] = jnp.concatenate([w1, w2, jnp.zeros((6, tm), F32)], axis=0)


def _xattn(x2d, k, v, gx, wq, gq, wo, gf, rw_t, rb, *, seq, tm):
    t, d = x2d.shape
    per_b = seq // tm
    full2 = lambda i: (0, 0)
    kv_spec = pl.BlockSpec((1,) + k.shape[1:], lambda i: (i // per_b, 0, 0))
    return pl.pallas_call(
        _xattn_kernel,
        grid=(t // tm,),
        in_specs=[pl.BlockSpec((tm, d), lambda i: (i, 0)), kv_spec, kv_spec,
                  pl.BlockSpec((1, d), full2), pl.BlockSpec(wq.shape, full2),
                  pl.BlockSpec((1, XATTN_DH), full2), pl.BlockSpec(wo.shape, full2),
                  pl.BlockSpec((1, d), full2), pl.BlockSpec(rw_t.shape, full2),
                  pl.BlockSpec(rb.shape, full2)],
        out_specs=[pl.BlockSpec((tm, d), lambda i: (i, 0)),
                   pl.BlockSpec((ROW_CHUNKS * tm, 128), lambda i: (i, 0)),
                   pl.BlockSpec((8, tm), lambda i: (0, i)),
                   pl.BlockSpec((8, tm), lambda i: (0, i))],
        out_shape=[jax.ShapeDtypeStruct((t, d), F32),
                   jax.ShapeDtypeStruct((ROW_CHUNKS * t, 128), jnp.uint32),
                   jax.ShapeDtypeStruct((8, t), jnp.int32),
                   jax.ShapeDtypeStruct((8, t), F32)],
        compiler_params=_cparams("parallel"),
        name="xattn_router",
    )(x2d, k, v, gx, wq, gq, wo, gf, rw_t, rb)


def _moe_plan_kernel(eidx_ref, d_ref, te_ref, na_ref, cnt_scr, carry_scr, *, tb, tm):
    ne = N_EXPERTS
    hp = lax.Precision.HIGHEST
    phase, j = pl.program_id(0), pl.program_id(1)
    rows = lax.broadcasted_iota(jnp.int32, (ne, tb), 0)
    oh1 = rows == eidx_ref[0:1, :]
    oh2 = rows == eidx_ref[1:2, :]
    a = oh1.astype(F32) + oh2.astype(F32)
    blk_cnt = jnp.broadcast_to(jnp.sum(a, axis=1, keepdims=True), cnt_scr.shape)

    @pl.when((phase == 0) & (j == 0))
    def _():
        cnt_scr[...] = jnp.zeros_like(cnt_scr)

    @pl.when(phase == 0)
    def _():
        cnt_scr[...] += blk_cnt

    @pl.when((phase == 1) & (j == 0))
    def _():
        padded = jnp.ceil(cnt_scr[...] * (1.0 / tm)) * tm
        er = lax.broadcasted_iota(jnp.int32, (ne, ne), 0)
        ec = lax.broadcasted_iota(jnp.int32, (ne, ne), 1)
        off = jnp.dot((ec < er).astype(F32), padded, precision=hp, preferred_element_type=F32)
        carry_scr[...] = off
        seg_end = (off + padded)[:, 0:1]
        tile_start = lax.broadcasted_iota(jnp.int32, (ne, te_ref.shape[1]), 1).astype(F32) * tm
        te = jnp.sum((seg_end <= tile_start).astype(F32), axis=0, keepdims=True)
        te_ref[...] = jnp.broadcast_to(jnp.minimum(te, ne - 1.0), te_ref.shape).astype(jnp.int32)
        total = jnp.sum(padded[:, 0:1], axis=0, keepdims=True)
        na_ref[...] = jnp.broadcast_to(total * (1.0 / tm), na_ref.shape).astype(jnp.int32)

    @pl.when(phase == 1)
    def _():
        before = (lax.broadcasted_iota(jnp.int32, (tb, tb), 0)
                  < lax.broadcasted_iota(jnp.int32, (tb, tb), 1)).astype(BF16)
        rank = carry_scr[:, 0:1] + _dot(a.astype(BF16), before)
        d1 = jnp.sum(jnp.where(oh1, rank, 0.0), axis=0, keepdims=True)
        d2 = jnp.sum(jnp.where(oh2, rank, 0.0), axis=0, keepdims=True)
        d_ref[...] = jnp.concatenate([d1, d2, jnp.zeros((6, tb), F32)], axis=0).astype(jnp.int32)
        carry_scr[...] += blk_cnt


def _moe_plan(eidx, *, tm, n_tiles, tb=512):
    t = eidx.shape[1]
    ntp = -(-n_tiles // 128) * 128
    return pl.pallas_call(
        functools.partial(_moe_plan_kernel, tb=tb, tm=tm),
        grid=(2, t // tb),
        in_specs=[pl.BlockSpec((8, tb), lambda p, j: (0, j))],
        out_specs=[pl.BlockSpec((8, tb), lambda p, j: (0, j * p)),
                   pl.BlockSpec((8, ntp), lambda p, j: (0, 0)),
                   pl.BlockSpec((8, 128), lambda p, j: (0, 0))],
        out_shape=[jax.ShapeDtypeStruct((8, t), jnp.int32),
                   jax.ShapeDtypeStruct((8, ntp), jnp.int32),
                   jax.ShapeDtypeStruct((8, 128), jnp.int32)],
        scratch_shapes=[pltpu.VMEM((N_EXPERTS, 128), F32), pltpu.VMEM((N_EXPERTS, 128), F32)],
        compiler_params=_cparams("arbitrary", "arbitrary"),
        name="moe_plan",
    )(eidx)


def _sc_mesh():
    return plsc.VectorSubcoreMesh(core_axis_name="c", subcore_axis_name="s",
                                  num_cores=SC_CORES, num_subcores=SC_SUBCORES)


def _sc_dispatch(rows, i1, i2, n_out):
    n = rows.shape[0]

    @functools.partial(pl.kernel, out_type=jax.ShapeDtypeStruct((n_out, 128), rows.dtype), mesh=_sc_mesh(),
                       name="moe_dispatch")
    def k(x_hbm, i1_hbm, i2_hbm, o_hbm):
        def body(x_vmem, i1_vmem, i2_vmem):
            pltpu.sync_copy(x_vmem, o_hbm.at[i1_vmem.at[0]])
            pltpu.sync_copy(x_vmem, o_hbm.at[i2_vmem.at[0]])

        pltpu.emit_pipeline(
            body, grid=(n // SC_WINDOW,),
            in_specs=[pl.BlockSpec((SC_WINDOW, 128), lambda i: (i, 0)),
                      pl.BlockSpec((1, SC_WINDOW), lambda i: (0, i)),
                      pl.BlockSpec((1, SC_WINDOW), lambda i: (0, i))],
            out_specs=[],
            core_axis_name=("c", "s"), dimension_semantics=(pltpu.PARALLEL,),
        )(x_hbm, i1_hbm, i2_hbm)

    return k(rows, i1, i2)


def _sc_collect(table, i1, i2):
    n = i1.shape[1]
    out = jax.ShapeDtypeStruct((n, 128), table.dtype)

    @functools.partial(pl.kernel, out_type=(out, out), mesh=_sc_mesh(), name="moe_collect")
    def k(t_hbm, i1_hbm, i2_hbm, o1_hbm, o2_hbm):
        def body(i1_vmem, i2_vmem, o1_vmem, o2_vmem):
            pltpu.sync_copy(t_hbm.at[i1_vmem.at[0]], o1_vmem)
            pltpu.sync_copy(t_hbm.at[i2_vmem.at[0]], o2_vmem)

        pltpu.emit_pipeline(
            body, grid=(n // SC_WINDOW,),
            in_specs=[pl.BlockSpec((1, SC_WINDOW), lambda i: (0, i)),
                      pl.BlockSpec((1, SC_WINDOW), lambda i: (0, i))],
            out_specs=[pl.BlockSpec((SC_WINDOW, 128), lambda i: (i, 0)),
                       pl.BlockSpec((SC_WINDOW, 128), lambda i: (i, 0))],
            core_axis_name=("c", "s"), dimension_semantics=(pltpu.PARALLEL,),
        )(i1_hbm, i2_hbm, o1_hbm, o2_hbm)

    return k(table, i1, i2)


def _experts_kernel(te_ref, na_ref, xs_ref, wg_ref, wu_ref, wd_ref, y_ref, *, tm):
    del te_ref

    @pl.when(pl.program_id(0) < na_ref[0])
    def _():
        hi, lo = _unpack_bf16_pairs(_load_row_chunks(xs_ref, tm))
        h = jnp.concatenate([hi, lo], axis=-1).astype(BF16)
        up = _dot(h, wg_ref[0])
        act = up * jax.nn.sigmoid(up) * _dot(h, wu_ref[0])
        _store_row_chunks(y_ref, _pack_bf16_pairs(_dot(act.astype(BF16), wd_ref[0])))


def _experts(tile_expert, n_active, xs, wg, wu, wd, *, tm):
    n_tiles = tile_expert.shape[0]
    _, d, dff = wg.shape
    rows = lambda i, te, na: (jnp.minimum(i, na[0] - 1), 0)
    return pl.pallas_call(
        functools.partial(_experts_kernel, tm=tm),
        grid_spec=pltpu.PrefetchScalarGridSpec(
            num_scalar_prefetch=2,
            grid=(n_tiles,),
            in_specs=[pl.BlockSpec((ROW_CHUNKS * tm, 128), rows),
                      pl.BlockSpec((1, d, dff), lambda i, te, na: (te[i], 0, 0)),
                      pl.BlockSpec((1, d, dff), lambda i, te, na: (te[i], 0, 0)),
                      pl.BlockSpec((1, dff, d), lambda i, te, na: (te[i], 0, 0))],
            out_specs=pl.BlockSpec((ROW_CHUNKS * tm, 128), rows)),
        out_shape=jax.ShapeDtypeStruct(xs.shape, xs.dtype),
        compiler_params=_cparams("arbitrary"),
        name="moe_experts",
    )(tile_expert, n_active, xs, wg, wu, wd)


def _moe_combine_kernel(x_ref, y1_ref, y2_ref, w_ref, o_ref, *, tm):
    half = x_ref.shape[1] // 2
    hi1, lo1 = _unpack_bf16_pairs(_load_row_chunks(y1_ref, tm))
    hi2, lo2 = _unpack_bf16_pairs(_load_row_chunks(y2_ref, tm))
    w1, w2 = w_ref[:, 0:1], w_ref[:, 1:2]
    o_ref[:, :half] = x_ref[:, :half] + w1 * hi1 + w2 * hi2
    o_ref[:, half:] = x_ref[:, half:] + w1 * lo1 + w2 * lo2


def _moe_combine(x2d, y1, y2, wcol, *, tm):
    t, d = x2d.shape
    chunk_spec = pl.BlockSpec((ROW_CHUNKS * tm, 128), lambda i: (i, 0))
    return pl.pallas_call(
        functools.partial(_moe_combine_kernel, tm=tm),
        grid=(t // tm,),
        in_specs=[pl.BlockSpec((tm, d), lambda i: (i, 0)), chunk_spec, chunk_spec,
                  pl.BlockSpec((tm, wcol.shape[1]), lambda i: (i, 0))],
        out_specs=pl.BlockSpec((tm, d), lambda i: (i, 0)),
        out_shape=jax.ShapeDtypeStruct((t, d), F32),
        compiler_params=_cparams("parallel"),
        name="moe_combine",
    )(x2d, y1, y2, wcol)


def _moe(x2d, hf_rows, eidx, wts, wg, wu, wd):
    t = x2d.shape[0]
    tm = MOE_TM
    n_tiles = 2 * t // tm + N_EXPERTS
    dest, te, na = _moe_plan(eidx, tm=tm, n_tiles=n_tiles)
    chunk = jnp.arange(ROW_CHUNKS, dtype=jnp.int32)[None, :]
    i1 = (ROW_CHUNKS * dest[0][:, None] + chunk).reshape(1, ROW_CHUNKS * t)
    i2 = (ROW_CHUNKS * dest[1][:, None] + chunk).reshape(1, ROW_CHUNKS * t)
    xs = _sc_dispatch(hf_rows, i1, i2, ROW_CHUNKS * n_tiles * tm)
    ys = _experts(te[0, :n_tiles], na[0, :1], xs, wg, wu, wd, tm=tm)
    y1, y2 = _sc_collect(ys, i1, i2)
    return _moe_combine(x2d, y1, y2, wts[:2].T, tm=512)


def _layout_w_in(w):
    sizes = (MLSTM_W, MLSTM_W, MLSTM_W, MLSTM_W, MLSTM_HEADS, MLSTM_HEADS,
             ATTN_W, ATTN_W, ATTN_W, GMLP_W, GMLP_W, N_BRANCH * w.shape[0])
    pts = np.cumsum(sizes)[:-1]
    mq, mk, mv, mo, mi, mf, aq, ak, av, gu, gv, gate = jnp.split(w, pts, axis=-1)
    pad = jnp.zeros((w.shape[0], IF_PAD - 2 * MLSTM_HEADS), w.dtype)
    main = jnp.concatenate([mq, mk, mv, mo, gu, gv, gate, mi, mf, pad], axis=-1).astype(BF16)
    attn = jnp.concatenate([a[:, g * ATTN_GW:(g + 1) * ATTN_GW] for g in range(len(ATTN_PATTERNS))
                            for a in (aq, ak, av)], axis=-1).astype(BF16)
    return main, attn


def kernel(x, mem, norm_mix, w_in, mlstm_conv, mlstm_gate_b, mlstm_norm, attn_qk_norm, gmlp_norm, gmlp_ws,
           gmlp_bs, w_branch_a, w_branch_b, w_branch_c, w_out, rel_bias, norm_xattn, norm_mem, w_xq, w_xkv,
           xattn_qk_norm, w_xo, norm_ffn, router_w, router_b, w_expert_gate, w_expert_up, w_expert_down):
    b, s, d = x.shape
    t = b * s
    depth = w_in.shape[0]
    x2d = x.reshape(t, d)

    biases = [_attn_bias(rel_bias, g) for g in range(len(ATTN_PATTERNS))]
    rw_t = jnp.zeros((N_EXPERT_GROUPS, 8, d), F32).at[:, :EXPERTS_PER_GROUP].set(
        router_w.T.reshape(N_EXPERT_GROUPS, EXPERTS_PER_GROUP, d)).reshape(ROUTER_ROWS, d)
    rb = jnp.full((N_EXPERT_GROUPS, 8), NEG, F32).at[:, :EXPERTS_PER_GROUP].set(
        router_b.astype(F32).reshape(N_EXPERT_GROUPS, EXPERTS_PER_GROUP)).reshape(ROUTER_ROWS, 1)
    tril = jnp.tril(jnp.ones((GMLP_CHUNK, GMLP_CHUNK), bool))

    for l in range(depth):
        w_main, w_attn = _layout_w_in(w_in[l])
        proj = _inproj(x2d, norm_mix[l][None], w_main, tm=1024, tn=1280)
        aproj = _attnproj(x2d, norm_mix[l][None], w_attn)

        gates_row = proj[:, OFF_IF:OFF_IF + 8].astype(F32).reshape(b, s, 8).transpose(0, 2, 1)
        gb_col = jnp.zeros((1, IF_PAD), F32).at[0, :8].set(mlstm_gate_b[l])
        ya = _mlstm(proj, gates_row, mlstm_conv[l], gb_col, mlstm_gate_b[l].reshape(8, 1),
                    mlstm_norm[l][None], batch=b, seq=s, blk=MLSTM_BLOCK)

        gq = jnp.tile(attn_qk_norm[l, 0], HEADS_PER_GROUP)[None]
        gk = jnp.tile(attn_qk_norm[l, 1], HEADS_PER_GROUP)[None]
        ybs, lses = [], []
        for g, (_, dilation) in enumerate(ATTN_PATTERNS):
            o, lse = _dattn(aproj, biases[g], gq, gk, seq=s, group=g, dilation=dilation)
            ybs.append(o)
            lses.append(lse)

        ws = jnp.where(tril, gmlp_ws[l], 0.0).astype(BF16)
        bsb = jnp.broadcast_to(gmlp_bs[l][:, :, None], (GMLP_GROUPS, GMLP_CHUNK, GMLP_GC)).astype(F32)
        x2d = _merge(ya, ybs, lses, proj, x2d, w_branch_a[l].astype(BF16), w_branch_b[l].astype(BF16),
                     w_branch_c[l].astype(BF16), w_out[l].astype(BF16), ws, bsb, gmlp_norm[l][None], tm=256)

        k_mem, v_mem = _memkv(mem, norm_mem[l][None], w_xkv[l].astype(BF16), xattn_qk_norm[l, 1][None])
        x2d, hf_rows, eidx, wts = _xattn(x2d, k_mem, v_mem, norm_xattn[l][None], w_xq[l].astype(BF16),
                                         xattn_qk_norm[l, 0][None], w_xo[l].astype(BF16), norm_ffn[l][None],
                                         rw_t, rb, seq=s, tm=512)

        x2d = _moe(x2d, hf_rows, eidx, wts, w_expert_gate[l].astype(BF16), w_expert_up[l].astype(BF16),
                   w_expert_down[l].astype(BF16))

    return x2d.reshape(b, s, d)
```

```python
import functools
import math

import jax
import jax.numpy as jnp
import numpy as np
from jax import lax
from jax.experimental import pallas as pl
from jax.experimental.pallas import tpu as pltpu
from jax.experimental.pallas import tpu_sc as plsc

F32 = jnp.float32
BF16 = jnp.bfloat16

EPS = 1e-6
NEG = -1e30

MLSTM_HEADS = 4
MLSTM_DH = 128
MLSTM_W = MLSTM_HEADS * MLSTM_DH
CONV_WIDTH = 4
MLSTM_BLOCK = 128

ATTN_PATTERNS = ((128, 1), (512, 4), (2048, 16))
HEADS_PER_GROUP = 4
ATTN_DH = 64
ATTN_GW = HEADS_PER_GROUP * ATTN_DH
ATTN_W = len(ATTN_PATTERNS) * ATTN_GW
ATTN_BLOCK = 128
REL_BUCKETS = 32
REL_MAX_DIST = 2048

GMLP_GROUPS = 4
GMLP_GC = 128
GMLP_W = GMLP_GROUPS * GMLP_GC
GMLP_CHUNK = 128

XATTN_HEADS = 4
XATTN_DH = 128
XATTN_W = XATTN_HEADS * XATTN_DH

N_EXPERTS = 16
N_EXPERT_GROUPS = 4
EXPERTS_PER_GROUP = 4
ROUTER_ROWS = 8 * N_EXPERT_GROUPS

N_BRANCH = 3

MOE_TM = 256
ROW_CHUNKS = 4
SC_CORES, SC_SUBCORES = 2, 16
SC_WINDOW = 128

OFF_MQ, OFF_MK, OFF_MV, OFF_MO = 0, 512, 1024, 1536
OFF_GU, OFF_GV = 2048, 2560
OFF_GATE = 3072
OFF_IF = 6144
IF_PAD = 256
N_PROJ = OFF_IF + IF_PAD

ATTN_TILE = 2048
ATTN_SUB = ATTN_TILE // ATTN_BLOCK
ATTN_SLAB = 2 * ATTN_DH
ATTN_COLS = HEADS_PER_GROUP * ATTN_SLAB + 2 * ATTN_GW

VMEM_LIMIT = 48 * 1024 * 1024


def _cparams(*sem):
    return pltpu.CompilerParams(dimension_semantics=sem, vmem_limit_bytes=VMEM_LIMIT)


def _rms(x, gain):
    return x * lax.rsqrt(jnp.mean(x * x, axis=-1, keepdims=True) + EPS) * gain


def _dot(a, b):
    return jnp.dot(a, b, preferred_element_type=F32)


def _dot_nt(a, b):
    return lax.dot_general(a, b, (((1,), (1,)), ((), ())), preferred_element_type=F32)


def _inproj_kernel(x_ref, g_ref, w_ref, o_ref, h_ref):
    @pl.when(pl.program_id(1) == 0)
    def _():
        h_ref[...] = _rms(x_ref[...], g_ref[...]).astype(BF16)

    o_ref[...] = _dot(h_ref[...], w_ref[...]).astype(o_ref.dtype)


def _inproj(x2d, gain, w, *, tm, tn):
    t, d = x2d.shape
    n = w.shape[1]
    return pl.pallas_call(
        _inproj_kernel,
        grid=(t // tm, n // tn),
        in_specs=[pl.BlockSpec((tm, d), lambda i, j: (i, 0)),
                  pl.BlockSpec((1, d), lambda i, j: (0, 0)),
                  pl.BlockSpec((d, tn), lambda i, j: (0, j))],
        out_specs=[pl.BlockSpec((tm, tn), lambda i, j: (i, j)),
                   pl.BlockSpec((tm, d), lambda i, j: (i, 0))],
        out_shape=[jax.ShapeDtypeStruct((t, n), BF16), jax.ShapeDtypeStruct((t, d), BF16)],
        compiler_params=_cparams("parallel", "arbitrary"),
        name="inproj",
    )(x2d, gain, w)


def _log_sigmoid(x):
    return jnp.minimum(x, 0.0) - jnp.log(1.0 + jnp.exp(-jnp.abs(x)))


def _mlstm_kernel(qk_ref, v_ref, og_ref, gc_ref, gr_ref, cw_ref, gbc_ref, gbr_ref, ng_ref, y_ref,
                  xe_scr, c_scr, n_scr, m_scr, *, blk):
    heads, dh, w = MLSTM_HEADS, MLSTM_DH, MLSTM_W
    hp = lax.Precision.HIGHEST

    @pl.when(pl.program_id(1) == 0)
    def _():
        xe_scr[0:8, :] = jnp.zeros((8, 2 * w), F32)
        c_scr[...] = jnp.zeros_like(c_scr)
        n_scr[...] = jnp.zeros_like(n_scr)
        m_scr[...] = jnp.zeros_like(m_scr)

    xe_scr[8:8 + blk, :] = qk_ref[...].astype(F32)
    cw = cw_ref[...]
    conv = cw[CONV_WIDTH - 1:CONV_WIDTH, :] * xe_scr[8:8 + blk, :]
    for j in range(CONV_WIDTH - 1):
        off = 8 - (CONV_WIDTH - 1) + j
        conv = conv + cw[j:j + 1, :] * xe_scr[off:off + blk, :]
    xe_scr[0:8, :] = xe_scr[blk:blk + 8, :]
    qk = conv * jax.nn.sigmoid(conv)

    gcol = gc_ref[...].astype(F32) + gbc_ref[...]
    grow = gr_ref[0] + gbr_ref[...]
    ri = lax.broadcasted_iota(jnp.int32, (blk, blk), 0)
    ci = lax.broadcasted_iota(jnp.int32, (blk, blk), 1)
    causal = ri >= ci
    tril = causal.astype(F32)
    triu = (ri <= ci).astype(F32)
    bcol = jnp.dot(tril, _log_sigmoid(gcol), precision=hp, preferred_element_type=F32)
    brow = jnp.dot(_log_sigmoid(grow), triu, precision=hp, preferred_element_type=F32)

    for h in range(heads):
        sl = slice(h * dh, (h + 1) * dh)
        b_c = bcol[:, heads + h:heads + h + 1]
        i_c = gcol[:, h:h + 1]
        b_r = brow[heads + h:heads + h + 1, :]
        i_r = grow[h:h + 1, :]
        m_st = m_scr[h:h + 1, 0:1]
        c_st = c_scr[h]
        n_st = n_scr[h:h + 1, :]

        d_mat = jnp.where(causal, b_c - b_r + i_r, NEG)
        inter = b_c + m_st
        m_t = jnp.maximum(inter, jnp.max(d_mat, axis=-1, keepdims=True))
        w_intra = jnp.exp(d_mat - m_t)
        w_inter = jnp.exp(inter - m_t)

        q_f = qk[:, sl]
        k_f = qk[:, w + h * dh:w + (h + 1) * dh] * (dh ** -0.5)
        q_b = q_f.astype(BF16)
        k_b = k_f.astype(BF16)
        v_b = v_ref[:, sl]

        s = _dot_nt(q_b, k_b) * w_intra
        num = _dot(s.astype(BF16), v_b) + w_inter * _dot(q_b, c_st.astype(BF16))
        den = jnp.sum(s, axis=-1, keepdims=True) + w_inter * jnp.sum(q_f * n_st, axis=-1, keepdims=True)
        hh = num / jnp.maximum(jnp.abs(den), jnp.exp(-m_t))
        hn = _rms(hh, ng_ref[:, sl])
        y_ref[:, sl] = (hn * jax.nn.sigmoid(og_ref[:, sl].astype(F32))).astype(y_ref.dtype)

        b_last = b_c[blk - 1:blk, :]
        dec = b_last - b_c + i_c
        m_new = jnp.maximum(b_last + m_st, jnp.max(dec, axis=0, keepdims=True))
        w_k = jnp.exp(dec - m_new)
        w_c = jnp.exp(b_last + m_st - m_new)
        kw = k_f * w_k
        c_scr[h] = w_c * c_st + _dot(kw.T.astype(BF16), v_b)
        n_scr[h:h + 1, :] = w_c * n_st + jnp.sum(kw, axis=0, keepdims=True)
        m_scr[h:h + 1, :] = jnp.broadcast_to(m_new, (1, m_scr.shape[1]))


def _mlstm(proj, gates_row, conv_w, gb_col, gb_row, norm_g, *, batch, seq, blk):
    t = proj.shape[0]
    nblk = seq // blk
    w = MLSTM_W
    row = lambda b, c: b * nblk + c
    return pl.pallas_call(
        functools.partial(_mlstm_kernel, blk=blk),
        grid=(batch, nblk),
        in_specs=[pl.BlockSpec((blk, 2 * w), lambda b, c: (row(b, c), OFF_MQ // (2 * w))),
                  pl.BlockSpec((blk, w), lambda b, c: (row(b, c), OFF_MV // w)),
                  pl.BlockSpec((blk, w), lambda b, c: (row(b, c), OFF_MO // w)),
                  pl.BlockSpec((blk, IF_PAD), lambda b, c: (row(b, c), OFF_IF // IF_PAD)),
                  pl.BlockSpec((1, 8, blk), lambda b, c: (b, 0, c)),
                  pl.BlockSpec((CONV_WIDTH, 2 * w), lambda b, c: (0, 0)),
                  pl.BlockSpec((1, IF_PAD), lambda b, c: (0, 0)),
                  pl.BlockSpec((8, 1), lambda b, c: (0, 0)),
                  pl.BlockSpec((1, w), lambda b, c: (0, 0))],
        out_specs=pl.BlockSpec((blk, w), lambda b, c: (row(b, c), 0)),
        out_shape=jax.ShapeDtypeStruct((t, w), BF16),
        scratch_shapes=[pltpu.VMEM((blk + 8, 2 * w), F32),
                        pltpu.VMEM((MLSTM_HEADS, MLSTM_DH, MLSTM_DH), F32),
                        pltpu.VMEM((8, MLSTM_DH), F32),
                        pltpu.VMEM((8, 128), F32)],
        compiler_params=_cparams("parallel", "arbitrary"),
        name="mlstm",
    )(proj, proj, proj, proj, gates_row, conv_w, gb_col, gb_row, norm_g)


def _attnproj_kernel(h_ref, w_ref, seg_ref, gq_ref, gk_ref, o_ref, r_scr):
    j = pl.program_id(1)
    gw, half = ATTN_GW, ATTN_SLAB // 2
    sub_rows = 512

    def head_norm(x, gain):
        sq = x * x
        hi = sq.astype(BF16)
        lo = (sq - hi.astype(F32)).astype(BF16)
        ss = _dot(hi, seg_ref[...]) + _dot(lo, seg_ref[...])
        return x * lax.rsqrt(ss * (1.0 / ATTN_DH) + EPS) * gain

    low = lax.broadcasted_iota(jnp.int32, (1, ATTN_SLAB), 1) < half
    for s in range(ATTN_TILE // sub_rows):
        rows = slice(s * sub_rows, (s + 1) * sub_rows)
        res = _dot(h_ref[rows, :], w_ref[...])
        q = head_norm(res[:, :gw], gq_ref[...]) * (ATTN_DH ** -0.5)
        k = head_norm(res[:, gw:2 * gw], gk_ref[...])
        slabs = []
        for pair in range(gw // ATTN_SLAB):
            qp = q[:, pair * ATTN_SLAB:(pair + 1) * ATTN_SLAB]
            slabs += [jnp.where(low, qp, 0.0), jnp.where(low, 0.0, qp)]
        slabs += [k[:, c * 128:(c + 1) * 128] for c in range(gw // 128)]
        slabs += [res[:, 2 * gw + c * 128:2 * gw + (c + 1) * 128] for c in range(gw // 128)]
        for c, slab in enumerate(slabs):
            r_scr[c, rows, :] = slab

    for g, (_, dil) in enumerate(ATTN_PATTERNS):
        @pl.when(j == g)
        def _(dil=dil):
            seg = ATTN_TILE // dil
            for r in range(dil):
                for c in range(r_scr.shape[0]):
                    src = r_scr[c, pl.ds(r, seg, stride=dil), :] if dil > 1 else r_scr[c]
                    o_ref[r * seg:(r + 1) * seg, c * 128:(c + 1) * 128] = src.astype(o_ref.dtype)


def _attnproj(h, w, seg_ones, gq, gk):
    t, d = h.shape
    ng = len(ATTN_PATTERNS)
    wcols = 3 * ATTN_GW
    const2 = lambda i, j: (0, 0)
    return pl.pallas_call(
        _attnproj_kernel,
        grid=(t // ATTN_TILE, ng),
        in_specs=[pl.BlockSpec((ATTN_TILE, d), lambda i, j: (i, 0)),
                  pl.BlockSpec((d, wcols), lambda i, j: (0, j)),
                  pl.BlockSpec((ATTN_GW, ATTN_GW), const2),
                  pl.BlockSpec((1, ATTN_GW), const2), pl.BlockSpec((1, ATTN_GW), const2)],
        out_specs=pl.BlockSpec((ATTN_TILE, ATTN_COLS), lambda i, j: (i, j)),
        out_shape=jax.ShapeDtypeStruct((t, ng * ATTN_COLS), BF16),
        scratch_shapes=[pltpu.VMEM((ATTN_COLS // 128, ATTN_TILE, 128), F32)],
        compiler_params=_cparams("parallel", "arbitrary"),
        name="attnproj",
    )(h, w, seg_ones, gq, gk)


def _dattn_kernel(q_ref, kc_ref, kp_ref, vc_ref, vp_ref, bias_ref, o_ref, lse_ref,
                  kx_scr, vx_scr, o_scr, l_scr, *, dil):
    blk = ATTN_BLOCK
    per = ATTN_SUB // dil
    first_tile = pl.program_id(1) == 0
    for r in range(dil):
        base = r * (per + 1) * blk
        last = slice((r * per + per - 1) * blk, (r * per + per) * blk)
        mine = slice(r * per * blk, (r + 1) * per * blk)
        kx_scr[base:base + blk, :] = kp_ref[last, :]
        vx_scr[base:base + blk, :] = vp_ref[last, :]
        kx_scr[base + blk:base + (per + 1) * blk, :] = kc_ref[mine, :]
        vx_scr[base + blk:base + (per + 1) * blk, :] = vc_ref[mine, :]

    low = lax.broadcasted_iota(jnp.int32, (1, ATTN_SLAB), 1) < ATTN_SLAB // 2
    no_prev = lax.broadcasted_iota(jnp.int32, (1, 2 * blk), 1) < blk
    for r in range(dil):
        for sub in range(per):
            u = r * per + sub
            win = slice((r * (per + 1) + sub) * blk, (r * (per + 1) + sub + 2) * blk)
            o_slabs, l_slabs = [], []
            for pair in range(ATTN_GW // ATTN_SLAB):
                cols = slice(pair * ATTN_SLAB, (pair + 1) * ATTN_SLAB)
                kx, vx = kx_scr[win, cols], vx_scr[win, cols]
                o_pair, l_pair = [], []
                for h in (2 * pair, 2 * pair + 1):
                    logits = _dot_nt(q_ref[u * blk:(u + 1) * blk, h * ATTN_SLAB:(h + 1) * ATTN_SLAB], kx)
                    logits = logits + bias_ref[h]
                    if sub == 0:
                        logits = jnp.where(first_tile & no_prev, NEG, logits)
                    m = jnp.max(logits, axis=-1, keepdims=True)
                    p = jnp.exp(logits - m)
                    l = jnp.sum(p, axis=-1, keepdims=True)
                    o_pair.append(_dot(p.astype(BF16), vx) / l)
                    l_pair.append(m + jnp.log(l))
                o_slabs.append(jnp.where(low, o_pair[0], o_pair[1]))
                l_slabs.append(jnp.where(low, l_pair[0], l_pair[1]))
            dst = pl.ds(sub * blk * dil + r, blk, stride=dil) if dil > 1 else slice(u * blk, (u + 1) * blk)
            for c in range(ATTN_GW // ATTN_SLAB):
                o_scr[c, dst, :] = o_slabs[c]
                l_scr[c, dst, :] = l_slabs[c]
    for c in range(ATTN_GW // ATTN_SLAB):
        o_ref[:, c * ATTN_SLAB:(c + 1) * ATTN_SLAB] = o_scr[c].astype(o_ref.dtype)
        lse_ref[:, c * ATTN_SLAB:(c + 1) * ATTN_SLAB] = l_scr[c]


def _dattn(aproj, bias, *, seq, group, dilation):
    t = aproj.shape[0]
    tiles = seq // ATTN_TILE
    qw = HEADS_PER_GROUP * ATTN_SLAB
    cq = group * ATTN_COLS // qw
    ck, cv = (group * ATTN_COLS + qw) // ATTN_GW, (group * ATTN_COLS + qw) // ATTN_GW + 1
    blk = (ATTN_TILE, ATTN_GW)
    cur = lambda c: (lambda b, j: (b * tiles + j, c))
    prev = lambda c: (lambda b, j: (b * tiles + jnp.maximum(j - 1, 0), c))
    xrows = ATTN_TILE + dilation * ATTN_BLOCK
    return pl.pallas_call(
        functools.partial(_dattn_kernel, dil=dilation),
        grid=(t // seq, tiles),
        in_specs=[pl.BlockSpec((ATTN_TILE, qw), cur(cq)),
                  pl.BlockSpec(blk, cur(ck)), pl.BlockSpec(blk, prev(ck)),
                  pl.BlockSpec(blk, cur(cv)), pl.BlockSpec(blk, prev(cv)),
                  pl.BlockSpec((HEADS_PER_GROUP, ATTN_BLOCK, 2 * ATTN_BLOCK), lambda b, j: (0, 0, 0))],
        out_specs=[pl.BlockSpec(blk, cur(0)), pl.BlockSpec(blk, cur(0))],
        out_shape=[jax.ShapeDtypeStruct((t, ATTN_GW), BF16), jax.ShapeDtypeStruct((t, ATTN_GW), F32)],
        scratch_shapes=[pltpu.VMEM((xrows, ATTN_GW), BF16), pltpu.VMEM((xrows, ATTN_GW), BF16),
                        pltpu.VMEM((ATTN_GW // ATTN_SLAB, ATTN_TILE, ATTN_SLAB), F32),
                        pltpu.VMEM((ATTN_GW // ATTN_SLAB, ATTN_TILE, ATTN_SLAB), F32)],
        compiler_params=_cparams("parallel", "arbitrary"),
        name=f"dattn{group}",
    )(aproj, aproj, aproj, aproj, aproj, bias)


def _rel_bucket(n):
    max_exact = REL_BUCKETS // 2
    nf = jnp.maximum(n, 1).astype(F32)
    log_b = max_exact + (jnp.log(nf / max_exact) / math.log(REL_MAX_DIST / max_exact)
                         * (REL_BUCKETS - max_exact)).astype(jnp.int32)
    return jnp.where(n < max_exact, n, jnp.minimum(log_b, REL_BUCKETS - 1))


def _attn_bias(rel_bias, group):
    window, dilation = ATTN_PATTERNS[group]
    steps = window // dilation
    hp = lax.Precision.HIGHEST
    hs = slice(group * HEADS_PER_GROUP, (group + 1) * HEADS_PER_GROUP)
    bucket = _rel_bucket(jnp.arange(steps + 1) * dilation)
    bias_steps = jnp.dot(jax.nn.one_hot(bucket, REL_BUCKETS, dtype=F32), rel_bias[:, hs].astype(F32),
                         precision=hp)
    qi = jnp.arange(ATTN_BLOCK)[:, None]
    ki = jnp.arange(2 * ATTN_BLOCK)[None, :]
    dist = ATTN_BLOCK + qi - ki
    ok = (dist >= 0) & (dist <= steps)
    sel = jax.nn.one_hot(jnp.clip(dist, 0, steps).reshape(-1), steps + 1, dtype=F32)
    bias = jnp.dot(sel, bias_steps, precision=hp).T.reshape(HEADS_PER_GROUP, ATTN_BLOCK, 2 * ATTN_BLOCK)
    return jnp.where(ok[None], bias, NEG)


def _merge_kernel(ya_ref, yb0_ref, yb1_ref, yb2_ref, l0_ref, l1_ref, l2_ref, gu_ref, gv_ref, gate_ref,
                  x_ref, wa_ref, wb_ref, wc_ref, wo_ref, ws_ref, bs_ref, gg_ref, o_ref, yc_scr, *, tm):
    d = x_ref.shape[1]
    l0, l1, l2 = l0_ref[...], l1_ref[...], l2_ref[...]
    mx = jnp.maximum(jnp.maximum(l0, l1), l2)
    e0, e1, e2 = jnp.exp(l0 - mx), jnp.exp(l1 - mx), jnp.exp(l2 - mx)
    inv = 1.0 / (e0 + e1 + e2)
    yb = jnp.concatenate([(yb0_ref[...].astype(F32) * (e0 * inv)).astype(BF16),
                          (yb1_ref[...].astype(F32) * (e1 * inv)).astype(BF16),
                          (yb2_ref[...].astype(F32) * (e2 * inv)).astype(BF16)], axis=-1)

    for j in range(tm // GMLP_CHUNK):
        rows = slice(j * GMLP_CHUNK, (j + 1) * GMLP_CHUNK)
        for g in range(GMLP_GROUPS):
            cols = slice(g * GMLP_GC, (g + 1) * GMLP_GC)
            u = jax.nn.gelu(gu_ref[rows, cols].astype(F32))
            v = _rms(jax.nn.gelu(gv_ref[rows, cols].astype(F32)), gg_ref[:, cols])
            mixed = _dot(ws_ref[g], v.astype(BF16)) + bs_ref[g]
            yc_scr[rows, cols] = (u * mixed).astype(BF16)

    merged = jax.nn.sigmoid(gate_ref[:, 0:d].astype(F32)) * _dot(ya_ref[...], wa_ref[...])
    merged = merged + jax.nn.sigmoid(gate_ref[:, d:2 * d].astype(F32)) * _dot(yb, wb_ref[...])
    merged = merged + jax.nn.sigmoid(gate_ref[:, 2 * d:3 * d].astype(F32)) * _dot(yc_scr[...], wc_ref[...])
    o_ref[...] = x_ref[...] + _dot(merged.astype(BF16), wo_ref[...])


def _merge(ya, ybs, lses, proj, x2d, wa, wb, wc, wo, ws, bsb, gg, *, tm):
    t, d = x2d.shape
    row = lambda c: (lambda i: (i, c))
    full2 = lambda i: (0, 0)
    full3 = lambda i: (0, 0, 0)
    gspec = pl.BlockSpec((tm, ATTN_GW), row(0))
    return pl.pallas_call(
        functools.partial(_merge_kernel, tm=tm),
        grid=(t // tm,),
        in_specs=[pl.BlockSpec((tm, MLSTM_W), row(0)),
                  gspec, gspec, gspec, gspec, gspec, gspec,
                  pl.BlockSpec((tm, GMLP_W), row(OFF_GU // GMLP_W)),
                  pl.BlockSpec((tm, GMLP_W), row(OFF_GV // GMLP_W)),
                  pl.BlockSpec((tm, N_BRANCH * d), row(OFF_GATE // (N_BRANCH * d))),
                  pl.BlockSpec((tm, d), row(0)),
                  pl.BlockSpec(wa.shape, full2), pl.BlockSpec(wb.shape, full2),
                  pl.BlockSpec(wc.shape, full2), pl.BlockSpec(wo.shape, full2),
                  pl.BlockSpec(ws.shape, full3), pl.BlockSpec(bsb.shape, full3),
                  pl.BlockSpec(gg.shape, full2)],
        out_specs=pl.BlockSpec((tm, d), row(0)),
        out_shape=jax.ShapeDtypeStruct((t, d), F32),
        scratch_shapes=[pltpu.VMEM((tm, GMLP_W), BF16)],
        compiler_params=_cparams("parallel"),
        name="merge",
    )(ya, *ybs, *lses, proj, proj, proj, x2d, wa, wb, wc, wo, ws, bsb, gg)


def _memkv_kernel(mem_ref, g_ref, w_ref, gk_ref, k_ref, v_ref):
    dh, w = XATTN_DH, XATTN_W
    kv = _dot(_rms(mem_ref[0], g_ref[...]).astype(BF16), w_ref[...])
    for h in range(XATTN_HEADS):
        sl = slice(h * dh, (h + 1) * dh)
        k_ref[0, :, sl] = _rms(kv[:, sl], gk_ref[...]).astype(k_ref.dtype)
    v_ref[0] = kv[:, w:].astype(v_ref.dtype)


def _memkv(mem, gain, w_kv, gk):
    b, m, d = mem.shape
    full2 = lambda i: (0, 0)
    return pl.pallas_call(
        _memkv_kernel,
        grid=(b,),
        in_specs=[pl.BlockSpec((1, m, d), lambda i: (i, 0, 0)),
                  pl.BlockSpec((1, d), full2),
                  pl.BlockSpec(w_kv.shape, full2),
                  pl.BlockSpec((1, XATTN_DH), full2)],
        out_specs=[pl.BlockSpec((1, m, XATTN_W), lambda i: (i, 0, 0)),
                   pl.BlockSpec((1, m, XATTN_W), lambda i: (i, 0, 0))],
        out_shape=[jax.ShapeDtypeStruct((b, m, XATTN_W), BF16),
                   jax.ShapeDtypeStruct((b, m, XATTN_W), BF16)],
        compiler_params=_cparams("parallel"),
        name="memkv",
    )(mem, gain, w_kv, gk)


def _route(logits):
    tm = logits.shape[1]
    e = jnp.exp(logits - jnp.max(logits, axis=0, keepdims=True))
    probs = e / jnp.sum(e, axis=0, keepdims=True)
    rowi = lax.broadcasted_iota(jnp.int32, (8, tm), 0)
    real = rowi < EXPERTS_PER_GROUP
    tops = []
    for g in range(N_EXPERT_GROUPS):
        pg = jnp.where(real, probs[8 * g:8 * g + 8, :], -0.5)
        m1 = jnp.max(pg, axis=0, keepdims=True)
        i1 = jnp.min(jnp.where(pg == m1, rowi, 8), axis=0, keepdims=True)
        pg2 = jnp.where(rowi == i1, -1.0, pg)
        m2 = jnp.max(pg2, axis=0, keepdims=True)
        i2 = jnp.min(jnp.where(pg2 == m2, rowi, 8), axis=0, keepdims=True)
        tops.append((m1, i1, m2, i2))
    best = jnp.zeros((1, tm), jnp.int32)
    best_score = tops[0][0] + tops[0][2]
    for g in range(1, N_EXPERT_GROUPS):
        score = tops[g][0] + tops[g][2]
        better = score > best_score
        best = jnp.where(better, g, best)
        best_score = jnp.where(better, score, best_score)
    m1, i1, m2, i2 = tops[0]
    for g in range(1, N_EXPERT_GROUPS):
        m1, i1, m2, i2 = (jnp.where(best == g, new, old) for new, old in zip(tops[g], (m1, i1, m2, i2)))
    tot = m1 + m2
    base = best * EXPERTS_PER_GROUP
    return base + i1, base + i2, m1 / tot, m2 / tot


def _pack_bf16_pairs(x):
    n = x.shape[1] // 2
    hi = lax.bitcast_convert_type(x[:, :n].astype(BF16).astype(F32), jnp.uint32)
    lo = lax.bitcast_convert_type(x[:, n:].astype(BF16).astype(F32), jnp.uint32)
    return hi | (lo >> 16)


def _unpack_bf16_pairs(p):
    hi = lax.bitcast_convert_type(p & jnp.uint32(0xFFFF0000), F32)
    lo = lax.bitcast_convert_type(p << 16, F32)
    return hi, lo


def _store_row_chunks(ref, packed):
    m = packed.shape[0]
    for j in range(ROW_CHUNKS):
        ref[pl.ds(j, m, stride=ROW_CHUNKS), :] = packed[:, j * 128:(j + 1) * 128]


def _load_row_chunks(ref, m):
    return jnp.concatenate([ref[pl.ds(j, m, stride=ROW_CHUNKS), :] for j in range(ROW_CHUNKS)], axis=-1)


def _xattn_kernel(x_ref, k_ref, v_ref, gx_ref, wq_ref, gq_ref, wo_ref, gf_ref, rw_ref, rb_ref,
                  xo_ref, hf_ref, eidx_ref, wts_ref):
    dh = XATTN_DH
    x = x_ref[...]
    q = _dot(_rms(x, gx_ref[...]).astype(BF16), wq_ref[...])
    outs = []
    for h in range(XATTN_HEADS):
        sl = slice(h * dh, (h + 1) * dh)
        q_h = (_rms(q[:, sl], gq_ref[...]) * (dh ** -0.5)).astype(BF16)
        logits = _dot_nt(q_h, k_ref[0, :, sl])
        p = jnp.exp(logits - jnp.max(logits, axis=-1, keepdims=True))
        o = _dot(p.astype(BF16), v_ref[0, :, sl]) / jnp.sum(p, axis=-1, keepdims=True)
        outs.append(o.astype(BF16))
    xn = x + _dot(jnp.concatenate(outs, axis=-1), wo_ref[...])
    xo_ref[...] = xn
    hf = _rms(xn, gf_ref[...])
    _store_row_chunks(hf_ref, _pack_bf16_pairs(hf))
    logits_t = lax.dot_general(rw_ref[...], hf, (((1,), (1,)), ((), ())),
                               precision=lax.Precision.HIGHEST, preferred_element_type=F32) + rb_ref[...]
    e1, e2, w1, w2 = _route(logits_t)
    tm = x.shape[0]
    eidx_ref[...] = jnp.concatenate([e1, e2, jnp.zeros((6, tm), jnp.int32)], axis=0)
    wts_ref[...] = jnp.concatenate([w1, w2, jnp.zeros((6, tm), F32)], axis=0)


def _xattn(x2d, k, v, gx, wq, gq, wo, gf, rw_t, rb, *, seq, tm):
    t, d = x2d.shape
    per_b = seq // tm
    full2 = lambda i: (0, 0)
    kv_spec = pl.BlockSpec((1,) + k.shape[1:], lambda i: (i // per_b, 0, 0))
    return pl.pallas_call(
        _xattn_kernel,
        grid=(t // tm,),
        in_specs=[pl.BlockSpec((tm, d), lambda i: (i, 0)), kv_spec, kv_spec,
                  pl.BlockSpec((1, d), full2), pl.BlockSpec(wq.shape, full2),
                  pl.BlockSpec((1, XATTN_DH), full2), pl.BlockSpec(wo.shape, full2),
                  pl.BlockSpec((1, d), full2), pl.BlockSpec(rw_t.shape, full2),
                  pl.BlockSpec(rb.shape, full2)],
        out_specs=[pl.BlockSpec((tm, d), lambda i: (i, 0)),
                   pl.BlockSpec((ROW_CHUNKS * tm, 128), lambda i: (i, 0)),
                   pl.BlockSpec((8, tm), lambda i: (0, i)),
                   pl.BlockSpec((8, tm), lambda i: (0, i))],
        out_shape=[jax.ShapeDtypeStruct((t, d), F32),
                   jax.ShapeDtypeStruct((ROW_CHUNKS * t, 128), jnp.uint32),
                   jax.ShapeDtypeStruct((8, t), jnp.int32),
                   jax.ShapeDtypeStruct((8, t), F32)],
        compiler_params=_cparams("parallel"),
        name="xattn_router",
    )(x2d, k, v, gx, wq, gq, wo, gf, rw_t, rb)


def _moe_plan_kernel(eidx_ref, d_ref, te_ref, na_ref, cnt_scr, carry_scr, *, tb, tm):
    ne = N_EXPERTS
    hp = lax.Precision.HIGHEST
    phase, j = pl.program_id(0), pl.program_id(1)
    rows = lax.broadcasted_iota(jnp.int32, (ne, tb), 0)
    oh1 = rows == eidx_ref[0:1, :]
    oh2 = rows == eidx_ref[1:2, :]
    a = oh1.astype(F32) + oh2.astype(F32)
    blk_cnt = jnp.broadcast_to(jnp.sum(a, axis=1, keepdims=True), cnt_scr.shape)

    @pl.when((phase == 0) & (j == 0))
    def _():
        cnt_scr[...] = jnp.zeros_like(cnt_scr)

    @pl.when(phase == 0)
    def _():
        cnt_scr[...] += blk_cnt

    @pl.when((phase == 1) & (j == 0))
    def _():
        padded = jnp.ceil(cnt_scr[...] * (1.0 / tm)) * tm
        er = lax.broadcasted_iota(jnp.int32, (ne, ne), 0)
        ec = lax.broadcasted_iota(jnp.int32, (ne, ne), 1)
        off = jnp.dot((ec < er).astype(F32), padded, precision=hp, preferred_element_type=F32)
        carry_scr[...] = off
        seg_end = (off + padded)[:, 0:1]
        tile_start = lax.broadcasted_iota(jnp.int32, (ne, te_ref.shape[1]), 1).astype(F32) * tm
        te = jnp.sum((seg_end <= tile_start).astype(F32), axis=0, keepdims=True)
        te_ref[...] = jnp.broadcast_to(jnp.minimum(te, ne - 1.0), te_ref.shape).astype(jnp.int32)
        total = jnp.sum(padded[:, 0:1], axis=0, keepdims=True)
        na_ref[...] = jnp.broadcast_to(total * (1.0 / tm), na_ref.shape).astype(jnp.int32)

    @pl.when(phase == 1)
    def _():
        before = (lax.broadcasted_iota(jnp.int32, (tb, tb), 0)
                  < lax.broadcasted_iota(jnp.int32, (tb, tb), 1)).astype(BF16)
        rank = carry_scr[:, 0:1] + _dot(a.astype(BF16), before)
        d1 = jnp.sum(jnp.where(oh1, rank, 0.0), axis=0, keepdims=True)
        d2 = jnp.sum(jnp.where(oh2, rank, 0.0), axis=0, keepdims=True)
        d_ref[...] = jnp.concatenate([d1, d2, jnp.zeros((6, tb), F32)], axis=0).astype(jnp.int32)
        carry_scr[...] += blk_cnt


def _moe_plan(eidx, *, tm, n_tiles, tb=512):
    t = eidx.shape[1]
    ntp = -(-n_tiles // 128) * 128
    return pl.pallas_call(
        functools.partial(_moe_plan_kernel, tb=tb, tm=tm),
        grid=(2, t // tb),
        in_specs=[pl.BlockSpec((8, tb), lambda p, j: (0, j))],
        out_specs=[pl.BlockSpec((8, tb), lambda p, j: (0, j * p)),
                   pl.BlockSpec((8, ntp), lambda p, j: (0, 0)),
                   pl.BlockSpec((8, 128), lambda p, j: (0, 0))],
        out_shape=[jax.ShapeDtypeStruct((8, t), jnp.int32),
                   jax.ShapeDtypeStruct((8, ntp), jnp.int32),
                   jax.ShapeDtypeStruct((8, 128), jnp.int32)],
        scratch_shapes=[pltpu.VMEM((N_EXPERTS, 128), F32), pltpu.VMEM((N_EXPERTS, 128), F32)],
        compiler_params=_cparams("arbitrary", "arbitrary"),
        name="moe_plan",
    )(eidx)


def _sc_mesh():
    return plsc.VectorSubcoreMesh(core_axis_name="c", subcore_axis_name="s",
                                  num_cores=SC_CORES, num_subcores=SC_SUBCORES)


def _sc_dispatch(rows, i1, i2, n_out):
    n = rows.shape[0]

    @functools.partial(pl.kernel, out_type=jax.ShapeDtypeStruct((n_out, 128), rows.dtype), mesh=_sc_mesh(),
                       name="moe_dispatch")
    def k(x_hbm, i1_hbm, i2_hbm, o_hbm):
        def body(x_vmem, i1_vmem, i2_vmem):
            pltpu.sync_copy(x_vmem, o_hbm.at[i1_vmem.at[0]])
            pltpu.sync_copy(x_vmem, o_hbm.at[i2_vmem.at[0]])

        pltpu.emit_pipeline(
            body, grid=(n // SC_WINDOW,),
            in_specs=[pl.BlockSpec((SC_WINDOW, 128), lambda i: (i, 0)),
                      pl.BlockSpec((1, SC_WINDOW), lambda i: (0, i)),
                      pl.BlockSpec((1, SC_WINDOW), lambda i: (0, i))],
            out_specs=[],
            core_axis_name=("c", "s"), dimension_semantics=(pltpu.PARALLEL,),
        )(x_hbm, i1_hbm, i2_hbm)

    return k(rows, i1, i2)


def _sc_collect(table, i1, i2):
    n = i1.shape[1]
    out = jax.ShapeDtypeStruct((n, 128), table.dtype)

    @functools.partial(pl.kernel, out_type=(out, out), mesh=_sc_mesh(), name="moe_collect")
    def k(t_hbm, i1_hbm, i2_hbm, o1_hbm, o2_hbm):
        def body(i1_vmem, i2_vmem, o1_vmem, o2_vmem):
            pltpu.sync_copy(t_hbm.at[i1_vmem.at[0]], o1_vmem)
            pltpu.sync_copy(t_hbm.at[i2_vmem.at[0]], o2_vmem)

        pltpu.emit_pipeline(
            body, grid=(n // SC_WINDOW,),
            in_specs=[pl.BlockSpec((1, SC_WINDOW), lambda i: (0, i)),
                      pl.BlockSpec((1, SC_WINDOW), lambda i: (0, i))],
            out_specs=[pl.BlockSpec((SC_WINDOW, 128), lambda i: (i, 0)),
                       pl.BlockSpec((SC_WINDOW, 128), lambda i: (i, 0))],
            core_axis_name=("c", "s"), dimension_semantics=(pltpu.PARALLEL,),
        )(i1_hbm, i2_hbm, o1_hbm, o2_hbm)

    return k(table, i1, i2)


def _experts_kernel(te_ref, na_ref, xs_ref, wg_ref, wu_ref, wd_ref, y_ref, *, tm):
    del te_ref

    @pl.when(pl.program_id(0) < na_ref[0])
    def _():
        hi, lo = _unpack_bf16_pairs(_load_row_chunks(xs_ref, tm))
        h = jnp.concatenate([hi, lo], axis=-1).astype(BF16)
        up = _dot(h, wg_ref[0])
        act = up * jax.nn.sigmoid(up) * _dot(h, wu_ref[0])
        _store_row_chunks(y_ref, _pack_bf16_pairs(_dot(act.astype(BF16), wd_ref[0])))


def _experts(tile_expert, n_active, xs, wg, wu, wd, *, tm):
    n_tiles = tile_expert.shape[0]
    _, d, dff = wg.shape
    rows = lambda i, te, na: (jnp.minimum(i, na[0] - 1), 0)
    return pl.pallas_call(
        functools.partial(_experts_kernel, tm=tm),
        grid_spec=pltpu.PrefetchScalarGridSpec(
            num_scalar_prefetch=2,
            grid=(n_tiles,),
            in_specs=[pl.BlockSpec((ROW_CHUNKS * tm, 128), rows),
                      pl.BlockSpec((1, d, dff), lambda i, te, na: (te[i], 0, 0)),
                      pl.BlockSpec((1, d, dff), lambda i, te, na: (te[i], 0, 0)),
                      pl.BlockSpec((1, dff, d), lambda i, te, na: (te[i], 0, 0))],
            out_specs=pl.BlockSpec((ROW_CHUNKS * tm, 128), rows)),
        out_shape=jax.ShapeDtypeStruct(xs.shape, xs.dtype),
        compiler_params=_cparams("arbitrary"),
        name="moe_experts",
    )(tile_expert, n_active, xs, wg, wu, wd)


def _moe_combine_kernel(x_ref, y1_ref, y2_ref, w_ref, o_ref, *, tm):
    half = x_ref.shape[1] // 2
    hi1, lo1 = _unpack_bf16_pairs(_load_row_chunks(y1_ref, tm))
    hi2, lo2 = _unpack_bf16_pairs(_load_row_chunks(y2_ref, tm))
    w1, w2 = w_ref[:, 0:1], w_ref[:, 1:2]
    o_ref[:, :half] = x_ref[:, :half] + w1 * hi1 + w2 * hi2
    o_ref[:, half:] = x_ref[:, half:] + w1 * lo1 + w2 * lo2


def _moe_combine(x2d, y1, y2, wcol, *, tm):
    t, d = x2d.shape
    chunk_spec = pl.BlockSpec((ROW_CHUNKS * tm, 128), lambda i: (i, 0))
    return pl.pallas_call(
        functools.partial(_moe_combine_kernel, tm=tm),
        grid=(t // tm,),
        in_specs=[pl.BlockSpec((tm, d), lambda i: (i, 0)), chunk_spec, chunk_spec,
                  pl.BlockSpec((tm, wcol.shape[1]), lambda i: (i, 0))],
        out_specs=pl.BlockSpec((tm, d), lambda i: (i, 0)),
        out_shape=jax.ShapeDtypeStruct((t, d), F32),
        compiler_params=_cparams("parallel"),
        name="moe_combine",
    )(x2d, y1, y2, wcol)


def _moe(x2d, hf_rows, eidx, wts, wg, wu, wd):
    t = x2d.shape[0]
    tm = MOE_TM
    n_tiles = 2 * t // tm + N_EXPERTS
    dest, te, na = _moe_plan(eidx, tm=tm, n_tiles=n_tiles)
    chunk = jnp.arange(ROW_CHUNKS, dtype=jnp.int32)[None, :]
    i1 = (ROW_CHUNKS * dest[0][:, None] + chunk).reshape(1, ROW_CHUNKS * t)
    i2 = (ROW_CHUNKS * dest[1][:, None] + chunk).reshape(1, ROW_CHUNKS * t)
    xs = _sc_dispatch(hf_rows, i1, i2, ROW_CHUNKS * n_tiles * tm)
    ys = _experts(te[0, :n_tiles], na[0, :1], xs, wg, wu, wd, tm=tm)
    y1, y2 = _sc_collect(ys, i1, i2)
    return _moe_combine(x2d, y1, y2, wts[:2].T, tm=512)


def _layout_w_in(w):
    sizes = (MLSTM_W, MLSTM_W, MLSTM_W, MLSTM_W, MLSTM_HEADS, MLSTM_HEADS,
             ATTN_W, ATTN_W, ATTN_W, GMLP_W, GMLP_W, N_BRANCH * w.shape[0])
    pts = np.cumsum(sizes)[:-1]
    mq, mk, mv, mo, mi, mf, aq, ak, av, gu, gv, gate = jnp.split(w, pts, axis=-1)
    pad = jnp.zeros((w.shape[0], IF_PAD - 2 * MLSTM_HEADS), w.dtype)
    main = jnp.concatenate([mq, mk, mv, mo, gu, gv, gate, mi, mf, pad], axis=-1).astype(BF16)
    attn = jnp.concatenate([a[:, g * ATTN_GW:(g + 1) * ATTN_GW] for g in range(len(ATTN_PATTERNS))
                            for a in (aq, ak, av)], axis=-1).astype(BF16)
    return main, attn


def kernel(x, mem, norm_mix, w_in, mlstm_conv, mlstm_gate_b, mlstm_norm, attn_qk_norm, gmlp_norm, gmlp_ws,
           gmlp_bs, w_branch_a, w_branch_b, w_branch_c, w_out, rel_bias, norm_xattn, norm_mem, w_xq, w_xkv,
           xattn_qk_norm, w_xo, norm_ffn, router_w, router_b, w_expert_gate, w_expert_up, w_expert_down):
    b, s, d = x.shape
    t = b * s
    depth = w_in.shape[0]
    x2d = x.reshape(t, d)

    biases = [_attn_bias(rel_bias, g) for g in range(len(ATTN_PATTERNS))]
    rw_t = jnp.zeros((N_EXPERT_GROUPS, 8, d), F32).at[:, :EXPERTS_PER_GROUP].set(
        router_w.T.reshape(N_EXPERT_GROUPS, EXPERTS_PER_GROUP, d)).reshape(ROUTER_ROWS, d)
    rb = jnp.full((N_EXPERT_GROUPS, 8), NEG, F32).at[:, :EXPERTS_PER_GROUP].set(
        router_b.astype(F32).reshape(N_EXPERT_GROUPS, EXPERTS_PER_GROUP)).reshape(ROUTER_ROWS, 1)
    tril = jnp.tril(jnp.ones((GMLP_CHUNK, GMLP_CHUNK), bool))
    head_of = jnp.arange(ATTN_GW) // ATTN_DH
    seg_ones = (head_of[:, None] == head_of[None, :]).astype(BF16)

    for l in range(depth):
        w_main, w_attn = _layout_w_in(w_in[l])
        proj, h_mix = _inproj(x2d, norm_mix[l][None], w_main, tm=1024, tn=1280)
        gq = jnp.tile(attn_qk_norm[l, 0], HEADS_PER_GROUP)[None]
        gk = jnp.tile(attn_qk_norm[l, 1], HEADS_PER_GROUP)[None]
        aproj = _attnproj(h_mix, w_attn, seg_ones, gq, gk)

        gates_row = proj[:, OFF_IF:OFF_IF + 8].astype(F32).reshape(b, s, 8).transpose(0, 2, 1)
        gb_col = jnp.zeros((1, IF_PAD), F32).at[0, :8].set(mlstm_gate_b[l])
        ya = _mlstm(proj, gates_row, mlstm_conv[l], gb_col, mlstm_gate_b[l].reshape(8, 1),
                    mlstm_norm[l][None], batch=b, seq=s, blk=MLSTM_BLOCK)

        ybs, lses = [], []
        for g, (_, dilation) in enumerate(ATTN_PATTERNS):
            o, lse = _dattn(aproj, biases[g], seq=s, group=g, dilation=dilation)
            ybs.append(o)
            lses.append(lse)

        ws = jnp.where(tril, gmlp_ws[l], 0.0).astype(BF16)
        bsb = jnp.broadcast_to(gmlp_bs[l][:, :, None], (GMLP_GROUPS, GMLP_CHUNK, GMLP_GC)).astype(F32)
        x2d = _merge(ya, ybs, lses, proj, x2d, w_branch_a[l].astype(BF16), w_branch_b[l].astype(BF16),
                     w_branch_c[l].astype(BF16), w_out[l].astype(BF16), ws, bsb, gmlp_norm[l][None], tm=256)

        k_mem, v_mem = _memkv(mem, norm_mem[l][None], w_xkv[l].astype(BF16), xattn_qk_norm[l, 1][None])
        x2d, hf_rows, eidx, wts = _xattn(x2d, k_mem, v_mem, norm_xattn[l][None], w_xq[l].astype(BF16),
                                         xattn_qk_norm[l, 0][None], w_xo[l].astype(BF16), norm_ffn[l][None],
                                         rw_t, rb, seq=s, tm=512)

        x2d = _moe(x2d, hf_rows, eidx, wts, w_expert_gate[l].astype(BF16), w_expert_up[l].astype(BF16),
                   w_expert_down[l].astype(BF16))

    return x2d.reshape(b, s, d)
```

```python
import functools
import math

import jax
import jax.numpy as jnp
import numpy as np
from jax import lax
from jax.experimental import pallas as pl
from jax.experimental.pallas import tpu as pltpu
from jax.experimental.pallas import tpu_sc as plsc

F32 = jnp.float32
BF16 = jnp.bfloat16

EPS = 1e-6
NEG = -1e30

MLSTM_HEADS = 4
MLSTM_DH = 128
MLSTM_W = MLSTM_HEADS * MLSTM_DH
CONV_WIDTH = 4
MLSTM_BLOCK = 128
MLSTM_NSUB = 4

ATTN_PATTERNS = ((128, 1), (512, 4), (2048, 16))
HEADS_PER_GROUP = 4
ATTN_DH = 64
ATTN_GW = HEADS_PER_GROUP * ATTN_DH
ATTN_W = len(ATTN_PATTERNS) * ATTN_GW
ATTN_BLOCK = 128
REL_BUCKETS = 32
REL_MAX_DIST = 2048

GMLP_GROUPS = 4
GMLP_GC = 128
GMLP_W = GMLP_GROUPS * GMLP_GC
GMLP_CHUNK = 128

XATTN_HEADS = 4
XATTN_DH = 128
XATTN_W = XATTN_HEADS * XATTN_DH

N_EXPERTS = 16
N_EXPERT_GROUPS = 4
EXPERTS_PER_GROUP = 4
ROUTER_ROWS = 8 * N_EXPERT_GROUPS

N_BRANCH = 3

MOE_TM = 256
ROW_CHUNKS = 4
SC_CORES, SC_SUBCORES = 2, 16
SC_WINDOW = 128

OFF_MQ, OFF_MK, OFF_MV, OFF_MO = 0, 512, 1024, 1536
OFF_GU, OFF_GV = 2048, 2560
OFF_GATE = 3072
OFF_IF = 6144
IF_PAD = 256
N_PROJ = OFF_IF + IF_PAD

ATTN_TILE = 2048
ATTN_SUB = ATTN_TILE // ATTN_BLOCK
ATTN_SLAB = 2 * ATTN_DH
ATTN_COLS = HEADS_PER_GROUP * ATTN_SLAB + 2 * ATTN_GW

VMEM_LIMIT = 48 * 1024 * 1024


def _cparams(*sem):
    return pltpu.CompilerParams(dimension_semantics=sem, vmem_limit_bytes=VMEM_LIMIT)


def _rms(x, gain):
    return x * lax.rsqrt(jnp.mean(x * x, axis=-1, keepdims=True) + EPS) * gain


def _dot(a, b):
    return jnp.dot(a, b, preferred_element_type=F32)


def _dot_nt(a, b):
    return lax.dot_general(a, b, (((1,), (1,)), ((), ())), preferred_element_type=F32)


def _inproj_kernel(x_ref, g_ref, w_ref, o_ref, h_ref):
    @pl.when(pl.program_id(1) == 0)
    def _():
        h_ref[...] = _rms(x_ref[...], g_ref[...]).astype(BF16)

    o_ref[...] = _dot(h_ref[...], w_ref[...]).astype(o_ref.dtype)


def _inproj(x2d, gain, w, *, tm, tn):
    t, d = x2d.shape
    n = w.shape[1]
    return pl.pallas_call(
        _inproj_kernel,
        grid=(t // tm, n // tn),
        in_specs=[pl.BlockSpec((tm, d), lambda i, j: (i, 0)),
                  pl.BlockSpec((1, d), lambda i, j: (0, 0)),
                  pl.BlockSpec((d, tn), lambda i, j: (0, j))],
        out_specs=[pl.BlockSpec((tm, tn), lambda i, j: (i, j)),
                   pl.BlockSpec((tm, d), lambda i, j: (i, 0))],
        out_shape=[jax.ShapeDtypeStruct((t, n), BF16), jax.ShapeDtypeStruct((t, d), BF16)],
        compiler_params=_cparams("parallel", "arbitrary"),
        name="inproj",
    )(x2d, gain, w)


def _log_sigmoid(x):
    return jnp.minimum(x, 0.0) - jnp.log(1.0 + jnp.exp(-jnp.abs(x)))


def _mlstm_kernel(qk_ref, v_ref, og_ref, gc_ref, gr_ref, cw_ref, gbc_ref, gbr_ref, ng_ref, y_ref,
                  xe_scr, c_scr, n_scr, m_scr, *, blk, nsub):
    heads, dh, w = MLSTM_HEADS, MLSTM_DH, MLSTM_W
    hp = lax.Precision.HIGHEST

    @pl.when(pl.program_id(1) == 0)
    def _():
        xe_scr[0:8, :] = jnp.zeros((8, 2 * w), F32)
        c_scr[...] = jnp.zeros_like(c_scr)
        n_scr[...] = jnp.zeros_like(n_scr)
        m_scr[...] = jnp.zeros_like(m_scr)

    xe_scr[8:8 + nsub * blk, :] = qk_ref[...].astype(F32)
    cw = cw_ref[...]
    ri = lax.broadcasted_iota(jnp.int32, (blk, blk), 0)
    ci = lax.broadcasted_iota(jnp.int32, (blk, blk), 1)
    causal = ri >= ci
    tril = causal.astype(F32)
    triu = (ri <= ci).astype(F32)
    state = [(c_scr[h], n_scr[h:h + 1, :], m_scr[h:h + 1, 0:1]) for h in range(heads)]
    for c in range(nsub):
        state = _mlstm_chunk(c * blk, blk, state, cw, causal, tril, triu, hp, xe_scr, v_ref, og_ref, gc_ref,
                             gr_ref, gbc_ref, gbr_ref, ng_ref, y_ref)
    xe_scr[0:8, :] = xe_scr[nsub * blk:nsub * blk + 8, :]
    for h, (c_st, n_st, m_st) in enumerate(state):
        c_scr[h] = c_st
        n_scr[h:h + 1, :] = n_st
        m_scr[h:h + 1, :] = jnp.broadcast_to(m_st, (1, m_scr.shape[1]))


def _mlstm_chunk(r0, blk, state, cw, causal, tril, triu, hp, xe_scr, v_ref, og_ref, gc_ref, gr_ref, gbc_ref,
                 gbr_ref, ng_ref, y_ref):
    heads, dh, w = MLSTM_HEADS, MLSTM_DH, MLSTM_W
    rows = slice(r0, r0 + blk)
    conv = cw[CONV_WIDTH - 1:CONV_WIDTH, :] * xe_scr[8 + r0:8 + r0 + blk, :]
    for j in range(CONV_WIDTH - 1):
        off = 8 + r0 - (CONV_WIDTH - 1) + j
        conv = conv + cw[j:j + 1, :] * xe_scr[off:off + blk, :]
    qk = conv * jax.nn.sigmoid(conv)

    gcol = gc_ref[rows, :].astype(F32) + gbc_ref[...]
    grow = gr_ref[0, :, rows] + gbr_ref[...]
    bcol = jnp.dot(tril, _log_sigmoid(gcol), precision=hp, preferred_element_type=F32)
    brow = jnp.dot(_log_sigmoid(grow), triu, precision=hp, preferred_element_type=F32)

    new_state = []
    for h in range(heads):
        sl = slice(h * dh, (h + 1) * dh)
        b_c = bcol[:, heads + h:heads + h + 1]
        i_c = gcol[:, h:h + 1]
        b_r = brow[heads + h:heads + h + 1, :]
        i_r = grow[h:h + 1, :]
        c_st, n_st, m_st = state[h]

        d_mat = jnp.where(causal, b_c - b_r + i_r, NEG)
        inter = b_c + m_st
        m_t = jnp.maximum(inter, jnp.max(d_mat, axis=-1, keepdims=True))
        w_intra = jnp.exp(d_mat - m_t)
        w_inter = jnp.exp(inter - m_t)

        q_f = qk[:, sl]
        k_f = qk[:, w + h * dh:w + (h + 1) * dh] * (dh ** -0.5)
        q_b = q_f.astype(BF16)
        k_b = k_f.astype(BF16)
        v_b = v_ref[rows, sl]

        s = _dot_nt(q_b, k_b) * w_intra
        num = _dot(s.astype(BF16), v_b) + w_inter * _dot(q_b, c_st.astype(BF16))
        den = jnp.sum(s, axis=-1, keepdims=True) + w_inter * jnp.sum(q_f * n_st, axis=-1, keepdims=True)
        hh = num / jnp.maximum(jnp.abs(den), jnp.exp(-m_t))
        hn = _rms(hh, ng_ref[:, sl])
        y_ref[rows, sl] = (hn * jax.nn.sigmoid(og_ref[rows, sl].astype(F32))).astype(y_ref.dtype)

        b_last = b_c[blk - 1:blk, :]
        dec = b_last - b_c + i_c
        m_new = jnp.maximum(b_last + m_st, jnp.max(dec, axis=0, keepdims=True))
        w_k = jnp.exp(dec - m_new)
        w_c = jnp.exp(b_last + m_st - m_new)
        kw = k_f * w_k
        new_state.append((w_c * c_st + _dot(kw.T.astype(BF16), v_b),
                          w_c * n_st + jnp.sum(kw, axis=0, keepdims=True),
                          m_new))
    return new_state


def _mlstm(proj, gates_row, conv_w, gb_col, gb_row, norm_g, *, batch, seq, blk, nsub):
    t = proj.shape[0]
    rows = blk * nsub
    nblk = seq // rows
    w = MLSTM_W
    row = lambda b, c: b * nblk + c
    return pl.pallas_call(
        functools.partial(_mlstm_kernel, blk=blk, nsub=nsub),
        grid=(batch, nblk),
        in_specs=[pl.BlockSpec((rows, 2 * w), lambda b, c: (row(b, c), OFF_MQ // (2 * w))),
                  pl.BlockSpec((rows, w), lambda b, c: (row(b, c), OFF_MV // w)),
                  pl.BlockSpec((rows, w), lambda b, c: (row(b, c), OFF_MO // w)),
                  pl.BlockSpec((rows, IF_PAD), lambda b, c: (row(b, c), OFF_IF // IF_PAD)),
                  pl.BlockSpec((1, 8, rows), lambda b, c: (b, 0, c)),
                  pl.BlockSpec((CONV_WIDTH, 2 * w), lambda b, c: (0, 0)),
                  pl.BlockSpec((1, IF_PAD), lambda b, c: (0, 0)),
                  pl.BlockSpec((8, 1), lambda b, c: (0, 0)),
                  pl.BlockSpec((1, w), lambda b, c: (0, 0))],
        out_specs=pl.BlockSpec((rows, w), lambda b, c: (row(b, c), 0)),
        out_shape=jax.ShapeDtypeStruct((t, w), BF16),
        scratch_shapes=[pltpu.VMEM((rows + 8, 2 * w), F32),
                        pltpu.VMEM((MLSTM_HEADS, MLSTM_DH, MLSTM_DH), F32),
                        pltpu.VMEM((8, MLSTM_DH), F32),
                        pltpu.VMEM((8, 128), F32)],
        compiler_params=_cparams("parallel", "arbitrary"),
        name="mlstm",
    )(proj, proj, proj, proj, gates_row, conv_w, gb_col, gb_row, norm_g)


def _attnproj_kernel(h_ref, w_ref, seg_ref, gq_ref, gk_ref, o_ref, r_scr):
    j = pl.program_id(1)
    gw, half = ATTN_GW, ATTN_SLAB // 2
    sub_rows = 512

    def head_norm(x, gain):
        sq = x * x
        hi = sq.astype(BF16)
        lo = (sq - hi.astype(F32)).astype(BF16)
        ss = _dot(hi, seg_ref[...]) + _dot(lo, seg_ref[...])
        return x * lax.rsqrt(ss * (1.0 / ATTN_DH) + EPS) * gain

    low = lax.broadcasted_iota(jnp.int32, (1, ATTN_SLAB), 1) < half
    for s in range(ATTN_TILE // sub_rows):
        rows = slice(s * sub_rows, (s + 1) * sub_rows)
        res = _dot(h_ref[rows, :], w_ref[...])
        q = head_norm(res[:, :gw], gq_ref[...]) * (ATTN_DH ** -0.5)
        k = head_norm(res[:, gw:2 * gw], gk_ref[...])
        slabs = []
        for pair in range(gw // ATTN_SLAB):
            qp = q[:, pair * ATTN_SLAB:(pair + 1) * ATTN_SLAB]
            slabs += [jnp.where(low, qp, 0.0), jnp.where(low, 0.0, qp)]
        slabs += [k[:, c * 128:(c + 1) * 128] for c in range(gw // 128)]
        slabs += [res[:, 2 * gw + c * 128:2 * gw + (c + 1) * 128] for c in range(gw // 128)]
        for c, slab in enumerate(slabs):
            r_scr[c, rows, :] = slab

    for g, (_, dil) in enumerate(ATTN_PATTERNS):
        @pl.when(j == g)
        def _(dil=dil):
            seg = ATTN_TILE // dil
            for r in range(dil):
                for c in range(r_scr.shape[0]):
                    src = r_scr[c, pl.ds(r, seg, stride=dil), :] if dil > 1 else r_scr[c]
                    o_ref[r * seg:(r + 1) * seg, c * 128:(c + 1) * 128] = src.astype(o_ref.dtype)


def _attnproj(h, w, seg_ones, gq, gk):
    t, d = h.shape
    ng = len(ATTN_PATTERNS)
    wcols = 3 * ATTN_GW
    const2 = lambda i, j: (0, 0)
    return pl.pallas_call(
        _attnproj_kernel,
        grid=(t // ATTN_TILE, ng),
        in_specs=[pl.BlockSpec((ATTN_TILE, d), lambda i, j: (i, 0)),
                  pl.BlockSpec((d, wcols), lambda i, j: (0, j)),
                  pl.BlockSpec((ATTN_GW, ATTN_GW), const2),
                  pl.BlockSpec((1, ATTN_GW), const2), pl.BlockSpec((1, ATTN_GW), const2)],
        out_specs=pl.BlockSpec((ATTN_TILE, ATTN_COLS), lambda i, j: (i, j)),
        out_shape=jax.ShapeDtypeStruct((t, ng * ATTN_COLS), BF16),
        scratch_shapes=[pltpu.VMEM((ATTN_COLS // 128, ATTN_TILE, 128), F32)],
        compiler_params=_cparams("parallel", "arbitrary"),
        name="attnproj",
    )(h, w, seg_ones, gq, gk)


def _dattn_kernel(q_ref, kc_ref, kp_ref, vc_ref, vp_ref, bias_ref, o_ref, lse_ref,
                  kx_scr, vx_scr, o_scr, l_scr, *, dil):
    blk = ATTN_BLOCK
    per = ATTN_SUB // dil
    first_tile = pl.program_id(1) == 0
    for r in range(dil):
        base = r * (per + 1) * blk
        last = slice((r * per + per - 1) * blk, (r * per + per) * blk)
        mine = slice(r * per * blk, (r + 1) * per * blk)
        kx_scr[base:base + blk, :] = kp_ref[last, :]
        vx_scr[base:base + blk, :] = vp_ref[last, :]
        kx_scr[base + blk:base + (per + 1) * blk, :] = kc_ref[mine, :]
        vx_scr[base + blk:base + (per + 1) * blk, :] = vc_ref[mine, :]

    low = lax.broadcasted_iota(jnp.int32, (1, ATTN_SLAB), 1) < ATTN_SLAB // 2
    no_prev = lax.broadcasted_iota(jnp.int32, (1, 2 * blk), 1) < blk
    for r in range(dil):
        for sub in range(per):
            u = r * per + sub
            win = slice((r * (per + 1) + sub) * blk, (r * (per + 1) + sub + 2) * blk)
            o_slabs, l_slabs = [], []
            for pair in range(ATTN_GW // ATTN_SLAB):
                cols = slice(pair * ATTN_SLAB, (pair + 1) * ATTN_SLAB)
                kx, vx = kx_scr[win, cols], vx_scr[win, cols]
                o_pair, l_pair = [], []
                for h in (2 * pair, 2 * pair + 1):
                    logits = _dot_nt(q_ref[u * blk:(u + 1) * blk, h * ATTN_SLAB:(h + 1) * ATTN_SLAB], kx)
                    logits = logits + bias_ref[h]
                    if sub == 0:
                        logits = jnp.where(first_tile & no_prev, NEG, logits)
                    m = jnp.max(logits, axis=-1, keepdims=True)
                    p = jnp.exp(logits - m)
                    l = jnp.sum(p, axis=-1, keepdims=True)
                    o_pair.append(_dot(p.astype(BF16), vx) / l)
                    l_pair.append(m + jnp.log(l))
                o_slabs.append(jnp.where(low, o_pair[0], o_pair[1]))
                l_slabs.append(jnp.where(low, l_pair[0], l_pair[1]))
            dst = pl.ds(sub * blk * dil + r, blk, stride=dil) if dil > 1 else slice(u * blk, (u + 1) * blk)
            for c in range(ATTN_GW // ATTN_SLAB):
                o_scr[c, dst, :] = o_slabs[c]
                l_scr[c, dst, :] = l_slabs[c]
    for c in range(ATTN_GW // ATTN_SLAB):
        o_ref[:, c * ATTN_SLAB:(c + 1) * ATTN_SLAB] = o_scr[c].astype(o_ref.dtype)
        lse_ref[:, c * ATTN_SLAB:(c + 1) * ATTN_SLAB] = l_scr[c]


def _dattn(aproj, bias, *, seq, group, dilation):
    t = aproj.shape[0]
    tiles = seq // ATTN_TILE
    qw = HEADS_PER_GROUP * ATTN_SLAB
    cq = group * ATTN_COLS // qw
    ck, cv = (group * ATTN_COLS + qw) // ATTN_GW, (group * ATTN_COLS + qw) // ATTN_GW + 1
    blk = (ATTN_TILE, ATTN_GW)
    cur = lambda c: (lambda b, j: (b * tiles + j, c))
    prev = lambda c: (lambda b, j: (b * tiles + jnp.maximum(j - 1, 0), c))
    xrows = ATTN_TILE + dilation * ATTN_BLOCK
    return pl.pallas_call(
        functools.partial(_dattn_kernel, dil=dilation),
        grid=(t // seq, tiles),
        in_specs=[pl.BlockSpec((ATTN_TILE, qw), cur(cq)),
                  pl.BlockSpec(blk, cur(ck)), pl.BlockSpec(blk, prev(ck)),
                  pl.BlockSpec(blk, cur(cv)), pl.BlockSpec(blk, prev(cv)),
                  pl.BlockSpec((HEADS_PER_GROUP, ATTN_BLOCK, 2 * ATTN_BLOCK), lambda b, j: (0, 0, 0))],
        out_specs=[pl.BlockSpec(blk, cur(0)), pl.BlockSpec(blk, cur(0))],
        out_shape=[jax.ShapeDtypeStruct((t, ATTN_GW), BF16), jax.ShapeDtypeStruct((t, ATTN_GW), F32)],
        scratch_shapes=[pltpu.VMEM((xrows, ATTN_GW), BF16), pltpu.VMEM((xrows, ATTN_GW), BF16),
                        pltpu.VMEM((ATTN_GW // ATTN_SLAB, ATTN_TILE, ATTN_SLAB), F32),
                        pltpu.VMEM((ATTN_GW // ATTN_SLAB, ATTN_TILE, ATTN_SLAB), F32)],
        compiler_params=_cparams("parallel", "arbitrary"),
        name=f"dattn{group}",
    )(aproj, aproj, aproj, aproj, aproj, bias)


def _rel_bucket(n):
    max_exact = REL_BUCKETS // 2
    nf = jnp.maximum(n, 1).astype(F32)
    log_b = max_exact + (jnp.log(nf / max_exact) / math.log(REL_MAX_DIST / max_exact)
                         * (REL_BUCKETS - max_exact)).astype(jnp.int32)
    return jnp.where(n < max_exact, n, jnp.minimum(log_b, REL_BUCKETS - 1))


def _attn_bias(rel_bias, group):
    window, dilation = ATTN_PATTERNS[group]
    steps = window // dilation
    hp = lax.Precision.HIGHEST
    hs = slice(group * HEADS_PER_GROUP, (group + 1) * HEADS_PER_GROUP)
    bucket = _rel_bucket(jnp.arange(steps + 1) * dilation)
    bias_steps = jnp.dot(jax.nn.one_hot(bucket, REL_BUCKETS, dtype=F32), rel_bias[:, hs].astype(F32),
                         precision=hp)
    qi = jnp.arange(ATTN_BLOCK)[:, None]
    ki = jnp.arange(2 * ATTN_BLOCK)[None, :]
    dist = ATTN_BLOCK + qi - ki
    ok = (dist >= 0) & (dist <= steps)
    sel = jax.nn.one_hot(jnp.clip(dist, 0, steps).reshape(-1), steps + 1, dtype=F32)
    bias = jnp.dot(sel, bias_steps, precision=hp).T.reshape(HEADS_PER_GROUP, ATTN_BLOCK, 2 * ATTN_BLOCK)
    return jnp.where(ok[None], bias, NEG)


def _merge_kernel(ya_ref, yb0_ref, yb1_ref, yb2_ref, l0_ref, l1_ref, l2_ref, gu_ref, gv_ref, gate_ref,
                  x_ref, wa_ref, wb_ref, wc_ref, wo_ref, ws_ref, bs_ref, gg_ref, o_ref, yc_scr, *, tm):
    d = x_ref.shape[1]
    l0, l1, l2 = l0_ref[...], l1_ref[...], l2_ref[...]
    mx = jnp.maximum(jnp.maximum(l0, l1), l2)
    e0, e1, e2 = jnp.exp(l0 - mx), jnp.exp(l1 - mx), jnp.exp(l2 - mx)
    inv = 1.0 / (e0 + e1 + e2)
    yb = jnp.concatenate([(yb0_ref[...].astype(F32) * (e0 * inv)).astype(BF16),
                          (yb1_ref[...].astype(F32) * (e1 * inv)).astype(BF16),
                          (yb2_ref[...].astype(F32) * (e2 * inv)).astype(BF16)], axis=-1)

    for j in range(tm // GMLP_CHUNK):
        rows = slice(j * GMLP_CHUNK, (j + 1) * GMLP_CHUNK)
        for g in range(GMLP_GROUPS):
            cols = slice(g * GMLP_GC, (g + 1) * GMLP_GC)
            u = jax.nn.gelu(gu_ref[rows, cols].astype(F32))
            v = _rms(jax.nn.gelu(gv_ref[rows, cols].astype(F32)), gg_ref[:, cols])
            mixed = _dot(ws_ref[g], v.astype(BF16)) + bs_ref[g]
            yc_scr[rows, cols] = (u * mixed).astype(BF16)

    merged = jax.nn.sigmoid(gate_ref[:, 0:d].astype(F32)) * _dot(ya_ref[...], wa_ref[...])
    merged = merged + jax.nn.sigmoid(gate_ref[:, d:2 * d].astype(F32)) * _dot(yb, wb_ref[...])
    merged = merged + jax.nn.sigmoid(gate_ref[:, 2 * d:3 * d].astype(F32)) * _dot(yc_scr[...], wc_ref[...])
    o_ref[...] = x_ref[...] + _dot(merged.astype(BF16), wo_ref[...])


def _merge(ya, ybs, lses, proj, x2d, wa, wb, wc, wo, ws, bsb, gg, *, tm):
    t, d = x2d.shape
    row = lambda c: (lambda i: (i, c))
    full2 = lambda i: (0, 0)
    full3 = lambda i: (0, 0, 0)
    gspec = pl.BlockSpec((tm, ATTN_GW), row(0))
    return pl.pallas_call(
        functools.partial(_merge_kernel, tm=tm),
        grid=(t // tm,),
        in_specs=[pl.BlockSpec((tm, MLSTM_W), row(0)),
                  gspec, gspec, gspec, gspec, gspec, gspec,
                  pl.BlockSpec((tm, GMLP_W), row(OFF_GU // GMLP_W)),
                  pl.BlockSpec((tm, GMLP_W), row(OFF_GV // GMLP_W)),
                  pl.BlockSpec((tm, N_BRANCH * d), row(OFF_GATE // (N_BRANCH * d))),
                  pl.BlockSpec((tm, d), row(0)),
                  pl.BlockSpec(wa.shape, full2), pl.BlockSpec(wb.shape, full2),
                  pl.BlockSpec(wc.shape, full2), pl.BlockSpec(wo.shape, full2),
                  pl.BlockSpec(ws.shape, full3), pl.BlockSpec(bsb.shape, full3),
                  pl.BlockSpec(gg.shape, full2)],
        out_specs=pl.BlockSpec((tm, d), row(0)),
        out_shape=jax.ShapeDtypeStruct((t, d), F32),
        scratch_shapes=[pltpu.VMEM((tm, GMLP_W), BF16)],
        compiler_params=_cparams("parallel"),
        name="merge",
    )(ya, *ybs, *lses, proj, proj, proj, x2d, wa, wb, wc, wo, ws, bsb, gg)


def _memkv_kernel(mem_ref, g_ref, w_ref, gk_ref, k_ref, v_ref):
    dh, w = XATTN_DH, XATTN_W
    kv = _dot(_rms(mem_ref[0], g_ref[...]).astype(BF16), w_ref[...])
    for h in range(XATTN_HEADS):
        sl = slice(h * dh, (h + 1) * dh)
        k_ref[0, :, sl] = _rms(kv[:, sl], gk_ref[...]).astype(k_ref.dtype)
    v_ref[0] = kv[:, w:].astype(v_ref.dtype)


def _memkv(mem, gain, w_kv, gk):
    b, m, d = mem.shape
    full2 = lambda i: (0, 0)
    return pl.pallas_call(
        _memkv_kernel,
        grid=(b,),
        in_specs=[pl.BlockSpec((1, m, d), lambda i: (i, 0, 0)),
                  pl.BlockSpec((1, d), full2),
                  pl.BlockSpec(w_kv.shape, full2),
                  pl.BlockSpec((1, XATTN_DH), full2)],
        out_specs=[pl.BlockSpec((1, m, XATTN_W), lambda i: (i, 0, 0)),
                   pl.BlockSpec((1, m, XATTN_W), lambda i: (i, 0, 0))],
        out_shape=[jax.ShapeDtypeStruct((b, m, XATTN_W), BF16),
                   jax.ShapeDtypeStruct((b, m, XATTN_W), BF16)],
        compiler_params=_cparams("parallel"),
        name="memkv",
    )(mem, gain, w_kv, gk)


def _route(logits):
    tm = logits.shape[1]
    e = jnp.exp(logits - jnp.max(logits, axis=0, keepdims=True))
    probs = e / jnp.sum(e, axis=0, keepdims=True)
    rowi = lax.broadcasted_iota(jnp.int32, (8, tm), 0)
    real = rowi < EXPERTS_PER_GROUP
    tops = []
    for g in range(N_EXPERT_GROUPS):
        pg = jnp.where(real, probs[8 * g:8 * g + 8, :], -0.5)
        m1 = jnp.max(pg, axis=0, keepdims=True)
        i1 = jnp.min(jnp.where(pg == m1, rowi, 8), axis=0, keepdims=True)
        pg2 = jnp.where(rowi == i1, -1.0, pg)
        m2 = jnp.max(pg2, axis=0, keepdims=True)
        i2 = jnp.min(jnp.where(pg2 == m2, rowi, 8), axis=0, keepdims=True)
        tops.append((m1, i1, m2, i2))
    best = jnp.zeros((1, tm), jnp.int32)
    best_score = tops[0][0] + tops[0][2]
    for g in range(1, N_EXPERT_GROUPS):
        score = tops[g][0] + tops[g][2]
        better = score > best_score
        best = jnp.where(better, g, best)
        best_score = jnp.where(better, score, best_score)
    m1, i1, m2, i2 = tops[0]
    for g in range(1, N_EXPERT_GROUPS):
        m1, i1, m2, i2 = (jnp.where(best == g, new, old) for new, old in zip(tops[g], (m1, i1, m2, i2)))
    tot = m1 + m2
    base = best * EXPERTS_PER_GROUP
    return base + i1, base + i2, m1 / tot, m2 / tot


def _pack_bf16_pairs(x):
    n = x.shape[1] // 2
    hi = lax.bitcast_convert_type(x[:, :n].astype(BF16).astype(F32), jnp.uint32)
    lo = lax.bitcast_convert_type(x[:, n:].astype(BF16).astype(F32), jnp.uint32)
    return hi | (lo >> 16)


def _unpack_bf16_pairs(p):
    hi = lax.bitcast_convert_type(p & jnp.uint32(0xFFFF0000), F32)
    lo = lax.bitcast_convert_type(p << 16, F32)
    return hi, lo


def _store_row_chunks(ref, packed):
    for j in range(ROW_CHUNKS):
        ref[j] = packed[:, j * 128:(j + 1) * 128]


def _load_row_chunks(ref):
    return jnp.concatenate([ref[j] for j in range(ROW_CHUNKS)], axis=-1)


def _xattn_kernel(x_ref, k_ref, v_ref, gx_ref, wq_ref, gq_ref, wo_ref, gf_ref, rw_ref, rb_ref,
                  xo_ref, hf_ref, eidx_ref, wts_ref):
    dh = XATTN_DH
    x = x_ref[...]
    q = _dot(_rms(x, gx_ref[...]).astype(BF16), wq_ref[...])
    outs = []
    for h in range(XATTN_HEADS):
        sl = slice(h * dh, (h + 1) * dh)
        q_h = (_rms(q[:, sl], gq_ref[...]) * (dh ** -0.5)).astype(BF16)
        logits = _dot_nt(q_h, k_ref[0, :, sl])
        p = jnp.exp(logits - jnp.max(logits, axis=-1, keepdims=True))
        o = _dot(p.astype(BF16), v_ref[0, :, sl]) / jnp.sum(p, axis=-1, keepdims=True)
        outs.append(o.astype(BF16))
    xn = x + _dot(jnp.concatenate(outs, axis=-1), wo_ref[...])
    xo_ref[...] = xn
    hf = _rms(xn, gf_ref[...])
    _store_row_chunks(hf_ref, _pack_bf16_pairs(hf))
    logits_t = lax.dot_general(rw_ref[...], hf, (((1,), (1,)), ((), ())),
                               precision=lax.Precision.HIGHEST, preferred_element_type=F32) + rb_ref[...]
    e1, e2, w1, w2 = _route(logits_t)
    tm = x.shape[0]
    eidx_ref[...] = jnp.concatenate([e1, e2, jnp.zeros((6, tm), jnp.int32)], axis=0)
    wts_ref[...] = jnp.concatenate([w1, w2, jnp.zeros((6, tm), F32)], axis=0)


def _xattn(x2d, k, v, gx, wq, gq, wo, gf, rw_t, rb, *, seq, tm):
    t, d = x2d.shape
    per_b = seq // tm
    full2 = lambda i: (0, 0)
    kv_spec = pl.BlockSpec((1,) + k.shape[1:], lambda i: (i // per_b, 0, 0))
    return pl.pallas_call(
        _xattn_kernel,
        grid=(t // tm,),
        in_specs=[pl.BlockSpec((tm, d), lambda i: (i, 0)), kv_spec, kv_spec,
                  pl.BlockSpec((1, d), full2), pl.BlockSpec(wq.shape, full2),
                  pl.BlockSpec((1, XATTN_DH), full2), pl.BlockSpec(wo.shape, full2),
                  pl.BlockSpec((1, d), full2), pl.BlockSpec(rw_t.shape, full2),
                  pl.BlockSpec(rb.shape, full2)],
        out_specs=[pl.BlockSpec((tm, d), lambda i: (i, 0)),
                   pl.BlockSpec((ROW_CHUNKS, tm, 128), lambda i: (0, i, 0)),
                   pl.BlockSpec((8, tm), lambda i: (0, i)),
                   pl.BlockSpec((8, tm), lambda i: (0, i))],
        out_shape=[jax.ShapeDtypeStruct((t, d), F32),
                   jax.ShapeDtypeStruct((ROW_CHUNKS, t, 128), jnp.uint32),
                   jax.ShapeDtypeStruct((8, t), jnp.int32),
                   jax.ShapeDtypeStruct((8, t), F32)],
        compiler_params=_cparams("parallel"),
        name="xattn_router",
    )(x2d, k, v, gx, wq, gq, wo, gf, rw_t, rb)


def _moe_plan_kernel(eidx_ref, i1_ref, i2_ref, te_ref, na_ref, cnt_scr, carry_scr, *, tb, tm, plane_rows):
    ne = N_EXPERTS
    hp = lax.Precision.HIGHEST
    phase, j = pl.program_id(0), pl.program_id(1)
    rows = lax.broadcasted_iota(jnp.int32, (ne, tb), 0)
    oh1 = rows == eidx_ref[0:1, :]
    oh2 = rows == eidx_ref[1:2, :]
    a = oh1.astype(F32) + oh2.astype(F32)
    blk_cnt = jnp.broadcast_to(jnp.sum(a, axis=1, keepdims=True), cnt_scr.shape)

    @pl.when((phase == 0) & (j == 0))
    def _():
        cnt_scr[...] = jnp.zeros_like(cnt_scr)

    @pl.when(phase == 0)
    def _():
        cnt_scr[...] += blk_cnt

    @pl.when((phase == 1) & (j == 0))
    def _():
        padded = jnp.ceil(cnt_scr[...] * (1.0 / tm)) * tm
        er = lax.broadcasted_iota(jnp.int32, (ne, ne), 0)
        ec = lax.broadcasted_iota(jnp.int32, (ne, ne), 1)
        off = jnp.dot((ec < er).astype(F32), padded, precision=hp, preferred_element_type=F32)
        carry_scr[...] = off
        seg_end = (off + padded)[:, 0:1]
        tile_start = lax.broadcasted_iota(jnp.int32, (ne, te_ref.shape[1]), 1).astype(F32) * tm
        te = jnp.sum((seg_end <= tile_start).astype(F32), axis=0, keepdims=True)
        te_ref[...] = jnp.broadcast_to(jnp.minimum(te, ne - 1.0), te_ref.shape).astype(jnp.int32)
        total = jnp.sum(padded[:, 0:1], axis=0, keepdims=True)
        na_ref[...] = jnp.broadcast_to(total * (1.0 / tm), na_ref.shape).astype(jnp.int32)

    @pl.when(phase == 1)
    def _():
        before = (lax.broadcasted_iota(jnp.int32, (tb, tb), 0)
                  < lax.broadcasted_iota(jnp.int32, (tb, tb), 1)).astype(BF16)
        rank = carry_scr[:, 0:1] + _dot(a.astype(BF16), before)
        d1 = jnp.sum(jnp.where(oh1, rank, 0.0), axis=0, keepdims=True).astype(jnp.int32)
        d2 = jnp.sum(jnp.where(oh2, rank, 0.0), axis=0, keepdims=True).astype(jnp.int32)
        plane = lax.broadcasted_iota(jnp.int32, (8, tb), 0) * plane_rows
        i1_ref[...] = jnp.where(plane < ROW_CHUNKS * plane_rows, plane + d1, 0)
        i2_ref[...] = jnp.where(plane < ROW_CHUNKS * plane_rows, plane + d2, 0)
        carry_scr[...] += blk_cnt


def _moe_plan(eidx, *, tm, n_tiles, tb=512):
    t = eidx.shape[1]
    ntp = -(-n_tiles // 128) * 128
    return pl.pallas_call(
        functools.partial(_moe_plan_kernel, tb=tb, tm=tm, plane_rows=n_tiles * tm),
        grid=(2, t // tb),
        in_specs=[pl.BlockSpec((8, tb), lambda p, j: (0, j))],
        out_specs=[pl.BlockSpec((8, tb), lambda p, j: (0, j * p)),
                   pl.BlockSpec((8, tb), lambda p, j: (0, j * p)),
                   pl.BlockSpec((8, ntp), lambda p, j: (0, 0)),
                   pl.BlockSpec((8, 128), lambda p, j: (0, 0))],
        out_shape=[jax.ShapeDtypeStruct((8, t), jnp.int32),
                   jax.ShapeDtypeStruct((8, t), jnp.int32),
                   jax.ShapeDtypeStruct((8, ntp), jnp.int32),
                   jax.ShapeDtypeStruct((8, 128), jnp.int32)],
        scratch_shapes=[pltpu.VMEM((N_EXPERTS, 128), F32), pltpu.VMEM((N_EXPERTS, 128), F32)],
        compiler_params=_cparams("arbitrary", "arbitrary"),
        name="moe_plan",
    )(eidx)


def _sc_mesh():
    return plsc.VectorSubcoreMesh(core_axis_name="c", subcore_axis_name="s",
                                  num_cores=SC_CORES, num_subcores=SC_SUBCORES)


def _sc_index_spec(tokens):
    nb = tokens // SC_WINDOW
    return pl.BlockSpec((1, SC_WINDOW), lambda i: (i // nb, i % nb))


def _sc_dispatch(rows, i1, i2, n_out):
    n = rows.shape[0]
    tokens = i1.shape[1]

    @functools.partial(pl.kernel, out_type=jax.ShapeDtypeStruct((n_out, 128), rows.dtype), mesh=_sc_mesh(),
                       name="moe_dispatch")
    def k(x_hbm, i1_hbm, i2_hbm, o_hbm):
        def body(x_vmem, i1_vmem, i2_vmem):
            pltpu.sync_copy(x_vmem, o_hbm.at[i1_vmem.at[0]])
            pltpu.sync_copy(x_vmem, o_hbm.at[i2_vmem.at[0]])

        pltpu.emit_pipeline(
            body, grid=(n // SC_WINDOW,),
            in_specs=[pl.BlockSpec((SC_WINDOW, 128), lambda i: (i, 0)),
                      _sc_index_spec(tokens), _sc_index_spec(tokens)],
            out_specs=[],
            core_axis_name=("c", "s"), dimension_semantics=(pltpu.PARALLEL,),
        )(x_hbm, i1_hbm, i2_hbm)

    return k(rows, i1, i2)


def _sc_collect(table, i1, i2):
    tokens = i1.shape[1]
    n = ROW_CHUNKS * tokens
    out = jax.ShapeDtypeStruct((n, 128), table.dtype)

    @functools.partial(pl.kernel, out_type=(out, out), mesh=_sc_mesh(), name="moe_collect")
    def k(t_hbm, i1_hbm, i2_hbm, o1_hbm, o2_hbm):
        def body(i1_vmem, i2_vmem, o1_vmem, o2_vmem):
            pltpu.sync_copy(t_hbm.at[i1_vmem.at[0]], o1_vmem)
            pltpu.sync_copy(t_hbm.at[i2_vmem.at[0]], o2_vmem)

        pltpu.emit_pipeline(
            body, grid=(n // SC_WINDOW,),
            in_specs=[_sc_index_spec(tokens), _sc_index_spec(tokens)],
            out_specs=[pl.BlockSpec((SC_WINDOW, 128), lambda i: (i, 0)),
                       pl.BlockSpec((SC_WINDOW, 128), lambda i: (i, 0))],
            core_axis_name=("c", "s"), dimension_semantics=(pltpu.PARALLEL,),
        )(i1_hbm, i2_hbm, o1_hbm, o2_hbm)

    return k(table, i1, i2)


def _experts_kernel(te_ref, na_ref, xs_ref, wg_ref, wu_ref, wd_ref, y_ref, wg_scr, wu_scr, wd_scr):
    i = pl.program_id(0)
    active = i < na_ref[0]

    @pl.when(active & ((i == 0) | (te_ref[i] != te_ref[jnp.maximum(i - 1, 0)])))
    def _():
        wg_scr[...] = wg_ref[0].astype(BF16)
        wu_scr[...] = wu_ref[0].astype(BF16)
        wd_scr[...] = wd_ref[0].astype(BF16)

    @pl.when(active)
    def _():
        hi, lo = _unpack_bf16_pairs(_load_row_chunks(xs_ref))
        h = jnp.concatenate([hi, lo], axis=-1).astype(BF16)
        up = _dot(h, wg_scr[...])
        act = up * jax.nn.sigmoid(up) * _dot(h, wu_scr[...])
        _store_row_chunks(y_ref, _pack_bf16_pairs(_dot(act.astype(BF16), wd_scr[...])))


def _experts(tile_expert, n_active, xs, wg, wu, wd, *, tm):
    n_tiles = tile_expert.shape[0]
    _, d, dff = wg.shape
    rows = lambda i, te, na: (0, jnp.minimum(i, na[0] - 1), 0)
    return pl.pallas_call(
        _experts_kernel,
        grid_spec=pltpu.PrefetchScalarGridSpec(
            num_scalar_prefetch=2,
            grid=(n_tiles,),
            in_specs=[pl.BlockSpec((ROW_CHUNKS, tm, 128), rows),
                      pl.BlockSpec((1, d, dff), lambda i, te, na: (te[i], 0, 0)),
                      pl.BlockSpec((1, d, dff), lambda i, te, na: (te[i], 0, 0)),
                      pl.BlockSpec((1, dff, d), lambda i, te, na: (te[i], 0, 0))],
            out_specs=pl.BlockSpec((ROW_CHUNKS, tm, 128), rows),
            scratch_shapes=[pltpu.VMEM((d, dff), BF16), pltpu.VMEM((d, dff), BF16), pltpu.VMEM((dff, d), BF16)]),
        out_shape=jax.ShapeDtypeStruct(xs.shape, xs.dtype),
        compiler_params=_cparams("arbitrary"),
        name="moe_experts",
    )(tile_expert, n_active, xs, wg, wu, wd)


def _moe_combine_kernel(x_ref, y1_ref, y2_ref, w_ref, o_ref):
    half = x_ref.shape[1] // 2
    hi1, lo1 = _unpack_bf16_pairs(_load_row_chunks(y1_ref))
    hi2, lo2 = _unpack_bf16_pairs(_load_row_chunks(y2_ref))
    w1, w2 = w_ref[:, 0:1], w_ref[:, 1:2]
    o_ref[:, :half] = x_ref[:, :half] + w1 * hi1 + w2 * hi2
    o_ref[:, half:] = x_ref[:, half:] + w1 * lo1 + w2 * lo2


def _moe_combine(x2d, y1, y2, wcol, *, tm):
    t, d = x2d.shape
    chunk_spec = pl.BlockSpec((ROW_CHUNKS, tm, 128), lambda i: (0, i, 0))
    return pl.pallas_call(
        _moe_combine_kernel,
        grid=(t // tm,),
        in_specs=[pl.BlockSpec((tm, d), lambda i: (i, 0)), chunk_spec, chunk_spec,
                  pl.BlockSpec((tm, wcol.shape[1]), lambda i: (i, 0))],
        out_specs=pl.BlockSpec((tm, d), lambda i: (i, 0)),
        out_shape=jax.ShapeDtypeStruct((t, d), F32),
        compiler_params=_cparams("parallel"),
        name="moe_combine",
    )(x2d, y1, y2, wcol)


def _moe(x2d, hf_rows, eidx, wts, wg, wu, wd):
    t = x2d.shape[0]
    tm = MOE_TM
    n_tiles = 2 * t // tm + N_EXPERTS
    plane = n_tiles * tm
    i1, i2, te, na = _moe_plan(eidx, tm=tm, n_tiles=n_tiles)
    xs = _sc_dispatch(hf_rows.reshape(ROW_CHUNKS * t, 128), i1, i2, ROW_CHUNKS * plane)
    ys = _experts(te[0, :n_tiles], na[0, :1], xs.reshape(ROW_CHUNKS, plane, 128), wg, wu, wd, tm=tm)
    y1, y2 = _sc_collect(ys.reshape(ROW_CHUNKS * plane, 128), i1, i2)
    return _moe_combine(x2d, y1.reshape(ROW_CHUNKS, t, 128), y2.reshape(ROW_CHUNKS, t, 128), wts[:2].T, tm=512)


def _layout_w_in(w):
    sizes = (MLSTM_W, MLSTM_W, MLSTM_W, MLSTM_W, MLSTM_HEADS, MLSTM_HEADS,
             ATTN_W, ATTN_W, ATTN_W, GMLP_W, GMLP_W, N_BRANCH * w.shape[0])
    pts = np.cumsum(sizes)[:-1]
    mq, mk, mv, mo, mi, mf, aq, ak, av, gu, gv, gate = jnp.split(w, pts, axis=-1)
    pad = jnp.zeros((w.shape[0], IF_PAD - 2 * MLSTM_HEADS), w.dtype)
    main = jnp.concatenate([mq, mk, mv, mo, gu, gv, gate, mi, mf, pad], axis=-1).astype(BF16)
    attn = jnp.concatenate([a[:, g * ATTN_GW:(g + 1) * ATTN_GW] for g in range(len(ATTN_PATTERNS))
                            for a in (aq, ak, av)], axis=-1).astype(BF16)
    return main, attn


def kernel(x, mem, norm_mix, w_in, mlstm_conv, mlstm_gate_b, mlstm_norm, attn_qk_norm, gmlp_norm, gmlp_ws,
           gmlp_bs, w_branch_a, w_branch_b, w_branch_c, w_out, rel_bias, norm_xattn, norm_mem, w_xq, w_xkv,
           xattn_qk_norm, w_xo, norm_ffn, router_w, router_b, w_expert_gate, w_expert_up, w_expert_down):
    b, s, d = x.shape
    t = b * s
    depth = w_in.shape[0]
    x2d = x.reshape(t, d)

    biases = [_attn_bias(rel_bias, g) for g in range(len(ATTN_PATTERNS))]
    rw_t = jnp.zeros((N_EXPERT_GROUPS, 8, d), F32).at[:, :EXPERTS_PER_GROUP].set(
        router_w.T.reshape(N_EXPERT_GROUPS, EXPERTS_PER_GROUP, d)).reshape(ROUTER_ROWS, d)
    rb = jnp.full((N_EXPERT_GROUPS, 8), NEG, F32).at[:, :EXPERTS_PER_GROUP].set(
        router_b.astype(F32).reshape(N_EXPERT_GROUPS, EXPERTS_PER_GROUP)).reshape(ROUTER_ROWS, 1)
    tril = jnp.tril(jnp.ones((GMLP_CHUNK, GMLP_CHUNK), bool))
    head_of = jnp.arange(ATTN_GW) // ATTN_DH
    seg_ones = (head_of[:, None] == head_of[None, :]).astype(BF16)

    for l in range(depth):
        w_main, w_attn = _layout_w_in(w_in[l])
        proj, h_mix = _inproj(x2d, norm_mix[l][None], w_main, tm=1024, tn=1280)
        gq = jnp.tile(attn_qk_norm[l, 0], HEADS_PER_GROUP)[None]
        gk = jnp.tile(attn_qk_norm[l, 1], HEADS_PER_GROUP)[None]
        aproj = _attnproj(h_mix, w_attn, seg_ones, gq, gk)

        gates_row = proj[:, OFF_IF:OFF_IF + 8].astype(F32).reshape(b, s, 8).transpose(0, 2, 1)
        gb_col = jnp.zeros((1, IF_PAD), F32).at[0, :8].set(mlstm_gate_b[l])
        ya = _mlstm(proj, gates_row, mlstm_conv[l], gb_col, mlstm_gate_b[l].reshape(8, 1),
                    mlstm_norm[l][None], batch=b, seq=s, blk=MLSTM_BLOCK, nsub=MLSTM_NSUB)

        ybs, lses = [], []
        for g, (_, dilation) in enumerate(ATTN_PATTERNS):
            o, lse = _dattn(aproj, biases[g], seq=s, group=g, dilation=dilation)
            ybs.append(o)
            lses.append(lse)

        ws = jnp.where(tril, gmlp_ws[l], 0.0).astype(BF16)
        bsb = jnp.broadcast_to(gmlp_bs[l][:, :, None], (GMLP_GROUPS, GMLP_CHUNK, GMLP_GC)).astype(F32)
        x2d = _merge(ya, ybs, lses, proj, x2d, w_branch_a[l].astype(BF16), w_branch_b[l].astype(BF16),
                     w_branch_c[l].astype(BF16), w_out[l].astype(BF16), ws, bsb, gmlp_norm[l][None], tm=256)

        k_mem, v_mem = _memkv(mem, norm_mem[l][None], w_xkv[l].astype(BF16), xattn_qk_norm[l, 1][None])
        x2d, hf_rows, eidx, wts = _xattn(x2d, k_mem, v_mem, norm_xattn[l][None], w_xq[l].astype(BF16),
                                         xattn_qk_norm[l, 0][None], w_xo[l].astype(BF16), norm_ffn[l][None],
                                         rw_t, rb, seq=s, tm=512)

        x2d = _moe(x2d, hf_rows, eidx, wts, w_expert_gate[l], w_expert_up[l], w_expert_down[l])

    return x2d.reshape(b, s, d)
```

```python
import functools
import math

import jax
import jax.numpy as jnp
import numpy as np
from jax import lax
from jax.experimental import pallas as pl
from jax.experimental.pallas import tpu as pltpu
from jax.experimental.pallas import tpu_sc as plsc

F32 = jnp.float32
BF16 = jnp.bfloat16

EPS = 1e-6
NEG = -1e30

MLSTM_HEADS = 4
MLSTM_DH = 128
MLSTM_W = MLSTM_HEADS * MLSTM_DH
CONV_WIDTH = 4
MLSTM_BLOCK = 128
MLSTM_NSUB = 1

ATTN_PATTERNS = ((128, 1), (512, 4), (2048, 16))
HEADS_PER_GROUP = 4
ATTN_DH = 64
ATTN_GW = HEADS_PER_GROUP * ATTN_DH
ATTN_W = len(ATTN_PATTERNS) * ATTN_GW
ATTN_BLOCK = 128
REL_BUCKETS = 32
REL_MAX_DIST = 2048

GMLP_GROUPS = 4
GMLP_GC = 128
GMLP_W = GMLP_GROUPS * GMLP_GC
GMLP_CHUNK = 128

XATTN_HEADS = 4
XATTN_DH = 128
XATTN_W = XATTN_HEADS * XATTN_DH

N_EXPERTS = 16
N_EXPERT_GROUPS = 4
EXPERTS_PER_GROUP = 4
ROUTER_ROWS = 8 * N_EXPERT_GROUPS

N_BRANCH = 3

MOE_TM = 256
ROW_CHUNKS = 4
SC_CORES, SC_SUBCORES = 2, 16
SC_WINDOW = 128

OFF_MQ, OFF_MK, OFF_MV, OFF_MO = 0, 512, 1024, 1536
OFF_GU, OFF_GV = 2048, 2560
OFF_GATE = 3072
OFF_IF = 6144
IF_PAD = 256
N_PROJ = OFF_IF + IF_PAD

ATTN_TILE = 2048
ATTN_SUB = ATTN_TILE // ATTN_BLOCK
ATTN_SLAB = 2 * ATTN_DH
ATTN_COLS = HEADS_PER_GROUP * ATTN_SLAB + 2 * ATTN_GW

VMEM_LIMIT = 48 * 1024 * 1024


def _cparams(*sem):
    return pltpu.CompilerParams(dimension_semantics=sem, vmem_limit_bytes=VMEM_LIMIT)


def _rms(x, gain):
    return x * lax.rsqrt(jnp.mean(x * x, axis=-1, keepdims=True) + EPS) * gain


def _dot(a, b):
    return jnp.dot(a, b, preferred_element_type=F32)


def _dot_nt(a, b):
    return lax.dot_general(a, b, (((1,), (1,)), ((), ())), preferred_element_type=F32)


def _inproj_kernel(x_ref, g_ref, w_ref, o_ref, h_ref):
    @pl.when(pl.program_id(1) == 0)
    def _():
        h_ref[...] = _rms(x_ref[...], g_ref[...]).astype(BF16)

    o_ref[...] = _dot(h_ref[...], w_ref[...]).astype(o_ref.dtype)


def _inproj(x2d, gain, w, *, tm, tn):
    t, d = x2d.shape
    n = w.shape[1]
    return pl.pallas_call(
        _inproj_kernel,
        grid=(t // tm, n // tn),
        in_specs=[pl.BlockSpec((tm, d), lambda i, j: (i, 0)),
                  pl.BlockSpec((1, d), lambda i, j: (0, 0)),
                  pl.BlockSpec((d, tn), lambda i, j: (0, j))],
        out_specs=[pl.BlockSpec((tm, tn), lambda i, j: (i, j)),
                   pl.BlockSpec((tm, d), lambda i, j: (i, 0))],
        out_shape=[jax.ShapeDtypeStruct((t, n), BF16), jax.ShapeDtypeStruct((t, d), BF16)],
        compiler_params=_cparams("parallel", "arbitrary"),
        name="inproj",
    )(x2d, gain, w)


def _log_sigmoid(x):
    return jnp.minimum(x, 0.0) - jnp.log(1.0 + jnp.exp(-jnp.abs(x)))


def _mlstm_kernel(qk_ref, v_ref, og_ref, gc_ref, gr_ref, cw_ref, gbc_ref, gbr_ref, ng_ref, y_ref,
                  xe_scr, c_scr, n_scr, m_scr, *, blk, nsub):
    heads, dh, w = MLSTM_HEADS, MLSTM_DH, MLSTM_W
    hp = lax.Precision.HIGHEST

    @pl.when(pl.program_id(1) == 0)
    def _():
        xe_scr[0:8, :] = jnp.zeros((8, 2 * w), F32)
        c_scr[...] = jnp.zeros_like(c_scr)
        n_scr[...] = jnp.zeros_like(n_scr)
        m_scr[...] = jnp.zeros_like(m_scr)

    xe_scr[8:8 + nsub * blk, :] = qk_ref[...].astype(F32)
    cw = cw_ref[...]
    ri = lax.broadcasted_iota(jnp.int32, (blk, blk), 0)
    ci = lax.broadcasted_iota(jnp.int32, (blk, blk), 1)
    causal = ri >= ci
    tril = causal.astype(F32)
    triu = (ri <= ci).astype(F32)
    state = [(c_scr[h], n_scr[h:h + 1, :], m_scr[h:h + 1, 0:1]) for h in range(heads)]
    for c in range(nsub):
        state = _mlstm_chunk(c * blk, blk, state, cw, causal, tril, triu, hp, xe_scr, v_ref, og_ref, gc_ref,
                             gr_ref, gbc_ref, gbr_ref, ng_ref, y_ref)
    xe_scr[0:8, :] = xe_scr[nsub * blk:nsub * blk + 8, :]
    for h, (c_st, n_st, m_st) in enumerate(state):
        c_scr[h] = c_st
        n_scr[h:h + 1, :] = n_st
        m_scr[h:h + 1, :] = jnp.broadcast_to(m_st, (1, m_scr.shape[1]))


def _mlstm_chunk(r0, blk, state, cw, causal, tril, triu, hp, xe_scr, v_ref, og_ref, gc_ref, gr_ref, gbc_ref,
                 gbr_ref, ng_ref, y_ref):
    heads, dh, w = MLSTM_HEADS, MLSTM_DH, MLSTM_W
    rows = slice(r0, r0 + blk)
    conv = cw[CONV_WIDTH - 1:CONV_WIDTH, :] * xe_scr[8 + r0:8 + r0 + blk, :]
    for j in range(CONV_WIDTH - 1):
        off = 8 + r0 - (CONV_WIDTH - 1) + j
        conv = conv + cw[j:j + 1, :] * xe_scr[off:off + blk, :]
    qk = conv * jax.nn.sigmoid(conv)

    gcol = gc_ref[rows, :].astype(F32) + gbc_ref[...]
    grow = gr_ref[0, :, rows] + gbr_ref[...]
    bcol = jnp.dot(tril, _log_sigmoid(gcol), precision=hp, preferred_element_type=F32)
    brow = jnp.dot(_log_sigmoid(grow), triu, precision=hp, preferred_element_type=F32)

    new_state = []
    for h in range(heads):
        sl = slice(h * dh, (h + 1) * dh)
        b_c = bcol[:, heads + h:heads + h + 1]
        i_c = gcol[:, h:h + 1]
        b_r = brow[heads + h:heads + h + 1, :]
        i_r = grow[h:h + 1, :]
        c_st, n_st, m_st = state[h]

        d_mat = jnp.where(causal, b_c - b_r + i_r, NEG)
        inter = b_c + m_st
        m_t = jnp.maximum(inter, jnp.max(d_mat, axis=-1, keepdims=True))
        w_intra = jnp.exp(d_mat - m_t)
        w_inter = jnp.exp(inter - m_t)

        q_f = qk[:, sl]
        k_f = qk[:, w + h * dh:w + (h + 1) * dh] * (dh ** -0.5)
        q_b = q_f.astype(BF16)
        k_b = k_f.astype(BF16)
        v_b = v_ref[rows, sl]

        s = _dot_nt(q_b, k_b) * w_intra
        num = _dot(s.astype(BF16), v_b) + w_inter * _dot(q_b, c_st.astype(BF16))
        den = jnp.sum(s, axis=-1, keepdims=True) + w_inter * jnp.sum(q_f * n_st, axis=-1, keepdims=True)
        hh = num / jnp.maximum(jnp.abs(den), jnp.exp(-m_t))
        hn = _rms(hh, ng_ref[:, sl])
        y_ref[rows, sl] = (hn * jax.nn.sigmoid(og_ref[rows, sl].astype(F32))).astype(y_ref.dtype)

        b_last = b_c[blk - 1:blk, :]
        dec = b_last - b_c + i_c
        m_new = jnp.maximum(b_last + m_st, jnp.max(dec, axis=0, keepdims=True))
        w_k = jnp.exp(dec - m_new)
        w_c = jnp.exp(b_last + m_st - m_new)
        kw = k_f * w_k
        new_state.append((w_c * c_st + _dot(kw.T.astype(BF16), v_b),
                          w_c * n_st + jnp.sum(kw, axis=0, keepdims=True),
                          m_new))
    return new_state


def _mlstm(proj, gates_row, conv_w, gb_col, gb_row, norm_g, *, batch, seq, blk, nsub):
    t = proj.shape[0]
    rows = blk * nsub
    nblk = seq // rows
    w = MLSTM_W
    row = lambda b, c: b * nblk + c
    return pl.pallas_call(
        functools.partial(_mlstm_kernel, blk=blk, nsub=nsub),
        grid=(batch, nblk),
        in_specs=[pl.BlockSpec((rows, 2 * w), lambda b, c: (row(b, c), OFF_MQ // (2 * w))),
                  pl.BlockSpec((rows, w), lambda b, c: (row(b, c), OFF_MV // w)),
                  pl.BlockSpec((rows, w), lambda b, c: (row(b, c), OFF_MO // w)),
                  pl.BlockSpec((rows, IF_PAD), lambda b, c: (row(b, c), OFF_IF // IF_PAD)),
                  pl.BlockSpec((1, 8, rows), lambda b, c: (b, 0, c)),
                  pl.BlockSpec((CONV_WIDTH, 2 * w), lambda b, c: (0, 0)),
                  pl.BlockSpec((1, IF_PAD), lambda b, c: (0, 0)),
                  pl.BlockSpec((8, 1), lambda b, c: (0, 0)),
                  pl.BlockSpec((1, w), lambda b, c: (0, 0))],
        out_specs=pl.BlockSpec((rows, w), lambda b, c: (row(b, c), 0)),
        out_shape=jax.ShapeDtypeStruct((t, w), BF16),
        scratch_shapes=[pltpu.VMEM((rows + 8, 2 * w), F32),
                        pltpu.VMEM((MLSTM_HEADS, MLSTM_DH, MLSTM_DH), F32),
                        pltpu.VMEM((8, MLSTM_DH), F32),
                        pltpu.VMEM((8, 128), F32)],
        compiler_params=_cparams("parallel", "arbitrary"),
        name="mlstm",
    )(proj, proj, proj, proj, gates_row, conv_w, gb_col, gb_row, norm_g)


def _attnproj_kernel(h_ref, w_ref, seg_ref, gq_ref, gk_ref, o_ref, r_scr):
    j = pl.program_id(1)
    gw, half = ATTN_GW, ATTN_SLAB // 2
    sub_rows = 512

    def head_norm(x, gain):
        sq = x * x
        hi = sq.astype(BF16)
        lo = (sq - hi.astype(F32)).astype(BF16)
        ss = _dot(hi, seg_ref[...]) + _dot(lo, seg_ref[...])
        return x * lax.rsqrt(ss * (1.0 / ATTN_DH) + EPS) * gain

    low = lax.broadcasted_iota(jnp.int32, (1, ATTN_SLAB), 1) < half
    for s in range(ATTN_TILE // sub_rows):
        rows = slice(s * sub_rows, (s + 1) * sub_rows)
        res = _dot(h_ref[rows, :], w_ref[...])
        q = head_norm(res[:, :gw], gq_ref[...]) * (ATTN_DH ** -0.5)
        k = head_norm(res[:, gw:2 * gw], gk_ref[...])
        slabs = []
        for pair in range(gw // ATTN_SLAB):
            qp = q[:, pair * ATTN_SLAB:(pair + 1) * ATTN_SLAB]
            slabs += [jnp.where(low, qp, 0.0), jnp.where(low, 0.0, qp)]
        slabs += [k[:, c * 128:(c + 1) * 128] for c in range(gw // 128)]
        slabs += [res[:, 2 * gw + c * 128:2 * gw + (c + 1) * 128] for c in range(gw // 128)]
        for c, slab in enumerate(slabs):
            r_scr[c, rows, :] = slab

    for g, (_, dil) in enumerate(ATTN_PATTERNS):
        @pl.when(j == g)
        def _(dil=dil):
            seg = ATTN_TILE // dil
            for r in range(dil):
                for c in range(r_scr.shape[0]):
                    src = r_scr[c, pl.ds(r, seg, stride=dil), :] if dil > 1 else r_scr[c]
                    o_ref[r * seg:(r + 1) * seg, c * 128:(c + 1) * 128] = src.astype(o_ref.dtype)


def _attnproj(h, w, seg_ones, gq, gk):
    t, d = h.shape
    ng = len(ATTN_PATTERNS)
    wcols = 3 * ATTN_GW
    const2 = lambda i, j: (0, 0)
    return pl.pallas_call(
        _attnproj_kernel,
        grid=(t // ATTN_TILE, ng),
        in_specs=[pl.BlockSpec((ATTN_TILE, d), lambda i, j: (i, 0)),
                  pl.BlockSpec((d, wcols), lambda i, j: (0, j)),
                  pl.BlockSpec((ATTN_GW, ATTN_GW), const2),
                  pl.BlockSpec((1, ATTN_GW), const2), pl.BlockSpec((1, ATTN_GW), const2)],
        out_specs=pl.BlockSpec((ATTN_TILE, ATTN_COLS), lambda i, j: (i, j)),
        out_shape=jax.ShapeDtypeStruct((t, ng * ATTN_COLS), BF16),
        scratch_shapes=[pltpu.VMEM((ATTN_COLS // 128, ATTN_TILE, 128), F32)],
        compiler_params=_cparams("parallel", "arbitrary"),
        name="attnproj",
    )(h, w, seg_ones, gq, gk)


def _dattn_kernel(q_ref, kc_ref, kp_ref, vc_ref, vp_ref, bias_ref, o_ref, lse_ref,
                  kx_scr, vx_scr, o_scr, l_scr, *, dil):
    blk = ATTN_BLOCK
    per = ATTN_SUB // dil
    first_tile = pl.program_id(1) == 0
    for r in range(dil):
        base = r * (per + 1) * blk
        last = slice((r * per + per - 1) * blk, (r * per + per) * blk)
        mine = slice(r * per * blk, (r + 1) * per * blk)
        kx_scr[base:base + blk, :] = kp_ref[last, :]
        vx_scr[base:base + blk, :] = vp_ref[last, :]
        kx_scr[base + blk:base + (per + 1) * blk, :] = kc_ref[mine, :]
        vx_scr[base + blk:base + (per + 1) * blk, :] = vc_ref[mine, :]

    low = lax.broadcasted_iota(jnp.int32, (1, ATTN_SLAB), 1) < ATTN_SLAB // 2
    no_prev = lax.broadcasted_iota(jnp.int32, (1, 2 * blk), 1) < blk
    for r in range(dil):
        for sub in range(per):
            u = r * per + sub
            win = slice((r * (per + 1) + sub) * blk, (r * (per + 1) + sub + 2) * blk)
            o_slabs, l_slabs = [], []
            for pair in range(ATTN_GW // ATTN_SLAB):
                cols = slice(pair * ATTN_SLAB, (pair + 1) * ATTN_SLAB)
                kx, vx = kx_scr[win, cols], vx_scr[win, cols]
                o_pair, l_pair = [], []
                for h in (2 * pair, 2 * pair + 1):
                    logits = _dot_nt(q_ref[u * blk:(u + 1) * blk, h * ATTN_SLAB:(h + 1) * ATTN_SLAB], kx)
                    logits = logits + bias_ref[h]
                    if sub == 0:
                        logits = jnp.where(first_tile & no_prev, NEG, logits)
                    m = jnp.max(logits, axis=-1, keepdims=True)
                    p = jnp.exp(logits - m)
                    l = jnp.sum(p, axis=-1, keepdims=True)
                    o_pair.append(_dot(p.astype(BF16), vx) / l)
                    l_pair.append(m + jnp.log(l))
                o_slabs.append(jnp.where(low, o_pair[0], o_pair[1]))
                l_slabs.append(jnp.where(low, l_pair[0], l_pair[1]))
            dst = pl.ds(sub * blk * dil + r, blk, stride=dil) if dil > 1 else slice(u * blk, (u + 1) * blk)
            for c in range(ATTN_GW // ATTN_SLAB):
                o_scr[c, dst, :] = o_slabs[c]
                l_scr[c, dst, :] = l_slabs[c]
    for c in range(ATTN_GW // ATTN_SLAB):
        o_ref[:, c * ATTN_SLAB:(c + 1) * ATTN_SLAB] = o_scr[c].astype(o_ref.dtype)
        lse_ref[:, c * ATTN_SLAB:(c + 1) * ATTN_SLAB] = l_scr[c]


def _dattn(aproj, bias, *, seq, group, dilation):
    t = aproj.shape[0]
    tiles = seq // ATTN_TILE
    qw = HEADS_PER_GROUP * ATTN_SLAB
    cq = group * ATTN_COLS // qw
    ck, cv = (group * ATTN_COLS + qw) // ATTN_GW, (group * ATTN_COLS + qw) // ATTN_GW + 1
    blk = (ATTN_TILE, ATTN_GW)
    cur = lambda c: (lambda b, j: (b * tiles + j, c))
    prev = lambda c: (lambda b, j: (b * tiles + jnp.maximum(j - 1, 0), c))
    xrows = ATTN_TILE + dilation * ATTN_BLOCK
    return pl.pallas_call(
        functools.partial(_dattn_kernel, dil=dilation),
        grid=(t // seq, tiles),
        in_specs=[pl.BlockSpec((ATTN_TILE, qw), cur(cq)),
                  pl.BlockSpec(blk, cur(ck)), pl.BlockSpec(blk, prev(ck)),
                  pl.BlockSpec(blk, cur(cv)), pl.BlockSpec(blk, prev(cv)),
                  pl.BlockSpec((HEADS_PER_GROUP, ATTN_BLOCK, 2 * ATTN_BLOCK), lambda b, j: (0, 0, 0))],
        out_specs=[pl.BlockSpec(blk, cur(0)), pl.BlockSpec(blk, cur(0))],
        out_shape=[jax.ShapeDtypeStruct((t, ATTN_GW), BF16), jax.ShapeDtypeStruct((t, ATTN_GW), F32)],
        scratch_shapes=[pltpu.VMEM((xrows, ATTN_GW), BF16), pltpu.VMEM((xrows, ATTN_GW), BF16),
                        pltpu.VMEM((ATTN_GW // ATTN_SLAB, ATTN_TILE, ATTN_SLAB), F32),
                        pltpu.VMEM((ATTN_GW // ATTN_SLAB, ATTN_TILE, ATTN_SLAB), F32)],
        compiler_params=_cparams("parallel", "arbitrary"),
        name=f"dattn{group}",
    )(aproj, aproj, aproj, aproj, aproj, bias)


def _rel_bucket(n):
    max_exact = REL_BUCKETS // 2
    nf = jnp.maximum(n, 1).astype(F32)
    log_b = max_exact + (jnp.log(nf / max_exact) / math.log(REL_MAX_DIST / max_exact)
                         * (REL_BUCKETS - max_exact)).astype(jnp.int32)
    return jnp.where(n < max_exact, n, jnp.minimum(log_b, REL_BUCKETS - 1))


def _attn_bias(rel_bias, group):
    window, dilation = ATTN_PATTERNS[group]
    steps = window // dilation
    hp = lax.Precision.HIGHEST
    hs = slice(group * HEADS_PER_GROUP, (group + 1) * HEADS_PER_GROUP)
    bucket = _rel_bucket(jnp.arange(steps + 1) * dilation)
    bias_steps = jnp.dot(jax.nn.one_hot(bucket, REL_BUCKETS, dtype=F32), rel_bias[:, hs].astype(F32),
                         precision=hp)
    qi = jnp.arange(ATTN_BLOCK)[:, None]
    ki = jnp.arange(2 * ATTN_BLOCK)[None, :]
    dist = ATTN_BLOCK + qi - ki
    ok = (dist >= 0) & (dist <= steps)
    sel = jax.nn.one_hot(jnp.clip(dist, 0, steps).reshape(-1), steps + 1, dtype=F32)
    bias = jnp.dot(sel, bias_steps, precision=hp).T.reshape(HEADS_PER_GROUP, ATTN_BLOCK, 2 * ATTN_BLOCK)
    return jnp.where(ok[None], bias, NEG)


def _merge_kernel(ya_ref, yb0_ref, yb1_ref, yb2_ref, l0_ref, l1_ref, l2_ref, gu_ref, gv_ref, gate_ref,
                  x_ref, wa_ref, wb_ref, wc_ref, wo_ref, ws_ref, bs_ref, gg_ref, o_ref, yc_scr, *, tm):
    d = x_ref.shape[1]
    l0, l1, l2 = l0_ref[...], l1_ref[...], l2_ref[...]
    mx = jnp.maximum(jnp.maximum(l0, l1), l2)
    e0, e1, e2 = jnp.exp(l0 - mx), jnp.exp(l1 - mx), jnp.exp(l2 - mx)
    inv = 1.0 / (e0 + e1 + e2)
    yb = jnp.concatenate([(yb0_ref[...].astype(F32) * (e0 * inv)).astype(BF16),
                          (yb1_ref[...].astype(F32) * (e1 * inv)).astype(BF16),
                          (yb2_ref[...].astype(F32) * (e2 * inv)).astype(BF16)], axis=-1)

    for j in range(tm // GMLP_CHUNK):
        rows = slice(j * GMLP_CHUNK, (j + 1) * GMLP_CHUNK)
        for g in range(GMLP_GROUPS):
            cols = slice(g * GMLP_GC, (g + 1) * GMLP_GC)
            u = jax.nn.gelu(gu_ref[rows, cols].astype(F32))
            v = _rms(jax.nn.gelu(gv_ref[rows, cols].astype(F32)), gg_ref[:, cols])
            mixed = _dot(ws_ref[g], v.astype(BF16)) + bs_ref[g]
            yc_scr[rows, cols] = (u * mixed).astype(BF16)

    merged = jax.nn.sigmoid(gate_ref[:, 0:d].astype(F32)) * _dot(ya_ref[...], wa_ref[...])
    merged = merged + jax.nn.sigmoid(gate_ref[:, d:2 * d].astype(F32)) * _dot(yb, wb_ref[...])
    merged = merged + jax.nn.sigmoid(gate_ref[:, 2 * d:3 * d].astype(F32)) * _dot(yc_scr[...], wc_ref[...])
    o_ref[...] = x_ref[...] + _dot(merged.astype(BF16), wo_ref[...])


def _merge(ya, ybs, lses, proj, x2d, wa, wb, wc, wo, ws, bsb, gg, *, tm):
    t, d = x2d.shape
    row = lambda c: (lambda i: (i, c))
    full2 = lambda i: (0, 0)
    full3 = lambda i: (0, 0, 0)
    gspec = pl.BlockSpec((tm, ATTN_GW), row(0))
    return pl.pallas_call(
        functools.partial(_merge_kernel, tm=tm),
        grid=(t // tm,),
        in_specs=[pl.BlockSpec((tm, MLSTM_W), row(0)),
                  gspec, gspec, gspec, gspec, gspec, gspec,
                  pl.BlockSpec((tm, GMLP_W), row(OFF_GU // GMLP_W)),
                  pl.BlockSpec((tm, GMLP_W), row(OFF_GV // GMLP_W)),
                  pl.BlockSpec((tm, N_BRANCH * d), row(OFF_GATE // (N_BRANCH * d))),
                  pl.BlockSpec((tm, d), row(0)),
                  pl.BlockSpec(wa.shape, full2), pl.BlockSpec(wb.shape, full2),
                  pl.BlockSpec(wc.shape, full2), pl.BlockSpec(wo.shape, full2),
                  pl.BlockSpec(ws.shape, full3), pl.BlockSpec(bsb.shape, full3),
                  pl.BlockSpec(gg.shape, full2)],
        out_specs=pl.BlockSpec((tm, d), row(0)),
        out_shape=jax.ShapeDtypeStruct((t, d), F32),
        scratch_shapes=[pltpu.VMEM((tm, GMLP_W), BF16)],
        compiler_params=_cparams("parallel"),
        name="merge",
    )(ya, *ybs, *lses, proj, proj, proj, x2d, wa, wb, wc, wo, ws, bsb, gg)


def _memkv_kernel(mem_ref, g_ref, w_ref, gk_ref, k_ref, v_ref):
    dh, w = XATTN_DH, XATTN_W
    kv = _dot(_rms(mem_ref[0], g_ref[...]).astype(BF16), w_ref[...])
    for h in range(XATTN_HEADS):
        sl = slice(h * dh, (h + 1) * dh)
        k_ref[0, :, sl] = _rms(kv[:, sl], gk_ref[...]).astype(k_ref.dtype)
    v_ref[0] = kv[:, w:].astype(v_ref.dtype)


def _memkv(mem, gain, w_kv, gk):
    b, m, d = mem.shape
    full2 = lambda i: (0, 0)
    return pl.pallas_call(
        _memkv_kernel,
        grid=(b,),
        in_specs=[pl.BlockSpec((1, m, d), lambda i: (i, 0, 0)),
                  pl.BlockSpec((1, d), full2),
                  pl.BlockSpec(w_kv.shape, full2),
                  pl.BlockSpec((1, XATTN_DH), full2)],
        out_specs=[pl.BlockSpec((1, m, XATTN_W), lambda i: (i, 0, 0)),
                   pl.BlockSpec((1, m, XATTN_W), lambda i: (i, 0, 0))],
        out_shape=[jax.ShapeDtypeStruct((b, m, XATTN_W), BF16),
                   jax.ShapeDtypeStruct((b, m, XATTN_W), BF16)],
        compiler_params=_cparams("parallel"),
        name="memkv",
    )(mem, gain, w_kv, gk)


def _route(logits):
    tm = logits.shape[1]
    e = jnp.exp(logits - jnp.max(logits, axis=0, keepdims=True))
    probs = e / jnp.sum(e, axis=0, keepdims=True)
    rowi = lax.broadcasted_iota(jnp.int32, (8, tm), 0)
    real = rowi < EXPERTS_PER_GROUP
    tops = []
    for g in range(N_EXPERT_GROUPS):
        pg = jnp.where(real, probs[8 * g:8 * g + 8, :], -0.5)
        m1 = jnp.max(pg, axis=0, keepdims=True)
        i1 = jnp.min(jnp.where(pg == m1, rowi, 8), axis=0, keepdims=True)
        pg2 = jnp.where(rowi == i1, -1.0, pg)
        m2 = jnp.max(pg2, axis=0, keepdims=True)
        i2 = jnp.min(jnp.where(pg2 == m2, rowi, 8), axis=0, keepdims=True)
        tops.append((m1, i1, m2, i2))
    best = jnp.zeros((1, tm), jnp.int32)
    best_score = tops[0][0] + tops[0][2]
    for g in range(1, N_EXPERT_GROUPS):
        score = tops[g][0] + tops[g][2]
        better = score > best_score
        best = jnp.where(better, g, best)
        best_score = jnp.where(better, score, best_score)
    m1, i1, m2, i2 = tops[0]
    for g in range(1, N_EXPERT_GROUPS):
        m1, i1, m2, i2 = (jnp.where(best == g, new, old) for new, old in zip(tops[g], (m1, i1, m2, i2)))
    tot = m1 + m2
    base = best * EXPERTS_PER_GROUP
    return base + i1, base + i2, m1 / tot, m2 / tot


def _pack_bf16_pairs(x):
    n = x.shape[1] // 2
    hi = lax.bitcast_convert_type(x[:, :n].astype(BF16).astype(F32), jnp.uint32)
    lo = lax.bitcast_convert_type(x[:, n:].astype(BF16).astype(F32), jnp.uint32)
    return hi | (lo >> 16)


def _unpack_bf16_pairs(p):
    hi = lax.bitcast_convert_type(p & jnp.uint32(0xFFFF0000), F32)
    lo = lax.bitcast_convert_type(p << 16, F32)
    return hi, lo


def _store_row_chunks(ref, packed):
    for j in range(ROW_CHUNKS):
        ref[j] = packed[:, j * 128:(j + 1) * 128]


def _load_row_chunks(ref):
    return jnp.concatenate([ref[j] for j in range(ROW_CHUNKS)], axis=-1)


def _xattn_kernel(x_ref, k_ref, v_ref, gx_ref, wq_ref, gq_ref, wo_ref, gf_ref, rw_ref, rb_ref,
                  xo_ref, hf_ref, eidx_ref, wts_ref):
    dh = XATTN_DH
    x = x_ref[...]
    q = _dot(_rms(x, gx_ref[...]).astype(BF16), wq_ref[...])
    outs = []
    for h in range(XATTN_HEADS):
        sl = slice(h * dh, (h + 1) * dh)
        q_h = (_rms(q[:, sl], gq_ref[...]) * (dh ** -0.5)).astype(BF16)
        logits = _dot_nt(q_h, k_ref[0, :, sl])
        p = jnp.exp(logits - jnp.max(logits, axis=-1, keepdims=True))
        o = _dot(p.astype(BF16), v_ref[0, :, sl]) / jnp.sum(p, axis=-1, keepdims=True)
        outs.append(o.astype(BF16))
    xn = x + _dot(jnp.concatenate(outs, axis=-1), wo_ref[...])
    xo_ref[...] = xn
    hf = _rms(xn, gf_ref[...])
    _store_row_chunks(hf_ref, _pack_bf16_pairs(hf))
    logits_t = lax.dot_general(rw_ref[...], hf, (((1,), (1,)), ((), ())),
                               precision=lax.Precision.HIGHEST, preferred_element_type=F32) + rb_ref[...]
    e1, e2, w1, w2 = _route(logits_t)
    tm = x.shape[0]
    eidx_ref[...] = jnp.concatenate([e1, e2, jnp.zeros((6, tm), jnp.int32)], axis=0)
    wts_ref[...] = jnp.concatenate([w1, w2, jnp.zeros((6, tm), F32)], axis=0)


def _xattn(x2d, k, v, gx, wq, gq, wo, gf, rw_t, rb, *, seq, tm):
    t, d = x2d.shape
    per_b = seq // tm
    full2 = lambda i: (0, 0)
    kv_spec = pl.BlockSpec((1,) + k.shape[1:], lambda i: (i // per_b, 0, 0))
    return pl.pallas_call(
        _xattn_kernel,
        grid=(t // tm,),
        in_specs=[pl.BlockSpec((tm, d), lambda i: (i, 0)), kv_spec, kv_spec,
                  pl.BlockSpec((1, d), full2), pl.BlockSpec(wq.shape, full2),
                  pl.BlockSpec((1, XATTN_DH), full2), pl.BlockSpec(wo.shape, full2),
                  pl.BlockSpec((1, d), full2), pl.BlockSpec(rw_t.shape, full2),
                  pl.BlockSpec(rb.shape, full2)],
        out_specs=[pl.BlockSpec((tm, d), lambda i: (i, 0)),
                   pl.BlockSpec((ROW_CHUNKS, tm, 128), lambda i: (0, i, 0)),
                   pl.BlockSpec((8, tm), lambda i: (0, i)),
                   pl.BlockSpec((8, tm), lambda i: (0, i))],
        out_shape=[jax.ShapeDtypeStruct((t, d), F32),
                   jax.ShapeDtypeStruct((ROW_CHUNKS, t, 128), jnp.uint32),
                   jax.ShapeDtypeStruct((8, t), jnp.int32),
                   jax.ShapeDtypeStruct((8, t), F32)],
        compiler_params=_cparams("parallel"),
        name="xattn_router",
    )(x2d, k, v, gx, wq, gq, wo, gf, rw_t, rb)


def _moe_plan_kernel(eidx_ref, i1_ref, i2_ref, te_ref, na_ref, cnt_scr, carry_scr, *, tb, tm, plane_rows):
    ne = N_EXPERTS
    hp = lax.Precision.HIGHEST
    phase, j = pl.program_id(0), pl.program_id(1)
    rows = lax.broadcasted_iota(jnp.int32, (ne, tb), 0)
    oh1 = rows == eidx_ref[0:1, :]
    oh2 = rows == eidx_ref[1:2, :]
    a = oh1.astype(F32) + oh2.astype(F32)
    blk_cnt = jnp.broadcast_to(jnp.sum(a, axis=1, keepdims=True), cnt_scr.shape)

    @pl.when((phase == 0) & (j == 0))
    def _():
        cnt_scr[...] = jnp.zeros_like(cnt_scr)

    @pl.when(phase == 0)
    def _():
        cnt_scr[...] += blk_cnt

    @pl.when((phase == 1) & (j == 0))
    def _():
        padded = jnp.ceil(cnt_scr[...] * (1.0 / tm)) * tm
        er = lax.broadcasted_iota(jnp.int32, (ne, ne), 0)
        ec = lax.broadcasted_iota(jnp.int32, (ne, ne), 1)
        off = jnp.dot((ec < er).astype(F32), padded, precision=hp, preferred_element_type=F32)
        carry_scr[...] = off
        seg_end = (off + padded)[:, 0:1]
        tile_start = lax.broadcasted_iota(jnp.int32, (ne, te_ref.shape[1]), 1).astype(F32) * tm
        te = jnp.sum((seg_end <= tile_start).astype(F32), axis=0, keepdims=True)
        te_ref[...] = jnp.broadcast_to(jnp.minimum(te, ne - 1.0), te_ref.shape).astype(jnp.int32)
        total = jnp.sum(padded[:, 0:1], axis=0, keepdims=True)
        na_ref[...] = jnp.broadcast_to(total * (1.0 / tm), na_ref.shape).astype(jnp.int32)

    @pl.when(phase == 1)
    def _():
        before = (lax.broadcasted_iota(jnp.int32, (tb, tb), 0)
                  < lax.broadcasted_iota(jnp.int32, (tb, tb), 1)).astype(BF16)
        rank = carry_scr[:, 0:1] + _dot(a.astype(BF16), before)
        d1 = jnp.sum(jnp.where(oh1, rank, 0.0), axis=0, keepdims=True).astype(jnp.int32)
        d2 = jnp.sum(jnp.where(oh2, rank, 0.0), axis=0, keepdims=True).astype(jnp.int32)
        plane = lax.broadcasted_iota(jnp.int32, (8, tb), 0) * plane_rows
        i1_ref[...] = jnp.where(plane < ROW_CHUNKS * plane_rows, plane + d1, 0)
        i2_ref[...] = jnp.where(plane < ROW_CHUNKS * plane_rows, plane + d2, 0)
        carry_scr[...] += blk_cnt


def _moe_plan(eidx, *, tm, n_tiles, tb=512):
    t = eidx.shape[1]
    ntp = -(-n_tiles // 128) * 128
    return pl.pallas_call(
        functools.partial(_moe_plan_kernel, tb=tb, tm=tm, plane_rows=n_tiles * tm),
        grid=(2, t // tb),
        in_specs=[pl.BlockSpec((8, tb), lambda p, j: (0, j))],
        out_specs=[pl.BlockSpec((8, tb), lambda p, j: (0, j * p)),
                   pl.BlockSpec((8, tb), lambda p, j: (0, j * p)),
                   pl.BlockSpec((8, ntp), lambda p, j: (0, 0)),
                   pl.BlockSpec((8, 128), lambda p, j: (0, 0))],
        out_shape=[jax.ShapeDtypeStruct((8, t), jnp.int32),
                   jax.ShapeDtypeStruct((8, t), jnp.int32),
                   jax.ShapeDtypeStruct((8, ntp), jnp.int32),
                   jax.ShapeDtypeStruct((8, 128), jnp.int32)],
        scratch_shapes=[pltpu.VMEM((N_EXPERTS, 128), F32), pltpu.VMEM((N_EXPERTS, 128), F32)],
        compiler_params=_cparams("arbitrary", "arbitrary"),
        name="moe_plan",
    )(eidx)


def _sc_mesh():
    return plsc.VectorSubcoreMesh(core_axis_name="c", subcore_axis_name="s",
                                  num_cores=SC_CORES, num_subcores=SC_SUBCORES)


def _sc_index_spec(tokens):
    nb = tokens // SC_WINDOW
    return pl.BlockSpec((1, SC_WINDOW), lambda i: (i // nb, i % nb))


def _sc_dispatch(rows, i1, i2, n_out):
    n = rows.shape[0]
    tokens = i1.shape[1]

    @functools.partial(pl.kernel, out_type=jax.ShapeDtypeStruct((n_out, 128), rows.dtype), mesh=_sc_mesh(),
                       name="moe_dispatch")
    def k(x_hbm, i1_hbm, i2_hbm, o_hbm):
        def body(x_vmem, i1_vmem, i2_vmem):
            pltpu.sync_copy(x_vmem, o_hbm.at[i1_vmem.at[0]])
            pltpu.sync_copy(x_vmem, o_hbm.at[i2_vmem.at[0]])

        pltpu.emit_pipeline(
            body, grid=(n // SC_WINDOW,),
            in_specs=[pl.BlockSpec((SC_WINDOW, 128), lambda i: (i, 0)),
                      _sc_index_spec(tokens), _sc_index_spec(tokens)],
            out_specs=[],
            core_axis_name=("c", "s"), dimension_semantics=(pltpu.PARALLEL,),
        )(x_hbm, i1_hbm, i2_hbm)

    return k(rows, i1, i2)


def _sc_collect(table, i1, i2):
    tokens = i1.shape[1]
    n = ROW_CHUNKS * tokens
    out = jax.ShapeDtypeStruct((n, 128), table.dtype)

    @functools.partial(pl.kernel, out_type=(out, out), mesh=_sc_mesh(), name="moe_collect")
    def k(t_hbm, i1_hbm, i2_hbm, o1_hbm, o2_hbm):
        def body(i1_vmem, i2_vmem, o1_vmem, o2_vmem):
            pltpu.sync_copy(t_hbm.at[i1_vmem.at[0]], o1_vmem)
            pltpu.sync_copy(t_hbm.at[i2_vmem.at[0]], o2_vmem)

        pltpu.emit_pipeline(
            body, grid=(n // SC_WINDOW,),
            in_specs=[_sc_index_spec(tokens), _sc_index_spec(tokens)],
            out_specs=[pl.BlockSpec((SC_WINDOW, 128), lambda i: (i, 0)),
                       pl.BlockSpec((SC_WINDOW, 128), lambda i: (i, 0))],
            core_axis_name=("c", "s"), dimension_semantics=(pltpu.PARALLEL,),
        )(i1_hbm, i2_hbm, o1_hbm, o2_hbm)

    return k(table, i1, i2)


def _experts_kernel(te_ref, na_ref, xs_ref, wg_ref, wu_ref, wd_ref, y_ref, wg_scr, wu_scr, wd_scr):
    i = pl.program_id(0)
    active = i < na_ref[0]

    @pl.when(active & ((i == 0) | (te_ref[i] != te_ref[jnp.maximum(i - 1, 0)])))
    def _():
        wg_scr[...] = wg_ref[0, 0].astype(BF16)
        wu_scr[...] = wu_ref[0, 0].astype(BF16)
        wd_scr[...] = wd_ref[0, 0].astype(BF16)

    @pl.when(active)
    def _():
        hi, lo = _unpack_bf16_pairs(_load_row_chunks(xs_ref))
        h = jnp.concatenate([hi, lo], axis=-1).astype(BF16)
        up = _dot(h, wg_scr[...])
        act = up * jax.nn.sigmoid(up) * _dot(h, wu_scr[...])
        _store_row_chunks(y_ref, _pack_bf16_pairs(_dot(act.astype(BF16), wd_scr[...])))


def _experts(tile_expert, n_active, xs, wg, wu, wd, *, layer, tm):
    n_tiles = tile_expert.shape[0]
    _, _, d, dff = wg.shape
    rows = lambda i, te, na: (0, jnp.minimum(i, na[0] - 1), 0)
    expert = lambda i, te, na: (layer, te[i], 0, 0)
    return pl.pallas_call(
        _experts_kernel,
        grid_spec=pltpu.PrefetchScalarGridSpec(
            num_scalar_prefetch=2,
            grid=(n_tiles,),
            in_specs=[pl.BlockSpec((ROW_CHUNKS, tm, 128), rows),
                      pl.BlockSpec((1, 1, d, dff), expert),
                      pl.BlockSpec((1, 1, d, dff), expert),
                      pl.BlockSpec((1, 1, dff, d), expert)],
            out_specs=pl.BlockSpec((ROW_CHUNKS, tm, 128), rows),
            scratch_shapes=[pltpu.VMEM((d, dff), BF16), pltpu.VMEM((d, dff), BF16), pltpu.VMEM((dff, d), BF16)]),
        out_shape=jax.ShapeDtypeStruct(xs.shape, xs.dtype),
        compiler_params=_cparams("arbitrary"),
        name="moe_experts",
    )(tile_expert, n_active, xs, wg, wu, wd)


def _moe_combine_kernel(x_ref, y1_ref, y2_ref, w_ref, o_ref):
    half = x_ref.shape[1] // 2
    hi1, lo1 = _unpack_bf16_pairs(_load_row_chunks(y1_ref))
    hi2, lo2 = _unpack_bf16_pairs(_load_row_chunks(y2_ref))
    w1, w2 = w_ref[:, 0:1], w_ref[:, 1:2]
    o_ref[:, :half] = x_ref[:, :half] + w1 * hi1 + w2 * hi2
    o_ref[:, half:] = x_ref[:, half:] + w1 * lo1 + w2 * lo2


def _moe_combine(x2d, y1, y2, wcol, *, tm):
    t, d = x2d.shape
    chunk_spec = pl.BlockSpec((ROW_CHUNKS, tm, 128), lambda i: (0, i, 0))
    return pl.pallas_call(
        _moe_combine_kernel,
        grid=(t // tm,),
        in_specs=[pl.BlockSpec((tm, d), lambda i: (i, 0)), chunk_spec, chunk_spec,
                  pl.BlockSpec((tm, wcol.shape[1]), lambda i: (i, 0))],
        out_specs=pl.BlockSpec((tm, d), lambda i: (i, 0)),
        out_shape=jax.ShapeDtypeStruct((t, d), F32),
        compiler_params=_cparams("parallel"),
        name="moe_combine",
    )(x2d, y1, y2, wcol)


def _moe(x2d, hf_rows, eidx, wts, wg, wu, wd, *, layer):
    t = x2d.shape[0]
    tm = MOE_TM
    n_tiles = 2 * t // tm + N_EXPERTS
    plane = n_tiles * tm
    i1, i2, te, na = _moe_plan(eidx, tm=tm, n_tiles=n_tiles)
    xs = _sc_dispatch(hf_rows.reshape(ROW_CHUNKS * t, 128), i1, i2, ROW_CHUNKS * plane)
    ys = _experts(te[0, :n_tiles], na[0, :1], xs.reshape(ROW_CHUNKS, plane, 128), wg, wu, wd,
                  layer=layer, tm=tm)
    y1, y2 = _sc_collect(ys.reshape(ROW_CHUNKS * plane, 128), i1, i2)
    return _moe_combine(x2d, y1.reshape(ROW_CHUNKS, t, 128), y2.reshape(ROW_CHUNKS, t, 128), wts[:2].T, tm=512)


def _layout_w_in(w):
    sizes = (MLSTM_W, MLSTM_W, MLSTM_W, MLSTM_W, MLSTM_HEADS, MLSTM_HEADS,
             ATTN_W, ATTN_W, ATTN_W, GMLP_W, GMLP_W, N_BRANCH * w.shape[0])
    pts = np.cumsum(sizes)[:-1]
    mq, mk, mv, mo, mi, mf, aq, ak, av, gu, gv, gate = jnp.split(w, pts, axis=-1)
    pad = jnp.zeros((w.shape[0], IF_PAD - 2 * MLSTM_HEADS), w.dtype)
    main = jnp.concatenate([mq, mk, mv, mo, gu, gv, gate, mi, mf, pad], axis=-1).astype(BF16)
    attn = jnp.concatenate([a[:, g * ATTN_GW:(g + 1) * ATTN_GW] for g in range(len(ATTN_PATTERNS))
                            for a in (aq, ak, av)], axis=-1).astype(BF16)
    return main, attn


def kernel(x, mem, norm_mix, w_in, mlstm_conv, mlstm_gate_b, mlstm_norm, attn_qk_norm, gmlp_norm, gmlp_ws,
           gmlp_bs, w_branch_a, w_branch_b, w_branch_c, w_out, rel_bias, norm_xattn, norm_mem, w_xq, w_xkv,
           xattn_qk_norm, w_xo, norm_ffn, router_w, router_b, w_expert_gate, w_expert_up, w_expert_down):
    b, s, d = x.shape
    t = b * s
    depth = w_in.shape[0]
    x2d = x.reshape(t, d)

    biases = [_attn_bias(rel_bias, g) for g in range(len(ATTN_PATTERNS))]
    rw_t = jnp.zeros((N_EXPERT_GROUPS, 8, d), F32).at[:, :EXPERTS_PER_GROUP].set(
        router_w.T.reshape(N_EXPERT_GROUPS, EXPERTS_PER_GROUP, d)).reshape(ROUTER_ROWS, d)
    rb = jnp.full((N_EXPERT_GROUPS, 8), NEG, F32).at[:, :EXPERTS_PER_GROUP].set(
        router_b.astype(F32).reshape(N_EXPERT_GROUPS, EXPERTS_PER_GROUP)).reshape(ROUTER_ROWS, 1)
    tril = jnp.tril(jnp.ones((GMLP_CHUNK, GMLP_CHUNK), bool))
    head_of = jnp.arange(ATTN_GW) // ATTN_DH
    seg_ones = (head_of[:, None] == head_of[None, :]).astype(BF16)

    for l in range(depth):
        w_main, w_attn = _layout_w_in(w_in[l])
        proj, h_mix = _inproj(x2d, norm_mix[l][None], w_main, tm=1024, tn=1280)
        gq = jnp.tile(attn_qk_norm[l, 0], HEADS_PER_GROUP)[None]
        gk = jnp.tile(attn_qk_norm[l, 1], HEADS_PER_GROUP)[None]
        aproj = _attnproj(h_mix, w_attn, seg_ones, gq, gk)

        gates_row = proj[:, OFF_IF:OFF_IF + 8].astype(F32).reshape(b, s, 8).transpose(0, 2, 1)
        gb_col = jnp.zeros((1, IF_PAD), F32).at[0, :8].set(mlstm_gate_b[l])
        ya = _mlstm(proj, gates_row, mlstm_conv[l], gb_col, mlstm_gate_b[l].reshape(8, 1),
                    mlstm_norm[l][None], batch=b, seq=s, blk=MLSTM_BLOCK, nsub=MLSTM_NSUB)

        ybs, lses = [], []
        for g, (_, dilation) in enumerate(ATTN_PATTERNS):
            o, lse = _dattn(aproj, biases[g], seq=s, group=g, dilation=dilation)
            ybs.append(o)
            lses.append(lse)

        ws = jnp.where(tril, gmlp_ws[l], 0.0).astype(BF16)
        bsb = jnp.broadcast_to(gmlp_bs[l][:, :, None], (GMLP_GROUPS, GMLP_CHUNK, GMLP_GC)).astype(F32)
        x2d = _merge(ya, ybs, lses, proj, x2d, w_branch_a[l].astype(BF16), w_branch_b[l].astype(BF16),
                     w_branch_c[l].astype(BF16), w_out[l].astype(BF16), ws, bsb, gmlp_norm[l][None], tm=256)

        k_mem, v_mem = _memkv(mem, norm_mem[l][None], w_xkv[l].astype(BF16), xattn_qk_norm[l, 1][None])
        x2d, hf_rows, eidx, wts = _xattn(x2d, k_mem, v_mem, norm_xattn[l][None], w_xq[l].astype(BF16),
                                         xattn_qk_norm[l, 0][None], w_xo[l].astype(BF16), norm_ffn[l][None],
                                         rw_t, rb, seq=s, tm=512)

        x2d = _moe(x2d, hf_rows, eidx, wts, w_expert_gate, w_expert_up, w_expert_down, layer=l)

    return x2d.reshape(b, s, d)
```

```python
import functools
import math

import jax
import jax.numpy as jnp
import numpy as np
from jax import lax
from jax.experimental import pallas as pl
from jax.experimental.pallas import tpu as pltpu
from jax.experimental.pallas import tpu_sc as plsc

F32 = jnp.float32
BF16 = jnp.bfloat16

EPS = 1e-6
NEG = -1e30

MLSTM_HEADS = 4
MLSTM_DH = 128
MLSTM_W = MLSTM_HEADS * MLSTM_DH
CONV_WIDTH = 4
MLSTM_BLOCK = 128
MLSTM_NSUB = 1

ATTN_PATTERNS = ((128, 1), (512, 4), (2048, 16))
assert ATTN_PATTERNS[0][1] == 1
HEADS_PER_GROUP = 4
ATTN_DH = 64
ATTN_GW = HEADS_PER_GROUP * ATTN_DH
ATTN_W = len(ATTN_PATTERNS) * ATTN_GW
ATTN_BLOCK = 128
REL_BUCKETS = 32
REL_MAX_DIST = 2048

GMLP_GROUPS = 4
GMLP_GC = 128
GMLP_W = GMLP_GROUPS * GMLP_GC
GMLP_CHUNK = 128

XATTN_HEADS = 4
XATTN_DH = 128
XATTN_W = XATTN_HEADS * XATTN_DH

N_EXPERTS = 16
N_EXPERT_GROUPS = 4
EXPERTS_PER_GROUP = 4
ROUTER_ROWS = 8 * N_EXPERT_GROUPS

N_BRANCH = 3

MOE_TM = 512
ROW_CHUNKS = 4
SC_CORES, SC_SUBCORES = 2, 16
SC_WINDOW = 128

OFF_MQ, OFF_MK, OFF_MV, OFF_MO = 0, 512, 1024, 1536
OFF_GU, OFF_GV = 2048, 2560
OFF_GATE = 3072
OFF_IF = 6144
IF_PAD = 256
N_PROJ = OFF_IF + IF_PAD

ATTN_TILE = 2048
ATTN_SUB = ATTN_TILE // ATTN_BLOCK
ATTN_SLAB = 2 * ATTN_DH
ATTN_COLS = HEADS_PER_GROUP * ATTN_SLAB + 2 * ATTN_GW

VMEM_LIMIT = 48 * 1024 * 1024


def _cparams(*sem):
    return pltpu.CompilerParams(dimension_semantics=sem, vmem_limit_bytes=VMEM_LIMIT)


def _rms(x, gain):
    return x * lax.rsqrt(jnp.mean(x * x, axis=-1, keepdims=True) + EPS) * gain


def _dot(a, b):
    return jnp.dot(a, b, preferred_element_type=F32)


def _dot_nt(a, b):
    return lax.dot_general(a, b, (((1,), (1,)), ((), ())), preferred_element_type=F32)


def _inproj_kernel(x_ref, g_ref, w_ref, o_ref, h_ref):
    @pl.when(pl.program_id(1) == 0)
    def _():
        h_ref[...] = _rms(x_ref[...], g_ref[...]).astype(BF16)

    o_ref[...] = _dot(h_ref[...], w_ref[...]).astype(o_ref.dtype)


def _inproj(x2d, gain, w, *, tm, tn):
    t, d = x2d.shape
    n = w.shape[1]
    return pl.pallas_call(
        _inproj_kernel,
        grid=(t // tm, n // tn),
        in_specs=[pl.BlockSpec((tm, d), lambda i, j: (i, 0)),
                  pl.BlockSpec((1, d), lambda i, j: (0, 0)),
                  pl.BlockSpec((d, tn), lambda i, j: (0, j))],
        out_specs=[pl.BlockSpec((tm, tn), lambda i, j: (i, j)),
                   pl.BlockSpec((tm, d), lambda i, j: (i, 0))],
        out_shape=[jax.ShapeDtypeStruct((t, n), BF16), jax.ShapeDtypeStruct((t, d), BF16)],
        compiler_params=_cparams("parallel", "arbitrary"),
        name="inproj",
    )(x2d, gain, w)


def _log_sigmoid(x):
    return jnp.minimum(x, 0.0) - jnp.log(1.0 + jnp.exp(-jnp.abs(x)))


def _mlstm_kernel(qk_ref, v_ref, og_ref, gc_ref, gr_ref, cw_ref, gbc_ref, gbr_ref, ng_ref, y_ref,
                  xe_scr, c_scr, n_scr, m_scr, *, blk, nsub):
    heads, dh, w = MLSTM_HEADS, MLSTM_DH, MLSTM_W
    hp = lax.Precision.HIGHEST

    @pl.when(pl.program_id(1) == 0)
    def _():
        xe_scr[0:8, :] = jnp.zeros((8, 2 * w), F32)
        c_scr[...] = jnp.zeros_like(c_scr)
        n_scr[...] = jnp.zeros_like(n_scr)
        m_scr[...] = jnp.zeros_like(m_scr)

    xe_scr[8:8 + nsub * blk, :] = qk_ref[...].astype(F32)
    cw = cw_ref[...]
    ri = lax.broadcasted_iota(jnp.int32, (blk, blk), 0)
    ci = lax.broadcasted_iota(jnp.int32, (blk, blk), 1)
    causal = ri >= ci
    tril = causal.astype(F32)
    triu = (ri <= ci).astype(F32)
    state = [(c_scr[h], n_scr[h:h + 1, :], m_scr[h:h + 1, 0:1]) for h in range(heads)]
    for c in range(nsub):
        state = _mlstm_chunk(c * blk, blk, state, cw, causal, tril, triu, hp, xe_scr, v_ref, og_ref, gc_ref,
                             gr_ref, gbc_ref, gbr_ref, ng_ref, y_ref)
    xe_scr[0:8, :] = xe_scr[nsub * blk:nsub * blk + 8, :]
    for h, (c_st, n_st, m_st) in enumerate(state):
        c_scr[h] = c_st
        n_scr[h:h + 1, :] = n_st
        m_scr[h:h + 1, :] = jnp.broadcast_to(m_st, (1, m_scr.shape[1]))


def _mlstm_chunk(r0, blk, state, cw, causal, tril, triu, hp, xe_scr, v_ref, og_ref, gc_ref, gr_ref, gbc_ref,
                 gbr_ref, ng_ref, y_ref):
    heads, dh, w = MLSTM_HEADS, MLSTM_DH, MLSTM_W
    rows = slice(r0, r0 + blk)
    conv = cw[CONV_WIDTH - 1:CONV_WIDTH, :] * xe_scr[8 + r0:8 + r0 + blk, :]
    for j in range(CONV_WIDTH - 1):
        off = 8 + r0 - (CONV_WIDTH - 1) + j
        conv = conv + cw[j:j + 1, :] * xe_scr[off:off + blk, :]
    qk = conv * jax.nn.sigmoid(conv)

    gcol = gc_ref[rows, :].astype(F32) + gbc_ref[...]
    grow = gr_ref[0, :, rows] + gbr_ref[...]
    bcol = jnp.dot(tril, _log_sigmoid(gcol), precision=hp, preferred_element_type=F32)
    brow = jnp.dot(_log_sigmoid(grow), triu, precision=hp, preferred_element_type=F32)

    new_state = []
    for h in range(heads):
        sl = slice(h * dh, (h + 1) * dh)
        b_c = bcol[:, heads + h:heads + h + 1]
        i_c = gcol[:, h:h + 1]
        b_r = brow[heads + h:heads + h + 1, :]
        i_r = grow[h:h + 1, :]
        c_st, n_st, m_st = state[h]

        d_mat = jnp.where(causal, b_c - b_r + i_r, NEG)
        inter = b_c + m_st
        m_t = jnp.maximum(inter, jnp.max(d_mat, axis=-1, keepdims=True))
        w_intra = jnp.exp(d_mat - m_t)
        w_inter = jnp.exp(inter - m_t)

        q_f = qk[:, sl]
        k_f = qk[:, w + h * dh:w + (h + 1) * dh] * (dh ** -0.5)
        q_b = q_f.astype(BF16)
        k_b = k_f.astype(BF16)
        v_b = v_ref[rows, sl]

        s = _dot_nt(q_b, k_b) * w_intra
        num = _dot(s.astype(BF16), v_b) + w_inter * _dot(q_b, c_st.astype(BF16))
        den = jnp.sum(s, axis=-1, keepdims=True) + w_inter * jnp.sum(q_f * n_st, axis=-1, keepdims=True)
        hh = num / jnp.maximum(jnp.abs(den), jnp.exp(-m_t))
        hn = _rms(hh, ng_ref[:, sl])
        y_ref[rows, sl] = (hn * jax.nn.sigmoid(og_ref[rows, sl].astype(F32))).astype(y_ref.dtype)

        b_last = b_c[blk - 1:blk, :]
        dec = b_last - b_c + i_c
        m_new = jnp.maximum(b_last + m_st, jnp.max(dec, axis=0, keepdims=True))
        w_k = jnp.exp(dec - m_new)
        w_c = jnp.exp(b_last + m_st - m_new)
        kw = k_f * w_k
        new_state.append((w_c * c_st + _dot(kw.T.astype(BF16), v_b),
                          w_c * n_st + jnp.sum(kw, axis=0, keepdims=True),
                          m_new))
    return new_state


def _mlstm(proj, gates_row, conv_w, gb_col, gb_row, norm_g, *, batch, seq, blk, nsub):
    t = proj.shape[0]
    rows = blk * nsub
    nblk = seq // rows
    w = MLSTM_W
    row = lambda b, c: b * nblk + c
    return pl.pallas_call(
        functools.partial(_mlstm_kernel, blk=blk, nsub=nsub),
        grid=(batch, nblk),
        in_specs=[pl.BlockSpec((rows, 2 * w), lambda b, c: (row(b, c), OFF_MQ // (2 * w))),
                  pl.BlockSpec((rows, w), lambda b, c: (row(b, c), OFF_MV // w)),
                  pl.BlockSpec((rows, w), lambda b, c: (row(b, c), OFF_MO // w)),
                  pl.BlockSpec((rows, IF_PAD), lambda b, c: (row(b, c), OFF_IF // IF_PAD)),
                  pl.BlockSpec((1, 8, rows), lambda b, c: (b, 0, c)),
                  pl.BlockSpec((CONV_WIDTH, 2 * w), lambda b, c: (0, 0)),
                  pl.BlockSpec((1, IF_PAD), lambda b, c: (0, 0)),
                  pl.BlockSpec((8, 1), lambda b, c: (0, 0)),
                  pl.BlockSpec((1, w), lambda b, c: (0, 0))],
        out_specs=pl.BlockSpec((rows, w), lambda b, c: (row(b, c), 0)),
        out_shape=jax.ShapeDtypeStruct((t, w), BF16),
        scratch_shapes=[pltpu.VMEM((rows + 8, 2 * w), F32),
                        pltpu.VMEM((MLSTM_HEADS, MLSTM_DH, MLSTM_DH), F32),
                        pltpu.VMEM((8, MLSTM_DH), F32),
                        pltpu.VMEM((8, 128), F32)],
        compiler_params=_cparams("parallel", "arbitrary"),
        name="mlstm",
    )(proj, proj, proj, proj, gates_row, conv_w, gb_col, gb_row, norm_g)


def _attnproj_kernel(h_ref, w_ref, seg_ref, gq_ref, gk_ref, o_ref, r_scr):
    j = pl.program_id(1)
    gw, half = ATTN_GW, ATTN_SLAB // 2
    sub_rows = 512

    def head_norm(x, gain):
        sq = x * x
        hi = sq.astype(BF16)
        lo = (sq - hi.astype(F32)).astype(BF16)
        ss = _dot(hi, seg_ref[...]) + _dot(lo, seg_ref[...])
        return x * lax.rsqrt(ss * (1.0 / ATTN_DH) + EPS) * gain

    low = lax.broadcasted_iota(jnp.int32, (1, ATTN_SLAB), 1) < half
    for s in range(ATTN_TILE // sub_rows):
        rows = slice(s * sub_rows, (s + 1) * sub_rows)
        res = _dot(h_ref[rows, :], w_ref[...])
        q = head_norm(res[:, :gw], gq_ref[...]) * (ATTN_DH ** -0.5)
        k = head_norm(res[:, gw:2 * gw], gk_ref[...])
        slabs = []
        for pair in range(gw // ATTN_SLAB):
            qp = q[:, pair * ATTN_SLAB:(pair + 1) * ATTN_SLAB]
            slabs += [jnp.where(low, qp, 0.0), jnp.where(low, 0.0, qp)]
        slabs += [k[:, c * 128:(c + 1) * 128] for c in range(gw // 128)]
        slabs += [res[:, 2 * gw + c * 128:2 * gw + (c + 1) * 128] for c in range(gw // 128)]
        @pl.when(j == 0)
        def _(slabs=slabs, rows=rows):
            for c, slab in enumerate(slabs):
                o_ref[rows, c * 128:(c + 1) * 128] = slab.astype(o_ref.dtype)

        @pl.when(j != 0)
        def _(slabs=slabs, rows=rows):
            for c, slab in enumerate(slabs):
                r_scr[c, rows, :] = slab

    for g, (_, dil) in enumerate(ATTN_PATTERNS):
        if dil == 1:
            continue

        @pl.when(j == g)
        def _(dil=dil):
            seg = ATTN_TILE // dil
            for r in range(dil):
                for c in range(r_scr.shape[0]):
                    o_ref[r * seg:(r + 1) * seg, c * 128:(c + 1) * 128] = (
                        r_scr[c, pl.ds(r, seg, stride=dil), :].astype(o_ref.dtype))


def _attnproj(h, w, seg_ones, gq, gk):
    t, d = h.shape
    ng = len(ATTN_PATTERNS)
    wcols = 3 * ATTN_GW
    const2 = lambda i, j: (0, 0)
    return pl.pallas_call(
        _attnproj_kernel,
        grid=(t // ATTN_TILE, ng),
        in_specs=[pl.BlockSpec((ATTN_TILE, d), lambda i, j: (i, 0)),
                  pl.BlockSpec((d, wcols), lambda i, j: (0, j)),
                  pl.BlockSpec((ATTN_GW, ATTN_GW), const2),
                  pl.BlockSpec((1, ATTN_GW), const2), pl.BlockSpec((1, ATTN_GW), const2)],
        out_specs=pl.BlockSpec((ATTN_TILE, ATTN_COLS), lambda i, j: (i, j)),
        out_shape=jax.ShapeDtypeStruct((t, ng * ATTN_COLS), BF16),
        scratch_shapes=[pltpu.VMEM((ATTN_COLS // 128, ATTN_TILE, 128), F32)],
        compiler_params=_cparams("parallel", "arbitrary"),
        name="attnproj",
    )(h, w, seg_ones, gq, gk)


def _dattn_kernel(q_ref, kc_ref, kp_ref, vc_ref, vp_ref, bias_ref, o_ref, lse_ref,
                  kx_scr, vx_scr, o_scr, l_scr, *, dil):
    blk = ATTN_BLOCK
    per = ATTN_SUB // dil
    first_tile = pl.program_id(1) == 0
    for r in range(dil):
        base = r * (per + 1) * blk
        last = slice((r * per + per - 1) * blk, (r * per + per) * blk)
        mine = slice(r * per * blk, (r + 1) * per * blk)
        kx_scr[base:base + blk, :] = kp_ref[last, :]
        vx_scr[base:base + blk, :] = vp_ref[last, :]
        kx_scr[base + blk:base + (per + 1) * blk, :] = kc_ref[mine, :]
        vx_scr[base + blk:base + (per + 1) * blk, :] = vc_ref[mine, :]

    low = lax.broadcasted_iota(jnp.int32, (1, ATTN_SLAB), 1) < ATTN_SLAB // 2
    no_prev = lax.broadcasted_iota(jnp.int32, (1, 2 * blk), 1) < blk
    for r in range(dil):
        for sub in range(per):
            u = r * per + sub
            win = slice((r * (per + 1) + sub) * blk, (r * (per + 1) + sub + 2) * blk)
            o_slabs, l_slabs = [], []
            for pair in range(ATTN_GW // ATTN_SLAB):
                cols = slice(pair * ATTN_SLAB, (pair + 1) * ATTN_SLAB)
                kx, vx = kx_scr[win, cols], vx_scr[win, cols]
                o_pair, l_pair = [], []
                for h in (2 * pair, 2 * pair + 1):
                    logits = _dot_nt(q_ref[u * blk:(u + 1) * blk, h * ATTN_SLAB:(h + 1) * ATTN_SLAB], kx)
                    logits = logits + bias_ref[h]
                    if sub == 0:
                        logits = jnp.where(first_tile & no_prev, NEG, logits)
                    m = jnp.max(logits, axis=-1, keepdims=True)
                    p = jnp.exp(logits - m)
                    l = jnp.sum(p, axis=-1, keepdims=True)
                    o_pair.append(_dot(p.astype(BF16), vx) / l)
                    l_pair.append(m + jnp.log(l))
                o_slabs.append(jnp.where(low, o_pair[0], o_pair[1]))
                l_slabs.append(jnp.where(low, l_pair[0], l_pair[1]))
            dst = pl.ds(sub * blk * dil + r, blk, stride=dil) if dil > 1 else slice(u * blk, (u + 1) * blk)
            for c in range(ATTN_GW // ATTN_SLAB):
                o_scr[c, dst, :] = o_slabs[c]
                l_scr[c, dst, :] = l_slabs[c]
    for c in range(ATTN_GW // ATTN_SLAB):
        o_ref[:, c * ATTN_SLAB:(c + 1) * ATTN_SLAB] = o_scr[c].astype(o_ref.dtype)
        lse_ref[:, c * ATTN_SLAB:(c + 1) * ATTN_SLAB] = l_scr[c]


def _dattn(aproj, bias, *, seq, group, dilation):
    t = aproj.shape[0]
    tiles = seq // ATTN_TILE
    qw = HEADS_PER_GROUP * ATTN_SLAB
    cq = group * ATTN_COLS // qw
    ck, cv = (group * ATTN_COLS + qw) // ATTN_GW, (group * ATTN_COLS + qw) // ATTN_GW + 1
    blk = (ATTN_TILE, ATTN_GW)
    cur = lambda c: (lambda b, j: (b * tiles + j, c))
    prev = lambda c: (lambda b, j: (b * tiles + jnp.maximum(j - 1, 0), c))
    xrows = ATTN_TILE + dilation * ATTN_BLOCK
    return pl.pallas_call(
        functools.partial(_dattn_kernel, dil=dilation),
        grid=(t // seq, tiles),
        in_specs=[pl.BlockSpec((ATTN_TILE, qw), cur(cq)),
                  pl.BlockSpec(blk, cur(ck)), pl.BlockSpec(blk, prev(ck)),
                  pl.BlockSpec(blk, cur(cv)), pl.BlockSpec(blk, prev(cv)),
                  pl.BlockSpec((HEADS_PER_GROUP, ATTN_BLOCK, 2 * ATTN_BLOCK), lambda b, j: (0, 0, 0))],
        out_specs=[pl.BlockSpec(blk, cur(0)), pl.BlockSpec(blk, cur(0))],
        out_shape=[jax.ShapeDtypeStruct((t, ATTN_GW), BF16), jax.ShapeDtypeStruct((t, ATTN_GW), F32)],
        scratch_shapes=[pltpu.VMEM((xrows, ATTN_GW), BF16), pltpu.VMEM((xrows, ATTN_GW), BF16),
                        pltpu.VMEM((ATTN_GW // ATTN_SLAB, ATTN_TILE, ATTN_SLAB), F32),
                        pltpu.VMEM((ATTN_GW // ATTN_SLAB, ATTN_TILE, ATTN_SLAB), F32)],
        compiler_params=_cparams("parallel", "arbitrary"),
        name=f"dattn{group}",
    )(aproj, aproj, aproj, aproj, aproj, bias)


def _rel_bucket(n):
    max_exact = REL_BUCKETS // 2
    nf = jnp.maximum(n, 1).astype(F32)
    log_b = max_exact + (jnp.log(nf / max_exact) / math.log(REL_MAX_DIST / max_exact)
                         * (REL_BUCKETS - max_exact)).astype(jnp.int32)
    return jnp.where(n < max_exact, n, jnp.minimum(log_b, REL_BUCKETS - 1))


def _attn_bias(rel_bias, group):
    window, dilation = ATTN_PATTERNS[group]
    steps = window // dilation
    hp = lax.Precision.HIGHEST
    hs = slice(group * HEADS_PER_GROUP, (group + 1) * HEADS_PER_GROUP)
    bucket = _rel_bucket(jnp.arange(steps + 1) * dilation)
    bias_steps = jnp.dot(jax.nn.one_hot(bucket, REL_BUCKETS, dtype=F32), rel_bias[:, hs].astype(F32),
                         precision=hp)
    qi = jnp.arange(ATTN_BLOCK)[:, None]
    ki = jnp.arange(2 * ATTN_BLOCK)[None, :]
    dist = ATTN_BLOCK + qi - ki
    ok = (dist >= 0) & (dist <= steps)
    sel = jax.nn.one_hot(jnp.clip(dist, 0, steps).reshape(-1), steps + 1, dtype=F32)
    bias = jnp.dot(sel, bias_steps, precision=hp).T.reshape(HEADS_PER_GROUP, ATTN_BLOCK, 2 * ATTN_BLOCK)
    return jnp.where(ok[None], bias, NEG)


def _merge_kernel(ya_ref, yb0_ref, yb1_ref, yb2_ref, l0_ref, l1_ref, l2_ref, gu_ref, gv_ref, gate_ref,
                  x_ref, wa_ref, wb_ref, wc_ref, wo_ref, ws_ref, bs_ref, gg_ref, o_ref, yc_scr, *, tm):
    d = x_ref.shape[1]
    l0, l1, l2 = l0_ref[...], l1_ref[...], l2_ref[...]
    mx = jnp.maximum(jnp.maximum(l0, l1), l2)
    e0, e1, e2 = jnp.exp(l0 - mx), jnp.exp(l1 - mx), jnp.exp(l2 - mx)
    inv = 1.0 / (e0 + e1 + e2)
    yb = jnp.concatenate([(yb0_ref[...].astype(F32) * (e0 * inv)).astype(BF16),
                          (yb1_ref[...].astype(F32) * (e1 * inv)).astype(BF16),
                          (yb2_ref[...].astype(F32) * (e2 * inv)).astype(BF16)], axis=-1)

    for j in range(tm // GMLP_CHUNK):
        rows = slice(j * GMLP_CHUNK, (j + 1) * GMLP_CHUNK)
        for g in range(GMLP_GROUPS):
            cols = slice(g * GMLP_GC, (g + 1) * GMLP_GC)
            u = jax.nn.gelu(gu_ref[rows, cols].astype(F32))
            v = _rms(jax.nn.gelu(gv_ref[rows, cols].astype(F32)), gg_ref[:, cols])
            mixed = _dot(ws_ref[g], v.astype(BF16)) + bs_ref[g]
            yc_scr[rows, cols] = (u * mixed).astype(BF16)

    merged = jax.nn.sigmoid(gate_ref[:, 0:d].astype(F32)) * _dot(ya_ref[...], wa_ref[...])
    merged = merged + jax.nn.sigmoid(gate_ref[:, d:2 * d].astype(F32)) * _dot(yb, wb_ref[...])
    merged = merged + jax.nn.sigmoid(gate_ref[:, 2 * d:3 * d].astype(F32)) * _dot(yc_scr[...], wc_ref[...])
    o_ref[...] = x_ref[...] + _dot(merged.astype(BF16), wo_ref[...])


def _merge(ya, ybs, lses, proj, x2d, wa, wb, wc, wo, ws, bsb, gg, *, tm):
    t, d = x2d.shape
    row = lambda c: (lambda i: (i, c))
    full2 = lambda i: (0, 0)
    full3 = lambda i: (0, 0, 0)
    gspec = pl.BlockSpec((tm, ATTN_GW), row(0))
    return pl.pallas_call(
        functools.partial(_merge_kernel, tm=tm),
        grid=(t // tm,),
        in_specs=[pl.BlockSpec((tm, MLSTM_W), row(0)),
                  gspec, gspec, gspec, gspec, gspec, gspec,
                  pl.BlockSpec((tm, GMLP_W), row(OFF_GU // GMLP_W)),
                  pl.BlockSpec((tm, GMLP_W), row(OFF_GV // GMLP_W)),
                  pl.BlockSpec((tm, N_BRANCH * d), row(OFF_GATE // (N_BRANCH * d))),
                  pl.BlockSpec((tm, d), row(0)),
                  pl.BlockSpec(wa.shape, full2), pl.BlockSpec(wb.shape, full2),
                  pl.BlockSpec(wc.shape, full2), pl.BlockSpec(wo.shape, full2),
                  pl.BlockSpec(ws.shape, full3), pl.BlockSpec(bsb.shape, full3),
                  pl.BlockSpec(gg.shape, full2)],
        out_specs=pl.BlockSpec((tm, d), row(0)),
        out_shape=jax.ShapeDtypeStruct((t, d), F32),
        scratch_shapes=[pltpu.VMEM((tm, GMLP_W), BF16)],
        compiler_params=_cparams("parallel"),
        name="merge",
    )(ya, *ybs, *lses, proj, proj, proj, x2d, wa, wb, wc, wo, ws, bsb, gg)


def _memkv_kernel(mem_ref, g_ref, w_ref, gk_ref, k_ref, v_ref):
    dh, w = XATTN_DH, XATTN_W
    kv = _dot(_rms(mem_ref[0], g_ref[...]).astype(BF16), w_ref[...])
    for h in range(XATTN_HEADS):
        sl = slice(h * dh, (h + 1) * dh)
        k_ref[0, :, sl] = _rms(kv[:, sl], gk_ref[...]).astype(k_ref.dtype)
    v_ref[0] = kv[:, w:].astype(v_ref.dtype)


def _memkv(mem, gain, w_kv, gk):
    b, m, d = mem.shape
    full2 = lambda i: (0, 0)
    return pl.pallas_call(
        _memkv_kernel,
        grid=(b,),
        in_specs=[pl.BlockSpec((1, m, d), lambda i: (i, 0, 0)),
                  pl.BlockSpec((1, d), full2),
                  pl.BlockSpec(w_kv.shape, full2),
                  pl.BlockSpec((1, XATTN_DH), full2)],
        out_specs=[pl.BlockSpec((1, m, XATTN_W), lambda i: (i, 0, 0)),
                   pl.BlockSpec((1, m, XATTN_W), lambda i: (i, 0, 0))],
        out_shape=[jax.ShapeDtypeStruct((b, m, XATTN_W), BF16),
                   jax.ShapeDtypeStruct((b, m, XATTN_W), BF16)],
        compiler_params=_cparams("parallel"),
        name="memkv",
    )(mem, gain, w_kv, gk)


def _route(logits):
    tm = logits.shape[1]
    e = jnp.exp(logits - jnp.max(logits, axis=0, keepdims=True))
    probs = e / jnp.sum(e, axis=0, keepdims=True)
    rowi = lax.broadcasted_iota(jnp.int32, (8, tm), 0)
    real = rowi < EXPERTS_PER_GROUP
    tops = []
    for g in range(N_EXPERT_GROUPS):
        pg = jnp.where(real, probs[8 * g:8 * g + 8, :], -0.5)
        m1 = jnp.max(pg, axis=0, keepdims=True)
        i1 = jnp.min(jnp.where(pg == m1, rowi, 8), axis=0, keepdims=True)
        pg2 = jnp.where(rowi == i1, -1.0, pg)
        m2 = jnp.max(pg2, axis=0, keepdims=True)
        i2 = jnp.min(jnp.where(pg2 == m2, rowi, 8), axis=0, keepdims=True)
        tops.append((m1, i1, m2, i2))
    best = jnp.zeros((1, tm), jnp.int32)
    best_score = tops[0][0] + tops[0][2]
    for g in range(1, N_EXPERT_GROUPS):
        score = tops[g][0] + tops[g][2]
        better = score > best_score
        best = jnp.where(better, g, best)
        best_score = jnp.where(better, score, best_score)
    m1, i1, m2, i2 = tops[0]
    for g in range(1, N_EXPERT_GROUPS):
        m1, i1, m2, i2 = (jnp.where(best == g, new, old) for new, old in zip(tops[g], (m1, i1, m2, i2)))
    tot = m1 + m2
    base = best * EXPERTS_PER_GROUP
    return base + i1, base + i2, m1 / tot, m2 / tot


def _pack_bf16_pairs(x):
    n = x.shape[1] // 2
    hi = lax.bitcast_convert_type(x[:, :n].astype(BF16).astype(F32), jnp.uint32)
    lo = lax.bitcast_convert_type(x[:, n:].astype(BF16).astype(F32), jnp.uint32)
    return hi | (lo >> 16)


def _unpack_bf16_pairs(p):
    hi = lax.bitcast_convert_type(p & jnp.uint32(0xFFFF0000), F32)
    lo = lax.bitcast_convert_type(p << 16, F32)
    return hi, lo


def _store_row_chunks(ref, packed):
    for j in range(ROW_CHUNKS):
        ref[j] = packed[:, j * 128:(j + 1) * 128]


def _load_row_chunks(ref):
    return jnp.concatenate([ref[j] for j in range(ROW_CHUNKS)], axis=-1)


def _xattn_kernel(x_ref, k_ref, v_ref, gx_ref, wq_ref, gq_ref, wo_ref, gf_ref, rw_ref, rb_ref,
                  xo_ref, hf_ref, eidx_ref, wts_ref):
    dh = XATTN_DH
    x = x_ref[...]
    q = _dot(_rms(x, gx_ref[...]).astype(BF16), wq_ref[...])
    outs = []
    for h in range(XATTN_HEADS):
        sl = slice(h * dh, (h + 1) * dh)
        q_h = (_rms(q[:, sl], gq_ref[...]) * (dh ** -0.5)).astype(BF16)
        logits = _dot_nt(q_h, k_ref[0, :, sl])
        p = jnp.exp(logits - jnp.max(logits, axis=-1, keepdims=True))
        o = _dot(p.astype(BF16), v_ref[0, :, sl]) / jnp.sum(p, axis=-1, keepdims=True)
        outs.append(o.astype(BF16))
    xn = x + _dot(jnp.concatenate(outs, axis=-1), wo_ref[...])
    xo_ref[...] = xn
    hf = _rms(xn, gf_ref[...])
    _store_row_chunks(hf_ref, _pack_bf16_pairs(hf))
    rw = rw_ref[...]
    rw_hi = rw.astype(BF16)
    rw_lo = (rw - rw_hi.astype(F32)).astype(BF16)
    hf_hi = hf.astype(BF16)
    hf_lo = (hf - hf_hi.astype(F32)).astype(BF16)
    logits_t = _dot_nt(rw_hi, hf_hi) + _dot_nt(rw_hi, hf_lo) + _dot_nt(rw_lo, hf_hi) + rb_ref[...]
    e1, e2, w1, w2 = _route(logits_t)
    tm = x.shape[0]
    eidx_ref[...] = jnp.concatenate([e1, e2, jnp.zeros((6, tm), jnp.int32)], axis=0)
    wts_ref[...] = jnp.concatenate([w1, w2, jnp.zeros((6, tm), F32)], axis=0)


def _xattn(x2d, k, v, gx, wq, gq, wo, gf, rw_t, rb, *, seq, tm):
    t, d = x2d.shape
    per_b = seq // tm
    full2 = lambda i: (0, 0)
    kv_spec = pl.BlockSpec((1,) + k.shape[1:], lambda i: (i // per_b, 0, 0))
    return pl.pallas_call(
        _xattn_kernel,
        grid=(t // tm,),
        in_specs=[pl.BlockSpec((tm, d), lambda i: (i, 0)), kv_spec, kv_spec,
                  pl.BlockSpec((1, d), full2), pl.BlockSpec(wq.shape, full2),
                  pl.BlockSpec((1, XATTN_DH), full2), pl.BlockSpec(wo.shape, full2),
                  pl.BlockSpec((1, d), full2), pl.BlockSpec(rw_t.shape, full2),
                  pl.BlockSpec(rb.shape, full2)],
        out_specs=[pl.BlockSpec((tm, d), lambda i: (i, 0)),
                   pl.BlockSpec((ROW_CHUNKS, tm, 128), lambda i: (0, i, 0)),
                   pl.BlockSpec((8, tm), lambda i: (0, i)),
                   pl.BlockSpec((8, tm), lambda i: (0, i))],
        out_shape=[jax.ShapeDtypeStruct((t, d), F32),
                   jax.ShapeDtypeStruct((ROW_CHUNKS, t, 128), jnp.uint32),
                   jax.ShapeDtypeStruct((8, t), jnp.int32),
                   jax.ShapeDtypeStruct((8, t), F32)],
        compiler_params=_cparams("parallel"),
        name="xattn_router",
    )(x2d, k, v, gx, wq, gq, wo, gf, rw_t, rb)


def _moe_plan_kernel(eidx_ref, i1_ref, i2_ref, te_ref, na_ref, cnt_scr, carry_scr, *, tb, tm, plane_rows):
    ne = N_EXPERTS
    hp = lax.Precision.HIGHEST
    phase, j = pl.program_id(0), pl.program_id(1)
    rows = lax.broadcasted_iota(jnp.int32, (ne, tb), 0)
    oh1 = rows == eidx_ref[0:1, :]
    oh2 = rows == eidx_ref[1:2, :]
    a = oh1.astype(F32) + oh2.astype(F32)
    blk_cnt = jnp.broadcast_to(jnp.sum(a, axis=1, keepdims=True), cnt_scr.shape)

    @pl.when((phase == 0) & (j == 0))
    def _():
        cnt_scr[...] = jnp.zeros_like(cnt_scr)

    @pl.when(phase == 0)
    def _():
        cnt_scr[...] += blk_cnt

    @pl.when((phase == 1) & (j == 0))
    def _():
        padded = jnp.ceil(cnt_scr[...] * (1.0 / tm)) * tm
        er = lax.broadcasted_iota(jnp.int32, (ne, ne), 0)
        ec = lax.broadcasted_iota(jnp.int32, (ne, ne), 1)
        off = jnp.dot((ec < er).astype(F32), padded, precision=hp, preferred_element_type=F32)
        carry_scr[...] = off
        seg_end = (off + padded)[:, 0:1]
        tile_start = lax.broadcasted_iota(jnp.int32, (ne, te_ref.shape[1]), 1).astype(F32) * tm
        te = jnp.sum((seg_end <= tile_start).astype(F32), axis=0, keepdims=True)
        te_ref[...] = jnp.broadcast_to(jnp.minimum(te, ne - 1.0), te_ref.shape).astype(jnp.int32)
        total = jnp.sum(padded[:, 0:1], axis=0, keepdims=True)
        na_ref[...] = jnp.broadcast_to(total * (1.0 / tm), na_ref.shape).astype(jnp.int32)

    @pl.when(phase == 1)
    def _():
        before = (lax.broadcasted_iota(jnp.int32, (tb, tb), 0)
                  < lax.broadcasted_iota(jnp.int32, (tb, tb), 1)).astype(BF16)
        rank = carry_scr[:, 0:1] + _dot(a.astype(BF16), before)
        d1 = jnp.sum(jnp.where(oh1, rank, 0.0), axis=0, keepdims=True).astype(jnp.int32)
        d2 = jnp.sum(jnp.where(oh2, rank, 0.0), axis=0, keepdims=True).astype(jnp.int32)
        plane = lax.broadcasted_iota(jnp.int32, (8, tb), 0) * plane_rows
        i1_ref[...] = jnp.where(plane < ROW_CHUNKS * plane_rows, plane + d1, 0)
        i2_ref[...] = jnp.where(plane < ROW_CHUNKS * plane_rows, plane + d2, 0)
        carry_scr[...] += blk_cnt


def _moe_plan(eidx, *, tm, n_tiles, tb=512):
    t = eidx.shape[1]
    ntp = -(-n_tiles // 128) * 128
    return pl.pallas_call(
        functools.partial(_moe_plan_kernel, tb=tb, tm=tm, plane_rows=n_tiles * tm),
        grid=(2, t // tb),
        in_specs=[pl.BlockSpec((8, tb), lambda p, j: (0, j))],
        out_specs=[pl.BlockSpec((8, tb), lambda p, j: (0, j * p)),
                   pl.BlockSpec((8, tb), lambda p, j: (0, j * p)),
                   pl.BlockSpec((8, ntp), lambda p, j: (0, 0)),
                   pl.BlockSpec((8, 128), lambda p, j: (0, 0))],
        out_shape=[jax.ShapeDtypeStruct((8, t), jnp.int32),
                   jax.ShapeDtypeStruct((8, t), jnp.int32),
                   jax.ShapeDtypeStruct((8, ntp), jnp.int32),
                   jax.ShapeDtypeStruct((8, 128), jnp.int32)],
        scratch_shapes=[pltpu.VMEM((N_EXPERTS, 128), F32), pltpu.VMEM((N_EXPERTS, 128), F32)],
        compiler_params=_cparams("arbitrary", "arbitrary"),
        name="moe_plan",
    )(eidx)


def _sc_mesh():
    return plsc.VectorSubcoreMesh(core_axis_name="c", subcore_axis_name="s",
                                  num_cores=SC_CORES, num_subcores=SC_SUBCORES)


def _sc_index_spec(tokens):
    nb = tokens // SC_WINDOW
    return pl.BlockSpec((1, SC_WINDOW), lambda i: (i // nb, i % nb))


def _sc_dispatch(rows, i1, i2, n_out):
    n = rows.shape[0]
    tokens = i1.shape[1]

    @functools.partial(pl.kernel, out_type=jax.ShapeDtypeStruct((n_out, 128), rows.dtype), mesh=_sc_mesh(),
                       name="moe_dispatch")
    def k(x_hbm, i1_hbm, i2_hbm, o_hbm):
        def body(x_vmem, i1_vmem, i2_vmem):
            pltpu.sync_copy(x_vmem, o_hbm.at[i1_vmem.at[0]])
            pltpu.sync_copy(x_vmem, o_hbm.at[i2_vmem.at[0]])

        pltpu.emit_pipeline(
            body, grid=(n // SC_WINDOW,),
            in_specs=[pl.BlockSpec((SC_WINDOW, 128), lambda i: (i, 0)),
                      _sc_index_spec(tokens), _sc_index_spec(tokens)],
            out_specs=[],
            core_axis_name=("c", "s"), dimension_semantics=(pltpu.PARALLEL,),
        )(x_hbm, i1_hbm, i2_hbm)

    return k(rows, i1, i2)


def _sc_collect(table, i1, i2):
    tokens = i1.shape[1]
    n = ROW_CHUNKS * tokens
    out = jax.ShapeDtypeStruct((n, 128), table.dtype)

    @functools.partial(pl.kernel, out_type=(out, out), mesh=_sc_mesh(), name="moe_collect")
    def k(t_hbm, i1_hbm, i2_hbm, o1_hbm, o2_hbm):
        def body(i1_vmem, i2_vmem, o1_vmem, o2_vmem):
            pltpu.sync_copy(t_hbm.at[i1_vmem.at[0]], o1_vmem)
            pltpu.sync_copy(t_hbm.at[i2_vmem.at[0]], o2_vmem)

        pltpu.emit_pipeline(
            body, grid=(n // SC_WINDOW,),
            in_specs=[_sc_index_spec(tokens), _sc_index_spec(tokens)],
            out_specs=[pl.BlockSpec((SC_WINDOW, 128), lambda i: (i, 0)),
                       pl.BlockSpec((SC_WINDOW, 128), lambda i: (i, 0))],
            core_axis_name=("c", "s"), dimension_semantics=(pltpu.PARALLEL,),
        )(i1_hbm, i2_hbm, o1_hbm, o2_hbm)

    return k(table, i1, i2)


def _experts_kernel(te_ref, na_ref, xs_ref, wg_ref, wu_ref, wd_ref, y_ref, wg_scr, wu_scr, wd_scr):
    i = pl.program_id(0)
    active = i < na_ref[0]

    @pl.when(active & ((i == 0) | (te_ref[i] != te_ref[jnp.maximum(i - 1, 0)])))
    def _():
        wg_scr[...] = wg_ref[0, 0].astype(BF16)
        wu_scr[...] = wu_ref[0, 0].astype(BF16)
        wd_scr[...] = wd_ref[0, 0].astype(BF16)

    @pl.when(active)
    def _():
        hi, lo = _unpack_bf16_pairs(_load_row_chunks(xs_ref))
        h = jnp.concatenate([hi, lo], axis=-1).astype(BF16)
        up = _dot(h, wg_scr[...])
        act = up * jax.nn.sigmoid(up) * _dot(h, wu_scr[...])
        _store_row_chunks(y_ref, _pack_bf16_pairs(_dot(act.astype(BF16), wd_scr[...])))


def _experts(tile_expert, n_active, xs, wg, wu, wd, *, layer, tm):
    n_tiles = tile_expert.shape[0]
    _, _, d, dff = wg.shape
    rows = lambda i, te, na: (0, jnp.minimum(i, na[0] - 1), 0)
    expert = lambda i, te, na: (layer, te[i], 0, 0)
    return pl.pallas_call(
        _experts_kernel,
        grid_spec=pltpu.PrefetchScalarGridSpec(
            num_scalar_prefetch=2,
            grid=(n_tiles,),
            in_specs=[pl.BlockSpec((ROW_CHUNKS, tm, 128), rows),
                      pl.BlockSpec((1, 1, d, dff), expert),
                      pl.BlockSpec((1, 1, d, dff), expert),
                      pl.BlockSpec((1, 1, dff, d), expert)],
            out_specs=pl.BlockSpec((ROW_CHUNKS, tm, 128), rows),
            scratch_shapes=[pltpu.VMEM((d, dff), BF16), pltpu.VMEM((d, dff), BF16), pltpu.VMEM((dff, d), BF16)]),
        out_shape=jax.ShapeDtypeStruct(xs.shape, xs.dtype),
        compiler_params=_cparams("arbitrary"),
        name="moe_experts",
    )(tile_expert, n_active, xs, wg, wu, wd)


def _moe_combine_kernel(x_ref, y1_ref, y2_ref, w_ref, o_ref):
    half = x_ref.shape[1] // 2
    hi1, lo1 = _unpack_bf16_pairs(_load_row_chunks(y1_ref))
    hi2, lo2 = _unpack_bf16_pairs(_load_row_chunks(y2_ref))
    w1, w2 = w_ref[:, 0:1], w_ref[:, 1:2]
    o_ref[:, :half] = x_ref[:, :half] + w1 * hi1 + w2 * hi2
    o_ref[:, half:] = x_ref[:, half:] + w1 * lo1 + w2 * lo2


def _moe_combine(x2d, y1, y2, wcol, *, tm):
    t, d = x2d.shape
    chunk_spec = pl.BlockSpec((ROW_CHUNKS, tm, 128), lambda i: (0, i, 0))
    return pl.pallas_call(
        _moe_combine_kernel,
        grid=(t // tm,),
        in_specs=[pl.BlockSpec((tm, d), lambda i: (i, 0)), chunk_spec, chunk_spec,
                  pl.BlockSpec((tm, wcol.shape[1]), lambda i: (i, 0))],
        out_specs=pl.BlockSpec((tm, d), lambda i: (i, 0)),
        out_shape=jax.ShapeDtypeStruct((t, d), F32),
        compiler_params=_cparams("parallel"),
        name="moe_combine",
    )(x2d, y1, y2, wcol)


def _moe(x2d, hf_rows, eidx, wts, wg, wu, wd, *, layer):
    t = x2d.shape[0]
    tm = MOE_TM
    n_tiles = 2 * t // tm + N_EXPERTS
    plane = n_tiles * tm
    i1, i2, te, na = _moe_plan(eidx, tm=tm, n_tiles=n_tiles)
    xs = _sc_dispatch(hf_rows.reshape(ROW_CHUNKS * t, 128), i1, i2, ROW_CHUNKS * plane)
    ys = _experts(te[0, :n_tiles], na[0, :1], xs.reshape(ROW_CHUNKS, plane, 128), wg, wu, wd,
                  layer=layer, tm=tm)
    y1, y2 = _sc_collect(ys.reshape(ROW_CHUNKS * plane, 128), i1, i2)
    return _moe_combine(x2d, y1.reshape(ROW_CHUNKS, t, 128), y2.reshape(ROW_CHUNKS, t, 128), wts[:2].T, tm=512)


def _layout_w_in(w):
    sizes = (MLSTM_W, MLSTM_W, MLSTM_W, MLSTM_W, MLSTM_HEADS, MLSTM_HEADS,
             ATTN_W, ATTN_W, ATTN_W, GMLP_W, GMLP_W, N_BRANCH * w.shape[0])
    pts = np.cumsum(sizes)[:-1]
    mq, mk, mv, mo, mi, mf, aq, ak, av, gu, gv, gate = jnp.split(w, pts, axis=-1)
    pad = jnp.zeros((w.shape[0], IF_PAD - 2 * MLSTM_HEADS), w.dtype)
    main = jnp.concatenate([mq, mk, mv, mo, gu, gv, gate, mi, mf, pad], axis=-1).astype(BF16)
    attn = jnp.concatenate([a[:, g * ATTN_GW:(g + 1) * ATTN_GW] for g in range(len(ATTN_PATTERNS))
                            for a in (aq, ak, av)], axis=-1).astype(BF16)
    return main, attn


def kernel(x, mem, norm_mix, w_in, mlstm_conv, mlstm_gate_b, mlstm_norm, attn_qk_norm, gmlp_norm, gmlp_ws,
           gmlp_bs, w_branch_a, w_branch_b, w_branch_c, w_out, rel_bias, norm_xattn, norm_mem, w_xq, w_xkv,
           xattn_qk_norm, w_xo, norm_ffn, router_w, router_b, w_expert_gate, w_expert_up, w_expert_down):
    b, s, d = x.shape
    t = b * s
    depth = w_in.shape[0]
    x2d = x.reshape(t, d)

    biases = [_attn_bias(rel_bias, g) for g in range(len(ATTN_PATTERNS))]
    rw_t = jnp.zeros((N_EXPERT_GROUPS, 8, d), F32).at[:, :EXPERTS_PER_GROUP].set(
        router_w.T.reshape(N_EXPERT_GROUPS, EXPERTS_PER_GROUP, d)).reshape(ROUTER_ROWS, d)
    rb = jnp.full((N_EXPERT_GROUPS, 8), NEG, F32).at[:, :EXPERTS_PER_GROUP].set(
        router_b.astype(F32).reshape(N_EXPERT_GROUPS, EXPERTS_PER_GROUP)).reshape(ROUTER_ROWS, 1)
    tril = jnp.tril(jnp.ones((GMLP_CHUNK, GMLP_CHUNK), bool))
    head_of = jnp.arange(ATTN_GW) // ATTN_DH
    seg_ones = (head_of[:, None] == head_of[None, :]).astype(BF16)

    for l in range(depth):
        w_main, w_attn = _layout_w_in(w_in[l])
        proj, h_mix = _inproj(x2d, norm_mix[l][None], w_main, tm=1024, tn=1280)
        gq = jnp.tile(attn_qk_norm[l, 0], HEADS_PER_GROUP)[None]
        gk = jnp.tile(attn_qk_norm[l, 1], HEADS_PER_GROUP)[None]
        aproj = _attnproj(h_mix, w_attn, seg_ones, gq, gk)

        gates_row = proj[:, OFF_IF:OFF_IF + 8].astype(F32).reshape(b, s, 8).transpose(0, 2, 1)
        gb_col = jnp.zeros((1, IF_PAD), F32).at[0, :8].set(mlstm_gate_b[l])
        ya = _mlstm(proj, gates_row, mlstm_conv[l], gb_col, mlstm_gate_b[l].reshape(8, 1),
                    mlstm_norm[l][None], batch=b, seq=s, blk=MLSTM_BLOCK, nsub=MLSTM_NSUB)

        ybs, lses = [], []
        for g, (_, dilation) in enumerate(ATTN_PATTERNS):
            o, lse = _dattn(aproj, biases[g], seq=s, group=g, dilation=dilation)
            ybs.append(o)
            lses.append(lse)

        ws = jnp.where(tril, gmlp_ws[l], 0.0).astype(BF16)
        bsb = jnp.broadcast_to(gmlp_bs[l][:, :, None], (GMLP_GROUPS, GMLP_CHUNK, GMLP_GC)).astype(F32)
        x2d = _merge(ya, ybs, lses, proj, x2d, w_branch_a[l].astype(BF16), w_branch_b[l].astype(BF16),
                     w_branch_c[l].astype(BF16), w_out[l].astype(BF16), ws, bsb, gmlp_norm[l][None], tm=256)

        k_mem, v_mem = _memkv(mem, norm_mem[l][None], w_xkv[l].astype(BF16), xattn_qk_norm[l, 1][None])
        x2d, hf_rows, eidx, wts = _xattn(x2d, k_mem, v_mem, norm_xattn[l][None], w_xq[l].astype(BF16),
                                         xattn_qk_norm[l, 0][None], w_xo[l].astype(BF16), norm_ffn[l][None],
                                         rw_t, rb, seq=s, tm=512)

        x2d = _moe(x2d, hf_rows, eidx, wts, w_expert_gate, w_expert_up, w_expert_down, layer=l)

    return x2d.reshape(b, s, d)
```

```python
import functools
import math

import jax
import jax.numpy as jnp
import numpy as np
from jax import lax
from jax.experimental import pallas as pl
from jax.experimental.pallas import tpu as pltpu
from jax.experimental.pallas import tpu_sc as plsc

F32 = jnp.float32
BF16 = jnp.bfloat16

EPS = 1e-6
NEG = -1e30

MLSTM_HEADS = 4
MLSTM_DH = 128
MLSTM_W = MLSTM_HEADS * MLSTM_DH
CONV_WIDTH = 4
MLSTM_BLOCK = 128
MLSTM_NSUB = 1
MLSTM_GROUP = 2

ATTN_PATTERNS = ((128, 1), (512, 4), (2048, 16))
HEADS_PER_GROUP = 4
ATTN_DH = 64
ATTN_GW = HEADS_PER_GROUP * ATTN_DH
ATTN_W = len(ATTN_PATTERNS) * ATTN_GW
ATTN_BLOCK = 128
REL_BUCKETS = 32
REL_MAX_DIST = 2048

GMLP_GROUPS = 4
GMLP_GC = 128
GMLP_W = GMLP_GROUPS * GMLP_GC
GMLP_CHUNK = 128

XATTN_HEADS = 4
XATTN_DH = 128
XATTN_W = XATTN_HEADS * XATTN_DH

N_EXPERTS = 16
N_EXPERT_GROUPS = 4
EXPERTS_PER_GROUP = 4
ROUTER_ROWS = 8 * N_EXPERT_GROUPS

N_BRANCH = 3

MOE_TM = 512
ROW_CHUNKS = 4
SC_CORES, SC_SUBCORES = 2, 16
SC_WINDOW = 128

OFF_MQ, OFF_MK, OFF_MV, OFF_MO = 0, 512, 1024, 1536
OFF_GU, OFF_GV = 2048, 2560
OFF_GATE = 3072
OFF_IF = 6144
IF_PAD = 256
N_PROJ = OFF_IF + IF_PAD

ATTN_TILE = 2048
ATTN_SUB = ATTN_TILE // ATTN_BLOCK
ATTN_SLAB = 2 * ATTN_DH
ATTN_COLS = HEADS_PER_GROUP * ATTN_SLAB + 2 * ATTN_GW

VMEM_LIMIT = 48 * 1024 * 1024


def _cparams(*sem, flags=None):
    return pltpu.CompilerParams(dimension_semantics=sem, vmem_limit_bytes=VMEM_LIMIT, flags=flags)


def _rms(x, gain):
    return x * lax.rsqrt(jnp.mean(x * x, axis=-1, keepdims=True) + EPS) * gain


def _dot(a, b):
    return jnp.dot(a, b, preferred_element_type=F32)


def _dot_nt(a, b):
    return lax.dot_general(a, b, (((1,), (1,)), ((), ())), preferred_element_type=F32)


def _inproj_kernel(x_ref, g_ref, w_ref, o_ref, h_ref):
    @pl.when(pl.program_id(1) == 0)
    def _():
        h_ref[...] = _rms(x_ref[...], g_ref[...]).astype(BF16)

    o_ref[...] = _dot(h_ref[...], w_ref[...]).astype(o_ref.dtype)


def _inproj(x2d, gain, w, *, tm, tn):
    t, d = x2d.shape
    n = w.shape[1]
    return pl.pallas_call(
        _inproj_kernel,
        grid=(t // tm, n // tn),
        in_specs=[pl.BlockSpec((tm, d), lambda i, j: (i, 0)),
                  pl.BlockSpec((1, d), lambda i, j: (0, 0)),
                  pl.BlockSpec((d, tn), lambda i, j: (0, j))],
        out_specs=[pl.BlockSpec((tm, tn), lambda i, j: (i, j)),
                   pl.BlockSpec((tm, d), lambda i, j: (i, 0))],
        out_shape=[jax.ShapeDtypeStruct((t, n), BF16), jax.ShapeDtypeStruct((t, d), BF16)],
        compiler_params=_cparams("parallel", "arbitrary"),
        name="inproj",
    )(x2d, gain, w)


def _log_sigmoid(x):
    return jnp.minimum(x, 0.0) - jnp.log(1.0 + jnp.exp(-jnp.abs(x)))


def _mlstm_kernel(qk_ref, v_ref, og_ref, gc_ref, gr_ref, cw_ref, gbc_ref, gbr_ref, ng_ref, y_ref,
                  xe_scr, s_scr, m_scr, *, blk, nsub, group):
    heads, w = MLSTM_HEADS, MLSTM_W

    @pl.when(pl.program_id(1) == 0)
    def _():
        xe_scr[:, 0:8, :] = jnp.zeros((group, 8, 2 * w), F32)
        s_scr[...] = jnp.zeros_like(s_scr)
        m_scr[...] = jnp.zeros_like(m_scr)

    cw = cw_ref[...]
    ri = lax.broadcasted_iota(jnp.int32, (blk, blk), 0)
    ci = lax.broadcasted_iota(jnp.int32, (blk, blk), 1)
    causal = ri >= ci
    tril = causal.astype(BF16)
    triu = (ri <= ci).astype(BF16)
    states = []
    for g in range(group):
        xe_scr[g, 8:8 + nsub * blk, :] = qk_ref[g].astype(F32)
        states.append([(s_scr[g, h], m_scr[g, h:h + 1, 0:1]) for h in range(heads)])
    for c in range(nsub):
        for g in range(group):
            states[g] = _mlstm_chunk(c * blk, blk, states[g], cw, causal, tril, triu, xe_scr.at[g], v_ref.at[g],
                                     og_ref.at[g], gc_ref.at[g], gr_ref.at[g], gbc_ref, gbr_ref, ng_ref,
                                     y_ref.at[g])
    for g in range(group):
        xe_scr[g, 0:8, :] = xe_scr[g, nsub * blk:nsub * blk + 8, :]
        for h, (s_st, m_st) in enumerate(states[g]):
            s_scr[g, h] = s_st
            m_scr[g, h:h + 1, :] = jnp.broadcast_to(m_st, (1, m_scr.shape[2]))


def _split_bf16(x):
    hi = x.astype(BF16)
    return hi, (x - hi.astype(F32)).astype(BF16)


def _mlstm_chunk(r0, blk, state, cw, causal, tril, triu, xe_scr, v_ref, og_ref, gc_ref, gr_ref, gbc_ref,
                 gbr_ref, ng_ref, y_ref):
    heads, dh, w = MLSTM_HEADS, MLSTM_DH, MLSTM_W
    rows = slice(r0, r0 + blk)
    conv = cw[CONV_WIDTH - 1:CONV_WIDTH, :] * xe_scr[8 + r0:8 + r0 + blk, :]
    for j in range(CONV_WIDTH - 1):
        off = 8 + r0 - (CONV_WIDTH - 1) + j
        conv = conv + cw[j:j + 1, :] * xe_scr[off:off + blk, :]
    qk = conv * jax.nn.sigmoid(conv)

    gcol = gc_ref[rows, :].astype(F32) + gbc_ref[...]
    grow = gr_ref[:, rows] + gbr_ref[...]
    lc_hi, lc_lo = _split_bf16(_log_sigmoid(gcol))
    lr_hi, lr_lo = _split_bf16(_log_sigmoid(grow))
    bcol = _dot(tril, lc_hi) + _dot(tril, lc_lo)
    brow = _dot(lr_hi, triu) + _dot(lr_lo, triu)
    ones = jnp.ones((blk, dh), BF16)

    new_state = []
    for h in range(heads):
        sl = slice(h * dh, (h + 1) * dh)
        b_c = bcol[:, heads + h:heads + h + 1]
        i_c = gcol[:, h:h + 1]
        b_r = brow[heads + h:heads + h + 1, :]
        i_r = grow[h:h + 1, :]
        s_st, m_st = state[h]

        d_mat = jnp.where(causal, b_c - b_r + i_r, NEG)
        inter = b_c + m_st
        m_t = jnp.maximum(inter, jnp.max(d_mat, axis=-1, keepdims=True))
        w_intra = jnp.exp(d_mat - m_t)
        w_inter = jnp.exp(inter - m_t)

        q_f = qk[:, sl]
        k_f = qk[:, w + h * dh:w + (h + 1) * dh] * (dh ** -0.5)
        q_b = q_f.astype(BF16)
        k_b = k_f.astype(BF16)
        v_ext = jnp.concatenate([v_ref[rows, sl], ones], axis=-1)

        s = _dot_nt(q_b, k_b) * w_intra
        tot = _dot(s.astype(BF16), v_ext) + w_inter * _dot(q_b, s_st.astype(BF16))
        num, den = tot[:, :dh], tot[:, dh:]
        hh = num / jnp.maximum(jnp.abs(den), jnp.exp(-m_t))
        hn = _rms(hh, ng_ref[:, sl])
        y_ref[rows, sl] = (hn * jax.nn.sigmoid(og_ref[rows, sl].astype(F32))).astype(y_ref.dtype)

        b_last = b_c[blk - 1:blk, :]
        dec = b_last - b_c + i_c
        m_new = jnp.maximum(b_last + m_st, jnp.max(dec, axis=0, keepdims=True))
        w_k = jnp.exp(dec - m_new)
        w_c = jnp.exp(b_last + m_st - m_new)
        kw = k_f * w_k
        new_state.append((w_c * s_st + _dot(kw.T.astype(BF16), v_ext), m_new))
    return new_state


def _mlstm(proj, gates_row, conv_w, gb_col, gb_row, norm_g, *, batch, seq, blk, nsub, group):
    t, npj = proj.shape
    rows = blk * nsub
    w = MLSTM_W
    proj3 = proj.reshape(batch, seq, npj)
    cols = lambda c: (lambda b, i: (b, i, c))
    const2 = lambda b, i: (0, 0)
    y = pl.pallas_call(
        functools.partial(_mlstm_kernel, blk=blk, nsub=nsub, group=group),
        grid=(batch // group, seq // rows),
        in_specs=[pl.BlockSpec((group, rows, 2 * w), cols(OFF_MQ // (2 * w))),
                  pl.BlockSpec((group, rows, w), cols(OFF_MV // w)),
                  pl.BlockSpec((group, rows, w), cols(OFF_MO // w)),
                  pl.BlockSpec((group, rows, IF_PAD), cols(OFF_IF // IF_PAD)),
                  pl.BlockSpec((group, 8, rows), lambda b, i: (b, 0, i)),
                  pl.BlockSpec((CONV_WIDTH, 2 * w), const2),
                  pl.BlockSpec((1, IF_PAD), const2),
                  pl.BlockSpec((8, 1), const2),
                  pl.BlockSpec((1, w), const2)],
        out_specs=pl.BlockSpec((group, rows, w), cols(0)),
        out_shape=jax.ShapeDtypeStruct((batch, seq, w), BF16),
        scratch_shapes=[pltpu.VMEM((group, rows + 8, 2 * w), F32),
                        pltpu.VMEM((group, MLSTM_HEADS, MLSTM_DH, 2 * MLSTM_DH), F32),
                        pltpu.VMEM((group, 8, 128), F32)],
        compiler_params=_cparams("parallel", "arbitrary"),
        name="mlstm",
    )(proj3, proj3, proj3, proj3, gates_row, conv_w, gb_col, gb_row, norm_g)
    return y.reshape(t, w)


def _attnproj_kernel(h_ref, w_ref, seg_ref, gq_ref, gk_ref, o_ref, r_scr):
    j = pl.program_id(1)
    gw, half = ATTN_GW, ATTN_SLAB // 2
    sub_rows = 512

    def head_norm(x, gain):
        sq = x * x
        hi = sq.astype(BF16)
        lo = (sq - hi.astype(F32)).astype(BF16)
        ss = _dot(hi, seg_ref[...]) + _dot(lo, seg_ref[...])
        return x * lax.rsqrt(ss * (1.0 / ATTN_DH) + EPS) * gain

    low = lax.broadcasted_iota(jnp.int32, (1, ATTN_SLAB), 1) < half
    for s in range(ATTN_TILE // sub_rows):
        rows = slice(s * sub_rows, (s + 1) * sub_rows)
        res = _dot(h_ref[rows, :], w_ref[...])
        q = head_norm(res[:, :gw], gq_ref[...]) * (ATTN_DH ** -0.5)
        k = head_norm(res[:, gw:2 * gw], gk_ref[...])
        slabs = []
        for pair in range(gw // ATTN_SLAB):
            qp = q[:, pair * ATTN_SLAB:(pair + 1) * ATTN_SLAB]
            slabs += [jnp.where(low, qp, 0.0), jnp.where(low, 0.0, qp)]
        slabs += [k[:, c * 128:(c + 1) * 128] for c in range(gw // 128)]
        slabs += [res[:, 2 * gw + c * 128:2 * gw + (c + 1) * 128] for c in range(gw // 128)]
        for c, slab in enumerate(slabs):
            r_scr[c, rows, :] = slab

    for g, (_, dil) in enumerate(ATTN_PATTERNS):
        @pl.when(j == g)
        def _(dil=dil):
            seg = ATTN_TILE // dil
            for r in range(dil):
                for c in range(r_scr.shape[0]):
                    src = r_scr[c, pl.ds(r, seg, stride=dil), :] if dil > 1 else r_scr[c]
                    o_ref[r * seg:(r + 1) * seg, c * 128:(c + 1) * 128] = src.astype(o_ref.dtype)


def _attnproj(h, w, seg_ones, gq, gk):
    t, d = h.shape
    ng = len(ATTN_PATTERNS)
    wcols = 3 * ATTN_GW
    const2 = lambda i, j: (0, 0)
    return pl.pallas_call(
        _attnproj_kernel,
        grid=(t // ATTN_TILE, ng),
        in_specs=[pl.BlockSpec((ATTN_TILE, d), lambda i, j: (i, 0)),
                  pl.BlockSpec((d, wcols), lambda i, j: (0, j)),
                  pl.BlockSpec((ATTN_GW, ATTN_GW), const2),
                  pl.BlockSpec((1, ATTN_GW), const2), pl.BlockSpec((1, ATTN_GW), const2)],
        out_specs=pl.BlockSpec((ATTN_TILE, ATTN_COLS), lambda i, j: (i, j)),
        out_shape=jax.ShapeDtypeStruct((t, ng * ATTN_COLS), BF16),
        scratch_shapes=[pltpu.VMEM((ATTN_COLS // 128, ATTN_TILE, 128), F32)],
        compiler_params=_cparams("parallel", "arbitrary"),
        name="attnproj",
    )(h, w, seg_ones, gq, gk)


def _dattn_kernel(q_ref, kc_ref, kp_ref, vc_ref, vp_ref, bias_ref, o_ref, lse_ref,
                  kx_scr, vx_scr, o_scr, l_scr, *, dil):
    blk = ATTN_BLOCK
    per = ATTN_SUB // dil
    first_tile = pl.program_id(1) == 0
    for r in range(dil):
        base = r * (per + 1) * blk
        last = slice((r * per + per - 1) * blk, (r * per + per) * blk)
        mine = slice(r * per * blk, (r + 1) * per * blk)
        kx_scr[base:base + blk, :] = kp_ref[last, :]
        vx_scr[base:base + blk, :] = vp_ref[last, :]
        kx_scr[base + blk:base + (per + 1) * blk, :] = kc_ref[mine, :]
        vx_scr[base + blk:base + (per + 1) * blk, :] = vc_ref[mine, :]

    low = lax.broadcasted_iota(jnp.int32, (1, ATTN_SLAB), 1) < ATTN_SLAB // 2
    no_prev = lax.broadcasted_iota(jnp.int32, (1, 2 * blk), 1) < blk
    for r in range(dil):
        for sub in range(per):
            u = r * per + sub
            win = slice((r * (per + 1) + sub) * blk, (r * (per + 1) + sub + 2) * blk)
            o_slabs, l_slabs = [], []
            for pair in range(ATTN_GW // ATTN_SLAB):
                cols = slice(pair * ATTN_SLAB, (pair + 1) * ATTN_SLAB)
                kx, vx = kx_scr[win, cols], vx_scr[win, cols]
                o_pair, l_pair = [], []
                for h in (2 * pair, 2 * pair + 1):
                    logits = _dot_nt(q_ref[u * blk:(u + 1) * blk, h * ATTN_SLAB:(h + 1) * ATTN_SLAB], kx)
                    logits = logits + bias_ref[h]
                    if sub == 0:
                        logits = jnp.where(first_tile & no_prev, NEG, logits)
                    m = jnp.max(logits, axis=-1, keepdims=True)
                    p = jnp.exp(logits - m)
                    l = jnp.sum(p, axis=-1, keepdims=True)
                    o_pair.append(_dot(p.astype(BF16), vx) / l)
                    l_pair.append(m + jnp.log(l))
                o_slabs.append(jnp.where(low, o_pair[0], o_pair[1]))
                l_slabs.append(jnp.where(low, l_pair[0], l_pair[1]))
            dst = pl.ds(sub * blk * dil + r, blk, stride=dil) if dil > 1 else slice(u * blk, (u + 1) * blk)
            for c in range(ATTN_GW // ATTN_SLAB):
                o_scr[c, dst, :] = o_slabs[c]
                l_scr[c, dst, :] = l_slabs[c]
    for c in range(ATTN_GW // ATTN_SLAB):
        o_ref[:, c * ATTN_SLAB:(c + 1) * ATTN_SLAB] = o_scr[c].astype(o_ref.dtype)
        lse_ref[:, c * ATTN_SLAB:(c + 1) * ATTN_SLAB] = l_scr[c]


def _dattn(aproj, bias, *, seq, group, dilation):
    t = aproj.shape[0]
    tiles = seq // ATTN_TILE
    qw = HEADS_PER_GROUP * ATTN_SLAB
    cq = group * ATTN_COLS // qw
    ck, cv = (group * ATTN_COLS + qw) // ATTN_GW, (group * ATTN_COLS + qw) // ATTN_GW + 1
    blk = (ATTN_TILE, ATTN_GW)
    cur = lambda c: (lambda b, j: (b * tiles + j, c))
    prev = lambda c: (lambda b, j: (b * tiles + jnp.maximum(j - 1, 0), c))
    xrows = ATTN_TILE + dilation * ATTN_BLOCK
    return pl.pallas_call(
        functools.partial(_dattn_kernel, dil=dilation),
        grid=(t // seq, tiles),
        in_specs=[pl.BlockSpec((ATTN_TILE, qw), cur(cq)),
                  pl.BlockSpec(blk, cur(ck)), pl.BlockSpec(blk, prev(ck)),
                  pl.BlockSpec(blk, cur(cv)), pl.BlockSpec(blk, prev(cv)),
                  pl.BlockSpec((HEADS_PER_GROUP, ATTN_BLOCK, 2 * ATTN_BLOCK), lambda b, j: (0, 0, 0))],
        out_specs=[pl.BlockSpec(blk, cur(0)), pl.BlockSpec(blk, cur(0))],
        out_shape=[jax.ShapeDtypeStruct((t, ATTN_GW), BF16), jax.ShapeDtypeStruct((t, ATTN_GW), F32)],
        scratch_shapes=[pltpu.VMEM((xrows, ATTN_GW), BF16), pltpu.VMEM((xrows, ATTN_GW), BF16),
                        pltpu.VMEM((ATTN_GW // ATTN_SLAB, ATTN_TILE, ATTN_SLAB), F32),
                        pltpu.VMEM((ATTN_GW // ATTN_SLAB, ATTN_TILE, ATTN_SLAB), F32)],
        compiler_params=_cparams("parallel", "arbitrary"),
        name=f"dattn{group}",
    )(aproj, aproj, aproj, aproj, aproj, bias)


def _rel_bucket(n):
    max_exact = REL_BUCKETS // 2
    nf = jnp.maximum(n, 1).astype(F32)
    log_b = max_exact + (jnp.log(nf / max_exact) / math.log(REL_MAX_DIST / max_exact)
                         * (REL_BUCKETS - max_exact)).astype(jnp.int32)
    return jnp.where(n < max_exact, n, jnp.minimum(log_b, REL_BUCKETS - 1))


def _attn_bias(rel_bias, group):
    window, dilation = ATTN_PATTERNS[group]
    steps = window // dilation
    hp = lax.Precision.HIGHEST
    hs = slice(group * HEADS_PER_GROUP, (group + 1) * HEADS_PER_GROUP)
    bucket = _rel_bucket(jnp.arange(steps + 1) * dilation)
    bias_steps = jnp.dot(jax.nn.one_hot(bucket, REL_BUCKETS, dtype=F32), rel_bias[:, hs].astype(F32),
                         precision=hp)
    qi = jnp.arange(ATTN_BLOCK)[:, None]
    ki = jnp.arange(2 * ATTN_BLOCK)[None, :]
    dist = ATTN_BLOCK + qi - ki
    ok = (dist >= 0) & (dist <= steps)
    sel = jax.nn.one_hot(jnp.clip(dist, 0, steps).reshape(-1), steps + 1, dtype=F32)
    bias = jnp.dot(sel, bias_steps, precision=hp).T.reshape(HEADS_PER_GROUP, ATTN_BLOCK, 2 * ATTN_BLOCK)
    return jnp.where(ok[None], bias, NEG)


def _merge_kernel(ya_ref, yb0_ref, yb1_ref, yb2_ref, l0_ref, l1_ref, l2_ref, gu_ref, gv_ref, gate_ref,
                  x_ref, wa_ref, wb_ref, wc_ref, wo_ref, ws_ref, bs_ref, gg_ref, o_ref, yc_scr, *, tm):
    d = x_ref.shape[1]
    l0, l1, l2 = l0_ref[...], l1_ref[...], l2_ref[...]
    mx = jnp.maximum(jnp.maximum(l0, l1), l2)
    e0, e1, e2 = jnp.exp(l0 - mx), jnp.exp(l1 - mx), jnp.exp(l2 - mx)
    inv = 1.0 / (e0 + e1 + e2)
    yb = jnp.concatenate([(yb0_ref[...].astype(F32) * (e0 * inv)).astype(BF16),
                          (yb1_ref[...].astype(F32) * (e1 * inv)).astype(BF16),
                          (yb2_ref[...].astype(F32) * (e2 * inv)).astype(BF16)], axis=-1)

    for j in range(tm // GMLP_CHUNK):
        rows = slice(j * GMLP_CHUNK, (j + 1) * GMLP_CHUNK)
        for g in range(GMLP_GROUPS):
            cols = slice(g * GMLP_GC, (g + 1) * GMLP_GC)
            u = jax.nn.gelu(gu_ref[rows, cols].astype(F32))
            v = _rms(jax.nn.gelu(gv_ref[rows, cols].astype(F32)), gg_ref[:, cols])
            mixed = _dot(ws_ref[g], v.astype(BF16)) + bs_ref[g]
            yc_scr[rows, cols] = (u * mixed).astype(BF16)

    merged = jax.nn.sigmoid(gate_ref[:, 0:d].astype(F32)) * _dot(ya_ref[...], wa_ref[...])
    merged = merged + jax.nn.sigmoid(gate_ref[:, d:2 * d].astype(F32)) * _dot(yb, wb_ref[...])
    merged = merged + jax.nn.sigmoid(gate_ref[:, 2 * d:3 * d].astype(F32)) * _dot(yc_scr[...], wc_ref[...])
    o_ref[...] = x_ref[...] + _dot(merged.astype(BF16), wo_ref[...])


def _merge(ya, ybs, lses, proj, x2d, wa, wb, wc, wo, ws, bsb, gg, *, tm):
    t, d = x2d.shape
    row = lambda c: (lambda i: (i, c))
    full2 = lambda i: (0, 0)
    full3 = lambda i: (0, 0, 0)
    gspec = pl.BlockSpec((tm, ATTN_GW), row(0))
    return pl.pallas_call(
        functools.partial(_merge_kernel, tm=tm),
        grid=(t // tm,),
        in_specs=[pl.BlockSpec((tm, MLSTM_W), row(0)),
                  gspec, gspec, gspec, gspec, gspec, gspec,
                  pl.BlockSpec((tm, GMLP_W), row(OFF_GU // GMLP_W)),
                  pl.BlockSpec((tm, GMLP_W), row(OFF_GV // GMLP_W)),
                  pl.BlockSpec((tm, N_BRANCH * d), row(OFF_GATE // (N_BRANCH * d))),
                  pl.BlockSpec((tm, d), row(0)),
                  pl.BlockSpec(wa.shape, full2), pl.BlockSpec(wb.shape, full2),
                  pl.BlockSpec(wc.shape, full2), pl.BlockSpec(wo.shape, full2),
                  pl.BlockSpec(ws.shape, full3), pl.BlockSpec(bsb.shape, full3),
                  pl.BlockSpec(gg.shape, full2)],
        out_specs=pl.BlockSpec((tm, d), row(0)),
        out_shape=jax.ShapeDtypeStruct((t, d), F32),
        scratch_shapes=[pltpu.VMEM((tm, GMLP_W), BF16)],
        compiler_params=_cparams("parallel"),
        name="merge",
    )(ya, *ybs, *lses, proj, proj, proj, x2d, wa, wb, wc, wo, ws, bsb, gg)


def _memkv_kernel(mem_ref, g_ref, w_ref, gk_ref, k_ref, v_ref):
    dh, w = XATTN_DH, XATTN_W
    kv = _dot(_rms(mem_ref[0], g_ref[...]).astype(BF16), w_ref[...])
    for h in range(XATTN_HEADS):
        sl = slice(h * dh, (h + 1) * dh)
        k_ref[0, :, sl] = _rms(kv[:, sl], gk_ref[...]).astype(k_ref.dtype)
    v_ref[0] = kv[:, w:].astype(v_ref.dtype)


def _memkv(mem, gain, w_kv, gk):
    b, m, d = mem.shape
    full2 = lambda i: (0, 0)
    return pl.pallas_call(
        _memkv_kernel,
        grid=(b,),
        in_specs=[pl.BlockSpec((1, m, d), lambda i: (i, 0, 0)),
                  pl.BlockSpec((1, d), full2),
                  pl.BlockSpec(w_kv.shape, full2),
                  pl.BlockSpec((1, XATTN_DH), full2)],
        out_specs=[pl.BlockSpec((1, m, XATTN_W), lambda i: (i, 0, 0)),
                   pl.BlockSpec((1, m, XATTN_W), lambda i: (i, 0, 0))],
        out_shape=[jax.ShapeDtypeStruct((b, m, XATTN_W), BF16),
                   jax.ShapeDtypeStruct((b, m, XATTN_W), BF16)],
        compiler_params=_cparams("parallel"),
        name="memkv",
    )(mem, gain, w_kv, gk)


def _route(logits):
    tm = logits.shape[1]
    e = jnp.exp(logits - jnp.max(logits, axis=0, keepdims=True))
    probs = e / jnp.sum(e, axis=0, keepdims=True)
    rowi = lax.broadcasted_iota(jnp.int32, (8, tm), 0)
    real = rowi < EXPERTS_PER_GROUP
    tops = []
    for g in range(N_EXPERT_GROUPS):
        pg = jnp.where(real, probs[8 * g:8 * g + 8, :], -0.5)
        m1 = jnp.max(pg, axis=0, keepdims=True)
        i1 = jnp.min(jnp.where(pg == m1, rowi, 8), axis=0, keepdims=True)
        pg2 = jnp.where(rowi == i1, -1.0, pg)
        m2 = jnp.max(pg2, axis=0, keepdims=True)
        i2 = jnp.min(jnp.where(pg2 == m2, rowi, 8), axis=0, keepdims=True)
        tops.append((m1, i1, m2, i2))
    best = jnp.zeros((1, tm), jnp.int32)
    best_score = tops[0][0] + tops[0][2]
    for g in range(1, N_EXPERT_GROUPS):
        score = tops[g][0] + tops[g][2]
        better = score > best_score
        best = jnp.where(better, g, best)
        best_score = jnp.where(better, score, best_score)
    m1, i1, m2, i2 = tops[0]
    for g in range(1, N_EXPERT_GROUPS):
        m1, i1, m2, i2 = (jnp.where(best == g, new, old) for new, old in zip(tops[g], (m1, i1, m2, i2)))
    tot = m1 + m2
    base = best * EXPERTS_PER_GROUP
    return base + i1, base + i2, m1 / tot, m2 / tot


def _pack_bf16_pairs(x):
    n = x.shape[1] // 2
    hi = lax.bitcast_convert_type(x[:, :n].astype(BF16).astype(F32), jnp.uint32)
    lo = lax.bitcast_convert_type(x[:, n:].astype(BF16).astype(F32), jnp.uint32)
    return hi | (lo >> 16)


def _unpack_bf16_pairs(p):
    hi = lax.bitcast_convert_type(p & jnp.uint32(0xFFFF0000), F32)
    lo = lax.bitcast_convert_type(p << 16, F32)
    return hi, lo


def _store_row_chunks(ref, packed):
    for j in range(ROW_CHUNKS):
        ref[j] = packed[:, j * 128:(j + 1) * 128]


def _load_row_chunks(ref):
    return jnp.concatenate([ref[j] for j in range(ROW_CHUNKS)], axis=-1)


def _xattn_kernel(x_ref, k_ref, v_ref, gx_ref, wq_ref, gq_ref, wo_ref, gf_ref, rw_ref, rb_ref,
                  xo_ref, hf_ref, eidx_ref, wts_ref):
    dh = XATTN_DH
    x = x_ref[...]
    q = _dot(_rms(x, gx_ref[...]).astype(BF16), wq_ref[...])
    outs = []
    for h in range(XATTN_HEADS):
        sl = slice(h * dh, (h + 1) * dh)
        q_h = (_rms(q[:, sl], gq_ref[...]) * (dh ** -0.5)).astype(BF16)
        logits = _dot_nt(q_h, k_ref[0, :, sl])
        p = jnp.exp(logits - jnp.max(logits, axis=-1, keepdims=True))
        o = _dot(p.astype(BF16), v_ref[0, :, sl]) / jnp.sum(p, axis=-1, keepdims=True)
        outs.append(o.astype(BF16))
    xn = x + _dot(jnp.concatenate(outs, axis=-1), wo_ref[...])
    xo_ref[...] = xn
    hf = _rms(xn, gf_ref[...])
    _store_row_chunks(hf_ref, _pack_bf16_pairs(hf))
    rw = rw_ref[...]
    rw_hi = rw.astype(BF16)
    rw_lo = (rw - rw_hi.astype(F32)).astype(BF16)
    hf_hi = hf.astype(BF16)
    hf_lo = (hf - hf_hi.astype(F32)).astype(BF16)
    logits_t = _dot_nt(rw_hi, hf_hi) + _dot_nt(rw_hi, hf_lo) + _dot_nt(rw_lo, hf_hi) + rb_ref[...]
    e1, e2, w1, w2 = _route(logits_t)
    tm = x.shape[0]
    eidx_ref[...] = jnp.concatenate([e1, e2, jnp.zeros((6, tm), jnp.int32)], axis=0)
    wts_ref[...] = jnp.concatenate([w1, w2, jnp.zeros((6, tm), F32)], axis=0)


def _xattn(x2d, k, v, gx, wq, gq, wo, gf, rw_t, rb, *, seq, tm):
    t, d = x2d.shape
    per_b = seq // tm
    full2 = lambda i: (0, 0)
    kv_spec = pl.BlockSpec((1,) + k.shape[1:], lambda i: (i // per_b, 0, 0))
    return pl.pallas_call(
        _xattn_kernel,
        grid=(t // tm,),
        in_specs=[pl.BlockSpec((tm, d), lambda i: (i, 0)), kv_spec, kv_spec,
                  pl.BlockSpec((1, d), full2), pl.BlockSpec(wq.shape, full2),
                  pl.BlockSpec((1, XATTN_DH), full2), pl.BlockSpec(wo.shape, full2),
                  pl.BlockSpec((1, d), full2), pl.BlockSpec(rw_t.shape, full2),
                  pl.BlockSpec(rb.shape, full2)],
        out_specs=[pl.BlockSpec((tm, d), lambda i: (i, 0)),
                   pl.BlockSpec((ROW_CHUNKS, tm, 128), lambda i: (0, i, 0)),
                   pl.BlockSpec((8, tm), lambda i: (0, i)),
                   pl.BlockSpec((8, tm), lambda i: (0, i))],
        out_shape=[jax.ShapeDtypeStruct((t, d), F32),
                   jax.ShapeDtypeStruct((ROW_CHUNKS, t, 128), jnp.uint32),
                   jax.ShapeDtypeStruct((8, t), jnp.int32),
                   jax.ShapeDtypeStruct((8, t), F32)],
        compiler_params=_cparams("parallel"),
        name="xattn_router",
    )(x2d, k, v, gx, wq, gq, wo, gf, rw_t, rb)


def _moe_plan_kernel(eidx_ref, i1_ref, i2_ref, te_ref, na_ref, cnt_scr, carry_scr, *, tb, tm, plane_rows):
    ne = N_EXPERTS
    hp = lax.Precision.HIGHEST
    phase, j = pl.program_id(0), pl.program_id(1)
    rows = lax.broadcasted_iota(jnp.int32, (ne, tb), 0)
    oh1 = rows == eidx_ref[0:1, :]
    oh2 = rows == eidx_ref[1:2, :]
    a = oh1.astype(F32) + oh2.astype(F32)
    blk_cnt = jnp.broadcast_to(jnp.sum(a, axis=1, keepdims=True), cnt_scr.shape)

    @pl.when((phase == 0) & (j == 0))
    def _():
        cnt_scr[...] = jnp.zeros_like(cnt_scr)

    @pl.when(phase == 0)
    def _():
        cnt_scr[...] += blk_cnt

    @pl.when((phase == 1) & (j == 0))
    def _():
        padded = jnp.ceil(cnt_scr[...] * (1.0 / tm)) * tm
        er = lax.broadcasted_iota(jnp.int32, (ne, ne), 0)
        ec = lax.broadcasted_iota(jnp.int32, (ne, ne), 1)
        off = jnp.dot((ec < er).astype(F32), padded, precision=hp, preferred_element_type=F32)
        carry_scr[...] = off
        seg_end = (off + padded)[:, 0:1]
        tile_start = lax.broadcasted_iota(jnp.int32, (ne, te_ref.shape[1]), 1).astype(F32) * tm
        te = jnp.sum((seg_end <= tile_start).astype(F32), axis=0, keepdims=True)
        te_ref[...] = jnp.broadcast_to(jnp.minimum(te, ne - 1.0), te_ref.shape).astype(jnp.int32)
        total = jnp.sum(padded[:, 0:1], axis=0, keepdims=True)
        na_ref[...] = jnp.broadcast_to(total * (1.0 / tm), na_ref.shape).astype(jnp.int32)

    @pl.when(phase == 1)
    def _():
        before = (lax.broadcasted_iota(jnp.int32, (tb, tb), 0)
                  < lax.broadcasted_iota(jnp.int32, (tb, tb), 1)).astype(BF16)
        rank = carry_scr[:, 0:1] + _dot(a.astype(BF16), before)
        d1 = jnp.sum(jnp.where(oh1, rank, 0.0), axis=0, keepdims=True).astype(jnp.int32)
        d2 = jnp.sum(jnp.where(oh2, rank, 0.0), axis=0, keepdims=True).astype(jnp.int32)
        plane = lax.broadcasted_iota(jnp.int32, (8, tb), 0) * plane_rows
        i1_ref[...] = jnp.where(plane < ROW_CHUNKS * plane_rows, plane + d1, 0)
        i2_ref[...] = jnp.where(plane < ROW_CHUNKS * plane_rows, plane + d2, 0)
        carry_scr[...] += blk_cnt


def _moe_plan(eidx, *, tm, n_tiles, tb=512):
    t = eidx.shape[1]
    ntp = -(-n_tiles // 128) * 128
    return pl.pallas_call(
        functools.partial(_moe_plan_kernel, tb=tb, tm=tm, plane_rows=n_tiles * tm),
        grid=(2, t // tb),
        in_specs=[pl.BlockSpec((8, tb), lambda p, j: (0, j))],
        out_specs=[pl.BlockSpec((8, tb), lambda p, j: (0, j * p)),
                   pl.BlockSpec((8, tb), lambda p, j: (0, j * p)),
                   pl.BlockSpec((8, ntp), lambda p, j: (0, 0)),
                   pl.BlockSpec((8, 128), lambda p, j: (0, 0))],
        out_shape=[jax.ShapeDtypeStruct((8, t), jnp.int32),
                   jax.ShapeDtypeStruct((8, t), jnp.int32),
                   jax.ShapeDtypeStruct((8, ntp), jnp.int32),
                   jax.ShapeDtypeStruct((8, 128), jnp.int32)],
        scratch_shapes=[pltpu.VMEM((N_EXPERTS, 128), F32), pltpu.VMEM((N_EXPERTS, 128), F32)],
        compiler_params=_cparams("arbitrary", "arbitrary"),
        name="moe_plan",
    )(eidx)


def _sc_mesh():
    return plsc.VectorSubcoreMesh(core_axis_name="c", subcore_axis_name="s",
                                  num_cores=SC_CORES, num_subcores=SC_SUBCORES)


def _sc_index_spec(tokens):
    nb = tokens // SC_WINDOW
    return pl.BlockSpec((1, SC_WINDOW), lambda i: (i // nb, i % nb))


def _sc_dispatch(rows, i1, i2, n_out):
    n = rows.shape[0]
    tokens = i1.shape[1]

    @functools.partial(pl.kernel, out_type=jax.ShapeDtypeStruct((n_out, 128), rows.dtype), mesh=_sc_mesh(),
                       name="moe_dispatch")
    def k(x_hbm, i1_hbm, i2_hbm, o_hbm):
        def body(x_vmem, i1_vmem, i2_vmem):
            pltpu.sync_copy(x_vmem, o_hbm.at[i1_vmem.at[0]])
            pltpu.sync_copy(x_vmem, o_hbm.at[i2_vmem.at[0]])

        pltpu.emit_pipeline(
            body, grid=(n // SC_WINDOW,),
            in_specs=[pl.BlockSpec((SC_WINDOW, 128), lambda i: (i, 0)),
                      _sc_index_spec(tokens), _sc_index_spec(tokens)],
            out_specs=[],
            core_axis_name=("c", "s"), dimension_semantics=(pltpu.PARALLEL,),
        )(x_hbm, i1_hbm, i2_hbm)

    return k(rows, i1, i2)


def _sc_collect(table, i1, i2):
    tokens = i1.shape[1]
    n = ROW_CHUNKS * tokens
    out = jax.ShapeDtypeStruct((n, 128), table.dtype)

    @functools.partial(pl.kernel, out_type=(out, out), mesh=_sc_mesh(), name="moe_collect")
    def k(t_hbm, i1_hbm, i2_hbm, o1_hbm, o2_hbm):
        def body(i1_vmem, i2_vmem, o1_vmem, o2_vmem):
            pltpu.sync_copy(t_hbm.at[i1_vmem.at[0]], o1_vmem)
            pltpu.sync_copy(t_hbm.at[i2_vmem.at[0]], o2_vmem)

        pltpu.emit_pipeline(
            body, grid=(n // SC_WINDOW,),
            in_specs=[_sc_index_spec(tokens), _sc_index_spec(tokens)],
            out_specs=[pl.BlockSpec((SC_WINDOW, 128), lambda i: (i, 0)),
                       pl.BlockSpec((SC_WINDOW, 128), lambda i: (i, 0))],
            core_axis_name=("c", "s"), dimension_semantics=(pltpu.PARALLEL,),
        )(i1_hbm, i2_hbm, o1_hbm, o2_hbm)

    return k(table, i1, i2)


def _experts_kernel(te_ref, na_ref, xs_ref, wg_ref, wu_ref, wd_ref, y_ref, wg_scr, wu_scr, wd_scr):
    i = pl.program_id(0)
    active = i < na_ref[0]

    @pl.when(active & ((i == 0) | (te_ref[i] != te_ref[jnp.maximum(i - 1, 0)])))
    def _():
        wg_scr[...] = wg_ref[0, 0].astype(BF16)
        wu_scr[...] = wu_ref[0, 0].astype(BF16)
        wd_scr[...] = wd_ref[0, 0].astype(BF16)

    @pl.when(active)
    def _():
        hi, lo = _unpack_bf16_pairs(_load_row_chunks(xs_ref))
        h = jnp.concatenate([hi, lo], axis=-1).astype(BF16)
        up = _dot(h, wg_scr[...])
        act = up * jax.nn.sigmoid(up) * _dot(h, wu_scr[...])
        _store_row_chunks(y_ref, _pack_bf16_pairs(_dot(act.astype(BF16), wd_scr[...])))


def _experts(tile_expert, n_active, xs, wg, wu, wd, *, layer, tm):
    n_tiles = tile_expert.shape[0]
    _, _, d, dff = wg.shape
    rows = lambda i, te, na: (0, jnp.minimum(i, na[0] - 1), 0)
    expert = lambda i, te, na: (layer, te[i], 0, 0)
    return pl.pallas_call(
        _experts_kernel,
        grid_spec=pltpu.PrefetchScalarGridSpec(
            num_scalar_prefetch=2,
            grid=(n_tiles,),
            in_specs=[pl.BlockSpec((ROW_CHUNKS, tm, 128), rows),
                      pl.BlockSpec((1, 1, d, dff), expert),
                      pl.BlockSpec((1, 1, d, dff), expert),
                      pl.BlockSpec((1, 1, dff, d), expert)],
            out_specs=pl.BlockSpec((ROW_CHUNKS, tm, 128), rows),
            scratch_shapes=[pltpu.VMEM((d, dff), BF16), pltpu.VMEM((d, dff), BF16), pltpu.VMEM((dff, d), BF16)]),
        out_shape=jax.ShapeDtypeStruct(xs.shape, xs.dtype),
        compiler_params=_cparams("arbitrary"),
        name="moe_experts",
    )(tile_expert, n_active, xs, wg, wu, wd)


def _moe_combine_kernel(x_ref, y1_ref, y2_ref, w_ref, o_ref):
    half = x_ref.shape[1] // 2
    hi1, lo1 = _unpack_bf16_pairs(_load_row_chunks(y1_ref))
    hi2, lo2 = _unpack_bf16_pairs(_load_row_chunks(y2_ref))
    w1, w2 = w_ref[:, 0:1], w_ref[:, 1:2]
    o_ref[:, :half] = x_ref[:, :half] + w1 * hi1 + w2 * hi2
    o_ref[:, half:] = x_ref[:, half:] + w1 * lo1 + w2 * lo2


def _moe_combine(x2d, y1, y2, wcol, *, tm):
    t, d = x2d.shape
    chunk_spec = pl.BlockSpec((ROW_CHUNKS, tm, 128), lambda i: (0, i, 0))
    return pl.pallas_call(
        _moe_combine_kernel,
        grid=(t // tm,),
        in_specs=[pl.BlockSpec((tm, d), lambda i: (i, 0)), chunk_spec, chunk_spec,
                  pl.BlockSpec((tm, wcol.shape[1]), lambda i: (i, 0))],
        out_specs=pl.BlockSpec((tm, d), lambda i: (i, 0)),
        out_shape=jax.ShapeDtypeStruct((t, d), F32),
        compiler_params=_cparams("parallel"),
        name="moe_combine",
    )(x2d, y1, y2, wcol)


def _moe(x2d, hf_rows, eidx, wts, wg, wu, wd, *, layer):
    t = x2d.shape[0]
    tm = MOE_TM
    n_tiles = 2 * t // tm + N_EXPERTS
    plane = n_tiles * tm
    i1, i2, te, na = _moe_plan(eidx, tm=tm, n_tiles=n_tiles)
    xs = _sc_dispatch(hf_rows.reshape(ROW_CHUNKS * t, 128), i1, i2, ROW_CHUNKS * plane)
    ys = _experts(te[0, :n_tiles], na[0, :1], xs.reshape(ROW_CHUNKS, plane, 128), wg, wu, wd,
                  layer=layer, tm=tm)
    y1, y2 = _sc_collect(ys.reshape(ROW_CHUNKS * plane, 128), i1, i2)
    return _moe_combine(x2d, y1.reshape(ROW_CHUNKS, t, 128), y2.reshape(ROW_CHUNKS, t, 128), wts[:2].T, tm=512)


def _layout_w_in(w):
    sizes = (MLSTM_W, MLSTM_W, MLSTM_W, MLSTM_W, MLSTM_HEADS, MLSTM_HEADS,
             ATTN_W, ATTN_W, ATTN_W, GMLP_W, GMLP_W, N_BRANCH * w.shape[0])
    pts = np.cumsum(sizes)[:-1]
    mq, mk, mv, mo, mi, mf, aq, ak, av, gu, gv, gate = jnp.split(w, pts, axis=-1)
    pad = jnp.zeros((w.shape[0], IF_PAD - 2 * MLSTM_HEADS), w.dtype)
    main = jnp.concatenate([mq, mk, mv, mo, gu, gv, gate, mi, mf, pad], axis=-1).astype(BF16)
    attn = jnp.concatenate([a[:, g * ATTN_GW:(g + 1) * ATTN_GW] for g in range(len(ATTN_PATTERNS))
                            for a in (aq, ak, av)], axis=-1).astype(BF16)
    return main, attn


def kernel(x, mem, norm_mix, w_in, mlstm_conv, mlstm_gate_b, mlstm_norm, attn_qk_norm, gmlp_norm, gmlp_ws,
           gmlp_bs, w_branch_a, w_branch_b, w_branch_c, w_out, rel_bias, norm_xattn, norm_mem, w_xq, w_xkv,
           xattn_qk_norm, w_xo, norm_ffn, router_w, router_b, w_expert_gate, w_expert_up, w_expert_down):
    b, s, d = x.shape
    t = b * s
    depth = w_in.shape[0]
    x2d = x.reshape(t, d)

    biases = [_attn_bias(rel_bias, g) for g in range(len(ATTN_PATTERNS))]
    rw_t = jnp.zeros((N_EXPERT_GROUPS, 8, d), F32).at[:, :EXPERTS_PER_GROUP].set(
        router_w.T.reshape(N_EXPERT_GROUPS, EXPERTS_PER_GROUP, d)).reshape(ROUTER_ROWS, d)
    rb = jnp.full((N_EXPERT_GROUPS, 8), NEG, F32).at[:, :EXPERTS_PER_GROUP].set(
        router_b.astype(F32).reshape(N_EXPERT_GROUPS, EXPERTS_PER_GROUP)).reshape(ROUTER_ROWS, 1)
    tril = jnp.tril(jnp.ones((GMLP_CHUNK, GMLP_CHUNK), bool))
    head_of = jnp.arange(ATTN_GW) // ATTN_DH
    seg_ones = (head_of[:, None] == head_of[None, :]).astype(BF16)

    for l in range(depth):
        w_main, w_attn = _layout_w_in(w_in[l])
        proj, h_mix = _inproj(x2d, norm_mix[l][None], w_main, tm=1024, tn=1280)
        gq = jnp.tile(attn_qk_norm[l, 0], HEADS_PER_GROUP)[None]
        gk = jnp.tile(attn_qk_norm[l, 1], HEADS_PER_GROUP)[None]
        aproj = _attnproj(h_mix, w_attn, seg_ones, gq, gk)

        gates_row = proj[:, OFF_IF:OFF_IF + 8].astype(F32).reshape(b, s, 8).transpose(0, 2, 1)
        gb_col = jnp.zeros((1, IF_PAD), F32).at[0, :8].set(mlstm_gate_b[l])
        ya = _mlstm(proj, gates_row, mlstm_conv[l], gb_col, mlstm_gate_b[l].reshape(8, 1),
                    mlstm_norm[l][None], batch=b, seq=s, blk=MLSTM_BLOCK, nsub=MLSTM_NSUB,
                    group=MLSTM_GROUP)

        ybs, lses = [], []
        for g, (_, dilation) in enumerate(ATTN_PATTERNS):
            o, lse = _dattn(aproj, biases[g], seq=s, group=g, dilation=dilation)
            ybs.append(o)
            lses.append(lse)

        ws = jnp.where(tril, gmlp_ws[l], 0.0).astype(BF16)
        bsb = jnp.broadcast_to(gmlp_bs[l][:, :, None], (GMLP_GROUPS, GMLP_CHUNK, GMLP_GC)).astype(F32)
        x2d = _merge(ya, ybs, lses, proj, x2d, w_branch_a[l].astype(BF16), w_branch_b[l].astype(BF16),
                     w_branch_c[l].astype(BF16), w_out[l].astype(BF16), ws, bsb, gmlp_norm[l][None], tm=256)

        k_mem, v_mem = _memkv(mem, norm_mem[l][None], w_xkv[l].astype(BF16), xattn_qk_norm[l, 1][None])
        x2d, hf_rows, eidx, wts = _xattn(x2d, k_mem, v_mem, norm_xattn[l][None], w_xq[l].astype(BF16),
                                         xattn_qk_norm[l, 0][None], w_xo[l].astype(BF16), norm_ffn[l][None],
                                         rw_t, rb, seq=s, tm=512)

        x2d = _moe(x2d, hf_rows, eidx, wts, w_expert_gate, w_expert_up, w_expert_down, layer=l)

    return x2d.reshape(b, s, d)
```

```python
import functools
import math

import jax
import jax.numpy as jnp
import numpy as np
from jax import lax
from jax.experimental import pallas as pl
from jax.experimental.pallas import tpu as pltpu
from jax.experimental.pallas import tpu_sc as plsc

F32 = jnp.float32
BF16 = jnp.bfloat16

EPS = 1e-6
NEG = -1e30

MLSTM_HEADS = 4
MLSTM_DH = 128
MLSTM_W = MLSTM_HEADS * MLSTM_DH
CONV_WIDTH = 4
MLSTM_BLOCK = 128
MLSTM_NSUB = 1
MLSTM_GROUP = 2

ATTN_PATTERNS = ((128, 1), (512, 4), (2048, 16))
HEADS_PER_GROUP = 4
ATTN_DH = 64
ATTN_GW = HEADS_PER_GROUP * ATTN_DH
ATTN_W = len(ATTN_PATTERNS) * ATTN_GW
ATTN_BLOCK = 128
REL_BUCKETS = 32
REL_MAX_DIST = 2048

GMLP_GROUPS = 4
GMLP_GC = 128
GMLP_W = GMLP_GROUPS * GMLP_GC
GMLP_CHUNK = 128

XATTN_HEADS = 4
XATTN_DH = 128
XATTN_W = XATTN_HEADS * XATTN_DH

N_EXPERTS = 16
N_EXPERT_GROUPS = 4
EXPERTS_PER_GROUP = 4
ROUTER_ROWS = 8 * N_EXPERT_GROUPS

N_BRANCH = 3

MOE_TM = 512
ROW_CHUNKS = 4
SC_CORES, SC_SUBCORES = 2, 16
SC_WINDOW = 128

OFF_MQ, OFF_MK, OFF_MV, OFF_MO = 0, 512, 1024, 1536
OFF_GU, OFF_GV = 2048, 2560
OFF_GATE = 3072
OFF_IF = 6144
IF_PAD = 256
N_PROJ = OFF_IF + IF_PAD

ATTN_TILE = 2048
ATTN_SUB = ATTN_TILE // ATTN_BLOCK
ATTN_SLAB = 2 * ATTN_DH
ATTN_COLS = HEADS_PER_GROUP * ATTN_SLAB + 2 * ATTN_GW

VMEM_LIMIT = 48 * 1024 * 1024


def _cparams(*sem, flags=None):
    return pltpu.CompilerParams(dimension_semantics=sem, vmem_limit_bytes=VMEM_LIMIT, flags=flags)


def _rms(x, gain):
    return x * lax.rsqrt(jnp.mean(x * x, axis=-1, keepdims=True) + EPS) * gain


def _dot(a, b):
    return jnp.dot(a, b, preferred_element_type=F32)


def _dot_nt(a, b):
    return lax.dot_general(a, b, (((1,), (1,)), ((), ())), preferred_element_type=F32)


def _inproj_kernel(x_ref, g_ref, w_ref, o_ref, h_ref):
    @pl.when(pl.program_id(1) == 0)
    def _():
        h_ref[...] = _rms(x_ref[...], g_ref[...]).astype(BF16)

    o_ref[...] = _dot(h_ref[...], w_ref[...]).astype(o_ref.dtype)


def _inproj(x2d, gain, w, *, tm, tn):
    t, d = x2d.shape
    n = w.shape[1]
    return pl.pallas_call(
        _inproj_kernel,
        grid=(t // tm, n // tn),
        in_specs=[pl.BlockSpec((tm, d), lambda i, j: (i, 0)),
                  pl.BlockSpec((1, d), lambda i, j: (0, 0)),
                  pl.BlockSpec((d, tn), lambda i, j: (0, j))],
        out_specs=[pl.BlockSpec((tm, tn), lambda i, j: (i, j)),
                   pl.BlockSpec((tm, d), lambda i, j: (i, 0))],
        out_shape=[jax.ShapeDtypeStruct((t, n), BF16), jax.ShapeDtypeStruct((t, d), BF16)],
        compiler_params=_cparams("parallel", "arbitrary"),
        name="inproj",
    )(x2d, gain, w)


def _log_sigmoid(x):
    return jnp.minimum(x, 0.0) - jnp.log(1.0 + jnp.exp(-jnp.abs(x)))


def _mlstm_kernel(qk_ref, v_ref, og_ref, gc_ref, gr_ref, cw_ref, gbc_ref, gbr_ref, ng_ref, y_ref,
                  xe_scr, s_scr, m_scr, *, blk, nsub, group):
    heads, w = MLSTM_HEADS, MLSTM_W

    @pl.when(pl.program_id(1) == 0)
    def _():
        xe_scr[:, 0:8, :] = jnp.zeros((group, 8, 2 * w), F32)
        s_scr[...] = jnp.zeros_like(s_scr)
        m_scr[...] = jnp.zeros_like(m_scr)

    cw = cw_ref[...]
    ri = lax.broadcasted_iota(jnp.int32, (blk, blk), 0)
    ci = lax.broadcasted_iota(jnp.int32, (blk, blk), 1)
    causal = ri >= ci
    tril = causal.astype(BF16)
    triu = (ri <= ci).astype(BF16)
    states = []
    for g in range(group):
        xe_scr[g, 8:8 + nsub * blk, :] = qk_ref[g].astype(F32)
        states.append([(s_scr[g, h], m_scr[g, h:h + 1, 0:1]) for h in range(heads)])
    for c in range(nsub):
        for g in range(group):
            states[g] = _mlstm_chunk(c * blk, blk, states[g], cw, causal, tril, triu, xe_scr.at[g], v_ref.at[g],
                                     og_ref.at[g], gc_ref.at[g], gr_ref.at[g], gbc_ref, gbr_ref, ng_ref,
                                     y_ref.at[g])
    for g in range(group):
        xe_scr[g, 0:8, :] = xe_scr[g, nsub * blk:nsub * blk + 8, :]
        for h, (s_st, m_st) in enumerate(states[g]):
            s_scr[g, h] = s_st
            m_scr[g, h:h + 1, :] = jnp.broadcast_to(m_st, (1, m_scr.shape[2]))


def _split_bf16(x):
    hi = x.astype(BF16)
    return hi, (x - hi.astype(F32)).astype(BF16)


def _mlstm_chunk(r0, blk, state, cw, causal, tril, triu, xe_scr, v_ref, og_ref, gc_ref, gr_ref, gbc_ref,
                 gbr_ref, ng_ref, y_ref):
    heads, dh, w = MLSTM_HEADS, MLSTM_DH, MLSTM_W
    rows = slice(r0, r0 + blk)
    conv = cw[CONV_WIDTH - 1:CONV_WIDTH, :] * xe_scr[8 + r0:8 + r0 + blk, :]
    for j in range(CONV_WIDTH - 1):
        off = 8 + r0 - (CONV_WIDTH - 1) + j
        conv = conv + cw[j:j + 1, :] * xe_scr[off:off + blk, :]
    qk = conv * jax.nn.sigmoid(conv)

    gcol = gc_ref[rows, :].astype(F32) + gbc_ref[...]
    grow = gr_ref[:, rows] + gbr_ref[...]
    lc_hi, lc_lo = _split_bf16(_log_sigmoid(gcol))
    lr_hi, lr_lo = _split_bf16(_log_sigmoid(grow))
    bcol = _dot(tril, lc_hi) + _dot(tril, lc_lo)
    brow = _dot(lr_hi, triu) + _dot(lr_lo, triu)
    ones = jnp.ones((blk, dh), BF16)

    new_state = []
    for h in range(heads):
        sl = slice(h * dh, (h + 1) * dh)
        b_c = bcol[:, heads + h:heads + h + 1]
        i_c = gcol[:, h:h + 1]
        b_r = brow[heads + h:heads + h + 1, :]
        i_r = grow[h:h + 1, :]
        s_st, m_st = state[h]

        d_mat = jnp.where(causal, b_c - b_r + i_r, NEG)
        inter = b_c + m_st
        m_t = jnp.maximum(inter, jnp.max(d_mat, axis=-1, keepdims=True))
        w_intra = jnp.exp(d_mat - m_t)
        w_inter = jnp.exp(inter - m_t)

        q_f = qk[:, sl]
        k_f = qk[:, w + h * dh:w + (h + 1) * dh] * (dh ** -0.5)
        q_b = q_f.astype(BF16)
        k_b = k_f.astype(BF16)
        v_ext = jnp.concatenate([v_ref[rows, sl], ones], axis=-1)

        s = _dot_nt(q_b, k_b) * w_intra
        tot = _dot(s.astype(BF16), v_ext) + w_inter * _dot(q_b, s_st.astype(BF16))
        num, den = tot[:, :dh], tot[:, dh:]
        hh = num / jnp.maximum(jnp.abs(den), jnp.exp(-m_t))
        hn = _rms(hh, ng_ref[:, sl])
        y_ref[rows, sl] = (hn * jax.nn.sigmoid(og_ref[rows, sl].astype(F32))).astype(y_ref.dtype)

        b_last = b_c[blk - 1:blk, :]
        dec = b_last - b_c + i_c
        m_new = jnp.maximum(b_last + m_st, jnp.max(dec, axis=0, keepdims=True))
        w_k = jnp.exp(dec - m_new)
        w_c = jnp.exp(b_last + m_st - m_new)
        kw = k_f * w_k
        new_state.append((w_c * s_st + _dot(kw.T.astype(BF16), v_ext), m_new))
    return new_state


def _mlstm(proj, gates_row, conv_w, gb_col, gb_row, norm_g, *, batch, seq, blk, nsub, group):
    t, npj = proj.shape
    rows = blk * nsub
    w = MLSTM_W
    proj3 = proj.reshape(batch, seq, npj)
    cols = lambda c: (lambda b, i: (b, i, c))
    const2 = lambda b, i: (0, 0)
    y = pl.pallas_call(
        functools.partial(_mlstm_kernel, blk=blk, nsub=nsub, group=group),
        grid=(batch // group, seq // rows),
        in_specs=[pl.BlockSpec((group, rows, 2 * w), cols(OFF_MQ // (2 * w))),
                  pl.BlockSpec((group, rows, w), cols(OFF_MV // w)),
                  pl.BlockSpec((group, rows, w), cols(OFF_MO // w)),
                  pl.BlockSpec((group, rows, IF_PAD), cols(OFF_IF // IF_PAD)),
                  pl.BlockSpec((group, 8, rows), lambda b, i: (b, 0, i)),
                  pl.BlockSpec((CONV_WIDTH, 2 * w), const2),
                  pl.BlockSpec((1, IF_PAD), const2),
                  pl.BlockSpec((8, 1), const2),
                  pl.BlockSpec((1, w), const2)],
        out_specs=pl.BlockSpec((group, rows, w), cols(0)),
        out_shape=jax.ShapeDtypeStruct((batch, seq, w), BF16),
        scratch_shapes=[pltpu.VMEM((group, rows + 8, 2 * w), F32),
                        pltpu.VMEM((group, MLSTM_HEADS, MLSTM_DH, 2 * MLSTM_DH), F32),
                        pltpu.VMEM((group, 8, 128), F32)],
        compiler_params=_cparams("parallel", "arbitrary"),
        name="mlstm",
    )(proj3, proj3, proj3, proj3, gates_row, conv_w, gb_col, gb_row, norm_g)
    return y.reshape(t, w)


def _attnproj_kernel(h_ref, w_ref, seg_ref, gq_ref, gk_ref, o_ref, r_scr, *, dil):
    gw, half = ATTN_GW, ATTN_SLAB // 2
    sub_rows = r_scr.shape[2]
    seg, sub_seg = ATTN_TILE // dil, sub_rows // dil

    def head_norm(x, gain):
        sq = x * x
        hi = sq.astype(BF16)
        lo = (sq - hi.astype(F32)).astype(BF16)
        ss = _dot(hi, seg_ref[...]) + _dot(lo, seg_ref[...])
        return x * lax.rsqrt(ss * (1.0 / ATTN_DH) + EPS) * gain

    low = lax.broadcasted_iota(jnp.int32, (1, ATTN_SLAB), 1) < half
    for s in range(ATTN_TILE // sub_rows):
        rows = slice(s * sub_rows, (s + 1) * sub_rows)
        res = _dot(h_ref[rows, :], w_ref[...])
        q = head_norm(res[:, :gw], gq_ref[...]) * (ATTN_DH ** -0.5)
        k = head_norm(res[:, gw:2 * gw], gk_ref[...])
        slabs = []
        for pair in range(gw // ATTN_SLAB):
            qp = q[:, pair * ATTN_SLAB:(pair + 1) * ATTN_SLAB]
            slabs += [jnp.where(low, qp, 0.0), jnp.where(low, 0.0, qp)]
        slabs += [k[:, c * 128:(c + 1) * 128] for c in range(gw // 128)]
        slabs += [res[:, 2 * gw + c * 128:2 * gw + (c + 1) * 128] for c in range(gw // 128)]
        for c, slab in enumerate(slabs):
            if dil == 1:
                o_ref[rows, c * 128:(c + 1) * 128] = slab.astype(o_ref.dtype)
            else:
                r_scr[s % 2, c] = slab
        if dil > 1:
            for r in range(dil):
                dst = slice(r * seg + s * sub_seg, r * seg + (s + 1) * sub_seg)
                for c in range(r_scr.shape[1]):
                    o_ref[dst, c * 128:(c + 1) * 128] = (
                        r_scr[s % 2, c, pl.ds(r, sub_seg, stride=dil), :].astype(o_ref.dtype))


def _attnproj(h, w, seg_ones, gq, gk, *, group, dilation):
    t, d = h.shape
    wcols = 3 * ATTN_GW
    const2 = lambda i: (0, 0)
    return pl.pallas_call(
        functools.partial(_attnproj_kernel, dil=dilation),
        grid=(t // ATTN_TILE,),
        in_specs=[pl.BlockSpec((ATTN_TILE, d), lambda i: (i, 0)),
                  pl.BlockSpec((d, wcols), lambda i: (0, group)),
                  pl.BlockSpec((ATTN_GW, ATTN_GW), const2),
                  pl.BlockSpec((1, ATTN_GW), const2), pl.BlockSpec((1, ATTN_GW), const2)],
        out_specs=pl.BlockSpec((ATTN_TILE, ATTN_COLS), lambda i: (i, 0)),
        out_shape=jax.ShapeDtypeStruct((t, ATTN_COLS), BF16),
        scratch_shapes=[pltpu.VMEM((2, ATTN_COLS // 128, 512, 128), F32)],
        compiler_params=_cparams("parallel"),
        name=f"attnproj{group}",
    )(h, w, seg_ones, gq, gk)


def _dattn_kernel(q_ref, kc_ref, kp_ref, vc_ref, vp_ref, bias_ref, o_ref, lse_ref,
                  kx_scr, vx_scr, o_scr, l_scr, *, dil):
    blk = ATTN_BLOCK
    per = ATTN_SUB // dil
    first_tile = pl.program_id(1) == 0
    for r in range(dil):
        base = r * (per + 1) * blk
        last = slice((r * per + per - 1) * blk, (r * per + per) * blk)
        mine = slice(r * per * blk, (r + 1) * per * blk)
        kx_scr[base:base + blk, :] = kp_ref[last, :]
        vx_scr[base:base + blk, :] = vp_ref[last, :]
        kx_scr[base + blk:base + (per + 1) * blk, :] = kc_ref[mine, :]
        vx_scr[base + blk:base + (per + 1) * blk, :] = vc_ref[mine, :]

    low = lax.broadcasted_iota(jnp.int32, (1, ATTN_SLAB), 1) < ATTN_SLAB // 2
    no_prev = lax.broadcasted_iota(jnp.int32, (1, 2 * blk), 1) < blk
    for r in range(dil):
        for sub in range(per):
            u = r * per + sub
            win = slice((r * (per + 1) + sub) * blk, (r * (per + 1) + sub + 2) * blk)
            o_slabs, l_slabs = [], []
            for pair in range(ATTN_GW // ATTN_SLAB):
                cols = slice(pair * ATTN_SLAB, (pair + 1) * ATTN_SLAB)
                kx, vx = kx_scr[win, cols], vx_scr[win, cols]
                o_pair, l_pair = [], []
                for h in (2 * pair, 2 * pair + 1):
                    logits = _dot_nt(q_ref[u * blk:(u + 1) * blk, h * ATTN_SLAB:(h + 1) * ATTN_SLAB], kx)
                    logits = logits + bias_ref[h]
                    if sub == 0:
                        logits = jnp.where(first_tile & no_prev, NEG, logits)
                    m = jnp.max(logits, axis=-1, keepdims=True)
                    p = jnp.exp(logits - m)
                    l = jnp.sum(p, axis=-1, keepdims=True)
                    o_pair.append(_dot(p.astype(BF16), vx) / l)
                    l_pair.append(m + jnp.log(l))
                o_slabs.append(jnp.where(low, o_pair[0], o_pair[1]))
                l_slabs.append(jnp.where(low, l_pair[0], l_pair[1]))
            dst = pl.ds(sub * blk * dil + r, blk, stride=dil) if dil > 1 else slice(u * blk, (u + 1) * blk)
            for c in range(ATTN_GW // ATTN_SLAB):
                o_scr[c, dst, :] = o_slabs[c]
                l_scr[c, dst, :] = l_slabs[c]
    for c in range(ATTN_GW // ATTN_SLAB):
        o_ref[:, c * ATTN_SLAB:(c + 1) * ATTN_SLAB] = o_scr[c].astype(o_ref.dtype)
        lse_ref[:, c * ATTN_SLAB:(c + 1) * ATTN_SLAB] = l_scr[c]


def _dattn(aproj, bias, *, seq, group, dilation):
    t = aproj.shape[0]
    tiles = seq // ATTN_TILE
    qw = HEADS_PER_GROUP * ATTN_SLAB
    cq, ck, cv = 0, qw // ATTN_GW, qw // ATTN_GW + 1
    blk = (ATTN_TILE, ATTN_GW)
    cur = lambda c: (lambda b, j: (b * tiles + j, c))
    prev = lambda c: (lambda b, j: (b * tiles + jnp.maximum(j - 1, 0), c))
    xrows = ATTN_TILE + dilation * ATTN_BLOCK
    return pl.pallas_call(
        functools.partial(_dattn_kernel, dil=dilation),
        grid=(t // seq, tiles),
        in_specs=[pl.BlockSpec((ATTN_TILE, qw), cur(cq)),
                  pl.BlockSpec(blk, cur(ck)), pl.BlockSpec(blk, prev(ck)),
                  pl.BlockSpec(blk, cur(cv)), pl.BlockSpec(blk, prev(cv)),
                  pl.BlockSpec((HEADS_PER_GROUP, ATTN_BLOCK, 2 * ATTN_BLOCK), lambda b, j: (0, 0, 0))],
        out_specs=[pl.BlockSpec(blk, cur(0)), pl.BlockSpec(blk, cur(0))],
        out_shape=[jax.ShapeDtypeStruct((t, ATTN_GW), BF16), jax.ShapeDtypeStruct((t, ATTN_GW), F32)],
        scratch_shapes=[pltpu.VMEM((xrows, ATTN_GW), BF16), pltpu.VMEM((xrows, ATTN_GW), BF16),
                        pltpu.VMEM((ATTN_GW // ATTN_SLAB, ATTN_TILE, ATTN_SLAB), F32),
                        pltpu.VMEM((ATTN_GW // ATTN_SLAB, ATTN_TILE, ATTN_SLAB), F32)],
        compiler_params=_cparams("parallel", "arbitrary"),
        name=f"dattn{group}",
    )(aproj, aproj, aproj, aproj, aproj, bias)


def _rel_bucket(n):
    max_exact = REL_BUCKETS // 2
    nf = jnp.maximum(n, 1).astype(F32)
    log_b = max_exact + (jnp.log(nf / max_exact) / math.log(REL_MAX_DIST / max_exact)
                         * (REL_BUCKETS - max_exact)).astype(jnp.int32)
    return jnp.where(n < max_exact, n, jnp.minimum(log_b, REL_BUCKETS - 1))


def _attn_bias(rel_bias, group):
    window, dilation = ATTN_PATTERNS[group]
    steps = window // dilation
    hp = lax.Precision.HIGHEST
    hs = slice(group * HEADS_PER_GROUP, (group + 1) * HEADS_PER_GROUP)
    bucket = _rel_bucket(jnp.arange(steps + 1) * dilation)
    bias_steps = jnp.dot(jax.nn.one_hot(bucket, REL_BUCKETS, dtype=F32), rel_bias[:, hs].astype(F32),
                         precision=hp)
    qi = jnp.arange(ATTN_BLOCK)[:, None]
    ki = jnp.arange(2 * ATTN_BLOCK)[None, :]
    dist = ATTN_BLOCK + qi - ki
    ok = (dist >= 0) & (dist <= steps)
    sel = jax.nn.one_hot(jnp.clip(dist, 0, steps).reshape(-1), steps + 1, dtype=F32)
    bias = jnp.dot(sel, bias_steps, precision=hp).T.reshape(HEADS_PER_GROUP, ATTN_BLOCK, 2 * ATTN_BLOCK)
    return jnp.where(ok[None], bias, NEG)


def _merge_kernel(ya_ref, yb0_ref, yb1_ref, yb2_ref, l0_ref, l1_ref, l2_ref, gu_ref, gv_ref, gate_ref,
                  x_ref, wa_ref, wb_ref, wc_ref, wo_ref, ws_ref, bs_ref, gg_ref, o_ref, yc_scr, *, tm):
    d = x_ref.shape[1]
    l0, l1, l2 = l0_ref[...], l1_ref[...], l2_ref[...]
    mx = jnp.maximum(jnp.maximum(l0, l1), l2)
    e0, e1, e2 = jnp.exp(l0 - mx), jnp.exp(l1 - mx), jnp.exp(l2 - mx)
    inv = 1.0 / (e0 + e1 + e2)
    yb = jnp.concatenate([(yb0_ref[...].astype(F32) * (e0 * inv)).astype(BF16),
                          (yb1_ref[...].astype(F32) * (e1 * inv)).astype(BF16),
                          (yb2_ref[...].astype(F32) * (e2 * inv)).astype(BF16)], axis=-1)

    for j in range(tm // GMLP_CHUNK):
        rows = slice(j * GMLP_CHUNK, (j + 1) * GMLP_CHUNK)
        for g in range(GMLP_GROUPS):
            cols = slice(g * GMLP_GC, (g + 1) * GMLP_GC)
            u = jax.nn.gelu(gu_ref[rows, cols].astype(F32))
            v = _rms(jax.nn.gelu(gv_ref[rows, cols].astype(F32)), gg_ref[:, cols])
            mixed = _dot(ws_ref[g], v.astype(BF16)) + bs_ref[g]
            yc_scr[rows, cols] = (u * mixed).astype(BF16)

    merged = jax.nn.sigmoid(gate_ref[:, 0:d].astype(F32)) * _dot(ya_ref[...], wa_ref[...])
    merged = merged + jax.nn.sigmoid(gate_ref[:, d:2 * d].astype(F32)) * _dot(yb, wb_ref[...])
    merged = merged + jax.nn.sigmoid(gate_ref[:, 2 * d:3 * d].astype(F32)) * _dot(yc_scr[...], wc_ref[...])
    o_ref[...] = x_ref[...] + _dot(merged.astype(BF16), wo_ref[...])


def _merge(ya, ybs, lses, proj, x2d, wa, wb, wc, wo, ws, bsb, gg, *, tm):
    t, d = x2d.shape
    row = lambda c: (lambda i: (i, c))
    full2 = lambda i: (0, 0)
    full3 = lambda i: (0, 0, 0)
    gspec = pl.BlockSpec((tm, ATTN_GW), row(0))
    return pl.pallas_call(
        functools.partial(_merge_kernel, tm=tm),
        grid=(t // tm,),
        in_specs=[pl.BlockSpec((tm, MLSTM_W), row(0)),
                  gspec, gspec, gspec, gspec, gspec, gspec,
                  pl.BlockSpec((tm, GMLP_W), row(OFF_GU // GMLP_W)),
                  pl.BlockSpec((tm, GMLP_W), row(OFF_GV // GMLP_W)),
                  pl.BlockSpec((tm, N_BRANCH * d), row(OFF_GATE // (N_BRANCH * d))),
                  pl.BlockSpec((tm, d), row(0)),
                  pl.BlockSpec(wa.shape, full2), pl.BlockSpec(wb.shape, full2),
                  pl.BlockSpec(wc.shape, full2), pl.BlockSpec(wo.shape, full2),
                  pl.BlockSpec(ws.shape, full3), pl.BlockSpec(bsb.shape, full3),
                  pl.BlockSpec(gg.shape, full2)],
        out_specs=pl.BlockSpec((tm, d), row(0)),
        out_shape=jax.ShapeDtypeStruct((t, d), F32),
        scratch_shapes=[pltpu.VMEM((tm, GMLP_W), BF16)],
        compiler_params=_cparams("parallel"),
        name="merge",
    )(ya, *ybs, *lses, proj, proj, proj, x2d, wa, wb, wc, wo, ws, bsb, gg)


def _memkv_kernel(mem_ref, g_ref, w_ref, gk_ref, k_ref, v_ref):
    dh, w = XATTN_DH, XATTN_W
    kv = _dot(_rms(mem_ref[0], g_ref[...]).astype(BF16), w_ref[...])
    for h in range(XATTN_HEADS):
        sl = slice(h * dh, (h + 1) * dh)
        k_ref[0, :, sl] = _rms(kv[:, sl], gk_ref[...]).astype(k_ref.dtype)
    v_ref[0] = kv[:, w:].astype(v_ref.dtype)


def _memkv(mem, gain, w_kv, gk):
    b, m, d = mem.shape
    full2 = lambda i: (0, 0)
    return pl.pallas_call(
        _memkv_kernel,
        grid=(b,),
        in_specs=[pl.BlockSpec((1, m, d), lambda i: (i, 0, 0)),
                  pl.BlockSpec((1, d), full2),
                  pl.BlockSpec(w_kv.shape, full2),
                  pl.BlockSpec((1, XATTN_DH), full2)],
        out_specs=[pl.BlockSpec((1, m, XATTN_W), lambda i: (i, 0, 0)),
                   pl.BlockSpec((1, m, XATTN_W), lambda i: (i, 0, 0))],
        out_shape=[jax.ShapeDtypeStruct((b, m, XATTN_W), BF16),
                   jax.ShapeDtypeStruct((b, m, XATTN_W), BF16)],
        compiler_params=_cparams("parallel"),
        name="memkv",
    )(mem, gain, w_kv, gk)


def _route(logits):
    tm = logits.shape[1]
    e = jnp.exp(logits - jnp.max(logits, axis=0, keepdims=True))
    probs = e / jnp.sum(e, axis=0, keepdims=True)
    rowi = lax.broadcasted_iota(jnp.int32, (8, tm), 0)
    real = rowi < EXPERTS_PER_GROUP
    tops = []
    for g in range(N_EXPERT_GROUPS):
        pg = jnp.where(real, probs[8 * g:8 * g + 8, :], -0.5)
        m1 = jnp.max(pg, axis=0, keepdims=True)
        i1 = jnp.min(jnp.where(pg == m1, rowi, 8), axis=0, keepdims=True)
        pg2 = jnp.where(rowi == i1, -1.0, pg)
        m2 = jnp.max(pg2, axis=0, keepdims=True)
        i2 = jnp.min(jnp.where(pg2 == m2, rowi, 8), axis=0, keepdims=True)
        tops.append((m1, i1, m2, i2))
    best = jnp.zeros((1, tm), jnp.int32)
    best_score = tops[0][0] + tops[0][2]
    for g in range(1, N_EXPERT_GROUPS):
        score = tops[g][0] + tops[g][2]
        better = score > best_score
        best = jnp.where(better, g, best)
        best_score = jnp.where(better, score, best_score)
    m1, i1, m2, i2 = tops[0]
    for g in range(1, N_EXPERT_GROUPS):
        m1, i1, m2, i2 = (jnp.where(best == g, new, old) for new, old in zip(tops[g], (m1, i1, m2, i2)))
    tot = m1 + m2
    base = best * EXPERTS_PER_GROUP
    return base + i1, base + i2, m1 / tot, m2 / tot


def _pack_bf16_pairs(x):
    n = x.shape[1] // 2
    hi = lax.bitcast_convert_type(x[:, :n].astype(BF16).astype(F32), jnp.uint32)
    lo = lax.bitcast_convert_type(x[:, n:].astype(BF16).astype(F32), jnp.uint32)
    return hi | (lo >> 16)


def _unpack_bf16_pairs(p):
    hi = lax.bitcast_convert_type(p & jnp.uint32(0xFFFF0000), F32)
    lo = lax.bitcast_convert_type(p << 16, F32)
    return hi, lo


def _store_row_chunks(ref, packed):
    for j in range(ROW_CHUNKS):
        ref[j] = packed[:, j * 128:(j + 1) * 128]


def _load_row_chunks(ref):
    return jnp.concatenate([ref[j] for j in range(ROW_CHUNKS)], axis=-1)


def _xattn_kernel(x_ref, k_ref, v_ref, gx_ref, wq_ref, gq_ref, wo_ref, gf_ref, rw_ref, rb_ref,
                  xo_ref, hf_ref, eidx_ref, wts_ref):
    dh = XATTN_DH
    x = x_ref[...]
    q = _dot(_rms(x, gx_ref[...]).astype(BF16), wq_ref[...])
    outs = []
    for h in range(XATTN_HEADS):
        sl = slice(h * dh, (h + 1) * dh)
        q_h = (_rms(q[:, sl], gq_ref[...]) * (dh ** -0.5)).astype(BF16)
        logits = _dot_nt(q_h, k_ref[0, :, sl])
        p = jnp.exp(logits - jnp.max(logits, axis=-1, keepdims=True))
        o = _dot(p.astype(BF16), v_ref[0, :, sl]) / jnp.sum(p, axis=-1, keepdims=True)
        outs.append(o.astype(BF16))
    xn = x + _dot(jnp.concatenate(outs, axis=-1), wo_ref[...])
    xo_ref[...] = xn
    hf = _rms(xn, gf_ref[...])
    _store_row_chunks(hf_ref, _pack_bf16_pairs(hf))
    rw = rw_ref[...]
    rw_hi = rw.astype(BF16)
    rw_lo = (rw - rw_hi.astype(F32)).astype(BF16)
    hf_hi = hf.astype(BF16)
    hf_lo = (hf - hf_hi.astype(F32)).astype(BF16)
    logits_t = _dot_nt(rw_hi, hf_hi) + _dot_nt(rw_hi, hf_lo) + _dot_nt(rw_lo, hf_hi) + rb_ref[...]
    e1, e2, w1, w2 = _route(logits_t)
    tm = x.shape[0]
    eidx_ref[...] = jnp.concatenate([e1, e2, jnp.zeros((6, tm), jnp.int32)], axis=0)
    wts_ref[...] = jnp.concatenate([w1, w2, jnp.zeros((6, tm), F32)], axis=0)


def _xattn(x2d, k, v, gx, wq, gq, wo, gf, rw_t, rb, *, seq, tm):
    t, d = x2d.shape
    per_b = seq // tm
    full2 = lambda i: (0, 0)
    kv_spec = pl.BlockSpec((1,) + k.shape[1:], lambda i: (i // per_b, 0, 0))
    return pl.pallas_call(
        _xattn_kernel,
        grid=(t // tm,),
        in_specs=[pl.BlockSpec((tm, d), lambda i: (i, 0)), kv_spec, kv_spec,
                  pl.BlockSpec((1, d), full2), pl.BlockSpec(wq.shape, full2),
                  pl.BlockSpec((1, XATTN_DH), full2), pl.BlockSpec(wo.shape, full2),
                  pl.BlockSpec((1, d), full2), pl.BlockSpec(rw_t.shape, full2),
                  pl.BlockSpec(rb.shape, full2)],
        out_specs=[pl.BlockSpec((tm, d), lambda i: (i, 0)),
                   pl.BlockSpec((ROW_CHUNKS, tm, 128), lambda i: (0, i, 0)),
                   pl.BlockSpec((8, tm), lambda i: (0, i)),
                   pl.BlockSpec((8, tm), lambda i: (0, i))],
        out_shape=[jax.ShapeDtypeStruct((t, d), F32),
                   jax.ShapeDtypeStruct((ROW_CHUNKS, t, 128), jnp.uint32),
                   jax.ShapeDtypeStruct((8, t), jnp.int32),
                   jax.ShapeDtypeStruct((8, t), F32)],
        compiler_params=_cparams("parallel"),
        name="xattn_router",
    )(x2d, k, v, gx, wq, gq, wo, gf, rw_t, rb)


def _moe_plan_kernel(eidx_ref, i1_ref, i2_ref, te_ref, na_ref, cnt_scr, carry_scr, *, tb, tm, plane_rows):
    ne = N_EXPERTS
    hp = lax.Precision.HIGHEST
    phase, j = pl.program_id(0), pl.program_id(1)
    rows = lax.broadcasted_iota(jnp.int32, (ne, tb), 0)
    oh1 = rows == eidx_ref[0:1, :]
    oh2 = rows == eidx_ref[1:2, :]
    a = oh1.astype(F32) + oh2.astype(F32)
    blk_cnt = jnp.broadcast_to(jnp.sum(a, axis=1, keepdims=True), cnt_scr.shape)

    @pl.when((phase == 0) & (j == 0))
    def _():
        cnt_scr[...] = jnp.zeros_like(cnt_scr)

    @pl.when(phase == 0)
    def _():
        cnt_scr[...] += blk_cnt

    @pl.when((phase == 1) & (j == 0))
    def _():
        padded = jnp.ceil(cnt_scr[...] * (1.0 / tm)) * tm
        er = lax.broadcasted_iota(jnp.int32, (ne, ne), 0)
        ec = lax.broadcasted_iota(jnp.int32, (ne, ne), 1)
        off = jnp.dot((ec < er).astype(F32), padded, precision=hp, preferred_element_type=F32)
        carry_scr[...] = off
        seg_end = (off + padded)[:, 0:1]
        tile_start = lax.broadcasted_iota(jnp.int32, (ne, te_ref.shape[1]), 1).astype(F32) * tm
        te = jnp.sum((seg_end <= tile_start).astype(F32), axis=0, keepdims=True)
        te_ref[...] = jnp.broadcast_to(jnp.minimum(te, ne - 1.0), te_ref.shape).astype(jnp.int32)
        total = jnp.sum(padded[:, 0:1], axis=0, keepdims=True)
        na_ref[...] = jnp.broadcast_to(total * (1.0 / tm), na_ref.shape).astype(jnp.int32)

    @pl.when(phase == 1)
    def _():
        before = (lax.broadcasted_iota(jnp.int32, (tb, tb), 0)
                  < lax.broadcasted_iota(jnp.int32, (tb, tb), 1)).astype(BF16)
        rank = carry_scr[:, 0:1] + _dot(a.astype(BF16), before)
        d1 = jnp.sum(jnp.where(oh1, rank, 0.0), axis=0, keepdims=True).astype(jnp.int32)
        d2 = jnp.sum(jnp.where(oh2, rank, 0.0), axis=0, keepdims=True).astype(jnp.int32)
        plane = lax.broadcasted_iota(jnp.int32, (8, tb), 0) * plane_rows
        i1_ref[...] = jnp.where(plane < ROW_CHUNKS * plane_rows, plane + d1, 0)
        i2_ref[...] = jnp.where(plane < ROW_CHUNKS * plane_rows, plane + d2, 0)
        carry_scr[...] += blk_cnt


def _moe_plan(eidx, *, tm, n_tiles, tb=512):
    t = eidx.shape[1]
    ntp = -(-n_tiles // 128) * 128
    return pl.pallas_call(
        functools.partial(_moe_plan_kernel, tb=tb, tm=tm, plane_rows=n_tiles * tm),
        grid=(2, t // tb),
        in_specs=[pl.BlockSpec((8, tb), lambda p, j: (0, j))],
        out_specs=[pl.BlockSpec((8, tb), lambda p, j: (0, j * p)),
                   pl.BlockSpec((8, tb), lambda p, j: (0, j * p)),
                   pl.BlockSpec((8, ntp), lambda p, j: (0, 0)),
                   pl.BlockSpec((8, 128), lambda p, j: (0, 0))],
        out_shape=[jax.ShapeDtypeStruct((8, t), jnp.int32),
                   jax.ShapeDtypeStruct((8, t), jnp.int32),
                   jax.ShapeDtypeStruct((8, ntp), jnp.int32),
                   jax.ShapeDtypeStruct((8, 128), jnp.int32)],
        scratch_shapes=[pltpu.VMEM((N_EXPERTS, 128), F32), pltpu.VMEM((N_EXPERTS, 128), F32)],
        compiler_params=_cparams("arbitrary", "arbitrary"),
        name="moe_plan",
    )(eidx)


def _sc_mesh():
    return plsc.VectorSubcoreMesh(core_axis_name="c", subcore_axis_name="s",
                                  num_cores=SC_CORES, num_subcores=SC_SUBCORES)


def _sc_index_spec(tokens):
    nb = tokens // SC_WINDOW
    return pl.BlockSpec((1, SC_WINDOW), lambda i: (i // nb, i % nb))


def _sc_dispatch(rows, i1, i2, n_out):
    n = rows.shape[0]
    tokens = i1.shape[1]

    @functools.partial(pl.kernel, out_type=jax.ShapeDtypeStruct((n_out, 128), rows.dtype), mesh=_sc_mesh(),
                       name="moe_dispatch")
    def k(x_hbm, i1_hbm, i2_hbm, o_hbm):
        def body(x_vmem, i1_vmem, i2_vmem):
            pltpu.sync_copy(x_vmem, o_hbm.at[i1_vmem.at[0]])
            pltpu.sync_copy(x_vmem, o_hbm.at[i2_vmem.at[0]])

        pltpu.emit_pipeline(
            body, grid=(n // SC_WINDOW,),
            in_specs=[pl.BlockSpec((SC_WINDOW, 128), lambda i: (i, 0)),
                      _sc_index_spec(tokens), _sc_index_spec(tokens)],
            out_specs=[],
            core_axis_name=("c", "s"), dimension_semantics=(pltpu.PARALLEL,),
        )(x_hbm, i1_hbm, i2_hbm)

    return k(rows, i1, i2)


def _sc_collect(table, i1, i2):
    tokens = i1.shape[1]
    n = ROW_CHUNKS * tokens
    out = jax.ShapeDtypeStruct((n, 128), table.dtype)

    @functools.partial(pl.kernel, out_type=(out, out), mesh=_sc_mesh(), name="moe_collect")
    def k(t_hbm, i1_hbm, i2_hbm, o1_hbm, o2_hbm):
        def body(i1_vmem, i2_vmem, o1_vmem, o2_vmem):
            pltpu.sync_copy(t_hbm.at[i1_vmem.at[0]], o1_vmem)
            pltpu.sync_copy(t_hbm.at[i2_vmem.at[0]], o2_vmem)

        pltpu.emit_pipeline(
            body, grid=(n // SC_WINDOW,),
            in_specs=[_sc_index_spec(tokens), _sc_index_spec(tokens)],
            out_specs=[pl.BlockSpec((SC_WINDOW, 128), lambda i: (i, 0)),
                       pl.BlockSpec((SC_WINDOW, 128), lambda i: (i, 0))],
            core_axis_name=("c", "s"), dimension_semantics=(pltpu.PARALLEL,),
        )(i1_hbm, i2_hbm, o1_hbm, o2_hbm)

    return k(table, i1, i2)


def _experts_kernel(te_ref, na_ref, xs_ref, wg_ref, wu_ref, wd_ref, y_ref, wg_scr, wu_scr, wd_scr):
    i = pl.program_id(0)
    active = i < na_ref[0]

    @pl.when(active & ((i == 0) | (te_ref[i] != te_ref[jnp.maximum(i - 1, 0)])))
    def _():
        wg_scr[...] = wg_ref[0, 0].astype(BF16)
        wu_scr[...] = wu_ref[0, 0].astype(BF16)
        wd_scr[...] = wd_ref[0, 0].astype(BF16)

    @pl.when(active)
    def _():
        hi, lo = _unpack_bf16_pairs(_load_row_chunks(xs_ref))
        h = jnp.concatenate([hi, lo], axis=-1).astype(BF16)
        up = _dot(h, wg_scr[...])
        act = up * jax.nn.sigmoid(up) * _dot(h, wu_scr[...])
        _store_row_chunks(y_ref, _pack_bf16_pairs(_dot(act.astype(BF16), wd_scr[...])))


def _experts(tile_expert, n_active, xs, wg, wu, wd, *, layer, tm):
    n_tiles = tile_expert.shape[0]
    _, _, d, dff = wg.shape
    rows = lambda i, te, na: (0, jnp.minimum(i, na[0] - 1), 0)
    expert = lambda i, te, na: (layer, te[i], 0, 0)
    return pl.pallas_call(
        _experts_kernel,
        grid_spec=pltpu.PrefetchScalarGridSpec(
            num_scalar_prefetch=2,
            grid=(n_tiles,),
            in_specs=[pl.BlockSpec((ROW_CHUNKS, tm, 128), rows),
                      pl.BlockSpec((1, 1, d, dff), expert),
                      pl.BlockSpec((1, 1, d, dff), expert),
                      pl.BlockSpec((1, 1, dff, d), expert)],
            out_specs=pl.BlockSpec((ROW_CHUNKS, tm, 128), rows),
            scratch_shapes=[pltpu.VMEM((d, dff), BF16), pltpu.VMEM((d, dff), BF16), pltpu.VMEM((dff, d), BF16)]),
        out_shape=jax.ShapeDtypeStruct(xs.shape, xs.dtype),
        compiler_params=_cparams("arbitrary"),
        name="moe_experts",
    )(tile_expert, n_active, xs, wg, wu, wd)


def _moe_combine_kernel(x_ref, y1_ref, y2_ref, w_ref, o_ref):
    half = x_ref.shape[1] // 2
    hi1, lo1 = _unpack_bf16_pairs(_load_row_chunks(y1_ref))
    hi2, lo2 = _unpack_bf16_pairs(_load_row_chunks(y2_ref))
    w1, w2 = w_ref[:, 0:1], w_ref[:, 1:2]
    o_ref[:, :half] = x_ref[:, :half] + w1 * hi1 + w2 * hi2
    o_ref[:, half:] = x_ref[:, half:] + w1 * lo1 + w2 * lo2


def _moe_combine(x2d, y1, y2, wcol, *, tm):
    t, d = x2d.shape
    chunk_spec = pl.BlockSpec((ROW_CHUNKS, tm, 128), lambda i: (0, i, 0))
    return pl.pallas_call(
        _moe_combine_kernel,
        grid=(t // tm,),
        in_specs=[pl.BlockSpec((tm, d), lambda i: (i, 0)), chunk_spec, chunk_spec,
                  pl.BlockSpec((tm, wcol.shape[1]), lambda i: (i, 0))],
        out_specs=pl.BlockSpec((tm, d), lambda i: (i, 0)),
        out_shape=jax.ShapeDtypeStruct((t, d), F32),
        compiler_params=_cparams("parallel"),
        name="moe_combine",
    )(x2d, y1, y2, wcol)


def _moe(x2d, hf_rows, eidx, wts, wg, wu, wd, *, layer):
    t = x2d.shape[0]
    tm = MOE_TM
    n_tiles = 2 * t // tm + N_EXPERTS
    plane = n_tiles * tm
    i1, i2, te, na = _moe_plan(eidx, tm=tm, n_tiles=n_tiles)
    xs = _sc_dispatch(hf_rows.reshape(ROW_CHUNKS * t, 128), i1, i2, ROW_CHUNKS * plane)
    ys = _experts(te[0, :n_tiles], na[0, :1], xs.reshape(ROW_CHUNKS, plane, 128), wg, wu, wd,
                  layer=layer, tm=tm)
    y1, y2 = _sc_collect(ys.reshape(ROW_CHUNKS * plane, 128), i1, i2)
    return _moe_combine(x2d, y1.reshape(ROW_CHUNKS, t, 128), y2.reshape(ROW_CHUNKS, t, 128), wts[:2].T, tm=512)


def _layout_w_in(w):
    sizes = (MLSTM_W, MLSTM_W, MLSTM_W, MLSTM_W, MLSTM_HEADS, MLSTM_HEADS,
             ATTN_W, ATTN_W, ATTN_W, GMLP_W, GMLP_W, N_BRANCH * w.shape[0])
    pts = np.cumsum(sizes)[:-1]
    mq, mk, mv, mo, mi, mf, aq, ak, av, gu, gv, gate = jnp.split(w, pts, axis=-1)
    pad = jnp.zeros((w.shape[0], IF_PAD - 2 * MLSTM_HEADS), w.dtype)
    main = jnp.concatenate([mq, mk, mv, mo, gu, gv, gate, mi, mf, pad], axis=-1).astype(BF16)
    attn = jnp.concatenate([a[:, g * ATTN_GW:(g + 1) * ATTN_GW] for g in range(len(ATTN_PATTERNS))
                            for a in (aq, ak, av)], axis=-1).astype(BF16)
    return main, attn


def kernel(x, mem, norm_mix, w_in, mlstm_conv, mlstm_gate_b, mlstm_norm, attn_qk_norm, gmlp_norm, gmlp_ws,
           gmlp_bs, w_branch_a, w_branch_b, w_branch_c, w_out, rel_bias, norm_xattn, norm_mem, w_xq, w_xkv,
           xattn_qk_norm, w_xo, norm_ffn, router_w, router_b, w_expert_gate, w_expert_up, w_expert_down):
    b, s, d = x.shape
    t = b * s
    depth = w_in.shape[0]
    x2d = x.reshape(t, d)

    biases = [_attn_bias(rel_bias, g) for g in range(len(ATTN_PATTERNS))]
    rw_t = jnp.zeros((N_EXPERT_GROUPS, 8, d), F32).at[:, :EXPERTS_PER_GROUP].set(
        router_w.T.reshape(N_EXPERT_GROUPS, EXPERTS_PER_GROUP, d)).reshape(ROUTER_ROWS, d)
    rb = jnp.full((N_EXPERT_GROUPS, 8), NEG, F32).at[:, :EXPERTS_PER_GROUP].set(
        router_b.astype(F32).reshape(N_EXPERT_GROUPS, EXPERTS_PER_GROUP)).reshape(ROUTER_ROWS, 1)
    tril = jnp.tril(jnp.ones((GMLP_CHUNK, GMLP_CHUNK), bool))
    head_of = jnp.arange(ATTN_GW) // ATTN_DH
    seg_ones = (head_of[:, None] == head_of[None, :]).astype(BF16)

    for l in range(depth):
        w_main, w_attn = _layout_w_in(w_in[l])
        proj, h_mix = _inproj(x2d, norm_mix[l][None], w_main, tm=1024, tn=1280)
        gq = jnp.tile(attn_qk_norm[l, 0], HEADS_PER_GROUP)[None]
        gk = jnp.tile(attn_qk_norm[l, 1], HEADS_PER_GROUP)[None]

        gates_row = proj[:, OFF_IF:OFF_IF + 8].astype(F32).reshape(b, s, 8).transpose(0, 2, 1)
        gb_col = jnp.zeros((1, IF_PAD), F32).at[0, :8].set(mlstm_gate_b[l])
        ya = _mlstm(proj, gates_row, mlstm_conv[l], gb_col, mlstm_gate_b[l].reshape(8, 1),
                    mlstm_norm[l][None], batch=b, seq=s, blk=MLSTM_BLOCK, nsub=MLSTM_NSUB,
                    group=MLSTM_GROUP)

        ybs, lses = [], []
        for g, (_, dilation) in enumerate(ATTN_PATTERNS):
            aproj = _attnproj(h_mix, w_attn, seg_ones, gq, gk, group=g, dilation=dilation)
            o, lse = _dattn(aproj, biases[g], seq=s, group=g, dilation=dilation)
            ybs.append(o)
            lses.append(lse)

        ws = jnp.where(tril, gmlp_ws[l], 0.0).astype(BF16)
        bsb = jnp.broadcast_to(gmlp_bs[l][:, :, None], (GMLP_GROUPS, GMLP_CHUNK, GMLP_GC)).astype(F32)
        x2d = _merge(ya, ybs, lses, proj, x2d, w_branch_a[l].astype(BF16), w_branch_b[l].astype(BF16),
                     w_branch_c[l].astype(BF16), w_out[l].astype(BF16), ws, bsb, gmlp_norm[l][None], tm=256)

        k_mem, v_mem = _memkv(mem, norm_mem[l][None], w_xkv[l].astype(BF16), xattn_qk_norm[l, 1][None])
        x2d, hf_rows, eidx, wts = _xattn(x2d, k_mem, v_mem, norm_xattn[l][None], w_xq[l].astype(BF16),
                                         xattn_qk_norm[l, 0][None], w_xo[l].astype(BF16), norm_ffn[l][None],
                                         rw_t, rb, seq=s, tm=512)

        x2d = _moe(x2d, hf_rows, eidx, wts, w_expert_gate, w_expert_up, w_expert_down, layer=l)

    return x2d.reshape(b, s, d)
```

```python
import functools
import math

import jax
import jax.numpy as jnp
import numpy as np
from jax import lax
from jax.experimental import pallas as pl
from jax.experimental.pallas import tpu as pltpu
from jax.experimental.pallas import tpu_sc as plsc

F32 = jnp.float32
BF16 = jnp.bfloat16

EPS = 1e-6
NEG = -1e30

MLSTM_HEADS = 4
MLSTM_DH = 128
MLSTM_W = MLSTM_HEADS * MLSTM_DH
CONV_WIDTH = 4
MLSTM_BLOCK = 128
MLSTM_NSUB = 1
MLSTM_GROUP = 2

ATTN_PATTERNS = ((128, 1), (512, 4), (2048, 16))
HEADS_PER_GROUP = 4
ATTN_DH = 64
ATTN_GW = HEADS_PER_GROUP * ATTN_DH
ATTN_W = len(ATTN_PATTERNS) * ATTN_GW
ATTN_BLOCK = 128
REL_BUCKETS = 32
REL_MAX_DIST = 2048

GMLP_GROUPS = 4
GMLP_GC = 128
GMLP_W = GMLP_GROUPS * GMLP_GC
GMLP_CHUNK = 128

XATTN_HEADS = 4
XATTN_DH = 128
XATTN_W = XATTN_HEADS * XATTN_DH

N_EXPERTS = 16
N_EXPERT_GROUPS = 4
EXPERTS_PER_GROUP = 4
ROUTER_ROWS = 8 * N_EXPERT_GROUPS

N_BRANCH = 3

MOE_TM = 512
ROW_CHUNKS = 4
SC_CORES, SC_SUBCORES = 2, 16
SC_WINDOW = 128

OFF_MQ, OFF_MK, OFF_MV, OFF_MO = 0, 512, 1024, 1536
OFF_GU, OFF_GV = 2048, 2560
OFF_GATE = 3072
OFF_IF = 6144
IF_PAD = 256
N_PROJ = OFF_IF + IF_PAD

ATTN_TILE = 2048
ATTN_SUB = ATTN_TILE // ATTN_BLOCK
ATTN_SLAB = 2 * ATTN_DH
ATTN_COLS = HEADS_PER_GROUP * ATTN_SLAB + 2 * ATTN_GW

VMEM_LIMIT = 48 * 1024 * 1024


def _cparams(*sem, flags=None):
    return pltpu.CompilerParams(dimension_semantics=sem, vmem_limit_bytes=VMEM_LIMIT, flags=flags)


def _rms(x, gain):
    return x * lax.rsqrt(jnp.mean(x * x, axis=-1, keepdims=True) + EPS) * gain


def _dot(a, b):
    return jnp.dot(a, b, preferred_element_type=F32)


def _dot_nt(a, b):
    return lax.dot_general(a, b, (((1,), (1,)), ((), ())), preferred_element_type=F32)


def _inproj_kernel(x_ref, g_ref, w_ref, o_ref, h_ref):
    @pl.when(pl.program_id(1) == 0)
    def _():
        h_ref[...] = _rms(x_ref[...], g_ref[...]).astype(BF16)

    o_ref[...] = _dot(h_ref[...], w_ref[0]).astype(o_ref.dtype)


def _inproj(x2d, gain, w, *, layer, tm, tn):
    t, d = x2d.shape
    n = w.shape[2]
    return pl.pallas_call(
        _inproj_kernel,
        grid=(t // tm, n // tn),
        in_specs=[pl.BlockSpec((tm, d), lambda i, j: (i, 0)),
                  pl.BlockSpec((1, d), lambda i, j: (0, 0)),
                  pl.BlockSpec((1, d, tn), lambda i, j: (layer, 0, j))],
        out_specs=[pl.BlockSpec((tm, tn), lambda i, j: (i, j)),
                   pl.BlockSpec((tm, d), lambda i, j: (i, 0))],
        out_shape=[jax.ShapeDtypeStruct((t, n), BF16), jax.ShapeDtypeStruct((t, d), BF16)],
        compiler_params=_cparams("parallel", "arbitrary"),
        name="inproj",
    )(x2d, gain, w)


def _log_sigmoid(x):
    return jnp.minimum(x, 0.0) - jnp.log(1.0 + jnp.exp(-jnp.abs(x)))


def _mlstm_kernel(qk_ref, v_ref, og_ref, gc_ref, gr_ref, cw_ref, gbc_ref, gbr_ref, ng_ref, y_ref,
                  xe_scr, s_scr, m_scr, *, blk, nsub, group):
    heads, w = MLSTM_HEADS, MLSTM_W

    @pl.when(pl.program_id(1) == 0)
    def _():
        xe_scr[:, 0:8, :] = jnp.zeros((group, 8, 2 * w), F32)
        s_scr[...] = jnp.zeros_like(s_scr)
        m_scr[...] = jnp.zeros_like(m_scr)

    cw = cw_ref[...]
    ri = lax.broadcasted_iota(jnp.int32, (blk, blk), 0)
    ci = lax.broadcasted_iota(jnp.int32, (blk, blk), 1)
    causal = ri >= ci
    tril = causal.astype(BF16)
    triu = (ri <= ci).astype(BF16)
    states = []
    for g in range(group):
        xe_scr[g, 8:8 + nsub * blk, :] = qk_ref[g].astype(F32)
        states.append([(s_scr[g, h], m_scr[g, h:h + 1, 0:1]) for h in range(heads)])
    for c in range(nsub):
        for g in range(group):
            states[g] = _mlstm_chunk(c * blk, blk, states[g], cw, causal, tril, triu, xe_scr.at[g], v_ref.at[g],
                                     og_ref.at[g], gc_ref.at[g], gr_ref.at[g], gbc_ref, gbr_ref, ng_ref,
                                     y_ref.at[g])
    for g in range(group):
        xe_scr[g, 0:8, :] = xe_scr[g, nsub * blk:nsub * blk + 8, :]
        for h, (s_st, m_st) in enumerate(states[g]):
            s_scr[g, h] = s_st
            m_scr[g, h:h + 1, :] = jnp.broadcast_to(m_st, (1, m_scr.shape[2]))


def _split_bf16(x):
    hi = x.astype(BF16)
    return hi, (x - hi.astype(F32)).astype(BF16)


def _mlstm_chunk(r0, blk, state, cw, causal, tril, triu, xe_scr, v_ref, og_ref, gc_ref, gr_ref, gbc_ref,
                 gbr_ref, ng_ref, y_ref):
    heads, dh, w = MLSTM_HEADS, MLSTM_DH, MLSTM_W
    rows = slice(r0, r0 + blk)
    conv = cw[CONV_WIDTH - 1:CONV_WIDTH, :] * xe_scr[8 + r0:8 + r0 + blk, :]
    for j in range(CONV_WIDTH - 1):
        off = 8 + r0 - (CONV_WIDTH - 1) + j
        conv = conv + cw[j:j + 1, :] * xe_scr[off:off + blk, :]
    qk = conv * jax.nn.sigmoid(conv)

    gcol = gc_ref[rows, :].astype(F32) + gbc_ref[...]
    grow = gr_ref[:, rows] + gbr_ref[...]
    lc_hi, lc_lo = _split_bf16(_log_sigmoid(gcol))
    lr_hi, lr_lo = _split_bf16(_log_sigmoid(grow))
    bcol = _dot(tril, lc_hi) + _dot(tril, lc_lo)
    brow = _dot(lr_hi, triu) + _dot(lr_lo, triu)
    ones = jnp.ones((blk, dh), BF16)

    new_state = []
    for h in range(heads):
        sl = slice(h * dh, (h + 1) * dh)
        b_c = bcol[:, heads + h:heads + h + 1]
        i_c = gcol[:, h:h + 1]
        b_r = brow[heads + h:heads + h + 1, :]
        i_r = grow[h:h + 1, :]
        s_st, m_st = state[h]

        d_mat = jnp.where(causal, b_c - b_r + i_r, NEG)
        inter = b_c + m_st
        m_t = jnp.maximum(inter, jnp.max(d_mat, axis=-1, keepdims=True))
        w_intra = jnp.exp(d_mat - m_t)
        w_inter = jnp.exp(inter - m_t)

        q_f = qk[:, sl]
        k_f = qk[:, w + h * dh:w + (h + 1) * dh] * (dh ** -0.5)
        q_b = q_f.astype(BF16)
        k_b = k_f.astype(BF16)
        v_ext = jnp.concatenate([v_ref[rows, sl], ones], axis=-1)

        s = _dot_nt(q_b, k_b) * w_intra
        tot = _dot(s.astype(BF16), v_ext) + w_inter * _dot(q_b, s_st.astype(BF16))
        num, den = tot[:, :dh], tot[:, dh:]
        hh = num / jnp.maximum(jnp.abs(den), jnp.exp(-m_t))
        hn = _rms(hh, ng_ref[:, sl])
        y_ref[rows, sl] = (hn * jax.nn.sigmoid(og_ref[rows, sl].astype(F32))).astype(y_ref.dtype)

        b_last = b_c[blk - 1:blk, :]
        dec = b_last - b_c + i_c
        m_new = jnp.maximum(b_last + m_st, jnp.max(dec, axis=0, keepdims=True))
        w_k = jnp.exp(dec - m_new)
        w_c = jnp.exp(b_last + m_st - m_new)
        kw = k_f * w_k
        new_state.append((w_c * s_st + _dot(kw.T.astype(BF16), v_ext), m_new))
    return new_state


def _mlstm(proj, gates_row, conv_w, gb_col, gb_row, norm_g, *, batch, seq, blk, nsub, group):
    t, npj = proj.shape
    rows = blk * nsub
    w = MLSTM_W
    proj3 = proj.reshape(batch, seq, npj)
    cols = lambda c: (lambda b, i: (b, i, c))
    const2 = lambda b, i: (0, 0)
    y = pl.pallas_call(
        functools.partial(_mlstm_kernel, blk=blk, nsub=nsub, group=group),
        grid=(batch // group, seq // rows),
        in_specs=[pl.BlockSpec((group, rows, 2 * w), cols(OFF_MQ // (2 * w))),
                  pl.BlockSpec((group, rows, w), cols(OFF_MV // w)),
                  pl.BlockSpec((group, rows, w), cols(OFF_MO // w)),
                  pl.BlockSpec((group, rows, IF_PAD), cols(OFF_IF // IF_PAD)),
                  pl.BlockSpec((group, 8, rows), lambda b, i: (b, 0, i)),
                  pl.BlockSpec((CONV_WIDTH, 2 * w), const2),
                  pl.BlockSpec((1, IF_PAD), const2),
                  pl.BlockSpec((8, 1), const2),
                  pl.BlockSpec((1, w), const2)],
        out_specs=pl.BlockSpec((group, rows, w), cols(0)),
        out_shape=jax.ShapeDtypeStruct((batch, seq, w), BF16),
        scratch_shapes=[pltpu.VMEM((group, rows + 8, 2 * w), F32),
                        pltpu.VMEM((group, MLSTM_HEADS, MLSTM_DH, 2 * MLSTM_DH), F32),
                        pltpu.VMEM((group, 8, 128), F32)],
        compiler_params=_cparams("parallel", "arbitrary"),
        name="mlstm",
    )(proj3, proj3, proj3, proj3, gates_row, conv_w, gb_col, gb_row, norm_g)
    return y.reshape(t, w)


def _attnproj_kernel(h_ref, w_ref, seg_ref, gq_ref, gk_ref, o_ref, r_scr, *, dil):
    gw, half = ATTN_GW, ATTN_SLAB // 2
    sub_rows = r_scr.shape[2]
    seg, sub_seg = ATTN_TILE // dil, sub_rows // dil

    def head_norm(x, gain):
        sq = x * x
        hi = sq.astype(BF16)
        lo = (sq - hi.astype(F32)).astype(BF16)
        ss = _dot(hi, seg_ref[...]) + _dot(lo, seg_ref[...])
        return x * lax.rsqrt(ss * (1.0 / ATTN_DH) + EPS) * gain

    low = lax.broadcasted_iota(jnp.int32, (1, ATTN_SLAB), 1) < half
    for s in range(ATTN_TILE // sub_rows):
        rows = slice(s * sub_rows, (s + 1) * sub_rows)
        res = _dot(h_ref[rows, :], w_ref[0])
        q = head_norm(res[:, :gw], gq_ref[...]) * (ATTN_DH ** -0.5)
        k = head_norm(res[:, gw:2 * gw], gk_ref[...])
        slabs = []
        for pair in range(gw // ATTN_SLAB):
            qp = q[:, pair * ATTN_SLAB:(pair + 1) * ATTN_SLAB]
            slabs += [jnp.where(low, qp, 0.0), jnp.where(low, 0.0, qp)]
        slabs += [k[:, c * 128:(c + 1) * 128] for c in range(gw // 128)]
        slabs += [res[:, 2 * gw + c * 128:2 * gw + (c + 1) * 128] for c in range(gw // 128)]
        for c, slab in enumerate(slabs):
            if dil == 1:
                o_ref[rows, c * 128:(c + 1) * 128] = slab.astype(o_ref.dtype)
            else:
                r_scr[s % 2, c] = slab
        if dil > 1:
            for r in range(dil):
                dst = slice(r * seg + s * sub_seg, r * seg + (s + 1) * sub_seg)
                for c in range(r_scr.shape[1]):
                    o_ref[dst, c * 128:(c + 1) * 128] = (
                        r_scr[s % 2, c, pl.ds(r, sub_seg, stride=dil), :].astype(o_ref.dtype))


def _attnproj(h, w, seg_ones, gq, gk, *, layer, group, dilation):
    t, d = h.shape
    wcols = 3 * ATTN_GW
    const2 = lambda i: (0, 0)
    return pl.pallas_call(
        functools.partial(_attnproj_kernel, dil=dilation),
        grid=(t // ATTN_TILE,),
        in_specs=[pl.BlockSpec((ATTN_TILE, d), lambda i: (i, 0)),
                  pl.BlockSpec((1, d, wcols), lambda i: (layer, 0, group)),
                  pl.BlockSpec((ATTN_GW, ATTN_GW), const2),
                  pl.BlockSpec((1, ATTN_GW), const2), pl.BlockSpec((1, ATTN_GW), const2)],
        out_specs=pl.BlockSpec((ATTN_TILE, ATTN_COLS), lambda i: (i, 0)),
        out_shape=jax.ShapeDtypeStruct((t, ATTN_COLS), BF16),
        scratch_shapes=[pltpu.VMEM((2, ATTN_COLS // 128, 512, 128), F32)],
        compiler_params=_cparams("parallel"),
        name=f"attnproj{group}",
    )(h, w, seg_ones, gq, gk)


def _dattn_kernel(q_ref, kc_ref, kp_ref, vc_ref, vp_ref, bias_ref, o_ref, lse_ref,
                  kx_scr, vx_scr, o_scr, l_scr, *, dil):
    blk = ATTN_BLOCK
    per = ATTN_SUB // dil
    first_tile = pl.program_id(1) == 0
    for r in range(dil):
        base = r * (per + 1) * blk
        last = slice((r * per + per - 1) * blk, (r * per + per) * blk)
        mine = slice(r * per * blk, (r + 1) * per * blk)
        kx_scr[base:base + blk, :] = kp_ref[last, :]
        vx_scr[base:base + blk, :] = vp_ref[last, :]
        kx_scr[base + blk:base + (per + 1) * blk, :] = kc_ref[mine, :]
        vx_scr[base + blk:base + (per + 1) * blk, :] = vc_ref[mine, :]

    low = lax.broadcasted_iota(jnp.int32, (1, ATTN_SLAB), 1) < ATTN_SLAB // 2
    no_prev = lax.broadcasted_iota(jnp.int32, (1, 2 * blk), 1) < blk
    for r in range(dil):
        for sub in range(per):
            u = r * per + sub
            win = slice((r * (per + 1) + sub) * blk, (r * (per + 1) + sub + 2) * blk)
            o_slabs, l_slabs = [], []
            for pair in range(ATTN_GW // ATTN_SLAB):
                cols = slice(pair * ATTN_SLAB, (pair + 1) * ATTN_SLAB)
                kx, vx = kx_scr[win, cols], vx_scr[win, cols]
                o_pair, l_pair = [], []
                for h in (2 * pair, 2 * pair + 1):
                    logits = _dot_nt(q_ref[u * blk:(u + 1) * blk, h * ATTN_SLAB:(h + 1) * ATTN_SLAB], kx)
                    logits = logits + bias_ref[h]
                    if sub == 0:
                        logits = jnp.where(first_tile & no_prev, NEG, logits)
                    m = jnp.max(logits, axis=-1, keepdims=True)
                    p = jnp.exp(logits - m)
                    l = jnp.sum(p, axis=-1, keepdims=True)
                    o_pair.append(_dot(p.astype(BF16), vx) / l)
                    l_pair.append(m + jnp.log(l))
                o_slabs.append(jnp.where(low, o_pair[0], o_pair[1]))
                l_slabs.append(jnp.where(low, l_pair[0], l_pair[1]))
            dst = pl.ds(sub * blk * dil + r, blk, stride=dil) if dil > 1 else slice(u * blk, (u + 1) * blk)
            for c in range(ATTN_GW // ATTN_SLAB):
                o_scr[c, dst, :] = o_slabs[c]
                l_scr[c, dst, :] = l_slabs[c]
    for c in range(ATTN_GW // ATTN_SLAB):
        o_ref[:, c * ATTN_SLAB:(c + 1) * ATTN_SLAB] = o_scr[c].astype(o_ref.dtype)
        lse_ref[:, c * ATTN_SLAB:(c + 1) * ATTN_SLAB] = l_scr[c]


def _dattn(aproj, bias, *, seq, group, dilation):
    t = aproj.shape[0]
    tiles = seq // ATTN_TILE
    qw = HEADS_PER_GROUP * ATTN_SLAB
    cq, ck, cv = 0, qw // ATTN_GW, qw // ATTN_GW + 1
    blk = (ATTN_TILE, ATTN_GW)
    cur = lambda c: (lambda b, j: (b * tiles + j, c))
    prev = lambda c: (lambda b, j: (b * tiles + jnp.maximum(j - 1, 0), c))
    xrows = ATTN_TILE + dilation * ATTN_BLOCK
    return pl.pallas_call(
        functools.partial(_dattn_kernel, dil=dilation),
        grid=(t // seq, tiles),
        in_specs=[pl.BlockSpec((ATTN_TILE, qw), cur(cq)),
                  pl.BlockSpec(blk, cur(ck)), pl.BlockSpec(blk, prev(ck)),
                  pl.BlockSpec(blk, cur(cv)), pl.BlockSpec(blk, prev(cv)),
                  pl.BlockSpec((HEADS_PER_GROUP, ATTN_BLOCK, 2 * ATTN_BLOCK), lambda b, j: (0, 0, 0))],
        out_specs=[pl.BlockSpec(blk, cur(0)), pl.BlockSpec(blk, cur(0))],
        out_shape=[jax.ShapeDtypeStruct((t, ATTN_GW), BF16), jax.ShapeDtypeStruct((t, ATTN_GW), F32)],
        scratch_shapes=[pltpu.VMEM((xrows, ATTN_GW), BF16), pltpu.VMEM((xrows, ATTN_GW), BF16),
                        pltpu.VMEM((ATTN_GW // ATTN_SLAB, ATTN_TILE, ATTN_SLAB), F32),
                        pltpu.VMEM((ATTN_GW // ATTN_SLAB, ATTN_TILE, ATTN_SLAB), F32)],
        compiler_params=_cparams("parallel", "arbitrary"),
        name=f"dattn{group}",
    )(aproj, aproj, aproj, aproj, aproj, bias)


def _rel_bucket(n):
    max_exact = REL_BUCKETS // 2
    nf = jnp.maximum(n, 1).astype(F32)
    log_b = max_exact + (jnp.log(nf / max_exact) / math.log(REL_MAX_DIST / max_exact)
                         * (REL_BUCKETS - max_exact)).astype(jnp.int32)
    return jnp.where(n < max_exact, n, jnp.minimum(log_b, REL_BUCKETS - 1))


def _attn_bias(rel_bias, group):
    window, dilation = ATTN_PATTERNS[group]
    steps = window // dilation
    hp = lax.Precision.HIGHEST
    hs = slice(group * HEADS_PER_GROUP, (group + 1) * HEADS_PER_GROUP)
    bucket = _rel_bucket(jnp.arange(steps + 1) * dilation)
    bias_steps = jnp.dot(jax.nn.one_hot(bucket, REL_BUCKETS, dtype=F32), rel_bias[:, hs].astype(F32),
                         precision=hp)
    qi = jnp.arange(ATTN_BLOCK)[:, None]
    ki = jnp.arange(2 * ATTN_BLOCK)[None, :]
    dist = ATTN_BLOCK + qi - ki
    ok = (dist >= 0) & (dist <= steps)
    sel = jax.nn.one_hot(jnp.clip(dist, 0, steps).reshape(-1), steps + 1, dtype=F32)
    bias = jnp.dot(sel, bias_steps, precision=hp).T.reshape(HEADS_PER_GROUP, ATTN_BLOCK, 2 * ATTN_BLOCK)
    return jnp.where(ok[None], bias, NEG)


def _merge_kernel(ya_ref, yb0_ref, yb1_ref, yb2_ref, l0_ref, l1_ref, l2_ref, gu_ref, gv_ref, gate_ref,
                  x_ref, wa_ref, wb_ref, wc_ref, wo_ref, ws_ref, bs_ref, gg_ref, o_ref, yc_scr, *, tm):
    d = x_ref.shape[1]
    l0, l1, l2 = l0_ref[...], l1_ref[...], l2_ref[...]
    mx = jnp.maximum(jnp.maximum(l0, l1), l2)
    e0, e1, e2 = jnp.exp(l0 - mx), jnp.exp(l1 - mx), jnp.exp(l2 - mx)
    inv = 1.0 / (e0 + e1 + e2)
    yb = jnp.concatenate([(yb0_ref[...].astype(F32) * (e0 * inv)).astype(BF16),
                          (yb1_ref[...].astype(F32) * (e1 * inv)).astype(BF16),
                          (yb2_ref[...].astype(F32) * (e2 * inv)).astype(BF16)], axis=-1)

    for j in range(tm // GMLP_CHUNK):
        rows = slice(j * GMLP_CHUNK, (j + 1) * GMLP_CHUNK)
        for g in range(GMLP_GROUPS):
            cols = slice(g * GMLP_GC, (g + 1) * GMLP_GC)
            u = jax.nn.gelu(gu_ref[rows, cols].astype(F32))
            v = _rms(jax.nn.gelu(gv_ref[rows, cols].astype(F32)), gg_ref[:, cols])
            mixed = _dot(ws_ref[g], v.astype(BF16)) + bs_ref[g]
            yc_scr[rows, cols] = (u * mixed).astype(BF16)

    merged = jax.nn.sigmoid(gate_ref[:, 0:d].astype(F32)) * _dot(ya_ref[...], wa_ref[...])
    merged = merged + jax.nn.sigmoid(gate_ref[:, d:2 * d].astype(F32)) * _dot(yb, wb_ref[...])
    merged = merged + jax.nn.sigmoid(gate_ref[:, 2 * d:3 * d].astype(F32)) * _dot(yc_scr[...], wc_ref[...])
    o_ref[...] = x_ref[...] + _dot(merged.astype(BF16), wo_ref[...])


def _merge(ya, ybs, lses, proj, x2d, wa, wb, wc, wo, ws, bsb, gg, *, tm):
    t, d = x2d.shape
    row = lambda c: (lambda i: (i, c))
    full2 = lambda i: (0, 0)
    full3 = lambda i: (0, 0, 0)
    gspec = pl.BlockSpec((tm, ATTN_GW), row(0))
    return pl.pallas_call(
        functools.partial(_merge_kernel, tm=tm),
        grid=(t // tm,),
        in_specs=[pl.BlockSpec((tm, MLSTM_W), row(0)),
                  gspec, gspec, gspec, gspec, gspec, gspec,
                  pl.BlockSpec((tm, GMLP_W), row(OFF_GU // GMLP_W)),
                  pl.BlockSpec((tm, GMLP_W), row(OFF_GV // GMLP_W)),
                  pl.BlockSpec((tm, N_BRANCH * d), row(OFF_GATE // (N_BRANCH * d))),
                  pl.BlockSpec((tm, d), row(0)),
                  pl.BlockSpec(wa.shape, full2), pl.BlockSpec(wb.shape, full2),
                  pl.BlockSpec(wc.shape, full2), pl.BlockSpec(wo.shape, full2),
                  pl.BlockSpec(ws.shape, full3), pl.BlockSpec(bsb.shape, full3),
                  pl.BlockSpec(gg.shape, full2)],
        out_specs=pl.BlockSpec((tm, d), row(0)),
        out_shape=jax.ShapeDtypeStruct((t, d), F32),
        scratch_shapes=[pltpu.VMEM((tm, GMLP_W), BF16)],
        compiler_params=_cparams("parallel"),
        name="merge",
    )(ya, *ybs, *lses, proj, proj, proj, x2d, wa, wb, wc, wo, ws, bsb, gg)


def _memkv_kernel(mem_ref, g_ref, w_ref, gk_ref, k_ref, v_ref):
    dh, w = XATTN_DH, XATTN_W
    kv = _dot(_rms(mem_ref[0], g_ref[...]).astype(BF16), w_ref[...])
    for h in range(XATTN_HEADS):
        sl = slice(h * dh, (h + 1) * dh)
        k_ref[0, :, sl] = _rms(kv[:, sl], gk_ref[...]).astype(k_ref.dtype)
    v_ref[0] = kv[:, w:].astype(v_ref.dtype)


def _memkv(mem, gain, w_kv, gk):
    b, m, d = mem.shape
    full2 = lambda i: (0, 0)
    return pl.pallas_call(
        _memkv_kernel,
        grid=(b,),
        in_specs=[pl.BlockSpec((1, m, d), lambda i: (i, 0, 0)),
                  pl.BlockSpec((1, d), full2),
                  pl.BlockSpec(w_kv.shape, full2),
                  pl.BlockSpec((1, XATTN_DH), full2)],
        out_specs=[pl.BlockSpec((1, m, XATTN_W), lambda i: (i, 0, 0)),
                   pl.BlockSpec((1, m, XATTN_W), lambda i: (i, 0, 0))],
        out_shape=[jax.ShapeDtypeStruct((b, m, XATTN_W), BF16),
                   jax.ShapeDtypeStruct((b, m, XATTN_W), BF16)],
        compiler_params=_cparams("parallel"),
        name="memkv",
    )(mem, gain, w_kv, gk)


def _route(logits):
    tm = logits.shape[1]
    e = jnp.exp(logits - jnp.max(logits, axis=0, keepdims=True))
    probs = e / jnp.sum(e, axis=0, keepdims=True)
    rowi = lax.broadcasted_iota(jnp.int32, (8, tm), 0)
    real = rowi < EXPERTS_PER_GROUP
    tops = []
    for g in range(N_EXPERT_GROUPS):
        pg = jnp.where(real, probs[8 * g:8 * g + 8, :], -0.5)
        m1 = jnp.max(pg, axis=0, keepdims=True)
        i1 = jnp.min(jnp.where(pg == m1, rowi, 8), axis=0, keepdims=True)
        pg2 = jnp.where(rowi == i1, -1.0, pg)
        m2 = jnp.max(pg2, axis=0, keepdims=True)
        i2 = jnp.min(jnp.where(pg2 == m2, rowi, 8), axis=0, keepdims=True)
        tops.append((m1, i1, m2, i2))
    best = jnp.zeros((1, tm), jnp.int32)
    best_score = tops[0][0] + tops[0][2]
    for g in range(1, N_EXPERT_GROUPS):
        score = tops[g][0] + tops[g][2]
        better = score > best_score
        best = jnp.where(better, g, best)
        best_score = jnp.where(better, score, best_score)
    m1, i1, m2, i2 = tops[0]
    for g in range(1, N_EXPERT_GROUPS):
        m1, i1, m2, i2 = (jnp.where(best == g, new, old) for new, old in zip(tops[g], (m1, i1, m2, i2)))
    tot = m1 + m2
    base = best * EXPERTS_PER_GROUP
    return base + i1, base + i2, m1 / tot, m2 / tot


def _pack_bf16_pairs(x):
    n = x.shape[1] // 2
    hi = lax.bitcast_convert_type(x[:, :n].astype(BF16).astype(F32), jnp.uint32)
    lo = lax.bitcast_convert_type(x[:, n:].astype(BF16).astype(F32), jnp.uint32)
    return hi | (lo >> 16)


def _unpack_bf16_pairs(p):
    hi = lax.bitcast_convert_type(p & jnp.uint32(0xFFFF0000), F32)
    lo = lax.bitcast_convert_type(p << 16, F32)
    return hi, lo


def _store_row_chunks(ref, packed):
    for j in range(ROW_CHUNKS):
        ref[j] = packed[:, j * 128:(j + 1) * 128]


def _load_row_chunks(ref):
    return jnp.concatenate([ref[j] for j in range(ROW_CHUNKS)], axis=-1)


def _xattn_kernel(x_ref, k_ref, v_ref, gx_ref, wq_ref, gq_ref, wo_ref, gf_ref, rw_ref, rb_ref,
                  xo_ref, hf_ref, eidx_ref, wts_ref):
    dh = XATTN_DH
    x = x_ref[...]
    q = _dot(_rms(x, gx_ref[...]).astype(BF16), wq_ref[...])
    outs = []
    for h in range(XATTN_HEADS):
        sl = slice(h * dh, (h + 1) * dh)
        q_h = (_rms(q[:, sl], gq_ref[...]) * (dh ** -0.5)).astype(BF16)
        logits = _dot_nt(q_h, k_ref[0, :, sl])
        p = jnp.exp(logits - jnp.max(logits, axis=-1, keepdims=True))
        o = _dot(p.astype(BF16), v_ref[0, :, sl]) / jnp.sum(p, axis=-1, keepdims=True)
        outs.append(o.astype(BF16))
    xn = x + _dot(jnp.concatenate(outs, axis=-1), wo_ref[...])
    xo_ref[...] = xn
    hf = _rms(xn, gf_ref[...])
    _store_row_chunks(hf_ref, _pack_bf16_pairs(hf))
    rw = rw_ref[...]
    rw_hi = rw.astype(BF16)
    rw_lo = (rw - rw_hi.astype(F32)).astype(BF16)
    hf_hi = hf.astype(BF16)
    hf_lo = (hf - hf_hi.astype(F32)).astype(BF16)
    logits_t = _dot_nt(rw_hi, hf_hi) + _dot_nt(rw_hi, hf_lo) + _dot_nt(rw_lo, hf_hi) + rb_ref[...]
    e1, e2, w1, w2 = _route(logits_t)
    tm = x.shape[0]
    eidx_ref[...] = jnp.concatenate([e1, e2, jnp.zeros((6, tm), jnp.int32)], axis=0)
    wts_ref[...] = jnp.concatenate([w1, w2, jnp.zeros((6, tm), F32)], axis=0)


def _xattn(x2d, k, v, gx, wq, gq, wo, gf, rw_t, rb, *, seq, tm):
    t, d = x2d.shape
    per_b = seq // tm
    full2 = lambda i: (0, 0)
    kv_spec = pl.BlockSpec((1,) + k.shape[1:], lambda i: (i // per_b, 0, 0))
    return pl.pallas_call(
        _xattn_kernel,
        grid=(t // tm,),
        in_specs=[pl.BlockSpec((tm, d), lambda i: (i, 0)), kv_spec, kv_spec,
                  pl.BlockSpec((1, d), full2), pl.BlockSpec(wq.shape, full2),
                  pl.BlockSpec((1, XATTN_DH), full2), pl.BlockSpec(wo.shape, full2),
                  pl.BlockSpec((1, d), full2), pl.BlockSpec(rw_t.shape, full2),
                  pl.BlockSpec(rb.shape, full2)],
        out_specs=[pl.BlockSpec((tm, d), lambda i: (i, 0)),
                   pl.BlockSpec((ROW_CHUNKS, tm, 128), lambda i: (0, i, 0)),
                   pl.BlockSpec((8, tm), lambda i: (0, i)),
                   pl.BlockSpec((8, tm), lambda i: (0, i))],
        out_shape=[jax.ShapeDtypeStruct((t, d), F32),
                   jax.ShapeDtypeStruct((ROW_CHUNKS, t, 128), jnp.uint32),
                   jax.ShapeDtypeStruct((8, t), jnp.int32),
                   jax.ShapeDtypeStruct((8, t), F32)],
        compiler_params=_cparams("parallel"),
        name="xattn_router",
    )(x2d, k, v, gx, wq, gq, wo, gf, rw_t, rb)


def _moe_plan_kernel(eidx_ref, i1_ref, i2_ref, te_ref, na_ref, cnt_scr, carry_scr, *, tb, tm, plane_rows):
    ne = N_EXPERTS
    hp = lax.Precision.HIGHEST
    phase, j = pl.program_id(0), pl.program_id(1)
    rows = lax.broadcasted_iota(jnp.int32, (ne, tb), 0)
    oh1 = rows == eidx_ref[0:1, :]
    oh2 = rows == eidx_ref[1:2, :]
    a = oh1.astype(F32) + oh2.astype(F32)
    blk_cnt = jnp.broadcast_to(jnp.sum(a, axis=1, keepdims=True), cnt_scr.shape)

    @pl.when((phase == 0) & (j == 0))
    def _():
        cnt_scr[...] = jnp.zeros_like(cnt_scr)

    @pl.when(phase == 0)
    def _():
        cnt_scr[...] += blk_cnt

    @pl.when((phase == 1) & (j == 0))
    def _():
        padded = jnp.ceil(cnt_scr[...] * (1.0 / tm)) * tm
        er = lax.broadcasted_iota(jnp.int32, (ne, ne), 0)
        ec = lax.broadcasted_iota(jnp.int32, (ne, ne), 1)
        off = jnp.dot((ec < er).astype(F32), padded, precision=hp, preferred_element_type=F32)
        carry_scr[...] = off
        seg_end = (off + padded)[:, 0:1]
        tile_start = lax.broadcasted_iota(jnp.int32, (ne, te_ref.shape[1]), 1).astype(F32) * tm
        te = jnp.sum((seg_end <= tile_start).astype(F32), axis=0, keepdims=True)
        te_ref[...] = jnp.broadcast_to(jnp.minimum(te, ne - 1.0), te_ref.shape).astype(jnp.int32)
        total = jnp.sum(padded[:, 0:1], axis=0, keepdims=True)
        na_ref[...] = jnp.broadcast_to(total * (1.0 / tm), na_ref.shape).astype(jnp.int32)

    @pl.when(phase == 1)
    def _():
        before = (lax.broadcasted_iota(jnp.int32, (tb, tb), 0)
                  < lax.broadcasted_iota(jnp.int32, (tb, tb), 1)).astype(BF16)
        rank = carry_scr[:, 0:1] + _dot(a.astype(BF16), before)
        d1 = jnp.sum(jnp.where(oh1, rank, 0.0), axis=0, keepdims=True).astype(jnp.int32)
        d2 = jnp.sum(jnp.where(oh2, rank, 0.0), axis=0, keepdims=True).astype(jnp.int32)
        plane = lax.broadcasted_iota(jnp.int32, (8, tb), 0) * plane_rows
        i1_ref[...] = jnp.where(plane < ROW_CHUNKS * plane_rows, plane + d1, 0)
        i2_ref[...] = jnp.where(plane < ROW_CHUNKS * plane_rows, plane + d2, 0)
        carry_scr[...] += blk_cnt


def _moe_plan(eidx, *, tm, n_tiles, tb=512):
    t = eidx.shape[1]
    ntp = -(-n_tiles // 128) * 128
    return pl.pallas_call(
        functools.partial(_moe_plan_kernel, tb=tb, tm=tm, plane_rows=n_tiles * tm),
        grid=(2, t // tb),
        in_specs=[pl.BlockSpec((8, tb), lambda p, j: (0, j))],
        out_specs=[pl.BlockSpec((8, tb), lambda p, j: (0, j * p)),
                   pl.BlockSpec((8, tb), lambda p, j: (0, j * p)),
                   pl.BlockSpec((8, ntp), lambda p, j: (0, 0)),
                   pl.BlockSpec((8, 128), lambda p, j: (0, 0))],
        out_shape=[jax.ShapeDtypeStruct((8, t), jnp.int32),
                   jax.ShapeDtypeStruct((8, t), jnp.int32),
                   jax.ShapeDtypeStruct((8, ntp), jnp.int32),
                   jax.ShapeDtypeStruct((8, 128), jnp.int32)],
        scratch_shapes=[pltpu.VMEM((N_EXPERTS, 128), F32), pltpu.VMEM((N_EXPERTS, 128), F32)],
        compiler_params=_cparams("arbitrary", "arbitrary"),
        name="moe_plan",
    )(eidx)


def _sc_mesh():
    return plsc.VectorSubcoreMesh(core_axis_name="c", subcore_axis_name="s",
                                  num_cores=SC_CORES, num_subcores=SC_SUBCORES)


def _sc_index_spec(tokens):
    nb = tokens // SC_WINDOW
    return pl.BlockSpec((1, SC_WINDOW), lambda i: (i // nb, i % nb))


def _sc_dispatch(rows, i1, i2, n_out):
    n = rows.shape[0]
    tokens = i1.shape[1]

    @functools.partial(pl.kernel, out_type=jax.ShapeDtypeStruct((n_out, 128), rows.dtype), mesh=_sc_mesh(),
                       name="moe_dispatch")
    def k(x_hbm, i1_hbm, i2_hbm, o_hbm):
        def body(x_vmem, i1_vmem, i2_vmem):
            pltpu.sync_copy(x_vmem, o_hbm.at[i1_vmem.at[0]])
            pltpu.sync_copy(x_vmem, o_hbm.at[i2_vmem.at[0]])

        pltpu.emit_pipeline(
            body, grid=(n // SC_WINDOW,),
            in_specs=[pl.BlockSpec((SC_WINDOW, 128), lambda i: (i, 0)),
                      _sc_index_spec(tokens), _sc_index_spec(tokens)],
            out_specs=[],
            core_axis_name=("c", "s"), dimension_semantics=(pltpu.PARALLEL,),
        )(x_hbm, i1_hbm, i2_hbm)

    return k(rows, i1, i2)


def _sc_collect(table, i1, i2):
    tokens = i1.shape[1]
    n = ROW_CHUNKS * tokens
    out = jax.ShapeDtypeStruct((n, 128), table.dtype)

    @functools.partial(pl.kernel, out_type=(out, out), mesh=_sc_mesh(), name="moe_collect")
    def k(t_hbm, i1_hbm, i2_hbm, o1_hbm, o2_hbm):
        def body(i1_vmem, i2_vmem, o1_vmem, o2_vmem):
            pltpu.sync_copy(t_hbm.at[i1_vmem.at[0]], o1_vmem)
            pltpu.sync_copy(t_hbm.at[i2_vmem.at[0]], o2_vmem)

        pltpu.emit_pipeline(
            body, grid=(n // SC_WINDOW,),
            in_specs=[_sc_index_spec(tokens), _sc_index_spec(tokens)],
            out_specs=[pl.BlockSpec((SC_WINDOW, 128), lambda i: (i, 0)),
                       pl.BlockSpec((SC_WINDOW, 128), lambda i: (i, 0))],
            core_axis_name=("c", "s"), dimension_semantics=(pltpu.PARALLEL,),
        )(i1_hbm, i2_hbm, o1_hbm, o2_hbm)

    return k(table, i1, i2)


def _experts_kernel(te_ref, na_ref, xs_ref, wg_ref, wu_ref, wd_ref, y_ref, wg_scr, wu_scr, wd_scr):
    i = pl.program_id(0)
    active = i < na_ref[0]

    @pl.when(active & ((i == 0) | (te_ref[i] != te_ref[jnp.maximum(i - 1, 0)])))
    def _():
        wg_scr[...] = wg_ref[0, 0].astype(BF16)
        wu_scr[...] = wu_ref[0, 0].astype(BF16)
        wd_scr[...] = wd_ref[0, 0].astype(BF16)

    @pl.when(active)
    def _():
        hi, lo = _unpack_bf16_pairs(_load_row_chunks(xs_ref))
        h = jnp.concatenate([hi, lo], axis=-1).astype(BF16)
        up = _dot(h, wg_scr[...])
        act = up * jax.nn.sigmoid(up) * _dot(h, wu_scr[...])
        _store_row_chunks(y_ref, _pack_bf16_pairs(_dot(act.astype(BF16), wd_scr[...])))


def _experts(tile_expert, n_active, xs, wg, wu, wd, *, layer, tm):
    n_tiles = tile_expert.shape[0]
    _, _, d, dff = wg.shape
    rows = lambda i, te, na: (0, jnp.minimum(i, na[0] - 1), 0)
    expert = lambda i, te, na: (layer, te[i], 0, 0)
    return pl.pallas_call(
        _experts_kernel,
        grid_spec=pltpu.PrefetchScalarGridSpec(
            num_scalar_prefetch=2,
            grid=(n_tiles,),
            in_specs=[pl.BlockSpec((ROW_CHUNKS, tm, 128), rows),
                      pl.BlockSpec((1, 1, d, dff), expert),
                      pl.BlockSpec((1, 1, d, dff), expert),
                      pl.BlockSpec((1, 1, dff, d), expert)],
            out_specs=pl.BlockSpec((ROW_CHUNKS, tm, 128), rows),
            scratch_shapes=[pltpu.VMEM((d, dff), BF16), pltpu.VMEM((d, dff), BF16), pltpu.VMEM((dff, d), BF16)]),
        out_shape=jax.ShapeDtypeStruct(xs.shape, xs.dtype),
        compiler_params=_cparams("arbitrary"),
        name="moe_experts",
    )(tile_expert, n_active, xs, wg, wu, wd)


def _moe_combine_kernel(x_ref, y1_ref, y2_ref, w_ref, o_ref):
    half = x_ref.shape[1] // 2
    hi1, lo1 = _unpack_bf16_pairs(_load_row_chunks(y1_ref))
    hi2, lo2 = _unpack_bf16_pairs(_load_row_chunks(y2_ref))
    w1, w2 = w_ref[:, 0:1], w_ref[:, 1:2]
    o_ref[:, :half] = x_ref[:, :half] + w1 * hi1 + w2 * hi2
    o_ref[:, half:] = x_ref[:, half:] + w1 * lo1 + w2 * lo2


def _moe_combine(x2d, y1, y2, wcol, *, tm):
    t, d = x2d.shape
    chunk_spec = pl.BlockSpec((ROW_CHUNKS, tm, 128), lambda i: (0, i, 0))
    return pl.pallas_call(
        _moe_combine_kernel,
        grid=(t // tm,),
        in_specs=[pl.BlockSpec((tm, d), lambda i: (i, 0)), chunk_spec, chunk_spec,
                  pl.BlockSpec((tm, wcol.shape[1]), lambda i: (i, 0))],
        out_specs=pl.BlockSpec((tm, d), lambda i: (i, 0)),
        out_shape=jax.ShapeDtypeStruct((t, d), F32),
        compiler_params=_cparams("parallel"),
        name="moe_combine",
    )(x2d, y1, y2, wcol)


def _moe(x2d, hf_rows, eidx, wts, wg, wu, wd, *, layer):
    t = x2d.shape[0]
    tm = MOE_TM
    n_tiles = 2 * t // tm + N_EXPERTS
    plane = n_tiles * tm
    i1, i2, te, na = _moe_plan(eidx, tm=tm, n_tiles=n_tiles)
    xs = _sc_dispatch(hf_rows.reshape(ROW_CHUNKS * t, 128), i1, i2, ROW_CHUNKS * plane)
    ys = _experts(te[0, :n_tiles], na[0, :1], xs.reshape(ROW_CHUNKS, plane, 128), wg, wu, wd,
                  layer=layer, tm=tm)
    y1, y2 = _sc_collect(ys.reshape(ROW_CHUNKS * plane, 128), i1, i2)
    return _moe_combine(x2d, y1.reshape(ROW_CHUNKS, t, 128), y2.reshape(ROW_CHUNKS, t, 128), wts[:2].T, tm=512)


def _w_in_layout_kernel(w_ref, main_ref, attn_ref):
    w = w_ref[0]
    src_if = 4 * MLSTM_W
    src_a = src_if + 2 * MLSTM_HEADS
    src_g = src_a + 3 * ATTN_W
    main_ref[0, :, OFF_MQ:OFF_GU] = w[:, 0:src_if].astype(BF16)
    main_ref[0, :, OFF_GU:OFF_IF] = w[:, src_g:src_g + OFF_IF - OFF_GU].astype(BF16)
    first = w[:, src_if:src_if + 128]
    lane = lax.broadcasted_iota(jnp.int32, first.shape, 1)
    main_ref[0, :, OFF_IF:OFF_IF + 128] = jnp.where(lane < 2 * MLSTM_HEADS, first, 0.0).astype(BF16)
    main_ref[0, :, OFF_IF + 128:N_PROJ] = jnp.zeros((w.shape[0], IF_PAD - 128), BF16)
    for g in range(len(ATTN_PATTERNS)):
        for j in range(3):
            src = src_a + j * ATTN_W + g * ATTN_GW
            dst = (3 * g + j) * ATTN_GW
            attn_ref[0, :, dst:dst + ATTN_GW] = w[:, src:src + ATTN_GW].astype(BF16)


def _w_in_layout(w_in, *, rows=256):
    depth, d, n_in = w_in.shape
    n_attn = 3 * ATTN_W
    return pl.pallas_call(
        _w_in_layout_kernel,
        grid=(depth, d // rows),
        in_specs=[pl.BlockSpec((1, rows, n_in), lambda l, i: (l, i, 0))],
        out_specs=[pl.BlockSpec((1, rows, N_PROJ), lambda l, i: (l, i, 0)),
                   pl.BlockSpec((1, rows, n_attn), lambda l, i: (l, i, 0))],
        out_shape=[jax.ShapeDtypeStruct((depth, d, N_PROJ), BF16),
                   jax.ShapeDtypeStruct((depth, d, n_attn), BF16)],
        compiler_params=_cparams("parallel", "parallel"),
        name="w_in_layout",
    )(w_in)


def kernel(x, mem, norm_mix, w_in, mlstm_conv, mlstm_gate_b, mlstm_norm, attn_qk_norm, gmlp_norm, gmlp_ws,
           gmlp_bs, w_branch_a, w_branch_b, w_branch_c, w_out, rel_bias, norm_xattn, norm_mem, w_xq, w_xkv,
           xattn_qk_norm, w_xo, norm_ffn, router_w, router_b, w_expert_gate, w_expert_up, w_expert_down):
    b, s, d = x.shape
    t = b * s
    depth = w_in.shape[0]
    x2d = x.reshape(t, d)

    biases = [_attn_bias(rel_bias, g) for g in range(len(ATTN_PATTERNS))]
    rw_t = jnp.zeros((N_EXPERT_GROUPS, 8, d), F32).at[:, :EXPERTS_PER_GROUP].set(
        router_w.T.reshape(N_EXPERT_GROUPS, EXPERTS_PER_GROUP, d)).reshape(ROUTER_ROWS, d)
    rb = jnp.full((N_EXPERT_GROUPS, 8), NEG, F32).at[:, :EXPERTS_PER_GROUP].set(
        router_b.astype(F32).reshape(N_EXPERT_GROUPS, EXPERTS_PER_GROUP)).reshape(ROUTER_ROWS, 1)
    tril = jnp.tril(jnp.ones((GMLP_CHUNK, GMLP_CHUNK), bool))
    head_of = jnp.arange(ATTN_GW) // ATTN_DH
    seg_ones = (head_of[:, None] == head_of[None, :]).astype(BF16)

    w_main, w_attn = _w_in_layout(w_in)

    for l in range(depth):
        proj, h_mix = _inproj(x2d, norm_mix[l][None], w_main, layer=l, tm=1024, tn=1280)
        gq = jnp.tile(attn_qk_norm[l, 0], HEADS_PER_GROUP)[None]
        gk = jnp.tile(attn_qk_norm[l, 1], HEADS_PER_GROUP)[None]

        gates_row = proj[:, OFF_IF:OFF_IF + 8].astype(F32).reshape(b, s, 8).transpose(0, 2, 1)
        gb_col = jnp.zeros((1, IF_PAD), F32).at[0, :8].set(mlstm_gate_b[l])
        ya = _mlstm(proj, gates_row, mlstm_conv[l], gb_col, mlstm_gate_b[l].reshape(8, 1),
                    mlstm_norm[l][None], batch=b, seq=s, blk=MLSTM_BLOCK, nsub=MLSTM_NSUB,
                    group=MLSTM_GROUP)

        ybs, lses = [], []
        for g, (_, dilation) in enumerate(ATTN_PATTERNS):
            aproj = _attnproj(h_mix, w_attn, seg_ones, gq, gk, layer=l, group=g, dilation=dilation)
            o, lse = _dattn(aproj, biases[g], seq=s, group=g, dilation=dilation)
            ybs.append(o)
            lses.append(lse)

        ws = jnp.where(tril, gmlp_ws[l], 0.0).astype(BF16)
        bsb = jnp.broadcast_to(gmlp_bs[l][:, :, None], (GMLP_GROUPS, GMLP_CHUNK, GMLP_GC)).astype(F32)
        x2d = _merge(ya, ybs, lses, proj, x2d, w_branch_a[l].astype(BF16), w_branch_b[l].astype(BF16),
                     w_branch_c[l].astype(BF16), w_out[l].astype(BF16), ws, bsb, gmlp_norm[l][None], tm=256)

        k_mem, v_mem = _memkv(mem, norm_mem[l][None], w_xkv[l].astype(BF16), xattn_qk_norm[l, 1][None])
        x2d, hf_rows, eidx, wts = _xattn(x2d, k_mem, v_mem, norm_xattn[l][None], w_xq[l].astype(BF16),
                                         xattn_qk_norm[l, 0][None], w_xo[l].astype(BF16), norm_ffn[l][None],
                                         rw_t, rb, seq=s, tm=512)

        x2d = _moe(x2d, hf_rows, eidx, wts, w_expert_gate, w_expert_up, w_expert_down, layer=l)

    return x2d.reshape(b, s, d)
```

```python
import functools
import math

import jax
import jax.numpy as jnp
import numpy as np
from jax import lax
from jax.experimental import pallas as pl
from jax.experimental.pallas import tpu as pltpu
from jax.experimental.pallas import tpu_sc as plsc

F32 = jnp.float32
BF16 = jnp.bfloat16

EPS = 1e-6
NEG = -1e30

MLSTM_HEADS = 4
MLSTM_DH = 128
MLSTM_W = MLSTM_HEADS * MLSTM_DH
CONV_WIDTH = 4
MLSTM_BLOCK = 128
MLSTM_NSUB = 1
MLSTM_GROUP = 2

ATTN_PATTERNS = ((128, 1), (512, 4), (2048, 16))
HEADS_PER_GROUP = 4
ATTN_DH = 64
ATTN_GW = HEADS_PER_GROUP * ATTN_DH
ATTN_W = len(ATTN_PATTERNS) * ATTN_GW
ATTN_BLOCK = 128
REL_BUCKETS = 32
REL_MAX_DIST = 2048

GMLP_GROUPS = 4
GMLP_GC = 128
GMLP_W = GMLP_GROUPS * GMLP_GC
GMLP_CHUNK = 128

XATTN_HEADS = 4
XATTN_DH = 128
XATTN_W = XATTN_HEADS * XATTN_DH

N_EXPERTS = 16
N_EXPERT_GROUPS = 4
EXPERTS_PER_GROUP = 4
ROUTER_ROWS = 8 * N_EXPERT_GROUPS

N_BRANCH = 3

MOE_TM = 512
ROW_CHUNKS = 4
SC_CORES, SC_SUBCORES = 2, 16
SC_WINDOW = 128

OFF_MQ, OFF_MK, OFF_MV, OFF_MO = 0, 512, 1024, 1536
OFF_GU, OFF_GV = 2048, 2560
OFF_GATE = 3072
OFF_IF = 6144
IF_PAD = 256
N_PROJ = OFF_IF + IF_PAD

ATTN_TILE = 2048
ATTN_SUB = ATTN_TILE // ATTN_BLOCK
ATTN_SLAB = 2 * ATTN_DH
ATTN_COLS = HEADS_PER_GROUP * ATTN_SLAB + 2 * ATTN_GW

VMEM_LIMIT = 48 * 1024 * 1024


def _cparams(*sem, flags=None):
    return pltpu.CompilerParams(dimension_semantics=sem, vmem_limit_bytes=VMEM_LIMIT, flags=flags)


def _rms(x, gain):
    return x * lax.rsqrt(jnp.mean(x * x, axis=-1, keepdims=True) + EPS) * gain


def _sigmoid(x):
    return 0.5 * jnp.tanh(0.5 * x) + 0.5


def _dot(a, b):
    return jnp.dot(a, b, preferred_element_type=F32)


def _dot_nt(a, b):
    return lax.dot_general(a, b, (((1,), (1,)), ((), ())), preferred_element_type=F32)


def _inproj_kernel(x_ref, g_ref, w_ref, o_ref, h_ref):
    @pl.when(pl.program_id(1) == 0)
    def _():
        h_ref[...] = _rms(x_ref[...], g_ref[...]).astype(BF16)

    o_ref[...] = _dot(h_ref[...], w_ref[0]).astype(o_ref.dtype)


def _inproj(x2d, gain, w, *, layer, tm, tn):
    t, d = x2d.shape
    n = w.shape[2]
    return pl.pallas_call(
        _inproj_kernel,
        grid=(t // tm, n // tn),
        in_specs=[pl.BlockSpec((tm, d), lambda i, j: (i, 0)),
                  pl.BlockSpec((1, d), lambda i, j: (0, 0)),
                  pl.BlockSpec((1, d, tn), lambda i, j: (layer, 0, j))],
        out_specs=[pl.BlockSpec((tm, tn), lambda i, j: (i, j)),
                   pl.BlockSpec((tm, d), lambda i, j: (i, 0))],
        out_shape=[jax.ShapeDtypeStruct((t, n), BF16), jax.ShapeDtypeStruct((t, d), BF16)],
        compiler_params=_cparams("parallel", "arbitrary"),
        name="inproj",
    )(x2d, gain, w)


def _log_sigmoid(x):
    return jnp.minimum(x, 0.0) - jnp.log(1.0 + jnp.exp(-jnp.abs(x)))


def _mlstm_kernel(qk_ref, v_ref, og_ref, gc_ref, gr_ref, cw_ref, gbc_ref, gbr_ref, ng_ref, y_ref,
                  xe_scr, s_scr, m_scr, *, blk, nsub, group):
    heads, w = MLSTM_HEADS, MLSTM_W

    @pl.when(pl.program_id(1) == 0)
    def _():
        xe_scr[:, 0:8, :] = jnp.zeros((group, 8, 2 * w), F32)
        s_scr[...] = jnp.zeros_like(s_scr)
        m_scr[...] = jnp.zeros_like(m_scr)

    cw = cw_ref[...]
    ri = lax.broadcasted_iota(jnp.int32, (blk, blk), 0)
    ci = lax.broadcasted_iota(jnp.int32, (blk, blk), 1)
    causal = ri >= ci
    tril = causal.astype(BF16)
    triu = (ri <= ci).astype(BF16)
    states = []
    for g in range(group):
        xe_scr[g, 8:8 + nsub * blk, :] = qk_ref[g].astype(F32)
        states.append([(s_scr[g, h], m_scr[g, h:h + 1, 0:1]) for h in range(heads)])
    for c in range(nsub):
        for g in range(group):
            states[g] = _mlstm_chunk(c * blk, blk, states[g], cw, causal, tril, triu, xe_scr.at[g], v_ref.at[g],
                                     og_ref.at[g], gc_ref.at[g], gr_ref.at[g], gbc_ref, gbr_ref, ng_ref,
                                     y_ref.at[g])
    for g in range(group):
        xe_scr[g, 0:8, :] = xe_scr[g, nsub * blk:nsub * blk + 8, :]
        for h, (s_st, m_st) in enumerate(states[g]):
            s_scr[g, h] = s_st
            m_scr[g, h:h + 1, :] = jnp.broadcast_to(m_st, (1, m_scr.shape[2]))


def _split_bf16(x):
    hi = x.astype(BF16)
    return hi, (x - hi.astype(F32)).astype(BF16)


def _mlstm_chunk(r0, blk, state, cw, causal, tril, triu, xe_scr, v_ref, og_ref, gc_ref, gr_ref, gbc_ref,
                 gbr_ref, ng_ref, y_ref):
    heads, dh, w = MLSTM_HEADS, MLSTM_DH, MLSTM_W
    rows = slice(r0, r0 + blk)
    conv = cw[CONV_WIDTH - 1:CONV_WIDTH, :] * xe_scr[8 + r0:8 + r0 + blk, :]
    for j in range(CONV_WIDTH - 1):
        off = 8 + r0 - (CONV_WIDTH - 1) + j
        conv = conv + cw[j:j + 1, :] * xe_scr[off:off + blk, :]
    qk = conv * _sigmoid(conv)

    gcol = gc_ref[rows, :].astype(F32) + gbc_ref[...]
    grow = gr_ref[:, rows] + gbr_ref[...]
    lc_hi, lc_lo = _split_bf16(_log_sigmoid(gcol))
    lr_hi, lr_lo = _split_bf16(_log_sigmoid(grow))
    bcol = _dot(tril, lc_hi) + _dot(tril, lc_lo)
    brow = _dot(lr_hi, triu) + _dot(lr_lo, triu)
    ones = jnp.ones((blk, dh), BF16)

    new_state = []
    for h in range(heads):
        sl = slice(h * dh, (h + 1) * dh)
        b_c = bcol[:, heads + h:heads + h + 1]
        i_c = gcol[:, h:h + 1]
        b_r = brow[heads + h:heads + h + 1, :]
        i_r = grow[h:h + 1, :]
        s_st, m_st = state[h]

        d_mat = jnp.where(causal, b_c - b_r + i_r, NEG)
        inter = b_c + m_st
        m_t = jnp.maximum(inter, jnp.max(d_mat, axis=-1, keepdims=True))
        w_intra = jnp.exp(d_mat - m_t)
        w_inter = jnp.exp(inter - m_t)

        q_f = qk[:, sl]
        k_f = qk[:, w + h * dh:w + (h + 1) * dh] * (dh ** -0.5)
        q_b = q_f.astype(BF16)
        k_b = k_f.astype(BF16)
        v_ext = jnp.concatenate([v_ref[rows, sl], ones], axis=-1)

        s = _dot_nt(q_b, k_b) * w_intra
        tot = _dot(s.astype(BF16), v_ext) + w_inter * _dot(q_b, s_st.astype(BF16))
        num, den = tot[:, :dh], tot[:, dh:]
        hh = num / jnp.maximum(jnp.abs(den), jnp.exp(-m_t))
        hn = _rms(hh, ng_ref[:, sl])
        y_ref[rows, sl] = (hn * _sigmoid(og_ref[rows, sl].astype(F32))).astype(y_ref.dtype)

        b_last = b_c[blk - 1:blk, :]
        dec = b_last - b_c + i_c
        m_new = jnp.maximum(b_last + m_st, jnp.max(dec, axis=0, keepdims=True))
        w_k = jnp.exp(dec - m_new)
        w_c = jnp.exp(b_last + m_st - m_new)
        kw = k_f * w_k
        new_state.append((w_c * s_st + _dot(kw.T.astype(BF16), v_ext), m_new))
    return new_state


def _mlstm(proj, gates_row, conv_w, gb_col, gb_row, norm_g, *, batch, seq, blk, nsub, group):
    t, npj = proj.shape
    rows = blk * nsub
    w = MLSTM_W
    proj3 = proj.reshape(batch, seq, npj)
    cols = lambda c: (lambda b, i: (b, i, c))
    const2 = lambda b, i: (0, 0)
    y = pl.pallas_call(
        functools.partial(_mlstm_kernel, blk=blk, nsub=nsub, group=group),
        grid=(batch // group, seq // rows),
        in_specs=[pl.BlockSpec((group, rows, 2 * w), cols(OFF_MQ // (2 * w))),
                  pl.BlockSpec((group, rows, w), cols(OFF_MV // w)),
                  pl.BlockSpec((group, rows, w), cols(OFF_MO // w)),
                  pl.BlockSpec((group, rows, IF_PAD), cols(OFF_IF // IF_PAD)),
                  pl.BlockSpec((group, 8, rows), lambda b, i: (b, 0, i)),
                  pl.BlockSpec((CONV_WIDTH, 2 * w), const2),
                  pl.BlockSpec((1, IF_PAD), const2),
                  pl.BlockSpec((8, 1), const2),
                  pl.BlockSpec((1, w), const2)],
        out_specs=pl.BlockSpec((group, rows, w), cols(0)),
        out_shape=jax.ShapeDtypeStruct((batch, seq, w), BF16),
        scratch_shapes=[pltpu.VMEM((group, rows + 8, 2 * w), F32),
                        pltpu.VMEM((group, MLSTM_HEADS, MLSTM_DH, 2 * MLSTM_DH), F32),
                        pltpu.VMEM((group, 8, 128), F32)],
        compiler_params=_cparams("parallel", "arbitrary"),
        name="mlstm",
    )(proj3, proj3, proj3, proj3, gates_row, conv_w, gb_col, gb_row, norm_g)
    return y.reshape(t, w)


def _attnproj_kernel(h_ref, w_ref, seg_ref, gq_ref, gk_ref, o_ref, r_scr, *, dil):
    gw, half = ATTN_GW, ATTN_SLAB // 2
    sub_rows = r_scr.shape[2]
    seg, sub_seg = ATTN_TILE // dil, sub_rows // dil

    def head_norm(x, gain):
        sq = x * x
        hi = sq.astype(BF16)
        lo = (sq - hi.astype(F32)).astype(BF16)
        ss = _dot(hi, seg_ref[...]) + _dot(lo, seg_ref[...])
        return x * lax.rsqrt(ss * (1.0 / ATTN_DH) + EPS) * gain

    low = lax.broadcasted_iota(jnp.int32, (1, ATTN_SLAB), 1) < half
    for s in range(ATTN_TILE // sub_rows):
        rows = slice(s * sub_rows, (s + 1) * sub_rows)
        res = _dot(h_ref[rows, :], w_ref[0])
        q = head_norm(res[:, :gw], gq_ref[...]) * (ATTN_DH ** -0.5)
        k = head_norm(res[:, gw:2 * gw], gk_ref[...])
        slabs = []
        for pair in range(gw // ATTN_SLAB):
            qp = q[:, pair * ATTN_SLAB:(pair + 1) * ATTN_SLAB]
            slabs += [jnp.where(low, qp, 0.0), jnp.where(low, 0.0, qp)]
        slabs += [k[:, c * 128:(c + 1) * 128] for c in range(gw // 128)]
        slabs += [res[:, 2 * gw + c * 128:2 * gw + (c + 1) * 128] for c in range(gw // 128)]
        for c, slab in enumerate(slabs):
            if dil == 1:
                o_ref[rows, c * 128:(c + 1) * 128] = slab.astype(o_ref.dtype)
            else:
                r_scr[s % 2, c] = slab
        if dil > 1:
            for r in range(dil):
                dst = slice(r * seg + s * sub_seg, r * seg + (s + 1) * sub_seg)
                for c in range(r_scr.shape[1]):
                    o_ref[dst, c * 128:(c + 1) * 128] = (
                        r_scr[s % 2, c, pl.ds(r, sub_seg, stride=dil), :].astype(o_ref.dtype))


def _attnproj(h, w, seg_ones, gq, gk, *, layer, group, dilation):
    t, d = h.shape
    wcols = 3 * ATTN_GW
    const2 = lambda i: (0, 0)
    return pl.pallas_call(
        functools.partial(_attnproj_kernel, dil=dilation),
        grid=(t // ATTN_TILE,),
        in_specs=[pl.BlockSpec((ATTN_TILE, d), lambda i: (i, 0)),
                  pl.BlockSpec((1, d, wcols), lambda i: (layer, 0, group)),
                  pl.BlockSpec((ATTN_GW, ATTN_GW), const2),
                  pl.BlockSpec((1, ATTN_GW), const2), pl.BlockSpec((1, ATTN_GW), const2)],
        out_specs=pl.BlockSpec((ATTN_TILE, ATTN_COLS), lambda i: (i, 0)),
        out_shape=jax.ShapeDtypeStruct((t, ATTN_COLS), BF16),
        scratch_shapes=[pltpu.VMEM((2, ATTN_COLS // 128, 512, 128), F32)],
        compiler_params=_cparams("parallel"),
        name=f"attnproj{group}",
    )(h, w, seg_ones, gq, gk)


def _dattn_kernel(q_ref, kc_ref, kp_ref, vc_ref, vp_ref, bias_ref, o_ref, lse_ref,
                  kx_scr, vx_scr, o_scr, l_scr, *, dil):
    blk = ATTN_BLOCK
    per = ATTN_SUB // dil
    first_tile = pl.program_id(1) == 0
    for r in range(dil):
        base = r * (per + 1) * blk
        last = slice((r * per + per - 1) * blk, (r * per + per) * blk)
        mine = slice(r * per * blk, (r + 1) * per * blk)
        kx_scr[base:base + blk, :] = kp_ref[last, :]
        vx_scr[base:base + blk, :] = vp_ref[last, :]
        kx_scr[base + blk:base + (per + 1) * blk, :] = kc_ref[mine, :]
        vx_scr[base + blk:base + (per + 1) * blk, :] = vc_ref[mine, :]

    low = lax.broadcasted_iota(jnp.int32, (1, ATTN_SLAB), 1) < ATTN_SLAB // 2
    no_prev = lax.broadcasted_iota(jnp.int32, (1, 2 * blk), 1) < blk
    for r in range(dil):
        for sub in range(per):
            u = r * per + sub
            win = slice((r * (per + 1) + sub) * blk, (r * (per + 1) + sub + 2) * blk)
            o_slabs, l_slabs = [], []
            for pair in range(ATTN_GW // ATTN_SLAB):
                cols = slice(pair * ATTN_SLAB, (pair + 1) * ATTN_SLAB)
                kx, vx = kx_scr[win, cols], vx_scr[win, cols]
                o_pair, l_pair = [], []
                for h in (2 * pair, 2 * pair + 1):
                    logits = _dot_nt(q_ref[u * blk:(u + 1) * blk, h * ATTN_SLAB:(h + 1) * ATTN_SLAB], kx)
                    logits = logits + bias_ref[h]
                    if sub == 0:
                        logits = jnp.where(first_tile & no_prev, NEG, logits)
                    m = jnp.max(logits, axis=-1, keepdims=True)
                    p = jnp.exp(logits - m)
                    l = jnp.sum(p, axis=-1, keepdims=True)
                    o_pair.append(_dot(p.astype(BF16), vx) / l)
                    l_pair.append(m + jnp.log(l))
                o_slabs.append(jnp.where(low, o_pair[0], o_pair[1]))
                l_slabs.append(jnp.where(low, l_pair[0], l_pair[1]))
            dst = pl.ds(sub * blk * dil + r, blk, stride=dil) if dil > 1 else slice(u * blk, (u + 1) * blk)
            for c in range(ATTN_GW // ATTN_SLAB):
                o_scr[c, dst, :] = o_slabs[c]
                l_scr[c, dst, :] = l_slabs[c]
    for c in range(ATTN_GW // ATTN_SLAB):
        o_ref[:, c * ATTN_SLAB:(c + 1) * ATTN_SLAB] = o_scr[c].astype(o_ref.dtype)
        lse_ref[:, c * ATTN_SLAB:(c + 1) * ATTN_SLAB] = l_scr[c]


def _dattn(aproj, bias, *, seq, group, dilation):
    t = aproj.shape[0]
    tiles = seq // ATTN_TILE
    qw = HEADS_PER_GROUP * ATTN_SLAB
    cq, ck, cv = 0, qw // ATTN_GW, qw // ATTN_GW + 1
    blk = (ATTN_TILE, ATTN_GW)
    cur = lambda c: (lambda b, j: (b * tiles + j, c))
    prev = lambda c: (lambda b, j: (b * tiles + jnp.maximum(j - 1, 0), c))
    xrows = ATTN_TILE + dilation * ATTN_BLOCK
    return pl.pallas_call(
        functools.partial(_dattn_kernel, dil=dilation),
        grid=(t // seq, tiles),
        in_specs=[pl.BlockSpec((ATTN_TILE, qw), cur(cq)),
                  pl.BlockSpec(blk, cur(ck)), pl.BlockSpec(blk, prev(ck)),
                  pl.BlockSpec(blk, cur(cv)), pl.BlockSpec(blk, prev(cv)),
                  pl.BlockSpec((HEADS_PER_GROUP, ATTN_BLOCK, 2 * ATTN_BLOCK), lambda b, j: (0, 0, 0))],
        out_specs=[pl.BlockSpec(blk, cur(0)), pl.BlockSpec(blk, cur(0))],
        out_shape=[jax.ShapeDtypeStruct((t, ATTN_GW), BF16), jax.ShapeDtypeStruct((t, ATTN_GW), F32)],
        scratch_shapes=[pltpu.VMEM((xrows, ATTN_GW), BF16), pltpu.VMEM((xrows, ATTN_GW), BF16),
                        pltpu.VMEM((ATTN_GW // ATTN_SLAB, ATTN_TILE, ATTN_SLAB), F32),
                        pltpu.VMEM((ATTN_GW // ATTN_SLAB, ATTN_TILE, ATTN_SLAB), F32)],
        compiler_params=_cparams("parallel", "arbitrary"),
        name=f"dattn{group}",
    )(aproj, aproj, aproj, aproj, aproj, bias)


def _rel_bucket(n):
    max_exact = REL_BUCKETS // 2
    nf = jnp.maximum(n, 1).astype(F32)
    log_b = max_exact + (jnp.log(nf / max_exact) / math.log(REL_MAX_DIST / max_exact)
                         * (REL_BUCKETS - max_exact)).astype(jnp.int32)
    return jnp.where(n < max_exact, n, jnp.minimum(log_b, REL_BUCKETS - 1))


def _attn_bias(rel_bias, group):
    window, dilation = ATTN_PATTERNS[group]
    steps = window // dilation
    hp = lax.Precision.HIGHEST
    hs = slice(group * HEADS_PER_GROUP, (group + 1) * HEADS_PER_GROUP)
    bucket = _rel_bucket(jnp.arange(steps + 1) * dilation)
    bias_steps = jnp.dot(jax.nn.one_hot(bucket, REL_BUCKETS, dtype=F32), rel_bias[:, hs].astype(F32),
                         precision=hp)
    qi = jnp.arange(ATTN_BLOCK)[:, None]
    ki = jnp.arange(2 * ATTN_BLOCK)[None, :]
    dist = ATTN_BLOCK + qi - ki
    ok = (dist >= 0) & (dist <= steps)
    sel = jax.nn.one_hot(jnp.clip(dist, 0, steps).reshape(-1), steps + 1, dtype=F32)
    bias = jnp.dot(sel, bias_steps, precision=hp).T.reshape(HEADS_PER_GROUP, ATTN_BLOCK, 2 * ATTN_BLOCK)
    return jnp.where(ok[None], bias, NEG)


def _merge_kernel(ya_ref, yb0_ref, yb1_ref, yb2_ref, l0_ref, l1_ref, l2_ref, gu_ref, gv_ref, gate_ref,
                  x_ref, wa_ref, wb_ref, wc_ref, wo_ref, ws_ref, bs_ref, gg_ref, o_ref, yc_scr, *, tm):
    d = x_ref.shape[1]
    l0, l1, l2 = l0_ref[...], l1_ref[...], l2_ref[...]
    mx = jnp.maximum(jnp.maximum(l0, l1), l2)
    e0, e1, e2 = jnp.exp(l0 - mx), jnp.exp(l1 - mx), jnp.exp(l2 - mx)
    inv = 1.0 / (e0 + e1 + e2)
    yb = jnp.concatenate([(yb0_ref[...].astype(F32) * (e0 * inv)).astype(BF16),
                          (yb1_ref[...].astype(F32) * (e1 * inv)).astype(BF16),
                          (yb2_ref[...].astype(F32) * (e2 * inv)).astype(BF16)], axis=-1)

    for j in range(tm // GMLP_CHUNK):
        rows = slice(j * GMLP_CHUNK, (j + 1) * GMLP_CHUNK)
        for g in range(GMLP_GROUPS):
            cols = slice(g * GMLP_GC, (g + 1) * GMLP_GC)
            u = jax.nn.gelu(gu_ref[rows, cols].astype(F32))
            v = _rms(jax.nn.gelu(gv_ref[rows, cols].astype(F32)), gg_ref[:, cols])
            mixed = _dot(ws_ref[g], v.astype(BF16)) + bs_ref[g]
            yc_scr[rows, cols] = (u * mixed).astype(BF16)

    def gate2(k):
        return jnp.tanh(0.5 * gate_ref[:, k * d:(k + 1) * d].astype(F32)) + 1.0

    merged2 = gate2(0) * _dot(ya_ref[...], wa_ref[...])
    merged2 = merged2 + gate2(1) * _dot(yb, wb_ref[...])
    merged2 = merged2 + gate2(2) * _dot(yc_scr[...], wc_ref[...])
    o_ref[...] = x_ref[...] + 0.5 * _dot(merged2.astype(BF16), wo_ref[...])


def _merge(ya, ybs, lses, proj, x2d, wa, wb, wc, wo, ws, bsb, gg, *, tm):
    t, d = x2d.shape
    row = lambda c: (lambda i: (i, c))
    full2 = lambda i: (0, 0)
    full3 = lambda i: (0, 0, 0)
    gspec = pl.BlockSpec((tm, ATTN_GW), row(0))
    return pl.pallas_call(
        functools.partial(_merge_kernel, tm=tm),
        grid=(t // tm,),
        in_specs=[pl.BlockSpec((tm, MLSTM_W), row(0)),
                  gspec, gspec, gspec, gspec, gspec, gspec,
                  pl.BlockSpec((tm, GMLP_W), row(OFF_GU // GMLP_W)),
                  pl.BlockSpec((tm, GMLP_W), row(OFF_GV // GMLP_W)),
                  pl.BlockSpec((tm, N_BRANCH * d), row(OFF_GATE // (N_BRANCH * d))),
                  pl.BlockSpec((tm, d), row(0)),
                  pl.BlockSpec(wa.shape, full2), pl.BlockSpec(wb.shape, full2),
                  pl.BlockSpec(wc.shape, full2), pl.BlockSpec(wo.shape, full2),
                  pl.BlockSpec(ws.shape, full3), pl.BlockSpec(bsb.shape, full3),
                  pl.BlockSpec(gg.shape, full2)],
        out_specs=pl.BlockSpec((tm, d), row(0)),
        out_shape=jax.ShapeDtypeStruct((t, d), F32),
        scratch_shapes=[pltpu.VMEM((tm, GMLP_W), BF16)],
        compiler_params=_cparams("parallel"),
        name="merge",
    )(ya, *ybs, *lses, proj, proj, proj, x2d, wa, wb, wc, wo, ws, bsb, gg)


def _memkv_kernel(mem_ref, g_ref, w_ref, gk_ref, k_ref, v_ref):
    dh, w = XATTN_DH, XATTN_W
    kv = _dot(_rms(mem_ref[0], g_ref[...]).astype(BF16), w_ref[...])
    for h in range(XATTN_HEADS):
        sl = slice(h * dh, (h + 1) * dh)
        k_ref[0, :, sl] = _rms(kv[:, sl], gk_ref[...]).astype(k_ref.dtype)
    v_ref[0] = kv[:, w:].astype(v_ref.dtype)


def _memkv(mem, gain, w_kv, gk):
    b, m, d = mem.shape
    full2 = lambda i: (0, 0)
    return pl.pallas_call(
        _memkv_kernel,
        grid=(b,),
        in_specs=[pl.BlockSpec((1, m, d), lambda i: (i, 0, 0)),
                  pl.BlockSpec((1, d), full2),
                  pl.BlockSpec(w_kv.shape, full2),
                  pl.BlockSpec((1, XATTN_DH), full2)],
        out_specs=[pl.BlockSpec((1, m, XATTN_W), lambda i: (i, 0, 0)),
                   pl.BlockSpec((1, m, XATTN_W), lambda i: (i, 0, 0))],
        out_shape=[jax.ShapeDtypeStruct((b, m, XATTN_W), BF16),
                   jax.ShapeDtypeStruct((b, m, XATTN_W), BF16)],
        compiler_params=_cparams("parallel"),
        name="memkv",
    )(mem, gain, w_kv, gk)


def _route(logits):
    tm = logits.shape[1]
    e = jnp.exp(logits - jnp.max(logits, axis=0, keepdims=True))
    probs = e / jnp.sum(e, axis=0, keepdims=True)
    rowi = lax.broadcasted_iota(jnp.int32, (8, tm), 0)
    real = rowi < EXPERTS_PER_GROUP
    tops = []
    for g in range(N_EXPERT_GROUPS):
        pg = jnp.where(real, probs[8 * g:8 * g + 8, :], -0.5)
        m1 = jnp.max(pg, axis=0, keepdims=True)
        i1 = jnp.min(jnp.where(pg == m1, rowi, 8), axis=0, keepdims=True)
        pg2 = jnp.where(rowi == i1, -1.0, pg)
        m2 = jnp.max(pg2, axis=0, keepdims=True)
        i2 = jnp.min(jnp.where(pg2 == m2, rowi, 8), axis=0, keepdims=True)
        tops.append((m1, i1, m2, i2))
    best = jnp.zeros((1, tm), jnp.int32)
    best_score = tops[0][0] + tops[0][2]
    for g in range(1, N_EXPERT_GROUPS):
        score = tops[g][0] + tops[g][2]
        better = score > best_score
        best = jnp.where(better, g, best)
        best_score = jnp.where(better, score, best_score)
    m1, i1, m2, i2 = tops[0]
    for g in range(1, N_EXPERT_GROUPS):
        m1, i1, m2, i2 = (jnp.where(best == g, new, old) for new, old in zip(tops[g], (m1, i1, m2, i2)))
    tot = m1 + m2
    base = best * EXPERTS_PER_GROUP
    return base + i1, base + i2, m1 / tot, m2 / tot


def _pack_bf16_pairs(x):
    n = x.shape[1] // 2
    hi = lax.bitcast_convert_type(x[:, :n].astype(BF16).astype(F32), jnp.uint32)
    lo = lax.bitcast_convert_type(x[:, n:].astype(BF16).astype(F32), jnp.uint32)
    return hi | (lo >> 16)


def _unpack_bf16_pairs(p):
    hi = lax.bitcast_convert_type(p & jnp.uint32(0xFFFF0000), F32)
    lo = lax.bitcast_convert_type(p << 16, F32)
    return hi, lo


def _store_row_chunks(ref, packed):
    for j in range(ROW_CHUNKS):
        ref[j] = packed[:, j * 128:(j + 1) * 128]


def _load_row_chunks(ref):
    return jnp.concatenate([ref[j] for j in range(ROW_CHUNKS)], axis=-1)


def _xattn_kernel(x_ref, k_ref, v_ref, gx_ref, wq_ref, gq_ref, wo_ref, gf_ref, rw_ref, rb_ref,
                  xo_ref, hf_ref, eidx_ref, wts_ref):
    dh = XATTN_DH
    x = x_ref[...]
    q = _dot(_rms(x, gx_ref[...]).astype(BF16), wq_ref[...])
    outs = []
    for h in range(XATTN_HEADS):
        sl = slice(h * dh, (h + 1) * dh)
        q_h = (_rms(q[:, sl], gq_ref[...]) * (dh ** -0.5)).astype(BF16)
        logits = _dot_nt(q_h, k_ref[0, :, sl])
        p = jnp.exp(logits - jnp.max(logits, axis=-1, keepdims=True))
        o = _dot(p.astype(BF16), v_ref[0, :, sl]) / jnp.sum(p, axis=-1, keepdims=True)
        outs.append(o.astype(BF16))
    xn = x + _dot(jnp.concatenate(outs, axis=-1), wo_ref[...])
    xo_ref[...] = xn
    hf = _rms(xn, gf_ref[...])
    _store_row_chunks(hf_ref, _pack_bf16_pairs(hf))
    rw = rw_ref[...]
    rw_hi = rw.astype(BF16)
    rw_lo = (rw - rw_hi.astype(F32)).astype(BF16)
    hf_hi = hf.astype(BF16)
    hf_lo = (hf - hf_hi.astype(F32)).astype(BF16)
    logits_t = _dot_nt(rw_hi, hf_hi) + _dot_nt(rw_hi, hf_lo) + _dot_nt(rw_lo, hf_hi) + rb_ref[...]
    e1, e2, w1, w2 = _route(logits_t)
    tm = x.shape[0]
    eidx_ref[...] = jnp.concatenate([e1, e2, jnp.zeros((6, tm), jnp.int32)], axis=0)
    wts_ref[...] = jnp.concatenate([w1, w2, jnp.zeros((6, tm), F32)], axis=0)


def _xattn(x2d, k, v, gx, wq, gq, wo, gf, rw_t, rb, *, seq, tm):
    t, d = x2d.shape
    per_b = seq // tm
    full2 = lambda i: (0, 0)
    kv_spec = pl.BlockSpec((1,) + k.shape[1:], lambda i: (i // per_b, 0, 0))
    return pl.pallas_call(
        _xattn_kernel,
        grid=(t // tm,),
        in_specs=[pl.BlockSpec((tm, d), lambda i: (i, 0)), kv_spec, kv_spec,
                  pl.BlockSpec((1, d), full2), pl.BlockSpec(wq.shape, full2),
                  pl.BlockSpec((1, XATTN_DH), full2), pl.BlockSpec(wo.shape, full2),
                  pl.BlockSpec((1, d), full2), pl.BlockSpec(rw_t.shape, full2),
                  pl.BlockSpec(rb.shape, full2)],
        out_specs=[pl.BlockSpec((tm, d), lambda i: (i, 0)),
                   pl.BlockSpec((ROW_CHUNKS, tm, 128), lambda i: (0, i, 0)),
                   pl.BlockSpec((8, tm), lambda i: (0, i)),
                   pl.BlockSpec((8, tm), lambda i: (0, i))],
        out_shape=[jax.ShapeDtypeStruct((t, d), F32),
                   jax.ShapeDtypeStruct((ROW_CHUNKS, t, 128), jnp.uint32),
                   jax.ShapeDtypeStruct((8, t), jnp.int32),
                   jax.ShapeDtypeStruct((8, t), F32)],
        compiler_params=_cparams("parallel"),
        name="xattn_router",
    )(x2d, k, v, gx, wq, gq, wo, gf, rw_t, rb)


def _moe_plan_kernel(eidx_ref, i1_ref, i2_ref, te_ref, na_ref, cnt_scr, carry_scr, *, tb, tm, plane_rows):
    ne = N_EXPERTS
    hp = lax.Precision.HIGHEST
    phase, j = pl.program_id(0), pl.program_id(1)
    rows = lax.broadcasted_iota(jnp.int32, (ne, tb), 0)
    oh1 = rows == eidx_ref[0:1, :]
    oh2 = rows == eidx_ref[1:2, :]
    a = oh1.astype(F32) + oh2.astype(F32)
    blk_cnt = jnp.broadcast_to(jnp.sum(a, axis=1, keepdims=True), cnt_scr.shape)

    @pl.when((phase == 0) & (j == 0))
    def _():
        cnt_scr[...] = jnp.zeros_like(cnt_scr)

    @pl.when(phase == 0)
    def _():
        cnt_scr[...] += blk_cnt

    @pl.when((phase == 1) & (j == 0))
    def _():
        padded = jnp.ceil(cnt_scr[...] * (1.0 / tm)) * tm
        er = lax.broadcasted_iota(jnp.int32, (ne, ne), 0)
        ec = lax.broadcasted_iota(jnp.int32, (ne, ne), 1)
        off = jnp.dot((ec < er).astype(F32), padded, precision=hp, preferred_element_type=F32)
        carry_scr[...] = off
        seg_end = (off + padded)[:, 0:1]
        tile_start = lax.broadcasted_iota(jnp.int32, (ne, te_ref.shape[1]), 1).astype(F32) * tm
        te = jnp.sum((seg_end <= tile_start).astype(F32), axis=0, keepdims=True)
        te_ref[...] = jnp.broadcast_to(jnp.minimum(te, ne - 1.0), te_ref.shape).astype(jnp.int32)
        total = jnp.sum(padded[:, 0:1], axis=0, keepdims=True)
        na_ref[...] = jnp.broadcast_to(total * (1.0 / tm), na_ref.shape).astype(jnp.int32)

    @pl.when(phase == 1)
    def _():
        before = (lax.broadcasted_iota(jnp.int32, (tb, tb), 0)
                  < lax.broadcasted_iota(jnp.int32, (tb, tb), 1)).astype(BF16)
        rank = carry_scr[:, 0:1] + _dot(a.astype(BF16), before)
        d1 = jnp.sum(jnp.where(oh1, rank, 0.0), axis=0, keepdims=True).astype(jnp.int32)
        d2 = jnp.sum(jnp.where(oh2, rank, 0.0), axis=0, keepdims=True).astype(jnp.int32)
        plane = lax.broadcasted_iota(jnp.int32, (8, tb), 0) * plane_rows
        i1_ref[...] = jnp.where(plane < ROW_CHUNKS * plane_rows, plane + d1, 0)
        i2_ref[...] = jnp.where(plane < ROW_CHUNKS * plane_rows, plane + d2, 0)
        carry_scr[...] += blk_cnt


def _moe_plan(eidx, *, tm, n_tiles, tb=512):
    t = eidx.shape[1]
    ntp = -(-n_tiles // 128) * 128
    return pl.pallas_call(
        functools.partial(_moe_plan_kernel, tb=tb, tm=tm, plane_rows=n_tiles * tm),
        grid=(2, t // tb),
        in_specs=[pl.BlockSpec((8, tb), lambda p, j: (0, j))],
        out_specs=[pl.BlockSpec((8, tb), lambda p, j: (0, j * p)),
                   pl.BlockSpec((8, tb), lambda p, j: (0, j * p)),
                   pl.BlockSpec((8, ntp), lambda p, j: (0, 0)),
                   pl.BlockSpec((8, 128), lambda p, j: (0, 0))],
        out_shape=[jax.ShapeDtypeStruct((8, t), jnp.int32),
                   jax.ShapeDtypeStruct((8, t), jnp.int32),
                   jax.ShapeDtypeStruct((8, ntp), jnp.int32),
                   jax.ShapeDtypeStruct((8, 128), jnp.int32)],
        scratch_shapes=[pltpu.VMEM((N_EXPERTS, 128), F32), pltpu.VMEM((N_EXPERTS, 128), F32)],
        compiler_params=_cparams("arbitrary", "arbitrary"),
        name="moe_plan",
    )(eidx)


def _sc_mesh():
    return plsc.VectorSubcoreMesh(core_axis_name="c", subcore_axis_name="s",
                                  num_cores=SC_CORES, num_subcores=SC_SUBCORES)


def _sc_index_spec(tokens):
    nb = tokens // SC_WINDOW
    return pl.BlockSpec((1, SC_WINDOW), lambda i: (i // nb, i % nb))


def _sc_dispatch(rows, i1, i2, n_out):
    n = rows.shape[0]
    tokens = i1.shape[1]

    @functools.partial(pl.kernel, out_type=jax.ShapeDtypeStruct((n_out, 128), rows.dtype), mesh=_sc_mesh(),
                       name="moe_dispatch")
    def k(x_hbm, i1_hbm, i2_hbm, o_hbm):
        def body(x_vmem, i1_vmem, i2_vmem):
            pltpu.sync_copy(x_vmem, o_hbm.at[i1_vmem.at[0]])
            pltpu.sync_copy(x_vmem, o_hbm.at[i2_vmem.at[0]])

        pltpu.emit_pipeline(
            body, grid=(n // SC_WINDOW,),
            in_specs=[pl.BlockSpec((SC_WINDOW, 128), lambda i: (i, 0)),
                      _sc_index_spec(tokens), _sc_index_spec(tokens)],
            out_specs=[],
            core_axis_name=("c", "s"), dimension_semantics=(pltpu.PARALLEL,),
        )(x_hbm, i1_hbm, i2_hbm)

    return k(rows, i1, i2)


def _sc_collect(table, i1, i2):
    tokens = i1.shape[1]
    n = ROW_CHUNKS * tokens
    out = jax.ShapeDtypeStruct((n, 128), table.dtype)

    @functools.partial(pl.kernel, out_type=(out, out), mesh=_sc_mesh(), name="moe_collect")
    def k(t_hbm, i1_hbm, i2_hbm, o1_hbm, o2_hbm):
        def body(i1_vmem, i2_vmem, o1_vmem, o2_vmem):
            pltpu.sync_copy(t_hbm.at[i1_vmem.at[0]], o1_vmem)
            pltpu.sync_copy(t_hbm.at[i2_vmem.at[0]], o2_vmem)

        pltpu.emit_pipeline(
            body, grid=(n // SC_WINDOW,),
            in_specs=[_sc_index_spec(tokens), _sc_index_spec(tokens)],
            out_specs=[pl.BlockSpec((SC_WINDOW, 128), lambda i: (i, 0)),
                       pl.BlockSpec((SC_WINDOW, 128), lambda i: (i, 0))],
            core_axis_name=("c", "s"), dimension_semantics=(pltpu.PARALLEL,),
        )(i1_hbm, i2_hbm, o1_hbm, o2_hbm)

    return k(table, i1, i2)


def _experts_kernel(te_ref, na_ref, xs_ref, wg_ref, wu_ref, wd_ref, y_ref, wg_scr, wu_scr, wd_scr):
    i = pl.program_id(0)
    active = i < na_ref[0]

    @pl.when(active & ((i == 0) | (te_ref[i] != te_ref[jnp.maximum(i - 1, 0)])))
    def _():
        wg_scr[...] = wg_ref[0, 0].astype(BF16)
        wu_scr[...] = wu_ref[0, 0].astype(BF16)
        wd_scr[...] = wd_ref[0, 0].astype(BF16)

    @pl.when(active)
    def _():
        hi, lo = _unpack_bf16_pairs(_load_row_chunks(xs_ref))
        h = jnp.concatenate([hi, lo], axis=-1).astype(BF16)
        up = _dot(h, wg_scr[...])
        act = up * _sigmoid(up) * _dot(h, wu_scr[...])
        _store_row_chunks(y_ref, _pack_bf16_pairs(_dot(act.astype(BF16), wd_scr[...])))


def _experts(tile_expert, n_active, xs, wg, wu, wd, *, layer, tm):
    n_tiles = tile_expert.shape[0]
    _, _, d, dff = wg.shape
    rows = lambda i, te, na: (0, jnp.minimum(i, na[0] - 1), 0)
    expert = lambda i, te, na: (layer, te[i], 0, 0)
    return pl.pallas_call(
        _experts_kernel,
        grid_spec=pltpu.PrefetchScalarGridSpec(
            num_scalar_prefetch=2,
            grid=(n_tiles,),
            in_specs=[pl.BlockSpec((ROW_CHUNKS, tm, 128), rows),
                      pl.BlockSpec((1, 1, d, dff), expert),
                      pl.BlockSpec((1, 1, d, dff), expert),
                      pl.BlockSpec((1, 1, dff, d), expert)],
            out_specs=pl.BlockSpec((ROW_CHUNKS, tm, 128), rows),
            scratch_shapes=[pltpu.VMEM((d, dff), BF16), pltpu.VMEM((d, dff), BF16), pltpu.VMEM((dff, d), BF16)]),
        out_shape=jax.ShapeDtypeStruct(xs.shape, xs.dtype),
        compiler_params=_cparams("arbitrary"),
        name="moe_experts",
    )(tile_expert, n_active, xs, wg, wu, wd)


def _moe_combine_kernel(x_ref, y1_ref, y2_ref, w_ref, o_ref):
    half = x_ref.shape[1] // 2
    hi1, lo1 = _unpack_bf16_pairs(_load_row_chunks(y1_ref))
    hi2, lo2 = _unpack_bf16_pairs(_load_row_chunks(y2_ref))
    w1, w2 = w_ref[:, 0:1], w_ref[:, 1:2]
    o_ref[:, :half] = x_ref[:, :half] + w1 * hi1 + w2 * hi2
    o_ref[:, half:] = x_ref[:, half:] + w1 * lo1 + w2 * lo2


def _moe_combine(x2d, y1, y2, wcol, *, tm):
    t, d = x2d.shape
    chunk_spec = pl.BlockSpec((ROW_CHUNKS, tm, 128), lambda i: (0, i, 0))
    return pl.pallas_call(
        _moe_combine_kernel,
        grid=(t // tm,),
        in_specs=[pl.BlockSpec((tm, d), lambda i: (i, 0)), chunk_spec, chunk_spec,
                  pl.BlockSpec((tm, wcol.shape[1]), lambda i: (i, 0))],
        out_specs=pl.BlockSpec((tm, d), lambda i: (i, 0)),
        out_shape=jax.ShapeDtypeStruct((t, d), F32),
        compiler_params=_cparams("parallel"),
        name="moe_combine",
    )(x2d, y1, y2, wcol)


def _moe(x2d, hf_rows, eidx, wts, wg, wu, wd, *, layer):
    t = x2d.shape[0]
    tm = MOE_TM
    n_tiles = 2 * t // tm + N_EXPERTS
    plane = n_tiles * tm
    i1, i2, te, na = _moe_plan(eidx, tm=tm, n_tiles=n_tiles)
    xs = _sc_dispatch(hf_rows.reshape(ROW_CHUNKS * t, 128), i1, i2, ROW_CHUNKS * plane)
    ys = _experts(te[0, :n_tiles], na[0, :1], xs.reshape(ROW_CHUNKS, plane, 128), wg, wu, wd,
                  layer=layer, tm=tm)
    y1, y2 = _sc_collect(ys.reshape(ROW_CHUNKS * plane, 128), i1, i2)
    return _moe_combine(x2d, y1.reshape(ROW_CHUNKS, t, 128), y2.reshape(ROW_CHUNKS, t, 128), wts[:2].T, tm=512)


def _w_in_layout_kernel(w_ref, main_ref, attn_ref):
    w = w_ref[0]
    src_if = 4 * MLSTM_W
    src_a = src_if + 2 * MLSTM_HEADS
    src_g = src_a + 3 * ATTN_W
    main_ref[0, :, OFF_MQ:OFF_GU] = w[:, 0:src_if].astype(BF16)
    main_ref[0, :, OFF_GU:OFF_IF] = w[:, src_g:src_g + OFF_IF - OFF_GU].astype(BF16)
    first = w[:, src_if:src_if + 128]
    lane = lax.broadcasted_iota(jnp.int32, first.shape, 1)
    main_ref[0, :, OFF_IF:OFF_IF + 128] = jnp.where(lane < 2 * MLSTM_HEADS, first, 0.0).astype(BF16)
    main_ref[0, :, OFF_IF + 128:N_PROJ] = jnp.zeros((w.shape[0], IF_PAD - 128), BF16)
    for g in range(len(ATTN_PATTERNS)):
        for j in range(3):
            src = src_a + j * ATTN_W + g * ATTN_GW
            dst = (3 * g + j) * ATTN_GW
            attn_ref[0, :, dst:dst + ATTN_GW] = w[:, src:src + ATTN_GW].astype(BF16)


def _w_in_layout(w_in, *, rows=256):
    depth, d, n_in = w_in.shape
    n_attn = 3 * ATTN_W
    return pl.pallas_call(
        _w_in_layout_kernel,
        grid=(depth, d // rows),
        in_specs=[pl.BlockSpec((1, rows, n_in), lambda l, i: (l, i, 0))],
        out_specs=[pl.BlockSpec((1, rows, N_PROJ), lambda l, i: (l, i, 0)),
                   pl.BlockSpec((1, rows, n_attn), lambda l, i: (l, i, 0))],
        out_shape=[jax.ShapeDtypeStruct((depth, d, N_PROJ), BF16),
                   jax.ShapeDtypeStruct((depth, d, n_attn), BF16)],
        compiler_params=_cparams("parallel", "parallel"),
        name="w_in_layout",
    )(w_in)


def kernel(x, mem, norm_mix, w_in, mlstm_conv, mlstm_gate_b, mlstm_norm, attn_qk_norm, gmlp_norm, gmlp_ws,
           gmlp_bs, w_branch_a, w_branch_b, w_branch_c, w_out, rel_bias, norm_xattn, norm_mem, w_xq, w_xkv,
           xattn_qk_norm, w_xo, norm_ffn, router_w, router_b, w_expert_gate, w_expert_up, w_expert_down):
    b, s, d = x.shape
    t = b * s
    depth = w_in.shape[0]
    x2d = x.reshape(t, d)

    biases = [_attn_bias(rel_bias, g) for g in range(len(ATTN_PATTERNS))]
    rw_t = jnp.zeros((N_EXPERT_GROUPS, 8, d), F32).at[:, :EXPERTS_PER_GROUP].set(
        router_w.T.reshape(N_EXPERT_GROUPS, EXPERTS_PER_GROUP, d)).reshape(ROUTER_ROWS, d)
    rb = jnp.full((N_EXPERT_GROUPS, 8), NEG, F32).at[:, :EXPERTS_PER_GROUP].set(
        router_b.astype(F32).reshape(N_EXPERT_GROUPS, EXPERTS_PER_GROUP)).reshape(ROUTER_ROWS, 1)
    tril = jnp.tril(jnp.ones((GMLP_CHUNK, GMLP_CHUNK), bool))
    head_of = jnp.arange(ATTN_GW) // ATTN_DH
    seg_ones = (head_of[:, None] == head_of[None, :]).astype(BF16)

    w_main, w_attn = _w_in_layout(w_in)

    for l in range(depth):
        proj, h_mix = _inproj(x2d, norm_mix[l][None], w_main, layer=l, tm=1024, tn=1280)
        gq = jnp.tile(attn_qk_norm[l, 0], HEADS_PER_GROUP)[None]
        gk = jnp.tile(attn_qk_norm[l, 1], HEADS_PER_GROUP)[None]

        gates_row = proj[:, OFF_IF:OFF_IF + 8].astype(F32).reshape(b, s, 8).transpose(0, 2, 1)
        gb_col = jnp.zeros((1, IF_PAD), F32).at[0, :8].set(mlstm_gate_b[l])
        ya = _mlstm(proj, gates_row, mlstm_conv[l], gb_col, mlstm_gate_b[l].reshape(8, 1),
                    mlstm_norm[l][None], batch=b, seq=s, blk=MLSTM_BLOCK, nsub=MLSTM_NSUB,
                    group=MLSTM_GROUP)

        ybs, lses = [], []
        for g, (_, dilation) in enumerate(ATTN_PATTERNS):
            aproj = _attnproj(h_mix, w_attn, seg_ones, gq, gk, layer=l, group=g, dilation=dilation)
            o, lse = _dattn(aproj, biases[g], seq=s, group=g, dilation=dilation)
            ybs.append(o)
            lses.append(lse)

        ws = jnp.where(tril, gmlp_ws[l], 0.0).astype(BF16)
        bsb = jnp.broadcast_to(gmlp_bs[l][:, :, None], (GMLP_GROUPS, GMLP_CHUNK, GMLP_GC)).astype(F32)
        x2d = _merge(ya, ybs, lses, proj, x2d, w_branch_a[l].astype(BF16), w_branch_b[l].astype(BF16),
                     w_branch_c[l].astype(BF16), w_out[l].astype(BF16), ws, bsb, gmlp_norm[l][None], tm=512)

        k_mem, v_mem = _memkv(mem, norm_mem[l][None], w_xkv[l].astype(BF16), xattn_qk_norm[l, 1][None])
        x2d, hf_rows, eidx, wts = _xattn(x2d, k_mem, v_mem, norm_xattn[l][None], w_xq[l].astype(BF16),
                                         xattn_qk_norm[l, 0][None], w_xo[l].astype(BF16), norm_ffn[l][None],
                                         rw_t, rb, seq=s, tm=512)

        x2d = _moe(x2d, hf_rows, eidx, wts, w_expert_gate, w_expert_up, w_expert_down, layer=l)

    return x2d.reshape(b, s, d)
```

```python
import functools
import math

import jax
import jax.numpy as jnp
import numpy as np
from jax import lax
from jax.experimental import pallas as pl
from jax.experimental.pallas import tpu as pltpu
from jax.experimental.pallas import tpu_sc as plsc

F32 = jnp.float32
BF16 = jnp.bfloat16

EPS = 1e-6
NEG = -1e30

MLSTM_HEADS = 4
MLSTM_DH = 128
MLSTM_W = MLSTM_HEADS * MLSTM_DH
CONV_WIDTH = 4
MLSTM_BLOCK = 128
MLSTM_NSUB = 1
MLSTM_GROUP = 2

ATTN_PATTERNS = ((128, 1), (512, 4), (2048, 16))
HEADS_PER_GROUP = 4
ATTN_DH = 64
ATTN_GW = HEADS_PER_GROUP * ATTN_DH
ATTN_W = len(ATTN_PATTERNS) * ATTN_GW
ATTN_BLOCK = 128
REL_BUCKETS = 32
REL_MAX_DIST = 2048

GMLP_GROUPS = 4
GMLP_GC = 128
GMLP_W = GMLP_GROUPS * GMLP_GC
GMLP_CHUNK = 128

XATTN_HEADS = 4
XATTN_DH = 128
XATTN_W = XATTN_HEADS * XATTN_DH

N_EXPERTS = 16
N_EXPERT_GROUPS = 4
EXPERTS_PER_GROUP = 4
ROUTER_ROWS = 8 * N_EXPERT_GROUPS

N_BRANCH = 3

MOE_TM = 512
ROW_CHUNKS = 4
SC_CORES, SC_SUBCORES = 2, 16
SC_WINDOW = 128

OFF_MQ, OFF_MK, OFF_MV, OFF_MO = 0, 512, 1024, 1536
OFF_GU, OFF_GV = 2048, 2560
OFF_GATE = 3072
OFF_IF = 6144
IF_PAD = 256
N_PROJ = OFF_IF + IF_PAD

ATTN_TILE = 2048
ATTN_SUB = ATTN_TILE // ATTN_BLOCK
ATTN_SLAB = 2 * ATTN_DH
ATTN_COLS = HEADS_PER_GROUP * ATTN_SLAB + 2 * ATTN_GW

VMEM_LIMIT = 48 * 1024 * 1024


def _cparams(*sem, flags=None):
    return pltpu.CompilerParams(dimension_semantics=sem, vmem_limit_bytes=VMEM_LIMIT, flags=flags)


def _rms(x, gain):
    return x * lax.rsqrt(jnp.mean(x * x, axis=-1, keepdims=True) + EPS) * gain


def _sigmoid(x):
    return 0.5 * jnp.tanh(0.5 * x) + 0.5


def _dot(a, b):
    return jnp.dot(a, b, preferred_element_type=F32)


def _dot_nt(a, b):
    return lax.dot_general(a, b, (((1,), (1,)), ((), ())), preferred_element_type=F32)


def _inproj_kernel(x_ref, g_ref, w_ref, o_ref, h_ref):
    @pl.when(pl.program_id(1) == 0)
    def _():
        h_ref[...] = _rms(x_ref[...], g_ref[...]).astype(BF16)

    o_ref[...] = _dot(h_ref[...], w_ref[0]).astype(o_ref.dtype)


def _inproj(x2d, gain, w, *, layer, tm, tn):
    t, d = x2d.shape
    n = w.shape[2]
    return pl.pallas_call(
        _inproj_kernel,
        grid=(t // tm, n // tn),
        in_specs=[pl.BlockSpec((tm, d), lambda i, j: (i, 0)),
                  pl.BlockSpec((1, d), lambda i, j: (0, 0)),
                  pl.BlockSpec((1, d, tn), lambda i, j: (layer, 0, j))],
        out_specs=[pl.BlockSpec((tm, tn), lambda i, j: (i, j)),
                   pl.BlockSpec((tm, d), lambda i, j: (i, 0))],
        out_shape=[jax.ShapeDtypeStruct((t, n), BF16), jax.ShapeDtypeStruct((t, d), BF16)],
        compiler_params=_cparams("parallel", "arbitrary"),
        name="inproj",
    )(x2d, gain, w)


def _log_sigmoid(x):
    return jnp.minimum(x, 0.0) - jnp.log(1.0 + jnp.exp(-jnp.abs(x)))


def _mlstm_kernel(qk_ref, v_ref, og_ref, gc_ref, gr_ref, cw_ref, gbc_ref, gbr_ref, ng_ref, y_ref,
                  xe_scr, s_scr, m_scr, *, blk, nsub, group):
    heads, w = MLSTM_HEADS, MLSTM_W

    @pl.when(pl.program_id(1) == 0)
    def _():
        xe_scr[:, 0:8, :] = jnp.zeros((group, 8, 2 * w), F32)
        s_scr[...] = jnp.zeros_like(s_scr)
        m_scr[...] = jnp.zeros_like(m_scr)

    cw = cw_ref[...]
    ri = lax.broadcasted_iota(jnp.int32, (blk, blk), 0)
    ci = lax.broadcasted_iota(jnp.int32, (blk, blk), 1)
    causal = ri >= ci
    tril = causal.astype(BF16)
    triu = (ri <= ci).astype(BF16)
    states = []
    for g in range(group):
        xe_scr[g, 8:8 + nsub * blk, :] = qk_ref[g].astype(F32)
        states.append([(s_scr[g, h], m_scr[g, h:h + 1, 0:1]) for h in range(heads)])
    for c in range(nsub):
        for g in range(group):
            states[g] = _mlstm_chunk(c * blk, blk, states[g], cw, causal, tril, triu, xe_scr.at[g], v_ref.at[g],
                                     og_ref.at[g], gc_ref.at[g], gr_ref.at[g], gbc_ref, gbr_ref, ng_ref,
                                     y_ref.at[g])
    for g in range(group):
        xe_scr[g, 0:8, :] = xe_scr[g, nsub * blk:nsub * blk + 8, :]
        for h, (s_st, m_st) in enumerate(states[g]):
            s_scr[g, h] = s_st
            m_scr[g, h:h + 1, :] = jnp.broadcast_to(m_st, (1, m_scr.shape[2]))


def _split_bf16(x):
    hi = x.astype(BF16)
    return hi, (x - hi.astype(F32)).astype(BF16)


def _mlstm_chunk(r0, blk, state, cw, causal, tril, triu, xe_scr, v_ref, og_ref, gc_ref, gr_ref, gbc_ref,
                 gbr_ref, ng_ref, y_ref):
    heads, dh, w = MLSTM_HEADS, MLSTM_DH, MLSTM_W
    rows = slice(r0, r0 + blk)
    conv = cw[CONV_WIDTH - 1:CONV_WIDTH, :] * xe_scr[8 + r0:8 + r0 + blk, :]
    for j in range(CONV_WIDTH - 1):
        off = 8 + r0 - (CONV_WIDTH - 1) + j
        conv = conv + cw[j:j + 1, :] * xe_scr[off:off + blk, :]
    qk = conv * _sigmoid(conv)

    gcol = gc_ref[rows, :].astype(F32) + gbc_ref[...]
    grow = gr_ref[:, rows] + gbr_ref[...]
    lc_hi, lc_lo = _split_bf16(_log_sigmoid(gcol))
    lr_hi, lr_lo = _split_bf16(_log_sigmoid(grow))
    bcol = _dot(tril, lc_hi) + _dot(tril, lc_lo)
    brow = _dot(lr_hi, triu) + _dot(lr_lo, triu)
    ones = jnp.ones((blk, dh), BF16)

    new_state = []
    for h in range(heads):
        sl = slice(h * dh, (h + 1) * dh)
        b_c = bcol[:, heads + h:heads + h + 1]
        i_c = gcol[:, h:h + 1]
        b_r = brow[heads + h:heads + h + 1, :]
        i_r = grow[h:h + 1, :]
        s_st, m_st = state[h]

        d_mat = jnp.where(causal, b_c - b_r + i_r, NEG)
        inter = b_c + m_st
        m_t = jnp.maximum(inter, jnp.max(d_mat, axis=-1, keepdims=True))
        w_intra = jnp.exp(d_mat - m_t)
        w_inter = jnp.exp(inter - m_t)

        q_f = qk[:, sl]
        k_f = qk[:, w + h * dh:w + (h + 1) * dh] * (dh ** -0.5)
        q_b = q_f.astype(BF16)
        k_b = k_f.astype(BF16)
        v_ext = jnp.concatenate([v_ref[rows, sl], ones], axis=-1)

        s = _dot_nt(q_b, k_b) * w_intra
        tot = _dot(s.astype(BF16), v_ext) + w_inter * _dot(q_b, s_st.astype(BF16))
        num, den = tot[:, :dh], tot[:, dh:]
        hh = num / jnp.maximum(jnp.abs(den), jnp.exp(-m_t))
        hn = _rms(hh, ng_ref[:, sl])
        y_ref[rows, sl] = (hn * _sigmoid(og_ref[rows, sl].astype(F32))).astype(y_ref.dtype)

        b_last = b_c[blk - 1:blk, :]
        dec = b_last - b_c + i_c
        m_new = jnp.maximum(b_last + m_st, jnp.max(dec, axis=0, keepdims=True))
        w_k = jnp.exp(dec - m_new)
        w_c = jnp.exp(b_last + m_st - m_new)
        kw = k_f * w_k
        new_state.append((w_c * s_st + _dot(kw.T.astype(BF16), v_ext), m_new))
    return new_state


def _mlstm(proj, gates_row, conv_w, gb_col, gb_row, norm_g, *, batch, seq, blk, nsub, group):
    t, npj = proj.shape
    rows = blk * nsub
    w = MLSTM_W
    proj3 = proj.reshape(batch, seq, npj)
    cols = lambda c: (lambda b, i: (b, i, c))
    const2 = lambda b, i: (0, 0)
    y = pl.pallas_call(
        functools.partial(_mlstm_kernel, blk=blk, nsub=nsub, group=group),
        grid=(batch // group, seq // rows),
        in_specs=[pl.BlockSpec((group, rows, 2 * w), cols(OFF_MQ // (2 * w))),
                  pl.BlockSpec((group, rows, w), cols(OFF_MV // w)),
                  pl.BlockSpec((group, rows, w), cols(OFF_MO // w)),
                  pl.BlockSpec((group, rows, IF_PAD), cols(OFF_IF // IF_PAD)),
                  pl.BlockSpec((group, 8, rows), lambda b, i: (b, 0, i)),
                  pl.BlockSpec((CONV_WIDTH, 2 * w), const2),
                  pl.BlockSpec((1, IF_PAD), const2),
                  pl.BlockSpec((8, 1), const2),
                  pl.BlockSpec((1, w), const2)],
        out_specs=pl.BlockSpec((group, rows, w), cols(0)),
        out_shape=jax.ShapeDtypeStruct((batch, seq, w), BF16),
        scratch_shapes=[pltpu.VMEM((group, rows + 8, 2 * w), F32),
                        pltpu.VMEM((group, MLSTM_HEADS, MLSTM_DH, 2 * MLSTM_DH), F32),
                        pltpu.VMEM((group, 8, 128), F32)],
        compiler_params=_cparams("parallel", "arbitrary"),
        name="mlstm",
    )(proj3, proj3, proj3, proj3, gates_row, conv_w, gb_col, gb_row, norm_g)
    return y.reshape(t, w)


def _attnproj_kernel(h_ref, w_ref, seg_ref, gq_ref, gk_ref, o_ref, r_scr, *, dil):
    gw, half = ATTN_GW, ATTN_SLAB // 2
    sub_rows = r_scr.shape[2]
    seg, sub_seg = ATTN_TILE // dil, sub_rows // dil

    def head_norm(x, gain):
        sq = x * x
        hi = sq.astype(BF16)
        lo = (sq - hi.astype(F32)).astype(BF16)
        ss = _dot(hi, seg_ref[...]) + _dot(lo, seg_ref[...])
        return x * lax.rsqrt(ss * (1.0 / ATTN_DH) + EPS) * gain

    low = lax.broadcasted_iota(jnp.int32, (1, ATTN_SLAB), 1) < half
    for s in range(ATTN_TILE // sub_rows):
        rows = slice(s * sub_rows, (s + 1) * sub_rows)
        res = _dot(h_ref[rows, :], w_ref[0])
        q = head_norm(res[:, :gw], gq_ref[...]) * (ATTN_DH ** -0.5)
        k = head_norm(res[:, gw:2 * gw], gk_ref[...])
        slabs = []
        for pair in range(gw // ATTN_SLAB):
            qp = q[:, pair * ATTN_SLAB:(pair + 1) * ATTN_SLAB]
            slabs += [jnp.where(low, qp, 0.0), jnp.where(low, 0.0, qp)]
        slabs += [k[:, c * 128:(c + 1) * 128] for c in range(gw // 128)]
        slabs += [res[:, 2 * gw + c * 128:2 * gw + (c + 1) * 128] for c in range(gw // 128)]
        for c, slab in enumerate(slabs):
            if dil == 1:
                o_ref[rows, c * 128:(c + 1) * 128] = slab.astype(o_ref.dtype)
            else:
                r_scr[s % 2, c] = slab
        if dil > 1:
            for r in range(dil):
                dst = slice(r * seg + s * sub_seg, r * seg + (s + 1) * sub_seg)
                for c in range(r_scr.shape[1]):
                    o_ref[dst, c * 128:(c + 1) * 128] = (
                        r_scr[s % 2, c, pl.ds(r, sub_seg, stride=dil), :].astype(o_ref.dtype))


def _attnproj(h, w, seg_ones, gq, gk, *, layer, group, dilation):
    t, d = h.shape
    wcols = 3 * ATTN_GW
    const2 = lambda i: (0, 0)
    return pl.pallas_call(
        functools.partial(_attnproj_kernel, dil=dilation),
        grid=(t // ATTN_TILE,),
        in_specs=[pl.BlockSpec((ATTN_TILE, d), lambda i: (i, 0)),
                  pl.BlockSpec((1, d, wcols), lambda i: (layer, 0, group)),
                  pl.BlockSpec((ATTN_GW, ATTN_GW), const2),
                  pl.BlockSpec((1, ATTN_GW), const2), pl.BlockSpec((1, ATTN_GW), const2)],
        out_specs=pl.BlockSpec((ATTN_TILE, ATTN_COLS), lambda i: (i, 0)),
        out_shape=jax.ShapeDtypeStruct((t, ATTN_COLS), BF16),
        scratch_shapes=[pltpu.VMEM((2, ATTN_COLS // 128, 512, 128), F32)],
        compiler_params=_cparams("parallel"),
        name=f"attnproj{group}",
    )(h, w, seg_ones, gq, gk)


def _dattn_kernel(q_ref, kc_ref, kp_ref, vc_ref, vp_ref, bias_ref, o_ref, lse_ref,
                  kx_scr, vx_scr, o_scr, l_scr, *, dil):
    blk = ATTN_BLOCK
    per = ATTN_SUB // dil
    first_tile = pl.program_id(1) == 0
    for r in range(dil):
        base = r * (per + 1) * blk
        last = slice((r * per + per - 1) * blk, (r * per + per) * blk)
        mine = slice(r * per * blk, (r + 1) * per * blk)
        kx_scr[base:base + blk, :] = kp_ref[last, :]
        vx_scr[base:base + blk, :] = vp_ref[last, :]
        kx_scr[base + blk:base + (per + 1) * blk, :] = kc_ref[mine, :]
        vx_scr[base + blk:base + (per + 1) * blk, :] = vc_ref[mine, :]

    low = lax.broadcasted_iota(jnp.int32, (1, ATTN_SLAB), 1) < ATTN_SLAB // 2
    no_prev = lax.broadcasted_iota(jnp.int32, (1, 2 * blk), 1) < blk
    for r in range(dil):
        for sub in range(per):
            u = r * per + sub
            win = slice((r * (per + 1) + sub) * blk, (r * (per + 1) + sub + 2) * blk)
            o_slabs, l_slabs = [], []
            for pair in range(ATTN_GW // ATTN_SLAB):
                cols = slice(pair * ATTN_SLAB, (pair + 1) * ATTN_SLAB)
                kx, vx = kx_scr[win, cols], vx_scr[win, cols]
                o_pair, l_pair = [], []
                for h in (2 * pair, 2 * pair + 1):
                    logits = _dot_nt(q_ref[u * blk:(u + 1) * blk, h * ATTN_SLAB:(h + 1) * ATTN_SLAB], kx)
                    logits = logits + bias_ref[h]
                    if sub == 0:
                        logits = jnp.where(first_tile & no_prev, NEG, logits)
                    m = jnp.max(logits, axis=-1, keepdims=True)
                    p = jnp.exp(logits - m)
                    l = jnp.sum(p, axis=-1, keepdims=True)
                    o_pair.append(_dot(p.astype(BF16), vx) / l)
                    l_pair.append(m + jnp.log(l))
                o_slabs.append(jnp.where(low, o_pair[0], o_pair[1]))
                l_slabs.append(jnp.where(low, l_pair[0], l_pair[1]))
            dst = pl.ds(sub * blk * dil + r, blk, stride=dil) if dil > 1 else slice(u * blk, (u + 1) * blk)
            for c in range(ATTN_GW // ATTN_SLAB):
                o_scr[c, dst, :] = o_slabs[c]
                l_scr[c, dst, :] = l_slabs[c]
    for c in range(ATTN_GW // ATTN_SLAB):
        o_ref[:, c * ATTN_SLAB:(c + 1) * ATTN_SLAB] = o_scr[c].astype(o_ref.dtype)
        lse_ref[:, c * ATTN_SLAB:(c + 1) * ATTN_SLAB] = l_scr[c]


def _dattn(aproj, bias, *, seq, group, dilation):
    t = aproj.shape[0]
    tiles = seq // ATTN_TILE
    qw = HEADS_PER_GROUP * ATTN_SLAB
    cq, ck, cv = 0, qw // ATTN_GW, qw // ATTN_GW + 1
    blk = (ATTN_TILE, ATTN_GW)
    cur = lambda c: (lambda b, j: (b * tiles + j, c))
    prev = lambda c: (lambda b, j: (b * tiles + jnp.maximum(j - 1, 0), c))
    xrows = ATTN_TILE + dilation * ATTN_BLOCK
    return pl.pallas_call(
        functools.partial(_dattn_kernel, dil=dilation),
        grid=(t // seq, tiles),
        in_specs=[pl.BlockSpec((ATTN_TILE, qw), cur(cq)),
                  pl.BlockSpec(blk, cur(ck)), pl.BlockSpec(blk, prev(ck)),
                  pl.BlockSpec(blk, cur(cv)), pl.BlockSpec(blk, prev(cv)),
                  pl.BlockSpec((HEADS_PER_GROUP, ATTN_BLOCK, 2 * ATTN_BLOCK), lambda b, j: (0, 0, 0))],
        out_specs=[pl.BlockSpec(blk, cur(0)), pl.BlockSpec(blk, cur(0))],
        out_shape=[jax.ShapeDtypeStruct((t, ATTN_GW), BF16), jax.ShapeDtypeStruct((t, ATTN_GW), F32)],
        scratch_shapes=[pltpu.VMEM((xrows, ATTN_GW), BF16), pltpu.VMEM((xrows, ATTN_GW), BF16),
                        pltpu.VMEM((ATTN_GW // ATTN_SLAB, ATTN_TILE, ATTN_SLAB), F32),
                        pltpu.VMEM((ATTN_GW // ATTN_SLAB, ATTN_TILE, ATTN_SLAB), F32)],
        compiler_params=_cparams("parallel", "arbitrary"),
        name=f"dattn{group}",
    )(aproj, aproj, aproj, aproj, aproj, bias)


def _rel_bucket(n):
    max_exact = REL_BUCKETS // 2
    nf = jnp.maximum(n, 1).astype(F32)
    log_b = max_exact + (jnp.log(nf / max_exact) / math.log(REL_MAX_DIST / max_exact)
                         * (REL_BUCKETS - max_exact)).astype(jnp.int32)
    return jnp.where(n < max_exact, n, jnp.minimum(log_b, REL_BUCKETS - 1))


def _attn_bias(rel_bias, group):
    window, dilation = ATTN_PATTERNS[group]
    steps = window // dilation
    hp = lax.Precision.HIGHEST
    hs = slice(group * HEADS_PER_GROUP, (group + 1) * HEADS_PER_GROUP)
    bucket = _rel_bucket(jnp.arange(steps + 1) * dilation)
    bias_steps = jnp.dot(jax.nn.one_hot(bucket, REL_BUCKETS, dtype=F32), rel_bias[:, hs].astype(F32),
                         precision=hp)
    qi = jnp.arange(ATTN_BLOCK)[:, None]
    ki = jnp.arange(2 * ATTN_BLOCK)[None, :]
    dist = ATTN_BLOCK + qi - ki
    ok = (dist >= 0) & (dist <= steps)
    sel = jax.nn.one_hot(jnp.clip(dist, 0, steps).reshape(-1), steps + 1, dtype=F32)
    bias = jnp.dot(sel, bias_steps, precision=hp).T.reshape(HEADS_PER_GROUP, ATTN_BLOCK, 2 * ATTN_BLOCK)
    return jnp.where(ok[None], bias, NEG)


def _merge_kernel(ya_ref, yb0_ref, yb1_ref, yb2_ref, l0_ref, l1_ref, l2_ref, gu_ref, gv_ref, gate_ref,
                  x_ref, wa_ref, wb_ref, wc_ref, wo_ref, ws_ref, bs_ref, gg_ref, o_ref, yc_scr, *, tm):
    d = x_ref.shape[1]
    l0, l1, l2 = l0_ref[...], l1_ref[...], l2_ref[...]
    mx = jnp.maximum(jnp.maximum(l0, l1), l2)
    e0, e1, e2 = jnp.exp(l0 - mx), jnp.exp(l1 - mx), jnp.exp(l2 - mx)
    inv = 1.0 / (e0 + e1 + e2)
    yb = jnp.concatenate([(yb0_ref[...].astype(F32) * (e0 * inv)).astype(BF16),
                          (yb1_ref[...].astype(F32) * (e1 * inv)).astype(BF16),
                          (yb2_ref[...].astype(F32) * (e2 * inv)).astype(BF16)], axis=-1)

    for j in range(tm // GMLP_CHUNK):
        rows = slice(j * GMLP_CHUNK, (j + 1) * GMLP_CHUNK)
        for g in range(GMLP_GROUPS):
            cols = slice(g * GMLP_GC, (g + 1) * GMLP_GC)
            u = jax.nn.gelu(gu_ref[rows, cols].astype(F32))
            v = _rms(jax.nn.gelu(gv_ref[rows, cols].astype(F32)), gg_ref[:, cols])
            mixed = _dot(ws_ref[g], v.astype(BF16)) + bs_ref[g]
            yc_scr[rows, cols] = (u * mixed).astype(BF16)

    def gate2(k):
        return jnp.tanh(0.5 * gate_ref[:, k * d:(k + 1) * d].astype(F32)) + 1.0

    merged2 = gate2(0) * _dot(ya_ref[...], wa_ref[...])
    merged2 = merged2 + gate2(1) * _dot(yb, wb_ref[...])
    merged2 = merged2 + gate2(2) * _dot(yc_scr[...], wc_ref[...])
    o_ref[...] = x_ref[...] + 0.5 * _dot(merged2.astype(BF16), wo_ref[...])


def _merge(ya, ybs, lses, proj, x2d, wa, wb, wc, wo, ws, bsb, gg, *, tm):
    t, d = x2d.shape
    row = lambda c: (lambda i: (i, c))
    full2 = lambda i: (0, 0)
    full3 = lambda i: (0, 0, 0)
    gspec = pl.BlockSpec((tm, ATTN_GW), row(0))
    return pl.pallas_call(
        functools.partial(_merge_kernel, tm=tm),
        grid=(t // tm,),
        in_specs=[pl.BlockSpec((tm, MLSTM_W), row(0)),
                  gspec, gspec, gspec, gspec, gspec, gspec,
                  pl.BlockSpec((tm, GMLP_W), row(OFF_GU // GMLP_W)),
                  pl.BlockSpec((tm, GMLP_W), row(OFF_GV // GMLP_W)),
                  pl.BlockSpec((tm, N_BRANCH * d), row(OFF_GATE // (N_BRANCH * d))),
                  pl.BlockSpec((tm, d), row(0)),
                  pl.BlockSpec(wa.shape, full2), pl.BlockSpec(wb.shape, full2),
                  pl.BlockSpec(wc.shape, full2), pl.BlockSpec(wo.shape, full2),
                  pl.BlockSpec(ws.shape, full3), pl.BlockSpec(bsb.shape, full3),
                  pl.BlockSpec(gg.shape, full2)],
        out_specs=pl.BlockSpec((tm, d), row(0)),
        out_shape=jax.ShapeDtypeStruct((t, d), F32),
        scratch_shapes=[pltpu.VMEM((tm, GMLP_W), BF16)],
        compiler_params=_cparams("parallel"),
        name="merge",
    )(ya, *ybs, *lses, proj, proj, proj, x2d, wa, wb, wc, wo, ws, bsb, gg)


def _memkv_kernel(mem_ref, g_ref, w_ref, gk_ref, k_ref, v_ref):
    dh, w = XATTN_DH, XATTN_W
    kv = _dot(_rms(mem_ref[0], g_ref[...]).astype(BF16), w_ref[...])
    for h in range(XATTN_HEADS):
        sl = slice(h * dh, (h + 1) * dh)
        k_ref[0, :, sl] = _rms(kv[:, sl], gk_ref[...]).astype(k_ref.dtype)
    v_ref[0] = kv[:, w:].astype(v_ref.dtype)


def _memkv(mem, gain, w_kv, gk):
    b, m, d = mem.shape
    full2 = lambda i: (0, 0)
    return pl.pallas_call(
        _memkv_kernel,
        grid=(b,),
        in_specs=[pl.BlockSpec((1, m, d), lambda i: (i, 0, 0)),
                  pl.BlockSpec((1, d), full2),
                  pl.BlockSpec(w_kv.shape, full2),
                  pl.BlockSpec((1, XATTN_DH), full2)],
        out_specs=[pl.BlockSpec((1, m, XATTN_W), lambda i: (i, 0, 0)),
                   pl.BlockSpec((1, m, XATTN_W), lambda i: (i, 0, 0))],
        out_shape=[jax.ShapeDtypeStruct((b, m, XATTN_W), BF16),
                   jax.ShapeDtypeStruct((b, m, XATTN_W), BF16)],
        compiler_params=_cparams("parallel"),
        name="memkv",
    )(mem, gain, w_kv, gk)


def _route(logits):
    tm = logits.shape[1]
    e = jnp.exp(logits - jnp.max(logits, axis=0, keepdims=True))
    probs = e / jnp.sum(e, axis=0, keepdims=True)
    rowi = lax.broadcasted_iota(jnp.int32, (8, tm), 0)
    real = rowi < EXPERTS_PER_GROUP
    tops = []
    for g in range(N_EXPERT_GROUPS):
        pg = jnp.where(real, probs[8 * g:8 * g + 8, :], -0.5)
        m1 = jnp.max(pg, axis=0, keepdims=True)
        i1 = jnp.min(jnp.where(pg == m1, rowi, 8), axis=0, keepdims=True)
        pg2 = jnp.where(rowi == i1, -1.0, pg)
        m2 = jnp.max(pg2, axis=0, keepdims=True)
        i2 = jnp.min(jnp.where(pg2 == m2, rowi, 8), axis=0, keepdims=True)
        tops.append((m1, i1, m2, i2))
    best = jnp.zeros((1, tm), jnp.int32)
    best_score = tops[0][0] + tops[0][2]
    for g in range(1, N_EXPERT_GROUPS):
        score = tops[g][0] + tops[g][2]
        better = score > best_score
        best = jnp.where(better, g, best)
        best_score = jnp.where(better, score, best_score)
    m1, i1, m2, i2 = tops[0]
    for g in range(1, N_EXPERT_GROUPS):
        m1, i1, m2, i2 = (jnp.where(best == g, new, old) for new, old in zip(tops[g], (m1, i1, m2, i2)))
    tot = m1 + m2
    base = best * EXPERTS_PER_GROUP
    return base + i1, base + i2, m1 / tot, m2 / tot


def _pack_bf16_pairs(x):
    n = x.shape[1] // 2
    hi = lax.bitcast_convert_type(x[:, :n].astype(BF16).astype(F32), jnp.uint32)
    lo = lax.bitcast_convert_type(x[:, n:].astype(BF16).astype(F32), jnp.uint32)
    return hi | (lo >> 16)


def _unpack_bf16_pairs(p):
    hi = lax.bitcast_convert_type(p & jnp.uint32(0xFFFF0000), F32)
    lo = lax.bitcast_convert_type(p << 16, F32)
    return hi, lo


def _store_row_chunks(ref, packed):
    for j in range(ROW_CHUNKS):
        ref[j] = packed[:, j * 128:(j + 1) * 128]


def _load_row_chunks(ref):
    return jnp.concatenate([ref[j] for j in range(ROW_CHUNKS)], axis=-1)


def _xattn_kernel(x_ref, k_ref, v_ref, gx_ref, wq_ref, gq_ref, wo_ref, gf_ref, rw_ref, rb_ref,
                  xo_ref, hf_ref, eidx_ref, wts_ref):
    dh = XATTN_DH
    x = x_ref[...]
    q = _dot(_rms(x, gx_ref[...]).astype(BF16), wq_ref[...])
    outs = []
    for h in range(XATTN_HEADS):
        sl = slice(h * dh, (h + 1) * dh)
        q_h = (_rms(q[:, sl], gq_ref[...]) * (dh ** -0.5)).astype(BF16)
        logits = _dot_nt(q_h, k_ref[0, :, sl])
        p = jnp.exp(logits - jnp.max(logits, axis=-1, keepdims=True))
        o = _dot(p.astype(BF16), v_ref[0, :, sl]) / jnp.sum(p, axis=-1, keepdims=True)
        outs.append(o.astype(BF16))
    xn = x + _dot(jnp.concatenate(outs, axis=-1), wo_ref[...])
    xo_ref[...] = xn
    hf = _rms(xn, gf_ref[...])
    _store_row_chunks(hf_ref, _pack_bf16_pairs(hf))
    rw = rw_ref[...]
    rw_hi = rw.astype(BF16)
    rw_lo = (rw - rw_hi.astype(F32)).astype(BF16)
    hf_hi = hf.astype(BF16)
    hf_lo = (hf - hf_hi.astype(F32)).astype(BF16)
    logits_t = _dot_nt(rw_hi, hf_hi) + _dot_nt(rw_hi, hf_lo) + _dot_nt(rw_lo, hf_hi) + rb_ref[...]
    e1, e2, w1, w2 = _route(logits_t)
    tm = x.shape[0]
    eidx_ref[...] = jnp.concatenate([e1, e2, jnp.zeros((6, tm), jnp.int32)], axis=0)
    wts_ref[...] = jnp.concatenate([w1, w2, jnp.zeros((6, tm), F32)], axis=0)


def _xattn(x2d, k, v, gx, wq, gq, wo, gf, rw_t, rb, *, seq, tm):
    t, d = x2d.shape
    per_b = seq // tm
    full2 = lambda i: (0, 0)
    kv_spec = pl.BlockSpec((1,) + k.shape[1:], lambda i: (i // per_b, 0, 0))
    return pl.pallas_call(
        _xattn_kernel,
        grid=(t // tm,),
        in_specs=[pl.BlockSpec((tm, d), lambda i: (i, 0)), kv_spec, kv_spec,
                  pl.BlockSpec((1, d), full2), pl.BlockSpec(wq.shape, full2),
                  pl.BlockSpec((1, XATTN_DH), full2), pl.BlockSpec(wo.shape, full2),
                  pl.BlockSpec((1, d), full2), pl.BlockSpec(rw_t.shape, full2),
                  pl.BlockSpec(rb.shape, full2)],
        out_specs=[pl.BlockSpec((tm, d), lambda i: (i, 0)),
                   pl.BlockSpec((ROW_CHUNKS, tm, 128), lambda i: (0, i, 0)),
                   pl.BlockSpec((8, tm), lambda i: (0, i)),
                   pl.BlockSpec((8, tm), lambda i: (0, i))],
        out_shape=[jax.ShapeDtypeStruct((t, d), F32),
                   jax.ShapeDtypeStruct((ROW_CHUNKS, t, 128), jnp.uint32),
                   jax.ShapeDtypeStruct((8, t), jnp.int32),
                   jax.ShapeDtypeStruct((8, t), F32)],
        compiler_params=_cparams("parallel"),
        name="xattn_router",
    )(x2d, k, v, gx, wq, gq, wo, gf, rw_t, rb)


def _moe_plan_kernel(eidx_ref, i1_ref, i2_ref, te_ref, na_ref, cnt_scr, carry_scr, *, tb, tm, plane_rows):
    ne = N_EXPERTS
    hp = lax.Precision.HIGHEST
    phase, j = pl.program_id(0), pl.program_id(1)
    rows = lax.broadcasted_iota(jnp.int32, (ne, tb), 0)
    oh1 = rows == eidx_ref[0:1, :]
    oh2 = rows == eidx_ref[1:2, :]
    a = oh1.astype(F32) + oh2.astype(F32)
    blk_cnt = jnp.broadcast_to(jnp.sum(a, axis=1, keepdims=True), cnt_scr.shape)

    @pl.when((phase == 0) & (j == 0))
    def _():
        cnt_scr[...] = jnp.zeros_like(cnt_scr)

    @pl.when(phase == 0)
    def _():
        cnt_scr[...] += blk_cnt

    @pl.when((phase == 1) & (j == 0))
    def _():
        padded = jnp.ceil(cnt_scr[...] * (1.0 / tm)) * tm
        er = lax.broadcasted_iota(jnp.int32, (ne, ne), 0)
        ec = lax.broadcasted_iota(jnp.int32, (ne, ne), 1)
        off = jnp.dot((ec < er).astype(F32), padded, precision=hp, preferred_element_type=F32)
        carry_scr[...] = off
        seg_end = (off + padded)[:, 0:1]
        tile_start = lax.broadcasted_iota(jnp.int32, (ne, te_ref.shape[1]), 1).astype(F32) * tm
        te = jnp.sum((seg_end <= tile_start).astype(F32), axis=0, keepdims=True)
        te_ref[...] = jnp.broadcast_to(jnp.minimum(te, ne - 1.0), te_ref.shape).astype(jnp.int32)
        total = jnp.sum(padded[:, 0:1], axis=0, keepdims=True)
        na_ref[...] = jnp.broadcast_to(total * (1.0 / tm), na_ref.shape).astype(jnp.int32)

    @pl.when(phase == 1)
    def _():
        before = (lax.broadcasted_iota(jnp.int32, (tb, tb), 0)
                  < lax.broadcasted_iota(jnp.int32, (tb, tb), 1)).astype(BF16)
        rank = carry_scr[:, 0:1] + _dot(a.astype(BF16), before)
        d1 = jnp.sum(jnp.where(oh1, rank, 0.0), axis=0, keepdims=True).astype(jnp.int32)
        d2 = jnp.sum(jnp.where(oh2, rank, 0.0), axis=0, keepdims=True).astype(jnp.int32)
        plane = lax.broadcasted_iota(jnp.int32, (8, tb), 0) * plane_rows
        i1_ref[...] = jnp.where(plane < ROW_CHUNKS * plane_rows, plane + d1, 0)
        i2_ref[...] = jnp.where(plane < ROW_CHUNKS * plane_rows, plane + d2, 0)
        carry_scr[...] += blk_cnt


def _moe_plan(eidx, *, tm, n_tiles, tb=512):
    t = eidx.shape[1]
    ntp = -(-n_tiles // 128) * 128
    return pl.pallas_call(
        functools.partial(_moe_plan_kernel, tb=tb, tm=tm, plane_rows=n_tiles * tm),
        grid=(2, t // tb),
        in_specs=[pl.BlockSpec((8, tb), lambda p, j: (0, j))],
        out_specs=[pl.BlockSpec((8, tb), lambda p, j: (0, j * p)),
                   pl.BlockSpec((8, tb), lambda p, j: (0, j * p)),
                   pl.BlockSpec((8, ntp), lambda p, j: (0, 0)),
                   pl.BlockSpec((8, 128), lambda p, j: (0, 0))],
        out_shape=[jax.ShapeDtypeStruct((8, t), jnp.int32),
                   jax.ShapeDtypeStruct((8, t), jnp.int32),
                   jax.ShapeDtypeStruct((8, ntp), jnp.int32),
                   jax.ShapeDtypeStruct((8, 128), jnp.int32)],
        scratch_shapes=[pltpu.VMEM((N_EXPERTS, 128), F32), pltpu.VMEM((N_EXPERTS, 128), F32)],
        compiler_params=_cparams("arbitrary", "arbitrary"),
        name="moe_plan",
    )(eidx)


def _sc_mesh():
    return plsc.VectorSubcoreMesh(core_axis_name="c", subcore_axis_name="s",
                                  num_cores=SC_CORES, num_subcores=SC_SUBCORES)


def _sc_index_spec(tokens):
    nb = tokens // SC_WINDOW
    return pl.BlockSpec((1, SC_WINDOW), lambda i: (i // nb, i % nb))


def _sc_dispatch(rows, i1, i2, n_out):
    n = rows.shape[0]
    tokens = i1.shape[1]

    @functools.partial(pl.kernel, out_type=jax.ShapeDtypeStruct((n_out, 128), rows.dtype), mesh=_sc_mesh(),
                       name="moe_dispatch")
    def k(x_hbm, i1_hbm, i2_hbm, o_hbm):
        def body(x_vmem, i1_vmem, i2_vmem):
            pltpu.sync_copy(x_vmem, o_hbm.at[i1_vmem.at[0]])
            pltpu.sync_copy(x_vmem, o_hbm.at[i2_vmem.at[0]])

        pltpu.emit_pipeline(
            body, grid=(n // SC_WINDOW,),
            in_specs=[pl.BlockSpec((SC_WINDOW, 128), lambda i: (i, 0)),
                      _sc_index_spec(tokens), _sc_index_spec(tokens)],
            out_specs=[],
            core_axis_name=("c", "s"), dimension_semantics=(pltpu.PARALLEL,),
        )(x_hbm, i1_hbm, i2_hbm)

    return k(rows, i1, i2)


def _sc_collect(table, i1, i2):
    tokens = i1.shape[1]
    n = ROW_CHUNKS * tokens
    out = jax.ShapeDtypeStruct((n, 128), table.dtype)

    @functools.partial(pl.kernel, out_type=(out, out), mesh=_sc_mesh(), name="moe_collect")
    def k(t_hbm, i1_hbm, i2_hbm, o1_hbm, o2_hbm):
        def body(i1_vmem, i2_vmem, o1_vmem, o2_vmem):
            pltpu.sync_copy(t_hbm.at[i1_vmem.at[0]], o1_vmem)
            pltpu.sync_copy(t_hbm.at[i2_vmem.at[0]], o2_vmem)

        pltpu.emit_pipeline(
            body, grid=(n // SC_WINDOW,),
            in_specs=[_sc_index_spec(tokens), _sc_index_spec(tokens)],
            out_specs=[pl.BlockSpec((SC_WINDOW, 128), lambda i: (i, 0)),
                       pl.BlockSpec((SC_WINDOW, 128), lambda i: (i, 0))],
            core_axis_name=("c", "s"), dimension_semantics=(pltpu.PARALLEL,),
        )(i1_hbm, i2_hbm, o1_hbm, o2_hbm)

    return k(table, i1, i2)


def _experts_kernel(te_ref, na_ref, xs_ref, wg_ref, wu_ref, wd_ref, y_ref, wg_scr, wu_scr, wd_scr):
    i = pl.program_id(0)
    active = i < na_ref[0]

    @pl.when(active & ((i == 0) | (te_ref[i] != te_ref[jnp.maximum(i - 1, 0)])))
    def _():
        wg_scr[...] = wg_ref[0, 0].astype(BF16)
        wu_scr[...] = wu_ref[0, 0].astype(BF16)
        wd_scr[...] = wd_ref[0, 0].astype(BF16)

    @pl.when(active)
    def _():
        hi, lo = _unpack_bf16_pairs(_load_row_chunks(xs_ref))
        h = jnp.concatenate([hi, lo], axis=-1).astype(BF16)
        up = _dot(h, wg_scr[...])
        act = up * _sigmoid(up) * _dot(h, wu_scr[...])
        _store_row_chunks(y_ref, _pack_bf16_pairs(_dot(act.astype(BF16), wd_scr[...])))


def _experts(tile_expert, n_active, xs, wg, wu, wd, *, layer, tm):
    n_tiles = tile_expert.shape[0]
    _, _, d, dff = wg.shape
    rows = lambda i, te, na: (0, jnp.minimum(i, na[0] - 1), 0)
    expert = lambda i, te, na: (layer, te[i], 0, 0)
    return pl.pallas_call(
        _experts_kernel,
        grid_spec=pltpu.PrefetchScalarGridSpec(
            num_scalar_prefetch=2,
            grid=(n_tiles,),
            in_specs=[pl.BlockSpec((ROW_CHUNKS, tm, 128), rows),
                      pl.BlockSpec((1, 1, d, dff), expert),
                      pl.BlockSpec((1, 1, d, dff), expert),
                      pl.BlockSpec((1, 1, dff, d), expert)],
            out_specs=pl.BlockSpec((ROW_CHUNKS, tm, 128), rows),
            scratch_shapes=[pltpu.VMEM((d, dff), BF16), pltpu.VMEM((d, dff), BF16), pltpu.VMEM((dff, d), BF16)]),
        out_shape=jax.ShapeDtypeStruct(xs.shape, xs.dtype),
        compiler_params=_cparams("arbitrary"),
        name="moe_experts",
    )(tile_expert, n_active, xs, wg, wu, wd)


def _moe_combine_kernel(x_ref, y1_ref, y2_ref, w_ref, o_ref):
    half = x_ref.shape[1] // 2
    hi1, lo1 = _unpack_bf16_pairs(_load_row_chunks(y1_ref))
    hi2, lo2 = _unpack_bf16_pairs(_load_row_chunks(y2_ref))
    w1, w2 = w_ref[:, 0:1], w_ref[:, 1:2]
    o_ref[:, :half] = x_ref[:, :half] + w1 * hi1 + w2 * hi2
    o_ref[:, half:] = x_ref[:, half:] + w1 * lo1 + w2 * lo2


def _moe_combine(x2d, y1, y2, wcol, *, tm):
    t, d = x2d.shape
    chunk_spec = pl.BlockSpec((ROW_CHUNKS, tm, 128), lambda i: (0, i, 0))
    return pl.pallas_call(
        _moe_combine_kernel,
        grid=(t // tm,),
        in_specs=[pl.BlockSpec((tm, d), lambda i: (i, 0)), chunk_spec, chunk_spec,
                  pl.BlockSpec((tm, wcol.shape[1]), lambda i: (i, 0))],
        out_specs=pl.BlockSpec((tm, d), lambda i: (i, 0)),
        out_shape=jax.ShapeDtypeStruct((t, d), F32),
        compiler_params=_cparams("parallel"),
        name="moe_combine",
    )(x2d, y1, y2, wcol)


def _moe(x2d, hf_rows, eidx, wts, wg, wu, wd, *, layer):
    t = x2d.shape[0]
    tm = MOE_TM
    n_tiles = 2 * t // tm + N_EXPERTS
    plane = n_tiles * tm
    i1, i2, te, na = _moe_plan(eidx, tm=tm, n_tiles=n_tiles)
    xs = _sc_dispatch(hf_rows.reshape(ROW_CHUNKS * t, 128), i1, i2, ROW_CHUNKS * plane)
    ys = _experts(te[0, :n_tiles], na[0, :1], xs.reshape(ROW_CHUNKS, plane, 128), wg, wu, wd,
                  layer=layer, tm=tm)
    y1, y2 = _sc_collect(ys.reshape(ROW_CHUNKS * plane, 128), i1, i2)
    return _moe_combine(x2d, y1.reshape(ROW_CHUNKS, t, 128), y2.reshape(ROW_CHUNKS, t, 128), wts[:2].T, tm=512)


def _w_in_layout_kernel(w_ref, main_ref, attn_ref):
    w = w_ref[0]
    src_if = 4 * MLSTM_W
    src_a = src_if + 2 * MLSTM_HEADS
    src_g = src_a + 3 * ATTN_W
    main_ref[0, :, OFF_MQ:OFF_GU] = w[:, 0:src_if].astype(BF16)
    main_ref[0, :, OFF_GU:OFF_IF] = w[:, src_g:src_g + OFF_IF - OFF_GU].astype(BF16)
    first = w[:, src_if:src_if + 128]
    lane = lax.broadcasted_iota(jnp.int32, first.shape, 1)
    main_ref[0, :, OFF_IF:OFF_IF + 128] = jnp.where(lane < 2 * MLSTM_HEADS, first, 0.0).astype(BF16)
    main_ref[0, :, OFF_IF + 128:N_PROJ] = jnp.zeros((w.shape[0], IF_PAD - 128), BF16)
    for g in range(len(ATTN_PATTERNS)):
        for j in range(3):
            src = src_a + j * ATTN_W + g * ATTN_GW
            dst = (3 * g + j) * ATTN_GW
            attn_ref[0, :, dst:dst + ATTN_GW] = w[:, src:src + ATTN_GW].astype(BF16)


def _w_in_layout(w_in, *, rows=256):
    depth, d, n_in = w_in.shape
    n_attn = 3 * ATTN_W
    return pl.pallas_call(
        _w_in_layout_kernel,
        grid=(depth, d // rows),
        in_specs=[pl.BlockSpec((1, rows, n_in), lambda l, i: (l, i, 0))],
        out_specs=[pl.BlockSpec((1, rows, N_PROJ), lambda l, i: (l, i, 0)),
                   pl.BlockSpec((1, rows, n_attn), lambda l, i: (l, i, 0))],
        out_shape=[jax.ShapeDtypeStruct((depth, d, N_PROJ), BF16),
                   jax.ShapeDtypeStruct((depth, d, n_attn), BF16)],
        compiler_params=_cparams("parallel", "parallel"),
        name="w_in_layout",
    )(w_in)


def kernel(x, mem, norm_mix, w_in, mlstm_conv, mlstm_gate_b, mlstm_norm, attn_qk_norm, gmlp_norm, gmlp_ws,
           gmlp_bs, w_branch_a, w_branch_b, w_branch_c, w_out, rel_bias, norm_xattn, norm_mem, w_xq, w_xkv,
           xattn_qk_norm, w_xo, norm_ffn, router_w, router_b, w_expert_gate, w_expert_up, w_expert_down):
    b, s, d = x.shape
    t = b * s
    depth = w_in.shape[0]
    x2d = x.reshape(t, d)

    biases = [_attn_bias(rel_bias, g) for g in range(len(ATTN_PATTERNS))]
    rw_t = jnp.zeros((N_EXPERT_GROUPS, 8, d), F32).at[:, :EXPERTS_PER_GROUP].set(
        router_w.T.reshape(N_EXPERT_GROUPS, EXPERTS_PER_GROUP, d)).reshape(ROUTER_ROWS, d)
    rb = jnp.full((N_EXPERT_GROUPS, 8), NEG, F32).at[:, :EXPERTS_PER_GROUP].set(
        router_b.astype(F32).reshape(N_EXPERT_GROUPS, EXPERTS_PER_GROUP)).reshape(ROUTER_ROWS, 1)
    tril = jnp.tril(jnp.ones((GMLP_CHUNK, GMLP_CHUNK), bool))
    head_of = jnp.arange(ATTN_GW) // ATTN_DH
    seg_ones = (head_of[:, None] == head_of[None, :]).astype(BF16)

    w_main, w_attn = _w_in_layout(w_in)

    for l in range(depth):
        proj, h_mix = _inproj(x2d, norm_mix[l][None], w_main, layer=l, tm=2048, tn=640)
        gq = jnp.tile(attn_qk_norm[l, 0], HEADS_PER_GROUP)[None]
        gk = jnp.tile(attn_qk_norm[l, 1], HEADS_PER_GROUP)[None]

        gates_row = proj[:, OFF_IF:OFF_IF + 8].astype(F32).reshape(b, s, 8).transpose(0, 2, 1)
        gb_col = jnp.zeros((1, IF_PAD), F32).at[0, :8].set(mlstm_gate_b[l])
        ya = _mlstm(proj, gates_row, mlstm_conv[l], gb_col, mlstm_gate_b[l].reshape(8, 1),
                    mlstm_norm[l][None], batch=b, seq=s, blk=MLSTM_BLOCK, nsub=MLSTM_NSUB,
                    group=MLSTM_GROUP)

        ybs, lses = [], []
        for g, (_, dilation) in enumerate(ATTN_PATTERNS):
            aproj = _attnproj(h_mix, w_attn, seg_ones, gq, gk, layer=l, group=g, dilation=dilation)
            o, lse = _dattn(aproj, biases[g], seq=s, group=g, dilation=dilation)
            ybs.append(o)
            lses.append(lse)

        ws = jnp.where(tril, gmlp_ws[l], 0.0).astype(BF16)
        bsb = jnp.broadcast_to(gmlp_bs[l][:, :, None], (GMLP_GROUPS, GMLP_CHUNK, GMLP_GC)).astype(F32)
        x2d = _merge(ya, ybs, lses, proj, x2d, w_branch_a[l].astype(BF16), w_branch_b[l].astype(BF16),
                     w_branch_c[l].astype(BF16), w_out[l].astype(BF16), ws, bsb, gmlp_norm[l][None], tm=512)

        k_mem, v_mem = _memkv(mem, norm_mem[l][None], w_xkv[l].astype(BF16), xattn_qk_norm[l, 1][None])
        x2d, hf_rows, eidx, wts = _xattn(x2d, k_mem, v_mem, norm_xattn[l][None], w_xq[l].astype(BF16),
                                         xattn_qk_norm[l, 0][None], w_xo[l].astype(BF16), norm_ffn[l][None],
                                         rw_t, rb, seq=s, tm=1024)

        x2d = _moe(x2d, hf_rows, eidx, wts, w_expert_gate, w_expert_up, w_expert_down, layer=l)

    return x2d.reshape(b, s, d)
```

```python
import functools
import math

import jax
import jax.numpy as jnp
import numpy as np
from jax import lax
from jax.experimental import pallas as pl
from jax.experimental.pallas import tpu as pltpu
from jax.experimental.pallas import tpu_sc as plsc

F32 = jnp.float32
BF16 = jnp.bfloat16

EPS = 1e-6
NEG = -1e30

MLSTM_HEADS = 4
MLSTM_DH = 128
MLSTM_W = MLSTM_HEADS * MLSTM_DH
CONV_WIDTH = 4
MLSTM_BLOCK = 128
MLSTM_NSUB = 1
MLSTM_GROUP = 2

ATTN_PATTERNS = ((128, 1), (512, 4), (2048, 16))
HEADS_PER_GROUP = 4
ATTN_DH = 64
ATTN_GW = HEADS_PER_GROUP * ATTN_DH
ATTN_W = len(ATTN_PATTERNS) * ATTN_GW
ATTN_BLOCK = 128
REL_BUCKETS = 32
REL_MAX_DIST = 2048

GMLP_GROUPS = 4
GMLP_GC = 128
GMLP_W = GMLP_GROUPS * GMLP_GC
GMLP_CHUNK = 128

XATTN_HEADS = 4
XATTN_DH = 128
XATTN_W = XATTN_HEADS * XATTN_DH

N_EXPERTS = 16
N_EXPERT_GROUPS = 4
EXPERTS_PER_GROUP = 4
ROUTER_ROWS = 8 * N_EXPERT_GROUPS

N_BRANCH = 3

MOE_TM = 512
ROW_CHUNKS = 4
SC_CORES, SC_SUBCORES = 2, 16
SC_WINDOW = 128

OFF_MQ, OFF_MK, OFF_MV, OFF_MO = 0, 512, 1024, 1536
OFF_GU, OFF_GV = 2048, 2560
OFF_GATE = 3072
OFF_IF = 6144
IF_PAD = 256
N_PROJ = OFF_IF + IF_PAD

ATTN_TILE = 2048
ATTN_SUB = ATTN_TILE // ATTN_BLOCK
ATTN_SLAB = 2 * ATTN_DH
ATTN_COLS = HEADS_PER_GROUP * ATTN_SLAB + 2 * ATTN_GW

VMEM_LIMIT = 56 * 1024 * 1024


def _cparams(*sem, flags=None):
    return pltpu.CompilerParams(dimension_semantics=sem, vmem_limit_bytes=VMEM_LIMIT, flags=flags)


def _rms(x, gain):
    return x * lax.rsqrt(jnp.mean(x * x, axis=-1, keepdims=True) + EPS) * gain


def _sigmoid(x):
    return 0.5 * jnp.tanh(0.5 * x) + 0.5


def _dot(a, b):
    return jnp.dot(a, b, preferred_element_type=F32)


def _dot_nt(a, b):
    return lax.dot_general(a, b, (((1,), (1,)), ((), ())), preferred_element_type=F32)


def _inproj_kernel(x_ref, g_ref, w_ref, o_ref, h_ref):
    @pl.when(pl.program_id(1) == 0)
    def _():
        h_ref[...] = _rms(x_ref[...], g_ref[...]).astype(BF16)

    o_ref[...] = _dot(h_ref[...], w_ref[0]).astype(o_ref.dtype)


def _inproj(x2d, gain, w, *, layer, tm, tn):
    t, d = x2d.shape
    n = w.shape[2]
    return pl.pallas_call(
        _inproj_kernel,
        grid=(t // tm, n // tn),
        in_specs=[pl.BlockSpec((tm, d), lambda i, j: (i, 0)),
                  pl.BlockSpec((1, d), lambda i, j: (0, 0)),
                  pl.BlockSpec((1, d, tn), lambda i, j: (layer, 0, j))],
        out_specs=[pl.BlockSpec((tm, tn), lambda i, j: (i, j)),
                   pl.BlockSpec((tm, d), lambda i, j: (i, 0))],
        out_shape=[jax.ShapeDtypeStruct((t, n), BF16), jax.ShapeDtypeStruct((t, d), BF16)],
        compiler_params=_cparams("parallel", "arbitrary"),
        name="inproj",
    )(x2d, gain, w)


def _log_sigmoid(x):
    return jnp.minimum(x, 0.0) - jnp.log(1.0 + jnp.exp(-jnp.abs(x)))


def _mlstm_kernel(qk_ref, v_ref, og_ref, gc_ref, gr_ref, cw_ref, gbc_ref, gbr_ref, ng_ref, y_ref,
                  xe_scr, s_scr, m_scr, *, blk, nsub, group):
    heads, w = MLSTM_HEADS, MLSTM_W

    @pl.when(pl.program_id(1) == 0)
    def _():
        xe_scr[:, 0:8, :] = jnp.zeros((group, 8, 2 * w), F32)
        s_scr[...] = jnp.zeros_like(s_scr)
        m_scr[...] = jnp.zeros_like(m_scr)

    cw = cw_ref[...]
    ri = lax.broadcasted_iota(jnp.int32, (blk, blk), 0)
    ci = lax.broadcasted_iota(jnp.int32, (blk, blk), 1)
    causal = ri >= ci
    tril = causal.astype(BF16)
    triu = (ri <= ci).astype(BF16)
    states = []
    for g in range(group):
        xe_scr[g, 8:8 + nsub * blk, :] = qk_ref[g].astype(F32)
        states.append([(s_scr[g, h], m_scr[g, h:h + 1, 0:1]) for h in range(heads)])
    for c in range(nsub):
        for g in range(group):
            states[g] = _mlstm_chunk(c * blk, blk, states[g], cw, causal, tril, triu, xe_scr.at[g], v_ref.at[g],
                                     og_ref.at[g], gc_ref.at[g], gr_ref.at[g], gbc_ref, gbr_ref, ng_ref,
                                     y_ref.at[g])
    for g in range(group):
        xe_scr[g, 0:8, :] = xe_scr[g, nsub * blk:nsub * blk + 8, :]
        for h, (s_st, m_st) in enumerate(states[g]):
            s_scr[g, h] = s_st
            m_scr[g, h:h + 1, :] = jnp.broadcast_to(m_st, (1, m_scr.shape[2]))


def _split_bf16(x):
    hi = x.astype(BF16)
    return hi, (x - hi.astype(F32)).astype(BF16)


def _mlstm_chunk(r0, blk, state, cw, causal, tril, triu, xe_scr, v_ref, og_ref, gc_ref, gr_ref, gbc_ref,
                 gbr_ref, ng_ref, y_ref):
    heads, dh, w = MLSTM_HEADS, MLSTM_DH, MLSTM_W
    rows = slice(r0, r0 + blk)
    conv = cw[CONV_WIDTH - 1:CONV_WIDTH, :] * xe_scr[8 + r0:8 + r0 + blk, :]
    for j in range(CONV_WIDTH - 1):
        off = 8 + r0 - (CONV_WIDTH - 1) + j
        conv = conv + cw[j:j + 1, :] * xe_scr[off:off + blk, :]
    qk = conv * _sigmoid(conv)

    gcol = gc_ref[rows, :].astype(F32) + gbc_ref[...]
    grow = gr_ref[:, rows] + gbr_ref[...]
    lc_hi, lc_lo = _split_bf16(_log_sigmoid(gcol))
    lr_hi, lr_lo = _split_bf16(_log_sigmoid(grow))
    bcol = _dot(tril, lc_hi) + _dot(tril, lc_lo)
    brow = _dot(lr_hi, triu) + _dot(lr_lo, triu)
    ones = jnp.ones((blk, dh), BF16)

    new_state = []
    for h in range(heads):
        sl = slice(h * dh, (h + 1) * dh)
        b_c = bcol[:, heads + h:heads + h + 1]
        i_c = gcol[:, h:h + 1]
        b_r = brow[heads + h:heads + h + 1, :]
        i_r = grow[h:h + 1, :]
        s_st, m_st = state[h]

        d_mat = jnp.where(causal, b_c - b_r + i_r, NEG)
        inter = b_c + m_st
        m_t = jnp.maximum(inter, jnp.max(d_mat, axis=-1, keepdims=True))
        w_intra = jnp.exp(d_mat - m_t)
        w_inter = jnp.exp(inter - m_t)

        q_f = qk[:, sl]
        k_f = qk[:, w + h * dh:w + (h + 1) * dh] * (dh ** -0.5)
        q_b = q_f.astype(BF16)
        k_b = k_f.astype(BF16)
        v_ext = jnp.concatenate([v_ref[rows, sl], ones], axis=-1)

        s = _dot_nt(q_b, k_b) * w_intra
        tot = _dot(s.astype(BF16), v_ext) + w_inter * _dot(q_b, s_st.astype(BF16))
        num, den = tot[:, :dh], tot[:, dh:]
        hh = num / jnp.maximum(jnp.abs(den), jnp.exp(-m_t))
        hn = _rms(hh, ng_ref[:, sl])
        y_ref[rows, sl] = (hn * _sigmoid(og_ref[rows, sl].astype(F32))).astype(y_ref.dtype)

        b_last = b_c[blk - 1:blk, :]
        dec = b_last - b_c + i_c
        m_new = jnp.maximum(b_last + m_st, jnp.max(dec, axis=0, keepdims=True))
        w_k = jnp.exp(dec - m_new)
        w_c = jnp.exp(b_last + m_st - m_new)
        kw = k_f * w_k
        new_state.append((w_c * s_st + _dot(kw.T.astype(BF16), v_ext), m_new))
    return new_state


def _mlstm(proj, gates_row, conv_w, gb_col, gb_row, norm_g, *, batch, seq, blk, nsub, group):
    t, npj = proj.shape
    rows = blk * nsub
    w = MLSTM_W
    proj3 = proj.reshape(batch, seq, npj)
    cols = lambda c: (lambda b, i: (b, i, c))
    const2 = lambda b, i: (0, 0)
    y = pl.pallas_call(
        functools.partial(_mlstm_kernel, blk=blk, nsub=nsub, group=group),
        grid=(batch // group, seq // rows),
        in_specs=[pl.BlockSpec((group, rows, 2 * w), cols(OFF_MQ // (2 * w))),
                  pl.BlockSpec((group, rows, w), cols(OFF_MV // w)),
                  pl.BlockSpec((group, rows, w), cols(OFF_MO // w)),
                  pl.BlockSpec((group, rows, IF_PAD), cols(OFF_IF // IF_PAD)),
                  pl.BlockSpec((group, 8, rows), lambda b, i: (b, 0, i)),
                  pl.BlockSpec((CONV_WIDTH, 2 * w), const2),
                  pl.BlockSpec((1, IF_PAD), const2),
                  pl.BlockSpec((8, 1), const2),
                  pl.BlockSpec((1, w), const2)],
        out_specs=pl.BlockSpec((group, rows, w), cols(0)),
        out_shape=jax.ShapeDtypeStruct((batch, seq, w), BF16),
        scratch_shapes=[pltpu.VMEM((group, rows + 8, 2 * w), F32),
                        pltpu.VMEM((group, MLSTM_HEADS, MLSTM_DH, 2 * MLSTM_DH), F32),
                        pltpu.VMEM((group, 8, 128), F32)],
        compiler_params=_cparams("parallel", "arbitrary"),
        name="mlstm",
    )(proj3, proj3, proj3, proj3, gates_row, conv_w, gb_col, gb_row, norm_g)
    return y.reshape(t, w)


def _attnproj_kernel(h_ref, w_ref, seg_ref, gq_ref, gk_ref, o_ref, r_scr, *, dil):
    gw, half = ATTN_GW, ATTN_SLAB // 2
    sub_rows = r_scr.shape[2]
    seg, sub_seg = ATTN_TILE // dil, sub_rows // dil

    def head_norm(x, gain):
        sq = x * x
        hi = sq.astype(BF16)
        lo = (sq - hi.astype(F32)).astype(BF16)
        ss = _dot(hi, seg_ref[...]) + _dot(lo, seg_ref[...])
        return x * lax.rsqrt(ss * (1.0 / ATTN_DH) + EPS) * gain

    low = lax.broadcasted_iota(jnp.int32, (1, ATTN_SLAB), 1) < half
    for s in range(ATTN_TILE // sub_rows):
        rows = slice(s * sub_rows, (s + 1) * sub_rows)
        res = _dot(h_ref[rows, :], w_ref[0])
        q = head_norm(res[:, :gw], gq_ref[...]) * (ATTN_DH ** -0.5)
        k = head_norm(res[:, gw:2 * gw], gk_ref[...])
        slabs = []
        for pair in range(gw // ATTN_SLAB):
            qp = q[:, pair * ATTN_SLAB:(pair + 1) * ATTN_SLAB]
            slabs += [jnp.where(low, qp, 0.0), jnp.where(low, 0.0, qp)]
        slabs += [k[:, c * 128:(c + 1) * 128] for c in range(gw // 128)]
        slabs += [res[:, 2 * gw + c * 128:2 * gw + (c + 1) * 128] for c in range(gw // 128)]
        for c, slab in enumerate(slabs):
            if dil == 1:
                o_ref[rows, c * 128:(c + 1) * 128] = slab.astype(o_ref.dtype)
            else:
                r_scr[s % 2, c] = slab
        if dil > 1:
            for r in range(dil):
                dst = slice(r * seg + s * sub_seg, r * seg + (s + 1) * sub_seg)
                for c in range(r_scr.shape[1]):
                    o_ref[dst, c * 128:(c + 1) * 128] = (
                        r_scr[s % 2, c, pl.ds(r, sub_seg, stride=dil), :].astype(o_ref.dtype))


def _attnproj(h, w, seg_ones, gq, gk, *, layer, group, dilation):
    t, d = h.shape
    wcols = 3 * ATTN_GW
    const2 = lambda i: (0, 0)
    return pl.pallas_call(
        functools.partial(_attnproj_kernel, dil=dilation),
        grid=(t // ATTN_TILE,),
        in_specs=[pl.BlockSpec((ATTN_TILE, d), lambda i: (i, 0)),
                  pl.BlockSpec((1, d, wcols), lambda i: (layer, 0, group)),
                  pl.BlockSpec((ATTN_GW, ATTN_GW), const2),
                  pl.BlockSpec((1, ATTN_GW), const2), pl.BlockSpec((1, ATTN_GW), const2)],
        out_specs=pl.BlockSpec((ATTN_TILE, ATTN_COLS), lambda i: (i, 0)),
        out_shape=jax.ShapeDtypeStruct((t, ATTN_COLS), BF16),
        scratch_shapes=[pltpu.VMEM((2, ATTN_COLS // 128, 512, 128), F32)],
        compiler_params=_cparams("parallel"),
        name=f"attnproj{group}",
    )(h, w, seg_ones, gq, gk)


def _dattn_kernel(q_ref, kc_ref, kp_ref, vc_ref, vp_ref, bias_ref, o_ref, lse_ref,
                  kx_scr, vx_scr, o_scr, l_scr, *, dil):
    blk = ATTN_BLOCK
    per = ATTN_SUB // dil
    first_tile = pl.program_id(1) == 0
    for r in range(dil):
        base = r * (per + 1) * blk
        last = slice((r * per + per - 1) * blk, (r * per + per) * blk)
        mine = slice(r * per * blk, (r + 1) * per * blk)
        kx_scr[base:base + blk, :] = kp_ref[last, :]
        vx_scr[base:base + blk, :] = vp_ref[last, :]
        kx_scr[base + blk:base + (per + 1) * blk, :] = kc_ref[mine, :]
        vx_scr[base + blk:base + (per + 1) * blk, :] = vc_ref[mine, :]

    low = lax.broadcasted_iota(jnp.int32, (1, ATTN_SLAB), 1) < ATTN_SLAB // 2
    no_prev = lax.broadcasted_iota(jnp.int32, (1, 2 * blk), 1) < blk
    for r in range(dil):
        for sub in range(per):
            u = r * per + sub
            win = slice((r * (per + 1) + sub) * blk, (r * (per + 1) + sub + 2) * blk)
            o_slabs, l_slabs = [], []
            for pair in range(ATTN_GW // ATTN_SLAB):
                cols = slice(pair * ATTN_SLAB, (pair + 1) * ATTN_SLAB)
                kx, vx = kx_scr[win, cols], vx_scr[win, cols]
                o_pair, l_pair = [], []
                for h in (2 * pair, 2 * pair + 1):
                    logits = _dot_nt(q_ref[u * blk:(u + 1) * blk, h * ATTN_SLAB:(h + 1) * ATTN_SLAB], kx)
                    logits = logits + bias_ref[h]
                    if sub == 0:
                        logits = jnp.where(first_tile & no_prev, NEG, logits)
                    m = jnp.max(logits, axis=-1, keepdims=True)
                    p = jnp.exp(logits - m)
                    l = jnp.sum(p, axis=-1, keepdims=True)
                    o_pair.append(_dot(p.astype(BF16), vx) / l)
                    l_pair.append(m + jnp.log(l))
                o_slabs.append(jnp.where(low, o_pair[0], o_pair[1]))
                l_slabs.append(jnp.where(low, l_pair[0], l_pair[1]))
            dst = pl.ds(sub * blk * dil + r, blk, stride=dil) if dil > 1 else slice(u * blk, (u + 1) * blk)
            for c in range(ATTN_GW // ATTN_SLAB):
                o_scr[c, dst, :] = o_slabs[c]
                l_scr[c, dst, :] = l_slabs[c]
    for c in range(ATTN_GW // ATTN_SLAB):
        o_ref[:, c * ATTN_SLAB:(c + 1) * ATTN_SLAB] = o_scr[c].astype(o_ref.dtype)
        lse_ref[:, c * ATTN_SLAB:(c + 1) * ATTN_SLAB] = l_scr[c]


def _dattn(aproj, bias, *, seq, group, dilation):
    t = aproj.shape[0]
    tiles = seq // ATTN_TILE
    qw = HEADS_PER_GROUP * ATTN_SLAB
    cq, ck, cv = 0, qw // ATTN_GW, qw // ATTN_GW + 1
    blk = (ATTN_TILE, ATTN_GW)
    cur = lambda c: (lambda b, j: (b * tiles + j, c))
    prev = lambda c: (lambda b, j: (b * tiles + jnp.maximum(j - 1, 0), c))
    xrows = ATTN_TILE + dilation * ATTN_BLOCK
    return pl.pallas_call(
        functools.partial(_dattn_kernel, dil=dilation),
        grid=(t // seq, tiles),
        in_specs=[pl.BlockSpec((ATTN_TILE, qw), cur(cq)),
                  pl.BlockSpec(blk, cur(ck)), pl.BlockSpec(blk, prev(ck)),
                  pl.BlockSpec(blk, cur(cv)), pl.BlockSpec(blk, prev(cv)),
                  pl.BlockSpec((HEADS_PER_GROUP, ATTN_BLOCK, 2 * ATTN_BLOCK), lambda b, j: (0, 0, 0))],
        out_specs=[pl.BlockSpec(blk, cur(0)), pl.BlockSpec(blk, cur(0))],
        out_shape=[jax.ShapeDtypeStruct((t, ATTN_GW), BF16), jax.ShapeDtypeStruct((t, ATTN_GW), F32)],
        scratch_shapes=[pltpu.VMEM((xrows, ATTN_GW), BF16), pltpu.VMEM((xrows, ATTN_GW), BF16),
                        pltpu.VMEM((ATTN_GW // ATTN_SLAB, ATTN_TILE, ATTN_SLAB), F32),
                        pltpu.VMEM((ATTN_GW // ATTN_SLAB, ATTN_TILE, ATTN_SLAB), F32)],
        compiler_params=_cparams("parallel", "arbitrary"),
        name=f"dattn{group}",
    )(aproj, aproj, aproj, aproj, aproj, bias)


def _rel_bucket(n):
    max_exact = REL_BUCKETS // 2
    nf = jnp.maximum(n, 1).astype(F32)
    log_b = max_exact + (jnp.log(nf / max_exact) / math.log(REL_MAX_DIST / max_exact)
                         * (REL_BUCKETS - max_exact)).astype(jnp.int32)
    return jnp.where(n < max_exact, n, jnp.minimum(log_b, REL_BUCKETS - 1))


def _attn_bias(rel_bias, group):
    window, dilation = ATTN_PATTERNS[group]
    steps = window // dilation
    hp = lax.Precision.HIGHEST
    hs = slice(group * HEADS_PER_GROUP, (group + 1) * HEADS_PER_GROUP)
    bucket = _rel_bucket(jnp.arange(steps + 1) * dilation)
    bias_steps = jnp.dot(jax.nn.one_hot(bucket, REL_BUCKETS, dtype=F32), rel_bias[:, hs].astype(F32),
                         precision=hp)
    qi = jnp.arange(ATTN_BLOCK)[:, None]
    ki = jnp.arange(2 * ATTN_BLOCK)[None, :]
    dist = ATTN_BLOCK + qi - ki
    ok = (dist >= 0) & (dist <= steps)
    sel = jax.nn.one_hot(jnp.clip(dist, 0, steps).reshape(-1), steps + 1, dtype=F32)
    bias = jnp.dot(sel, bias_steps, precision=hp).T.reshape(HEADS_PER_GROUP, ATTN_BLOCK, 2 * ATTN_BLOCK)
    return jnp.where(ok[None], bias, NEG)


def _merge_kernel(ya_ref, yb0_ref, yb1_ref, yb2_ref, l0_ref, l1_ref, l2_ref, gu_ref, gv_ref, gate_ref,
                  x_ref, wa_ref, wb_ref, wc_ref, wo_ref, ws_ref, bs_ref, gg_ref, o_ref, yc_scr, *, tm):
    d = x_ref.shape[1]
    l0, l1, l2 = l0_ref[...], l1_ref[...], l2_ref[...]
    mx = jnp.maximum(jnp.maximum(l0, l1), l2)
    e0, e1, e2 = jnp.exp(l0 - mx), jnp.exp(l1 - mx), jnp.exp(l2 - mx)
    inv = 1.0 / (e0 + e1 + e2)
    yb = jnp.concatenate([(yb0_ref[...].astype(F32) * (e0 * inv)).astype(BF16),
                          (yb1_ref[...].astype(F32) * (e1 * inv)).astype(BF16),
                          (yb2_ref[...].astype(F32) * (e2 * inv)).astype(BF16)], axis=-1)

    for j in range(tm // GMLP_CHUNK):
        rows = slice(j * GMLP_CHUNK, (j + 1) * GMLP_CHUNK)
        for g in range(GMLP_GROUPS):
            cols = slice(g * GMLP_GC, (g + 1) * GMLP_GC)
            u = jax.nn.gelu(gu_ref[rows, cols].astype(F32))
            v = _rms(jax.nn.gelu(gv_ref[rows, cols].astype(F32)), gg_ref[:, cols])
            mixed = _dot(ws_ref[g], v.astype(BF16)) + bs_ref[g]
            yc_scr[rows, cols] = (u * mixed).astype(BF16)

    def gate2(k):
        return jnp.tanh(0.5 * gate_ref[:, k * d:(k + 1) * d].astype(F32)) + 1.0

    merged2 = gate2(0) * _dot(ya_ref[...], wa_ref[...])
    merged2 = merged2 + gate2(1) * _dot(yb, wb_ref[...])
    merged2 = merged2 + gate2(2) * _dot(yc_scr[...], wc_ref[...])
    o_ref[...] = x_ref[...] + 0.5 * _dot(merged2.astype(BF16), wo_ref[...])


def _merge(ya, ybs, lses, proj, x2d, wa, wb, wc, wo, ws, bsb, gg, *, tm):
    t, d = x2d.shape
    row = lambda c: (lambda i: (i, c))
    full2 = lambda i: (0, 0)
    full3 = lambda i: (0, 0, 0)
    gspec = pl.BlockSpec((tm, ATTN_GW), row(0))
    return pl.pallas_call(
        functools.partial(_merge_kernel, tm=tm),
        grid=(t // tm,),
        in_specs=[pl.BlockSpec((tm, MLSTM_W), row(0)),
                  gspec, gspec, gspec, gspec, gspec, gspec,
                  pl.BlockSpec((tm, GMLP_W), row(OFF_GU // GMLP_W)),
                  pl.BlockSpec((tm, GMLP_W), row(OFF_GV // GMLP_W)),
                  pl.BlockSpec((tm, N_BRANCH * d), row(OFF_GATE // (N_BRANCH * d))),
                  pl.BlockSpec((tm, d), row(0)),
                  pl.BlockSpec(wa.shape, full2), pl.BlockSpec(wb.shape, full2),
                  pl.BlockSpec(wc.shape, full2), pl.BlockSpec(wo.shape, full2),
                  pl.BlockSpec(ws.shape, full3), pl.BlockSpec(bsb.shape, full3),
                  pl.BlockSpec(gg.shape, full2)],
        out_specs=pl.BlockSpec((tm, d), row(0)),
        out_shape=jax.ShapeDtypeStruct((t, d), F32),
        scratch_shapes=[pltpu.VMEM((tm, GMLP_W), BF16)],
        compiler_params=_cparams("parallel"),
        name="merge",
    )(ya, *ybs, *lses, proj, proj, proj, x2d, wa, wb, wc, wo, ws, bsb, gg)


def _memkv_kernel(mem_ref, g_ref, w_ref, gk_ref, k_ref, v_ref):
    dh, w = XATTN_DH, XATTN_W
    kv = _dot(_rms(mem_ref[0], g_ref[...]).astype(BF16), w_ref[...])
    for h in range(XATTN_HEADS):
        sl = slice(h * dh, (h + 1) * dh)
        k_ref[0, :, sl] = _rms(kv[:, sl], gk_ref[...]).astype(k_ref.dtype)
    v_ref[0] = kv[:, w:].astype(v_ref.dtype)


def _memkv(mem, gain, w_kv, gk):
    b, m, d = mem.shape
    full2 = lambda i: (0, 0)
    return pl.pallas_call(
        _memkv_kernel,
        grid=(b,),
        in_specs=[pl.BlockSpec((1, m, d), lambda i: (i, 0, 0)),
                  pl.BlockSpec((1, d), full2),
                  pl.BlockSpec(w_kv.shape, full2),
                  pl.BlockSpec((1, XATTN_DH), full2)],
        out_specs=[pl.BlockSpec((1, m, XATTN_W), lambda i: (i, 0, 0)),
                   pl.BlockSpec((1, m, XATTN_W), lambda i: (i, 0, 0))],
        out_shape=[jax.ShapeDtypeStruct((b, m, XATTN_W), BF16),
                   jax.ShapeDtypeStruct((b, m, XATTN_W), BF16)],
        compiler_params=_cparams("parallel"),
        name="memkv",
    )(mem, gain, w_kv, gk)


def _route(logits):
    tm = logits.shape[1]
    e = jnp.exp(logits - jnp.max(logits, axis=0, keepdims=True))
    probs = e / jnp.sum(e, axis=0, keepdims=True)
    rowi = lax.broadcasted_iota(jnp.int32, (8, tm), 0)
    real = rowi < EXPERTS_PER_GROUP
    tops = []
    for g in range(N_EXPERT_GROUPS):
        pg = jnp.where(real, probs[8 * g:8 * g + 8, :], -0.5)
        m1 = jnp.max(pg, axis=0, keepdims=True)
        i1 = jnp.min(jnp.where(pg == m1, rowi, 8), axis=0, keepdims=True)
        pg2 = jnp.where(rowi == i1, -1.0, pg)
        m2 = jnp.max(pg2, axis=0, keepdims=True)
        i2 = jnp.min(jnp.where(pg2 == m2, rowi, 8), axis=0, keepdims=True)
        tops.append((m1, i1, m2, i2))
    best = jnp.zeros((1, tm), jnp.int32)
    best_score = tops[0][0] + tops[0][2]
    for g in range(1, N_EXPERT_GROUPS):
        score = tops[g][0] + tops[g][2]
        better = score > best_score
        best = jnp.where(better, g, best)
        best_score = jnp.where(better, score, best_score)
    m1, i1, m2, i2 = tops[0]
    for g in range(1, N_EXPERT_GROUPS):
        m1, i1, m2, i2 = (jnp.where(best == g, new, old) for new, old in zip(tops[g], (m1, i1, m2, i2)))
    tot = m1 + m2
    base = best * EXPERTS_PER_GROUP
    return base + i1, base + i2, m1 / tot, m2 / tot


def _pack_bf16_pairs(x):
    n = x.shape[1] // 2
    hi = lax.bitcast_convert_type(x[:, :n].astype(BF16).astype(F32), jnp.uint32)
    lo = lax.bitcast_convert_type(x[:, n:].astype(BF16).astype(F32), jnp.uint32)
    return hi | (lo >> 16)


def _unpack_bf16_pairs(p):
    hi = lax.bitcast_convert_type(p & jnp.uint32(0xFFFF0000), F32)
    lo = lax.bitcast_convert_type(p << 16, F32)
    return hi, lo


def _store_row_chunks(ref, packed):
    for j in range(ROW_CHUNKS):
        ref[j] = packed[:, j * 128:(j + 1) * 128]


def _load_row_chunks(ref):
    return jnp.concatenate([ref[j] for j in range(ROW_CHUNKS)], axis=-1)


def _xattn_kernel(x_ref, k_ref, v_ref, gx_ref, wq_ref, gq_ref, wo_ref, gf_ref, rw_ref, rb_ref,
                  xo_ref, hf_ref, eidx_ref, wts_ref):
    dh = XATTN_DH
    x = x_ref[...]
    q = _dot(_rms(x, gx_ref[...]).astype(BF16), wq_ref[...])
    outs = []
    for h in range(XATTN_HEADS):
        sl = slice(h * dh, (h + 1) * dh)
        q_h = (_rms(q[:, sl], gq_ref[...]) * (dh ** -0.5)).astype(BF16)
        logits = _dot_nt(q_h, k_ref[0, :, sl])
        p = jnp.exp(logits - jnp.max(logits, axis=-1, keepdims=True))
        o = _dot(p.astype(BF16), v_ref[0, :, sl]) / jnp.sum(p, axis=-1, keepdims=True)
        outs.append(o.astype(BF16))
    xn = x + _dot(jnp.concatenate(outs, axis=-1), wo_ref[...])
    xo_ref[...] = xn
    hf = _rms(xn, gf_ref[...])
    _store_row_chunks(hf_ref, _pack_bf16_pairs(hf))
    rw = rw_ref[...]
    rw_hi = rw.astype(BF16)
    rw_lo = (rw - rw_hi.astype(F32)).astype(BF16)
    hf_hi = hf.astype(BF16)
    hf_lo = (hf - hf_hi.astype(F32)).astype(BF16)
    logits_t = _dot_nt(rw_hi, hf_hi) + _dot_nt(rw_hi, hf_lo) + _dot_nt(rw_lo, hf_hi) + rb_ref[...]
    e1, e2, w1, w2 = _route(logits_t)
    tm = x.shape[0]
    eidx_ref[...] = jnp.concatenate([e1, e2, jnp.zeros((6, tm), jnp.int32)], axis=0)
    wts_ref[...] = jnp.concatenate([w1, w2, jnp.zeros((6, tm), F32)], axis=0)


def _xattn(x2d, k, v, gx, wq, gq, wo, gf, rw_t, rb, *, seq, tm):
    t, d = x2d.shape
    per_b = seq // tm
    full2 = lambda i: (0, 0)
    kv_spec = pl.BlockSpec((1,) + k.shape[1:], lambda i: (i // per_b, 0, 0))
    return pl.pallas_call(
        _xattn_kernel,
        grid=(t // tm,),
        in_specs=[pl.BlockSpec((tm, d), lambda i: (i, 0)), kv_spec, kv_spec,
                  pl.BlockSpec((1, d), full2), pl.BlockSpec(wq.shape, full2),
                  pl.BlockSpec((1, XATTN_DH), full2), pl.BlockSpec(wo.shape, full2),
                  pl.BlockSpec((1, d), full2), pl.BlockSpec(rw_t.shape, full2),
                  pl.BlockSpec(rb.shape, full2)],
        out_specs=[pl.BlockSpec((tm, d), lambda i: (i, 0)),
                   pl.BlockSpec((ROW_CHUNKS, tm, 128), lambda i: (0, i, 0)),
                   pl.BlockSpec((8, tm), lambda i: (0, i)),
                   pl.BlockSpec((8, tm), lambda i: (0, i))],
        out_shape=[jax.ShapeDtypeStruct((t, d), F32),
                   jax.ShapeDtypeStruct((ROW_CHUNKS, t, 128), jnp.uint32),
                   jax.ShapeDtypeStruct((8, t), jnp.int32),
                   jax.ShapeDtypeStruct((8, t), F32)],
        compiler_params=_cparams("parallel"),
        name="xattn_router",
    )(x2d, k, v, gx, wq, gq, wo, gf, rw_t, rb)


def _moe_plan_kernel(eidx_ref, i1_ref, i2_ref, te_ref, na_ref, cnt_scr, carry_scr, *, tb, tm, plane_rows):
    ne = N_EXPERTS
    hp = lax.Precision.HIGHEST
    phase, j = pl.program_id(0), pl.program_id(1)
    rows = lax.broadcasted_iota(jnp.int32, (ne, tb), 0)
    oh1 = rows == eidx_ref[0:1, :]
    oh2 = rows == eidx_ref[1:2, :]
    a = oh1.astype(F32) + oh2.astype(F32)
    blk_cnt = jnp.broadcast_to(jnp.sum(a, axis=1, keepdims=True), cnt_scr.shape)

    @pl.when((phase == 0) & (j == 0))
    def _():
        cnt_scr[...] = jnp.zeros_like(cnt_scr)

    @pl.when(phase == 0)
    def _():
        cnt_scr[...] += blk_cnt

    @pl.when((phase == 1) & (j == 0))
    def _():
        padded = jnp.ceil(cnt_scr[...] * (1.0 / tm)) * tm
        er = lax.broadcasted_iota(jnp.int32, (ne, ne), 0)
        ec = lax.broadcasted_iota(jnp.int32, (ne, ne), 1)
        off = jnp.dot((ec < er).astype(F32), padded, precision=hp, preferred_element_type=F32)
        carry_scr[...] = off
        seg_end = (off + padded)[:, 0:1]
        tile_start = lax.broadcasted_iota(jnp.int32, (ne, te_ref.shape[1]), 1).astype(F32) * tm
        te = jnp.sum((seg_end <= tile_start).astype(F32), axis=0, keepdims=True)
        te_ref[...] = jnp.broadcast_to(jnp.minimum(te, ne - 1.0), te_ref.shape).astype(jnp.int32)
        total = jnp.sum(padded[:, 0:1], axis=0, keepdims=True)
        na_ref[...] = jnp.broadcast_to(total * (1.0 / tm), na_ref.shape).astype(jnp.int32)

    @pl.when(phase == 1)
    def _():
        before = (lax.broadcasted_iota(jnp.int32, (tb, tb), 0)
                  < lax.broadcasted_iota(jnp.int32, (tb, tb), 1)).astype(BF16)
        rank = carry_scr[:, 0:1] + _dot(a.astype(BF16), before)
        d1 = jnp.sum(jnp.where(oh1, rank, 0.0), axis=0, keepdims=True).astype(jnp.int32)
        d2 = jnp.sum(jnp.where(oh2, rank, 0.0), axis=0, keepdims=True).astype(jnp.int32)
        plane = lax.broadcasted_iota(jnp.int32, (8, tb), 0) * plane_rows
        i1_ref[...] = jnp.where(plane < ROW_CHUNKS * plane_rows, plane + d1, 0)
        i2_ref[...] = jnp.where(plane < ROW_CHUNKS * plane_rows, plane + d2, 0)
        carry_scr[...] += blk_cnt


def _moe_plan(eidx, *, tm, n_tiles, tb=512):
    t = eidx.shape[1]
    ntp = -(-n_tiles // 128) * 128
    return pl.pallas_call(
        functools.partial(_moe_plan_kernel, tb=tb, tm=tm, plane_rows=n_tiles * tm),
        grid=(2, t // tb),
        in_specs=[pl.BlockSpec((8, tb), lambda p, j: (0, j))],
        out_specs=[pl.BlockSpec((8, tb), lambda p, j: (0, j * p)),
                   pl.BlockSpec((8, tb), lambda p, j: (0, j * p)),
                   pl.BlockSpec((8, ntp), lambda p, j: (0, 0)),
                   pl.BlockSpec((8, 128), lambda p, j: (0, 0))],
        out_shape=[jax.ShapeDtypeStruct((8, t), jnp.int32),
                   jax.ShapeDtypeStruct((8, t), jnp.int32),
                   jax.ShapeDtypeStruct((8, ntp), jnp.int32),
                   jax.ShapeDtypeStruct((8, 128), jnp.int32)],
        scratch_shapes=[pltpu.VMEM((N_EXPERTS, 128), F32), pltpu.VMEM((N_EXPERTS, 128), F32)],
        compiler_params=_cparams("arbitrary", "arbitrary"),
        name="moe_plan",
    )(eidx)


def _sc_mesh():
    return plsc.VectorSubcoreMesh(core_axis_name="c", subcore_axis_name="s",
                                  num_cores=SC_CORES, num_subcores=SC_SUBCORES)


def _sc_index_spec(tokens):
    nb = tokens // SC_WINDOW
    return pl.BlockSpec((1, SC_WINDOW), lambda i: (i // nb, i % nb))


def _sc_dispatch(rows, i1, i2, n_out):
    n = rows.shape[0]
    tokens = i1.shape[1]

    @functools.partial(pl.kernel, out_type=jax.ShapeDtypeStruct((n_out, 128), rows.dtype), mesh=_sc_mesh(),
                       name="moe_dispatch")
    def k(x_hbm, i1_hbm, i2_hbm, o_hbm):
        def body(x_vmem, i1_vmem, i2_vmem):
            pltpu.sync_copy(x_vmem, o_hbm.at[i1_vmem.at[0]])
            pltpu.sync_copy(x_vmem, o_hbm.at[i2_vmem.at[0]])

        pltpu.emit_pipeline(
            body, grid=(n // SC_WINDOW,),
            in_specs=[pl.BlockSpec((SC_WINDOW, 128), lambda i: (i, 0)),
                      _sc_index_spec(tokens), _sc_index_spec(tokens)],
            out_specs=[],
            core_axis_name=("c", "s"), dimension_semantics=(pltpu.PARALLEL,),
        )(x_hbm, i1_hbm, i2_hbm)

    return k(rows, i1, i2)


def _sc_collect(table, i1, i2):
    tokens = i1.shape[1]
    n = ROW_CHUNKS * tokens
    out = jax.ShapeDtypeStruct((n, 128), table.dtype)

    @functools.partial(pl.kernel, out_type=(out, out), mesh=_sc_mesh(), name="moe_collect")
    def k(t_hbm, i1_hbm, i2_hbm, o1_hbm, o2_hbm):
        def body(i1_vmem, i2_vmem, o1_vmem, o2_vmem):
            pltpu.sync_copy(t_hbm.at[i1_vmem.at[0]], o1_vmem)
            pltpu.sync_copy(t_hbm.at[i2_vmem.at[0]], o2_vmem)

        pltpu.emit_pipeline(
            body, grid=(n // SC_WINDOW,),
            in_specs=[_sc_index_spec(tokens), _sc_index_spec(tokens)],
            out_specs=[pl.BlockSpec((SC_WINDOW, 128), lambda i: (i, 0)),
                       pl.BlockSpec((SC_WINDOW, 128), lambda i: (i, 0))],
            core_axis_name=("c", "s"), dimension_semantics=(pltpu.PARALLEL,),
        )(i1_hbm, i2_hbm, o1_hbm, o2_hbm)

    return k(table, i1, i2)


def _experts_kernel(te_ref, na_ref, xs_ref, wg_ref, wu_ref, wd_ref, y_ref, wg_scr, wu_scr, wd_scr):
    i = pl.program_id(0)
    active = i < na_ref[0]

    @pl.when(active & ((i == 0) | (te_ref[i] != te_ref[jnp.maximum(i - 1, 0)])))
    def _():
        wg_scr[...] = wg_ref[0, 0].astype(BF16)
        wu_scr[...] = wu_ref[0, 0].astype(BF16)
        wd_scr[...] = wd_ref[0, 0].astype(BF16)

    @pl.when(active)
    def _():
        hi, lo = _unpack_bf16_pairs(_load_row_chunks(xs_ref))
        h = jnp.concatenate([hi, lo], axis=-1).astype(BF16)
        up = _dot(h, wg_scr[...])
        act = up * _sigmoid(up) * _dot(h, wu_scr[...])
        _store_row_chunks(y_ref, _pack_bf16_pairs(_dot(act.astype(BF16), wd_scr[...])))


def _experts(tile_expert, n_active, xs, wg, wu, wd, *, layer, tm):
    n_tiles = tile_expert.shape[0]
    _, _, d, dff = wg.shape
    rows = lambda i, te, na: (0, jnp.minimum(i, na[0] - 1), 0)
    expert = lambda i, te, na: (layer, te[i], 0, 0)
    return pl.pallas_call(
        _experts_kernel,
        grid_spec=pltpu.PrefetchScalarGridSpec(
            num_scalar_prefetch=2,
            grid=(n_tiles,),
            in_specs=[pl.BlockSpec((ROW_CHUNKS, tm, 128), rows),
                      pl.BlockSpec((1, 1, d, dff), expert),
                      pl.BlockSpec((1, 1, d, dff), expert),
                      pl.BlockSpec((1, 1, dff, d), expert)],
            out_specs=pl.BlockSpec((ROW_CHUNKS, tm, 128), rows),
            scratch_shapes=[pltpu.VMEM((d, dff), BF16), pltpu.VMEM((d, dff), BF16), pltpu.VMEM((dff, d), BF16)]),
        out_shape=jax.ShapeDtypeStruct(xs.shape, xs.dtype),
        compiler_params=_cparams("arbitrary"),
        name="moe_experts",
    )(tile_expert, n_active, xs, wg, wu, wd)


def _moe_combine_kernel(x_ref, y1_ref, y2_ref, w_ref, o_ref):
    half = x_ref.shape[1] // 2
    hi1, lo1 = _unpack_bf16_pairs(_load_row_chunks(y1_ref))
    hi2, lo2 = _unpack_bf16_pairs(_load_row_chunks(y2_ref))
    w1, w2 = w_ref[:, 0:1], w_ref[:, 1:2]
    o_ref[:, :half] = x_ref[:, :half] + w1 * hi1 + w2 * hi2
    o_ref[:, half:] = x_ref[:, half:] + w1 * lo1 + w2 * lo2


def _moe_combine(x2d, y1, y2, wcol, *, tm):
    t, d = x2d.shape
    chunk_spec = pl.BlockSpec((ROW_CHUNKS, tm, 128), lambda i: (0, i, 0))
    return pl.pallas_call(
        _moe_combine_kernel,
        grid=(t // tm,),
        in_specs=[pl.BlockSpec((tm, d), lambda i: (i, 0)), chunk_spec, chunk_spec,
                  pl.BlockSpec((tm, wcol.shape[1]), lambda i: (i, 0))],
        out_specs=pl.BlockSpec((tm, d), lambda i: (i, 0)),
        out_shape=jax.ShapeDtypeStruct((t, d), F32),
        compiler_params=_cparams("parallel"),
        name="moe_combine",
    )(x2d, y1, y2, wcol)


def _moe(x2d, hf_rows, eidx, wts, wg, wu, wd, *, layer):
    t = x2d.shape[0]
    tm = MOE_TM
    n_tiles = 2 * t // tm + N_EXPERTS
    plane = n_tiles * tm
    i1, i2, te, na = _moe_plan(eidx, tm=tm, n_tiles=n_tiles)
    xs = _sc_dispatch(hf_rows.reshape(ROW_CHUNKS * t, 128), i1, i2, ROW_CHUNKS * plane)
    ys = _experts(te[0, :n_tiles], na[0, :1], xs.reshape(ROW_CHUNKS, plane, 128), wg, wu, wd,
                  layer=layer, tm=tm)
    y1, y2 = _sc_collect(ys.reshape(ROW_CHUNKS * plane, 128), i1, i2)
    return _moe_combine(x2d, y1.reshape(ROW_CHUNKS, t, 128), y2.reshape(ROW_CHUNKS, t, 128), wts[:2].T, tm=512)


def _w_in_layout_kernel(w_ref, main_ref, attn_ref):
    w = w_ref[0]
    src_if = 4 * MLSTM_W
    src_a = src_if + 2 * MLSTM_HEADS
    src_g = src_a + 3 * ATTN_W
    main_ref[0, :, OFF_MQ:OFF_GU] = w[:, 0:src_if].astype(BF16)
    main_ref[0, :, OFF_GU:OFF_IF] = w[:, src_g:src_g + OFF_IF - OFF_GU].astype(BF16)
    first = w[:, src_if:src_if + 128]
    lane = lax.broadcasted_iota(jnp.int32, first.shape, 1)
    main_ref[0, :, OFF_IF:OFF_IF + 128] = jnp.where(lane < 2 * MLSTM_HEADS, first, 0.0).astype(BF16)
    main_ref[0, :, OFF_IF + 128:N_PROJ] = jnp.zeros((w.shape[0], IF_PAD - 128), BF16)
    for g in range(len(ATTN_PATTERNS)):
        for j in range(3):
            src = src_a + j * ATTN_W + g * ATTN_GW
            dst = (3 * g + j) * ATTN_GW
            attn_ref[0, :, dst:dst + ATTN_GW] = w[:, src:src + ATTN_GW].astype(BF16)


def _w_in_layout(w_in, *, rows=256):
    depth, d, n_in = w_in.shape
    n_attn = 3 * ATTN_W
    return pl.pallas_call(
        _w_in_layout_kernel,
        grid=(depth, d // rows),
        in_specs=[pl.BlockSpec((1, rows, n_in), lambda l, i: (l, i, 0))],
        out_specs=[pl.BlockSpec((1, rows, N_PROJ), lambda l, i: (l, i, 0)),
                   pl.BlockSpec((1, rows, n_attn), lambda l, i: (l, i, 0))],
        out_shape=[jax.ShapeDtypeStruct((depth, d, N_PROJ), BF16),
                   jax.ShapeDtypeStruct((depth, d, n_attn), BF16)],
        compiler_params=_cparams("parallel", "parallel"),
        name="w_in_layout",
    )(w_in)


def kernel(x, mem, norm_mix, w_in, mlstm_conv, mlstm_gate_b, mlstm_norm, attn_qk_norm, gmlp_norm, gmlp_ws,
           gmlp_bs, w_branch_a, w_branch_b, w_branch_c, w_out, rel_bias, norm_xattn, norm_mem, w_xq, w_xkv,
           xattn_qk_norm, w_xo, norm_ffn, router_w, router_b, w_expert_gate, w_expert_up, w_expert_down):
    b, s, d = x.shape
    t = b * s
    depth = w_in.shape[0]
    x2d = x.reshape(t, d)

    biases = [_attn_bias(rel_bias, g) for g in range(len(ATTN_PATTERNS))]
    rw_t = jnp.zeros((N_EXPERT_GROUPS, 8, d), F32).at[:, :EXPERTS_PER_GROUP].set(
        router_w.T.reshape(N_EXPERT_GROUPS, EXPERTS_PER_GROUP, d)).reshape(ROUTER_ROWS, d)
    rb = jnp.full((N_EXPERT_GROUPS, 8), NEG, F32).at[:, :EXPERTS_PER_GROUP].set(
        router_b.astype(F32).reshape(N_EXPERT_GROUPS, EXPERTS_PER_GROUP)).reshape(ROUTER_ROWS, 1)
    tril = jnp.tril(jnp.ones((GMLP_CHUNK, GMLP_CHUNK), bool))
    head_of = jnp.arange(ATTN_GW) // ATTN_DH
    seg_ones = (head_of[:, None] == head_of[None, :]).astype(BF16)

    w_main, w_attn = _w_in_layout(w_in)

    for l in range(depth):
        proj, h_mix = _inproj(x2d, norm_mix[l][None], w_main, layer=l, tm=1024, tn=3200)
        gq = jnp.tile(attn_qk_norm[l, 0], HEADS_PER_GROUP)[None]
        gk = jnp.tile(attn_qk_norm[l, 1], HEADS_PER_GROUP)[None]

        gates_row = proj[:, OFF_IF:OFF_IF + 8].astype(F32).reshape(b, s, 8).transpose(0, 2, 1)
        gb_col = jnp.zeros((1, IF_PAD), F32).at[0, :8].set(mlstm_gate_b[l])
        ya = _mlstm(proj, gates_row, mlstm_conv[l], gb_col, mlstm_gate_b[l].reshape(8, 1),
                    mlstm_norm[l][None], batch=b, seq=s, blk=MLSTM_BLOCK, nsub=MLSTM_NSUB,
                    group=MLSTM_GROUP)

        ybs, lses = [], []
        for g, (_, dilation) in enumerate(ATTN_PATTERNS):
            aproj = _attnproj(h_mix, w_attn, seg_ones, gq, gk, layer=l, group=g, dilation=dilation)
            o, lse = _dattn(aproj, biases[g], seq=s, group=g, dilation=dilation)
            ybs.append(o)
            lses.append(lse)

        ws = jnp.where(tril, gmlp_ws[l], 0.0).astype(BF16)
        bsb = jnp.broadcast_to(gmlp_bs[l][:, :, None], (GMLP_GROUPS, GMLP_CHUNK, GMLP_GC)).astype(F32)
        x2d = _merge(ya, ybs, lses, proj, x2d, w_branch_a[l].astype(BF16), w_branch_b[l].astype(BF16),
                     w_branch_c[l].astype(BF16), w_out[l].astype(BF16), ws, bsb, gmlp_norm[l][None], tm=512)

        k_mem, v_mem = _memkv(mem, norm_mem[l][None], w_xkv[l].astype(BF16), xattn_qk_norm[l, 1][None])
        x2d, hf_rows, eidx, wts = _xattn(x2d, k_mem, v_mem, norm_xattn[l][None], w_xq[l].astype(BF16),
                                         xattn_qk_norm[l, 0][None], w_xo[l].astype(BF16), norm_ffn[l][None],
                                         rw_t, rb, seq=s, tm=1024)

        x2d = _moe(x2d, hf_rows, eidx, wts, w_expert_gate, w_expert_up, w_expert_down, layer=l)

    return x2d.reshape(b, s, d)
```

```python
import functools
import math

import jax
import jax.numpy as jnp
import numpy as np
from jax import lax
from jax.experimental import pallas as pl
from jax.experimental.pallas import tpu as pltpu
from jax.experimental.pallas import tpu_sc as plsc

F32 = jnp.float32
BF16 = jnp.bfloat16

EPS = 1e-6
NEG = -1e30

MLSTM_HEADS = 4
MLSTM_DH = 128
MLSTM_W = MLSTM_HEADS * MLSTM_DH
CONV_WIDTH = 4
MLSTM_BLOCK = 128
MLSTM_NSUB = 1
MLSTM_GROUP = 2

ATTN_PATTERNS = ((128, 1), (512, 4), (2048, 16))
HEADS_PER_GROUP = 4
ATTN_DH = 64
ATTN_GW = HEADS_PER_GROUP * ATTN_DH
ATTN_W = len(ATTN_PATTERNS) * ATTN_GW
ATTN_BLOCK = 128
REL_BUCKETS = 32
REL_MAX_DIST = 2048

GMLP_GROUPS = 4
GMLP_GC = 128
GMLP_W = GMLP_GROUPS * GMLP_GC
GMLP_CHUNK = 128

XATTN_HEADS = 4
XATTN_DH = 128
XATTN_W = XATTN_HEADS * XATTN_DH

N_EXPERTS = 16
N_EXPERT_GROUPS = 4
EXPERTS_PER_GROUP = 4
ROUTER_ROWS = 8 * N_EXPERT_GROUPS

N_BRANCH = 3

MOE_TM = 512
ROW_CHUNKS = 4
SC_CORES, SC_SUBCORES = 2, 16
SC_WINDOW = 128

OFF_MQ, OFF_MK, OFF_MV, OFF_MO = 0, 512, 1024, 1536
OFF_GU, OFF_GV = 2048, 2560
OFF_GATE = 3072
OFF_IF = 6144
IF_PAD = 256
N_PROJ = OFF_IF + IF_PAD

ATTN_TILE = 2048
ATTN_SUB = ATTN_TILE // ATTN_BLOCK
ATTN_SLAB = 2 * ATTN_DH
ATTN_COLS = HEADS_PER_GROUP * ATTN_SLAB + 2 * ATTN_GW

VMEM_LIMIT = 48 * 1024 * 1024
VMEM_LIMIT_INPROJ = 56 * 1024 * 1024
VMEM_LIMIT_SMALL = 24 * 1024 * 1024


def _cparams(*sem, vmem=VMEM_LIMIT):
    return pltpu.CompilerParams(dimension_semantics=sem, vmem_limit_bytes=vmem)


def _rms(x, gain):
    return x * lax.rsqrt(jnp.mean(x * x, axis=-1, keepdims=True) + EPS) * gain


def _sigmoid(x):
    return 0.5 * jnp.tanh(0.5 * x) + 0.5


def _dot(a, b):
    return jnp.dot(a, b, preferred_element_type=F32)


def _dot_nt(a, b):
    return lax.dot_general(a, b, (((1,), (1,)), ((), ())), preferred_element_type=F32)


def _inproj_kernel(x_ref, g_ref, w_ref, o_ref, h_ref):
    @pl.when(pl.program_id(1) == 0)
    def _():
        h_ref[...] = _rms(x_ref[...], g_ref[...]).astype(BF16)

    o_ref[...] = _dot(h_ref[...], w_ref[0]).astype(o_ref.dtype)


def _inproj(x2d, gain, w, *, layer, tm, tn):
    t, d = x2d.shape
    n = w.shape[2]
    return pl.pallas_call(
        _inproj_kernel,
        grid=(t // tm, n // tn),
        in_specs=[pl.BlockSpec((tm, d), lambda i, j: (i, 0)),
                  pl.BlockSpec((1, d), lambda i, j: (0, 0)),
                  pl.BlockSpec((1, d, tn), lambda i, j: (layer, 0, j))],
        out_specs=[pl.BlockSpec((tm, tn), lambda i, j: (i, j)),
                   pl.BlockSpec((tm, d), lambda i, j: (i, 0))],
        out_shape=[jax.ShapeDtypeStruct((t, n), BF16), jax.ShapeDtypeStruct((t, d), BF16)],
        compiler_params=_cparams("parallel", "arbitrary", vmem=VMEM_LIMIT_INPROJ),
        name="inproj",
    )(x2d, gain, w)


def _log_sigmoid(x):
    return jnp.minimum(x, 0.0) - jnp.log(1.0 + jnp.exp(-jnp.abs(x)))


def _mlstm_kernel(qk_ref, v_ref, og_ref, gc_ref, gr_ref, cw_ref, gbc_ref, gbr_ref, ng_ref, y_ref,
                  xe_scr, s_scr, m_scr, *, blk, nsub, group):
    heads, w = MLSTM_HEADS, MLSTM_W

    @pl.when(pl.program_id(1) == 0)
    def _():
        xe_scr[:, 0:8, :] = jnp.zeros((group, 8, 2 * w), F32)
        s_scr[...] = jnp.zeros_like(s_scr)
        m_scr[...] = jnp.zeros_like(m_scr)

    cw = cw_ref[...]
    ri = lax.broadcasted_iota(jnp.int32, (blk, blk), 0)
    ci = lax.broadcasted_iota(jnp.int32, (blk, blk), 1)
    causal = ri >= ci
    tril = causal.astype(BF16)
    triu = (ri <= ci).astype(BF16)
    states = []
    for g in range(group):
        xe_scr[g, 8:8 + nsub * blk, :] = qk_ref[g].astype(F32)
        states.append([(s_scr[g, h], m_scr[g, h:h + 1, 0:1]) for h in range(heads)])
    for c in range(nsub):
        for g in range(group):
            states[g] = _mlstm_chunk(c * blk, blk, states[g], cw, causal, tril, triu, xe_scr.at[g], v_ref.at[g],
                                     og_ref.at[g], gc_ref.at[g], gr_ref.at[g], gbc_ref, gbr_ref, ng_ref,
                                     y_ref.at[g])
    for g in range(group):
        xe_scr[g, 0:8, :] = xe_scr[g, nsub * blk:nsub * blk + 8, :]
        for h, (s_st, m_st) in enumerate(states[g]):
            s_scr[g, h] = s_st
            m_scr[g, h:h + 1, :] = jnp.broadcast_to(m_st, (1, m_scr.shape[2]))


def _split_bf16(x):
    hi = x.astype(BF16)
    return hi, (x - hi.astype(F32)).astype(BF16)


def _mlstm_chunk(r0, blk, state, cw, causal, tril, triu, xe_scr, v_ref, og_ref, gc_ref, gr_ref, gbc_ref,
                 gbr_ref, ng_ref, y_ref):
    heads, dh, w = MLSTM_HEADS, MLSTM_DH, MLSTM_W
    rows = slice(r0, r0 + blk)
    conv = cw[CONV_WIDTH - 1:CONV_WIDTH, :] * xe_scr[8 + r0:8 + r0 + blk, :]
    for j in range(CONV_WIDTH - 1):
        off = 8 + r0 - (CONV_WIDTH - 1) + j
        conv = conv + cw[j:j + 1, :] * xe_scr[off:off + blk, :]
    qk = conv * _sigmoid(conv)

    gcol = gc_ref[rows, :].astype(F32) + gbc_ref[...]
    grow = gr_ref[:, rows] + gbr_ref[...]
    lc_hi, lc_lo = _split_bf16(_log_sigmoid(gcol))
    lr_hi, lr_lo = _split_bf16(_log_sigmoid(grow))
    bcol = _dot(tril, lc_hi) + _dot(tril, lc_lo)
    brow = _dot(lr_hi, triu) + _dot(lr_lo, triu)
    ones = jnp.ones((blk, dh), BF16)

    new_state = []
    for h in range(heads):
        sl = slice(h * dh, (h + 1) * dh)
        b_c = bcol[:, heads + h:heads + h + 1]
        i_c = gcol[:, h:h + 1]
        b_r = brow[heads + h:heads + h + 1, :]
        i_r = grow[h:h + 1, :]
        s_st, m_st = state[h]

        d_mat = jnp.where(causal, b_c - b_r + i_r, NEG)
        inter = b_c + m_st
        m_t = jnp.maximum(inter, jnp.max(d_mat, axis=-1, keepdims=True))
        w_intra = jnp.exp(d_mat - m_t)
        w_inter = jnp.exp(inter - m_t)

        q_f = qk[:, sl]
        k_f = qk[:, w + h * dh:w + (h + 1) * dh] * (dh ** -0.5)
        q_b = q_f.astype(BF16)
        k_b = k_f.astype(BF16)
        v_ext = jnp.concatenate([v_ref[rows, sl], ones], axis=-1)

        s = _dot_nt(q_b, k_b) * w_intra
        tot = _dot(s.astype(BF16), v_ext) + w_inter * _dot(q_b, s_st.astype(BF16))
        num, den = tot[:, :dh], tot[:, dh:]
        hh = num / jnp.maximum(jnp.abs(den), jnp.exp(-m_t))
        hn = _rms(hh, ng_ref[:, sl])
        y_ref[rows, sl] = (hn * _sigmoid(og_ref[rows, sl].astype(F32))).astype(y_ref.dtype)

        b_last = b_c[blk - 1:blk, :]
        dec = b_last - b_c + i_c
        m_new = jnp.maximum(b_last + m_st, jnp.max(dec, axis=0, keepdims=True))
        w_k = jnp.exp(dec - m_new)
        w_c = jnp.exp(b_last + m_st - m_new)
        kw = k_f * w_k
        new_state.append((w_c * s_st + _dot(kw.T.astype(BF16), v_ext), m_new))
    return new_state


def _mlstm(proj, gates_row, conv_w, gb_col, gb_row, norm_g, *, batch, seq, blk, nsub, group):
    t, npj = proj.shape
    rows = blk * nsub
    w = MLSTM_W
    proj3 = proj.reshape(batch, seq, npj)
    cols = lambda c: (lambda b, i: (b, i, c))
    const2 = lambda b, i: (0, 0)
    y = pl.pallas_call(
        functools.partial(_mlstm_kernel, blk=blk, nsub=nsub, group=group),
        grid=(batch // group, seq // rows),
        in_specs=[pl.BlockSpec((group, rows, 2 * w), cols(OFF_MQ // (2 * w))),
                  pl.BlockSpec((group, rows, w), cols(OFF_MV // w)),
                  pl.BlockSpec((group, rows, w), cols(OFF_MO // w)),
                  pl.BlockSpec((group, rows, IF_PAD), cols(OFF_IF // IF_PAD)),
                  pl.BlockSpec((group, 8, rows), lambda b, i: (b, 0, i)),
                  pl.BlockSpec((CONV_WIDTH, 2 * w), const2),
                  pl.BlockSpec((1, IF_PAD), const2),
                  pl.BlockSpec((8, 1), const2),
                  pl.BlockSpec((1, w), const2)],
        out_specs=pl.BlockSpec((group, rows, w), cols(0)),
        out_shape=jax.ShapeDtypeStruct((batch, seq, w), BF16),
        scratch_shapes=[pltpu.VMEM((group, rows + 8, 2 * w), F32),
                        pltpu.VMEM((group, MLSTM_HEADS, MLSTM_DH, 2 * MLSTM_DH), F32),
                        pltpu.VMEM((group, 8, 128), F32)],
        compiler_params=_cparams("parallel", "arbitrary"),
        name="mlstm",
    )(proj3, proj3, proj3, proj3, gates_row, conv_w, gb_col, gb_row, norm_g)
    return y.reshape(t, w)


def _attnproj_kernel(h_ref, w_ref, seg_ref, gq_ref, gk_ref, o_ref, r_scr, *, dil):
    gw, half = ATTN_GW, ATTN_SLAB // 2
    sub_rows = r_scr.shape[2]
    seg, sub_seg = ATTN_TILE // dil, sub_rows // dil

    def head_norm(x, gain):
        sq = x * x
        hi = sq.astype(BF16)
        lo = (sq - hi.astype(F32)).astype(BF16)
        ss = _dot(hi, seg_ref[...]) + _dot(lo, seg_ref[...])
        return x * lax.rsqrt(ss * (1.0 / ATTN_DH) + EPS) * gain

    low = lax.broadcasted_iota(jnp.int32, (1, ATTN_SLAB), 1) < half
    for s in range(ATTN_TILE // sub_rows):
        rows = slice(s * sub_rows, (s + 1) * sub_rows)
        res = _dot(h_ref[rows, :], w_ref[0])
        q = head_norm(res[:, :gw], gq_ref[...]) * (ATTN_DH ** -0.5)
        k = head_norm(res[:, gw:2 * gw], gk_ref[...])
        slabs = []
        for pair in range(gw // ATTN_SLAB):
            qp = q[:, pair * ATTN_SLAB:(pair + 1) * ATTN_SLAB]
            slabs += [jnp.where(low, qp, 0.0), jnp.where(low, 0.0, qp)]
        slabs += [k[:, c * 128:(c + 1) * 128] for c in range(gw // 128)]
        slabs += [res[:, 2 * gw + c * 128:2 * gw + (c + 1) * 128] for c in range(gw // 128)]
        for c, slab in enumerate(slabs):
            if dil == 1:
                o_ref[rows, c * 128:(c + 1) * 128] = slab.astype(o_ref.dtype)
            else:
                r_scr[s % 2, c] = slab
        if dil > 1:
            for r in range(dil):
                dst = slice(r * seg + s * sub_seg, r * seg + (s + 1) * sub_seg)
                for c in range(r_scr.shape[1]):
                    o_ref[dst, c * 128:(c + 1) * 128] = (
                        r_scr[s % 2, c, pl.ds(r, sub_seg, stride=dil), :].astype(o_ref.dtype))


def _attnproj(h, w, seg_ones, gq, gk, *, layer, group, dilation):
    t, d = h.shape
    wcols = 3 * ATTN_GW
    const2 = lambda i: (0, 0)
    return pl.pallas_call(
        functools.partial(_attnproj_kernel, dil=dilation),
        grid=(t // ATTN_TILE,),
        in_specs=[pl.BlockSpec((ATTN_TILE, d), lambda i: (i, 0)),
                  pl.BlockSpec((1, d, wcols), lambda i: (layer, 0, group)),
                  pl.BlockSpec((ATTN_GW, ATTN_GW), const2),
                  pl.BlockSpec((1, ATTN_GW), const2), pl.BlockSpec((1, ATTN_GW), const2)],
        out_specs=pl.BlockSpec((ATTN_TILE, ATTN_COLS), lambda i: (i, 0)),
        out_shape=jax.ShapeDtypeStruct((t, ATTN_COLS), BF16),
        scratch_shapes=[pltpu.VMEM((2, ATTN_COLS // 128, 512, 128), F32)],
        compiler_params=_cparams("parallel"),
        name=f"attnproj{group}",
    )(h, w, seg_ones, gq, gk)


def _dattn_kernel(q_ref, kc_ref, kp_ref, vc_ref, vp_ref, bias_ref, o_ref, lse_ref,
                  kx_scr, vx_scr, o_scr, l_scr, *, dil):
    blk = ATTN_BLOCK
    per = ATTN_SUB // dil
    first_tile = pl.program_id(1) == 0
    for r in range(dil):
        base = r * (per + 1) * blk
        last = slice((r * per + per - 1) * blk, (r * per + per) * blk)
        mine = slice(r * per * blk, (r + 1) * per * blk)
        kx_scr[base:base + blk, :] = kp_ref[last, :]
        vx_scr[base:base + blk, :] = vp_ref[last, :]
        kx_scr[base + blk:base + (per + 1) * blk, :] = kc_ref[mine, :]
        vx_scr[base + blk:base + (per + 1) * blk, :] = vc_ref[mine, :]

    low = lax.broadcasted_iota(jnp.int32, (1, ATTN_SLAB), 1) < ATTN_SLAB // 2
    no_prev = lax.broadcasted_iota(jnp.int32, (1, 2 * blk), 1) < blk
    for r in range(dil):
        for sub in range(per):
            u = r * per + sub
            win = slice((r * (per + 1) + sub) * blk, (r * (per + 1) + sub + 2) * blk)
            o_slabs, l_slabs = [], []
            for pair in range(ATTN_GW // ATTN_SLAB):
                cols = slice(pair * ATTN_SLAB, (pair + 1) * ATTN_SLAB)
                kx, vx = kx_scr[win, cols], vx_scr[win, cols]
                o_pair, l_pair = [], []
                for h in (2 * pair, 2 * pair + 1):
                    logits = _dot_nt(q_ref[u * blk:(u + 1) * blk, h * ATTN_SLAB:(h + 1) * ATTN_SLAB], kx)
                    logits = logits + bias_ref[h]
                    if sub == 0:
                        logits = jnp.where(first_tile & no_prev, NEG, logits)
                    m = jnp.max(logits, axis=-1, keepdims=True)
                    p = jnp.exp(logits - m)
                    l = jnp.sum(p, axis=-1, keepdims=True)
                    o_pair.append(_dot(p.astype(BF16), vx) / l)
                    l_pair.append(m + jnp.log(l))
                o_slabs.append(jnp.where(low, o_pair[0], o_pair[1]))
                l_slabs.append(jnp.where(low, l_pair[0], l_pair[1]))
            dst = pl.ds(sub * blk * dil + r, blk, stride=dil) if dil > 1 else slice(u * blk, (u + 1) * blk)
            for c in range(ATTN_GW // ATTN_SLAB):
                o_scr[c, dst, :] = o_slabs[c]
                l_scr[c, dst, :] = l_slabs[c]
    for c in range(ATTN_GW // ATTN_SLAB):
        o_ref[:, c * ATTN_SLAB:(c + 1) * ATTN_SLAB] = o_scr[c].astype(o_ref.dtype)
        lse_ref[:, c * ATTN_SLAB:(c + 1) * ATTN_SLAB] = l_scr[c]


def _dattn(aproj, bias, *, seq, group, dilation):
    t = aproj.shape[0]
    tiles = seq // ATTN_TILE
    qw = HEADS_PER_GROUP * ATTN_SLAB
    cq, ck, cv = 0, qw // ATTN_GW, qw // ATTN_GW + 1
    blk = (ATTN_TILE, ATTN_GW)
    cur = lambda c: (lambda b, j: (b * tiles + j, c))
    prev = lambda c: (lambda b, j: (b * tiles + jnp.maximum(j - 1, 0), c))
    xrows = ATTN_TILE + dilation * ATTN_BLOCK
    return pl.pallas_call(
        functools.partial(_dattn_kernel, dil=dilation),
        grid=(t // seq, tiles),
        in_specs=[pl.BlockSpec((ATTN_TILE, qw), cur(cq)),
                  pl.BlockSpec(blk, cur(ck)), pl.BlockSpec(blk, prev(ck)),
                  pl.BlockSpec(blk, cur(cv)), pl.BlockSpec(blk, prev(cv)),
                  pl.BlockSpec((HEADS_PER_GROUP, ATTN_BLOCK, 2 * ATTN_BLOCK), lambda b, j: (0, 0, 0))],
        out_specs=[pl.BlockSpec(blk, cur(0)), pl.BlockSpec(blk, cur(0))],
        out_shape=[jax.ShapeDtypeStruct((t, ATTN_GW), BF16), jax.ShapeDtypeStruct((t, ATTN_GW), F32)],
        scratch_shapes=[pltpu.VMEM((xrows, ATTN_GW), BF16), pltpu.VMEM((xrows, ATTN_GW), BF16),
                        pltpu.VMEM((ATTN_GW // ATTN_SLAB, ATTN_TILE, ATTN_SLAB), F32),
                        pltpu.VMEM((ATTN_GW // ATTN_SLAB, ATTN_TILE, ATTN_SLAB), F32)],
        compiler_params=_cparams("parallel", "arbitrary"),
        name=f"dattn{group}",
    )(aproj, aproj, aproj, aproj, aproj, bias)


def _rel_bucket(n):
    max_exact = REL_BUCKETS // 2
    nf = jnp.maximum(n, 1).astype(F32)
    log_b = max_exact + (jnp.log(nf / max_exact) / math.log(REL_MAX_DIST / max_exact)
                         * (REL_BUCKETS - max_exact)).astype(jnp.int32)
    return jnp.where(n < max_exact, n, jnp.minimum(log_b, REL_BUCKETS - 1))


def _attn_bias(rel_bias, group):
    window, dilation = ATTN_PATTERNS[group]
    steps = window // dilation
    hp = lax.Precision.HIGHEST
    hs = slice(group * HEADS_PER_GROUP, (group + 1) * HEADS_PER_GROUP)
    bucket = _rel_bucket(jnp.arange(steps + 1) * dilation)
    bias_steps = jnp.dot(jax.nn.one_hot(bucket, REL_BUCKETS, dtype=F32), rel_bias[:, hs].astype(F32),
                         precision=hp)
    qi = jnp.arange(ATTN_BLOCK)[:, None]
    ki = jnp.arange(2 * ATTN_BLOCK)[None, :]
    dist = ATTN_BLOCK + qi - ki
    ok = (dist >= 0) & (dist <= steps)
    sel = jax.nn.one_hot(jnp.clip(dist, 0, steps).reshape(-1), steps + 1, dtype=F32)
    bias = jnp.dot(sel, bias_steps, precision=hp).T.reshape(HEADS_PER_GROUP, ATTN_BLOCK, 2 * ATTN_BLOCK)
    return jnp.where(ok[None], bias, NEG)


def _merge_kernel(ya_ref, yb0_ref, yb1_ref, yb2_ref, l0_ref, l1_ref, l2_ref, gu_ref, gv_ref, gate_ref,
                  x_ref, wa_ref, wb_ref, wc_ref, wo_ref, ws_ref, bs_ref, gg_ref, o_ref, yc_scr, *, tm):
    d = x_ref.shape[1]
    l0, l1, l2 = l0_ref[...], l1_ref[...], l2_ref[...]
    mx = jnp.maximum(jnp.maximum(l0, l1), l2)
    e0, e1, e2 = jnp.exp(l0 - mx), jnp.exp(l1 - mx), jnp.exp(l2 - mx)
    inv = 1.0 / (e0 + e1 + e2)
    yb = jnp.concatenate([(yb0_ref[...].astype(F32) * (e0 * inv)).astype(BF16),
                          (yb1_ref[...].astype(F32) * (e1 * inv)).astype(BF16),
                          (yb2_ref[...].astype(F32) * (e2 * inv)).astype(BF16)], axis=-1)

    for j in range(tm // GMLP_CHUNK):
        rows = slice(j * GMLP_CHUNK, (j + 1) * GMLP_CHUNK)
        for g in range(GMLP_GROUPS):
            cols = slice(g * GMLP_GC, (g + 1) * GMLP_GC)
            u = jax.nn.gelu(gu_ref[rows, cols].astype(F32))
            v = _rms(jax.nn.gelu(gv_ref[rows, cols].astype(F32)), gg_ref[:, cols])
            mixed = _dot(ws_ref[g], v.astype(BF16)) + bs_ref[g]
            yc_scr[rows, cols] = (u * mixed).astype(BF16)

    def gate2(k):
        return jnp.tanh(0.5 * gate_ref[:, k * d:(k + 1) * d].astype(F32)) + 1.0

    merged2 = gate2(0) * _dot(ya_ref[...], wa_ref[...])
    merged2 = merged2 + gate2(1) * _dot(yb, wb_ref[...])
    merged2 = merged2 + gate2(2) * _dot(yc_scr[...], wc_ref[...])
    o_ref[...] = x_ref[...] + 0.5 * _dot(merged2.astype(BF16), wo_ref[...])


def _merge(ya, ybs, lses, proj, x2d, wa, wb, wc, wo, ws, bsb, gg, *, tm):
    t, d = x2d.shape
    row = lambda c: (lambda i: (i, c))
    full2 = lambda i: (0, 0)
    full3 = lambda i: (0, 0, 0)
    gspec = pl.BlockSpec((tm, ATTN_GW), row(0))
    return pl.pallas_call(
        functools.partial(_merge_kernel, tm=tm),
        grid=(t // tm,),
        in_specs=[pl.BlockSpec((tm, MLSTM_W), row(0)),
                  gspec, gspec, gspec, gspec, gspec, gspec,
                  pl.BlockSpec((tm, GMLP_W), row(OFF_GU // GMLP_W)),
                  pl.BlockSpec((tm, GMLP_W), row(OFF_GV // GMLP_W)),
                  pl.BlockSpec((tm, N_BRANCH * d), row(OFF_GATE // (N_BRANCH * d))),
                  pl.BlockSpec((tm, d), row(0)),
                  pl.BlockSpec(wa.shape, full2), pl.BlockSpec(wb.shape, full2),
                  pl.BlockSpec(wc.shape, full2), pl.BlockSpec(wo.shape, full2),
                  pl.BlockSpec(ws.shape, full3), pl.BlockSpec(bsb.shape, full3),
                  pl.BlockSpec(gg.shape, full2)],
        out_specs=pl.BlockSpec((tm, d), row(0)),
        out_shape=jax.ShapeDtypeStruct((t, d), F32),
        scratch_shapes=[pltpu.VMEM((tm, GMLP_W), BF16)],
        compiler_params=_cparams("parallel"),
        name="merge",
    )(ya, *ybs, *lses, proj, proj, proj, x2d, wa, wb, wc, wo, ws, bsb, gg)


def _memkv_kernel(mem_ref, g_ref, w_ref, gk_ref, k_ref, v_ref):
    dh, w = XATTN_DH, XATTN_W
    kv = _dot(_rms(mem_ref[0], g_ref[...]).astype(BF16), w_ref[...])
    for h in range(XATTN_HEADS):
        sl = slice(h * dh, (h + 1) * dh)
        k_ref[0, :, sl] = _rms(kv[:, sl], gk_ref[...]).astype(k_ref.dtype)
    v_ref[0] = kv[:, w:].astype(v_ref.dtype)


def _memkv(mem, gain, w_kv, gk):
    b, m, d = mem.shape
    full2 = lambda i: (0, 0)
    return pl.pallas_call(
        _memkv_kernel,
        grid=(b,),
        in_specs=[pl.BlockSpec((1, m, d), lambda i: (i, 0, 0)),
                  pl.BlockSpec((1, d), full2),
                  pl.BlockSpec(w_kv.shape, full2),
                  pl.BlockSpec((1, XATTN_DH), full2)],
        out_specs=[pl.BlockSpec((1, m, XATTN_W), lambda i: (i, 0, 0)),
                   pl.BlockSpec((1, m, XATTN_W), lambda i: (i, 0, 0))],
        out_shape=[jax.ShapeDtypeStruct((b, m, XATTN_W), BF16),
                   jax.ShapeDtypeStruct((b, m, XATTN_W), BF16)],
        compiler_params=_cparams("parallel", vmem=VMEM_LIMIT_SMALL),
        name="memkv",
    )(mem, gain, w_kv, gk)


def _route(logits):
    tm = logits.shape[1]
    e = jnp.exp(logits - jnp.max(logits, axis=0, keepdims=True))
    probs = e / jnp.sum(e, axis=0, keepdims=True)
    rowi = lax.broadcasted_iota(jnp.int32, (8, tm), 0)
    real = rowi < EXPERTS_PER_GROUP
    tops = []
    for g in range(N_EXPERT_GROUPS):
        pg = jnp.where(real, probs[8 * g:8 * g + 8, :], -0.5)
        m1 = jnp.max(pg, axis=0, keepdims=True)
        i1 = jnp.min(jnp.where(pg == m1, rowi, 8), axis=0, keepdims=True)
        pg2 = jnp.where(rowi == i1, -1.0, pg)
        m2 = jnp.max(pg2, axis=0, keepdims=True)
        i2 = jnp.min(jnp.where(pg2 == m2, rowi, 8), axis=0, keepdims=True)
        tops.append((m1, i1, m2, i2))
    best = jnp.zeros((1, tm), jnp.int32)
    best_score = tops[0][0] + tops[0][2]
    for g in range(1, N_EXPERT_GROUPS):
        score = tops[g][0] + tops[g][2]
        better = score > best_score
        best = jnp.where(better, g, best)
        best_score = jnp.where(better, score, best_score)
    m1, i1, m2, i2 = tops[0]
    for g in range(1, N_EXPERT_GROUPS):
        m1, i1, m2, i2 = (jnp.where(best == g, new, old) for new, old in zip(tops[g], (m1, i1, m2, i2)))
    tot = m1 + m2
    base = best * EXPERTS_PER_GROUP
    return base + i1, base + i2, m1 / tot, m2 / tot


def _pack_bf16_pairs(x):
    n = x.shape[1] // 2
    hi = lax.bitcast_convert_type(x[:, :n].astype(BF16).astype(F32), jnp.uint32)
    lo = lax.bitcast_convert_type(x[:, n:].astype(BF16).astype(F32), jnp.uint32)
    return hi | (lo >> 16)


def _unpack_bf16_pairs(p):
    hi = lax.bitcast_convert_type(p & jnp.uint32(0xFFFF0000), F32)
    lo = lax.bitcast_convert_type(p << 16, F32)
    return hi, lo


def _store_row_chunks(ref, packed):
    for j in range(ROW_CHUNKS):
        ref[j] = packed[:, j * 128:(j + 1) * 128]


def _load_row_chunks(ref):
    return jnp.concatenate([ref[j] for j in range(ROW_CHUNKS)], axis=-1)


def _xattn_kernel(x_ref, k_ref, v_ref, gx_ref, wq_ref, gq_ref, wo_ref, gf_ref, rw_ref, rb_ref,
                  xo_ref, hf_ref, eidx_ref, wts_ref):
    dh = XATTN_DH
    x = x_ref[...]
    q = _dot(_rms(x, gx_ref[...]).astype(BF16), wq_ref[...])
    outs = []
    for h in range(XATTN_HEADS):
        sl = slice(h * dh, (h + 1) * dh)
        q_h = (_rms(q[:, sl], gq_ref[...]) * (dh ** -0.5)).astype(BF16)
        logits = _dot_nt(q_h, k_ref[0, :, sl])
        p = jnp.exp(logits - jnp.max(logits, axis=-1, keepdims=True))
        o = _dot(p.astype(BF16), v_ref[0, :, sl]) / jnp.sum(p, axis=-1, keepdims=True)
        outs.append(o.astype(BF16))
    xn = x + _dot(jnp.concatenate(outs, axis=-1), wo_ref[...])
    xo_ref[...] = xn
    hf = _rms(xn, gf_ref[...])
    _store_row_chunks(hf_ref, _pack_bf16_pairs(hf))
    rw = rw_ref[...]
    rw_hi = rw.astype(BF16)
    rw_lo = (rw - rw_hi.astype(F32)).astype(BF16)
    hf_hi = hf.astype(BF16)
    hf_lo = (hf - hf_hi.astype(F32)).astype(BF16)
    logits_t = _dot_nt(rw_hi, hf_hi) + _dot_nt(rw_hi, hf_lo) + _dot_nt(rw_lo, hf_hi) + rb_ref[...]
    e1, e2, w1, w2 = _route(logits_t)
    tm = x.shape[0]
    eidx_ref[...] = jnp.concatenate([e1, e2, jnp.zeros((6, tm), jnp.int32)], axis=0)
    wts_ref[...] = jnp.concatenate([w1, w2, jnp.zeros((6, tm), F32)], axis=0)


def _xattn(x2d, k, v, gx, wq, gq, wo, gf, rw_t, rb, *, seq, tm):
    t, d = x2d.shape
    per_b = seq // tm
    full2 = lambda i: (0, 0)
    kv_spec = pl.BlockSpec((1,) + k.shape[1:], lambda i: (i // per_b, 0, 0))
    return pl.pallas_call(
        _xattn_kernel,
        grid=(t // tm,),
        in_specs=[pl.BlockSpec((tm, d), lambda i: (i, 0)), kv_spec, kv_spec,
                  pl.BlockSpec((1, d), full2), pl.BlockSpec(wq.shape, full2),
                  pl.BlockSpec((1, XATTN_DH), full2), pl.BlockSpec(wo.shape, full2),
                  pl.BlockSpec((1, d), full2), pl.BlockSpec(rw_t.shape, full2),
                  pl.BlockSpec(rb.shape, full2)],
        out_specs=[pl.BlockSpec((tm, d), lambda i: (i, 0)),
                   pl.BlockSpec((ROW_CHUNKS, tm, 128), lambda i: (0, i, 0)),
                   pl.BlockSpec((8, tm), lambda i: (0, i)),
                   pl.BlockSpec((8, tm), lambda i: (0, i))],
        out_shape=[jax.ShapeDtypeStruct((t, d), F32),
                   jax.ShapeDtypeStruct((ROW_CHUNKS, t, 128), jnp.uint32),
                   jax.ShapeDtypeStruct((8, t), jnp.int32),
                   jax.ShapeDtypeStruct((8, t), F32)],
        compiler_params=_cparams("parallel"),
        name="xattn_router",
    )(x2d, k, v, gx, wq, gq, wo, gf, rw_t, rb)


def _moe_plan_kernel(eidx_ref, i1_ref, i2_ref, te_ref, na_ref, cnt_scr, carry_scr, *, tb, tm, plane_rows):
    ne = N_EXPERTS
    hp = lax.Precision.HIGHEST
    phase, j = pl.program_id(0), pl.program_id(1)
    rows = lax.broadcasted_iota(jnp.int32, (ne, tb), 0)
    oh1 = rows == eidx_ref[0:1, :]
    oh2 = rows == eidx_ref[1:2, :]
    a = oh1.astype(F32) + oh2.astype(F32)
    blk_cnt = jnp.broadcast_to(jnp.sum(a, axis=1, keepdims=True), cnt_scr.shape)

    @pl.when((phase == 0) & (j == 0))
    def _():
        cnt_scr[...] = jnp.zeros_like(cnt_scr)

    @pl.when(phase == 0)
    def _():
        cnt_scr[...] += blk_cnt

    @pl.when((phase == 1) & (j == 0))
    def _():
        padded = jnp.ceil(cnt_scr[...] * (1.0 / tm)) * tm
        er = lax.broadcasted_iota(jnp.int32, (ne, ne), 0)
        ec = lax.broadcasted_iota(jnp.int32, (ne, ne), 1)
        off = jnp.dot((ec < er).astype(F32), padded, precision=hp, preferred_element_type=F32)
        carry_scr[...] = off
        seg_end = (off + padded)[:, 0:1]
        tile_start = lax.broadcasted_iota(jnp.int32, (ne, te_ref.shape[1]), 1).astype(F32) * tm
        te = jnp.sum((seg_end <= tile_start).astype(F32), axis=0, keepdims=True)
        te_ref[...] = jnp.broadcast_to(jnp.minimum(te, ne - 1.0), te_ref.shape).astype(jnp.int32)
        total = jnp.sum(padded[:, 0:1], axis=0, keepdims=True)
        na_ref[...] = jnp.broadcast_to(total * (1.0 / tm), na_ref.shape).astype(jnp.int32)

    @pl.when(phase == 1)
    def _():
        before = (lax.broadcasted_iota(jnp.int32, (tb, tb), 0)
                  < lax.broadcasted_iota(jnp.int32, (tb, tb), 1)).astype(BF16)
        rank = carry_scr[:, 0:1] + _dot(a.astype(BF16), before)
        d1 = jnp.sum(jnp.where(oh1, rank, 0.0), axis=0, keepdims=True).astype(jnp.int32)
        d2 = jnp.sum(jnp.where(oh2, rank, 0.0), axis=0, keepdims=True).astype(jnp.int32)
        plane = lax.broadcasted_iota(jnp.int32, (8, tb), 0) * plane_rows
        i1_ref[...] = jnp.where(plane < ROW_CHUNKS * plane_rows, plane + d1, 0)
        i2_ref[...] = jnp.where(plane < ROW_CHUNKS * plane_rows, plane + d2, 0)
        carry_scr[...] += blk_cnt


def _moe_plan(eidx, *, tm, n_tiles, tb=512):
    t = eidx.shape[1]
    ntp = -(-n_tiles // 128) * 128
    return pl.pallas_call(
        functools.partial(_moe_plan_kernel, tb=tb, tm=tm, plane_rows=n_tiles * tm),
        grid=(2, t // tb),
        in_specs=[pl.BlockSpec((8, tb), lambda p, j: (0, j))],
        out_specs=[pl.BlockSpec((8, tb), lambda p, j: (0, j * p)),
                   pl.BlockSpec((8, tb), lambda p, j: (0, j * p)),
                   pl.BlockSpec((8, ntp), lambda p, j: (0, 0)),
                   pl.BlockSpec((8, 128), lambda p, j: (0, 0))],
        out_shape=[jax.ShapeDtypeStruct((8, t), jnp.int32),
                   jax.ShapeDtypeStruct((8, t), jnp.int32),
                   jax.ShapeDtypeStruct((8, ntp), jnp.int32),
                   jax.ShapeDtypeStruct((8, 128), jnp.int32)],
        scratch_shapes=[pltpu.VMEM((N_EXPERTS, 128), F32), pltpu.VMEM((N_EXPERTS, 128), F32)],
        compiler_params=_cparams("arbitrary", "arbitrary", vmem=VMEM_LIMIT_SMALL),
        name="moe_plan",
    )(eidx)


def _sc_mesh():
    return plsc.VectorSubcoreMesh(core_axis_name="c", subcore_axis_name="s",
                                  num_cores=SC_CORES, num_subcores=SC_SUBCORES)


def _sc_index_spec(tokens):
    nb = tokens // SC_WINDOW
    return pl.BlockSpec((1, SC_WINDOW), lambda i: (i // nb, i % nb))


def _sc_dispatch(rows, i1, i2, n_out):
    n = rows.shape[0]
    tokens = i1.shape[1]

    @functools.partial(pl.kernel, out_type=jax.ShapeDtypeStruct((n_out, 128), rows.dtype), mesh=_sc_mesh(),
                       name="moe_dispatch")
    def k(x_hbm, i1_hbm, i2_hbm, o_hbm):
        def body(x_vmem, i1_vmem, i2_vmem):
            pltpu.sync_copy(x_vmem, o_hbm.at[i1_vmem.at[0]])
            pltpu.sync_copy(x_vmem, o_hbm.at[i2_vmem.at[0]])

        pltpu.emit_pipeline(
            body, grid=(n // SC_WINDOW,),
            in_specs=[pl.BlockSpec((SC_WINDOW, 128), lambda i: (i, 0)),
                      _sc_index_spec(tokens), _sc_index_spec(tokens)],
            out_specs=[],
            core_axis_name=("c", "s"), dimension_semantics=(pltpu.PARALLEL,),
        )(x_hbm, i1_hbm, i2_hbm)

    return k(rows, i1, i2)


def _sc_collect(table, i1, i2):
    tokens = i1.shape[1]
    n = ROW_CHUNKS * tokens
    out = jax.ShapeDtypeStruct((n, 128), table.dtype)

    @functools.partial(pl.kernel, out_type=(out, out), mesh=_sc_mesh(), name="moe_collect")
    def k(t_hbm, i1_hbm, i2_hbm, o1_hbm, o2_hbm):
        def body(i1_vmem, i2_vmem, o1_vmem, o2_vmem):
            pltpu.sync_copy(t_hbm.at[i1_vmem.at[0]], o1_vmem)
            pltpu.sync_copy(t_hbm.at[i2_vmem.at[0]], o2_vmem)

        pltpu.emit_pipeline(
            body, grid=(n // SC_WINDOW,),
            in_specs=[_sc_index_spec(tokens), _sc_index_spec(tokens)],
            out_specs=[pl.BlockSpec((SC_WINDOW, 128), lambda i: (i, 0)),
                       pl.BlockSpec((SC_WINDOW, 128), lambda i: (i, 0))],
            core_axis_name=("c", "s"), dimension_semantics=(pltpu.PARALLEL,),
        )(i1_hbm, i2_hbm, o1_hbm, o2_hbm)

    return k(table, i1, i2)


def _experts_kernel(te_ref, na_ref, xs_ref, wg_ref, wu_ref, wd_ref, y_ref, wg_scr, wu_scr, wd_scr):
    i = pl.program_id(0)
    active = i < na_ref[0]

    @pl.when(active & ((i == 0) | (te_ref[i] != te_ref[jnp.maximum(i - 1, 0)])))
    def _():
        wg_scr[...] = wg_ref[0, 0].astype(BF16)
        wu_scr[...] = wu_ref[0, 0].astype(BF16)
        wd_scr[...] = wd_ref[0, 0].astype(BF16)

    @pl.when(active)
    def _():
        hi, lo = _unpack_bf16_pairs(_load_row_chunks(xs_ref))
        h = jnp.concatenate([hi, lo], axis=-1).astype(BF16)
        up = _dot(h, wg_scr[...])
        act = up * _sigmoid(up) * _dot(h, wu_scr[...])
        _store_row_chunks(y_ref, _pack_bf16_pairs(_dot(act.astype(BF16), wd_scr[...])))


def _experts(tile_expert, n_active, xs, wg, wu, wd, *, layer, tm):
    n_tiles = tile_expert.shape[0]
    _, _, d, dff = wg.shape
    rows = lambda i, te, na: (0, jnp.minimum(i, na[0] - 1), 0)
    expert = lambda i, te, na: (layer, te[i], 0, 0)
    return pl.pallas_call(
        _experts_kernel,
        grid_spec=pltpu.PrefetchScalarGridSpec(
            num_scalar_prefetch=2,
            grid=(n_tiles,),
            in_specs=[pl.BlockSpec((ROW_CHUNKS, tm, 128), rows),
                      pl.BlockSpec((1, 1, d, dff), expert),
                      pl.BlockSpec((1, 1, d, dff), expert),
                      pl.BlockSpec((1, 1, dff, d), expert)],
            out_specs=pl.BlockSpec((ROW_CHUNKS, tm, 128), rows),
            scratch_shapes=[pltpu.VMEM((d, dff), BF16), pltpu.VMEM((d, dff), BF16), pltpu.VMEM((dff, d), BF16)]),
        out_shape=jax.ShapeDtypeStruct(xs.shape, xs.dtype),
        compiler_params=_cparams("arbitrary"),
        name="moe_experts",
    )(tile_expert, n_active, xs, wg, wu, wd)


def _moe_combine_kernel(x_ref, y1_ref, y2_ref, w_ref, o_ref):
    half = x_ref.shape[1] // 2
    hi1, lo1 = _unpack_bf16_pairs(_load_row_chunks(y1_ref))
    hi2, lo2 = _unpack_bf16_pairs(_load_row_chunks(y2_ref))
    w1, w2 = w_ref[:, 0:1], w_ref[:, 1:2]
    o_ref[:, :half] = x_ref[:, :half] + w1 * hi1 + w2 * hi2
    o_ref[:, half:] = x_ref[:, half:] + w1 * lo1 + w2 * lo2


def _moe_combine(x2d, y1, y2, wcol, *, tm):
    t, d = x2d.shape
    chunk_spec = pl.BlockSpec((ROW_CHUNKS, tm, 128), lambda i: (0, i, 0))
    return pl.pallas_call(
        _moe_combine_kernel,
        grid=(t // tm,),
        in_specs=[pl.BlockSpec((tm, d), lambda i: (i, 0)), chunk_spec, chunk_spec,
                  pl.BlockSpec((tm, wcol.shape[1]), lambda i: (i, 0))],
        out_specs=pl.BlockSpec((tm, d), lambda i: (i, 0)),
        out_shape=jax.ShapeDtypeStruct((t, d), F32),
        compiler_params=_cparams("parallel", vmem=VMEM_LIMIT_SMALL),
        name="moe_combine",
    )(x2d, y1, y2, wcol)


def _moe(x2d, hf_rows, eidx, wts, wg, wu, wd, *, layer):
    t = x2d.shape[0]
    tm = MOE_TM
    n_tiles = 2 * t // tm + N_EXPERTS
    plane = n_tiles * tm
    i1, i2, te, na = _moe_plan(eidx, tm=tm, n_tiles=n_tiles)
    xs = _sc_dispatch(hf_rows.reshape(ROW_CHUNKS * t, 128), i1, i2, ROW_CHUNKS * plane)
    ys = _experts(te[0, :n_tiles], na[0, :1], xs.reshape(ROW_CHUNKS, plane, 128), wg, wu, wd,
                  layer=layer, tm=tm)
    y1, y2 = _sc_collect(ys.reshape(ROW_CHUNKS * plane, 128), i1, i2)
    return _moe_combine(x2d, y1.reshape(ROW_CHUNKS, t, 128), y2.reshape(ROW_CHUNKS, t, 128), wts[:2].T, tm=512)


def _w_in_layout_kernel(w_ref, main_ref, attn_ref):
    w = w_ref[0]
    src_if = 4 * MLSTM_W
    src_a = src_if + 2 * MLSTM_HEADS
    src_g = src_a + 3 * ATTN_W
    main_ref[0, :, OFF_MQ:OFF_GU] = w[:, 0:src_if].astype(BF16)
    main_ref[0, :, OFF_GU:OFF_IF] = w[:, src_g:src_g + OFF_IF - OFF_GU].astype(BF16)
    first = w[:, src_if:src_if + 128]
    lane = lax.broadcasted_iota(jnp.int32, first.shape, 1)
    main_ref[0, :, OFF_IF:OFF_IF + 128] = jnp.where(lane < 2 * MLSTM_HEADS, first, 0.0).astype(BF16)
    main_ref[0, :, OFF_IF + 128:N_PROJ] = jnp.zeros((w.shape[0], IF_PAD - 128), BF16)
    for g in range(len(ATTN_PATTERNS)):
        for j in range(3):
            src = src_a + j * ATTN_W + g * ATTN_GW
            dst = (3 * g + j) * ATTN_GW
            attn_ref[0, :, dst:dst + ATTN_GW] = w[:, src:src + ATTN_GW].astype(BF16)


def _w_in_layout(w_in, *, rows=256):
    depth, d, n_in = w_in.shape
    n_attn = 3 * ATTN_W
    return pl.pallas_call(
        _w_in_layout_kernel,
        grid=(depth, d // rows),
        in_specs=[pl.BlockSpec((1, rows, n_in), lambda l, i: (l, i, 0))],
        out_specs=[pl.BlockSpec((1, rows, N_PROJ), lambda l, i: (l, i, 0)),
                   pl.BlockSpec((1, rows, n_attn), lambda l, i: (l, i, 0))],
        out_shape=[jax.ShapeDtypeStruct((depth, d, N_PROJ), BF16),
                   jax.ShapeDtypeStruct((depth, d, n_attn), BF16)],
        compiler_params=_cparams("parallel", "parallel"),
        name="w_in_layout",
    )(w_in)


def kernel(x, mem, norm_mix, w_in, mlstm_conv, mlstm_gate_b, mlstm_norm, attn_qk_norm, gmlp_norm, gmlp_ws,
           gmlp_bs, w_branch_a, w_branch_b, w_branch_c, w_out, rel_bias, norm_xattn, norm_mem, w_xq, w_xkv,
           xattn_qk_norm, w_xo, norm_ffn, router_w, router_b, w_expert_gate, w_expert_up, w_expert_down):
    b, s, d = x.shape
    t = b * s
    depth = w_in.shape[0]
    x2d = x.reshape(t, d)

    biases = [_attn_bias(rel_bias, g) for g in range(len(ATTN_PATTERNS))]
    rw_t = jnp.zeros((N_EXPERT_GROUPS, 8, d), F32).at[:, :EXPERTS_PER_GROUP].set(
        router_w.T.reshape(N_EXPERT_GROUPS, EXPERTS_PER_GROUP, d)).reshape(ROUTER_ROWS, d)
    rb = jnp.full((N_EXPERT_GROUPS, 8), NEG, F32).at[:, :EXPERTS_PER_GROUP].set(
        router_b.astype(F32).reshape(N_EXPERT_GROUPS, EXPERTS_PER_GROUP)).reshape(ROUTER_ROWS, 1)
    tril = jnp.tril(jnp.ones((GMLP_CHUNK, GMLP_CHUNK), bool))
    head_of = jnp.arange(ATTN_GW) // ATTN_DH
    seg_ones = (head_of[:, None] == head_of[None, :]).astype(BF16)

    w_main, w_attn = _w_in_layout(w_in)

    for l in range(depth):
        proj, h_mix = _inproj(x2d, norm_mix[l][None], w_main, layer=l, tm=1024, tn=3200)
        gq = jnp.tile(attn_qk_norm[l, 0], HEADS_PER_GROUP)[None]
        gk = jnp.tile(attn_qk_norm[l, 1], HEADS_PER_GROUP)[None]

        gates_row = proj[:, OFF_IF:OFF_IF + 8].astype(F32).reshape(b, s, 8).transpose(0, 2, 1)
        gb_col = jnp.zeros((1, IF_PAD), F32).at[0, :8].set(mlstm_gate_b[l])
        ya = _mlstm(proj, gates_row, mlstm_conv[l], gb_col, mlstm_gate_b[l].reshape(8, 1),
                    mlstm_norm[l][None], batch=b, seq=s, blk=MLSTM_BLOCK, nsub=MLSTM_NSUB,
                    group=MLSTM_GROUP)

        ybs, lses = [], []
        for g, (_, dilation) in enumerate(ATTN_PATTERNS):
            aproj = _attnproj(h_mix, w_attn, seg_ones, gq, gk, layer=l, group=g, dilation=dilation)
            o, lse = _dattn(aproj, biases[g], seq=s, group=g, dilation=dilation)
            ybs.append(o)
            lses.append(lse)

        ws = jnp.where(tril, gmlp_ws[l], 0.0).astype(BF16)
        bsb = jnp.broadcast_to(gmlp_bs[l][:, :, None], (GMLP_GROUPS, GMLP_CHUNK, GMLP_GC)).astype(F32)
        x2d = _merge(ya, ybs, lses, proj, x2d, w_branch_a[l].astype(BF16), w_branch_b[l].astype(BF16),
                     w_branch_c[l].astype(BF16), w_out[l].astype(BF16), ws, bsb, gmlp_norm[l][None], tm=512)

        k_mem, v_mem = _memkv(mem, norm_mem[l][None], w_xkv[l].astype(BF16), xattn_qk_norm[l, 1][None])
        x2d, hf_rows, eidx, wts = _xattn(x2d, k_mem, v_mem, norm_xattn[l][None], w_xq[l].astype(BF16),
                                         xattn_qk_norm[l, 0][None], w_xo[l].astype(BF16), norm_ffn[l][None],
                                         rw_t, rb, seq=s, tm=1024)

        x2d = _moe(x2d, hf_rows, eidx, wts, w_expert_gate, w_expert_up, w_expert_down, layer=l)

    return x2d.reshape(b, s, d)
```

```python
import functools
import math

import jax
import jax.numpy as jnp
import numpy as np
from jax import lax
from jax.experimental import pallas as pl
from jax.experimental.pallas import tpu as pltpu
from jax.experimental.pallas import tpu_sc as plsc

F32 = jnp.float32
BF16 = jnp.bfloat16

EPS = 1e-6
NEG = -1e30

MLSTM_HEADS = 4
MLSTM_DH = 128
MLSTM_W = MLSTM_HEADS * MLSTM_DH
CONV_WIDTH = 4
MLSTM_BLOCK = 128
MLSTM_NSUB = 1
MLSTM_GROUP = 2

ATTN_PATTERNS = ((128, 1), (512, 4), (2048, 16))
HEADS_PER_GROUP = 4
ATTN_DH = 64
ATTN_GW = HEADS_PER_GROUP * ATTN_DH
ATTN_W = len(ATTN_PATTERNS) * ATTN_GW
ATTN_BLOCK = 128
REL_BUCKETS = 32
REL_MAX_DIST = 2048

GMLP_GROUPS = 4
GMLP_GC = 128
GMLP_W = GMLP_GROUPS * GMLP_GC
GMLP_CHUNK = 128

XATTN_HEADS = 4
XATTN_DH = 128
XATTN_W = XATTN_HEADS * XATTN_DH

N_EXPERTS = 16
N_EXPERT_GROUPS = 4
EXPERTS_PER_GROUP = 4
ROUTER_ROWS = 8 * N_EXPERT_GROUPS

N_BRANCH = 3

MOE_TM = 1024
ROW_CHUNKS = 4
SC_CORES, SC_SUBCORES = 2, 16
SC_WINDOW = 128

OFF_MQ, OFF_MK, OFF_MV, OFF_MO = 0, 512, 1024, 1536
OFF_GU, OFF_GV = 2048, 2560
OFF_GATE = 3072
OFF_IF = 6144
IF_PAD = 256
N_PROJ = OFF_IF + IF_PAD

ATTN_TILE = 2048
ATTN_SUB = ATTN_TILE // ATTN_BLOCK
ATTN_SLAB = 2 * ATTN_DH
ATTN_COLS = HEADS_PER_GROUP * ATTN_SLAB + 2 * ATTN_GW

VMEM_LIMIT = 48 * 1024 * 1024
VMEM_LIMIT_INPROJ = 56 * 1024 * 1024
VMEM_LIMIT_SMALL = 24 * 1024 * 1024


def _cparams(*sem, vmem=VMEM_LIMIT):
    return pltpu.CompilerParams(dimension_semantics=sem, vmem_limit_bytes=vmem)


def _rms(x, gain):
    return x * lax.rsqrt(jnp.mean(x * x, axis=-1, keepdims=True) + EPS) * gain


def _sigmoid(x):
    return 0.5 * jnp.tanh(0.5 * x) + 0.5


def _dot(a, b):
    return jnp.dot(a, b, preferred_element_type=F32)


def _dot_nt(a, b):
    return lax.dot_general(a, b, (((1,), (1,)), ((), ())), preferred_element_type=F32)


def _inproj_kernel(x_ref, g_ref, w_ref, o_ref, h_ref):
    @pl.when(pl.program_id(1) == 0)
    def _():
        h_ref[...] = _rms(x_ref[...], g_ref[...]).astype(BF16)

    o_ref[...] = _dot(h_ref[...], w_ref[0]).astype(o_ref.dtype)


def _inproj(x2d, gain, w, *, layer, tm, tn):
    t, d = x2d.shape
    n = w.shape[2]
    return pl.pallas_call(
        _inproj_kernel,
        grid=(t // tm, n // tn),
        in_specs=[pl.BlockSpec((tm, d), lambda i, j: (i, 0)),
                  pl.BlockSpec((1, d), lambda i, j: (0, 0)),
                  pl.BlockSpec((1, d, tn), lambda i, j: (layer, 0, j))],
        out_specs=[pl.BlockSpec((tm, tn), lambda i, j: (i, j)),
                   pl.BlockSpec((tm, d), lambda i, j: (i, 0))],
        out_shape=[jax.ShapeDtypeStruct((t, n), BF16), jax.ShapeDtypeStruct((t, d), BF16)],
        compiler_params=_cparams("parallel", "arbitrary", vmem=VMEM_LIMIT_INPROJ),
        name="inproj",
    )(x2d, gain, w)


def _log_sigmoid(x):
    return jnp.minimum(x, 0.0) - jnp.log(1.0 + jnp.exp(-jnp.abs(x)))


def _mlstm_kernel(qk_ref, v_ref, og_ref, gc_ref, gr_ref, cw_ref, gbc_ref, gbr_ref, ng_ref, y_ref,
                  xe_scr, s_scr, m_scr, *, blk, nsub, group):
    heads, w = MLSTM_HEADS, MLSTM_W

    @pl.when(pl.program_id(1) == 0)
    def _():
        xe_scr[:, 0:8, :] = jnp.zeros((group, 8, 2 * w), F32)
        s_scr[...] = jnp.zeros_like(s_scr)
        m_scr[...] = jnp.zeros_like(m_scr)

    cw = cw_ref[...]
    ri = lax.broadcasted_iota(jnp.int32, (blk, blk), 0)
    ci = lax.broadcasted_iota(jnp.int32, (blk, blk), 1)
    causal = ri >= ci
    tril = causal.astype(BF16)
    triu = (ri <= ci).astype(BF16)
    states = []
    for g in range(group):
        xe_scr[g, 8:8 + nsub * blk, :] = qk_ref[g].astype(F32)
        states.append([(s_scr[g, h], m_scr[g, h:h + 1, 0:1]) for h in range(heads)])
    for c in range(nsub):
        for g in range(group):
            states[g] = _mlstm_chunk(c * blk, blk, states[g], cw, causal, tril, triu, xe_scr.at[g], v_ref.at[g],
                                     og_ref.at[g], gc_ref.at[g], gr_ref.at[g], gbc_ref, gbr_ref, ng_ref,
                                     y_ref.at[g])
    for g in range(group):
        xe_scr[g, 0:8, :] = xe_scr[g, nsub * blk:nsub * blk + 8, :]
        for h, (s_st, m_st) in enumerate(states[g]):
            s_scr[g, h] = s_st
            m_scr[g, h:h + 1, :] = jnp.broadcast_to(m_st, (1, m_scr.shape[2]))


def _split_bf16(x):
    hi = x.astype(BF16)
    return hi, (x - hi.astype(F32)).astype(BF16)


def _mlstm_chunk(r0, blk, state, cw, causal, tril, triu, xe_scr, v_ref, og_ref, gc_ref, gr_ref, gbc_ref,
                 gbr_ref, ng_ref, y_ref):
    heads, dh, w = MLSTM_HEADS, MLSTM_DH, MLSTM_W
    rows = slice(r0, r0 + blk)
    conv = cw[CONV_WIDTH - 1:CONV_WIDTH, :] * xe_scr[8 + r0:8 + r0 + blk, :]
    for j in range(CONV_WIDTH - 1):
        off = 8 + r0 - (CONV_WIDTH - 1) + j
        conv = conv + cw[j:j + 1, :] * xe_scr[off:off + blk, :]
    qk = conv * _sigmoid(conv)

    gcol = gc_ref[rows, :].astype(F32) + gbc_ref[...]
    grow = gr_ref[:, rows] + gbr_ref[...]
    lc_hi, lc_lo = _split_bf16(_log_sigmoid(gcol))
    lr_hi, lr_lo = _split_bf16(_log_sigmoid(grow))
    bcol = _dot(tril, lc_hi) + _dot(tril, lc_lo)
    brow = _dot(lr_hi, triu) + _dot(lr_lo, triu)
    ones = jnp.ones((blk, dh), BF16)

    new_state = []
    for h in range(heads):
        sl = slice(h * dh, (h + 1) * dh)
        b_c = bcol[:, heads + h:heads + h + 1]
        i_c = gcol[:, h:h + 1]
        b_r = brow[heads + h:heads + h + 1, :]
        i_r = grow[h:h + 1, :]
        s_st, m_st = state[h]

        d_mat = jnp.where(causal, b_c - b_r + i_r, NEG)
        inter = b_c + m_st
        m_t = jnp.maximum(inter, jnp.max(d_mat, axis=-1, keepdims=True))
        w_intra = jnp.exp(d_mat - m_t)
        w_inter = jnp.exp(inter - m_t)

        q_f = qk[:, sl]
        k_f = qk[:, w + h * dh:w + (h + 1) * dh] * (dh ** -0.5)
        q_b = q_f.astype(BF16)
        k_b = k_f.astype(BF16)
        v_ext = jnp.concatenate([v_ref[rows, sl], ones], axis=-1)

        s = _dot_nt(q_b, k_b) * w_intra
        tot = _dot(s.astype(BF16), v_ext) + w_inter * _dot(q_b, s_st.astype(BF16))
        num, den = tot[:, :dh], tot[:, dh:]
        hh = num / jnp.maximum(jnp.abs(den), jnp.exp(-m_t))
        hn = _rms(hh, ng_ref[:, sl])
        y_ref[rows, sl] = (hn * _sigmoid(og_ref[rows, sl].astype(F32))).astype(y_ref.dtype)

        b_last = b_c[blk - 1:blk, :]
        dec = b_last - b_c + i_c
        m_new = jnp.maximum(b_last + m_st, jnp.max(dec, axis=0, keepdims=True))
        w_k = jnp.exp(dec - m_new)
        w_c = jnp.exp(b_last + m_st - m_new)
        kw = k_f * w_k
        new_state.append((w_c * s_st + _dot(kw.T.astype(BF16), v_ext), m_new))
    return new_state


def _mlstm(proj, gates_row, conv_w, gb_col, gb_row, norm_g, *, batch, seq, blk, nsub, group):
    t, npj = proj.shape
    rows = blk * nsub
    w = MLSTM_W
    proj3 = proj.reshape(batch, seq, npj)
    cols = lambda c: (lambda b, i: (b, i, c))
    const2 = lambda b, i: (0, 0)
    y = pl.pallas_call(
        functools.partial(_mlstm_kernel, blk=blk, nsub=nsub, group=group),
        grid=(batch // group, seq // rows),
        in_specs=[pl.BlockSpec((group, rows, 2 * w), cols(OFF_MQ // (2 * w))),
                  pl.BlockSpec((group, rows, w), cols(OFF_MV // w)),
                  pl.BlockSpec((group, rows, w), cols(OFF_MO // w)),
                  pl.BlockSpec((group, rows, IF_PAD), cols(OFF_IF // IF_PAD)),
                  pl.BlockSpec((group, 8, rows), lambda b, i: (b, 0, i)),
                  pl.BlockSpec((CONV_WIDTH, 2 * w), const2),
                  pl.BlockSpec((1, IF_PAD), const2),
                  pl.BlockSpec((8, 1), const2),
                  pl.BlockSpec((1, w), const2)],
        out_specs=pl.BlockSpec((group, rows, w), cols(0)),
        out_shape=jax.ShapeDtypeStruct((batch, seq, w), BF16),
        scratch_shapes=[pltpu.VMEM((group, rows + 8, 2 * w), F32),
                        pltpu.VMEM((group, MLSTM_HEADS, MLSTM_DH, 2 * MLSTM_DH), F32),
                        pltpu.VMEM((group, 8, 128), F32)],
        compiler_params=_cparams("parallel", "arbitrary"),
        name="mlstm",
    )(proj3, proj3, proj3, proj3, gates_row, conv_w, gb_col, gb_row, norm_g)
    return y.reshape(t, w)


def _attnproj_kernel(h_ref, w_ref, seg_ref, gq_ref, gk_ref, o_ref, r_scr, *, dil):
    gw, half = ATTN_GW, ATTN_SLAB // 2
    sub_rows = r_scr.shape[2]
    seg, sub_seg = ATTN_TILE // dil, sub_rows // dil

    def head_norm(x, gain):
        sq = x * x
        hi = sq.astype(BF16)
        lo = (sq - hi.astype(F32)).astype(BF16)
        ss = _dot(hi, seg_ref[...]) + _dot(lo, seg_ref[...])
        return x * lax.rsqrt(ss * (1.0 / ATTN_DH) + EPS) * gain

    low = lax.broadcasted_iota(jnp.int32, (1, ATTN_SLAB), 1) < half
    for s in range(ATTN_TILE // sub_rows):
        rows = slice(s * sub_rows, (s + 1) * sub_rows)
        res = _dot(h_ref[rows, :], w_ref[0])
        q = head_norm(res[:, :gw], gq_ref[...]) * (ATTN_DH ** -0.5)
        k = head_norm(res[:, gw:2 * gw], gk_ref[...])
        slabs = []
        for pair in range(gw // ATTN_SLAB):
            qp = q[:, pair * ATTN_SLAB:(pair + 1) * ATTN_SLAB]
            slabs += [jnp.where(low, qp, 0.0), jnp.where(low, 0.0, qp)]
        slabs += [k[:, c * 128:(c + 1) * 128] for c in range(gw // 128)]
        slabs += [res[:, 2 * gw + c * 128:2 * gw + (c + 1) * 128] for c in range(gw // 128)]
        for c, slab in enumerate(slabs):
            if dil == 1:
                o_ref[rows, c * 128:(c + 1) * 128] = slab.astype(o_ref.dtype)
            else:
                r_scr[s % 2, c] = slab
        if dil > 1:
            for r in range(dil):
                dst = slice(r * seg + s * sub_seg, r * seg + (s + 1) * sub_seg)
                for c in range(r_scr.shape[1]):
                    o_ref[dst, c * 128:(c + 1) * 128] = (
                        r_scr[s % 2, c, pl.ds(r, sub_seg, stride=dil), :].astype(o_ref.dtype))


def _attnproj(h, w, seg_ones, gq, gk, *, layer, group, dilation):
    t, d = h.shape
    wcols = 3 * ATTN_GW
    const2 = lambda i: (0, 0)
    return pl.pallas_call(
        functools.partial(_attnproj_kernel, dil=dilation),
        grid=(t // ATTN_TILE,),
        in_specs=[pl.BlockSpec((ATTN_TILE, d), lambda i: (i, 0)),
                  pl.BlockSpec((1, d, wcols), lambda i: (layer, 0, group)),
                  pl.BlockSpec((ATTN_GW, ATTN_GW), const2),
                  pl.BlockSpec((1, ATTN_GW), const2), pl.BlockSpec((1, ATTN_GW), const2)],
        out_specs=pl.BlockSpec((ATTN_TILE, ATTN_COLS), lambda i: (i, 0)),
        out_shape=jax.ShapeDtypeStruct((t, ATTN_COLS), BF16),
        scratch_shapes=[pltpu.VMEM((2, ATTN_COLS // 128, 512, 128), F32)],
        compiler_params=_cparams("parallel"),
        name=f"attnproj{group}",
    )(h, w, seg_ones, gq, gk)


def _dattn_kernel(q_ref, kc_ref, kp_ref, vc_ref, vp_ref, bias_ref, o_ref, lse_ref,
                  kx_scr, vx_scr, o_scr, l_scr, *, dil):
    blk = ATTN_BLOCK
    per = ATTN_SUB // dil
    first_tile = pl.program_id(1) == 0
    for r in range(dil):
        base = r * (per + 1) * blk
        last = slice((r * per + per - 1) * blk, (r * per + per) * blk)
        mine = slice(r * per * blk, (r + 1) * per * blk)
        kx_scr[base:base + blk, :] = kp_ref[last, :]
        vx_scr[base:base + blk, :] = vp_ref[last, :]
        kx_scr[base + blk:base + (per + 1) * blk, :] = kc_ref[mine, :]
        vx_scr[base + blk:base + (per + 1) * blk, :] = vc_ref[mine, :]

    low = lax.broadcasted_iota(jnp.int32, (1, ATTN_SLAB), 1) < ATTN_SLAB // 2
    no_prev = lax.broadcasted_iota(jnp.int32, (1, 2 * blk), 1) < blk
    for r in range(dil):
        for sub in range(per):
            u = r * per + sub
            win = slice((r * (per + 1) + sub) * blk, (r * (per + 1) + sub + 2) * blk)
            o_slabs, l_slabs = [], []
            for pair in range(ATTN_GW // ATTN_SLAB):
                cols = slice(pair * ATTN_SLAB, (pair + 1) * ATTN_SLAB)
                kx, vx = kx_scr[win, cols], vx_scr[win, cols]
                o_pair, l_pair = [], []
                for h in (2 * pair, 2 * pair + 1):
                    logits = _dot_nt(q_ref[u * blk:(u + 1) * blk, h * ATTN_SLAB:(h + 1) * ATTN_SLAB], kx)
                    logits = logits + bias_ref[h]
                    if sub == 0:
                        logits = jnp.where(first_tile & no_prev, NEG, logits)
                    m = jnp.max(logits, axis=-1, keepdims=True)
                    p = jnp.exp(logits - m)
                    l = jnp.sum(p, axis=-1, keepdims=True)
                    o_pair.append(_dot(p.astype(BF16), vx) / l)
                    l_pair.append(m + jnp.log(l))
                o_slabs.append(jnp.where(low, o_pair[0], o_pair[1]))
                l_slabs.append(jnp.where(low, l_pair[0], l_pair[1]))
            dst = pl.ds(sub * blk * dil + r, blk, stride=dil) if dil > 1 else slice(u * blk, (u + 1) * blk)
            for c in range(ATTN_GW // ATTN_SLAB):
                o_scr[c, dst, :] = o_slabs[c]
                l_scr[c, dst, :] = l_slabs[c]
    for c in range(ATTN_GW // ATTN_SLAB):
        o_ref[:, c * ATTN_SLAB:(c + 1) * ATTN_SLAB] = o_scr[c].astype(o_ref.dtype)
        lse_ref[:, c * ATTN_SLAB:(c + 1) * ATTN_SLAB] = l_scr[c]


def _dattn(aproj, bias, *, seq, group, dilation):
    t = aproj.shape[0]
    tiles = seq // ATTN_TILE
    qw = HEADS_PER_GROUP * ATTN_SLAB
    cq, ck, cv = 0, qw // ATTN_GW, qw // ATTN_GW + 1
    blk = (ATTN_TILE, ATTN_GW)
    cur = lambda c: (lambda b, j: (b * tiles + j, c))
    prev = lambda c: (lambda b, j: (b * tiles + jnp.maximum(j - 1, 0), c))
    xrows = ATTN_TILE + dilation * ATTN_BLOCK
    return pl.pallas_call(
        functools.partial(_dattn_kernel, dil=dilation),
        grid=(t // seq, tiles),
        in_specs=[pl.BlockSpec((ATTN_TILE, qw), cur(cq)),
                  pl.BlockSpec(blk, cur(ck)), pl.BlockSpec(blk, prev(ck)),
                  pl.BlockSpec(blk, cur(cv)), pl.BlockSpec(blk, prev(cv)),
                  pl.BlockSpec((HEADS_PER_GROUP, ATTN_BLOCK, 2 * ATTN_BLOCK), lambda b, j: (0, 0, 0))],
        out_specs=[pl.BlockSpec(blk, cur(0)), pl.BlockSpec(blk, cur(0))],
        out_shape=[jax.ShapeDtypeStruct((t, ATTN_GW), BF16), jax.ShapeDtypeStruct((t, ATTN_GW), F32)],
        scratch_shapes=[pltpu.VMEM((xrows, ATTN_GW), BF16), pltpu.VMEM((xrows, ATTN_GW), BF16),
                        pltpu.VMEM((ATTN_GW // ATTN_SLAB, ATTN_TILE, ATTN_SLAB), F32),
                        pltpu.VMEM((ATTN_GW // ATTN_SLAB, ATTN_TILE, ATTN_SLAB), F32)],
        compiler_params=_cparams("parallel", "arbitrary"),
        name=f"dattn{group}",
    )(aproj, aproj, aproj, aproj, aproj, bias)


def _rel_bucket(n):
    max_exact = REL_BUCKETS // 2
    nf = jnp.maximum(n, 1).astype(F32)
    log_b = max_exact + (jnp.log(nf / max_exact) / math.log(REL_MAX_DIST / max_exact)
                         * (REL_BUCKETS - max_exact)).astype(jnp.int32)
    return jnp.where(n < max_exact, n, jnp.minimum(log_b, REL_BUCKETS - 1))


def _attn_bias(rel_bias, group):
    window, dilation = ATTN_PATTERNS[group]
    steps = window // dilation
    hp = lax.Precision.HIGHEST
    hs = slice(group * HEADS_PER_GROUP, (group + 1) * HEADS_PER_GROUP)
    bucket = _rel_bucket(jnp.arange(steps + 1) * dilation)
    bias_steps = jnp.dot(jax.nn.one_hot(bucket, REL_BUCKETS, dtype=F32), rel_bias[:, hs].astype(F32),
                         precision=hp)
    qi = jnp.arange(ATTN_BLOCK)[:, None]
    ki = jnp.arange(2 * ATTN_BLOCK)[None, :]
    dist = ATTN_BLOCK + qi - ki
    ok = (dist >= 0) & (dist <= steps)
    sel = jax.nn.one_hot(jnp.clip(dist, 0, steps).reshape(-1), steps + 1, dtype=F32)
    bias = jnp.dot(sel, bias_steps, precision=hp).T.reshape(HEADS_PER_GROUP, ATTN_BLOCK, 2 * ATTN_BLOCK)
    return jnp.where(ok[None], bias, NEG)


def _merge_kernel(ya_ref, yb0_ref, yb1_ref, yb2_ref, l0_ref, l1_ref, l2_ref, gu_ref, gv_ref, gate_ref,
                  x_ref, wa_ref, wb_ref, wc_ref, wo_ref, ws_ref, bs_ref, gg_ref, o_ref, yc_scr, *, tm):
    d = x_ref.shape[1]
    l0, l1, l2 = l0_ref[...], l1_ref[...], l2_ref[...]
    mx = jnp.maximum(jnp.maximum(l0, l1), l2)
    e0, e1, e2 = jnp.exp(l0 - mx), jnp.exp(l1 - mx), jnp.exp(l2 - mx)
    inv = 1.0 / (e0 + e1 + e2)
    yb = jnp.concatenate([(yb0_ref[...].astype(F32) * (e0 * inv)).astype(BF16),
                          (yb1_ref[...].astype(F32) * (e1 * inv)).astype(BF16),
                          (yb2_ref[...].astype(F32) * (e2 * inv)).astype(BF16)], axis=-1)

    for j in range(tm // GMLP_CHUNK):
        rows = slice(j * GMLP_CHUNK, (j + 1) * GMLP_CHUNK)
        for g in range(GMLP_GROUPS):
            cols = slice(g * GMLP_GC, (g + 1) * GMLP_GC)
            u = jax.nn.gelu(gu_ref[rows, cols].astype(F32))
            v = _rms(jax.nn.gelu(gv_ref[rows, cols].astype(F32)), gg_ref[:, cols])
            mixed = _dot(ws_ref[g], v.astype(BF16)) + bs_ref[g]
            yc_scr[rows, cols] = (u * mixed).astype(BF16)

    def gate2(k):
        return jnp.tanh(0.5 * gate_ref[:, k * d:(k + 1) * d].astype(F32)) + 1.0

    merged2 = gate2(0) * _dot(ya_ref[...], wa_ref[...])
    merged2 = merged2 + gate2(1) * _dot(yb, wb_ref[...])
    merged2 = merged2 + gate2(2) * _dot(yc_scr[...], wc_ref[...])
    o_ref[...] = x_ref[...] + 0.5 * _dot(merged2.astype(BF16), wo_ref[...])


def _merge(ya, ybs, lses, proj, x2d, wa, wb, wc, wo, ws, bsb, gg, *, tm):
    t, d = x2d.shape
    row = lambda c: (lambda i: (i, c))
    full2 = lambda i: (0, 0)
    full3 = lambda i: (0, 0, 0)
    gspec = pl.BlockSpec((tm, ATTN_GW), row(0))
    return pl.pallas_call(
        functools.partial(_merge_kernel, tm=tm),
        grid=(t // tm,),
        in_specs=[pl.BlockSpec((tm, MLSTM_W), row(0)),
                  gspec, gspec, gspec, gspec, gspec, gspec,
                  pl.BlockSpec((tm, GMLP_W), row(OFF_GU // GMLP_W)),
                  pl.BlockSpec((tm, GMLP_W), row(OFF_GV // GMLP_W)),
                  pl.BlockSpec((tm, N_BRANCH * d), row(OFF_GATE // (N_BRANCH * d))),
                  pl.BlockSpec((tm, d), row(0)),
                  pl.BlockSpec(wa.shape, full2), pl.BlockSpec(wb.shape, full2),
                  pl.BlockSpec(wc.shape, full2), pl.BlockSpec(wo.shape, full2),
                  pl.BlockSpec(ws.shape, full3), pl.BlockSpec(bsb.shape, full3),
                  pl.BlockSpec(gg.shape, full2)],
        out_specs=pl.BlockSpec((tm, d), row(0)),
        out_shape=jax.ShapeDtypeStruct((t, d), F32),
        scratch_shapes=[pltpu.VMEM((tm, GMLP_W), BF16)],
        compiler_params=_cparams("parallel"),
        name="merge",
    )(ya, *ybs, *lses, proj, proj, proj, x2d, wa, wb, wc, wo, ws, bsb, gg)


def _memkv_kernel(mem_ref, g_ref, w_ref, gk_ref, k_ref, v_ref):
    dh, w = XATTN_DH, XATTN_W
    kv = _dot(_rms(mem_ref[0], g_ref[...]).astype(BF16), w_ref[...])
    for h in range(XATTN_HEADS):
        sl = slice(h * dh, (h + 1) * dh)
        k_ref[0, :, sl] = _rms(kv[:, sl], gk_ref[...]).astype(k_ref.dtype)
    v_ref[0] = kv[:, w:].astype(v_ref.dtype)


def _memkv(mem, gain, w_kv, gk):
    b, m, d = mem.shape
    full2 = lambda i: (0, 0)
    return pl.pallas_call(
        _memkv_kernel,
        grid=(b,),
        in_specs=[pl.BlockSpec((1, m, d), lambda i: (i, 0, 0)),
                  pl.BlockSpec((1, d), full2),
                  pl.BlockSpec(w_kv.shape, full2),
                  pl.BlockSpec((1, XATTN_DH), full2)],
        out_specs=[pl.BlockSpec((1, m, XATTN_W), lambda i: (i, 0, 0)),
                   pl.BlockSpec((1, m, XATTN_W), lambda i: (i, 0, 0))],
        out_shape=[jax.ShapeDtypeStruct((b, m, XATTN_W), BF16),
                   jax.ShapeDtypeStruct((b, m, XATTN_W), BF16)],
        compiler_params=_cparams("parallel", vmem=VMEM_LIMIT_SMALL),
        name="memkv",
    )(mem, gain, w_kv, gk)


def _route(logits):
    tm = logits.shape[1]
    e = jnp.exp(logits - jnp.max(logits, axis=0, keepdims=True))
    probs = e / jnp.sum(e, axis=0, keepdims=True)
    rowi = lax.broadcasted_iota(jnp.int32, (8, tm), 0)
    real = rowi < EXPERTS_PER_GROUP
    tops = []
    for g in range(N_EXPERT_GROUPS):
        pg = jnp.where(real, probs[8 * g:8 * g + 8, :], -0.5)
        m1 = jnp.max(pg, axis=0, keepdims=True)
        i1 = jnp.min(jnp.where(pg == m1, rowi, 8), axis=0, keepdims=True)
        pg2 = jnp.where(rowi == i1, -1.0, pg)
        m2 = jnp.max(pg2, axis=0, keepdims=True)
        i2 = jnp.min(jnp.where(pg2 == m2, rowi, 8), axis=0, keepdims=True)
        tops.append((m1, i1, m2, i2))
    best = jnp.zeros((1, tm), jnp.int32)
    best_score = tops[0][0] + tops[0][2]
    for g in range(1, N_EXPERT_GROUPS):
        score = tops[g][0] + tops[g][2]
        better = score > best_score
        best = jnp.where(better, g, best)
        best_score = jnp.where(better, score, best_score)
    m1, i1, m2, i2 = tops[0]
    for g in range(1, N_EXPERT_GROUPS):
        m1, i1, m2, i2 = (jnp.where(best == g, new, old) for new, old in zip(tops[g], (m1, i1, m2, i2)))
    tot = m1 + m2
    base = best * EXPERTS_PER_GROUP
    return base + i1, base + i2, m1 / tot, m2 / tot


def _pack_bf16_pairs(x):
    n = x.shape[1] // 2
    hi = lax.bitcast_convert_type(x[:, :n].astype(BF16).astype(F32), jnp.uint32)
    lo = lax.bitcast_convert_type(x[:, n:].astype(BF16).astype(F32), jnp.uint32)
    return hi | (lo >> 16)


def _unpack_bf16_pairs(p):
    hi = lax.bitcast_convert_type(p & jnp.uint32(0xFFFF0000), F32)
    lo = lax.bitcast_convert_type(p << 16, F32)
    return hi, lo


def _store_row_chunks(ref, packed):
    for j in range(ROW_CHUNKS):
        ref[j] = packed[:, j * 128:(j + 1) * 128]


def _load_row_chunks(ref):
    return jnp.concatenate([ref[j] for j in range(ROW_CHUNKS)], axis=-1)


def _xattn_kernel(x_ref, k_ref, v_ref, gx_ref, wq_ref, gq_ref, wo_ref, gf_ref, rw_ref, rb_ref,
                  xo_ref, hf_ref, eidx_ref, wts_ref):
    dh = XATTN_DH
    x = x_ref[...]
    q = _dot(_rms(x, gx_ref[...]).astype(BF16), wq_ref[...])
    outs = []
    for h in range(XATTN_HEADS):
        sl = slice(h * dh, (h + 1) * dh)
        q_h = (_rms(q[:, sl], gq_ref[...]) * (dh ** -0.5)).astype(BF16)
        logits = _dot_nt(q_h, k_ref[0, :, sl])
        p = jnp.exp(logits - jnp.max(logits, axis=-1, keepdims=True))
        o = _dot(p.astype(BF16), v_ref[0, :, sl]) / jnp.sum(p, axis=-1, keepdims=True)
        outs.append(o.astype(BF16))
    xn = x + _dot(jnp.concatenate(outs, axis=-1), wo_ref[...])
    xo_ref[...] = xn
    hf = _rms(xn, gf_ref[...])
    _store_row_chunks(hf_ref, _pack_bf16_pairs(hf))
    rw = rw_ref[...]
    rw_hi = rw.astype(BF16)
    rw_lo = (rw - rw_hi.astype(F32)).astype(BF16)
    hf_hi = hf.astype(BF16)
    hf_lo = (hf - hf_hi.astype(F32)).astype(BF16)
    logits_t = _dot_nt(rw_hi, hf_hi) + _dot_nt(rw_hi, hf_lo) + _dot_nt(rw_lo, hf_hi) + rb_ref[...]
    e1, e2, w1, w2 = _route(logits_t)
    tm = x.shape[0]
    eidx_ref[...] = jnp.concatenate([e1, e2, jnp.zeros((6, tm), jnp.int32)], axis=0)
    wts_ref[...] = jnp.concatenate([w1, w2, jnp.zeros((6, tm), F32)], axis=0)


def _xattn(x2d, k, v, gx, wq, gq, wo, gf, rw_t, rb, *, seq, tm):
    t, d = x2d.shape
    per_b = seq // tm
    full2 = lambda i: (0, 0)
    kv_spec = pl.BlockSpec((1,) + k.shape[1:], lambda i: (i // per_b, 0, 0))
    return pl.pallas_call(
        _xattn_kernel,
        grid=(t // tm,),
        in_specs=[pl.BlockSpec((tm, d), lambda i: (i, 0)), kv_spec, kv_spec,
                  pl.BlockSpec((1, d), full2), pl.BlockSpec(wq.shape, full2),
                  pl.BlockSpec((1, XATTN_DH), full2), pl.BlockSpec(wo.shape, full2),
                  pl.BlockSpec((1, d), full2), pl.BlockSpec(rw_t.shape, full2),
                  pl.BlockSpec(rb.shape, full2)],
        out_specs=[pl.BlockSpec((tm, d), lambda i: (i, 0)),
                   pl.BlockSpec((ROW_CHUNKS, tm, 128), lambda i: (0, i, 0)),
                   pl.BlockSpec((8, tm), lambda i: (0, i)),
                   pl.BlockSpec((8, tm), lambda i: (0, i))],
        out_shape=[jax.ShapeDtypeStruct((t, d), F32),
                   jax.ShapeDtypeStruct((ROW_CHUNKS, t, 128), jnp.uint32),
                   jax.ShapeDtypeStruct((8, t), jnp.int32),
                   jax.ShapeDtypeStruct((8, t), F32)],
        compiler_params=_cparams("parallel"),
        name="xattn_router",
    )(x2d, k, v, gx, wq, gq, wo, gf, rw_t, rb)


def _moe_plan_kernel(eidx_ref, i1_ref, i2_ref, te_ref, na_ref, cnt_scr, carry_scr, *, tb, tm, plane_rows):
    ne = N_EXPERTS
    hp = lax.Precision.HIGHEST
    phase, j = pl.program_id(0), pl.program_id(1)
    rows = lax.broadcasted_iota(jnp.int32, (ne, tb), 0)
    oh1 = rows == eidx_ref[0:1, :]
    oh2 = rows == eidx_ref[1:2, :]
    a = oh1.astype(F32) + oh2.astype(F32)
    blk_cnt = jnp.broadcast_to(jnp.sum(a, axis=1, keepdims=True), cnt_scr.shape)

    @pl.when((phase == 0) & (j == 0))
    def _():
        cnt_scr[...] = jnp.zeros_like(cnt_scr)

    @pl.when(phase == 0)
    def _():
        cnt_scr[...] += blk_cnt

    @pl.when((phase == 1) & (j == 0))
    def _():
        padded = jnp.ceil(cnt_scr[...] * (1.0 / tm)) * tm
        er = lax.broadcasted_iota(jnp.int32, (ne, ne), 0)
        ec = lax.broadcasted_iota(jnp.int32, (ne, ne), 1)
        off = jnp.dot((ec < er).astype(F32), padded, precision=hp, preferred_element_type=F32)
        carry_scr[...] = off
        seg_end = (off + padded)[:, 0:1]
        tile_start = lax.broadcasted_iota(jnp.int32, (ne, te_ref.shape[1]), 1).astype(F32) * tm
        te = jnp.sum((seg_end <= tile_start).astype(F32), axis=0, keepdims=True)
        te_ref[...] = jnp.broadcast_to(jnp.minimum(te, ne - 1.0), te_ref.shape).astype(jnp.int32)
        total = jnp.sum(padded[:, 0:1], axis=0, keepdims=True)
        na_ref[...] = jnp.broadcast_to(total * (1.0 / tm), na_ref.shape).astype(jnp.int32)

    @pl.when(phase == 1)
    def _():
        before = (lax.broadcasted_iota(jnp.int32, (tb, tb), 0)
                  < lax.broadcasted_iota(jnp.int32, (tb, tb), 1)).astype(BF16)
        rank = carry_scr[:, 0:1] + _dot(a.astype(BF16), before)
        d1 = jnp.sum(jnp.where(oh1, rank, 0.0), axis=0, keepdims=True).astype(jnp.int32)
        d2 = jnp.sum(jnp.where(oh2, rank, 0.0), axis=0, keepdims=True).astype(jnp.int32)
        plane = lax.broadcasted_iota(jnp.int32, (8, tb), 0) * plane_rows
        i1_ref[...] = jnp.where(plane < ROW_CHUNKS * plane_rows, plane + d1, 0)
        i2_ref[...] = jnp.where(plane < ROW_CHUNKS * plane_rows, plane + d2, 0)
        carry_scr[...] += blk_cnt


def _moe_plan(eidx, *, tm, n_tiles, tb=512):
    t = eidx.shape[1]
    ntp = -(-n_tiles // 128) * 128
    return pl.pallas_call(
        functools.partial(_moe_plan_kernel, tb=tb, tm=tm, plane_rows=n_tiles * tm),
        grid=(2, t // tb),
        in_specs=[pl.BlockSpec((8, tb), lambda p, j: (0, j))],
        out_specs=[pl.BlockSpec((8, tb), lambda p, j: (0, j * p)),
                   pl.BlockSpec((8, tb), lambda p, j: (0, j * p)),
                   pl.BlockSpec((8, ntp), lambda p, j: (0, 0)),
                   pl.BlockSpec((8, 128), lambda p, j: (0, 0))],
        out_shape=[jax.ShapeDtypeStruct((8, t), jnp.int32),
                   jax.ShapeDtypeStruct((8, t), jnp.int32),
                   jax.ShapeDtypeStruct((8, ntp), jnp.int32),
                   jax.ShapeDtypeStruct((8, 128), jnp.int32)],
        scratch_shapes=[pltpu.VMEM((N_EXPERTS, 128), F32), pltpu.VMEM((N_EXPERTS, 128), F32)],
        compiler_params=_cparams("arbitrary", "arbitrary", vmem=VMEM_LIMIT_SMALL),
        name="moe_plan",
    )(eidx)


def _sc_mesh():
    return plsc.VectorSubcoreMesh(core_axis_name="c", subcore_axis_name="s",
                                  num_cores=SC_CORES, num_subcores=SC_SUBCORES)


def _sc_index_spec(tokens):
    nb = tokens // SC_WINDOW
    return pl.BlockSpec((1, SC_WINDOW), lambda i: (i // nb, i % nb))


def _sc_dispatch(rows, i1, i2, n_out):
    n = rows.shape[0]
    tokens = i1.shape[1]

    @functools.partial(pl.kernel, out_type=jax.ShapeDtypeStruct((n_out, 128), rows.dtype), mesh=_sc_mesh(),
                       name="moe_dispatch")
    def k(x_hbm, i1_hbm, i2_hbm, o_hbm):
        def body(x_vmem, i1_vmem, i2_vmem):
            pltpu.sync_copy(x_vmem, o_hbm.at[i1_vmem.at[0]])
            pltpu.sync_copy(x_vmem, o_hbm.at[i2_vmem.at[0]])

        pltpu.emit_pipeline(
            body, grid=(n // SC_WINDOW,),
            in_specs=[pl.BlockSpec((SC_WINDOW, 128), lambda i: (i, 0)),
                      _sc_index_spec(tokens), _sc_index_spec(tokens)],
            out_specs=[],
            core_axis_name=("c", "s"), dimension_semantics=(pltpu.PARALLEL,),
        )(x_hbm, i1_hbm, i2_hbm)

    return k(rows, i1, i2)


def _sc_collect(table, i1, i2):
    tokens = i1.shape[1]
    n = ROW_CHUNKS * tokens
    out = jax.ShapeDtypeStruct((n, 128), table.dtype)

    @functools.partial(pl.kernel, out_type=(out, out), mesh=_sc_mesh(), name="moe_collect")
    def k(t_hbm, i1_hbm, i2_hbm, o1_hbm, o2_hbm):
        def body(i1_vmem, i2_vmem, o1_vmem, o2_vmem):
            pltpu.sync_copy(t_hbm.at[i1_vmem.at[0]], o1_vmem)
            pltpu.sync_copy(t_hbm.at[i2_vmem.at[0]], o2_vmem)

        pltpu.emit_pipeline(
            body, grid=(n // SC_WINDOW,),
            in_specs=[_sc_index_spec(tokens), _sc_index_spec(tokens)],
            out_specs=[pl.BlockSpec((SC_WINDOW, 128), lambda i: (i, 0)),
                       pl.BlockSpec((SC_WINDOW, 128), lambda i: (i, 0))],
            core_axis_name=("c", "s"), dimension_semantics=(pltpu.PARALLEL,),
        )(i1_hbm, i2_hbm, o1_hbm, o2_hbm)

    return k(table, i1, i2)


def _experts_kernel(te_ref, na_ref, xs_ref, wg_ref, wu_ref, wd_ref, y_ref, wg_scr, wu_scr, wd_scr):
    i = pl.program_id(0)
    active = i < na_ref[0]

    @pl.when(active & ((i == 0) | (te_ref[i] != te_ref[jnp.maximum(i - 1, 0)])))
    def _():
        wg_scr[...] = wg_ref[0, 0].astype(BF16)
        wu_scr[...] = wu_ref[0, 0].astype(BF16)
        wd_scr[...] = wd_ref[0, 0].astype(BF16)

    @pl.when(active)
    def _():
        hi, lo = _unpack_bf16_pairs(_load_row_chunks(xs_ref))
        h = jnp.concatenate([hi, lo], axis=-1).astype(BF16)
        up = _dot(h, wg_scr[...])
        act = up * _sigmoid(up) * _dot(h, wu_scr[...])
        _store_row_chunks(y_ref, _pack_bf16_pairs(_dot(act.astype(BF16), wd_scr[...])))


def _experts(tile_expert, n_active, xs, wg, wu, wd, *, layer, tm):
    n_tiles = tile_expert.shape[0]
    _, _, d, dff = wg.shape
    rows = lambda i, te, na: (0, jnp.minimum(i, na[0] - 1), 0)
    expert = lambda i, te, na: (layer, te[i], 0, 0)
    return pl.pallas_call(
        _experts_kernel,
        grid_spec=pltpu.PrefetchScalarGridSpec(
            num_scalar_prefetch=2,
            grid=(n_tiles,),
            in_specs=[pl.BlockSpec((ROW_CHUNKS, tm, 128), rows),
                      pl.BlockSpec((1, 1, d, dff), expert),
                      pl.BlockSpec((1, 1, d, dff), expert),
                      pl.BlockSpec((1, 1, dff, d), expert)],
            out_specs=pl.BlockSpec((ROW_CHUNKS, tm, 128), rows),
            scratch_shapes=[pltpu.VMEM((d, dff), BF16), pltpu.VMEM((d, dff), BF16), pltpu.VMEM((dff, d), BF16)]),
        out_shape=jax.ShapeDtypeStruct(xs.shape, xs.dtype),
        compiler_params=_cparams("arbitrary"),
        name="moe_experts",
    )(tile_expert, n_active, xs, wg, wu, wd)


def _moe_combine_kernel(x_ref, y1_ref, y2_ref, w_ref, o_ref):
    half = x_ref.shape[1] // 2
    hi1, lo1 = _unpack_bf16_pairs(_load_row_chunks(y1_ref))
    hi2, lo2 = _unpack_bf16_pairs(_load_row_chunks(y2_ref))
    w1, w2 = w_ref[:, 0:1], w_ref[:, 1:2]
    o_ref[:, :half] = x_ref[:, :half] + w1 * hi1 + w2 * hi2
    o_ref[:, half:] = x_ref[:, half:] + w1 * lo1 + w2 * lo2


def _moe_combine(x2d, y1, y2, wcol, *, tm):
    t, d = x2d.shape
    chunk_spec = pl.BlockSpec((ROW_CHUNKS, tm, 128), lambda i: (0, i, 0))
    return pl.pallas_call(
        _moe_combine_kernel,
        grid=(t // tm,),
        in_specs=[pl.BlockSpec((tm, d), lambda i: (i, 0)), chunk_spec, chunk_spec,
                  pl.BlockSpec((tm, wcol.shape[1]), lambda i: (i, 0))],
        out_specs=pl.BlockSpec((tm, d), lambda i: (i, 0)),
        out_shape=jax.ShapeDtypeStruct((t, d), F32),
        compiler_params=_cparams("parallel", vmem=VMEM_LIMIT_SMALL),
        name="moe_combine",
    )(x2d, y1, y2, wcol)


def _moe(x2d, hf_rows, eidx, wts, wg, wu, wd, *, layer):
    t = x2d.shape[0]
    tm = MOE_TM
    n_tiles = 2 * t // tm + N_EXPERTS
    plane = n_tiles * tm
    i1, i2, te, na = _moe_plan(eidx, tm=tm, n_tiles=n_tiles)
    xs = _sc_dispatch(hf_rows.reshape(ROW_CHUNKS * t, 128), i1, i2, ROW_CHUNKS * plane)
    ys = _experts(te[0, :n_tiles], na[0, :1], xs.reshape(ROW_CHUNKS, plane, 128), wg, wu, wd,
                  layer=layer, tm=tm)
    y1, y2 = _sc_collect(ys.reshape(ROW_CHUNKS * plane, 128), i1, i2)
    return _moe_combine(x2d, y1.reshape(ROW_CHUNKS, t, 128), y2.reshape(ROW_CHUNKS, t, 128), wts[:2].T, tm=512)


def _w_in_layout_kernel(w_ref, main_ref, attn_ref):
    w = w_ref[0]
    src_if = 4 * MLSTM_W
    src_a = src_if + 2 * MLSTM_HEADS
    src_g = src_a + 3 * ATTN_W
    main_ref[0, :, OFF_MQ:OFF_GU] = w[:, 0:src_if].astype(BF16)
    main_ref[0, :, OFF_GU:OFF_IF] = w[:, src_g:src_g + OFF_IF - OFF_GU].astype(BF16)
    first = w[:, src_if:src_if + 128]
    lane = lax.broadcasted_iota(jnp.int32, first.shape, 1)
    main_ref[0, :, OFF_IF:OFF_IF + 128] = jnp.where(lane < 2 * MLSTM_HEADS, first, 0.0).astype(BF16)
    main_ref[0, :, OFF_IF + 128:N_PROJ] = jnp.zeros((w.shape[0], IF_PAD - 128), BF16)
    for g in range(len(ATTN_PATTERNS)):
        for j in range(3):
            src = src_a + j * ATTN_W + g * ATTN_GW
            dst = (3 * g + j) * ATTN_GW
            attn_ref[0, :, dst:dst + ATTN_GW] = w[:, src:src + ATTN_GW].astype(BF16)


def _w_in_layout(w_in, *, rows=256):
    depth, d, n_in = w_in.shape
    n_attn = 3 * ATTN_W
    return pl.pallas_call(
        _w_in_layout_kernel,
        grid=(depth, d // rows),
        in_specs=[pl.BlockSpec((1, rows, n_in), lambda l, i: (l, i, 0))],
        out_specs=[pl.BlockSpec((1, rows, N_PROJ), lambda l, i: (l, i, 0)),
                   pl.BlockSpec((1, rows, n_attn), lambda l, i: (l, i, 0))],
        out_shape=[jax.ShapeDtypeStruct((depth, d, N_PROJ), BF16),
                   jax.ShapeDtypeStruct((depth, d, n_attn), BF16)],
        compiler_params=_cparams("parallel", "parallel"),
        name="w_in_layout",
    )(w_in)


def kernel(x, mem, norm_mix, w_in, mlstm_conv, mlstm_gate_b, mlstm_norm, attn_qk_norm, gmlp_norm, gmlp_ws,
           gmlp_bs, w_branch_a, w_branch_b, w_branch_c, w_out, rel_bias, norm_xattn, norm_mem, w_xq, w_xkv,
           xattn_qk_norm, w_xo, norm_ffn, router_w, router_b, w_expert_gate, w_expert_up, w_expert_down):
    b, s, d = x.shape
    t = b * s
    depth = w_in.shape[0]
    x2d = x.reshape(t, d)

    biases = [_attn_bias(rel_bias, g) for g in range(len(ATTN_PATTERNS))]
    rw_t = jnp.zeros((N_EXPERT_GROUPS, 8, d), F32).at[:, :EXPERTS_PER_GROUP].set(
        router_w.T.reshape(N_EXPERT_GROUPS, EXPERTS_PER_GROUP, d)).reshape(ROUTER_ROWS, d)
    rb = jnp.full((N_EXPERT_GROUPS, 8), NEG, F32).at[:, :EXPERTS_PER_GROUP].set(
        router_b.astype(F32).reshape(N_EXPERT_GROUPS, EXPERTS_PER_GROUP)).reshape(ROUTER_ROWS, 1)
    tril = jnp.tril(jnp.ones((GMLP_CHUNK, GMLP_CHUNK), bool))
    head_of = jnp.arange(ATTN_GW) // ATTN_DH
    seg_ones = (head_of[:, None] == head_of[None, :]).astype(BF16)

    w_main, w_attn = _w_in_layout(w_in)

    for l in range(depth):
        proj, h_mix = _inproj(x2d, norm_mix[l][None], w_main, layer=l, tm=1024, tn=3200)
        gq = jnp.tile(attn_qk_norm[l, 0], HEADS_PER_GROUP)[None]
        gk = jnp.tile(attn_qk_norm[l, 1], HEADS_PER_GROUP)[None]

        gates_row = proj[:, OFF_IF:OFF_IF + 8].astype(F32).reshape(b, s, 8).transpose(0, 2, 1)
        gb_col = jnp.zeros((1, IF_PAD), F32).at[0, :8].set(mlstm_gate_b[l])
        ya = _mlstm(proj, gates_row, mlstm_conv[l], gb_col, mlstm_gate_b[l].reshape(8, 1),
                    mlstm_norm[l][None], batch=b, seq=s, blk=MLSTM_BLOCK, nsub=MLSTM_NSUB,
                    group=MLSTM_GROUP)

        ybs, lses = [], []
        for g, (_, dilation) in enumerate(ATTN_PATTERNS):
            aproj = _attnproj(h_mix, w_attn, seg_ones, gq, gk, layer=l, group=g, dilation=dilation)
            o, lse = _dattn(aproj, biases[g], seq=s, group=g, dilation=dilation)
            ybs.append(o)
            lses.append(lse)

        ws = jnp.where(tril, gmlp_ws[l], 0.0).astype(BF16)
        bsb = jnp.broadcast_to(gmlp_bs[l][:, :, None], (GMLP_GROUPS, GMLP_CHUNK, GMLP_GC)).astype(F32)
        x2d = _merge(ya, ybs, lses, proj, x2d, w_branch_a[l].astype(BF16), w_branch_b[l].astype(BF16),
                     w_branch_c[l].astype(BF16), w_out[l].astype(BF16), ws, bsb, gmlp_norm[l][None], tm=512)

        k_mem, v_mem = _memkv(mem, norm_mem[l][None], w_xkv[l].astype(BF16), xattn_qk_norm[l, 1][None])
        x2d, hf_rows, eidx, wts = _xattn(x2d, k_mem, v_mem, norm_xattn[l][None], w_xq[l].astype(BF16),
                                         xattn_qk_norm[l, 0][None], w_xo[l].astype(BF16), norm_ffn[l][None],
                                         rw_t, rb, seq=s, tm=1024)

        x2d = _moe(x2d, hf_rows, eidx, wts, w_expert_gate, w_expert_up, w_expert_down, layer=l)

    return x2d.reshape(b, s, d)
```

```python
import functools
import math

import jax
import jax.numpy as jnp
import numpy as np
from jax import lax
from jax.experimental import pallas as pl
from jax.experimental.pallas import tpu as pltpu
from jax.experimental.pallas import tpu_sc as plsc

F32 = jnp.float32
BF16 = jnp.bfloat16

EPS = 1e-6
NEG = -1e30

MLSTM_HEADS = 4
MLSTM_DH = 128
MLSTM_W = MLSTM_HEADS * MLSTM_DH
CONV_WIDTH = 4
MLSTM_BLOCK = 128
MLSTM_NSUB = 1
MLSTM_GROUP = 4

ATTN_PATTERNS = ((128, 1), (512, 4), (2048, 16))
HEADS_PER_GROUP = 4
ATTN_DH = 64
ATTN_GW = HEADS_PER_GROUP * ATTN_DH
ATTN_W = len(ATTN_PATTERNS) * ATTN_GW
ATTN_BLOCK = 128
REL_BUCKETS = 32
REL_MAX_DIST = 2048

GMLP_GROUPS = 4
GMLP_GC = 128
GMLP_W = GMLP_GROUPS * GMLP_GC
GMLP_CHUNK = 128

XATTN_HEADS = 4
XATTN_DH = 128
XATTN_W = XATTN_HEADS * XATTN_DH
XATTN_SUB = 1024

N_EXPERTS = 16
N_EXPERT_GROUPS = 4
EXPERTS_PER_GROUP = 4
ROUTER_ROWS = 8 * N_EXPERT_GROUPS

N_BRANCH = 3

MOE_TM = 1024
ROW_CHUNKS = 4
SC_CORES, SC_SUBCORES = 2, 16
SC_WINDOW = 128

OFF_MQ, OFF_MK, OFF_MV, OFF_MO = 0, 512, 1024, 1536
OFF_GU, OFF_GV = 2048, 2560
OFF_GATE = 3072
OFF_IF = 6144
IF_PAD = 256
N_PROJ = OFF_IF + IF_PAD

ATTN_TILE = 2048
ATTN_SUB = ATTN_TILE // ATTN_BLOCK
ATTN_SLAB = 2 * ATTN_DH
ATTN_COLS = HEADS_PER_GROUP * ATTN_SLAB + 2 * ATTN_GW

VMEM_LIMIT = 48 * 1024 * 1024
VMEM_LIMIT_INPROJ = 56 * 1024 * 1024
VMEM_LIMIT_SMALL = 24 * 1024 * 1024


def _cparams(*sem, vmem=VMEM_LIMIT):
    return pltpu.CompilerParams(dimension_semantics=sem, vmem_limit_bytes=vmem)


def _rms(x, gain):
    return x * lax.rsqrt(jnp.mean(x * x, axis=-1, keepdims=True) + EPS) * gain


def _sigmoid(x):
    return 0.5 * jnp.tanh(0.5 * x) + 0.5


def _dot(a, b):
    return jnp.dot(a, b, preferred_element_type=F32)


def _dot_nt(a, b):
    return lax.dot_general(a, b, (((1,), (1,)), ((), ())), preferred_element_type=F32)


def _inproj_kernel(x_ref, g_ref, w_ref, o_ref, h_ref):
    @pl.when(pl.program_id(1) == 0)
    def _():
        h_ref[...] = _rms(x_ref[...], g_ref[...]).astype(BF16)

    o_ref[...] = _dot(h_ref[...], w_ref[0]).astype(o_ref.dtype)


def _inproj(x2d, gain, w, *, layer, tm, tn):
    t, d = x2d.shape
    n = w.shape[2]
    return pl.pallas_call(
        _inproj_kernel,
        grid=(t // tm, n // tn),
        in_specs=[pl.BlockSpec((tm, d), lambda i, j: (i, 0)),
                  pl.BlockSpec((1, d), lambda i, j: (0, 0)),
                  pl.BlockSpec((1, d, tn), lambda i, j: (layer, 0, j))],
        out_specs=[pl.BlockSpec((tm, tn), lambda i, j: (i, j)),
                   pl.BlockSpec((tm, d), lambda i, j: (i, 0))],
        out_shape=[jax.ShapeDtypeStruct((t, n), BF16), jax.ShapeDtypeStruct((t, d), BF16)],
        compiler_params=_cparams("parallel", "arbitrary", vmem=VMEM_LIMIT_INPROJ),
        name="inproj",
    )(x2d, gain, w)


def _log_sigmoid(x):
    return jnp.minimum(x, 0.0) - jnp.log(1.0 + jnp.exp(-jnp.abs(x)))


def _mlstm_kernel(qk_ref, v_ref, og_ref, gc_ref, gr_ref, cw_ref, gbc_ref, gbr_ref, ng_ref, y_ref,
                  xe_scr, s_scr, m_scr, *, blk, nsub, group):
    heads, w = MLSTM_HEADS, MLSTM_W

    @pl.when(pl.program_id(1) == 0)
    def _():
        xe_scr[:, 0:8, :] = jnp.zeros((group, 8, 2 * w), F32)
        s_scr[...] = jnp.zeros_like(s_scr)
        m_scr[...] = jnp.zeros_like(m_scr)

    cw = cw_ref[...]
    ri = lax.broadcasted_iota(jnp.int32, (blk, blk), 0)
    ci = lax.broadcasted_iota(jnp.int32, (blk, blk), 1)
    causal = ri >= ci
    tril = causal.astype(BF16)
    triu = (ri <= ci).astype(BF16)
    states = []
    for g in range(group):
        xe_scr[g, 8:8 + nsub * blk, :] = qk_ref[g].astype(F32)
        states.append([(s_scr[g, h], m_scr[g, h:h + 1, 0:1]) for h in range(heads)])
    for c in range(nsub):
        for g in range(group):
            states[g] = _mlstm_chunk(c * blk, blk, states[g], cw, causal, tril, triu, xe_scr.at[g], v_ref.at[g],
                                     og_ref.at[g], gc_ref.at[g], gr_ref.at[g], gbc_ref, gbr_ref, ng_ref,
                                     y_ref.at[g])
    for g in range(group):
        xe_scr[g, 0:8, :] = xe_scr[g, nsub * blk:nsub * blk + 8, :]
        for h, (s_st, m_st) in enumerate(states[g]):
            s_scr[g, h] = s_st
            m_scr[g, h:h + 1, :] = jnp.broadcast_to(m_st, (1, m_scr.shape[2]))


def _split_bf16(x):
    hi = x.astype(BF16)
    return hi, (x - hi.astype(F32)).astype(BF16)


def _mlstm_chunk(r0, blk, state, cw, causal, tril, triu, xe_scr, v_ref, og_ref, gc_ref, gr_ref, gbc_ref,
                 gbr_ref, ng_ref, y_ref):
    heads, dh, w = MLSTM_HEADS, MLSTM_DH, MLSTM_W
    rows = slice(r0, r0 + blk)
    conv = cw[CONV_WIDTH - 1:CONV_WIDTH, :] * xe_scr[8 + r0:8 + r0 + blk, :]
    for j in range(CONV_WIDTH - 1):
        off = 8 + r0 - (CONV_WIDTH - 1) + j
        conv = conv + cw[j:j + 1, :] * xe_scr[off:off + blk, :]
    qk = conv * _sigmoid(conv)

    gcol = gc_ref[rows, :].astype(F32) + gbc_ref[...]
    grow = gr_ref[:, rows] + gbr_ref[...]
    lc_hi, lc_lo = _split_bf16(_log_sigmoid(gcol))
    lr_hi, lr_lo = _split_bf16(_log_sigmoid(grow))
    bcol = _dot(tril, lc_hi) + _dot(tril, lc_lo)
    brow = _dot(lr_hi, triu) + _dot(lr_lo, triu)
    ones = jnp.ones((blk, dh), BF16)

    new_state = []
    for h in range(heads):
        sl = slice(h * dh, (h + 1) * dh)
        b_c = bcol[:, heads + h:heads + h + 1]
        i_c = gcol[:, h:h + 1]
        b_r = brow[heads + h:heads + h + 1, :]
        i_r = grow[h:h + 1, :]
        s_st, m_st = state[h]

        d_mat = jnp.where(causal, b_c - b_r + i_r, NEG)
        inter = b_c + m_st
        m_t = jnp.maximum(inter, jnp.max(d_mat, axis=-1, keepdims=True))
        w_intra = jnp.exp(d_mat - m_t)
        w_inter = jnp.exp(inter - m_t)

        q_f = qk[:, sl]
        k_f = qk[:, w + h * dh:w + (h + 1) * dh] * (dh ** -0.5)
        q_b = q_f.astype(BF16)
        k_b = k_f.astype(BF16)
        v_ext = jnp.concatenate([v_ref[rows, sl], ones], axis=-1)

        s = _dot_nt(q_b, k_b) * w_intra
        tot = _dot(s.astype(BF16), v_ext) + w_inter * _dot(q_b, s_st.astype(BF16))
        num, den = tot[:, :dh], tot[:, dh:]
        hh = num / jnp.maximum(jnp.abs(den), jnp.exp(-m_t))
        hn = _rms(hh, ng_ref[:, sl])
        y_ref[rows, sl] = (hn * _sigmoid(og_ref[rows, sl].astype(F32))).astype(y_ref.dtype)

        b_last = b_c[blk - 1:blk, :]
        dec = b_last - b_c + i_c
        m_new = jnp.maximum(b_last + m_st, jnp.max(dec, axis=0, keepdims=True))
        w_k = jnp.exp(dec - m_new)
        w_c = jnp.exp(b_last + m_st - m_new)
        kw = k_f * w_k
        new_state.append((w_c * s_st + _dot(kw.T.astype(BF16), v_ext), m_new))
    return new_state


def _mlstm(proj, gates_row, conv_w, gb_col, gb_row, norm_g, *, batch, seq, blk, nsub, group):
    t, npj = proj.shape
    rows = blk * nsub
    w = MLSTM_W
    proj3 = proj.reshape(batch, seq, npj)
    cols = lambda c: (lambda b, i: (b, i, c))
    const2 = lambda b, i: (0, 0)
    y = pl.pallas_call(
        functools.partial(_mlstm_kernel, blk=blk, nsub=nsub, group=group),
        grid=(batch // group, seq // rows),
        in_specs=[pl.BlockSpec((group, rows, 2 * w), cols(OFF_MQ // (2 * w))),
                  pl.BlockSpec((group, rows, w), cols(OFF_MV // w)),
                  pl.BlockSpec((group, rows, w), cols(OFF_MO // w)),
                  pl.BlockSpec((group, rows, IF_PAD), cols(OFF_IF // IF_PAD)),
                  pl.BlockSpec((group, 8, rows), lambda b, i: (b, 0, i)),
                  pl.BlockSpec((CONV_WIDTH, 2 * w), const2),
                  pl.BlockSpec((1, IF_PAD), const2),
                  pl.BlockSpec((8, 1), const2),
                  pl.BlockSpec((1, w), const2)],
        out_specs=pl.BlockSpec((group, rows, w), cols(0)),
        out_shape=jax.ShapeDtypeStruct((batch, seq, w), BF16),
        scratch_shapes=[pltpu.VMEM((group, rows + 8, 2 * w), F32),
                        pltpu.VMEM((group, MLSTM_HEADS, MLSTM_DH, 2 * MLSTM_DH), F32),
                        pltpu.VMEM((group, 8, 128), F32)],
        compiler_params=_cparams("parallel", "arbitrary"),
        name="mlstm",
    )(proj3, proj3, proj3, proj3, gates_row, conv_w, gb_col, gb_row, norm_g)
    return y.reshape(t, w)


def _prefix_max(x):
    n = x.shape[1]
    lane = lax.broadcasted_iota(jnp.int32, x.shape, 1)
    shift = 1
    while shift < n:
        x = jnp.maximum(x, jnp.where(lane >= shift, pltpu.roll(x, shift, 1), NEG))
        shift *= 2
    return x


def _mlstm_rows_kernel(qk_ref, v_ref, og_ref, gi_ref, gf_ref, cw_ref, bi_ref, bf_ref, ng_ref, y_ref,
                       xe_scr, s_scr, m_scr, *, blk, group):
    heads, dh, w = MLSTM_HEADS, MLSTM_DH, MLSTM_W

    @pl.when(pl.program_id(1) == 0)
    def _():
        xe_scr[:, 0:8, :] = jnp.zeros((group, 8, 2 * w), F32)
        s_scr[...] = jnp.zeros_like(s_scr)
        m_scr[...] = jnp.zeros_like(m_scr)

    cw = cw_ref[...]
    causal = lax.broadcasted_iota(jnp.int32, (blk, blk), 0) >= lax.broadcasted_iota(jnp.int32, (blk, blk), 1)
    triu = (lax.broadcasted_iota(jnp.int32, (blk, blk), 0)
            <= lax.broadcasted_iota(jnp.int32, (blk, blk), 1)).astype(BF16)
    ones = jnp.ones((blk, dh), BF16)
    s_in = [[s_scr[g, h] for h in range(heads)] for g in range(group)]
    m_in = [m_scr[g, :, 0:1] for g in range(group)]
    s_out = [[None] * heads for _ in range(group)]
    m_out = [None] * group
    per_seq = []
    for g in range(group):
        xe_scr[g, 8:8 + blk, :] = qk_ref[g].astype(F32)
        conv = cw[CONV_WIDTH - 1:CONV_WIDTH, :] * xe_scr[g, 8:8 + blk, :]
        for j in range(CONV_WIDTH - 1):
            off = 8 - (CONV_WIDTH - 1) + j
            conv = conv + cw[j:j + 1, :] * xe_scr[g, off:off + blk, :]
        xe_scr[g, 0:8, :] = xe_scr[g, blk:blk + 8, :]
        qk = conv * _sigmoid(conv)

        i_r = gi_ref[g] + bi_ref[...]
        lf_hi, lf_lo = _split_bf16(_log_sigmoid(gf_ref[g] + bf_ref[...]))
        b_r = _dot(lf_hi, triu) + _dot(lf_lo, triu)
        m_st = m_in[g]
        a_r = i_r - b_r
        inter = b_r + m_st
        m_t = jnp.maximum(inter, b_r + _prefix_max(a_r))
        b_last = b_r[:, blk - 1:blk]
        dec = b_last - b_r + i_r
        m_new = jnp.maximum(b_last + m_st, jnp.max(dec, axis=1, keepdims=True))
        w_c = jnp.exp(b_last + m_st - m_new)
        m_out[g] = m_new
        pack = jnp.concatenate([b_r - m_t, jnp.exp(inter - m_t), jnp.exp(-m_t), jnp.exp(dec - m_new),
                                jnp.zeros((blk - 32, blk), F32)], axis=0)
        per_seq.append((qk, a_r, pack.T, w_c))

    chains = [(g, h) for h in range(heads) for g in range(group)]
    st = {}
    for g, h in chains:
        qk = per_seq[g][0]
        sl = slice(h * dh, (h + 1) * dh)
        q_b = qk[:, sl].astype(BF16)
        k_f = qk[:, w + h * dh:w + (h + 1) * dh] * (dh ** -0.5)
        v_ext = jnp.concatenate([v_ref[g, :, sl], ones], axis=-1)
        st[g, h] = (q_b, k_f, v_ext, _dot_nt(q_b, k_f.astype(BF16)), _dot(q_b, s_in[g][h].astype(BF16)))
    for g, h in chains:
        q_b, k_f, v_ext, qk_t, q_state = st[g, h]
        _, a_r, cols, _ = per_seq[g]
        u_c, w_inter = cols[:, h:h + 1], cols[:, 8 + h:9 + h]
        w_intra = jnp.exp(jnp.where(causal, u_c + a_r[h:h + 1, :], NEG))
        st[g, h] = (k_f, v_ext, _dot((qk_t * w_intra).astype(BF16), v_ext) + w_inter * q_state)
    for g, h in chains:
        k_f, v_ext, tot = st[g, h]
        _, _, cols, w_c = per_seq[g]
        em_c, w_k = cols[:, 16 + h:17 + h], cols[:, 24 + h:25 + h]
        sl = slice(h * dh, (h + 1) * dh)
        num, den = tot[:, :dh], tot[:, dh:]
        hh = num / jnp.maximum(jnp.abs(den), em_c)
        hn = _rms(hh, ng_ref[:, sl])
        y_ref[g, :, sl] = (hn * _sigmoid(og_ref[g, :, sl].astype(F32))).astype(y_ref.dtype)
        s_out[g][h] = w_c[h:h + 1, :] * s_in[g][h] + _dot((k_f * w_k).T.astype(BF16), v_ext)
    for g in range(group):
        m_scr[g] = jnp.broadcast_to(m_out[g], m_scr.shape[1:])
        for h in range(heads):
            s_scr[g, h] = s_out[g][h]


def _mlstm_rows(proj, gates_i, gates_f, conv_w, bias_i, bias_f, norm_g, *, batch, seq, blk, group):
    t, npj = proj.shape
    w = MLSTM_W
    proj3 = proj.reshape(batch, seq, npj)
    cols = lambda c: (lambda b, i: (b, i, c))
    const2 = lambda b, i: (0, 0)
    gate_spec = pl.BlockSpec((group, 8, blk), lambda b, i: (b, 0, i))
    y = pl.pallas_call(
        functools.partial(_mlstm_rows_kernel, blk=blk, group=group),
        grid=(batch // group, seq // blk),
        in_specs=[pl.BlockSpec((group, blk, 2 * w), cols(OFF_MQ // (2 * w))),
                  pl.BlockSpec((group, blk, w), cols(OFF_MV // w)),
                  pl.BlockSpec((group, blk, w), cols(OFF_MO // w)),
                  gate_spec, gate_spec,
                  pl.BlockSpec((CONV_WIDTH, 2 * w), const2),
                  pl.BlockSpec((8, 1), const2), pl.BlockSpec((8, 1), const2),
                  pl.BlockSpec((1, w), const2)],
        out_specs=pl.BlockSpec((group, blk, w), cols(0)),
        out_shape=jax.ShapeDtypeStruct((batch, seq, w), BF16),
        scratch_shapes=[pltpu.VMEM((group, blk + 8, 2 * w), F32),
                        pltpu.VMEM((group, MLSTM_HEADS, MLSTM_DH, 2 * MLSTM_DH), F32),
                        pltpu.VMEM((group, 8, 128), F32)],
        compiler_params=_cparams("parallel", "arbitrary", vmem=VMEM_LIMIT_SMALL),
        name="mlstm",
    )(proj3, proj3, proj3, gates_i, gates_f, conv_w, bias_i, bias_f, norm_g)
    return y.reshape(t, w)


def _attnproj_kernel(h_ref, w_ref, seg_ref, gq_ref, gk_ref, o_ref, r_scr, *, dil):
    gw, half = ATTN_GW, ATTN_SLAB // 2
    sub_rows = r_scr.shape[2]
    seg, sub_seg = ATTN_TILE // dil, sub_rows // dil

    def head_norm(x, gain):
        sq = x * x
        hi = sq.astype(BF16)
        lo = (sq - hi.astype(F32)).astype(BF16)
        ss = _dot(hi, seg_ref[...]) + _dot(lo, seg_ref[...])
        return x * lax.rsqrt(ss * (1.0 / ATTN_DH) + EPS) * gain

    low = lax.broadcasted_iota(jnp.int32, (1, ATTN_SLAB), 1) < half
    for s in range(ATTN_TILE // sub_rows):
        rows = slice(s * sub_rows, (s + 1) * sub_rows)
        res = _dot(h_ref[rows, :], w_ref[0])
        q = head_norm(res[:, :gw], gq_ref[...]) * (ATTN_DH ** -0.5)
        k = head_norm(res[:, gw:2 * gw], gk_ref[...])
        slabs = []
        for pair in range(gw // ATTN_SLAB):
            qp = q[:, pair * ATTN_SLAB:(pair + 1) * ATTN_SLAB]
            slabs += [jnp.where(low, qp, 0.0), jnp.where(low, 0.0, qp)]
        slabs += [k[:, c * 128:(c + 1) * 128] for c in range(gw // 128)]
        slabs += [res[:, 2 * gw + c * 128:2 * gw + (c + 1) * 128] for c in range(gw // 128)]
        for c, slab in enumerate(slabs):
            if dil == 1:
                o_ref[rows, c * 128:(c + 1) * 128] = slab.astype(o_ref.dtype)
            else:
                r_scr[s % 2, c] = slab
        if dil > 1:
            for r in range(dil):
                dst = slice(r * seg + s * sub_seg, r * seg + (s + 1) * sub_seg)
                for c in range(r_scr.shape[1]):
                    o_ref[dst, c * 128:(c + 1) * 128] = (
                        r_scr[s % 2, c, pl.ds(r, sub_seg, stride=dil), :].astype(o_ref.dtype))


def _attnproj(h, w, seg_ones, gq, gk, *, layer, group, dilation):
    t, d = h.shape
    wcols = 3 * ATTN_GW
    const2 = lambda i: (0, 0)
    return pl.pallas_call(
        functools.partial(_attnproj_kernel, dil=dilation),
        grid=(t // ATTN_TILE,),
        in_specs=[pl.BlockSpec((ATTN_TILE, d), lambda i: (i, 0)),
                  pl.BlockSpec((1, d, wcols), lambda i: (layer, 0, group)),
                  pl.BlockSpec((ATTN_GW, ATTN_GW), const2),
                  pl.BlockSpec((1, ATTN_GW), const2), pl.BlockSpec((1, ATTN_GW), const2)],
        out_specs=pl.BlockSpec((ATTN_TILE, ATTN_COLS), lambda i: (i, 0)),
        out_shape=jax.ShapeDtypeStruct((t, ATTN_COLS), BF16),
        scratch_shapes=[pltpu.VMEM((2, ATTN_COLS // 128, 512, 128), F32)],
        compiler_params=_cparams("parallel"),
        name=f"attnproj{group}",
    )(h, w, seg_ones, gq, gk)


def _dattn_kernel(q_ref, kc_ref, kp_ref, vc_ref, vp_ref, bias_ref, o_ref, lse_ref,
                  kx_scr, vx_scr, o_scr, l_scr, *, dil):
    blk = ATTN_BLOCK
    per = ATTN_SUB // dil
    first_tile = pl.program_id(1) == 0
    for r in range(dil):
        base = r * (per + 1) * blk
        last = slice((r * per + per - 1) * blk, (r * per + per) * blk)
        mine = slice(r * per * blk, (r + 1) * per * blk)
        kx_scr[base:base + blk, :] = kp_ref[last, :]
        vx_scr[base:base + blk, :] = vp_ref[last, :]
        kx_scr[base + blk:base + (per + 1) * blk, :] = kc_ref[mine, :]
        vx_scr[base + blk:base + (per + 1) * blk, :] = vc_ref[mine, :]

    low = lax.broadcasted_iota(jnp.int32, (1, ATTN_SLAB), 1) < ATTN_SLAB // 2
    no_prev = lax.broadcasted_iota(jnp.int32, (1, 2 * blk), 1) < blk
    for r in range(dil):
        for sub in range(per):
            u = r * per + sub
            win = slice((r * (per + 1) + sub) * blk, (r * (per + 1) + sub + 2) * blk)
            o_slabs, l_slabs = [], []
            for pair in range(ATTN_GW // ATTN_SLAB):
                cols = slice(pair * ATTN_SLAB, (pair + 1) * ATTN_SLAB)
                kx, vx = kx_scr[win, cols], vx_scr[win, cols]
                o_pair, l_pair = [], []
                for h in (2 * pair, 2 * pair + 1):
                    logits = _dot_nt(q_ref[u * blk:(u + 1) * blk, h * ATTN_SLAB:(h + 1) * ATTN_SLAB], kx)
                    logits = logits + bias_ref[h]
                    if sub == 0:
                        logits = jnp.where(first_tile & no_prev, NEG, logits)
                    m = jnp.max(logits, axis=-1, keepdims=True)
                    p = jnp.exp(logits - m)
                    l = jnp.sum(p, axis=-1, keepdims=True)
                    o_pair.append(_dot(p.astype(BF16), vx) / l)
                    l_pair.append(m + jnp.log(l))
                o_slabs.append(jnp.where(low, o_pair[0], o_pair[1]))
                l_slabs.append(jnp.where(low, l_pair[0], l_pair[1]))
            dst = pl.ds(sub * blk * dil + r, blk, stride=dil) if dil > 1 else slice(u * blk, (u + 1) * blk)
            for c in range(ATTN_GW // ATTN_SLAB):
                o_scr[c, dst, :] = o_slabs[c]
                l_scr[c, dst, :] = l_slabs[c]
    for c in range(ATTN_GW // ATTN_SLAB):
        o_ref[:, c * ATTN_SLAB:(c + 1) * ATTN_SLAB] = o_scr[c].astype(o_ref.dtype)
        lse_ref[:, c * ATTN_SLAB:(c + 1) * ATTN_SLAB] = l_scr[c]


def _dattn(aproj, bias, *, seq, group, dilation):
    t = aproj.shape[0]
    tiles = seq // ATTN_TILE
    qw = HEADS_PER_GROUP * ATTN_SLAB
    cq, ck, cv = 0, qw // ATTN_GW, qw // ATTN_GW + 1
    blk = (ATTN_TILE, ATTN_GW)
    cur = lambda c: (lambda b, j: (b * tiles + j, c))
    prev = lambda c: (lambda b, j: (b * tiles + jnp.maximum(j - 1, 0), c))
    xrows = ATTN_TILE + dilation * ATTN_BLOCK
    return pl.pallas_call(
        functools.partial(_dattn_kernel, dil=dilation),
        grid=(t // seq, tiles),
        in_specs=[pl.BlockSpec((ATTN_TILE, qw), cur(cq)),
                  pl.BlockSpec(blk, cur(ck)), pl.BlockSpec(blk, prev(ck)),
                  pl.BlockSpec(blk, cur(cv)), pl.BlockSpec(blk, prev(cv)),
                  pl.BlockSpec((HEADS_PER_GROUP, ATTN_BLOCK, 2 * ATTN_BLOCK), lambda b, j: (0, 0, 0))],
        out_specs=[pl.BlockSpec(blk, cur(0)), pl.BlockSpec(blk, cur(0))],
        out_shape=[jax.ShapeDtypeStruct((t, ATTN_GW), BF16), jax.ShapeDtypeStruct((t, ATTN_GW), F32)],
        scratch_shapes=[pltpu.VMEM((xrows, ATTN_GW), BF16), pltpu.VMEM((xrows, ATTN_GW), BF16),
                        pltpu.VMEM((ATTN_GW // ATTN_SLAB, ATTN_TILE, ATTN_SLAB), F32),
                        pltpu.VMEM((ATTN_GW // ATTN_SLAB, ATTN_TILE, ATTN_SLAB), F32)],
        compiler_params=_cparams("parallel", "arbitrary"),
        name=f"dattn{group}",
    )(aproj, aproj, aproj, aproj, aproj, bias)


def _rel_bucket(n):
    max_exact = REL_BUCKETS // 2
    nf = jnp.maximum(n, 1).astype(F32)
    log_b = max_exact + (jnp.log(nf / max_exact) / math.log(REL_MAX_DIST / max_exact)
                         * (REL_BUCKETS - max_exact)).astype(jnp.int32)
    return jnp.where(n < max_exact, n, jnp.minimum(log_b, REL_BUCKETS - 1))


def _attn_bias(rel_bias, group):
    window, dilation = ATTN_PATTERNS[group]
    steps = window // dilation
    hp = lax.Precision.HIGHEST
    hs = slice(group * HEADS_PER_GROUP, (group + 1) * HEADS_PER_GROUP)
    bucket = _rel_bucket(jnp.arange(steps + 1) * dilation)
    bias_steps = jnp.dot(jax.nn.one_hot(bucket, REL_BUCKETS, dtype=F32), rel_bias[:, hs].astype(F32),
                         precision=hp)
    qi = jnp.arange(ATTN_BLOCK)[:, None]
    ki = jnp.arange(2 * ATTN_BLOCK)[None, :]
    dist = ATTN_BLOCK + qi - ki
    ok = (dist >= 0) & (dist <= steps)
    sel = jax.nn.one_hot(jnp.clip(dist, 0, steps).reshape(-1), steps + 1, dtype=F32)
    bias = jnp.dot(sel, bias_steps, precision=hp).T.reshape(HEADS_PER_GROUP, ATTN_BLOCK, 2 * ATTN_BLOCK)
    return jnp.where(ok[None], bias, NEG)


def _merge_kernel(ya_ref, yb0_ref, yb1_ref, yb2_ref, l0_ref, l1_ref, l2_ref, gu_ref, gv_ref, gate_ref,
                  x_ref, wa_ref, wb_ref, wc_ref, wo_ref, ws_ref, bs_ref, gg_ref, o_ref, yc_scr, *, tm):
    d = x_ref.shape[1]
    l0, l1, l2 = l0_ref[...], l1_ref[...], l2_ref[...]
    mx = jnp.maximum(jnp.maximum(l0, l1), l2)
    e0, e1, e2 = jnp.exp(l0 - mx), jnp.exp(l1 - mx), jnp.exp(l2 - mx)
    inv = 1.0 / (e0 + e1 + e2)
    yb = jnp.concatenate([(yb0_ref[...].astype(F32) * (e0 * inv)).astype(BF16),
                          (yb1_ref[...].astype(F32) * (e1 * inv)).astype(BF16),
                          (yb2_ref[...].astype(F32) * (e2 * inv)).astype(BF16)], axis=-1)

    for j in range(tm // GMLP_CHUNK):
        rows = slice(j * GMLP_CHUNK, (j + 1) * GMLP_CHUNK)
        for g in range(GMLP_GROUPS):
            cols = slice(g * GMLP_GC, (g + 1) * GMLP_GC)
            u = jax.nn.gelu(gu_ref[rows, cols].astype(F32))
            v = _rms(jax.nn.gelu(gv_ref[rows, cols].astype(F32)), gg_ref[:, cols])
            mixed = _dot(ws_ref[g], v.astype(BF16)) + bs_ref[g]
            yc_scr[rows, cols] = (u * mixed).astype(BF16)

    def gate2(k):
        return jnp.tanh(0.5 * gate_ref[:, k * d:(k + 1) * d].astype(F32)) + 1.0

    merged2 = gate2(0) * _dot(ya_ref[...], wa_ref[...])
    merged2 = merged2 + gate2(1) * _dot(yb, wb_ref[...])
    merged2 = merged2 + gate2(2) * _dot(yc_scr[...], wc_ref[...])
    o_ref[...] = x_ref[...] + 0.5 * _dot(merged2.astype(BF16), wo_ref[...])


def _merge(ya, ybs, lses, proj, x2d, wa, wb, wc, wo, ws, bsb, gg, *, tm):
    t, d = x2d.shape
    row = lambda c: (lambda i: (i, c))
    full2 = lambda i: (0, 0)
    full3 = lambda i: (0, 0, 0)
    gspec = pl.BlockSpec((tm, ATTN_GW), row(0))
    return pl.pallas_call(
        functools.partial(_merge_kernel, tm=tm),
        grid=(t // tm,),
        in_specs=[pl.BlockSpec((tm, MLSTM_W), row(0)),
                  gspec, gspec, gspec, gspec, gspec, gspec,
                  pl.BlockSpec((tm, GMLP_W), row(OFF_GU // GMLP_W)),
                  pl.BlockSpec((tm, GMLP_W), row(OFF_GV // GMLP_W)),
                  pl.BlockSpec((tm, N_BRANCH * d), row(OFF_GATE // (N_BRANCH * d))),
                  pl.BlockSpec((tm, d), row(0)),
                  pl.BlockSpec(wa.shape, full2), pl.BlockSpec(wb.shape, full2),
                  pl.BlockSpec(wc.shape, full2), pl.BlockSpec(wo.shape, full2),
                  pl.BlockSpec(ws.shape, full3), pl.BlockSpec(bsb.shape, full3),
                  pl.BlockSpec(gg.shape, full2)],
        out_specs=pl.BlockSpec((tm, d), row(0)),
        out_shape=jax.ShapeDtypeStruct((t, d), F32),
        scratch_shapes=[pltpu.VMEM((tm, GMLP_W), BF16)],
        compiler_params=_cparams("parallel"),
        name="merge",
    )(ya, *ybs, *lses, proj, proj, proj, x2d, wa, wb, wc, wo, ws, bsb, gg)


def _memkv_kernel(mem_ref, g_ref, w_ref, gk_ref, k_ref, v_ref):
    dh, w = XATTN_DH, XATTN_W
    kv = _dot(_rms(mem_ref[0], g_ref[...]).astype(BF16), w_ref[...])
    for h in range(XATTN_HEADS):
        sl = slice(h * dh, (h + 1) * dh)
        k_ref[0, :, sl] = _rms(kv[:, sl], gk_ref[...]).astype(k_ref.dtype)
    v_ref[0] = kv[:, w:].astype(v_ref.dtype)


def _memkv(mem, gain, w_kv, gk):
    b, m, d = mem.shape
    full2 = lambda i: (0, 0)
    return pl.pallas_call(
        _memkv_kernel,
        grid=(b,),
        in_specs=[pl.BlockSpec((1, m, d), lambda i: (i, 0, 0)),
                  pl.BlockSpec((1, d), full2),
                  pl.BlockSpec(w_kv.shape, full2),
                  pl.BlockSpec((1, XATTN_DH), full2)],
        out_specs=[pl.BlockSpec((1, m, XATTN_W), lambda i: (i, 0, 0)),
                   pl.BlockSpec((1, m, XATTN_W), lambda i: (i, 0, 0))],
        out_shape=[jax.ShapeDtypeStruct((b, m, XATTN_W), BF16),
                   jax.ShapeDtypeStruct((b, m, XATTN_W), BF16)],
        compiler_params=_cparams("parallel", vmem=VMEM_LIMIT_SMALL),
        name="memkv",
    )(mem, gain, w_kv, gk)


def _route(logits):
    tm = logits.shape[1]
    e = jnp.exp(logits - jnp.max(logits, axis=0, keepdims=True))
    probs = e / jnp.sum(e, axis=0, keepdims=True)
    rowi = lax.broadcasted_iota(jnp.int32, (8, tm), 0)
    real = rowi < EXPERTS_PER_GROUP
    tops = []
    for g in range(N_EXPERT_GROUPS):
        pg = jnp.where(real, probs[8 * g:8 * g + 8, :], -0.5)
        m1 = jnp.max(pg, axis=0, keepdims=True)
        i1 = jnp.min(jnp.where(pg == m1, rowi, 8), axis=0, keepdims=True)
        pg2 = jnp.where(rowi == i1, -1.0, pg)
        m2 = jnp.max(pg2, axis=0, keepdims=True)
        i2 = jnp.min(jnp.where(pg2 == m2, rowi, 8), axis=0, keepdims=True)
        tops.append((m1, i1, m2, i2))
    best = jnp.zeros((1, tm), jnp.int32)
    best_score = tops[0][0] + tops[0][2]
    for g in range(1, N_EXPERT_GROUPS):
        score = tops[g][0] + tops[g][2]
        better = score > best_score
        best = jnp.where(better, g, best)
        best_score = jnp.where(better, score, best_score)
    m1, i1, m2, i2 = tops[0]
    for g in range(1, N_EXPERT_GROUPS):
        m1, i1, m2, i2 = (jnp.where(best == g, new, old) for new, old in zip(tops[g], (m1, i1, m2, i2)))
    tot = m1 + m2
    base = best * EXPERTS_PER_GROUP
    return base + i1, base + i2, m1 / tot, m2 / tot


def _pack_bf16_pairs(x):
    n = x.shape[1] // 2
    hi = lax.bitcast_convert_type(x[:, :n].astype(BF16).astype(F32), jnp.uint32)
    lo = lax.bitcast_convert_type(x[:, n:].astype(BF16).astype(F32), jnp.uint32)
    return hi | (lo >> 16)


def _unpack_bf16_pairs(p):
    hi = lax.bitcast_convert_type(p & jnp.uint32(0xFFFF0000), F32)
    lo = lax.bitcast_convert_type(p << 16, F32)
    return hi, lo


def _store_row_chunks(ref, packed):
    for j in range(ROW_CHUNKS):
        ref[j] = packed[:, j * 128:(j + 1) * 128]


def _load_row_chunks(ref):
    return jnp.concatenate([ref[j] for j in range(ROW_CHUNKS)], axis=-1)


def _xattn_kernel(x_ref, k_ref, v_ref, gx_ref, wq_ref, gq_ref, wo_ref, gf_ref, rw_ref, rb_ref,
                  xo_ref, hf_ref, eidx_ref, wts_ref, *, sub):
    dh = XATTN_DH
    rw = rw_ref[...]
    rw_hi, rw_lo = _split_bf16(rw)
    for s in range(x_ref.shape[0] // sub):
        rows = slice(s * sub, (s + 1) * sub)
        x = x_ref[rows, :]
        q = _dot(_rms(x, gx_ref[...]).astype(BF16), wq_ref[...])
        outs = []
        for h in range(XATTN_HEADS):
            sl = slice(h * dh, (h + 1) * dh)
            q_h = (_rms(q[:, sl], gq_ref[...]) * (dh ** -0.5)).astype(BF16)
            logits = _dot_nt(q_h, k_ref[0, :, sl])
            p = jnp.exp(logits - jnp.max(logits, axis=-1, keepdims=True))
            o = _dot(p.astype(BF16), v_ref[0, :, sl]) / jnp.sum(p, axis=-1, keepdims=True)
            outs.append(o.astype(BF16))
        xn = x + _dot(jnp.concatenate(outs, axis=-1), wo_ref[...])
        xo_ref[rows, :] = xn
        hf = _rms(xn, gf_ref[...])
        packed = _pack_bf16_pairs(hf)
        for j in range(ROW_CHUNKS):
            hf_ref[j, rows, :] = packed[:, j * 128:(j + 1) * 128]
        hf_hi, hf_lo = _split_bf16(hf)
        logits_t = _dot_nt(rw_hi, hf_hi) + _dot_nt(rw_hi, hf_lo) + _dot_nt(rw_lo, hf_hi) + rb_ref[...]
        e1, e2, w1, w2 = _route(logits_t)
        eidx_ref[:, rows] = jnp.concatenate([e1, e2, jnp.zeros((6, sub), jnp.int32)], axis=0)
        wts_ref[:, rows] = jnp.concatenate([w1, w2, jnp.zeros((6, sub), F32)], axis=0)


def _xattn(x2d, k, v, gx, wq, gq, wo, gf, rw_t, rb, *, seq, tm):
    t, d = x2d.shape
    per_b = seq // tm
    full2 = lambda i: (0, 0)
    kv_spec = pl.BlockSpec((1,) + k.shape[1:], lambda i: (i // per_b, 0, 0))
    return pl.pallas_call(
        functools.partial(_xattn_kernel, sub=min(tm, XATTN_SUB)),
        grid=(t // tm,),
        in_specs=[pl.BlockSpec((tm, d), lambda i: (i, 0)), kv_spec, kv_spec,
                  pl.BlockSpec((1, d), full2), pl.BlockSpec(wq.shape, full2),
                  pl.BlockSpec((1, XATTN_DH), full2), pl.BlockSpec(wo.shape, full2),
                  pl.BlockSpec((1, d), full2), pl.BlockSpec(rw_t.shape, full2),
                  pl.BlockSpec(rb.shape, full2)],
        out_specs=[pl.BlockSpec((tm, d), lambda i: (i, 0)),
                   pl.BlockSpec((ROW_CHUNKS, tm, 128), lambda i: (0, i, 0)),
                   pl.BlockSpec((8, tm), lambda i: (0, i)),
                   pl.BlockSpec((8, tm), lambda i: (0, i))],
        out_shape=[jax.ShapeDtypeStruct((t, d), F32),
                   jax.ShapeDtypeStruct((ROW_CHUNKS, t, 128), jnp.uint32),
                   jax.ShapeDtypeStruct((8, t), jnp.int32),
                   jax.ShapeDtypeStruct((8, t), F32)],
        compiler_params=_cparams("parallel"),
        name="xattn_router",
    )(x2d, k, v, gx, wq, gq, wo, gf, rw_t, rb)


def _moe_plan_kernel(eidx_ref, i1_ref, i2_ref, te_ref, na_ref, cnt_scr, carry_scr, *, tb, tm, plane_rows):
    ne = N_EXPERTS
    hp = lax.Precision.HIGHEST
    phase, j = pl.program_id(0), pl.program_id(1)
    rows = lax.broadcasted_iota(jnp.int32, (ne, tb), 0)
    oh1 = rows == eidx_ref[0:1, :]
    oh2 = rows == eidx_ref[1:2, :]
    a = oh1.astype(F32) + oh2.astype(F32)
    blk_cnt = jnp.broadcast_to(jnp.sum(a, axis=1, keepdims=True), cnt_scr.shape)

    @pl.when((phase == 0) & (j == 0))
    def _():
        cnt_scr[...] = jnp.zeros_like(cnt_scr)

    @pl.when(phase == 0)
    def _():
        cnt_scr[...] += blk_cnt

    @pl.when((phase == 1) & (j == 0))
    def _():
        padded = jnp.ceil(cnt_scr[...] * (1.0 / tm)) * tm
        er = lax.broadcasted_iota(jnp.int32, (ne, ne), 0)
        ec = lax.broadcasted_iota(jnp.int32, (ne, ne), 1)
        off = jnp.dot((ec < er).astype(F32), padded, precision=hp, preferred_element_type=F32)
        carry_scr[...] = off
        seg_end = (off + padded)[:, 0:1]
        tile_start = lax.broadcasted_iota(jnp.int32, (ne, te_ref.shape[1]), 1).astype(F32) * tm
        te = jnp.sum((seg_end <= tile_start).astype(F32), axis=0, keepdims=True)
        te_ref[...] = jnp.broadcast_to(jnp.minimum(te, ne - 1.0), te_ref.shape).astype(jnp.int32)
        total = jnp.sum(padded[:, 0:1], axis=0, keepdims=True)
        na_ref[...] = jnp.broadcast_to(total * (1.0 / tm), na_ref.shape).astype(jnp.int32)

    @pl.when(phase == 1)
    def _():
        before = (lax.broadcasted_iota(jnp.int32, (tb, tb), 0)
                  < lax.broadcasted_iota(jnp.int32, (tb, tb), 1)).astype(BF16)
        rank = carry_scr[:, 0:1] + _dot(a.astype(BF16), before)
        d1 = jnp.sum(jnp.where(oh1, rank, 0.0), axis=0, keepdims=True).astype(jnp.int32)
        d2 = jnp.sum(jnp.where(oh2, rank, 0.0), axis=0, keepdims=True).astype(jnp.int32)
        plane = lax.broadcasted_iota(jnp.int32, (8, tb), 0) * plane_rows
        i1_ref[...] = jnp.where(plane < ROW_CHUNKS * plane_rows, plane + d1, 0)
        i2_ref[...] = jnp.where(plane < ROW_CHUNKS * plane_rows, plane + d2, 0)
        carry_scr[...] += blk_cnt


def _moe_plan(eidx, *, tm, n_tiles, tb=512):
    t = eidx.shape[1]
    ntp = -(-n_tiles // 128) * 128
    return pl.pallas_call(
        functools.partial(_moe_plan_kernel, tb=tb, tm=tm, plane_rows=n_tiles * tm),
        grid=(2, t // tb),
        in_specs=[pl.BlockSpec((8, tb), lambda p, j: (0, j))],
        out_specs=[pl.BlockSpec((8, tb), lambda p, j: (0, j * p)),
                   pl.BlockSpec((8, tb), lambda p, j: (0, j * p)),
                   pl.BlockSpec((8, ntp), lambda p, j: (0, 0)),
                   pl.BlockSpec((8, 128), lambda p, j: (0, 0))],
        out_shape=[jax.ShapeDtypeStruct((8, t), jnp.int32),
                   jax.ShapeDtypeStruct((8, t), jnp.int32),
                   jax.ShapeDtypeStruct((8, ntp), jnp.int32),
                   jax.ShapeDtypeStruct((8, 128), jnp.int32)],
        scratch_shapes=[pltpu.VMEM((N_EXPERTS, 128), F32), pltpu.VMEM((N_EXPERTS, 128), F32)],
        compiler_params=_cparams("arbitrary", "arbitrary", vmem=VMEM_LIMIT_SMALL),
        name="moe_plan",
    )(eidx)


def _sc_mesh():
    return plsc.VectorSubcoreMesh(core_axis_name="c", subcore_axis_name="s",
                                  num_cores=SC_CORES, num_subcores=SC_SUBCORES)


def _sc_index_spec(tokens):
    nb = tokens // SC_WINDOW
    return pl.BlockSpec((1, SC_WINDOW), lambda i: (i // nb, i % nb))


def _sc_dispatch(rows, i1, i2, n_out):
    n = rows.shape[0]
    tokens = i1.shape[1]

    @functools.partial(pl.kernel, out_type=jax.ShapeDtypeStruct((n_out, 128), rows.dtype), mesh=_sc_mesh(),
                       name="moe_dispatch")
    def k(x_hbm, i1_hbm, i2_hbm, o_hbm):
        def body(x_vmem, i1_vmem, i2_vmem):
            pltpu.sync_copy(x_vmem, o_hbm.at[i1_vmem.at[0]])
            pltpu.sync_copy(x_vmem, o_hbm.at[i2_vmem.at[0]])

        pltpu.emit_pipeline(
            body, grid=(n // SC_WINDOW,),
            in_specs=[pl.BlockSpec((SC_WINDOW, 128), lambda i: (i, 0)),
                      _sc_index_spec(tokens), _sc_index_spec(tokens)],
            out_specs=[],
            core_axis_name=("c", "s"), dimension_semantics=(pltpu.PARALLEL,),
        )(x_hbm, i1_hbm, i2_hbm)

    return k(rows, i1, i2)


def _sc_collect(table, i1, i2):
    tokens = i1.shape[1]
    n = ROW_CHUNKS * tokens
    out = jax.ShapeDtypeStruct((n, 128), table.dtype)

    @functools.partial(pl.kernel, out_type=(out, out), mesh=_sc_mesh(), name="moe_collect")
    def k(t_hbm, i1_hbm, i2_hbm, o1_hbm, o2_hbm):
        def body(i1_vmem, i2_vmem, o1_vmem, o2_vmem):
            pltpu.sync_copy(t_hbm.at[i1_vmem.at[0]], o1_vmem)
            pltpu.sync_copy(t_hbm.at[i2_vmem.at[0]], o2_vmem)

        pltpu.emit_pipeline(
            body, grid=(n // SC_WINDOW,),
            in_specs=[_sc_index_spec(tokens), _sc_index_spec(tokens)],
            out_specs=[pl.BlockSpec((SC_WINDOW, 128), lambda i: (i, 0)),
                       pl.BlockSpec((SC_WINDOW, 128), lambda i: (i, 0))],
            core_axis_name=("c", "s"), dimension_semantics=(pltpu.PARALLEL,),
        )(i1_hbm, i2_hbm, o1_hbm, o2_hbm)

    return k(table, i1, i2)


def _experts_kernel(te_ref, na_ref, xs_ref, wg_ref, wu_ref, wd_ref, y_ref, wg_scr, wu_scr, wd_scr):
    i = pl.program_id(0)
    active = i < na_ref[0]

    @pl.when(active & ((i == 0) | (te_ref[i] != te_ref[jnp.maximum(i - 1, 0)])))
    def _():
        wg_scr[...] = wg_ref[0, 0].astype(BF16)
        wu_scr[...] = wu_ref[0, 0].astype(BF16)
        wd_scr[...] = wd_ref[0, 0].astype(BF16)

    @pl.when(active)
    def _():
        hi, lo = _unpack_bf16_pairs(_load_row_chunks(xs_ref))
        h = jnp.concatenate([hi, lo], axis=-1).astype(BF16)
        up = _dot(h, wg_scr[...])
        act = up * _sigmoid(up) * _dot(h, wu_scr[...])
        _store_row_chunks(y_ref, _pack_bf16_pairs(_dot(act.astype(BF16), wd_scr[...])))


def _experts(tile_expert, n_active, xs, wg, wu, wd, *, layer, tm):
    n_tiles = tile_expert.shape[0]
    _, _, d, dff = wg.shape
    rows = lambda i, te, na: (0, jnp.minimum(i, na[0] - 1), 0)
    expert = lambda i, te, na: (layer, te[i], 0, 0)
    return pl.pallas_call(
        _experts_kernel,
        grid_spec=pltpu.PrefetchScalarGridSpec(
            num_scalar_prefetch=2,
            grid=(n_tiles,),
            in_specs=[pl.BlockSpec((ROW_CHUNKS, tm, 128), rows),
                      pl.BlockSpec((1, 1, d, dff), expert),
                      pl.BlockSpec((1, 1, d, dff), expert),
                      pl.BlockSpec((1, 1, dff, d), expert)],
            out_specs=pl.BlockSpec((ROW_CHUNKS, tm, 128), rows),
            scratch_shapes=[pltpu.VMEM((d, dff), BF16), pltpu.VMEM((d, dff), BF16), pltpu.VMEM((dff, d), BF16)]),
        out_shape=jax.ShapeDtypeStruct(xs.shape, xs.dtype),
        compiler_params=_cparams("arbitrary"),
        name="moe_experts",
    )(tile_expert, n_active, xs, wg, wu, wd)


def _moe_combine_kernel(x_ref, y1_ref, y2_ref, w_ref, o_ref):
    half = x_ref.shape[1] // 2
    hi1, lo1 = _unpack_bf16_pairs(_load_row_chunks(y1_ref))
    hi2, lo2 = _unpack_bf16_pairs(_load_row_chunks(y2_ref))
    w1, w2 = w_ref[:, 0:1], w_ref[:, 1:2]
    o_ref[:, :half] = x_ref[:, :half] + w1 * hi1 + w2 * hi2
    o_ref[:, half:] = x_ref[:, half:] + w1 * lo1 + w2 * lo2


def _moe_combine(x2d, y1, y2, wcol, *, tm):
    t, d = x2d.shape
    chunk_spec = pl.BlockSpec((ROW_CHUNKS, tm, 128), lambda i: (0, i, 0))
    return pl.pallas_call(
        _moe_combine_kernel,
        grid=(t // tm,),
        in_specs=[pl.BlockSpec((tm, d), lambda i: (i, 0)), chunk_spec, chunk_spec,
                  pl.BlockSpec((tm, wcol.shape[1]), lambda i: (i, 0))],
        out_specs=pl.BlockSpec((tm, d), lambda i: (i, 0)),
        out_shape=jax.ShapeDtypeStruct((t, d), F32),
        compiler_params=_cparams("parallel", vmem=VMEM_LIMIT_SMALL),
        name="moe_combine",
    )(x2d, y1, y2, wcol)


def _moe(x2d, hf_rows, eidx, wts, wg, wu, wd, *, layer):
    t = x2d.shape[0]
    tm = MOE_TM
    n_tiles = 2 * t // tm + N_EXPERTS
    plane = n_tiles * tm
    i1, i2, te, na = _moe_plan(eidx, tm=tm, n_tiles=n_tiles)
    xs = _sc_dispatch(hf_rows.reshape(ROW_CHUNKS * t, 128), i1, i2, ROW_CHUNKS * plane)
    ys = _experts(te[0, :n_tiles], na[0, :1], xs.reshape(ROW_CHUNKS, plane, 128), wg, wu, wd,
                  layer=layer, tm=tm)
    y1, y2 = _sc_collect(ys.reshape(ROW_CHUNKS * plane, 128), i1, i2)
    return _moe_combine(x2d, y1.reshape(ROW_CHUNKS, t, 128), y2.reshape(ROW_CHUNKS, t, 128), wts[:2].T, tm=512)


def _w_in_layout_kernel(w_ref, main_ref, attn_ref):
    w = w_ref[0]
    src_if = 4 * MLSTM_W
    src_a = src_if + 2 * MLSTM_HEADS
    src_g = src_a + 3 * ATTN_W
    main_ref[0, :, OFF_MQ:OFF_GU] = w[:, 0:src_if].astype(BF16)
    main_ref[0, :, OFF_GU:OFF_IF] = w[:, src_g:src_g + OFF_IF - OFF_GU].astype(BF16)
    first = w[:, src_if:src_if + 128]
    lane = lax.broadcasted_iota(jnp.int32, first.shape, 1)
    main_ref[0, :, OFF_IF:OFF_IF + 128] = jnp.where(lane < 2 * MLSTM_HEADS, first, 0.0).astype(BF16)
    main_ref[0, :, OFF_IF + 128:N_PROJ] = jnp.zeros((w.shape[0], IF_PAD - 128), BF16)
    for g in range(len(ATTN_PATTERNS)):
        for j in range(3):
            src = src_a + j * ATTN_W + g * ATTN_GW
            dst = (3 * g + j) * ATTN_GW
            attn_ref[0, :, dst:dst + ATTN_GW] = w[:, src:src + ATTN_GW].astype(BF16)


def _w_in_layout(w_in, *, rows=256):
    depth, d, n_in = w_in.shape
    n_attn = 3 * ATTN_W
    return pl.pallas_call(
        _w_in_layout_kernel,
        grid=(depth, d // rows),
        in_specs=[pl.BlockSpec((1, rows, n_in), lambda l, i: (l, i, 0))],
        out_specs=[pl.BlockSpec((1, rows, N_PROJ), lambda l, i: (l, i, 0)),
                   pl.BlockSpec((1, rows, n_attn), lambda l, i: (l, i, 0))],
        out_shape=[jax.ShapeDtypeStruct((depth, d, N_PROJ), BF16),
                   jax.ShapeDtypeStruct((depth, d, n_attn), BF16)],
        compiler_params=_cparams("parallel", "parallel"),
        name="w_in_layout",
    )(w_in)


def kernel(x, mem, norm_mix, w_in, mlstm_conv, mlstm_gate_b, mlstm_norm, attn_qk_norm, gmlp_norm, gmlp_ws,
           gmlp_bs, w_branch_a, w_branch_b, w_branch_c, w_out, rel_bias, norm_xattn, norm_mem, w_xq, w_xkv,
           xattn_qk_norm, w_xo, norm_ffn, router_w, router_b, w_expert_gate, w_expert_up, w_expert_down):
    b, s, d = x.shape
    t = b * s
    depth = w_in.shape[0]
    x2d = x.reshape(t, d)

    biases = [_attn_bias(rel_bias, g) for g in range(len(ATTN_PATTERNS))]
    rw_t = jnp.zeros((N_EXPERT_GROUPS, 8, d), F32).at[:, :EXPERTS_PER_GROUP].set(
        router_w.T.reshape(N_EXPERT_GROUPS, EXPERTS_PER_GROUP, d)).reshape(ROUTER_ROWS, d)
    rb = jnp.full((N_EXPERT_GROUPS, 8), NEG, F32).at[:, :EXPERTS_PER_GROUP].set(
        router_b.astype(F32).reshape(N_EXPERT_GROUPS, EXPERTS_PER_GROUP)).reshape(ROUTER_ROWS, 1)
    tril = jnp.tril(jnp.ones((GMLP_CHUNK, GMLP_CHUNK), bool))
    head_of = jnp.arange(ATTN_GW) // ATTN_DH
    seg_ones = (head_of[:, None] == head_of[None, :]).astype(BF16)

    w_main, w_attn = _w_in_layout(w_in)

    for l in range(depth):
        proj, h_mix = _inproj(x2d, norm_mix[l][None], w_main, layer=l, tm=1024, tn=3200)
        gq = jnp.tile(attn_qk_norm[l, 0], HEADS_PER_GROUP)[None]
        gk = jnp.tile(attn_qk_norm[l, 1], HEADS_PER_GROUP)[None]

        nh = MLSTM_HEADS
        gates_row = proj[:, OFF_IF:OFF_IF + 2 * nh].astype(F32).reshape(b, s, 2 * nh).transpose(0, 2, 1)
        pad_rows = jnp.zeros((b, 8 - nh, s), F32)
        gates_i = jnp.concatenate([gates_row[:, :nh], pad_rows], axis=1)
        gates_f = jnp.concatenate([gates_row[:, nh:], pad_rows], axis=1)
        bias_i = jnp.zeros((8, 1), F32).at[:nh, 0].set(mlstm_gate_b[l, :nh])
        bias_f = jnp.zeros((8, 1), F32).at[:nh, 0].set(mlstm_gate_b[l, nh:])
        ya = _mlstm_rows(proj, gates_i, gates_f, mlstm_conv[l], bias_i, bias_f, mlstm_norm[l][None],
                         batch=b, seq=s, blk=MLSTM_BLOCK, group=MLSTM_GROUP)

        ybs, lses = [], []
        for g, (_, dilation) in enumerate(ATTN_PATTERNS):
            aproj = _attnproj(h_mix, w_attn, seg_ones, gq, gk, layer=l, group=g, dilation=dilation)
            o, lse = _dattn(aproj, biases[g], seq=s, group=g, dilation=dilation)
            ybs.append(o)
            lses.append(lse)

        ws = jnp.where(tril, gmlp_ws[l], 0.0).astype(BF16)
        bsb = jnp.broadcast_to(gmlp_bs[l][:, :, None], (GMLP_GROUPS, GMLP_CHUNK, GMLP_GC)).astype(F32)
        x2d = _merge(ya, ybs, lses, proj, x2d, w_branch_a[l].astype(BF16), w_branch_b[l].astype(BF16),
                     w_branch_c[l].astype(BF16), w_out[l].astype(BF16), ws, bsb, gmlp_norm[l][None], tm=512)

        k_mem, v_mem = _memkv(mem, norm_mem[l][None], w_xkv[l].astype(BF16), xattn_qk_norm[l, 1][None])
        x2d, hf_rows, eidx, wts = _xattn(x2d, k_mem, v_mem, norm_xattn[l][None], w_xq[l].astype(BF16),
                                         xattn_qk_norm[l, 0][None], w_xo[l].astype(BF16), norm_ffn[l][None],
                                         rw_t, rb, seq=s, tm=1024)

        x2d = _moe(x2d, hf_rows, eidx, wts, w_expert_gate, w_expert_up, w_expert_down, layer=l)

    return x2d.reshape(b, s, d)
```

```python
import functools
import math

import jax
import jax.numpy as jnp
import numpy as np
from jax import lax
from jax.experimental import pallas as pl
from jax.experimental.pallas import tpu as pltpu
from jax.experimental.pallas import tpu_sc as plsc

F32 = jnp.float32
BF16 = jnp.bfloat16

EPS = 1e-6
NEG = -1e30

MLSTM_HEADS = 4
MLSTM_DH = 128
MLSTM_W = MLSTM_HEADS * MLSTM_DH
CONV_WIDTH = 4
MLSTM_BLOCK = 128
MLSTM_NSUB = 1
MLSTM_GROUP = 4

ATTN_PATTERNS = ((128, 1), (512, 4), (2048, 16))
HEADS_PER_GROUP = 4
ATTN_DH = 64
ATTN_GW = HEADS_PER_GROUP * ATTN_DH
ATTN_W = len(ATTN_PATTERNS) * ATTN_GW
ATTN_BLOCK = 128
REL_BUCKETS = 32
REL_MAX_DIST = 2048

GMLP_GROUPS = 4
GMLP_GC = 128
GMLP_W = GMLP_GROUPS * GMLP_GC
GMLP_CHUNK = 128

XATTN_HEADS = 4
XATTN_DH = 128
XATTN_W = XATTN_HEADS * XATTN_DH
XATTN_SUB = 1024

N_EXPERTS = 16
N_EXPERT_GROUPS = 4
EXPERTS_PER_GROUP = 4
ROUTER_ROWS = 8 * N_EXPERT_GROUPS

N_BRANCH = 3

MOE_TM = 1024
ROW_CHUNKS = 4
SC_CORES, SC_SUBCORES = 2, 16
SC_WINDOW = 128

OFF_MQ, OFF_MK, OFF_MV, OFF_MO = 0, 512, 1024, 1536
OFF_GU, OFF_GV = 2048, 2560
OFF_GATE = 3072
OFF_IF = 6144
IF_PAD = 256
N_PROJ = OFF_IF + IF_PAD

ATTN_TILE = 2048
ATTN_SUB = ATTN_TILE // ATTN_BLOCK
ATTN_SLAB = 2 * ATTN_DH
ATTN_COLS = HEADS_PER_GROUP * ATTN_SLAB + 2 * ATTN_GW

VMEM_LIMIT = 48 * 1024 * 1024
VMEM_LIMIT_INPROJ = 56 * 1024 * 1024
VMEM_LIMIT_SMALL = 24 * 1024 * 1024


def _cparams(*sem, vmem=VMEM_LIMIT):
    return pltpu.CompilerParams(dimension_semantics=sem, vmem_limit_bytes=vmem)


def _rms(x, gain):
    return x * lax.rsqrt(jnp.mean(x * x, axis=-1, keepdims=True) + EPS) * gain


def _sigmoid(x):
    return 0.5 * jnp.tanh(0.5 * x) + 0.5


def _dot(a, b):
    return jnp.dot(a, b, preferred_element_type=F32)


def _dot_nt(a, b):
    return lax.dot_general(a, b, (((1,), (1,)), ((), ())), preferred_element_type=F32)


def _inproj_kernel(x_ref, g_ref, w_ref, o_ref, h_ref):
    @pl.when(pl.program_id(1) == 0)
    def _():
        h_ref[...] = _rms(x_ref[...], g_ref[...]).astype(BF16)

    o_ref[...] = _dot_nt(h_ref[...], w_ref[0]).astype(o_ref.dtype)


def _inproj(x2d, gain, w, *, layer, tm, tn):
    t, d = x2d.shape
    n = w.shape[1]
    return pl.pallas_call(
        _inproj_kernel,
        grid=(t // tm, n // tn),
        in_specs=[pl.BlockSpec((tm, d), lambda i, j: (i, 0)),
                  pl.BlockSpec((1, d), lambda i, j: (0, 0)),
                  pl.BlockSpec((1, tn, d), lambda i, j: (layer, j, 0))],
        out_specs=[pl.BlockSpec((tm, tn), lambda i, j: (i, j)),
                   pl.BlockSpec((tm, d), lambda i, j: (i, 0))],
        out_shape=[jax.ShapeDtypeStruct((t, n), BF16), jax.ShapeDtypeStruct((t, d), BF16)],
        compiler_params=_cparams("parallel", "arbitrary", vmem=VMEM_LIMIT_INPROJ),
        name="inproj",
    )(x2d, gain, w)


def _log_sigmoid(x):
    return jnp.minimum(x, 0.0) - jnp.log(1.0 + jnp.exp(-jnp.abs(x)))


def _mlstm_kernel(qk_ref, v_ref, og_ref, gc_ref, gr_ref, cw_ref, gbc_ref, gbr_ref, ng_ref, y_ref,
                  xe_scr, s_scr, m_scr, *, blk, nsub, group):
    heads, w = MLSTM_HEADS, MLSTM_W

    @pl.when(pl.program_id(1) == 0)
    def _():
        xe_scr[:, 0:8, :] = jnp.zeros((group, 8, 2 * w), F32)
        s_scr[...] = jnp.zeros_like(s_scr)
        m_scr[...] = jnp.zeros_like(m_scr)

    cw = cw_ref[...]
    ri = lax.broadcasted_iota(jnp.int32, (blk, blk), 0)
    ci = lax.broadcasted_iota(jnp.int32, (blk, blk), 1)
    causal = ri >= ci
    tril = causal.astype(BF16)
    triu = (ri <= ci).astype(BF16)
    states = []
    for g in range(group):
        xe_scr[g, 8:8 + nsub * blk, :] = qk_ref[g].astype(F32)
        states.append([(s_scr[g, h], m_scr[g, h:h + 1, 0:1]) for h in range(heads)])
    for c in range(nsub):
        for g in range(group):
            states[g] = _mlstm_chunk(c * blk, blk, states[g], cw, causal, tril, triu, xe_scr.at[g], v_ref.at[g],
                                     og_ref.at[g], gc_ref.at[g], gr_ref.at[g], gbc_ref, gbr_ref, ng_ref,
                                     y_ref.at[g])
    for g in range(group):
        xe_scr[g, 0:8, :] = xe_scr[g, nsub * blk:nsub * blk + 8, :]
        for h, (s_st, m_st) in enumerate(states[g]):
            s_scr[g, h] = s_st
            m_scr[g, h:h + 1, :] = jnp.broadcast_to(m_st, (1, m_scr.shape[2]))


def _split_bf16(x):
    hi = x.astype(BF16)
    return hi, (x - hi.astype(F32)).astype(BF16)


def _mlstm_chunk(r0, blk, state, cw, causal, tril, triu, xe_scr, v_ref, og_ref, gc_ref, gr_ref, gbc_ref,
                 gbr_ref, ng_ref, y_ref):
    heads, dh, w = MLSTM_HEADS, MLSTM_DH, MLSTM_W
    rows = slice(r0, r0 + blk)
    conv = cw[CONV_WIDTH - 1:CONV_WIDTH, :] * xe_scr[8 + r0:8 + r0 + blk, :]
    for j in range(CONV_WIDTH - 1):
        off = 8 + r0 - (CONV_WIDTH - 1) + j
        conv = conv + cw[j:j + 1, :] * xe_scr[off:off + blk, :]
    qk = conv * _sigmoid(conv)

    gcol = gc_ref[rows, :].astype(F32) + gbc_ref[...]
    grow = gr_ref[:, rows] + gbr_ref[...]
    lc_hi, lc_lo = _split_bf16(_log_sigmoid(gcol))
    lr_hi, lr_lo = _split_bf16(_log_sigmoid(grow))
    bcol = _dot(tril, lc_hi) + _dot(tril, lc_lo)
    brow = _dot(lr_hi, triu) + _dot(lr_lo, triu)
    ones = jnp.ones((blk, dh), BF16)

    new_state = []
    for h in range(heads):
        sl = slice(h * dh, (h + 1) * dh)
        b_c = bcol[:, heads + h:heads + h + 1]
        i_c = gcol[:, h:h + 1]
        b_r = brow[heads + h:heads + h + 1, :]
        i_r = grow[h:h + 1, :]
        s_st, m_st = state[h]

        d_mat = jnp.where(causal, b_c - b_r + i_r, NEG)
        inter = b_c + m_st
        m_t = jnp.maximum(inter, jnp.max(d_mat, axis=-1, keepdims=True))
        w_intra = jnp.exp(d_mat - m_t)
        w_inter = jnp.exp(inter - m_t)

        q_f = qk[:, sl]
        k_f = qk[:, w + h * dh:w + (h + 1) * dh] * (dh ** -0.5)
        q_b = q_f.astype(BF16)
        k_b = k_f.astype(BF16)
        v_ext = jnp.concatenate([v_ref[rows, sl], ones], axis=-1)

        s = _dot_nt(q_b, k_b) * w_intra
        tot = _dot(s.astype(BF16), v_ext) + w_inter * _dot(q_b, s_st.astype(BF16))
        num, den = tot[:, :dh], tot[:, dh:]
        hh = num / jnp.maximum(jnp.abs(den), jnp.exp(-m_t))
        hn = _rms(hh, ng_ref[:, sl])
        y_ref[rows, sl] = (hn * _sigmoid(og_ref[rows, sl].astype(F32))).astype(y_ref.dtype)

        b_last = b_c[blk - 1:blk, :]
        dec = b_last - b_c + i_c
        m_new = jnp.maximum(b_last + m_st, jnp.max(dec, axis=0, keepdims=True))
        w_k = jnp.exp(dec - m_new)
        w_c = jnp.exp(b_last + m_st - m_new)
        kw = k_f * w_k
        new_state.append((w_c * s_st + _dot(kw.T.astype(BF16), v_ext), m_new))
    return new_state


def _mlstm(proj, gates_row, conv_w, gb_col, gb_row, norm_g, *, batch, seq, blk, nsub, group):
    t, npj = proj.shape
    rows = blk * nsub
    w = MLSTM_W
    proj3 = proj.reshape(batch, seq, npj)
    cols = lambda c: (lambda b, i: (b, i, c))
    const2 = lambda b, i: (0, 0)
    y = pl.pallas_call(
        functools.partial(_mlstm_kernel, blk=blk, nsub=nsub, group=group),
        grid=(batch // group, seq // rows),
        in_specs=[pl.BlockSpec((group, rows, 2 * w), cols(OFF_MQ // (2 * w))),
                  pl.BlockSpec((group, rows, w), cols(OFF_MV // w)),
                  pl.BlockSpec((group, rows, w), cols(OFF_MO // w)),
                  pl.BlockSpec((group, rows, IF_PAD), cols(OFF_IF // IF_PAD)),
                  pl.BlockSpec((group, 8, rows), lambda b, i: (b, 0, i)),
                  pl.BlockSpec((CONV_WIDTH, 2 * w), const2),
                  pl.BlockSpec((1, IF_PAD), const2),
                  pl.BlockSpec((8, 1), const2),
                  pl.BlockSpec((1, w), const2)],
        out_specs=pl.BlockSpec((group, rows, w), cols(0)),
        out_shape=jax.ShapeDtypeStruct((batch, seq, w), BF16),
        scratch_shapes=[pltpu.VMEM((group, rows + 8, 2 * w), F32),
                        pltpu.VMEM((group, MLSTM_HEADS, MLSTM_DH, 2 * MLSTM_DH), F32),
                        pltpu.VMEM((group, 8, 128), F32)],
        compiler_params=_cparams("parallel", "arbitrary"),
        name="mlstm",
    )(proj3, proj3, proj3, proj3, gates_row, conv_w, gb_col, gb_row, norm_g)
    return y.reshape(t, w)


def _prefix_max(x):
    n = x.shape[1]
    lane = lax.broadcasted_iota(jnp.int32, x.shape, 1)
    shift = 1
    while shift < n:
        x = jnp.maximum(x, jnp.where(lane >= shift, pltpu.roll(x, shift, 1), NEG))
        shift *= 2
    return x


def _mlstm_rows_kernel(qk_ref, v_ref, og_ref, gi_ref, gf_ref, cw_ref, bi_ref, bf_ref, ng_ref, y_ref,
                       xe_scr, s_scr, m_scr, *, blk, group):
    heads, dh, w = MLSTM_HEADS, MLSTM_DH, MLSTM_W

    @pl.when(pl.program_id(1) == 0)
    def _():
        xe_scr[:, 0:8, :] = jnp.zeros((group, 8, 2 * w), F32)
        s_scr[...] = jnp.zeros_like(s_scr)
        m_scr[...] = jnp.zeros_like(m_scr)

    cw = cw_ref[...]
    causal = lax.broadcasted_iota(jnp.int32, (blk, blk), 0) >= lax.broadcasted_iota(jnp.int32, (blk, blk), 1)
    triu = (lax.broadcasted_iota(jnp.int32, (blk, blk), 0)
            <= lax.broadcasted_iota(jnp.int32, (blk, blk), 1)).astype(BF16)
    ones = jnp.ones((blk, dh), BF16)
    s_in = [[s_scr[g, h] for h in range(heads)] for g in range(group)]
    m_in = [m_scr[g, :, 0:1] for g in range(group)]
    s_out = [[None] * heads for _ in range(group)]
    m_out = [None] * group
    per_seq = []
    for g in range(group):
        xe_scr[g, 8:8 + blk, :] = qk_ref[g].astype(F32)
        conv = cw[CONV_WIDTH - 1:CONV_WIDTH, :] * xe_scr[g, 8:8 + blk, :]
        for j in range(CONV_WIDTH - 1):
            off = 8 - (CONV_WIDTH - 1) + j
            conv = conv + cw[j:j + 1, :] * xe_scr[g, off:off + blk, :]
        xe_scr[g, 0:8, :] = xe_scr[g, blk:blk + 8, :]
        qk = conv * _sigmoid(conv)

        i_r = gi_ref[g] + bi_ref[...]
        lf_hi, lf_lo = _split_bf16(_log_sigmoid(gf_ref[g] + bf_ref[...]))
        b_r = _dot(lf_hi, triu) + _dot(lf_lo, triu)
        m_st = m_in[g]
        a_r = i_r - b_r
        inter = b_r + m_st
        m_t = jnp.maximum(inter, b_r + _prefix_max(a_r))
        b_last = b_r[:, blk - 1:blk]
        dec = b_last - b_r + i_r
        m_new = jnp.maximum(b_last + m_st, jnp.max(dec, axis=1, keepdims=True))
        w_c = jnp.exp(b_last + m_st - m_new)
        m_out[g] = m_new
        pack = jnp.concatenate([b_r - m_t, jnp.exp(inter - m_t), jnp.exp(-m_t), jnp.exp(dec - m_new),
                                jnp.zeros((blk - 32, blk), F32)], axis=0)
        per_seq.append((qk, a_r, pack.T, w_c))

    chains = [(g, h) for h in range(heads) for g in range(group)]
    st = {}
    for g, h in chains:
        qk = per_seq[g][0]
        sl = slice(h * dh, (h + 1) * dh)
        q_b = qk[:, sl].astype(BF16)
        k_f = qk[:, w + h * dh:w + (h + 1) * dh] * (dh ** -0.5)
        v_ext = jnp.concatenate([v_ref[g, :, sl], ones], axis=-1)
        st[g, h] = (q_b, k_f, v_ext, _dot_nt(q_b, k_f.astype(BF16)), _dot(q_b, s_in[g][h].astype(BF16)))
    for g, h in chains:
        q_b, k_f, v_ext, qk_t, q_state = st[g, h]
        _, a_r, cols, _ = per_seq[g]
        u_c, w_inter = cols[:, h:h + 1], cols[:, 8 + h:9 + h]
        w_intra = jnp.exp(jnp.where(causal, u_c + a_r[h:h + 1, :], NEG))
        st[g, h] = (k_f, v_ext, _dot((qk_t * w_intra).astype(BF16), v_ext) + w_inter * q_state)
    for g, h in chains:
        k_f, v_ext, tot = st[g, h]
        _, _, cols, w_c = per_seq[g]
        em_c, w_k = cols[:, 16 + h:17 + h], cols[:, 24 + h:25 + h]
        sl = slice(h * dh, (h + 1) * dh)
        num, den = tot[:, :dh], tot[:, dh:]
        hh = num / jnp.maximum(jnp.abs(den), em_c)
        hn = _rms(hh, ng_ref[:, sl])
        y_ref[g, :, sl] = (hn * _sigmoid(og_ref[g, :, sl].astype(F32))).astype(y_ref.dtype)
        s_out[g][h] = w_c[h:h + 1, :] * s_in[g][h] + _dot((k_f * w_k).T.astype(BF16), v_ext)
    for g in range(group):
        m_scr[g] = jnp.broadcast_to(m_out[g], m_scr.shape[1:])
        for h in range(heads):
            s_scr[g, h] = s_out[g][h]


def _mlstm_rows(proj, gates_i, gates_f, conv_w, bias_i, bias_f, norm_g, *, batch, seq, blk, group):
    t, npj = proj.shape
    w = MLSTM_W
    proj3 = proj.reshape(batch, seq, npj)
    cols = lambda c: (lambda b, i: (b, i, c))
    const2 = lambda b, i: (0, 0)
    gate_spec = pl.BlockSpec((group, 8, blk), lambda b, i: (b, 0, i))
    y = pl.pallas_call(
        functools.partial(_mlstm_rows_kernel, blk=blk, group=group),
        grid=(batch // group, seq // blk),
        in_specs=[pl.BlockSpec((group, blk, 2 * w), cols(OFF_MQ // (2 * w))),
                  pl.BlockSpec((group, blk, w), cols(OFF_MV // w)),
                  pl.BlockSpec((group, blk, w), cols(OFF_MO // w)),
                  gate_spec, gate_spec,
                  pl.BlockSpec((CONV_WIDTH, 2 * w), const2),
                  pl.BlockSpec((8, 1), const2), pl.BlockSpec((8, 1), const2),
                  pl.BlockSpec((1, w), const2)],
        out_specs=pl.BlockSpec((group, blk, w), cols(0)),
        out_shape=jax.ShapeDtypeStruct((batch, seq, w), BF16),
        scratch_shapes=[pltpu.VMEM((group, blk + 8, 2 * w), F32),
                        pltpu.VMEM((group, MLSTM_HEADS, MLSTM_DH, 2 * MLSTM_DH), F32),
                        pltpu.VMEM((group, 8, 128), F32)],
        compiler_params=_cparams("parallel", "arbitrary", vmem=VMEM_LIMIT_SMALL),
        name="mlstm",
    )(proj3, proj3, proj3, gates_i, gates_f, conv_w, bias_i, bias_f, norm_g)
    return y.reshape(t, w)


def _attnproj_kernel(h_ref, w_ref, seg_ref, gq_ref, gk_ref, o_ref, r_scr, *, dil):
    gw, half = ATTN_GW, ATTN_SLAB // 2
    sub_rows = r_scr.shape[2]
    seg, sub_seg = ATTN_TILE // dil, sub_rows // dil

    def head_norm(x, gain):
        sq = x * x
        hi = sq.astype(BF16)
        lo = (sq - hi.astype(F32)).astype(BF16)
        ss = _dot(hi, seg_ref[...]) + _dot(lo, seg_ref[...])
        return x * lax.rsqrt(ss * (1.0 / ATTN_DH) + EPS) * gain

    low = lax.broadcasted_iota(jnp.int32, (1, ATTN_SLAB), 1) < half
    for s in range(ATTN_TILE // sub_rows):
        rows = slice(s * sub_rows, (s + 1) * sub_rows)
        res = _dot_nt(h_ref[rows, :], w_ref[0])
        q = head_norm(res[:, :gw], gq_ref[...]) * (ATTN_DH ** -0.5)
        k = head_norm(res[:, gw:2 * gw], gk_ref[...])
        slabs = []
        for pair in range(gw // ATTN_SLAB):
            qp = q[:, pair * ATTN_SLAB:(pair + 1) * ATTN_SLAB]
            slabs += [jnp.where(low, qp, 0.0), jnp.where(low, 0.0, qp)]
        slabs += [k[:, c * 128:(c + 1) * 128] for c in range(gw // 128)]
        slabs += [res[:, 2 * gw + c * 128:2 * gw + (c + 1) * 128] for c in range(gw // 128)]
        for c, slab in enumerate(slabs):
            if dil == 1:
                o_ref[rows, c * 128:(c + 1) * 128] = slab.astype(o_ref.dtype)
            else:
                r_scr[s % 2, c] = slab
        if dil > 1:
            for r in range(dil):
                dst = slice(r * seg + s * sub_seg, r * seg + (s + 1) * sub_seg)
                for c in range(r_scr.shape[1]):
                    o_ref[dst, c * 128:(c + 1) * 128] = (
                        r_scr[s % 2, c, pl.ds(r, sub_seg, stride=dil), :].astype(o_ref.dtype))


def _attnproj(h, w, seg_ones, gq, gk, *, layer, group, dilation):
    t, d = h.shape
    wcols = 3 * ATTN_GW
    const2 = lambda i: (0, 0)
    return pl.pallas_call(
        functools.partial(_attnproj_kernel, dil=dilation),
        grid=(t // ATTN_TILE,),
        in_specs=[pl.BlockSpec((ATTN_TILE, d), lambda i: (i, 0)),
                  pl.BlockSpec((1, wcols, d), lambda i: (layer, group, 0)),
                  pl.BlockSpec((ATTN_GW, ATTN_GW), const2),
                  pl.BlockSpec((1, ATTN_GW), const2), pl.BlockSpec((1, ATTN_GW), const2)],
        out_specs=pl.BlockSpec((ATTN_TILE, ATTN_COLS), lambda i: (i, 0)),
        out_shape=jax.ShapeDtypeStruct((t, ATTN_COLS), BF16),
        scratch_shapes=[pltpu.VMEM((2, ATTN_COLS // 128, 512, 128), F32)],
        compiler_params=_cparams("parallel"),
        name=f"attnproj{group}",
    )(h, w, seg_ones, gq, gk)


def _dattn_kernel(q_ref, kc_ref, kp_ref, vc_ref, vp_ref, bias_ref, o_ref, lse_ref,
                  kx_scr, vx_scr, o_scr, l_scr, *, dil):
    blk = ATTN_BLOCK
    per = ATTN_SUB // dil
    first_tile = pl.program_id(1) == 0
    for r in range(dil):
        base = r * (per + 1) * blk
        last = slice((r * per + per - 1) * blk, (r * per + per) * blk)
        mine = slice(r * per * blk, (r + 1) * per * blk)
        kx_scr[base:base + blk, :] = kp_ref[last, :]
        vx_scr[base:base + blk, :] = vp_ref[last, :]
        kx_scr[base + blk:base + (per + 1) * blk, :] = kc_ref[mine, :]
        vx_scr[base + blk:base + (per + 1) * blk, :] = vc_ref[mine, :]

    low = lax.broadcasted_iota(jnp.int32, (1, ATTN_SLAB), 1) < ATTN_SLAB // 2
    no_prev = lax.broadcasted_iota(jnp.int32, (1, 2 * blk), 1) < blk
    for r in range(dil):
        for sub in range(per):
            u = r * per + sub
            win = slice((r * (per + 1) + sub) * blk, (r * (per + 1) + sub + 2) * blk)
            o_slabs, l_slabs = [], []
            for pair in range(ATTN_GW // ATTN_SLAB):
                cols = slice(pair * ATTN_SLAB, (pair + 1) * ATTN_SLAB)
                kx, vx = kx_scr[win, cols], vx_scr[win, cols]
                o_pair, l_pair = [], []
                for h in (2 * pair, 2 * pair + 1):
                    logits = _dot_nt(q_ref[u * blk:(u + 1) * blk, h * ATTN_SLAB:(h + 1) * ATTN_SLAB], kx)
                    logits = logits + bias_ref[h]
                    if sub == 0:
                        logits = jnp.where(first_tile & no_prev, NEG, logits)
                    m = jnp.max(logits, axis=-1, keepdims=True)
                    p = jnp.exp(logits - m)
                    l = jnp.sum(p, axis=-1, keepdims=True)
                    o_pair.append(_dot(p.astype(BF16), vx) / l)
                    l_pair.append(m + jnp.log(l))
                o_slabs.append(jnp.where(low, o_pair[0], o_pair[1]))
                l_slabs.append(jnp.where(low, l_pair[0], l_pair[1]))
            dst = pl.ds(sub * blk * dil + r, blk, stride=dil) if dil > 1 else slice(u * blk, (u + 1) * blk)
            for c in range(ATTN_GW // ATTN_SLAB):
                o_scr[c, dst, :] = o_slabs[c]
                l_scr[c, dst, :] = l_slabs[c]
    for c in range(ATTN_GW // ATTN_SLAB):
        o_ref[:, c * ATTN_SLAB:(c + 1) * ATTN_SLAB] = o_scr[c].astype(o_ref.dtype)
        lse_ref[:, c * ATTN_SLAB:(c + 1) * ATTN_SLAB] = l_scr[c]


def _dattn(aproj, bias, *, seq, group, dilation):
    t = aproj.shape[0]
    tiles = seq // ATTN_TILE
    qw = HEADS_PER_GROUP * ATTN_SLAB
    cq, ck, cv = 0, qw // ATTN_GW, qw // ATTN_GW + 1
    blk = (ATTN_TILE, ATTN_GW)
    cur = lambda c: (lambda b, j: (b * tiles + j, c))
    prev = lambda c: (lambda b, j: (b * tiles + jnp.maximum(j - 1, 0), c))
    xrows = ATTN_TILE + dilation * ATTN_BLOCK
    return pl.pallas_call(
        functools.partial(_dattn_kernel, dil=dilation),
        grid=(t // seq, tiles),
        in_specs=[pl.BlockSpec((ATTN_TILE, qw), cur(cq)),
                  pl.BlockSpec(blk, cur(ck)), pl.BlockSpec(blk, prev(ck)),
                  pl.BlockSpec(blk, cur(cv)), pl.BlockSpec(blk, prev(cv)),
                  pl.BlockSpec((HEADS_PER_GROUP, ATTN_BLOCK, 2 * ATTN_BLOCK), lambda b, j: (0, 0, 0))],
        out_specs=[pl.BlockSpec(blk, cur(0)), pl.BlockSpec(blk, cur(0))],
        out_shape=[jax.ShapeDtypeStruct((t, ATTN_GW), BF16), jax.ShapeDtypeStruct((t, ATTN_GW), F32)],
        scratch_shapes=[pltpu.VMEM((xrows, ATTN_GW), BF16), pltpu.VMEM((xrows, ATTN_GW), BF16),
                        pltpu.VMEM((ATTN_GW // ATTN_SLAB, ATTN_TILE, ATTN_SLAB), F32),
                        pltpu.VMEM((ATTN_GW // ATTN_SLAB, ATTN_TILE, ATTN_SLAB), F32)],
        compiler_params=_cparams("parallel", "arbitrary"),
        name=f"dattn{group}",
    )(aproj, aproj, aproj, aproj, aproj, bias)


def _rel_bucket(n):
    max_exact = REL_BUCKETS // 2
    nf = jnp.maximum(n, 1).astype(F32)
    log_b = max_exact + (jnp.log(nf / max_exact) / math.log(REL_MAX_DIST / max_exact)
                         * (REL_BUCKETS - max_exact)).astype(jnp.int32)
    return jnp.where(n < max_exact, n, jnp.minimum(log_b, REL_BUCKETS - 1))


def _attn_bias(rel_bias, group):
    window, dilation = ATTN_PATTERNS[group]
    steps = window // dilation
    hp = lax.Precision.HIGHEST
    hs = slice(group * HEADS_PER_GROUP, (group + 1) * HEADS_PER_GROUP)
    bucket = _rel_bucket(jnp.arange(steps + 1) * dilation)
    bias_steps = jnp.dot(jax.nn.one_hot(bucket, REL_BUCKETS, dtype=F32), rel_bias[:, hs].astype(F32),
                         precision=hp)
    qi = jnp.arange(ATTN_BLOCK)[:, None]
    ki = jnp.arange(2 * ATTN_BLOCK)[None, :]
    dist = ATTN_BLOCK + qi - ki
    ok = (dist >= 0) & (dist <= steps)
    sel = jax.nn.one_hot(jnp.clip(dist, 0, steps).reshape(-1), steps + 1, dtype=F32)
    bias = jnp.dot(sel, bias_steps, precision=hp).T.reshape(HEADS_PER_GROUP, ATTN_BLOCK, 2 * ATTN_BLOCK)
    return jnp.where(ok[None], bias, NEG)


def _merge_kernel(ya_ref, yb0_ref, yb1_ref, yb2_ref, l0_ref, l1_ref, l2_ref, gu_ref, gv_ref, gate_ref,
                  x_ref, wa_ref, wb_ref, wc_ref, wo_ref, ws_ref, bs_ref, gg_ref, o_ref, yc_scr, *, tm):
    d = x_ref.shape[1]
    l0, l1, l2 = l0_ref[...], l1_ref[...], l2_ref[...]
    mx = jnp.maximum(jnp.maximum(l0, l1), l2)
    e0, e1, e2 = jnp.exp(l0 - mx), jnp.exp(l1 - mx), jnp.exp(l2 - mx)
    inv = 1.0 / (e0 + e1 + e2)
    yb = jnp.concatenate([(yb0_ref[...].astype(F32) * (e0 * inv)).astype(BF16),
                          (yb1_ref[...].astype(F32) * (e1 * inv)).astype(BF16),
                          (yb2_ref[...].astype(F32) * (e2 * inv)).astype(BF16)], axis=-1)

    for j in range(tm // GMLP_CHUNK):
        rows = slice(j * GMLP_CHUNK, (j + 1) * GMLP_CHUNK)
        for g in range(GMLP_GROUPS):
            cols = slice(g * GMLP_GC, (g + 1) * GMLP_GC)
            u = jax.nn.gelu(gu_ref[rows, cols].astype(F32))
            v = _rms(jax.nn.gelu(gv_ref[rows, cols].astype(F32)), gg_ref[:, cols])
            mixed = _dot(ws_ref[g], v.astype(BF16)) + bs_ref[g]
            yc_scr[rows, cols] = (u * mixed).astype(BF16)

    def gate2(k):
        return jnp.tanh(0.5 * gate_ref[:, k * d:(k + 1) * d].astype(F32)) + 1.0

    merged2 = gate2(0) * _dot(ya_ref[...], wa_ref[...])
    merged2 = merged2 + gate2(1) * _dot(yb, wb_ref[...])
    merged2 = merged2 + gate2(2) * _dot(yc_scr[...], wc_ref[...])
    o_ref[...] = x_ref[...] + 0.5 * _dot(merged2.astype(BF16), wo_ref[...])


def _merge(ya, ybs, lses, proj, x2d, wa, wb, wc, wo, ws, bsb, gg, *, tm):
    t, d = x2d.shape
    row = lambda c: (lambda i: (i, c))
    full2 = lambda i: (0, 0)
    full3 = lambda i: (0, 0, 0)
    gspec = pl.BlockSpec((tm, ATTN_GW), row(0))
    return pl.pallas_call(
        functools.partial(_merge_kernel, tm=tm),
        grid=(t // tm,),
        in_specs=[pl.BlockSpec((tm, MLSTM_W), row(0)),
                  gspec, gspec, gspec, gspec, gspec, gspec,
                  pl.BlockSpec((tm, GMLP_W), row(OFF_GU // GMLP_W)),
                  pl.BlockSpec((tm, GMLP_W), row(OFF_GV // GMLP_W)),
                  pl.BlockSpec((tm, N_BRANCH * d), row(OFF_GATE // (N_BRANCH * d))),
                  pl.BlockSpec((tm, d), row(0)),
                  pl.BlockSpec(wa.shape, full2), pl.BlockSpec(wb.shape, full2),
                  pl.BlockSpec(wc.shape, full2), pl.BlockSpec(wo.shape, full2),
                  pl.BlockSpec(ws.shape, full3), pl.BlockSpec(bsb.shape, full3),
                  pl.BlockSpec(gg.shape, full2)],
        out_specs=pl.BlockSpec((tm, d), row(0)),
        out_shape=jax.ShapeDtypeStruct((t, d), F32),
        scratch_shapes=[pltpu.VMEM((tm, GMLP_W), BF16)],
        compiler_params=_cparams("parallel"),
        name="merge",
    )(ya, *ybs, *lses, proj, proj, proj, x2d, wa, wb, wc, wo, ws, bsb, gg)


def _memkv_kernel(mem_ref, g_ref, w_ref, gk_ref, k_ref, v_ref):
    dh, w = XATTN_DH, XATTN_W
    kv = _dot(_rms(mem_ref[0], g_ref[...]).astype(BF16), w_ref[...])
    for h in range(XATTN_HEADS):
        sl = slice(h * dh, (h + 1) * dh)
        k_ref[0, :, sl] = _rms(kv[:, sl], gk_ref[...]).astype(k_ref.dtype)
    v_ref[0] = kv[:, w:].astype(v_ref.dtype)


def _memkv(mem, gain, w_kv, gk):
    b, m, d = mem.shape
    full2 = lambda i: (0, 0)
    return pl.pallas_call(
        _memkv_kernel,
        grid=(b,),
        in_specs=[pl.BlockSpec((1, m, d), lambda i: (i, 0, 0)),
                  pl.BlockSpec((1, d), full2),
                  pl.BlockSpec(w_kv.shape, full2),
                  pl.BlockSpec((1, XATTN_DH), full2)],
        out_specs=[pl.BlockSpec((1, m, XATTN_W), lambda i: (i, 0, 0)),
                   pl.BlockSpec((1, m, XATTN_W), lambda i: (i, 0, 0))],
        out_shape=[jax.ShapeDtypeStruct((b, m, XATTN_W), BF16),
                   jax.ShapeDtypeStruct((b, m, XATTN_W), BF16)],
        compiler_params=_cparams("parallel", vmem=VMEM_LIMIT_SMALL),
        name="memkv",
    )(mem, gain, w_kv, gk)


def _route(logits):
    tm = logits.shape[1]
    e = jnp.exp(logits - jnp.max(logits, axis=0, keepdims=True))
    probs = e / jnp.sum(e, axis=0, keepdims=True)
    rowi = lax.broadcasted_iota(jnp.int32, (8, tm), 0)
    real = rowi < EXPERTS_PER_GROUP
    tops = []
    for g in range(N_EXPERT_GROUPS):
        pg = jnp.where(real, probs[8 * g:8 * g + 8, :], -0.5)
        m1 = jnp.max(pg, axis=0, keepdims=True)
        i1 = jnp.min(jnp.where(pg == m1, rowi, 8), axis=0, keepdims=True)
        pg2 = jnp.where(rowi == i1, -1.0, pg)
        m2 = jnp.max(pg2, axis=0, keepdims=True)
        i2 = jnp.min(jnp.where(pg2 == m2, rowi, 8), axis=0, keepdims=True)
        tops.append((m1, i1, m2, i2))
    best = jnp.zeros((1, tm), jnp.int32)
    best_score = tops[0][0] + tops[0][2]
    for g in range(1, N_EXPERT_GROUPS):
        score = tops[g][0] + tops[g][2]
        better = score > best_score
        best = jnp.where(better, g, best)
        best_score = jnp.where(better, score, best_score)
    m1, i1, m2, i2 = tops[0]
    for g in range(1, N_EXPERT_GROUPS):
        m1, i1, m2, i2 = (jnp.where(best == g, new, old) for new, old in zip(tops[g], (m1, i1, m2, i2)))
    tot = m1 + m2
    base = best * EXPERTS_PER_GROUP
    return base + i1, base + i2, m1 / tot, m2 / tot


def _pack_bf16_pairs(x):
    n = x.shape[1] // 2
    hi = lax.bitcast_convert_type(x[:, :n].astype(BF16).astype(F32), jnp.uint32)
    lo = lax.bitcast_convert_type(x[:, n:].astype(BF16).astype(F32), jnp.uint32)
    return hi | (lo >> 16)


def _unpack_bf16_pairs(p):
    hi = lax.bitcast_convert_type(p & jnp.uint32(0xFFFF0000), F32)
    lo = lax.bitcast_convert_type(p << 16, F32)
    return hi, lo


def _store_row_chunks(ref, packed):
    for j in range(ROW_CHUNKS):
        ref[j] = packed[:, j * 128:(j + 1) * 128]


def _load_row_chunks(ref):
    return jnp.concatenate([ref[j] for j in range(ROW_CHUNKS)], axis=-1)


def _xattn_kernel(x_ref, k_ref, v_ref, gx_ref, wq_ref, gq_ref, wo_ref, gf_ref, rw_ref, rb_ref,
                  xo_ref, hf_ref, eidx_ref, wts_ref, *, sub):
    dh = XATTN_DH
    rw = rw_ref[...]
    rw_hi, rw_lo = _split_bf16(rw)
    for s in range(x_ref.shape[0] // sub):
        rows = slice(s * sub, (s + 1) * sub)
        x = x_ref[rows, :]
        q = _dot(_rms(x, gx_ref[...]).astype(BF16), wq_ref[...])
        outs = []
        for h in range(XATTN_HEADS):
            sl = slice(h * dh, (h + 1) * dh)
            q_h = (_rms(q[:, sl], gq_ref[...]) * (dh ** -0.5)).astype(BF16)
            logits = _dot_nt(q_h, k_ref[0, :, sl])
            p = jnp.exp(logits - jnp.max(logits, axis=-1, keepdims=True))
            o = _dot(p.astype(BF16), v_ref[0, :, sl]) / jnp.sum(p, axis=-1, keepdims=True)
            outs.append(o.astype(BF16))
        xn = x + _dot(jnp.concatenate(outs, axis=-1), wo_ref[...])
        xo_ref[rows, :] = xn
        hf = _rms(xn, gf_ref[...])
        packed = _pack_bf16_pairs(hf)
        for j in range(ROW_CHUNKS):
            hf_ref[j, rows, :] = packed[:, j * 128:(j + 1) * 128]
        hf_hi, hf_lo = _split_bf16(hf)
        logits_t = _dot_nt(rw_hi, hf_hi) + _dot_nt(rw_hi, hf_lo) + _dot_nt(rw_lo, hf_hi) + rb_ref[...]
        e1, e2, w1, w2 = _route(logits_t)
        eidx_ref[:, rows] = jnp.concatenate([e1, e2, jnp.zeros((6, sub), jnp.int32)], axis=0)
        wts_ref[:, rows] = jnp.concatenate([w1, w2, jnp.zeros((6, sub), F32)], axis=0)


def _xattn(x2d, k, v, gx, wq, gq, wo, gf, rw_t, rb, *, seq, tm):
    t, d = x2d.shape
    per_b = seq // tm
    full2 = lambda i: (0, 0)
    kv_spec = pl.BlockSpec((1,) + k.shape[1:], lambda i: (i // per_b, 0, 0))
    return pl.pallas_call(
        functools.partial(_xattn_kernel, sub=min(tm, XATTN_SUB)),
        grid=(t // tm,),
        in_specs=[pl.BlockSpec((tm, d), lambda i: (i, 0)), kv_spec, kv_spec,
                  pl.BlockSpec((1, d), full2), pl.BlockSpec(wq.shape, full2),
                  pl.BlockSpec((1, XATTN_DH), full2), pl.BlockSpec(wo.shape, full2),
                  pl.BlockSpec((1, d), full2), pl.BlockSpec(rw_t.shape, full2),
                  pl.BlockSpec(rb.shape, full2)],
        out_specs=[pl.BlockSpec((tm, d), lambda i: (i, 0)),
                   pl.BlockSpec((ROW_CHUNKS, tm, 128), lambda i: (0, i, 0)),
                   pl.BlockSpec((8, tm), lambda i: (0, i)),
                   pl.BlockSpec((8, tm), lambda i: (0, i))],
        out_shape=[jax.ShapeDtypeStruct((t, d), F32),
                   jax.ShapeDtypeStruct((ROW_CHUNKS, t, 128), jnp.uint32),
                   jax.ShapeDtypeStruct((8, t), jnp.int32),
                   jax.ShapeDtypeStruct((8, t), F32)],
        compiler_params=_cparams("parallel"),
        name="xattn_router",
    )(x2d, k, v, gx, wq, gq, wo, gf, rw_t, rb)


def _moe_plan_kernel(eidx_ref, i1_ref, i2_ref, te_ref, na_ref, cnt_scr, carry_scr, *, tb, tm, plane_rows):
    ne = N_EXPERTS
    hp = lax.Precision.HIGHEST
    phase, j = pl.program_id(0), pl.program_id(1)
    rows = lax.broadcasted_iota(jnp.int32, (ne, tb), 0)
    oh1 = rows == eidx_ref[0:1, :]
    oh2 = rows == eidx_ref[1:2, :]
    a = oh1.astype(F32) + oh2.astype(F32)
    blk_cnt = jnp.broadcast_to(jnp.sum(a, axis=1, keepdims=True), cnt_scr.shape)

    @pl.when((phase == 0) & (j == 0))
    def _():
        cnt_scr[...] = jnp.zeros_like(cnt_scr)

    @pl.when(phase == 0)
    def _():
        cnt_scr[...] += blk_cnt

    @pl.when((phase == 1) & (j == 0))
    def _():
        padded = jnp.ceil(cnt_scr[...] * (1.0 / tm)) * tm
        er = lax.broadcasted_iota(jnp.int32, (ne, ne), 0)
        ec = lax.broadcasted_iota(jnp.int32, (ne, ne), 1)
        off = jnp.dot((ec < er).astype(F32), padded, precision=hp, preferred_element_type=F32)
        carry_scr[...] = off
        seg_end = (off + padded)[:, 0:1]
        tile_start = lax.broadcasted_iota(jnp.int32, (ne, te_ref.shape[1]), 1).astype(F32) * tm
        te = jnp.sum((seg_end <= tile_start).astype(F32), axis=0, keepdims=True)
        te_ref[...] = jnp.broadcast_to(jnp.minimum(te, ne - 1.0), te_ref.shape).astype(jnp.int32)
        total = jnp.sum(padded[:, 0:1], axis=0, keepdims=True)
        na_ref[...] = jnp.broadcast_to(total * (1.0 / tm), na_ref.shape).astype(jnp.int32)

    @pl.when(phase == 1)
    def _():
        before = (lax.broadcasted_iota(jnp.int32, (tb, tb), 0)
                  < lax.broadcasted_iota(jnp.int32, (tb, tb), 1)).astype(BF16)
        rank = carry_scr[:, 0:1] + _dot(a.astype(BF16), before)
        d1 = jnp.sum(jnp.where(oh1, rank, 0.0), axis=0, keepdims=True).astype(jnp.int32)
        d2 = jnp.sum(jnp.where(oh2, rank, 0.0), axis=0, keepdims=True).astype(jnp.int32)
        plane = lax.broadcasted_iota(jnp.int32, (8, tb), 0) * plane_rows
        i1_ref[...] = jnp.where(plane < ROW_CHUNKS * plane_rows, plane + d1, 0)
        i2_ref[...] = jnp.where(plane < ROW_CHUNKS * plane_rows, plane + d2, 0)
        carry_scr[...] += blk_cnt


def _moe_plan(eidx, *, tm, n_tiles, tb=512):
    t = eidx.shape[1]
    ntp = -(-n_tiles // 128) * 128
    return pl.pallas_call(
        functools.partial(_moe_plan_kernel, tb=tb, tm=tm, plane_rows=n_tiles * tm),
        grid=(2, t // tb),
        in_specs=[pl.BlockSpec((8, tb), lambda p, j: (0, j))],
        out_specs=[pl.BlockSpec((8, tb), lambda p, j: (0, j * p)),
                   pl.BlockSpec((8, tb), lambda p, j: (0, j * p)),
                   pl.BlockSpec((8, ntp), lambda p, j: (0, 0)),
                   pl.BlockSpec((8, 128), lambda p, j: (0, 0))],
        out_shape=[jax.ShapeDtypeStruct((8, t), jnp.int32),
                   jax.ShapeDtypeStruct((8, t), jnp.int32),
                   jax.ShapeDtypeStruct((8, ntp), jnp.int32),
                   jax.ShapeDtypeStruct((8, 128), jnp.int32)],
        scratch_shapes=[pltpu.VMEM((N_EXPERTS, 128), F32), pltpu.VMEM((N_EXPERTS, 128), F32)],
        compiler_params=_cparams("arbitrary", "arbitrary", vmem=VMEM_LIMIT_SMALL),
        name="moe_plan",
    )(eidx)


def _sc_mesh():
    return plsc.VectorSubcoreMesh(core_axis_name="c", subcore_axis_name="s",
                                  num_cores=SC_CORES, num_subcores=SC_SUBCORES)


def _sc_index_spec(tokens):
    nb = tokens // SC_WINDOW
    return pl.BlockSpec((1, SC_WINDOW), lambda i: (i // nb, i % nb))


def _sc_dispatch(rows, i1, i2, n_out):
    n = rows.shape[0]
    tokens = i1.shape[1]

    @functools.partial(pl.kernel, out_type=jax.ShapeDtypeStruct((n_out, 128), rows.dtype), mesh=_sc_mesh(),
                       name="moe_dispatch")
    def k(x_hbm, i1_hbm, i2_hbm, o_hbm):
        def body(x_vmem, i1_vmem, i2_vmem):
            pltpu.sync_copy(x_vmem, o_hbm.at[i1_vmem.at[0]])
            pltpu.sync_copy(x_vmem, o_hbm.at[i2_vmem.at[0]])

        pltpu.emit_pipeline(
            body, grid=(n // SC_WINDOW,),
            in_specs=[pl.BlockSpec((SC_WINDOW, 128), lambda i: (i, 0)),
                      _sc_index_spec(tokens), _sc_index_spec(tokens)],
            out_specs=[],
            core_axis_name=("c", "s"), dimension_semantics=(pltpu.PARALLEL,),
        )(x_hbm, i1_hbm, i2_hbm)

    return k(rows, i1, i2)


def _sc_collect(table, i1, i2):
    tokens = i1.shape[1]
    n = ROW_CHUNKS * tokens
    out = jax.ShapeDtypeStruct((n, 128), table.dtype)

    @functools.partial(pl.kernel, out_type=(out, out), mesh=_sc_mesh(), name="moe_collect")
    def k(t_hbm, i1_hbm, i2_hbm, o1_hbm, o2_hbm):
        def body(i1_vmem, i2_vmem, o1_vmem, o2_vmem):
            pltpu.sync_copy(t_hbm.at[i1_vmem.at[0]], o1_vmem)
            pltpu.sync_copy(t_hbm.at[i2_vmem.at[0]], o2_vmem)

        pltpu.emit_pipeline(
            body, grid=(n // SC_WINDOW,),
            in_specs=[_sc_index_spec(tokens), _sc_index_spec(tokens)],
            out_specs=[pl.BlockSpec((SC_WINDOW, 128), lambda i: (i, 0)),
                       pl.BlockSpec((SC_WINDOW, 128), lambda i: (i, 0))],
            core_axis_name=("c", "s"), dimension_semantics=(pltpu.PARALLEL,),
        )(i1_hbm, i2_hbm, o1_hbm, o2_hbm)

    return k(table, i1, i2)


def _experts_kernel(te_ref, na_ref, xs_ref, wg_ref, wu_ref, wd_ref, y_ref, wg_scr, wu_scr, wd_scr):
    i = pl.program_id(0)
    active = i < na_ref[0]

    @pl.when(active & ((i == 0) | (te_ref[i] != te_ref[jnp.maximum(i - 1, 0)])))
    def _():
        wg_scr[...] = wg_ref[0, 0].astype(BF16)
        wu_scr[...] = wu_ref[0, 0].astype(BF16)
        wd_scr[...] = wd_ref[0, 0].astype(BF16)

    @pl.when(active)
    def _():
        hi, lo = _unpack_bf16_pairs(_load_row_chunks(xs_ref))
        h = jnp.concatenate([hi, lo], axis=-1).astype(BF16)
        up = _dot(h, wg_scr[...])
        act = up * _sigmoid(up) * _dot(h, wu_scr[...])
        _store_row_chunks(y_ref, _pack_bf16_pairs(_dot(act.astype(BF16), wd_scr[...])))


def _experts(tile_expert, n_active, xs, wg, wu, wd, *, layer, tm):
    n_tiles = tile_expert.shape[0]
    _, _, d, dff = wg.shape
    rows = lambda i, te, na: (0, jnp.minimum(i, na[0] - 1), 0)
    expert = lambda i, te, na: (layer, te[i], 0, 0)
    return pl.pallas_call(
        _experts_kernel,
        grid_spec=pltpu.PrefetchScalarGridSpec(
            num_scalar_prefetch=2,
            grid=(n_tiles,),
            in_specs=[pl.BlockSpec((ROW_CHUNKS, tm, 128), rows),
                      pl.BlockSpec((1, 1, d, dff), expert),
                      pl.BlockSpec((1, 1, d, dff), expert),
                      pl.BlockSpec((1, 1, dff, d), expert)],
            out_specs=pl.BlockSpec((ROW_CHUNKS, tm, 128), rows),
            scratch_shapes=[pltpu.VMEM((d, dff), BF16), pltpu.VMEM((d, dff), BF16), pltpu.VMEM((dff, d), BF16)]),
        out_shape=jax.ShapeDtypeStruct(xs.shape, xs.dtype),
        compiler_params=_cparams("arbitrary"),
        name="moe_experts",
    )(tile_expert, n_active, xs, wg, wu, wd)


def _moe_combine_kernel(x_ref, y1_ref, y2_ref, w_ref, o_ref):
    half = x_ref.shape[1] // 2
    hi1, lo1 = _unpack_bf16_pairs(_load_row_chunks(y1_ref))
    hi2, lo2 = _unpack_bf16_pairs(_load_row_chunks(y2_ref))
    w1, w2 = w_ref[:, 0:1], w_ref[:, 1:2]
    o_ref[:, :half] = x_ref[:, :half] + w1 * hi1 + w2 * hi2
    o_ref[:, half:] = x_ref[:, half:] + w1 * lo1 + w2 * lo2


def _moe_combine(x2d, y1, y2, wcol, *, tm):
    t, d = x2d.shape
    chunk_spec = pl.BlockSpec((ROW_CHUNKS, tm, 128), lambda i: (0, i, 0))
    return pl.pallas_call(
        _moe_combine_kernel,
        grid=(t // tm,),
        in_specs=[pl.BlockSpec((tm, d), lambda i: (i, 0)), chunk_spec, chunk_spec,
                  pl.BlockSpec((tm, wcol.shape[1]), lambda i: (i, 0))],
        out_specs=pl.BlockSpec((tm, d), lambda i: (i, 0)),
        out_shape=jax.ShapeDtypeStruct((t, d), F32),
        compiler_params=_cparams("parallel", vmem=VMEM_LIMIT_SMALL),
        name="moe_combine",
    )(x2d, y1, y2, wcol)


def _moe(x2d, hf_rows, eidx, wts, wg, wu, wd, *, layer):
    t = x2d.shape[0]
    tm = MOE_TM
    n_tiles = 2 * t // tm + N_EXPERTS
    plane = n_tiles * tm
    i1, i2, te, na = _moe_plan(eidx, tm=tm, n_tiles=n_tiles)
    xs = _sc_dispatch(hf_rows.reshape(ROW_CHUNKS * t, 128), i1, i2, ROW_CHUNKS * plane)
    ys = _experts(te[0, :n_tiles], na[0, :1], xs.reshape(ROW_CHUNKS, plane, 128), wg, wu, wd,
                  layer=layer, tm=tm)
    y1, y2 = _sc_collect(ys.reshape(ROW_CHUNKS * plane, 128), i1, i2)
    return _moe_combine(x2d, y1.reshape(ROW_CHUNKS, t, 128), y2.reshape(ROW_CHUNKS, t, 128), wts[:2].T, tm=512)


W_ROWS = 256


def _w_rows_kernel(start_ref, valid_ref, w_ref, o_ref):
    del start_ref
    row = lax.broadcasted_iota(jnp.int32, w_ref.shape[1:], 0)
    o_ref[0] = jnp.where(row < valid_ref[pl.program_id(1)], w_ref[0], 0.0).astype(o_ref.dtype)


def _w_rows(w_t, starts, valid):
    depth, _, d = w_t.shape
    nblk = len(starts)
    return pl.pallas_call(
        _w_rows_kernel,
        grid_spec=pltpu.PrefetchScalarGridSpec(
            num_scalar_prefetch=2,
            grid=(depth, nblk),
            in_specs=[pl.BlockSpec((pl.Element(1), pl.Element(W_ROWS), pl.Element(d)),
                                   lambda l, c, st, va: (l, pl.multiple_of(st[c], 8), 0))],
            out_specs=pl.BlockSpec((1, W_ROWS, d), lambda l, c, st, va: (l, c, 0))),
        out_shape=jax.ShapeDtypeStruct((depth, nblk * W_ROWS, d), BF16),
        compiler_params=_cparams("parallel", "arbitrary", vmem=VMEM_LIMIT_SMALL),
        name="w_in_rows",
    )(jnp.asarray(starts, jnp.int32), jnp.asarray(valid, jnp.int32), w_t)


def _w_in_layout(w_in):
    w_t = jnp.swapaxes(w_in, 1, 2)
    src_if = 4 * MLSTM_W
    src_a = src_if + 2 * MLSTM_HEADS
    src_g = src_a + 3 * ATTN_W
    starts = list(range(0, src_if, W_ROWS)) + [src_g + k * W_ROWS for k in range((OFF_IF - OFF_GU) // W_ROWS)]
    valid = [W_ROWS] * len(starts)
    starts.append(src_if)
    valid.append(2 * MLSTM_HEADS)
    assert len(starts) * W_ROWS == N_PROJ and ATTN_GW == W_ROWS
    a_starts = [src_a + j * ATTN_W + g * ATTN_GW for g in range(len(ATTN_PATTERNS)) for j in range(3)]
    return _w_rows(w_t, starts, valid), _w_rows(w_t, a_starts, [W_ROWS] * len(a_starts))


def kernel(x, mem, norm_mix, w_in, mlstm_conv, mlstm_gate_b, mlstm_norm, attn_qk_norm, gmlp_norm, gmlp_ws,
           gmlp_bs, w_branch_a, w_branch_b, w_branch_c, w_out, rel_bias, norm_xattn, norm_mem, w_xq, w_xkv,
           xattn_qk_norm, w_xo, norm_ffn, router_w, router_b, w_expert_gate, w_expert_up, w_expert_down):
    b, s, d = x.shape
    t = b * s
    depth = w_in.shape[0]
    x2d = x.reshape(t, d)

    biases = [_attn_bias(rel_bias, g) for g in range(len(ATTN_PATTERNS))]
    rw_t = jnp.zeros((N_EXPERT_GROUPS, 8, d), F32).at[:, :EXPERTS_PER_GROUP].set(
        router_w.T.reshape(N_EXPERT_GROUPS, EXPERTS_PER_GROUP, d)).reshape(ROUTER_ROWS, d)
    rb = jnp.full((N_EXPERT_GROUPS, 8), NEG, F32).at[:, :EXPERTS_PER_GROUP].set(
        router_b.astype(F32).reshape(N_EXPERT_GROUPS, EXPERTS_PER_GROUP)).reshape(ROUTER_ROWS, 1)
    tril = jnp.tril(jnp.ones((GMLP_CHUNK, GMLP_CHUNK), bool))
    head_of = jnp.arange(ATTN_GW) // ATTN_DH
    seg_ones = (head_of[:, None] == head_of[None, :]).astype(BF16)

    w_main, w_attn = _w_in_layout(w_in)

    for l in range(depth):
        proj, h_mix = _inproj(x2d, norm_mix[l][None], w_main, layer=l, tm=1024, tn=3200)
        gq = jnp.tile(attn_qk_norm[l, 0], HEADS_PER_GROUP)[None]
        gk = jnp.tile(attn_qk_norm[l, 1], HEADS_PER_GROUP)[None]

        nh = MLSTM_HEADS
        gates_row = proj[:, OFF_IF:OFF_IF + 2 * nh].astype(F32).reshape(b, s, 2 * nh).transpose(0, 2, 1)
        pad_rows = jnp.zeros((b, 8 - nh, s), F32)
        gates_i = jnp.concatenate([gates_row[:, :nh], pad_rows], axis=1)
        gates_f = jnp.concatenate([gates_row[:, nh:], pad_rows], axis=1)
        bias_i = jnp.zeros((8, 1), F32).at[:nh, 0].set(mlstm_gate_b[l, :nh])
        bias_f = jnp.zeros((8, 1), F32).at[:nh, 0].set(mlstm_gate_b[l, nh:])
        ya = _mlstm_rows(proj, gates_i, gates_f, mlstm_conv[l], bias_i, bias_f, mlstm_norm[l][None],
                         batch=b, seq=s, blk=MLSTM_BLOCK, group=MLSTM_GROUP)

        ybs, lses = [], []
        for g, (_, dilation) in enumerate(ATTN_PATTERNS):
            aproj = _attnproj(h_mix, w_attn, seg_ones, gq, gk, layer=l, group=g, dilation=dilation)
            o, lse = _dattn(aproj, biases[g], seq=s, group=g, dilation=dilation)
            ybs.append(o)
            lses.append(lse)

        ws = jnp.where(tril, gmlp_ws[l], 0.0).astype(BF16)
        bsb = jnp.broadcast_to(gmlp_bs[l][:, :, None], (GMLP_GROUPS, GMLP_CHUNK, GMLP_GC)).astype(F32)
        x2d = _merge(ya, ybs, lses, proj, x2d, w_branch_a[l].astype(BF16), w_branch_b[l].astype(BF16),
                     w_branch_c[l].astype(BF16), w_out[l].astype(BF16), ws, bsb, gmlp_norm[l][None], tm=512)

        k_mem, v_mem = _memkv(mem, norm_mem[l][None], w_xkv[l].astype(BF16), xattn_qk_norm[l, 1][None])
        x2d, hf_rows, eidx, wts = _xattn(x2d, k_mem, v_mem, norm_xattn[l][None], w_xq[l].astype(BF16),
                                         xattn_qk_norm[l, 0][None], w_xo[l].astype(BF16), norm_ffn[l][None],
                                         rw_t, rb, seq=s, tm=1024)

        x2d = _moe(x2d, hf_rows, eidx, wts, w_expert_gate, w_expert_up, w_expert_down, layer=l)

    return x2d.reshape(b, s, d)
```

```python
import functools
import math

import jax
import jax.numpy as jnp
import numpy as np
from jax import lax
from jax.experimental import pallas as pl
from jax.experimental.pallas import tpu as pltpu
from jax.experimental.pallas import tpu_sc as plsc

F32 = jnp.float32
BF16 = jnp.bfloat16

EPS = 1e-6
NEG = -1e30

MLSTM_HEADS = 4
MLSTM_DH = 128
MLSTM_W = MLSTM_HEADS * MLSTM_DH
CONV_WIDTH = 4
MLSTM_BLOCK = 128
MLSTM_NSUB = 1
MLSTM_GROUP = 4

ATTN_PATTERNS = ((128, 1), (512, 4), (2048, 16))
HEADS_PER_GROUP = 4
ATTN_DH = 64
ATTN_GW = HEADS_PER_GROUP * ATTN_DH
ATTN_W = len(ATTN_PATTERNS) * ATTN_GW
ATTN_BLOCK = 128
REL_BUCKETS = 32
REL_MAX_DIST = 2048

GMLP_GROUPS = 4
GMLP_GC = 128
GMLP_W = GMLP_GROUPS * GMLP_GC
GMLP_CHUNK = 128

XATTN_HEADS = 4
XATTN_DH = 128
XATTN_W = XATTN_HEADS * XATTN_DH
XATTN_SUB = 1024

N_EXPERTS = 16
N_EXPERT_GROUPS = 4
EXPERTS_PER_GROUP = 4
ROUTER_ROWS = 8 * N_EXPERT_GROUPS

N_BRANCH = 3

MOE_TM = 1024
ROW_CHUNKS = 4
SC_CORES, SC_SUBCORES = 2, 16
SC_WINDOW = 128

OFF_MQ, OFF_MK, OFF_MV, OFF_MO = 0, 512, 1024, 1536
OFF_GU, OFF_GV = 2048, 2560
OFF_GATE = 3072
OFF_IF = 6144
IF_PAD = 256
N_PROJ = OFF_IF + IF_PAD

ATTN_TILE = 2048
ATTN_SUB = ATTN_TILE // ATTN_BLOCK
ATTN_SLAB = 2 * ATTN_DH
ATTN_COLS = HEADS_PER_GROUP * ATTN_SLAB + 2 * ATTN_GW

VMEM_LIMIT = 48 * 1024 * 1024
VMEM_LIMIT_INPROJ = 56 * 1024 * 1024
VMEM_LIMIT_SMALL = 24 * 1024 * 1024


def _cparams(*sem, vmem=VMEM_LIMIT):
    return pltpu.CompilerParams(dimension_semantics=sem, vmem_limit_bytes=vmem)


def _rms(x, gain):
    return x * lax.rsqrt(jnp.mean(x * x, axis=-1, keepdims=True) + EPS) * gain


def _sigmoid(x):
    return 0.5 * jnp.tanh(0.5 * x) + 0.5


def _dot(a, b):
    return jnp.dot(a, b, preferred_element_type=F32)


def _dot_nt(a, b):
    return lax.dot_general(a, b, (((1,), (1,)), ((), ())), preferred_element_type=F32)


def _inproj_kernel(x_ref, g_ref, w_ref, wg_ref, o_ref, h_ref, gt_ref):
    @pl.when(pl.program_id(1) == 0)
    def _():
        h = _rms(x_ref[...], g_ref[...]).astype(BF16)
        h_ref[...] = h
        gt_ref[...] = _dot_nt(wg_ref[0, 0:128, :], h)[:gt_ref.shape[0], :]

    o_ref[...] = _dot_nt(h_ref[...], w_ref[0]).astype(o_ref.dtype)


def _inproj(x2d, gain, w, *, layer, tm, tn):
    t, d = x2d.shape
    n = OFF_IF
    return pl.pallas_call(
        _inproj_kernel,
        grid=(t // tm, n // tn),
        in_specs=[pl.BlockSpec((tm, d), lambda i, j: (i, 0)),
                  pl.BlockSpec((1, d), lambda i, j: (0, 0)),
                  pl.BlockSpec((1, tn, d), lambda i, j: (layer, j, 0)),
                  pl.BlockSpec((1, IF_PAD, d), lambda i, j: (layer, OFF_IF // IF_PAD, 0))],
        out_specs=[pl.BlockSpec((tm, tn), lambda i, j: (i, j)),
                   pl.BlockSpec((tm, d), lambda i, j: (i, 0)),
                   pl.BlockSpec((8, tm), lambda i, j: (0, i))],
        out_shape=[jax.ShapeDtypeStruct((t, n), BF16), jax.ShapeDtypeStruct((t, d), BF16),
                   jax.ShapeDtypeStruct((8, t), F32)],
        compiler_params=_cparams("parallel", "arbitrary", vmem=VMEM_LIMIT_INPROJ),
        name="inproj",
    )(x2d, gain, w, w)


def _log_sigmoid(x):
    return jnp.minimum(x, 0.0) - jnp.log(1.0 + jnp.exp(-jnp.abs(x)))


def _mlstm_kernel(qk_ref, v_ref, og_ref, gc_ref, gr_ref, cw_ref, gbc_ref, gbr_ref, ng_ref, y_ref,
                  xe_scr, s_scr, m_scr, *, blk, nsub, group):
    heads, w = MLSTM_HEADS, MLSTM_W

    @pl.when(pl.program_id(1) == 0)
    def _():
        xe_scr[:, 0:8, :] = jnp.zeros((group, 8, 2 * w), F32)
        s_scr[...] = jnp.zeros_like(s_scr)
        m_scr[...] = jnp.zeros_like(m_scr)

    cw = cw_ref[...]
    ri = lax.broadcasted_iota(jnp.int32, (blk, blk), 0)
    ci = lax.broadcasted_iota(jnp.int32, (blk, blk), 1)
    causal = ri >= ci
    tril = causal.astype(BF16)
    triu = (ri <= ci).astype(BF16)
    states = []
    for g in range(group):
        xe_scr[g, 8:8 + nsub * blk, :] = qk_ref[g].astype(F32)
        states.append([(s_scr[g, h], m_scr[g, h:h + 1, 0:1]) for h in range(heads)])
    for c in range(nsub):
        for g in range(group):
            states[g] = _mlstm_chunk(c * blk, blk, states[g], cw, causal, tril, triu, xe_scr.at[g], v_ref.at[g],
                                     og_ref.at[g], gc_ref.at[g], gr_ref.at[g], gbc_ref, gbr_ref, ng_ref,
                                     y_ref.at[g])
    for g in range(group):
        xe_scr[g, 0:8, :] = xe_scr[g, nsub * blk:nsub * blk + 8, :]
        for h, (s_st, m_st) in enumerate(states[g]):
            s_scr[g, h] = s_st
            m_scr[g, h:h + 1, :] = jnp.broadcast_to(m_st, (1, m_scr.shape[2]))


def _split_bf16(x):
    hi = x.astype(BF16)
    return hi, (x - hi.astype(F32)).astype(BF16)


def _mlstm_chunk(r0, blk, state, cw, causal, tril, triu, xe_scr, v_ref, og_ref, gc_ref, gr_ref, gbc_ref,
                 gbr_ref, ng_ref, y_ref):
    heads, dh, w = MLSTM_HEADS, MLSTM_DH, MLSTM_W
    rows = slice(r0, r0 + blk)
    conv = cw[CONV_WIDTH - 1:CONV_WIDTH, :] * xe_scr[8 + r0:8 + r0 + blk, :]
    for j in range(CONV_WIDTH - 1):
        off = 8 + r0 - (CONV_WIDTH - 1) + j
        conv = conv + cw[j:j + 1, :] * xe_scr[off:off + blk, :]
    qk = conv * _sigmoid(conv)

    gcol = gc_ref[rows, :].astype(F32) + gbc_ref[...]
    grow = gr_ref[:, rows] + gbr_ref[...]
    lc_hi, lc_lo = _split_bf16(_log_sigmoid(gcol))
    lr_hi, lr_lo = _split_bf16(_log_sigmoid(grow))
    bcol = _dot(tril, lc_hi) + _dot(tril, lc_lo)
    brow = _dot(lr_hi, triu) + _dot(lr_lo, triu)
    ones = jnp.ones((blk, dh), BF16)

    new_state = []
    for h in range(heads):
        sl = slice(h * dh, (h + 1) * dh)
        b_c = bcol[:, heads + h:heads + h + 1]
        i_c = gcol[:, h:h + 1]
        b_r = brow[heads + h:heads + h + 1, :]
        i_r = grow[h:h + 1, :]
        s_st, m_st = state[h]

        d_mat = jnp.where(causal, b_c - b_r + i_r, NEG)
        inter = b_c + m_st
        m_t = jnp.maximum(inter, jnp.max(d_mat, axis=-1, keepdims=True))
        w_intra = jnp.exp(d_mat - m_t)
        w_inter = jnp.exp(inter - m_t)

        q_f = qk[:, sl]
        k_f = qk[:, w + h * dh:w + (h + 1) * dh] * (dh ** -0.5)
        q_b = q_f.astype(BF16)
        k_b = k_f.astype(BF16)
        v_ext = jnp.concatenate([v_ref[rows, sl], ones], axis=-1)

        s = _dot_nt(q_b, k_b) * w_intra
        tot = _dot(s.astype(BF16), v_ext) + w_inter * _dot(q_b, s_st.astype(BF16))
        num, den = tot[:, :dh], tot[:, dh:]
        hh = num / jnp.maximum(jnp.abs(den), jnp.exp(-m_t))
        hn = _rms(hh, ng_ref[:, sl])
        y_ref[rows, sl] = (hn * _sigmoid(og_ref[rows, sl].astype(F32))).astype(y_ref.dtype)

        b_last = b_c[blk - 1:blk, :]
        dec = b_last - b_c + i_c
        m_new = jnp.maximum(b_last + m_st, jnp.max(dec, axis=0, keepdims=True))
        w_k = jnp.exp(dec - m_new)
        w_c = jnp.exp(b_last + m_st - m_new)
        kw = k_f * w_k
        new_state.append((w_c * s_st + _dot(kw.T.astype(BF16), v_ext), m_new))
    return new_state


def _mlstm(proj, gates_row, conv_w, gb_col, gb_row, norm_g, *, batch, seq, blk, nsub, group):
    t, npj = proj.shape
    rows = blk * nsub
    w = MLSTM_W
    proj3 = proj.reshape(batch, seq, npj)
    cols = lambda c: (lambda b, i: (b, i, c))
    const2 = lambda b, i: (0, 0)
    y = pl.pallas_call(
        functools.partial(_mlstm_kernel, blk=blk, nsub=nsub, group=group),
        grid=(batch // group, seq // rows),
        in_specs=[pl.BlockSpec((group, rows, 2 * w), cols(OFF_MQ // (2 * w))),
                  pl.BlockSpec((group, rows, w), cols(OFF_MV // w)),
                  pl.BlockSpec((group, rows, w), cols(OFF_MO // w)),
                  pl.BlockSpec((group, rows, IF_PAD), cols(OFF_IF // IF_PAD)),
                  pl.BlockSpec((group, 8, rows), lambda b, i: (b, 0, i)),
                  pl.BlockSpec((CONV_WIDTH, 2 * w), const2),
                  pl.BlockSpec((1, IF_PAD), const2),
                  pl.BlockSpec((8, 1), const2),
                  pl.BlockSpec((1, w), const2)],
        out_specs=pl.BlockSpec((group, rows, w), cols(0)),
        out_shape=jax.ShapeDtypeStruct((batch, seq, w), BF16),
        scratch_shapes=[pltpu.VMEM((group, rows + 8, 2 * w), F32),
                        pltpu.VMEM((group, MLSTM_HEADS, MLSTM_DH, 2 * MLSTM_DH), F32),
                        pltpu.VMEM((group, 8, 128), F32)],
        compiler_params=_cparams("parallel", "arbitrary"),
        name="mlstm",
    )(proj3, proj3, proj3, proj3, gates_row, conv_w, gb_col, gb_row, norm_g)
    return y.reshape(t, w)


def _prefix_max(x):
    n = x.shape[1]
    lane = lax.broadcasted_iota(jnp.int32, x.shape, 1)
    shift = 1
    while shift < n:
        x = jnp.maximum(x, jnp.where(lane >= shift, pltpu.roll(x, shift, 1), NEG))
        shift *= 2
    return x


def _mlstm_rows_kernel(qk_ref, v_ref, og_ref, *rest, blk, group):
    gate_refs = rest[:group]
    cw_ref, bi_ref, bf_ref, ng_ref, y_ref, xe_scr, s_scr, m_scr = rest[group:]
    heads, dh, w = MLSTM_HEADS, MLSTM_DH, MLSTM_W

    @pl.when(pl.program_id(1) == 0)
    def _():
        xe_scr[:, 0:8, :] = jnp.zeros((group, 8, 2 * w), F32)
        s_scr[...] = jnp.zeros_like(s_scr)
        m_scr[...] = jnp.zeros_like(m_scr)

    cw = cw_ref[...]
    causal = lax.broadcasted_iota(jnp.int32, (blk, blk), 0) >= lax.broadcasted_iota(jnp.int32, (blk, blk), 1)
    triu = (lax.broadcasted_iota(jnp.int32, (blk, blk), 0)
            <= lax.broadcasted_iota(jnp.int32, (blk, blk), 1)).astype(BF16)
    ones = jnp.ones((blk, dh), BF16)
    s_in = [[s_scr[g, h] for h in range(heads)] for g in range(group)]
    m_in = [m_scr[g, :, 0:1] for g in range(group)]
    s_out = [[None] * heads for _ in range(group)]
    m_out = [None] * group
    per_seq = []
    for g in range(group):
        xe_scr[g, 8:8 + blk, :] = qk_ref[g].astype(F32)
        conv = cw[CONV_WIDTH - 1:CONV_WIDTH, :] * xe_scr[g, 8:8 + blk, :]
        for j in range(CONV_WIDTH - 1):
            off = 8 - (CONV_WIDTH - 1) + j
            conv = conv + cw[j:j + 1, :] * xe_scr[g, off:off + blk, :]
        xe_scr[g, 0:8, :] = xe_scr[g, blk:blk + 8, :]
        qk = conv * _sigmoid(conv)

        gates = gate_refs[g][...]
        i_r = gates + bi_ref[...]
        lf_hi, lf_lo = _split_bf16(_log_sigmoid(pltpu.roll(gates, heads, 0) + bf_ref[...]))
        b_r = _dot(lf_hi, triu) + _dot(lf_lo, triu)
        m_st = m_in[g]
        a_r = i_r - b_r
        inter = b_r + m_st
        m_t = jnp.maximum(inter, b_r + _prefix_max(a_r))
        b_last = b_r[:, blk - 1:blk]
        dec = b_last - b_r + i_r
        m_new = jnp.maximum(b_last + m_st, jnp.max(dec, axis=1, keepdims=True))
        w_c = jnp.exp(b_last + m_st - m_new)
        m_out[g] = m_new
        pack = jnp.concatenate([b_r - m_t, jnp.exp(inter - m_t), jnp.exp(-m_t), jnp.exp(dec - m_new),
                                jnp.zeros((blk - 32, blk), F32)], axis=0)
        per_seq.append((qk, a_r, pack.T, w_c))

    chains = [(g, h) for h in range(heads) for g in range(group)]
    st = {}
    for g, h in chains:
        qk = per_seq[g][0]
        sl = slice(h * dh, (h + 1) * dh)
        q_b = qk[:, sl].astype(BF16)
        k_f = qk[:, w + h * dh:w + (h + 1) * dh] * (dh ** -0.5)
        v_ext = jnp.concatenate([v_ref[g, :, sl], ones], axis=-1)
        st[g, h] = (q_b, k_f, v_ext, _dot_nt(q_b, k_f.astype(BF16)), _dot(q_b, s_in[g][h].astype(BF16)))
    for g, h in chains:
        q_b, k_f, v_ext, qk_t, q_state = st[g, h]
        _, a_r, cols, _ = per_seq[g]
        u_c, w_inter = cols[:, h:h + 1], cols[:, 8 + h:9 + h]
        w_intra = jnp.exp(jnp.where(causal, u_c + a_r[h:h + 1, :], NEG))
        st[g, h] = (k_f, v_ext, _dot((qk_t * w_intra).astype(BF16), v_ext) + w_inter * q_state)
    for g, h in chains:
        k_f, v_ext, tot = st[g, h]
        _, _, cols, w_c = per_seq[g]
        em_c, w_k = cols[:, 16 + h:17 + h], cols[:, 24 + h:25 + h]
        sl = slice(h * dh, (h + 1) * dh)
        num, den = tot[:, :dh], tot[:, dh:]
        hh = num / jnp.maximum(jnp.abs(den), em_c)
        hn = _rms(hh, ng_ref[:, sl])
        y_ref[g, :, sl] = (hn * _sigmoid(og_ref[g, :, sl].astype(F32))).astype(y_ref.dtype)
        s_out[g][h] = w_c[h:h + 1, :] * s_in[g][h] + _dot((k_f * w_k).T.astype(BF16), v_ext)
    for g in range(group):
        m_scr[g] = jnp.broadcast_to(m_out[g], m_scr.shape[1:])
        for h in range(heads):
            s_scr[g, h] = s_out[g][h]


def _mlstm_rows(proj, gates_t, conv_w, bias_i, bias_f, norm_g, *, batch, seq, blk, group):
    t, npj = proj.shape
    w = MLSTM_W
    proj3 = proj.reshape(batch, seq, npj)
    cols = lambda c: (lambda b, i: (b, i, c))
    const2 = lambda b, i: (0, 0)
    nblk = seq // blk
    gate_specs = [pl.BlockSpec((8, blk), functools.partial(lambda b, i, g: (0, (b * group + g) * nblk + i), g=g))
                  for g in range(group)]
    y = pl.pallas_call(
        functools.partial(_mlstm_rows_kernel, blk=blk, group=group),
        grid=(batch // group, seq // blk),
        in_specs=[pl.BlockSpec((group, blk, 2 * w), cols(OFF_MQ // (2 * w))),
                  pl.BlockSpec((group, blk, w), cols(OFF_MV // w)),
                  pl.BlockSpec((group, blk, w), cols(OFF_MO // w)),
                  *gate_specs,
                  pl.BlockSpec((CONV_WIDTH, 2 * w), const2),
                  pl.BlockSpec((8, 1), const2), pl.BlockSpec((8, 1), const2),
                  pl.BlockSpec((1, w), const2)],
        out_specs=pl.BlockSpec((group, blk, w), cols(0)),
        out_shape=jax.ShapeDtypeStruct((batch, seq, w), BF16),
        scratch_shapes=[pltpu.VMEM((group, blk + 8, 2 * w), F32),
                        pltpu.VMEM((group, MLSTM_HEADS, MLSTM_DH, 2 * MLSTM_DH), F32),
                        pltpu.VMEM((group, 8, 128), F32)],
        compiler_params=_cparams("parallel", "arbitrary", vmem=VMEM_LIMIT_SMALL),
        name="mlstm",
    )(proj3, proj3, proj3, *([gates_t] * group), conv_w, bias_i, bias_f, norm_g)
    return y.reshape(t, w)


def _attnproj_kernel(h_ref, w_ref, seg_ref, gq_ref, gk_ref, o_ref, r_scr, *, dil):
    gw, half = ATTN_GW, ATTN_SLAB // 2
    sub_rows = r_scr.shape[2]
    seg, sub_seg = ATTN_TILE // dil, sub_rows // dil

    def head_norm(x, gain):
        sq = x * x
        hi = sq.astype(BF16)
        lo = (sq - hi.astype(F32)).astype(BF16)
        ss = _dot(hi, seg_ref[...]) + _dot(lo, seg_ref[...])
        return x * lax.rsqrt(ss * (1.0 / ATTN_DH) + EPS) * gain

    low = lax.broadcasted_iota(jnp.int32, (1, ATTN_SLAB), 1) < half
    for s in range(ATTN_TILE // sub_rows):
        rows = slice(s * sub_rows, (s + 1) * sub_rows)
        res = _dot_nt(h_ref[rows, :], w_ref[0])
        q = head_norm(res[:, :gw], gq_ref[...]) * (ATTN_DH ** -0.5)
        k = head_norm(res[:, gw:2 * gw], gk_ref[...])
        slabs = []
        for pair in range(gw // ATTN_SLAB):
            qp = q[:, pair * ATTN_SLAB:(pair + 1) * ATTN_SLAB]
            slabs += [jnp.where(low, qp, 0.0), jnp.where(low, 0.0, qp)]
        slabs += [k[:, c * 128:(c + 1) * 128] for c in range(gw // 128)]
        slabs += [res[:, 2 * gw + c * 128:2 * gw + (c + 1) * 128] for c in range(gw // 128)]
        for c, slab in enumerate(slabs):
            if dil == 1:
                o_ref[rows, c * 128:(c + 1) * 128] = slab.astype(o_ref.dtype)
            else:
                r_scr[s % 2, c] = slab
        if dil > 1:
            for r in range(dil):
                dst = slice(r * seg + s * sub_seg, r * seg + (s + 1) * sub_seg)
                for c in range(r_scr.shape[1]):
                    o_ref[dst, c * 128:(c + 1) * 128] = (
                        r_scr[s % 2, c, pl.ds(r, sub_seg, stride=dil), :].astype(o_ref.dtype))


def _attnproj(h, w, seg_ones, gq, gk, *, layer, group, dilation):
    t, d = h.shape
    wcols = 3 * ATTN_GW
    const2 = lambda i: (0, 0)
    return pl.pallas_call(
        functools.partial(_attnproj_kernel, dil=dilation),
        grid=(t // ATTN_TILE,),
        in_specs=[pl.BlockSpec((ATTN_TILE, d), lambda i: (i, 0)),
                  pl.BlockSpec((1, wcols, d), lambda i: (layer, group, 0)),
                  pl.BlockSpec((ATTN_GW, ATTN_GW), const2),
                  pl.BlockSpec((1, ATTN_GW), const2), pl.BlockSpec((1, ATTN_GW), const2)],
        out_specs=pl.BlockSpec((ATTN_TILE, ATTN_COLS), lambda i: (i, 0)),
        out_shape=jax.ShapeDtypeStruct((t, ATTN_COLS), BF16),
        scratch_shapes=[pltpu.VMEM((2, ATTN_COLS // 128, 512, 128), F32)],
        compiler_params=_cparams("parallel"),
        name=f"attnproj{group}",
    )(h, w, seg_ones, gq, gk)


def _dattn_kernel(q_ref, kc_ref, kp_ref, vc_ref, vp_ref, bias_ref, o_ref, lse_ref,
                  kx_scr, vx_scr, o_scr, l_scr, *, dil):
    blk = ATTN_BLOCK
    per = ATTN_SUB // dil
    first_tile = pl.program_id(1) == 0
    for r in range(dil):
        base = r * (per + 1) * blk
        last = slice((r * per + per - 1) * blk, (r * per + per) * blk)
        mine = slice(r * per * blk, (r + 1) * per * blk)
        kx_scr[base:base + blk, :] = kp_ref[last, :]
        vx_scr[base:base + blk, :] = vp_ref[last, :]
        kx_scr[base + blk:base + (per + 1) * blk, :] = kc_ref[mine, :]
        vx_scr[base + blk:base + (per + 1) * blk, :] = vc_ref[mine, :]

    low = lax.broadcasted_iota(jnp.int32, (1, ATTN_SLAB), 1) < ATTN_SLAB // 2
    no_prev = lax.broadcasted_iota(jnp.int32, (1, 2 * blk), 1) < blk
    for r in range(dil):
        for sub in range(per):
            u = r * per + sub
            win = slice((r * (per + 1) + sub) * blk, (r * (per + 1) + sub + 2) * blk)
            o_slabs, l_slabs = [], []
            for pair in range(ATTN_GW // ATTN_SLAB):
                cols = slice(pair * ATTN_SLAB, (pair + 1) * ATTN_SLAB)
                kx, vx = kx_scr[win, cols], vx_scr[win, cols]
                o_pair, l_pair = [], []
                for h in (2 * pair, 2 * pair + 1):
                    logits = _dot_nt(q_ref[u * blk:(u + 1) * blk, h * ATTN_SLAB:(h + 1) * ATTN_SLAB], kx)
                    logits = logits + bias_ref[h]
                    if sub == 0:
                        logits = jnp.where(first_tile & no_prev, NEG, logits)
                    m = jnp.max(logits, axis=-1, keepdims=True)
                    p = jnp.exp(logits - m)
                    l = jnp.sum(p, axis=-1, keepdims=True)
                    o_pair.append(_dot(p.astype(BF16), vx) / l)
                    l_pair.append(m + jnp.log(l))
                o_slabs.append(jnp.where(low, o_pair[0], o_pair[1]))
                l_slabs.append(jnp.where(low, l_pair[0], l_pair[1]))
            dst = pl.ds(sub * blk * dil + r, blk, stride=dil) if dil > 1 else slice(u * blk, (u + 1) * blk)
            for c in range(ATTN_GW // ATTN_SLAB):
                o_scr[c, dst, :] = o_slabs[c]
                l_scr[c, dst, :] = l_slabs[c]
    for c in range(ATTN_GW // ATTN_SLAB):
        o_ref[:, c * ATTN_SLAB:(c + 1) * ATTN_SLAB] = o_scr[c].astype(o_ref.dtype)
        lse_ref[:, c * ATTN_SLAB:(c + 1) * ATTN_SLAB] = l_scr[c]


def _dattn(aproj, bias, *, seq, group, dilation):
    t = aproj.shape[0]
    tiles = seq // ATTN_TILE
    qw = HEADS_PER_GROUP * ATTN_SLAB
    cq, ck, cv = 0, qw // ATTN_GW, qw // ATTN_GW + 1
    blk = (ATTN_TILE, ATTN_GW)
    cur = lambda c: (lambda b, j: (b * tiles + j, c))
    prev = lambda c: (lambda b, j: (b * tiles + jnp.maximum(j - 1, 0), c))
    xrows = ATTN_TILE + dilation * ATTN_BLOCK
    return pl.pallas_call(
        functools.partial(_dattn_kernel, dil=dilation),
        grid=(t // seq, tiles),
        in_specs=[pl.BlockSpec((ATTN_TILE, qw), cur(cq)),
                  pl.BlockSpec(blk, cur(ck)), pl.BlockSpec(blk, prev(ck)),
                  pl.BlockSpec(blk, cur(cv)), pl.BlockSpec(blk, prev(cv)),
                  pl.BlockSpec((HEADS_PER_GROUP, ATTN_BLOCK, 2 * ATTN_BLOCK), lambda b, j: (0, 0, 0))],
        out_specs=[pl.BlockSpec(blk, cur(0)), pl.BlockSpec(blk, cur(0))],
        out_shape=[jax.ShapeDtypeStruct((t, ATTN_GW), BF16), jax.ShapeDtypeStruct((t, ATTN_GW), F32)],
        scratch_shapes=[pltpu.VMEM((xrows, ATTN_GW), BF16), pltpu.VMEM((xrows, ATTN_GW), BF16),
                        pltpu.VMEM((ATTN_GW // ATTN_SLAB, ATTN_TILE, ATTN_SLAB), F32),
                        pltpu.VMEM((ATTN_GW // ATTN_SLAB, ATTN_TILE, ATTN_SLAB), F32)],
        compiler_params=_cparams("parallel", "arbitrary"),
        name=f"dattn{group}",
    )(aproj, aproj, aproj, aproj, aproj, bias)


def _rel_bucket(n):
    max_exact = REL_BUCKETS // 2
    nf = jnp.maximum(n, 1).astype(F32)
    log_b = max_exact + (jnp.log(nf / max_exact) / math.log(REL_MAX_DIST / max_exact)
                         * (REL_BUCKETS - max_exact)).astype(jnp.int32)
    return jnp.where(n < max_exact, n, jnp.minimum(log_b, REL_BUCKETS - 1))


def _attn_bias(rel_bias, group):
    window, dilation = ATTN_PATTERNS[group]
    steps = window // dilation
    hp = lax.Precision.HIGHEST
    hs = slice(group * HEADS_PER_GROUP, (group + 1) * HEADS_PER_GROUP)
    bucket = _rel_bucket(jnp.arange(steps + 1) * dilation)
    bias_steps = jnp.dot(jax.nn.one_hot(bucket, REL_BUCKETS, dtype=F32), rel_bias[:, hs].astype(F32),
                         precision=hp)
    qi = jnp.arange(ATTN_BLOCK)[:, None]
    ki = jnp.arange(2 * ATTN_BLOCK)[None, :]
    dist = ATTN_BLOCK + qi - ki
    ok = (dist >= 0) & (dist <= steps)
    sel = jax.nn.one_hot(jnp.clip(dist, 0, steps).reshape(-1), steps + 1, dtype=F32)
    bias = jnp.dot(sel, bias_steps, precision=hp).T.reshape(HEADS_PER_GROUP, ATTN_BLOCK, 2 * ATTN_BLOCK)
    return jnp.where(ok[None], bias, NEG)


def _merge_kernel(ya_ref, yb0_ref, yb1_ref, yb2_ref, l0_ref, l1_ref, l2_ref, gu_ref, gv_ref, gate_ref,
                  x_ref, wa_ref, wb_ref, wc_ref, wo_ref, ws_ref, bs_ref, gg_ref, o_ref, yc_scr, *, tm):
    d = x_ref.shape[1]
    l0, l1, l2 = l0_ref[...], l1_ref[...], l2_ref[...]
    mx = jnp.maximum(jnp.maximum(l0, l1), l2)
    e0, e1, e2 = jnp.exp(l0 - mx), jnp.exp(l1 - mx), jnp.exp(l2 - mx)
    inv = 1.0 / (e0 + e1 + e2)
    yb = jnp.concatenate([(yb0_ref[...].astype(F32) * (e0 * inv)).astype(BF16),
                          (yb1_ref[...].astype(F32) * (e1 * inv)).astype(BF16),
                          (yb2_ref[...].astype(F32) * (e2 * inv)).astype(BF16)], axis=-1)

    for j in range(tm // GMLP_CHUNK):
        rows = slice(j * GMLP_CHUNK, (j + 1) * GMLP_CHUNK)
        for g in range(GMLP_GROUPS):
            cols = slice(g * GMLP_GC, (g + 1) * GMLP_GC)
            u = jax.nn.gelu(gu_ref[rows, cols].astype(F32))
            v = _rms(jax.nn.gelu(gv_ref[rows, cols].astype(F32)), gg_ref[:, cols])
            mixed = _dot(ws_ref[g], v.astype(BF16)) + bs_ref[g]
            yc_scr[rows, cols] = (u * mixed).astype(BF16)

    def gate2(k):
        return jnp.tanh(0.5 * gate_ref[:, k * d:(k + 1) * d].astype(F32)) + 1.0

    merged2 = gate2(0) * _dot(ya_ref[...], wa_ref[...])
    merged2 = merged2 + gate2(1) * _dot(yb, wb_ref[...])
    merged2 = merged2 + gate2(2) * _dot(yc_scr[...], wc_ref[...])
    o_ref[...] = x_ref[...] + 0.5 * _dot(merged2.astype(BF16), wo_ref[...])


def _merge(ya, ybs, lses, proj, x2d, wa, wb, wc, wo, ws, bsb, gg, *, tm):
    t, d = x2d.shape
    row = lambda c: (lambda i: (i, c))
    full2 = lambda i: (0, 0)
    full3 = lambda i: (0, 0, 0)
    gspec = pl.BlockSpec((tm, ATTN_GW), row(0))
    return pl.pallas_call(
        functools.partial(_merge_kernel, tm=tm),
        grid=(t // tm,),
        in_specs=[pl.BlockSpec((tm, MLSTM_W), row(0)),
                  gspec, gspec, gspec, gspec, gspec, gspec,
                  pl.BlockSpec((tm, GMLP_W), row(OFF_GU // GMLP_W)),
                  pl.BlockSpec((tm, GMLP_W), row(OFF_GV // GMLP_W)),
                  pl.BlockSpec((tm, N_BRANCH * d), row(OFF_GATE // (N_BRANCH * d))),
                  pl.BlockSpec((tm, d), row(0)),
                  pl.BlockSpec(wa.shape, full2), pl.BlockSpec(wb.shape, full2),
                  pl.BlockSpec(wc.shape, full2), pl.BlockSpec(wo.shape, full2),
                  pl.BlockSpec(ws.shape, full3), pl.BlockSpec(bsb.shape, full3),
                  pl.BlockSpec(gg.shape, full2)],
        out_specs=pl.BlockSpec((tm, d), row(0)),
        out_shape=jax.ShapeDtypeStruct((t, d), F32),
        scratch_shapes=[pltpu.VMEM((tm, GMLP_W), BF16)],
        compiler_params=_cparams("parallel"),
        name="merge",
    )(ya, *ybs, *lses, proj, proj, proj, x2d, wa, wb, wc, wo, ws, bsb, gg)


def _memkv_kernel(mem_ref, g_ref, w_ref, gk_ref, k_ref, v_ref):
    dh, w = XATTN_DH, XATTN_W
    kv = _dot(_rms(mem_ref[0], g_ref[...]).astype(BF16), w_ref[...])
    for h in range(XATTN_HEADS):
        sl = slice(h * dh, (h + 1) * dh)
        k_ref[0, :, sl] = _rms(kv[:, sl], gk_ref[...]).astype(k_ref.dtype)
    v_ref[0] = kv[:, w:].astype(v_ref.dtype)


def _memkv(mem, gain, w_kv, gk):
    b, m, d = mem.shape
    full2 = lambda i: (0, 0)
    return pl.pallas_call(
        _memkv_kernel,
        grid=(b,),
        in_specs=[pl.BlockSpec((1, m, d), lambda i: (i, 0, 0)),
                  pl.BlockSpec((1, d), full2),
                  pl.BlockSpec(w_kv.shape, full2),
                  pl.BlockSpec((1, XATTN_DH), full2)],
        out_specs=[pl.BlockSpec((1, m, XATTN_W), lambda i: (i, 0, 0)),
                   pl.BlockSpec((1, m, XATTN_W), lambda i: (i, 0, 0))],
        out_shape=[jax.ShapeDtypeStruct((b, m, XATTN_W), BF16),
                   jax.ShapeDtypeStruct((b, m, XATTN_W), BF16)],
        compiler_params=_cparams("parallel", vmem=VMEM_LIMIT_SMALL),
        name="memkv",
    )(mem, gain, w_kv, gk)


def _route(logits):
    tm = logits.shape[1]
    e = jnp.exp(logits - jnp.max(logits, axis=0, keepdims=True))
    probs = e / jnp.sum(e, axis=0, keepdims=True)
    rowi = lax.broadcasted_iota(jnp.int32, (8, tm), 0)
    real = rowi < EXPERTS_PER_GROUP
    tops = []
    for g in range(N_EXPERT_GROUPS):
        pg = jnp.where(real, probs[8 * g:8 * g + 8, :], -0.5)
        m1 = jnp.max(pg, axis=0, keepdims=True)
        i1 = jnp.min(jnp.where(pg == m1, rowi, 8), axis=0, keepdims=True)
        pg2 = jnp.where(rowi == i1, -1.0, pg)
        m2 = jnp.max(pg2, axis=0, keepdims=True)
        i2 = jnp.min(jnp.where(pg2 == m2, rowi, 8), axis=0, keepdims=True)
        tops.append((m1, i1, m2, i2))
    best = jnp.zeros((1, tm), jnp.int32)
    best_score = tops[0][0] + tops[0][2]
    for g in range(1, N_EXPERT_GROUPS):
        score = tops[g][0] + tops[g][2]
        better = score > best_score
        best = jnp.where(better, g, best)
        best_score = jnp.where(better, score, best_score)
    m1, i1, m2, i2 = tops[0]
    for g in range(1, N_EXPERT_GROUPS):
        m1, i1, m2, i2 = (jnp.where(best == g, new, old) for new, old in zip(tops[g], (m1, i1, m2, i2)))
    tot = m1 + m2
    base = best * EXPERTS_PER_GROUP
    return base + i1, base + i2, m1 / tot, m2 / tot


def _pack_bf16_pairs(x):
    n = x.shape[1] // 2
    hi = lax.bitcast_convert_type(x[:, :n].astype(BF16).astype(F32), jnp.uint32)
    lo = lax.bitcast_convert_type(x[:, n:].astype(BF16).astype(F32), jnp.uint32)
    return hi | (lo >> 16)


def _unpack_bf16_pairs(p):
    hi = lax.bitcast_convert_type(p & jnp.uint32(0xFFFF0000), F32)
    lo = lax.bitcast_convert_type(p << 16, F32)
    return hi, lo


def _store_row_chunks(ref, packed):
    for j in range(ROW_CHUNKS):
        ref[j] = packed[:, j * 128:(j + 1) * 128]


def _load_row_chunks(ref):
    return jnp.concatenate([ref[j] for j in range(ROW_CHUNKS)], axis=-1)


def _xattn_kernel(x_ref, k_ref, v_ref, gx_ref, wq_ref, gq_ref, wo_ref, gf_ref, rw_ref, rb_ref,
                  xo_ref, hf_ref, eidx_ref, wts_ref, *, sub):
    dh = XATTN_DH
    rw = rw_ref[...]
    rw_hi, rw_lo = _split_bf16(rw)
    for s in range(x_ref.shape[0] // sub):
        rows = slice(s * sub, (s + 1) * sub)
        x = x_ref[rows, :]
        q = _dot(_rms(x, gx_ref[...]).astype(BF16), wq_ref[...])
        outs = []
        for h in range(XATTN_HEADS):
            sl = slice(h * dh, (h + 1) * dh)
            q_h = (_rms(q[:, sl], gq_ref[...]) * (dh ** -0.5)).astype(BF16)
            logits = _dot_nt(q_h, k_ref[0, :, sl])
            p = jnp.exp(logits - jnp.max(logits, axis=-1, keepdims=True))
            o = _dot(p.astype(BF16), v_ref[0, :, sl]) / jnp.sum(p, axis=-1, keepdims=True)
            outs.append(o.astype(BF16))
        xn = x + _dot(jnp.concatenate(outs, axis=-1), wo_ref[...])
        xo_ref[rows, :] = xn
        hf = _rms(xn, gf_ref[...])
        packed = _pack_bf16_pairs(hf)
        for j in range(ROW_CHUNKS):
            hf_ref[j, rows, :] = packed[:, j * 128:(j + 1) * 128]
        hf_hi, hf_lo = _split_bf16(hf)
        logits_t = _dot_nt(rw_hi, hf_hi) + _dot_nt(rw_hi, hf_lo) + _dot_nt(rw_lo, hf_hi) + rb_ref[...]
        e1, e2, w1, w2 = _route(logits_t)
        eidx_ref[:, rows] = jnp.concatenate([e1, e2, jnp.zeros((6, sub), jnp.int32)], axis=0)
        wts_ref[:, rows] = jnp.concatenate([w1, w2, jnp.zeros((6, sub), F32)], axis=0)


def _xattn(x2d, k, v, gx, wq, gq, wo, gf, rw_t, rb, *, seq, tm):
    t, d = x2d.shape
    per_b = seq // tm
    full2 = lambda i: (0, 0)
    kv_spec = pl.BlockSpec((1,) + k.shape[1:], lambda i: (i // per_b, 0, 0))
    return pl.pallas_call(
        functools.partial(_xattn_kernel, sub=min(tm, XATTN_SUB)),
        grid=(t // tm,),
        in_specs=[pl.BlockSpec((tm, d), lambda i: (i, 0)), kv_spec, kv_spec,
                  pl.BlockSpec((1, d), full2), pl.BlockSpec(wq.shape, full2),
                  pl.BlockSpec((1, XATTN_DH), full2), pl.BlockSpec(wo.shape, full2),
                  pl.BlockSpec((1, d), full2), pl.BlockSpec(rw_t.shape, full2),
                  pl.BlockSpec(rb.shape, full2)],
        out_specs=[pl.BlockSpec((tm, d), lambda i: (i, 0)),
                   pl.BlockSpec((ROW_CHUNKS, tm, 128), lambda i: (0, i, 0)),
                   pl.BlockSpec((8, tm), lambda i: (0, i)),
                   pl.BlockSpec((8, tm), lambda i: (0, i))],
        out_shape=[jax.ShapeDtypeStruct((t, d), F32),
                   jax.ShapeDtypeStruct((ROW_CHUNKS, t, 128), jnp.uint32),
                   jax.ShapeDtypeStruct((8, t), jnp.int32),
                   jax.ShapeDtypeStruct((8, t), F32)],
        compiler_params=_cparams("parallel"),
        name="xattn_router",
    )(x2d, k, v, gx, wq, gq, wo, gf, rw_t, rb)


def _moe_plan_kernel(eidx_ref, i1_ref, i2_ref, te_ref, na_ref, cnt_scr, carry_scr, *, tb, tm, plane_rows):
    ne = N_EXPERTS
    hp = lax.Precision.HIGHEST
    phase, j = pl.program_id(0), pl.program_id(1)
    rows = lax.broadcasted_iota(jnp.int32, (ne, tb), 0)
    oh1 = rows == eidx_ref[0:1, :]
    oh2 = rows == eidx_ref[1:2, :]
    a = oh1.astype(F32) + oh2.astype(F32)
    blk_cnt = jnp.broadcast_to(jnp.sum(a, axis=1, keepdims=True), cnt_scr.shape)

    @pl.when((phase == 0) & (j == 0))
    def _():
        cnt_scr[...] = jnp.zeros_like(cnt_scr)

    @pl.when(phase == 0)
    def _():
        cnt_scr[...] += blk_cnt

    @pl.when((phase == 1) & (j == 0))
    def _():
        padded = jnp.ceil(cnt_scr[...] * (1.0 / tm)) * tm
        er = lax.broadcasted_iota(jnp.int32, (ne, ne), 0)
        ec = lax.broadcasted_iota(jnp.int32, (ne, ne), 1)
        off = jnp.dot((ec < er).astype(F32), padded, precision=hp, preferred_element_type=F32)
        carry_scr[...] = off
        seg_end = (off + padded)[:, 0:1]
        tile_start = lax.broadcasted_iota(jnp.int32, (ne, te_ref.shape[1]), 1).astype(F32) * tm
        te = jnp.sum((seg_end <= tile_start).astype(F32), axis=0, keepdims=True)
        te_ref[...] = jnp.broadcast_to(jnp.minimum(te, ne - 1.0), te_ref.shape).astype(jnp.int32)
        total = jnp.sum(padded[:, 0:1], axis=0, keepdims=True)
        na_ref[...] = jnp.broadcast_to(total * (1.0 / tm), na_ref.shape).astype(jnp.int32)

    @pl.when(phase == 1)
    def _():
        before = (lax.broadcasted_iota(jnp.int32, (tb, tb), 0)
                  < lax.broadcasted_iota(jnp.int32, (tb, tb), 1)).astype(BF16)
        rank = carry_scr[:, 0:1] + _dot(a.astype(BF16), before)
        d1 = jnp.sum(jnp.where(oh1, rank, 0.0), axis=0, keepdims=True).astype(jnp.int32)
        d2 = jnp.sum(jnp.where(oh2, rank, 0.0), axis=0, keepdims=True).astype(jnp.int32)
        plane = lax.broadcasted_iota(jnp.int32, (8, tb), 0) * plane_rows
        i1_ref[...] = jnp.where(plane < ROW_CHUNKS * plane_rows, plane + d1, 0)
        i2_ref[...] = jnp.where(plane < ROW_CHUNKS * plane_rows, plane + d2, 0)
        carry_scr[...] += blk_cnt


def _moe_plan(eidx, *, tm, n_tiles, tb=512):
    t = eidx.shape[1]
    ntp = -(-n_tiles // 128) * 128
    return pl.pallas_call(
        functools.partial(_moe_plan_kernel, tb=tb, tm=tm, plane_rows=n_tiles * tm),
        grid=(2, t // tb),
        in_specs=[pl.BlockSpec((8, tb), lambda p, j: (0, j))],
        out_specs=[pl.BlockSpec((8, tb), lambda p, j: (0, j * p)),
                   pl.BlockSpec((8, tb), lambda p, j: (0, j * p)),
                   pl.BlockSpec((8, ntp), lambda p, j: (0, 0)),
                   pl.BlockSpec((8, 128), lambda p, j: (0, 0))],
        out_shape=[jax.ShapeDtypeStruct((8, t), jnp.int32),
                   jax.ShapeDtypeStruct((8, t), jnp.int32),
                   jax.ShapeDtypeStruct((8, ntp), jnp.int32),
                   jax.ShapeDtypeStruct((8, 128), jnp.int32)],
        scratch_shapes=[pltpu.VMEM((N_EXPERTS, 128), F32), pltpu.VMEM((N_EXPERTS, 128), F32)],
        compiler_params=_cparams("arbitrary", "arbitrary", vmem=VMEM_LIMIT_SMALL),
        name="moe_plan",
    )(eidx)


def _sc_mesh():
    return plsc.VectorSubcoreMesh(core_axis_name="c", subcore_axis_name="s",
                                  num_cores=SC_CORES, num_subcores=SC_SUBCORES)


def _sc_index_spec(tokens):
    nb = tokens // SC_WINDOW
    return pl.BlockSpec((1, SC_WINDOW), lambda i: (i // nb, i % nb))


def _sc_dispatch(rows, i1, i2, n_out):
    n = rows.shape[0]
    tokens = i1.shape[1]

    @functools.partial(pl.kernel, out_type=jax.ShapeDtypeStruct((n_out, 128), rows.dtype), mesh=_sc_mesh(),
                       name="moe_dispatch")
    def k(x_hbm, i1_hbm, i2_hbm, o_hbm):
        def body(x_vmem, i1_vmem, i2_vmem):
            pltpu.sync_copy(x_vmem, o_hbm.at[i1_vmem.at[0]])
            pltpu.sync_copy(x_vmem, o_hbm.at[i2_vmem.at[0]])

        pltpu.emit_pipeline(
            body, grid=(n // SC_WINDOW,),
            in_specs=[pl.BlockSpec((SC_WINDOW, 128), lambda i: (i, 0)),
                      _sc_index_spec(tokens), _sc_index_spec(tokens)],
            out_specs=[],
            core_axis_name=("c", "s"), dimension_semantics=(pltpu.PARALLEL,),
        )(x_hbm, i1_hbm, i2_hbm)

    return k(rows, i1, i2)


def _sc_collect(table, i1, i2):
    tokens = i1.shape[1]
    n = ROW_CHUNKS * tokens
    out = jax.ShapeDtypeStruct((n, 128), table.dtype)

    @functools.partial(pl.kernel, out_type=(out, out), mesh=_sc_mesh(), name="moe_collect")
    def k(t_hbm, i1_hbm, i2_hbm, o1_hbm, o2_hbm):
        def body(i1_vmem, i2_vmem, o1_vmem, o2_vmem):
            pltpu.sync_copy(t_hbm.at[i1_vmem.at[0]], o1_vmem)
            pltpu.sync_copy(t_hbm.at[i2_vmem.at[0]], o2_vmem)

        pltpu.emit_pipeline(
            body, grid=(n // SC_WINDOW,),
            in_specs=[_sc_index_spec(tokens), _sc_index_spec(tokens)],
            out_specs=[pl.BlockSpec((SC_WINDOW, 128), lambda i: (i, 0)),
                       pl.BlockSpec((SC_WINDOW, 128), lambda i: (i, 0))],
            core_axis_name=("c", "s"), dimension_semantics=(pltpu.PARALLEL,),
        )(i1_hbm, i2_hbm, o1_hbm, o2_hbm)

    return k(table, i1, i2)


def _experts_kernel(te_ref, na_ref, xs_ref, wg_ref, wu_ref, wd_ref, y_ref, wg_scr, wu_scr, wd_scr):
    i = pl.program_id(0)
    active = i < na_ref[0]

    @pl.when(active & ((i == 0) | (te_ref[i] != te_ref[jnp.maximum(i - 1, 0)])))
    def _():
        wg_scr[...] = wg_ref[0, 0].astype(BF16)
        wu_scr[...] = wu_ref[0, 0].astype(BF16)
        wd_scr[...] = wd_ref[0, 0].astype(BF16)

    @pl.when(active)
    def _():
        hi, lo = _unpack_bf16_pairs(_load_row_chunks(xs_ref))
        h = jnp.concatenate([hi, lo], axis=-1).astype(BF16)
        up = _dot(h, wg_scr[...])
        act = up * _sigmoid(up) * _dot(h, wu_scr[...])
        _store_row_chunks(y_ref, _pack_bf16_pairs(_dot(act.astype(BF16), wd_scr[...])))


def _experts(tile_expert, n_active, xs, wg, wu, wd, *, layer, tm):
    n_tiles = tile_expert.shape[0]
    _, _, d, dff = wg.shape
    rows = lambda i, te, na: (0, jnp.minimum(i, na[0] - 1), 0)
    expert = lambda i, te, na: (layer, te[i], 0, 0)
    return pl.pallas_call(
        _experts_kernel,
        grid_spec=pltpu.PrefetchScalarGridSpec(
            num_scalar_prefetch=2,
            grid=(n_tiles,),
            in_specs=[pl.BlockSpec((ROW_CHUNKS, tm, 128), rows),
                      pl.BlockSpec((1, 1, d, dff), expert),
                      pl.BlockSpec((1, 1, d, dff), expert),
                      pl.BlockSpec((1, 1, dff, d), expert)],
            out_specs=pl.BlockSpec((ROW_CHUNKS, tm, 128), rows),
            scratch_shapes=[pltpu.VMEM((d, dff), BF16), pltpu.VMEM((d, dff), BF16), pltpu.VMEM((dff, d), BF16)]),
        out_shape=jax.ShapeDtypeStruct(xs.shape, xs.dtype),
        compiler_params=_cparams("arbitrary"),
        name="moe_experts",
    )(tile_expert, n_active, xs, wg, wu, wd)


def _moe_combine_kernel(x_ref, y1_ref, y2_ref, w_ref, o_ref):
    half = x_ref.shape[1] // 2
    hi1, lo1 = _unpack_bf16_pairs(_load_row_chunks(y1_ref))
    hi2, lo2 = _unpack_bf16_pairs(_load_row_chunks(y2_ref))
    tm = x_ref.shape[0]
    w_cols = jnp.concatenate([w_ref[...], jnp.zeros((128 - w_ref.shape[0], tm), F32)], axis=0).T
    w1, w2 = w_cols[:, 0:1], w_cols[:, 1:2]
    o_ref[:, :half] = x_ref[:, :half] + w1 * hi1 + w2 * hi2
    o_ref[:, half:] = x_ref[:, half:] + w1 * lo1 + w2 * lo2


def _moe_combine(x2d, y1, y2, wts, *, tm):
    t, d = x2d.shape
    chunk_spec = pl.BlockSpec((ROW_CHUNKS, tm, 128), lambda i: (0, i, 0))
    return pl.pallas_call(
        _moe_combine_kernel,
        grid=(t // tm,),
        in_specs=[pl.BlockSpec((tm, d), lambda i: (i, 0)), chunk_spec, chunk_spec,
                  pl.BlockSpec((wts.shape[0], tm), lambda i: (0, i))],
        out_specs=pl.BlockSpec((tm, d), lambda i: (i, 0)),
        out_shape=jax.ShapeDtypeStruct((t, d), F32),
        compiler_params=_cparams("parallel", vmem=VMEM_LIMIT_SMALL),
        name="moe_combine",
    )(x2d, y1, y2, wts)


def _moe(x2d, hf_rows, eidx, wts, wg, wu, wd, *, layer):
    t = x2d.shape[0]
    tm = MOE_TM
    n_tiles = 2 * t // tm + N_EXPERTS
    plane = n_tiles * tm
    i1, i2, te, na = _moe_plan(eidx, tm=tm, n_tiles=n_tiles)
    xs = _sc_dispatch(hf_rows.reshape(ROW_CHUNKS * t, 128), i1, i2, ROW_CHUNKS * plane)
    ys = _experts(te[0, :n_tiles], na[0, :1], xs.reshape(ROW_CHUNKS, plane, 128), wg, wu, wd,
                  layer=layer, tm=tm)
    y1, y2 = _sc_collect(ys.reshape(ROW_CHUNKS * plane, 128), i1, i2)
    return _moe_combine(x2d, y1.reshape(ROW_CHUNKS, t, 128), y2.reshape(ROW_CHUNKS, t, 128), wts, tm=512)


W_ROWS = 256


def _w_rows_kernel(start_ref, valid_ref, w_ref, o_ref):
    del start_ref
    row = lax.broadcasted_iota(jnp.int32, w_ref.shape[1:], 0)
    o_ref[0] = jnp.where(row < valid_ref[pl.program_id(1)], w_ref[0], 0.0).astype(o_ref.dtype)


def _w_rows(w_t, starts, valid):
    depth, _, d = w_t.shape
    nblk = len(starts)
    return pl.pallas_call(
        _w_rows_kernel,
        grid_spec=pltpu.PrefetchScalarGridSpec(
            num_scalar_prefetch=2,
            grid=(depth, nblk),
            in_specs=[pl.BlockSpec((pl.Element(1), pl.Element(W_ROWS), pl.Element(d)),
                                   lambda l, c, st, va: (l, pl.multiple_of(st[c], 8), 0))],
            out_specs=pl.BlockSpec((1, W_ROWS, d), lambda l, c, st, va: (l, c, 0))),
        out_shape=jax.ShapeDtypeStruct((depth, nblk * W_ROWS, d), BF16),
        compiler_params=_cparams("parallel", "arbitrary", vmem=VMEM_LIMIT_SMALL),
        name="w_in_rows",
    )(jnp.asarray(starts, jnp.int32), jnp.asarray(valid, jnp.int32), w_t)


def _w_in_layout(w_in):
    w_t = jnp.swapaxes(w_in, 1, 2)
    src_if = 4 * MLSTM_W
    src_a = src_if + 2 * MLSTM_HEADS
    src_g = src_a + 3 * ATTN_W
    starts = list(range(0, src_if, W_ROWS)) + [src_g + k * W_ROWS for k in range((OFF_IF - OFF_GU) // W_ROWS)]
    valid = [W_ROWS] * len(starts)
    starts.append(src_if)
    valid.append(2 * MLSTM_HEADS)
    assert len(starts) * W_ROWS == N_PROJ and ATTN_GW == W_ROWS
    a_starts = [src_a + j * ATTN_W + g * ATTN_GW for g in range(len(ATTN_PATTERNS)) for j in range(3)]
    return _w_rows(w_t, starts, valid), _w_rows(w_t, a_starts, [W_ROWS] * len(a_starts))


def kernel(x, mem, norm_mix, w_in, mlstm_conv, mlstm_gate_b, mlstm_norm, attn_qk_norm, gmlp_norm, gmlp_ws,
           gmlp_bs, w_branch_a, w_branch_b, w_branch_c, w_out, rel_bias, norm_xattn, norm_mem, w_xq, w_xkv,
           xattn_qk_norm, w_xo, norm_ffn, router_w, router_b, w_expert_gate, w_expert_up, w_expert_down):
    b, s, d = x.shape
    t = b * s
    depth = w_in.shape[0]
    x2d = x.reshape(t, d)

    biases = [_attn_bias(rel_bias, g) for g in range(len(ATTN_PATTERNS))]
    rw_t = jnp.zeros((N_EXPERT_GROUPS, 8, d), F32).at[:, :EXPERTS_PER_GROUP].set(
        router_w.T.reshape(N_EXPERT_GROUPS, EXPERTS_PER_GROUP, d)).reshape(ROUTER_ROWS, d)
    rb = jnp.full((N_EXPERT_GROUPS, 8), NEG, F32).at[:, :EXPERTS_PER_GROUP].set(
        router_b.astype(F32).reshape(N_EXPERT_GROUPS, EXPERTS_PER_GROUP)).reshape(ROUTER_ROWS, 1)
    tril = jnp.tril(jnp.ones((GMLP_CHUNK, GMLP_CHUNK), bool))
    head_of = jnp.arange(ATTN_GW) // ATTN_DH
    seg_ones = (head_of[:, None] == head_of[None, :]).astype(BF16)

    w_main, w_attn = _w_in_layout(w_in)

    for l in range(depth):
        proj, h_mix, gates_t = _inproj(x2d, norm_mix[l][None], w_main, layer=l, tm=1024, tn=3072)
        gq = jnp.tile(attn_qk_norm[l, 0], HEADS_PER_GROUP)[None]
        gk = jnp.tile(attn_qk_norm[l, 1], HEADS_PER_GROUP)[None]

        nh = MLSTM_HEADS
        bias_i = jnp.zeros((8, 1), F32).at[:nh, 0].set(mlstm_gate_b[l, :nh])
        bias_f = jnp.zeros((8, 1), F32).at[:nh, 0].set(mlstm_gate_b[l, nh:])
        ya = _mlstm_rows(proj, gates_t, mlstm_conv[l], bias_i, bias_f, mlstm_norm[l][None],
                         batch=b, seq=s, blk=MLSTM_BLOCK, group=MLSTM_GROUP)

        ybs, lses = [], []
        for g, (_, dilation) in enumerate(ATTN_PATTERNS):
            aproj = _attnproj(h_mix, w_attn, seg_ones, gq, gk, layer=l, group=g, dilation=dilation)
            o, lse = _dattn(aproj, biases[g], seq=s, group=g, dilation=dilation)
            ybs.append(o)
            lses.append(lse)

        ws = jnp.where(tril, gmlp_ws[l], 0.0).astype(BF16)
        bsb = jnp.broadcast_to(gmlp_bs[l][:, :, None], (GMLP_GROUPS, GMLP_CHUNK, GMLP_GC)).astype(F32)
        x2d = _merge(ya, ybs, lses, proj, x2d, w_branch_a[l].astype(BF16), w_branch_b[l].astype(BF16),
                     w_branch_c[l].astype(BF16), w_out[l].astype(BF16), ws, bsb, gmlp_norm[l][None], tm=512)

        k_mem, v_mem = _memkv(mem, norm_mem[l][None], w_xkv[l].astype(BF16), xattn_qk_norm[l, 1][None])
        x2d, hf_rows, eidx, wts = _xattn(x2d, k_mem, v_mem, norm_xattn[l][None], w_xq[l].astype(BF16),
                                         xattn_qk_norm[l, 0][None], w_xo[l].astype(BF16), norm_ffn[l][None],
                                         rw_t, rb, seq=s, tm=1024)

        x2d = _moe(x2d, hf_rows, eidx, wts, w_expert_gate, w_expert_up, w_expert_down, layer=l)

    return x2d.reshape(b, s, d)
```

```python
import functools
import math

import jax
import jax.numpy as jnp
import numpy as np
from jax import lax
from jax.experimental import pallas as pl
from jax.experimental.pallas import tpu as pltpu
from jax.experimental.pallas import tpu_sc as plsc

F32 = jnp.float32
BF16 = jnp.bfloat16

EPS = 1e-6
NEG = -1e30

MLSTM_HEADS = 4
MLSTM_DH = 128
MLSTM_W = MLSTM_HEADS * MLSTM_DH
CONV_WIDTH = 4
MLSTM_BLOCK = 128
MLSTM_NSUB = 1
MLSTM_GROUP = 4

ATTN_PATTERNS = ((128, 1), (512, 4), (2048, 16))
HEADS_PER_GROUP = 4
ATTN_DH = 64
ATTN_GW = HEADS_PER_GROUP * ATTN_DH
ATTN_W = len(ATTN_PATTERNS) * ATTN_GW
ATTN_BLOCK = 128
REL_BUCKETS = 32
REL_MAX_DIST = 2048

GMLP_GROUPS = 4
GMLP_GC = 128
GMLP_W = GMLP_GROUPS * GMLP_GC
GMLP_CHUNK = 128

XATTN_HEADS = 4
XATTN_DH = 128
XATTN_W = XATTN_HEADS * XATTN_DH
XATTN_SUB = 1024

N_EXPERTS = 16
N_EXPERT_GROUPS = 4
EXPERTS_PER_GROUP = 4
ROUTER_ROWS = 8 * N_EXPERT_GROUPS

N_BRANCH = 3

MOE_TM = 1024
ROW_CHUNKS = 4
SC_CORES, SC_SUBCORES = 2, 16
SC_WINDOW = 128

OFF_MQ, OFF_MK, OFF_MV, OFF_MO = 0, 512, 1024, 1536
OFF_GU, OFF_GV = 2048, 2560
OFF_GATE = 3072
OFF_IF = 6144
IF_PAD = 256
N_PROJ = OFF_IF + IF_PAD

ATTN_TILE = 2048
ATTN_SUB = ATTN_TILE // ATTN_BLOCK
ATTN_SLAB = 2 * ATTN_DH
ATTN_COLS = HEADS_PER_GROUP * ATTN_SLAB + 2 * ATTN_GW

VMEM_LIMIT = 48 * 1024 * 1024
VMEM_LIMIT_INPROJ = 56 * 1024 * 1024
VMEM_LIMIT_SMALL = 24 * 1024 * 1024


def _cparams(*sem, vmem=VMEM_LIMIT):
    return pltpu.CompilerParams(dimension_semantics=sem, vmem_limit_bytes=vmem)


def _rms(x, gain):
    return x * lax.rsqrt(jnp.mean(x * x, axis=-1, keepdims=True) + EPS) * gain


def _sigmoid(x):
    return 0.5 * jnp.tanh(0.5 * x) + 0.5


def _dot(a, b):
    return jnp.dot(a, b, preferred_element_type=F32)


def _dot_nt(a, b):
    return lax.dot_general(a, b, (((1,), (1,)), ((), ())), preferred_element_type=F32)


def _inproj_kernel(*refs, pending_moe):
    if pending_moe:
        x_ref, y1_ref, y2_ref, wt_ref, g_ref, w_ref, wg_ref, o_ref, h_ref, gt_ref, xn_ref = refs
    else:
        x_ref, g_ref, w_ref, wg_ref, o_ref, h_ref, gt_ref = refs

    @pl.when(pl.program_id(1) == 0)
    def _():
        if pending_moe:
            x = jnp.concatenate(_moe_combined_halves(x_ref, y1_ref, y2_ref, wt_ref), axis=-1)
            xn_ref[...] = x
        else:
            x = x_ref[...]
        h = _rms(x, g_ref[...]).astype(BF16)
        h_ref[...] = h
        gt_ref[...] = _dot_nt(wg_ref[0, 0:128, :], h)[:gt_ref.shape[0], :]

    o_ref[...] = _dot_nt(h_ref[...], w_ref[0]).astype(o_ref.dtype)


def _inproj(x2d, gain, w, *, layer, tm, tn, pending_moe=None):
    t, d = x2d.shape
    n = OFF_IF
    rows = lambda i, j: (i, 0)
    in_specs = [pl.BlockSpec((tm, d), rows)]
    out_specs = [pl.BlockSpec((tm, tn), lambda i, j: (i, j)), pl.BlockSpec((tm, d), rows),
                 pl.BlockSpec((8, tm), lambda i, j: (0, i))]
    out_shape = [jax.ShapeDtypeStruct((t, n), BF16), jax.ShapeDtypeStruct((t, d), BF16),
                 jax.ShapeDtypeStruct((8, t), F32)]
    operands = [x2d]
    if pending_moe is not None:
        chunk_spec = pl.BlockSpec((ROW_CHUNKS, tm, 128), lambda i, j: (0, i, 0))
        in_specs += [chunk_spec, chunk_spec, pl.BlockSpec((8, tm), lambda i, j: (0, i))]
        operands += list(pending_moe)
        out_specs.append(pl.BlockSpec((tm, d), rows))
        out_shape.append(jax.ShapeDtypeStruct((t, d), F32))
    in_specs += [pl.BlockSpec((1, d), lambda i, j: (0, 0)),
                 pl.BlockSpec((1, tn, d), lambda i, j: (layer, j, 0)),
                 pl.BlockSpec((1, IF_PAD, d), lambda i, j: (layer, OFF_IF // IF_PAD, 0))]
    return pl.pallas_call(
        functools.partial(_inproj_kernel, pending_moe=pending_moe is not None),
        grid=(t // tm, n // tn),
        in_specs=in_specs, out_specs=out_specs, out_shape=out_shape,
        compiler_params=_cparams("parallel", "arbitrary", vmem=VMEM_LIMIT_INPROJ),
        name="inproj",
    )(*operands, gain, w, w)


def _log_sigmoid(x):
    return jnp.minimum(x, 0.0) - jnp.log(1.0 + jnp.exp(-jnp.abs(x)))


def _mlstm_kernel(qk_ref, v_ref, og_ref, gc_ref, gr_ref, cw_ref, gbc_ref, gbr_ref, ng_ref, y_ref,
                  xe_scr, s_scr, m_scr, *, blk, nsub, group):
    heads, w = MLSTM_HEADS, MLSTM_W

    @pl.when(pl.program_id(1) == 0)
    def _():
        xe_scr[:, 0:8, :] = jnp.zeros((group, 8, 2 * w), F32)
        s_scr[...] = jnp.zeros_like(s_scr)
        m_scr[...] = jnp.zeros_like(m_scr)

    cw = cw_ref[...]
    ri = lax.broadcasted_iota(jnp.int32, (blk, blk), 0)
    ci = lax.broadcasted_iota(jnp.int32, (blk, blk), 1)
    causal = ri >= ci
    tril = causal.astype(BF16)
    triu = (ri <= ci).astype(BF16)
    states = []
    for g in range(group):
        xe_scr[g, 8:8 + nsub * blk, :] = qk_ref[g].astype(F32)
        states.append([(s_scr[g, h], m_scr[g, h:h + 1, 0:1]) for h in range(heads)])
    for c in range(nsub):
        for g in range(group):
            states[g] = _mlstm_chunk(c * blk, blk, states[g], cw, causal, tril, triu, xe_scr.at[g], v_ref.at[g],
                                     og_ref.at[g], gc_ref.at[g], gr_ref.at[g], gbc_ref, gbr_ref, ng_ref,
                                     y_ref.at[g])
    for g in range(group):
        xe_scr[g, 0:8, :] = xe_scr[g, nsub * blk:nsub * blk + 8, :]
        for h, (s_st, m_st) in enumerate(states[g]):
            s_scr[g, h] = s_st
            m_scr[g, h:h + 1, :] = jnp.broadcast_to(m_st, (1, m_scr.shape[2]))


def _split_bf16(x):
    hi = x.astype(BF16)
    return hi, (x - hi.astype(F32)).astype(BF16)


def _mlstm_chunk(r0, blk, state, cw, causal, tril, triu, xe_scr, v_ref, og_ref, gc_ref, gr_ref, gbc_ref,
                 gbr_ref, ng_ref, y_ref):
    heads, dh, w = MLSTM_HEADS, MLSTM_DH, MLSTM_W
    rows = slice(r0, r0 + blk)
    conv = cw[CONV_WIDTH - 1:CONV_WIDTH, :] * xe_scr[8 + r0:8 + r0 + blk, :]
    for j in range(CONV_WIDTH - 1):
        off = 8 + r0 - (CONV_WIDTH - 1) + j
        conv = conv + cw[j:j + 1, :] * xe_scr[off:off + blk, :]
    qk = conv * _sigmoid(conv)

    gcol = gc_ref[rows, :].astype(F32) + gbc_ref[...]
    grow = gr_ref[:, rows] + gbr_ref[...]
    lc_hi, lc_lo = _split_bf16(_log_sigmoid(gcol))
    lr_hi, lr_lo = _split_bf16(_log_sigmoid(grow))
    bcol = _dot(tril, lc_hi) + _dot(tril, lc_lo)
    brow = _dot(lr_hi, triu) + _dot(lr_lo, triu)
    ones = jnp.ones((blk, dh), BF16)

    new_state = []
    for h in range(heads):
        sl = slice(h * dh, (h + 1) * dh)
        b_c = bcol[:, heads + h:heads + h + 1]
        i_c = gcol[:, h:h + 1]
        b_r = brow[heads + h:heads + h + 1, :]
        i_r = grow[h:h + 1, :]
        s_st, m_st = state[h]

        d_mat = jnp.where(causal, b_c - b_r + i_r, NEG)
        inter = b_c + m_st
        m_t = jnp.maximum(inter, jnp.max(d_mat, axis=-1, keepdims=True))
        w_intra = jnp.exp(d_mat - m_t)
        w_inter = jnp.exp(inter - m_t)

        q_f = qk[:, sl]
        k_f = qk[:, w + h * dh:w + (h + 1) * dh] * (dh ** -0.5)
        q_b = q_f.astype(BF16)
        k_b = k_f.astype(BF16)
        v_ext = jnp.concatenate([v_ref[rows, sl], ones], axis=-1)

        s = _dot_nt(q_b, k_b) * w_intra
        tot = _dot(s.astype(BF16), v_ext) + w_inter * _dot(q_b, s_st.astype(BF16))
        num, den = tot[:, :dh], tot[:, dh:]
        hh = num / jnp.maximum(jnp.abs(den), jnp.exp(-m_t))
        hn = _rms(hh, ng_ref[:, sl])
        y_ref[rows, sl] = (hn * _sigmoid(og_ref[rows, sl].astype(F32))).astype(y_ref.dtype)

        b_last = b_c[blk - 1:blk, :]
        dec = b_last - b_c + i_c
        m_new = jnp.maximum(b_last + m_st, jnp.max(dec, axis=0, keepdims=True))
        w_k = jnp.exp(dec - m_new)
        w_c = jnp.exp(b_last + m_st - m_new)
        kw = k_f * w_k
        new_state.append((w_c * s_st + _dot(kw.T.astype(BF16), v_ext), m_new))
    return new_state


def _mlstm(proj, gates_row, conv_w, gb_col, gb_row, norm_g, *, batch, seq, blk, nsub, group):
    t, npj = proj.shape
    rows = blk * nsub
    w = MLSTM_W
    proj3 = proj.reshape(batch, seq, npj)
    cols = lambda c: (lambda b, i: (b, i, c))
    const2 = lambda b, i: (0, 0)
    y = pl.pallas_call(
        functools.partial(_mlstm_kernel, blk=blk, nsub=nsub, group=group),
        grid=(batch // group, seq // rows),
        in_specs=[pl.BlockSpec((group, rows, 2 * w), cols(OFF_MQ // (2 * w))),
                  pl.BlockSpec((group, rows, w), cols(OFF_MV // w)),
                  pl.BlockSpec((group, rows, w), cols(OFF_MO // w)),
                  pl.BlockSpec((group, rows, IF_PAD), cols(OFF_IF // IF_PAD)),
                  pl.BlockSpec((group, 8, rows), lambda b, i: (b, 0, i)),
                  pl.BlockSpec((CONV_WIDTH, 2 * w), const2),
                  pl.BlockSpec((1, IF_PAD), const2),
                  pl.BlockSpec((8, 1), const2),
                  pl.BlockSpec((1, w), const2)],
        out_specs=pl.BlockSpec((group, rows, w), cols(0)),
        out_shape=jax.ShapeDtypeStruct((batch, seq, w), BF16),
        scratch_shapes=[pltpu.VMEM((group, rows + 8, 2 * w), F32),
                        pltpu.VMEM((group, MLSTM_HEADS, MLSTM_DH, 2 * MLSTM_DH), F32),
                        pltpu.VMEM((group, 8, 128), F32)],
        compiler_params=_cparams("parallel", "arbitrary"),
        name="mlstm",
    )(proj3, proj3, proj3, proj3, gates_row, conv_w, gb_col, gb_row, norm_g)
    return y.reshape(t, w)


def _prefix_max(x):
    n = x.shape[1]
    lane = lax.broadcasted_iota(jnp.int32, x.shape, 1)
    shift = 1
    while shift < n:
        x = jnp.maximum(x, jnp.where(lane >= shift, pltpu.roll(x, shift, 1), NEG))
        shift *= 2
    return x


def _mlstm_rows_kernel(qk_ref, v_ref, og_ref, *rest, blk, group):
    gate_refs = rest[:group]
    cw_ref, bi_ref, bf_ref, ng_ref, y_ref, xe_scr, s_scr, m_scr = rest[group:]
    heads, dh, w = MLSTM_HEADS, MLSTM_DH, MLSTM_W

    @pl.when(pl.program_id(1) == 0)
    def _():
        xe_scr[:, 0:8, :] = jnp.zeros((group, 8, 2 * w), F32)
        s_scr[...] = jnp.zeros_like(s_scr)
        m_scr[...] = jnp.zeros_like(m_scr)

    cw = cw_ref[...]
    causal = lax.broadcasted_iota(jnp.int32, (blk, blk), 0) >= lax.broadcasted_iota(jnp.int32, (blk, blk), 1)
    triu = (lax.broadcasted_iota(jnp.int32, (blk, blk), 0)
            <= lax.broadcasted_iota(jnp.int32, (blk, blk), 1)).astype(BF16)
    ones = jnp.ones((blk, dh), BF16)
    s_in = [[s_scr[g, h] for h in range(heads)] for g in range(group)]
    m_in = [m_scr[g, :, 0:1] for g in range(group)]
    s_out = [[None] * heads for _ in range(group)]
    m_out = [None] * group
    per_seq = []
    for g in range(group):
        xe_scr[g, 8:8 + blk, :] = qk_ref[g].astype(F32)
        conv = cw[CONV_WIDTH - 1:CONV_WIDTH, :] * xe_scr[g, 8:8 + blk, :]
        for j in range(CONV_WIDTH - 1):
            off = 8 - (CONV_WIDTH - 1) + j
            conv = conv + cw[j:j + 1, :] * xe_scr[g, off:off + blk, :]
        xe_scr[g, 0:8, :] = xe_scr[g, blk:blk + 8, :]
        qk = conv * _sigmoid(conv)

        gates = gate_refs[g][...]
        i_r = gates + bi_ref[...]
        lf_hi, lf_lo = _split_bf16(_log_sigmoid(pltpu.roll(gates, heads, 0) + bf_ref[...]))
        b_r = _dot(lf_hi, triu) + _dot(lf_lo, triu)
        m_st = m_in[g]
        a_r = i_r - b_r
        inter = b_r + m_st
        m_t = jnp.maximum(inter, b_r + _prefix_max(a_r))
        b_last = b_r[:, blk - 1:blk]
        dec = b_last - b_r + i_r
        m_new = jnp.maximum(b_last + m_st, jnp.max(dec, axis=1, keepdims=True))
        w_c = jnp.exp(b_last + m_st - m_new)
        m_out[g] = m_new
        pack = jnp.concatenate([b_r - m_t, jnp.exp(inter - m_t), jnp.exp(-m_t), jnp.exp(dec - m_new),
                                jnp.zeros((blk - 32, blk), F32)], axis=0)
        per_seq.append((qk, a_r, pack.T, w_c))

    chains = [(g, h) for h in range(heads) for g in range(group)]
    st = {}
    for g, h in chains:
        qk = per_seq[g][0]
        sl = slice(h * dh, (h + 1) * dh)
        q_b = qk[:, sl].astype(BF16)
        k_f = qk[:, w + h * dh:w + (h + 1) * dh] * (dh ** -0.5)
        v_ext = jnp.concatenate([v_ref[g, :, sl], ones], axis=-1)
        st[g, h] = (q_b, k_f, v_ext, _dot_nt(q_b, k_f.astype(BF16)), _dot(q_b, s_in[g][h].astype(BF16)))
    for g, h in chains:
        q_b, k_f, v_ext, qk_t, q_state = st[g, h]
        _, a_r, cols, _ = per_seq[g]
        u_c, w_inter = cols[:, h:h + 1], cols[:, 8 + h:9 + h]
        w_intra = jnp.exp(jnp.where(causal, u_c + a_r[h:h + 1, :], NEG))
        st[g, h] = (k_f, v_ext, _dot((qk_t * w_intra).astype(BF16), v_ext) + w_inter * q_state)
    for g, h in chains:
        k_f, v_ext, tot = st[g, h]
        _, _, cols, w_c = per_seq[g]
        em_c, w_k = cols[:, 16 + h:17 + h], cols[:, 24 + h:25 + h]
        sl = slice(h * dh, (h + 1) * dh)
        num, den = tot[:, :dh], tot[:, dh:]
        hh = num / jnp.maximum(jnp.abs(den), em_c)
        hn = _rms(hh, ng_ref[:, sl])
        y_ref[g, :, sl] = (hn * _sigmoid(og_ref[g, :, sl].astype(F32))).astype(y_ref.dtype)
        s_out[g][h] = w_c[h:h + 1, :] * s_in[g][h] + _dot((k_f * w_k).T.astype(BF16), v_ext)
    for g in range(group):
        m_scr[g] = jnp.broadcast_to(m_out[g], m_scr.shape[1:])
        for h in range(heads):
            s_scr[g, h] = s_out[g][h]


def _mlstm_rows(proj, gates_t, conv_w, bias_i, bias_f, norm_g, *, batch, seq, blk, group):
    t, npj = proj.shape
    w = MLSTM_W
    proj3 = proj.reshape(batch, seq, npj)
    cols = lambda c: (lambda b, i: (b, i, c))
    const2 = lambda b, i: (0, 0)
    nblk = seq // blk
    gate_specs = [pl.BlockSpec((8, blk), functools.partial(lambda b, i, g: (0, (b * group + g) * nblk + i), g=g))
                  for g in range(group)]
    y = pl.pallas_call(
        functools.partial(_mlstm_rows_kernel, blk=blk, group=group),
        grid=(batch // group, seq // blk),
        in_specs=[pl.BlockSpec((group, blk, 2 * w), cols(OFF_MQ // (2 * w))),
                  pl.BlockSpec((group, blk, w), cols(OFF_MV // w)),
                  pl.BlockSpec((group, blk, w), cols(OFF_MO // w)),
                  *gate_specs,
                  pl.BlockSpec((CONV_WIDTH, 2 * w), const2),
                  pl.BlockSpec((8, 1), const2), pl.BlockSpec((8, 1), const2),
                  pl.BlockSpec((1, w), const2)],
        out_specs=pl.BlockSpec((group, blk, w), cols(0)),
        out_shape=jax.ShapeDtypeStruct((batch, seq, w), BF16),
        scratch_shapes=[pltpu.VMEM((group, blk + 8, 2 * w), F32),
                        pltpu.VMEM((group, MLSTM_HEADS, MLSTM_DH, 2 * MLSTM_DH), F32),
                        pltpu.VMEM((group, 8, 128), F32)],
        compiler_params=_cparams("parallel", "arbitrary", vmem=VMEM_LIMIT_SMALL),
        name="mlstm",
    )(proj3, proj3, proj3, *([gates_t] * group), conv_w, bias_i, bias_f, norm_g)
    return y.reshape(t, w)


def _attnproj_kernel(h_ref, w_ref, seg_ref, gq_ref, gk_ref, o_ref, r_scr, *, dil):
    gw, half = ATTN_GW, ATTN_SLAB // 2
    sub_rows = r_scr.shape[2]
    seg, sub_seg = ATTN_TILE // dil, sub_rows // dil

    def head_norm(x, gain):
        sq = x * x
        hi = sq.astype(BF16)
        lo = (sq - hi.astype(F32)).astype(BF16)
        ss = _dot(hi, seg_ref[...]) + _dot(lo, seg_ref[...])
        return x * lax.rsqrt(ss * (1.0 / ATTN_DH) + EPS) * gain

    low = lax.broadcasted_iota(jnp.int32, (1, ATTN_SLAB), 1) < half
    for s in range(ATTN_TILE // sub_rows):
        rows = slice(s * sub_rows, (s + 1) * sub_rows)
        res = _dot_nt(h_ref[rows, :], w_ref[0])
        q = head_norm(res[:, :gw], gq_ref[...]) * (ATTN_DH ** -0.5)
        k = head_norm(res[:, gw:2 * gw], gk_ref[...])
        slabs = []
        for pair in range(gw // ATTN_SLAB):
            qp = q[:, pair * ATTN_SLAB:(pair + 1) * ATTN_SLAB]
            slabs += [jnp.where(low, qp, 0.0), jnp.where(low, 0.0, qp)]
        slabs += [k[:, c * 128:(c + 1) * 128] for c in range(gw // 128)]
        slabs += [res[:, 2 * gw + c * 128:2 * gw + (c + 1) * 128] for c in range(gw // 128)]
        for c, slab in enumerate(slabs):
            if dil == 1:
                o_ref[rows, c * 128:(c + 1) * 128] = slab.astype(o_ref.dtype)
            else:
                r_scr[s % 2, c] = slab
        if dil > 1:
            for r in range(dil):
                dst = slice(r * seg + s * sub_seg, r * seg + (s + 1) * sub_seg)
                for c in range(r_scr.shape[1]):
                    o_ref[dst, c * 128:(c + 1) * 128] = (
                        r_scr[s % 2, c, pl.ds(r, sub_seg, stride=dil), :].astype(o_ref.dtype))


def _attnproj(h, w, seg_ones, gq, gk, *, layer, group, dilation):
    t, d = h.shape
    wcols = 3 * ATTN_GW
    const2 = lambda i: (0, 0)
    return pl.pallas_call(
        functools.partial(_attnproj_kernel, dil=dilation),
        grid=(t // ATTN_TILE,),
        in_specs=[pl.BlockSpec((ATTN_TILE, d), lambda i: (i, 0)),
                  pl.BlockSpec((1, wcols, d), lambda i: (layer, group, 0)),
                  pl.BlockSpec((ATTN_GW, ATTN_GW), const2),
                  pl.BlockSpec((1, ATTN_GW), const2), pl.BlockSpec((1, ATTN_GW), const2)],
        out_specs=pl.BlockSpec((ATTN_TILE, ATTN_COLS), lambda i: (i, 0)),
        out_shape=jax.ShapeDtypeStruct((t, ATTN_COLS), BF16),
        scratch_shapes=[pltpu.VMEM((2, ATTN_COLS // 128, 512, 128), F32)],
        compiler_params=_cparams("parallel"),
        name=f"attnproj{group}",
    )(h, w, seg_ones, gq, gk)


def _dattn_kernel(q_ref, kc_ref, kp_ref, vc_ref, vp_ref, bias_ref, o_ref, lse_ref,
                  kx_scr, vx_scr, o_scr, l_scr, *, dil):
    blk = ATTN_BLOCK
    per = ATTN_SUB // dil
    first_tile = pl.program_id(1) == 0
    for r in range(dil):
        base = r * (per + 1) * blk
        last = slice((r * per + per - 1) * blk, (r * per + per) * blk)
        mine = slice(r * per * blk, (r + 1) * per * blk)
        kx_scr[base:base + blk, :] = kp_ref[last, :]
        vx_scr[base:base + blk, :] = vp_ref[last, :]
        kx_scr[base + blk:base + (per + 1) * blk, :] = kc_ref[mine, :]
        vx_scr[base + blk:base + (per + 1) * blk, :] = vc_ref[mine, :]

    low = lax.broadcasted_iota(jnp.int32, (1, ATTN_SLAB), 1) < ATTN_SLAB // 2
    no_prev = lax.broadcasted_iota(jnp.int32, (1, 2 * blk), 1) < blk
    for r in range(dil):
        for sub in range(per):
            u = r * per + sub
            win = slice((r * (per + 1) + sub) * blk, (r * (per + 1) + sub + 2) * blk)
            o_slabs, l_slabs = [], []
            for pair in range(ATTN_GW // ATTN_SLAB):
                cols = slice(pair * ATTN_SLAB, (pair + 1) * ATTN_SLAB)
                kx, vx = kx_scr[win, cols], vx_scr[win, cols]
                o_pair, l_pair = [], []
                for h in (2 * pair, 2 * pair + 1):
                    logits = _dot_nt(q_ref[u * blk:(u + 1) * blk, h * ATTN_SLAB:(h + 1) * ATTN_SLAB], kx)
                    logits = logits + bias_ref[h]
                    if sub == 0:
                        logits = jnp.where(first_tile & no_prev, NEG, logits)
                    m = jnp.max(logits, axis=-1, keepdims=True)
                    p = jnp.exp(logits - m)
                    l = jnp.sum(p, axis=-1, keepdims=True)
                    o_pair.append(_dot(p.astype(BF16), vx) / l)
                    l_pair.append(m + jnp.log(l))
                o_slabs.append(jnp.where(low, o_pair[0], o_pair[1]))
                l_slabs.append(jnp.where(low, l_pair[0], l_pair[1]))
            dst = pl.ds(sub * blk * dil + r, blk, stride=dil) if dil > 1 else slice(u * blk, (u + 1) * blk)
            for c in range(ATTN_GW // ATTN_SLAB):
                o_scr[c, dst, :] = o_slabs[c]
                l_scr[c, dst, :] = l_slabs[c]
    for c in range(ATTN_GW // ATTN_SLAB):
        o_ref[:, c * ATTN_SLAB:(c + 1) * ATTN_SLAB] = o_scr[c].astype(o_ref.dtype)
        lse_ref[:, c * ATTN_SLAB:(c + 1) * ATTN_SLAB] = l_scr[c]


def _dattn(aproj, bias, *, seq, group, dilation):
    t = aproj.shape[0]
    tiles = seq // ATTN_TILE
    qw = HEADS_PER_GROUP * ATTN_SLAB
    cq, ck, cv = 0, qw // ATTN_GW, qw // ATTN_GW + 1
    blk = (ATTN_TILE, ATTN_GW)
    cur = lambda c: (lambda b, j: (b * tiles + j, c))
    prev = lambda c: (lambda b, j: (b * tiles + jnp.maximum(j - 1, 0), c))
    xrows = ATTN_TILE + dilation * ATTN_BLOCK
    return pl.pallas_call(
        functools.partial(_dattn_kernel, dil=dilation),
        grid=(t // seq, tiles),
        in_specs=[pl.BlockSpec((ATTN_TILE, qw), cur(cq)),
                  pl.BlockSpec(blk, cur(ck)), pl.BlockSpec(blk, prev(ck)),
                  pl.BlockSpec(blk, cur(cv)), pl.BlockSpec(blk, prev(cv)),
                  pl.BlockSpec((HEADS_PER_GROUP, ATTN_BLOCK, 2 * ATTN_BLOCK), lambda b, j: (0, 0, 0))],
        out_specs=[pl.BlockSpec(blk, cur(0)), pl.BlockSpec(blk, cur(0))],
        out_shape=[jax.ShapeDtypeStruct((t, ATTN_GW), BF16), jax.ShapeDtypeStruct((t, ATTN_GW), F32)],
        scratch_shapes=[pltpu.VMEM((xrows, ATTN_GW), BF16), pltpu.VMEM((xrows, ATTN_GW), BF16),
                        pltpu.VMEM((ATTN_GW // ATTN_SLAB, ATTN_TILE, ATTN_SLAB), F32),
                        pltpu.VMEM((ATTN_GW // ATTN_SLAB, ATTN_TILE, ATTN_SLAB), F32)],
        compiler_params=_cparams("parallel", "arbitrary"),
        name=f"dattn{group}",
    )(aproj, aproj, aproj, aproj, aproj, bias)


def _rel_bucket(n):
    max_exact = REL_BUCKETS // 2
    nf = jnp.maximum(n, 1).astype(F32)
    log_b = max_exact + (jnp.log(nf / max_exact) / math.log(REL_MAX_DIST / max_exact)
                         * (REL_BUCKETS - max_exact)).astype(jnp.int32)
    return jnp.where(n < max_exact, n, jnp.minimum(log_b, REL_BUCKETS - 1))


def _attn_bias(rel_bias, group):
    window, dilation = ATTN_PATTERNS[group]
    steps = window // dilation
    hp = lax.Precision.HIGHEST
    hs = slice(group * HEADS_PER_GROUP, (group + 1) * HEADS_PER_GROUP)
    bucket = _rel_bucket(jnp.arange(steps + 1) * dilation)
    bias_steps = jnp.dot(jax.nn.one_hot(bucket, REL_BUCKETS, dtype=F32), rel_bias[:, hs].astype(F32),
                         precision=hp)
    qi = jnp.arange(ATTN_BLOCK)[:, None]
    ki = jnp.arange(2 * ATTN_BLOCK)[None, :]
    dist = ATTN_BLOCK + qi - ki
    ok = (dist >= 0) & (dist <= steps)
    sel = jax.nn.one_hot(jnp.clip(dist, 0, steps).reshape(-1), steps + 1, dtype=F32)
    bias = jnp.dot(sel, bias_steps, precision=hp).T.reshape(HEADS_PER_GROUP, ATTN_BLOCK, 2 * ATTN_BLOCK)
    return jnp.where(ok[None], bias, NEG)


def _merge_kernel(ya_ref, yb0_ref, yb1_ref, yb2_ref, l0_ref, l1_ref, l2_ref, gu_ref, gv_ref, gate_ref,
                  x_ref, wa_ref, wb_ref, wc_ref, wo_ref, ws_ref, bs_ref, gg_ref, o_ref, yc_scr, *, tm):
    d = x_ref.shape[1]
    l0, l1, l2 = l0_ref[...], l1_ref[...], l2_ref[...]
    mx = jnp.maximum(jnp.maximum(l0, l1), l2)
    e0, e1, e2 = jnp.exp(l0 - mx), jnp.exp(l1 - mx), jnp.exp(l2 - mx)
    inv = 1.0 / (e0 + e1 + e2)
    yb = jnp.concatenate([(yb0_ref[...].astype(F32) * (e0 * inv)).astype(BF16),
                          (yb1_ref[...].astype(F32) * (e1 * inv)).astype(BF16),
                          (yb2_ref[...].astype(F32) * (e2 * inv)).astype(BF16)], axis=-1)

    for j in range(tm // GMLP_CHUNK):
        rows = slice(j * GMLP_CHUNK, (j + 1) * GMLP_CHUNK)
        for g in range(GMLP_GROUPS):
            cols = slice(g * GMLP_GC, (g + 1) * GMLP_GC)
            u = jax.nn.gelu(gu_ref[rows, cols].astype(F32))
            v = _rms(jax.nn.gelu(gv_ref[rows, cols].astype(F32)), gg_ref[:, cols])
            mixed = _dot(ws_ref[g], v.astype(BF16)) + bs_ref[g]
            yc_scr[rows, cols] = (u * mixed).astype(BF16)

    def gate2(k):
        return jnp.tanh(0.5 * gate_ref[:, k * d:(k + 1) * d].astype(F32)) + 1.0

    merged2 = gate2(0) * _dot(ya_ref[...], wa_ref[...])
    merged2 = merged2 + gate2(1) * _dot(yb, wb_ref[...])
    merged2 = merged2 + gate2(2) * _dot(yc_scr[...], wc_ref[...])
    o_ref[...] = x_ref[...] + 0.5 * _dot(merged2.astype(BF16), wo_ref[...])


def _merge(ya, ybs, lses, proj, x2d, wa, wb, wc, wo, ws, bsb, gg, *, tm):
    t, d = x2d.shape
    row = lambda c: (lambda i: (i, c))
    full2 = lambda i: (0, 0)
    full3 = lambda i: (0, 0, 0)
    gspec = pl.BlockSpec((tm, ATTN_GW), row(0))
    return pl.pallas_call(
        functools.partial(_merge_kernel, tm=tm),
        grid=(t // tm,),
        in_specs=[pl.BlockSpec((tm, MLSTM_W), row(0)),
                  gspec, gspec, gspec, gspec, gspec, gspec,
                  pl.BlockSpec((tm, GMLP_W), row(OFF_GU // GMLP_W)),
                  pl.BlockSpec((tm, GMLP_W), row(OFF_GV // GMLP_W)),
                  pl.BlockSpec((tm, N_BRANCH * d), row(OFF_GATE // (N_BRANCH * d))),
                  pl.BlockSpec((tm, d), row(0)),
                  pl.BlockSpec(wa.shape, full2), pl.BlockSpec(wb.shape, full2),
                  pl.BlockSpec(wc.shape, full2), pl.BlockSpec(wo.shape, full2),
                  pl.BlockSpec(ws.shape, full3), pl.BlockSpec(bsb.shape, full3),
                  pl.BlockSpec(gg.shape, full2)],
        out_specs=pl.BlockSpec((tm, d), row(0)),
        out_shape=jax.ShapeDtypeStruct((t, d), F32),
        scratch_shapes=[pltpu.VMEM((tm, GMLP_W), BF16)],
        compiler_params=_cparams("parallel"),
        name="merge",
    )(ya, *ybs, *lses, proj, proj, proj, x2d, wa, wb, wc, wo, ws, bsb, gg)


def _memkv_kernel(mem_ref, g_ref, w_ref, gk_ref, k_ref, v_ref):
    dh, w = XATTN_DH, XATTN_W
    kv = _dot(_rms(mem_ref[0], g_ref[...]).astype(BF16), w_ref[...])
    for h in range(XATTN_HEADS):
        sl = slice(h * dh, (h + 1) * dh)
        k_ref[0, :, sl] = _rms(kv[:, sl], gk_ref[...]).astype(k_ref.dtype)
    v_ref[0] = kv[:, w:].astype(v_ref.dtype)


def _memkv(mem, gain, w_kv, gk):
    b, m, d = mem.shape
    full2 = lambda i: (0, 0)
    return pl.pallas_call(
        _memkv_kernel,
        grid=(b,),
        in_specs=[pl.BlockSpec((1, m, d), lambda i: (i, 0, 0)),
                  pl.BlockSpec((1, d), full2),
                  pl.BlockSpec(w_kv.shape, full2),
                  pl.BlockSpec((1, XATTN_DH), full2)],
        out_specs=[pl.BlockSpec((1, m, XATTN_W), lambda i: (i, 0, 0)),
                   pl.BlockSpec((1, m, XATTN_W), lambda i: (i, 0, 0))],
        out_shape=[jax.ShapeDtypeStruct((b, m, XATTN_W), BF16),
                   jax.ShapeDtypeStruct((b, m, XATTN_W), BF16)],
        compiler_params=_cparams("parallel", vmem=VMEM_LIMIT_SMALL),
        name="memkv",
    )(mem, gain, w_kv, gk)


def _route(logits):
    tm = logits.shape[1]
    e = jnp.exp(logits - jnp.max(logits, axis=0, keepdims=True))
    probs = e / jnp.sum(e, axis=0, keepdims=True)
    rowi = lax.broadcasted_iota(jnp.int32, (8, tm), 0)
    real = rowi < EXPERTS_PER_GROUP
    tops = []
    for g in range(N_EXPERT_GROUPS):
        pg = jnp.where(real, probs[8 * g:8 * g + 8, :], -0.5)
        m1 = jnp.max(pg, axis=0, keepdims=True)
        i1 = jnp.min(jnp.where(pg == m1, rowi, 8), axis=0, keepdims=True)
        pg2 = jnp.where(rowi == i1, -1.0, pg)
        m2 = jnp.max(pg2, axis=0, keepdims=True)
        i2 = jnp.min(jnp.where(pg2 == m2, rowi, 8), axis=0, keepdims=True)
        tops.append((m1, i1, m2, i2))
    best = jnp.zeros((1, tm), jnp.int32)
    best_score = tops[0][0] + tops[0][2]
    for g in range(1, N_EXPERT_GROUPS):
        score = tops[g][0] + tops[g][2]
        better = score > best_score
        best = jnp.where(better, g, best)
        best_score = jnp.where(better, score, best_score)
    m1, i1, m2, i2 = tops[0]
    for g in range(1, N_EXPERT_GROUPS):
        m1, i1, m2, i2 = (jnp.where(best == g, new, old) for new, old in zip(tops[g], (m1, i1, m2, i2)))
    tot = m1 + m2
    base = best * EXPERTS_PER_GROUP
    return base + i1, base + i2, m1 / tot, m2 / tot


def _pack_bf16_pairs(x):
    n = x.shape[1] // 2
    hi = lax.bitcast_convert_type(x[:, :n].astype(BF16).astype(F32), jnp.uint32)
    lo = lax.bitcast_convert_type(x[:, n:].astype(BF16).astype(F32), jnp.uint32)
    return hi | (lo >> 16)


def _unpack_bf16_pairs(p):
    hi = lax.bitcast_convert_type(p & jnp.uint32(0xFFFF0000), F32)
    lo = lax.bitcast_convert_type(p << 16, F32)
    return hi, lo


def _store_row_chunks(ref, packed):
    for j in range(ROW_CHUNKS):
        ref[j] = packed[:, j * 128:(j + 1) * 128]


def _load_row_chunks(ref):
    return jnp.concatenate([ref[j] for j in range(ROW_CHUNKS)], axis=-1)


def _xattn_kernel(x_ref, k_ref, v_ref, gx_ref, wq_ref, gq_ref, wo_ref, gf_ref, rw_ref, rb_ref,
                  xo_ref, hf_ref, eidx_ref, wts_ref, *, sub):
    dh = XATTN_DH
    rw = rw_ref[...]
    rw_hi, rw_lo = _split_bf16(rw)
    for s in range(x_ref.shape[0] // sub):
        rows = slice(s * sub, (s + 1) * sub)
        x = x_ref[rows, :]
        q = _dot(_rms(x, gx_ref[...]).astype(BF16), wq_ref[...])
        outs = []
        for h in range(XATTN_HEADS):
            sl = slice(h * dh, (h + 1) * dh)
            q_h = (_rms(q[:, sl], gq_ref[...]) * (dh ** -0.5)).astype(BF16)
            logits = _dot_nt(q_h, k_ref[0, :, sl])
            p = jnp.exp(logits - jnp.max(logits, axis=-1, keepdims=True))
            o = _dot(p.astype(BF16), v_ref[0, :, sl]) / jnp.sum(p, axis=-1, keepdims=True)
            outs.append(o.astype(BF16))
        xn = x + _dot(jnp.concatenate(outs, axis=-1), wo_ref[...])
        xo_ref[rows, :] = xn
        hf = _rms(xn, gf_ref[...])
        packed = _pack_bf16_pairs(hf)
        for j in range(ROW_CHUNKS):
            hf_ref[j, rows, :] = packed[:, j * 128:(j + 1) * 128]
        hf_hi, hf_lo = _split_bf16(hf)
        logits_t = _dot_nt(rw_hi, hf_hi) + _dot_nt(rw_hi, hf_lo) + _dot_nt(rw_lo, hf_hi) + rb_ref[...]
        e1, e2, w1, w2 = _route(logits_t)
        eidx_ref[:, rows] = jnp.concatenate([e1, e2, jnp.zeros((6, sub), jnp.int32)], axis=0)
        wts_ref[:, rows] = jnp.concatenate([w1, w2, jnp.zeros((6, sub), F32)], axis=0)


def _xattn(x2d, k, v, gx, wq, gq, wo, gf, rw_t, rb, *, seq, tm):
    t, d = x2d.shape
    per_b = seq // tm
    full2 = lambda i: (0, 0)
    kv_spec = pl.BlockSpec((1,) + k.shape[1:], lambda i: (i // per_b, 0, 0))
    return pl.pallas_call(
        functools.partial(_xattn_kernel, sub=min(tm, XATTN_SUB)),
        grid=(t // tm,),
        in_specs=[pl.BlockSpec((tm, d), lambda i: (i, 0)), kv_spec, kv_spec,
                  pl.BlockSpec((1, d), full2), pl.BlockSpec(wq.shape, full2),
                  pl.BlockSpec((1, XATTN_DH), full2), pl.BlockSpec(wo.shape, full2),
                  pl.BlockSpec((1, d), full2), pl.BlockSpec(rw_t.shape, full2),
                  pl.BlockSpec(rb.shape, full2)],
        out_specs=[pl.BlockSpec((tm, d), lambda i: (i, 0)),
                   pl.BlockSpec((ROW_CHUNKS, tm, 128), lambda i: (0, i, 0)),
                   pl.BlockSpec((8, tm), lambda i: (0, i)),
                   pl.BlockSpec((8, tm), lambda i: (0, i))],
        out_shape=[jax.ShapeDtypeStruct((t, d), F32),
                   jax.ShapeDtypeStruct((ROW_CHUNKS, t, 128), jnp.uint32),
                   jax.ShapeDtypeStruct((8, t), jnp.int32),
                   jax.ShapeDtypeStruct((8, t), F32)],
        compiler_params=_cparams("parallel"),
        name="xattn_router",
    )(x2d, k, v, gx, wq, gq, wo, gf, rw_t, rb)


def _moe_plan_kernel(eidx_ref, i1_ref, i2_ref, te_ref, na_ref, cnt_scr, carry_scr, *, tb, tm, plane_rows):
    ne = N_EXPERTS
    hp = lax.Precision.HIGHEST
    phase, j = pl.program_id(0), pl.program_id(1)
    rows = lax.broadcasted_iota(jnp.int32, (ne, tb), 0)
    oh1 = rows == eidx_ref[0:1, :]
    oh2 = rows == eidx_ref[1:2, :]
    a = oh1.astype(F32) + oh2.astype(F32)
    blk_cnt = jnp.broadcast_to(jnp.sum(a, axis=1, keepdims=True), cnt_scr.shape)

    @pl.when((phase == 0) & (j == 0))
    def _():
        cnt_scr[...] = jnp.zeros_like(cnt_scr)

    @pl.when(phase == 0)
    def _():
        cnt_scr[...] += blk_cnt

    @pl.when((phase == 1) & (j == 0))
    def _():
        padded = jnp.ceil(cnt_scr[...] * (1.0 / tm)) * tm
        er = lax.broadcasted_iota(jnp.int32, (ne, ne), 0)
        ec = lax.broadcasted_iota(jnp.int32, (ne, ne), 1)
        off = jnp.dot((ec < er).astype(F32), padded, precision=hp, preferred_element_type=F32)
        carry_scr[...] = off
        seg_end = (off + padded)[:, 0:1]
        tile_start = lax.broadcasted_iota(jnp.int32, (ne, te_ref.shape[1]), 1).astype(F32) * tm
        te = jnp.sum((seg_end <= tile_start).astype(F32), axis=0, keepdims=True)
        te_ref[...] = jnp.broadcast_to(jnp.minimum(te, ne - 1.0), te_ref.shape).astype(jnp.int32)
        total = jnp.sum(padded[:, 0:1], axis=0, keepdims=True)
        na_ref[...] = jnp.broadcast_to(total * (1.0 / tm), na_ref.shape).astype(jnp.int32)

    @pl.when(phase == 1)
    def _():
        before = (lax.broadcasted_iota(jnp.int32, (tb, tb), 0)
                  < lax.broadcasted_iota(jnp.int32, (tb, tb), 1)).astype(BF16)
        rank = carry_scr[:, 0:1] + _dot(a.astype(BF16), before)
        d1 = jnp.sum(jnp.where(oh1, rank, 0.0), axis=0, keepdims=True).astype(jnp.int32)
        d2 = jnp.sum(jnp.where(oh2, rank, 0.0), axis=0, keepdims=True).astype(jnp.int32)
        plane = lax.broadcasted_iota(jnp.int32, (8, tb), 0) * plane_rows
        i1_ref[...] = jnp.where(plane < ROW_CHUNKS * plane_rows, plane + d1, 0)
        i2_ref[...] = jnp.where(plane < ROW_CHUNKS * plane_rows, plane + d2, 0)
        carry_scr[...] += blk_cnt


def _moe_plan(eidx, *, tm, n_tiles, tb=512):
    t = eidx.shape[1]
    ntp = -(-n_tiles // 128) * 128
    return pl.pallas_call(
        functools.partial(_moe_plan_kernel, tb=tb, tm=tm, plane_rows=n_tiles * tm),
        grid=(2, t // tb),
        in_specs=[pl.BlockSpec((8, tb), lambda p, j: (0, j))],
        out_specs=[pl.BlockSpec((8, tb), lambda p, j: (0, j * p)),
                   pl.BlockSpec((8, tb), lambda p, j: (0, j * p)),
                   pl.BlockSpec((8, ntp), lambda p, j: (0, 0)),
                   pl.BlockSpec((8, 128), lambda p, j: (0, 0))],
        out_shape=[jax.ShapeDtypeStruct((8, t), jnp.int32),
                   jax.ShapeDtypeStruct((8, t), jnp.int32),
                   jax.ShapeDtypeStruct((8, ntp), jnp.int32),
                   jax.ShapeDtypeStruct((8, 128), jnp.int32)],
        scratch_shapes=[pltpu.VMEM((N_EXPERTS, 128), F32), pltpu.VMEM((N_EXPERTS, 128), F32)],
        compiler_params=_cparams("arbitrary", "arbitrary", vmem=VMEM_LIMIT_SMALL),
        name="moe_plan",
    )(eidx)


def _sc_mesh():
    return plsc.VectorSubcoreMesh(core_axis_name="c", subcore_axis_name="s",
                                  num_cores=SC_CORES, num_subcores=SC_SUBCORES)


def _sc_index_spec(tokens):
    nb = tokens // SC_WINDOW
    return pl.BlockSpec((1, SC_WINDOW), lambda i: (i // nb, i % nb))


def _sc_dispatch(rows, i1, i2, n_out):
    n = rows.shape[0]
    tokens = i1.shape[1]

    @functools.partial(pl.kernel, out_type=jax.ShapeDtypeStruct((n_out, 128), rows.dtype), mesh=_sc_mesh(),
                       name="moe_dispatch")
    def k(x_hbm, i1_hbm, i2_hbm, o_hbm):
        def body(x_vmem, i1_vmem, i2_vmem):
            pltpu.sync_copy(x_vmem, o_hbm.at[i1_vmem.at[0]])
            pltpu.sync_copy(x_vmem, o_hbm.at[i2_vmem.at[0]])

        pltpu.emit_pipeline(
            body, grid=(n // SC_WINDOW,),
            in_specs=[pl.BlockSpec((SC_WINDOW, 128), lambda i: (i, 0)),
                      _sc_index_spec(tokens), _sc_index_spec(tokens)],
            out_specs=[],
            core_axis_name=("c", "s"), dimension_semantics=(pltpu.PARALLEL,),
        )(x_hbm, i1_hbm, i2_hbm)

    return k(rows, i1, i2)


def _sc_collect(table, i1, i2):
    tokens = i1.shape[1]
    n = ROW_CHUNKS * tokens
    out = jax.ShapeDtypeStruct((n, 128), table.dtype)

    @functools.partial(pl.kernel, out_type=(out, out), mesh=_sc_mesh(), name="moe_collect")
    def k(t_hbm, i1_hbm, i2_hbm, o1_hbm, o2_hbm):
        def body(i1_vmem, i2_vmem, o1_vmem, o2_vmem):
            pltpu.sync_copy(t_hbm.at[i1_vmem.at[0]], o1_vmem)
            pltpu.sync_copy(t_hbm.at[i2_vmem.at[0]], o2_vmem)

        pltpu.emit_pipeline(
            body, grid=(n // SC_WINDOW,),
            in_specs=[_sc_index_spec(tokens), _sc_index_spec(tokens)],
            out_specs=[pl.BlockSpec((SC_WINDOW, 128), lambda i: (i, 0)),
                       pl.BlockSpec((SC_WINDOW, 128), lambda i: (i, 0))],
            core_axis_name=("c", "s"), dimension_semantics=(pltpu.PARALLEL,),
        )(i1_hbm, i2_hbm, o1_hbm, o2_hbm)

    return k(table, i1, i2)


def _experts_kernel(te_ref, na_ref, xs_ref, wg_ref, wu_ref, wd_ref, y_ref, wg_scr, wu_scr, wd_scr):
    i = pl.program_id(0)
    active = i < na_ref[0]

    @pl.when(active & ((i == 0) | (te_ref[i] != te_ref[jnp.maximum(i - 1, 0)])))
    def _():
        wg_scr[...] = wg_ref[0, 0].astype(BF16)
        wu_scr[...] = wu_ref[0, 0].astype(BF16)
        wd_scr[...] = wd_ref[0, 0].astype(BF16)

    @pl.when(active)
    def _():
        hi, lo = _unpack_bf16_pairs(_load_row_chunks(xs_ref))
        h = jnp.concatenate([hi, lo], axis=-1).astype(BF16)
        up = _dot(h, wg_scr[...])
        act = up * _sigmoid(up) * _dot(h, wu_scr[...])
        _store_row_chunks(y_ref, _pack_bf16_pairs(_dot(act.astype(BF16), wd_scr[...])))


def _experts(tile_expert, n_active, xs, wg, wu, wd, *, layer, tm):
    n_tiles = tile_expert.shape[0]
    _, _, d, dff = wg.shape
    rows = lambda i, te, na: (0, jnp.minimum(i, na[0] - 1), 0)
    expert = lambda i, te, na: (layer, te[i], 0, 0)
    return pl.pallas_call(
        _experts_kernel,
        grid_spec=pltpu.PrefetchScalarGridSpec(
            num_scalar_prefetch=2,
            grid=(n_tiles,),
            in_specs=[pl.BlockSpec((ROW_CHUNKS, tm, 128), rows),
                      pl.BlockSpec((1, 1, d, dff), expert),
                      pl.BlockSpec((1, 1, d, dff), expert),
                      pl.BlockSpec((1, 1, dff, d), expert)],
            out_specs=pl.BlockSpec((ROW_CHUNKS, tm, 128), rows),
            scratch_shapes=[pltpu.VMEM((d, dff), BF16), pltpu.VMEM((d, dff), BF16), pltpu.VMEM((dff, d), BF16)]),
        out_shape=jax.ShapeDtypeStruct(xs.shape, xs.dtype),
        compiler_params=_cparams("arbitrary"),
        name="moe_experts",
    )(tile_expert, n_active, xs, wg, wu, wd)


def _moe_combined_halves(x_ref, y1_ref, y2_ref, w_ref):
    half = x_ref.shape[1] // 2
    hi1, lo1 = _unpack_bf16_pairs(_load_row_chunks(y1_ref))
    hi2, lo2 = _unpack_bf16_pairs(_load_row_chunks(y2_ref))
    tm = x_ref.shape[0]
    w_cols = jnp.concatenate([w_ref[...], jnp.zeros((128 - w_ref.shape[0], tm), F32)], axis=0).T
    w1, w2 = w_cols[:, 0:1], w_cols[:, 1:2]
    return x_ref[:, :half] + w1 * hi1 + w2 * hi2, x_ref[:, half:] + w1 * lo1 + w2 * lo2


def _moe_combine_kernel(x_ref, y1_ref, y2_ref, w_ref, o_ref):
    half = x_ref.shape[1] // 2
    o_ref[:, :half], o_ref[:, half:] = _moe_combined_halves(x_ref, y1_ref, y2_ref, w_ref)


def _moe_combine(x2d, y1, y2, wts, *, tm):
    t, d = x2d.shape
    chunk_spec = pl.BlockSpec((ROW_CHUNKS, tm, 128), lambda i: (0, i, 0))
    return pl.pallas_call(
        _moe_combine_kernel,
        grid=(t // tm,),
        in_specs=[pl.BlockSpec((tm, d), lambda i: (i, 0)), chunk_spec, chunk_spec,
                  pl.BlockSpec((wts.shape[0], tm), lambda i: (0, i))],
        out_specs=pl.BlockSpec((tm, d), lambda i: (i, 0)),
        out_shape=jax.ShapeDtypeStruct((t, d), F32),
        compiler_params=_cparams("parallel", vmem=VMEM_LIMIT_SMALL),
        name="moe_combine",
    )(x2d, y1, y2, wts)


def _moe(hf_rows, eidx, wg, wu, wd, *, layer):
    t = hf_rows.shape[1]
    tm = MOE_TM
    n_tiles = 2 * t // tm + N_EXPERTS
    plane = n_tiles * tm
    i1, i2, te, na = _moe_plan(eidx, tm=tm, n_tiles=n_tiles)
    xs = _sc_dispatch(hf_rows.reshape(ROW_CHUNKS * t, 128), i1, i2, ROW_CHUNKS * plane)
    ys = _experts(te[0, :n_tiles], na[0, :1], xs.reshape(ROW_CHUNKS, plane, 128), wg, wu, wd,
                  layer=layer, tm=tm)
    y1, y2 = _sc_collect(ys.reshape(ROW_CHUNKS * plane, 128), i1, i2)
    return y1.reshape(ROW_CHUNKS, t, 128), y2.reshape(ROW_CHUNKS, t, 128)


W_ROWS = 256


def _w_rows_kernel(start_ref, valid_ref, w_ref, o_ref):
    del start_ref
    row = lax.broadcasted_iota(jnp.int32, w_ref.shape[1:], 0)
    o_ref[0] = jnp.where(row < valid_ref[pl.program_id(1)], w_ref[0], 0.0).astype(o_ref.dtype)


def _w_rows(w_t, starts, valid):
    depth, _, d = w_t.shape
    nblk = len(starts)
    return pl.pallas_call(
        _w_rows_kernel,
        grid_spec=pltpu.PrefetchScalarGridSpec(
            num_scalar_prefetch=2,
            grid=(depth, nblk),
            in_specs=[pl.BlockSpec((pl.Element(1), pl.Element(W_ROWS), pl.Element(d)),
                                   lambda l, c, st, va: (l, pl.multiple_of(st[c], 8), 0))],
            out_specs=pl.BlockSpec((1, W_ROWS, d), lambda l, c, st, va: (l, c, 0))),
        out_shape=jax.ShapeDtypeStruct((depth, nblk * W_ROWS, d), BF16),
        compiler_params=_cparams("parallel", "arbitrary", vmem=VMEM_LIMIT_SMALL),
        name="w_in_rows",
    )(jnp.asarray(starts, jnp.int32), jnp.asarray(valid, jnp.int32), w_t)


def _w_in_layout(w_in):
    w_t = jnp.swapaxes(w_in, 1, 2)
    src_if = 4 * MLSTM_W
    src_a = src_if + 2 * MLSTM_HEADS
    src_g = src_a + 3 * ATTN_W
    starts = list(range(0, src_if, W_ROWS)) + [src_g + k * W_ROWS for k in range((OFF_IF - OFF_GU) // W_ROWS)]
    valid = [W_ROWS] * len(starts)
    starts.append(src_if)
    valid.append(2 * MLSTM_HEADS)
    assert len(starts) * W_ROWS == N_PROJ and ATTN_GW == W_ROWS
    a_starts = [src_a + j * ATTN_W + g * ATTN_GW for g in range(len(ATTN_PATTERNS)) for j in range(3)]
    return _w_rows(w_t, starts, valid), _w_rows(w_t, a_starts, [W_ROWS] * len(a_starts))


def kernel(x, mem, norm_mix, w_in, mlstm_conv, mlstm_gate_b, mlstm_norm, attn_qk_norm, gmlp_norm, gmlp_ws,
           gmlp_bs, w_branch_a, w_branch_b, w_branch_c, w_out, rel_bias, norm_xattn, norm_mem, w_xq, w_xkv,
           xattn_qk_norm, w_xo, norm_ffn, router_w, router_b, w_expert_gate, w_expert_up, w_expert_down):
    b, s, d = x.shape
    t = b * s
    depth = w_in.shape[0]
    x2d = x.reshape(t, d)

    biases = [_attn_bias(rel_bias, g) for g in range(len(ATTN_PATTERNS))]
    rw_t = jnp.zeros((N_EXPERT_GROUPS, 8, d), F32).at[:, :EXPERTS_PER_GROUP].set(
        router_w.T.reshape(N_EXPERT_GROUPS, EXPERTS_PER_GROUP, d)).reshape(ROUTER_ROWS, d)
    rb = jnp.full((N_EXPERT_GROUPS, 8), NEG, F32).at[:, :EXPERTS_PER_GROUP].set(
        router_b.astype(F32).reshape(N_EXPERT_GROUPS, EXPERTS_PER_GROUP)).reshape(ROUTER_ROWS, 1)
    tril = jnp.tril(jnp.ones((GMLP_CHUNK, GMLP_CHUNK), bool))
    head_of = jnp.arange(ATTN_GW) // ATTN_DH
    seg_ones = (head_of[:, None] == head_of[None, :]).astype(BF16)

    w_main, w_attn = _w_in_layout(w_in)

    pending_moe = None
    for l in range(depth):
        if pending_moe is None:
            proj, h_mix, gates_t = _inproj(x2d, norm_mix[l][None], w_main, layer=l, tm=1024, tn=3072)
        else:
            proj, h_mix, gates_t, x2d = _inproj(x2d, norm_mix[l][None], w_main, layer=l, tm=1024, tn=1536,
                                                pending_moe=pending_moe)
        gq = jnp.tile(attn_qk_norm[l, 0], HEADS_PER_GROUP)[None]
        gk = jnp.tile(attn_qk_norm[l, 1], HEADS_PER_GROUP)[None]

        nh = MLSTM_HEADS
        bias_i = jnp.zeros((8, 1), F32).at[:nh, 0].set(mlstm_gate_b[l, :nh])
        bias_f = jnp.zeros((8, 1), F32).at[:nh, 0].set(mlstm_gate_b[l, nh:])
        ya = _mlstm_rows(proj, gates_t, mlstm_conv[l], bias_i, bias_f, mlstm_norm[l][None],
                         batch=b, seq=s, blk=MLSTM_BLOCK, group=MLSTM_GROUP)

        ybs, lses = [], []
        for g, (_, dilation) in enumerate(ATTN_PATTERNS):
            aproj = _attnproj(h_mix, w_attn, seg_ones, gq, gk, layer=l, group=g, dilation=dilation)
            o, lse = _dattn(aproj, biases[g], seq=s, group=g, dilation=dilation)
            ybs.append(o)
            lses.append(lse)

        ws = jnp.where(tril, gmlp_ws[l], 0.0).astype(BF16)
        bsb = jnp.broadcast_to(gmlp_bs[l][:, :, None], (GMLP_GROUPS, GMLP_CHUNK, GMLP_GC)).astype(F32)
        x2d = _merge(ya, ybs, lses, proj, x2d, w_branch_a[l].astype(BF16), w_branch_b[l].astype(BF16),
                     w_branch_c[l].astype(BF16), w_out[l].astype(BF16), ws, bsb, gmlp_norm[l][None], tm=512)

        k_mem, v_mem = _memkv(mem, norm_mem[l][None], w_xkv[l].astype(BF16), xattn_qk_norm[l, 1][None])
        x2d, hf_rows, eidx, wts = _xattn(x2d, k_mem, v_mem, norm_xattn[l][None], w_xq[l].astype(BF16),
                                         xattn_qk_norm[l, 0][None], w_xo[l].astype(BF16), norm_ffn[l][None],
                                         rw_t, rb, seq=s, tm=1024)

        y1, y2 = _moe(hf_rows, eidx, w_expert_gate, w_expert_up, w_expert_down, layer=l)
        pending_moe = (y1, y2, wts)

    return _moe_combine(x2d, *pending_moe, tm=512).reshape(b, s, d)
```

```python
import functools
import math

import jax
import jax.numpy as jnp
import numpy as np
from jax import lax
from jax.experimental import pallas as pl
from jax.experimental.pallas import tpu as pltpu
from jax.experimental.pallas import tpu_sc as plsc

F32 = jnp.float32
BF16 = jnp.bfloat16

EPS = 1e-6
NEG = -1e30

MLSTM_HEADS = 4
MLSTM_DH = 128
MLSTM_W = MLSTM_HEADS * MLSTM_DH
CONV_WIDTH = 4
MLSTM_BLOCK = 128
MLSTM_GROUP = 4

ATTN_PATTERNS = ((128, 1), (512, 4), (2048, 16))
HEADS_PER_GROUP = 4
ATTN_DH = 64
ATTN_GW = HEADS_PER_GROUP * ATTN_DH
ATTN_W = len(ATTN_PATTERNS) * ATTN_GW
ATTN_BLOCK = 128
REL_BUCKETS = 32
REL_MAX_DIST = 2048

GMLP_GROUPS = 4
GMLP_GC = 128
GMLP_W = GMLP_GROUPS * GMLP_GC
GMLP_CHUNK = 128

XATTN_HEADS = 4
XATTN_DH = 128
XATTN_W = XATTN_HEADS * XATTN_DH
XATTN_SUB = 1024

N_EXPERTS = 16
N_EXPERT_GROUPS = 4
EXPERTS_PER_GROUP = 4
ROUTER_ROWS = 8 * N_EXPERT_GROUPS

N_BRANCH = 3

MOE_TM = 1024
ROW_CHUNKS = 4
SC_CORES, SC_SUBCORES = 2, 16
SC_WINDOW = 128

OFF_MQ, OFF_MK, OFF_MV, OFF_MO = 0, 512, 1024, 1536
OFF_GU, OFF_GV = 2048, 2560
OFF_GATE = 3072
OFF_IF = 6144
IF_PAD = 256
N_PROJ = OFF_IF + IF_PAD

ATTN_TILE = 2048
ATTN_SUB = ATTN_TILE // ATTN_BLOCK
ATTN_SLAB = 2 * ATTN_DH
ATTN_COLS = HEADS_PER_GROUP * ATTN_SLAB + 2 * ATTN_GW

VMEM_LIMIT = 48 * 1024 * 1024
VMEM_LIMIT_INPROJ = 56 * 1024 * 1024
VMEM_LIMIT_SMALL = 24 * 1024 * 1024

INPROJ_TM, INPROJ_TN = 1024, 3072
ATTNPROJ_SUB = 512
MERGE_TM = 512
XATTN_TM = 1024
COMBINE_TM = 512


def _cparams(*sem, vmem=VMEM_LIMIT):
    return pltpu.CompilerParams(dimension_semantics=sem, vmem_limit_bytes=vmem)


def _rms(x, gain):
    return x * lax.rsqrt(jnp.mean(x * x, axis=-1, keepdims=True) + EPS) * gain


def _sigmoid(x):
    return 0.5 * jnp.tanh(0.5 * x) + 0.5


def _dot(a, b):
    return jnp.dot(a, b, preferred_element_type=F32)


def _dot_nt(a, b):
    return lax.dot_general(a, b, (((1,), (1,)), ((), ())), preferred_element_type=F32)


def _inproj_kernel(x_ref, g_ref, w_ref, wg_ref, o_ref, h_ref, gt_ref):
    @pl.when(pl.program_id(1) == 0)
    def _():
        h = _rms(x_ref[...], g_ref[...]).astype(BF16)
        h_ref[...] = h
        gt_ref[...] = _dot_nt(wg_ref[0, 0:128, :], h)[:gt_ref.shape[0], :]

    o_ref[...] = _dot_nt(h_ref[...], w_ref[0]).astype(o_ref.dtype)


def _inproj(x2d, gain, w, *, layer, tm, tn):
    t, d = x2d.shape
    n = OFF_IF
    return pl.pallas_call(
        _inproj_kernel,
        grid=(t // tm, n // tn),
        in_specs=[pl.BlockSpec((tm, d), lambda i, j: (i, 0)),
                  pl.BlockSpec((1, d), lambda i, j: (0, 0)),
                  pl.BlockSpec((1, tn, d), lambda i, j: (layer, j, 0)),
                  pl.BlockSpec((1, IF_PAD, d), lambda i, j: (layer, OFF_IF // IF_PAD, 0))],
        out_specs=[pl.BlockSpec((tm, tn), lambda i, j: (i, j)),
                   pl.BlockSpec((tm, d), lambda i, j: (i, 0)),
                   pl.BlockSpec((8, tm), lambda i, j: (0, i))],
        out_shape=[jax.ShapeDtypeStruct((t, n), BF16), jax.ShapeDtypeStruct((t, d), BF16),
                   jax.ShapeDtypeStruct((8, t), F32)],
        compiler_params=_cparams("parallel", "arbitrary", vmem=VMEM_LIMIT_INPROJ),
        name="inproj",
    )(x2d, gain, w, w)


def _log_sigmoid(x):
    return jnp.minimum(x, 0.0) - jnp.log(1.0 + jnp.exp(-jnp.abs(x)))


def _split_bf16(x):
    hi = x.astype(BF16)
    return hi, (x - hi.astype(F32)).astype(BF16)


def _prefix_max(x):
    n = x.shape[1]
    lane = lax.broadcasted_iota(jnp.int32, x.shape, 1)
    shift = 1
    while shift < n:
        x = jnp.maximum(x, jnp.where(lane >= shift, pltpu.roll(x, shift, 1), NEG))
        shift *= 2
    return x


def _mlstm_rows_kernel(qk_ref, v_ref, og_ref, *rest, blk, group):
    gate_refs = rest[:group]
    cw_ref, bi_ref, bf_ref, ng_ref, y_ref, xe_scr, s_scr, m_scr = rest[group:]
    heads, dh, w = MLSTM_HEADS, MLSTM_DH, MLSTM_W

    @pl.when(pl.program_id(1) == 0)
    def _():
        xe_scr[:, 0:8, :] = jnp.zeros((group, 8, 2 * w), F32)
        s_scr[...] = jnp.zeros_like(s_scr)
        m_scr[...] = jnp.zeros_like(m_scr)

    cw = cw_ref[...]
    causal = lax.broadcasted_iota(jnp.int32, (blk, blk), 0) >= lax.broadcasted_iota(jnp.int32, (blk, blk), 1)
    triu = (lax.broadcasted_iota(jnp.int32, (blk, blk), 0)
            <= lax.broadcasted_iota(jnp.int32, (blk, blk), 1)).astype(BF16)
    ones = jnp.ones((blk, dh), BF16)
    s_in = [[s_scr[g, h] for h in range(heads)] for g in range(group)]
    m_in = [m_scr[g, :, 0:1] for g in range(group)]
    s_out = [[None] * heads for _ in range(group)]
    m_out = [None] * group
    per_seq = []
    for g in range(group):
        xe_scr[g, 8:8 + blk, :] = qk_ref[g].astype(F32)
        conv = cw[CONV_WIDTH - 1:CONV_WIDTH, :] * xe_scr[g, 8:8 + blk, :]
        for j in range(CONV_WIDTH - 1):
            off = 8 - (CONV_WIDTH - 1) + j
            conv = conv + cw[j:j + 1, :] * xe_scr[g, off:off + blk, :]
        xe_scr[g, 0:8, :] = xe_scr[g, blk:blk + 8, :]
        qk = conv * _sigmoid(conv)

        gates = gate_refs[g][...]
        i_r = gates + bi_ref[...]
        lf_hi, lf_lo = _split_bf16(_log_sigmoid(pltpu.roll(gates, heads, 0) + bf_ref[...]))
        b_r = _dot(lf_hi, triu) + _dot(lf_lo, triu)
        m_st = m_in[g]
        a_r = i_r - b_r
        inter = b_r + m_st
        m_t = jnp.maximum(inter, b_r + _prefix_max(a_r))
        b_last = b_r[:, blk - 1:blk]
        dec = b_last - b_r + i_r
        m_new = jnp.maximum(b_last + m_st, jnp.max(dec, axis=1, keepdims=True))
        w_c = jnp.exp(b_last + m_st - m_new)
        m_out[g] = m_new
        pack = jnp.concatenate([b_r - m_t, jnp.exp(inter - m_t), jnp.exp(-m_t), jnp.exp(dec - m_new),
                                jnp.zeros((blk - 32, blk), F32)], axis=0)
        per_seq.append((qk, a_r, pack.T, w_c))

    chains = [(g, h) for h in range(heads) for g in range(group)]
    st = {}
    for g, h in chains:
        qk = per_seq[g][0]
        sl = slice(h * dh, (h + 1) * dh)
        q_b = qk[:, sl].astype(BF16)
        k_f = qk[:, w + h * dh:w + (h + 1) * dh] * (dh ** -0.5)
        v_ext = jnp.concatenate([v_ref[g, :, sl], ones], axis=-1)
        st[g, h] = (q_b, k_f, v_ext, _dot_nt(q_b, k_f.astype(BF16)), _dot(q_b, s_in[g][h].astype(BF16)))
    for g, h in chains:
        q_b, k_f, v_ext, qk_t, q_state = st[g, h]
        _, a_r, cols, _ = per_seq[g]
        u_c, w_inter = cols[:, h:h + 1], cols[:, 8 + h:9 + h]
        w_intra = jnp.exp(jnp.where(causal, u_c + a_r[h:h + 1, :], NEG))
        st[g, h] = (k_f, v_ext, _dot((qk_t * w_intra).astype(BF16), v_ext) + w_inter * q_state)
    for g, h in chains:
        k_f, v_ext, tot = st[g, h]
        _, _, cols, w_c = per_seq[g]
        em_c, w_k = cols[:, 16 + h:17 + h], cols[:, 24 + h:25 + h]
        sl = slice(h * dh, (h + 1) * dh)
        num, den = tot[:, :dh], tot[:, dh:]
        hh = num / jnp.maximum(jnp.abs(den), em_c)
        hn = _rms(hh, ng_ref[:, sl])
        y_ref[g, :, sl] = (hn * _sigmoid(og_ref[g, :, sl].astype(F32))).astype(y_ref.dtype)
        s_out[g][h] = w_c[h:h + 1, :] * s_in[g][h] + _dot((k_f * w_k).T.astype(BF16), v_ext)
    for g in range(group):
        m_scr[g] = jnp.broadcast_to(m_out[g], m_scr.shape[1:])
        for h in range(heads):
            s_scr[g, h] = s_out[g][h]


def _mlstm_rows(proj, gates_t, conv_w, bias_i, bias_f, norm_g, *, batch, seq, blk, group):
    t, npj = proj.shape
    w = MLSTM_W
    proj3 = proj.reshape(batch, seq, npj)
    cols = lambda c: (lambda b, i: (b, i, c))
    const2 = lambda b, i: (0, 0)
    nblk = seq // blk
    gate_specs = [pl.BlockSpec((8, blk), functools.partial(lambda b, i, g: (0, (b * group + g) * nblk + i), g=g))
                  for g in range(group)]
    y = pl.pallas_call(
        functools.partial(_mlstm_rows_kernel, blk=blk, group=group),
        grid=(batch // group, seq // blk),
        in_specs=[pl.BlockSpec((group, blk, 2 * w), cols(OFF_MQ // (2 * w))),
                  pl.BlockSpec((group, blk, w), cols(OFF_MV // w)),
                  pl.BlockSpec((group, blk, w), cols(OFF_MO // w)),
                  *gate_specs,
                  pl.BlockSpec((CONV_WIDTH, 2 * w), const2),
                  pl.BlockSpec((8, 1), const2), pl.BlockSpec((8, 1), const2),
                  pl.BlockSpec((1, w), const2)],
        out_specs=pl.BlockSpec((group, blk, w), cols(0)),
        out_shape=jax.ShapeDtypeStruct((batch, seq, w), BF16),
        scratch_shapes=[pltpu.VMEM((group, blk + 8, 2 * w), F32),
                        pltpu.VMEM((group, MLSTM_HEADS, MLSTM_DH, 2 * MLSTM_DH), F32),
                        pltpu.VMEM((group, 8, 128), F32)],
        compiler_params=_cparams("parallel", "arbitrary", vmem=VMEM_LIMIT_SMALL),
        name="mlstm",
    )(proj3, proj3, proj3, *([gates_t] * group), conv_w, bias_i, bias_f, norm_g)
    return y.reshape(t, w)


def _attnproj_kernel(h_ref, w_ref, seg_ref, gq_ref, gk_ref, o_ref, r_scr, *, dil):
    gw, half = ATTN_GW, ATTN_SLAB // 2
    sub_rows = r_scr.shape[2]
    seg, sub_seg = ATTN_TILE // dil, sub_rows // dil

    def head_norm(x, gain):
        ss = _dot((x * x).astype(BF16), seg_ref[...])
        return x * lax.rsqrt(ss * (1.0 / ATTN_DH) + EPS) * gain

    low = lax.broadcasted_iota(jnp.int32, (1, ATTN_SLAB), 1) < half
    for s in range(ATTN_TILE // sub_rows):
        rows = slice(s * sub_rows, (s + 1) * sub_rows)
        res = _dot_nt(h_ref[rows, :], w_ref[0])
        q = head_norm(res[:, :gw], gq_ref[...]) * (ATTN_DH ** -0.5)
        k = head_norm(res[:, gw:2 * gw], gk_ref[...])
        slabs = []
        for pair in range(gw // ATTN_SLAB):
            qp = q[:, pair * ATTN_SLAB:(pair + 1) * ATTN_SLAB]
            slabs += [jnp.where(low, qp, 0.0), jnp.where(low, 0.0, qp)]
        slabs += [k[:, c * 128:(c + 1) * 128] for c in range(gw // 128)]
        slabs += [res[:, 2 * gw + c * 128:2 * gw + (c + 1) * 128] for c in range(gw // 128)]
        for c, slab in enumerate(slabs):
            if dil == 1:
                o_ref[rows, c * 128:(c + 1) * 128] = slab.astype(o_ref.dtype)
            else:
                r_scr[s % 2, c] = slab
        if dil > 1:
            for r in range(dil):
                dst = slice(r * seg + s * sub_seg, r * seg + (s + 1) * sub_seg)
                for c in range(r_scr.shape[1]):
                    o_ref[dst, c * 128:(c + 1) * 128] = (
                        r_scr[s % 2, c, pl.ds(r, sub_seg, stride=dil), :].astype(o_ref.dtype))


def _attnproj(h, w, seg_ones, gq, gk, *, layer, group, dilation):
    t, d = h.shape
    wcols = 3 * ATTN_GW
    const2 = lambda i: (0, 0)
    return pl.pallas_call(
        functools.partial(_attnproj_kernel, dil=dilation),
        grid=(t // ATTN_TILE,),
        in_specs=[pl.BlockSpec((ATTN_TILE, d), lambda i: (i, 0)),
                  pl.BlockSpec((1, wcols, d), lambda i: (layer, group, 0)),
                  pl.BlockSpec((ATTN_GW, ATTN_GW), const2),
                  pl.BlockSpec((1, ATTN_GW), const2), pl.BlockSpec((1, ATTN_GW), const2)],
        out_specs=pl.BlockSpec((ATTN_TILE, ATTN_COLS), lambda i: (i, 0)),
        out_shape=jax.ShapeDtypeStruct((t, ATTN_COLS), BF16),
        scratch_shapes=[pltpu.VMEM((2, ATTN_COLS // 128, ATTNPROJ_SUB, 128), F32)],
        compiler_params=_cparams("parallel"),
        name=f"attnproj{group}",
    )(h, w, seg_ones, gq, gk)


def _dattn_kernel(q_ref, kc_ref, kp_ref, vc_ref, vp_ref, bias_ref, o_ref, lse_ref,
                  kx_scr, vx_scr, o_scr, l_scr, *, dil):
    blk = ATTN_BLOCK
    per = ATTN_SUB // dil
    first_tile = pl.program_id(1) == 0
    for r in range(dil):
        base = r * (per + 1) * blk
        last = slice((r * per + per - 1) * blk, (r * per + per) * blk)
        mine = slice(r * per * blk, (r + 1) * per * blk)
        kx_scr[base:base + blk, :] = kp_ref[last, :]
        vx_scr[base:base + blk, :] = vp_ref[last, :]
        kx_scr[base + blk:base + (per + 1) * blk, :] = kc_ref[mine, :]
        vx_scr[base + blk:base + (per + 1) * blk, :] = vc_ref[mine, :]

    low = lax.broadcasted_iota(jnp.int32, (1, ATTN_SLAB), 1) < ATTN_SLAB // 2
    no_prev = lax.broadcasted_iota(jnp.int32, (1, 2 * blk), 1) < blk
    for r in range(dil):
        for sub in range(per):
            u = r * per + sub
            win = slice((r * (per + 1) + sub) * blk, (r * (per + 1) + sub + 2) * blk)
            o_slabs, l_slabs = [], []
            for pair in range(ATTN_GW // ATTN_SLAB):
                cols = slice(pair * ATTN_SLAB, (pair + 1) * ATTN_SLAB)
                kx, vx = kx_scr[win, cols], vx_scr[win, cols]
                o_pair, l_pair = [], []
                for h in (2 * pair, 2 * pair + 1):
                    logits = _dot_nt(q_ref[u * blk:(u + 1) * blk, h * ATTN_SLAB:(h + 1) * ATTN_SLAB], kx)
                    logits = logits + bias_ref[h]
                    if sub == 0:
                        logits = jnp.where(first_tile & no_prev, NEG, logits)
                    m = jnp.max(logits, axis=-1, keepdims=True)
                    p = jnp.exp(logits - m)
                    l = jnp.sum(p, axis=-1, keepdims=True)
                    o_pair.append(_dot(p.astype(BF16), vx) / l)
                    l_pair.append(m + jnp.log(l))
                o_slabs.append(jnp.where(low, o_pair[0], o_pair[1]))
                l_slabs.append(jnp.where(low, l_pair[0], l_pair[1]))
            dst = pl.ds(sub * blk * dil + r, blk, stride=dil) if dil > 1 else slice(u * blk, (u + 1) * blk)
            for c in range(ATTN_GW // ATTN_SLAB):
                o_scr[c, dst, :] = o_slabs[c]
                l_scr[c, dst, :] = l_slabs[c]
    for c in range(ATTN_GW // ATTN_SLAB):
        o_ref[:, c * ATTN_SLAB:(c + 1) * ATTN_SLAB] = o_scr[c].astype(o_ref.dtype)
        lse_ref[:, c * ATTN_SLAB:(c + 1) * ATTN_SLAB] = l_scr[c]


def _dattn(aproj, bias, *, seq, group, dilation):
    t = aproj.shape[0]
    tiles = seq // ATTN_TILE
    qw = HEADS_PER_GROUP * ATTN_SLAB
    cq, ck, cv = 0, qw // ATTN_GW, qw // ATTN_GW + 1
    blk = (ATTN_TILE, ATTN_GW)
    cur = lambda c: (lambda b, j: (b * tiles + j, c))
    prev = lambda c: (lambda b, j: (b * tiles + jnp.maximum(j - 1, 0), c))
    xrows = ATTN_TILE + dilation * ATTN_BLOCK
    return pl.pallas_call(
        functools.partial(_dattn_kernel, dil=dilation),
        grid=(t // seq, tiles),
        in_specs=[pl.BlockSpec((ATTN_TILE, qw), cur(cq)),
                  pl.BlockSpec(blk, cur(ck)), pl.BlockSpec(blk, prev(ck)),
                  pl.BlockSpec(blk, cur(cv)), pl.BlockSpec(blk, prev(cv)),
                  pl.BlockSpec((HEADS_PER_GROUP, ATTN_BLOCK, 2 * ATTN_BLOCK), lambda b, j: (0, 0, 0))],
        out_specs=[pl.BlockSpec(blk, cur(0)), pl.BlockSpec(blk, cur(0))],
        out_shape=[jax.ShapeDtypeStruct((t, ATTN_GW), BF16), jax.ShapeDtypeStruct((t, ATTN_GW), F32)],
        scratch_shapes=[pltpu.VMEM((xrows, ATTN_GW), BF16), pltpu.VMEM((xrows, ATTN_GW), BF16),
                        pltpu.VMEM((ATTN_GW // ATTN_SLAB, ATTN_TILE, ATTN_SLAB), F32),
                        pltpu.VMEM((ATTN_GW // ATTN_SLAB, ATTN_TILE, ATTN_SLAB), F32)],
        compiler_params=_cparams("parallel", "arbitrary"),
        name=f"dattn{group}",
    )(aproj, aproj, aproj, aproj, aproj, bias)


def _rel_bucket(n):
    max_exact = REL_BUCKETS // 2
    nf = jnp.maximum(n, 1).astype(F32)
    log_b = max_exact + (jnp.log(nf / max_exact) / math.log(REL_MAX_DIST / max_exact)
                         * (REL_BUCKETS - max_exact)).astype(jnp.int32)
    return jnp.where(n < max_exact, n, jnp.minimum(log_b, REL_BUCKETS - 1))


def _attn_bias(rel_bias, group):
    window, dilation = ATTN_PATTERNS[group]
    steps = window // dilation
    hp = lax.Precision.HIGHEST
    hs = slice(group * HEADS_PER_GROUP, (group + 1) * HEADS_PER_GROUP)
    bucket = _rel_bucket(jnp.arange(steps + 1) * dilation)
    bias_steps = jnp.dot(jax.nn.one_hot(bucket, REL_BUCKETS, dtype=F32), rel_bias[:, hs].astype(F32),
                         precision=hp)
    qi = jnp.arange(ATTN_BLOCK)[:, None]
    ki = jnp.arange(2 * ATTN_BLOCK)[None, :]
    dist = ATTN_BLOCK + qi - ki
    ok = (dist >= 0) & (dist <= steps)
    sel = jax.nn.one_hot(jnp.clip(dist, 0, steps).reshape(-1), steps + 1, dtype=F32)
    bias = jnp.dot(sel, bias_steps, precision=hp).T.reshape(HEADS_PER_GROUP, ATTN_BLOCK, 2 * ATTN_BLOCK)
    return jnp.where(ok[None], bias, NEG)


def _merge_kernel(ya_ref, yb0_ref, yb1_ref, yb2_ref, l0_ref, l1_ref, l2_ref, gu_ref, gv_ref, gate_ref,
                  x_ref, wa_ref, wb_ref, wc_ref, wo_ref, ws_ref, bs_ref, gg_ref, o_ref, yc_scr, *, tm):
    d = x_ref.shape[1]
    l0, l1, l2 = l0_ref[...], l1_ref[...], l2_ref[...]
    mx = jnp.maximum(jnp.maximum(l0, l1), l2)
    e0, e1, e2 = jnp.exp(l0 - mx), jnp.exp(l1 - mx), jnp.exp(l2 - mx)
    inv = 1.0 / (e0 + e1 + e2)
    yb = jnp.concatenate([(yb0_ref[...].astype(F32) * (e0 * inv)).astype(BF16),
                          (yb1_ref[...].astype(F32) * (e1 * inv)).astype(BF16),
                          (yb2_ref[...].astype(F32) * (e2 * inv)).astype(BF16)], axis=-1)

    for j in range(tm // GMLP_CHUNK):
        rows = slice(j * GMLP_CHUNK, (j + 1) * GMLP_CHUNK)
        for g in range(GMLP_GROUPS):
            cols = slice(g * GMLP_GC, (g + 1) * GMLP_GC)
            u = jax.nn.gelu(gu_ref[rows, cols].astype(F32))
            v = _rms(jax.nn.gelu(gv_ref[rows, cols].astype(F32)), gg_ref[:, cols])
            mixed = _dot(ws_ref[g], v.astype(BF16)) + bs_ref[g]
            yc_scr[rows, cols] = (u * mixed).astype(BF16)

    def gate2(k):
        return jnp.tanh(0.5 * gate_ref[:, k * d:(k + 1) * d].astype(F32)) + 1.0

    merged2 = gate2(0) * _dot(ya_ref[...], wa_ref[...])
    merged2 = merged2 + gate2(1) * _dot(yb, wb_ref[...])
    merged2 = merged2 + gate2(2) * _dot(yc_scr[...], wc_ref[...])
    o_ref[...] = x_ref[...] + 0.5 * _dot(merged2.astype(BF16), wo_ref[...])


def _merge(ya, ybs, lses, proj, x2d, wa, wb, wc, wo, ws, bsb, gg, *, tm):
    t, d = x2d.shape
    row = lambda c: (lambda i: (i, c))
    full2 = lambda i: (0, 0)
    full3 = lambda i: (0, 0, 0)
    gspec = pl.BlockSpec((tm, ATTN_GW), row(0))
    return pl.pallas_call(
        functools.partial(_merge_kernel, tm=tm),
        grid=(t // tm,),
        in_specs=[pl.BlockSpec((tm, MLSTM_W), row(0)),
                  gspec, gspec, gspec, gspec, gspec, gspec,
                  pl.BlockSpec((tm, GMLP_W), row(OFF_GU // GMLP_W)),
                  pl.BlockSpec((tm, GMLP_W), row(OFF_GV // GMLP_W)),
                  pl.BlockSpec((tm, N_BRANCH * d), row(OFF_GATE // (N_BRANCH * d))),
                  pl.BlockSpec((tm, d), row(0)),
                  pl.BlockSpec(wa.shape, full2), pl.BlockSpec(wb.shape, full2),
                  pl.BlockSpec(wc.shape, full2), pl.BlockSpec(wo.shape, full2),
                  pl.BlockSpec(ws.shape, full3), pl.BlockSpec(bsb.shape, full3),
                  pl.BlockSpec(gg.shape, full2)],
        out_specs=pl.BlockSpec((tm, d), row(0)),
        out_shape=jax.ShapeDtypeStruct((t, d), F32),
        scratch_shapes=[pltpu.VMEM((tm, GMLP_W), BF16)],
        compiler_params=_cparams("parallel"),
        name="merge",
    )(ya, *ybs, *lses, proj, proj, proj, x2d, wa, wb, wc, wo, ws, bsb, gg)


def _memkv_kernel(mem_ref, g_ref, w_ref, gk_ref, k_ref, v_ref):
    dh, w = XATTN_DH, XATTN_W
    kv = _dot(_rms(mem_ref[0], g_ref[...]).astype(BF16), w_ref[...])
    for h in range(XATTN_HEADS):
        sl = slice(h * dh, (h + 1) * dh)
        k_ref[0, :, sl] = _rms(kv[:, sl], gk_ref[...]).astype(k_ref.dtype)
    v_ref[0] = kv[:, w:].astype(v_ref.dtype)


def _memkv(mem, gain, w_kv, gk):
    b, m, d = mem.shape
    full2 = lambda i: (0, 0)
    return pl.pallas_call(
        _memkv_kernel,
        grid=(b,),
        in_specs=[pl.BlockSpec((1, m, d), lambda i: (i, 0, 0)),
                  pl.BlockSpec((1, d), full2),
                  pl.BlockSpec(w_kv.shape, full2),
                  pl.BlockSpec((1, XATTN_DH), full2)],
        out_specs=[pl.BlockSpec((1, m, XATTN_W), lambda i: (i, 0, 0)),
                   pl.BlockSpec((1, m, XATTN_W), lambda i: (i, 0, 0))],
        out_shape=[jax.ShapeDtypeStruct((b, m, XATTN_W), BF16),
                   jax.ShapeDtypeStruct((b, m, XATTN_W), BF16)],
        compiler_params=_cparams("parallel", vmem=VMEM_LIMIT_SMALL),
        name="memkv",
    )(mem, gain, w_kv, gk)


def _route(logits):
    tm = logits.shape[1]
    e = jnp.exp(logits - jnp.max(logits, axis=0, keepdims=True))
    probs = e / jnp.sum(e, axis=0, keepdims=True)
    rowi = lax.broadcasted_iota(jnp.int32, (8, tm), 0)
    real = rowi < EXPERTS_PER_GROUP
    tops = []
    for g in range(N_EXPERT_GROUPS):
        pg = jnp.where(real, probs[8 * g:8 * g + 8, :], -0.5)
        m1 = jnp.max(pg, axis=0, keepdims=True)
        i1 = jnp.min(jnp.where(pg == m1, rowi, 8), axis=0, keepdims=True)
        pg2 = jnp.where(rowi == i1, -1.0, pg)
        m2 = jnp.max(pg2, axis=0, keepdims=True)
        i2 = jnp.min(jnp.where(pg2 == m2, rowi, 8), axis=0, keepdims=True)
        tops.append((m1, i1, m2, i2))
    best = jnp.zeros((1, tm), jnp.int32)
    best_score = tops[0][0] + tops[0][2]
    for g in range(1, N_EXPERT_GROUPS):
        score = tops[g][0] + tops[g][2]
        better = score > best_score
        best = jnp.where(better, g, best)
        best_score = jnp.where(better, score, best_score)
    m1, i1, m2, i2 = tops[0]
    for g in range(1, N_EXPERT_GROUPS):
        m1, i1, m2, i2 = (jnp.where(best == g, new, old) for new, old in zip(tops[g], (m1, i1, m2, i2)))
    tot = m1 + m2
    base = best * EXPERTS_PER_GROUP
    return base + i1, base + i2, m1 / tot, m2 / tot


def _pack_bf16_pairs(x):
    n = x.shape[1] // 2
    hi = lax.bitcast_convert_type(x[:, :n].astype(BF16).astype(F32), jnp.uint32)
    lo = lax.bitcast_convert_type(x[:, n:].astype(BF16).astype(F32), jnp.uint32)
    return hi | (lo >> 16)


def _unpack_bf16_pairs(p):
    hi = lax.bitcast_convert_type(p & jnp.uint32(0xFFFF0000), F32)
    lo = lax.bitcast_convert_type(p << 16, F32)
    return hi, lo


def _store_row_chunks(ref, packed):
    for j in range(ROW_CHUNKS):
        ref[j] = packed[:, j * 128:(j + 1) * 128]


def _load_row_chunks(ref):
    return jnp.concatenate([ref[j] for j in range(ROW_CHUNKS)], axis=-1)


def _xattn_kernel(x_ref, k_ref, v_ref, gx_ref, wq_ref, gq_ref, wo_ref, gf_ref, rw_ref, rb_ref,
                  xo_ref, hf_ref, eidx_ref, wts_ref, *, sub):
    dh = XATTN_DH
    rw = rw_ref[...]
    rw_hi, rw_lo = _split_bf16(rw)
    for s in range(x_ref.shape[0] // sub):
        rows = slice(s * sub, (s + 1) * sub)
        x = x_ref[rows, :]
        q = _dot(_rms(x, gx_ref[...]).astype(BF16), wq_ref[...])
        outs = []
        for h in range(XATTN_HEADS):
            sl = slice(h * dh, (h + 1) * dh)
            q_h = (_rms(q[:, sl], gq_ref[...]) * (dh ** -0.5)).astype(BF16)
            logits = _dot_nt(q_h, k_ref[0, :, sl])
            p = jnp.exp(logits - jnp.max(logits, axis=-1, keepdims=True))
            o = _dot(p.astype(BF16), v_ref[0, :, sl]) / jnp.sum(p, axis=-1, keepdims=True)
            outs.append(o.astype(BF16))
        xn = x + _dot(jnp.concatenate(outs, axis=-1), wo_ref[...])
        xo_ref[rows, :] = xn
        hf = _rms(xn, gf_ref[...])
        packed = _pack_bf16_pairs(hf)
        for j in range(ROW_CHUNKS):
            hf_ref[j, rows, :] = packed[:, j * 128:(j + 1) * 128]
        hf_hi, hf_lo = _split_bf16(hf)
        logits_t = _dot_nt(rw_hi, hf_hi) + _dot_nt(rw_hi, hf_lo) + _dot_nt(rw_lo, hf_hi) + rb_ref[...]
        e1, e2, w1, w2 = _route(logits_t)
        eidx_ref[:, rows] = jnp.concatenate([e1, e2, jnp.zeros((6, sub), jnp.int32)], axis=0)
        wts_ref[:, rows] = jnp.concatenate([w1, w2, jnp.zeros((6, sub), F32)], axis=0)


def _xattn(x2d, k, v, gx, wq, gq, wo, gf, rw_t, rb, *, seq, tm):
    t, d = x2d.shape
    per_b = seq // tm
    full2 = lambda i: (0, 0)
    kv_spec = pl.BlockSpec((1,) + k.shape[1:], lambda i: (i // per_b, 0, 0))
    return pl.pallas_call(
        functools.partial(_xattn_kernel, sub=min(tm, XATTN_SUB)),
        grid=(t // tm,),
        in_specs=[pl.BlockSpec((tm, d), lambda i: (i, 0)), kv_spec, kv_spec,
                  pl.BlockSpec((1, d), full2), pl.BlockSpec(wq.shape, full2),
                  pl.BlockSpec((1, XATTN_DH), full2), pl.BlockSpec(wo.shape, full2),
                  pl.BlockSpec((1, d), full2), pl.BlockSpec(rw_t.shape, full2),
                  pl.BlockSpec(rb.shape, full2)],
        out_specs=[pl.BlockSpec((tm, d), lambda i: (i, 0)),
                   pl.BlockSpec((ROW_CHUNKS, tm, 128), lambda i: (0, i, 0)),
                   pl.BlockSpec((8, tm), lambda i: (0, i)),
                   pl.BlockSpec((8, tm), lambda i: (0, i))],
        out_shape=[jax.ShapeDtypeStruct((t, d), F32),
                   jax.ShapeDtypeStruct((ROW_CHUNKS, t, 128), jnp.uint32),
                   jax.ShapeDtypeStruct((8, t), jnp.int32),
                   jax.ShapeDtypeStruct((8, t), F32)],
        compiler_params=_cparams("parallel"),
        name="xattn_router",
    )(x2d, k, v, gx, wq, gq, wo, gf, rw_t, rb)


def _moe_plan_kernel(eidx_ref, i1_ref, i2_ref, te_ref, na_ref, cnt_scr, carry_scr, *, tb, tm, plane_rows):
    ne = N_EXPERTS
    hp = lax.Precision.HIGHEST
    phase, j = pl.program_id(0), pl.program_id(1)
    rows = lax.broadcasted_iota(jnp.int32, (ne, tb), 0)
    oh1 = rows == eidx_ref[0:1, :]
    oh2 = rows == eidx_ref[1:2, :]
    a = oh1.astype(F32) + oh2.astype(F32)
    blk_cnt = jnp.broadcast_to(jnp.sum(a, axis=1, keepdims=True), cnt_scr.shape)

    @pl.when((phase == 0) & (j == 0))
    def _():
        cnt_scr[...] = jnp.zeros_like(cnt_scr)

    @pl.when(phase == 0)
    def _():
        cnt_scr[...] += blk_cnt

    @pl.when((phase == 1) & (j == 0))
    def _():
        padded = jnp.ceil(cnt_scr[...] * (1.0 / tm)) * tm
        er = lax.broadcasted_iota(jnp.int32, (ne, ne), 0)
        ec = lax.broadcasted_iota(jnp.int32, (ne, ne), 1)
        off = jnp.dot((ec < er).astype(F32), padded, precision=hp, preferred_element_type=F32)
        carry_scr[...] = off
        seg_end = (off + padded)[:, 0:1]
        tile_start = lax.broadcasted_iota(jnp.int32, (ne, te_ref.shape[1]), 1).astype(F32) * tm
        te = jnp.sum((seg_end <= tile_start).astype(F32), axis=0, keepdims=True)
        te_ref[...] = jnp.broadcast_to(jnp.minimum(te, ne - 1.0), te_ref.shape).astype(jnp.int32)
        total = jnp.sum(padded[:, 0:1], axis=0, keepdims=True)
        na_ref[...] = jnp.broadcast_to(total * (1.0 / tm), na_ref.shape).astype(jnp.int32)

    @pl.when(phase == 1)
    def _():
        before = (lax.broadcasted_iota(jnp.int32, (tb, tb), 0)
                  < lax.broadcasted_iota(jnp.int32, (tb, tb), 1)).astype(BF16)
        rank = carry_scr[:, 0:1] + _dot(a.astype(BF16), before)
        d1 = jnp.sum(jnp.where(oh1, rank, 0.0), axis=0, keepdims=True).astype(jnp.int32)
        d2 = jnp.sum(jnp.where(oh2, rank, 0.0), axis=0, keepdims=True).astype(jnp.int32)
        plane = lax.broadcasted_iota(jnp.int32, (8, tb), 0) * plane_rows
        i1_ref[...] = jnp.where(plane < ROW_CHUNKS * plane_rows, plane + d1, 0)
        i2_ref[...] = jnp.where(plane < ROW_CHUNKS * plane_rows, plane + d2, 0)
        carry_scr[...] += blk_cnt


def _moe_plan(eidx, *, tm, n_tiles, tb=512):
    t = eidx.shape[1]
    ntp = -(-n_tiles // 128) * 128
    return pl.pallas_call(
        functools.partial(_moe_plan_kernel, tb=tb, tm=tm, plane_rows=n_tiles * tm),
        grid=(2, t // tb),
        in_specs=[pl.BlockSpec((8, tb), lambda p, j: (0, j))],
        out_specs=[pl.BlockSpec((8, tb), lambda p, j: (0, j * p)),
                   pl.BlockSpec((8, tb), lambda p, j: (0, j * p)),
                   pl.BlockSpec((8, ntp), lambda p, j: (0, 0)),
                   pl.BlockSpec((8, 128), lambda p, j: (0, 0))],
        out_shape=[jax.ShapeDtypeStruct((8, t), jnp.int32),
                   jax.ShapeDtypeStruct((8, t), jnp.int32),
                   jax.ShapeDtypeStruct((8, ntp), jnp.int32),
                   jax.ShapeDtypeStruct((8, 128), jnp.int32)],
        scratch_shapes=[pltpu.VMEM((N_EXPERTS, 128), F32), pltpu.VMEM((N_EXPERTS, 128), F32)],
        compiler_params=_cparams("arbitrary", "arbitrary", vmem=VMEM_LIMIT_SMALL),
        name="moe_plan",
    )(eidx)


def _sc_mesh():
    return plsc.VectorSubcoreMesh(core_axis_name="c", subcore_axis_name="s",
                                  num_cores=SC_CORES, num_subcores=SC_SUBCORES)


def _sc_index_spec(tokens):
    nb = tokens // SC_WINDOW
    return pl.BlockSpec((1, SC_WINDOW), lambda i: (i // nb, i % nb))


def _sc_dispatch(rows, i1, i2, n_out):
    n = rows.shape[0]
    tokens = i1.shape[1]

    @functools.partial(pl.kernel, out_type=jax.ShapeDtypeStruct((n_out, 128), rows.dtype), mesh=_sc_mesh(),
                       name="moe_dispatch")
    def k(x_hbm, i1_hbm, i2_hbm, o_hbm):
        def body(x_vmem, i1_vmem, i2_vmem):
            pltpu.sync_copy(x_vmem, o_hbm.at[i1_vmem.at[0]])
            pltpu.sync_copy(x_vmem, o_hbm.at[i2_vmem.at[0]])

        pltpu.emit_pipeline(
            body, grid=(n // SC_WINDOW,),
            in_specs=[pl.BlockSpec((SC_WINDOW, 128), lambda i: (i, 0)),
                      _sc_index_spec(tokens), _sc_index_spec(tokens)],
            out_specs=[],
            core_axis_name=("c", "s"), dimension_semantics=(pltpu.PARALLEL,),
        )(x_hbm, i1_hbm, i2_hbm)

    return k(rows, i1, i2)


def _sc_collect(table, i1, i2):
    tokens = i1.shape[1]
    n = ROW_CHUNKS * tokens
    out = jax.ShapeDtypeStruct((n, 128), table.dtype)

    @functools.partial(pl.kernel, out_type=(out, out), mesh=_sc_mesh(), name="moe_collect")
    def k(t_hbm, i1_hbm, i2_hbm, o1_hbm, o2_hbm):
        def body(i1_vmem, i2_vmem, o1_vmem, o2_vmem):
            pltpu.sync_copy(t_hbm.at[i1_vmem.at[0]], o1_vmem)
            pltpu.sync_copy(t_hbm.at[i2_vmem.at[0]], o2_vmem)

        pltpu.emit_pipeline(
            body, grid=(n // SC_WINDOW,),
            in_specs=[_sc_index_spec(tokens), _sc_index_spec(tokens)],
            out_specs=[pl.BlockSpec((SC_WINDOW, 128), lambda i: (i, 0)),
                       pl.BlockSpec((SC_WINDOW, 128), lambda i: (i, 0))],
            core_axis_name=("c", "s"), dimension_semantics=(pltpu.PARALLEL,),
        )(i1_hbm, i2_hbm, o1_hbm, o2_hbm)

    return k(table, i1, i2)


def _experts_kernel(te_ref, na_ref, xs_ref, wg_ref, wu_ref, wd_ref, y_ref, wg_scr, wu_scr, wd_scr):
    i = pl.program_id(0)
    active = i < na_ref[0]

    @pl.when(active & ((i == 0) | (te_ref[i] != te_ref[jnp.maximum(i - 1, 0)])))
    def _():
        wg_scr[...] = wg_ref[0, 0].astype(BF16)
        wu_scr[...] = wu_ref[0, 0].astype(BF16)
        wd_scr[...] = wd_ref[0, 0].astype(BF16)

    @pl.when(active)
    def _():
        hi, lo = _unpack_bf16_pairs(_load_row_chunks(xs_ref))
        h = jnp.concatenate([hi, lo], axis=-1).astype(BF16)
        up = _dot(h, wg_scr[...])
        act = up * _sigmoid(up) * _dot(h, wu_scr[...])
        _store_row_chunks(y_ref, _pack_bf16_pairs(_dot(act.astype(BF16), wd_scr[...])))


def _experts(tile_expert, n_active, xs, wg, wu, wd, *, layer, tm):
    n_tiles = tile_expert.shape[0]
    _, _, d, dff = wg.shape
    rows = lambda i, te, na: (0, jnp.minimum(i, na[0] - 1), 0)
    expert = lambda i, te, na: (layer, te[i], 0, 0)
    return pl.pallas_call(
        _experts_kernel,
        grid_spec=pltpu.PrefetchScalarGridSpec(
            num_scalar_prefetch=2,
            grid=(n_tiles,),
            in_specs=[pl.BlockSpec((ROW_CHUNKS, tm, 128), rows),
                      pl.BlockSpec((1, 1, d, dff), expert),
                      pl.BlockSpec((1, 1, d, dff), expert),
                      pl.BlockSpec((1, 1, dff, d), expert)],
            out_specs=pl.BlockSpec((ROW_CHUNKS, tm, 128), rows),
            scratch_shapes=[pltpu.VMEM((d, dff), BF16), pltpu.VMEM((d, dff), BF16), pltpu.VMEM((dff, d), BF16)]),
        out_shape=jax.ShapeDtypeStruct(xs.shape, xs.dtype),
        compiler_params=_cparams("arbitrary"),
        name="moe_experts",
    )(tile_expert, n_active, xs, wg, wu, wd)


def _moe_combine_kernel(x_ref, y1_ref, y2_ref, w_ref, o_ref):
    half = x_ref.shape[1] // 2
    hi1, lo1 = _unpack_bf16_pairs(_load_row_chunks(y1_ref))
    hi2, lo2 = _unpack_bf16_pairs(_load_row_chunks(y2_ref))
    tm = x_ref.shape[0]
    w_cols = jnp.concatenate([w_ref[...], jnp.zeros((128 - w_ref.shape[0], tm), F32)], axis=0).T
    w1, w2 = w_cols[:, 0:1], w_cols[:, 1:2]
    o_ref[:, :half] = x_ref[:, :half] + w1 * hi1 + w2 * hi2
    o_ref[:, half:] = x_ref[:, half:] + w1 * lo1 + w2 * lo2


def _moe_combine(x2d, y1, y2, wts, *, tm):
    t, d = x2d.shape
    chunk_spec = pl.BlockSpec((ROW_CHUNKS, tm, 128), lambda i: (0, i, 0))
    return pl.pallas_call(
        _moe_combine_kernel,
        grid=(t // tm,),
        in_specs=[pl.BlockSpec((tm, d), lambda i: (i, 0)), chunk_spec, chunk_spec,
                  pl.BlockSpec((wts.shape[0], tm), lambda i: (0, i))],
        out_specs=pl.BlockSpec((tm, d), lambda i: (i, 0)),
        out_shape=jax.ShapeDtypeStruct((t, d), F32),
        compiler_params=_cparams("parallel", vmem=VMEM_LIMIT_SMALL),
        name="moe_combine",
    )(x2d, y1, y2, wts)


def _moe(x2d, hf_rows, eidx, wts, wg, wu, wd, *, layer):
    t = x2d.shape[0]
    tm = MOE_TM
    n_tiles = 2 * t // tm + N_EXPERTS
    plane = n_tiles * tm
    i1, i2, te, na = _moe_plan(eidx, tm=tm, n_tiles=n_tiles)
    xs = _sc_dispatch(hf_rows.reshape(ROW_CHUNKS * t, 128), i1, i2, ROW_CHUNKS * plane)
    ys = _experts(te[0, :n_tiles], na[0, :1], xs.reshape(ROW_CHUNKS, plane, 128), wg, wu, wd,
                  layer=layer, tm=tm)
    y1, y2 = _sc_collect(ys.reshape(ROW_CHUNKS * plane, 128), i1, i2)
    return _moe_combine(x2d, y1.reshape(ROW_CHUNKS, t, 128), y2.reshape(ROW_CHUNKS, t, 128), wts,
                        tm=COMBINE_TM)


W_ROWS = 256


def _w_rows_kernel(start_ref, valid_ref, w_ref, o_ref):
    del start_ref
    row = lax.broadcasted_iota(jnp.int32, w_ref.shape[1:], 0)
    o_ref[0] = jnp.where(row < valid_ref[pl.program_id(1)], w_ref[0], 0.0).astype(o_ref.dtype)


def _w_rows(w_t, starts, valid):
    depth, _, d = w_t.shape
    nblk = len(starts)
    return pl.pallas_call(
        _w_rows_kernel,
        grid_spec=pltpu.PrefetchScalarGridSpec(
            num_scalar_prefetch=2,
            grid=(depth, nblk),
            in_specs=[pl.BlockSpec((pl.Element(1), pl.Element(W_ROWS), pl.Element(d)),
                                   lambda l, c, st, va: (l, pl.multiple_of(st[c], 8), 0))],
            out_specs=pl.BlockSpec((1, W_ROWS, d), lambda l, c, st, va: (l, c, 0))),
        out_shape=jax.ShapeDtypeStruct((depth, nblk * W_ROWS, d), BF16),
        compiler_params=_cparams("parallel", "arbitrary", vmem=VMEM_LIMIT_SMALL),
        name="w_in_rows",
    )(jnp.asarray(starts, jnp.int32), jnp.asarray(valid, jnp.int32), w_t)


def _w_in_layout(w_in):
    w_t = jnp.swapaxes(w_in, 1, 2)
    src_if = 4 * MLSTM_W
    src_a = src_if + 2 * MLSTM_HEADS
    src_g = src_a + 3 * ATTN_W
    starts = list(range(0, src_if, W_ROWS)) + [src_g + k * W_ROWS for k in range((OFF_IF - OFF_GU) // W_ROWS)]
    valid = [W_ROWS] * len(starts)
    starts.append(src_if)
    valid.append(2 * MLSTM_HEADS)
    assert len(starts) * W_ROWS == N_PROJ and ATTN_GW == W_ROWS
    a_starts = [src_a + j * ATTN_W + g * ATTN_GW for g in range(len(ATTN_PATTERNS)) for j in range(3)]
    return _w_rows(w_t, starts, valid), _w_rows(w_t, a_starts, [W_ROWS] * len(a_starts))


def kernel(x, mem, norm_mix, w_in, mlstm_conv, mlstm_gate_b, mlstm_norm, attn_qk_norm, gmlp_norm, gmlp_ws,
           gmlp_bs, w_branch_a, w_branch_b, w_branch_c, w_out, rel_bias, norm_xattn, norm_mem, w_xq, w_xkv,
           xattn_qk_norm, w_xo, norm_ffn, router_w, router_b, w_expert_gate, w_expert_up, w_expert_down):
    b, s, d = x.shape
    t = b * s
    depth = w_in.shape[0]
    x2d = x.reshape(t, d)

    biases = [_attn_bias(rel_bias, g) for g in range(len(ATTN_PATTERNS))]
    rw_t = jnp.zeros((N_EXPERT_GROUPS, 8, d), F32).at[:, :EXPERTS_PER_GROUP].set(
        router_w.T.reshape(N_EXPERT_GROUPS, EXPERTS_PER_GROUP, d)).reshape(ROUTER_ROWS, d)
    rb = jnp.full((N_EXPERT_GROUPS, 8), NEG, F32).at[:, :EXPERTS_PER_GROUP].set(
        router_b.astype(F32).reshape(N_EXPERT_GROUPS, EXPERTS_PER_GROUP)).reshape(ROUTER_ROWS, 1)
    tril = jnp.tril(jnp.ones((GMLP_CHUNK, GMLP_CHUNK), bool))
    head_of = jnp.arange(ATTN_GW) // ATTN_DH
    seg_ones = (head_of[:, None] == head_of[None, :]).astype(BF16)

    w_main, w_attn = _w_in_layout(w_in)

    for l in range(depth):
        proj, h_mix, gates_t = _inproj(x2d, norm_mix[l][None], w_main, layer=l, tm=INPROJ_TM,
                                       tn=INPROJ_TN)
        gq = jnp.tile(attn_qk_norm[l, 0], HEADS_PER_GROUP)[None]
        gk = jnp.tile(attn_qk_norm[l, 1], HEADS_PER_GROUP)[None]

        nh = MLSTM_HEADS
        bias_i = jnp.zeros((8, 1), F32).at[:nh, 0].set(mlstm_gate_b[l, :nh])
        bias_f = jnp.zeros((8, 1), F32).at[:nh, 0].set(mlstm_gate_b[l, nh:])
        ya = _mlstm_rows(proj, gates_t, mlstm_conv[l], bias_i, bias_f, mlstm_norm[l][None],
                         batch=b, seq=s, blk=MLSTM_BLOCK, group=MLSTM_GROUP)

        ybs, lses = [], []
        for g, (_, dilation) in enumerate(ATTN_PATTERNS):
            aproj = _attnproj(h_mix, w_attn, seg_ones, gq, gk, layer=l, group=g, dilation=dilation)
            o, lse = _dattn(aproj, biases[g], seq=s, group=g, dilation=dilation)
            ybs.append(o)
            lses.append(lse)

        ws = jnp.where(tril, gmlp_ws[l], 0.0).astype(BF16)
        bsb = jnp.broadcast_to(gmlp_bs[l][:, :, None], (GMLP_GROUPS, GMLP_CHUNK, GMLP_GC)).astype(F32)
        x2d = _merge(ya, ybs, lses, proj, x2d, w_branch_a[l].astype(BF16), w_branch_b[l].astype(BF16),
                     w_branch_c[l].astype(BF16), w_out[l].astype(BF16), ws, bsb, gmlp_norm[l][None],
                     tm=MERGE_TM)

        k_mem, v_mem = _memkv(mem, norm_mem[l][None], w_xkv[l].astype(BF16), xattn_qk_norm[l, 1][None])
        x2d, hf_rows, eidx, wts = _xattn(x2d, k_mem, v_mem, norm_xattn[l][None], w_xq[l].astype(BF16),
                                         xattn_qk_norm[l, 0][None], w_xo[l].astype(BF16), norm_ffn[l][None],
                                         rw_t, rb, seq=s, tm=XATTN_TM)

        x2d = _moe(x2d, hf_rows, eidx, wts, w_expert_gate, w_expert_up, w_expert_down, layer=l)

    return x2d.reshape(b, s, d)
```

```python
import functools
import math

import jax
import jax.numpy as jnp
import numpy as np
from jax import lax
from jax.experimental import pallas as pl
from jax.experimental.pallas import tpu as pltpu
from jax.experimental.pallas import tpu_sc as plsc

F32 = jnp.float32
BF16 = jnp.bfloat16

EPS = 1e-6
NEG = -1e30

MLSTM_HEADS = 4
MLSTM_DH = 128
MLSTM_W = MLSTM_HEADS * MLSTM_DH
CONV_WIDTH = 4
MLSTM_BLOCK = 128
MLSTM_GROUP = 4

ATTN_PATTERNS = ((128, 1), (512, 4), (2048, 16))
HEADS_PER_GROUP = 4
ATTN_DH = 64
ATTN_GW = HEADS_PER_GROUP * ATTN_DH
ATTN_W = len(ATTN_PATTERNS) * ATTN_GW
ATTN_BLOCK = 128
REL_BUCKETS = 32
REL_MAX_DIST = 2048

GMLP_GROUPS = 4
GMLP_GC = 128
GMLP_W = GMLP_GROUPS * GMLP_GC
GMLP_CHUNK = 128

XATTN_HEADS = 4
XATTN_DH = 128
XATTN_W = XATTN_HEADS * XATTN_DH
XATTN_SUB = 1024

N_EXPERTS = 16
N_EXPERT_GROUPS = 4
EXPERTS_PER_GROUP = 4
ROUTER_ROWS = 8 * N_EXPERT_GROUPS

N_BRANCH = 3

MOE_TM = 1024
EXPERT_SUB = 256
ROW_CHUNKS = 4
SC_CORES, SC_SUBCORES = 2, 16
SC_WINDOW = 128

OFF_MQ, OFF_MK, OFF_MV, OFF_MO = 0, 512, 1024, 1536
OFF_GU, OFF_GV = 2048, 2560
OFF_GATE = 3072
OFF_IF = 6144
IF_PAD = 256
N_PROJ = OFF_IF + IF_PAD

ATTN_TILE = 2048
ATTN_SUB = ATTN_TILE // ATTN_BLOCK
ATTN_SLAB = 2 * ATTN_DH
ATTN_COLS = HEADS_PER_GROUP * ATTN_SLAB + 2 * ATTN_GW

VMEM_LIMIT = 48 * 1024 * 1024
VMEM_LIMIT_INPROJ = 56 * 1024 * 1024
VMEM_LIMIT_SMALL = 24 * 1024 * 1024

INPROJ_TM, INPROJ_TN = 1024, 3072
ATTNPROJ_SUB = 512
MERGE_TM = 512
XATTN_TM = 1024
COMBINE_TM = 512


def _cparams(*sem, vmem=VMEM_LIMIT):
    return pltpu.CompilerParams(dimension_semantics=sem, vmem_limit_bytes=vmem)


def _rms(x, gain):
    return x * lax.rsqrt(jnp.mean(x * x, axis=-1, keepdims=True) + EPS) * gain


def _sigmoid(x):
    return 0.5 * jnp.tanh(0.5 * x) + 0.5


def _dot(a, b):
    return jnp.dot(a, b, preferred_element_type=F32)


def _dot_nt(a, b):
    return lax.dot_general(a, b, (((1,), (1,)), ((), ())), preferred_element_type=F32)


def _inproj_kernel(x_ref, g_ref, w_ref, wg_ref, o_ref, h_ref, gt_ref):
    @pl.when(pl.program_id(1) == 0)
    def _():
        h = _rms(x_ref[...], g_ref[...]).astype(BF16)
        h_ref[...] = h
        gt_ref[...] = _dot_nt(wg_ref[0, 0:128, :], h)[:gt_ref.shape[0], :]

    o_ref[...] = _dot_nt(h_ref[...], w_ref[0]).astype(o_ref.dtype)


def _inproj(x2d, gain, w, *, layer, tm, tn):
    t, d = x2d.shape
    n = OFF_IF
    return pl.pallas_call(
        _inproj_kernel,
        grid=(t // tm, n // tn),
        in_specs=[pl.BlockSpec((tm, d), lambda i, j: (i, 0)),
                  pl.BlockSpec((1, d), lambda i, j: (0, 0)),
                  pl.BlockSpec((1, tn, d), lambda i, j: (layer, j, 0)),
                  pl.BlockSpec((1, IF_PAD, d), lambda i, j: (layer, OFF_IF // IF_PAD, 0))],
        out_specs=[pl.BlockSpec((tm, tn), lambda i, j: (i, j)),
                   pl.BlockSpec((tm, d), lambda i, j: (i, 0)),
                   pl.BlockSpec((8, tm), lambda i, j: (0, i))],
        out_shape=[jax.ShapeDtypeStruct((t, n), BF16), jax.ShapeDtypeStruct((t, d), BF16),
                   jax.ShapeDtypeStruct((8, t), F32)],
        compiler_params=_cparams("parallel", "arbitrary", vmem=VMEM_LIMIT_INPROJ),
        name="inproj",
    )(x2d, gain, w, w)


def _log_sigmoid(x):
    return jnp.minimum(x, 0.0) - jnp.log(1.0 + jnp.exp(-jnp.abs(x)))


def _split_bf16(x):
    hi = x.astype(BF16)
    return hi, (x - hi.astype(F32)).astype(BF16)


def _prefix_max(x):
    n = x.shape[1]
    lane = lax.broadcasted_iota(jnp.int32, x.shape, 1)
    shift = 1
    while shift < n:
        x = jnp.maximum(x, jnp.where(lane >= shift, pltpu.roll(x, shift, 1), NEG))
        shift *= 2
    return x


def _mlstm_rows_kernel(qk_ref, v_ref, og_ref, *rest, blk, group):
    gate_refs = rest[:group]
    cw_ref, bi_ref, bf_ref, ng_ref, y_ref, xe_scr, s_scr, m_scr = rest[group:]
    heads, dh, w = MLSTM_HEADS, MLSTM_DH, MLSTM_W

    @pl.when(pl.program_id(1) == 0)
    def _():
        xe_scr[:, 0:8, :] = jnp.zeros((group, 8, 2 * w), F32)
        s_scr[...] = jnp.zeros_like(s_scr)
        m_scr[...] = jnp.zeros_like(m_scr)

    cw = cw_ref[...]
    causal = lax.broadcasted_iota(jnp.int32, (blk, blk), 0) >= lax.broadcasted_iota(jnp.int32, (blk, blk), 1)
    triu = (lax.broadcasted_iota(jnp.int32, (blk, blk), 0)
            <= lax.broadcasted_iota(jnp.int32, (blk, blk), 1)).astype(BF16)
    ones = jnp.ones((blk, dh), BF16)
    s_in = [[s_scr[g, h] for h in range(heads)] for g in range(group)]
    m_in = [m_scr[g, :, 0:1] for g in range(group)]
    s_out = [[None] * heads for _ in range(group)]
    m_out = [None] * group
    per_seq = []
    for g in range(group):
        xe_scr[g, 8:8 + blk, :] = qk_ref[g].astype(F32)
        conv = cw[CONV_WIDTH - 1:CONV_WIDTH, :] * xe_scr[g, 8:8 + blk, :]
        for j in range(CONV_WIDTH - 1):
            off = 8 - (CONV_WIDTH - 1) + j
            conv = conv + cw[j:j + 1, :] * xe_scr[g, off:off + blk, :]
        xe_scr[g, 0:8, :] = xe_scr[g, blk:blk + 8, :]
        qk = conv * _sigmoid(conv)

        gates = gate_refs[g][...]
        i_r = gates + bi_ref[...]
        lf_hi, lf_lo = _split_bf16(_log_sigmoid(pltpu.roll(gates, heads, 0) + bf_ref[...]))
        b_r = _dot(lf_hi, triu) + _dot(lf_lo, triu)
        m_st = m_in[g]
        a_r = i_r - b_r
        inter = b_r + m_st
        m_t = jnp.maximum(inter, b_r + _prefix_max(a_r))
        b_last = b_r[:, blk - 1:blk]
        dec = b_last - b_r + i_r
        m_new = jnp.maximum(b_last + m_st, jnp.max(dec, axis=1, keepdims=True))
        w_c = jnp.exp(b_last + m_st - m_new)
        m_out[g] = m_new
        pack = jnp.concatenate([b_r - m_t, jnp.exp(inter - m_t), jnp.exp(-m_t), jnp.exp(dec - m_new),
                                jnp.zeros((blk - 32, blk), F32)], axis=0)
        per_seq.append((qk, a_r, pack.T, w_c))

    chains = [(g, h) for h in range(heads) for g in range(group)]
    st = {}
    for g, h in chains:
        qk = per_seq[g][0]
        sl = slice(h * dh, (h + 1) * dh)
        q_b = qk[:, sl].astype(BF16)
        k_f = qk[:, w + h * dh:w + (h + 1) * dh] * (dh ** -0.5)
        v_ext = jnp.concatenate([v_ref[g, :, sl], ones], axis=-1)
        st[g, h] = (q_b, k_f, v_ext, _dot_nt(q_b, k_f.astype(BF16)), _dot(q_b, s_in[g][h].astype(BF16)))
    for g, h in chains:
        q_b, k_f, v_ext, qk_t, q_state = st[g, h]
        _, a_r, cols, _ = per_seq[g]
        u_c, w_inter = cols[:, h:h + 1], cols[:, 8 + h:9 + h]
        w_intra = jnp.exp(jnp.where(causal, u_c + a_r[h:h + 1, :], NEG))
        st[g, h] = (k_f, v_ext, _dot((qk_t * w_intra).astype(BF16), v_ext) + w_inter * q_state)
    for g, h in chains:
        k_f, v_ext, tot = st[g, h]
        _, _, cols, w_c = per_seq[g]
        em_c, w_k = cols[:, 16 + h:17 + h], cols[:, 24 + h:25 + h]
        sl = slice(h * dh, (h + 1) * dh)
        num, den = tot[:, :dh], tot[:, dh:]
        hh = num / jnp.maximum(jnp.abs(den), em_c)
        hn = _rms(hh, ng_ref[:, sl])
        y_ref[g, :, sl] = (hn * _sigmoid(og_ref[g, :, sl].astype(F32))).astype(y_ref.dtype)
        s_out[g][h] = w_c[h:h + 1, :] * s_in[g][h] + _dot((k_f * w_k).T.astype(BF16), v_ext)
    for g in range(group):
        m_scr[g] = jnp.broadcast_to(m_out[g], m_scr.shape[1:])
        for h in range(heads):
            s_scr[g, h] = s_out[g][h]


def _mlstm_rows(proj, gates_t, conv_w, bias_i, bias_f, norm_g, *, batch, seq, blk, group):
    t, npj = proj.shape
    w = MLSTM_W
    proj3 = proj.reshape(batch, seq, npj)
    cols = lambda c: (lambda b, i: (b, i, c))
    const2 = lambda b, i: (0, 0)
    nblk = seq // blk
    gate_specs = [pl.BlockSpec((8, blk), functools.partial(lambda b, i, g: (0, (b * group + g) * nblk + i), g=g))
                  for g in range(group)]
    y = pl.pallas_call(
        functools.partial(_mlstm_rows_kernel, blk=blk, group=group),
        grid=(batch // group, seq // blk),
        in_specs=[pl.BlockSpec((group, blk, 2 * w), cols(OFF_MQ // (2 * w))),
                  pl.BlockSpec((group, blk, w), cols(OFF_MV // w)),
                  pl.BlockSpec((group, blk, w), cols(OFF_MO // w)),
                  *gate_specs,
                  pl.BlockSpec((CONV_WIDTH, 2 * w), const2),
                  pl.BlockSpec((8, 1), const2), pl.BlockSpec((8, 1), const2),
                  pl.BlockSpec((1, w), const2)],
        out_specs=pl.BlockSpec((group, blk, w), cols(0)),
        out_shape=jax.ShapeDtypeStruct((batch, seq, w), BF16),
        scratch_shapes=[pltpu.VMEM((group, blk + 8, 2 * w), F32),
                        pltpu.VMEM((group, MLSTM_HEADS, MLSTM_DH, 2 * MLSTM_DH), F32),
                        pltpu.VMEM((group, 8, 128), F32)],
        compiler_params=_cparams("parallel", "arbitrary", vmem=VMEM_LIMIT_SMALL),
        name="mlstm",
    )(proj3, proj3, proj3, *([gates_t] * group), conv_w, bias_i, bias_f, norm_g)
    return y.reshape(t, w)


def _attnproj_kernel(h_ref, w_ref, seg_ref, gq_ref, gk_ref, o_ref, r_scr, *, dil):
    gw, half = ATTN_GW, ATTN_SLAB // 2
    sub_rows = r_scr.shape[2]
    seg, sub_seg = ATTN_TILE // dil, sub_rows // dil

    def head_norm(x, gain):
        ss = _dot((x * x).astype(BF16), seg_ref[...])
        return x * lax.rsqrt(ss * (1.0 / ATTN_DH) + EPS) * gain

    low = lax.broadcasted_iota(jnp.int32, (1, ATTN_SLAB), 1) < half
    for s in range(ATTN_TILE // sub_rows):
        rows = slice(s * sub_rows, (s + 1) * sub_rows)
        res = _dot_nt(h_ref[rows, :], w_ref[0])
        q = head_norm(res[:, :gw], gq_ref[...]) * (ATTN_DH ** -0.5)
        k = head_norm(res[:, gw:2 * gw], gk_ref[...])
        slabs = []
        for pair in range(gw // ATTN_SLAB):
            qp = q[:, pair * ATTN_SLAB:(pair + 1) * ATTN_SLAB]
            slabs += [jnp.where(low, qp, 0.0), jnp.where(low, 0.0, qp)]
        slabs += [k[:, c * 128:(c + 1) * 128] for c in range(gw // 128)]
        slabs += [res[:, 2 * gw + c * 128:2 * gw + (c + 1) * 128] for c in range(gw // 128)]
        for c, slab in enumerate(slabs):
            if dil == 1:
                o_ref[rows, c * 128:(c + 1) * 128] = slab.astype(o_ref.dtype)
            else:
                r_scr[s % 2, c] = slab
        if dil > 1:
            for r in range(dil):
                dst = slice(r * seg + s * sub_seg, r * seg + (s + 1) * sub_seg)
                for c in range(r_scr.shape[1]):
                    o_ref[dst, c * 128:(c + 1) * 128] = (
                        r_scr[s % 2, c, pl.ds(r, sub_seg, stride=dil), :].astype(o_ref.dtype))


def _attnproj(h, w, seg_ones, gq, gk, *, layer, group, dilation):
    t, d = h.shape
    wcols = 3 * ATTN_GW
    const2 = lambda i: (0, 0)
    return pl.pallas_call(
        functools.partial(_attnproj_kernel, dil=dilation),
        grid=(t // ATTN_TILE,),
        in_specs=[pl.BlockSpec((ATTN_TILE, d), lambda i: (i, 0)),
                  pl.BlockSpec((1, wcols, d), lambda i: (layer, group, 0)),
                  pl.BlockSpec((ATTN_GW, ATTN_GW), const2),
                  pl.BlockSpec((1, ATTN_GW), const2), pl.BlockSpec((1, ATTN_GW), const2)],
        out_specs=pl.BlockSpec((ATTN_TILE, ATTN_COLS), lambda i: (i, 0)),
        out_shape=jax.ShapeDtypeStruct((t, ATTN_COLS), BF16),
        scratch_shapes=[pltpu.VMEM((2, ATTN_COLS // 128, ATTNPROJ_SUB, 128), F32)],
        compiler_params=_cparams("parallel"),
        name=f"attnproj{group}",
    )(h, w, seg_ones, gq, gk)


def _dattn_kernel(q_ref, kc_ref, kp_ref, vc_ref, vp_ref, bias_ref, o_ref, lse_ref,
                  kx_scr, vx_scr, o_scr, l_scr, *, dil):
    blk = ATTN_BLOCK
    per = ATTN_SUB // dil
    first_tile = pl.program_id(1) == 0
    for r in range(dil):
        base = r * (per + 1) * blk
        last = slice((r * per + per - 1) * blk, (r * per + per) * blk)
        mine = slice(r * per * blk, (r + 1) * per * blk)
        kx_scr[base:base + blk, :] = kp_ref[last, :]
        vx_scr[base:base + blk, :] = vp_ref[last, :]
        kx_scr[base + blk:base + (per + 1) * blk, :] = kc_ref[mine, :]
        vx_scr[base + blk:base + (per + 1) * blk, :] = vc_ref[mine, :]

    low = lax.broadcasted_iota(jnp.int32, (1, ATTN_SLAB), 1) < ATTN_SLAB // 2
    no_prev = lax.broadcasted_iota(jnp.int32, (1, 2 * blk), 1) < blk
    for r in range(dil):
        for sub in range(per):
            u = r * per + sub
            win = slice((r * (per + 1) + sub) * blk, (r * (per + 1) + sub + 2) * blk)
            o_slabs, l_slabs = [], []
            for pair in range(ATTN_GW // ATTN_SLAB):
                cols = slice(pair * ATTN_SLAB, (pair + 1) * ATTN_SLAB)
                kx, vx = kx_scr[win, cols], vx_scr[win, cols]
                o_pair, l_pair = [], []
                for h in (2 * pair, 2 * pair + 1):
                    logits = _dot_nt(q_ref[u * blk:(u + 1) * blk, h * ATTN_SLAB:(h + 1) * ATTN_SLAB], kx)
                    logits = logits + bias_ref[h]
                    if sub == 0:
                        logits = jnp.where(first_tile & no_prev, NEG, logits)
                    m = jnp.max(logits, axis=-1, keepdims=True)
                    p = jnp.exp(logits - m)
                    l = jnp.sum(p, axis=-1, keepdims=True)
                    o_pair.append(_dot(p.astype(BF16), vx) / l)
                    l_pair.append(m + jnp.log(l))
                o_slabs.append(jnp.where(low, o_pair[0], o_pair[1]))
                l_slabs.append(jnp.where(low, l_pair[0], l_pair[1]))
            dst = pl.ds(sub * blk * dil + r, blk, stride=dil) if dil > 1 else slice(u * blk, (u + 1) * blk)
            for c in range(ATTN_GW // ATTN_SLAB):
                o_scr[c, dst, :] = o_slabs[c]
                l_scr[c, dst, :] = l_slabs[c]
    for c in range(ATTN_GW // ATTN_SLAB):
        o_ref[:, c * ATTN_SLAB:(c + 1) * ATTN_SLAB] = o_scr[c].astype(o_ref.dtype)
        lse_ref[:, c * ATTN_SLAB:(c + 1) * ATTN_SLAB] = l_scr[c]


def _dattn(aproj, bias, *, seq, group, dilation):
    t = aproj.shape[0]
    tiles = seq // ATTN_TILE
    qw = HEADS_PER_GROUP * ATTN_SLAB
    cq, ck, cv = 0, qw // ATTN_GW, qw // ATTN_GW + 1
    blk = (ATTN_TILE, ATTN_GW)
    cur = lambda c: (lambda b, j: (b * tiles + j, c))
    prev = lambda c: (lambda b, j: (b * tiles + jnp.maximum(j - 1, 0), c))
    xrows = ATTN_TILE + dilation * ATTN_BLOCK
    return pl.pallas_call(
        functools.partial(_dattn_kernel, dil=dilation),
        grid=(t // seq, tiles),
        in_specs=[pl.BlockSpec((ATTN_TILE, qw), cur(cq)),
                  pl.BlockSpec(blk, cur(ck)), pl.BlockSpec(blk, prev(ck)),
                  pl.BlockSpec(blk, cur(cv)), pl.BlockSpec(blk, prev(cv)),
                  pl.BlockSpec((HEADS_PER_GROUP, ATTN_BLOCK, 2 * ATTN_BLOCK), lambda b, j: (0, 0, 0))],
        out_specs=[pl.BlockSpec(blk, cur(0)), pl.BlockSpec(blk, cur(0))],
        out_shape=[jax.ShapeDtypeStruct((t, ATTN_GW), BF16), jax.ShapeDtypeStruct((t, ATTN_GW), F32)],
        scratch_shapes=[pltpu.VMEM((xrows, ATTN_GW), BF16), pltpu.VMEM((xrows, ATTN_GW), BF16),
                        pltpu.VMEM((ATTN_GW // ATTN_SLAB, ATTN_TILE, ATTN_SLAB), F32),
                        pltpu.VMEM((ATTN_GW // ATTN_SLAB, ATTN_TILE, ATTN_SLAB), F32)],
        compiler_params=_cparams("parallel", "arbitrary"),
        name=f"dattn{group}",
    )(aproj, aproj, aproj, aproj, aproj, bias)


def _rel_bucket(n):
    max_exact = REL_BUCKETS // 2
    nf = jnp.maximum(n, 1).astype(F32)
    log_b = max_exact + (jnp.log(nf / max_exact) / math.log(REL_MAX_DIST / max_exact)
                         * (REL_BUCKETS - max_exact)).astype(jnp.int32)
    return jnp.where(n < max_exact, n, jnp.minimum(log_b, REL_BUCKETS - 1))


def _attn_bias(rel_bias, group):
    window, dilation = ATTN_PATTERNS[group]
    steps = window // dilation
    hp = lax.Precision.HIGHEST
    hs = slice(group * HEADS_PER_GROUP, (group + 1) * HEADS_PER_GROUP)
    bucket = _rel_bucket(jnp.arange(steps + 1) * dilation)
    bias_steps = jnp.dot(jax.nn.one_hot(bucket, REL_BUCKETS, dtype=F32), rel_bias[:, hs].astype(F32),
                         precision=hp)
    qi = jnp.arange(ATTN_BLOCK)[:, None]
    ki = jnp.arange(2 * ATTN_BLOCK)[None, :]
    dist = ATTN_BLOCK + qi - ki
    ok = (dist >= 0) & (dist <= steps)
    sel = jax.nn.one_hot(jnp.clip(dist, 0, steps).reshape(-1), steps + 1, dtype=F32)
    bias = jnp.dot(sel, bias_steps, precision=hp).T.reshape(HEADS_PER_GROUP, ATTN_BLOCK, 2 * ATTN_BLOCK)
    return jnp.where(ok[None], bias, NEG)


def _merge_kernel(ya_ref, yb0_ref, yb1_ref, yb2_ref, l0_ref, l1_ref, l2_ref, gu_ref, gv_ref, gate_ref,
                  x_ref, wa_ref, wb_ref, wc_ref, wo_ref, ws_ref, bs_ref, gg_ref, o_ref, yc_scr, *, tm):
    d = x_ref.shape[1]
    l0, l1, l2 = l0_ref[...], l1_ref[...], l2_ref[...]
    mx = jnp.maximum(jnp.maximum(l0, l1), l2)
    e0, e1, e2 = jnp.exp(l0 - mx), jnp.exp(l1 - mx), jnp.exp(l2 - mx)
    inv = 1.0 / (e0 + e1 + e2)
    yb = jnp.concatenate([(yb0_ref[...].astype(F32) * (e0 * inv)).astype(BF16),
                          (yb1_ref[...].astype(F32) * (e1 * inv)).astype(BF16),
                          (yb2_ref[...].astype(F32) * (e2 * inv)).astype(BF16)], axis=-1)

    for j in range(tm // GMLP_CHUNK):
        rows = slice(j * GMLP_CHUNK, (j + 1) * GMLP_CHUNK)
        for g in range(GMLP_GROUPS):
            cols = slice(g * GMLP_GC, (g + 1) * GMLP_GC)
            u = jax.nn.gelu(gu_ref[rows, cols].astype(F32))
            v = _rms(jax.nn.gelu(gv_ref[rows, cols].astype(F32)), gg_ref[:, cols])
            mixed = _dot(ws_ref[g], v.astype(BF16)) + bs_ref[g]
            yc_scr[rows, cols] = (u * mixed).astype(BF16)

    def gate2(k):
        return jnp.tanh(0.5 * gate_ref[:, k * d:(k + 1) * d].astype(F32)) + 1.0

    merged2 = gate2(0) * _dot(ya_ref[...], wa_ref[...])
    merged2 = merged2 + gate2(1) * _dot(yb, wb_ref[...])
    merged2 = merged2 + gate2(2) * _dot(yc_scr[...], wc_ref[...])
    o_ref[...] = x_ref[...] + 0.5 * _dot(merged2.astype(BF16), wo_ref[...])


def _merge(ya, ybs, lses, proj, x2d, wa, wb, wc, wo, ws, bsb, gg, *, tm):
    t, d = x2d.shape
    row = lambda c: (lambda i: (i, c))
    full2 = lambda i: (0, 0)
    full3 = lambda i: (0, 0, 0)
    gspec = pl.BlockSpec((tm, ATTN_GW), row(0))
    return pl.pallas_call(
        functools.partial(_merge_kernel, tm=tm),
        grid=(t // tm,),
        in_specs=[pl.BlockSpec((tm, MLSTM_W), row(0)),
                  gspec, gspec, gspec, gspec, gspec, gspec,
                  pl.BlockSpec((tm, GMLP_W), row(OFF_GU // GMLP_W)),
                  pl.BlockSpec((tm, GMLP_W), row(OFF_GV // GMLP_W)),
                  pl.BlockSpec((tm, N_BRANCH * d), row(OFF_GATE // (N_BRANCH * d))),
                  pl.BlockSpec((tm, d), row(0)),
                  pl.BlockSpec(wa.shape, full2), pl.BlockSpec(wb.shape, full2),
                  pl.BlockSpec(wc.shape, full2), pl.BlockSpec(wo.shape, full2),
                  pl.BlockSpec(ws.shape, full3), pl.BlockSpec(bsb.shape, full3),
                  pl.BlockSpec(gg.shape, full2)],
        out_specs=pl.BlockSpec((tm, d), row(0)),
        out_shape=jax.ShapeDtypeStruct((t, d), F32),
        scratch_shapes=[pltpu.VMEM((tm, GMLP_W), BF16)],
        compiler_params=_cparams("parallel"),
        name="merge",
    )(ya, *ybs, *lses, proj, proj, proj, x2d, wa, wb, wc, wo, ws, bsb, gg)


def _memkv_kernel(mem_ref, g_ref, w_ref, gk_ref, k_ref, v_ref):
    dh, w = XATTN_DH, XATTN_W
    kv = _dot(_rms(mem_ref[0], g_ref[...]).astype(BF16), w_ref[...])
    for h in range(XATTN_HEADS):
        sl = slice(h * dh, (h + 1) * dh)
        k_ref[0, :, sl] = _rms(kv[:, sl], gk_ref[...]).astype(k_ref.dtype)
    v_ref[0] = kv[:, w:].astype(v_ref.dtype)


def _memkv(mem, gain, w_kv, gk):
    b, m, d = mem.shape
    full2 = lambda i: (0, 0)
    return pl.pallas_call(
        _memkv_kernel,
        grid=(b,),
        in_specs=[pl.BlockSpec((1, m, d), lambda i: (i, 0, 0)),
                  pl.BlockSpec((1, d), full2),
                  pl.BlockSpec(w_kv.shape, full2),
                  pl.BlockSpec((1, XATTN_DH), full2)],
        out_specs=[pl.BlockSpec((1, m, XATTN_W), lambda i: (i, 0, 0)),
                   pl.BlockSpec((1, m, XATTN_W), lambda i: (i, 0, 0))],
        out_shape=[jax.ShapeDtypeStruct((b, m, XATTN_W), BF16),
                   jax.ShapeDtypeStruct((b, m, XATTN_W), BF16)],
        compiler_params=_cparams("parallel", vmem=VMEM_LIMIT_SMALL),
        name="memkv",
    )(mem, gain, w_kv, gk)


def _route(logits):
    tm = logits.shape[1]
    e = jnp.exp(logits - jnp.max(logits, axis=0, keepdims=True))
    probs = e / jnp.sum(e, axis=0, keepdims=True)
    rowi = lax.broadcasted_iota(jnp.int32, (8, tm), 0)
    real = rowi < EXPERTS_PER_GROUP
    tops = []
    for g in range(N_EXPERT_GROUPS):
        pg = jnp.where(real, probs[8 * g:8 * g + 8, :], -0.5)
        m1 = jnp.max(pg, axis=0, keepdims=True)
        i1 = jnp.min(jnp.where(pg == m1, rowi, 8), axis=0, keepdims=True)
        pg2 = jnp.where(rowi == i1, -1.0, pg)
        m2 = jnp.max(pg2, axis=0, keepdims=True)
        i2 = jnp.min(jnp.where(pg2 == m2, rowi, 8), axis=0, keepdims=True)
        tops.append((m1, i1, m2, i2))
    best = jnp.zeros((1, tm), jnp.int32)
    best_score = tops[0][0] + tops[0][2]
    for g in range(1, N_EXPERT_GROUPS):
        score = tops[g][0] + tops[g][2]
        better = score > best_score
        best = jnp.where(better, g, best)
        best_score = jnp.where(better, score, best_score)
    m1, i1, m2, i2 = tops[0]
    for g in range(1, N_EXPERT_GROUPS):
        m1, i1, m2, i2 = (jnp.where(best == g, new, old) for new, old in zip(tops[g], (m1, i1, m2, i2)))
    tot = m1 + m2
    base = best * EXPERTS_PER_GROUP
    return base + i1, base + i2, m1 / tot, m2 / tot


def _pack_bf16_pairs(x):
    n = x.shape[1] // 2
    hi = lax.bitcast_convert_type(x[:, :n].astype(BF16).astype(F32), jnp.uint32)
    lo = lax.bitcast_convert_type(x[:, n:].astype(BF16).astype(F32), jnp.uint32)
    return hi | (lo >> 16)


def _unpack_bf16_pairs(p):
    hi = lax.bitcast_convert_type(p & jnp.uint32(0xFFFF0000), F32)
    lo = lax.bitcast_convert_type(p << 16, F32)
    return hi, lo


def _store_row_chunks(ref, packed):
    for j in range(ROW_CHUNKS):
        ref[j] = packed[:, j * 128:(j + 1) * 128]


def _load_row_chunks(ref):
    return jnp.concatenate([ref[j] for j in range(ROW_CHUNKS)], axis=-1)


def _xattn_kernel(x_ref, k_ref, v_ref, gx_ref, wq_ref, gq_ref, wo_ref, gf_ref, rw_ref, rb_ref,
                  xo_ref, hf_ref, eidx_ref, wts_ref, *, sub):
    dh = XATTN_DH
    rw = rw_ref[...]
    rw_hi, rw_lo = _split_bf16(rw)
    for s in range(x_ref.shape[0] // sub):
        rows = slice(s * sub, (s + 1) * sub)
        x = x_ref[rows, :]
        q = _dot(_rms(x, gx_ref[...]).astype(BF16), wq_ref[...])
        outs = []
        for h in range(XATTN_HEADS):
            sl = slice(h * dh, (h + 1) * dh)
            q_h = (_rms(q[:, sl], gq_ref[...]) * (dh ** -0.5)).astype(BF16)
            logits = _dot_nt(q_h, k_ref[0, :, sl])
            p = jnp.exp(logits - jnp.max(logits, axis=-1, keepdims=True))
            o = _dot(p.astype(BF16), v_ref[0, :, sl]) / jnp.sum(p, axis=-1, keepdims=True)
            outs.append(o.astype(BF16))
        xn = x + _dot(jnp.concatenate(outs, axis=-1), wo_ref[...])
        xo_ref[rows, :] = xn
        hf = _rms(xn, gf_ref[...])
        packed = _pack_bf16_pairs(hf)
        for j in range(ROW_CHUNKS):
            hf_ref[j, rows, :] = packed[:, j * 128:(j + 1) * 128]
        hf_hi, hf_lo = _split_bf16(hf)
        logits_t = _dot_nt(rw_hi, hf_hi) + _dot_nt(rw_hi, hf_lo) + _dot_nt(rw_lo, hf_hi) + rb_ref[...]
        e1, e2, w1, w2 = _route(logits_t)
        eidx_ref[:, rows] = jnp.concatenate([e1, e2, jnp.zeros((6, sub), jnp.int32)], axis=0)
        wts_ref[:, rows] = jnp.concatenate([w1, w2, jnp.zeros((6, sub), F32)], axis=0)


def _xattn(x2d, k, v, gx, wq, gq, wo, gf, rw_t, rb, *, seq, tm):
    t, d = x2d.shape
    per_b = seq // tm
    full2 = lambda i: (0, 0)
    kv_spec = pl.BlockSpec((1,) + k.shape[1:], lambda i: (i // per_b, 0, 0))
    return pl.pallas_call(
        functools.partial(_xattn_kernel, sub=min(tm, XATTN_SUB)),
        grid=(t // tm,),
        in_specs=[pl.BlockSpec((tm, d), lambda i: (i, 0)), kv_spec, kv_spec,
                  pl.BlockSpec((1, d), full2), pl.BlockSpec(wq.shape, full2),
                  pl.BlockSpec((1, XATTN_DH), full2), pl.BlockSpec(wo.shape, full2),
                  pl.BlockSpec((1, d), full2), pl.BlockSpec(rw_t.shape, full2),
                  pl.BlockSpec(rb.shape, full2)],
        out_specs=[pl.BlockSpec((tm, d), lambda i: (i, 0)),
                   pl.BlockSpec((ROW_CHUNKS, tm, 128), lambda i: (0, i, 0)),
                   pl.BlockSpec((8, tm), lambda i: (0, i)),
                   pl.BlockSpec((8, tm), lambda i: (0, i))],
        out_shape=[jax.ShapeDtypeStruct((t, d), F32),
                   jax.ShapeDtypeStruct((ROW_CHUNKS, t, 128), jnp.uint32),
                   jax.ShapeDtypeStruct((8, t), jnp.int32),
                   jax.ShapeDtypeStruct((8, t), F32)],
        compiler_params=_cparams("parallel"),
        name="xattn_router",
    )(x2d, k, v, gx, wq, gq, wo, gf, rw_t, rb)


def _moe_plan_kernel(eidx_ref, i1_ref, i2_ref, te_ref, na_ref, cnt_scr, carry_scr, *, tb, tm, plane_rows):
    ne = N_EXPERTS
    hp = lax.Precision.HIGHEST
    phase, j = pl.program_id(0), pl.program_id(1)
    rows = lax.broadcasted_iota(jnp.int32, (ne, tb), 0)
    oh1 = rows == eidx_ref[0:1, :]
    oh2 = rows == eidx_ref[1:2, :]
    a = oh1.astype(F32) + oh2.astype(F32)
    blk_cnt = jnp.broadcast_to(jnp.sum(a, axis=1, keepdims=True), cnt_scr.shape)

    @pl.when((phase == 0) & (j == 0))
    def _():
        cnt_scr[...] = jnp.zeros_like(cnt_scr)

    @pl.when(phase == 0)
    def _():
        cnt_scr[...] += blk_cnt

    @pl.when((phase == 1) & (j == 0))
    def _():
        padded = jnp.ceil(cnt_scr[...] * (1.0 / tm)) * tm
        er = lax.broadcasted_iota(jnp.int32, (ne, ne), 0)
        ec = lax.broadcasted_iota(jnp.int32, (ne, ne), 1)
        off = jnp.dot((ec < er).astype(F32), padded, precision=hp, preferred_element_type=F32)
        carry_scr[...] = off
        seg_end = (off + padded)[:, 0:1]
        tile_start = lax.broadcasted_iota(jnp.int32, (ne, te_ref.shape[1]), 1).astype(F32) * tm
        te = jnp.minimum(jnp.sum((seg_end <= tile_start).astype(F32), axis=0, keepdims=True), ne - 1.0)
        seg_start, used_end = off[:, 0:1], (off + cnt_scr[...])[:, 0:1]
        in_seg = (seg_start <= tile_start) & (tile_start < seg_end)
        used = jnp.sum(jnp.where(in_seg, jnp.clip(used_end - tile_start, 0.0, float(tm)), 0.0),
                       axis=0, keepdims=True)
        row = lax.broadcasted_iota(jnp.int32, te_ref.shape, 0)
        te_ref[...] = jnp.where(row == 1, used, te).astype(jnp.int32)
        total = jnp.sum(padded[:, 0:1], axis=0, keepdims=True)
        na_ref[...] = jnp.broadcast_to(total * (1.0 / tm), na_ref.shape).astype(jnp.int32)

    @pl.when(phase == 1)
    def _():
        before = (lax.broadcasted_iota(jnp.int32, (tb, tb), 0)
                  < lax.broadcasted_iota(jnp.int32, (tb, tb), 1)).astype(BF16)
        rank = carry_scr[:, 0:1] + _dot(a.astype(BF16), before)
        d1 = jnp.sum(jnp.where(oh1, rank, 0.0), axis=0, keepdims=True).astype(jnp.int32)
        d2 = jnp.sum(jnp.where(oh2, rank, 0.0), axis=0, keepdims=True).astype(jnp.int32)
        plane = lax.broadcasted_iota(jnp.int32, (8, tb), 0) * plane_rows
        i1_ref[...] = jnp.where(plane < ROW_CHUNKS * plane_rows, plane + d1, 0)
        i2_ref[...] = jnp.where(plane < ROW_CHUNKS * plane_rows, plane + d2, 0)
        carry_scr[...] += blk_cnt


def _moe_plan(eidx, *, tm, n_tiles, tb=512):
    t = eidx.shape[1]
    ntp = -(-n_tiles // 128) * 128
    return pl.pallas_call(
        functools.partial(_moe_plan_kernel, tb=tb, tm=tm, plane_rows=n_tiles * tm),
        grid=(2, t // tb),
        in_specs=[pl.BlockSpec((8, tb), lambda p, j: (0, j))],
        out_specs=[pl.BlockSpec((8, tb), lambda p, j: (0, j * p)),
                   pl.BlockSpec((8, tb), lambda p, j: (0, j * p)),
                   pl.BlockSpec((8, ntp), lambda p, j: (0, 0)),
                   pl.BlockSpec((8, 128), lambda p, j: (0, 0))],
        out_shape=[jax.ShapeDtypeStruct((8, t), jnp.int32),
                   jax.ShapeDtypeStruct((8, t), jnp.int32),
                   jax.ShapeDtypeStruct((8, ntp), jnp.int32),
                   jax.ShapeDtypeStruct((8, 128), jnp.int32)],
        scratch_shapes=[pltpu.VMEM((N_EXPERTS, 128), F32), pltpu.VMEM((N_EXPERTS, 128), F32)],
        compiler_params=_cparams("arbitrary", "arbitrary", vmem=VMEM_LIMIT_SMALL),
        name="moe_plan",
    )(eidx)


def _sc_mesh():
    return plsc.VectorSubcoreMesh(core_axis_name="c", subcore_axis_name="s",
                                  num_cores=SC_CORES, num_subcores=SC_SUBCORES)


def _sc_index_spec(tokens):
    nb = tokens // SC_WINDOW
    return pl.BlockSpec((1, SC_WINDOW), lambda i: (i // nb, i % nb))


def _sc_dispatch(rows, i1, i2, n_out):
    n = rows.shape[0]
    tokens = i1.shape[1]

    @functools.partial(pl.kernel, out_type=jax.ShapeDtypeStruct((n_out, 128), rows.dtype), mesh=_sc_mesh(),
                       name="moe_dispatch")
    def k(x_hbm, i1_hbm, i2_hbm, o_hbm):
        def body(x_vmem, i1_vmem, i2_vmem):
            pltpu.sync_copy(x_vmem, o_hbm.at[i1_vmem.at[0]])
            pltpu.sync_copy(x_vmem, o_hbm.at[i2_vmem.at[0]])

        pltpu.emit_pipeline(
            body, grid=(n // SC_WINDOW,),
            in_specs=[pl.BlockSpec((SC_WINDOW, 128), lambda i: (i, 0)),
                      _sc_index_spec(tokens), _sc_index_spec(tokens)],
            out_specs=[],
            core_axis_name=("c", "s"), dimension_semantics=(pltpu.PARALLEL,),
        )(x_hbm, i1_hbm, i2_hbm)

    return k(rows, i1, i2)


def _sc_collect(table, i1, i2):
    tokens = i1.shape[1]
    n = ROW_CHUNKS * tokens
    out = jax.ShapeDtypeStruct((n, 128), table.dtype)

    @functools.partial(pl.kernel, out_type=(out, out), mesh=_sc_mesh(), name="moe_collect")
    def k(t_hbm, i1_hbm, i2_hbm, o1_hbm, o2_hbm):
        def body(i1_vmem, i2_vmem, o1_vmem, o2_vmem):
            pltpu.sync_copy(t_hbm.at[i1_vmem.at[0]], o1_vmem)
            pltpu.sync_copy(t_hbm.at[i2_vmem.at[0]], o2_vmem)

        pltpu.emit_pipeline(
            body, grid=(n // SC_WINDOW,),
            in_specs=[_sc_index_spec(tokens), _sc_index_spec(tokens)],
            out_specs=[pl.BlockSpec((SC_WINDOW, 128), lambda i: (i, 0)),
                       pl.BlockSpec((SC_WINDOW, 128), lambda i: (i, 0))],
            core_axis_name=("c", "s"), dimension_semantics=(pltpu.PARALLEL,),
        )(i1_hbm, i2_hbm, o1_hbm, o2_hbm)

    return k(table, i1, i2)


def _experts_kernel(te_ref, tv_ref, na_ref, xs_ref, wg_ref, wu_ref, wd_ref, y_ref, wg_scr, wu_scr, wd_scr):
    i = pl.program_id(0)
    active = i < na_ref[0]
    tm = xs_ref.shape[1]

    @pl.when(active & ((i == 0) | (te_ref[i] != te_ref[jnp.maximum(i - 1, 0)])))
    def _():
        wg_scr[...] = wg_ref[0, 0].astype(BF16)
        wu_scr[...] = wu_ref[0, 0].astype(BF16)
        wd_scr[...] = wd_ref[0, 0].astype(BF16)

    def ffn(rows):
        packed = jnp.concatenate([xs_ref[j, rows, :] for j in range(ROW_CHUNKS)], axis=-1)
        hi, lo = _unpack_bf16_pairs(packed)
        h = jnp.concatenate([hi, lo], axis=-1).astype(BF16)
        up = _dot(h, wg_scr[...])
        act = up * _sigmoid(up) * _dot(h, wu_scr[...])
        out = _pack_bf16_pairs(_dot(act.astype(BF16), wd_scr[...]))
        for j in range(ROW_CHUNKS):
            y_ref[j, rows, :] = out[:, j * 128:(j + 1) * 128]

    @pl.when(active & (tv_ref[i] == tm))
    def _():
        ffn(slice(0, tm))

    for s in range(tm // EXPERT_SUB):
        @pl.when(active & (tv_ref[i] < tm) & (s * EXPERT_SUB < tv_ref[i]))
        def _(s=s):
            ffn(slice(s * EXPERT_SUB, (s + 1) * EXPERT_SUB))


def _experts(tile_expert, tile_used, n_active, xs, wg, wu, wd, *, layer, tm):
    n_tiles = tile_expert.shape[0]
    _, _, d, dff = wg.shape
    rows = lambda i, te, tv, na: (0, jnp.minimum(i, na[0] - 1), 0)
    expert = lambda i, te, tv, na: (layer, te[i], 0, 0)
    return pl.pallas_call(
        _experts_kernel,
        grid_spec=pltpu.PrefetchScalarGridSpec(
            num_scalar_prefetch=3,
            grid=(n_tiles,),
            in_specs=[pl.BlockSpec((ROW_CHUNKS, tm, 128), rows),
                      pl.BlockSpec((1, 1, d, dff), expert),
                      pl.BlockSpec((1, 1, d, dff), expert),
                      pl.BlockSpec((1, 1, dff, d), expert)],
            out_specs=pl.BlockSpec((ROW_CHUNKS, tm, 128), rows),
            scratch_shapes=[pltpu.VMEM((d, dff), BF16), pltpu.VMEM((d, dff), BF16), pltpu.VMEM((dff, d), BF16)]),
        out_shape=jax.ShapeDtypeStruct(xs.shape, xs.dtype),
        compiler_params=_cparams("arbitrary"),
        name="moe_experts",
    )(tile_expert, tile_used, n_active, xs, wg, wu, wd)


def _moe_combine_kernel(x_ref, y1_ref, y2_ref, w_ref, o_ref):
    half = x_ref.shape[1] // 2
    hi1, lo1 = _unpack_bf16_pairs(_load_row_chunks(y1_ref))
    hi2, lo2 = _unpack_bf16_pairs(_load_row_chunks(y2_ref))
    tm = x_ref.shape[0]
    w_cols = jnp.concatenate([w_ref[...], jnp.zeros((128 - w_ref.shape[0], tm), F32)], axis=0).T
    w1, w2 = w_cols[:, 0:1], w_cols[:, 1:2]
    o_ref[:, :half] = x_ref[:, :half] + w1 * hi1 + w2 * hi2
    o_ref[:, half:] = x_ref[:, half:] + w1 * lo1 + w2 * lo2


def _moe_combine(x2d, y1, y2, wts, *, tm):
    t, d = x2d.shape
    chunk_spec = pl.BlockSpec((ROW_CHUNKS, tm, 128), lambda i: (0, i, 0))
    return pl.pallas_call(
        _moe_combine_kernel,
        grid=(t // tm,),
        in_specs=[pl.BlockSpec((tm, d), lambda i: (i, 0)), chunk_spec, chunk_spec,
                  pl.BlockSpec((wts.shape[0], tm), lambda i: (0, i))],
        out_specs=pl.BlockSpec((tm, d), lambda i: (i, 0)),
        out_shape=jax.ShapeDtypeStruct((t, d), F32),
        compiler_params=_cparams("parallel", vmem=VMEM_LIMIT_SMALL),
        name="moe_combine",
    )(x2d, y1, y2, wts)


def _moe(x2d, hf_rows, eidx, wts, wg, wu, wd, *, layer):
    t = x2d.shape[0]
    tm = MOE_TM
    n_tiles = 2 * t // tm + N_EXPERTS
    plane = n_tiles * tm
    i1, i2, te, na = _moe_plan(eidx, tm=tm, n_tiles=n_tiles)
    xs = _sc_dispatch(hf_rows.reshape(ROW_CHUNKS * t, 128), i1, i2, ROW_CHUNKS * plane)
    ys = _experts(te[0, :n_tiles], te[1, :n_tiles], na[0, :1], xs.reshape(ROW_CHUNKS, plane, 128), wg, wu, wd,
                  layer=layer, tm=tm)
    y1, y2 = _sc_collect(ys.reshape(ROW_CHUNKS * plane, 128), i1, i2)
    return _moe_combine(x2d, y1.reshape(ROW_CHUNKS, t, 128), y2.reshape(ROW_CHUNKS, t, 128), wts,
                        tm=COMBINE_TM)


W_ROWS = 256


def _w_rows_kernel(start_ref, valid_ref, w_ref, o_ref):
    del start_ref
    row = lax.broadcasted_iota(jnp.int32, w_ref.shape[1:], 0)
    o_ref[0] = jnp.where(row < valid_ref[pl.program_id(1)], w_ref[0], 0.0).astype(o_ref.dtype)


def _w_rows(w_t, starts, valid):
    depth, _, d = w_t.shape
    nblk = len(starts)
    return pl.pallas_call(
        _w_rows_kernel,
        grid_spec=pltpu.PrefetchScalarGridSpec(
            num_scalar_prefetch=2,
            grid=(depth, nblk),
            in_specs=[pl.BlockSpec((pl.Element(1), pl.Element(W_ROWS), pl.Element(d)),
                                   lambda l, c, st, va: (l, pl.multiple_of(st[c], 8), 0))],
            out_specs=pl.BlockSpec((1, W_ROWS, d), lambda l, c, st, va: (l, c, 0))),
        out_shape=jax.ShapeDtypeStruct((depth, nblk * W_ROWS, d), BF16),
        compiler_params=_cparams("parallel", "arbitrary", vmem=VMEM_LIMIT_SMALL),
        name="w_in_rows",
    )(jnp.asarray(starts, jnp.int32), jnp.asarray(valid, jnp.int32), w_t)


def _w_in_layout(w_in):
    w_t = jnp.swapaxes(w_in, 1, 2)
    src_if = 4 * MLSTM_W
    src_a = src_if + 2 * MLSTM_HEADS
    src_g = src_a + 3 * ATTN_W
    starts = list(range(0, src_if, W_ROWS)) + [src_g + k * W_ROWS for k in range((OFF_IF - OFF_GU) // W_ROWS)]
    valid = [W_ROWS] * len(starts)
    starts.append(src_if)
    valid.append(2 * MLSTM_HEADS)
    assert len(starts) * W_ROWS == N_PROJ and ATTN_GW == W_ROWS
    a_starts = [src_a + j * ATTN_W + g * ATTN_GW for g in range(len(ATTN_PATTERNS)) for j in range(3)]
    return _w_rows(w_t, starts, valid), _w_rows(w_t, a_starts, [W_ROWS] * len(a_starts))


def kernel(x, mem, norm_mix, w_in, mlstm_conv, mlstm_gate_b, mlstm_norm, attn_qk_norm, gmlp_norm, gmlp_ws,
           gmlp_bs, w_branch_a, w_branch_b, w_branch_c, w_out, rel_bias, norm_xattn, norm_mem, w_xq, w_xkv,
           xattn_qk_norm, w_xo, norm_ffn, router_w, router_b, w_expert_gate, w_expert_up, w_expert_down):
    b, s, d = x.shape
    t = b * s
    depth = w_in.shape[0]
    x2d = x.reshape(t, d)

    biases = [_attn_bias(rel_bias, g) for g in range(len(ATTN_PATTERNS))]
    rw_t = jnp.zeros((N_EXPERT_GROUPS, 8, d), F32).at[:, :EXPERTS_PER_GROUP].set(
        router_w.T.reshape(N_EXPERT_GROUPS, EXPERTS_PER_GROUP, d)).reshape(ROUTER_ROWS, d)
    rb = jnp.full((N_EXPERT_GROUPS, 8), NEG, F32).at[:, :EXPERTS_PER_GROUP].set(
        router_b.astype(F32).reshape(N_EXPERT_GROUPS, EXPERTS_PER_GROUP)).reshape(ROUTER_ROWS, 1)
    tril = jnp.tril(jnp.ones((GMLP_CHUNK, GMLP_CHUNK), bool))
    head_of = jnp.arange(ATTN_GW) // ATTN_DH
    seg_ones = (head_of[:, None] == head_of[None, :]).astype(BF16)

    w_main, w_attn = _w_in_layout(w_in)

    for l in range(depth):
        proj, h_mix, gates_t = _inproj(x2d, norm_mix[l][None], w_main, layer=l, tm=INPROJ_TM,
                                       tn=INPROJ_TN)
        gq = jnp.tile(attn_qk_norm[l, 0], HEADS_PER_GROUP)[None]
        gk = jnp.tile(attn_qk_norm[l, 1], HEADS_PER_GROUP)[None]

        nh = MLSTM_HEADS
        bias_i = jnp.zeros((8, 1), F32).at[:nh, 0].set(mlstm_gate_b[l, :nh])
        bias_f = jnp.zeros((8, 1), F32).at[:nh, 0].set(mlstm_gate_b[l, nh:])
        ya = _mlstm_rows(proj, gates_t, mlstm_conv[l], bias_i, bias_f, mlstm_norm[l][None],
                         batch=b, seq=s, blk=MLSTM_BLOCK, group=MLSTM_GROUP)

        ybs, lses = [], []
        for g, (_, dilation) in enumerate(ATTN_PATTERNS):
            aproj = _attnproj(h_mix, w_attn, seg_ones, gq, gk, layer=l, group=g, dilation=dilation)
            o, lse = _dattn(aproj, biases[g], seq=s, group=g, dilation=dilation)
            ybs.append(o)
            lses.append(lse)

        ws = jnp.where(tril, gmlp_ws[l], 0.0).astype(BF16)
        bsb = jnp.broadcast_to(gmlp_bs[l][:, :, None], (GMLP_GROUPS, GMLP_CHUNK, GMLP_GC)).astype(F32)
        x2d = _merge(ya, ybs, lses, proj, x2d, w_branch_a[l].astype(BF16), w_branch_b[l].astype(BF16),
                     w_branch_c[l].astype(BF16), w_out[l].astype(BF16), ws, bsb, gmlp_norm[l][None],
                     tm=MERGE_TM)

        k_mem, v_mem = _memkv(mem, norm_mem[l][None], w_xkv[l].astype(BF16), xattn_qk_norm[l, 1][None])
        x2d, hf_rows, eidx, wts = _xattn(x2d, k_mem, v_mem, norm_xattn[l][None], w_xq[l].astype(BF16),
                                         xattn_qk_norm[l, 0][None], w_xo[l].astype(BF16), norm_ffn[l][None],
                                         rw_t, rb, seq=s, tm=XATTN_TM)

        x2d = _moe(x2d, hf_rows, eidx, wts, w_expert_gate, w_expert_up, w_expert_down, layer=l)

    return x2d.reshape(b, s, d)
```

```python
import functools
import math

import jax
import jax.numpy as jnp
import numpy as np
from jax import lax
from jax.experimental import pallas as pl
from jax.experimental.pallas import tpu as pltpu
from jax.experimental.pallas import tpu_sc as plsc

F32 = jnp.float32
BF16 = jnp.bfloat16

EPS = 1e-6
NEG = -1e30

MLSTM_HEADS = 4
MLSTM_DH = 128
MLSTM_W = MLSTM_HEADS * MLSTM_DH
CONV_WIDTH = 4
MLSTM_BLOCK = 128
MLSTM_GROUP = 4

ATTN_PATTERNS = ((128, 1), (512, 4), (2048, 16))
HEADS_PER_GROUP = 4
ATTN_DH = 64
ATTN_GW = HEADS_PER_GROUP * ATTN_DH
ATTN_W = len(ATTN_PATTERNS) * ATTN_GW
ATTN_BLOCK = 128
REL_BUCKETS = 32
REL_MAX_DIST = 2048

GMLP_GROUPS = 4
GMLP_GC = 128
GMLP_W = GMLP_GROUPS * GMLP_GC
GMLP_CHUNK = 128

XATTN_HEADS = 4
XATTN_DH = 128
XATTN_W = XATTN_HEADS * XATTN_DH
XATTN_SUB = 1024

N_EXPERTS = 16
N_EXPERT_GROUPS = 4
EXPERTS_PER_GROUP = 4
ROUTER_ROWS = 8 * N_EXPERT_GROUPS

N_BRANCH = 3

MOE_TM = 1024
ROW_CHUNKS = 4
SC_CORES, SC_SUBCORES = 2, 16
SC_WINDOW = 128

OFF_MQ, OFF_MK, OFF_MV, OFF_MO = 0, 512, 1024, 1536
OFF_GU, OFF_GV = 2048, 2560
OFF_GATE = 3072
OFF_IF = 6144
IF_PAD = 256
N_PROJ = OFF_IF + IF_PAD

ATTN_TILE = 2048
ATTN_SUB = ATTN_TILE // ATTN_BLOCK
ATTN_SLAB = 2 * ATTN_DH
ATTN_COLS = HEADS_PER_GROUP * ATTN_SLAB + 2 * ATTN_GW

VMEM_LIMIT = 48 * 1024 * 1024
VMEM_LIMIT_INPROJ = 56 * 1024 * 1024
VMEM_LIMIT_SMALL = 24 * 1024 * 1024

INPROJ_TM, INPROJ_TN = 1024, 3072
ATTNPROJ_SUB = 512
MERGE_TM = 512
XATTN_TM = 1024
COMBINE_TM = 512


def _cparams(*sem, vmem=VMEM_LIMIT):
    return pltpu.CompilerParams(dimension_semantics=sem, vmem_limit_bytes=vmem)


def _rms(x, gain):
    return x * lax.rsqrt(jnp.mean(x * x, axis=-1, keepdims=True) + EPS) * gain


def _sigmoid(x):
    return 0.5 * jnp.tanh(0.5 * x) + 0.5


def _dot(a, b):
    return jnp.dot(a, b, preferred_element_type=F32)


def _dot_nt(a, b):
    return lax.dot_general(a, b, (((1,), (1,)), ((), ())), preferred_element_type=F32)


def _inproj_kernel(x_ref, g_ref, w_ref, wg_ref, o_ref, h_ref, gt_ref):
    @pl.when(pl.program_id(1) == 0)
    def _():
        h = _rms(x_ref[...], g_ref[...]).astype(BF16)
        h_ref[...] = h
        gt_ref[...] = _dot_nt(wg_ref[0, 0:128, :], h)[:gt_ref.shape[0], :]

    o_ref[...] = _dot_nt(h_ref[...], w_ref[0]).astype(o_ref.dtype)


def _inproj(x2d, gain, w, *, layer, tm, tn):
    t, d = x2d.shape
    n = OFF_IF
    return pl.pallas_call(
        _inproj_kernel,
        grid=(t // tm, n // tn),
        in_specs=[pl.BlockSpec((tm, d), lambda i, j: (i, 0)),
                  pl.BlockSpec((1, d), lambda i, j: (0, 0)),
                  pl.BlockSpec((1, tn, d), lambda i, j: (layer, j, 0)),
                  pl.BlockSpec((1, IF_PAD, d), lambda i, j: (layer, OFF_IF // IF_PAD, 0))],
        out_specs=[pl.BlockSpec((tm, tn), lambda i, j: (i, j)),
                   pl.BlockSpec((tm, d), lambda i, j: (i, 0)),
                   pl.BlockSpec((8, tm), lambda i, j: (0, i))],
        out_shape=[jax.ShapeDtypeStruct((t, n), BF16), jax.ShapeDtypeStruct((t, d), BF16),
                   jax.ShapeDtypeStruct((8, t), F32)],
        compiler_params=_cparams("parallel", "arbitrary", vmem=VMEM_LIMIT_INPROJ),
        name="inproj",
    )(x2d, gain, w, w)


def _log_sigmoid(x):
    return jnp.minimum(x, 0.0) - jnp.log(1.0 + jnp.exp(-jnp.abs(x)))


def _split_bf16(x):
    hi = x.astype(BF16)
    return hi, (x - hi.astype(F32)).astype(BF16)


def _prefix_max(x):
    n = x.shape[1]
    lane = lax.broadcasted_iota(jnp.int32, x.shape, 1)
    shift = 1
    while shift < n:
        x = jnp.maximum(x, jnp.where(lane >= shift, pltpu.roll(x, shift, 1), NEG))
        shift *= 2
    return x


def _mlstm_rows_kernel(qk_ref, v_ref, og_ref, *rest, blk, group):
    gate_refs = rest[:group]
    cw_ref, bi_ref, bf_ref, ng_ref, y_ref, xe_scr, s_scr, m_scr = rest[group:]
    heads, dh, w = MLSTM_HEADS, MLSTM_DH, MLSTM_W

    @pl.when(pl.program_id(1) == 0)
    def _():
        xe_scr[:, 0:8, :] = jnp.zeros((group, 8, 2 * w), F32)
        s_scr[...] = jnp.zeros_like(s_scr)
        m_scr[...] = jnp.zeros_like(m_scr)

    cw = cw_ref[...]
    causal = lax.broadcasted_iota(jnp.int32, (blk, blk), 0) >= lax.broadcasted_iota(jnp.int32, (blk, blk), 1)
    triu = (lax.broadcasted_iota(jnp.int32, (blk, blk), 0)
            <= lax.broadcasted_iota(jnp.int32, (blk, blk), 1)).astype(BF16)
    ones = jnp.ones((blk, dh), BF16)
    s_in = [[s_scr[g, h] for h in range(heads)] for g in range(group)]
    m_in = [m_scr[g, :, 0:1] for g in range(group)]
    s_out = [[None] * heads for _ in range(group)]
    m_out = [None] * group
    per_seq = []
    for g in range(group):
        xe_scr[g, 8:8 + blk, :] = qk_ref[g].astype(F32)
        conv = cw[CONV_WIDTH - 1:CONV_WIDTH, :] * xe_scr[g, 8:8 + blk, :]
        for j in range(CONV_WIDTH - 1):
            off = 8 - (CONV_WIDTH - 1) + j
            conv = conv + cw[j:j + 1, :] * xe_scr[g, off:off + blk, :]
        xe_scr[g, 0:8, :] = xe_scr[g, blk:blk + 8, :]
        qk = conv * _sigmoid(conv)

        gates = gate_refs[g][...]
        i_r = gates + bi_ref[...]
        lf_hi, lf_lo = _split_bf16(_log_sigmoid(pltpu.roll(gates, heads, 0) + bf_ref[...]))
        b_r = _dot(lf_hi, triu) + _dot(lf_lo, triu)
        m_st = m_in[g]
        a_r = i_r - b_r
        inter = b_r + m_st
        m_t = jnp.maximum(inter, b_r + _prefix_max(a_r))
        b_last = b_r[:, blk - 1:blk]
        dec = b_last - b_r + i_r
        m_new = jnp.maximum(b_last + m_st, jnp.max(dec, axis=1, keepdims=True))
        w_c = jnp.exp(b_last + m_st - m_new)
        m_out[g] = m_new
        pack = jnp.concatenate([b_r - m_t, jnp.exp(inter - m_t), jnp.exp(-m_t), jnp.exp(dec - m_new),
                                jnp.zeros((blk - 32, blk), F32)], axis=0)
        per_seq.append((qk, a_r, pack.T, w_c))

    chains = [(g, h) for h in range(heads) for g in range(group)]
    st = {}
    for g, h in chains:
        qk = per_seq[g][0]
        sl = slice(h * dh, (h + 1) * dh)
        q_b = qk[:, sl].astype(BF16)
        k_f = qk[:, w + h * dh:w + (h + 1) * dh] * (dh ** -0.5)
        v_ext = jnp.concatenate([v_ref[g, :, sl], ones], axis=-1)
        st[g, h] = (q_b, k_f, v_ext, _dot_nt(q_b, k_f.astype(BF16)), _dot(q_b, s_in[g][h].astype(BF16)))
    for g, h in chains:
        q_b, k_f, v_ext, qk_t, q_state = st[g, h]
        _, a_r, cols, _ = per_seq[g]
        u_c, w_inter = cols[:, h:h + 1], cols[:, 8 + h:9 + h]
        w_intra = jnp.exp(jnp.where(causal, u_c + a_r[h:h + 1, :], NEG))
        st[g, h] = (k_f, v_ext, _dot((qk_t * w_intra).astype(BF16), v_ext) + w_inter * q_state)
    for g, h in chains:
        k_f, v_ext, tot = st[g, h]
        _, _, cols, w_c = per_seq[g]
        em_c, w_k = cols[:, 16 + h:17 + h], cols[:, 24 + h:25 + h]
        sl = slice(h * dh, (h + 1) * dh)
        num, den = tot[:, :dh], tot[:, dh:]
        hh = num / jnp.maximum(jnp.abs(den), em_c)
        hn = _rms(hh, ng_ref[:, sl])
        y_ref[g, :, sl] = (hn * _sigmoid(og_ref[g, :, sl].astype(F32))).astype(y_ref.dtype)
        s_out[g][h] = w_c[h:h + 1, :] * s_in[g][h] + _dot((k_f * w_k).T.astype(BF16), v_ext)
    for g in range(group):
        m_scr[g] = jnp.broadcast_to(m_out[g], m_scr.shape[1:])
        for h in range(heads):
            s_scr[g, h] = s_out[g][h]


def _mlstm_rows(proj, gates_t, conv_w, bias_i, bias_f, norm_g, *, batch, seq, blk, group):
    t, npj = proj.shape
    w = MLSTM_W
    proj3 = proj.reshape(batch, seq, npj)
    cols = lambda c: (lambda b, i: (b, i, c))
    const2 = lambda b, i: (0, 0)
    nblk = seq // blk
    gate_specs = [pl.BlockSpec((8, blk), functools.partial(lambda b, i, g: (0, (b * group + g) * nblk + i), g=g))
                  for g in range(group)]
    y = pl.pallas_call(
        functools.partial(_mlstm_rows_kernel, blk=blk, group=group),
        grid=(batch // group, seq // blk),
        in_specs=[pl.BlockSpec((group, blk, 2 * w), cols(OFF_MQ // (2 * w))),
                  pl.BlockSpec((group, blk, w), cols(OFF_MV // w)),
                  pl.BlockSpec((group, blk, w), cols(OFF_MO // w)),
                  *gate_specs,
                  pl.BlockSpec((CONV_WIDTH, 2 * w), const2),
                  pl.BlockSpec((8, 1), const2), pl.BlockSpec((8, 1), const2),
                  pl.BlockSpec((1, w), const2)],
        out_specs=pl.BlockSpec((group, blk, w), cols(0)),
        out_shape=jax.ShapeDtypeStruct((batch, seq, w), BF16),
        scratch_shapes=[pltpu.VMEM((group, blk + 8, 2 * w), F32),
                        pltpu.VMEM((group, MLSTM_HEADS, MLSTM_DH, 2 * MLSTM_DH), F32),
                        pltpu.VMEM((group, 8, 128), F32)],
        compiler_params=_cparams("parallel", "arbitrary", vmem=VMEM_LIMIT_SMALL),
        name="mlstm",
    )(proj3, proj3, proj3, *([gates_t] * group), conv_w, bias_i, bias_f, norm_g)
    return y.reshape(t, w)


def _attnproj_kernel(h_ref, w_ref, seg_ref, gq_ref, gk_ref, o_ref, r_scr, *, dil):
    gw, half = ATTN_GW, ATTN_SLAB // 2
    sub_rows = r_scr.shape[2]
    seg, sub_seg = ATTN_TILE // dil, sub_rows // dil

    def head_norm(x, gain):
        ss = _dot((x * x).astype(BF16), seg_ref[...])
        return x * lax.rsqrt(ss * (1.0 / ATTN_DH) + EPS) * gain

    low = lax.broadcasted_iota(jnp.int32, (1, ATTN_SLAB), 1) < half
    for s in range(ATTN_TILE // sub_rows):
        rows = slice(s * sub_rows, (s + 1) * sub_rows)
        res = _dot_nt(h_ref[rows, :], w_ref[0])
        q = head_norm(res[:, :gw], gq_ref[...]) * (ATTN_DH ** -0.5)
        k = head_norm(res[:, gw:2 * gw], gk_ref[...])
        slabs = []
        for pair in range(gw // ATTN_SLAB):
            qp = q[:, pair * ATTN_SLAB:(pair + 1) * ATTN_SLAB]
            slabs += [jnp.where(low, qp, 0.0), jnp.where(low, 0.0, qp)]
        slabs += [k[:, c * 128:(c + 1) * 128] for c in range(gw // 128)]
        slabs += [res[:, 2 * gw + c * 128:2 * gw + (c + 1) * 128] for c in range(gw // 128)]
        for c, slab in enumerate(slabs):
            if dil == 1:
                o_ref[rows, c * 128:(c + 1) * 128] = slab.astype(o_ref.dtype)
            else:
                r_scr[s % 2, c] = slab
        if dil > 1:
            for r in range(dil):
                dst = slice(r * seg + s * sub_seg, r * seg + (s + 1) * sub_seg)
                for c in range(r_scr.shape[1]):
                    o_ref[dst, c * 128:(c + 1) * 128] = (
                        r_scr[s % 2, c, pl.ds(r, sub_seg, stride=dil), :].astype(o_ref.dtype))


def _attnproj(h, w, seg_ones, gq, gk, *, layer, group, dilation):
    t, d = h.shape
    wcols = 3 * ATTN_GW
    const2 = lambda i: (0, 0)
    return pl.pallas_call(
        functools.partial(_attnproj_kernel, dil=dilation),
        grid=(t // ATTN_TILE,),
        in_specs=[pl.BlockSpec((ATTN_TILE, d), lambda i: (i, 0)),
                  pl.BlockSpec((1, wcols, d), lambda i: (layer, group, 0)),
                  pl.BlockSpec((ATTN_GW, ATTN_GW), const2),
                  pl.BlockSpec((1, ATTN_GW), const2), pl.BlockSpec((1, ATTN_GW), const2)],
        out_specs=pl.BlockSpec((ATTN_TILE, ATTN_COLS), lambda i: (i, 0)),
        out_shape=jax.ShapeDtypeStruct((t, ATTN_COLS), BF16),
        scratch_shapes=[pltpu.VMEM((2, ATTN_COLS // 128, ATTNPROJ_SUB, 128), F32)],
        compiler_params=_cparams("parallel"),
        name=f"attnproj{group}",
    )(h, w, seg_ones, gq, gk)


def _dattn_kernel(q_ref, kc_ref, kp_ref, vc_ref, vp_ref, bias_ref, o_ref, lse_ref,
                  kx_scr, vx_scr, o_scr, l_scr, *, dil):
    blk = ATTN_BLOCK
    per = ATTN_SUB // dil
    first_tile = pl.program_id(1) == 0
    for r in range(dil):
        base = r * (per + 1) * blk
        last = slice((r * per + per - 1) * blk, (r * per + per) * blk)
        mine = slice(r * per * blk, (r + 1) * per * blk)
        kx_scr[base:base + blk, :] = kp_ref[last, :]
        vx_scr[base:base + blk, :] = vp_ref[last, :]
        kx_scr[base + blk:base + (per + 1) * blk, :] = kc_ref[mine, :]
        vx_scr[base + blk:base + (per + 1) * blk, :] = vc_ref[mine, :]

    low = lax.broadcasted_iota(jnp.int32, (1, ATTN_SLAB), 1) < ATTN_SLAB // 2
    no_prev = lax.broadcasted_iota(jnp.int32, (1, 2 * blk), 1) < blk
    for r in range(dil):
        for sub in range(per):
            u = r * per + sub
            win = slice((r * (per + 1) + sub) * blk, (r * (per + 1) + sub + 2) * blk)
            o_slabs, l_slabs = [], []
            for pair in range(ATTN_GW // ATTN_SLAB):
                cols = slice(pair * ATTN_SLAB, (pair + 1) * ATTN_SLAB)
                kx, vx = kx_scr[win, cols], vx_scr[win, cols]
                o_pair, l_pair = [], []
                for h in (2 * pair, 2 * pair + 1):
                    logits = _dot_nt(q_ref[u * blk:(u + 1) * blk, h * ATTN_SLAB:(h + 1) * ATTN_SLAB], kx)
                    logits = logits + bias_ref[h]
                    if sub == 0:
                        logits = jnp.where(first_tile & no_prev, NEG, logits)
                    m = jnp.max(logits, axis=-1, keepdims=True)
                    p = jnp.exp(logits - m)
                    l = jnp.sum(p, axis=-1, keepdims=True)
                    o_pair.append(_dot(p.astype(BF16), vx) / l)
                    l_pair.append(m + jnp.log(l))
                o_slabs.append(jnp.where(low, o_pair[0], o_pair[1]))
                l_slabs.append(jnp.where(low, l_pair[0], l_pair[1]))
            dst = pl.ds(sub * blk * dil + r, blk, stride=dil) if dil > 1 else slice(u * blk, (u + 1) * blk)
            for c in range(ATTN_GW // ATTN_SLAB):
                o_scr[c, dst, :] = o_slabs[c]
                l_scr[c, dst, :] = l_slabs[c]
    for c in range(ATTN_GW // ATTN_SLAB):
        o_ref[:, c * ATTN_SLAB:(c + 1) * ATTN_SLAB] = o_scr[c].astype(o_ref.dtype)
        lse_ref[:, c * ATTN_SLAB:(c + 1) * ATTN_SLAB] = l_scr[c]


def _dattn(aproj, bias, *, seq, group, dilation):
    t = aproj.shape[0]
    tiles = seq // ATTN_TILE
    qw = HEADS_PER_GROUP * ATTN_SLAB
    cq, ck, cv = 0, qw // ATTN_GW, qw // ATTN_GW + 1
    blk = (ATTN_TILE, ATTN_GW)
    cur = lambda c: (lambda b, j: (b * tiles + j, c))
    prev = lambda c: (lambda b, j: (b * tiles + jnp.maximum(j - 1, 0), c))
    xrows = ATTN_TILE + dilation * ATTN_BLOCK
    return pl.pallas_call(
        functools.partial(_dattn_kernel, dil=dilation),
        grid=(t // seq, tiles),
        in_specs=[pl.BlockSpec((ATTN_TILE, qw), cur(cq)),
                  pl.BlockSpec(blk, cur(ck)), pl.BlockSpec(blk, prev(ck)),
                  pl.BlockSpec(blk, cur(cv)), pl.BlockSpec(blk, prev(cv)),
                  pl.BlockSpec((HEADS_PER_GROUP, ATTN_BLOCK, 2 * ATTN_BLOCK), lambda b, j: (0, 0, 0))],
        out_specs=[pl.BlockSpec(blk, cur(0)), pl.BlockSpec(blk, cur(0))],
        out_shape=[jax.ShapeDtypeStruct((t, ATTN_GW), BF16), jax.ShapeDtypeStruct((t, ATTN_GW), F32)],
        scratch_shapes=[pltpu.VMEM((xrows, ATTN_GW), BF16), pltpu.VMEM((xrows, ATTN_GW), BF16),
                        pltpu.VMEM((ATTN_GW // ATTN_SLAB, ATTN_TILE, ATTN_SLAB), F32),
                        pltpu.VMEM((ATTN_GW // ATTN_SLAB, ATTN_TILE, ATTN_SLAB), F32)],
        compiler_params=_cparams("parallel", "arbitrary"),
        name=f"dattn{group}",
    )(aproj, aproj, aproj, aproj, aproj, bias)


def _rel_bucket(n):
    max_exact = REL_BUCKETS // 2
    nf = jnp.maximum(n, 1).astype(F32)
    log_b = max_exact + (jnp.log(nf / max_exact) / math.log(REL_MAX_DIST / max_exact)
                         * (REL_BUCKETS - max_exact)).astype(jnp.int32)
    return jnp.where(n < max_exact, n, jnp.minimum(log_b, REL_BUCKETS - 1))


def _attn_bias(rel_bias, group):
    window, dilation = ATTN_PATTERNS[group]
    steps = window // dilation
    hp = lax.Precision.HIGHEST
    hs = slice(group * HEADS_PER_GROUP, (group + 1) * HEADS_PER_GROUP)
    bucket = _rel_bucket(jnp.arange(steps + 1) * dilation)
    bias_steps = jnp.dot(jax.nn.one_hot(bucket, REL_BUCKETS, dtype=F32), rel_bias[:, hs].astype(F32),
                         precision=hp)
    qi = jnp.arange(ATTN_BLOCK)[:, None]
    ki = jnp.arange(2 * ATTN_BLOCK)[None, :]
    dist = ATTN_BLOCK + qi - ki
    ok = (dist >= 0) & (dist <= steps)
    sel = jax.nn.one_hot(jnp.clip(dist, 0, steps).reshape(-1), steps + 1, dtype=F32)
    bias = jnp.dot(sel, bias_steps, precision=hp).T.reshape(HEADS_PER_GROUP, ATTN_BLOCK, 2 * ATTN_BLOCK)
    return jnp.where(ok[None], bias, NEG)


def _merge_kernel(ya_ref, yb0_ref, yb1_ref, yb2_ref, l0_ref, l1_ref, l2_ref, gu_ref, gv_ref, gate_ref,
                  x_ref, wa_ref, wb_ref, wc_ref, wo_ref, ws_ref, bs_ref, gg_ref, o_ref, yc_scr, *, tm):
    d = x_ref.shape[1]
    l0, l1, l2 = l0_ref[...], l1_ref[...], l2_ref[...]
    mx = jnp.maximum(jnp.maximum(l0, l1), l2)
    e0, e1, e2 = jnp.exp(l0 - mx), jnp.exp(l1 - mx), jnp.exp(l2 - mx)
    inv = 1.0 / (e0 + e1 + e2)
    yb = jnp.concatenate([(yb0_ref[...].astype(F32) * (e0 * inv)).astype(BF16),
                          (yb1_ref[...].astype(F32) * (e1 * inv)).astype(BF16),
                          (yb2_ref[...].astype(F32) * (e2 * inv)).astype(BF16)], axis=-1)

    for j in range(tm // GMLP_CHUNK):
        rows = slice(j * GMLP_CHUNK, (j + 1) * GMLP_CHUNK)
        for g in range(GMLP_GROUPS):
            cols = slice(g * GMLP_GC, (g + 1) * GMLP_GC)
            u = jax.nn.gelu(gu_ref[rows, cols].astype(F32))
            v = _rms(jax.nn.gelu(gv_ref[rows, cols].astype(F32)), gg_ref[:, cols])
            mixed = _dot(ws_ref[g], v.astype(BF16)) + bs_ref[g]
            yc_scr[rows, cols] = (u * mixed).astype(BF16)

    def gate2(k):
        return jnp.tanh(0.5 * gate_ref[:, k * d:(k + 1) * d].astype(F32)) + 1.0

    merged2 = gate2(0) * _dot(ya_ref[...], wa_ref[...])
    merged2 = merged2 + gate2(1) * _dot(yb, wb_ref[...])
    merged2 = merged2 + gate2(2) * _dot(yc_scr[...], wc_ref[...])
    o_ref[...] = x_ref[...] + 0.5 * _dot(merged2.astype(BF16), wo_ref[...])


def _merge(ya, ybs, lses, proj, x2d, wa, wb, wc, wo, ws, bsb, gg, *, tm):
    t, d = x2d.shape
    row = lambda c: (lambda i: (i, c))
    full2 = lambda i: (0, 0)
    full3 = lambda i: (0, 0, 0)
    gspec = pl.BlockSpec((tm, ATTN_GW), row(0))
    return pl.pallas_call(
        functools.partial(_merge_kernel, tm=tm),
        grid=(t // tm,),
        in_specs=[pl.BlockSpec((tm, MLSTM_W), row(0)),
                  gspec, gspec, gspec, gspec, gspec, gspec,
                  pl.BlockSpec((tm, GMLP_W), row(OFF_GU // GMLP_W)),
                  pl.BlockSpec((tm, GMLP_W), row(OFF_GV // GMLP_W)),
                  pl.BlockSpec((tm, N_BRANCH * d), row(OFF_GATE // (N_BRANCH * d))),
                  pl.BlockSpec((tm, d), row(0)),
                  pl.BlockSpec(wa.shape, full2), pl.BlockSpec(wb.shape, full2),
                  pl.BlockSpec(wc.shape, full2), pl.BlockSpec(wo.shape, full2),
                  pl.BlockSpec(ws.shape, full3), pl.BlockSpec(bsb.shape, full3),
                  pl.BlockSpec(gg.shape, full2)],
        out_specs=pl.BlockSpec((tm, d), row(0)),
        out_shape=jax.ShapeDtypeStruct((t, d), F32),
        scratch_shapes=[pltpu.VMEM((tm, GMLP_W), BF16)],
        compiler_params=_cparams("parallel"),
        name="merge",
    )(ya, *ybs, *lses, proj, proj, proj, x2d, wa, wb, wc, wo, ws, bsb, gg)


def _memkv_kernel(mem_ref, g_ref, w_ref, gk_ref, k_ref, v_ref):
    dh, w = XATTN_DH, XATTN_W
    kv = _dot(_rms(mem_ref[0], g_ref[...]).astype(BF16), w_ref[...])
    for h in range(XATTN_HEADS):
        sl = slice(h * dh, (h + 1) * dh)
        k_ref[0, :, sl] = _rms(kv[:, sl], gk_ref[...]).astype(k_ref.dtype)
    v_ref[0] = kv[:, w:].astype(v_ref.dtype)


def _memkv(mem, gain, w_kv, gk):
    b, m, d = mem.shape
    full2 = lambda i: (0, 0)
    return pl.pallas_call(
        _memkv_kernel,
        grid=(b,),
        in_specs=[pl.BlockSpec((1, m, d), lambda i: (i, 0, 0)),
                  pl.BlockSpec((1, d), full2),
                  pl.BlockSpec(w_kv.shape, full2),
                  pl.BlockSpec((1, XATTN_DH), full2)],
        out_specs=[pl.BlockSpec((1, m, XATTN_W), lambda i: (i, 0, 0)),
                   pl.BlockSpec((1, m, XATTN_W), lambda i: (i, 0, 0))],
        out_shape=[jax.ShapeDtypeStruct((b, m, XATTN_W), BF16),
                   jax.ShapeDtypeStruct((b, m, XATTN_W), BF16)],
        compiler_params=_cparams("parallel", vmem=VMEM_LIMIT_SMALL),
        name="memkv",
    )(mem, gain, w_kv, gk)


def _route(logits):
    tm = logits.shape[1]
    e = jnp.exp(logits - jnp.max(logits, axis=0, keepdims=True))
    probs = e / jnp.sum(e, axis=0, keepdims=True)
    rowi = lax.broadcasted_iota(jnp.int32, (8, tm), 0)
    real = rowi < EXPERTS_PER_GROUP
    tops = []
    for g in range(N_EXPERT_GROUPS):
        pg = jnp.where(real, probs[8 * g:8 * g + 8, :], -0.5)
        m1 = jnp.max(pg, axis=0, keepdims=True)
        i1 = jnp.min(jnp.where(pg == m1, rowi, 8), axis=0, keepdims=True)
        pg2 = jnp.where(rowi == i1, -1.0, pg)
        m2 = jnp.max(pg2, axis=0, keepdims=True)
        i2 = jnp.min(jnp.where(pg2 == m2, rowi, 8), axis=0, keepdims=True)
        tops.append((m1, i1, m2, i2))
    best = jnp.zeros((1, tm), jnp.int32)
    best_score = tops[0][0] + tops[0][2]
    for g in range(1, N_EXPERT_GROUPS):
        score = tops[g][0] + tops[g][2]
        better = score > best_score
        best = jnp.where(better, g, best)
        best_score = jnp.where(better, score, best_score)
    m1, i1, m2, i2 = tops[0]
    for g in range(1, N_EXPERT_GROUPS):
        m1, i1, m2, i2 = (jnp.where(best == g, new, old) for new, old in zip(tops[g], (m1, i1, m2, i2)))
    tot = m1 + m2
    base = best * EXPERTS_PER_GROUP
    return base + i1, base + i2, m1 / tot, m2 / tot


def _pack_bf16_pairs(x):
    n = x.shape[1] // 2
    hi = lax.bitcast_convert_type(x[:, :n].astype(BF16).astype(F32), jnp.uint32)
    lo = lax.bitcast_convert_type(x[:, n:].astype(BF16).astype(F32), jnp.uint32)
    return hi | (lo >> 16)


def _unpack_bf16_pairs(p):
    hi = lax.bitcast_convert_type(p & jnp.uint32(0xFFFF0000), F32)
    lo = lax.bitcast_convert_type(p << 16, F32)
    return hi, lo


def _store_row_chunks(ref, packed):
    for j in range(ROW_CHUNKS):
        ref[j] = packed[:, j * 128:(j + 1) * 128]


def _load_row_chunks(ref):
    return jnp.concatenate([ref[j] for j in range(ROW_CHUNKS)], axis=-1)


def _xattn_kernel(x_ref, k_ref, v_ref, gx_ref, wq_ref, gq_ref, wo_ref, gf_ref, rw_ref, rb_ref,
                  xo_ref, hf_ref, eidx_ref, wts_ref, *, sub):
    dh = XATTN_DH
    rw = rw_ref[...]
    rw_hi, rw_lo = _split_bf16(rw)
    for s in range(x_ref.shape[0] // sub):
        rows = slice(s * sub, (s + 1) * sub)
        x = x_ref[rows, :]
        q = _dot(_rms(x, gx_ref[...]).astype(BF16), wq_ref[...])
        outs = []
        for h in range(XATTN_HEADS):
            sl = slice(h * dh, (h + 1) * dh)
            q_h = (_rms(q[:, sl], gq_ref[...]) * (dh ** -0.5)).astype(BF16)
            logits = _dot_nt(q_h, k_ref[0, :, sl])
            p = jnp.exp(logits - jnp.max(logits, axis=-1, keepdims=True))
            o = _dot(p.astype(BF16), v_ref[0, :, sl]) / jnp.sum(p, axis=-1, keepdims=True)
            outs.append(o.astype(BF16))
        xn = x + _dot(jnp.concatenate(outs, axis=-1), wo_ref[...])
        xo_ref[rows, :] = xn
        hf = _rms(xn, gf_ref[...])
        packed = _pack_bf16_pairs(hf)
        for j in range(ROW_CHUNKS):
            hf_ref[j, rows, :] = packed[:, j * 128:(j + 1) * 128]
        hf_hi, hf_lo = _split_bf16(hf)
        logits_t = _dot_nt(rw_hi, hf_hi) + _dot_nt(rw_hi, hf_lo) + _dot_nt(rw_lo, hf_hi) + rb_ref[...]
        e1, e2, w1, w2 = _route(logits_t)
        eidx_ref[:, rows] = jnp.concatenate([e1, e2, jnp.zeros((6, sub), jnp.int32)], axis=0)
        wts_ref[:, rows] = jnp.concatenate([w1, w2, jnp.zeros((6, sub), F32)], axis=0)


def _xattn(x2d, k, v, gx, wq, gq, wo, gf, rw_t, rb, *, seq, tm):
    t, d = x2d.shape
    per_b = seq // tm
    full2 = lambda i: (0, 0)
    kv_spec = pl.BlockSpec((1,) + k.shape[1:], lambda i: (i // per_b, 0, 0))
    return pl.pallas_call(
        functools.partial(_xattn_kernel, sub=min(tm, XATTN_SUB)),
        grid=(t // tm,),
        in_specs=[pl.BlockSpec((tm, d), lambda i: (i, 0)), kv_spec, kv_spec,
                  pl.BlockSpec((1, d), full2), pl.BlockSpec(wq.shape, full2),
                  pl.BlockSpec((1, XATTN_DH), full2), pl.BlockSpec(wo.shape, full2),
                  pl.BlockSpec((1, d), full2), pl.BlockSpec(rw_t.shape, full2),
                  pl.BlockSpec(rb.shape, full2)],
        out_specs=[pl.BlockSpec((tm, d), lambda i: (i, 0)),
                   pl.BlockSpec((ROW_CHUNKS, tm, 128), lambda i: (0, i, 0)),
                   pl.BlockSpec((8, tm), lambda i: (0, i)),
                   pl.BlockSpec((8, tm), lambda i: (0, i))],
        out_shape=[jax.ShapeDtypeStruct((t, d), F32),
                   jax.ShapeDtypeStruct((ROW_CHUNKS, t, 128), jnp.uint32),
                   jax.ShapeDtypeStruct((8, t), jnp.int32),
                   jax.ShapeDtypeStruct((8, t), F32)],
        compiler_params=_cparams("parallel"),
        name="xattn_router",
    )(x2d, k, v, gx, wq, gq, wo, gf, rw_t, rb)


def _moe_plan_kernel(eidx_ref, i1_ref, i2_ref, te_ref, na_ref, cnt_scr, carry_scr, *, tb, tm, plane_rows):
    ne = N_EXPERTS
    hp = lax.Precision.HIGHEST
    phase, j = pl.program_id(0), pl.program_id(1)
    rows = lax.broadcasted_iota(jnp.int32, (ne, tb), 0)
    oh1 = rows == eidx_ref[0:1, :]
    oh2 = rows == eidx_ref[1:2, :]
    a = oh1.astype(F32) + oh2.astype(F32)
    blk_cnt = jnp.broadcast_to(jnp.sum(a, axis=1, keepdims=True), cnt_scr.shape)

    @pl.when((phase == 0) & (j == 0))
    def _():
        cnt_scr[...] = jnp.zeros_like(cnt_scr)

    @pl.when(phase == 0)
    def _():
        cnt_scr[...] += blk_cnt

    @pl.when((phase == 1) & (j == 0))
    def _():
        padded = jnp.ceil(cnt_scr[...] * (1.0 / tm)) * tm
        er = lax.broadcasted_iota(jnp.int32, (ne, ne), 0)
        ec = lax.broadcasted_iota(jnp.int32, (ne, ne), 1)
        off = jnp.dot((ec < er).astype(F32), padded, precision=hp, preferred_element_type=F32)
        carry_scr[...] = off
        seg_end = (off + padded)[:, 0:1]
        tile_start = lax.broadcasted_iota(jnp.int32, (ne, te_ref.shape[1]), 1).astype(F32) * tm
        te = jnp.sum((seg_end <= tile_start).astype(F32), axis=0, keepdims=True)
        te_ref[...] = jnp.broadcast_to(jnp.minimum(te, ne - 1.0), te_ref.shape).astype(jnp.int32)
        total = jnp.sum(padded[:, 0:1], axis=0, keepdims=True)
        na_ref[...] = jnp.broadcast_to(total * (1.0 / tm), na_ref.shape).astype(jnp.int32)

    @pl.when(phase == 1)
    def _():
        before = (lax.broadcasted_iota(jnp.int32, (tb, tb), 0)
                  < lax.broadcasted_iota(jnp.int32, (tb, tb), 1)).astype(BF16)
        rank = carry_scr[:, 0:1] + _dot(a.astype(BF16), before)
        d1 = jnp.sum(jnp.where(oh1, rank, 0.0), axis=0, keepdims=True).astype(jnp.int32)
        d2 = jnp.sum(jnp.where(oh2, rank, 0.0), axis=0, keepdims=True).astype(jnp.int32)
        plane = lax.broadcasted_iota(jnp.int32, (8, tb), 0) * plane_rows
        i1_ref[...] = jnp.where(plane < ROW_CHUNKS * plane_rows, plane + d1, 0)
        i2_ref[...] = jnp.where(plane < ROW_CHUNKS * plane_rows, plane + d2, 0)
        carry_scr[...] += blk_cnt


def _moe_plan(eidx, *, tm, n_tiles, tb=512):
    t = eidx.shape[1]
    ntp = -(-n_tiles // 128) * 128
    return pl.pallas_call(
        functools.partial(_moe_plan_kernel, tb=tb, tm=tm, plane_rows=n_tiles * tm),
        grid=(2, t // tb),
        in_specs=[pl.BlockSpec((8, tb), lambda p, j: (0, j))],
        out_specs=[pl.BlockSpec((8, tb), lambda p, j: (0, j * p)),
                   pl.BlockSpec((8, tb), lambda p, j: (0, j * p)),
                   pl.BlockSpec((8, ntp), lambda p, j: (0, 0)),
                   pl.BlockSpec((8, 128), lambda p, j: (0, 0))],
        out_shape=[jax.ShapeDtypeStruct((8, t), jnp.int32),
                   jax.ShapeDtypeStruct((8, t), jnp.int32),
                   jax.ShapeDtypeStruct((8, ntp), jnp.int32),
                   jax.ShapeDtypeStruct((8, 128), jnp.int32)],
        scratch_shapes=[pltpu.VMEM((N_EXPERTS, 128), F32), pltpu.VMEM((N_EXPERTS, 128), F32)],
        compiler_params=_cparams("arbitrary", "arbitrary", vmem=VMEM_LIMIT_SMALL),
        name="moe_plan",
    )(eidx)


def _sc_mesh():
    return plsc.VectorSubcoreMesh(core_axis_name="c", subcore_axis_name="s",
                                  num_cores=SC_CORES, num_subcores=SC_SUBCORES)


def _sc_index_spec(tokens, first=0):
    nb = tokens // SC_WINDOW
    return pl.BlockSpec((1, SC_WINDOW), lambda i: (i // nb, first // SC_WINDOW + i % nb))


def _sc_dispatch(rows, i1, i2, n_out):
    n = rows.shape[0]
    tokens = i1.shape[1]

    @functools.partial(pl.kernel, out_type=jax.ShapeDtypeStruct((n_out, 128), rows.dtype), mesh=_sc_mesh(),
                       name="moe_dispatch")
    def k(x_hbm, i1_hbm, i2_hbm, o_hbm):
        def body(x_vmem, i1_vmem, i2_vmem):
            pltpu.sync_copy(x_vmem, o_hbm.at[i1_vmem.at[0]])
            pltpu.sync_copy(x_vmem, o_hbm.at[i2_vmem.at[0]])

        pltpu.emit_pipeline(
            body, grid=(n // SC_WINDOW,),
            in_specs=[pl.BlockSpec((SC_WINDOW, 128), lambda i: (i, 0)),
                      _sc_index_spec(tokens), _sc_index_spec(tokens)],
            out_specs=[],
            core_axis_name=("c", "s"), dimension_semantics=(pltpu.PARALLEL,),
        )(x_hbm, i1_hbm, i2_hbm)

    return k(rows, i1, i2)


def _sc_collect(table, i1, i2, *, first, tokens):
    n = ROW_CHUNKS * tokens
    out = jax.ShapeDtypeStruct((n, 128), table.dtype)

    @functools.partial(pl.kernel, out_type=(out, out), mesh=_sc_mesh(), name="moe_collect")
    def k(t_hbm, i1_hbm, i2_hbm, o1_hbm, o2_hbm):
        def body(i1_vmem, i2_vmem, o1_vmem, o2_vmem):
            pltpu.sync_copy(t_hbm.at[i1_vmem.at[0]], o1_vmem)
            pltpu.sync_copy(t_hbm.at[i2_vmem.at[0]], o2_vmem)

        pltpu.emit_pipeline(
            body, grid=(n // SC_WINDOW,),
            in_specs=[_sc_index_spec(tokens, first), _sc_index_spec(tokens, first)],
            out_specs=[pl.BlockSpec((SC_WINDOW, 128), lambda i: (i, 0)),
                       pl.BlockSpec((SC_WINDOW, 128), lambda i: (i, 0))],
            core_axis_name=("c", "s"), dimension_semantics=(pltpu.PARALLEL,),
        )(i1_hbm, i2_hbm, o1_hbm, o2_hbm)

    return k(table, i1, i2)


def _experts_kernel(te_ref, na_ref, xs_ref, wg_ref, wu_ref, wd_ref, y_ref, wg_scr, wu_scr, wd_scr):
    i = pl.program_id(0)
    active = i < na_ref[0]

    @pl.when(active & ((i == 0) | (te_ref[i] != te_ref[jnp.maximum(i - 1, 0)])))
    def _():
        wg_scr[...] = wg_ref[0, 0].astype(BF16)
        wu_scr[...] = wu_ref[0, 0].astype(BF16)
        wd_scr[...] = wd_ref[0, 0].astype(BF16)

    @pl.when(active)
    def _():
        hi, lo = _unpack_bf16_pairs(_load_row_chunks(xs_ref))
        h = jnp.concatenate([hi, lo], axis=-1).astype(BF16)
        up = _dot(h, wg_scr[...])
        act = up * _sigmoid(up) * _dot(h, wu_scr[...])
        _store_row_chunks(y_ref, _pack_bf16_pairs(_dot(act.astype(BF16), wd_scr[...])))


def _experts(tile_expert, n_active, xs, wg, wu, wd, *, layer, tm):
    n_tiles = tile_expert.shape[0]
    _, _, d, dff = wg.shape
    rows = lambda i, te, na: (0, jnp.minimum(i, na[0] - 1), 0)
    expert = lambda i, te, na: (layer, te[i], 0, 0)
    return pl.pallas_call(
        _experts_kernel,
        grid_spec=pltpu.PrefetchScalarGridSpec(
            num_scalar_prefetch=2,
            grid=(n_tiles,),
            in_specs=[pl.BlockSpec((ROW_CHUNKS, tm, 128), rows),
                      pl.BlockSpec((1, 1, d, dff), expert),
                      pl.BlockSpec((1, 1, d, dff), expert),
                      pl.BlockSpec((1, 1, dff, d), expert)],
            out_specs=pl.BlockSpec((ROW_CHUNKS, tm, 128), rows),
            scratch_shapes=[pltpu.VMEM((d, dff), BF16), pltpu.VMEM((d, dff), BF16), pltpu.VMEM((dff, d), BF16)]),
        out_shape=jax.ShapeDtypeStruct(xs.shape, xs.dtype),
        compiler_params=_cparams("arbitrary"),
        name="moe_experts",
    )(tile_expert, n_active, xs, wg, wu, wd)


def _moe_combine_kernel(x_ref, y1_ref, y2_ref, w_ref, o_ref):
    half = x_ref.shape[1] // 2
    hi1, lo1 = _unpack_bf16_pairs(_load_row_chunks(y1_ref))
    hi2, lo2 = _unpack_bf16_pairs(_load_row_chunks(y2_ref))
    tm = x_ref.shape[0]
    w_cols = jnp.concatenate([w_ref[...], jnp.zeros((128 - w_ref.shape[0], tm), F32)], axis=0).T
    w1, w2 = w_cols[:, 0:1], w_cols[:, 1:2]
    o_ref[:, :half] = x_ref[:, :half] + w1 * hi1 + w2 * hi2
    o_ref[:, half:] = x_ref[:, half:] + w1 * lo1 + w2 * lo2


def _moe_combine(x2d, y1, y2, wts, *, tm, first, partial=None):
    t, d = x2d.shape
    tokens = y1.shape[1]
    blk0 = first // tm
    row = lambda i: (blk0 + i, 0)
    chunk_spec = pl.BlockSpec((ROW_CHUNKS, tm, 128), lambda i: (0, i, 0))
    in_specs = [pl.BlockSpec((tm, d), row), chunk_spec, chunk_spec,
                pl.BlockSpec((wts.shape[0], tm), lambda i: (0, blk0 + i))]
    operands = [x2d, y1, y2, wts]
    aliases = {}
    kernel_fn = _moe_combine_kernel
    if partial is not None:
        in_specs.append(pl.BlockSpec(memory_space=pl.ANY))
        operands.append(partial)
        aliases = {len(operands) - 1: 0}
        kernel_fn = lambda x, a, b, w, _partial, o: _moe_combine_kernel(x, a, b, w, o)
    return pl.pallas_call(
        kernel_fn,
        grid=(tokens // tm,),
        in_specs=in_specs,
        out_specs=pl.BlockSpec((tm, d), row),
        out_shape=jax.ShapeDtypeStruct((t, d), F32),
        input_output_aliases=aliases,
        compiler_params=_cparams("parallel", vmem=VMEM_LIMIT_SMALL),
        name="moe_combine",
    )(*operands)


def _moe(x2d, hf_rows, eidx, wts, wg, wu, wd, *, layer):
    t = x2d.shape[0]
    tm = MOE_TM
    n_tiles = 2 * t // tm + N_EXPERTS
    plane = n_tiles * tm
    i1, i2, te, na = _moe_plan(eidx, tm=tm, n_tiles=n_tiles)
    xs = _sc_dispatch(hf_rows.reshape(ROW_CHUNKS * t, 128), i1, i2, ROW_CHUNKS * plane)
    ys = _experts(te[0, :n_tiles], na[0, :1], xs.reshape(ROW_CHUNKS, plane, 128), wg, wu, wd,
                  layer=layer, tm=tm)
    table, half, out = ys.reshape(ROW_CHUNKS * plane, 128), t // 2, None
    for first in (0, half):
        y1, y2 = _sc_collect(table, i1, i2, first=first, tokens=half)
        out = _moe_combine(x2d, y1.reshape(ROW_CHUNKS, half, 128), y2.reshape(ROW_CHUNKS, half, 128), wts,
                           tm=COMBINE_TM, first=first, partial=out)
    return out


W_ROWS = 256


def _w_rows_kernel(start_ref, valid_ref, w_ref, o_ref):
    del start_ref
    row = lax.broadcasted_iota(jnp.int32, w_ref.shape[1:], 0)
    o_ref[0] = jnp.where(row < valid_ref[pl.program_id(1)], w_ref[0], 0.0).astype(o_ref.dtype)


def _w_rows(w_t, starts, valid):
    depth, _, d = w_t.shape
    nblk = len(starts)
    return pl.pallas_call(
        _w_rows_kernel,
        grid_spec=pltpu.PrefetchScalarGridSpec(
            num_scalar_prefetch=2,
            grid=(depth, nblk),
            in_specs=[pl.BlockSpec((pl.Element(1), pl.Element(W_ROWS), pl.Element(d)),
                                   lambda l, c, st, va: (l, pl.multiple_of(st[c], 8), 0))],
            out_specs=pl.BlockSpec((1, W_ROWS, d), lambda l, c, st, va: (l, c, 0))),
        out_shape=jax.ShapeDtypeStruct((depth, nblk * W_ROWS, d), BF16),
        compiler_params=_cparams("parallel", "arbitrary", vmem=VMEM_LIMIT_SMALL),
        name="w_in_rows",
    )(jnp.asarray(starts, jnp.int32), jnp.asarray(valid, jnp.int32), w_t)


def _w_in_layout(w_in):
    w_t = jnp.swapaxes(w_in, 1, 2)
    src_if = 4 * MLSTM_W
    src_a = src_if + 2 * MLSTM_HEADS
    src_g = src_a + 3 * ATTN_W
    starts = list(range(0, src_if, W_ROWS)) + [src_g + k * W_ROWS for k in range((OFF_IF - OFF_GU) // W_ROWS)]
    valid = [W_ROWS] * len(starts)
    starts.append(src_if)
    valid.append(2 * MLSTM_HEADS)
    assert len(starts) * W_ROWS == N_PROJ and ATTN_GW == W_ROWS
    a_starts = [src_a + j * ATTN_W + g * ATTN_GW for g in range(len(ATTN_PATTERNS)) for j in range(3)]
    return _w_rows(w_t, starts, valid), _w_rows(w_t, a_starts, [W_ROWS] * len(a_starts))


def kernel(x, mem, norm_mix, w_in, mlstm_conv, mlstm_gate_b, mlstm_norm, attn_qk_norm, gmlp_norm, gmlp_ws,
           gmlp_bs, w_branch_a, w_branch_b, w_branch_c, w_out, rel_bias, norm_xattn, norm_mem, w_xq, w_xkv,
           xattn_qk_norm, w_xo, norm_ffn, router_w, router_b, w_expert_gate, w_expert_up, w_expert_down):
    b, s, d = x.shape
    t = b * s
    depth = w_in.shape[0]
    x2d = x.reshape(t, d)

    biases = [_attn_bias(rel_bias, g) for g in range(len(ATTN_PATTERNS))]
    rw_t = jnp.zeros((N_EXPERT_GROUPS, 8, d), F32).at[:, :EXPERTS_PER_GROUP].set(
        router_w.T.reshape(N_EXPERT_GROUPS, EXPERTS_PER_GROUP, d)).reshape(ROUTER_ROWS, d)
    rb = jnp.full((N_EXPERT_GROUPS, 8), NEG, F32).at[:, :EXPERTS_PER_GROUP].set(
        router_b.astype(F32).reshape(N_EXPERT_GROUPS, EXPERTS_PER_GROUP)).reshape(ROUTER_ROWS, 1)
    tril = jnp.tril(jnp.ones((GMLP_CHUNK, GMLP_CHUNK), bool))
    head_of = jnp.arange(ATTN_GW) // ATTN_DH
    seg_ones = (head_of[:, None] == head_of[None, :]).astype(BF16)

    w_main, w_attn = _w_in_layout(w_in)

    for l in range(depth):
        proj, h_mix, gates_t = _inproj(x2d, norm_mix[l][None], w_main, layer=l, tm=INPROJ_TM,
                                       tn=INPROJ_TN)
        gq = jnp.tile(attn_qk_norm[l, 0], HEADS_PER_GROUP)[None]
        gk = jnp.tile(attn_qk_norm[l, 1], HEADS_PER_GROUP)[None]

        nh = MLSTM_HEADS
        bias_i = jnp.zeros((8, 1), F32).at[:nh, 0].set(mlstm_gate_b[l, :nh])
        bias_f = jnp.zeros((8, 1), F32).at[:nh, 0].set(mlstm_gate_b[l, nh:])
        ya = _mlstm_rows(proj, gates_t, mlstm_conv[l], bias_i, bias_f, mlstm_norm[l][None],
                         batch=b, seq=s, blk=MLSTM_BLOCK, group=MLSTM_GROUP)

        ybs, lses = [], []
        for g, (_, dilation) in enumerate(ATTN_PATTERNS):
            aproj = _attnproj(h_mix, w_attn, seg_ones, gq, gk, layer=l, group=g, dilation=dilation)
            o, lse = _dattn(aproj, biases[g], seq=s, group=g, dilation=dilation)
            ybs.append(o)
            lses.append(lse)

        ws = jnp.where(tril, gmlp_ws[l], 0.0).astype(BF16)
        bsb = jnp.broadcast_to(gmlp_bs[l][:, :, None], (GMLP_GROUPS, GMLP_CHUNK, GMLP_GC)).astype(F32)
        x2d = _merge(ya, ybs, lses, proj, x2d, w_branch_a[l].astype(BF16), w_branch_b[l].astype(BF16),
                     w_branch_c[l].astype(BF16), w_out[l].astype(BF16), ws, bsb, gmlp_norm[l][None],
                     tm=MERGE_TM)

        k_mem, v_mem = _memkv(mem, norm_mem[l][None], w_xkv[l].astype(BF16), xattn_qk_norm[l, 1][None])
        x2d, hf_rows, eidx, wts = _xattn(x2d, k_mem, v_mem, norm_xattn[l][None], w_xq[l].astype(BF16),
                                         xattn_qk_norm[l, 0][None], w_xo[l].astype(BF16), norm_ffn[l][None],
                                         rw_t, rb, seq=s, tm=XATTN_TM)

        x2d = _moe(x2d, hf_rows, eidx, wts, w_expert_gate, w_expert_up, w_expert_down, layer=l)

    return x2d.reshape(b, s, d)
```

```python
import functools
import math

import jax
import jax.numpy as jnp
import numpy as np
from jax import lax
from jax.experimental import pallas as pl
from jax.experimental.pallas import tpu as pltpu
from jax.experimental.pallas import tpu_sc as plsc

F32 = jnp.float32
BF16 = jnp.bfloat16

EPS = 1e-6
NEG = -1e30

MLSTM_HEADS = 4
MLSTM_DH = 128
MLSTM_W = MLSTM_HEADS * MLSTM_DH
CONV_WIDTH = 4
MLSTM_BLOCK = 128
MLSTM_GROUP = 4

ATTN_PATTERNS = ((128, 1), (512, 4), (2048, 16))
HEADS_PER_GROUP = 4
ATTN_DH = 64
ATTN_GW = HEADS_PER_GROUP * ATTN_DH
ATTN_W = len(ATTN_PATTERNS) * ATTN_GW
ATTN_BLOCK = 128
REL_BUCKETS = 32
REL_MAX_DIST = 2048

GMLP_GROUPS = 4
GMLP_GC = 128
GMLP_W = GMLP_GROUPS * GMLP_GC
GMLP_CHUNK = 128

XATTN_HEADS = 4
XATTN_DH = 128
XATTN_W = XATTN_HEADS * XATTN_DH
XATTN_SUB = 1024

N_EXPERTS = 16
N_EXPERT_GROUPS = 4
EXPERTS_PER_GROUP = 4
ROUTER_ROWS = 8 * N_EXPERT_GROUPS

N_BRANCH = 3

MOE_TM = 1024
ROW_CHUNKS = 4
SC_CORES, SC_SUBCORES = 2, 16
SC_WINDOW = 128

OFF_MQ, OFF_MK, OFF_MV, OFF_MO = 0, 512, 1024, 1536
OFF_GU, OFF_GV = 2048, 2560
OFF_GATE = 3072
OFF_IF = 6144
IF_PAD = 256
N_PROJ = OFF_IF + IF_PAD

ATTN_TILE = 2048
ATTN_SUB = ATTN_TILE // ATTN_BLOCK
ATTN_SLAB = 2 * ATTN_DH
ATTN_COLS = HEADS_PER_GROUP * ATTN_SLAB + 2 * ATTN_GW

VMEM_LIMIT = 48 * 1024 * 1024
VMEM_LIMIT_INPROJ = 56 * 1024 * 1024
VMEM_LIMIT_SMALL = 24 * 1024 * 1024

INPROJ_TM, INPROJ_TN = 1024, 3072
ATTNPROJ_SUB = 512
MERGE_TM = 512
XATTN_TM = 1024
COMBINE_TM = 512


def _cparams(*sem, vmem=VMEM_LIMIT):
    return pltpu.CompilerParams(dimension_semantics=sem, vmem_limit_bytes=vmem)


def _rms(x, gain):
    return x * lax.rsqrt(jnp.mean(x * x, axis=-1, keepdims=True) + EPS) * gain


def _sigmoid(x):
    return 0.5 * jnp.tanh(0.5 * x) + 0.5


def _dot(a, b):
    return jnp.dot(a, b, preferred_element_type=F32)


def _dot_nt(a, b):
    return lax.dot_general(a, b, (((1,), (1,)), ((), ())), preferred_element_type=F32)


def _inproj_kernel(x_ref, g_ref, w_ref, wg_ref, o_ref, h_ref, gt_ref):
    @pl.when(pl.program_id(1) == 0)
    def _():
        h = _rms(x_ref[...], g_ref[...]).astype(BF16)
        h_ref[...] = h
        gt_ref[...] = _dot_nt(wg_ref[0, 0:128, :], h)[:gt_ref.shape[0], :]

    o_ref[...] = _dot_nt(h_ref[...], w_ref[0]).astype(o_ref.dtype)


def _inproj(x2d, gain, w, *, layer, tm, tn):
    t, d = x2d.shape
    n = OFF_IF
    return pl.pallas_call(
        _inproj_kernel,
        grid=(t // tm, n // tn),
        in_specs=[pl.BlockSpec((tm, d), lambda i, j: (i, 0)),
                  pl.BlockSpec((1, d), lambda i, j: (0, 0)),
                  pl.BlockSpec((1, tn, d), lambda i, j: (layer, j, 0)),
                  pl.BlockSpec((1, IF_PAD, d), lambda i, j: (layer, OFF_IF // IF_PAD, 0))],
        out_specs=[pl.BlockSpec((tm, tn), lambda i, j: (i, j)),
                   pl.BlockSpec((tm, d), lambda i, j: (i, 0)),
                   pl.BlockSpec((8, tm), lambda i, j: (0, i))],
        out_shape=[jax.ShapeDtypeStruct((t, n), BF16), jax.ShapeDtypeStruct((t, d), BF16),
                   jax.ShapeDtypeStruct((8, t), F32)],
        compiler_params=_cparams("parallel", "arbitrary", vmem=VMEM_LIMIT_INPROJ),
        name="inproj",
    )(x2d, gain, w, w)


def _log_sigmoid(x):
    return jnp.minimum(x, 0.0) - jnp.log(1.0 + jnp.exp(-jnp.abs(x)))


def _split_bf16(x):
    hi = x.astype(BF16)
    return hi, (x - hi.astype(F32)).astype(BF16)


def _prefix_max(x):
    n = x.shape[1]
    lane = lax.broadcasted_iota(jnp.int32, x.shape, 1)
    shift = 1
    while shift < n:
        x = jnp.maximum(x, jnp.where(lane >= shift, pltpu.roll(x, shift, 1), NEG))
        shift *= 2
    return x


def _mlstm_rows_kernel(qk_ref, v_ref, og_ref, *rest, blk, group):
    gate_refs = rest[:group]
    cw_ref, bi_ref, bf_ref, ng_ref, y_ref, xe_scr, s_scr, m_scr = rest[group:]
    heads, dh, w = MLSTM_HEADS, MLSTM_DH, MLSTM_W

    @pl.when(pl.program_id(1) == 0)
    def _():
        xe_scr[:, 0:8, :] = jnp.zeros((group, 8, 2 * w), F32)
        s_scr[...] = jnp.zeros_like(s_scr)
        m_scr[...] = jnp.zeros_like(m_scr)

    cw = cw_ref[...]
    causal = lax.broadcasted_iota(jnp.int32, (blk, blk), 0) >= lax.broadcasted_iota(jnp.int32, (blk, blk), 1)
    triu = (lax.broadcasted_iota(jnp.int32, (blk, blk), 0)
            <= lax.broadcasted_iota(jnp.int32, (blk, blk), 1)).astype(BF16)
    ones = jnp.ones((blk, dh), BF16)
    s_in = [[s_scr[g, h] for h in range(heads)] for g in range(group)]
    m_in = [m_scr[g, :, 0:1] for g in range(group)]
    s_out = [[None] * heads for _ in range(group)]
    m_out = [None] * group
    per_seq = []
    for g in range(group):
        xe_scr[g, 8:8 + blk, :] = qk_ref[g].astype(F32)
        conv = cw[CONV_WIDTH - 1:CONV_WIDTH, :] * xe_scr[g, 8:8 + blk, :]
        for j in range(CONV_WIDTH - 1):
            off = 8 - (CONV_WIDTH - 1) + j
            conv = conv + cw[j:j + 1, :] * xe_scr[g, off:off + blk, :]
        xe_scr[g, 0:8, :] = xe_scr[g, blk:blk + 8, :]
        qk = conv * _sigmoid(conv)

        gates = gate_refs[g][...]
        i_r = gates + bi_ref[...]
        lf_hi, lf_lo = _split_bf16(_log_sigmoid(pltpu.roll(gates, heads, 0) + bf_ref[...]))
        b_r = _dot(lf_hi, triu) + _dot(lf_lo, triu)
        m_st = m_in[g]
        a_r = i_r - b_r
        inter = b_r + m_st
        m_t = jnp.maximum(inter, b_r + _prefix_max(a_r))
        b_last = b_r[:, blk - 1:blk]
        dec = b_last - b_r + i_r
        m_new = jnp.maximum(b_last + m_st, jnp.max(dec, axis=1, keepdims=True))
        w_c = jnp.exp(b_last + m_st - m_new)
        m_out[g] = m_new
        pack = jnp.concatenate([b_r - m_t, jnp.exp(inter - m_t), jnp.exp(-m_t), jnp.exp(dec - m_new),
                                jnp.zeros((blk - 32, blk), F32)], axis=0)
        per_seq.append((qk, a_r, pack.T, w_c))

    chains = [(g, h) for h in range(heads) for g in range(group)]
    st = {}
    for g, h in chains:
        qk = per_seq[g][0]
        sl = slice(h * dh, (h + 1) * dh)
        q_b = qk[:, sl].astype(BF16)
        k_f = qk[:, w + h * dh:w + (h + 1) * dh] * (dh ** -0.5)
        v_ext = jnp.concatenate([v_ref[g, :, sl], ones], axis=-1)
        st[g, h] = (q_b, k_f, v_ext, _dot_nt(q_b, k_f.astype(BF16)), _dot(q_b, s_in[g][h].astype(BF16)))
    for g, h in chains:
        q_b, k_f, v_ext, qk_t, q_state = st[g, h]
        _, a_r, cols, _ = per_seq[g]
        u_c, w_inter = cols[:, h:h + 1], cols[:, 8 + h:9 + h]
        w_intra = jnp.exp(jnp.where(causal, u_c + a_r[h:h + 1, :], NEG))
        st[g, h] = (k_f, v_ext, _dot((qk_t * w_intra).astype(BF16), v_ext) + w_inter * q_state)
    for g, h in chains:
        k_f, v_ext, tot = st[g, h]
        _, _, cols, w_c = per_seq[g]
        em_c, w_k = cols[:, 16 + h:17 + h], cols[:, 24 + h:25 + h]
        sl = slice(h * dh, (h + 1) * dh)
        num, den = tot[:, :dh], tot[:, dh:]
        hh = num / jnp.maximum(jnp.abs(den), em_c)
        hn = _rms(hh, ng_ref[:, sl])
        y_ref[g, :, sl] = (hn * _sigmoid(og_ref[g, :, sl].astype(F32))).astype(y_ref.dtype)
        s_out[g][h] = w_c[h:h + 1, :] * s_in[g][h] + _dot((k_f * w_k).T.astype(BF16), v_ext)
    for g in range(group):
        m_scr[g] = jnp.broadcast_to(m_out[g], m_scr.shape[1:])
        for h in range(heads):
            s_scr[g, h] = s_out[g][h]


def _mlstm_rows(proj, gates_t, conv_w, bias_i, bias_f, norm_g, *, batch, seq, blk, group):
    t, npj = proj.shape
    w = MLSTM_W
    proj3 = proj.reshape(batch, seq, npj)
    cols = lambda c: (lambda b, i: (b, i, c))
    const2 = lambda b, i: (0, 0)
    nblk = seq // blk
    gate_specs = [pl.BlockSpec((8, blk), functools.partial(lambda b, i, g: (0, (b * group + g) * nblk + i), g=g))
                  for g in range(group)]
    y = pl.pallas_call(
        functools.partial(_mlstm_rows_kernel, blk=blk, group=group),
        grid=(batch // group, seq // blk),
        in_specs=[pl.BlockSpec((group, blk, 2 * w), cols(OFF_MQ // (2 * w))),
                  pl.BlockSpec((group, blk, w), cols(OFF_MV // w)),
                  pl.BlockSpec((group, blk, w), cols(OFF_MO // w)),
                  *gate_specs,
                  pl.BlockSpec((CONV_WIDTH, 2 * w), const2),
                  pl.BlockSpec((8, 1), const2), pl.BlockSpec((8, 1), const2),
                  pl.BlockSpec((1, w), const2)],
        out_specs=pl.BlockSpec((group, blk, w), cols(0)),
        out_shape=jax.ShapeDtypeStruct((batch, seq, w), BF16),
        scratch_shapes=[pltpu.VMEM((group, blk + 8, 2 * w), F32),
                        pltpu.VMEM((group, MLSTM_HEADS, MLSTM_DH, 2 * MLSTM_DH), F32),
                        pltpu.VMEM((group, 8, 128), F32)],
        compiler_params=_cparams("parallel", "arbitrary", vmem=VMEM_LIMIT_SMALL),
        name="mlstm",
    )(proj3, proj3, proj3, *([gates_t] * group), conv_w, bias_i, bias_f, norm_g)
    return y.reshape(t, w)


def _attnproj_kernel(h_ref, w_ref, seg_ref, gq_ref, gk_ref, o_ref, r_scr, *, dil):
    gw, half = ATTN_GW, ATTN_SLAB // 2
    sub_rows = ATTNPROJ_SUB
    seg, sub_seg = ATTN_TILE // dil, sub_rows // dil

    def head_norm(x, gain):
        ss = _dot((x * x).astype(BF16), seg_ref[...])
        return x * lax.rsqrt(ss * (1.0 / ATTN_DH) + EPS) * gain

    low = lax.broadcasted_iota(jnp.int32, (1, ATTN_SLAB), 1) < half
    for s in range(ATTN_TILE // sub_rows):
        rows = slice(s * sub_rows, (s + 1) * sub_rows)
        res = _dot_nt(h_ref[rows, :], w_ref[0])
        q = head_norm(res[:, :gw], gq_ref[...]) * (ATTN_DH ** -0.5)
        k = head_norm(res[:, gw:2 * gw], gk_ref[...])
        slabs = []
        for pair in range(gw // ATTN_SLAB):
            qp = q[:, pair * ATTN_SLAB:(pair + 1) * ATTN_SLAB]
            slabs += [jnp.where(low, qp, 0.0), jnp.where(low, 0.0, qp)]
        slabs += [k[:, c * 128:(c + 1) * 128] for c in range(gw // 128)]
        slabs += [res[:, 2 * gw + c * 128:2 * gw + (c + 1) * 128] for c in range(gw // 128)]
        pitch = dil + 1 if dil % 16 == 0 else dil
        for c, slab in enumerate(slabs):
            if dil == 1:
                o_ref[rows, c * 128:(c + 1) * 128] = slab.astype(o_ref.dtype)
            elif pitch == dil:
                r_scr[s % 2, c, 0:sub_rows, :] = slab
            else:
                for i in range(sub_seg):
                    r_scr[s % 2, c, pitch * i:pitch * i + dil, :] = slab[dil * i:dil * (i + 1), :]
        if dil > 1:
            for r in range(dil):
                dst = slice(r * seg + s * sub_seg, r * seg + (s + 1) * sub_seg)
                for c in range(r_scr.shape[1]):
                    o_ref[dst, c * 128:(c + 1) * 128] = (
                        r_scr[s % 2, c, pl.ds(r, sub_seg, stride=pitch), :].astype(o_ref.dtype))


def _attnproj(h, w, seg_ones, gq, gk, *, layer, group, dilation):
    t, d = h.shape
    wcols = 3 * ATTN_GW
    const2 = lambda i: (0, 0)
    return pl.pallas_call(
        functools.partial(_attnproj_kernel, dil=dilation),
        grid=(t // ATTN_TILE,),
        in_specs=[pl.BlockSpec((ATTN_TILE, d), lambda i: (i, 0)),
                  pl.BlockSpec((1, wcols, d), lambda i: (layer, group, 0)),
                  pl.BlockSpec((ATTN_GW, ATTN_GW), const2),
                  pl.BlockSpec((1, ATTN_GW), const2), pl.BlockSpec((1, ATTN_GW), const2)],
        out_specs=pl.BlockSpec((ATTN_TILE, ATTN_COLS), lambda i: (i, 0)),
        out_shape=jax.ShapeDtypeStruct((t, ATTN_COLS), BF16),
        scratch_shapes=[pltpu.VMEM((2, ATTN_COLS // 128, ATTNPROJ_SUB + ATTNPROJ_SUB // 16, 128), F32)],
        compiler_params=_cparams("parallel"),
        name=f"attnproj{group}",
    )(h, w, seg_ones, gq, gk)


def _dattn_kernel(q_ref, kc_ref, kp_ref, vc_ref, vp_ref, bias_ref, o_ref, lse_ref,
                  kx_scr, vx_scr, o_scr, l_scr, *, dil):
    blk = ATTN_BLOCK
    per = ATTN_SUB // dil
    pitch = dil + 1 if dil % 16 == 0 else dil
    first_tile = pl.program_id(1) == 0
    for r in range(dil):
        base = r * (per + 1) * blk
        last = slice((r * per + per - 1) * blk, (r * per + per) * blk)
        mine = slice(r * per * blk, (r + 1) * per * blk)
        kx_scr[base:base + blk, :] = kp_ref[last, :]
        vx_scr[base:base + blk, :] = vp_ref[last, :]
        kx_scr[base + blk:base + (per + 1) * blk, :] = kc_ref[mine, :]
        vx_scr[base + blk:base + (per + 1) * blk, :] = vc_ref[mine, :]

    low = lax.broadcasted_iota(jnp.int32, (1, ATTN_SLAB), 1) < ATTN_SLAB // 2
    no_prev = lax.broadcasted_iota(jnp.int32, (1, 2 * blk), 1) < blk
    for r in range(dil):
        for sub in range(per):
            u = r * per + sub
            win = slice((r * (per + 1) + sub) * blk, (r * (per + 1) + sub + 2) * blk)
            o_slabs, l_slabs = [], []
            for pair in range(ATTN_GW // ATTN_SLAB):
                cols = slice(pair * ATTN_SLAB, (pair + 1) * ATTN_SLAB)
                kx, vx = kx_scr[win, cols], vx_scr[win, cols]
                o_pair, l_pair = [], []
                for h in (2 * pair, 2 * pair + 1):
                    logits = _dot_nt(q_ref[u * blk:(u + 1) * blk, h * ATTN_SLAB:(h + 1) * ATTN_SLAB], kx)
                    logits = logits + bias_ref[h]
                    if sub == 0:
                        logits = jnp.where(first_tile & no_prev, NEG, logits)
                    m = jnp.max(logits, axis=-1, keepdims=True)
                    p = jnp.exp(logits - m)
                    l = jnp.sum(p, axis=-1, keepdims=True)
                    o_pair.append(_dot(p.astype(BF16), vx) / l)
                    l_pair.append(m + jnp.log(l))
                o_slabs.append(jnp.where(low, o_pair[0], o_pair[1]))
                l_slabs.append(jnp.where(low, l_pair[0], l_pair[1]))
            for c in range(ATTN_GW // ATTN_SLAB):
                cols = slice(c * ATTN_SLAB, (c + 1) * ATTN_SLAB)
                if dil == 1:
                    o_ref[u * blk:(u + 1) * blk, cols] = o_slabs[c].astype(o_ref.dtype)
                    lse_ref[u * blk:(u + 1) * blk, cols] = l_slabs[c]
                else:
                    dst = pl.ds(sub * blk * pitch + r, blk, stride=pitch)
                    o_scr[c, dst, :] = o_slabs[c]
                    l_scr[c, dst, :] = l_slabs[c]
    if dil > 1:
        for c in range(ATTN_GW // ATTN_SLAB):
            cols = slice(c * ATTN_SLAB, (c + 1) * ATTN_SLAB)
            if pitch == dil:
                o_ref[:, cols] = o_scr[c, 0:ATTN_TILE, :].astype(o_ref.dtype)
                lse_ref[:, cols] = l_scr[c, 0:ATTN_TILE, :]
            else:
                for i in range(ATTN_TILE // dil):
                    o_ref[dil * i:dil * (i + 1), cols] = o_scr[c, pitch * i:pitch * i + dil, :].astype(o_ref.dtype)
                    lse_ref[dil * i:dil * (i + 1), cols] = l_scr[c, pitch * i:pitch * i + dil, :]


def _dattn(aproj, bias, *, seq, group, dilation):
    t = aproj.shape[0]
    tiles = seq // ATTN_TILE
    qw = HEADS_PER_GROUP * ATTN_SLAB
    cq, ck, cv = 0, qw // ATTN_GW, qw // ATTN_GW + 1
    blk = (ATTN_TILE, ATTN_GW)
    cur = lambda c: (lambda b, j: (b * tiles + j, c))
    prev = lambda c: (lambda b, j: (b * tiles + jnp.maximum(j - 1, 0), c))
    xrows = ATTN_TILE + dilation * ATTN_BLOCK
    return pl.pallas_call(
        functools.partial(_dattn_kernel, dil=dilation),
        grid=(t // seq, tiles),
        in_specs=[pl.BlockSpec((ATTN_TILE, qw), cur(cq)),
                  pl.BlockSpec(blk, cur(ck)), pl.BlockSpec(blk, prev(ck)),
                  pl.BlockSpec(blk, cur(cv)), pl.BlockSpec(blk, prev(cv)),
                  pl.BlockSpec((HEADS_PER_GROUP, ATTN_BLOCK, 2 * ATTN_BLOCK), lambda b, j: (0, 0, 0))],
        out_specs=[pl.BlockSpec(blk, cur(0)), pl.BlockSpec(blk, cur(0))],
        out_shape=[jax.ShapeDtypeStruct((t, ATTN_GW), BF16), jax.ShapeDtypeStruct((t, ATTN_GW), F32)],
        scratch_shapes=[pltpu.VMEM((xrows, ATTN_GW), BF16), pltpu.VMEM((xrows, ATTN_GW), BF16),
                        pltpu.VMEM((ATTN_GW // ATTN_SLAB, ATTN_TILE + ATTN_TILE // 16, ATTN_SLAB), F32),
                        pltpu.VMEM((ATTN_GW // ATTN_SLAB, ATTN_TILE + ATTN_TILE // 16, ATTN_SLAB), F32)],
        compiler_params=_cparams("parallel", "arbitrary"),
        name=f"dattn{group}",
    )(aproj, aproj, aproj, aproj, aproj, bias)


def _rel_bucket(n):
    max_exact = REL_BUCKETS // 2
    nf = jnp.maximum(n, 1).astype(F32)
    log_b = max_exact + (jnp.log(nf / max_exact) / math.log(REL_MAX_DIST / max_exact)
                         * (REL_BUCKETS - max_exact)).astype(jnp.int32)
    return jnp.where(n < max_exact, n, jnp.minimum(log_b, REL_BUCKETS - 1))


def _attn_bias(rel_bias, group):
    window, dilation = ATTN_PATTERNS[group]
    steps = window // dilation
    hp = lax.Precision.HIGHEST
    hs = slice(group * HEADS_PER_GROUP, (group + 1) * HEADS_PER_GROUP)
    bucket = _rel_bucket(jnp.arange(steps + 1) * dilation)
    bias_steps = jnp.dot(jax.nn.one_hot(bucket, REL_BUCKETS, dtype=F32), rel_bias[:, hs].astype(F32),
                         precision=hp)
    qi = jnp.arange(ATTN_BLOCK)[:, None]
    ki = jnp.arange(2 * ATTN_BLOCK)[None, :]
    dist = ATTN_BLOCK + qi - ki
    ok = (dist >= 0) & (dist <= steps)
    sel = jax.nn.one_hot(jnp.clip(dist, 0, steps).reshape(-1), steps + 1, dtype=F32)
    bias = jnp.dot(sel, bias_steps, precision=hp).T.reshape(HEADS_PER_GROUP, ATTN_BLOCK, 2 * ATTN_BLOCK)
    return jnp.where(ok[None], bias, NEG)


def _merge_kernel(ya_ref, yb0_ref, yb1_ref, yb2_ref, l0_ref, l1_ref, l2_ref, gu_ref, gv_ref, gate_ref,
                  x_ref, wa_ref, wb_ref, wc_ref, wo_ref, ws_ref, bs_ref, gg_ref, o_ref, yc_scr, *, tm):
    d = x_ref.shape[1]
    l0, l1, l2 = l0_ref[...], l1_ref[...], l2_ref[...]
    mx = jnp.maximum(jnp.maximum(l0, l1), l2)
    e0, e1, e2 = jnp.exp(l0 - mx), jnp.exp(l1 - mx), jnp.exp(l2 - mx)
    inv = 1.0 / (e0 + e1 + e2)
    yb = jnp.concatenate([(yb0_ref[...].astype(F32) * (e0 * inv)).astype(BF16),
                          (yb1_ref[...].astype(F32) * (e1 * inv)).astype(BF16),
                          (yb2_ref[...].astype(F32) * (e2 * inv)).astype(BF16)], axis=-1)

    for j in range(tm // GMLP_CHUNK):
        rows = slice(j * GMLP_CHUNK, (j + 1) * GMLP_CHUNK)
        for g in range(GMLP_GROUPS):
            cols = slice(g * GMLP_GC, (g + 1) * GMLP_GC)
            u = jax.nn.gelu(gu_ref[rows, cols].astype(F32))
            v = _rms(jax.nn.gelu(gv_ref[rows, cols].astype(F32)), gg_ref[:, cols])
            mixed = _dot(ws_ref[g], v.astype(BF16)) + bs_ref[g]
            yc_scr[rows, cols] = (u * mixed).astype(BF16)

    def gate2(k):
        return jnp.tanh(0.5 * gate_ref[:, k * d:(k + 1) * d].astype(F32)) + 1.0

    merged2 = gate2(0) * _dot(ya_ref[...], wa_ref[...])
    merged2 = merged2 + gate2(1) * _dot(yb, wb_ref[...])
    merged2 = merged2 + gate2(2) * _dot(yc_scr[...], wc_ref[...])
    o_ref[...] = x_ref[...] + 0.5 * _dot(merged2.astype(BF16), wo_ref[...])


def _merge(ya, ybs, lses, proj, x2d, wa, wb, wc, wo, ws, bsb, gg, *, tm):
    t, d = x2d.shape
    row = lambda c: (lambda i: (i, c))
    full2 = lambda i: (0, 0)
    full3 = lambda i: (0, 0, 0)
    gspec = pl.BlockSpec((tm, ATTN_GW), row(0))
    return pl.pallas_call(
        functools.partial(_merge_kernel, tm=tm),
        grid=(t // tm,),
        in_specs=[pl.BlockSpec((tm, MLSTM_W), row(0)),
                  gspec, gspec, gspec, gspec, gspec, gspec,
                  pl.BlockSpec((tm, GMLP_W), row(OFF_GU // GMLP_W)),
                  pl.BlockSpec((tm, GMLP_W), row(OFF_GV // GMLP_W)),
                  pl.BlockSpec((tm, N_BRANCH * d), row(OFF_GATE // (N_BRANCH * d))),
                  pl.BlockSpec((tm, d), row(0)),
                  pl.BlockSpec(wa.shape, full2), pl.BlockSpec(wb.shape, full2),
                  pl.BlockSpec(wc.shape, full2), pl.BlockSpec(wo.shape, full2),
                  pl.BlockSpec(ws.shape, full3), pl.BlockSpec(bsb.shape, full3),
                  pl.BlockSpec(gg.shape, full2)],
        out_specs=pl.BlockSpec((tm, d), row(0)),
        out_shape=jax.ShapeDtypeStruct((t, d), F32),
        scratch_shapes=[pltpu.VMEM((tm, GMLP_W), BF16)],
        compiler_params=_cparams("parallel"),
        name="merge",
    )(ya, *ybs, *lses, proj, proj, proj, x2d, wa, wb, wc, wo, ws, bsb, gg)


def _memkv_kernel(mem_ref, g_ref, w_ref, gk_ref, k_ref, v_ref):
    dh, w = XATTN_DH, XATTN_W
    kv = _dot(_rms(mem_ref[0], g_ref[...]).astype(BF16), w_ref[...])
    for h in range(XATTN_HEADS):
        sl = slice(h * dh, (h + 1) * dh)
        k_ref[0, :, sl] = _rms(kv[:, sl], gk_ref[...]).astype(k_ref.dtype)
    v_ref[0] = kv[:, w:].astype(v_ref.dtype)


def _memkv(mem, gain, w_kv, gk):
    b, m, d = mem.shape
    full2 = lambda i: (0, 0)
    return pl.pallas_call(
        _memkv_kernel,
        grid=(b,),
        in_specs=[pl.BlockSpec((1, m, d), lambda i: (i, 0, 0)),
                  pl.BlockSpec((1, d), full2),
                  pl.BlockSpec(w_kv.shape, full2),
                  pl.BlockSpec((1, XATTN_DH), full2)],
        out_specs=[pl.BlockSpec((1, m, XATTN_W), lambda i: (i, 0, 0)),
                   pl.BlockSpec((1, m, XATTN_W), lambda i: (i, 0, 0))],
        out_shape=[jax.ShapeDtypeStruct((b, m, XATTN_W), BF16),
                   jax.ShapeDtypeStruct((b, m, XATTN_W), BF16)],
        compiler_params=_cparams("parallel", vmem=VMEM_LIMIT_SMALL),
        name="memkv",
    )(mem, gain, w_kv, gk)


def _route(logits):
    tm = logits.shape[1]
    e = jnp.exp(logits - jnp.max(logits, axis=0, keepdims=True))
    probs = e / jnp.sum(e, axis=0, keepdims=True)
    rowi = lax.broadcasted_iota(jnp.int32, (8, tm), 0)
    real = rowi < EXPERTS_PER_GROUP
    tops = []
    for g in range(N_EXPERT_GROUPS):
        pg = jnp.where(real, probs[8 * g:8 * g + 8, :], -0.5)
        m1 = jnp.max(pg, axis=0, keepdims=True)
        i1 = jnp.min(jnp.where(pg == m1, rowi, 8), axis=0, keepdims=True)
        pg2 = jnp.where(rowi == i1, -1.0, pg)
        m2 = jnp.max(pg2, axis=0, keepdims=True)
        i2 = jnp.min(jnp.where(pg2 == m2, rowi, 8), axis=0, keepdims=True)
        tops.append((m1, i1, m2, i2))
    best = jnp.zeros((1, tm), jnp.int32)
    best_score = tops[0][0] + tops[0][2]
    for g in range(1, N_EXPERT_GROUPS):
        score = tops[g][0] + tops[g][2]
        better = score > best_score
        best = jnp.where(better, g, best)
        best_score = jnp.where(better, score, best_score)
    m1, i1, m2, i2 = tops[0]
    for g in range(1, N_EXPERT_GROUPS):
        m1, i1, m2, i2 = (jnp.where(best == g, new, old) for new, old in zip(tops[g], (m1, i1, m2, i2)))
    tot = m1 + m2
    base = best * EXPERTS_PER_GROUP
    return base + i1, base + i2, m1 / tot, m2 / tot


def _pack_bf16_pairs(x):
    n = x.shape[1] // 2
    hi = lax.bitcast_convert_type(x[:, :n].astype(BF16).astype(F32), jnp.uint32)
    lo = lax.bitcast_convert_type(x[:, n:].astype(BF16).astype(F32), jnp.uint32)
    return hi | (lo >> 16)


def _unpack_bf16_pairs(p):
    hi = lax.bitcast_convert_type(p & jnp.uint32(0xFFFF0000), F32)
    lo = lax.bitcast_convert_type(p << 16, F32)
    return hi, lo


def _store_row_chunks(ref, packed):
    for j in range(ROW_CHUNKS):
        ref[j] = packed[:, j * 128:(j + 1) * 128]


def _load_row_chunks(ref):
    return jnp.concatenate([ref[j] for j in range(ROW_CHUNKS)], axis=-1)


def _xattn_kernel(x_ref, k_ref, v_ref, gx_ref, wq_ref, gq_ref, wo_ref, gf_ref, rw_ref, rb_ref,
                  xo_ref, hf_ref, eidx_ref, wts_ref, *, sub):
    dh = XATTN_DH
    rw = rw_ref[...]
    rw_hi, rw_lo = _split_bf16(rw)
    for s in range(x_ref.shape[0] // sub):
        rows = slice(s * sub, (s + 1) * sub)
        x = x_ref[rows, :]
        q = _dot(_rms(x, gx_ref[...]).astype(BF16), wq_ref[...])
        outs = []
        for h in range(XATTN_HEADS):
            sl = slice(h * dh, (h + 1) * dh)
            q_h = (_rms(q[:, sl], gq_ref[...]) * (dh ** -0.5)).astype(BF16)
            logits = _dot_nt(q_h, k_ref[0, :, sl])
            p = jnp.exp(logits - jnp.max(logits, axis=-1, keepdims=True))
            o = _dot(p.astype(BF16), v_ref[0, :, sl]) / jnp.sum(p, axis=-1, keepdims=True)
            outs.append(o.astype(BF16))
        xn = x + _dot(jnp.concatenate(outs, axis=-1), wo_ref[...])
        xo_ref[rows, :] = xn
        hf = _rms(xn, gf_ref[...])
        packed = _pack_bf16_pairs(hf)
        for j in range(ROW_CHUNKS):
            hf_ref[j, rows, :] = packed[:, j * 128:(j + 1) * 128]
        hf_hi, hf_lo = _split_bf16(hf)
        logits_t = _dot_nt(rw_hi, hf_hi) + _dot_nt(rw_hi, hf_lo) + _dot_nt(rw_lo, hf_hi) + rb_ref[...]
        e1, e2, w1, w2 = _route(logits_t)
        eidx_ref[:, rows] = jnp.concatenate([e1, e2, jnp.zeros((6, sub), jnp.int32)], axis=0)
        wts_ref[:, rows] = jnp.concatenate([w1, w2, jnp.zeros((6, sub), F32)], axis=0)


def _xattn(x2d, k, v, gx, wq, gq, wo, gf, rw_t, rb, *, seq, tm):
    t, d = x2d.shape
    per_b = seq // tm
    full2 = lambda i: (0, 0)
    kv_spec = pl.BlockSpec((1,) + k.shape[1:], lambda i: (i // per_b, 0, 0))
    return pl.pallas_call(
        functools.partial(_xattn_kernel, sub=min(tm, XATTN_SUB)),
        grid=(t // tm,),
        in_specs=[pl.BlockSpec((tm, d), lambda i: (i, 0)), kv_spec, kv_spec,
                  pl.BlockSpec((1, d), full2), pl.BlockSpec(wq.shape, full2),
                  pl.BlockSpec((1, XATTN_DH), full2), pl.BlockSpec(wo.shape, full2),
                  pl.BlockSpec((1, d), full2), pl.BlockSpec(rw_t.shape, full2),
                  pl.BlockSpec(rb.shape, full2)],
        out_specs=[pl.BlockSpec((tm, d), lambda i: (i, 0)),
                   pl.BlockSpec((ROW_CHUNKS, tm, 128), lambda i: (0, i, 0)),
                   pl.BlockSpec((8, tm), lambda i: (0, i)),
                   pl.BlockSpec((8, tm), lambda i: (0, i))],
        out_shape=[jax.ShapeDtypeStruct((t, d), F32),
                   jax.ShapeDtypeStruct((ROW_CHUNKS, t, 128), jnp.uint32),
                   jax.ShapeDtypeStruct((8, t), jnp.int32),
                   jax.ShapeDtypeStruct((8, t), F32)],
        compiler_params=_cparams("parallel"),
        name="xattn_router",
    )(x2d, k, v, gx, wq, gq, wo, gf, rw_t, rb)


def _moe_plan_kernel(eidx_ref, i1_ref, i2_ref, te_ref, na_ref, cnt_scr, carry_scr, *, tb, tm, plane_rows):
    ne = N_EXPERTS
    hp = lax.Precision.HIGHEST
    phase, j = pl.program_id(0), pl.program_id(1)
    rows = lax.broadcasted_iota(jnp.int32, (ne, tb), 0)
    oh1 = rows == eidx_ref[0:1, :]
    oh2 = rows == eidx_ref[1:2, :]
    a = oh1.astype(F32) + oh2.astype(F32)
    blk_cnt = jnp.broadcast_to(jnp.sum(a, axis=1, keepdims=True), cnt_scr.shape)

    @pl.when((phase == 0) & (j == 0))
    def _():
        cnt_scr[...] = jnp.zeros_like(cnt_scr)

    @pl.when(phase == 0)
    def _():
        cnt_scr[...] += blk_cnt

    @pl.when((phase == 1) & (j == 0))
    def _():
        padded = jnp.ceil(cnt_scr[...] * (1.0 / tm)) * tm
        er = lax.broadcasted_iota(jnp.int32, (ne, ne), 0)
        ec = lax.broadcasted_iota(jnp.int32, (ne, ne), 1)
        off = jnp.dot((ec < er).astype(F32), padded, precision=hp, preferred_element_type=F32)
        carry_scr[...] = off
        seg_end = (off + padded)[:, 0:1]
        tile_start = lax.broadcasted_iota(jnp.int32, (ne, te_ref.shape[1]), 1).astype(F32) * tm
        te = jnp.sum((seg_end <= tile_start).astype(F32), axis=0, keepdims=True)
        te_ref[...] = jnp.broadcast_to(jnp.minimum(te, ne - 1.0), te_ref.shape).astype(jnp.int32)
        total = jnp.sum(padded[:, 0:1], axis=0, keepdims=True)
        na_ref[...] = jnp.broadcast_to(total * (1.0 / tm), na_ref.shape).astype(jnp.int32)

    @pl.when(phase == 1)
    def _():
        before = (lax.broadcasted_iota(jnp.int32, (tb, tb), 0)
                  < lax.broadcasted_iota(jnp.int32, (tb, tb), 1)).astype(BF16)
        rank = carry_scr[:, 0:1] + _dot(a.astype(BF16), before)
        d1 = jnp.sum(jnp.where(oh1, rank, 0.0), axis=0, keepdims=True).astype(jnp.int32)
        d2 = jnp.sum(jnp.where(oh2, rank, 0.0), axis=0, keepdims=True).astype(jnp.int32)
        plane = lax.broadcasted_iota(jnp.int32, (8, tb), 0) * plane_rows
        i1_ref[...] = jnp.where(plane < ROW_CHUNKS * plane_rows, plane + d1, 0)
        i2_ref[...] = jnp.where(plane < ROW_CHUNKS * plane_rows, plane + d2, 0)
        carry_scr[...] += blk_cnt


def _moe_plan(eidx, *, tm, n_tiles, tb=512):
    t = eidx.shape[1]
    ntp = -(-n_tiles // 128) * 128
    return pl.pallas_call(
        functools.partial(_moe_plan_kernel, tb=tb, tm=tm, plane_rows=n_tiles * tm),
        grid=(2, t // tb),
        in_specs=[pl.BlockSpec((8, tb), lambda p, j: (0, j))],
        out_specs=[pl.BlockSpec((8, tb), lambda p, j: (0, j * p)),
                   pl.BlockSpec((8, tb), lambda p, j: (0, j * p)),
                   pl.BlockSpec((8, ntp), lambda p, j: (0, 0)),
                   pl.BlockSpec((8, 128), lambda p, j: (0, 0))],
        out_shape=[jax.ShapeDtypeStruct((8, t), jnp.int32),
                   jax.ShapeDtypeStruct((8, t), jnp.int32),
                   jax.ShapeDtypeStruct((8, ntp), jnp.int32),
                   jax.ShapeDtypeStruct((8, 128), jnp.int32)],
        scratch_shapes=[pltpu.VMEM((N_EXPERTS, 128), F32), pltpu.VMEM((N_EXPERTS, 128), F32)],
        compiler_params=_cparams("arbitrary", "arbitrary", vmem=VMEM_LIMIT_SMALL),
        name="moe_plan",
    )(eidx)


def _sc_mesh():
    return plsc.VectorSubcoreMesh(core_axis_name="c", subcore_axis_name="s",
                                  num_cores=SC_CORES, num_subcores=SC_SUBCORES)


def _sc_index_spec(tokens):
    nb = tokens // SC_WINDOW
    return pl.BlockSpec((1, SC_WINDOW), lambda i: (i // nb, i % nb))


def _sc_dispatch(rows, i1, i2, n_out):
    n = rows.shape[0]
    tokens = i1.shape[1]

    @functools.partial(pl.kernel, out_type=jax.ShapeDtypeStruct((n_out, 128), rows.dtype), mesh=_sc_mesh(),
                       name="moe_dispatch")
    def k(x_hbm, i1_hbm, i2_hbm, o_hbm):
        def body(x_vmem, i1_vmem, i2_vmem):
            pltpu.sync_copy(x_vmem, o_hbm.at[i1_vmem.at[0]])
            pltpu.sync_copy(x_vmem, o_hbm.at[i2_vmem.at[0]])

        pltpu.emit_pipeline(
            body, grid=(n // SC_WINDOW,),
            in_specs=[pl.BlockSpec((SC_WINDOW, 128), lambda i: (i, 0)),
                      _sc_index_spec(tokens), _sc_index_spec(tokens)],
            out_specs=[],
            core_axis_name=("c", "s"), dimension_semantics=(pltpu.PARALLEL,),
        )(x_hbm, i1_hbm, i2_hbm)

    return k(rows, i1, i2)


def _sc_collect(table, i1, i2):
    tokens = i1.shape[1]
    n = ROW_CHUNKS * tokens
    out = jax.ShapeDtypeStruct((n, 128), table.dtype)

    @functools.partial(pl.kernel, out_type=(out, out), mesh=_sc_mesh(), name="moe_collect",
                       scratch_types=[pltpu.SemaphoreType.DMA, pltpu.SemaphoreType.DMA])
    def k(t_hbm, i1_hbm, i2_hbm, o1_hbm, o2_hbm, sem1, sem2):
        def body(i1_vmem, i2_vmem, o1_vmem, o2_vmem):
            first = pltpu.async_copy(t_hbm.at[i1_vmem.at[0]], o1_vmem, sem1)
            second = pltpu.async_copy(t_hbm.at[i2_vmem.at[0]], o2_vmem, sem2)
            first.wait()
            second.wait()

        pltpu.emit_pipeline(
            body, grid=(n // SC_WINDOW,),
            in_specs=[_sc_index_spec(tokens), _sc_index_spec(tokens)],
            out_specs=[pl.BlockSpec((SC_WINDOW, 128), lambda i: (i, 0)),
                       pl.BlockSpec((SC_WINDOW, 128), lambda i: (i, 0))],
            core_axis_name=("c", "s"), dimension_semantics=(pltpu.PARALLEL,),
        )(i1_hbm, i2_hbm, o1_hbm, o2_hbm)

    return k(table, i1, i2)


def _experts_kernel(te_ref, na_ref, xs_ref, wg_ref, wu_ref, wd_ref, y_ref, wg_scr, wu_scr, wd_scr):
    i = pl.program_id(0)
    active = i < na_ref[0]

    @pl.when(active & ((i == 0) | (te_ref[i] != te_ref[jnp.maximum(i - 1, 0)])))
    def _():
        wg_scr[...] = wg_ref[0, 0].astype(BF16)
        wu_scr[...] = wu_ref[0, 0].astype(BF16)
        wd_scr[...] = wd_ref[0, 0].astype(BF16)

    @pl.when(active)
    def _():
        hi, lo = _unpack_bf16_pairs(_load_row_chunks(xs_ref))
        h = jnp.concatenate([hi, lo], axis=-1).astype(BF16)
        up = _dot(h, wg_scr[...])
        act = up * _sigmoid(up) * _dot(h, wu_scr[...])
        _store_row_chunks(y_ref, _pack_bf16_pairs(_dot(act.astype(BF16), wd_scr[...])))


def _experts(tile_expert, n_active, xs, wg, wu, wd, *, layer, tm):
    n_tiles = tile_expert.shape[0]
    _, _, d, dff = wg.shape
    rows = lambda i, te, na: (0, jnp.minimum(i, na[0] - 1), 0)
    expert = lambda i, te, na: (layer, te[i], 0, 0)
    return pl.pallas_call(
        _experts_kernel,
        grid_spec=pltpu.PrefetchScalarGridSpec(
            num_scalar_prefetch=2,
            grid=(n_tiles,),
            in_specs=[pl.BlockSpec((ROW_CHUNKS, tm, 128), rows),
                      pl.BlockSpec((1, 1, d, dff), expert),
                      pl.BlockSpec((1, 1, d, dff), expert),
                      pl.BlockSpec((1, 1, dff, d), expert)],
            out_specs=pl.BlockSpec((ROW_CHUNKS, tm, 128), rows),
            scratch_shapes=[pltpu.VMEM((d, dff), BF16), pltpu.VMEM((d, dff), BF16), pltpu.VMEM((dff, d), BF16)]),
        out_shape=jax.ShapeDtypeStruct(xs.shape, xs.dtype),
        compiler_params=_cparams("arbitrary"),
        name="moe_experts",
    )(tile_expert, n_active, xs, wg, wu, wd)


def _moe_combine_kernel(x_ref, y1_ref, y2_ref, w_ref, o_ref):
    half = x_ref.shape[1] // 2
    hi1, lo1 = _unpack_bf16_pairs(_load_row_chunks(y1_ref))
    hi2, lo2 = _unpack_bf16_pairs(_load_row_chunks(y2_ref))
    tm = x_ref.shape[0]
    w_cols = jnp.concatenate([w_ref[...], jnp.zeros((128 - w_ref.shape[0], tm), F32)], axis=0).T
    w1, w2 = w_cols[:, 0:1], w_cols[:, 1:2]
    o_ref[:, :half] = x_ref[:, :half] + w1 * hi1 + w2 * hi2
    o_ref[:, half:] = x_ref[:, half:] + w1 * lo1 + w2 * lo2


def _moe_combine(x2d, y1, y2, wts, *, tm):
    t, d = x2d.shape
    chunk_spec = pl.BlockSpec((ROW_CHUNKS, tm, 128), lambda i: (0, i, 0))
    return pl.pallas_call(
        _moe_combine_kernel,
        grid=(t // tm,),
        in_specs=[pl.BlockSpec((tm, d), lambda i: (i, 0)), chunk_spec, chunk_spec,
                  pl.BlockSpec((wts.shape[0], tm), lambda i: (0, i))],
        out_specs=pl.BlockSpec((tm, d), lambda i: (i, 0)),
        out_shape=jax.ShapeDtypeStruct((t, d), F32),
        compiler_params=_cparams("parallel", vmem=VMEM_LIMIT_SMALL),
        name="moe_combine",
    )(x2d, y1, y2, wts)


def _moe(x2d, hf_rows, eidx, wts, wg, wu, wd, *, layer):
    t = x2d.shape[0]
    tm = MOE_TM
    n_tiles = 2 * t // tm + N_EXPERTS
    plane = n_tiles * tm
    i1, i2, te, na = _moe_plan(eidx, tm=tm, n_tiles=n_tiles)
    xs = _sc_dispatch(hf_rows.reshape(ROW_CHUNKS * t, 128), i1, i2, ROW_CHUNKS * plane)
    ys = _experts(te[0, :n_tiles], na[0, :1], xs.reshape(ROW_CHUNKS, plane, 128), wg, wu, wd,
                  layer=layer, tm=tm)
    y1, y2 = _sc_collect(ys.reshape(ROW_CHUNKS * plane, 128), i1, i2)
    return _moe_combine(x2d, y1.reshape(ROW_CHUNKS, t, 128), y2.reshape(ROW_CHUNKS, t, 128), wts,
                        tm=COMBINE_TM)


W_ROWS = 256


def _w_rows_kernel(start_ref, valid_ref, w_ref, o_ref):
    del start_ref
    row = lax.broadcasted_iota(jnp.int32, w_ref.shape[1:], 0)
    o_ref[0] = jnp.where(row < valid_ref[pl.program_id(1)], w_ref[0], 0.0).astype(o_ref.dtype)


def _w_rows(w_t, starts, valid):
    depth, _, d = w_t.shape
    nblk = len(starts)
    return pl.pallas_call(
        _w_rows_kernel,
        grid_spec=pltpu.PrefetchScalarGridSpec(
            num_scalar_prefetch=2,
            grid=(depth, nblk),
            in_specs=[pl.BlockSpec((pl.Element(1), pl.Element(W_ROWS), pl.Element(d)),
                                   lambda l, c, st, va: (l, pl.multiple_of(st[c], 8), 0))],
            out_specs=pl.BlockSpec((1, W_ROWS, d), lambda l, c, st, va: (l, c, 0))),
        out_shape=jax.ShapeDtypeStruct((depth, nblk * W_ROWS, d), BF16),
        compiler_params=_cparams("parallel", "arbitrary", vmem=VMEM_LIMIT_SMALL),
        name="w_in_rows",
    )(jnp.asarray(starts, jnp.int32), jnp.asarray(valid, jnp.int32), w_t)


def _w_in_layout(w_in):
    w_t = jnp.swapaxes(w_in, 1, 2)
    src_if = 4 * MLSTM_W
    src_a = src_if + 2 * MLSTM_HEADS
    src_g = src_a + 3 * ATTN_W
    starts = list(range(0, src_if, W_ROWS)) + [src_g + k * W_ROWS for k in range((OFF_IF - OFF_GU) // W_ROWS)]
    valid = [W_ROWS] * len(starts)
    starts.append(src_if)
    valid.append(2 * MLSTM_HEADS)
    assert len(starts) * W_ROWS == N_PROJ and ATTN_GW == W_ROWS
    a_starts = [src_a + j * ATTN_W + g * ATTN_GW for g in range(len(ATTN_PATTERNS)) for j in range(3)]
    return _w_rows(w_t, starts, valid), _w_rows(w_t, a_starts, [W_ROWS] * len(a_starts))


def kernel(x, mem, norm_mix, w_in, mlstm_conv, mlstm_gate_b, mlstm_norm, attn_qk_norm, gmlp_norm, gmlp_ws,
           gmlp_bs, w_branch_a, w_branch_b, w_branch_c, w_out, rel_bias, norm_xattn, norm_mem, w_xq, w_xkv,
           xattn_qk_norm, w_xo, norm_ffn, router_w, router_b, w_expert_gate, w_expert_up, w_expert_down):
    b, s, d = x.shape
    t = b * s
    depth = w_in.shape[0]
    x2d = x.reshape(t, d)

    biases = [_attn_bias(rel_bias, g) for g in range(len(ATTN_PATTERNS))]
    rw_t = jnp.zeros((N_EXPERT_GROUPS, 8, d), F32).at[:, :EXPERTS_PER_GROUP].set(
        router_w.T.reshape(N_EXPERT_GROUPS, EXPERTS_PER_GROUP, d)).reshape(ROUTER_ROWS, d)
    rb = jnp.full((N_EXPERT_GROUPS, 8), NEG, F32).at[:, :EXPERTS_PER_GROUP].set(
        router_b.astype(F32).reshape(N_EXPERT_GROUPS, EXPERTS_PER_GROUP)).reshape(ROUTER_ROWS, 1)
    tril = jnp.tril(jnp.ones((GMLP_CHUNK, GMLP_CHUNK), bool))
    head_of = jnp.arange(ATTN_GW) // ATTN_DH
    seg_ones = (head_of[:, None] == head_of[None, :]).astype(BF16)

    w_main, w_attn = _w_in_layout(w_in)

    for l in range(depth):
        proj, h_mix, gates_t = _inproj(x2d, norm_mix[l][None], w_main, layer=l, tm=INPROJ_TM,
                                       tn=INPROJ_TN)
        gq = jnp.tile(attn_qk_norm[l, 0], HEADS_PER_GROUP)[None]
        gk = jnp.tile(attn_qk_norm[l, 1], HEADS_PER_GROUP)[None]

        nh = MLSTM_HEADS
        bias_i = jnp.zeros((8, 1), F32).at[:nh, 0].set(mlstm_gate_b[l, :nh])
        bias_f = jnp.zeros((8, 1), F32).at[:nh, 0].set(mlstm_gate_b[l, nh:])
        ya = _mlstm_rows(proj, gates_t, mlstm_conv[l], bias_i, bias_f, mlstm_norm[l][None],
                         batch=b, seq=s, blk=MLSTM_BLOCK, group=MLSTM_GROUP)

        ybs, lses = [], []
        for g, (_, dilation) in enumerate(ATTN_PATTERNS):
            aproj = _attnproj(h_mix, w_attn, seg_ones, gq, gk, layer=l, group=g, dilation=dilation)
            o, lse = _dattn(aproj, biases[g], seq=s, group=g, dilation=dilation)
            ybs.append(o)
            lses.append(lse)

        ws = jnp.where(tril, gmlp_ws[l], 0.0).astype(BF16)
        bsb = jnp.broadcast_to(gmlp_bs[l][:, :, None], (GMLP_GROUPS, GMLP_CHUNK, GMLP_GC)).astype(F32)
        x2d = _merge(ya, ybs, lses, proj, x2d, w_branch_a[l].astype(BF16), w_branch_b[l].astype(BF16),
                     w_branch_c[l].astype(BF16), w_out[l].astype(BF16), ws, bsb, gmlp_norm[l][None],
                     tm=MERGE_TM)

        k_mem, v_mem = _memkv(mem, norm_mem[l][None], w_xkv[l].astype(BF16), xattn_qk_norm[l, 1][None])
        x2d, hf_rows, eidx, wts = _xattn(x2d, k_mem, v_mem, norm_xattn[l][None], w_xq[l].astype(BF16),
                                         xattn_qk_norm[l, 0][None], w_xo[l].astype(BF16), norm_ffn[l][None],
                                         rw_t, rb, seq=s, tm=XATTN_TM)

        x2d = _moe(x2d, hf_rows, eidx, wts, w_expert_gate, w_expert_up, w_expert_down, layer=l)

    return x2d.reshape(b, s, d)
```

```python
import functools
import math

import jax
import jax.numpy as jnp
import numpy as np
from jax import lax
from jax.experimental import pallas as pl
from jax.experimental.pallas import tpu as pltpu
from jax.experimental.pallas import tpu_sc as plsc

F32 = jnp.float32
BF16 = jnp.bfloat16

EPS = 1e-6
NEG = -1e30

MLSTM_HEADS = 4
MLSTM_DH = 128
MLSTM_W = MLSTM_HEADS * MLSTM_DH
CONV_WIDTH = 4
MLSTM_BLOCK = 128
MLSTM_GROUP = 4

ATTN_PATTERNS = ((128, 1), (512, 4), (2048, 16))
HEADS_PER_GROUP = 4
ATTN_DH = 64
ATTN_GW = HEADS_PER_GROUP * ATTN_DH
ATTN_W = len(ATTN_PATTERNS) * ATTN_GW
ATTN_BLOCK = 128
REL_BUCKETS = 32
REL_MAX_DIST = 2048

GMLP_GROUPS = 4
GMLP_GC = 128
GMLP_W = GMLP_GROUPS * GMLP_GC
GMLP_CHUNK = 128

XATTN_HEADS = 4
XATTN_DH = 128
XATTN_W = XATTN_HEADS * XATTN_DH
XATTN_SUB = 1024

N_EXPERTS = 16
N_EXPERT_GROUPS = 4
EXPERTS_PER_GROUP = 4
ROUTER_ROWS = 8 * N_EXPERT_GROUPS

N_BRANCH = 3

MOE_TM = 1024
ROW_CHUNKS = 4
SC_CORES, SC_SUBCORES = 2, 16
SC_WINDOW = 128

OFF_MQ, OFF_MK, OFF_MV, OFF_MO = 0, 512, 1024, 1536
OFF_GU, OFF_GV = 2048, 2560
OFF_GATE = 3072
OFF_IF = 6144
IF_PAD = 256
N_PROJ = OFF_IF + IF_PAD

ATTN_TILE = 2048
ATTN_SUB = ATTN_TILE // ATTN_BLOCK
ATTN_SLAB = 2 * ATTN_DH
ATTN_COLS = HEADS_PER_GROUP * ATTN_SLAB + 2 * ATTN_GW

VMEM_LIMIT = 48 * 1024 * 1024
VMEM_LIMIT_INPROJ = 56 * 1024 * 1024
VMEM_LIMIT_SMALL = 24 * 1024 * 1024

INPROJ_TM, INPROJ_TN = 1024, 3072
ATTNPROJ_SUB = 512
MERGE_TM = 512
XATTN_TM = 1024
COMBINE_TM = 512
PLAN_TB = 1024


def _cparams(*sem, vmem=VMEM_LIMIT):
    return pltpu.CompilerParams(dimension_semantics=sem, vmem_limit_bytes=vmem)


def _rms(x, gain):
    return x * lax.rsqrt(jnp.mean(x * x, axis=-1, keepdims=True) + EPS) * gain


def _sigmoid(x):
    return 0.5 * jnp.tanh(0.5 * x) + 0.5


def _dot(a, b):
    return jnp.dot(a, b, preferred_element_type=F32)


def _dot_nt(a, b):
    return lax.dot_general(a, b, (((1,), (1,)), ((), ())), preferred_element_type=F32)


def _inproj_kernel(x_ref, g_ref, w_ref, wg_ref, o_ref, h_ref, gt_ref):
    @pl.when(pl.program_id(1) == 0)
    def _():
        h = _rms(x_ref[...], g_ref[...]).astype(BF16)
        h_ref[...] = h
        gt_ref[...] = _dot_nt(wg_ref[0, 0:128, :], h)[:gt_ref.shape[0], :]

    o_ref[...] = _dot_nt(h_ref[...], w_ref[0]).astype(o_ref.dtype)


def _inproj(x2d, gain, w, *, layer, tm, tn):
    t, d = x2d.shape
    n = OFF_IF
    return pl.pallas_call(
        _inproj_kernel,
        grid=(t // tm, n // tn),
        in_specs=[pl.BlockSpec((tm, d), lambda i, j: (i, 0)),
                  pl.BlockSpec((1, d), lambda i, j: (0, 0)),
                  pl.BlockSpec((1, tn, d), lambda i, j: (layer, j, 0)),
                  pl.BlockSpec((1, IF_PAD, d), lambda i, j: (layer, OFF_IF // IF_PAD, 0))],
        out_specs=[pl.BlockSpec((tm, tn), lambda i, j: (i, j)),
                   pl.BlockSpec((tm, d), lambda i, j: (i, 0)),
                   pl.BlockSpec((8, tm), lambda i, j: (0, i))],
        out_shape=[jax.ShapeDtypeStruct((t, n), BF16), jax.ShapeDtypeStruct((t, d), BF16),
                   jax.ShapeDtypeStruct((8, t), F32)],
        compiler_params=_cparams("parallel", "arbitrary", vmem=VMEM_LIMIT_INPROJ),
        name="inproj",
    )(x2d, gain, w, w)


def _log_sigmoid(x):
    return jnp.minimum(x, 0.0) - jnp.log(1.0 + jnp.exp(-jnp.abs(x)))


def _split_bf16(x):
    hi = x.astype(BF16)
    return hi, (x - hi.astype(F32)).astype(BF16)


def _prefix_max(x):
    n = x.shape[1]
    lane = lax.broadcasted_iota(jnp.int32, x.shape, 1)
    shift = 1
    while shift < n:
        x = jnp.maximum(x, jnp.where(lane >= shift, pltpu.roll(x, shift, 1), NEG))
        shift *= 2
    return x


def _mlstm_rows_kernel(qk_ref, v_ref, og_ref, *rest, blk, group):
    gate_refs = rest[:group]
    cw_ref, bi_ref, bf_ref, ng_ref, y_ref, xe_scr, s_scr, m_scr = rest[group:]
    heads, dh, w = MLSTM_HEADS, MLSTM_DH, MLSTM_W

    @pl.when(pl.program_id(1) == 0)
    def _():
        xe_scr[:, 0:8, :] = jnp.zeros((group, 8, 2 * w), F32)
        s_scr[...] = jnp.zeros_like(s_scr)
        m_scr[...] = jnp.zeros_like(m_scr)

    cw = cw_ref[...]
    causal = lax.broadcasted_iota(jnp.int32, (blk, blk), 0) >= lax.broadcasted_iota(jnp.int32, (blk, blk), 1)
    triu = (lax.broadcasted_iota(jnp.int32, (blk, blk), 0)
            <= lax.broadcasted_iota(jnp.int32, (blk, blk), 1)).astype(BF16)
    ones = jnp.ones((blk, dh), BF16)
    s_in = [[s_scr[g, h] for h in range(heads)] for g in range(group)]
    m_in = [m_scr[g, :, 0:1] for g in range(group)]
    s_out = [[None] * heads for _ in range(group)]
    m_out = [None] * group
    per_seq = []
    for g in range(group):
        xe_scr[g, 8:8 + blk, :] = qk_ref[g].astype(F32)
        conv = cw[CONV_WIDTH - 1:CONV_WIDTH, :] * xe_scr[g, 8:8 + blk, :]
        for j in range(CONV_WIDTH - 1):
            off = 8 - (CONV_WIDTH - 1) + j
            conv = conv + cw[j:j + 1, :] * xe_scr[g, off:off + blk, :]
        xe_scr[g, 0:8, :] = xe_scr[g, blk:blk + 8, :]
        qk = conv * _sigmoid(conv)

        gates = gate_refs[g][...]
        i_r = gates + bi_ref[...]
        lf_hi, lf_lo = _split_bf16(_log_sigmoid(pltpu.roll(gates, heads, 0) + bf_ref[...]))
        b_r = _dot(lf_hi, triu) + _dot(lf_lo, triu)
        m_st = m_in[g]
        a_r = i_r - b_r
        inter = b_r + m_st
        m_t = jnp.maximum(inter, b_r + _prefix_max(a_r))
        b_last = b_r[:, blk - 1:blk]
        dec = b_last - b_r + i_r
        m_new = jnp.maximum(b_last + m_st, jnp.max(dec, axis=1, keepdims=True))
        w_c = jnp.exp(b_last + m_st - m_new)
        m_out[g] = m_new
        pack = jnp.concatenate([b_r - m_t, jnp.exp(inter - m_t), jnp.exp(-m_t), jnp.exp(dec - m_new),
                                jnp.zeros((blk - 32, blk), F32)], axis=0)
        per_seq.append((qk, a_r, pack.T, w_c))

    chains = [(g, h) for h in range(heads) for g in range(group)]
    st = {}
    for g, h in chains:
        qk = per_seq[g][0]
        sl = slice(h * dh, (h + 1) * dh)
        q_b = qk[:, sl].astype(BF16)
        k_f = qk[:, w + h * dh:w + (h + 1) * dh] * (dh ** -0.5)
        v_ext = jnp.concatenate([v_ref[g, :, sl], ones], axis=-1)
        st[g, h] = (q_b, k_f, v_ext, _dot_nt(q_b, k_f.astype(BF16)), _dot(q_b, s_in[g][h].astype(BF16)))
    for g, h in chains:
        q_b, k_f, v_ext, qk_t, q_state = st[g, h]
        _, a_r, cols, _ = per_seq[g]
        u_c, w_inter = cols[:, h:h + 1], cols[:, 8 + h:9 + h]
        w_intra = jnp.exp(jnp.where(causal, u_c + a_r[h:h + 1, :], NEG))
        st[g, h] = (k_f, v_ext, _dot((qk_t * w_intra).astype(BF16), v_ext) + w_inter * q_state)
    for g, h in chains:
        k_f, v_ext, tot = st[g, h]
        _, _, cols, w_c = per_seq[g]
        em_c, w_k = cols[:, 16 + h:17 + h], cols[:, 24 + h:25 + h]
        sl = slice(h * dh, (h + 1) * dh)
        num, den = tot[:, :dh], tot[:, dh:]
        hh = num / jnp.maximum(jnp.abs(den), em_c)
        hn = _rms(hh, ng_ref[:, sl])
        y_ref[g, :, sl] = (hn * _sigmoid(og_ref[g, :, sl].astype(F32))).astype(y_ref.dtype)
        s_out[g][h] = w_c[h:h + 1, :] * s_in[g][h] + _dot((k_f * w_k).T.astype(BF16), v_ext)
    for g in range(group):
        m_scr[g] = jnp.broadcast_to(m_out[g], m_scr.shape[1:])
        for h in range(heads):
            s_scr[g, h] = s_out[g][h]


def _mlstm_rows(proj, gates_t, conv_w, bias_i, bias_f, norm_g, *, batch, seq, blk, group):
    t, npj = proj.shape
    w = MLSTM_W
    proj3 = proj.reshape(batch, seq, npj)
    cols = lambda c: (lambda b, i: (b, i, c))
    const2 = lambda b, i: (0, 0)
    nblk = seq // blk
    gate_specs = [pl.BlockSpec((8, blk), functools.partial(lambda b, i, g: (0, (b * group + g) * nblk + i), g=g))
                  for g in range(group)]
    y = pl.pallas_call(
        functools.partial(_mlstm_rows_kernel, blk=blk, group=group),
        grid=(batch // group, seq // blk),
        in_specs=[pl.BlockSpec((group, blk, 2 * w), cols(OFF_MQ // (2 * w))),
                  pl.BlockSpec((group, blk, w), cols(OFF_MV // w)),
                  pl.BlockSpec((group, blk, w), cols(OFF_MO // w)),
                  *gate_specs,
                  pl.BlockSpec((CONV_WIDTH, 2 * w), const2),
                  pl.BlockSpec((8, 1), const2), pl.BlockSpec((8, 1), const2),
                  pl.BlockSpec((1, w), const2)],
        out_specs=pl.BlockSpec((group, blk, w), cols(0)),
        out_shape=jax.ShapeDtypeStruct((batch, seq, w), BF16),
        scratch_shapes=[pltpu.VMEM((group, blk + 8, 2 * w), F32),
                        pltpu.VMEM((group, MLSTM_HEADS, MLSTM_DH, 2 * MLSTM_DH), F32),
                        pltpu.VMEM((group, 8, 128), F32)],
        compiler_params=_cparams("parallel", "arbitrary", vmem=VMEM_LIMIT_SMALL),
        name="mlstm",
    )(proj3, proj3, proj3, *([gates_t] * group), conv_w, bias_i, bias_f, norm_g)
    return y.reshape(t, w)


def _attnproj_kernel(h_ref, w_ref, seg_ref, gq_ref, gk_ref, o_ref, r_scr, *, dil):
    gw, half = ATTN_GW, ATTN_SLAB // 2
    sub_rows = ATTNPROJ_SUB
    seg, sub_seg = ATTN_TILE // dil, sub_rows // dil

    def head_norm(x, gain):
        ss = _dot((x * x).astype(BF16), seg_ref[...])
        return x * lax.rsqrt(ss * (1.0 / ATTN_DH) + EPS) * gain

    low = lax.broadcasted_iota(jnp.int32, (1, ATTN_SLAB), 1) < half
    for s in range(ATTN_TILE // sub_rows):
        rows = slice(s * sub_rows, (s + 1) * sub_rows)
        res = _dot_nt(h_ref[rows, :], w_ref[0])
        q = head_norm(res[:, :gw], gq_ref[...]) * (ATTN_DH ** -0.5)
        k = head_norm(res[:, gw:2 * gw], gk_ref[...])
        slabs = []
        for pair in range(gw // ATTN_SLAB):
            qp = q[:, pair * ATTN_SLAB:(pair + 1) * ATTN_SLAB]
            slabs += [jnp.where(low, qp, 0.0), jnp.where(low, 0.0, qp)]
        slabs += [k[:, c * 128:(c + 1) * 128] for c in range(gw // 128)]
        slabs += [res[:, 2 * gw + c * 128:2 * gw + (c + 1) * 128] for c in range(gw // 128)]
        pitch = dil + 1 if dil % 16 == 0 else dil
        for c, slab in enumerate(slabs):
            if dil == 1:
                o_ref[rows, c * 128:(c + 1) * 128] = slab.astype(o_ref.dtype)
            elif pitch == dil:
                r_scr[s % 2, c, 0:sub_rows, :] = slab
            else:
                for i in range(sub_seg):
                    r_scr[s % 2, c, pitch * i:pitch * i + dil, :] = slab[dil * i:dil * (i + 1), :]
        if dil > 1:
            for r in range(dil):
                dst = slice(r * seg + s * sub_seg, r * seg + (s + 1) * sub_seg)
                for c in range(r_scr.shape[1]):
                    o_ref[dst, c * 128:(c + 1) * 128] = (
                        r_scr[s % 2, c, pl.ds(r, sub_seg, stride=pitch), :].astype(o_ref.dtype))


def _attnproj(h, w, seg_ones, gq, gk, *, layer, group, dilation):
    t, d = h.shape
    wcols = 3 * ATTN_GW
    const2 = lambda i: (0, 0)
    return pl.pallas_call(
        functools.partial(_attnproj_kernel, dil=dilation),
        grid=(t // ATTN_TILE,),
        in_specs=[pl.BlockSpec((ATTN_TILE, d), lambda i: (i, 0)),
                  pl.BlockSpec((1, wcols, d), lambda i: (layer, group, 0)),
                  pl.BlockSpec((ATTN_GW, ATTN_GW), const2),
                  pl.BlockSpec((1, ATTN_GW), const2), pl.BlockSpec((1, ATTN_GW), const2)],
        out_specs=pl.BlockSpec((ATTN_TILE, ATTN_COLS), lambda i: (i, 0)),
        out_shape=jax.ShapeDtypeStruct((t, ATTN_COLS), BF16),
        scratch_shapes=[pltpu.VMEM((2, ATTN_COLS // 128, ATTNPROJ_SUB + ATTNPROJ_SUB // 16, 128), F32)],
        compiler_params=_cparams("parallel"),
        name=f"attnproj{group}",
    )(h, w, seg_ones, gq, gk)


def _dattn_kernel(q_ref, kc_ref, kp_ref, vc_ref, vp_ref, bias_ref, o_ref, lse_ref,
                  kx_scr, vx_scr, o_scr, l_scr, *, dil):
    blk = ATTN_BLOCK
    per = ATTN_SUB // dil
    pitch = dil + 1 if dil % 16 == 0 else dil
    first_tile = pl.program_id(1) == 0
    for r in range(dil):
        base = r * (per + 1) * blk
        last = slice((r * per + per - 1) * blk, (r * per + per) * blk)
        mine = slice(r * per * blk, (r + 1) * per * blk)
        kx_scr[base:base + blk, :] = kp_ref[last, :]
        vx_scr[base:base + blk, :] = vp_ref[last, :]
        kx_scr[base + blk:base + (per + 1) * blk, :] = kc_ref[mine, :]
        vx_scr[base + blk:base + (per + 1) * blk, :] = vc_ref[mine, :]

    low = lax.broadcasted_iota(jnp.int32, (1, ATTN_SLAB), 1) < ATTN_SLAB // 2
    no_prev = lax.broadcasted_iota(jnp.int32, (1, 2 * blk), 1) < blk
    for r in range(dil):
        for sub in range(per):
            u = r * per + sub
            win = slice((r * (per + 1) + sub) * blk, (r * (per + 1) + sub + 2) * blk)
            o_slabs, l_slabs = [], []
            for pair in range(ATTN_GW // ATTN_SLAB):
                cols = slice(pair * ATTN_SLAB, (pair + 1) * ATTN_SLAB)
                kx, vx = kx_scr[win, cols], vx_scr[win, cols]
                o_pair, l_pair = [], []
                for h in (2 * pair, 2 * pair + 1):
                    logits = _dot_nt(q_ref[u * blk:(u + 1) * blk, h * ATTN_SLAB:(h + 1) * ATTN_SLAB], kx)
                    logits = logits + bias_ref[h]
                    if sub == 0:
                        logits = jnp.where(first_tile & no_prev, NEG, logits)
                    m = jnp.max(logits, axis=-1, keepdims=True)
                    p = jnp.exp(logits - m)
                    l = jnp.sum(p, axis=-1, keepdims=True)
                    o_pair.append(_dot(p.astype(BF16), vx) / l)
                    l_pair.append(m + jnp.log(l))
                o_slabs.append(jnp.where(low, o_pair[0], o_pair[1]))
                l_slabs.append(jnp.where(low, l_pair[0], l_pair[1]))
            for c in range(ATTN_GW // ATTN_SLAB):
                cols = slice(c * ATTN_SLAB, (c + 1) * ATTN_SLAB)
                if dil == 1:
                    o_ref[u * blk:(u + 1) * blk, cols] = o_slabs[c].astype(o_ref.dtype)
                    lse_ref[u * blk:(u + 1) * blk, cols] = l_slabs[c]
                else:
                    dst = pl.ds(sub * blk * pitch + r, blk, stride=pitch)
                    o_scr[c, dst, :] = o_slabs[c]
                    l_scr[c, dst, :] = l_slabs[c]
    if dil > 1:
        for c in range(ATTN_GW // ATTN_SLAB):
            cols = slice(c * ATTN_SLAB, (c + 1) * ATTN_SLAB)
            if pitch == dil:
                o_ref[:, cols] = o_scr[c, 0:ATTN_TILE, :].astype(o_ref.dtype)
                lse_ref[:, cols] = l_scr[c, 0:ATTN_TILE, :]
            else:
                for i in range(ATTN_TILE // dil):
                    o_ref[dil * i:dil * (i + 1), cols] = o_scr[c, pitch * i:pitch * i + dil, :].astype(o_ref.dtype)
                    lse_ref[dil * i:dil * (i + 1), cols] = l_scr[c, pitch * i:pitch * i + dil, :]


def _dattn(aproj, bias, *, seq, group, dilation):
    t = aproj.shape[0]
    tiles = seq // ATTN_TILE
    qw = HEADS_PER_GROUP * ATTN_SLAB
    cq, ck, cv = 0, qw // ATTN_GW, qw // ATTN_GW + 1
    blk = (ATTN_TILE, ATTN_GW)
    cur = lambda c: (lambda b, j: (b * tiles + j, c))
    prev = lambda c: (lambda b, j: (b * tiles + jnp.maximum(j - 1, 0), c))
    xrows = ATTN_TILE + dilation * ATTN_BLOCK
    return pl.pallas_call(
        functools.partial(_dattn_kernel, dil=dilation),
        grid=(t // seq, tiles),
        in_specs=[pl.BlockSpec((ATTN_TILE, qw), cur(cq)),
                  pl.BlockSpec(blk, cur(ck)), pl.BlockSpec(blk, prev(ck)),
                  pl.BlockSpec(blk, cur(cv)), pl.BlockSpec(blk, prev(cv)),
                  pl.BlockSpec((HEADS_PER_GROUP, ATTN_BLOCK, 2 * ATTN_BLOCK), lambda b, j: (0, 0, 0))],
        out_specs=[pl.BlockSpec(blk, cur(0)), pl.BlockSpec(blk, cur(0))],
        out_shape=[jax.ShapeDtypeStruct((t, ATTN_GW), BF16), jax.ShapeDtypeStruct((t, ATTN_GW), F32)],
        scratch_shapes=[pltpu.VMEM((xrows, ATTN_GW), BF16), pltpu.VMEM((xrows, ATTN_GW), BF16),
                        pltpu.VMEM((ATTN_GW // ATTN_SLAB, ATTN_TILE + ATTN_TILE // 16, ATTN_SLAB), F32),
                        pltpu.VMEM((ATTN_GW // ATTN_SLAB, ATTN_TILE + ATTN_TILE // 16, ATTN_SLAB), F32)],
        compiler_params=_cparams("parallel", "arbitrary"),
        name=f"dattn{group}",
    )(aproj, aproj, aproj, aproj, aproj, bias)


def _rel_bucket(n):
    max_exact = REL_BUCKETS // 2
    nf = jnp.maximum(n, 1).astype(F32)
    log_b = max_exact + (jnp.log(nf / max_exact) / math.log(REL_MAX_DIST / max_exact)
                         * (REL_BUCKETS - max_exact)).astype(jnp.int32)
    return jnp.where(n < max_exact, n, jnp.minimum(log_b, REL_BUCKETS - 1))


def _attn_bias(rel_bias, group):
    window, dilation = ATTN_PATTERNS[group]
    steps = window // dilation
    hp = lax.Precision.HIGHEST
    hs = slice(group * HEADS_PER_GROUP, (group + 1) * HEADS_PER_GROUP)
    bucket = _rel_bucket(jnp.arange(steps + 1) * dilation)
    bias_steps = jnp.dot(jax.nn.one_hot(bucket, REL_BUCKETS, dtype=F32), rel_bias[:, hs].astype(F32),
                         precision=hp)
    qi = jnp.arange(ATTN_BLOCK)[:, None]
    ki = jnp.arange(2 * ATTN_BLOCK)[None, :]
    dist = ATTN_BLOCK + qi - ki
    ok = (dist >= 0) & (dist <= steps)
    sel = jax.nn.one_hot(jnp.clip(dist, 0, steps).reshape(-1), steps + 1, dtype=F32)
    bias = jnp.dot(sel, bias_steps, precision=hp).T.reshape(HEADS_PER_GROUP, ATTN_BLOCK, 2 * ATTN_BLOCK)
    return jnp.where(ok[None], bias, NEG)


def _merge_kernel(ya_ref, yb0_ref, yb1_ref, yb2_ref, l0_ref, l1_ref, l2_ref, gu_ref, gv_ref, gate_ref,
                  x_ref, wa_ref, wb_ref, wc_ref, wo_ref, ws_ref, bs_ref, gg_ref, o_ref, yc_scr, *, tm):
    d = x_ref.shape[1]
    l0, l1, l2 = l0_ref[...], l1_ref[...], l2_ref[...]
    mx = jnp.maximum(jnp.maximum(l0, l1), l2)
    e0, e1, e2 = jnp.exp(l0 - mx), jnp.exp(l1 - mx), jnp.exp(l2 - mx)
    inv = 1.0 / (e0 + e1 + e2)
    yb = jnp.concatenate([(yb0_ref[...].astype(F32) * (e0 * inv)).astype(BF16),
                          (yb1_ref[...].astype(F32) * (e1 * inv)).astype(BF16),
                          (yb2_ref[...].astype(F32) * (e2 * inv)).astype(BF16)], axis=-1)

    for j in range(tm // GMLP_CHUNK):
        rows = slice(j * GMLP_CHUNK, (j + 1) * GMLP_CHUNK)
        for g in range(GMLP_GROUPS):
            cols = slice(g * GMLP_GC, (g + 1) * GMLP_GC)
            u = jax.nn.gelu(gu_ref[rows, cols].astype(F32))
            v = _rms(jax.nn.gelu(gv_ref[rows, cols].astype(F32)), gg_ref[:, cols])
            mixed = _dot(ws_ref[g], v.astype(BF16)) + bs_ref[g]
            yc_scr[rows, cols] = (u * mixed).astype(BF16)

    def gate2(k):
        return jnp.tanh(0.5 * gate_ref[:, k * d:(k + 1) * d].astype(F32)) + 1.0

    merged2 = gate2(0) * _dot(ya_ref[...], wa_ref[...])
    merged2 = merged2 + gate2(1) * _dot(yb, wb_ref[...])
    merged2 = merged2 + gate2(2) * _dot(yc_scr[...], wc_ref[...])
    o_ref[...] = x_ref[...] + 0.5 * _dot(merged2.astype(BF16), wo_ref[...])


def _merge(ya, ybs, lses, proj, x2d, wa, wb, wc, wo, ws, bsb, gg, *, tm):
    t, d = x2d.shape
    row = lambda c: (lambda i: (i, c))
    full2 = lambda i: (0, 0)
    full3 = lambda i: (0, 0, 0)
    gspec = pl.BlockSpec((tm, ATTN_GW), row(0))
    return pl.pallas_call(
        functools.partial(_merge_kernel, tm=tm),
        grid=(t // tm,),
        in_specs=[pl.BlockSpec((tm, MLSTM_W), row(0)),
                  gspec, gspec, gspec, gspec, gspec, gspec,
                  pl.BlockSpec((tm, GMLP_W), row(OFF_GU // GMLP_W)),
                  pl.BlockSpec((tm, GMLP_W), row(OFF_GV // GMLP_W)),
                  pl.BlockSpec((tm, N_BRANCH * d), row(OFF_GATE // (N_BRANCH * d))),
                  pl.BlockSpec((tm, d), row(0)),
                  pl.BlockSpec(wa.shape, full2), pl.BlockSpec(wb.shape, full2),
                  pl.BlockSpec(wc.shape, full2), pl.BlockSpec(wo.shape, full2),
                  pl.BlockSpec(ws.shape, full3), pl.BlockSpec(bsb.shape, full3),
                  pl.BlockSpec(gg.shape, full2)],
        out_specs=pl.BlockSpec((tm, d), row(0)),
        out_shape=jax.ShapeDtypeStruct((t, d), F32),
        scratch_shapes=[pltpu.VMEM((tm, GMLP_W), BF16)],
        compiler_params=_cparams("parallel"),
        name="merge",
    )(ya, *ybs, *lses, proj, proj, proj, x2d, wa, wb, wc, wo, ws, bsb, gg)


def _memkv_kernel(mem_ref, g_ref, w_ref, gk_ref, k_ref, v_ref):
    dh, w = XATTN_DH, XATTN_W
    kv = _dot(_rms(mem_ref[0], g_ref[...]).astype(BF16), w_ref[...])
    for h in range(XATTN_HEADS):
        sl = slice(h * dh, (h + 1) * dh)
        k_ref[0, :, sl] = _rms(kv[:, sl], gk_ref[...]).astype(k_ref.dtype)
    v_ref[0] = kv[:, w:].astype(v_ref.dtype)


def _memkv(mem, gain, w_kv, gk):
    b, m, d = mem.shape
    full2 = lambda i: (0, 0)
    return pl.pallas_call(
        _memkv_kernel,
        grid=(b,),
        in_specs=[pl.BlockSpec((1, m, d), lambda i: (i, 0, 0)),
                  pl.BlockSpec((1, d), full2),
                  pl.BlockSpec(w_kv.shape, full2),
                  pl.BlockSpec((1, XATTN_DH), full2)],
        out_specs=[pl.BlockSpec((1, m, XATTN_W), lambda i: (i, 0, 0)),
                   pl.BlockSpec((1, m, XATTN_W), lambda i: (i, 0, 0))],
        out_shape=[jax.ShapeDtypeStruct((b, m, XATTN_W), BF16),
                   jax.ShapeDtypeStruct((b, m, XATTN_W), BF16)],
        compiler_params=_cparams("parallel", vmem=VMEM_LIMIT_SMALL),
        name="memkv",
    )(mem, gain, w_kv, gk)


def _route(logits):
    tm = logits.shape[1]
    e = jnp.exp(logits - jnp.max(logits, axis=0, keepdims=True))
    probs = e / jnp.sum(e, axis=0, keepdims=True)
    rowi = lax.broadcasted_iota(jnp.int32, (8, tm), 0)
    real = rowi < EXPERTS_PER_GROUP
    tops = []
    for g in range(N_EXPERT_GROUPS):
        pg = jnp.where(real, probs[8 * g:8 * g + 8, :], -0.5)
        m1 = jnp.max(pg, axis=0, keepdims=True)
        i1 = jnp.min(jnp.where(pg == m1, rowi, 8), axis=0, keepdims=True)
        pg2 = jnp.where(rowi == i1, -1.0, pg)
        m2 = jnp.max(pg2, axis=0, keepdims=True)
        i2 = jnp.min(jnp.where(pg2 == m2, rowi, 8), axis=0, keepdims=True)
        tops.append((m1, i1, m2, i2))
    best = jnp.zeros((1, tm), jnp.int32)
    best_score = tops[0][0] + tops[0][2]
    for g in range(1, N_EXPERT_GROUPS):
        score = tops[g][0] + tops[g][2]
        better = score > best_score
        best = jnp.where(better, g, best)
        best_score = jnp.where(better, score, best_score)
    m1, i1, m2, i2 = tops[0]
    for g in range(1, N_EXPERT_GROUPS):
        m1, i1, m2, i2 = (jnp.where(best == g, new, old) for new, old in zip(tops[g], (m1, i1, m2, i2)))
    tot = m1 + m2
    base = best * EXPERTS_PER_GROUP
    return base + i1, base + i2, m1 / tot, m2 / tot


def _pack_bf16_pairs(x):
    n = x.shape[1] // 2
    hi = lax.bitcast_convert_type(x[:, :n].astype(BF16).astype(F32), jnp.uint32)
    lo = lax.bitcast_convert_type(x[:, n:].astype(BF16).astype(F32), jnp.uint32)
    return hi | (lo >> 16)


def _unpack_bf16_pairs(p):
    hi = lax.bitcast_convert_type(p & jnp.uint32(0xFFFF0000), F32)
    lo = lax.bitcast_convert_type(p << 16, F32)
    return hi, lo


def _store_row_chunks(ref, packed):
    for j in range(ROW_CHUNKS):
        ref[j] = packed[:, j * 128:(j + 1) * 128]


def _load_row_chunks(ref):
    return jnp.concatenate([ref[j] for j in range(ROW_CHUNKS)], axis=-1)


def _xattn_kernel(x_ref, k_ref, v_ref, gx_ref, wq_ref, gq_ref, wo_ref, gf_ref, rw_ref, rb_ref,
                  xo_ref, hf_ref, eidx_ref, wts_ref, *, sub):
    dh = XATTN_DH
    rw = rw_ref[...]
    rw_hi, rw_lo = _split_bf16(rw)
    for s in range(x_ref.shape[0] // sub):
        rows = slice(s * sub, (s + 1) * sub)
        x = x_ref[rows, :]
        q = _dot(_rms(x, gx_ref[...]).astype(BF16), wq_ref[...])
        outs = []
        for h in range(XATTN_HEADS):
            sl = slice(h * dh, (h + 1) * dh)
            q_h = (_rms(q[:, sl], gq_ref[...]) * (dh ** -0.5)).astype(BF16)
            logits = _dot_nt(q_h, k_ref[0, :, sl])
            p = jnp.exp(logits - jnp.max(logits, axis=-1, keepdims=True))
            o = _dot(p.astype(BF16), v_ref[0, :, sl]) / jnp.sum(p, axis=-1, keepdims=True)
            outs.append(o.astype(BF16))
        xn = x + _dot(jnp.concatenate(outs, axis=-1), wo_ref[...])
        xo_ref[rows, :] = xn
        hf = _rms(xn, gf_ref[...])
        packed = _pack_bf16_pairs(hf)
        for j in range(ROW_CHUNKS):
            hf_ref[j, rows, :] = packed[:, j * 128:(j + 1) * 128]
        hf_hi, hf_lo = _split_bf16(hf)
        logits_t = _dot_nt(rw_hi, hf_hi) + _dot_nt(rw_hi, hf_lo) + _dot_nt(rw_lo, hf_hi) + rb_ref[...]
        e1, e2, w1, w2 = _route(logits_t)
        eidx_ref[:, rows] = jnp.concatenate([e1, e2, jnp.zeros((6, sub), jnp.int32)], axis=0)
        wts_ref[:, rows] = jnp.concatenate([w1, w2, jnp.zeros((6, sub), F32)], axis=0)


def _xattn(x2d, k, v, gx, wq, gq, wo, gf, rw_t, rb, *, seq, tm):
    t, d = x2d.shape
    per_b = seq // tm
    full2 = lambda i: (0, 0)
    kv_spec = pl.BlockSpec((1,) + k.shape[1:], lambda i: (i // per_b, 0, 0))
    return pl.pallas_call(
        functools.partial(_xattn_kernel, sub=min(tm, XATTN_SUB)),
        grid=(t // tm,),
        in_specs=[pl.BlockSpec((tm, d), lambda i: (i, 0)), kv_spec, kv_spec,
                  pl.BlockSpec((1, d), full2), pl.BlockSpec(wq.shape, full2),
                  pl.BlockSpec((1, XATTN_DH), full2), pl.BlockSpec(wo.shape, full2),
                  pl.BlockSpec((1, d), full2), pl.BlockSpec(rw_t.shape, full2),
                  pl.BlockSpec(rb.shape, full2)],
        out_specs=[pl.BlockSpec((tm, d), lambda i: (i, 0)),
                   pl.BlockSpec((ROW_CHUNKS, tm, 128), lambda i: (0, i, 0)),
                   pl.BlockSpec((8, tm), lambda i: (0, i)),
                   pl.BlockSpec((8, tm), lambda i: (0, i))],
        out_shape=[jax.ShapeDtypeStruct((t, d), F32),
                   jax.ShapeDtypeStruct((ROW_CHUNKS, t, 128), jnp.uint32),
                   jax.ShapeDtypeStruct((8, t), jnp.int32),
                   jax.ShapeDtypeStruct((8, t), F32)],
        compiler_params=_cparams("parallel"),
        name="xattn_router",
    )(x2d, k, v, gx, wq, gq, wo, gf, rw_t, rb)


def _moe_plan_kernel(eidx_ref, i1_ref, i2_ref, te_ref, na_ref, cnt_scr, carry_scr, *, tb, tm, plane_rows):
    ne = N_EXPERTS
    hp = lax.Precision.HIGHEST
    phase, j = pl.program_id(0), pl.program_id(1)
    rows = lax.broadcasted_iota(jnp.int32, (ne, tb), 0)
    oh1 = rows == eidx_ref[0:1, :]
    oh2 = rows == eidx_ref[1:2, :]
    a = oh1.astype(F32) + oh2.astype(F32)
    blk_cnt = jnp.broadcast_to(jnp.sum(a, axis=1, keepdims=True), cnt_scr.shape)

    @pl.when((phase == 0) & (j == 0))
    def _():
        cnt_scr[...] = jnp.zeros_like(cnt_scr)

    @pl.when(phase == 0)
    def _():
        cnt_scr[...] += blk_cnt

    @pl.when((phase == 1) & (j == 0))
    def _():
        padded = jnp.ceil(cnt_scr[...] * (1.0 / tm)) * tm
        er = lax.broadcasted_iota(jnp.int32, (ne, ne), 0)
        ec = lax.broadcasted_iota(jnp.int32, (ne, ne), 1)
        off = jnp.dot((ec < er).astype(F32), padded, precision=hp, preferred_element_type=F32)
        carry_scr[...] = off
        seg_end = (off + padded)[:, 0:1]
        tile_start = lax.broadcasted_iota(jnp.int32, (ne, te_ref.shape[1]), 1).astype(F32) * tm
        te = jnp.sum((seg_end <= tile_start).astype(F32), axis=0, keepdims=True)
        te_ref[...] = jnp.broadcast_to(jnp.minimum(te, ne - 1.0), te_ref.shape).astype(jnp.int32)
        total = jnp.sum(padded[:, 0:1], axis=0, keepdims=True)
        na_ref[...] = jnp.broadcast_to(total * (1.0 / tm), na_ref.shape).astype(jnp.int32)

    @pl.when(phase == 1)
    def _():
        before = (lax.broadcasted_iota(jnp.int32, (tb, tb), 0)
                  < lax.broadcasted_iota(jnp.int32, (tb, tb), 1)).astype(BF16)
        rank = carry_scr[:, 0:1] + _dot(a.astype(BF16), before)
        d1 = jnp.sum(jnp.where(oh1, rank, 0.0), axis=0, keepdims=True).astype(jnp.int32)
        d2 = jnp.sum(jnp.where(oh2, rank, 0.0), axis=0, keepdims=True).astype(jnp.int32)
        plane = lax.broadcasted_iota(jnp.int32, (8, tb), 0) * plane_rows
        i1_ref[...] = jnp.where(plane < ROW_CHUNKS * plane_rows, plane + d1, 0)
        i2_ref[...] = jnp.where(plane < ROW_CHUNKS * plane_rows, plane + d2, 0)
        carry_scr[...] += blk_cnt


def _moe_plan(eidx, *, tm, n_tiles, tb=PLAN_TB):
    t = eidx.shape[1]
    ntp = -(-n_tiles // 128) * 128
    return pl.pallas_call(
        functools.partial(_moe_plan_kernel, tb=tb, tm=tm, plane_rows=n_tiles * tm),
        grid=(2, t // tb),
        in_specs=[pl.BlockSpec((8, tb), lambda p, j: (0, j))],
        out_specs=[pl.BlockSpec((8, tb), lambda p, j: (0, j * p)),
                   pl.BlockSpec((8, tb), lambda p, j: (0, j * p)),
                   pl.BlockSpec((8, ntp), lambda p, j: (0, 0)),
                   pl.BlockSpec((8, 128), lambda p, j: (0, 0))],
        out_shape=[jax.ShapeDtypeStruct((8, t), jnp.int32),
                   jax.ShapeDtypeStruct((8, t), jnp.int32),
                   jax.ShapeDtypeStruct((8, ntp), jnp.int32),
                   jax.ShapeDtypeStruct((8, 128), jnp.int32)],
        scratch_shapes=[pltpu.VMEM((N_EXPERTS, 128), F32), pltpu.VMEM((N_EXPERTS, 128), F32)],
        compiler_params=_cparams("arbitrary", "arbitrary", vmem=VMEM_LIMIT_SMALL),
        name="moe_plan",
    )(eidx)


def _sc_mesh():
    return plsc.VectorSubcoreMesh(core_axis_name="c", subcore_axis_name="s",
                                  num_cores=SC_CORES, num_subcores=SC_SUBCORES)


def _sc_index_spec(tokens):
    nb = tokens // SC_WINDOW
    return pl.BlockSpec((1, SC_WINDOW), lambda i: (i // nb, i % nb))


def _sc_dispatch(rows, i1, i2, n_out):
    n = rows.shape[0]
    tokens = i1.shape[1]

    @functools.partial(pl.kernel, out_type=jax.ShapeDtypeStruct((n_out, 128), rows.dtype), mesh=_sc_mesh(),
                       name="moe_dispatch")
    def k(x_hbm, i1_hbm, i2_hbm, o_hbm):
        def body(x_vmem, i1_vmem, i2_vmem):
            pltpu.sync_copy(x_vmem, o_hbm.at[i1_vmem.at[0]])
            pltpu.sync_copy(x_vmem, o_hbm.at[i2_vmem.at[0]])

        pltpu.emit_pipeline(
            body, grid=(n // SC_WINDOW,),
            in_specs=[pl.BlockSpec((SC_WINDOW, 128), lambda i: (i, 0)),
                      _sc_index_spec(tokens), _sc_index_spec(tokens)],
            out_specs=[],
            core_axis_name=("c", "s"), dimension_semantics=(pltpu.PARALLEL,),
        )(x_hbm, i1_hbm, i2_hbm)

    return k(rows, i1, i2)


def _sc_collect(table, i1, i2):
    tokens = i1.shape[1]
    n = ROW_CHUNKS * tokens
    out = jax.ShapeDtypeStruct((n, 128), table.dtype)

    @functools.partial(pl.kernel, out_type=(out, out), mesh=_sc_mesh(), name="moe_collect",
                       scratch_types=[pltpu.SemaphoreType.DMA, pltpu.SemaphoreType.DMA])
    def k(t_hbm, i1_hbm, i2_hbm, o1_hbm, o2_hbm, sem1, sem2):
        def body(i1_vmem, i2_vmem, o1_vmem, o2_vmem):
            first = pltpu.async_copy(t_hbm.at[i1_vmem.at[0]], o1_vmem, sem1)
            second = pltpu.async_copy(t_hbm.at[i2_vmem.at[0]], o2_vmem, sem2)
            first.wait()
            second.wait()

        pltpu.emit_pipeline(
            body, grid=(n // SC_WINDOW,),
            in_specs=[_sc_index_spec(tokens), _sc_index_spec(tokens)],
            out_specs=[pl.BlockSpec((SC_WINDOW, 128), lambda i: (i, 0)),
                       pl.BlockSpec((SC_WINDOW, 128), lambda i: (i, 0))],
            core_axis_name=("c", "s"), dimension_semantics=(pltpu.PARALLEL,),
        )(i1_hbm, i2_hbm, o1_hbm, o2_hbm)

    return k(table, i1, i2)


def _experts_kernel(te_ref, na_ref, xs_ref, wg_ref, wu_ref, wd_ref, y_ref, wg_scr, wu_scr, wd_scr):
    i = pl.program_id(0)
    active = i < na_ref[0]

    @pl.when(active & ((i == 0) | (te_ref[i] != te_ref[jnp.maximum(i - 1, 0)])))
    def _():
        wg_scr[...] = wg_ref[0, 0].astype(BF16)
        wu_scr[...] = wu_ref[0, 0].astype(BF16)
        wd_scr[...] = wd_ref[0, 0].astype(BF16)

    @pl.when(active)
    def _():
        hi, lo = _unpack_bf16_pairs(_load_row_chunks(xs_ref))
        h = jnp.concatenate([hi, lo], axis=-1).astype(BF16)
        up = _dot(h, wg_scr[...])
        act = up * _sigmoid(up) * _dot(h, wu_scr[...])
        _store_row_chunks(y_ref, _pack_bf16_pairs(_dot(act.astype(BF16), wd_scr[...])))


def _experts(tile_expert, n_active, xs, wg, wu, wd, *, layer, tm):
    n_tiles = tile_expert.shape[0]
    _, _, d, dff = wg.shape
    rows = lambda i, te, na: (0, jnp.minimum(i, na[0] - 1), 0)
    expert = lambda i, te, na: (layer, te[i], 0, 0)
    return pl.pallas_call(
        _experts_kernel,
        grid_spec=pltpu.PrefetchScalarGridSpec(
            num_scalar_prefetch=2,
            grid=(n_tiles,),
            in_specs=[pl.BlockSpec((ROW_CHUNKS, tm, 128), rows),
                      pl.BlockSpec((1, 1, d, dff), expert),
                      pl.BlockSpec((1, 1, d, dff), expert),
                      pl.BlockSpec((1, 1, dff, d), expert)],
            out_specs=pl.BlockSpec((ROW_CHUNKS, tm, 128), rows),
            scratch_shapes=[pltpu.VMEM((d, dff), BF16), pltpu.VMEM((d, dff), BF16), pltpu.VMEM((dff, d), BF16)]),
        out_shape=jax.ShapeDtypeStruct(xs.shape, xs.dtype),
        compiler_params=_cparams("arbitrary"),
        name="moe_experts",
    )(tile_expert, n_active, xs, wg, wu, wd)


def _moe_combine_kernel(x_ref, y1_ref, y2_ref, w_ref, o_ref):
    half = x_ref.shape[1] // 2
    hi1, lo1 = _unpack_bf16_pairs(_load_row_chunks(y1_ref))
    hi2, lo2 = _unpack_bf16_pairs(_load_row_chunks(y2_ref))
    tm = x_ref.shape[0]
    w_cols = jnp.concatenate([w_ref[...], jnp.zeros((128 - w_ref.shape[0], tm), F32)], axis=0).T
    w1, w2 = w_cols[:, 0:1], w_cols[:, 1:2]
    o_ref[:, :half] = x_ref[:, :half] + w1 * hi1 + w2 * hi2
    o_ref[:, half:] = x_ref[:, half:] + w1 * lo1 + w2 * lo2


def _moe_combine(x2d, y1, y2, wts, *, tm):
    t, d = x2d.shape
    chunk_spec = pl.BlockSpec((ROW_CHUNKS, tm, 128), lambda i: (0, i, 0))
    return pl.pallas_call(
        _moe_combine_kernel,
        grid=(t // tm,),
        in_specs=[pl.BlockSpec((tm, d), lambda i: (i, 0)), chunk_spec, chunk_spec,
                  pl.BlockSpec((wts.shape[0], tm), lambda i: (0, i))],
        out_specs=pl.BlockSpec((tm, d), lambda i: (i, 0)),
        out_shape=jax.ShapeDtypeStruct((t, d), F32),
        compiler_params=_cparams("parallel", vmem=VMEM_LIMIT_SMALL),
        name="moe_combine",
    )(x2d, y1, y2, wts)


def _moe(x2d, hf_rows, eidx, wts, wg, wu, wd, *, layer):
    t = x2d.shape[0]
    tm = MOE_TM
    n_tiles = 2 * t // tm + N_EXPERTS
    plane = n_tiles * tm
    i1, i2, te, na = _moe_plan(eidx, tm=tm, n_tiles=n_tiles)
    xs = _sc_dispatch(hf_rows.reshape(ROW_CHUNKS * t, 128), i1, i2, ROW_CHUNKS * plane)
    ys = _experts(te[0, :n_tiles], na[0, :1], xs.reshape(ROW_CHUNKS, plane, 128), wg, wu, wd,
                  layer=layer, tm=tm)
    y1, y2 = _sc_collect(ys.reshape(ROW_CHUNKS * plane, 128), i1, i2)
    return _moe_combine(x2d, y1.reshape(ROW_CHUNKS, t, 128), y2.reshape(ROW_CHUNKS, t, 128), wts,
                        tm=COMBINE_TM)


W_ROWS = 256


def _w_rows_kernel(start_ref, valid_ref, w_ref, o_ref):
    del start_ref
    row = lax.broadcasted_iota(jnp.int32, w_ref.shape[1:], 0)
    o_ref[0] = jnp.where(row < valid_ref[pl.program_id(1)], w_ref[0], 0.0).astype(o_ref.dtype)


def _w_rows(w_t, starts, valid):
    depth, _, d = w_t.shape
    nblk = len(starts)
    return pl.pallas_call(
        _w_rows_kernel,
        grid_spec=pltpu.PrefetchScalarGridSpec(
            num_scalar_prefetch=2,
            grid=(depth, nblk),
            in_specs=[pl.BlockSpec((pl.Element(1), pl.Element(W_ROWS), pl.Element(d)),
                                   lambda l, c, st, va: (l, pl.multiple_of(st[c], 8), 0))],
            out_specs=pl.BlockSpec((1, W_ROWS, d), lambda l, c, st, va: (l, c, 0))),
        out_shape=jax.ShapeDtypeStruct((depth, nblk * W_ROWS, d), BF16),
        compiler_params=_cparams("parallel", "arbitrary", vmem=VMEM_LIMIT_SMALL),
        name="w_in_rows",
    )(jnp.asarray(starts, jnp.int32), jnp.asarray(valid, jnp.int32), w_t)


def _w_in_layout(w_in):
    w_t = jnp.swapaxes(w_in, 1, 2)
    src_if = 4 * MLSTM_W
    src_a = src_if + 2 * MLSTM_HEADS
    src_g = src_a + 3 * ATTN_W
    starts = list(range(0, src_if, W_ROWS)) + [src_g + k * W_ROWS for k in range((OFF_IF - OFF_GU) // W_ROWS)]
    valid = [W_ROWS] * len(starts)
    starts.append(src_if)
    valid.append(2 * MLSTM_HEADS)
    assert len(starts) * W_ROWS == N_PROJ and ATTN_GW == W_ROWS
    a_starts = [src_a + j * ATTN_W + g * ATTN_GW for g in range(len(ATTN_PATTERNS)) for j in range(3)]
    return _w_rows(w_t, starts, valid), _w_rows(w_t, a_starts, [W_ROWS] * len(a_starts))


def kernel(x, mem, norm_mix, w_in, mlstm_conv, mlstm_gate_b, mlstm_norm, attn_qk_norm, gmlp_norm, gmlp_ws,
           gmlp_bs, w_branch_a, w_branch_b, w_branch_c, w_out, rel_bias, norm_xattn, norm_mem, w_xq, w_xkv,
           xattn_qk_norm, w_xo, norm_ffn, router_w, router_b, w_expert_gate, w_expert_up, w_expert_down):
    b, s, d = x.shape
    t = b * s
    depth = w_in.shape[0]
    x2d = x.reshape(t, d)

    biases = [_attn_bias(rel_bias, g) for g in range(len(ATTN_PATTERNS))]
    rw_t = jnp.zeros((N_EXPERT_GROUPS, 8, d), F32).at[:, :EXPERTS_PER_GROUP].set(
        router_w.T.reshape(N_EXPERT_GROUPS, EXPERTS_PER_GROUP, d)).reshape(ROUTER_ROWS, d)
    rb = jnp.full((N_EXPERT_GROUPS, 8), NEG, F32).at[:, :EXPERTS_PER_GROUP].set(
        router_b.astype(F32).reshape(N_EXPERT_GROUPS, EXPERTS_PER_GROUP)).reshape(ROUTER_ROWS, 1)
    tril = jnp.tril(jnp.ones((GMLP_CHUNK, GMLP_CHUNK), bool))
    head_of = jnp.arange(ATTN_GW) // ATTN_DH
    seg_ones = (head_of[:, None] == head_of[None, :]).astype(BF16)

    w_main, w_attn = _w_in_layout(w_in)

    for l in range(depth):
        proj, h_mix, gates_t = _inproj(x2d, norm_mix[l][None], w_main, layer=l, tm=INPROJ_TM,
                                       tn=INPROJ_TN)
        gq = jnp.tile(attn_qk_norm[l, 0], HEADS_PER_GROUP)[None]
        gk = jnp.tile(attn_qk_norm[l, 1], HEADS_PER_GROUP)[None]

        nh = MLSTM_HEADS
        bias_i = jnp.zeros((8, 1), F32).at[:nh, 0].set(mlstm_gate_b[l, :nh])
        bias_f = jnp.zeros((8, 1), F32).at[:nh, 0].set(mlstm_gate_b[l, nh:])
        ya = _mlstm_rows(proj, gates_t, mlstm_conv[l], bias_i, bias_f, mlstm_norm[l][None],
                         batch=b, seq=s, blk=MLSTM_BLOCK, group=MLSTM_GROUP)

        ybs, lses = [], []
        for g, (_, dilation) in enumerate(ATTN_PATTERNS):
            aproj = _attnproj(h_mix, w_attn, seg_ones, gq, gk, layer=l, group=g, dilation=dilation)
            o, lse = _dattn(aproj, biases[g], seq=s, group=g, dilation=dilation)
            ybs.append(o)
            lses.append(lse)

        ws = jnp.where(tril, gmlp_ws[l], 0.0).astype(BF16)
        bsb = jnp.broadcast_to(gmlp_bs[l][:, :, None], (GMLP_GROUPS, GMLP_CHUNK, GMLP_GC)).astype(F32)
        x2d = _merge(ya, ybs, lses, proj, x2d, w_branch_a[l].astype(BF16), w_branch_b[l].astype(BF16),
                     w_branch_c[l].astype(BF16), w_out[l].astype(BF16), ws, bsb, gmlp_norm[l][None],
                     tm=MERGE_TM)

        k_mem, v_mem = _memkv(mem, norm_mem[l][None], w_xkv[l].astype(BF16), xattn_qk_norm[l, 1][None])
        x2d, hf_rows, eidx, wts = _xattn(x2d, k_mem, v_mem, norm_xattn[l][None], w_xq[l].astype(BF16),
                                         xattn_qk_norm[l, 0][None], w_xo[l].astype(BF16), norm_ffn[l][None],
                                         rw_t, rb, seq=s, tm=XATTN_TM)

        x2d = _moe(x2d, hf_rows, eidx, wts, w_expert_gate, w_expert_up, w_expert_down, layer=l)

    return x2d.reshape(b, s, d)
```

```python
import functools
import math

import jax
import jax.numpy as jnp
import numpy as np
from jax import lax
from jax.experimental import pallas as pl
from jax.experimental.pallas import tpu as pltpu
from jax.experimental.pallas import tpu_sc as plsc

F32 = jnp.float32
BF16 = jnp.bfloat16

EPS = 1e-6
NEG = -1e30

MLSTM_HEADS = 4
MLSTM_DH = 128
MLSTM_W = MLSTM_HEADS * MLSTM_DH
CONV_WIDTH = 4
MLSTM_BLOCK = 128
MLSTM_GROUP = 4

ATTN_PATTERNS = ((128, 1), (512, 4), (2048, 16))
HEADS_PER_GROUP = 4
ATTN_DH = 64
ATTN_GW = HEADS_PER_GROUP * ATTN_DH
ATTN_W = len(ATTN_PATTERNS) * ATTN_GW
ATTN_BLOCK = 128
REL_BUCKETS = 32
REL_MAX_DIST = 2048

GMLP_GROUPS = 4
GMLP_GC = 128
GMLP_W = GMLP_GROUPS * GMLP_GC
GMLP_CHUNK = 128

XATTN_HEADS = 4
XATTN_DH = 128
XATTN_W = XATTN_HEADS * XATTN_DH
XATTN_SUB = 1024

N_EXPERTS = 16
N_EXPERT_GROUPS = 4
EXPERTS_PER_GROUP = 4
ROUTER_ROWS = 8 * N_EXPERT_GROUPS

N_BRANCH = 3

MOE_TM = 1024
ROW_CHUNKS = 4
SC_CORES, SC_SUBCORES = 2, 16
SC_WINDOW = 128

OFF_MQ, OFF_MK, OFF_MV, OFF_MO = 0, 512, 1024, 1536
OFF_GU, OFF_GV = 2048, 2560
OFF_GATE = 3072
OFF_IF = 6144
IF_PAD = 256
N_PROJ = OFF_IF + IF_PAD

ATTN_TILE = 2048
ATTN_SUB = ATTN_TILE // ATTN_BLOCK
ATTN_SLAB = 2 * ATTN_DH
ATTN_COLS = HEADS_PER_GROUP * ATTN_SLAB + 2 * ATTN_GW

VMEM_LIMIT = 48 * 1024 * 1024
VMEM_LIMIT_INPROJ = 56 * 1024 * 1024
VMEM_LIMIT_SMALL = 24 * 1024 * 1024

INPROJ_TM, INPROJ_TN = 1024, 3072
ATTNPROJ_SUB = 512
MERGE_TM = 512
XATTN_TM = 1024
COMBINE_TM = 512
PLAN_TB = 1024


def _cparams(*sem, vmem=VMEM_LIMIT):
    return pltpu.CompilerParams(dimension_semantics=sem, vmem_limit_bytes=vmem)


def _rms(x, gain):
    return x * lax.rsqrt(jnp.mean(x * x, axis=-1, keepdims=True) + EPS) * gain


def _sigmoid(x):
    return 0.5 * jnp.tanh(0.5 * x) + 0.5


def _silu(x):
    half = 0.5 * x
    return half + half * jnp.tanh(half)


def _gelu(x):
    c = math.sqrt(2.0 / math.pi)
    half = 0.5 * x
    return half + half * jnp.tanh(x * (c + (c * 0.044715) * (x * x)))


def _dot(a, b):
    return jnp.dot(a, b, preferred_element_type=F32)


def _dot_nt(a, b):
    return lax.dot_general(a, b, (((1,), (1,)), ((), ())), preferred_element_type=F32)


def _inproj_kernel(x_ref, g_ref, w_ref, wg_ref, o_ref, h_ref, gt_ref):
    @pl.when(pl.program_id(1) == 0)
    def _():
        h = _rms(x_ref[...], g_ref[...]).astype(BF16)
        h_ref[...] = h
        gt_ref[...] = _dot_nt(wg_ref[0, 0:128, :], h)[:gt_ref.shape[0], :]

    o_ref[...] = _dot_nt(h_ref[...], w_ref[0]).astype(o_ref.dtype)


def _inproj(x2d, gain, w, *, layer, tm, tn):
    t, d = x2d.shape
    n = OFF_IF
    return pl.pallas_call(
        _inproj_kernel,
        grid=(t // tm, n // tn),
        in_specs=[pl.BlockSpec((tm, d), lambda i, j: (i, 0)),
                  pl.BlockSpec((1, d), lambda i, j: (0, 0)),
                  pl.BlockSpec((1, tn, d), lambda i, j: (layer, j, 0)),
                  pl.BlockSpec((1, IF_PAD, d), lambda i, j: (layer, OFF_IF // IF_PAD, 0))],
        out_specs=[pl.BlockSpec((tm, tn), lambda i, j: (i, j)),
                   pl.BlockSpec((tm, d), lambda i, j: (i, 0)),
                   pl.BlockSpec((8, tm), lambda i, j: (0, i))],
        out_shape=[jax.ShapeDtypeStruct((t, n), BF16), jax.ShapeDtypeStruct((t, d), BF16),
                   jax.ShapeDtypeStruct((8, t), F32)],
        compiler_params=_cparams("parallel", "arbitrary", vmem=VMEM_LIMIT_INPROJ),
        name="inproj",
    )(x2d, gain, w, w)


def _log_sigmoid(x):
    return jnp.minimum(x, 0.0) - jnp.log(1.0 + jnp.exp(-jnp.abs(x)))


def _split_bf16(x):
    hi = x.astype(BF16)
    return hi, (x - hi.astype(F32)).astype(BF16)


def _prefix_max(x):
    n = x.shape[1]
    lane = lax.broadcasted_iota(jnp.int32, x.shape, 1)
    shift = 1
    while shift < n:
        x = jnp.maximum(x, jnp.where(lane >= shift, pltpu.roll(x, shift, 1), NEG))
        shift *= 2
    return x


def _mlstm_rows_kernel(qk_ref, v_ref, og_ref, *rest, blk, group):
    gate_refs = rest[:group]
    cw_ref, bi_ref, bf_ref, ng_ref, y_ref, xe_scr, s_scr, m_scr = rest[group:]
    heads, dh, w = MLSTM_HEADS, MLSTM_DH, MLSTM_W

    @pl.when(pl.program_id(1) == 0)
    def _():
        xe_scr[:, 0:8, :] = jnp.zeros((group, 8, 2 * w), F32)
        s_scr[...] = jnp.zeros_like(s_scr)
        m_scr[...] = jnp.zeros_like(m_scr)

    cw = cw_ref[...]
    causal = lax.broadcasted_iota(jnp.int32, (blk, blk), 0) >= lax.broadcasted_iota(jnp.int32, (blk, blk), 1)
    triu = (lax.broadcasted_iota(jnp.int32, (blk, blk), 0)
            <= lax.broadcasted_iota(jnp.int32, (blk, blk), 1)).astype(BF16)
    ones = jnp.ones((blk, dh), BF16)
    s_in = [[s_scr[g, h] for h in range(heads)] for g in range(group)]
    m_in = [m_scr[g, :, 0:1] for g in range(group)]
    s_out = [[None] * heads for _ in range(group)]
    m_out = [None] * group
    per_seq = []
    for g in range(group):
        xe_scr[g, 8:8 + blk, :] = qk_ref[g].astype(F32)
        conv = cw[CONV_WIDTH - 1:CONV_WIDTH, :] * xe_scr[g, 8:8 + blk, :]
        for j in range(CONV_WIDTH - 1):
            off = 8 - (CONV_WIDTH - 1) + j
            conv = conv + cw[j:j + 1, :] * xe_scr[g, off:off + blk, :]
        xe_scr[g, 0:8, :] = xe_scr[g, blk:blk + 8, :]
        qk = _silu(conv)

        gates = gate_refs[g][...]
        i_r = gates + bi_ref[...]
        lf_hi, lf_lo = _split_bf16(_log_sigmoid(pltpu.roll(gates, heads, 0) + bf_ref[...]))
        b_r = _dot(lf_hi, triu) + _dot(lf_lo, triu)
        m_st = m_in[g]
        a_r = i_r - b_r
        inter = b_r + m_st
        m_t = jnp.maximum(inter, b_r + _prefix_max(a_r))
        b_last = b_r[:, blk - 1:blk]
        dec = b_last - b_r + i_r
        m_new = jnp.maximum(b_last + m_st, jnp.max(dec, axis=1, keepdims=True))
        w_c = jnp.exp(b_last + m_st - m_new)
        m_out[g] = m_new
        pack = jnp.concatenate([b_r - m_t, jnp.exp(inter - m_t), jnp.exp(-m_t), jnp.exp(dec - m_new),
                                jnp.zeros((blk - 32, blk), F32)], axis=0)
        per_seq.append((qk, a_r, pack.T, w_c))

    chains = [(g, h) for h in range(heads) for g in range(group)]
    st = {}
    for g, h in chains:
        qk = per_seq[g][0]
        sl = slice(h * dh, (h + 1) * dh)
        q_b = qk[:, sl].astype(BF16)
        k_f = qk[:, w + h * dh:w + (h + 1) * dh] * (dh ** -0.5)
        v_ext = jnp.concatenate([v_ref[g, :, sl], ones], axis=-1)
        st[g, h] = (q_b, k_f, v_ext, _dot_nt(q_b, k_f.astype(BF16)), _dot(q_b, s_in[g][h].astype(BF16)))
    for g, h in chains:
        q_b, k_f, v_ext, qk_t, q_state = st[g, h]
        _, a_r, cols, _ = per_seq[g]
        u_c, w_inter = cols[:, h:h + 1], cols[:, 8 + h:9 + h]
        w_intra = jnp.exp(jnp.where(causal, u_c + a_r[h:h + 1, :], NEG))
        st[g, h] = (k_f, v_ext, _dot((qk_t * w_intra).astype(BF16), v_ext) + w_inter * q_state)
    for g, h in chains:
        k_f, v_ext, tot = st[g, h]
        _, _, cols, w_c = per_seq[g]
        em_c, w_k = cols[:, 16 + h:17 + h], cols[:, 24 + h:25 + h]
        sl = slice(h * dh, (h + 1) * dh)
        num, den = tot[:, :dh], tot[:, dh:]
        hh = num / jnp.maximum(jnp.abs(den), em_c)
        hn = _rms(hh, ng_ref[:, sl])
        y_ref[g, :, sl] = (hn * _sigmoid(og_ref[g, :, sl].astype(F32))).astype(y_ref.dtype)
        s_out[g][h] = w_c[h:h + 1, :] * s_in[g][h] + _dot((k_f * w_k).T.astype(BF16), v_ext)
    for g in range(group):
        m_scr[g] = jnp.broadcast_to(m_out[g], m_scr.shape[1:])
        for h in range(heads):
            s_scr[g, h] = s_out[g][h]


def _mlstm_rows(proj, gates_t, conv_w, bias_i, bias_f, norm_g, *, batch, seq, blk, group):
    t, npj = proj.shape
    w = MLSTM_W
    proj3 = proj.reshape(batch, seq, npj)
    cols = lambda c: (lambda b, i: (b, i, c))
    const2 = lambda b, i: (0, 0)
    nblk = seq // blk
    gate_specs = [pl.BlockSpec((8, blk), functools.partial(lambda b, i, g: (0, (b * group + g) * nblk + i), g=g))
                  for g in range(group)]
    y = pl.pallas_call(
        functools.partial(_mlstm_rows_kernel, blk=blk, group=group),
        grid=(batch // group, seq // blk),
        in_specs=[pl.BlockSpec((group, blk, 2 * w), cols(OFF_MQ // (2 * w))),
                  pl.BlockSpec((group, blk, w), cols(OFF_MV // w)),
                  pl.BlockSpec((group, blk, w), cols(OFF_MO // w)),
                  *gate_specs,
                  pl.BlockSpec((CONV_WIDTH, 2 * w), const2),
                  pl.BlockSpec((8, 1), const2), pl.BlockSpec((8, 1), const2),
                  pl.BlockSpec((1, w), const2)],
        out_specs=pl.BlockSpec((group, blk, w), cols(0)),
        out_shape=jax.ShapeDtypeStruct((batch, seq, w), BF16),
        scratch_shapes=[pltpu.VMEM((group, blk + 8, 2 * w), F32),
                        pltpu.VMEM((group, MLSTM_HEADS, MLSTM_DH, 2 * MLSTM_DH), F32),
                        pltpu.VMEM((group, 8, 128), F32)],
        compiler_params=_cparams("parallel", "arbitrary", vmem=VMEM_LIMIT_SMALL),
        name="mlstm",
    )(proj3, proj3, proj3, *([gates_t] * group), conv_w, bias_i, bias_f, norm_g)
    return y.reshape(t, w)


def _attnproj_kernel(h_ref, w_ref, seg_ref, gq_ref, gk_ref, o_ref, r_scr, *, dil):
    gw, half = ATTN_GW, ATTN_SLAB // 2
    sub_rows = ATTNPROJ_SUB
    seg, sub_seg = ATTN_TILE // dil, sub_rows // dil

    def head_norm(x, gain):
        ss = _dot((x * x).astype(BF16), seg_ref[...])
        return x * lax.rsqrt(ss * (1.0 / ATTN_DH) + EPS) * gain

    low = lax.broadcasted_iota(jnp.int32, (1, ATTN_SLAB), 1) < half
    for s in range(ATTN_TILE // sub_rows):
        rows = slice(s * sub_rows, (s + 1) * sub_rows)
        res = _dot_nt(h_ref[rows, :], w_ref[0])
        q = head_norm(res[:, :gw], gq_ref[...]) * (ATTN_DH ** -0.5)
        k = head_norm(res[:, gw:2 * gw], gk_ref[...])
        slabs = []
        for pair in range(gw // ATTN_SLAB):
            qp = q[:, pair * ATTN_SLAB:(pair + 1) * ATTN_SLAB]
            slabs += [jnp.where(low, qp, 0.0), jnp.where(low, 0.0, qp)]
        slabs += [k[:, c * 128:(c + 1) * 128] for c in range(gw // 128)]
        slabs += [res[:, 2 * gw + c * 128:2 * gw + (c + 1) * 128] for c in range(gw // 128)]
        pitch = dil + 1 if dil % 16 == 0 else dil
        for c, slab in enumerate(slabs):
            if dil == 1:
                o_ref[rows, c * 128:(c + 1) * 128] = slab.astype(o_ref.dtype)
            elif pitch == dil:
                r_scr[s % 2, c, 0:sub_rows, :] = slab
            else:
                for i in range(sub_seg):
                    r_scr[s % 2, c, pitch * i:pitch * i + dil, :] = slab[dil * i:dil * (i + 1), :]
        if dil > 1:
            for r in range(dil):
                dst = slice(r * seg + s * sub_seg, r * seg + (s + 1) * sub_seg)
                for c in range(r_scr.shape[1]):
                    o_ref[dst, c * 128:(c + 1) * 128] = (
                        r_scr[s % 2, c, pl.ds(r, sub_seg, stride=pitch), :].astype(o_ref.dtype))


def _attnproj(h, w, seg_ones, gq, gk, *, layer, group, dilation):
    t, d = h.shape
    wcols = 3 * ATTN_GW
    const2 = lambda i: (0, 0)
    return pl.pallas_call(
        functools.partial(_attnproj_kernel, dil=dilation),
        grid=(t // ATTN_TILE,),
        in_specs=[pl.BlockSpec((ATTN_TILE, d), lambda i: (i, 0)),
                  pl.BlockSpec((1, wcols, d), lambda i: (layer, group, 0)),
                  pl.BlockSpec((ATTN_GW, ATTN_GW), const2),
                  pl.BlockSpec((1, ATTN_GW), const2), pl.BlockSpec((1, ATTN_GW), const2)],
        out_specs=pl.BlockSpec((ATTN_TILE, ATTN_COLS), lambda i: (i, 0)),
        out_shape=jax.ShapeDtypeStruct((t, ATTN_COLS), BF16),
        scratch_shapes=[pltpu.VMEM((2, ATTN_COLS // 128, ATTNPROJ_SUB + ATTNPROJ_SUB // 16, 128), F32)],
        compiler_params=_cparams("parallel"),
        name=f"attnproj{group}",
    )(h, w, seg_ones, gq, gk)


def _dattn_kernel(q_ref, kc_ref, kp_ref, vc_ref, vp_ref, bias_ref, o_ref, lse_ref,
                  kx_scr, vx_scr, o_scr, l_scr, *, dil):
    blk = ATTN_BLOCK
    per = ATTN_SUB // dil
    pitch = dil + 1 if dil % 16 == 0 else dil
    first_tile = pl.program_id(1) == 0
    for r in range(dil):
        base = r * (per + 1) * blk
        last = slice((r * per + per - 1) * blk, (r * per + per) * blk)
        mine = slice(r * per * blk, (r + 1) * per * blk)
        kx_scr[base:base + blk, :] = kp_ref[last, :]
        vx_scr[base:base + blk, :] = vp_ref[last, :]
        kx_scr[base + blk:base + (per + 1) * blk, :] = kc_ref[mine, :]
        vx_scr[base + blk:base + (per + 1) * blk, :] = vc_ref[mine, :]

    low = lax.broadcasted_iota(jnp.int32, (1, ATTN_SLAB), 1) < ATTN_SLAB // 2
    no_prev = lax.broadcasted_iota(jnp.int32, (1, 2 * blk), 1) < blk
    for r in range(dil):
        for sub in range(per):
            u = r * per + sub
            win = slice((r * (per + 1) + sub) * blk, (r * (per + 1) + sub + 2) * blk)
            o_slabs, l_slabs = [], []
            for pair in range(ATTN_GW // ATTN_SLAB):
                cols = slice(pair * ATTN_SLAB, (pair + 1) * ATTN_SLAB)
                kx, vx = kx_scr[win, cols], vx_scr[win, cols]
                o_pair, l_pair = [], []
                for h in (2 * pair, 2 * pair + 1):
                    logits = _dot_nt(q_ref[u * blk:(u + 1) * blk, h * ATTN_SLAB:(h + 1) * ATTN_SLAB], kx)
                    logits = logits + bias_ref[h]
                    if sub == 0:
                        logits = jnp.where(first_tile & no_prev, NEG, logits)
                    m = jnp.max(logits, axis=-1, keepdims=True)
                    p = jnp.exp(logits - m)
                    l = jnp.sum(p, axis=-1, keepdims=True)
                    o_pair.append(_dot(p.astype(BF16), vx) / l)
                    l_pair.append(m + jnp.log(l))
                o_slabs.append(jnp.where(low, o_pair[0], o_pair[1]))
                l_slabs.append(jnp.where(low, l_pair[0], l_pair[1]))
            for c in range(ATTN_GW // ATTN_SLAB):
                cols = slice(c * ATTN_SLAB, (c + 1) * ATTN_SLAB)
                if dil == 1:
                    o_ref[u * blk:(u + 1) * blk, cols] = o_slabs[c].astype(o_ref.dtype)
                    lse_ref[u * blk:(u + 1) * blk, cols] = l_slabs[c]
                else:
                    dst = pl.ds(sub * blk * pitch + r, blk, stride=pitch)
                    o_scr[c, dst, :] = o_slabs[c]
                    l_scr[c, dst, :] = l_slabs[c]
    if dil > 1:
        for c in range(ATTN_GW // ATTN_SLAB):
            cols = slice(c * ATTN_SLAB, (c + 1) * ATTN_SLAB)
            if pitch == dil:
                o_ref[:, cols] = o_scr[c, 0:ATTN_TILE, :].astype(o_ref.dtype)
                lse_ref[:, cols] = l_scr[c, 0:ATTN_TILE, :]
            else:
                for i in range(ATTN_TILE // dil):
                    o_ref[dil * i:dil * (i + 1), cols] = o_scr[c, pitch * i:pitch * i + dil, :].astype(o_ref.dtype)
                    lse_ref[dil * i:dil * (i + 1), cols] = l_scr[c, pitch * i:pitch * i + dil, :]


def _dattn(aproj, bias, *, seq, group, dilation):
    t = aproj.shape[0]
    tiles = seq // ATTN_TILE
    qw = HEADS_PER_GROUP * ATTN_SLAB
    cq, ck, cv = 0, qw // ATTN_GW, qw // ATTN_GW + 1
    blk = (ATTN_TILE, ATTN_GW)
    cur = lambda c: (lambda b, j: (b * tiles + j, c))
    prev = lambda c: (lambda b, j: (b * tiles + jnp.maximum(j - 1, 0), c))
    xrows = ATTN_TILE + dilation * ATTN_BLOCK
    return pl.pallas_call(
        functools.partial(_dattn_kernel, dil=dilation),
        grid=(t // seq, tiles),
        in_specs=[pl.BlockSpec((ATTN_TILE, qw), cur(cq)),
                  pl.BlockSpec(blk, cur(ck)), pl.BlockSpec(blk, prev(ck)),
                  pl.BlockSpec(blk, cur(cv)), pl.BlockSpec(blk, prev(cv)),
                  pl.BlockSpec((HEADS_PER_GROUP, ATTN_BLOCK, 2 * ATTN_BLOCK), lambda b, j: (0, 0, 0))],
        out_specs=[pl.BlockSpec(blk, cur(0)), pl.BlockSpec(blk, cur(0))],
        out_shape=[jax.ShapeDtypeStruct((t, ATTN_GW), BF16), jax.ShapeDtypeStruct((t, ATTN_GW), F32)],
        scratch_shapes=[pltpu.VMEM((xrows, ATTN_GW), BF16), pltpu.VMEM((xrows, ATTN_GW), BF16),
                        pltpu.VMEM((ATTN_GW // ATTN_SLAB, ATTN_TILE + ATTN_TILE // 16, ATTN_SLAB), F32),
                        pltpu.VMEM((ATTN_GW // ATTN_SLAB, ATTN_TILE + ATTN_TILE // 16, ATTN_SLAB), F32)],
        compiler_params=_cparams("parallel", "arbitrary"),
        name=f"dattn{group}",
    )(aproj, aproj, aproj, aproj, aproj, bias)


def _rel_bucket(n):
    max_exact = REL_BUCKETS // 2
    nf = jnp.maximum(n, 1).astype(F32)
    log_b = max_exact + (jnp.log(nf / max_exact) / math.log(REL_MAX_DIST / max_exact)
                         * (REL_BUCKETS - max_exact)).astype(jnp.int32)
    return jnp.where(n < max_exact, n, jnp.minimum(log_b, REL_BUCKETS - 1))


def _attn_bias(rel_bias, group):
    window, dilation = ATTN_PATTERNS[group]
    steps = window // dilation
    hp = lax.Precision.HIGHEST
    hs = slice(group * HEADS_PER_GROUP, (group + 1) * HEADS_PER_GROUP)
    bucket = _rel_bucket(jnp.arange(steps + 1) * dilation)
    bias_steps = jnp.dot(jax.nn.one_hot(bucket, REL_BUCKETS, dtype=F32), rel_bias[:, hs].astype(F32),
                         precision=hp)
    qi = jnp.arange(ATTN_BLOCK)[:, None]
    ki = jnp.arange(2 * ATTN_BLOCK)[None, :]
    dist = ATTN_BLOCK + qi - ki
    ok = (dist >= 0) & (dist <= steps)
    sel = jax.nn.one_hot(jnp.clip(dist, 0, steps).reshape(-1), steps + 1, dtype=F32)
    bias = jnp.dot(sel, bias_steps, precision=hp).T.reshape(HEADS_PER_GROUP, ATTN_BLOCK, 2 * ATTN_BLOCK)
    return jnp.where(ok[None], bias, NEG)


def _merge_kernel(ya_ref, yb0_ref, yb1_ref, yb2_ref, l0_ref, l1_ref, l2_ref, gu_ref, gv_ref, gate_ref,
                  x_ref, wa_ref, wb_ref, wc_ref, wo_ref, ws_ref, bs_ref, gg_ref, o_ref, yc_scr, *, tm):
    d = x_ref.shape[1]
    l0, l1, l2 = l0_ref[...], l1_ref[...], l2_ref[...]
    mx = jnp.maximum(jnp.maximum(l0, l1), l2)
    e0, e1, e2 = jnp.exp(l0 - mx), jnp.exp(l1 - mx), jnp.exp(l2 - mx)
    inv = 1.0 / (e0 + e1 + e2)
    yb = jnp.concatenate([(yb0_ref[...].astype(F32) * (e0 * inv)).astype(BF16),
                          (yb1_ref[...].astype(F32) * (e1 * inv)).astype(BF16),
                          (yb2_ref[...].astype(F32) * (e2 * inv)).astype(BF16)], axis=-1)

    for j in range(tm // GMLP_CHUNK):
        rows = slice(j * GMLP_CHUNK, (j + 1) * GMLP_CHUNK)
        for g in range(GMLP_GROUPS):
            cols = slice(g * GMLP_GC, (g + 1) * GMLP_GC)
            u = _gelu(gu_ref[rows, cols].astype(F32))
            v = _rms(_gelu(gv_ref[rows, cols].astype(F32)), gg_ref[:, cols])
            mixed = _dot(ws_ref[g], v.astype(BF16)) + bs_ref[g]
            yc_scr[rows, cols] = (u * mixed).astype(BF16)

    def gate2(k):
        return jnp.tanh(0.5 * gate_ref[:, k * d:(k + 1) * d].astype(F32)) + 1.0

    merged2 = gate2(0) * _dot(ya_ref[...], wa_ref[...])
    merged2 = merged2 + gate2(1) * _dot(yb, wb_ref[...])
    merged2 = merged2 + gate2(2) * _dot(yc_scr[...], wc_ref[...])
    o_ref[...] = x_ref[...] + 0.5 * _dot(merged2.astype(BF16), wo_ref[...])


def _merge(ya, ybs, lses, proj, x2d, wa, wb, wc, wo, ws, bsb, gg, *, tm):
    t, d = x2d.shape
    row = lambda c: (lambda i: (i, c))
    full2 = lambda i: (0, 0)
    full3 = lambda i: (0, 0, 0)
    gspec = pl.BlockSpec((tm, ATTN_GW), row(0))
    return pl.pallas_call(
        functools.partial(_merge_kernel, tm=tm),
        grid=(t // tm,),
        in_specs=[pl.BlockSpec((tm, MLSTM_W), row(0)),
                  gspec, gspec, gspec, gspec, gspec, gspec,
                  pl.BlockSpec((tm, GMLP_W), row(OFF_GU // GMLP_W)),
                  pl.BlockSpec((tm, GMLP_W), row(OFF_GV // GMLP_W)),
                  pl.BlockSpec((tm, N_BRANCH * d), row(OFF_GATE // (N_BRANCH * d))),
                  pl.BlockSpec((tm, d), row(0)),
                  pl.BlockSpec(wa.shape, full2), pl.BlockSpec(wb.shape, full2),
                  pl.BlockSpec(wc.shape, full2), pl.BlockSpec(wo.shape, full2),
                  pl.BlockSpec(ws.shape, full3), pl.BlockSpec(bsb.shape, full3),
                  pl.BlockSpec(gg.shape, full2)],
        out_specs=pl.BlockSpec((tm, d), row(0)),
        out_shape=jax.ShapeDtypeStruct((t, d), F32),
        scratch_shapes=[pltpu.VMEM((tm, GMLP_W), BF16)],
        compiler_params=_cparams("parallel"),
        name="merge",
    )(ya, *ybs, *lses, proj, proj, proj, x2d, wa, wb, wc, wo, ws, bsb, gg)


def _memkv_kernel(mem_ref, g_ref, w_ref, gk_ref, k_ref, v_ref):
    dh, w = XATTN_DH, XATTN_W
    kv = _dot(_rms(mem_ref[0], g_ref[...]).astype(BF16), w_ref[...])
    for h in range(XATTN_HEADS):
        sl = slice(h * dh, (h + 1) * dh)
        k_ref[0, :, sl] = _rms(kv[:, sl], gk_ref[...]).astype(k_ref.dtype)
    v_ref[0] = kv[:, w:].astype(v_ref.dtype)


def _memkv(mem, gain, w_kv, gk):
    b, m, d = mem.shape
    full2 = lambda i: (0, 0)
    return pl.pallas_call(
        _memkv_kernel,
        grid=(b,),
        in_specs=[pl.BlockSpec((1, m, d), lambda i: (i, 0, 0)),
                  pl.BlockSpec((1, d), full2),
                  pl.BlockSpec(w_kv.shape, full2),
                  pl.BlockSpec((1, XATTN_DH), full2)],
        out_specs=[pl.BlockSpec((1, m, XATTN_W), lambda i: (i, 0, 0)),
                   pl.BlockSpec((1, m, XATTN_W), lambda i: (i, 0, 0))],
        out_shape=[jax.ShapeDtypeStruct((b, m, XATTN_W), BF16),
                   jax.ShapeDtypeStruct((b, m, XATTN_W), BF16)],
        compiler_params=_cparams("parallel", vmem=VMEM_LIMIT_SMALL),
        name="memkv",
    )(mem, gain, w_kv, gk)


def _route(logits):
    tm = logits.shape[1]
    e = jnp.exp(logits - jnp.max(logits, axis=0, keepdims=True))
    probs = e / jnp.sum(e, axis=0, keepdims=True)
    rowi = lax.broadcasted_iota(jnp.int32, (8, tm), 0)
    real = rowi < EXPERTS_PER_GROUP
    tops = []
    for g in range(N_EXPERT_GROUPS):
        pg = jnp.where(real, probs[8 * g:8 * g + 8, :], -0.5)
        m1 = jnp.max(pg, axis=0, keepdims=True)
        i1 = jnp.min(jnp.where(pg == m1, rowi, 8), axis=0, keepdims=True)
        pg2 = jnp.where(rowi == i1, -1.0, pg)
        m2 = jnp.max(pg2, axis=0, keepdims=True)
        i2 = jnp.min(jnp.where(pg2 == m2, rowi, 8), axis=0, keepdims=True)
        tops.append((m1, i1, m2, i2))
    best = jnp.zeros((1, tm), jnp.int32)
    best_score = tops[0][0] + tops[0][2]
    for g in range(1, N_EXPERT_GROUPS):
        score = tops[g][0] + tops[g][2]
        better = score > best_score
        best = jnp.where(better, g, best)
        best_score = jnp.where(better, score, best_score)
    m1, i1, m2, i2 = tops[0]
    for g in range(1, N_EXPERT_GROUPS):
        m1, i1, m2, i2 = (jnp.where(best == g, new, old) for new, old in zip(tops[g], (m1, i1, m2, i2)))
    tot = m1 + m2
    base = best * EXPERTS_PER_GROUP
    return base + i1, base + i2, m1 / tot, m2 / tot


def _pack_bf16_pairs(x):
    n = x.shape[1] // 2
    hi = lax.bitcast_convert_type(x[:, :n].astype(BF16).astype(F32), jnp.uint32)
    lo = lax.bitcast_convert_type(x[:, n:].astype(BF16).astype(F32), jnp.uint32)
    return hi | (lo >> 16)


def _unpack_bf16_pairs(p):
    hi = lax.bitcast_convert_type(p & jnp.uint32(0xFFFF0000), F32)
    lo = lax.bitcast_convert_type(p << 16, F32)
    return hi, lo


def _store_row_chunks(ref, packed):
    for j in range(ROW_CHUNKS):
        ref[j] = packed[:, j * 128:(j + 1) * 128]


def _load_row_chunks(ref):
    return jnp.concatenate([ref[j] for j in range(ROW_CHUNKS)], axis=-1)


def _xattn_kernel(x_ref, k_ref, v_ref, gx_ref, wq_ref, gq_ref, wo_ref, gf_ref, rw_ref, rb_ref,
                  xo_ref, hf_ref, eidx_ref, wts_ref, *, sub):
    dh = XATTN_DH
    rw = rw_ref[...]
    rw_hi, rw_lo = _split_bf16(rw)
    for s in range(x_ref.shape[0] // sub):
        rows = slice(s * sub, (s + 1) * sub)
        x = x_ref[rows, :]
        q = _dot(_rms(x, gx_ref[...]).astype(BF16), wq_ref[...])
        outs = []
        for h in range(XATTN_HEADS):
            sl = slice(h * dh, (h + 1) * dh)
            q_h = (_rms(q[:, sl], gq_ref[...]) * (dh ** -0.5)).astype(BF16)
            logits = _dot_nt(q_h, k_ref[0, :, sl])
            p = jnp.exp(logits - jnp.max(logits, axis=-1, keepdims=True))
            o = _dot(p.astype(BF16), v_ref[0, :, sl]) / jnp.sum(p, axis=-1, keepdims=True)
            outs.append(o.astype(BF16))
        xn = x + _dot(jnp.concatenate(outs, axis=-1), wo_ref[...])
        xo_ref[rows, :] = xn
        hf = _rms(xn, gf_ref[...])
        packed = _pack_bf16_pairs(hf)
        for j in range(ROW_CHUNKS):
            hf_ref[j, rows, :] = packed[:, j * 128:(j + 1) * 128]
        hf_hi, hf_lo = _split_bf16(hf)
        logits_t = _dot_nt(rw_hi, hf_hi) + _dot_nt(rw_hi, hf_lo) + _dot_nt(rw_lo, hf_hi) + rb_ref[...]
        e1, e2, w1, w2 = _route(logits_t)
        eidx_ref[:, rows] = jnp.concatenate([e1, e2, jnp.zeros((6, sub), jnp.int32)], axis=0)
        wts_ref[:, rows] = jnp.concatenate([w1, w2, jnp.zeros((6, sub), F32)], axis=0)


def _xattn(x2d, k, v, gx, wq, gq, wo, gf, rw_t, rb, *, seq, tm):
    t, d = x2d.shape
    per_b = seq // tm
    full2 = lambda i: (0, 0)
    kv_spec = pl.BlockSpec((1,) + k.shape[1:], lambda i: (i // per_b, 0, 0))
    return pl.pallas_call(
        functools.partial(_xattn_kernel, sub=min(tm, XATTN_SUB)),
        grid=(t // tm,),
        in_specs=[pl.BlockSpec((tm, d), lambda i: (i, 0)), kv_spec, kv_spec,
                  pl.BlockSpec((1, d), full2), pl.BlockSpec(wq.shape, full2),
                  pl.BlockSpec((1, XATTN_DH), full2), pl.BlockSpec(wo.shape, full2),
                  pl.BlockSpec((1, d), full2), pl.BlockSpec(rw_t.shape, full2),
                  pl.BlockSpec(rb.shape, full2)],
        out_specs=[pl.BlockSpec((tm, d), lambda i: (i, 0)),
                   pl.BlockSpec((ROW_CHUNKS, tm, 128), lambda i: (0, i, 0)),
                   pl.BlockSpec((8, tm), lambda i: (0, i)),
                   pl.BlockSpec((8, tm), lambda i: (0, i))],
        out_shape=[jax.ShapeDtypeStruct((t, d), F32),
                   jax.ShapeDtypeStruct((ROW_CHUNKS, t, 128), jnp.uint32),
                   jax.ShapeDtypeStruct((8, t), jnp.int32),
                   jax.ShapeDtypeStruct((8, t), F32)],
        compiler_params=_cparams("parallel"),
        name="xattn_router",
    )(x2d, k, v, gx, wq, gq, wo, gf, rw_t, rb)


def _moe_plan_kernel(eidx_ref, i1_ref, i2_ref, te_ref, na_ref, cnt_scr, carry_scr, *, tb, tm, plane_rows):
    ne = N_EXPERTS
    hp = lax.Precision.HIGHEST
    phase, j = pl.program_id(0), pl.program_id(1)
    rows = lax.broadcasted_iota(jnp.int32, (ne, tb), 0)
    oh1 = rows == eidx_ref[0:1, :]
    oh2 = rows == eidx_ref[1:2, :]
    a = oh1.astype(F32) + oh2.astype(F32)
    blk_cnt = jnp.broadcast_to(jnp.sum(a, axis=1, keepdims=True), cnt_scr.shape)

    @pl.when((phase == 0) & (j == 0))
    def _():
        cnt_scr[...] = jnp.zeros_like(cnt_scr)

    @pl.when(phase == 0)
    def _():
        cnt_scr[...] += blk_cnt

    @pl.when((phase == 1) & (j == 0))
    def _():
        padded = jnp.ceil(cnt_scr[...] * (1.0 / tm)) * tm
        er = lax.broadcasted_iota(jnp.int32, (ne, ne), 0)
        ec = lax.broadcasted_iota(jnp.int32, (ne, ne), 1)
        off = jnp.dot((ec < er).astype(F32), padded, precision=hp, preferred_element_type=F32)
        carry_scr[...] = off
        seg_end = (off + padded)[:, 0:1]
        tile_start = lax.broadcasted_iota(jnp.int32, (ne, te_ref.shape[1]), 1).astype(F32) * tm
        te = jnp.sum((seg_end <= tile_start).astype(F32), axis=0, keepdims=True)
        te_ref[...] = jnp.broadcast_to(jnp.minimum(te, ne - 1.0), te_ref.shape).astype(jnp.int32)
        total = jnp.sum(padded[:, 0:1], axis=0, keepdims=True)
        na_ref[...] = jnp.broadcast_to(total * (1.0 / tm), na_ref.shape).astype(jnp.int32)

    @pl.when(phase == 1)
    def _():
        before = (lax.broadcasted_iota(jnp.int32, (tb, tb), 0)
                  < lax.broadcasted_iota(jnp.int32, (tb, tb), 1)).astype(BF16)
        rank = carry_scr[:, 0:1] + _dot(a.astype(BF16), before)
        d1 = jnp.sum(jnp.where(oh1, rank, 0.0), axis=0, keepdims=True).astype(jnp.int32)
        d2 = jnp.sum(jnp.where(oh2, rank, 0.0), axis=0, keepdims=True).astype(jnp.int32)
        plane = lax.broadcasted_iota(jnp.int32, (8, tb), 0) * plane_rows
        i1_ref[...] = jnp.where(plane < ROW_CHUNKS * plane_rows, plane + d1, 0)
        i2_ref[...] = jnp.where(plane < ROW_CHUNKS * plane_rows, plane + d2, 0)
        carry_scr[...] += blk_cnt


def _moe_plan(eidx, *, tm, n_tiles, tb=PLAN_TB):
    t = eidx.shape[1]
    ntp = -(-n_tiles // 128) * 128
    return pl.pallas_call(
        functools.partial(_moe_plan_kernel, tb=tb, tm=tm, plane_rows=n_tiles * tm),
        grid=(2, t // tb),
        in_specs=[pl.BlockSpec((8, tb), lambda p, j: (0, j))],
        out_specs=[pl.BlockSpec((8, tb), lambda p, j: (0, j * p)),
                   pl.BlockSpec((8, tb), lambda p, j: (0, j * p)),
                   pl.BlockSpec((8, ntp), lambda p, j: (0, 0)),
                   pl.BlockSpec((8, 128), lambda p, j: (0, 0))],
        out_shape=[jax.ShapeDtypeStruct((8, t), jnp.int32),
                   jax.ShapeDtypeStruct((8, t), jnp.int32),
                   jax.ShapeDtypeStruct((8, ntp), jnp.int32),
                   jax.ShapeDtypeStruct((8, 128), jnp.int32)],
        scratch_shapes=[pltpu.VMEM((N_EXPERTS, 128), F32), pltpu.VMEM((N_EXPERTS, 128), F32)],
        compiler_params=_cparams("arbitrary", "arbitrary", vmem=VMEM_LIMIT_SMALL),
        name="moe_plan",
    )(eidx)


def _sc_mesh():
    return plsc.VectorSubcoreMesh(core_axis_name="c", subcore_axis_name="s",
                                  num_cores=SC_CORES, num_subcores=SC_SUBCORES)


def _sc_index_spec(tokens):
    nb = tokens // SC_WINDOW
    return pl.BlockSpec((1, SC_WINDOW), lambda i: (i // nb, i % nb))


def _sc_dispatch(rows, i1, i2, n_out):
    n = rows.shape[0]
    tokens = i1.shape[1]

    @functools.partial(pl.kernel, out_type=jax.ShapeDtypeStruct((n_out, 128), rows.dtype), mesh=_sc_mesh(),
                       name="moe_dispatch")
    def k(x_hbm, i1_hbm, i2_hbm, o_hbm):
        def body(x_vmem, i1_vmem, i2_vmem):
            pltpu.sync_copy(x_vmem, o_hbm.at[i1_vmem.at[0]])
            pltpu.sync_copy(x_vmem, o_hbm.at[i2_vmem.at[0]])

        pltpu.emit_pipeline(
            body, grid=(n // SC_WINDOW,),
            in_specs=[pl.BlockSpec((SC_WINDOW, 128), lambda i: (i, 0)),
                      _sc_index_spec(tokens), _sc_index_spec(tokens)],
            out_specs=[],
            core_axis_name=("c", "s"), dimension_semantics=(pltpu.PARALLEL,),
        )(x_hbm, i1_hbm, i2_hbm)

    return k(rows, i1, i2)


def _sc_collect(table, i1, i2):
    tokens = i1.shape[1]
    n = ROW_CHUNKS * tokens
    out = jax.ShapeDtypeStruct((n, 128), table.dtype)

    @functools.partial(pl.kernel, out_type=(out, out), mesh=_sc_mesh(), name="moe_collect",
                       scratch_types=[pltpu.SemaphoreType.DMA, pltpu.SemaphoreType.DMA])
    def k(t_hbm, i1_hbm, i2_hbm, o1_hbm, o2_hbm, sem1, sem2):
        def body(i1_vmem, i2_vmem, o1_vmem, o2_vmem):
            first = pltpu.async_copy(t_hbm.at[i1_vmem.at[0]], o1_vmem, sem1)
            second = pltpu.async_copy(t_hbm.at[i2_vmem.at[0]], o2_vmem, sem2)
            first.wait()
            second.wait()

        pltpu.emit_pipeline(
            body, grid=(n // SC_WINDOW,),
            in_specs=[_sc_index_spec(tokens), _sc_index_spec(tokens)],
            out_specs=[pl.BlockSpec((SC_WINDOW, 128), lambda i: (i, 0)),
                       pl.BlockSpec((SC_WINDOW, 128), lambda i: (i, 0))],
            core_axis_name=("c", "s"), dimension_semantics=(pltpu.PARALLEL,),
        )(i1_hbm, i2_hbm, o1_hbm, o2_hbm)

    return k(table, i1, i2)


def _experts_kernel(te_ref, na_ref, xs_ref, wg_ref, wu_ref, wd_ref, y_ref, wg_scr, wu_scr, wd_scr):
    i = pl.program_id(0)
    active = i < na_ref[0]

    @pl.when(active & ((i == 0) | (te_ref[i] != te_ref[jnp.maximum(i - 1, 0)])))
    def _():
        wg_scr[...] = wg_ref[0, 0].astype(BF16)
        wu_scr[...] = wu_ref[0, 0].astype(BF16)
        wd_scr[...] = wd_ref[0, 0].astype(BF16)

    @pl.when(active)
    def _():
        hi, lo = _unpack_bf16_pairs(_load_row_chunks(xs_ref))
        h = jnp.concatenate([hi, lo], axis=-1).astype(BF16)
        up = _dot(h, wg_scr[...])
        act = _silu(up) * _dot(h, wu_scr[...])
        _store_row_chunks(y_ref, _pack_bf16_pairs(_dot(act.astype(BF16), wd_scr[...])))


def _experts(tile_expert, n_active, xs, wg, wu, wd, *, layer, tm):
    n_tiles = tile_expert.shape[0]
    _, _, d, dff = wg.shape
    rows = lambda i, te, na: (0, jnp.minimum(i, na[0] - 1), 0)
    expert = lambda i, te, na: (layer, te[i], 0, 0)
    return pl.pallas_call(
        _experts_kernel,
        grid_spec=pltpu.PrefetchScalarGridSpec(
            num_scalar_prefetch=2,
            grid=(n_tiles,),
            in_specs=[pl.BlockSpec((ROW_CHUNKS, tm, 128), rows),
                      pl.BlockSpec((1, 1, d, dff), expert),
                      pl.BlockSpec((1, 1, d, dff), expert),
                      pl.BlockSpec((1, 1, dff, d), expert)],
            out_specs=pl.BlockSpec((ROW_CHUNKS, tm, 128), rows),
            scratch_shapes=[pltpu.VMEM((d, dff), BF16), pltpu.VMEM((d, dff), BF16), pltpu.VMEM((dff, d), BF16)]),
        out_shape=jax.ShapeDtypeStruct(xs.shape, xs.dtype),
        compiler_params=_cparams("arbitrary"),
        name="moe_experts",
    )(tile_expert, n_active, xs, wg, wu, wd)


def _moe_combine_kernel(x_ref, y1_ref, y2_ref, w_ref, o_ref):
    half = x_ref.shape[1] // 2
    hi1, lo1 = _unpack_bf16_pairs(_load_row_chunks(y1_ref))
    hi2, lo2 = _unpack_bf16_pairs(_load_row_chunks(y2_ref))
    tm = x_ref.shape[0]
    w_cols = jnp.concatenate([w_ref[...], jnp.zeros((128 - w_ref.shape[0], tm), F32)], axis=0).T
    w1, w2 = w_cols[:, 0:1], w_cols[:, 1:2]
    o_ref[:, :half] = x_ref[:, :half] + w1 * hi1 + w2 * hi2
    o_ref[:, half:] = x_ref[:, half:] + w1 * lo1 + w2 * lo2


def _moe_combine(x2d, y1, y2, wts, *, tm):
    t, d = x2d.shape
    chunk_spec = pl.BlockSpec((ROW_CHUNKS, tm, 128), lambda i: (0, i, 0))
    return pl.pallas_call(
        _moe_combine_kernel,
        grid=(t // tm,),
        in_specs=[pl.BlockSpec((tm, d), lambda i: (i, 0)), chunk_spec, chunk_spec,
                  pl.BlockSpec((wts.shape[0], tm), lambda i: (0, i))],
        out_specs=pl.BlockSpec((tm, d), lambda i: (i, 0)),
        out_shape=jax.ShapeDtypeStruct((t, d), F32),
        compiler_params=_cparams("parallel", vmem=VMEM_LIMIT_SMALL),
        name="moe_combine",
    )(x2d, y1, y2, wts)


def _moe(x2d, hf_rows, eidx, wts, wg, wu, wd, *, layer):
    t = x2d.shape[0]
    tm = MOE_TM
    n_tiles = 2 * t // tm + N_EXPERTS
    plane = n_tiles * tm
    i1, i2, te, na = _moe_plan(eidx, tm=tm, n_tiles=n_tiles)
    xs = _sc_dispatch(hf_rows.reshape(ROW_CHUNKS * t, 128), i1, i2, ROW_CHUNKS * plane)
    ys = _experts(te[0, :n_tiles], na[0, :1], xs.reshape(ROW_CHUNKS, plane, 128), wg, wu, wd,
                  layer=layer, tm=tm)
    y1, y2 = _sc_collect(ys.reshape(ROW_CHUNKS * plane, 128), i1, i2)
    return _moe_combine(x2d, y1.reshape(ROW_CHUNKS, t, 128), y2.reshape(ROW_CHUNKS, t, 128), wts,
                        tm=COMBINE_TM)


W_ROWS = 256


def _w_rows_kernel(start_ref, valid_ref, w_ref, o_ref):
    del start_ref
    row = lax.broadcasted_iota(jnp.int32, w_ref.shape[1:], 0)
    o_ref[0] = jnp.where(row < valid_ref[pl.program_id(1)], w_ref[0], 0.0).astype(o_ref.dtype)


def _w_rows(w_t, starts, valid):
    depth, _, d = w_t.shape
    nblk = len(starts)
    return pl.pallas_call(
        _w_rows_kernel,
        grid_spec=pltpu.PrefetchScalarGridSpec(
            num_scalar_prefetch=2,
            grid=(depth, nblk),
            in_specs=[pl.BlockSpec((pl.Element(1), pl.Element(W_ROWS), pl.Element(d)),
                                   lambda l, c, st, va: (l, pl.multiple_of(st[c], 8), 0))],
            out_specs=pl.BlockSpec((1, W_ROWS, d), lambda l, c, st, va: (l, c, 0))),
        out_shape=jax.ShapeDtypeStruct((depth, nblk * W_ROWS, d), BF16),
        compiler_params=_cparams("parallel", "arbitrary", vmem=VMEM_LIMIT_SMALL),
        name="w_in_rows",
    )(jnp.asarray(starts, jnp.int32), jnp.asarray(valid, jnp.int32), w_t)


def _w_in_layout(w_in):
    w_t = jnp.swapaxes(w_in, 1, 2)
    src_if = 4 * MLSTM_W
    src_a = src_if + 2 * MLSTM_HEADS
    src_g = src_a + 3 * ATTN_W
    starts = list(range(0, src_if, W_ROWS)) + [src_g + k * W_ROWS for k in range((OFF_IF - OFF_GU) // W_ROWS)]
    valid = [W_ROWS] * len(starts)
    starts.append(src_if)
    valid.append(2 * MLSTM_HEADS)
    assert len(starts) * W_ROWS == N_PROJ and ATTN_GW == W_ROWS
    a_starts = [src_a + j * ATTN_W + g * ATTN_GW for g in range(len(ATTN_PATTERNS)) for j in range(3)]
    return _w_rows(w_t, starts, valid), _w_rows(w_t, a_starts, [W_ROWS] * len(a_starts))


def kernel(x, mem, norm_mix, w_in, mlstm_conv, mlstm_gate_b, mlstm_norm, attn_qk_norm, gmlp_norm, gmlp_ws,
           gmlp_bs, w_branch_a, w_branch_b, w_branch_c, w_out, rel_bias, norm_xattn, norm_mem, w_xq, w_xkv,
           xattn_qk_norm, w_xo, norm_ffn, router_w, router_b, w_expert_gate, w_expert_up, w_expert_down):
    b, s, d = x.shape
    t = b * s
    depth = w_in.shape[0]
    x2d = x.reshape(t, d)

    biases = [_attn_bias(rel_bias, g) for g in range(len(ATTN_PATTERNS))]
    rw_t = jnp.zeros((N_EXPERT_GROUPS, 8, d), F32).at[:, :EXPERTS_PER_GROUP].set(
        router_w.T.reshape(N_EXPERT_GROUPS, EXPERTS_PER_GROUP, d)).reshape(ROUTER_ROWS, d)
    rb = jnp.full((N_EXPERT_GROUPS, 8), NEG, F32).at[:, :EXPERTS_PER_GROUP].set(
        router_b.astype(F32).reshape(N_EXPERT_GROUPS, EXPERTS_PER_GROUP)).reshape(ROUTER_ROWS, 1)
    tril = jnp.tril(jnp.ones((GMLP_CHUNK, GMLP_CHUNK), bool))
    head_of = jnp.arange(ATTN_GW) // ATTN_DH
    seg_ones = (head_of[:, None] == head_of[None, :]).astype(BF16)

    w_main, w_attn = _w_in_layout(w_in)

    for l in range(depth):
        proj, h_mix, gates_t = _inproj(x2d, norm_mix[l][None], w_main, layer=l, tm=INPROJ_TM,
                                       tn=INPROJ_TN)
        gq = jnp.tile(attn_qk_norm[l, 0], HEADS_PER_GROUP)[None]
        gk = jnp.tile(attn_qk_norm[l, 1], HEADS_PER_GROUP)[None]

        nh = MLSTM_HEADS
        bias_i = jnp.zeros((8, 1), F32).at[:nh, 0].set(mlstm_gate_b[l, :nh])
        bias_f = jnp.zeros((8, 1), F32).at[:nh, 0].set(mlstm_gate_b[l, nh:])
        ya = _mlstm_rows(proj, gates_t, mlstm_conv[l], bias_i, bias_f, mlstm_norm[l][None],
                         batch=b, seq=s, blk=MLSTM_BLOCK, group=MLSTM_GROUP)

        ybs, lses = [], []
        for g, (_, dilation) in enumerate(ATTN_PATTERNS):
            aproj = _attnproj(h_mix, w_attn, seg_ones, gq, gk, layer=l, group=g, dilation=dilation)
            o, lse = _dattn(aproj, biases[g], seq=s, group=g, dilation=dilation)
            ybs.append(o)
            lses.append(lse)

        ws = jnp.where(tril, gmlp_ws[l], 0.0).astype(BF16)
        bsb = jnp.broadcast_to(gmlp_bs[l][:, :, None], (GMLP_GROUPS, GMLP_CHUNK, GMLP_GC)).astype(F32)
        x2d = _merge(ya, ybs, lses, proj, x2d, w_branch_a[l].astype(BF16), w_branch_b[l].astype(BF16),
                     w_branch_c[l].astype(BF16), w_out[l].astype(BF16), ws, bsb, gmlp_norm[l][None],
                     tm=MERGE_TM)

        k_mem, v_mem = _memkv(mem, norm_mem[l][None], w_xkv[l].astype(BF16), xattn_qk_norm[l, 1][None])
        x2d, hf_rows, eidx, wts = _xattn(x2d, k_mem, v_mem, norm_xattn[l][None], w_xq[l].astype(BF16),
                                         xattn_qk_norm[l, 0][None], w_xo[l].astype(BF16), norm_ffn[l][None],
                                         rw_t, rb, seq=s, tm=XATTN_TM)

        x2d = _moe(x2d, hf_rows, eidx, wts, w_expert_gate, w_expert_up, w_expert_down, layer=l)

    return x2d.reshape(b, s, d)
```

```python
import functools
import math

import jax
import jax.numpy as jnp
import numpy as np
from jax import lax
from jax.experimental import pallas as pl
from jax.experimental.pallas import tpu as pltpu
from jax.experimental.pallas import tpu_sc as plsc

F32 = jnp.float32
BF16 = jnp.bfloat16

EPS = 1e-6
NEG = -1e30

MLSTM_HEADS = 4
MLSTM_DH = 128
MLSTM_W = MLSTM_HEADS * MLSTM_DH
CONV_WIDTH = 4
MLSTM_BLOCK = 128
MLSTM_GROUP = 4

ATTN_PATTERNS = ((128, 1), (512, 4), (2048, 16))
HEADS_PER_GROUP = 4
ATTN_DH = 64
ATTN_GW = HEADS_PER_GROUP * ATTN_DH
ATTN_W = len(ATTN_PATTERNS) * ATTN_GW
ATTN_BLOCK = 128
REL_BUCKETS = 32
REL_MAX_DIST = 2048

GMLP_GROUPS = 4
GMLP_GC = 128
GMLP_W = GMLP_GROUPS * GMLP_GC
GMLP_CHUNK = 128

XATTN_HEADS = 4
XATTN_DH = 128
XATTN_W = XATTN_HEADS * XATTN_DH
XATTN_SUB = 1024

N_EXPERTS = 16
N_EXPERT_GROUPS = 4
EXPERTS_PER_GROUP = 4
ROUTER_ROWS = 8 * N_EXPERT_GROUPS

N_BRANCH = 3

MOE_TM = 1024
ROW_CHUNKS = 4
SC_CORES, SC_SUBCORES = 2, 16
SC_WINDOW = 128

OFF_MQ, OFF_MK, OFF_MV, OFF_MO = 0, 512, 1024, 1536
OFF_GU, OFF_GV = 2048, 2560
OFF_GATE = 3072
OFF_IF = 6144
IF_PAD = 256
N_PROJ = OFF_IF + IF_PAD

ATTN_TILE = 2048
ATTN_SUB = ATTN_TILE // ATTN_BLOCK
ATTN_SLAB = 2 * ATTN_DH
ATTN_COLS = HEADS_PER_GROUP * ATTN_SLAB + 2 * ATTN_GW

VMEM_LIMIT = 48 * 1024 * 1024
VMEM_LIMIT_INPROJ = 56 * 1024 * 1024
VMEM_LIMIT_SMALL = 24 * 1024 * 1024

INPROJ_TM, INPROJ_TN = 1024, 3072
ATTNPROJ_SUB = 512
MERGE_TM = 512
XATTN_TM = 1024
COMBINE_TM = 512
PLAN_TB = 1024


def _cparams(*sem, vmem=VMEM_LIMIT):
    return pltpu.CompilerParams(dimension_semantics=sem, vmem_limit_bytes=vmem)


def _rms(x, gain):
    return x * lax.rsqrt(jnp.mean(x * x, axis=-1, keepdims=True) + EPS) * gain


def _sigmoid(x):
    return 0.5 * jnp.tanh(0.5 * x) + 0.5


def _silu(x):
    half = 0.5 * x
    return half + half * jnp.tanh(half)


def _gelu(x):
    c = math.sqrt(2.0 / math.pi)
    half = 0.5 * x
    return half + half * jnp.tanh(x * (c + (c * 0.044715) * (x * x)))


def _dot(a, b):
    return jnp.dot(a, b, preferred_element_type=F32)


def _dot_nt(a, b):
    return lax.dot_general(a, b, (((1,), (1,)), ((), ())), preferred_element_type=F32)


def _inproj_kernel(x_ref, g_ref, w_ref, wg_ref, o_ref, h_ref, gt_ref):
    @pl.when(pl.program_id(1) == 0)
    def _():
        h = _rms(x_ref[...], g_ref[...]).astype(BF16)
        h_ref[...] = h
        gt_ref[...] = _dot_nt(wg_ref[0, 0:128, :], h)[:gt_ref.shape[0], :]

    o_ref[...] = _dot_nt(h_ref[...], w_ref[0]).astype(o_ref.dtype)


def _inproj(x2d, gain, w, *, layer, tm, tn):
    t, d = x2d.shape
    n = OFF_IF
    return pl.pallas_call(
        _inproj_kernel,
        grid=(t // tm, n // tn),
        in_specs=[pl.BlockSpec((tm, d), lambda i, j: (i, 0)),
                  pl.BlockSpec((1, d), lambda i, j: (0, 0)),
                  pl.BlockSpec((1, tn, d), lambda i, j: (layer, j, 0)),
                  pl.BlockSpec((1, IF_PAD, d), lambda i, j: (layer, OFF_IF // IF_PAD, 0))],
        out_specs=[pl.BlockSpec((tm, tn), lambda i, j: (i, j)),
                   pl.BlockSpec((tm, d), lambda i, j: (i, 0)),
                   pl.BlockSpec((8, tm), lambda i, j: (0, i))],
        out_shape=[jax.ShapeDtypeStruct((t, n), BF16), jax.ShapeDtypeStruct((t, d), BF16),
                   jax.ShapeDtypeStruct((8, t), F32)],
        compiler_params=_cparams("parallel", "arbitrary", vmem=VMEM_LIMIT_INPROJ),
        name="inproj",
    )(x2d, gain, w, w)


def _log_sigmoid(x):
    return jnp.minimum(x, 0.0) - jnp.log(1.0 + jnp.exp(-jnp.abs(x)))


def _split_bf16(x):
    hi = x.astype(BF16)
    return hi, (x - hi.astype(F32)).astype(BF16)


def _prefix_max(x):
    n = x.shape[1]
    lane = lax.broadcasted_iota(jnp.int32, x.shape, 1)
    shift = 1
    while shift < n:
        x = jnp.maximum(x, jnp.where(lane >= shift, pltpu.roll(x, shift, 1), NEG))
        shift *= 2
    return x


def _mlstm_rows_kernel(qk_ref, v_ref, og_ref, *rest, blk, group):
    gate_refs = rest[:group]
    cw_ref, bi_ref, bf_ref, ng_ref, y_ref, xe_scr, s_scr, m_scr = rest[group:]
    heads, dh, w = MLSTM_HEADS, MLSTM_DH, MLSTM_W

    @pl.when(pl.program_id(1) == 0)
    def _():
        xe_scr[:, 0:8, :] = jnp.zeros((group, 8, 2 * w), F32)
        s_scr[...] = jnp.zeros_like(s_scr)
        m_scr[...] = jnp.zeros_like(m_scr)

    cw = cw_ref[...]
    causal = lax.broadcasted_iota(jnp.int32, (blk, blk), 0) >= lax.broadcasted_iota(jnp.int32, (blk, blk), 1)
    triu = (lax.broadcasted_iota(jnp.int32, (blk, blk), 0)
            <= lax.broadcasted_iota(jnp.int32, (blk, blk), 1)).astype(BF16)
    ones = jnp.ones((blk, dh), BF16)
    s_in = [[s_scr[g, h] for h in range(heads)] for g in range(group)]
    m_in = [m_scr[g, :, 0:1] for g in range(group)]
    s_out = [[None] * heads for _ in range(group)]
    m_out = [None] * group
    per_seq = []
    for g in range(group):
        xe_scr[g, 8:8 + blk, :] = qk_ref[g].astype(F32)
        conv = cw[CONV_WIDTH - 1:CONV_WIDTH, :] * xe_scr[g, 8:8 + blk, :]
        for j in range(CONV_WIDTH - 1):
            off = 8 - (CONV_WIDTH - 1) + j
            conv = conv + cw[j:j + 1, :] * xe_scr[g, off:off + blk, :]
        xe_scr[g, 0:8, :] = xe_scr[g, blk:blk + 8, :]
        qk = _silu(conv)

        gates = gate_refs[g][...]
        i_r = gates + bi_ref[...]
        lf_hi, lf_lo = _split_bf16(_log_sigmoid(pltpu.roll(gates, heads, 0) + bf_ref[...]))
        b_r = _dot(lf_hi, triu) + _dot(lf_lo, triu)
        m_st = m_in[g]
        a_r = i_r - b_r
        inter = b_r + m_st
        m_t = jnp.maximum(inter, b_r + _prefix_max(a_r))
        b_last = b_r[:, blk - 1:blk]
        dec = b_last - b_r + i_r
        m_new = jnp.maximum(b_last + m_st, jnp.max(dec, axis=1, keepdims=True))
        w_c = jnp.exp(b_last + m_st - m_new)
        m_out[g] = m_new
        pack = jnp.concatenate([b_r - m_t, jnp.exp(inter - m_t), jnp.exp(-m_t), jnp.exp(dec - m_new),
                                jnp.zeros((blk - 32, blk), F32)], axis=0)
        per_seq.append((qk, a_r, pack.T, w_c))

    chains = [(g, h) for h in range(heads) for g in range(group)]
    st = {}
    for g, h in chains:
        qk = per_seq[g][0]
        sl = slice(h * dh, (h + 1) * dh)
        q_b = qk[:, sl].astype(BF16)
        k_f = qk[:, w + h * dh:w + (h + 1) * dh] * (dh ** -0.5)
        v_ext = jnp.concatenate([v_ref[g, :, sl], ones], axis=-1)
        st[g, h] = (q_b, k_f, v_ext, _dot_nt(q_b, k_f.astype(BF16)), _dot(q_b, s_in[g][h].astype(BF16)))
    for g, h in chains:
        q_b, k_f, v_ext, qk_t, q_state = st[g, h]
        _, a_r, cols, _ = per_seq[g]
        u_c, w_inter = cols[:, h:h + 1], cols[:, 8 + h:9 + h]
        w_intra = jnp.exp(jnp.where(causal, u_c + a_r[h:h + 1, :], NEG))
        st[g, h] = (k_f, v_ext, _dot((qk_t * w_intra).astype(BF16), v_ext) + w_inter * q_state)
    for g, h in chains:
        k_f, v_ext, tot = st[g, h]
        _, _, cols, w_c = per_seq[g]
        em_c, w_k = cols[:, 16 + h:17 + h], cols[:, 24 + h:25 + h]
        sl = slice(h * dh, (h + 1) * dh)
        num, den = tot[:, :dh], tot[:, dh:]
        hh = num / jnp.maximum(jnp.abs(den), em_c)
        hn = _rms(hh, ng_ref[:, sl])
        y_ref[g, :, sl] = (hn * _sigmoid(og_ref[g, :, sl].astype(F32))).astype(y_ref.dtype)
        s_out[g][h] = w_c[h:h + 1, :] * s_in[g][h] + _dot((k_f * w_k).T.astype(BF16), v_ext)
    for g in range(group):
        m_scr[g] = jnp.broadcast_to(m_out[g], m_scr.shape[1:])
        for h in range(heads):
            s_scr[g, h] = s_out[g][h]


def _mlstm_rows(proj, gates_t, conv_w, bias_i, bias_f, norm_g, *, batch, seq, blk, group):
    t, npj = proj.shape
    w = MLSTM_W
    proj3 = proj.reshape(batch, seq, npj)
    cols = lambda c: (lambda b, i: (b, i, c))
    const2 = lambda b, i: (0, 0)
    nblk = seq // blk
    gate_specs = [pl.BlockSpec((8, blk), functools.partial(lambda b, i, g: (0, (b * group + g) * nblk + i), g=g))
                  for g in range(group)]
    y = pl.pallas_call(
        functools.partial(_mlstm_rows_kernel, blk=blk, group=group),
        grid=(batch // group, seq // blk),
        in_specs=[pl.BlockSpec((group, blk, 2 * w), cols(OFF_MQ // (2 * w))),
                  pl.BlockSpec((group, blk, w), cols(OFF_MV // w)),
                  pl.BlockSpec((group, blk, w), cols(OFF_MO // w)),
                  *gate_specs,
                  pl.BlockSpec((CONV_WIDTH, 2 * w), const2),
                  pl.BlockSpec((8, 1), const2), pl.BlockSpec((8, 1), const2),
                  pl.BlockSpec((1, w), const2)],
        out_specs=pl.BlockSpec((group, blk, w), cols(0)),
        out_shape=jax.ShapeDtypeStruct((batch, seq, w), BF16),
        scratch_shapes=[pltpu.VMEM((group, blk + 8, 2 * w), F32),
                        pltpu.VMEM((group, MLSTM_HEADS, MLSTM_DH, 2 * MLSTM_DH), F32),
                        pltpu.VMEM((group, 8, 128), F32)],
        compiler_params=_cparams("parallel", "arbitrary", vmem=VMEM_LIMIT_SMALL),
        name="mlstm",
    )(proj3, proj3, proj3, *([gates_t] * group), conv_w, bias_i, bias_f, norm_g)
    return y.reshape(t, w)


def _attnproj_kernel(h_ref, w_ref, seg_ref, gq_ref, gk_ref, o_ref, r_scr, *, dil):
    gw, half = ATTN_GW, ATTN_SLAB // 2
    sub_rows = ATTNPROJ_SUB
    seg, sub_seg = ATTN_TILE // dil, sub_rows // dil

    def head_norm(x, gain):
        ss = _dot((x * x).astype(BF16), seg_ref[...])
        return x * lax.rsqrt(ss * (1.0 / ATTN_DH) + EPS) * gain

    low = lax.broadcasted_iota(jnp.int32, (1, ATTN_SLAB), 1) < half
    for s in range(ATTN_TILE // sub_rows):
        rows = slice(s * sub_rows, (s + 1) * sub_rows)
        res = _dot_nt(h_ref[rows, :], w_ref[0])
        q = head_norm(res[:, :gw], gq_ref[...]) * (ATTN_DH ** -0.5)
        k = head_norm(res[:, gw:2 * gw], gk_ref[...])
        slabs = []
        for pair in range(gw // ATTN_SLAB):
            qp = q[:, pair * ATTN_SLAB:(pair + 1) * ATTN_SLAB]
            slabs += [jnp.where(low, qp, 0.0), jnp.where(low, 0.0, qp)]
        slabs += [k[:, c * 128:(c + 1) * 128] for c in range(gw // 128)]
        slabs += [res[:, 2 * gw + c * 128:2 * gw + (c + 1) * 128] for c in range(gw // 128)]
        pitch = dil + 1 if dil % 16 == 0 else dil
        for c, slab in enumerate(slabs):
            if dil == 1:
                o_ref[rows, c * 128:(c + 1) * 128] = slab.astype(o_ref.dtype)
            elif pitch == dil:
                r_scr[s % 2, c, 0:sub_rows, :] = slab
            else:
                for i in range(sub_seg):
                    r_scr[s % 2, c, pitch * i:pitch * i + dil, :] = slab[dil * i:dil * (i + 1), :]
        if dil > 1:
            for r in range(dil):
                dst = slice(r * seg + s * sub_seg, r * seg + (s + 1) * sub_seg)
                for c in range(r_scr.shape[1]):
                    o_ref[dst, c * 128:(c + 1) * 128] = (
                        r_scr[s % 2, c, pl.ds(r, sub_seg, stride=pitch), :].astype(o_ref.dtype))


def _attnproj(h, w, seg_ones, gq, gk, *, layer, group, dilation):
    t, d = h.shape
    wcols = 3 * ATTN_GW
    const2 = lambda i: (0, 0)
    return pl.pallas_call(
        functools.partial(_attnproj_kernel, dil=dilation),
        grid=(t // ATTN_TILE,),
        in_specs=[pl.BlockSpec((ATTN_TILE, d), lambda i: (i, 0)),
                  pl.BlockSpec((1, wcols, d), lambda i: (layer, group, 0)),
                  pl.BlockSpec((ATTN_GW, ATTN_GW), const2),
                  pl.BlockSpec((1, ATTN_GW), const2), pl.BlockSpec((1, ATTN_GW), const2)],
        out_specs=pl.BlockSpec((ATTN_TILE, ATTN_COLS), lambda i: (i, 0)),
        out_shape=jax.ShapeDtypeStruct((t, ATTN_COLS), BF16),
        scratch_shapes=[pltpu.VMEM((2, ATTN_COLS // 128, ATTNPROJ_SUB + ATTNPROJ_SUB // 16, 128), F32)],
        compiler_params=_cparams("parallel"),
        name=f"attnproj{group}",
    )(h, w, seg_ones, gq, gk)


def _dattn_kernel(q_ref, kc_ref, kp_ref, vc_ref, vp_ref, bias_ref, o_ref, lse_ref,
                  kx_scr, vx_scr, o_scr, l_scr, *, dil):
    blk = ATTN_BLOCK
    per = ATTN_SUB // dil
    pitch = dil + 1 if dil % 16 == 0 else dil
    first_tile = pl.program_id(1) == 0
    for r in range(dil):
        base = r * (per + 1) * blk
        last = slice((r * per + per - 1) * blk, (r * per + per) * blk)
        mine = slice(r * per * blk, (r + 1) * per * blk)
        kx_scr[base:base + blk, :] = kp_ref[last, :]
        vx_scr[base:base + blk, :] = vp_ref[last, :]
        kx_scr[base + blk:base + (per + 1) * blk, :] = kc_ref[mine, :]
        vx_scr[base + blk:base + (per + 1) * blk, :] = vc_ref[mine, :]

    low = lax.broadcasted_iota(jnp.int32, (1, ATTN_SLAB), 1) < ATTN_SLAB // 2
    no_prev = lax.broadcasted_iota(jnp.int32, (1, 2 * blk), 1) < blk
    for r in range(dil):
        for sub in range(per):
            u = r * per + sub
            win = slice((r * (per + 1) + sub) * blk, (r * (per + 1) + sub + 2) * blk)
            o_slabs, l_slabs = [], []
            for pair in range(ATTN_GW // ATTN_SLAB):
                cols = slice(pair * ATTN_SLAB, (pair + 1) * ATTN_SLAB)
                kx, vx = kx_scr[win, cols], vx_scr[win, cols]
                o_pair, l_pair = [], []
                for h in (2 * pair, 2 * pair + 1):
                    logits = _dot_nt(q_ref[u * blk:(u + 1) * blk, h * ATTN_SLAB:(h + 1) * ATTN_SLAB], kx)
                    logits = logits + bias_ref[h]
                    if sub == 0:
                        logits = jnp.where(first_tile & no_prev, NEG, logits)
                    m = jnp.max(logits, axis=-1, keepdims=True)
                    p = jnp.exp(logits - m)
                    l = jnp.sum(p, axis=-1, keepdims=True)
                    o_pair.append(_dot(p.astype(BF16), vx) / l)
                    l_pair.append(m + jnp.log(l))
                o_slabs.append(jnp.where(low, o_pair[0], o_pair[1]))
                l_slabs.append(jnp.where(low, l_pair[0], l_pair[1]))
            for c in range(ATTN_GW // ATTN_SLAB):
                cols = slice(c * ATTN_SLAB, (c + 1) * ATTN_SLAB)
                if dil == 1:
                    o_ref[u * blk:(u + 1) * blk, cols] = o_slabs[c].astype(o_ref.dtype)
                    lse_ref[u * blk:(u + 1) * blk, cols] = l_slabs[c]
                else:
                    dst = pl.ds(sub * blk * pitch + r, blk, stride=pitch)
                    o_scr[c, dst, :] = o_slabs[c]
                    l_scr[c, dst, :] = l_slabs[c]
    if dil > 1:
        for c in range(ATTN_GW // ATTN_SLAB):
            cols = slice(c * ATTN_SLAB, (c + 1) * ATTN_SLAB)
            if pitch == dil:
                o_ref[:, cols] = o_scr[c, 0:ATTN_TILE, :].astype(o_ref.dtype)
                lse_ref[:, cols] = l_scr[c, 0:ATTN_TILE, :]
            else:
                for i in range(ATTN_TILE // dil):
                    o_ref[dil * i:dil * (i + 1), cols] = o_scr[c, pitch * i:pitch * i + dil, :].astype(o_ref.dtype)
                    lse_ref[dil * i:dil * (i + 1), cols] = l_scr[c, pitch * i:pitch * i + dil, :]


def _dattn(aproj, bias, *, seq, group, dilation):
    t = aproj.shape[0]
    tiles = seq // ATTN_TILE
    qw = HEADS_PER_GROUP * ATTN_SLAB
    cq, ck, cv = 0, qw // ATTN_GW, qw // ATTN_GW + 1
    blk = (ATTN_TILE, ATTN_GW)
    cur = lambda c: (lambda b, j: (b * tiles + j, c))
    prev = lambda c: (lambda b, j: (b * tiles + jnp.maximum(j - 1, 0), c))
    xrows = ATTN_TILE + dilation * ATTN_BLOCK
    return pl.pallas_call(
        functools.partial(_dattn_kernel, dil=dilation),
        grid=(t // seq, tiles),
        in_specs=[pl.BlockSpec((ATTN_TILE, qw), cur(cq)),
                  pl.BlockSpec(blk, cur(ck)), pl.BlockSpec(blk, prev(ck)),
                  pl.BlockSpec(blk, cur(cv)), pl.BlockSpec(blk, prev(cv)),
                  pl.BlockSpec((HEADS_PER_GROUP, ATTN_BLOCK, 2 * ATTN_BLOCK), lambda b, j: (0, 0, 0))],
        out_specs=[pl.BlockSpec(blk, cur(0)), pl.BlockSpec(blk, cur(0))],
        out_shape=[jax.ShapeDtypeStruct((t, ATTN_GW), BF16), jax.ShapeDtypeStruct((t, ATTN_GW), F32)],
        scratch_shapes=[pltpu.VMEM((xrows, ATTN_GW), BF16), pltpu.VMEM((xrows, ATTN_GW), BF16),
                        pltpu.VMEM((ATTN_GW // ATTN_SLAB, ATTN_TILE + ATTN_TILE // 16, ATTN_SLAB), F32),
                        pltpu.VMEM((ATTN_GW // ATTN_SLAB, ATTN_TILE + ATTN_TILE // 16, ATTN_SLAB), F32)],
        compiler_params=_cparams("parallel", "arbitrary"),
        name=f"dattn{group}",
    )(aproj, aproj, aproj, aproj, aproj, bias)


def _rel_bucket(n):
    max_exact = REL_BUCKETS // 2
    nf = jnp.maximum(n, 1).astype(F32)
    log_b = max_exact + (jnp.log(nf / max_exact) / math.log(REL_MAX_DIST / max_exact)
                         * (REL_BUCKETS - max_exact)).astype(jnp.int32)
    return jnp.where(n < max_exact, n, jnp.minimum(log_b, REL_BUCKETS - 1))


def _attn_bias(rel_bias, group):
    window, dilation = ATTN_PATTERNS[group]
    steps = window // dilation
    hp = lax.Precision.HIGHEST
    hs = slice(group * HEADS_PER_GROUP, (group + 1) * HEADS_PER_GROUP)
    bucket = _rel_bucket(jnp.arange(steps + 1) * dilation)
    bias_steps = jnp.dot(jax.nn.one_hot(bucket, REL_BUCKETS, dtype=F32), rel_bias[:, hs].astype(F32),
                         precision=hp)
    qi = jnp.arange(ATTN_BLOCK)[:, None]
    ki = jnp.arange(2 * ATTN_BLOCK)[None, :]
    dist = ATTN_BLOCK + qi - ki
    ok = (dist >= 0) & (dist <= steps)
    sel = jax.nn.one_hot(jnp.clip(dist, 0, steps).reshape(-1), steps + 1, dtype=F32)
    bias = jnp.dot(sel, bias_steps, precision=hp).T.reshape(HEADS_PER_GROUP, ATTN_BLOCK, 2 * ATTN_BLOCK)
    return jnp.where(ok[None], bias, NEG)


def _merge_kernel(ya_ref, yb0_ref, yb1_ref, yb2_ref, l0_ref, l1_ref, l2_ref, gu_ref, gv_ref, gate_ref,
                  x_ref, wa_ref, wb_ref, wc_ref, wo_ref, ws_ref, bs_ref, gg_ref, o_ref, yc_scr, *, tm):
    d = x_ref.shape[1]
    l0, l1, l2 = l0_ref[...], l1_ref[...], l2_ref[...]
    mx = jnp.maximum(jnp.maximum(l0, l1), l2)
    e0, e1, e2 = jnp.exp(l0 - mx), jnp.exp(l1 - mx), jnp.exp(l2 - mx)
    inv = 1.0 / (e0 + e1 + e2)
    yb = jnp.concatenate([(yb0_ref[...].astype(F32) * (e0 * inv)).astype(BF16),
                          (yb1_ref[...].astype(F32) * (e1 * inv)).astype(BF16),
                          (yb2_ref[...].astype(F32) * (e2 * inv)).astype(BF16)], axis=-1)

    for j in range(tm // GMLP_CHUNK):
        rows = slice(j * GMLP_CHUNK, (j + 1) * GMLP_CHUNK)
        for g in range(GMLP_GROUPS):
            cols = slice(g * GMLP_GC, (g + 1) * GMLP_GC)
            u = _gelu(gu_ref[rows, cols].astype(F32))
            v = _rms(_gelu(gv_ref[rows, cols].astype(F32)), gg_ref[:, cols])
            mixed = _dot(ws_ref[g], v.astype(BF16)) + bs_ref[g]
            yc_scr[rows, cols] = (u * mixed).astype(BF16)

    def gate2(k):
        return jnp.tanh(0.5 * gate_ref[:, k * d:(k + 1) * d].astype(F32)) + 1.0

    merged2 = gate2(0) * _dot(ya_ref[...], wa_ref[...])
    merged2 = merged2 + gate2(1) * _dot(yb, wb_ref[...])
    merged2 = merged2 + gate2(2) * _dot(yc_scr[...], wc_ref[...])
    o_ref[...] = x_ref[...] + 0.5 * _dot(merged2.astype(BF16), wo_ref[...])


def _merge(ya, ybs, lses, proj, x2d, wa, wb, wc, wo, ws, bsb, gg, *, tm):
    t, d = x2d.shape
    row = lambda c: (lambda i: (i, c))
    full2 = lambda i: (0, 0)
    full3 = lambda i: (0, 0, 0)
    gspec = pl.BlockSpec((tm, ATTN_GW), row(0))
    return pl.pallas_call(
        functools.partial(_merge_kernel, tm=tm),
        grid=(t // tm,),
        in_specs=[pl.BlockSpec((tm, MLSTM_W), row(0)),
                  gspec, gspec, gspec, gspec, gspec, gspec,
                  pl.BlockSpec((tm, GMLP_W), row(OFF_GU // GMLP_W)),
                  pl.BlockSpec((tm, GMLP_W), row(OFF_GV // GMLP_W)),
                  pl.BlockSpec((tm, N_BRANCH * d), row(OFF_GATE // (N_BRANCH * d))),
                  pl.BlockSpec((tm, d), row(0)),
                  pl.BlockSpec(wa.shape, full2), pl.BlockSpec(wb.shape, full2),
                  pl.BlockSpec(wc.shape, full2), pl.BlockSpec(wo.shape, full2),
                  pl.BlockSpec(ws.shape, full3), pl.BlockSpec(bsb.shape, full3),
                  pl.BlockSpec(gg.shape, full2)],
        out_specs=pl.BlockSpec((tm, d), row(0)),
        out_shape=jax.ShapeDtypeStruct((t, d), F32),
        scratch_shapes=[pltpu.VMEM((tm, GMLP_W), BF16)],
        compiler_params=_cparams("parallel"),
        name="merge",
    )(ya, *ybs, *lses, proj, proj, proj, x2d, wa, wb, wc, wo, ws, bsb, gg)


def _memkv_kernel(mem_ref, g_ref, w_ref, gk_ref, k_ref, v_ref):
    dh, w = XATTN_DH, XATTN_W
    kv = _dot(_rms(mem_ref[0], g_ref[...]).astype(BF16), w_ref[...])
    for h in range(XATTN_HEADS):
        sl = slice(h * dh, (h + 1) * dh)
        k_ref[0, :, sl] = _rms(kv[:, sl], gk_ref[...]).astype(k_ref.dtype)
    v_ref[0] = kv[:, w:].astype(v_ref.dtype)


def _memkv(mem, gain, w_kv, gk):
    b, m, d = mem.shape
    full2 = lambda i: (0, 0)
    return pl.pallas_call(
        _memkv_kernel,
        grid=(b,),
        in_specs=[pl.BlockSpec((1, m, d), lambda i: (i, 0, 0)),
                  pl.BlockSpec((1, d), full2),
                  pl.BlockSpec(w_kv.shape, full2),
                  pl.BlockSpec((1, XATTN_DH), full2)],
        out_specs=[pl.BlockSpec((1, m, XATTN_W), lambda i: (i, 0, 0)),
                   pl.BlockSpec((1, m, XATTN_W), lambda i: (i, 0, 0))],
        out_shape=[jax.ShapeDtypeStruct((b, m, XATTN_W), BF16),
                   jax.ShapeDtypeStruct((b, m, XATTN_W), BF16)],
        compiler_params=_cparams("parallel", vmem=VMEM_LIMIT_SMALL),
        name="memkv",
    )(mem, gain, w_kv, gk)


def _route(logits):
    tm = logits.shape[1]
    e = jnp.exp(logits - jnp.max(logits, axis=0, keepdims=True))
    probs = e / jnp.sum(e, axis=0, keepdims=True)
    rowi = lax.broadcasted_iota(jnp.int32, (8, tm), 0)
    real = rowi < EXPERTS_PER_GROUP
    tops = []
    for g in range(N_EXPERT_GROUPS):
        pg = jnp.where(real, probs[8 * g:8 * g + 8, :], -0.5)
        m1 = jnp.max(pg, axis=0, keepdims=True)
        i1 = jnp.min(jnp.where(pg == m1, rowi, 8), axis=0, keepdims=True)
        pg2 = jnp.where(rowi == i1, -1.0, pg)
        m2 = jnp.max(pg2, axis=0, keepdims=True)
        i2 = jnp.min(jnp.where(pg2 == m2, rowi, 8), axis=0, keepdims=True)
        tops.append((m1, i1, m2, i2))
    best = jnp.zeros((1, tm), jnp.int32)
    best_score = tops[0][0] + tops[0][2]
    for g in range(1, N_EXPERT_GROUPS):
        score = tops[g][0] + tops[g][2]
        better = score > best_score
        best = jnp.where(better, g, best)
        best_score = jnp.where(better, score, best_score)
    m1, i1, m2, i2 = tops[0]
    for g in range(1, N_EXPERT_GROUPS):
        m1, i1, m2, i2 = (jnp.where(best == g, new, old) for new, old in zip(tops[g], (m1, i1, m2, i2)))
    tot = m1 + m2
    base = best * EXPERTS_PER_GROUP
    return base + i1, base + i2, m1 / tot, m2 / tot


def _pack_bf16_pairs(x):
    n = x.shape[1] // 2
    hi = lax.bitcast_convert_type(x[:, :n].astype(BF16).astype(F32), jnp.uint32)
    lo = lax.bitcast_convert_type(x[:, n:].astype(BF16).astype(F32), jnp.uint32)
    return hi | (lo >> 16)


def _unpack_bf16_pairs(p):
    hi = lax.bitcast_convert_type(p & jnp.uint32(0xFFFF0000), F32)
    lo = lax.bitcast_convert_type(p << 16, F32)
    return hi, lo


def _store_row_chunks(ref, packed):
    for j in range(ROW_CHUNKS):
        ref[j] = packed[:, j * 128:(j + 1) * 128]


def _load_row_chunks(ref):
    return jnp.concatenate([ref[j] for j in range(ROW_CHUNKS)], axis=-1)


def _xattn_kernel(x_ref, k_ref, v_ref, gx_ref, wq_ref, gq_ref, wo_ref, gf_ref, rw_ref, rb_ref,
                  xo_ref, hf_ref, eidx_ref, wts_ref, *, sub):
    dh = XATTN_DH
    rw = rw_ref[...]
    rw_hi, rw_lo = _split_bf16(rw)
    for s in range(x_ref.shape[0] // sub):
        rows = slice(s * sub, (s + 1) * sub)
        x = x_ref[rows, :]
        q = _dot(_rms(x, gx_ref[...]).astype(BF16), wq_ref[...])
        outs = []
        for h in range(XATTN_HEADS):
            sl = slice(h * dh, (h + 1) * dh)
            q_h = (_rms(q[:, sl], gq_ref[...]) * (dh ** -0.5)).astype(BF16)
            logits = _dot_nt(q_h, k_ref[0, :, sl])
            p = jnp.exp(logits - jnp.max(logits, axis=-1, keepdims=True))
            o = _dot(p.astype(BF16), v_ref[0, :, sl]) / jnp.sum(p, axis=-1, keepdims=True)
            outs.append(o.astype(BF16))
        xn = x + _dot(jnp.concatenate(outs, axis=-1), wo_ref[...])
        xo_ref[rows, :] = xn
        hf = _rms(xn, gf_ref[...])
        packed = _pack_bf16_pairs(hf)
        for j in range(ROW_CHUNKS):
            hf_ref[j, rows, :] = packed[:, j * 128:(j + 1) * 128]
        hf_hi, hf_lo = _split_bf16(hf)
        logits_t = _dot_nt(rw_hi, hf_hi) + _dot_nt(rw_hi, hf_lo) + _dot_nt(rw_lo, hf_hi) + rb_ref[...]
        e1, e2, w1, w2 = _route(logits_t)
        eidx_ref[:, rows] = jnp.concatenate([e1, e2, jnp.zeros((6, sub), jnp.int32)], axis=0)
        wts_ref[:, rows] = jnp.concatenate([w1, w2, jnp.zeros((6, sub), F32)], axis=0)


def _xattn(x2d, k, v, gx, wq, gq, wo, gf, rw_t, rb, *, seq, tm):
    t, d = x2d.shape
    per_b = seq // tm
    full2 = lambda i: (0, 0)
    kv_spec = pl.BlockSpec((1,) + k.shape[1:], lambda i: (i // per_b, 0, 0))
    return pl.pallas_call(
        functools.partial(_xattn_kernel, sub=min(tm, XATTN_SUB)),
        grid=(t // tm,),
        in_specs=[pl.BlockSpec((tm, d), lambda i: (i, 0)), kv_spec, kv_spec,
                  pl.BlockSpec((1, d), full2), pl.BlockSpec(wq.shape, full2),
                  pl.BlockSpec((1, XATTN_DH), full2), pl.BlockSpec(wo.shape, full2),
                  pl.BlockSpec((1, d), full2), pl.BlockSpec(rw_t.shape, full2),
                  pl.BlockSpec(rb.shape, full2)],
        out_specs=[pl.BlockSpec((tm, d), lambda i: (i, 0)),
                   pl.BlockSpec((ROW_CHUNKS, tm, 128), lambda i: (0, i, 0)),
                   pl.BlockSpec((8, tm), lambda i: (0, i)),
                   pl.BlockSpec((8, tm), lambda i: (0, i))],
        out_shape=[jax.ShapeDtypeStruct((t, d), F32),
                   jax.ShapeDtypeStruct((ROW_CHUNKS, t, 128), jnp.uint32),
                   jax.ShapeDtypeStruct((8, t), jnp.int32),
                   jax.ShapeDtypeStruct((8, t), F32)],
        compiler_params=_cparams("parallel"),
        name="xattn_router",
    )(x2d, k, v, gx, wq, gq, wo, gf, rw_t, rb)


def _moe_plan_kernel(eidx_ref, i1_ref, i2_ref, te_ref, na_ref, cnt_scr, carry_scr, *, tb, tm, plane_rows):
    ne = N_EXPERTS
    hp = lax.Precision.HIGHEST
    phase, j = pl.program_id(0), pl.program_id(1)
    rows = lax.broadcasted_iota(jnp.int32, (ne, tb), 0)
    oh1 = rows == eidx_ref[0:1, :]
    oh2 = rows == eidx_ref[1:2, :]
    a = oh1.astype(F32) + oh2.astype(F32)
    blk_cnt = jnp.broadcast_to(jnp.sum(a, axis=1, keepdims=True), cnt_scr.shape)

    @pl.when((phase == 0) & (j == 0))
    def _():
        cnt_scr[...] = jnp.zeros_like(cnt_scr)

    @pl.when(phase == 0)
    def _():
        cnt_scr[...] += blk_cnt

    @pl.when((phase == 1) & (j == 0))
    def _():
        padded = jnp.ceil(cnt_scr[...] * (1.0 / tm)) * tm
        er = lax.broadcasted_iota(jnp.int32, (ne, ne), 0)
        ec = lax.broadcasted_iota(jnp.int32, (ne, ne), 1)
        off = jnp.dot((ec < er).astype(F32), padded, precision=hp, preferred_element_type=F32)
        carry_scr[...] = off
        seg_end = (off + padded)[:, 0:1]
        tile_start = lax.broadcasted_iota(jnp.int32, (ne, te_ref.shape[1]), 1).astype(F32) * tm
        te = jnp.sum((seg_end <= tile_start).astype(F32), axis=0, keepdims=True)
        te_ref[...] = jnp.broadcast_to(jnp.minimum(te, ne - 1.0), te_ref.shape).astype(jnp.int32)
        total = jnp.sum(padded[:, 0:1], axis=0, keepdims=True)
        na_ref[...] = jnp.broadcast_to(total * (1.0 / tm), na_ref.shape).astype(jnp.int32)

    @pl.when(phase == 1)
    def _():
        before = (lax.broadcasted_iota(jnp.int32, (tb, tb), 0)
                  < lax.broadcasted_iota(jnp.int32, (tb, tb), 1)).astype(BF16)
        rank = carry_scr[:, 0:1] + _dot(a.astype(BF16), before)
        d1 = jnp.sum(jnp.where(oh1, rank, 0.0), axis=0, keepdims=True).astype(jnp.int32)
        d2 = jnp.sum(jnp.where(oh2, rank, 0.0), axis=0, keepdims=True).astype(jnp.int32)
        plane = lax.broadcasted_iota(jnp.int32, (8, tb), 0) * plane_rows
        i1_ref[...] = jnp.where(plane < ROW_CHUNKS * plane_rows, plane + d1, 0)
        i2_ref[...] = jnp.where(plane < ROW_CHUNKS * plane_rows, plane + d2, 0)
        carry_scr[...] += blk_cnt


def _moe_plan(eidx, *, tm, n_tiles, tb=PLAN_TB):
    t = eidx.shape[1]
    ntp = -(-n_tiles // 128) * 128
    return pl.pallas_call(
        functools.partial(_moe_plan_kernel, tb=tb, tm=tm, plane_rows=n_tiles * tm),
        grid=(2, t // tb),
        in_specs=[pl.BlockSpec((8, tb), lambda p, j: (0, j))],
        out_specs=[pl.BlockSpec((8, tb), lambda p, j: (0, j * p)),
                   pl.BlockSpec((8, tb), lambda p, j: (0, j * p)),
                   pl.BlockSpec((8, ntp), lambda p, j: (0, 0)),
                   pl.BlockSpec((8, 128), lambda p, j: (0, 0))],
        out_shape=[jax.ShapeDtypeStruct((8, t), jnp.int32),
                   jax.ShapeDtypeStruct((8, t), jnp.int32),
                   jax.ShapeDtypeStruct((8, ntp), jnp.int32),
                   jax.ShapeDtypeStruct((8, 128), jnp.int32)],
        scratch_shapes=[pltpu.VMEM((N_EXPERTS, 128), F32), pltpu.VMEM((N_EXPERTS, 128), F32)],
        compiler_params=_cparams("arbitrary", "arbitrary", vmem=VMEM_LIMIT_SMALL),
        name="moe_plan",
    )(eidx)


def _sc_mesh():
    return plsc.VectorSubcoreMesh(core_axis_name="c", subcore_axis_name="s",
                                  num_cores=SC_CORES, num_subcores=SC_SUBCORES)


def _sc_index_spec(tokens):
    nb = tokens // SC_WINDOW
    return pl.BlockSpec((1, SC_WINDOW), lambda i: (i // nb, i % nb))


def _sc_dispatch(rows, i1, i2, n_out):
    n = rows.shape[0]
    tokens = i1.shape[1]

    @functools.partial(pl.kernel, out_type=jax.ShapeDtypeStruct((n_out, 128), rows.dtype), mesh=_sc_mesh(),
                       name="moe_dispatch")
    def k(x_hbm, i1_hbm, i2_hbm, o_hbm):
        def body(x_vmem, i1_vmem, i2_vmem):
            pltpu.sync_copy(x_vmem, o_hbm.at[i1_vmem.at[0]])
            pltpu.sync_copy(x_vmem, o_hbm.at[i2_vmem.at[0]])

        pltpu.emit_pipeline(
            body, grid=(n // SC_WINDOW,),
            in_specs=[pl.BlockSpec((SC_WINDOW, 128), lambda i: (i, 0)),
                      _sc_index_spec(tokens), _sc_index_spec(tokens)],
            out_specs=[],
            core_axis_name=("c", "s"), dimension_semantics=(pltpu.PARALLEL,),
        )(x_hbm, i1_hbm, i2_hbm)

    return k(rows, i1, i2)


def _sc_collect(table, i1, i2):
    tokens = i1.shape[1]
    n = ROW_CHUNKS * tokens
    out = jax.ShapeDtypeStruct((n, 128), table.dtype)

    @functools.partial(pl.kernel, out_type=(out, out), mesh=_sc_mesh(), name="moe_collect",
                       scratch_types=[pltpu.SemaphoreType.DMA, pltpu.SemaphoreType.DMA])
    def k(t_hbm, i1_hbm, i2_hbm, o1_hbm, o2_hbm, sem1, sem2):
        def body(i1_vmem, i2_vmem, o1_vmem, o2_vmem):
            first = pltpu.async_copy(t_hbm.at[i1_vmem.at[0]], o1_vmem, sem1)
            second = pltpu.async_copy(t_hbm.at[i2_vmem.at[0]], o2_vmem, sem2)
            first.wait()
            second.wait()

        pltpu.emit_pipeline(
            body, grid=(n // SC_WINDOW,),
            in_specs=[_sc_index_spec(tokens), _sc_index_spec(tokens)],
            out_specs=[pl.BlockSpec((SC_WINDOW, 128), lambda i: (i, 0)),
                       pl.BlockSpec((SC_WINDOW, 128), lambda i: (i, 0))],
            core_axis_name=("c", "s"), dimension_semantics=(pltpu.PARALLEL,),
        )(i1_hbm, i2_hbm, o1_hbm, o2_hbm)

    return k(table, i1, i2)


def _experts_kernel(te_ref, na_ref, xs_ref, wg_ref, wu_ref, wd_ref, y_ref, wg_scr, wu_scr, wd_scr):
    i = pl.program_id(0)
    active = i < na_ref[0]

    @pl.when(active & ((i == 0) | (te_ref[i] != te_ref[jnp.maximum(i - 1, 0)])))
    def _():
        wg_scr[...] = wg_ref[0, 0].astype(BF16)
        wu_scr[...] = wu_ref[0, 0].astype(BF16)
        wd_scr[...] = wd_ref[0, 0].astype(BF16)

    @pl.when(active)
    def _():
        hi, lo = _unpack_bf16_pairs(_load_row_chunks(xs_ref))
        h = jnp.concatenate([hi, lo], axis=-1).astype(BF16)
        up = _dot(h, wg_scr[...])
        act = _silu(up) * _dot(h, wu_scr[...])
        _store_row_chunks(y_ref, _pack_bf16_pairs(_dot(act.astype(BF16), wd_scr[...])))


def _experts(tile_expert, n_active, xs, wg, wu, wd, *, layer, tm):
    n_tiles = tile_expert.shape[0]
    _, _, d, dff = wg.shape
    rows = lambda i, te, na: (0, jnp.minimum(i, na[0] - 1), 0)
    expert = lambda i, te, na: (layer, te[i], 0, 0)
    return pl.pallas_call(
        _experts_kernel,
        grid_spec=pltpu.PrefetchScalarGridSpec(
            num_scalar_prefetch=2,
            grid=(n_tiles,),
            in_specs=[pl.BlockSpec((ROW_CHUNKS, tm, 128), rows),
                      pl.BlockSpec((1, 1, d, dff), expert),
                      pl.BlockSpec((1, 1, d, dff), expert),
                      pl.BlockSpec((1, 1, dff, d), expert)],
            out_specs=pl.BlockSpec((ROW_CHUNKS, tm, 128), rows),
            scratch_shapes=[pltpu.VMEM((d, dff), BF16), pltpu.VMEM((d, dff), BF16), pltpu.VMEM((dff, d), BF16)]),
        out_shape=jax.ShapeDtypeStruct(xs.shape, xs.dtype),
        compiler_params=_cparams("arbitrary"),
        name="moe_experts",
    )(tile_expert, n_active, xs, wg, wu, wd)


def _moe_combine_kernel(x_ref, y1_ref, y2_ref, w_ref, o_ref):
    half = x_ref.shape[1] // 2
    hi1, lo1 = _unpack_bf16_pairs(_load_row_chunks(y1_ref))
    hi2, lo2 = _unpack_bf16_pairs(_load_row_chunks(y2_ref))
    tm = x_ref.shape[0]
    w_cols = jnp.concatenate([w_ref[...], jnp.zeros((128 - w_ref.shape[0], tm), F32)], axis=0).T
    w1, w2 = w_cols[:, 0:1], w_cols[:, 1:2]
    o_ref[:, :half] = x_ref[:, :half] + w1 * hi1 + w2 * hi2
    o_ref[:, half:] = x_ref[:, half:] + w1 * lo1 + w2 * lo2


def _moe_combine(x2d, y1, y2, wts, *, tm):
    t, d = x2d.shape
    chunk_spec = pl.BlockSpec((ROW_CHUNKS, tm, 128), lambda i: (0, i, 0))
    return pl.pallas_call(
        _moe_combine_kernel,
        grid=(t // tm,),
        in_specs=[pl.BlockSpec((tm, d), lambda i: (i, 0)), chunk_spec, chunk_spec,
                  pl.BlockSpec((wts.shape[0], tm), lambda i: (0, i))],
        out_specs=pl.BlockSpec((tm, d), lambda i: (i, 0)),
        out_shape=jax.ShapeDtypeStruct((t, d), F32),
        compiler_params=_cparams("parallel", vmem=VMEM_LIMIT_SMALL),
        name="moe_combine",
    )(x2d, y1, y2, wts)


def _moe(x2d, hf_rows, eidx, wts, wg, wu, wd, *, layer):
    t = x2d.shape[0]
    tm = MOE_TM
    n_tiles = 2 * t // tm + N_EXPERTS
    plane = n_tiles * tm
    i1, i2, te, na = _moe_plan(eidx, tm=tm, n_tiles=n_tiles)
    xs = _sc_dispatch(hf_rows.reshape(ROW_CHUNKS * t, 128), i1, i2, ROW_CHUNKS * plane)
    ys = _experts(te[0, :n_tiles], na[0, :1], xs.reshape(ROW_CHUNKS, plane, 128), wg, wu, wd,
                  layer=layer, tm=tm)
    y1, y2 = _sc_collect(ys.reshape(ROW_CHUNKS * plane, 128), i1, i2)
    return _moe_combine(x2d, y1.reshape(ROW_CHUNKS, t, 128), y2.reshape(ROW_CHUNKS, t, 128), wts,
                        tm=COMBINE_TM)


W_ROWS = 256


def _w_rows_kernel(start_ref, valid_ref, w_ref, o_ref):
    del start_ref
    row = lax.broadcasted_iota(jnp.int32, w_ref.shape, 1)
    o_ref[...] = jnp.where(row < valid_ref[pl.program_id(0)], w_ref[...], 0.0).astype(o_ref.dtype)


def _w_rows(w_t, starts, valid):
    depth, _, d = w_t.shape
    nblk = len(starts)
    return pl.pallas_call(
        _w_rows_kernel,
        grid_spec=pltpu.PrefetchScalarGridSpec(
            num_scalar_prefetch=2,
            grid=(nblk,),
            in_specs=[pl.BlockSpec((pl.Element(depth), pl.Element(W_ROWS), pl.Element(d)),
                                   lambda c, st, va: (0, pl.multiple_of(st[c], 8), 0))],
            out_specs=pl.BlockSpec((depth, W_ROWS, d), lambda c, st, va: (0, c, 0))),
        out_shape=jax.ShapeDtypeStruct((depth, nblk * W_ROWS, d), BF16),
        compiler_params=_cparams("arbitrary", vmem=VMEM_LIMIT_SMALL),
        name="w_in_rows",
    )(jnp.asarray(starts, jnp.int32), jnp.asarray(valid, jnp.int32), w_t)


def _w_in_layout(w_in):
    w_t = jnp.swapaxes(w_in, 1, 2)
    src_if = 4 * MLSTM_W
    src_a = src_if + 2 * MLSTM_HEADS
    src_g = src_a + 3 * ATTN_W
    starts = list(range(0, src_if, W_ROWS)) + [src_g + k * W_ROWS for k in range((OFF_IF - OFF_GU) // W_ROWS)]
    valid = [W_ROWS] * len(starts)
    starts.append(src_if)
    valid.append(2 * MLSTM_HEADS)
    assert len(starts) * W_ROWS == N_PROJ and ATTN_GW == W_ROWS
    a_starts = [src_a + j * ATTN_W + g * ATTN_GW for g in range(len(ATTN_PATTERNS)) for j in range(3)]
    return _w_rows(w_t, starts, valid), _w_rows(w_t, a_starts, [W_ROWS] * len(a_starts))


def kernel(x, mem, norm_mix, w_in, mlstm_conv, mlstm_gate_b, mlstm_norm, attn_qk_norm, gmlp_norm, gmlp_ws,
           gmlp_bs, w_branch_a, w_branch_b, w_branch_c, w_out, rel_bias, norm_xattn, norm_mem, w_xq, w_xkv,
           xattn_qk_norm, w_xo, norm_ffn, router_w, router_b, w_expert_gate, w_expert_up, w_expert_down):
    b, s, d = x.shape
    t = b * s
    depth = w_in.shape[0]
    x2d = x.reshape(t, d)

    biases = [_attn_bias(rel_bias, g) for g in range(len(ATTN_PATTERNS))]
    rw_t = jnp.zeros((N_EXPERT_GROUPS, 8, d), F32).at[:, :EXPERTS_PER_GROUP].set(
        router_w.T.reshape(N_EXPERT_GROUPS, EXPERTS_PER_GROUP, d)).reshape(ROUTER_ROWS, d)
    rb = jnp.full((N_EXPERT_GROUPS, 8), NEG, F32).at[:, :EXPERTS_PER_GROUP].set(
        router_b.astype(F32).reshape(N_EXPERT_GROUPS, EXPERTS_PER_GROUP)).reshape(ROUTER_ROWS, 1)
    tril = jnp.tril(jnp.ones((GMLP_CHUNK, GMLP_CHUNK), bool))
    head_of = jnp.arange(ATTN_GW) // ATTN_DH
    seg_ones = (head_of[:, None] == head_of[None, :]).astype(BF16)

    w_main, w_attn = _w_in_layout(w_in)

    for l in range(depth):
        proj, h_mix, gates_t = _inproj(x2d, norm_mix[l][None], w_main, layer=l, tm=INPROJ_TM,
                                       tn=INPROJ_TN)
        gq = jnp.tile(attn_qk_norm[l, 0], HEADS_PER_GROUP)[None]
        gk = jnp.tile(attn_qk_norm[l, 1], HEADS_PER_GROUP)[None]

        nh = MLSTM_HEADS
        bias_i = jnp.zeros((8, 1), F32).at[:nh, 0].set(mlstm_gate_b[l, :nh])
        bias_f = jnp.zeros((8, 1), F32).at[:nh, 0].set(mlstm_gate_b[l, nh:])
        ya = _mlstm_rows(proj, gates_t, mlstm_conv[l], bias_i, bias_f, mlstm_norm[l][None],
                         batch=b, seq=s, blk=MLSTM_BLOCK, group=MLSTM_GROUP)

        ybs, lses = [], []
        for g, (_, dilation) in enumerate(ATTN_PATTERNS):
            aproj = _attnproj(h_mix, w_attn, seg_ones, gq, gk, layer=l, group=g, dilation=dilation)
            o, lse = _dattn(aproj, biases[g], seq=s, group=g, dilation=dilation)
            ybs.append(o)
            lses.append(lse)

        ws = jnp.where(tril, gmlp_ws[l], 0.0).astype(BF16)
        bsb = jnp.broadcast_to(gmlp_bs[l][:, :, None], (GMLP_GROUPS, GMLP_CHUNK, GMLP_GC)).astype(F32)
        x2d = _merge(ya, ybs, lses, proj, x2d, w_branch_a[l].astype(BF16), w_branch_b[l].astype(BF16),
                     w_branch_c[l].astype(BF16), w_out[l].astype(BF16), ws, bsb, gmlp_norm[l][None],
                     tm=MERGE_TM)

        k_mem, v_mem = _memkv(mem, norm_mem[l][None], w_xkv[l].astype(BF16), xattn_qk_norm[l, 1][None])
        x2d, hf_rows, eidx, wts = _xattn(x2d, k_mem, v_mem, norm_xattn[l][None], w_xq[l].astype(BF16),
                                         xattn_qk_norm[l, 0][None], w_xo[l].astype(BF16), norm_ffn[l][None],
                                         rw_t, rb, seq=s, tm=XATTN_TM)

        x2d = _moe(x2d, hf_rows, eidx, wts, w_expert_gate, w_expert_up, w_expert_down, layer=l)

    return x2d.reshape(b, s, d)
```

```python
import functools
import math

import jax
import jax.numpy as jnp
import numpy as np
from jax import lax
from jax.experimental import pallas as pl
from jax.experimental.pallas import tpu as pltpu
from jax.experimental.pallas import tpu_sc as plsc

F32 = jnp.float32
BF16 = jnp.bfloat16

EPS = 1e-6
NEG = -1e30

MLSTM_HEADS = 4
MLSTM_DH = 128
MLSTM_W = MLSTM_HEADS * MLSTM_DH
CONV_WIDTH = 4
MLSTM_BLOCK = 128
MLSTM_GROUP = 4

ATTN_PATTERNS = ((128, 1), (512, 4), (2048, 16))
HEADS_PER_GROUP = 4
ATTN_DH = 64
ATTN_GW = HEADS_PER_GROUP * ATTN_DH
ATTN_W = len(ATTN_PATTERNS) * ATTN_GW
ATTN_BLOCK = 128
REL_BUCKETS = 32
REL_MAX_DIST = 2048

GMLP_GROUPS = 4
GMLP_GC = 128
GMLP_W = GMLP_GROUPS * GMLP_GC
GMLP_CHUNK = 128

XATTN_HEADS = 4
XATTN_DH = 128
XATTN_W = XATTN_HEADS * XATTN_DH
XATTN_SUB = 1024

N_EXPERTS = 16
N_EXPERT_GROUPS = 4
EXPERTS_PER_GROUP = 4
ROUTER_ROWS = 8 * N_EXPERT_GROUPS

N_BRANCH = 3

MOE_TM = 1024
ROW_CHUNKS = 4
SC_CORES, SC_SUBCORES = 2, 16
SC_WINDOW = 128

OFF_MQ, OFF_MK, OFF_MV, OFF_MO = 0, 512, 1024, 1536
OFF_GU, OFF_GV = 2048, 2560
OFF_GATE = 3072
OFF_IF = 6144
IF_PAD = 256
N_PROJ = OFF_IF + IF_PAD

ATTN_TILE = 2048
ATTN_SUB = ATTN_TILE // ATTN_BLOCK
ATTN_SLAB = 2 * ATTN_DH
ATTN_COLS = HEADS_PER_GROUP * ATTN_SLAB + 2 * ATTN_GW

VMEM_LIMIT = 48 * 1024 * 1024
VMEM_LIMIT_INPROJ = 56 * 1024 * 1024
VMEM_LIMIT_SMALL = 24 * 1024 * 1024

INPROJ_TM, INPROJ_TN = 1024, 3072
ATTNPROJ_SUB = 512
MERGE_TM = 512
XATTN_TM = 1024
COMBINE_TM = 512
PLAN_TB = 1024


def _cparams(*sem, vmem=VMEM_LIMIT):
    return pltpu.CompilerParams(dimension_semantics=sem, vmem_limit_bytes=vmem)


def _rms(x, gain):
    return x * lax.rsqrt(jnp.mean(x * x, axis=-1, keepdims=True) + EPS) * gain


def _sigmoid(x):
    return 0.5 * jnp.tanh(0.5 * x) + 0.5


def _silu(x):
    half = 0.5 * x
    return half + half * jnp.tanh(half)


def _gelu(x):
    c = math.sqrt(2.0 / math.pi)
    half = 0.5 * x
    return half + half * jnp.tanh(x * (c + (c * 0.044715) * (x * x)))


def _dot(a, b):
    return jnp.dot(a, b, preferred_element_type=F32)


def _dot_nt(a, b):
    return lax.dot_general(a, b, (((1,), (1,)), ((), ())), preferred_element_type=F32)


def _inproj_kernel(x_ref, g_ref, w_ref, wg_ref, o_ref, h_ref, gt_ref):
    @pl.when(pl.program_id(1) == 0)
    def _():
        h = _rms(x_ref[...], g_ref[...]).astype(BF16)
        h_ref[...] = h
        gt_ref[...] = _dot_nt(wg_ref[0, 0:128, :], h)[:gt_ref.shape[0], :]

    o_ref[...] = _dot_nt(h_ref[...], w_ref[0]).astype(o_ref.dtype)


def _inproj(x2d, gain, w, *, layer, tm, tn):
    t, d = x2d.shape
    n = OFF_IF
    return pl.pallas_call(
        _inproj_kernel,
        grid=(t // tm, n // tn),
        in_specs=[pl.BlockSpec((tm, d), lambda i, j: (i, 0)),
                  pl.BlockSpec((1, d), lambda i, j: (0, 0)),
                  pl.BlockSpec((1, tn, d), lambda i, j: (layer, j, 0)),
                  pl.BlockSpec((1, IF_PAD, d), lambda i, j: (layer, OFF_IF // IF_PAD, 0))],
        out_specs=[pl.BlockSpec((tm, tn), lambda i, j: (i, j)),
                   pl.BlockSpec((tm, d), lambda i, j: (i, 0)),
                   pl.BlockSpec((8, tm), lambda i, j: (0, i))],
        out_shape=[jax.ShapeDtypeStruct((t, n), BF16), jax.ShapeDtypeStruct((t, d), BF16),
                   jax.ShapeDtypeStruct((8, t), F32)],
        compiler_params=_cparams("parallel", "arbitrary", vmem=VMEM_LIMIT_INPROJ),
        name="inproj",
    )(x2d, gain, w, w)


def _log_sigmoid(x):
    return jnp.minimum(x, 0.0) - jnp.log(1.0 + jnp.exp(-jnp.abs(x)))


def _split_bf16(x):
    hi = x.astype(BF16)
    return hi, (x - hi.astype(F32)).astype(BF16)


def _prefix_max(x):
    n = x.shape[1]
    lane = lax.broadcasted_iota(jnp.int32, x.shape, 1)
    shift = 1
    while shift < n:
        x = jnp.maximum(x, jnp.where(lane >= shift, pltpu.roll(x, shift, 1), NEG))
        shift *= 2
    return x


def _mlstm_rows_kernel(qk_ref, v_ref, og_ref, *rest, blk, group):
    gate_refs = rest[:group]
    cw_ref, bi_ref, bf_ref, ng_ref, y_ref, xe_scr, s_scr, m_scr = rest[group:]
    heads, dh, w = MLSTM_HEADS, MLSTM_DH, MLSTM_W

    @pl.when(pl.program_id(1) == 0)
    def _():
        xe_scr[:, 0:8, :] = jnp.zeros((group, 8, 2 * w), F32)
        s_scr[...] = jnp.zeros_like(s_scr)
        m_scr[...] = jnp.zeros_like(m_scr)

    cw = cw_ref[...]
    causal = lax.broadcasted_iota(jnp.int32, (blk, blk), 0) >= lax.broadcasted_iota(jnp.int32, (blk, blk), 1)
    triu = (lax.broadcasted_iota(jnp.int32, (blk, blk), 0)
            <= lax.broadcasted_iota(jnp.int32, (blk, blk), 1)).astype(BF16)
    ones = jnp.ones((blk, dh), BF16)
    s_in = [[s_scr[g, h] for h in range(heads)] for g in range(group)]
    m_in = [m_scr[g, :, 0:1] for g in range(group)]
    s_out = [[None] * heads for _ in range(group)]
    m_out = [None] * group
    per_seq = []
    for g in range(group):
        xe_scr[g, 8:8 + blk, :] = qk_ref[g].astype(F32)
        conv = cw[CONV_WIDTH - 1:CONV_WIDTH, :] * xe_scr[g, 8:8 + blk, :]
        for j in range(CONV_WIDTH - 1):
            off = 8 - (CONV_WIDTH - 1) + j
            conv = conv + cw[j:j + 1, :] * xe_scr[g, off:off + blk, :]
        xe_scr[g, 0:8, :] = xe_scr[g, blk:blk + 8, :]
        qk = _silu(conv)

        gates = gate_refs[g][...]
        i_r = gates + bi_ref[...]
        lf_hi, lf_lo = _split_bf16(_log_sigmoid(pltpu.roll(gates, heads, 0) + bf_ref[...]))
        b_r = _dot(lf_hi, triu) + _dot(lf_lo, triu)
        m_st = m_in[g]
        a_r = i_r - b_r
        inter = b_r + m_st
        m_t = jnp.maximum(inter, b_r + _prefix_max(a_r))
        b_last = b_r[:, blk - 1:blk]
        dec = b_last - b_r + i_r
        m_new = jnp.maximum(b_last + m_st, jnp.max(dec, axis=1, keepdims=True))
        w_c = jnp.exp(b_last + m_st - m_new)
        m_out[g] = m_new
        pack = jnp.concatenate([b_r - m_t, jnp.exp(inter - m_t), jnp.exp(-m_t), jnp.exp(dec - m_new),
                                jnp.zeros((blk - 32, blk), F32)], axis=0)
        per_seq.append((qk, a_r, pack.T, w_c))

    chains = [(g, h) for h in range(heads) for g in range(group)]
    st = {}
    for g, h in chains:
        qk = per_seq[g][0]
        sl = slice(h * dh, (h + 1) * dh)
        q_b = qk[:, sl].astype(BF16)
        k_f = qk[:, w + h * dh:w + (h + 1) * dh] * (dh ** -0.5)
        v_ext = jnp.concatenate([v_ref[g, :, sl], ones], axis=-1)
        st[g, h] = (q_b, k_f, v_ext, _dot_nt(q_b, k_f.astype(BF16)), _dot(q_b, s_in[g][h].astype(BF16)))
    for g, h in chains:
        q_b, k_f, v_ext, qk_t, q_state = st[g, h]
        _, a_r, cols, _ = per_seq[g]
        u_c, w_inter = cols[:, h:h + 1], cols[:, 8 + h:9 + h]
        w_intra = jnp.exp(jnp.where(causal, u_c + a_r[h:h + 1, :], NEG))
        st[g, h] = (k_f, v_ext, _dot((qk_t * w_intra).astype(BF16), v_ext) + w_inter * q_state)
    for g, h in chains:
        k_f, v_ext, tot = st[g, h]
        _, _, cols, w_c = per_seq[g]
        em_c, w_k = cols[:, 16 + h:17 + h], cols[:, 24 + h:25 + h]
        sl = slice(h * dh, (h + 1) * dh)
        num, den = tot[:, :dh], tot[:, dh:]
        hh = num / jnp.maximum(jnp.abs(den), em_c)
        hn = _rms(hh, ng_ref[:, sl])
        y_ref[g, :, sl] = (hn * _sigmoid(og_ref[g, :, sl].astype(F32))).astype(y_ref.dtype)
        s_out[g][h] = w_c[h:h + 1, :] * s_in[g][h] + _dot((k_f * w_k).T.astype(BF16), v_ext)
    for g in range(group):
        m_scr[g] = jnp.broadcast_to(m_out[g], m_scr.shape[1:])
        for h in range(heads):
            s_scr[g, h] = s_out[g][h]


def _mlstm_rows(proj, gates_t, conv_w, bias_i, bias_f, norm_g, *, batch, seq, blk, group):
    t, npj = proj.shape
    w = MLSTM_W
    proj3 = proj.reshape(batch, seq, npj)
    cols = lambda c: (lambda b, i: (b, i, c))
    const2 = lambda b, i: (0, 0)
    nblk = seq // blk
    gate_specs = [pl.BlockSpec((8, blk), functools.partial(lambda b, i, g: (0, (b * group + g) * nblk + i), g=g))
                  for g in range(group)]
    y = pl.pallas_call(
        functools.partial(_mlstm_rows_kernel, blk=blk, group=group),
        grid=(batch // group, seq // blk),
        in_specs=[pl.BlockSpec((group, blk, 2 * w), cols(OFF_MQ // (2 * w))),
                  pl.BlockSpec((group, blk, w), cols(OFF_MV // w)),
                  pl.BlockSpec((group, blk, w), cols(OFF_MO // w)),
                  *gate_specs,
                  pl.BlockSpec((CONV_WIDTH, 2 * w), const2),
                  pl.BlockSpec((8, 1), const2), pl.BlockSpec((8, 1), const2),
                  pl.BlockSpec((1, w), const2)],
        out_specs=pl.BlockSpec((group, blk, w), cols(0)),
        out_shape=jax.ShapeDtypeStruct((batch, seq, w), BF16),
        scratch_shapes=[pltpu.VMEM((group, blk + 8, 2 * w), F32),
                        pltpu.VMEM((group, MLSTM_HEADS, MLSTM_DH, 2 * MLSTM_DH), F32),
                        pltpu.VMEM((group, 8, 128), F32)],
        compiler_params=_cparams("parallel", "arbitrary", vmem=VMEM_LIMIT_SMALL),
        name="mlstm",
    )(proj3, proj3, proj3, *([gates_t] * group), conv_w, bias_i, bias_f, norm_g)
    return y.reshape(t, w)


def _attnproj_kernel(h_ref, w_ref, seg_ref, gq_ref, gk_ref, o_ref, r_scr, *, dil):
    gw, half = ATTN_GW, ATTN_SLAB // 2
    sub_rows = ATTNPROJ_SUB
    seg, sub_seg = ATTN_TILE // dil, sub_rows // dil

    def head_norm(x, gain):
        ss = _dot((x * x).astype(BF16), seg_ref[...])
        return x * lax.rsqrt(ss * (1.0 / ATTN_DH) + EPS) * gain

    low = lax.broadcasted_iota(jnp.int32, (1, ATTN_SLAB), 1) < half
    for s in range(ATTN_TILE // sub_rows):
        rows = slice(s * sub_rows, (s + 1) * sub_rows)
        res = _dot_nt(h_ref[rows, :], w_ref[0])
        q = head_norm(res[:, :gw], gq_ref[...]) * (ATTN_DH ** -0.5)
        k = head_norm(res[:, gw:2 * gw], gk_ref[...])
        slabs = []
        for pair in range(gw // ATTN_SLAB):
            qp = q[:, pair * ATTN_SLAB:(pair + 1) * ATTN_SLAB]
            slabs += [jnp.where(low, qp, 0.0), jnp.where(low, 0.0, qp)]
        slabs += [k[:, c * 128:(c + 1) * 128] for c in range(gw // 128)]
        slabs += [res[:, 2 * gw + c * 128:2 * gw + (c + 1) * 128] for c in range(gw // 128)]
        pitch = dil + 1 if dil % 16 == 0 else dil
        for c, slab in enumerate(slabs):
            if dil == 1:
                o_ref[rows, c * 128:(c + 1) * 128] = slab.astype(o_ref.dtype)
            elif pitch == dil:
                r_scr[s % 2, c, 0:sub_rows, :] = slab
            else:
                for i in range(sub_seg):
                    r_scr[s % 2, c, pitch * i:pitch * i + dil, :] = slab[dil * i:dil * (i + 1), :]
        if dil > 1:
            for r in range(dil):
                dst = slice(r * seg + s * sub_seg, r * seg + (s + 1) * sub_seg)
                for c in range(r_scr.shape[1]):
                    o_ref[dst, c * 128:(c + 1) * 128] = (
                        r_scr[s % 2, c, pl.ds(r, sub_seg, stride=pitch), :].astype(o_ref.dtype))


def _attnproj(h, w, seg_ones, gq, gk, *, layer, group, dilation):
    t, d = h.shape
    wcols = 3 * ATTN_GW
    const2 = lambda i: (0, 0)
    return pl.pallas_call(
        functools.partial(_attnproj_kernel, dil=dilation),
        grid=(t // ATTN_TILE,),
        in_specs=[pl.BlockSpec((ATTN_TILE, d), lambda i: (i, 0)),
                  pl.BlockSpec((1, wcols, d), lambda i: (layer, group, 0)),
                  pl.BlockSpec((ATTN_GW, ATTN_GW), const2),
                  pl.BlockSpec((1, ATTN_GW), const2), pl.BlockSpec((1, ATTN_GW), const2)],
        out_specs=pl.BlockSpec((ATTN_TILE, ATTN_COLS), lambda i: (i, 0)),
        out_shape=jax.ShapeDtypeStruct((t, ATTN_COLS), BF16),
        scratch_shapes=[pltpu.VMEM((2, ATTN_COLS // 128, ATTNPROJ_SUB + ATTNPROJ_SUB // 16, 128), F32)],
        compiler_params=_cparams("parallel"),
        name=f"attnproj{group}",
    )(h, w, seg_ones, gq, gk)


def _dattn_kernel(q_ref, kc_ref, kp_ref, vc_ref, vp_ref, b0_ref, o_ref, lse_ref,
                  kx_scr, vx_scr, o_scr, l_scr, bias_scr, *, dil):
    blk = ATTN_BLOCK
    per = ATTN_SUB // dil
    pitch = dil + 1 if dil % 16 == 0 else dil
    first_tile = pl.program_id(1) == 0

    @pl.when(first_tile)
    def _():
        for h in range(HEADS_PER_GROUP):
            bias_scr[h] = pltpu.roll(jnp.broadcast_to(b0_ref[h], (blk, 2 * blk)), 0, 1, stride=1, stride_axis=0)

    for r in range(dil):
        base = r * (per + 1) * blk
        last = slice((r * per + per - 1) * blk, (r * per + per) * blk)
        mine = slice(r * per * blk, (r + 1) * per * blk)
        kx_scr[base:base + blk, :] = kp_ref[last, :]
        vx_scr[base:base + blk, :] = vp_ref[last, :]
        kx_scr[base + blk:base + (per + 1) * blk, :] = kc_ref[mine, :]
        vx_scr[base + blk:base + (per + 1) * blk, :] = vc_ref[mine, :]

    low = lax.broadcasted_iota(jnp.int32, (1, ATTN_SLAB), 1) < ATTN_SLAB // 2
    no_prev = lax.broadcasted_iota(jnp.int32, (1, 2 * blk), 1) < blk
    for r in range(dil):
        for sub in range(per):
            u = r * per + sub
            win = slice((r * (per + 1) + sub) * blk, (r * (per + 1) + sub + 2) * blk)
            o_slabs, l_slabs = [], []
            for pair in range(ATTN_GW // ATTN_SLAB):
                cols = slice(pair * ATTN_SLAB, (pair + 1) * ATTN_SLAB)
                kx, vx = kx_scr[win, cols], vx_scr[win, cols]
                o_pair, l_pair = [], []
                for h in (2 * pair, 2 * pair + 1):
                    logits = _dot_nt(q_ref[u * blk:(u + 1) * blk, h * ATTN_SLAB:(h + 1) * ATTN_SLAB], kx)
                    logits = logits + bias_scr[h]
                    if sub == 0:
                        logits = jnp.where(first_tile & no_prev, NEG, logits)
                    m = jnp.max(logits, axis=-1, keepdims=True)
                    p = jnp.exp(logits - m)
                    l = jnp.sum(p, axis=-1, keepdims=True)
                    o_pair.append(_dot(p.astype(BF16), vx) / l)
                    l_pair.append(m + jnp.log(l))
                o_slabs.append(jnp.where(low, o_pair[0], o_pair[1]))
                l_slabs.append(jnp.where(low, l_pair[0], l_pair[1]))
            for c in range(ATTN_GW // ATTN_SLAB):
                cols = slice(c * ATTN_SLAB, (c + 1) * ATTN_SLAB)
                if dil == 1:
                    o_ref[u * blk:(u + 1) * blk, cols] = o_slabs[c].astype(o_ref.dtype)
                    lse_ref[u * blk:(u + 1) * blk, cols] = l_slabs[c]
                else:
                    dst = pl.ds(sub * blk * pitch + r, blk, stride=pitch)
                    o_scr[c, dst, :] = o_slabs[c]
                    l_scr[c, dst, :] = l_slabs[c]
    if dil > 1:
        for c in range(ATTN_GW // ATTN_SLAB):
            cols = slice(c * ATTN_SLAB, (c + 1) * ATTN_SLAB)
            if pitch == dil:
                o_ref[:, cols] = o_scr[c, 0:ATTN_TILE, :].astype(o_ref.dtype)
                lse_ref[:, cols] = l_scr[c, 0:ATTN_TILE, :]
            else:
                for i in range(ATTN_TILE // dil):
                    o_ref[dil * i:dil * (i + 1), cols] = o_scr[c, pitch * i:pitch * i + dil, :].astype(o_ref.dtype)
                    lse_ref[dil * i:dil * (i + 1), cols] = l_scr[c, pitch * i:pitch * i + dil, :]


def _dattn(aproj, bias, *, seq, group, dilation):
    t = aproj.shape[0]
    tiles = seq // ATTN_TILE
    qw = HEADS_PER_GROUP * ATTN_SLAB
    cq, ck, cv = 0, qw // ATTN_GW, qw // ATTN_GW + 1
    blk = (ATTN_TILE, ATTN_GW)
    cur = lambda c: (lambda b, j: (b * tiles + j, c))
    prev = lambda c: (lambda b, j: (b * tiles + jnp.maximum(j - 1, 0), c))
    xrows = ATTN_TILE + dilation * ATTN_BLOCK
    return pl.pallas_call(
        functools.partial(_dattn_kernel, dil=dilation),
        grid=(t // seq, tiles),
        in_specs=[pl.BlockSpec((ATTN_TILE, qw), cur(cq)),
                  pl.BlockSpec(blk, cur(ck)), pl.BlockSpec(blk, prev(ck)),
                  pl.BlockSpec(blk, cur(cv)), pl.BlockSpec(blk, prev(cv)),
                  pl.BlockSpec((HEADS_PER_GROUP, 1, 2 * ATTN_BLOCK), lambda b, j: (0, 0, 0))],
        out_specs=[pl.BlockSpec(blk, cur(0)), pl.BlockSpec(blk, cur(0))],
        out_shape=[jax.ShapeDtypeStruct((t, ATTN_GW), BF16), jax.ShapeDtypeStruct((t, ATTN_GW), F32)],
        scratch_shapes=[pltpu.VMEM((xrows, ATTN_GW), BF16), pltpu.VMEM((xrows, ATTN_GW), BF16),
                        pltpu.VMEM((ATTN_GW // ATTN_SLAB, ATTN_TILE + ATTN_TILE // 16, ATTN_SLAB), F32),
                        pltpu.VMEM((ATTN_GW // ATTN_SLAB, ATTN_TILE + ATTN_TILE // 16, ATTN_SLAB), F32),
                        pltpu.VMEM((HEADS_PER_GROUP, ATTN_BLOCK, 2 * ATTN_BLOCK), F32)],
        compiler_params=_cparams("parallel", "arbitrary"),
        name=f"dattn{group}",
    )(aproj, aproj, aproj, aproj, aproj, bias)


def _rel_bucket(n):
    max_exact = REL_BUCKETS // 2
    nf = jnp.maximum(n, 1).astype(F32)
    log_b = max_exact + (jnp.log(nf / max_exact) / math.log(REL_MAX_DIST / max_exact)
                         * (REL_BUCKETS - max_exact)).astype(jnp.int32)
    return jnp.where(n < max_exact, n, jnp.minimum(log_b, REL_BUCKETS - 1))


def _attn_bias(rel_bias, group):
    window, dilation = ATTN_PATTERNS[group]
    steps = window // dilation
    assert steps == ATTN_BLOCK
    hs = slice(group * HEADS_PER_GROUP, (group + 1) * HEADS_PER_GROUP)
    bucket = _rel_bucket((steps - jnp.arange(steps + 1)) * dilation)
    by_dist = jnp.dot(jax.nn.one_hot(bucket, REL_BUCKETS, dtype=F32), rel_bias[:, hs].astype(F32),
                      precision=lax.Precision.HIGHEST)
    row0 = jnp.concatenate([by_dist, jnp.full((2 * ATTN_BLOCK - steps - 1, HEADS_PER_GROUP), NEG, F32)], axis=0)
    return row0.T[:, None, :]


def _merge_kernel(ya_ref, yb0_ref, yb1_ref, yb2_ref, l0_ref, l1_ref, l2_ref, gu_ref, gv_ref, gate_ref,
                  x_ref, wa_ref, wb_ref, wc_ref, wo_ref, ws_ref, bs_ref, gg_ref, o_ref, yc_scr, *, tm):
    d = x_ref.shape[1]
    l0, l1, l2 = l0_ref[...], l1_ref[...], l2_ref[...]
    mx = jnp.maximum(jnp.maximum(l0, l1), l2)
    e0, e1, e2 = jnp.exp(l0 - mx), jnp.exp(l1 - mx), jnp.exp(l2 - mx)
    inv = 1.0 / (e0 + e1 + e2)
    yb = jnp.concatenate([(yb0_ref[...].astype(F32) * (e0 * inv)).astype(BF16),
                          (yb1_ref[...].astype(F32) * (e1 * inv)).astype(BF16),
                          (yb2_ref[...].astype(F32) * (e2 * inv)).astype(BF16)], axis=-1)

    for j in range(tm // GMLP_CHUNK):
        rows = slice(j * GMLP_CHUNK, (j + 1) * GMLP_CHUNK)
        for g in range(GMLP_GROUPS):
            cols = slice(g * GMLP_GC, (g + 1) * GMLP_GC)
            u = _gelu(gu_ref[rows, cols].astype(F32))
            v = _rms(_gelu(gv_ref[rows, cols].astype(F32)), gg_ref[:, cols])
            mixed = _dot(ws_ref[g], v.astype(BF16)) + bs_ref[g]
            yc_scr[rows, cols] = (u * mixed).astype(BF16)

    def gate2(k):
        return jnp.tanh(0.5 * gate_ref[:, k * d:(k + 1) * d].astype(F32)) + 1.0

    merged2 = gate2(0) * _dot(ya_ref[...], wa_ref[...])
    merged2 = merged2 + gate2(1) * _dot(yb, wb_ref[...])
    merged2 = merged2 + gate2(2) * _dot(yc_scr[...], wc_ref[...])
    o_ref[...] = x_ref[...] + 0.5 * _dot(merged2.astype(BF16), wo_ref[...])


def _merge(ya, ybs, lses, proj, x2d, wa, wb, wc, wo, ws, bsb, gg, *, tm):
    t, d = x2d.shape
    row = lambda c: (lambda i: (i, c))
    full2 = lambda i: (0, 0)
    full3 = lambda i: (0, 0, 0)
    gspec = pl.BlockSpec((tm, ATTN_GW), row(0))
    return pl.pallas_call(
        functools.partial(_merge_kernel, tm=tm),
        grid=(t // tm,),
        in_specs=[pl.BlockSpec((tm, MLSTM_W), row(0)),
                  gspec, gspec, gspec, gspec, gspec, gspec,
                  pl.BlockSpec((tm, GMLP_W), row(OFF_GU // GMLP_W)),
                  pl.BlockSpec((tm, GMLP_W), row(OFF_GV // GMLP_W)),
                  pl.BlockSpec((tm, N_BRANCH * d), row(OFF_GATE // (N_BRANCH * d))),
                  pl.BlockSpec((tm, d), row(0)),
                  pl.BlockSpec(wa.shape, full2), pl.BlockSpec(wb.shape, full2),
                  pl.BlockSpec(wc.shape, full2), pl.BlockSpec(wo.shape, full2),
                  pl.BlockSpec(ws.shape, full3), pl.BlockSpec(bsb.shape, full3),
                  pl.BlockSpec(gg.shape, full2)],
        out_specs=pl.BlockSpec((tm, d), row(0)),
        out_shape=jax.ShapeDtypeStruct((t, d), F32),
        scratch_shapes=[pltpu.VMEM((tm, GMLP_W), BF16)],
        compiler_params=_cparams("parallel"),
        name="merge",
    )(ya, *ybs, *lses, proj, proj, proj, x2d, wa, wb, wc, wo, ws, bsb, gg)


def _memkv_kernel(mem_ref, g_ref, w_ref, gk_ref, k_ref, v_ref):
    dh, w = XATTN_DH, XATTN_W
    kv = _dot(_rms(mem_ref[0], g_ref[...]).astype(BF16), w_ref[...])
    for h in range(XATTN_HEADS):
        sl = slice(h * dh, (h + 1) * dh)
        k_ref[0, :, sl] = _rms(kv[:, sl], gk_ref[...]).astype(k_ref.dtype)
    v_ref[0] = kv[:, w:].astype(v_ref.dtype)


def _memkv(mem, gain, w_kv, gk):
    b, m, d = mem.shape
    full2 = lambda i: (0, 0)
    return pl.pallas_call(
        _memkv_kernel,
        grid=(b,),
        in_specs=[pl.BlockSpec((1, m, d), lambda i: (i, 0, 0)),
                  pl.BlockSpec((1, d), full2),
                  pl.BlockSpec(w_kv.shape, full2),
                  pl.BlockSpec((1, XATTN_DH), full2)],
        out_specs=[pl.BlockSpec((1, m, XATTN_W), lambda i: (i, 0, 0)),
                   pl.BlockSpec((1, m, XATTN_W), lambda i: (i, 0, 0))],
        out_shape=[jax.ShapeDtypeStruct((b, m, XATTN_W), BF16),
                   jax.ShapeDtypeStruct((b, m, XATTN_W), BF16)],
        compiler_params=_cparams("parallel", vmem=VMEM_LIMIT_SMALL),
        name="memkv",
    )(mem, gain, w_kv, gk)


def _route(logits):
    tm = logits.shape[1]
    e = jnp.exp(logits - jnp.max(logits, axis=0, keepdims=True))
    probs = e / jnp.sum(e, axis=0, keepdims=True)
    rowi = lax.broadcasted_iota(jnp.int32, (8, tm), 0)
    real = rowi < EXPERTS_PER_GROUP
    tops = []
    for g in range(N_EXPERT_GROUPS):
        pg = jnp.where(real, probs[8 * g:8 * g + 8, :], -0.5)
        m1 = jnp.max(pg, axis=0, keepdims=True)
        i1 = jnp.min(jnp.where(pg == m1, rowi, 8), axis=0, keepdims=True)
        pg2 = jnp.where(rowi == i1, -1.0, pg)
        m2 = jnp.max(pg2, axis=0, keepdims=True)
        i2 = jnp.min(jnp.where(pg2 == m2, rowi, 8), axis=0, keepdims=True)
        tops.append((m1, i1, m2, i2))
    best = jnp.zeros((1, tm), jnp.int32)
    best_score = tops[0][0] + tops[0][2]
    for g in range(1, N_EXPERT_GROUPS):
        score = tops[g][0] + tops[g][2]
        better = score > best_score
        best = jnp.where(better, g, best)
        best_score = jnp.where(better, score, best_score)
    m1, i1, m2, i2 = tops[0]
    for g in range(1, N_EXPERT_GROUPS):
        m1, i1, m2, i2 = (jnp.where(best == g, new, old) for new, old in zip(tops[g], (m1, i1, m2, i2)))
    tot = m1 + m2
    base = best * EXPERTS_PER_GROUP
    return base + i1, base + i2, m1 / tot, m2 / tot


def _pack_bf16_pairs(x):
    n = x.shape[1] // 2
    hi = lax.bitcast_convert_type(x[:, :n].astype(BF16).astype(F32), jnp.uint32)
    lo = lax.bitcast_convert_type(x[:, n:].astype(BF16).astype(F32), jnp.uint32)
    return hi | (lo >> 16)


def _unpack_bf16_pairs(p):
    hi = lax.bitcast_convert_type(p & jnp.uint32(0xFFFF0000), F32)
    lo = lax.bitcast_convert_type(p << 16, F32)
    return hi, lo


def _store_row_chunks(ref, packed):
    for j in range(ROW_CHUNKS):
        ref[j] = packed[:, j * 128:(j + 1) * 128]


def _load_row_chunks(ref):
    return jnp.concatenate([ref[j] for j in range(ROW_CHUNKS)], axis=-1)


def _xattn_kernel(x_ref, k_ref, v_ref, gx_ref, wq_ref, gq_ref, wo_ref, gf_ref, rw_ref, rb_ref,
                  xo_ref, hf_ref, eidx_ref, wts_ref, *, sub):
    dh = XATTN_DH
    rw = rw_ref[...]
    rw_hi, rw_lo = _split_bf16(rw)
    for s in range(x_ref.shape[0] // sub):
        rows = slice(s * sub, (s + 1) * sub)
        x = x_ref[rows, :]
        q = _dot(_rms(x, gx_ref[...]).astype(BF16), wq_ref[...])
        outs = []
        for h in range(XATTN_HEADS):
            sl = slice(h * dh, (h + 1) * dh)
            q_h = (_rms(q[:, sl], gq_ref[...]) * (dh ** -0.5)).astype(BF16)
            logits = _dot_nt(q_h, k_ref[0, :, sl])
            p = jnp.exp(logits - jnp.max(logits, axis=-1, keepdims=True))
            o = _dot(p.astype(BF16), v_ref[0, :, sl]) / jnp.sum(p, axis=-1, keepdims=True)
            outs.append(o.astype(BF16))
        xn = x + _dot(jnp.concatenate(outs, axis=-1), wo_ref[...])
        xo_ref[rows, :] = xn
        hf = _rms(xn, gf_ref[...])
        packed = _pack_bf16_pairs(hf)
        for j in range(ROW_CHUNKS):
            hf_ref[j, rows, :] = packed[:, j * 128:(j + 1) * 128]
        hf_hi, hf_lo = _split_bf16(hf)
        logits_t = _dot_nt(rw_hi, hf_hi) + _dot_nt(rw_hi, hf_lo) + _dot_nt(rw_lo, hf_hi) + rb_ref[...]
        e1, e2, w1, w2 = _route(logits_t)
        eidx_ref[:, rows] = jnp.concatenate([e1, e2, jnp.zeros((6, sub), jnp.int32)], axis=0)
        wts_ref[:, rows] = jnp.concatenate([w1, w2, jnp.zeros((6, sub), F32)], axis=0)


def _xattn(x2d, k, v, gx, wq, gq, wo, gf, rw_t, rb, *, seq, tm):
    t, d = x2d.shape
    per_b = seq // tm
    full2 = lambda i: (0, 0)
    kv_spec = pl.BlockSpec((1,) + k.shape[1:], lambda i: (i // per_b, 0, 0))
    return pl.pallas_call(
        functools.partial(_xattn_kernel, sub=min(tm, XATTN_SUB)),
        grid=(t // tm,),
        in_specs=[pl.BlockSpec((tm, d), lambda i: (i, 0)), kv_spec, kv_spec,
                  pl.BlockSpec((1, d), full2), pl.BlockSpec(wq.shape, full2),
                  pl.BlockSpec((1, XATTN_DH), full2), pl.BlockSpec(wo.shape, full2),
                  pl.BlockSpec((1, d), full2), pl.BlockSpec(rw_t.shape, full2),
                  pl.BlockSpec(rb.shape, full2)],
        out_specs=[pl.BlockSpec((tm, d), lambda i: (i, 0)),
                   pl.BlockSpec((ROW_CHUNKS, tm, 128), lambda i: (0, i, 0)),
                   pl.BlockSpec((8, tm), lambda i: (0, i)),
                   pl.BlockSpec((8, tm), lambda i: (0, i))],
        out_shape=[jax.ShapeDtypeStruct((t, d), F32),
                   jax.ShapeDtypeStruct((ROW_CHUNKS, t, 128), jnp.uint32),
                   jax.ShapeDtypeStruct((8, t), jnp.int32),
                   jax.ShapeDtypeStruct((8, t), F32)],
        compiler_params=_cparams("parallel"),
        name="xattn_router",
    )(x2d, k, v, gx, wq, gq, wo, gf, rw_t, rb)


def _moe_plan_kernel(eidx_ref, i1_ref, i2_ref, te_ref, na_ref, cnt_scr, carry_scr, *, tb, tm, plane_rows):
    ne = N_EXPERTS
    hp = lax.Precision.HIGHEST
    phase, j = pl.program_id(0), pl.program_id(1)
    rows = lax.broadcasted_iota(jnp.int32, (ne, tb), 0)
    oh1 = rows == eidx_ref[0:1, :]
    oh2 = rows == eidx_ref[1:2, :]
    a = oh1.astype(F32) + oh2.astype(F32)
    blk_cnt = jnp.broadcast_to(jnp.sum(a, axis=1, keepdims=True), cnt_scr.shape)

    @pl.when((phase == 0) & (j == 0))
    def _():
        cnt_scr[...] = jnp.zeros_like(cnt_scr)

    @pl.when(phase == 0)
    def _():
        cnt_scr[...] += blk_cnt

    @pl.when((phase == 1) & (j == 0))
    def _():
        padded = jnp.ceil(cnt_scr[...] * (1.0 / tm)) * tm
        er = lax.broadcasted_iota(jnp.int32, (ne, ne), 0)
        ec = lax.broadcasted_iota(jnp.int32, (ne, ne), 1)
        off = jnp.dot((ec < er).astype(F32), padded, precision=hp, preferred_element_type=F32)
        carry_scr[...] = off
        seg_end = (off + padded)[:, 0:1]
        tile_start = lax.broadcasted_iota(jnp.int32, (ne, te_ref.shape[1]), 1).astype(F32) * tm
        te = jnp.sum((seg_end <= tile_start).astype(F32), axis=0, keepdims=True)
        te_ref[...] = jnp.broadcast_to(jnp.minimum(te, ne - 1.0), te_ref.shape).astype(jnp.int32)
        total = jnp.sum(padded[:, 0:1], axis=0, keepdims=True)
        na_ref[...] = jnp.broadcast_to(total * (1.0 / tm), na_ref.shape).astype(jnp.int32)

    @pl.when(phase == 1)
    def _():
        before = (lax.broadcasted_iota(jnp.int32, (tb, tb), 0)
                  < lax.broadcasted_iota(jnp.int32, (tb, tb), 1)).astype(BF16)
        rank = carry_scr[:, 0:1] + _dot(a.astype(BF16), before)
        d1 = jnp.sum(jnp.where(oh1, rank, 0.0), axis=0, keepdims=True).astype(jnp.int32)
        d2 = jnp.sum(jnp.where(oh2, rank, 0.0), axis=0, keepdims=True).astype(jnp.int32)
        plane = lax.broadcasted_iota(jnp.int32, (8, tb), 0) * plane_rows
        i1_ref[...] = jnp.where(plane < ROW_CHUNKS * plane_rows, plane + d1, 0)
        i2_ref[...] = jnp.where(plane < ROW_CHUNKS * plane_rows, plane + d2, 0)
        carry_scr[...] += blk_cnt


def _moe_plan(eidx, *, tm, n_tiles, tb=PLAN_TB):
    t = eidx.shape[1]
    ntp = -(-n_tiles // 128) * 128
    return pl.pallas_call(
        functools.partial(_moe_plan_kernel, tb=tb, tm=tm, plane_rows=n_tiles * tm),
        grid=(2, t // tb),
        in_specs=[pl.BlockSpec((8, tb), lambda p, j: (0, j))],
        out_specs=[pl.BlockSpec((8, tb), lambda p, j: (0, j * p)),
                   pl.BlockSpec((8, tb), lambda p, j: (0, j * p)),
                   pl.BlockSpec((8, ntp), lambda p, j: (0, 0)),
                   pl.BlockSpec((8, 128), lambda p, j: (0, 0))],
        out_shape=[jax.ShapeDtypeStruct((8, t), jnp.int32),
                   jax.ShapeDtypeStruct((8, t), jnp.int32),
                   jax.ShapeDtypeStruct((8, ntp), jnp.int32),
                   jax.ShapeDtypeStruct((8, 128), jnp.int32)],
        scratch_shapes=[pltpu.VMEM((N_EXPERTS, 128), F32), pltpu.VMEM((N_EXPERTS, 128), F32)],
        compiler_params=_cparams("arbitrary", "arbitrary", vmem=VMEM_LIMIT_SMALL),
        name="moe_plan",
    )(eidx)


def _sc_mesh():
    return plsc.VectorSubcoreMesh(core_axis_name="c", subcore_axis_name="s",
                                  num_cores=SC_CORES, num_subcores=SC_SUBCORES)


def _sc_index_spec(tokens):
    nb = tokens // SC_WINDOW
    return pl.BlockSpec((1, SC_WINDOW), lambda i: (i // nb, i % nb))


def _sc_dispatch(rows, i1, i2, n_out):
    n = rows.shape[0]
    tokens = i1.shape[1]

    @functools.partial(pl.kernel, out_type=jax.ShapeDtypeStruct((n_out, 128), rows.dtype), mesh=_sc_mesh(),
                       name="moe_dispatch")
    def k(x_hbm, i1_hbm, i2_hbm, o_hbm):
        def body(x_vmem, i1_vmem, i2_vmem):
            pltpu.sync_copy(x_vmem, o_hbm.at[i1_vmem.at[0]])
            pltpu.sync_copy(x_vmem, o_hbm.at[i2_vmem.at[0]])

        pltpu.emit_pipeline(
            body, grid=(n // SC_WINDOW,),
            in_specs=[pl.BlockSpec((SC_WINDOW, 128), lambda i: (i, 0)),
                      _sc_index_spec(tokens), _sc_index_spec(tokens)],
            out_specs=[],
            core_axis_name=("c", "s"), dimension_semantics=(pltpu.PARALLEL,),
        )(x_hbm, i1_hbm, i2_hbm)

    return k(rows, i1, i2)


def _sc_collect(table, i1, i2):
    tokens = i1.shape[1]
    n = ROW_CHUNKS * tokens
    out = jax.ShapeDtypeStruct((n, 128), table.dtype)

    @functools.partial(pl.kernel, out_type=(out, out), mesh=_sc_mesh(), name="moe_collect",
                       scratch_types=[pltpu.SemaphoreType.DMA, pltpu.SemaphoreType.DMA])
    def k(t_hbm, i1_hbm, i2_hbm, o1_hbm, o2_hbm, sem1, sem2):
        def body(i1_vmem, i2_vmem, o1_vmem, o2_vmem):
            first = pltpu.async_copy(t_hbm.at[i1_vmem.at[0]], o1_vmem, sem1)
            second = pltpu.async_copy(t_hbm.at[i2_vmem.at[0]], o2_vmem, sem2)
            first.wait()
            second.wait()

        pltpu.emit_pipeline(
            body, grid=(n // SC_WINDOW,),
            in_specs=[_sc_index_spec(tokens), _sc_index_spec(tokens)],
            out_specs=[pl.BlockSpec((SC_WINDOW, 128), lambda i: (i, 0)),
                       pl.BlockSpec((SC_WINDOW, 128), lambda i: (i, 0))],
            core_axis_name=("c", "s"), dimension_semantics=(pltpu.PARALLEL,),
        )(i1_hbm, i2_hbm, o1_hbm, o2_hbm)

    return k(table, i1, i2)


def _experts_kernel(te_ref, na_ref, xs_ref, wg_ref, wu_ref, wd_ref, y_ref, wg_scr, wu_scr, wd_scr):
    i = pl.program_id(0)
    active = i < na_ref[0]

    @pl.when(active & ((i == 0) | (te_ref[i] != te_ref[jnp.maximum(i - 1, 0)])))
    def _():
        wg_scr[...] = wg_ref[0, 0].astype(BF16)
        wu_scr[...] = wu_ref[0, 0].astype(BF16)
        wd_scr[...] = wd_ref[0, 0].astype(BF16)

    @pl.when(active)
    def _():
        hi, lo = _unpack_bf16_pairs(_load_row_chunks(xs_ref))
        h = jnp.concatenate([hi, lo], axis=-1).astype(BF16)
        up = _dot(h, wg_scr[...])
        act = _silu(up) * _dot(h, wu_scr[...])
        _store_row_chunks(y_ref, _pack_bf16_pairs(_dot(act.astype(BF16), wd_scr[...])))


def _experts(tile_expert, n_active, xs, wg, wu, wd, *, layer, tm):
    n_tiles = tile_expert.shape[0]
    _, _, d, dff = wg.shape
    rows = lambda i, te, na: (0, jnp.minimum(i, na[0] - 1), 0)
    expert = lambda i, te, na: (layer, te[i], 0, 0)
    return pl.pallas_call(
        _experts_kernel,
        grid_spec=pltpu.PrefetchScalarGridSpec(
            num_scalar_prefetch=2,
            grid=(n_tiles,),
            in_specs=[pl.BlockSpec((ROW_CHUNKS, tm, 128), rows),
                      pl.BlockSpec((1, 1, d, dff), expert),
                      pl.BlockSpec((1, 1, d, dff), expert),
                      pl.BlockSpec((1, 1, dff, d), expert)],
            out_specs=pl.BlockSpec((ROW_CHUNKS, tm, 128), rows),
            scratch_shapes=[pltpu.VMEM((d, dff), BF16), pltpu.VMEM((d, dff), BF16), pltpu.VMEM((dff, d), BF16)]),
        out_shape=jax.ShapeDtypeStruct(xs.shape, xs.dtype),
        compiler_params=_cparams("arbitrary"),
        name="moe_experts",
    )(tile_expert, n_active, xs, wg, wu, wd)


def _moe_combine_kernel(x_ref, y1_ref, y2_ref, w_ref, o_ref):
    half = x_ref.shape[1] // 2
    hi1, lo1 = _unpack_bf16_pairs(_load_row_chunks(y1_ref))
    hi2, lo2 = _unpack_bf16_pairs(_load_row_chunks(y2_ref))
    tm = x_ref.shape[0]
    w_cols = jnp.concatenate([w_ref[...], jnp.zeros((128 - w_ref.shape[0], tm), F32)], axis=0).T
    w1, w2 = w_cols[:, 0:1], w_cols[:, 1:2]
    o_ref[:, :half] = x_ref[:, :half] + w1 * hi1 + w2 * hi2
    o_ref[:, half:] = x_ref[:, half:] + w1 * lo1 + w2 * lo2


def _moe_combine(x2d, y1, y2, wts, *, tm):
    t, d = x2d.shape
    chunk_spec = pl.BlockSpec((ROW_CHUNKS, tm, 128), lambda i: (0, i, 0))
    return pl.pallas_call(
        _moe_combine_kernel,
        grid=(t // tm,),
        in_specs=[pl.BlockSpec((tm, d), lambda i: (i, 0)), chunk_spec, chunk_spec,
                  pl.BlockSpec((wts.shape[0], tm), lambda i: (0, i))],
        out_specs=pl.BlockSpec((tm, d), lambda i: (i, 0)),
        out_shape=jax.ShapeDtypeStruct((t, d), F32),
        compiler_params=_cparams("parallel", vmem=VMEM_LIMIT_SMALL),
        name="moe_combine",
    )(x2d, y1, y2, wts)


def _moe(x2d, hf_rows, eidx, wts, wg, wu, wd, *, layer):
    t = x2d.shape[0]
    tm = MOE_TM
    n_tiles = 2 * t // tm + N_EXPERTS
    plane = n_tiles * tm
    i1, i2, te, na = _moe_plan(eidx, tm=tm, n_tiles=n_tiles)
    xs = _sc_dispatch(hf_rows.reshape(ROW_CHUNKS * t, 128), i1, i2, ROW_CHUNKS * plane)
    ys = _experts(te[0, :n_tiles], na[0, :1], xs.reshape(ROW_CHUNKS, plane, 128), wg, wu, wd,
                  layer=layer, tm=tm)
    y1, y2 = _sc_collect(ys.reshape(ROW_CHUNKS * plane, 128), i1, i2)
    return _moe_combine(x2d, y1.reshape(ROW_CHUNKS, t, 128), y2.reshape(ROW_CHUNKS, t, 128), wts,
                        tm=COMBINE_TM)


W_ROWS = 256


def _w_rows_kernel(start_ref, valid_ref, w_ref, o_ref):
    del start_ref
    row = lax.broadcasted_iota(jnp.int32, w_ref.shape, 1)
    o_ref[...] = jnp.where(row < valid_ref[pl.program_id(0)], w_ref[...], 0.0).astype(o_ref.dtype)


def _w_rows(w_t, starts, valid):
    depth, _, d = w_t.shape
    nblk = len(starts)
    return pl.pallas_call(
        _w_rows_kernel,
        grid_spec=pltpu.PrefetchScalarGridSpec(
            num_scalar_prefetch=2,
            grid=(nblk,),
            in_specs=[pl.BlockSpec((pl.Element(depth), pl.Element(W_ROWS), pl.Element(d)),
                                   lambda c, st, va: (0, pl.multiple_of(st[c], 8), 0))],
            out_specs=pl.BlockSpec((depth, W_ROWS, d), lambda c, st, va: (0, c, 0))),
        out_shape=jax.ShapeDtypeStruct((depth, nblk * W_ROWS, d), BF16),
        compiler_params=_cparams("arbitrary", vmem=VMEM_LIMIT_SMALL),
        name="w_in_rows",
    )(jnp.asarray(starts, jnp.int32), jnp.asarray(valid, jnp.int32), w_t)


def _w_in_layout(w_in):
    w_t = jnp.swapaxes(w_in, 1, 2)
    src_if = 4 * MLSTM_W
    src_a = src_if + 2 * MLSTM_HEADS
    src_g = src_a + 3 * ATTN_W
    starts = list(range(0, src_if, W_ROWS)) + [src_g + k * W_ROWS for k in range((OFF_IF - OFF_GU) // W_ROWS)]
    valid = [W_ROWS] * len(starts)
    starts.append(src_if)
    valid.append(2 * MLSTM_HEADS)
    assert len(starts) * W_ROWS == N_PROJ and ATTN_GW == W_ROWS
    a_starts = [src_a + j * ATTN_W + g * ATTN_GW for g in range(len(ATTN_PATTERNS)) for j in range(3)]
    return _w_rows(w_t, starts, valid), _w_rows(w_t, a_starts, [W_ROWS] * len(a_starts))


def kernel(x, mem, norm_mix, w_in, mlstm_conv, mlstm_gate_b, mlstm_norm, attn_qk_norm, gmlp_norm, gmlp_ws,
           gmlp_bs, w_branch_a, w_branch_b, w_branch_c, w_out, rel_bias, norm_xattn, norm_mem, w_xq, w_xkv,
           xattn_qk_norm, w_xo, norm_ffn, router_w, router_b, w_expert_gate, w_expert_up, w_expert_down):
    b, s, d = x.shape
    t = b * s
    depth = w_in.shape[0]
    x2d = x.reshape(t, d)

    biases = [_attn_bias(rel_bias, g) for g in range(len(ATTN_PATTERNS))]
    rw_t = jnp.zeros((N_EXPERT_GROUPS, 8, d), F32).at[:, :EXPERTS_PER_GROUP].set(
        router_w.T.reshape(N_EXPERT_GROUPS, EXPERTS_PER_GROUP, d)).reshape(ROUTER_ROWS, d)
    rb = jnp.full((N_EXPERT_GROUPS, 8), NEG, F32).at[:, :EXPERTS_PER_GROUP].set(
        router_b.astype(F32).reshape(N_EXPERT_GROUPS, EXPERTS_PER_GROUP)).reshape(ROUTER_ROWS, 1)
    tril = jnp.tril(jnp.ones((GMLP_CHUNK, GMLP_CHUNK), bool))
    head_of = jnp.arange(ATTN_GW) // ATTN_DH
    seg_ones = (head_of[:, None] == head_of[None, :]).astype(BF16)

    w_main, w_attn = _w_in_layout(w_in)

    for l in range(depth):
        proj, h_mix, gates_t = _inproj(x2d, norm_mix[l][None], w_main, layer=l, tm=INPROJ_TM,
                                       tn=INPROJ_TN)
        gq = jnp.tile(attn_qk_norm[l, 0], HEADS_PER_GROUP)[None]
        gk = jnp.tile(attn_qk_norm[l, 1], HEADS_PER_GROUP)[None]

        nh = MLSTM_HEADS
        bias_i = jnp.zeros((8, 1), F32).at[:nh, 0].set(mlstm_gate_b[l, :nh])
        bias_f = jnp.zeros((8, 1), F32).at[:nh, 0].set(mlstm_gate_b[l, nh:])
        ya = _mlstm_rows(proj, gates_t, mlstm_conv[l], bias_i, bias_f, mlstm_norm[l][None],
                         batch=b, seq=s, blk=MLSTM_BLOCK, group=MLSTM_GROUP)

        ybs, lses = [], []
        for g, (_, dilation) in enumerate(ATTN_PATTERNS):
            aproj = _attnproj(h_mix, w_attn, seg_ones, gq, gk, layer=l, group=g, dilation=dilation)
            o, lse = _dattn(aproj, biases[g], seq=s, group=g, dilation=dilation)
            ybs.append(o)
            lses.append(lse)

        ws = jnp.where(tril, gmlp_ws[l], 0.0).astype(BF16)
        bsb = jnp.broadcast_to(gmlp_bs[l][:, :, None], (GMLP_GROUPS, GMLP_CHUNK, GMLP_GC)).astype(F32)
        x2d = _merge(ya, ybs, lses, proj, x2d, w_branch_a[l].astype(BF16), w_branch_b[l].astype(BF16),
                     w_branch_c[l].astype(BF16), w_out[l].astype(BF16), ws, bsb, gmlp_norm[l][None],
                     tm=MERGE_TM)

        k_mem, v_mem = _memkv(mem, norm_mem[l][None], w_xkv[l].astype(BF16), xattn_qk_norm[l, 1][None])
        x2d, hf_rows, eidx, wts = _xattn(x2d, k_mem, v_mem, norm_xattn[l][None], w_xq[l].astype(BF16),
                                         xattn_qk_norm[l, 0][None], w_xo[l].astype(BF16), norm_ffn[l][None],
                                         rw_t, rb, seq=s, tm=XATTN_TM)

        x2d = _moe(x2d, hf_rows, eidx, wts, w_expert_gate, w_expert_up, w_expert_down, layer=l)

    return x2d.reshape(b, s, d)
```

```python
import functools
import math

import jax
import jax.numpy as jnp
import numpy as np
from jax import lax
from jax.experimental import pallas as pl
from jax.experimental.pallas import tpu as pltpu
from jax.experimental.pallas import tpu_sc as plsc

F32 = jnp.float32
BF16 = jnp.bfloat16

EPS = 1e-6
NEG = -1e30

MLSTM_HEADS = 4
MLSTM_DH = 128
MLSTM_W = MLSTM_HEADS * MLSTM_DH
CONV_WIDTH = 4
MLSTM_BLOCK = 128
MLSTM_GROUP = 4

ATTN_PATTERNS = ((128, 1), (512, 4), (2048, 16))
HEADS_PER_GROUP = 4
ATTN_DH = 64
ATTN_GW = HEADS_PER_GROUP * ATTN_DH
ATTN_W = len(ATTN_PATTERNS) * ATTN_GW
ATTN_BLOCK = 128
REL_BUCKETS = 32
REL_MAX_DIST = 2048

GMLP_GROUPS = 4
GMLP_GC = 128
GMLP_W = GMLP_GROUPS * GMLP_GC
GMLP_CHUNK = 128

XATTN_HEADS = 4
XATTN_DH = 128
XATTN_W = XATTN_HEADS * XATTN_DH
XATTN_SUB = 1024

N_EXPERTS = 16
N_EXPERT_GROUPS = 4
EXPERTS_PER_GROUP = 4
ROUTER_ROWS = 8 * N_EXPERT_GROUPS

N_BRANCH = 3

MOE_TM = 1024
ROW_CHUNKS = 4
SC_CORES, SC_SUBCORES = 2, 16
SC_WINDOW = 128

OFF_MQ, OFF_MK, OFF_MV, OFF_MO = 0, 512, 1024, 1536
OFF_GU, OFF_GV = 2048, 2560
OFF_GATE = 3072
OFF_IF = 6144
IF_PAD = 256
N_PROJ = OFF_IF + IF_PAD

ATTN_TILE = 2048
ATTN_SUB = ATTN_TILE // ATTN_BLOCK
ATTN_SLAB = 2 * ATTN_DH
ATTN_COLS = HEADS_PER_GROUP * ATTN_SLAB + 2 * ATTN_GW

VMEM_LIMIT = 48 * 1024 * 1024
VMEM_LIMIT_INPROJ = 56 * 1024 * 1024
VMEM_LIMIT_SMALL = 24 * 1024 * 1024

INPROJ_TM, INPROJ_TN = 1024, 3072
ATTNPROJ_SUB = 512
MERGE_TM = 512
XATTN_TM = 1024
COMBINE_TM = 1024
PLAN_TB = 1024


def _cparams(*sem, vmem=VMEM_LIMIT):
    return pltpu.CompilerParams(dimension_semantics=sem, vmem_limit_bytes=vmem)


def _rms(x, gain):
    return x * lax.rsqrt(jnp.mean(x * x, axis=-1, keepdims=True) + EPS) * gain


def _sigmoid(x):
    return 0.5 * jnp.tanh(0.5 * x) + 0.5


def _silu(x):
    half = 0.5 * x
    return half + half * jnp.tanh(half)


def _gelu(x):
    c = math.sqrt(2.0 / math.pi)
    half = 0.5 * x
    return half + half * jnp.tanh(x * (c + (c * 0.044715) * (x * x)))


def _dot(a, b):
    return jnp.dot(a, b, preferred_element_type=F32)


def _dot_nt(a, b):
    return lax.dot_general(a, b, (((1,), (1,)), ((), ())), preferred_element_type=F32)


def _inproj_kernel(x_ref, g_ref, w_ref, wg_ref, o_ref, h_ref, gt_ref):
    @pl.when(pl.program_id(1) == 0)
    def _():
        h = _rms(x_ref[...], g_ref[...]).astype(BF16)
        h_ref[...] = h
        gt_ref[...] = _dot_nt(wg_ref[0, 0:128, :], h)[:gt_ref.shape[0], :]

    o_ref[...] = _dot_nt(h_ref[...], w_ref[0]).astype(o_ref.dtype)


def _inproj(x2d, gain, w, *, layer, tm, tn):
    t, d = x2d.shape
    n = OFF_IF
    return pl.pallas_call(
        _inproj_kernel,
        grid=(t // tm, n // tn),
        in_specs=[pl.BlockSpec((tm, d), lambda i, j: (i, 0)),
                  pl.BlockSpec((1, d), lambda i, j: (0, 0)),
                  pl.BlockSpec((1, tn, d), lambda i, j: (layer, j, 0)),
                  pl.BlockSpec((1, IF_PAD, d), lambda i, j: (layer, OFF_IF // IF_PAD, 0))],
        out_specs=[pl.BlockSpec((tm, tn), lambda i, j: (i, j)),
                   pl.BlockSpec((tm, d), lambda i, j: (i, 0)),
                   pl.BlockSpec((8, tm), lambda i, j: (0, i))],
        out_shape=[jax.ShapeDtypeStruct((t, n), BF16), jax.ShapeDtypeStruct((t, d), BF16),
                   jax.ShapeDtypeStruct((8, t), F32)],
        compiler_params=_cparams("parallel", "arbitrary", vmem=VMEM_LIMIT_INPROJ),
        name="inproj",
    )(x2d, gain, w, w)


def _log_sigmoid(x):
    return jnp.minimum(x, 0.0) - jnp.log(1.0 + jnp.exp(-jnp.abs(x)))


def _split_bf16(x):
    hi = x.astype(BF16)
    return hi, (x - hi.astype(F32)).astype(BF16)


def _prefix_max(x):
    n = x.shape[1]
    lane = lax.broadcasted_iota(jnp.int32, x.shape, 1)
    shift = 1
    while shift < n:
        x = jnp.maximum(x, jnp.where(lane >= shift, pltpu.roll(x, shift, 1), NEG))
        shift *= 2
    return x


def _mlstm_rows_kernel(qk_ref, v_ref, og_ref, *rest, blk, group):
    gate_refs = rest[:group]
    cw_ref, bi_ref, bf_ref, ng_ref, y_ref, xe_scr, s_scr, m_scr = rest[group:]
    heads, dh, w = MLSTM_HEADS, MLSTM_DH, MLSTM_W

    @pl.when(pl.program_id(1) == 0)
    def _():
        xe_scr[:, 0:8, :] = jnp.zeros((group, 8, 2 * w), F32)
        s_scr[...] = jnp.zeros_like(s_scr)
        m_scr[...] = jnp.zeros_like(m_scr)

    cw = cw_ref[...]
    causal = lax.broadcasted_iota(jnp.int32, (blk, blk), 0) >= lax.broadcasted_iota(jnp.int32, (blk, blk), 1)
    triu = (lax.broadcasted_iota(jnp.int32, (blk, blk), 0)
            <= lax.broadcasted_iota(jnp.int32, (blk, blk), 1)).astype(BF16)
    ones = jnp.ones((blk, dh), BF16)
    s_in = [[s_scr[g, h] for h in range(heads)] for g in range(group)]
    m_in = [m_scr[g, :, 0:1] for g in range(group)]
    s_out = [[None] * heads for _ in range(group)]
    m_out = [None] * group
    per_seq = []
    for g in range(group):
        xe_scr[g, 8:8 + blk, :] = qk_ref[g].astype(F32)
        conv = cw[CONV_WIDTH - 1:CONV_WIDTH, :] * xe_scr[g, 8:8 + blk, :]
        for j in range(CONV_WIDTH - 1):
            off = 8 - (CONV_WIDTH - 1) + j
            conv = conv + cw[j:j + 1, :] * xe_scr[g, off:off + blk, :]
        xe_scr[g, 0:8, :] = xe_scr[g, blk:blk + 8, :]
        qk = _silu(conv)

        gates = gate_refs[g][...]
        i_r = gates + bi_ref[...]
        lf_hi, lf_lo = _split_bf16(_log_sigmoid(pltpu.roll(gates, heads, 0) + bf_ref[...]))
        b_r = _dot(lf_hi, triu) + _dot(lf_lo, triu)
        m_st = m_in[g]
        a_r = i_r - b_r
        inter = b_r + m_st
        m_t = jnp.maximum(inter, b_r + _prefix_max(a_r))
        b_last = b_r[:, blk - 1:blk]
        dec = b_last - b_r + i_r
        m_new = jnp.maximum(b_last + m_st, jnp.max(dec, axis=1, keepdims=True))
        w_c = jnp.exp(b_last + m_st - m_new)
        m_out[g] = m_new
        pack = jnp.concatenate([b_r - m_t, jnp.exp(inter - m_t), jnp.exp(-m_t), jnp.exp(dec - m_new),
                                jnp.zeros((blk - 32, blk), F32)], axis=0)
        per_seq.append((qk, a_r, pack.T, w_c))

    chains = [(g, h) for h in range(heads) for g in range(group)]
    st = {}
    for g, h in chains:
        qk = per_seq[g][0]
        sl = slice(h * dh, (h + 1) * dh)
        q_b = qk[:, sl].astype(BF16)
        k_f = qk[:, w + h * dh:w + (h + 1) * dh] * (dh ** -0.5)
        v_ext = jnp.concatenate([v_ref[g, :, sl], ones], axis=-1)
        st[g, h] = (q_b, k_f, v_ext, _dot_nt(q_b, k_f.astype(BF16)), _dot(q_b, s_in[g][h].astype(BF16)))
    for g, h in chains:
        q_b, k_f, v_ext, qk_t, q_state = st[g, h]
        _, a_r, cols, _ = per_seq[g]
        u_c, w_inter = cols[:, h:h + 1], cols[:, 8 + h:9 + h]
        w_intra = jnp.exp(jnp.where(causal, u_c + a_r[h:h + 1, :], NEG))
        st[g, h] = (k_f, v_ext, _dot((qk_t * w_intra).astype(BF16), v_ext) + w_inter * q_state)
    for g, h in chains:
        k_f, v_ext, tot = st[g, h]
        _, _, cols, w_c = per_seq[g]
        em_c, w_k = cols[:, 16 + h:17 + h], cols[:, 24 + h:25 + h]
        sl = slice(h * dh, (h + 1) * dh)
        num, den = tot[:, :dh], tot[:, dh:]
        hh = num / jnp.maximum(jnp.abs(den), em_c)
        hn = _rms(hh, ng_ref[:, sl])
        y_ref[g, :, sl] = (hn * _sigmoid(og_ref[g, :, sl].astype(F32))).astype(y_ref.dtype)
        s_out[g][h] = w_c[h:h + 1, :] * s_in[g][h] + _dot((k_f * w_k).T.astype(BF16), v_ext)
    for g in range(group):
        m_scr[g] = jnp.broadcast_to(m_out[g], m_scr.shape[1:])
        for h in range(heads):
            s_scr[g, h] = s_out[g][h]


def _mlstm_rows(proj, gates_t, conv_w, bias_i, bias_f, norm_g, *, batch, seq, blk, group):
    t, npj = proj.shape
    w = MLSTM_W
    proj3 = proj.reshape(batch, seq, npj)
    cols = lambda c: (lambda b, i: (b, i, c))
    const2 = lambda b, i: (0, 0)
    nblk = seq // blk
    gate_specs = [pl.BlockSpec((8, blk), functools.partial(lambda b, i, g: (0, (b * group + g) * nblk + i), g=g))
                  for g in range(group)]
    y = pl.pallas_call(
        functools.partial(_mlstm_rows_kernel, blk=blk, group=group),
        grid=(batch // group, seq // blk),
        in_specs=[pl.BlockSpec((group, blk, 2 * w), cols(OFF_MQ // (2 * w))),
                  pl.BlockSpec((group, blk, w), cols(OFF_MV // w)),
                  pl.BlockSpec((group, blk, w), cols(OFF_MO // w)),
                  *gate_specs,
                  pl.BlockSpec((CONV_WIDTH, 2 * w), const2),
                  pl.BlockSpec((8, 1), const2), pl.BlockSpec((8, 1), const2),
                  pl.BlockSpec((1, w), const2)],
        out_specs=pl.BlockSpec((group, blk, w), cols(0)),
        out_shape=jax.ShapeDtypeStruct((batch, seq, w), BF16),
        scratch_shapes=[pltpu.VMEM((group, blk + 8, 2 * w), F32),
                        pltpu.VMEM((group, MLSTM_HEADS, MLSTM_DH, 2 * MLSTM_DH), F32),
                        pltpu.VMEM((group, 8, 128), F32)],
        compiler_params=_cparams("parallel", "arbitrary", vmem=VMEM_LIMIT_SMALL),
        name="mlstm",
    )(proj3, proj3, proj3, *([gates_t] * group), conv_w, bias_i, bias_f, norm_g)
    return y.reshape(t, w)


def _attnproj_kernel(h_ref, w_ref, seg_ref, gq_ref, gk_ref, o_ref, r_scr, *, dil):
    gw, half = ATTN_GW, ATTN_SLAB // 2
    sub_rows = ATTNPROJ_SUB
    seg, sub_seg = ATTN_TILE // dil, sub_rows // dil

    def head_norm(x, gain):
        ss = _dot((x * x).astype(BF16), seg_ref[...])
        return x * lax.rsqrt(ss * (1.0 / ATTN_DH) + EPS) * gain

    low = lax.broadcasted_iota(jnp.int32, (1, ATTN_SLAB), 1) < half
    for s in range(ATTN_TILE // sub_rows):
        rows = slice(s * sub_rows, (s + 1) * sub_rows)
        res = _dot_nt(h_ref[rows, :], w_ref[0])
        q = head_norm(res[:, :gw], gq_ref[...]) * (ATTN_DH ** -0.5)
        k = head_norm(res[:, gw:2 * gw], gk_ref[...])
        slabs = []
        for pair in range(gw // ATTN_SLAB):
            qp = q[:, pair * ATTN_SLAB:(pair + 1) * ATTN_SLAB]
            slabs += [jnp.where(low, qp, 0.0), jnp.where(low, 0.0, qp)]
        slabs += [k[:, c * 128:(c + 1) * 128] for c in range(gw // 128)]
        slabs += [res[:, 2 * gw + c * 128:2 * gw + (c + 1) * 128] for c in range(gw // 128)]
        pitch = dil + 1 if dil % 16 == 0 else dil
        for c, slab in enumerate(slabs):
            if dil == 1:
                o_ref[rows, c * 128:(c + 1) * 128] = slab.astype(o_ref.dtype)
            elif pitch == dil:
                r_scr[s % 2, c, 0:sub_rows, :] = slab
            else:
                for i in range(sub_seg):
                    r_scr[s % 2, c, pitch * i:pitch * i + dil, :] = slab[dil * i:dil * (i + 1), :]
        if dil > 1:
            for r in range(dil):
                dst = slice(r * seg + s * sub_seg, r * seg + (s + 1) * sub_seg)
                for c in range(r_scr.shape[1]):
                    o_ref[dst, c * 128:(c + 1) * 128] = (
                        r_scr[s % 2, c, pl.ds(r, sub_seg, stride=pitch), :].astype(o_ref.dtype))


def _attnproj(h, w, seg_ones, gq, gk, *, layer, group, dilation):
    t, d = h.shape
    wcols = 3 * ATTN_GW
    const2 = lambda i: (0, 0)
    return pl.pallas_call(
        functools.partial(_attnproj_kernel, dil=dilation),
        grid=(t // ATTN_TILE,),
        in_specs=[pl.BlockSpec((ATTN_TILE, d), lambda i: (i, 0)),
                  pl.BlockSpec((1, wcols, d), lambda i: (layer, group, 0)),
                  pl.BlockSpec((ATTN_GW, ATTN_GW), const2),
                  pl.BlockSpec((1, ATTN_GW), const2), pl.BlockSpec((1, ATTN_GW), const2)],
        out_specs=pl.BlockSpec((ATTN_TILE, ATTN_COLS), lambda i: (i, 0)),
        out_shape=jax.ShapeDtypeStruct((t, ATTN_COLS), BF16),
        scratch_shapes=[pltpu.VMEM((2, ATTN_COLS // 128, ATTNPROJ_SUB + ATTNPROJ_SUB // 16, 128), F32)],
        compiler_params=_cparams("parallel"),
        name=f"attnproj{group}",
    )(h, w, seg_ones, gq, gk)


def _dattn_kernel(q_ref, kc_ref, kp_ref, vc_ref, vp_ref, b0_ref, o_ref, lse_ref,
                  kx_scr, vx_scr, o_scr, l_scr, bias_scr, *, dil):
    blk = ATTN_BLOCK
    per = ATTN_SUB // dil
    pitch = dil + 1 if dil % 16 == 0 else dil
    first_tile = pl.program_id(1) == 0

    @pl.when(first_tile)
    def _():
        for h in range(HEADS_PER_GROUP):
            bias_scr[h] = pltpu.roll(jnp.broadcast_to(b0_ref[h], (blk, 2 * blk)), 0, 1, stride=1, stride_axis=0)

    for r in range(dil):
        base = r * (per + 1) * blk
        last = slice((r * per + per - 1) * blk, (r * per + per) * blk)
        mine = slice(r * per * blk, (r + 1) * per * blk)
        kx_scr[base:base + blk, :] = kp_ref[last, :]
        vx_scr[base:base + blk, :] = vp_ref[last, :]
        kx_scr[base + blk:base + (per + 1) * blk, :] = kc_ref[mine, :]
        vx_scr[base + blk:base + (per + 1) * blk, :] = vc_ref[mine, :]

    low = lax.broadcasted_iota(jnp.int32, (1, ATTN_SLAB), 1) < ATTN_SLAB // 2
    no_prev = lax.broadcasted_iota(jnp.int32, (1, 2 * blk), 1) < blk
    for r in range(dil):
        for sub in range(per):
            u = r * per + sub
            win = slice((r * (per + 1) + sub) * blk, (r * (per + 1) + sub + 2) * blk)
            o_slabs, l_slabs = [], []
            for pair in range(ATTN_GW // ATTN_SLAB):
                cols = slice(pair * ATTN_SLAB, (pair + 1) * ATTN_SLAB)
                kx, vx = kx_scr[win, cols], vx_scr[win, cols]
                o_pair, l_pair = [], []
                for h in (2 * pair, 2 * pair + 1):
                    logits = _dot_nt(q_ref[u * blk:(u + 1) * blk, h * ATTN_SLAB:(h + 1) * ATTN_SLAB], kx)
                    logits = logits + bias_scr[h]
                    if sub == 0:
                        logits = jnp.where(first_tile & no_prev, NEG, logits)
                    m = jnp.max(logits, axis=-1, keepdims=True)
                    p = jnp.exp(logits - m)
                    l = jnp.sum(p, axis=-1, keepdims=True)
                    o_pair.append(_dot(p.astype(BF16), vx) / l)
                    l_pair.append(m + jnp.log(l))
                o_slabs.append(jnp.where(low, o_pair[0], o_pair[1]))
                l_slabs.append(jnp.where(low, l_pair[0], l_pair[1]))
            for c in range(ATTN_GW // ATTN_SLAB):
                cols = slice(c * ATTN_SLAB, (c + 1) * ATTN_SLAB)
                if dil == 1:
                    o_ref[u * blk:(u + 1) * blk, cols] = o_slabs[c].astype(o_ref.dtype)
                    lse_ref[u * blk:(u + 1) * blk, cols] = l_slabs[c]
                else:
                    dst = pl.ds(sub * blk * pitch + r, blk, stride=pitch)
                    o_scr[c, dst, :] = o_slabs[c]
                    l_scr[c, dst, :] = l_slabs[c]
    if dil > 1:
        for c in range(ATTN_GW // ATTN_SLAB):
            cols = slice(c * ATTN_SLAB, (c + 1) * ATTN_SLAB)
            if pitch == dil:
                o_ref[:, cols] = o_scr[c, 0:ATTN_TILE, :].astype(o_ref.dtype)
                lse_ref[:, cols] = l_scr[c, 0:ATTN_TILE, :]
            else:
                for i in range(ATTN_TILE // dil):
                    o_ref[dil * i:dil * (i + 1), cols] = o_scr[c, pitch * i:pitch * i + dil, :].astype(o_ref.dtype)
                    lse_ref[dil * i:dil * (i + 1), cols] = l_scr[c, pitch * i:pitch * i + dil, :]


def _dattn(aproj, bias, *, seq, group, dilation):
    t = aproj.shape[0]
    tiles = seq // ATTN_TILE
    qw = HEADS_PER_GROUP * ATTN_SLAB
    cq, ck, cv = 0, qw // ATTN_GW, qw // ATTN_GW + 1
    blk = (ATTN_TILE, ATTN_GW)
    cur = lambda c: (lambda b, j: (b * tiles + j, c))
    prev = lambda c: (lambda b, j: (b * tiles + jnp.maximum(j - 1, 0), c))
    xrows = ATTN_TILE + dilation * ATTN_BLOCK
    return pl.pallas_call(
        functools.partial(_dattn_kernel, dil=dilation),
        grid=(t // seq, tiles),
        in_specs=[pl.BlockSpec((ATTN_TILE, qw), cur(cq)),
                  pl.BlockSpec(blk, cur(ck)), pl.BlockSpec(blk, prev(ck)),
                  pl.BlockSpec(blk, cur(cv)), pl.BlockSpec(blk, prev(cv)),
                  pl.BlockSpec((HEADS_PER_GROUP, 1, 2 * ATTN_BLOCK), lambda b, j: (0, 0, 0))],
        out_specs=[pl.BlockSpec(blk, cur(0)), pl.BlockSpec(blk, cur(0))],
        out_shape=[jax.ShapeDtypeStruct((t, ATTN_GW), BF16), jax.ShapeDtypeStruct((t, ATTN_GW), F32)],
        scratch_shapes=[pltpu.VMEM((xrows, ATTN_GW), BF16), pltpu.VMEM((xrows, ATTN_GW), BF16),
                        pltpu.VMEM((ATTN_GW // ATTN_SLAB, ATTN_TILE + ATTN_TILE // 16, ATTN_SLAB), F32),
                        pltpu.VMEM((ATTN_GW // ATTN_SLAB, ATTN_TILE + ATTN_TILE // 16, ATTN_SLAB), F32),
                        pltpu.VMEM((HEADS_PER_GROUP, ATTN_BLOCK, 2 * ATTN_BLOCK), F32)],
        compiler_params=_cparams("parallel", "arbitrary"),
        name=f"dattn{group}",
    )(aproj, aproj, aproj, aproj, aproj, bias)


def _rel_bucket(n):
    max_exact = REL_BUCKETS // 2
    nf = jnp.maximum(n, 1).astype(F32)
    log_b = max_exact + (jnp.log(nf / max_exact) / math.log(REL_MAX_DIST / max_exact)
                         * (REL_BUCKETS - max_exact)).astype(jnp.int32)
    return jnp.where(n < max_exact, n, jnp.minimum(log_b, REL_BUCKETS - 1))


def _attn_bias(rel_bias, group):
    window, dilation = ATTN_PATTERNS[group]
    steps = window // dilation
    assert steps == ATTN_BLOCK
    hs = slice(group * HEADS_PER_GROUP, (group + 1) * HEADS_PER_GROUP)
    bucket = _rel_bucket((steps - jnp.arange(steps + 1)) * dilation)
    by_dist = jnp.dot(jax.nn.one_hot(bucket, REL_BUCKETS, dtype=F32), rel_bias[:, hs].astype(F32),
                      precision=lax.Precision.HIGHEST)
    row0 = jnp.concatenate([by_dist, jnp.full((2 * ATTN_BLOCK - steps - 1, HEADS_PER_GROUP), NEG, F32)], axis=0)
    return row0.T[:, None, :]


def _merge_kernel(ya_ref, yb0_ref, yb1_ref, yb2_ref, l0_ref, l1_ref, l2_ref, gu_ref, gv_ref, gate_ref,
                  x_ref, wa_ref, wb_ref, wc_ref, wo_ref, ws_ref, bs_ref, gg_ref, o_ref, yc_scr, *, tm):
    d = x_ref.shape[1]
    l0, l1, l2 = l0_ref[...], l1_ref[...], l2_ref[...]
    mx = jnp.maximum(jnp.maximum(l0, l1), l2)
    e0, e1, e2 = jnp.exp(l0 - mx), jnp.exp(l1 - mx), jnp.exp(l2 - mx)
    inv = 1.0 / (e0 + e1 + e2)
    yb = jnp.concatenate([(yb0_ref[...].astype(F32) * (e0 * inv)).astype(BF16),
                          (yb1_ref[...].astype(F32) * (e1 * inv)).astype(BF16),
                          (yb2_ref[...].astype(F32) * (e2 * inv)).astype(BF16)], axis=-1)

    for j in range(tm // GMLP_CHUNK):
        rows = slice(j * GMLP_CHUNK, (j + 1) * GMLP_CHUNK)
        for g in range(GMLP_GROUPS):
            cols = slice(g * GMLP_GC, (g + 1) * GMLP_GC)
            u = _gelu(gu_ref[rows, cols].astype(F32))
            v = _rms(_gelu(gv_ref[rows, cols].astype(F32)), gg_ref[:, cols])
            mixed = _dot(ws_ref[g], v.astype(BF16)) + bs_ref[g]
            yc_scr[rows, cols] = (u * mixed).astype(BF16)

    def gate2(k):
        return jnp.tanh(0.5 * gate_ref[:, k * d:(k + 1) * d].astype(F32)) + 1.0

    merged2 = gate2(0) * _dot(ya_ref[...], wa_ref[...])
    merged2 = merged2 + gate2(1) * _dot(yb, wb_ref[...])
    merged2 = merged2 + gate2(2) * _dot(yc_scr[...], wc_ref[...])
    o_ref[...] = x_ref[...] + 0.5 * _dot(merged2.astype(BF16), wo_ref[...])


def _merge(ya, ybs, lses, proj, x2d, wa, wb, wc, wo, ws, bsb, gg, *, tm):
    t, d = x2d.shape
    row = lambda c: (lambda i: (i, c))
    full2 = lambda i: (0, 0)
    full3 = lambda i: (0, 0, 0)
    gspec = pl.BlockSpec((tm, ATTN_GW), row(0))
    return pl.pallas_call(
        functools.partial(_merge_kernel, tm=tm),
        grid=(t // tm,),
        in_specs=[pl.BlockSpec((tm, MLSTM_W), row(0)),
                  gspec, gspec, gspec, gspec, gspec, gspec,
                  pl.BlockSpec((tm, GMLP_W), row(OFF_GU // GMLP_W)),
                  pl.BlockSpec((tm, GMLP_W), row(OFF_GV // GMLP_W)),
                  pl.BlockSpec((tm, N_BRANCH * d), row(OFF_GATE // (N_BRANCH * d))),
                  pl.BlockSpec((tm, d), row(0)),
                  pl.BlockSpec(wa.shape, full2), pl.BlockSpec(wb.shape, full2),
                  pl.BlockSpec(wc.shape, full2), pl.BlockSpec(wo.shape, full2),
                  pl.BlockSpec(ws.shape, full3), pl.BlockSpec(bsb.shape, full3),
                  pl.BlockSpec(gg.shape, full2)],
        out_specs=pl.BlockSpec((tm, d), row(0)),
        out_shape=jax.ShapeDtypeStruct((t, d), F32),
        scratch_shapes=[pltpu.VMEM((tm, GMLP_W), BF16)],
        compiler_params=_cparams("parallel"),
        name="merge",
    )(ya, *ybs, *lses, proj, proj, proj, x2d, wa, wb, wc, wo, ws, bsb, gg)


def _memkv_kernel(mem_ref, g_ref, w_ref, gk_ref, k_ref, v_ref):
    dh, w = XATTN_DH, XATTN_W
    kv = _dot(_rms(mem_ref[0], g_ref[...]).astype(BF16), w_ref[...])
    for h in range(XATTN_HEADS):
        sl = slice(h * dh, (h + 1) * dh)
        k_ref[0, :, sl] = _rms(kv[:, sl], gk_ref[...]).astype(k_ref.dtype)
    v_ref[0] = kv[:, w:].astype(v_ref.dtype)


def _memkv(mem, gain, w_kv, gk):
    b, m, d = mem.shape
    full2 = lambda i: (0, 0)
    return pl.pallas_call(
        _memkv_kernel,
        grid=(b,),
        in_specs=[pl.BlockSpec((1, m, d), lambda i: (i, 0, 0)),
                  pl.BlockSpec((1, d), full2),
                  pl.BlockSpec(w_kv.shape, full2),
                  pl.BlockSpec((1, XATTN_DH), full2)],
        out_specs=[pl.BlockSpec((1, m, XATTN_W), lambda i: (i, 0, 0)),
                   pl.BlockSpec((1, m, XATTN_W), lambda i: (i, 0, 0))],
        out_shape=[jax.ShapeDtypeStruct((b, m, XATTN_W), BF16),
                   jax.ShapeDtypeStruct((b, m, XATTN_W), BF16)],
        compiler_params=_cparams("parallel", vmem=VMEM_LIMIT_SMALL),
        name="memkv",
    )(mem, gain, w_kv, gk)


def _route(logits):
    tm = logits.shape[1]
    e = jnp.exp(logits - jnp.max(logits, axis=0, keepdims=True))
    probs = e / jnp.sum(e, axis=0, keepdims=True)
    rowi = lax.broadcasted_iota(jnp.int32, (8, tm), 0)
    real = rowi < EXPERTS_PER_GROUP
    tops = []
    for g in range(N_EXPERT_GROUPS):
        pg = jnp.where(real, probs[8 * g:8 * g + 8, :], -0.5)
        m1 = jnp.max(pg, axis=0, keepdims=True)
        i1 = jnp.min(jnp.where(pg == m1, rowi, 8), axis=0, keepdims=True)
        pg2 = jnp.where(rowi == i1, -1.0, pg)
        m2 = jnp.max(pg2, axis=0, keepdims=True)
        i2 = jnp.min(jnp.where(pg2 == m2, rowi, 8), axis=0, keepdims=True)
        tops.append((m1, i1, m2, i2))
    best = jnp.zeros((1, tm), jnp.int32)
    best_score = tops[0][0] + tops[0][2]
    for g in range(1, N_EXPERT_GROUPS):
        score = tops[g][0] + tops[g][2]
        better = score > best_score
        best = jnp.where(better, g, best)
        best_score = jnp.where(better, score, best_score)
    m1, i1, m2, i2 = tops[0]
    for g in range(1, N_EXPERT_GROUPS):
        m1, i1, m2, i2 = (jnp.where(best == g, new, old) for new, old in zip(tops[g], (m1, i1, m2, i2)))
    tot = m1 + m2
    base = best * EXPERTS_PER_GROUP
    return base + i1, base + i2, m1 / tot, m2 / tot


def _pack_bf16_pairs(x):
    n = x.shape[1] // 2
    hi = lax.bitcast_convert_type(x[:, :n].astype(BF16).astype(F32), jnp.uint32)
    lo = lax.bitcast_convert_type(x[:, n:].astype(BF16).astype(F32), jnp.uint32)
    return hi | (lo >> 16)


def _unpack_bf16_pairs(p):
    hi = lax.bitcast_convert_type(p & jnp.uint32(0xFFFF0000), F32)
    lo = lax.bitcast_convert_type(p << 16, F32)
    return hi, lo


def _store_row_chunks(ref, packed):
    for j in range(ROW_CHUNKS):
        ref[j] = packed[:, j * 128:(j + 1) * 128]


def _load_row_chunks(ref):
    return jnp.concatenate([ref[j] for j in range(ROW_CHUNKS)], axis=-1)


def _xattn_kernel(x_ref, k_ref, v_ref, gx_ref, wq_ref, gq_ref, wo_ref, gf_ref, rw_ref, rb_ref,
                  xo_ref, hf_ref, eidx_ref, wts_ref, *, sub):
    dh = XATTN_DH
    rw = rw_ref[...]
    rw_hi, rw_lo = _split_bf16(rw)
    for s in range(x_ref.shape[0] // sub):
        rows = slice(s * sub, (s + 1) * sub)
        x = x_ref[rows, :]
        q = _dot(_rms(x, gx_ref[...]).astype(BF16), wq_ref[...])
        outs = []
        for h in range(XATTN_HEADS):
            sl = slice(h * dh, (h + 1) * dh)
            q_h = (_rms(q[:, sl], gq_ref[...]) * (dh ** -0.5)).astype(BF16)
            logits = _dot_nt(q_h, k_ref[0, :, sl])
            p = jnp.exp(logits - jnp.max(logits, axis=-1, keepdims=True))
            o = _dot(p.astype(BF16), v_ref[0, :, sl]) / jnp.sum(p, axis=-1, keepdims=True)
            outs.append(o.astype(BF16))
        xn = x + _dot(jnp.concatenate(outs, axis=-1), wo_ref[...])
        xo_ref[rows, :] = xn
        hf = _rms(xn, gf_ref[...])
        packed = _pack_bf16_pairs(hf)
        for j in range(ROW_CHUNKS):
            hf_ref[j, rows, :] = packed[:, j * 128:(j + 1) * 128]
        hf_hi, hf_lo = _split_bf16(hf)
        logits_t = _dot_nt(rw_hi, hf_hi) + _dot_nt(rw_hi, hf_lo) + _dot_nt(rw_lo, hf_hi) + rb_ref[...]
        e1, e2, w1, w2 = _route(logits_t)
        eidx_ref[:, rows] = jnp.concatenate([e1, e2, jnp.zeros((6, sub), jnp.int32)], axis=0)
        wts_ref[:, rows] = jnp.concatenate([w1, w2, jnp.zeros((6, sub), F32)], axis=0)


def _xattn(x2d, k, v, gx, wq, gq, wo, gf, rw_t, rb, *, seq, tm):
    t, d = x2d.shape
    per_b = seq // tm
    full2 = lambda i: (0, 0)
    kv_spec = pl.BlockSpec((1,) + k.shape[1:], lambda i: (i // per_b, 0, 0))
    return pl.pallas_call(
        functools.partial(_xattn_kernel, sub=min(tm, XATTN_SUB)),
        grid=(t // tm,),
        in_specs=[pl.BlockSpec((tm, d), lambda i: (i, 0)), kv_spec, kv_spec,
                  pl.BlockSpec((1, d), full2), pl.BlockSpec(wq.shape, full2),
                  pl.BlockSpec((1, XATTN_DH), full2), pl.BlockSpec(wo.shape, full2),
                  pl.BlockSpec((1, d), full2), pl.BlockSpec(rw_t.shape, full2),
                  pl.BlockSpec(rb.shape, full2)],
        out_specs=[pl.BlockSpec((tm, d), lambda i: (i, 0)),
                   pl.BlockSpec((ROW_CHUNKS, tm, 128), lambda i: (0, i, 0)),
                   pl.BlockSpec((8, tm), lambda i: (0, i)),
                   pl.BlockSpec((8, tm), lambda i: (0, i))],
        out_shape=[jax.ShapeDtypeStruct((t, d), F32),
                   jax.ShapeDtypeStruct((ROW_CHUNKS, t, 128), jnp.uint32),
                   jax.ShapeDtypeStruct((8, t), jnp.int32),
                   jax.ShapeDtypeStruct((8, t), F32)],
        compiler_params=_cparams("parallel"),
        name="xattn_router",
    )(x2d, k, v, gx, wq, gq, wo, gf, rw_t, rb)


def _moe_plan_kernel(eidx_ref, i1_ref, i2_ref, te_ref, na_ref, cnt_scr, carry_scr, *, tb, tm, plane_rows):
    ne = N_EXPERTS
    hp = lax.Precision.HIGHEST
    phase, j = pl.program_id(0), pl.program_id(1)
    rows = lax.broadcasted_iota(jnp.int32, (ne, tb), 0)
    oh1 = rows == eidx_ref[0:1, :]
    oh2 = rows == eidx_ref[1:2, :]
    a = oh1.astype(F32) + oh2.astype(F32)
    blk_cnt = jnp.broadcast_to(jnp.sum(a, axis=1, keepdims=True), cnt_scr.shape)

    @pl.when((phase == 0) & (j == 0))
    def _():
        cnt_scr[...] = jnp.zeros_like(cnt_scr)

    @pl.when(phase == 0)
    def _():
        cnt_scr[...] += blk_cnt

    @pl.when((phase == 1) & (j == 0))
    def _():
        padded = jnp.ceil(cnt_scr[...] * (1.0 / tm)) * tm
        er = lax.broadcasted_iota(jnp.int32, (ne, ne), 0)
        ec = lax.broadcasted_iota(jnp.int32, (ne, ne), 1)
        off = jnp.dot((ec < er).astype(F32), padded, precision=hp, preferred_element_type=F32)
        carry_scr[...] = off
        seg_end = (off + padded)[:, 0:1]
        tile_start = lax.broadcasted_iota(jnp.int32, (ne, te_ref.shape[1]), 1).astype(F32) * tm
        te = jnp.sum((seg_end <= tile_start).astype(F32), axis=0, keepdims=True)
        te_ref[...] = jnp.broadcast_to(jnp.minimum(te, ne - 1.0), te_ref.shape).astype(jnp.int32)
        total = jnp.sum(padded[:, 0:1], axis=0, keepdims=True)
        na_ref[...] = jnp.broadcast_to(total * (1.0 / tm), na_ref.shape).astype(jnp.int32)

    @pl.when(phase == 1)
    def _():
        before = (lax.broadcasted_iota(jnp.int32, (tb, tb), 0)
                  < lax.broadcasted_iota(jnp.int32, (tb, tb), 1)).astype(BF16)
        rank = carry_scr[:, 0:1] + _dot(a.astype(BF16), before)
        d1 = jnp.sum(jnp.where(oh1, rank, 0.0), axis=0, keepdims=True).astype(jnp.int32)
        d2 = jnp.sum(jnp.where(oh2, rank, 0.0), axis=0, keepdims=True).astype(jnp.int32)
        plane = lax.broadcasted_iota(jnp.int32, (8, tb), 0) * plane_rows
        i1_ref[...] = jnp.where(plane < ROW_CHUNKS * plane_rows, plane + d1, 0)
        i2_ref[...] = jnp.where(plane < ROW_CHUNKS * plane_rows, plane + d2, 0)
        carry_scr[...] += blk_cnt


def _moe_plan(eidx, *, tm, n_tiles, tb=PLAN_TB):
    t = eidx.shape[1]
    ntp = -(-n_tiles // 128) * 128
    return pl.pallas_call(
        functools.partial(_moe_plan_kernel, tb=tb, tm=tm, plane_rows=n_tiles * tm),
        grid=(2, t // tb),
        in_specs=[pl.BlockSpec((8, tb), lambda p, j: (0, j))],
        out_specs=[pl.BlockSpec((8, tb), lambda p, j: (0, j * p)),
                   pl.BlockSpec((8, tb), lambda p, j: (0, j * p)),
                   pl.BlockSpec((8, ntp), lambda p, j: (0, 0)),
                   pl.BlockSpec((8, 128), lambda p, j: (0, 0))],
        out_shape=[jax.ShapeDtypeStruct((8, t), jnp.int32),
                   jax.ShapeDtypeStruct((8, t), jnp.int32),
                   jax.ShapeDtypeStruct((8, ntp), jnp.int32),
                   jax.ShapeDtypeStruct((8, 128), jnp.int32)],
        scratch_shapes=[pltpu.VMEM((N_EXPERTS, 128), F32), pltpu.VMEM((N_EXPERTS, 128), F32)],
        compiler_params=_cparams("arbitrary", "arbitrary", vmem=VMEM_LIMIT_SMALL),
        name="moe_plan",
    )(eidx)


def _sc_mesh():
    return plsc.VectorSubcoreMesh(core_axis_name="c", subcore_axis_name="s",
                                  num_cores=SC_CORES, num_subcores=SC_SUBCORES)


def _sc_index_spec(tokens):
    nb = tokens // SC_WINDOW
    return pl.BlockSpec((1, SC_WINDOW), lambda i: (i // nb, i % nb))


def _sc_dispatch(rows, i1, i2, n_out):
    n = rows.shape[0]
    tokens = i1.shape[1]

    @functools.partial(pl.kernel, out_type=jax.ShapeDtypeStruct((n_out, 128), rows.dtype), mesh=_sc_mesh(),
                       name="moe_dispatch")
    def k(x_hbm, i1_hbm, i2_hbm, o_hbm):
        def body(x_vmem, i1_vmem, i2_vmem):
            pltpu.sync_copy(x_vmem, o_hbm.at[i1_vmem.at[0]])
            pltpu.sync_copy(x_vmem, o_hbm.at[i2_vmem.at[0]])

        pltpu.emit_pipeline(
            body, grid=(n // SC_WINDOW,),
            in_specs=[pl.BlockSpec((SC_WINDOW, 128), lambda i: (i, 0)),
                      _sc_index_spec(tokens), _sc_index_spec(tokens)],
            out_specs=[],
            core_axis_name=("c", "s"), dimension_semantics=(pltpu.PARALLEL,),
        )(x_hbm, i1_hbm, i2_hbm)

    return k(rows, i1, i2)


def _sc_collect(table, i1, i2):
    tokens = i1.shape[1]
    n = ROW_CHUNKS * tokens
    out = jax.ShapeDtypeStruct((n, 128), table.dtype)

    @functools.partial(pl.kernel, out_type=(out, out), mesh=_sc_mesh(), name="moe_collect",
                       scratch_types=[pltpu.SemaphoreType.DMA, pltpu.SemaphoreType.DMA])
    def k(t_hbm, i1_hbm, i2_hbm, o1_hbm, o2_hbm, sem1, sem2):
        def body(i1_vmem, i2_vmem, o1_vmem, o2_vmem):
            first = pltpu.async_copy(t_hbm.at[i1_vmem.at[0]], o1_vmem, sem1)
            second = pltpu.async_copy(t_hbm.at[i2_vmem.at[0]], o2_vmem, sem2)
            first.wait()
            second.wait()

        pltpu.emit_pipeline(
            body, grid=(n // SC_WINDOW,),
            in_specs=[_sc_index_spec(tokens), _sc_index_spec(tokens)],
            out_specs=[pl.BlockSpec((SC_WINDOW, 128), lambda i: (i, 0)),
                       pl.BlockSpec((SC_WINDOW, 128), lambda i: (i, 0))],
            core_axis_name=("c", "s"), dimension_semantics=(pltpu.PARALLEL,),
        )(i1_hbm, i2_hbm, o1_hbm, o2_hbm)

    return k(table, i1, i2)


def _experts_kernel(te_ref, na_ref, xs_ref, wg_ref, wu_ref, wd_ref, y_ref, wg_scr, wu_scr, wd_scr):
    i = pl.program_id(0)
    active = i < na_ref[0]

    @pl.when(active & ((i == 0) | (te_ref[i] != te_ref[jnp.maximum(i - 1, 0)])))
    def _():
        wg_scr[...] = wg_ref[0, 0].astype(BF16)
        wu_scr[...] = wu_ref[0, 0].astype(BF16)
        wd_scr[...] = wd_ref[0, 0].astype(BF16)

    @pl.when(active)
    def _():
        hi, lo = _unpack_bf16_pairs(_load_row_chunks(xs_ref))
        h = jnp.concatenate([hi, lo], axis=-1).astype(BF16)
        up = _dot(h, wg_scr[...])
        act = _silu(up) * _dot(h, wu_scr[...])
        _store_row_chunks(y_ref, _pack_bf16_pairs(_dot(act.astype(BF16), wd_scr[...])))


def _experts(tile_expert, n_active, xs, wg, wu, wd, *, layer, tm):
    n_tiles = tile_expert.shape[0]
    _, _, d, dff = wg.shape
    rows = lambda i, te, na: (0, jnp.minimum(i, na[0] - 1), 0)
    expert = lambda i, te, na: (layer, te[i], 0, 0)
    return pl.pallas_call(
        _experts_kernel,
        grid_spec=pltpu.PrefetchScalarGridSpec(
            num_scalar_prefetch=2,
            grid=(n_tiles,),
            in_specs=[pl.BlockSpec((ROW_CHUNKS, tm, 128), rows),
                      pl.BlockSpec((1, 1, d, dff), expert),
                      pl.BlockSpec((1, 1, d, dff), expert),
                      pl.BlockSpec((1, 1, dff, d), expert)],
            out_specs=pl.BlockSpec((ROW_CHUNKS, tm, 128), rows),
            scratch_shapes=[pltpu.VMEM((d, dff), BF16), pltpu.VMEM((d, dff), BF16), pltpu.VMEM((dff, d), BF16)]),
        out_shape=jax.ShapeDtypeStruct(xs.shape, xs.dtype),
        compiler_params=_cparams("arbitrary"),
        name="moe_experts",
    )(tile_expert, n_active, xs, wg, wu, wd)


def _moe_combine_kernel(x_ref, y1_ref, y2_ref, w_ref, o_ref):
    half = x_ref.shape[1] // 2
    hi1, lo1 = _unpack_bf16_pairs(_load_row_chunks(y1_ref))
    hi2, lo2 = _unpack_bf16_pairs(_load_row_chunks(y2_ref))
    tm = x_ref.shape[0]
    w_cols = jnp.concatenate([w_ref[...], jnp.zeros((128 - w_ref.shape[0], tm), F32)], axis=0).T
    w1, w2 = w_cols[:, 0:1], w_cols[:, 1:2]
    o_ref[:, :half] = x_ref[:, :half] + w1 * hi1 + w2 * hi2
    o_ref[:, half:] = x_ref[:, half:] + w1 * lo1 + w2 * lo2


def _moe_combine(x2d, y1, y2, wts, *, tm):
    t, d = x2d.shape
    chunk_spec = pl.BlockSpec((ROW_CHUNKS, tm, 128), lambda i: (0, i, 0))
    return pl.pallas_call(
        _moe_combine_kernel,
        grid=(t // tm,),
        in_specs=[pl.BlockSpec((tm, d), lambda i: (i, 0)), chunk_spec, chunk_spec,
                  pl.BlockSpec((wts.shape[0], tm), lambda i: (0, i))],
        out_specs=pl.BlockSpec((tm, d), lambda i: (i, 0)),
        out_shape=jax.ShapeDtypeStruct((t, d), F32),
        compiler_params=_cparams("parallel"),
        name="moe_combine",
    )(x2d, y1, y2, wts)


def _moe(x2d, hf_rows, eidx, wts, wg, wu, wd, *, layer):
    t = x2d.shape[0]
    tm = MOE_TM
    n_tiles = 2 * t // tm + N_EXPERTS
    plane = n_tiles * tm
    i1, i2, te, na = _moe_plan(eidx, tm=tm, n_tiles=n_tiles)
    xs = _sc_dispatch(hf_rows.reshape(ROW_CHUNKS * t, 128), i1, i2, ROW_CHUNKS * plane)
    ys = _experts(te[0, :n_tiles], na[0, :1], xs.reshape(ROW_CHUNKS, plane, 128), wg, wu, wd,
                  layer=layer, tm=tm)
    y1, y2 = _sc_collect(ys.reshape(ROW_CHUNKS * plane, 128), i1, i2)
    return _moe_combine(x2d, y1.reshape(ROW_CHUNKS, t, 128), y2.reshape(ROW_CHUNKS, t, 128), wts,
                        tm=COMBINE_TM)


W_ROWS = 256


def _w_rows_kernel(start_ref, valid_ref, w_ref, o_ref):
    del start_ref
    row = lax.broadcasted_iota(jnp.int32, w_ref.shape, 1)
    o_ref[...] = jnp.where(row < valid_ref[pl.program_id(0)], w_ref[...], 0.0).astype(o_ref.dtype)


def _w_rows(w_t, starts, valid):
    depth, _, d = w_t.shape
    nblk = len(starts)
    return pl.pallas_call(
        _w_rows_kernel,
        grid_spec=pltpu.PrefetchScalarGridSpec(
            num_scalar_prefetch=2,
            grid=(nblk,),
            in_specs=[pl.BlockSpec((pl.Element(depth), pl.Element(W_ROWS), pl.Element(d)),
                                   lambda c, st, va: (0, pl.multiple_of(st[c], 8), 0))],
            out_specs=pl.BlockSpec((depth, W_ROWS, d), lambda c, st, va: (0, c, 0))),
        out_shape=jax.ShapeDtypeStruct((depth, nblk * W_ROWS, d), BF16),
        compiler_params=_cparams("arbitrary", vmem=VMEM_LIMIT_SMALL),
        name="w_in_rows",
    )(jnp.asarray(starts, jnp.int32), jnp.asarray(valid, jnp.int32), w_t)


def _w_in_layout(w_in):
    w_t = jnp.swapaxes(w_in, 1, 2)
    src_if = 4 * MLSTM_W
    src_a = src_if + 2 * MLSTM_HEADS
    src_g = src_a + 3 * ATTN_W
    starts = list(range(0, src_if, W_ROWS)) + [src_g + k * W_ROWS for k in range((OFF_IF - OFF_GU) // W_ROWS)]
    valid = [W_ROWS] * len(starts)
    starts.append(src_if)
    valid.append(2 * MLSTM_HEADS)
    assert len(starts) * W_ROWS == N_PROJ and ATTN_GW == W_ROWS
    a_starts = [src_a + j * ATTN_W + g * ATTN_GW for g in range(len(ATTN_PATTERNS)) for j in range(3)]
    return _w_rows(w_t, starts, valid), _w_rows(w_t, a_starts, [W_ROWS] * len(a_starts))


def kernel(x, mem, norm_mix, w_in, mlstm_conv, mlstm_gate_b, mlstm_norm, attn_qk_norm, gmlp_norm, gmlp_ws,
           gmlp_bs, w_branch_a, w_branch_b, w_branch_c, w_out, rel_bias, norm_xattn, norm_mem, w_xq, w_xkv,
           xattn_qk_norm, w_xo, norm_ffn, router_w, router_b, w_expert_gate, w_expert_up, w_expert_down):
    b, s, d = x.shape
    t = b * s
    depth = w_in.shape[0]
    x2d = x.reshape(t, d)

    biases = [_attn_bias(rel_bias, g) for g in range(len(ATTN_PATTERNS))]
    rw_t = jnp.zeros((N_EXPERT_GROUPS, 8, d), F32).at[:, :EXPERTS_PER_GROUP].set(
        router_w.T.reshape(N_EXPERT_GROUPS, EXPERTS_PER_GROUP, d)).reshape(ROUTER_ROWS, d)
    rb = jnp.full((N_EXPERT_GROUPS, 8), NEG, F32).at[:, :EXPERTS_PER_GROUP].set(
        router_b.astype(F32).reshape(N_EXPERT_GROUPS, EXPERTS_PER_GROUP)).reshape(ROUTER_ROWS, 1)
    tril = jnp.tril(jnp.ones((GMLP_CHUNK, GMLP_CHUNK), bool))
    head_of = jnp.arange(ATTN_GW) // ATTN_DH
    seg_ones = (head_of[:, None] == head_of[None, :]).astype(BF16)

    w_main, w_attn = _w_in_layout(w_in)

    for l in range(depth):
        proj, h_mix, gates_t = _inproj(x2d, norm_mix[l][None], w_main, layer=l, tm=INPROJ_TM,
                                       tn=INPROJ_TN)
        gq = jnp.tile(attn_qk_norm[l, 0], HEADS_PER_GROUP)[None]
        gk = jnp.tile(attn_qk_norm[l, 1], HEADS_PER_GROUP)[None]

        nh = MLSTM_HEADS
        bias_i = jnp.zeros((8, 1), F32).at[:nh, 0].set(mlstm_gate_b[l, :nh])
        bias_f = jnp.zeros((8, 1), F32).at[:nh, 0].set(mlstm_gate_b[l, nh:])
        ya = _mlstm_rows(proj, gates_t, mlstm_conv[l], bias_i, bias_f, mlstm_norm[l][None],
                         batch=b, seq=s, blk=MLSTM_BLOCK, group=MLSTM_GROUP)

        ybs, lses = [], []
        for g, (_, dilation) in enumerate(ATTN_PATTERNS):
            aproj = _attnproj(h_mix, w_attn, seg_ones, gq, gk, layer=l, group=g, dilation=dilation)
            o, lse = _dattn(aproj, biases[g], seq=s, group=g, dilation=dilation)
            ybs.append(o)
            lses.append(lse)

        ws = jnp.where(tril, gmlp_ws[l], 0.0).astype(BF16)
        bsb = jnp.broadcast_to(gmlp_bs[l][:, :, None], (GMLP_GROUPS, GMLP_CHUNK, GMLP_GC)).astype(F32)
        x2d = _merge(ya, ybs, lses, proj, x2d, w_branch_a[l].astype(BF16), w_branch_b[l].astype(BF16),
                     w_branch_c[l].astype(BF16), w_out[l].astype(BF16), ws, bsb, gmlp_norm[l][None],
                     tm=MERGE_TM)

        k_mem, v_mem = _memkv(mem, norm_mem[l][None], w_xkv[l].astype(BF16), xattn_qk_norm[l, 1][None])
        x2d, hf_rows, eidx, wts = _xattn(x2d, k_mem, v_mem, norm_xattn[l][None], w_xq[l].astype(BF16),
                                         xattn_qk_norm[l, 0][None], w_xo[l].astype(BF16), norm_ffn[l][None],
                                         rw_t, rb, seq=s, tm=XATTN_TM)

        x2d = _moe(x2d, hf_rows, eidx, wts, w_expert_gate, w_expert_up, w_expert_down, layer=l)

    return x2d.reshape(b, s, d)
```

```python
import functools
import math

import jax
import jax.numpy as jnp
import numpy as np
from jax import lax
from jax.experimental import pallas as pl
from jax.experimental.pallas import tpu as pltpu
from jax.experimental.pallas import tpu_sc as plsc

F32 = jnp.float32
BF16 = jnp.bfloat16

EPS = 1e-6
NEG = -1e30

MLSTM_HEADS = 4
MLSTM_DH = 128
MLSTM_W = MLSTM_HEADS * MLSTM_DH
CONV_WIDTH = 4
MLSTM_BLOCK = 128
MLSTM_GROUP = 4

ATTN_PATTERNS = ((128, 1), (512, 4), (2048, 16))
HEADS_PER_GROUP = 4
ATTN_DH = 64
ATTN_GW = HEADS_PER_GROUP * ATTN_DH
ATTN_W = len(ATTN_PATTERNS) * ATTN_GW
ATTN_BLOCK = 128
REL_BUCKETS = 32
REL_MAX_DIST = 2048

GMLP_GROUPS = 4
GMLP_GC = 128
GMLP_W = GMLP_GROUPS * GMLP_GC
GMLP_CHUNK = 128

XATTN_HEADS = 4
XATTN_DH = 128
XATTN_W = XATTN_HEADS * XATTN_DH
XATTN_SUB = 1024

N_EXPERTS = 16
N_EXPERT_GROUPS = 4
EXPERTS_PER_GROUP = 4
ROUTER_ROWS = 8 * N_EXPERT_GROUPS

N_BRANCH = 3

MOE_TM = 1024
ROW_CHUNKS = 4
SC_CORES, SC_SUBCORES = 2, 16
SC_WINDOW = 128

OFF_MQ, OFF_MK, OFF_MV, OFF_MO = 0, 512, 1024, 1536
OFF_GU, OFF_GV = 2048, 2560
OFF_GATE = 3072
OFF_IF = 6144
IF_PAD = 256
N_PROJ = OFF_IF + IF_PAD

ATTN_TILE = 2048
ATTN_SUB = ATTN_TILE // ATTN_BLOCK
ATTN_SLAB = 2 * ATTN_DH
ATTN_COLS = HEADS_PER_GROUP * ATTN_SLAB + 2 * ATTN_GW

VMEM_LIMIT = 48 * 1024 * 1024
VMEM_LIMIT_INPROJ = 56 * 1024 * 1024
VMEM_LIMIT_SMALL = 24 * 1024 * 1024

INPROJ_TM, INPROJ_TN = 1024, 3072
ATTNPROJ_SUB = 512
MERGE_TM = 512
XATTN_TM = 1024
COMBINE_TM = 1024
PLAN_TB = 1024


def _cparams(*sem, vmem=VMEM_LIMIT):
    return pltpu.CompilerParams(dimension_semantics=sem, vmem_limit_bytes=vmem)


def _rms(x, gain):
    return x * lax.rsqrt(jnp.mean(x * x, axis=-1, keepdims=True) + EPS) * gain


def _sigmoid(x):
    return 0.5 * jnp.tanh(0.5 * x) + 0.5


def _silu(x):
    half = 0.5 * x
    return half + half * jnp.tanh(half)


def _gelu(x):
    c = math.sqrt(2.0 / math.pi)
    half = 0.5 * x
    return half + half * jnp.tanh(x * (c + (c * 0.044715) * (x * x)))


def _dot(a, b):
    return jnp.dot(a, b, preferred_element_type=F32)


def _dot_nt(a, b):
    return lax.dot_general(a, b, (((1,), (1,)), ((), ())), preferred_element_type=F32)


def _inproj_kernel(x_ref, g_ref, w_ref, wg_ref, o_ref, h_ref, gt_ref):
    @pl.when(pl.program_id(1) == 0)
    def _():
        h = _rms(x_ref[...], g_ref[...]).astype(BF16)
        h_ref[...] = h
        gt_ref[...] = _dot_nt(wg_ref[0, 0:128, :], h)[:gt_ref.shape[0], :]

    o_ref[...] = _dot_nt(h_ref[...], w_ref[0]).astype(o_ref.dtype)


def _inproj(x2d, gain, w, *, layer, tm, tn):
    t, d = x2d.shape
    n = OFF_IF
    return pl.pallas_call(
        _inproj_kernel,
        grid=(t // tm, n // tn),
        in_specs=[pl.BlockSpec((tm, d), lambda i, j: (i, 0)),
                  pl.BlockSpec((1, d), lambda i, j: (0, 0)),
                  pl.BlockSpec((1, tn, d), lambda i, j: (layer, j, 0)),
                  pl.BlockSpec((1, IF_PAD, d), lambda i, j: (layer, OFF_IF // IF_PAD, 0))],
        out_specs=[pl.BlockSpec((tm, tn), lambda i, j: (i, j)),
                   pl.BlockSpec((tm, d), lambda i, j: (i, 0)),
                   pl.BlockSpec((8, tm), lambda i, j: (0, i))],
        out_shape=[jax.ShapeDtypeStruct((t, n), BF16), jax.ShapeDtypeStruct((t, d), BF16),
                   jax.ShapeDtypeStruct((8, t), F32)],
        compiler_params=_cparams("parallel", "arbitrary", vmem=VMEM_LIMIT_INPROJ),
        name="inproj",
    )(x2d, gain, w, w)


def _log_sigmoid(x):
    return jnp.minimum(x, 0.0) - jnp.log(1.0 + jnp.exp(-jnp.abs(x)))


def _split_bf16(x):
    hi = x.astype(BF16)
    return hi, (x - hi.astype(F32)).astype(BF16)


def _prefix_max(x):
    n = x.shape[1]
    lane = lax.broadcasted_iota(jnp.int32, x.shape, 1)
    shift = 1
    while shift < n:
        x = jnp.maximum(x, jnp.where(lane >= shift, pltpu.roll(x, shift, 1), NEG))
        shift *= 2
    return x


def _mlstm_rows_kernel(qk_ref, v_ref, og_ref, *rest, blk, group):
    gate_refs = rest[:group]
    cw_ref, bi_ref, bf_ref, ng_ref, y_ref, xe_scr, s_scr, m_scr = rest[group:]
    heads, dh, w = MLSTM_HEADS, MLSTM_DH, MLSTM_W

    @pl.when(pl.program_id(1) == 0)
    def _():
        xe_scr[:, 0:8, :] = jnp.zeros((group, 8, 2 * w), F32)
        s_scr[...] = jnp.zeros_like(s_scr)
        m_scr[...] = jnp.zeros_like(m_scr)

    cw = cw_ref[...]
    causal = lax.broadcasted_iota(jnp.int32, (blk, blk), 0) >= lax.broadcasted_iota(jnp.int32, (blk, blk), 1)
    triu = (lax.broadcasted_iota(jnp.int32, (blk, blk), 0)
            <= lax.broadcasted_iota(jnp.int32, (blk, blk), 1)).astype(BF16)
    ones = jnp.ones((blk, dh), BF16)
    s_in = [[s_scr[g, h] for h in range(heads)] for g in range(group)]
    m_in = [m_scr[g, :, 0:1] for g in range(group)]
    s_out = [[None] * heads for _ in range(group)]
    m_out = [None] * group
    per_seq = []
    for g in range(group):
        xe_scr[g, 8:8 + blk, :] = qk_ref[g].astype(F32)
        conv = cw[CONV_WIDTH - 1:CONV_WIDTH, :] * xe_scr[g, 8:8 + blk, :]
        for j in range(CONV_WIDTH - 1):
            off = 8 - (CONV_WIDTH - 1) + j
            conv = conv + cw[j:j + 1, :] * xe_scr[g, off:off + blk, :]
        xe_scr[g, 0:8, :] = xe_scr[g, blk:blk + 8, :]
        qk = _silu(conv)

        gates = gate_refs[g][...]
        i_r = gates + bi_ref[...]
        lf_hi, lf_lo = _split_bf16(_log_sigmoid(pltpu.roll(gates, heads, 0) + bf_ref[...]))
        b_r = _dot(lf_hi, triu) + _dot(lf_lo, triu)
        m_st = m_in[g]
        a_r = i_r - b_r
        inter = b_r + m_st
        m_t = jnp.maximum(inter, b_r + _prefix_max(a_r))
        b_last = b_r[:, blk - 1:blk]
        dec = b_last - b_r + i_r
        m_new = jnp.maximum(b_last + m_st, jnp.max(dec, axis=1, keepdims=True))
        w_c = jnp.exp(b_last + m_st - m_new)
        m_out[g] = m_new
        pack = jnp.concatenate([b_r - m_t, jnp.exp(inter - m_t), jnp.exp(-m_t), jnp.exp(dec - m_new),
                                jnp.zeros((blk - 32, blk), F32)], axis=0)
        per_seq.append((qk, a_r, pack.T, w_c))

    chains = [(g, h) for h in range(heads) for g in range(group)]
    st = {}
    for g, h in chains:
        qk = per_seq[g][0]
        sl = slice(h * dh, (h + 1) * dh)
        q_b = qk[:, sl].astype(BF16)
        k_f = qk[:, w + h * dh:w + (h + 1) * dh] * (dh ** -0.5)
        v_ext = jnp.concatenate([v_ref[g, :, sl], ones], axis=-1)
        st[g, h] = (q_b, k_f, v_ext, _dot_nt(q_b, k_f.astype(BF16)), _dot(q_b, s_in[g][h].astype(BF16)))
    for g, h in chains:
        q_b, k_f, v_ext, qk_t, q_state = st[g, h]
        _, a_r, cols, _ = per_seq[g]
        u_c, w_inter = cols[:, h:h + 1], cols[:, 8 + h:9 + h]
        w_intra = jnp.exp(jnp.where(causal, u_c + a_r[h:h + 1, :], NEG))
        st[g, h] = (k_f, v_ext, _dot((qk_t * w_intra).astype(BF16), v_ext) + w_inter * q_state)
    for g, h in chains:
        k_f, v_ext, tot = st[g, h]
        _, _, cols, w_c = per_seq[g]
        em_c, w_k = cols[:, 16 + h:17 + h], cols[:, 24 + h:25 + h]
        sl = slice(h * dh, (h + 1) * dh)
        num, den = tot[:, :dh], tot[:, dh:]
        hh = num / jnp.maximum(jnp.abs(den), em_c)
        hn = _rms(hh, ng_ref[:, sl])
        y_ref[g, :, sl] = (hn * _sigmoid(og_ref[g, :, sl].astype(F32))).astype(y_ref.dtype)
        s_out[g][h] = w_c[h:h + 1, :] * s_in[g][h] + _dot((k_f * w_k).T.astype(BF16), v_ext)
    for g in range(group):
        m_scr[g] = jnp.broadcast_to(m_out[g], m_scr.shape[1:])
        for h in range(heads):
            s_scr[g, h] = s_out[g][h]


def _mlstm_rows(proj, gates_t, conv_w, bias_i, bias_f, norm_g, *, batch, seq, blk, group):
    t, npj = proj.shape
    w = MLSTM_W
    proj3 = proj.reshape(batch, seq, npj)
    cols = lambda c: (lambda b, i: (b, i, c))
    const2 = lambda b, i: (0, 0)
    nblk = seq // blk
    gate_specs = [pl.BlockSpec((8, blk), functools.partial(lambda b, i, g: (0, (b * group + g) * nblk + i), g=g))
                  for g in range(group)]
    y = pl.pallas_call(
        functools.partial(_mlstm_rows_kernel, blk=blk, group=group),
        grid=(batch // group, seq // blk),
        in_specs=[pl.BlockSpec((group, blk, 2 * w), cols(OFF_MQ // (2 * w))),
                  pl.BlockSpec((group, blk, w), cols(OFF_MV // w)),
                  pl.BlockSpec((group, blk, w), cols(OFF_MO // w)),
                  *gate_specs,
                  pl.BlockSpec((CONV_WIDTH, 2 * w), const2),
                  pl.BlockSpec((8, 1), const2), pl.BlockSpec((8, 1), const2),
                  pl.BlockSpec((1, w), const2)],
        out_specs=pl.BlockSpec((group, blk, w), cols(0)),
        out_shape=jax.ShapeDtypeStruct((batch, seq, w), BF16),
        scratch_shapes=[pltpu.VMEM((group, blk + 8, 2 * w), F32),
                        pltpu.VMEM((group, MLSTM_HEADS, MLSTM_DH, 2 * MLSTM_DH), F32),
                        pltpu.VMEM((group, 8, 128), F32)],
        compiler_params=_cparams("parallel", "arbitrary", vmem=VMEM_LIMIT_SMALL),
        name="mlstm",
    )(proj3, proj3, proj3, *([gates_t] * group), conv_w, bias_i, bias_f, norm_g)
    return y.reshape(t, w)


def _attnproj_kernel(h_ref, w_ref, seg_ref, gq_ref, gk_ref, o_ref, r_scr, *, dil):
    gw, half = ATTN_GW, ATTN_SLAB // 2
    sub_rows = ATTNPROJ_SUB
    seg, sub_seg = ATTN_TILE // dil, sub_rows // dil

    def head_norm(x, gain):
        ss = _dot((x * x).astype(BF16), seg_ref[...])
        return x * lax.rsqrt(ss * (1.0 / ATTN_DH) + EPS) * gain

    low = lax.broadcasted_iota(jnp.int32, (1, ATTN_SLAB), 1) < half
    for s in range(ATTN_TILE // sub_rows):
        rows = slice(s * sub_rows, (s + 1) * sub_rows)
        res = _dot_nt(h_ref[rows, :], w_ref[0])
        q = head_norm(res[:, :gw], gq_ref[...]) * (ATTN_DH ** -0.5)
        k = head_norm(res[:, gw:2 * gw], gk_ref[...])
        slabs = []
        for pair in range(gw // ATTN_SLAB):
            qp = q[:, pair * ATTN_SLAB:(pair + 1) * ATTN_SLAB]
            slabs += [jnp.where(low, qp, 0.0), jnp.where(low, 0.0, qp)]
        slabs += [k[:, c * 128:(c + 1) * 128] for c in range(gw // 128)]
        slabs += [res[:, 2 * gw + c * 128:2 * gw + (c + 1) * 128] for c in range(gw // 128)]
        pitch = dil + 1 if dil % 16 == 0 else dil
        for c, slab in enumerate(slabs):
            if dil == 1:
                o_ref[rows, c * 128:(c + 1) * 128] = slab.astype(o_ref.dtype)
            elif pitch == dil:
                r_scr[s % 2, c, 0:sub_rows, :] = slab
            else:
                for i in range(sub_seg):
                    r_scr[s % 2, c, pitch * i:pitch * i + dil, :] = slab[dil * i:dil * (i + 1), :]
        if dil > 1:
            for r in range(dil):
                dst = slice(r * seg + s * sub_seg, r * seg + (s + 1) * sub_seg)
                for c in range(r_scr.shape[1]):
                    o_ref[dst, c * 128:(c + 1) * 128] = (
                        r_scr[s % 2, c, pl.ds(r, sub_seg, stride=pitch), :].astype(o_ref.dtype))


def _attnproj(h, w, seg_ones, gq, gk, *, layer, group, dilation):
    t, d = h.shape
    wcols = 3 * ATTN_GW
    const2 = lambda i: (0, 0)
    return pl.pallas_call(
        functools.partial(_attnproj_kernel, dil=dilation),
        grid=(t // ATTN_TILE,),
        in_specs=[pl.BlockSpec((ATTN_TILE, d), lambda i: (i, 0)),
                  pl.BlockSpec((1, wcols, d), lambda i: (layer, group, 0)),
                  pl.BlockSpec((ATTN_GW, ATTN_GW), const2),
                  pl.BlockSpec((1, ATTN_GW), const2), pl.BlockSpec((1, ATTN_GW), const2)],
        out_specs=pl.BlockSpec((ATTN_TILE, ATTN_COLS), lambda i: (i, 0)),
        out_shape=jax.ShapeDtypeStruct((t, ATTN_COLS), BF16),
        scratch_shapes=[pltpu.VMEM((2, ATTN_COLS // 128, ATTNPROJ_SUB + ATTNPROJ_SUB // 16, 128), F32)],
        compiler_params=_cparams("parallel"),
        name=f"attnproj{group}",
    )(h, w, seg_ones, gq, gk)


def _dattn_kernel(q_ref, kc_ref, kp_ref, vc_ref, vp_ref, b0_ref, o_ref, lse_ref,
                  kx_scr, vx_scr, o_scr, l_scr, bias_scr, *, dil):
    blk = ATTN_BLOCK
    per = ATTN_SUB // dil
    pitch = dil + 1 if dil % 16 == 0 else dil
    first_tile = pl.program_id(1) == 0

    @pl.when(first_tile)
    def _():
        for h in range(HEADS_PER_GROUP):
            bias_scr[h] = pltpu.roll(jnp.broadcast_to(b0_ref[h], (blk, 2 * blk)), 0, 1, stride=1, stride_axis=0)

    for r in range(dil):
        base = r * (per + 1) * blk
        last = slice((r * per + per - 1) * blk, (r * per + per) * blk)
        mine = slice(r * per * blk, (r + 1) * per * blk)
        kx_scr[base:base + blk, :] = kp_ref[last, :]
        vx_scr[base:base + blk, :] = vp_ref[last, :]
        kx_scr[base + blk:base + (per + 1) * blk, :] = kc_ref[mine, :]
        vx_scr[base + blk:base + (per + 1) * blk, :] = vc_ref[mine, :]

    low = lax.broadcasted_iota(jnp.int32, (1, ATTN_SLAB), 1) < ATTN_SLAB // 2
    no_prev = lax.broadcasted_iota(jnp.int32, (1, 2 * blk), 1) < blk
    for r in range(dil):
        for sub in range(per):
            u = r * per + sub
            win = slice((r * (per + 1) + sub) * blk, (r * (per + 1) + sub + 2) * blk)
            o_slabs, l_slabs = [], []
            for pair in range(ATTN_GW // ATTN_SLAB):
                cols = slice(pair * ATTN_SLAB, (pair + 1) * ATTN_SLAB)
                kx, vx = kx_scr[win, cols], vx_scr[win, cols]
                o_pair, l_pair = [], []
                for h in (2 * pair, 2 * pair + 1):
                    logits = _dot_nt(q_ref[u * blk:(u + 1) * blk, h * ATTN_SLAB:(h + 1) * ATTN_SLAB], kx)
                    logits = logits + bias_scr[h]
                    if sub == 0:
                        logits = jnp.where(first_tile & no_prev, NEG, logits)
                    m = jnp.max(logits, axis=-1, keepdims=True)
                    p = jnp.exp(logits - m)
                    l = jnp.sum(p, axis=-1, keepdims=True)
                    o_pair.append(_dot(p.astype(BF16), vx) / l)
                    l_pair.append(m + jnp.log(l))
                o_slabs.append(jnp.where(low, o_pair[0], o_pair[1]))
                l_slabs.append(jnp.where(low, l_pair[0], l_pair[1]))
            for c in range(ATTN_GW // ATTN_SLAB):
                cols = slice(c * ATTN_SLAB, (c + 1) * ATTN_SLAB)
                if dil == 1:
                    o_ref[u * blk:(u + 1) * blk, cols] = o_slabs[c].astype(o_ref.dtype)
                    lse_ref[u * blk:(u + 1) * blk, cols] = l_slabs[c]
                else:
                    dst = pl.ds(sub * blk * pitch + r, blk, stride=pitch)
                    o_scr[c, dst, :] = o_slabs[c]
                    l_scr[c, dst, :] = l_slabs[c]
    if dil > 1:
        for c in range(ATTN_GW // ATTN_SLAB):
            cols = slice(c * ATTN_SLAB, (c + 1) * ATTN_SLAB)
            if pitch == dil:
                o_ref[:, cols] = o_scr[c, 0:ATTN_TILE, :].astype(o_ref.dtype)
                lse_ref[:, cols] = l_scr[c, 0:ATTN_TILE, :]
            else:
                for i in range(ATTN_TILE // dil):
                    o_ref[dil * i:dil * (i + 1), cols] = o_scr[c, pitch * i:pitch * i + dil, :].astype(o_ref.dtype)
                    lse_ref[dil * i:dil * (i + 1), cols] = l_scr[c, pitch * i:pitch * i + dil, :]


def _dattn(aproj, bias, *, seq, group, dilation):
    t = aproj.shape[0]
    tiles = seq // ATTN_TILE
    qw = HEADS_PER_GROUP * ATTN_SLAB
    cq, ck, cv = 0, qw // ATTN_GW, qw // ATTN_GW + 1
    blk = (ATTN_TILE, ATTN_GW)
    cur = lambda c: (lambda b, j: (b * tiles + j, c))
    prev = lambda c: (lambda b, j: (b * tiles + jnp.maximum(j - 1, 0), c))
    xrows = ATTN_TILE + dilation * ATTN_BLOCK
    return pl.pallas_call(
        functools.partial(_dattn_kernel, dil=dilation),
        grid=(t // seq, tiles),
        in_specs=[pl.BlockSpec((ATTN_TILE, qw), cur(cq)),
                  pl.BlockSpec(blk, cur(ck)), pl.BlockSpec(blk, prev(ck)),
                  pl.BlockSpec(blk, cur(cv)), pl.BlockSpec(blk, prev(cv)),
                  pl.BlockSpec((HEADS_PER_GROUP, 1, 2 * ATTN_BLOCK), lambda b, j: (0, 0, 0))],
        out_specs=[pl.BlockSpec(blk, cur(0)), pl.BlockSpec(blk, cur(0))],
        out_shape=[jax.ShapeDtypeStruct((t, ATTN_GW), BF16), jax.ShapeDtypeStruct((t, ATTN_GW), F32)],
        scratch_shapes=[pltpu.VMEM((xrows, ATTN_GW), BF16), pltpu.VMEM((xrows, ATTN_GW), BF16),
                        pltpu.VMEM((ATTN_GW // ATTN_SLAB, ATTN_TILE + ATTN_TILE // 16, ATTN_SLAB), F32),
                        pltpu.VMEM((ATTN_GW // ATTN_SLAB, ATTN_TILE + ATTN_TILE // 16, ATTN_SLAB), F32),
                        pltpu.VMEM((HEADS_PER_GROUP, ATTN_BLOCK, 2 * ATTN_BLOCK), F32)],
        compiler_params=_cparams("parallel", "arbitrary"),
        name=f"dattn{group}",
    )(aproj, aproj, aproj, aproj, aproj, bias)


def _rel_bucket(n):
    max_exact = REL_BUCKETS // 2
    nf = jnp.maximum(n, 1).astype(F32)
    log_b = max_exact + (jnp.log(nf / max_exact) / math.log(REL_MAX_DIST / max_exact)
                         * (REL_BUCKETS - max_exact)).astype(jnp.int32)
    return jnp.where(n < max_exact, n, jnp.minimum(log_b, REL_BUCKETS - 1))


def _attn_bias(rel_bias, group):
    window, dilation = ATTN_PATTERNS[group]
    steps = window // dilation
    assert steps == ATTN_BLOCK
    hs = slice(group * HEADS_PER_GROUP, (group + 1) * HEADS_PER_GROUP)
    bucket = _rel_bucket((steps - jnp.arange(steps + 1)) * dilation)
    by_dist = jnp.dot(jax.nn.one_hot(bucket, REL_BUCKETS, dtype=F32), rel_bias[:, hs].astype(F32),
                      precision=lax.Precision.HIGHEST)
    row0 = jnp.concatenate([by_dist, jnp.full((2 * ATTN_BLOCK - steps - 1, HEADS_PER_GROUP), NEG, F32)], axis=0)
    return row0.T[:, None, :]


def _merge_kernel(ya_ref, yb0_ref, yb1_ref, yb2_ref, l0_ref, l1_ref, l2_ref, gu_ref, gv_ref, gate_ref,
                  x_ref, wa_ref, wb_ref, wc_ref, wo_ref, ws_ref, bs_ref, gg_ref, o_ref, yc_scr, *, tm):
    d = x_ref.shape[1]
    l0, l1, l2 = l0_ref[...], l1_ref[...], l2_ref[...]
    mx = jnp.maximum(jnp.maximum(l0, l1), l2)
    e0, e1, e2 = jnp.exp(l0 - mx), jnp.exp(l1 - mx), jnp.exp(l2 - mx)
    inv = 1.0 / (e0 + e1 + e2)
    yb = jnp.concatenate([(yb0_ref[...].astype(F32) * (e0 * inv)).astype(BF16),
                          (yb1_ref[...].astype(F32) * (e1 * inv)).astype(BF16),
                          (yb2_ref[...].astype(F32) * (e2 * inv)).astype(BF16)], axis=-1)

    for j in range(tm // GMLP_CHUNK):
        rows = slice(j * GMLP_CHUNK, (j + 1) * GMLP_CHUNK)
        for g in range(GMLP_GROUPS):
            cols = slice(g * GMLP_GC, (g + 1) * GMLP_GC)
            u = _gelu(gu_ref[rows, cols].astype(F32))
            v = _rms(_gelu(gv_ref[rows, cols].astype(F32)), gg_ref[:, cols])
            mixed = _dot(ws_ref[g], v.astype(BF16)) + bs_ref[g]
            yc_scr[rows, cols] = (u * mixed).astype(BF16)

    def gate2(k):
        return jnp.tanh(0.5 * gate_ref[:, k * d:(k + 1) * d].astype(F32)) + 1.0

    merged2 = gate2(0) * _dot(ya_ref[...], wa_ref[...])
    merged2 = merged2 + gate2(1) * _dot(yb, wb_ref[...])
    merged2 = merged2 + gate2(2) * _dot(yc_scr[...], wc_ref[...])
    o_ref[...] = x_ref[...] + 0.5 * _dot(merged2.astype(BF16), wo_ref[...])


def _merge(ya, ybs, lses, proj, x2d, wa, wb, wc, wo, ws, bsb, gg, *, tm):
    t, d = x2d.shape
    row = lambda c: (lambda i: (i, c))
    full2 = lambda i: (0, 0)
    full3 = lambda i: (0, 0, 0)
    gspec = pl.BlockSpec((tm, ATTN_GW), row(0))
    return pl.pallas_call(
        functools.partial(_merge_kernel, tm=tm),
        grid=(t // tm,),
        in_specs=[pl.BlockSpec((tm, MLSTM_W), row(0)),
                  gspec, gspec, gspec, gspec, gspec, gspec,
                  pl.BlockSpec((tm, GMLP_W), row(OFF_GU // GMLP_W)),
                  pl.BlockSpec((tm, GMLP_W), row(OFF_GV // GMLP_W)),
                  pl.BlockSpec((tm, N_BRANCH * d), row(OFF_GATE // (N_BRANCH * d))),
                  pl.BlockSpec((tm, d), row(0)),
                  pl.BlockSpec(wa.shape, full2), pl.BlockSpec(wb.shape, full2),
                  pl.BlockSpec(wc.shape, full2), pl.BlockSpec(wo.shape, full2),
                  pl.BlockSpec(ws.shape, full3), pl.BlockSpec(bsb.shape, full3),
                  pl.BlockSpec(gg.shape, full2)],
        out_specs=pl.BlockSpec((tm, d), row(0)),
        out_shape=jax.ShapeDtypeStruct((t, d), F32),
        scratch_shapes=[pltpu.VMEM((tm, GMLP_W), BF16)],
        compiler_params=_cparams("parallel"),
        name="merge",
    )(ya, *ybs, *lses, proj, proj, proj, x2d, wa, wb, wc, wo, ws, bsb, gg)


def _memkv_kernel(mem_ref, g_ref, w_ref, gk_ref, k_ref, v_ref):
    dh, w = XATTN_DH, XATTN_W
    kv = _dot(_rms(mem_ref[0], g_ref[...]).astype(BF16), w_ref[...])
    for h in range(XATTN_HEADS):
        sl = slice(h * dh, (h + 1) * dh)
        k_ref[0, :, sl] = _rms(kv[:, sl], gk_ref[...]).astype(k_ref.dtype)
    v_ref[0] = kv[:, w:].astype(v_ref.dtype)


def _memkv(mem, gain, w_kv, gk):
    b, m, d = mem.shape
    full2 = lambda i: (0, 0)
    return pl.pallas_call(
        _memkv_kernel,
        grid=(b,),
        in_specs=[pl.BlockSpec((1, m, d), lambda i: (i, 0, 0)),
                  pl.BlockSpec((1, d), full2),
                  pl.BlockSpec(w_kv.shape, full2),
                  pl.BlockSpec((1, XATTN_DH), full2)],
        out_specs=[pl.BlockSpec((1, m, XATTN_W), lambda i: (i, 0, 0)),
                   pl.BlockSpec((1, m, XATTN_W), lambda i: (i, 0, 0))],
        out_shape=[jax.ShapeDtypeStruct((b, m, XATTN_W), BF16),
                   jax.ShapeDtypeStruct((b, m, XATTN_W), BF16)],
        compiler_params=_cparams("parallel", vmem=VMEM_LIMIT_SMALL),
        name="memkv",
    )(mem, gain, w_kv, gk)


def _route(logits):
    tm = logits.shape[1]
    e = jnp.exp(logits - jnp.max(logits, axis=0, keepdims=True))
    probs = e / jnp.sum(e, axis=0, keepdims=True)
    rowi = lax.broadcasted_iota(jnp.int32, (8, tm), 0)
    real = rowi < EXPERTS_PER_GROUP
    tops = []
    for g in range(N_EXPERT_GROUPS):
        pg = jnp.where(real, probs[8 * g:8 * g + 8, :], -0.5)
        m1 = jnp.max(pg, axis=0, keepdims=True)
        i1 = jnp.min(jnp.where(pg == m1, rowi, 8), axis=0, keepdims=True)
        pg2 = jnp.where(rowi == i1, -1.0, pg)
        m2 = jnp.max(pg2, axis=0, keepdims=True)
        i2 = jnp.min(jnp.where(pg2 == m2, rowi, 8), axis=0, keepdims=True)
        tops.append((m1, i1, m2, i2))
    best = jnp.zeros((1, tm), jnp.int32)
    best_score = tops[0][0] + tops[0][2]
    for g in range(1, N_EXPERT_GROUPS):
        score = tops[g][0] + tops[g][2]
        better = score > best_score
        best = jnp.where(better, g, best)
        best_score = jnp.where(better, score, best_score)
    m1, i1, m2, i2 = tops[0]
    for g in range(1, N_EXPERT_GROUPS):
        m1, i1, m2, i2 = (jnp.where(best == g, new, old) for new, old in zip(tops[g], (m1, i1, m2, i2)))
    tot = m1 + m2
    base = best * EXPERTS_PER_GROUP
    return base + i1, base + i2, m1 / tot, m2 / tot


def _pack_bf16_pairs(x):
    n = x.shape[1] // 2
    hi = lax.bitcast_convert_type(x[:, :n].astype(BF16).astype(F32), jnp.uint32)
    lo = lax.bitcast_convert_type(x[:, n:].astype(BF16).astype(F32), jnp.uint32)
    return hi | (lo >> 16)


def _unpack_bf16_pairs(p):
    hi = lax.bitcast_convert_type(p & jnp.uint32(0xFFFF0000), F32)
    lo = lax.bitcast_convert_type(p << 16, F32)
    return hi, lo


def _store_row_chunks(ref, packed):
    for j in range(ROW_CHUNKS):
        ref[j] = packed[:, j * 128:(j + 1) * 128]


def _load_row_chunks(ref):
    return jnp.concatenate([ref[j] for j in range(ROW_CHUNKS)], axis=-1)


def _xattn_kernel(x_ref, k_ref, v_ref, gx_ref, wq_ref, gq_ref, wo_ref, gf_ref, rw_ref, rb_ref,
                  xo_ref, hf_ref, eidx_ref, wts_ref, *, sub):
    dh = XATTN_DH
    rw = rw_ref[...]
    rw_hi, rw_lo = _split_bf16(rw)
    for s in range(x_ref.shape[0] // sub):
        rows = slice(s * sub, (s + 1) * sub)
        x = x_ref[rows, :]
        q = _dot(_rms(x, gx_ref[...]).astype(BF16), wq_ref[...])
        outs = []
        for h in range(XATTN_HEADS):
            sl = slice(h * dh, (h + 1) * dh)
            q_h = (_rms(q[:, sl], gq_ref[...]) * (dh ** -0.5)).astype(BF16)
            logits = _dot_nt(q_h, k_ref[0, :, sl])
            p = jnp.exp(logits - jnp.max(logits, axis=-1, keepdims=True))
            o = _dot(p.astype(BF16), v_ref[0, :, sl]) / jnp.sum(p, axis=-1, keepdims=True)
            outs.append(o.astype(BF16))
        xn = x + _dot(jnp.concatenate(outs, axis=-1), wo_ref[...])
        xo_ref[rows, :] = xn
        hf = _rms(xn, gf_ref[...])
        packed = _pack_bf16_pairs(hf)
        for j in range(ROW_CHUNKS):
            hf_ref[j, rows, :] = packed[:, j * 128:(j + 1) * 128]
        hf_hi, hf_lo = _split_bf16(hf)
        logits_t = _dot_nt(rw_hi, hf_hi) + _dot_nt(rw_hi, hf_lo) + _dot_nt(rw_lo, hf_hi) + rb_ref[...]
        e1, e2, w1, w2 = _route(logits_t)
        eidx_ref[:, rows] = jnp.concatenate([e1, e2, jnp.zeros((6, sub), jnp.int32)], axis=0)
        wts_ref[:, rows] = jnp.concatenate([w1, w2, jnp.zeros((6, sub), F32)], axis=0)


def _xattn(x2d, k, v, gx, wq, gq, wo, gf, rw_t, rb, *, seq, tm):
    t, d = x2d.shape
    per_b = seq // tm
    full2 = lambda i: (0, 0)
    kv_spec = pl.BlockSpec((1,) + k.shape[1:], lambda i: (i // per_b, 0, 0))
    return pl.pallas_call(
        functools.partial(_xattn_kernel, sub=min(tm, XATTN_SUB)),
        grid=(t // tm,),
        in_specs=[pl.BlockSpec((tm, d), lambda i: (i, 0)), kv_spec, kv_spec,
                  pl.BlockSpec((1, d), full2), pl.BlockSpec(wq.shape, full2),
                  pl.BlockSpec((1, XATTN_DH), full2), pl.BlockSpec(wo.shape, full2),
                  pl.BlockSpec((1, d), full2), pl.BlockSpec(rw_t.shape, full2),
                  pl.BlockSpec(rb.shape, full2)],
        out_specs=[pl.BlockSpec((tm, d), lambda i: (i, 0)),
                   pl.BlockSpec((ROW_CHUNKS, tm, 128), lambda i: (0, i, 0)),
                   pl.BlockSpec((8, tm), lambda i: (0, i)),
                   pl.BlockSpec((8, tm), lambda i: (0, i))],
        out_shape=[jax.ShapeDtypeStruct((t, d), F32),
                   jax.ShapeDtypeStruct((ROW_CHUNKS, t, 128), jnp.uint32),
                   jax.ShapeDtypeStruct((8, t), jnp.int32),
                   jax.ShapeDtypeStruct((8, t), F32)],
        compiler_params=_cparams("parallel"),
        name="xattn_router",
    )(x2d, k, v, gx, wq, gq, wo, gf, rw_t, rb)


def _moe_plan_kernel(eidx_ref, i1_ref, i2_ref, te_ref, na_ref, cnt_scr, carry_scr, *, tb, tm, plane_rows):
    ne = N_EXPERTS
    hp = lax.Precision.HIGHEST
    phase, j = pl.program_id(0), pl.program_id(1)
    rows = lax.broadcasted_iota(jnp.int32, (ne, tb), 0)
    oh1 = rows == eidx_ref[0:1, :]
    oh2 = rows == eidx_ref[1:2, :]
    a = oh1.astype(F32) + oh2.astype(F32)
    blk_cnt = jnp.broadcast_to(jnp.sum(a, axis=1, keepdims=True), cnt_scr.shape)

    @pl.when((phase == 0) & (j == 0))
    def _():
        cnt_scr[...] = jnp.zeros_like(cnt_scr)

    @pl.when(phase == 0)
    def _():
        cnt_scr[...] += blk_cnt

    @pl.when((phase == 1) & (j == 0))
    def _():
        padded = jnp.ceil(cnt_scr[...] * (1.0 / tm)) * tm
        er = lax.broadcasted_iota(jnp.int32, (ne, ne), 0)
        ec = lax.broadcasted_iota(jnp.int32, (ne, ne), 1)
        off = jnp.dot((ec < er).astype(F32), padded, precision=hp, preferred_element_type=F32)
        carry_scr[...] = off
        seg_end = (off + padded)[:, 0:1]
        tile_start = lax.broadcasted_iota(jnp.int32, (ne, te_ref.shape[1]), 1).astype(F32) * tm
        te = jnp.sum((seg_end <= tile_start).astype(F32), axis=0, keepdims=True)
        te_ref[...] = jnp.broadcast_to(jnp.minimum(te, ne - 1.0), te_ref.shape).astype(jnp.int32)
        total = jnp.sum(padded[:, 0:1], axis=0, keepdims=True)
        na_ref[...] = jnp.broadcast_to(total * (1.0 / tm), na_ref.shape).astype(jnp.int32)

    @pl.when(phase == 1)
    def _():
        before = (lax.broadcasted_iota(jnp.int32, (tb, tb), 0)
                  < lax.broadcasted_iota(jnp.int32, (tb, tb), 1)).astype(BF16)
        rank = carry_scr[:, 0:1] + _dot(a.astype(BF16), before)
        d1 = jnp.sum(jnp.where(oh1, rank, 0.0), axis=0, keepdims=True).astype(jnp.int32)
        d2 = jnp.sum(jnp.where(oh2, rank, 0.0), axis=0, keepdims=True).astype(jnp.int32)
        plane = lax.broadcasted_iota(jnp.int32, (8, tb), 0) * plane_rows
        i1_ref[...] = jnp.where(plane < ROW_CHUNKS * plane_rows, plane + d1, 0)
        i2_ref[...] = jnp.where(plane < ROW_CHUNKS * plane_rows, plane + d2, 0)
        carry_scr[...] += blk_cnt


def _moe_plan(eidx, *, tm, n_tiles, tb=PLAN_TB):
    t = eidx.shape[1]
    ntp = -(-n_tiles // 128) * 128
    return pl.pallas_call(
        functools.partial(_moe_plan_kernel, tb=tb, tm=tm, plane_rows=n_tiles * tm),
        grid=(2, t // tb),
        in_specs=[pl.BlockSpec((8, tb), lambda p, j: (0, j))],
        out_specs=[pl.BlockSpec((8, tb), lambda p, j: (0, j * p)),
                   pl.BlockSpec((8, tb), lambda p, j: (0, j * p)),
                   pl.BlockSpec((8, ntp), lambda p, j: (0, 0)),
                   pl.BlockSpec((8, 128), lambda p, j: (0, 0))],
        out_shape=[jax.ShapeDtypeStruct((8, t), jnp.int32),
                   jax.ShapeDtypeStruct((8, t), jnp.int32),
                   jax.ShapeDtypeStruct((8, ntp), jnp.int32),
                   jax.ShapeDtypeStruct((8, 128), jnp.int32)],
        scratch_shapes=[pltpu.VMEM((N_EXPERTS, 128), F32), pltpu.VMEM((N_EXPERTS, 128), F32)],
        compiler_params=_cparams("arbitrary", "arbitrary", vmem=VMEM_LIMIT_SMALL),
        name="moe_plan",
    )(eidx)


def _sc_mesh():
    return plsc.VectorSubcoreMesh(core_axis_name="c", subcore_axis_name="s",
                                  num_cores=SC_CORES, num_subcores=SC_SUBCORES)


def _sc_index_spec(tokens):
    nb = tokens // SC_WINDOW
    return pl.BlockSpec((1, SC_WINDOW), lambda i: (i // nb, i % nb))


def _sc_dispatch(rows, i1, i2, n_out):
    n = rows.shape[0]
    tokens = i1.shape[1]

    @functools.partial(pl.kernel, out_type=jax.ShapeDtypeStruct((n_out, 128), rows.dtype), mesh=_sc_mesh(),
                       name="moe_dispatch")
    def k(x_hbm, i1_hbm, i2_hbm, o_hbm):
        def body(x_vmem, i1_vmem, i2_vmem):
            pltpu.sync_copy(x_vmem, o_hbm.at[i1_vmem.at[0]])
            pltpu.sync_copy(x_vmem, o_hbm.at[i2_vmem.at[0]])

        pltpu.emit_pipeline(
            body, grid=(n // SC_WINDOW,),
            in_specs=[pl.BlockSpec((SC_WINDOW, 128), lambda i: (i, 0)),
                      _sc_index_spec(tokens), _sc_index_spec(tokens)],
            out_specs=[],
            core_axis_name=("c", "s"), dimension_semantics=(pltpu.PARALLEL,),
        )(x_hbm, i1_hbm, i2_hbm)

    return k(rows, i1, i2)


def _sc_collect(table, i1, i2):
    tokens = i1.shape[1]
    n = ROW_CHUNKS * tokens
    out = jax.ShapeDtypeStruct((n, 128), table.dtype)

    @functools.partial(pl.kernel, out_type=(out, out), mesh=_sc_mesh(), name="moe_collect",
                       scratch_types=[pltpu.SemaphoreType.DMA, pltpu.SemaphoreType.DMA])
    def k(t_hbm, i1_hbm, i2_hbm, o1_hbm, o2_hbm, sem1, sem2):
        def body(i1_vmem, i2_vmem, o1_vmem, o2_vmem):
            first = pltpu.async_copy(t_hbm.at[i1_vmem.at[0]], o1_vmem, sem1)
            second = pltpu.async_copy(t_hbm.at[i2_vmem.at[0]], o2_vmem, sem2)
            first.wait()
            second.wait()

        pltpu.emit_pipeline(
            body, grid=(n // SC_WINDOW,),
            in_specs=[_sc_index_spec(tokens), _sc_index_spec(tokens)],
            out_specs=[pl.BlockSpec((SC_WINDOW, 128), lambda i: (i, 0)),
                       pl.BlockSpec((SC_WINDOW, 128), lambda i: (i, 0))],
            core_axis_name=("c", "s"), dimension_semantics=(pltpu.PARALLEL,),
        )(i1_hbm, i2_hbm, o1_hbm, o2_hbm)

    return k(table, i1, i2)


def _experts_kernel(te_ref, na_ref, xs_ref, wg_ref, wu_ref, wd_ref, y_ref, wg_scr, wu_scr, wd_scr):
    i = pl.program_id(0)
    active = i < na_ref[0]

    @pl.when(active & ((i == 0) | (te_ref[i] != te_ref[jnp.maximum(i - 1, 0)])))
    def _():
        wg_scr[...] = wg_ref[0, 0].astype(BF16)
        wu_scr[...] = wu_ref[0, 0].astype(BF16)
        wd_scr[...] = wd_ref[0, 0].astype(BF16)

    @pl.when(active)
    def _():
        hi, lo = _unpack_bf16_pairs(_load_row_chunks(xs_ref))
        h = jnp.concatenate([hi, lo], axis=-1).astype(BF16)
        up = _dot(h, wg_scr[...])
        act = _silu(up) * _dot(h, wu_scr[...])
        _store_row_chunks(y_ref, _pack_bf16_pairs(_dot(act.astype(BF16), wd_scr[...])))


def _experts(tile_expert, n_active, xs, wg, wu, wd, *, layer, tm):
    n_tiles = tile_expert.shape[0]
    _, _, d, dff = wg.shape
    rows = lambda i, te, na: (0, jnp.minimum(i, na[0] - 1), 0)
    expert = lambda i, te, na: (layer, te[i], 0, 0)
    return pl.pallas_call(
        _experts_kernel,
        grid_spec=pltpu.PrefetchScalarGridSpec(
            num_scalar_prefetch=2,
            grid=(n_tiles,),
            in_specs=[pl.BlockSpec((ROW_CHUNKS, tm, 128), rows),
                      pl.BlockSpec((1, 1, d, dff), expert),
                      pl.BlockSpec((1, 1, d, dff), expert),
                      pl.BlockSpec((1, 1, dff, d), expert)],
            out_specs=pl.BlockSpec((ROW_CHUNKS, tm, 128), rows),
            scratch_shapes=[pltpu.VMEM((d, dff), BF16), pltpu.VMEM((d, dff), BF16), pltpu.VMEM((dff, d), BF16)]),
        out_shape=jax.ShapeDtypeStruct(xs.shape, xs.dtype),
        compiler_params=_cparams("arbitrary"),
        name="moe_experts",
    )(tile_expert, n_active, xs, wg, wu, wd)


def _moe_combine_kernel(x_ref, y1_ref, y2_ref, w_ref, o_ref):
    half = x_ref.shape[1] // 2
    hi1, lo1 = _unpack_bf16_pairs(_load_row_chunks(y1_ref))
    hi2, lo2 = _unpack_bf16_pairs(_load_row_chunks(y2_ref))
    tm = x_ref.shape[0]
    w_cols = jnp.concatenate([w_ref[...], jnp.zeros((128 - w_ref.shape[0], tm), F32)], axis=0).T
    w1, w2 = w_cols[:, 0:1], w_cols[:, 1:2]
    o_ref[:, :half] = x_ref[:, :half] + w1 * hi1 + w2 * hi2
    o_ref[:, half:] = x_ref[:, half:] + w1 * lo1 + w2 * lo2


def _moe_combine(x2d, y1, y2, wts, *, tm):
    t, d = x2d.shape
    chunk_spec = pl.BlockSpec((ROW_CHUNKS, tm, 128), lambda i: (0, i, 0))
    return pl.pallas_call(
        _moe_combine_kernel,
        grid=(t // tm,),
        in_specs=[pl.BlockSpec((tm, d), lambda i: (i, 0)), chunk_spec, chunk_spec,
                  pl.BlockSpec((wts.shape[0], tm), lambda i: (0, i))],
        out_specs=pl.BlockSpec((tm, d), lambda i: (i, 0)),
        out_shape=jax.ShapeDtypeStruct((t, d), F32),
        compiler_params=_cparams("parallel"),
        name="moe_combine",
    )(x2d, y1, y2, wts)


def _moe(x2d, hf_rows, eidx, wts, wg, wu, wd, *, layer):
    t = x2d.shape[0]
    tm = MOE_TM
    n_tiles = 2 * t // tm + N_EXPERTS
    plane = n_tiles * tm
    i1, i2, te, na = _moe_plan(eidx, tm=tm, n_tiles=n_tiles)
    xs = _sc_dispatch(hf_rows.reshape(ROW_CHUNKS * t, 128), i1, i2, ROW_CHUNKS * plane)
    ys = _experts(te[0, :n_tiles], na[0, :1], xs.reshape(ROW_CHUNKS, plane, 128), wg, wu, wd,
                  layer=layer, tm=tm)
    y1, y2 = _sc_collect(ys.reshape(ROW_CHUNKS * plane, 128), i1, i2)
    return _moe_combine(x2d, y1.reshape(ROW_CHUNKS, t, 128), y2.reshape(ROW_CHUNKS, t, 128), wts,
                        tm=COMBINE_TM)


W_ROWS = 256


def _w_rows_kernel(start_ref, valid_ref, w_ref, o_ref):
    del start_ref
    row = lax.broadcasted_iota(jnp.int32, w_ref.shape, 1)
    o_ref[...] = jnp.where(row < valid_ref[pl.program_id(0)], w_ref[...], 0.0).astype(o_ref.dtype)


def _w_rows(w_t, starts, valid):
    depth, _, d = w_t.shape
    nblk = len(starts)
    return pl.pallas_call(
        _w_rows_kernel,
        grid_spec=pltpu.PrefetchScalarGridSpec(
            num_scalar_prefetch=2,
            grid=(nblk,),
            in_specs=[pl.BlockSpec((pl.Element(depth), pl.Element(W_ROWS), pl.Element(d)),
                                   lambda c, st, va: (0, pl.multiple_of(st[c], 8), 0))],
            out_specs=pl.BlockSpec((depth, W_ROWS, d), lambda c, st, va: (0, c, 0))),
        out_shape=jax.ShapeDtypeStruct((depth, nblk * W_ROWS, d), BF16),
        compiler_params=_cparams("arbitrary", vmem=VMEM_LIMIT_SMALL),
        name="w_in_rows",
    )(jnp.asarray(starts, jnp.int32), jnp.asarray(valid, jnp.int32), w_t)


def _w_in_layout(w_in):
    w_t = jnp.swapaxes(w_in, 1, 2)
    src_if = 4 * MLSTM_W
    src_a = src_if + 2 * MLSTM_HEADS
    src_g = src_a + 3 * ATTN_W
    starts = list(range(0, src_if, W_ROWS)) + [src_g + k * W_ROWS for k in range((OFF_IF - OFF_GU) // W_ROWS)]
    valid = [W_ROWS] * len(starts)
    starts.append(src_if)
    valid.append(2 * MLSTM_HEADS)
    assert len(starts) * W_ROWS == N_PROJ and ATTN_GW == W_ROWS
    a_starts = [src_a + j * ATTN_W + g * ATTN_GW for g in range(len(ATTN_PATTERNS)) for j in range(3)]
    return _w_rows(w_t, starts, valid), _w_rows(w_t, a_starts, [W_ROWS] * len(a_starts))


def kernel(x, mem, norm_mix, w_in, mlstm_conv, mlstm_gate_b, mlstm_norm, attn_qk_norm, gmlp_norm, gmlp_ws,
           gmlp_bs, w_branch_a, w_branch_b, w_branch_c, w_out, rel_bias, norm_xattn, norm_mem, w_xq, w_xkv,
           xattn_qk_norm, w_xo, norm_ffn, router_w, router_b, w_expert_gate, w_expert_up, w_expert_down):
    b, s, d = x.shape
    t = b * s
    depth = w_in.shape[0]
    assert d == 2 * ROW_CHUNKS * 128 and OFF_IF == OFF_GATE + N_BRANCH * d
    assert s % ATTN_TILE == 0 and s % MLSTM_BLOCK == 0 and b % MLSTM_GROUP == 0
    assert all(window == dil * ATTN_BLOCK and ATTN_SUB % dil == 0 for window, dil in ATTN_PATTERNS)
    assert t % max(INPROJ_TM, MERGE_TM, XATTN_TM, COMBINE_TM, MOE_TM, PLAN_TB) == 0 and s % XATTN_TM == 0
    x2d = x.reshape(t, d)

    biases = [_attn_bias(rel_bias, g) for g in range(len(ATTN_PATTERNS))]
    rw_t = jnp.zeros((N_EXPERT_GROUPS, 8, d), F32).at[:, :EXPERTS_PER_GROUP].set(
        router_w.T.reshape(N_EXPERT_GROUPS, EXPERTS_PER_GROUP, d)).reshape(ROUTER_ROWS, d)
    rb = jnp.full((N_EXPERT_GROUPS, 8), NEG, F32).at[:, :EXPERTS_PER_GROUP].set(
        router_b.astype(F32).reshape(N_EXPERT_GROUPS, EXPERTS_PER_GROUP)).reshape(ROUTER_ROWS, 1)
    tril = jnp.tril(jnp.ones((GMLP_CHUNK, GMLP_CHUNK), bool))
    head_of = jnp.arange(ATTN_GW) // ATTN_DH
    seg_ones = (head_of[:, None] == head_of[None, :]).astype(BF16)

    w_main, w_attn = _w_in_layout(w_in)

    for l in range(depth):
        proj, h_mix, gates_t = _inproj(x2d, norm_mix[l][None], w_main, layer=l, tm=INPROJ_TM,
                                       tn=INPROJ_TN)
        gq = jnp.tile(attn_qk_norm[l, 0], HEADS_PER_GROUP)[None]
        gk = jnp.tile(attn_qk_norm[l, 1], HEADS_PER_GROUP)[None]

        nh = MLSTM_HEADS
        bias_i = jnp.zeros((8, 1), F32).at[:nh, 0].set(mlstm_gate_b[l, :nh])
        bias_f = jnp.zeros((8, 1), F32).at[:nh, 0].set(mlstm_gate_b[l, nh:])
        ya = _mlstm_rows(proj, gates_t, mlstm_conv[l], bias_i, bias_f, mlstm_norm[l][None],
                         batch=b, seq=s, blk=MLSTM_BLOCK, group=MLSTM_GROUP)

        ybs, lses = [], []
        for g, (_, dilation) in enumerate(ATTN_PATTERNS):
            aproj = _attnproj(h_mix, w_attn, seg_ones, gq, gk, layer=l, group=g, dilation=dilation)
            o, lse = _dattn(aproj, biases[g], seq=s, group=g, dilation=dilation)
            ybs.append(o)
            lses.append(lse)

        ws = jnp.where(tril, gmlp_ws[l], 0.0).astype(BF16)
        bsb = jnp.broadcast_to(gmlp_bs[l][:, :, None], (GMLP_GROUPS, GMLP_CHUNK, GMLP_GC)).astype(F32)
        x2d = _merge(ya, ybs, lses, proj, x2d, w_branch_a[l].astype(BF16), w_branch_b[l].astype(BF16),
                     w_branch_c[l].astype(BF16), w_out[l].astype(BF16), ws, bsb, gmlp_norm[l][None],
                     tm=MERGE_TM)

        k_mem, v_mem = _memkv(mem, norm_mem[l][None], w_xkv[l].astype(BF16), xattn_qk_norm[l, 1][None])
        x2d, hf_rows, eidx, wts = _xattn(x2d, k_mem, v_mem, norm_xattn[l][None], w_xq[l].astype(BF16),
                                         xattn_qk_norm[l, 0][None], w_xo[l].astype(BF16), norm_ffn[l][None],
                                         rw_t, rb, seq=s, tm=XATTN_TM)

        x2d = _moe(x2d, hf_rows, eidx, wts, w_expert_gate, w_expert_up, w_expert_down, layer=l)

    return x2d.reshape(b, s, d)
```

```python
import functools
import math

import jax
import jax.numpy as jnp
import numpy as np
from jax import lax
from jax.experimental import pallas as pl
from jax.experimental.pallas import tpu as pltpu
from jax.experimental.pallas import tpu_sc as plsc

F32 = jnp.float32
BF16 = jnp.bfloat16

EPS = 1e-6
NEG = -1e30

MLSTM_HEADS = 4
MLSTM_DH = 128
MLSTM_W = MLSTM_HEADS * MLSTM_DH
CONV_WIDTH = 4
MLSTM_BLOCK = 128
MLSTM_GROUP = 4

ATTN_PATTERNS = ((128, 1), (512, 4), (2048, 16))
HEADS_PER_GROUP = 4
ATTN_DH = 64
ATTN_GW = HEADS_PER_GROUP * ATTN_DH
ATTN_W = len(ATTN_PATTERNS) * ATTN_GW
ATTN_BLOCK = 128
REL_BUCKETS = 32
REL_MAX_DIST = 2048

GMLP_GROUPS = 4
GMLP_GC = 128
GMLP_W = GMLP_GROUPS * GMLP_GC
GMLP_CHUNK = 128

XATTN_HEADS = 4
XATTN_DH = 128
XATTN_W = XATTN_HEADS * XATTN_DH
XATTN_SUB = 1024

N_EXPERTS = 16
N_EXPERT_GROUPS = 4
EXPERTS_PER_GROUP = 4
ROUTER_ROWS = 8 * N_EXPERT_GROUPS

N_BRANCH = 3

MOE_TM = 1024
ROW_CHUNKS = 4
SC_CORES, SC_SUBCORES = 2, 16
SC_WINDOW = 128

OFF_MQ, OFF_MK, OFF_MV, OFF_MO = 0, 512, 1024, 1536
OFF_GU, OFF_GV = 2048, 2560
OFF_GATE = 3072
OFF_IF = 6144
IF_PAD = 256
N_PROJ = OFF_IF + IF_PAD

ATTN_TILE = 2048
ATTN_SUB = ATTN_TILE // ATTN_BLOCK
ATTN_SLAB = 2 * ATTN_DH
ATTN_COLS = HEADS_PER_GROUP * ATTN_SLAB + 2 * ATTN_GW

VMEM_LIMIT = 48 * 1024 * 1024
VMEM_LIMIT_INPROJ = 56 * 1024 * 1024
VMEM_LIMIT_SMALL = 24 * 1024 * 1024
VMEM_LIMIT_MID = 36 * 1024 * 1024

INPROJ_TM, INPROJ_TN = 1024, 3072
ATTNPROJ_SUB = 512
MERGE_TM = 512
XATTN_TM = 1024
COMBINE_TM = 1024
PLAN_TB = 1024


def _cparams(*sem, vmem=VMEM_LIMIT):
    return pltpu.CompilerParams(dimension_semantics=sem, vmem_limit_bytes=vmem)


def _rms(x, gain):
    return x * lax.rsqrt(jnp.mean(x * x, axis=-1, keepdims=True) + EPS) * gain


def _sigmoid(x):
    return 0.5 * jnp.tanh(0.5 * x) + 0.5


def _silu(x):
    half = 0.5 * x
    return half + half * jnp.tanh(half)


def _gelu(x):
    c = math.sqrt(2.0 / math.pi)
    half = 0.5 * x
    return half + half * jnp.tanh(x * (c + (c * 0.044715) * (x * x)))


def _dot(a, b):
    return jnp.dot(a, b, preferred_element_type=F32)


def _dot_nt(a, b):
    return lax.dot_general(a, b, (((1,), (1,)), ((), ())), preferred_element_type=F32)


def _inproj_kernel(x_ref, g_ref, w_ref, wg_ref, o_ref, h_ref, gt_ref):
    @pl.when(pl.program_id(1) == 0)
    def _():
        h = _rms(x_ref[...], g_ref[...]).astype(BF16)
        h_ref[...] = h
        gt_ref[...] = _dot_nt(wg_ref[0, 0:128, :], h)[:gt_ref.shape[0], :]

    o_ref[...] = _dot_nt(h_ref[...], w_ref[0]).astype(o_ref.dtype)


def _inproj(x2d, gain, w, *, layer, tm, tn):
    t, d = x2d.shape
    n = OFF_IF
    return pl.pallas_call(
        _inproj_kernel,
        grid=(t // tm, n // tn),
        in_specs=[pl.BlockSpec((tm, d), lambda i, j: (i, 0)),
                  pl.BlockSpec((1, d), lambda i, j: (0, 0)),
                  pl.BlockSpec((1, tn, d), lambda i, j: (layer, j, 0)),
                  pl.BlockSpec((1, IF_PAD, d), lambda i, j: (layer, OFF_IF // IF_PAD, 0))],
        out_specs=[pl.BlockSpec((tm, tn), lambda i, j: (i, j)),
                   pl.BlockSpec((tm, d), lambda i, j: (i, 0)),
                   pl.BlockSpec((8, tm), lambda i, j: (0, i))],
        out_shape=[jax.ShapeDtypeStruct((t, n), BF16), jax.ShapeDtypeStruct((t, d), BF16),
                   jax.ShapeDtypeStruct((8, t), F32)],
        compiler_params=_cparams("parallel", "arbitrary", vmem=VMEM_LIMIT_INPROJ),
        name="inproj",
    )(x2d, gain, w, w)


def _log_sigmoid(x):
    return jnp.minimum(x, 0.0) - jnp.log(1.0 + jnp.exp(-jnp.abs(x)))


def _split_bf16(x):
    hi = x.astype(BF16)
    return hi, (x - hi.astype(F32)).astype(BF16)


def _prefix_max(x):
    n = x.shape[1]
    lane = lax.broadcasted_iota(jnp.int32, x.shape, 1)
    shift = 1
    while shift < n:
        x = jnp.maximum(x, jnp.where(lane >= shift, pltpu.roll(x, shift, 1), NEG))
        shift *= 2
    return x


def _mlstm_rows_kernel(qk_ref, v_ref, og_ref, *rest, blk, group):
    gate_refs = rest[:group]
    cw_ref, bi_ref, bf_ref, ng_ref, y_ref, xe_scr, s_scr, m_scr = rest[group:]
    heads, dh, w = MLSTM_HEADS, MLSTM_DH, MLSTM_W

    @pl.when(pl.program_id(1) == 0)
    def _():
        xe_scr[:, 0:8, :] = jnp.zeros((group, 8, 2 * w), F32)
        s_scr[...] = jnp.zeros_like(s_scr)
        m_scr[...] = jnp.zeros_like(m_scr)

    cw = cw_ref[...]
    causal = lax.broadcasted_iota(jnp.int32, (blk, blk), 0) >= lax.broadcasted_iota(jnp.int32, (blk, blk), 1)
    triu = (lax.broadcasted_iota(jnp.int32, (blk, blk), 0)
            <= lax.broadcasted_iota(jnp.int32, (blk, blk), 1)).astype(BF16)
    ones = jnp.ones((blk, dh), BF16)
    s_in = [[s_scr[g, h] for h in range(heads)] for g in range(group)]
    m_in = [m_scr[g, :, 0:1] for g in range(group)]
    s_out = [[None] * heads for _ in range(group)]
    m_out = [None] * group
    per_seq = []
    for g in range(group):
        xe_scr[g, 8:8 + blk, :] = qk_ref[g].astype(F32)
        conv = cw[CONV_WIDTH - 1:CONV_WIDTH, :] * xe_scr[g, 8:8 + blk, :]
        for j in range(CONV_WIDTH - 1):
            off = 8 - (CONV_WIDTH - 1) + j
            conv = conv + cw[j:j + 1, :] * xe_scr[g, off:off + blk, :]
        xe_scr[g, 0:8, :] = xe_scr[g, blk:blk + 8, :]
        qk = _silu(conv)

        gates = gate_refs[g][...]
        i_r = gates + bi_ref[...]
        lf_hi, lf_lo = _split_bf16(_log_sigmoid(pltpu.roll(gates, heads, 0) + bf_ref[...]))
        b_r = _dot(lf_hi, triu) + _dot(lf_lo, triu)
        m_st = m_in[g]
        a_r = i_r - b_r
        inter = b_r + m_st
        m_t = jnp.maximum(inter, b_r + _prefix_max(a_r))
        b_last = b_r[:, blk - 1:blk]
        dec = b_last - b_r + i_r
        m_new = jnp.maximum(b_last + m_st, jnp.max(dec, axis=1, keepdims=True))
        w_c = jnp.exp(b_last + m_st - m_new)
        m_out[g] = m_new
        pack = jnp.concatenate([b_r - m_t, jnp.exp(inter - m_t), jnp.exp(-m_t), jnp.exp(dec - m_new),
                                jnp.zeros((blk - 32, blk), F32)], axis=0)
        per_seq.append((qk, a_r, pack.T, w_c))

    chains = [(g, h) for h in range(heads) for g in range(group)]
    st = {}
    for g, h in chains:
        qk = per_seq[g][0]
        sl = slice(h * dh, (h + 1) * dh)
        q_b = qk[:, sl].astype(BF16)
        k_f = qk[:, w + h * dh:w + (h + 1) * dh] * (dh ** -0.5)
        v_ext = jnp.concatenate([v_ref[g, :, sl], ones], axis=-1)
        st[g, h] = (q_b, k_f, v_ext, _dot_nt(q_b, k_f.astype(BF16)), _dot(q_b, s_in[g][h].astype(BF16)))
    for g, h in chains:
        q_b, k_f, v_ext, qk_t, q_state = st[g, h]
        _, a_r, cols, _ = per_seq[g]
        u_c, w_inter = cols[:, h:h + 1], cols[:, 8 + h:9 + h]
        w_intra = jnp.exp(jnp.where(causal, u_c + a_r[h:h + 1, :], NEG))
        st[g, h] = (k_f, v_ext, _dot((qk_t * w_intra).astype(BF16), v_ext) + w_inter * q_state)
    for g, h in chains:
        k_f, v_ext, tot = st[g, h]
        _, _, cols, w_c = per_seq[g]
        em_c, w_k = cols[:, 16 + h:17 + h], cols[:, 24 + h:25 + h]
        sl = slice(h * dh, (h + 1) * dh)
        num, den = tot[:, :dh], tot[:, dh:]
        hh = num / jnp.maximum(jnp.abs(den), em_c)
        hn = _rms(hh, ng_ref[:, sl])
        y_ref[g, :, sl] = (hn * _sigmoid(og_ref[g, :, sl].astype(F32))).astype(y_ref.dtype)
        s_out[g][h] = w_c[h:h + 1, :] * s_in[g][h] + _dot((k_f * w_k).T.astype(BF16), v_ext)
    for g in range(group):
        m_scr[g] = jnp.broadcast_to(m_out[g], m_scr.shape[1:])
        for h in range(heads):
            s_scr[g, h] = s_out[g][h]


def _mlstm_rows(proj, gates_t, conv_w, bias_i, bias_f, norm_g, *, batch, seq, blk, group):
    t, npj = proj.shape
    w = MLSTM_W
    proj3 = proj.reshape(batch, seq, npj)
    cols = lambda c: (lambda b, i: (b, i, c))
    const2 = lambda b, i: (0, 0)
    nblk = seq // blk
    gate_specs = [pl.BlockSpec((8, blk), functools.partial(lambda b, i, g: (0, (b * group + g) * nblk + i), g=g))
                  for g in range(group)]
    y = pl.pallas_call(
        functools.partial(_mlstm_rows_kernel, blk=blk, group=group),
        grid=(batch // group, seq // blk),
        in_specs=[pl.BlockSpec((group, blk, 2 * w), cols(OFF_MQ // (2 * w))),
                  pl.BlockSpec((group, blk, w), cols(OFF_MV // w)),
                  pl.BlockSpec((group, blk, w), cols(OFF_MO // w)),
                  *gate_specs,
                  pl.BlockSpec((CONV_WIDTH, 2 * w), const2),
                  pl.BlockSpec((8, 1), const2), pl.BlockSpec((8, 1), const2),
                  pl.BlockSpec((1, w), const2)],
        out_specs=pl.BlockSpec((group, blk, w), cols(0)),
        out_shape=jax.ShapeDtypeStruct((batch, seq, w), BF16),
        scratch_shapes=[pltpu.VMEM((group, blk + 8, 2 * w), F32),
                        pltpu.VMEM((group, MLSTM_HEADS, MLSTM_DH, 2 * MLSTM_DH), F32),
                        pltpu.VMEM((group, 8, 128), F32)],
        compiler_params=_cparams("parallel", "arbitrary", vmem=VMEM_LIMIT_SMALL),
        name="mlstm",
    )(proj3, proj3, proj3, *([gates_t] * group), conv_w, bias_i, bias_f, norm_g)
    return y.reshape(t, w)


def _attnproj_kernel(h_ref, w_ref, seg_ref, gq_ref, gk_ref, o_ref, r_scr, *, dil):
    gw, half = ATTN_GW, ATTN_SLAB // 2
    sub_rows = ATTNPROJ_SUB
    seg, sub_seg = ATTN_TILE // dil, sub_rows // dil

    def head_norm(x, gain):
        ss = _dot((x * x).astype(BF16), seg_ref[...])
        return x * lax.rsqrt(ss * (1.0 / ATTN_DH) + EPS) * gain

    low = lax.broadcasted_iota(jnp.int32, (1, ATTN_SLAB), 1) < half
    for s in range(ATTN_TILE // sub_rows):
        rows = slice(s * sub_rows, (s + 1) * sub_rows)
        res = _dot_nt(h_ref[rows, :], w_ref[0])
        q = head_norm(res[:, :gw], gq_ref[...]) * (ATTN_DH ** -0.5)
        k = head_norm(res[:, gw:2 * gw], gk_ref[...])
        slabs = []
        for pair in range(gw // ATTN_SLAB):
            qp = q[:, pair * ATTN_SLAB:(pair + 1) * ATTN_SLAB]
            slabs += [jnp.where(low, qp, 0.0), jnp.where(low, 0.0, qp)]
        slabs += [k[:, c * 128:(c + 1) * 128] for c in range(gw // 128)]
        slabs += [res[:, 2 * gw + c * 128:2 * gw + (c + 1) * 128] for c in range(gw // 128)]
        pitch = dil + 1 if dil % 16 == 0 else dil
        for c, slab in enumerate(slabs):
            if dil == 1:
                o_ref[rows, c * 128:(c + 1) * 128] = slab.astype(o_ref.dtype)
            elif pitch == dil:
                r_scr[s % 2, c, 0:sub_rows, :] = slab
            else:
                for i in range(sub_seg):
                    r_scr[s % 2, c, pitch * i:pitch * i + dil, :] = slab[dil * i:dil * (i + 1), :]
        if dil > 1:
            for r in range(dil):
                dst = slice(r * seg + s * sub_seg, r * seg + (s + 1) * sub_seg)
                for c in range(r_scr.shape[1]):
                    o_ref[dst, c * 128:(c + 1) * 128] = (
                        r_scr[s % 2, c, pl.ds(r, sub_seg, stride=pitch), :].astype(o_ref.dtype))


def _attnproj(h, w, seg_ones, gq, gk, *, layer, group, dilation):
    t, d = h.shape
    wcols = 3 * ATTN_GW
    const2 = lambda i: (0, 0)
    return pl.pallas_call(
        functools.partial(_attnproj_kernel, dil=dilation),
        grid=(t // ATTN_TILE,),
        in_specs=[pl.BlockSpec((ATTN_TILE, d), lambda i: (i, 0)),
                  pl.BlockSpec((1, wcols, d), lambda i: (layer, group, 0)),
                  pl.BlockSpec((ATTN_GW, ATTN_GW), const2),
                  pl.BlockSpec((1, ATTN_GW), const2), pl.BlockSpec((1, ATTN_GW), const2)],
        out_specs=pl.BlockSpec((ATTN_TILE, ATTN_COLS), lambda i: (i, 0)),
        out_shape=jax.ShapeDtypeStruct((t, ATTN_COLS), BF16),
        scratch_shapes=[pltpu.VMEM((2, ATTN_COLS // 128, ATTNPROJ_SUB + ATTNPROJ_SUB // 16, 128), F32)],
        compiler_params=_cparams("parallel", vmem=VMEM_LIMIT_MID),
        name=f"attnproj{group}",
    )(h, w, seg_ones, gq, gk)


def _dattn_kernel(q_ref, kc_ref, kp_ref, vc_ref, vp_ref, b0_ref, o_ref, lse_ref,
                  kx_scr, vx_scr, o_scr, l_scr, bias_scr, *, dil):
    blk = ATTN_BLOCK
    per = ATTN_SUB // dil
    pitch = dil + 1 if dil % 16 == 0 else dil
    first_tile = pl.program_id(1) == 0

    @pl.when(first_tile)
    def _():
        for h in range(HEADS_PER_GROUP):
            bias_scr[h] = pltpu.roll(jnp.broadcast_to(b0_ref[h], (blk, 2 * blk)), 0, 1, stride=1, stride_axis=0)

    for r in range(dil):
        base = r * (per + 1) * blk
        last = slice((r * per + per - 1) * blk, (r * per + per) * blk)
        mine = slice(r * per * blk, (r + 1) * per * blk)
        kx_scr[base:base + blk, :] = kp_ref[last, :]
        vx_scr[base:base + blk, :] = vp_ref[last, :]
        kx_scr[base + blk:base + (per + 1) * blk, :] = kc_ref[mine, :]
        vx_scr[base + blk:base + (per + 1) * blk, :] = vc_ref[mine, :]

    low = lax.broadcasted_iota(jnp.int32, (1, ATTN_SLAB), 1) < ATTN_SLAB // 2
    no_prev = lax.broadcasted_iota(jnp.int32, (1, 2 * blk), 1) < blk
    for r in range(dil):
        for sub in range(per):
            u = r * per + sub
            win = slice((r * (per + 1) + sub) * blk, (r * (per + 1) + sub + 2) * blk)
            o_slabs, l_slabs = [], []
            for pair in range(ATTN_GW // ATTN_SLAB):
                cols = slice(pair * ATTN_SLAB, (pair + 1) * ATTN_SLAB)
                kx, vx = kx_scr[win, cols], vx_scr[win, cols]
                o_pair, l_pair = [], []
                for h in (2 * pair, 2 * pair + 1):
                    logits = _dot_nt(q_ref[u * blk:(u + 1) * blk, h * ATTN_SLAB:(h + 1) * ATTN_SLAB], kx)
                    logits = logits + bias_scr[h]
                    if sub == 0:
                        logits = jnp.where(first_tile & no_prev, NEG, logits)
                    m = jnp.max(logits, axis=-1, keepdims=True)
                    p = jnp.exp(logits - m)
                    l = jnp.sum(p, axis=-1, keepdims=True)
                    o_pair.append(_dot(p.astype(BF16), vx) / l)
                    l_pair.append(m + jnp.log(l))
                o_slabs.append(jnp.where(low, o_pair[0], o_pair[1]))
                l_slabs.append(jnp.where(low, l_pair[0], l_pair[1]))
            for c in range(ATTN_GW // ATTN_SLAB):
                cols = slice(c * ATTN_SLAB, (c + 1) * ATTN_SLAB)
                if dil == 1:
                    o_ref[u * blk:(u + 1) * blk, cols] = o_slabs[c].astype(o_ref.dtype)
                    lse_ref[u * blk:(u + 1) * blk, cols] = l_slabs[c]
                else:
                    dst = pl.ds(sub * blk * pitch + r, blk, stride=pitch)
                    o_scr[c, dst, :] = o_slabs[c]
                    l_scr[c, dst, :] = l_slabs[c]
    if dil > 1:
        for c in range(ATTN_GW // ATTN_SLAB):
            cols = slice(c * ATTN_SLAB, (c + 1) * ATTN_SLAB)
            if pitch == dil:
                o_ref[:, cols] = o_scr[c, 0:ATTN_TILE, :].astype(o_ref.dtype)
                lse_ref[:, cols] = l_scr[c, 0:ATTN_TILE, :]
            else:
                for i in range(ATTN_TILE // dil):
                    o_ref[dil * i:dil * (i + 1), cols] = o_scr[c, pitch * i:pitch * i + dil, :].astype(o_ref.dtype)
                    lse_ref[dil * i:dil * (i + 1), cols] = l_scr[c, pitch * i:pitch * i + dil, :]


def _dattn(aproj, bias, *, seq, group, dilation):
    t = aproj.shape[0]
    tiles = seq // ATTN_TILE
    qw = HEADS_PER_GROUP * ATTN_SLAB
    cq, ck, cv = 0, qw // ATTN_GW, qw // ATTN_GW + 1
    blk = (ATTN_TILE, ATTN_GW)
    cur = lambda c: (lambda b, j: (b * tiles + j, c))
    prev = lambda c: (lambda b, j: (b * tiles + jnp.maximum(j - 1, 0), c))
    xrows = ATTN_TILE + dilation * ATTN_BLOCK
    return pl.pallas_call(
        functools.partial(_dattn_kernel, dil=dilation),
        grid=(t // seq, tiles),
        in_specs=[pl.BlockSpec((ATTN_TILE, qw), cur(cq)),
                  pl.BlockSpec(blk, cur(ck)), pl.BlockSpec(blk, prev(ck)),
                  pl.BlockSpec(blk, cur(cv)), pl.BlockSpec(blk, prev(cv)),
                  pl.BlockSpec((HEADS_PER_GROUP, 1, 2 * ATTN_BLOCK), lambda b, j: (0, 0, 0))],
        out_specs=[pl.BlockSpec(blk, cur(0)), pl.BlockSpec(blk, cur(0))],
        out_shape=[jax.ShapeDtypeStruct((t, ATTN_GW), BF16), jax.ShapeDtypeStruct((t, ATTN_GW), F32)],
        scratch_shapes=[pltpu.VMEM((xrows, ATTN_GW), BF16), pltpu.VMEM((xrows, ATTN_GW), BF16),
                        pltpu.VMEM((ATTN_GW // ATTN_SLAB, ATTN_TILE + ATTN_TILE // 16, ATTN_SLAB), F32),
                        pltpu.VMEM((ATTN_GW // ATTN_SLAB, ATTN_TILE + ATTN_TILE // 16, ATTN_SLAB), F32),
                        pltpu.VMEM((HEADS_PER_GROUP, ATTN_BLOCK, 2 * ATTN_BLOCK), F32)],
        compiler_params=_cparams("parallel", "arbitrary", vmem=VMEM_LIMIT_MID),
        name=f"dattn{group}",
    )(aproj, aproj, aproj, aproj, aproj, bias)


def _rel_bucket(n):
    max_exact = REL_BUCKETS // 2
    nf = jnp.maximum(n, 1).astype(F32)
    log_b = max_exact + (jnp.log(nf / max_exact) / math.log(REL_MAX_DIST / max_exact)
                         * (REL_BUCKETS - max_exact)).astype(jnp.int32)
    return jnp.where(n < max_exact, n, jnp.minimum(log_b, REL_BUCKETS - 1))


def _attn_bias(rel_bias, group):
    window, dilation = ATTN_PATTERNS[group]
    steps = window // dilation
    assert steps == ATTN_BLOCK
    hs = slice(group * HEADS_PER_GROUP, (group + 1) * HEADS_PER_GROUP)
    bucket = _rel_bucket((steps - jnp.arange(steps + 1)) * dilation)
    by_dist = jnp.dot(jax.nn.one_hot(bucket, REL_BUCKETS, dtype=F32), rel_bias[:, hs].astype(F32),
                      precision=lax.Precision.HIGHEST)
    row0 = jnp.concatenate([by_dist, jnp.full((2 * ATTN_BLOCK - steps - 1, HEADS_PER_GROUP), NEG, F32)], axis=0)
    return row0.T[:, None, :]


def _merge_kernel(ya_ref, yb0_ref, yb1_ref, yb2_ref, l0_ref, l1_ref, l2_ref, gu_ref, gv_ref, gate_ref,
                  x_ref, wa_ref, wb_ref, wc_ref, wo_ref, ws_ref, bs_ref, gg_ref, o_ref, yc_scr, *, tm):
    d = x_ref.shape[1]
    l0, l1, l2 = l0_ref[...], l1_ref[...], l2_ref[...]
    mx = jnp.maximum(jnp.maximum(l0, l1), l2)
    e0, e1, e2 = jnp.exp(l0 - mx), jnp.exp(l1 - mx), jnp.exp(l2 - mx)
    inv = 1.0 / (e0 + e1 + e2)
    yb = jnp.concatenate([(yb0_ref[...].astype(F32) * (e0 * inv)).astype(BF16),
                          (yb1_ref[...].astype(F32) * (e1 * inv)).astype(BF16),
                          (yb2_ref[...].astype(F32) * (e2 * inv)).astype(BF16)], axis=-1)

    for j in range(tm // GMLP_CHUNK):
        rows = slice(j * GMLP_CHUNK, (j + 1) * GMLP_CHUNK)
        for g in range(GMLP_GROUPS):
            cols = slice(g * GMLP_GC, (g + 1) * GMLP_GC)
            u = _gelu(gu_ref[rows, cols].astype(F32))
            v = _rms(_gelu(gv_ref[rows, cols].astype(F32)), gg_ref[:, cols])
            mixed = _dot(ws_ref[g], v.astype(BF16)) + bs_ref[g]
            yc_scr[rows, cols] = (u * mixed).astype(BF16)

    def gate2(k):
        return jnp.tanh(0.5 * gate_ref[:, k * d:(k + 1) * d].astype(F32)) + 1.0

    merged2 = gate2(0) * _dot(ya_ref[...], wa_ref[...])
    merged2 = merged2 + gate2(1) * _dot(yb, wb_ref[...])
    merged2 = merged2 + gate2(2) * _dot(yc_scr[...], wc_ref[...])
    o_ref[...] = x_ref[...] + 0.5 * _dot(merged2.astype(BF16), wo_ref[...])


def _merge(ya, ybs, lses, proj, x2d, wa, wb, wc, wo, ws, bsb, gg, *, tm):
    t, d = x2d.shape
    row = lambda c: (lambda i: (i, c))
    full2 = lambda i: (0, 0)
    full3 = lambda i: (0, 0, 0)
    gspec = pl.BlockSpec((tm, ATTN_GW), row(0))
    return pl.pallas_call(
        functools.partial(_merge_kernel, tm=tm),
        grid=(t // tm,),
        in_specs=[pl.BlockSpec((tm, MLSTM_W), row(0)),
                  gspec, gspec, gspec, gspec, gspec, gspec,
                  pl.BlockSpec((tm, GMLP_W), row(OFF_GU // GMLP_W)),
                  pl.BlockSpec((tm, GMLP_W), row(OFF_GV // GMLP_W)),
                  pl.BlockSpec((tm, N_BRANCH * d), row(OFF_GATE // (N_BRANCH * d))),
                  pl.BlockSpec((tm, d), row(0)),
                  pl.BlockSpec(wa.shape, full2), pl.BlockSpec(wb.shape, full2),
                  pl.BlockSpec(wc.shape, full2), pl.BlockSpec(wo.shape, full2),
                  pl.BlockSpec(ws.shape, full3), pl.BlockSpec(bsb.shape, full3),
                  pl.BlockSpec(gg.shape, full2)],
        out_specs=pl.BlockSpec((tm, d), row(0)),
        out_shape=jax.ShapeDtypeStruct((t, d), F32),
        scratch_shapes=[pltpu.VMEM((tm, GMLP_W), BF16)],
        compiler_params=_cparams("parallel"),
        name="merge",
    )(ya, *ybs, *lses, proj, proj, proj, x2d, wa, wb, wc, wo, ws, bsb, gg)


def _memkv_kernel(mem_ref, g_ref, w_ref, gk_ref, k_ref, v_ref):
    dh, w = XATTN_DH, XATTN_W
    kv = _dot(_rms(mem_ref[0], g_ref[...]).astype(BF16), w_ref[...])
    for h in range(XATTN_HEADS):
        sl = slice(h * dh, (h + 1) * dh)
        k_ref[0, :, sl] = _rms(kv[:, sl], gk_ref[...]).astype(k_ref.dtype)
    v_ref[0] = kv[:, w:].astype(v_ref.dtype)


def _memkv(mem, gain, w_kv, gk):
    b, m, d = mem.shape
    full2 = lambda i: (0, 0)
    return pl.pallas_call(
        _memkv_kernel,
        grid=(b,),
        in_specs=[pl.BlockSpec((1, m, d), lambda i: (i, 0, 0)),
                  pl.BlockSpec((1, d), full2),
                  pl.BlockSpec(w_kv.shape, full2),
                  pl.BlockSpec((1, XATTN_DH), full2)],
        out_specs=[pl.BlockSpec((1, m, XATTN_W), lambda i: (i, 0, 0)),
                   pl.BlockSpec((1, m, XATTN_W), lambda i: (i, 0, 0))],
        out_shape=[jax.ShapeDtypeStruct((b, m, XATTN_W), BF16),
                   jax.ShapeDtypeStruct((b, m, XATTN_W), BF16)],
        compiler_params=_cparams("parallel", vmem=VMEM_LIMIT_SMALL),
        name="memkv",
    )(mem, gain, w_kv, gk)


def _route(logits):
    tm = logits.shape[1]
    e = jnp.exp(logits - jnp.max(logits, axis=0, keepdims=True))
    probs = e / jnp.sum(e, axis=0, keepdims=True)
    rowi = lax.broadcasted_iota(jnp.int32, (8, tm), 0)
    real = rowi < EXPERTS_PER_GROUP
    tops = []
    for g in range(N_EXPERT_GROUPS):
        pg = jnp.where(real, probs[8 * g:8 * g + 8, :], -0.5)
        m1 = jnp.max(pg, axis=0, keepdims=True)
        i1 = jnp.min(jnp.where(pg == m1, rowi, 8), axis=0, keepdims=True)
        pg2 = jnp.where(rowi == i1, -1.0, pg)
        m2 = jnp.max(pg2, axis=0, keepdims=True)
        i2 = jnp.min(jnp.where(pg2 == m2, rowi, 8), axis=0, keepdims=True)
        tops.append((m1, i1, m2, i2))
    best = jnp.zeros((1, tm), jnp.int32)
    best_score = tops[0][0] + tops[0][2]
    for g in range(1, N_EXPERT_GROUPS):
        score = tops[g][0] + tops[g][2]
        better = score > best_score
        best = jnp.where(better, g, best)
        best_score = jnp.where(better, score, best_score)
    m1, i1, m2, i2 = tops[0]
    for g in range(1, N_EXPERT_GROUPS):
        m1, i1, m2, i2 = (jnp.where(best == g, new, old) for new, old in zip(tops[g], (m1, i1, m2, i2)))
    tot = m1 + m2
    base = best * EXPERTS_PER_GROUP
    return base + i1, base + i2, m1 / tot, m2 / tot


def _pack_bf16_pairs(x):
    n = x.shape[1] // 2
    hi = lax.bitcast_convert_type(x[:, :n].astype(BF16).astype(F32), jnp.uint32)
    lo = lax.bitcast_convert_type(x[:, n:].astype(BF16).astype(F32), jnp.uint32)
    return hi | (lo >> 16)


def _unpack_bf16_pairs(p):
    hi = lax.bitcast_convert_type(p & jnp.uint32(0xFFFF0000), F32)
    lo = lax.bitcast_convert_type(p << 16, F32)
    return hi, lo


def _store_row_chunks(ref, packed):
    for j in range(ROW_CHUNKS):
        ref[j] = packed[:, j * 128:(j + 1) * 128]


def _load_row_chunks(ref):
    return jnp.concatenate([ref[j] for j in range(ROW_CHUNKS)], axis=-1)


def _xattn_kernel(x_ref, k_ref, v_ref, gx_ref, wq_ref, gq_ref, wo_ref, gf_ref, rw_ref, rb_ref,
                  xo_ref, hf_ref, eidx_ref, wts_ref, *, sub):
    dh = XATTN_DH
    rw = rw_ref[...]
    rw_hi, rw_lo = _split_bf16(rw)
    for s in range(x_ref.shape[0] // sub):
        rows = slice(s * sub, (s + 1) * sub)
        x = x_ref[rows, :]
        q = _dot(_rms(x, gx_ref[...]).astype(BF16), wq_ref[...])
        outs = []
        for h in range(XATTN_HEADS):
            sl = slice(h * dh, (h + 1) * dh)
            q_h = (_rms(q[:, sl], gq_ref[...]) * (dh ** -0.5)).astype(BF16)
            logits = _dot_nt(q_h, k_ref[0, :, sl])
            p = jnp.exp(logits - jnp.max(logits, axis=-1, keepdims=True))
            o = _dot(p.astype(BF16), v_ref[0, :, sl]) / jnp.sum(p, axis=-1, keepdims=True)
            outs.append(o.astype(BF16))
        xn = x + _dot(jnp.concatenate(outs, axis=-1), wo_ref[...])
        xo_ref[rows, :] = xn
        hf = _rms(xn, gf_ref[...])
        packed = _pack_bf16_pairs(hf)
        for j in range(ROW_CHUNKS):
            hf_ref[j, rows, :] = packed[:, j * 128:(j + 1) * 128]
        hf_hi, hf_lo = _split_bf16(hf)
        logits_t = _dot_nt(rw_hi, hf_hi) + _dot_nt(rw_hi, hf_lo) + _dot_nt(rw_lo, hf_hi) + rb_ref[...]
        e1, e2, w1, w2 = _route(logits_t)
        eidx_ref[:, rows] = jnp.concatenate([e1, e2, jnp.zeros((6, sub), jnp.int32)], axis=0)
        wts_ref[:, rows] = jnp.concatenate([w1, w2, jnp.zeros((6, sub), F32)], axis=0)


def _xattn(x2d, k, v, gx, wq, gq, wo, gf, rw_t, rb, *, seq, tm):
    t, d = x2d.shape
    per_b = seq // tm
    full2 = lambda i: (0, 0)
    kv_spec = pl.BlockSpec((1,) + k.shape[1:], lambda i: (i // per_b, 0, 0))
    return pl.pallas_call(
        functools.partial(_xattn_kernel, sub=min(tm, XATTN_SUB)),
        grid=(t // tm,),
        in_specs=[pl.BlockSpec((tm, d), lambda i: (i, 0)), kv_spec, kv_spec,
                  pl.BlockSpec((1, d), full2), pl.BlockSpec(wq.shape, full2),
                  pl.BlockSpec((1, XATTN_DH), full2), pl.BlockSpec(wo.shape, full2),
                  pl.BlockSpec((1, d), full2), pl.BlockSpec(rw_t.shape, full2),
                  pl.BlockSpec(rb.shape, full2)],
        out_specs=[pl.BlockSpec((tm, d), lambda i: (i, 0)),
                   pl.BlockSpec((ROW_CHUNKS, tm, 128), lambda i: (0, i, 0)),
                   pl.BlockSpec((8, tm), lambda i: (0, i)),
                   pl.BlockSpec((8, tm), lambda i: (0, i))],
        out_shape=[jax.ShapeDtypeStruct((t, d), F32),
                   jax.ShapeDtypeStruct((ROW_CHUNKS, t, 128), jnp.uint32),
                   jax.ShapeDtypeStruct((8, t), jnp.int32),
                   jax.ShapeDtypeStruct((8, t), F32)],
        compiler_params=_cparams("parallel"),
        name="xattn_router",
    )(x2d, k, v, gx, wq, gq, wo, gf, rw_t, rb)


def _moe_plan_kernel(eidx_ref, i1_ref, i2_ref, te_ref, na_ref, cnt_scr, carry_scr, *, tb, tm, plane_rows):
    ne = N_EXPERTS
    hp = lax.Precision.HIGHEST
    phase, j = pl.program_id(0), pl.program_id(1)
    rows = lax.broadcasted_iota(jnp.int32, (ne, tb), 0)
    oh1 = rows == eidx_ref[0:1, :]
    oh2 = rows == eidx_ref[1:2, :]
    a = oh1.astype(F32) + oh2.astype(F32)
    blk_cnt = jnp.broadcast_to(jnp.sum(a, axis=1, keepdims=True), cnt_scr.shape)

    @pl.when((phase == 0) & (j == 0))
    def _():
        cnt_scr[...] = jnp.zeros_like(cnt_scr)

    @pl.when(phase == 0)
    def _():
        cnt_scr[...] += blk_cnt

    @pl.when((phase == 1) & (j == 0))
    def _():
        padded = jnp.ceil(cnt_scr[...] * (1.0 / tm)) * tm
        er = lax.broadcasted_iota(jnp.int32, (ne, ne), 0)
        ec = lax.broadcasted_iota(jnp.int32, (ne, ne), 1)
        off = jnp.dot((ec < er).astype(F32), padded, precision=hp, preferred_element_type=F32)
        carry_scr[...] = off
        seg_end = (off + padded)[:, 0:1]
        tile_start = lax.broadcasted_iota(jnp.int32, (ne, te_ref.shape[1]), 1).astype(F32) * tm
        te = jnp.sum((seg_end <= tile_start).astype(F32), axis=0, keepdims=True)
        te_ref[...] = jnp.broadcast_to(jnp.minimum(te, ne - 1.0), te_ref.shape).astype(jnp.int32)
        total = jnp.sum(padded[:, 0:1], axis=0, keepdims=True)
        na_ref[...] = jnp.broadcast_to(total * (1.0 / tm), na_ref.shape).astype(jnp.int32)

    @pl.when(phase == 1)
    def _():
        before = (lax.broadcasted_iota(jnp.int32, (tb, tb), 0)
                  < lax.broadcasted_iota(jnp.int32, (tb, tb), 1)).astype(BF16)
        rank = carry_scr[:, 0:1] + _dot(a.astype(BF16), before)
        d1 = jnp.sum(jnp.where(oh1, rank, 0.0), axis=0, keepdims=True).astype(jnp.int32)
        d2 = jnp.sum(jnp.where(oh2, rank, 0.0), axis=0, keepdims=True).astype(jnp.int32)
        plane = lax.broadcasted_iota(jnp.int32, (8, tb), 0) * plane_rows
        i1_ref[...] = jnp.where(plane < ROW_CHUNKS * plane_rows, plane + d1, 0)
        i2_ref[...] = jnp.where(plane < ROW_CHUNKS * plane_rows, plane + d2, 0)
        carry_scr[...] += blk_cnt


def _moe_plan(eidx, *, tm, n_tiles, tb=PLAN_TB):
    t = eidx.shape[1]
    ntp = -(-n_tiles // 128) * 128
    return pl.pallas_call(
        functools.partial(_moe_plan_kernel, tb=tb, tm=tm, plane_rows=n_tiles * tm),
        grid=(2, t // tb),
        in_specs=[pl.BlockSpec((8, tb), lambda p, j: (0, j))],
        out_specs=[pl.BlockSpec((8, tb), lambda p, j: (0, j * p)),
                   pl.BlockSpec((8, tb), lambda p, j: (0, j * p)),
                   pl.BlockSpec((8, ntp), lambda p, j: (0, 0)),
                   pl.BlockSpec((8, 128), lambda p, j: (0, 0))],
        out_shape=[jax.ShapeDtypeStruct((8, t), jnp.int32),
                   jax.ShapeDtypeStruct((8, t), jnp.int32),
                   jax.ShapeDtypeStruct((8, ntp), jnp.int32),
                   jax.ShapeDtypeStruct((8, 128), jnp.int32)],
        scratch_shapes=[pltpu.VMEM((N_EXPERTS, 128), F32), pltpu.VMEM((N_EXPERTS, 128), F32)],
        compiler_params=_cparams("arbitrary", "arbitrary", vmem=VMEM_LIMIT_SMALL),
        name="moe_plan",
    )(eidx)


def _sc_mesh():
    return plsc.VectorSubcoreMesh(core_axis_name="c", subcore_axis_name="s",
                                  num_cores=SC_CORES, num_subcores=SC_SUBCORES)


def _sc_index_spec(tokens):
    nb = tokens // SC_WINDOW
    return pl.BlockSpec((1, SC_WINDOW), lambda i: (i // nb, i % nb))


def _sc_dispatch(rows, i1, i2, n_out):
    n = rows.shape[0]
    tokens = i1.shape[1]

    @functools.partial(pl.kernel, out_type=jax.ShapeDtypeStruct((n_out, 128), rows.dtype), mesh=_sc_mesh(),
                       name="moe_dispatch")
    def k(x_hbm, i1_hbm, i2_hbm, o_hbm):
        def body(x_vmem, i1_vmem, i2_vmem):
            pltpu.sync_copy(x_vmem, o_hbm.at[i1_vmem.at[0]])
            pltpu.sync_copy(x_vmem, o_hbm.at[i2_vmem.at[0]])

        pltpu.emit_pipeline(
            body, grid=(n // SC_WINDOW,),
            in_specs=[pl.BlockSpec((SC_WINDOW, 128), lambda i: (i, 0)),
                      _sc_index_spec(tokens), _sc_index_spec(tokens)],
            out_specs=[],
            core_axis_name=("c", "s"), dimension_semantics=(pltpu.PARALLEL,),
        )(x_hbm, i1_hbm, i2_hbm)

    return k(rows, i1, i2)


def _sc_collect(table, i1, i2):
    tokens = i1.shape[1]
    n = ROW_CHUNKS * tokens
    out = jax.ShapeDtypeStruct((n, 128), table.dtype)

    @functools.partial(pl.kernel, out_type=(out, out), mesh=_sc_mesh(), name="moe_collect",
                       scratch_types=[pltpu.SemaphoreType.DMA, pltpu.SemaphoreType.DMA])
    def k(t_hbm, i1_hbm, i2_hbm, o1_hbm, o2_hbm, sem1, sem2):
        def body(i1_vmem, i2_vmem, o1_vmem, o2_vmem):
            first = pltpu.async_copy(t_hbm.at[i1_vmem.at[0]], o1_vmem, sem1)
            second = pltpu.async_copy(t_hbm.at[i2_vmem.at[0]], o2_vmem, sem2)
            first.wait()
            second.wait()

        pltpu.emit_pipeline(
            body, grid=(n // SC_WINDOW,),
            in_specs=[_sc_index_spec(tokens), _sc_index_spec(tokens)],
            out_specs=[pl.BlockSpec((SC_WINDOW, 128), lambda i: (i, 0)),
                       pl.BlockSpec((SC_WINDOW, 128), lambda i: (i, 0))],
            core_axis_name=("c", "s"), dimension_semantics=(pltpu.PARALLEL,),
        )(i1_hbm, i2_hbm, o1_hbm, o2_hbm)

    return k(table, i1, i2)


def _experts_kernel(te_ref, na_ref, xs_ref, wg_ref, wu_ref, wd_ref, y_ref, wg_scr, wu_scr, wd_scr):
    i = pl.program_id(0)
    active = i < na_ref[0]

    @pl.when(active & ((i == 0) | (te_ref[i] != te_ref[jnp.maximum(i - 1, 0)])))
    def _():
        wg_scr[...] = wg_ref[0, 0].astype(BF16)
        wu_scr[...] = wu_ref[0, 0].astype(BF16)
        wd_scr[...] = wd_ref[0, 0].astype(BF16)

    @pl.when(active)
    def _():
        hi, lo = _unpack_bf16_pairs(_load_row_chunks(xs_ref))
        h = jnp.concatenate([hi, lo], axis=-1).astype(BF16)
        up = _dot(h, wg_scr[...])
        act = _silu(up) * _dot(h, wu_scr[...])
        _store_row_chunks(y_ref, _pack_bf16_pairs(_dot(act.astype(BF16), wd_scr[...])))


def _experts(tile_expert, n_active, xs, wg, wu, wd, *, layer, tm):
    n_tiles = tile_expert.shape[0]
    _, _, d, dff = wg.shape
    rows = lambda i, te, na: (0, jnp.minimum(i, na[0] - 1), 0)
    expert = lambda i, te, na: (layer, te[i], 0, 0)
    return pl.pallas_call(
        _experts_kernel,
        grid_spec=pltpu.PrefetchScalarGridSpec(
            num_scalar_prefetch=2,
            grid=(n_tiles,),
            in_specs=[pl.BlockSpec((ROW_CHUNKS, tm, 128), rows),
                      pl.BlockSpec((1, 1, d, dff), expert),
                      pl.BlockSpec((1, 1, d, dff), expert),
                      pl.BlockSpec((1, 1, dff, d), expert)],
            out_specs=pl.BlockSpec((ROW_CHUNKS, tm, 128), rows),
            scratch_shapes=[pltpu.VMEM((d, dff), BF16), pltpu.VMEM((d, dff), BF16), pltpu.VMEM((dff, d), BF16)]),
        out_shape=jax.ShapeDtypeStruct(xs.shape, xs.dtype),
        compiler_params=_cparams("arbitrary"),
        name="moe_experts",
    )(tile_expert, n_active, xs, wg, wu, wd)


def _moe_combine_kernel(x_ref, y1_ref, y2_ref, w_ref, o_ref):
    half = x_ref.shape[1] // 2
    hi1, lo1 = _unpack_bf16_pairs(_load_row_chunks(y1_ref))
    hi2, lo2 = _unpack_bf16_pairs(_load_row_chunks(y2_ref))
    tm = x_ref.shape[0]
    w_cols = jnp.concatenate([w_ref[...], jnp.zeros((128 - w_ref.shape[0], tm), F32)], axis=0).T
    w1, w2 = w_cols[:, 0:1], w_cols[:, 1:2]
    o_ref[:, :half] = x_ref[:, :half] + w1 * hi1 + w2 * hi2
    o_ref[:, half:] = x_ref[:, half:] + w1 * lo1 + w2 * lo2


def _moe_combine(x2d, y1, y2, wts, *, tm):
    t, d = x2d.shape
    chunk_spec = pl.BlockSpec((ROW_CHUNKS, tm, 128), lambda i: (0, i, 0))
    return pl.pallas_call(
        _moe_combine_kernel,
        grid=(t // tm,),
        in_specs=[pl.BlockSpec((tm, d), lambda i: (i, 0)), chunk_spec, chunk_spec,
                  pl.BlockSpec((wts.shape[0], tm), lambda i: (0, i))],
        out_specs=pl.BlockSpec((tm, d), lambda i: (i, 0)),
        out_shape=jax.ShapeDtypeStruct((t, d), F32),
        compiler_params=_cparams("parallel", vmem=VMEM_LIMIT_MID),
        name="moe_combine",
    )(x2d, y1, y2, wts)


def _moe(x2d, hf_rows, eidx, wts, wg, wu, wd, *, layer):
    t = x2d.shape[0]
    tm = MOE_TM
    n_tiles = 2 * t // tm + N_EXPERTS
    plane = n_tiles * tm
    i1, i2, te, na = _moe_plan(eidx, tm=tm, n_tiles=n_tiles)
    xs = _sc_dispatch(hf_rows.reshape(ROW_CHUNKS * t, 128), i1, i2, ROW_CHUNKS * plane)
    ys = _experts(te[0, :n_tiles], na[0, :1], xs.reshape(ROW_CHUNKS, plane, 128), wg, wu, wd,
                  layer=layer, tm=tm)
    y1, y2 = _sc_collect(ys.reshape(ROW_CHUNKS * plane, 128), i1, i2)
    return _moe_combine(x2d, y1.reshape(ROW_CHUNKS, t, 128), y2.reshape(ROW_CHUNKS, t, 128), wts,
                        tm=COMBINE_TM)


W_ROWS = 256


def _w_rows_kernel(start_ref, valid_ref, w_ref, o_ref):
    del start_ref
    row = lax.broadcasted_iota(jnp.int32, w_ref.shape, 1)
    o_ref[...] = jnp.where(row < valid_ref[pl.program_id(0)], w_ref[...], 0.0).astype(o_ref.dtype)


def _w_rows(w_t, starts, valid):
    depth, _, d = w_t.shape
    nblk = len(starts)
    return pl.pallas_call(
        _w_rows_kernel,
        grid_spec=pltpu.PrefetchScalarGridSpec(
            num_scalar_prefetch=2,
            grid=(nblk,),
            in_specs=[pl.BlockSpec((pl.Element(depth), pl.Element(W_ROWS), pl.Element(d)),
                                   lambda c, st, va: (0, pl.multiple_of(st[c], 8), 0))],
            out_specs=pl.BlockSpec((depth, W_ROWS, d), lambda c, st, va: (0, c, 0))),
        out_shape=jax.ShapeDtypeStruct((depth, nblk * W_ROWS, d), BF16),
        compiler_params=_cparams("arbitrary", vmem=VMEM_LIMIT_SMALL),
        name="w_in_rows",
    )(jnp.asarray(starts, jnp.int32), jnp.asarray(valid, jnp.int32), w_t)


def _w_in_layout(w_in):
    w_t = jnp.swapaxes(w_in, 1, 2)
    src_if = 4 * MLSTM_W
    src_a = src_if + 2 * MLSTM_HEADS
    src_g = src_a + 3 * ATTN_W
    starts = list(range(0, src_if, W_ROWS)) + [src_g + k * W_ROWS for k in range((OFF_IF - OFF_GU) // W_ROWS)]
    valid = [W_ROWS] * len(starts)
    starts.append(src_if)
    valid.append(2 * MLSTM_HEADS)
    assert len(starts) * W_ROWS == N_PROJ and ATTN_GW == W_ROWS
    a_starts = [src_a + j * ATTN_W + g * ATTN_GW for g in range(len(ATTN_PATTERNS)) for j in range(3)]
    return _w_rows(w_t, starts, valid), _w_rows(w_t, a_starts, [W_ROWS] * len(a_starts))


def kernel(x, mem, norm_mix, w_in, mlstm_conv, mlstm_gate_b, mlstm_norm, attn_qk_norm, gmlp_norm, gmlp_ws,
           gmlp_bs, w_branch_a, w_branch_b, w_branch_c, w_out, rel_bias, norm_xattn, norm_mem, w_xq, w_xkv,
           xattn_qk_norm, w_xo, norm_ffn, router_w, router_b, w_expert_gate, w_expert_up, w_expert_down):
    b, s, d = x.shape
    t = b * s
    depth = w_in.shape[0]
    assert d == 2 * ROW_CHUNKS * 128 and OFF_IF == OFF_GATE + N_BRANCH * d
    assert s % ATTN_TILE == 0 and s % MLSTM_BLOCK == 0 and b % MLSTM_GROUP == 0
    assert all(window == dil * ATTN_BLOCK and ATTN_SUB % dil == 0 for window, dil in ATTN_PATTERNS)
    assert t % max(INPROJ_TM, MERGE_TM, XATTN_TM, COMBINE_TM, MOE_TM, PLAN_TB) == 0 and s % XATTN_TM == 0
    x2d = x.reshape(t, d)

    biases = [_attn_bias(rel_bias, g) for g in range(len(ATTN_PATTERNS))]
    rw_t = jnp.zeros((N_EXPERT_GROUPS, 8, d), F32).at[:, :EXPERTS_PER_GROUP].set(
        router_w.T.reshape(N_EXPERT_GROUPS, EXPERTS_PER_GROUP, d)).reshape(ROUTER_ROWS, d)
    rb = jnp.full((N_EXPERT_GROUPS, 8), NEG, F32).at[:, :EXPERTS_PER_GROUP].set(
        router_b.astype(F32).reshape(N_EXPERT_GROUPS, EXPERTS_PER_GROUP)).reshape(ROUTER_ROWS, 1)
    tril = jnp.tril(jnp.ones((GMLP_CHUNK, GMLP_CHUNK), bool))
    head_of = jnp.arange(ATTN_GW) // ATTN_DH
    seg_ones = (head_of[:, None] == head_of[None, :]).astype(BF16)

    w_main, w_attn = _w_in_layout(w_in)

    for l in range(depth):
        proj, h_mix, gates_t = _inproj(x2d, norm_mix[l][None], w_main, layer=l, tm=INPROJ_TM,
                                       tn=INPROJ_TN)
        gq = jnp.tile(attn_qk_norm[l, 0], HEADS_PER_GROUP)[None]
        gk = jnp.tile(attn_qk_norm[l, 1], HEADS_PER_GROUP)[None]

        nh = MLSTM_HEADS
        bias_i = jnp.zeros((8, 1), F32).at[:nh, 0].set(mlstm_gate_b[l, :nh])
        bias_f = jnp.zeros((8, 1), F32).at[:nh, 0].set(mlstm_gate_b[l, nh:])
        ya = _mlstm_rows(proj, gates_t, mlstm_conv[l], bias_i, bias_f, mlstm_norm[l][None],
                         batch=b, seq=s, blk=MLSTM_BLOCK, group=MLSTM_GROUP)

        ybs, lses = [], []
        for g, (_, dilation) in enumerate(ATTN_PATTERNS):
            aproj = _attnproj(h_mix, w_attn, seg_ones, gq, gk, layer=l, group=g, dilation=dilation)
            o, lse = _dattn(aproj, biases[g], seq=s, group=g, dilation=dilation)
            ybs.append(o)
            lses.append(lse)

        ws = jnp.where(tril, gmlp_ws[l], 0.0).astype(BF16)
        bsb = jnp.broadcast_to(gmlp_bs[l][:, :, None], (GMLP_GROUPS, GMLP_CHUNK, GMLP_GC)).astype(F32)
        x2d = _merge(ya, ybs, lses, proj, x2d, w_branch_a[l].astype(BF16), w_branch_b[l].astype(BF16),
                     w_branch_c[l].astype(BF16), w_out[l].astype(BF16), ws, bsb, gmlp_norm[l][None],
                     tm=MERGE_TM)

        k_mem, v_mem = _memkv(mem, norm_mem[l][None], w_xkv[l].astype(BF16), xattn_qk_norm[l, 1][None])
        x2d, hf_rows, eidx, wts = _xattn(x2d, k_mem, v_mem, norm_xattn[l][None], w_xq[l].astype(BF16),
                                         xattn_qk_norm[l, 0][None], w_xo[l].astype(BF16), norm_ffn[l][None],
                                         rw_t, rb, seq=s, tm=XATTN_TM)

        x2d = _moe(x2d, hf_rows, eidx, wts, w_expert_gate, w_expert_up, w_expert_down, layer=l)

    return x2d.reshape(b, s, d)
```

```python
import functools
import math

import jax
import jax.numpy as jnp
import numpy as np
from jax import lax
from jax.experimental import pallas as pl
from jax.experimental.pallas import tpu as pltpu
from jax.experimental.pallas import tpu_sc as plsc

F32 = jnp.float32
BF16 = jnp.bfloat16

EPS = 1e-6
NEG = -1e30

MLSTM_HEADS = 4
MLSTM_DH = 128
MLSTM_W = MLSTM_HEADS * MLSTM_DH
CONV_WIDTH = 4
MLSTM_BLOCK = 128
MLSTM_GROUP = 4

ATTN_PATTERNS = ((128, 1), (512, 4), (2048, 16))
HEADS_PER_GROUP = 4
ATTN_DH = 64
ATTN_GW = HEADS_PER_GROUP * ATTN_DH
ATTN_W = len(ATTN_PATTERNS) * ATTN_GW
ATTN_BLOCK = 128
REL_BUCKETS = 32
REL_MAX_DIST = 2048

GMLP_GROUPS = 4
GMLP_GC = 128
GMLP_W = GMLP_GROUPS * GMLP_GC
GMLP_CHUNK = 128

XATTN_HEADS = 4
XATTN_DH = 128
XATTN_W = XATTN_HEADS * XATTN_DH
XATTN_SUB = 1024

N_EXPERTS = 16
N_EXPERT_GROUPS = 4
EXPERTS_PER_GROUP = 4
ROUTER_ROWS = 8 * N_EXPERT_GROUPS

N_BRANCH = 3

MOE_TM = 1024
ROW_CHUNKS = 4
SC_CORES, SC_SUBCORES = 2, 16
SC_WINDOW = 128

OFF_MQ, OFF_MK, OFF_MV, OFF_MO = 0, 512, 1024, 1536
OFF_GU, OFF_GV = 2048, 2560
OFF_GATE = 3072
OFF_IF = 6144
IF_PAD = 256
N_PROJ = OFF_IF + IF_PAD

ATTN_TILE = 2048
ATTN_SUB = ATTN_TILE // ATTN_BLOCK
ATTN_SLAB = 2 * ATTN_DH
ATTN_COLS = HEADS_PER_GROUP * ATTN_SLAB + 2 * ATTN_GW

VMEM_LIMIT = 48 * 1024 * 1024
VMEM_LIMIT_INPROJ = 56 * 1024 * 1024
VMEM_LIMIT_SMALL = 24 * 1024 * 1024

INPROJ_TM, INPROJ_TN = 1024, 2048
ATTNPROJ_SUB = 512
MERGE_TM = 512
XATTN_TM = 1024
COMBINE_TM = 1024
PLAN_TB = 1024


def _cparams(*sem, vmem=VMEM_LIMIT):
    return pltpu.CompilerParams(dimension_semantics=sem, vmem_limit_bytes=vmem)


def _rms(x, gain):
    return x * lax.rsqrt(jnp.mean(x * x, axis=-1, keepdims=True) + EPS) * gain


def _sigmoid(x):
    return 0.5 * jnp.tanh(0.5 * x) + 0.5


def _silu(x):
    half = 0.5 * x
    return half + half * jnp.tanh(half)


def _gelu(x):
    c = math.sqrt(2.0 / math.pi)
    half = 0.5 * x
    return half + half * jnp.tanh(x * (c + (c * 0.044715) * (x * x)))


def _dot(a, b):
    return jnp.dot(a, b, preferred_element_type=F32)


def _dot_nt(a, b):
    return lax.dot_general(a, b, (((1,), (1,)), ((), ())), preferred_element_type=F32)


def _inproj_kernel(x_ref, g_ref, w_ref, wg_ref, o_ref, h_ref, gt_ref):
    @pl.when(pl.program_id(1) == 0)
    def _():
        h = _rms(x_ref[...], g_ref[...]).astype(BF16)
        h_ref[...] = h
        gt_ref[...] = _dot_nt(wg_ref[0, 0:128, :], h)[:gt_ref.shape[0], :]

    o_ref[...] = _dot_nt(h_ref[...], w_ref[0]).astype(o_ref.dtype)


def _inproj(x2d, gain, w, *, layer, tm, tn):
    t, d = x2d.shape
    n = OFF_IF
    return pl.pallas_call(
        _inproj_kernel,
        grid=(t // tm, n // tn),
        in_specs=[pl.BlockSpec((tm, d), lambda i, j: (i, 0)),
                  pl.BlockSpec((1, d), lambda i, j: (0, 0)),
                  pl.BlockSpec((1, tn, d), lambda i, j: (layer, j, 0)),
                  pl.BlockSpec((1, IF_PAD, d), lambda i, j: (layer, OFF_IF // IF_PAD, 0))],
        out_specs=[pl.BlockSpec((tm, tn), lambda i, j: (i, j)),
                   pl.BlockSpec((tm, d), lambda i, j: (i, 0)),
                   pl.BlockSpec((8, tm), lambda i, j: (0, i))],
        out_shape=[jax.ShapeDtypeStruct((t, n), BF16), jax.ShapeDtypeStruct((t, d), BF16),
                   jax.ShapeDtypeStruct((8, t), F32)],
        compiler_params=_cparams("parallel", "arbitrary"),
        name="inproj",
    )(x2d, gain, w, w)


def _log_sigmoid(x):
    return jnp.minimum(x, 0.0) - jnp.log(1.0 + jnp.exp(-jnp.abs(x)))


def _split_bf16(x):
    hi = x.astype(BF16)
    return hi, (x - hi.astype(F32)).astype(BF16)


def _prefix_max(x):
    n = x.shape[1]
    lane = lax.broadcasted_iota(jnp.int32, x.shape, 1)
    shift = 1
    while shift < n:
        x = jnp.maximum(x, jnp.where(lane >= shift, pltpu.roll(x, shift, 1), NEG))
        shift *= 2
    return x


def _mlstm_rows_kernel(qk_ref, v_ref, og_ref, *rest, blk, group):
    gate_refs = rest[:group]
    cw_ref, bi_ref, bf_ref, ng_ref, y_ref, xe_scr, s_scr, m_scr = rest[group:]
    heads, dh, w = MLSTM_HEADS, MLSTM_DH, MLSTM_W

    @pl.when(pl.program_id(1) == 0)
    def _():
        xe_scr[:, 0:8, :] = jnp.zeros((group, 8, 2 * w), F32)
        s_scr[...] = jnp.zeros_like(s_scr)
        m_scr[...] = jnp.zeros_like(m_scr)

    cw = cw_ref[...]
    causal = lax.broadcasted_iota(jnp.int32, (blk, blk), 0) >= lax.broadcasted_iota(jnp.int32, (blk, blk), 1)
    triu = (lax.broadcasted_iota(jnp.int32, (blk, blk), 0)
            <= lax.broadcasted_iota(jnp.int32, (blk, blk), 1)).astype(BF16)
    ones = jnp.ones((blk, dh), BF16)
    s_in = [[s_scr[g, h] for h in range(heads)] for g in range(group)]
    m_in = [m_scr[g, :, 0:1] for g in range(group)]
    s_out = [[None] * heads for _ in range(group)]
    m_out = [None] * group
    per_seq = []
    for g in range(group):
        xe_scr[g, 8:8 + blk, :] = qk_ref[g].astype(F32)
        conv = cw[CONV_WIDTH - 1:CONV_WIDTH, :] * xe_scr[g, 8:8 + blk, :]
        for j in range(CONV_WIDTH - 1):
            off = 8 - (CONV_WIDTH - 1) + j
            conv = conv + cw[j:j + 1, :] * xe_scr[g, off:off + blk, :]
        xe_scr[g, 0:8, :] = xe_scr[g, blk:blk + 8, :]
        qk = _silu(conv)

        gates = gate_refs[g][...]
        i_r = gates + bi_ref[...]
        lf_hi, lf_lo = _split_bf16(_log_sigmoid(pltpu.roll(gates, heads, 0) + bf_ref[...]))
        b_r = _dot(lf_hi, triu) + _dot(lf_lo, triu)
        m_st = m_in[g]
        a_r = i_r - b_r
        inter = b_r + m_st
        m_t = jnp.maximum(inter, b_r + _prefix_max(a_r))
        b_last = b_r[:, blk - 1:blk]
        dec = b_last - b_r + i_r
        m_new = jnp.maximum(b_last + m_st, jnp.max(dec, axis=1, keepdims=True))
        w_c = jnp.exp(b_last + m_st - m_new)
        m_out[g] = m_new
        pack = jnp.concatenate([b_r - m_t, jnp.exp(inter - m_t), jnp.exp(-m_t), jnp.exp(dec - m_new),
                                jnp.zeros((blk - 32, blk), F32)], axis=0)
        per_seq.append((qk, a_r, pack.T, w_c))

    chains = [(g, h) for h in range(heads) for g in range(group)]
    st = {}
    for g, h in chains:
        qk = per_seq[g][0]
        sl = slice(h * dh, (h + 1) * dh)
        q_b = qk[:, sl].astype(BF16)
        k_f = qk[:, w + h * dh:w + (h + 1) * dh] * (dh ** -0.5)
        v_ext = jnp.concatenate([v_ref[g, :, sl], ones], axis=-1)
        st[g, h] = (q_b, k_f, v_ext, _dot_nt(q_b, k_f.astype(BF16)), _dot(q_b, s_in[g][h].astype(BF16)))
    for g, h in chains:
        q_b, k_f, v_ext, qk_t, q_state = st[g, h]
        _, a_r, cols, _ = per_seq[g]
        u_c, w_inter = cols[:, h:h + 1], cols[:, 8 + h:9 + h]
        w_intra = jnp.exp(jnp.where(causal, u_c + a_r[h:h + 1, :], NEG))
        st[g, h] = (k_f, v_ext, _dot((qk_t * w_intra).astype(BF16), v_ext) + w_inter * q_state)
    for g, h in chains:
        k_f, v_ext, tot = st[g, h]
        _, _, cols, w_c = per_seq[g]
        em_c, w_k = cols[:, 16 + h:17 + h], cols[:, 24 + h:25 + h]
        sl = slice(h * dh, (h + 1) * dh)
        num, den = tot[:, :dh], tot[:, dh:]
        hh = num / jnp.maximum(jnp.abs(den), em_c)
        hn = _rms(hh, ng_ref[:, sl])
        y_ref[g, :, sl] = (hn * _sigmoid(og_ref[g, :, sl].astype(F32))).astype(y_ref.dtype)
        s_out[g][h] = w_c[h:h + 1, :] * s_in[g][h] + _dot((k_f * w_k).T.astype(BF16), v_ext)
    for g in range(group):
        m_scr[g] = jnp.broadcast_to(m_out[g], m_scr.shape[1:])
        for h in range(heads):
            s_scr[g, h] = s_out[g][h]


def _mlstm_rows(proj, gates_t, conv_w, bias_i, bias_f, norm_g, *, batch, seq, blk, group):
    t, npj = proj.shape
    w = MLSTM_W
    proj3 = proj.reshape(batch, seq, npj)
    cols = lambda c: (lambda b, i: (b, i, c))
    const2 = lambda b, i: (0, 0)
    nblk = seq // blk
    gate_specs = [pl.BlockSpec((8, blk), functools.partial(lambda b, i, g: (0, (b * group + g) * nblk + i), g=g))
                  for g in range(group)]
    y = pl.pallas_call(
        functools.partial(_mlstm_rows_kernel, blk=blk, group=group),
        grid=(batch // group, seq // blk),
        in_specs=[pl.BlockSpec((group, blk, 2 * w), cols(OFF_MQ // (2 * w))),
                  pl.BlockSpec((group, blk, w), cols(OFF_MV // w)),
                  pl.BlockSpec((group, blk, w), cols(OFF_MO // w)),
                  *gate_specs,
                  pl.BlockSpec((CONV_WIDTH, 2 * w), const2),
                  pl.BlockSpec((8, 1), const2), pl.BlockSpec((8, 1), const2),
                  pl.BlockSpec((1, w), const2)],
        out_specs=pl.BlockSpec((group, blk, w), cols(0)),
        out_shape=jax.ShapeDtypeStruct((batch, seq, w), BF16),
        scratch_shapes=[pltpu.VMEM((group, blk + 8, 2 * w), F32),
                        pltpu.VMEM((group, MLSTM_HEADS, MLSTM_DH, 2 * MLSTM_DH), F32),
                        pltpu.VMEM((group, 8, 128), F32)],
        compiler_params=_cparams("parallel", "arbitrary", vmem=VMEM_LIMIT_SMALL),
        name="mlstm",
    )(proj3, proj3, proj3, *([gates_t] * group), conv_w, bias_i, bias_f, norm_g)
    return y.reshape(t, w)


def _attnproj_kernel(h_ref, w_ref, seg_ref, gq_ref, gk_ref, o_ref, r_scr, *, dil):
    gw, half = ATTN_GW, ATTN_SLAB // 2
    sub_rows = ATTNPROJ_SUB
    seg, sub_seg = ATTN_TILE // dil, sub_rows // dil

    def head_norm(x, gain):
        ss = _dot((x * x).astype(BF16), seg_ref[...])
        return x * lax.rsqrt(ss * (1.0 / ATTN_DH) + EPS) * gain

    low = lax.broadcasted_iota(jnp.int32, (1, ATTN_SLAB), 1) < half
    for s in range(ATTN_TILE // sub_rows):
        rows = slice(s * sub_rows, (s + 1) * sub_rows)
        res = _dot_nt(h_ref[rows, :], w_ref[0])
        q = head_norm(res[:, :gw], gq_ref[...]) * (ATTN_DH ** -0.5)
        k = head_norm(res[:, gw:2 * gw], gk_ref[...])
        slabs = []
        for pair in range(gw // ATTN_SLAB):
            qp = q[:, pair * ATTN_SLAB:(pair + 1) * ATTN_SLAB]
            slabs += [jnp.where(low, qp, 0.0), jnp.where(low, 0.0, qp)]
        slabs += [k[:, c * 128:(c + 1) * 128] for c in range(gw // 128)]
        slabs += [res[:, 2 * gw + c * 128:2 * gw + (c + 1) * 128] for c in range(gw // 128)]
        pitch = dil + 1 if dil % 16 == 0 else dil
        for c, slab in enumerate(slabs):
            if dil == 1:
                o_ref[rows, c * 128:(c + 1) * 128] = slab.astype(o_ref.dtype)
            elif pitch == dil:
                r_scr[s % 2, c, 0:sub_rows, :] = slab
            else:
                for i in range(sub_seg):
                    r_scr[s % 2, c, pitch * i:pitch * i + dil, :] = slab[dil * i:dil * (i + 1), :]
        if dil > 1:
            for r in range(dil):
                dst = slice(r * seg + s * sub_seg, r * seg + (s + 1) * sub_seg)
                for c in range(r_scr.shape[1]):
                    o_ref[dst, c * 128:(c + 1) * 128] = (
                        r_scr[s % 2, c, pl.ds(r, sub_seg, stride=pitch), :].astype(o_ref.dtype))


def _attnproj(h, w, seg_ones, gq, gk, *, layer, group, dilation):
    t, d = h.shape
    wcols = 3 * ATTN_GW
    const2 = lambda i: (0, 0)
    return pl.pallas_call(
        functools.partial(_attnproj_kernel, dil=dilation),
        grid=(t // ATTN_TILE,),
        in_specs=[pl.BlockSpec((ATTN_TILE, d), lambda i: (i, 0)),
                  pl.BlockSpec((1, wcols, d), lambda i: (layer, group, 0)),
                  pl.BlockSpec((ATTN_GW, ATTN_GW), const2),
                  pl.BlockSpec((1, ATTN_GW), const2), pl.BlockSpec((1, ATTN_GW), const2)],
        out_specs=pl.BlockSpec((ATTN_TILE, ATTN_COLS), lambda i: (i, 0)),
        out_shape=jax.ShapeDtypeStruct((t, ATTN_COLS), BF16),
        scratch_shapes=[pltpu.VMEM((2, ATTN_COLS // 128, ATTNPROJ_SUB + ATTNPROJ_SUB // 16, 128), F32)],
        compiler_params=_cparams("parallel"),
        name=f"attnproj{group}",
    )(h, w, seg_ones, gq, gk)


def _dattn_kernel(q_ref, kc_ref, kp_ref, vc_ref, vp_ref, b0_ref, o_ref, lse_ref,
                  kx_scr, vx_scr, o_scr, l_scr, bias_scr, *, dil):
    blk = ATTN_BLOCK
    per = ATTN_SUB // dil
    pitch = dil + 1 if dil % 16 == 0 else dil
    first_tile = pl.program_id(1) == 0

    @pl.when(first_tile)
    def _():
        for h in range(HEADS_PER_GROUP):
            bias_scr[h] = pltpu.roll(jnp.broadcast_to(b0_ref[h], (blk, 2 * blk)), 0, 1, stride=1, stride_axis=0)

    for r in range(dil):
        base = r * (per + 1) * blk
        last = slice((r * per + per - 1) * blk, (r * per + per) * blk)
        mine = slice(r * per * blk, (r + 1) * per * blk)
        kx_scr[base:base + blk, :] = kp_ref[last, :]
        vx_scr[base:base + blk, :] = vp_ref[last, :]
        kx_scr[base + blk:base + (per + 1) * blk, :] = kc_ref[mine, :]
        vx_scr[base + blk:base + (per + 1) * blk, :] = vc_ref[mine, :]

    low = lax.broadcasted_iota(jnp.int32, (1, ATTN_SLAB), 1) < ATTN_SLAB // 2
    no_prev = lax.broadcasted_iota(jnp.int32, (1, 2 * blk), 1) < blk
    for r in range(dil):
        for sub in range(per):
            u = r * per + sub
            win = slice((r * (per + 1) + sub) * blk, (r * (per + 1) + sub + 2) * blk)
            o_slabs, l_slabs = [], []
            for pair in range(ATTN_GW // ATTN_SLAB):
                cols = slice(pair * ATTN_SLAB, (pair + 1) * ATTN_SLAB)
                kx, vx = kx_scr[win, cols], vx_scr[win, cols]
                o_pair, l_pair = [], []
                for h in (2 * pair, 2 * pair + 1):
                    logits = _dot_nt(q_ref[u * blk:(u + 1) * blk, h * ATTN_SLAB:(h + 1) * ATTN_SLAB], kx)
                    logits = logits + bias_scr[h]
                    if sub == 0:
                        logits = jnp.where(first_tile & no_prev, NEG, logits)
                    m = jnp.max(logits, axis=-1, keepdims=True)
                    p = jnp.exp(logits - m)
                    l = jnp.sum(p, axis=-1, keepdims=True)
                    o_pair.append(_dot(p.astype(BF16), vx) / l)
                    l_pair.append(m + jnp.log(l))
                o_slabs.append(jnp.where(low, o_pair[0], o_pair[1]))
                l_slabs.append(jnp.where(low, l_pair[0], l_pair[1]))
            for c in range(ATTN_GW // ATTN_SLAB):
                cols = slice(c * ATTN_SLAB, (c + 1) * ATTN_SLAB)
                if dil == 1:
                    o_ref[u * blk:(u + 1) * blk, cols] = o_slabs[c].astype(o_ref.dtype)
                    lse_ref[u * blk:(u + 1) * blk, cols] = l_slabs[c]
                else:
                    dst = pl.ds(sub * blk * pitch + r, blk, stride=pitch)
                    o_scr[c, dst, :] = o_slabs[c]
                    l_scr[c, dst, :] = l_slabs[c]
    if dil > 1:
        for c in range(ATTN_GW // ATTN_SLAB):
            cols = slice(c * ATTN_SLAB, (c + 1) * ATTN_SLAB)
            if pitch == dil:
                o_ref[:, cols] = o_scr[c, 0:ATTN_TILE, :].astype(o_ref.dtype)
                lse_ref[:, cols] = l_scr[c, 0:ATTN_TILE, :]
            else:
                for i in range(ATTN_TILE // dil):
                    o_ref[dil * i:dil * (i + 1), cols] = o_scr[c, pitch * i:pitch * i + dil, :].astype(o_ref.dtype)
                    lse_ref[dil * i:dil * (i + 1), cols] = l_scr[c, pitch * i:pitch * i + dil, :]


def _dattn(aproj, bias, *, seq, group, dilation):
    t = aproj.shape[0]
    tiles = seq // ATTN_TILE
    qw = HEADS_PER_GROUP * ATTN_SLAB
    cq, ck, cv = 0, qw // ATTN_GW, qw // ATTN_GW + 1
    blk = (ATTN_TILE, ATTN_GW)
    cur = lambda c: (lambda b, j: (b * tiles + j, c))
    prev = lambda c: (lambda b, j: (b * tiles + jnp.maximum(j - 1, 0), c))
    xrows = ATTN_TILE + dilation * ATTN_BLOCK
    return pl.pallas_call(
        functools.partial(_dattn_kernel, dil=dilation),
        grid=(t // seq, tiles),
        in_specs=[pl.BlockSpec((ATTN_TILE, qw), cur(cq)),
                  pl.BlockSpec(blk, cur(ck)), pl.BlockSpec(blk, prev(ck)),
                  pl.BlockSpec(blk, cur(cv)), pl.BlockSpec(blk, prev(cv)),
                  pl.BlockSpec((HEADS_PER_GROUP, 1, 2 * ATTN_BLOCK), lambda b, j: (0, 0, 0))],
        out_specs=[pl.BlockSpec(blk, cur(0)), pl.BlockSpec(blk, cur(0))],
        out_shape=[jax.ShapeDtypeStruct((t, ATTN_GW), BF16), jax.ShapeDtypeStruct((t, ATTN_GW), F32)],
        scratch_shapes=[pltpu.VMEM((xrows, ATTN_GW), BF16), pltpu.VMEM((xrows, ATTN_GW), BF16),
                        pltpu.VMEM((ATTN_GW // ATTN_SLAB, ATTN_TILE + ATTN_TILE // 16, ATTN_SLAB), F32),
                        pltpu.VMEM((ATTN_GW // ATTN_SLAB, ATTN_TILE + ATTN_TILE // 16, ATTN_SLAB), F32),
                        pltpu.VMEM((HEADS_PER_GROUP, ATTN_BLOCK, 2 * ATTN_BLOCK), F32)],
        compiler_params=_cparams("parallel", "arbitrary"),
        name=f"dattn{group}",
    )(aproj, aproj, aproj, aproj, aproj, bias)


def _rel_bucket(n):
    max_exact = REL_BUCKETS // 2
    nf = jnp.maximum(n, 1).astype(F32)
    log_b = max_exact + (jnp.log(nf / max_exact) / math.log(REL_MAX_DIST / max_exact)
                         * (REL_BUCKETS - max_exact)).astype(jnp.int32)
    return jnp.where(n < max_exact, n, jnp.minimum(log_b, REL_BUCKETS - 1))


def _attn_bias(rel_bias, group):
    window, dilation = ATTN_PATTERNS[group]
    steps = window // dilation
    assert steps == ATTN_BLOCK
    hs = slice(group * HEADS_PER_GROUP, (group + 1) * HEADS_PER_GROUP)
    bucket = _rel_bucket((steps - jnp.arange(steps + 1)) * dilation)
    by_dist = jnp.dot(jax.nn.one_hot(bucket, REL_BUCKETS, dtype=F32), rel_bias[:, hs].astype(F32),
                      precision=lax.Precision.HIGHEST)
    row0 = jnp.concatenate([by_dist, jnp.full((2 * ATTN_BLOCK - steps - 1, HEADS_PER_GROUP), NEG, F32)], axis=0)
    return row0.T[:, None, :]


def _merge_kernel(ya_ref, yb0_ref, yb1_ref, yb2_ref, l0_ref, l1_ref, l2_ref, gu_ref, gv_ref, gate_ref,
                  x_ref, wa_ref, wb_ref, wc_ref, wo_ref, ws_ref, bs_ref, gg_ref, o_ref, yc_scr, *, tm):
    d = x_ref.shape[1]
    l0, l1, l2 = l0_ref[...], l1_ref[...], l2_ref[...]
    mx = jnp.maximum(jnp.maximum(l0, l1), l2)
    e0, e1, e2 = jnp.exp(l0 - mx), jnp.exp(l1 - mx), jnp.exp(l2 - mx)
    inv = 1.0 / (e0 + e1 + e2)
    yb = jnp.concatenate([(yb0_ref[...].astype(F32) * (e0 * inv)).astype(BF16),
                          (yb1_ref[...].astype(F32) * (e1 * inv)).astype(BF16),
                          (yb2_ref[...].astype(F32) * (e2 * inv)).astype(BF16)], axis=-1)

    for j in range(tm // GMLP_CHUNK):
        rows = slice(j * GMLP_CHUNK, (j + 1) * GMLP_CHUNK)
        for g in range(GMLP_GROUPS):
            cols = slice(g * GMLP_GC, (g + 1) * GMLP_GC)
            u = _gelu(gu_ref[rows, cols].astype(F32))
            v = _rms(_gelu(gv_ref[rows, cols].astype(F32)), gg_ref[:, cols])
            mixed = _dot(ws_ref[g], v.astype(BF16)) + bs_ref[g]
            yc_scr[rows, cols] = (u * mixed).astype(BF16)

    def gate2(k):
        return jnp.tanh(0.5 * gate_ref[:, k * d:(k + 1) * d].astype(F32)) + 1.0

    merged2 = gate2(0) * _dot(ya_ref[...], wa_ref[...])
    merged2 = merged2 + gate2(1) * _dot(yb, wb_ref[...])
    merged2 = merged2 + gate2(2) * _dot(yc_scr[...], wc_ref[...])
    o_ref[...] = x_ref[...] + 0.5 * _dot(merged2.astype(BF16), wo_ref[...])


def _merge(ya, ybs, lses, proj, x2d, wa, wb, wc, wo, ws, bsb, gg, *, tm):
    t, d = x2d.shape
    row = lambda c: (lambda i: (i, c))
    full2 = lambda i: (0, 0)
    full3 = lambda i: (0, 0, 0)
    gspec = pl.BlockSpec((tm, ATTN_GW), row(0))
    return pl.pallas_call(
        functools.partial(_merge_kernel, tm=tm),
        grid=(t // tm,),
        in_specs=[pl.BlockSpec((tm, MLSTM_W), row(0)),
                  gspec, gspec, gspec, gspec, gspec, gspec,
                  pl.BlockSpec((tm, GMLP_W), row(OFF_GU // GMLP_W)),
                  pl.BlockSpec((tm, GMLP_W), row(OFF_GV // GMLP_W)),
                  pl.BlockSpec((tm, N_BRANCH * d), row(OFF_GATE // (N_BRANCH * d))),
                  pl.BlockSpec((tm, d), row(0)),
                  pl.BlockSpec(wa.shape, full2), pl.BlockSpec(wb.shape, full2),
                  pl.BlockSpec(wc.shape, full2), pl.BlockSpec(wo.shape, full2),
                  pl.BlockSpec(ws.shape, full3), pl.BlockSpec(bsb.shape, full3),
                  pl.BlockSpec(gg.shape, full2)],
        out_specs=pl.BlockSpec((tm, d), row(0)),
        out_shape=jax.ShapeDtypeStruct((t, d), F32),
        scratch_shapes=[pltpu.VMEM((tm, GMLP_W), BF16)],
        compiler_params=_cparams("parallel"),
        name="merge",
    )(ya, *ybs, *lses, proj, proj, proj, x2d, wa, wb, wc, wo, ws, bsb, gg)


def _memkv_kernel(mem_ref, g_ref, w_ref, gk_ref, k_ref, v_ref):
    dh, w = XATTN_DH, XATTN_W
    kv = _dot(_rms(mem_ref[0], g_ref[...]).astype(BF16), w_ref[...])
    for h in range(XATTN_HEADS):
        sl = slice(h * dh, (h + 1) * dh)
        k_ref[0, :, sl] = _rms(kv[:, sl], gk_ref[...]).astype(k_ref.dtype)
    v_ref[0] = kv[:, w:].astype(v_ref.dtype)


def _memkv(mem, gain, w_kv, gk):
    b, m, d = mem.shape
    full2 = lambda i: (0, 0)
    return pl.pallas_call(
        _memkv_kernel,
        grid=(b,),
        in_specs=[pl.BlockSpec((1, m, d), lambda i: (i, 0, 0)),
                  pl.BlockSpec((1, d), full2),
                  pl.BlockSpec(w_kv.shape, full2),
                  pl.BlockSpec((1, XATTN_DH), full2)],
        out_specs=[pl.BlockSpec((1, m, XATTN_W), lambda i: (i, 0, 0)),
                   pl.BlockSpec((1, m, XATTN_W), lambda i: (i, 0, 0))],
        out_shape=[jax.ShapeDtypeStruct((b, m, XATTN_W), BF16),
                   jax.ShapeDtypeStruct((b, m, XATTN_W), BF16)],
        compiler_params=_cparams("parallel", vmem=VMEM_LIMIT_SMALL),
        name="memkv",
    )(mem, gain, w_kv, gk)


def _route(logits):
    tm = logits.shape[1]
    e = jnp.exp(logits - jnp.max(logits, axis=0, keepdims=True))
    probs = e / jnp.sum(e, axis=0, keepdims=True)
    rowi = lax.broadcasted_iota(jnp.int32, (8, tm), 0)
    real = rowi < EXPERTS_PER_GROUP
    tops = []
    for g in range(N_EXPERT_GROUPS):
        pg = jnp.where(real, probs[8 * g:8 * g + 8, :], -0.5)
        m1 = jnp.max(pg, axis=0, keepdims=True)
        i1 = jnp.min(jnp.where(pg == m1, rowi, 8), axis=0, keepdims=True)
        pg2 = jnp.where(rowi == i1, -1.0, pg)
        m2 = jnp.max(pg2, axis=0, keepdims=True)
        i2 = jnp.min(jnp.where(pg2 == m2, rowi, 8), axis=0, keepdims=True)
        tops.append((m1, i1, m2, i2))
    best = jnp.zeros((1, tm), jnp.int32)
    best_score = tops[0][0] + tops[0][2]
    for g in range(1, N_EXPERT_GROUPS):
        score = tops[g][0] + tops[g][2]
        better = score > best_score
        best = jnp.where(better, g, best)
        best_score = jnp.where(better, score, best_score)
    m1, i1, m2, i2 = tops[0]
    for g in range(1, N_EXPERT_GROUPS):
        m1, i1, m2, i2 = (jnp.where(best == g, new, old) for new, old in zip(tops[g], (m1, i1, m2, i2)))
    tot = m1 + m2
    base = best * EXPERTS_PER_GROUP
    return base + i1, base + i2, m1 / tot, m2 / tot


def _pack_bf16_pairs(x):
    n = x.shape[1] // 2
    hi = lax.bitcast_convert_type(x[:, :n].astype(BF16).astype(F32), jnp.uint32)
    lo = lax.bitcast_convert_type(x[:, n:].astype(BF16).astype(F32), jnp.uint32)
    return hi | (lo >> 16)


def _unpack_bf16_pairs(p):
    hi = lax.bitcast_convert_type(p & jnp.uint32(0xFFFF0000), F32)
    lo = lax.bitcast_convert_type(p << 16, F32)
    return hi, lo


def _store_row_chunks(ref, packed):
    for j in range(ROW_CHUNKS):
        ref[j] = packed[:, j * 128:(j + 1) * 128]


def _load_row_chunks(ref):
    return jnp.concatenate([ref[j] for j in range(ROW_CHUNKS)], axis=-1)


def _xattn_kernel(x_ref, k_ref, v_ref, gx_ref, wq_ref, gq_ref, wo_ref, gf_ref, rw_ref, rb_ref,
                  xo_ref, hf_ref, eidx_ref, wts_ref, *, sub):
    dh = XATTN_DH
    rw = rw_ref[...]
    rw_hi, rw_lo = _split_bf16(rw)
    for s in range(x_ref.shape[0] // sub):
        rows = slice(s * sub, (s + 1) * sub)
        x = x_ref[rows, :]
        q = _dot(_rms(x, gx_ref[...]).astype(BF16), wq_ref[...])
        outs = []
        for h in range(XATTN_HEADS):
            sl = slice(h * dh, (h + 1) * dh)
            q_h = (_rms(q[:, sl], gq_ref[...]) * (dh ** -0.5)).astype(BF16)
            logits = _dot_nt(q_h, k_ref[0, :, sl])
            p = jnp.exp(logits - jnp.max(logits, axis=-1, keepdims=True))
            o = _dot(p.astype(BF16), v_ref[0, :, sl]) / jnp.sum(p, axis=-1, keepdims=True)
            outs.append(o.astype(BF16))
        xn = x + _dot(jnp.concatenate(outs, axis=-1), wo_ref[...])
        xo_ref[rows, :] = xn
        hf = _rms(xn, gf_ref[...])
        packed = _pack_bf16_pairs(hf)
        for j in range(ROW_CHUNKS):
            hf_ref[j, rows, :] = packed[:, j * 128:(j + 1) * 128]
        hf_hi, hf_lo = _split_bf16(hf)
        logits_t = _dot_nt(rw_hi, hf_hi) + _dot_nt(rw_hi, hf_lo) + _dot_nt(rw_lo, hf_hi) + rb_ref[...]
        e1, e2, w1, w2 = _route(logits_t)
        eidx_ref[:, rows] = jnp.concatenate([e1, e2, jnp.zeros((6, sub), jnp.int32)], axis=0)
        wts_ref[:, rows] = jnp.concatenate([w1, w2, jnp.zeros((6, sub), F32)], axis=0)


def _xattn(x2d, k, v, gx, wq, gq, wo, gf, rw_t, rb, *, seq, tm):
    t, d = x2d.shape
    per_b = seq // tm
    full2 = lambda i: (0, 0)
    kv_spec = pl.BlockSpec((1,) + k.shape[1:], lambda i: (i // per_b, 0, 0))
    return pl.pallas_call(
        functools.partial(_xattn_kernel, sub=min(tm, XATTN_SUB)),
        grid=(t // tm,),
        in_specs=[pl.BlockSpec((tm, d), lambda i: (i, 0)), kv_spec, kv_spec,
                  pl.BlockSpec((1, d), full2), pl.BlockSpec(wq.shape, full2),
                  pl.BlockSpec((1, XATTN_DH), full2), pl.BlockSpec(wo.shape, full2),
                  pl.BlockSpec((1, d), full2), pl.BlockSpec(rw_t.shape, full2),
                  pl.BlockSpec(rb.shape, full2)],
        out_specs=[pl.BlockSpec((tm, d), lambda i: (i, 0)),
                   pl.BlockSpec((ROW_CHUNKS, tm, 128), lambda i: (0, i, 0)),
                   pl.BlockSpec((8, tm), lambda i: (0, i)),
                   pl.BlockSpec((8, tm), lambda i: (0, i))],
        out_shape=[jax.ShapeDtypeStruct((t, d), F32),
                   jax.ShapeDtypeStruct((ROW_CHUNKS, t, 128), jnp.uint32),
                   jax.ShapeDtypeStruct((8, t), jnp.int32),
                   jax.ShapeDtypeStruct((8, t), F32)],
        compiler_params=_cparams("parallel"),
        name="xattn_router",
    )(x2d, k, v, gx, wq, gq, wo, gf, rw_t, rb)


def _moe_plan_kernel(eidx_ref, i1_ref, i2_ref, te_ref, na_ref, cnt_scr, carry_scr, *, tb, tm, plane_rows):
    ne = N_EXPERTS
    hp = lax.Precision.HIGHEST
    phase, j = pl.program_id(0), pl.program_id(1)
    rows = lax.broadcasted_iota(jnp.int32, (ne, tb), 0)
    oh1 = rows == eidx_ref[0:1, :]
    oh2 = rows == eidx_ref[1:2, :]
    a = oh1.astype(F32) + oh2.astype(F32)
    blk_cnt = jnp.broadcast_to(jnp.sum(a, axis=1, keepdims=True), cnt_scr.shape)

    @pl.when((phase == 0) & (j == 0))
    def _():
        cnt_scr[...] = jnp.zeros_like(cnt_scr)

    @pl.when(phase == 0)
    def _():
        cnt_scr[...] += blk_cnt

    @pl.when((phase == 1) & (j == 0))
    def _():
        padded = jnp.ceil(cnt_scr[...] * (1.0 / tm)) * tm
        er = lax.broadcasted_iota(jnp.int32, (ne, ne), 0)
        ec = lax.broadcasted_iota(jnp.int32, (ne, ne), 1)
        off = jnp.dot((ec < er).astype(F32), padded, precision=hp, preferred_element_type=F32)
        carry_scr[...] = off
        seg_end = (off + padded)[:, 0:1]
        tile_start = lax.broadcasted_iota(jnp.int32, (ne, te_ref.shape[1]), 1).astype(F32) * tm
        te = jnp.sum((seg_end <= tile_start).astype(F32), axis=0, keepdims=True)
        te_ref[...] = jnp.broadcast_to(jnp.minimum(te, ne - 1.0), te_ref.shape).astype(jnp.int32)
        total = jnp.sum(padded[:, 0:1], axis=0, keepdims=True)
        na_ref[...] = jnp.broadcast_to(total * (1.0 / tm), na_ref.shape).astype(jnp.int32)

    @pl.when(phase == 1)
    def _():
        before = (lax.broadcasted_iota(jnp.int32, (tb, tb), 0)
                  < lax.broadcasted_iota(jnp.int32, (tb, tb), 1)).astype(BF16)
        rank = carry_scr[:, 0:1] + _dot(a.astype(BF16), before)
        d1 = jnp.sum(jnp.where(oh1, rank, 0.0), axis=0, keepdims=True).astype(jnp.int32)
        d2 = jnp.sum(jnp.where(oh2, rank, 0.0), axis=0, keepdims=True).astype(jnp.int32)
        plane = lax.broadcasted_iota(jnp.int32, (8, tb), 0) * plane_rows
        i1_ref[...] = jnp.where(plane < ROW_CHUNKS * plane_rows, plane + d1, 0)
        i2_ref[...] = jnp.where(plane < ROW_CHUNKS * plane_rows, plane + d2, 0)
        carry_scr[...] += blk_cnt


def _moe_plan(eidx, *, tm, n_tiles, tb=PLAN_TB):
    t = eidx.shape[1]
    ntp = -(-n_tiles // 128) * 128
    return pl.pallas_call(
        functools.partial(_moe_plan_kernel, tb=tb, tm=tm, plane_rows=n_tiles * tm),
        grid=(2, t // tb),
        in_specs=[pl.BlockSpec((8, tb), lambda p, j: (0, j))],
        out_specs=[pl.BlockSpec((8, tb), lambda p, j: (0, j * p)),
                   pl.BlockSpec((8, tb), lambda p, j: (0, j * p)),
                   pl.BlockSpec((8, ntp), lambda p, j: (0, 0)),
                   pl.BlockSpec((8, 128), lambda p, j: (0, 0))],
        out_shape=[jax.ShapeDtypeStruct((8, t), jnp.int32),
                   jax.ShapeDtypeStruct((8, t), jnp.int32),
                   jax.ShapeDtypeStruct((8, ntp), jnp.int32),
                   jax.ShapeDtypeStruct((8, 128), jnp.int32)],
        scratch_shapes=[pltpu.VMEM((N_EXPERTS, 128), F32), pltpu.VMEM((N_EXPERTS, 128), F32)],
        compiler_params=_cparams("arbitrary", "arbitrary", vmem=VMEM_LIMIT_SMALL),
        name="moe_plan",
    )(eidx)


def _sc_mesh():
    return plsc.VectorSubcoreMesh(core_axis_name="c", subcore_axis_name="s",
                                  num_cores=SC_CORES, num_subcores=SC_SUBCORES)


def _sc_index_spec(tokens):
    nb = tokens // SC_WINDOW
    return pl.BlockSpec((1, SC_WINDOW), lambda i: (i // nb, i % nb))


def _sc_dispatch(rows, i1, i2, n_out):
    n = rows.shape[0]
    tokens = i1.shape[1]

    @functools.partial(pl.kernel, out_type=jax.ShapeDtypeStruct((n_out, 128), rows.dtype), mesh=_sc_mesh(),
                       name="moe_dispatch")
    def k(x_hbm, i1_hbm, i2_hbm, o_hbm):
        def body(x_vmem, i1_vmem, i2_vmem):
            pltpu.sync_copy(x_vmem, o_hbm.at[i1_vmem.at[0]])
            pltpu.sync_copy(x_vmem, o_hbm.at[i2_vmem.at[0]])

        pltpu.emit_pipeline(
            body, grid=(n // SC_WINDOW,),
            in_specs=[pl.BlockSpec((SC_WINDOW, 128), lambda i: (i, 0)),
                      _sc_index_spec(tokens), _sc_index_spec(tokens)],
            out_specs=[],
            core_axis_name=("c", "s"), dimension_semantics=(pltpu.PARALLEL,),
        )(x_hbm, i1_hbm, i2_hbm)

    return k(rows, i1, i2)


def _sc_collect(table, i1, i2):
    tokens = i1.shape[1]
    n = ROW_CHUNKS * tokens
    out = jax.ShapeDtypeStruct((n, 128), table.dtype)

    @functools.partial(pl.kernel, out_type=(out, out), mesh=_sc_mesh(), name="moe_collect",
                       scratch_types=[pltpu.SemaphoreType.DMA, pltpu.SemaphoreType.DMA])
    def k(t_hbm, i1_hbm, i2_hbm, o1_hbm, o2_hbm, sem1, sem2):
        def body(i1_vmem, i2_vmem, o1_vmem, o2_vmem):
            first = pltpu.async_copy(t_hbm.at[i1_vmem.at[0]], o1_vmem, sem1)
            second = pltpu.async_copy(t_hbm.at[i2_vmem.at[0]], o2_vmem, sem2)
            first.wait()
            second.wait()

        pltpu.emit_pipeline(
            body, grid=(n // SC_WINDOW,),
            in_specs=[_sc_index_spec(tokens), _sc_index_spec(tokens)],
            out_specs=[pl.BlockSpec((SC_WINDOW, 128), lambda i: (i, 0)),
                       pl.BlockSpec((SC_WINDOW, 128), lambda i: (i, 0))],
            core_axis_name=("c", "s"), dimension_semantics=(pltpu.PARALLEL,),
        )(i1_hbm, i2_hbm, o1_hbm, o2_hbm)

    return k(table, i1, i2)


def _experts_kernel(te_ref, na_ref, xs_ref, wg_ref, wu_ref, wd_ref, y_ref, wg_scr, wu_scr, wd_scr):
    i = pl.program_id(0)
    active = i < na_ref[0]

    @pl.when(active & ((i == 0) | (te_ref[i] != te_ref[jnp.maximum(i - 1, 0)])))
    def _():
        wg_scr[...] = wg_ref[0, 0].astype(BF16)
        wu_scr[...] = wu_ref[0, 0].astype(BF16)
        wd_scr[...] = wd_ref[0, 0].astype(BF16)

    @pl.when(active)
    def _():
        hi, lo = _unpack_bf16_pairs(_load_row_chunks(xs_ref))
        h = jnp.concatenate([hi, lo], axis=-1).astype(BF16)
        up = _dot(h, wg_scr[...])
        act = _silu(up) * _dot(h, wu_scr[...])
        _store_row_chunks(y_ref, _pack_bf16_pairs(_dot(act.astype(BF16), wd_scr[...])))


def _experts(tile_expert, n_active, xs, wg, wu, wd, *, layer, tm):
    n_tiles = tile_expert.shape[0]
    _, _, d, dff = wg.shape
    rows = lambda i, te, na: (0, jnp.minimum(i, na[0] - 1), 0)
    expert = lambda i, te, na: (layer, te[i], 0, 0)
    return pl.pallas_call(
        _experts_kernel,
        grid_spec=pltpu.PrefetchScalarGridSpec(
            num_scalar_prefetch=2,
            grid=(n_tiles,),
            in_specs=[pl.BlockSpec((ROW_CHUNKS, tm, 128), rows),
                      pl.BlockSpec((1, 1, d, dff), expert),
                      pl.BlockSpec((1, 1, d, dff), expert),
                      pl.BlockSpec((1, 1, dff, d), expert)],
            out_specs=pl.BlockSpec((ROW_CHUNKS, tm, 128), rows),
            scratch_shapes=[pltpu.VMEM((d, dff), BF16), pltpu.VMEM((d, dff), BF16), pltpu.VMEM((dff, d), BF16)]),
        out_shape=jax.ShapeDtypeStruct(xs.shape, xs.dtype),
        compiler_params=_cparams("arbitrary"),
        name="moe_experts",
    )(tile_expert, n_active, xs, wg, wu, wd)


def _moe_combine_kernel(x_ref, y1_ref, y2_ref, w_ref, o_ref):
    half = x_ref.shape[1] // 2
    hi1, lo1 = _unpack_bf16_pairs(_load_row_chunks(y1_ref))
    hi2, lo2 = _unpack_bf16_pairs(_load_row_chunks(y2_ref))
    tm = x_ref.shape[0]
    w_cols = jnp.concatenate([w_ref[...], jnp.zeros((128 - w_ref.shape[0], tm), F32)], axis=0).T
    w1, w2 = w_cols[:, 0:1], w_cols[:, 1:2]
    o_ref[:, :half] = x_ref[:, :half] + w1 * hi1 + w2 * hi2
    o_ref[:, half:] = x_ref[:, half:] + w1 * lo1 + w2 * lo2


def _moe_combine(x2d, y1, y2, wts, *, tm):
    t, d = x2d.shape
    chunk_spec = pl.BlockSpec((ROW_CHUNKS, tm, 128), lambda i: (0, i, 0))
    return pl.pallas_call(
        _moe_combine_kernel,
        grid=(t // tm,),
        in_specs=[pl.BlockSpec((tm, d), lambda i: (i, 0)), chunk_spec, chunk_spec,
                  pl.BlockSpec((wts.shape[0], tm), lambda i: (0, i))],
        out_specs=pl.BlockSpec((tm, d), lambda i: (i, 0)),
        out_shape=jax.ShapeDtypeStruct((t, d), F32),
        compiler_params=_cparams("parallel"),
        name="moe_combine",
    )(x2d, y1, y2, wts)


def _moe(x2d, hf_rows, eidx, wts, wg, wu, wd, *, layer):
    t = x2d.shape[0]
    tm = MOE_TM
    n_tiles = 2 * t // tm + N_EXPERTS
    plane = n_tiles * tm
    i1, i2, te, na = _moe_plan(eidx, tm=tm, n_tiles=n_tiles)
    xs = _sc_dispatch(hf_rows.reshape(ROW_CHUNKS * t, 128), i1, i2, ROW_CHUNKS * plane)
    ys = _experts(te[0, :n_tiles], na[0, :1], xs.reshape(ROW_CHUNKS, plane, 128), wg, wu, wd,
                  layer=layer, tm=tm)
    y1, y2 = _sc_collect(ys.reshape(ROW_CHUNKS * plane, 128), i1, i2)
    return _moe_combine(x2d, y1.reshape(ROW_CHUNKS, t, 128), y2.reshape(ROW_CHUNKS, t, 128), wts,
                        tm=COMBINE_TM)


W_ROWS = 256


def _w_rows_kernel(start_ref, valid_ref, w_ref, o_ref):
    del start_ref
    row = lax.broadcasted_iota(jnp.int32, w_ref.shape, 1)
    o_ref[...] = jnp.where(row < valid_ref[pl.program_id(0)], w_ref[...], 0.0).astype(o_ref.dtype)


def _w_rows(w_t, starts, valid):
    depth, _, d = w_t.shape
    nblk = len(starts)
    return pl.pallas_call(
        _w_rows_kernel,
        grid_spec=pltpu.PrefetchScalarGridSpec(
            num_scalar_prefetch=2,
            grid=(nblk,),
            in_specs=[pl.BlockSpec((pl.Element(depth), pl.Element(W_ROWS), pl.Element(d)),
                                   lambda c, st, va: (0, pl.multiple_of(st[c], 8), 0))],
            out_specs=pl.BlockSpec((depth, W_ROWS, d), lambda c, st, va: (0, c, 0))),
        out_shape=jax.ShapeDtypeStruct((depth, nblk * W_ROWS, d), BF16),
        compiler_params=_cparams("arbitrary", vmem=VMEM_LIMIT_SMALL),
        name="w_in_rows",
    )(jnp.asarray(starts, jnp.int32), jnp.asarray(valid, jnp.int32), w_t)


def _w_in_layout(w_in):
    w_t = jnp.swapaxes(w_in, 1, 2)
    src_if = 4 * MLSTM_W
    src_a = src_if + 2 * MLSTM_HEADS
    src_g = src_a + 3 * ATTN_W
    starts = list(range(0, src_if, W_ROWS)) + [src_g + k * W_ROWS for k in range((OFF_IF - OFF_GU) // W_ROWS)]
    valid = [W_ROWS] * len(starts)
    starts.append(src_if)
    valid.append(2 * MLSTM_HEADS)
    assert len(starts) * W_ROWS == N_PROJ and ATTN_GW == W_ROWS
    a_starts = [src_a + j * ATTN_W + g * ATTN_GW for g in range(len(ATTN_PATTERNS)) for j in range(3)]
    return _w_rows(w_t, starts, valid), _w_rows(w_t, a_starts, [W_ROWS] * len(a_starts))


def kernel(x, mem, norm_mix, w_in, mlstm_conv, mlstm_gate_b, mlstm_norm, attn_qk_norm, gmlp_norm, gmlp_ws,
           gmlp_bs, w_branch_a, w_branch_b, w_branch_c, w_out, rel_bias, norm_xattn, norm_mem, w_xq, w_xkv,
           xattn_qk_norm, w_xo, norm_ffn, router_w, router_b, w_expert_gate, w_expert_up, w_expert_down):
    b, s, d = x.shape
    t = b * s
    depth = w_in.shape[0]
    assert d == 2 * ROW_CHUNKS * 128 and OFF_IF == OFF_GATE + N_BRANCH * d
    assert s % ATTN_TILE == 0 and s % MLSTM_BLOCK == 0 and b % MLSTM_GROUP == 0
    assert all(window == dil * ATTN_BLOCK and ATTN_SUB % dil == 0 for window, dil in ATTN_PATTERNS)
    assert t % max(INPROJ_TM, MERGE_TM, XATTN_TM, COMBINE_TM, MOE_TM, PLAN_TB) == 0 and s % XATTN_TM == 0
    x2d = x.reshape(t, d)

    biases = [_attn_bias(rel_bias, g) for g in range(len(ATTN_PATTERNS))]
    rw_t = jnp.zeros((N_EXPERT_GROUPS, 8, d), F32).at[:, :EXPERTS_PER_GROUP].set(
        router_w.T.reshape(N_EXPERT_GROUPS, EXPERTS_PER_GROUP, d)).reshape(ROUTER_ROWS, d)
    rb = jnp.full((N_EXPERT_GROUPS, 8), NEG, F32).at[:, :EXPERTS_PER_GROUP].set(
        router_b.astype(F32).reshape(N_EXPERT_GROUPS, EXPERTS_PER_GROUP)).reshape(ROUTER_ROWS, 1)
    tril = jnp.tril(jnp.ones((GMLP_CHUNK, GMLP_CHUNK), bool))
    head_of = jnp.arange(ATTN_GW) // ATTN_DH
    seg_ones = (head_of[:, None] == head_of[None, :]).astype(BF16)

    w_main, w_attn = _w_in_layout(w_in)

    for l in range(depth):
        proj, h_mix, gates_t = _inproj(x2d, norm_mix[l][None], w_main, layer=l, tm=INPROJ_TM,
                                       tn=INPROJ_TN)
        gq = jnp.tile(attn_qk_norm[l, 0], HEADS_PER_GROUP)[None]
        gk = jnp.tile(attn_qk_norm[l, 1], HEADS_PER_GROUP)[None]

        nh = MLSTM_HEADS
        bias_i = jnp.zeros((8, 1), F32).at[:nh, 0].set(mlstm_gate_b[l, :nh])
        bias_f = jnp.zeros((8, 1), F32).at[:nh, 0].set(mlstm_gate_b[l, nh:])
        ya = _mlstm_rows(proj, gates_t, mlstm_conv[l], bias_i, bias_f, mlstm_norm[l][None],
                         batch=b, seq=s, blk=MLSTM_BLOCK, group=MLSTM_GROUP)

        ybs, lses = [], []
        for g, (_, dilation) in enumerate(ATTN_PATTERNS):
            aproj = _attnproj(h_mix, w_attn, seg_ones, gq, gk, layer=l, group=g, dilation=dilation)
            o, lse = _dattn(aproj, biases[g], seq=s, group=g, dilation=dilation)
            ybs.append(o)
            lses.append(lse)

        ws = jnp.where(tril, gmlp_ws[l], 0.0).astype(BF16)
        bsb = jnp.broadcast_to(gmlp_bs[l][:, :, None], (GMLP_GROUPS, GMLP_CHUNK, GMLP_GC)).astype(F32)
        x2d = _merge(ya, ybs, lses, proj, x2d, w_branch_a[l].astype(BF16), w_branch_b[l].astype(BF16),
                     w_branch_c[l].astype(BF16), w_out[l].astype(BF16), ws, bsb, gmlp_norm[l][None],
                     tm=MERGE_TM)

        k_mem, v_mem = _memkv(mem, norm_mem[l][None], w_xkv[l].astype(BF16), xattn_qk_norm[l, 1][None])
        x2d, hf_rows, eidx, wts = _xattn(x2d, k_mem, v_mem, norm_xattn[l][None], w_xq[l].astype(BF16),
                                         xattn_qk_norm[l, 0][None], w_xo[l].astype(BF16), norm_ffn[l][None],
                                         rw_t, rb, seq=s, tm=XATTN_TM)

        x2d = _moe(x2d, hf_rows, eidx, wts, w_expert_gate, w_expert_up, w_expert_down, layer=l)

    return x2d.reshape(b, s, d)
```

```python
import functools
import math

import jax
import jax.numpy as jnp
import numpy as np
from jax import lax
from jax.experimental import pallas as pl
from jax.experimental.pallas import tpu as pltpu
from jax.experimental.pallas import tpu_sc as plsc

F32 = jnp.float32
BF16 = jnp.bfloat16

EPS = 1e-6
NEG = -1e30

MLSTM_HEADS = 4
MLSTM_DH = 128
MLSTM_W = MLSTM_HEADS * MLSTM_DH
CONV_WIDTH = 4
MLSTM_BLOCK = 128
MLSTM_GROUP = 4

ATTN_PATTERNS = ((128, 1), (512, 4), (2048, 16))
HEADS_PER_GROUP = 4
ATTN_DH = 64
ATTN_GW = HEADS_PER_GROUP * ATTN_DH
ATTN_W = len(ATTN_PATTERNS) * ATTN_GW
ATTN_BLOCK = 128
REL_BUCKETS = 32
REL_MAX_DIST = 2048

GMLP_GROUPS = 4
GMLP_GC = 128
GMLP_W = GMLP_GROUPS * GMLP_GC
GMLP_CHUNK = 128

XATTN_HEADS = 4
XATTN_DH = 128
XATTN_W = XATTN_HEADS * XATTN_DH
XATTN_SUB = 1024

N_EXPERTS = 16
N_EXPERT_GROUPS = 4
EXPERTS_PER_GROUP = 4
ROUTER_ROWS = 8 * N_EXPERT_GROUPS

N_BRANCH = 3

MOE_TM = 1024
ROW_CHUNKS = 4
SC_CORES, SC_SUBCORES = 2, 16
SC_WINDOW = 128

OFF_MQ, OFF_MK, OFF_MV, OFF_MO = 0, 512, 1024, 1536
OFF_GU, OFF_GV = 2048, 2560
OFF_GATE = 3072
OFF_IF = 6144
IF_PAD = 256
N_PROJ = OFF_IF + IF_PAD

ATTN_TILE = 2048
ATTN_SUB = ATTN_TILE // ATTN_BLOCK
ATTN_SLAB = 2 * ATTN_DH
ATTN_COLS = HEADS_PER_GROUP * ATTN_SLAB + 2 * ATTN_GW

VMEM_LIMIT = 48 * 1024 * 1024
VMEM_LIMIT_INPROJ = 56 * 1024 * 1024
VMEM_LIMIT_SMALL = 24 * 1024 * 1024

INPROJ_TM, INPROJ_TN = 1024, 3072
ATTNPROJ_SUB = 512
MERGE_TM = 512
XATTN_TM = 1024
COMBINE_TM = 1024
PLAN_TB = 1024


def _cparams(*sem, vmem=VMEM_LIMIT):
    return pltpu.CompilerParams(dimension_semantics=sem, vmem_limit_bytes=vmem)


def _rms(x, gain):
    return x * lax.rsqrt(jnp.mean(x * x, axis=-1, keepdims=True) + EPS) * gain


def _sigmoid(x):
    return 0.5 * jnp.tanh(0.5 * x) + 0.5


def _silu(x):
    half = 0.5 * x
    return half + half * jnp.tanh(half)


def _gelu(x):
    c = math.sqrt(2.0 / math.pi)
    half = 0.5 * x
    return half + half * jnp.tanh(x * (c + (c * 0.044715) * (x * x)))


def _dot(a, b):
    return jnp.dot(a, b, preferred_element_type=F32)


def _dot_nt(a, b):
    return lax.dot_general(a, b, (((1,), (1,)), ((), ())), preferred_element_type=F32)


def _inproj_kernel(x_ref, g_ref, w_ref, wg_ref, o_ref, h_ref, gt_ref):
    @pl.when(pl.program_id(1) == 0)
    def _():
        h = _rms(x_ref[...], g_ref[...]).astype(BF16)
        h_ref[...] = h
        gt_ref[...] = _dot_nt(wg_ref[0, 0:128, :], h)[:gt_ref.shape[0], :]

    o_ref[...] = _dot_nt(h_ref[...], w_ref[0]).astype(o_ref.dtype)


def _inproj(x2d, gain, w, *, layer, tm, tn):
    t, d = x2d.shape
    n = OFF_IF
    return pl.pallas_call(
        _inproj_kernel,
        grid=(t // tm, n // tn),
        in_specs=[pl.BlockSpec((tm, d), lambda i, j: (i, 0)),
                  pl.BlockSpec((1, d), lambda i, j: (0, 0)),
                  pl.BlockSpec((1, tn, d), lambda i, j: (layer, j, 0)),
                  pl.BlockSpec((1, IF_PAD, d), lambda i, j: (layer, OFF_IF // IF_PAD, 0))],
        out_specs=[pl.BlockSpec((tm, tn), lambda i, j: (i, j)),
                   pl.BlockSpec((tm, d), lambda i, j: (i, 0)),
                   pl.BlockSpec((8, tm), lambda i, j: (0, i))],
        out_shape=[jax.ShapeDtypeStruct((t, n), BF16), jax.ShapeDtypeStruct((t, d), BF16),
                   jax.ShapeDtypeStruct((8, t), F32)],
        compiler_params=_cparams("parallel", "arbitrary", vmem=VMEM_LIMIT_INPROJ),
        name="inproj",
    )(x2d, gain, w, w)


def _log_sigmoid(x):
    return jnp.minimum(x, 0.0) - jnp.log(1.0 + jnp.exp(-jnp.abs(x)))


def _split_bf16(x):
    hi = x.astype(BF16)
    return hi, (x - hi.astype(F32)).astype(BF16)


def _prefix_max(x):
    n = x.shape[1]
    lane = lax.broadcasted_iota(jnp.int32, x.shape, 1)
    shift = 1
    while shift < n:
        x = jnp.maximum(x, jnp.where(lane >= shift, pltpu.roll(x, shift, 1), NEG))
        shift *= 2
    return x


def _mlstm_rows_kernel(qk_ref, v_ref, og_ref, *rest, blk, group):
    gate_refs = rest[:group]
    cw_ref, bi_ref, bf_ref, ng_ref, y_ref, xe_scr, s_scr, m_scr = rest[group:]
    heads, dh, w = MLSTM_HEADS, MLSTM_DH, MLSTM_W

    @pl.when(pl.program_id(1) == 0)
    def _():
        xe_scr[:, 0:8, :] = jnp.zeros((group, 8, 2 * w), F32)
        s_scr[...] = jnp.zeros_like(s_scr)
        m_scr[...] = jnp.zeros_like(m_scr)

    cw = cw_ref[...]
    causal = lax.broadcasted_iota(jnp.int32, (blk, blk), 0) >= lax.broadcasted_iota(jnp.int32, (blk, blk), 1)
    triu = (lax.broadcasted_iota(jnp.int32, (blk, blk), 0)
            <= lax.broadcasted_iota(jnp.int32, (blk, blk), 1)).astype(BF16)
    ones = jnp.ones((blk, dh), BF16)
    s_in = [[s_scr[g, h] for h in range(heads)] for g in range(group)]
    m_in = [m_scr[g, :, 0:1] for g in range(group)]
    s_out = [[None] * heads for _ in range(group)]
    m_out = [None] * group
    per_seq = []
    for g in range(group):
        xe_scr[g, 8:8 + blk, :] = qk_ref[g].astype(F32)
        conv = cw[CONV_WIDTH - 1:CONV_WIDTH, :] * xe_scr[g, 8:8 + blk, :]
        for j in range(CONV_WIDTH - 1):
            off = 8 - (CONV_WIDTH - 1) + j
            conv = conv + cw[j:j + 1, :] * xe_scr[g, off:off + blk, :]
        xe_scr[g, 0:8, :] = xe_scr[g, blk:blk + 8, :]
        qk = _silu(conv)

        gates = gate_refs[g][...]
        i_r = gates + bi_ref[...]
        lf_hi, lf_lo = _split_bf16(_log_sigmoid(pltpu.roll(gates, heads, 0) + bf_ref[...]))
        b_r = _dot(lf_hi, triu) + _dot(lf_lo, triu)
        m_st = m_in[g]
        a_r = i_r - b_r
        inter = b_r + m_st
        m_t = jnp.maximum(inter, b_r + _prefix_max(a_r))
        b_last = b_r[:, blk - 1:blk]
        dec = b_last - b_r + i_r
        m_new = jnp.maximum(b_last + m_st, jnp.max(dec, axis=1, keepdims=True))
        w_c = jnp.exp(b_last + m_st - m_new)
        m_out[g] = m_new
        pack = jnp.concatenate([b_r - m_t, jnp.exp(inter - m_t), jnp.exp(-m_t), jnp.exp(dec - m_new),
                                jnp.zeros((blk - 32, blk), F32)], axis=0)
        per_seq.append((qk, a_r, pack.T, w_c))

    chains = [(g, h) for h in range(heads) for g in range(group)]
    st = {}
    for g, h in chains:
        qk = per_seq[g][0]
        sl = slice(h * dh, (h + 1) * dh)
        q_b = qk[:, sl].astype(BF16)
        k_f = qk[:, w + h * dh:w + (h + 1) * dh] * (dh ** -0.5)
        v_ext = jnp.concatenate([v_ref[g, :, sl], ones], axis=-1)
        st[g, h] = (q_b, k_f, v_ext, _dot_nt(q_b, k_f.astype(BF16)), _dot(q_b, s_in[g][h].astype(BF16)))
    for g, h in chains:
        q_b, k_f, v_ext, qk_t, q_state = st[g, h]
        _, a_r, cols, _ = per_seq[g]
        u_c, w_inter = cols[:, h:h + 1], cols[:, 8 + h:9 + h]
        w_intra = jnp.exp(jnp.where(causal, u_c + a_r[h:h + 1, :], NEG))
        st[g, h] = (k_f, v_ext, _dot((qk_t * w_intra).astype(BF16), v_ext) + w_inter * q_state)
    for g, h in chains:
        k_f, v_ext, tot = st[g, h]
        _, _, cols, w_c = per_seq[g]
        em_c, w_k = cols[:, 16 + h:17 + h], cols[:, 24 + h:25 + h]
        sl = slice(h * dh, (h + 1) * dh)
        num, den = tot[:, :dh], tot[:, dh:]
        hh = num / jnp.maximum(jnp.abs(den), em_c)
        hn = _rms(hh, ng_ref[:, sl])
        y_ref[g, :, sl] = (hn * _sigmoid(og_ref[g, :, sl].astype(F32))).astype(y_ref.dtype)
        s_out[g][h] = w_c[h:h + 1, :] * s_in[g][h] + _dot((k_f * w_k).T.astype(BF16), v_ext)
    for g in range(group):
        m_scr[g] = jnp.broadcast_to(m_out[g], m_scr.shape[1:])
        for h in range(heads):
            s_scr[g, h] = s_out[g][h]


def _mlstm_rows(proj, gates_t, conv_w, bias_i, bias_f, norm_g, *, batch, seq, blk, group):
    t, npj = proj.shape
    w = MLSTM_W
    proj3 = proj.reshape(batch, seq, npj)
    cols = lambda c: (lambda b, i: (b, i, c))
    const2 = lambda b, i: (0, 0)
    nblk = seq // blk
    gate_specs = [pl.BlockSpec((8, blk), functools.partial(lambda b, i, g: (0, (b * group + g) * nblk + i), g=g))
                  for g in range(group)]
    y = pl.pallas_call(
        functools.partial(_mlstm_rows_kernel, blk=blk, group=group),
        grid=(batch // group, seq // blk),
        in_specs=[pl.BlockSpec((group, blk, 2 * w), cols(OFF_MQ // (2 * w))),
                  pl.BlockSpec((group, blk, w), cols(OFF_MV // w)),
                  pl.BlockSpec((group, blk, w), cols(OFF_MO // w)),
                  *gate_specs,
                  pl.BlockSpec((CONV_WIDTH, 2 * w), const2),
                  pl.BlockSpec((8, 1), const2), pl.BlockSpec((8, 1), const2),
                  pl.BlockSpec((1, w), const2)],
        out_specs=pl.BlockSpec((group, blk, w), cols(0)),
        out_shape=jax.ShapeDtypeStruct((batch, seq, w), BF16),
        scratch_shapes=[pltpu.VMEM((group, blk + 8, 2 * w), F32),
                        pltpu.VMEM((group, MLSTM_HEADS, MLSTM_DH, 2 * MLSTM_DH), F32),
                        pltpu.VMEM((group, 8, 128), F32)],
        compiler_params=_cparams("parallel", "arbitrary", vmem=VMEM_LIMIT_SMALL),
        name="mlstm",
    )(proj3, proj3, proj3, *([gates_t] * group), conv_w, bias_i, bias_f, norm_g)
    return y.reshape(t, w)


def _attnproj_kernel(h_ref, w_ref, seg_ref, gq_ref, gk_ref, o_ref, r_scr, *, dil):
    gw, half = ATTN_GW, ATTN_SLAB // 2
    sub_rows = ATTNPROJ_SUB
    seg, sub_seg = ATTN_TILE // dil, sub_rows // dil

    def head_norm(x, gain):
        ss = _dot((x * x).astype(BF16), seg_ref[...])
        return x * lax.rsqrt(ss * (1.0 / ATTN_DH) + EPS) * gain

    low = lax.broadcasted_iota(jnp.int32, (1, ATTN_SLAB), 1) < half
    for s in range(ATTN_TILE // sub_rows):
        rows = slice(s * sub_rows, (s + 1) * sub_rows)
        res = _dot_nt(h_ref[rows, :], w_ref[0])
        q = head_norm(res[:, :gw], gq_ref[...]) * (ATTN_DH ** -0.5)
        k = head_norm(res[:, gw:2 * gw], gk_ref[...])
        slabs = []
        for pair in range(gw // ATTN_SLAB):
            qp = q[:, pair * ATTN_SLAB:(pair + 1) * ATTN_SLAB]
            slabs += [jnp.where(low, qp, 0.0), jnp.where(low, 0.0, qp)]
        slabs += [k[:, c * 128:(c + 1) * 128] for c in range(gw // 128)]
        slabs += [res[:, 2 * gw + c * 128:2 * gw + (c + 1) * 128] for c in range(gw // 128)]
        pitch = dil + 1 if dil % 16 == 0 else dil
        for c, slab in enumerate(slabs):
            if dil == 1:
                o_ref[rows, c * 128:(c + 1) * 128] = slab.astype(o_ref.dtype)
            elif pitch == dil:
                r_scr[s % 2, c, 0:sub_rows, :] = slab
            else:
                for i in range(sub_seg):
                    r_scr[s % 2, c, pitch * i:pitch * i + dil, :] = slab[dil * i:dil * (i + 1), :]
        if dil > 1:
            for r in range(dil):
                dst = slice(r * seg + s * sub_seg, r * seg + (s + 1) * sub_seg)
                for c in range(r_scr.shape[1]):
                    o_ref[dst, c * 128:(c + 1) * 128] = (
                        r_scr[s % 2, c, pl.ds(r, sub_seg, stride=pitch), :].astype(o_ref.dtype))


def _attnproj(h, w, seg_ones, gq, gk, *, layer, group, dilation):
    t, d = h.shape
    wcols = 3 * ATTN_GW
    const2 = lambda i: (0, 0)
    return pl.pallas_call(
        functools.partial(_attnproj_kernel, dil=dilation),
        grid=(t // ATTN_TILE,),
        in_specs=[pl.BlockSpec((ATTN_TILE, d), lambda i: (i, 0)),
                  pl.BlockSpec((1, wcols, d), lambda i: (layer, group, 0)),
                  pl.BlockSpec((ATTN_GW, ATTN_GW), const2),
                  pl.BlockSpec((1, ATTN_GW), const2), pl.BlockSpec((1, ATTN_GW), const2)],
        out_specs=pl.BlockSpec((ATTN_TILE, ATTN_COLS), lambda i: (i, 0)),
        out_shape=jax.ShapeDtypeStruct((t, ATTN_COLS), BF16),
        scratch_shapes=[pltpu.VMEM((2, ATTN_COLS // 128, ATTNPROJ_SUB + ATTNPROJ_SUB // 16, 128), F32)],
        compiler_params=_cparams("parallel"),
        name=f"attnproj{group}",
    )(h, w, seg_ones, gq, gk)


def _dattn_kernel(q_ref, kc_ref, kp_ref, vc_ref, vp_ref, b0_ref, o_ref, lse_ref,
                  kx_scr, vx_scr, o_scr, l_scr, bias_scr, *, dil):
    blk = ATTN_BLOCK
    per = ATTN_SUB // dil
    pitch = dil + 1 if dil % 16 == 0 else dil
    first_tile = pl.program_id(1) == 0

    @pl.when(first_tile)
    def _():
        for h in range(HEADS_PER_GROUP):
            bias_scr[h] = pltpu.roll(jnp.broadcast_to(b0_ref[h], (blk, 2 * blk)), 0, 1, stride=1, stride_axis=0)

    for r in range(dil):
        base = r * (per + 1) * blk
        last = slice((r * per + per - 1) * blk, (r * per + per) * blk)
        mine = slice(r * per * blk, (r + 1) * per * blk)
        kx_scr[base:base + blk, :] = kp_ref[last, :]
        vx_scr[base:base + blk, :] = vp_ref[last, :]
        kx_scr[base + blk:base + (per + 1) * blk, :] = kc_ref[mine, :]
        vx_scr[base + blk:base + (per + 1) * blk, :] = vc_ref[mine, :]

    low = lax.broadcasted_iota(jnp.int32, (1, ATTN_SLAB), 1) < ATTN_SLAB // 2
    no_prev = lax.broadcasted_iota(jnp.int32, (1, 2 * blk), 1) < blk
    for r in range(dil):
        for sub in range(per):
            u = r * per + sub
            win = slice((r * (per + 1) + sub) * blk, (r * (per + 1) + sub + 2) * blk)
            o_slabs, l_slabs = [], []
            for pair in range(ATTN_GW // ATTN_SLAB):
                cols = slice(pair * ATTN_SLAB, (pair + 1) * ATTN_SLAB)
                kx, vx = kx_scr[win, cols], vx_scr[win, cols]
                o_pair, l_pair = [], []
                for h in (2 * pair, 2 * pair + 1):
                    logits = _dot_nt(q_ref[u * blk:(u + 1) * blk, h * ATTN_SLAB:(h + 1) * ATTN_SLAB], kx)
                    logits = logits + bias_scr[h]
                    if sub == 0:
                        logits = jnp.where(first_tile & no_prev, NEG, logits)
                    m = jnp.max(logits, axis=-1, keepdims=True)
                    p = jnp.exp(logits - m)
                    l = jnp.sum(p, axis=-1, keepdims=True)
                    o_pair.append(_dot(p.astype(BF16), vx) / l)
                    l_pair.append(m + jnp.log(l))
                o_slabs.append(jnp.where(low, o_pair[0], o_pair[1]))
                l_slabs.append(jnp.where(low, l_pair[0], l_pair[1]))
            for c in range(ATTN_GW // ATTN_SLAB):
                cols = slice(c * ATTN_SLAB, (c + 1) * ATTN_SLAB)
                if dil == 1:
                    o_ref[u * blk:(u + 1) * blk, cols] = o_slabs[c].astype(o_ref.dtype)
                    lse_ref[u * blk:(u + 1) * blk, cols] = l_slabs[c]
                else:
                    dst = pl.ds(sub * blk * pitch + r, blk, stride=pitch)
                    o_scr[c, dst, :] = o_slabs[c]
                    l_scr[c, dst, :] = l_slabs[c]
    if dil > 1:
        for c in range(ATTN_GW // ATTN_SLAB):
            cols = slice(c * ATTN_SLAB, (c + 1) * ATTN_SLAB)
            if pitch == dil:
                o_ref[:, cols] = o_scr[c, 0:ATTN_TILE, :].astype(o_ref.dtype)
                lse_ref[:, cols] = l_scr[c, 0:ATTN_TILE, :]
            else:
                for i in range(ATTN_TILE // dil):
                    o_ref[dil * i:dil * (i + 1), cols] = o_scr[c, pitch * i:pitch * i + dil, :].astype(o_ref.dtype)
                    lse_ref[dil * i:dil * (i + 1), cols] = l_scr[c, pitch * i:pitch * i + dil, :]


def _dattn(aproj, bias, *, seq, group, dilation):
    t = aproj.shape[0]
    tiles = seq // ATTN_TILE
    qw = HEADS_PER_GROUP * ATTN_SLAB
    cq, ck, cv = 0, qw // ATTN_GW, qw // ATTN_GW + 1
    blk = (ATTN_TILE, ATTN_GW)
    cur = lambda c: (lambda b, j: (b * tiles + j, c))
    prev = lambda c: (lambda b, j: (b * tiles + jnp.maximum(j - 1, 0), c))
    xrows = ATTN_TILE + dilation * ATTN_BLOCK
    return pl.pallas_call(
        functools.partial(_dattn_kernel, dil=dilation),
        grid=(t // seq, tiles),
        in_specs=[pl.BlockSpec((ATTN_TILE, qw), cur(cq)),
                  pl.BlockSpec(blk, cur(ck)), pl.BlockSpec(blk, prev(ck)),
                  pl.BlockSpec(blk, cur(cv)), pl.BlockSpec(blk, prev(cv)),
                  pl.BlockSpec((HEADS_PER_GROUP, 1, 2 * ATTN_BLOCK), lambda b, j: (0, 0, 0))],
        out_specs=[pl.BlockSpec(blk, cur(0)), pl.BlockSpec(blk, cur(0))],
        out_shape=[jax.ShapeDtypeStruct((t, ATTN_GW), BF16), jax.ShapeDtypeStruct((t, ATTN_GW), F32)],
        scratch_shapes=[pltpu.VMEM((xrows, ATTN_GW), BF16), pltpu.VMEM((xrows, ATTN_GW), BF16),
                        pltpu.VMEM((ATTN_GW // ATTN_SLAB, ATTN_TILE + ATTN_TILE // 16, ATTN_SLAB), F32),
                        pltpu.VMEM((ATTN_GW // ATTN_SLAB, ATTN_TILE + ATTN_TILE // 16, ATTN_SLAB), F32),
                        pltpu.VMEM((HEADS_PER_GROUP, ATTN_BLOCK, 2 * ATTN_BLOCK), F32)],
        compiler_params=_cparams("parallel", "arbitrary"),
        name=f"dattn{group}",
    )(aproj, aproj, aproj, aproj, aproj, bias)


def _rel_bucket(n):
    max_exact = REL_BUCKETS // 2
    nf = jnp.maximum(n, 1).astype(F32)
    log_b = max_exact + (jnp.log(nf / max_exact) / math.log(REL_MAX_DIST / max_exact)
                         * (REL_BUCKETS - max_exact)).astype(jnp.int32)
    return jnp.where(n < max_exact, n, jnp.minimum(log_b, REL_BUCKETS - 1))


def _attn_bias(rel_bias, group):
    window, dilation = ATTN_PATTERNS[group]
    steps = window // dilation
    assert steps == ATTN_BLOCK
    hs = slice(group * HEADS_PER_GROUP, (group + 1) * HEADS_PER_GROUP)
    bucket = _rel_bucket((steps - jnp.arange(steps + 1)) * dilation)
    by_dist = jnp.dot(jax.nn.one_hot(bucket, REL_BUCKETS, dtype=F32), rel_bias[:, hs].astype(F32),
                      precision=lax.Precision.HIGHEST)
    row0 = jnp.concatenate([by_dist, jnp.full((2 * ATTN_BLOCK - steps - 1, HEADS_PER_GROUP), NEG, F32)], axis=0)
    return row0.T[:, None, :]


def _merge_kernel(ya_ref, yb0_ref, yb1_ref, yb2_ref, l0_ref, l1_ref, l2_ref, gu_ref, gv_ref, gate_ref,
                  x_ref, wa_ref, wb_ref, wc_ref, wo_ref, ws_ref, bs_ref, gg_ref, o_ref, yc_scr, *, tm):
    d = x_ref.shape[1]
    l0, l1, l2 = l0_ref[...], l1_ref[...], l2_ref[...]
    mx = jnp.maximum(jnp.maximum(l0, l1), l2)
    e0, e1, e2 = jnp.exp(l0 - mx), jnp.exp(l1 - mx), jnp.exp(l2 - mx)
    inv = 1.0 / (e0 + e1 + e2)
    yb = jnp.concatenate([(yb0_ref[...].astype(F32) * (e0 * inv)).astype(BF16),
                          (yb1_ref[...].astype(F32) * (e1 * inv)).astype(BF16),
                          (yb2_ref[...].astype(F32) * (e2 * inv)).astype(BF16)], axis=-1)

    for j in range(tm // GMLP_CHUNK):
        rows = slice(j * GMLP_CHUNK, (j + 1) * GMLP_CHUNK)
        for g in range(GMLP_GROUPS):
            cols = slice(g * GMLP_GC, (g + 1) * GMLP_GC)
            u = _gelu(gu_ref[rows, cols].astype(F32))
            v = _rms(_gelu(gv_ref[rows, cols].astype(F32)), gg_ref[:, cols])
            mixed = _dot(ws_ref[g], v.astype(BF16)) + bs_ref[g]
            yc_scr[rows, cols] = (u * mixed).astype(BF16)

    def gate2(k):
        return jnp.tanh(0.5 * gate_ref[:, k * d:(k + 1) * d].astype(F32)) + 1.0

    merged2 = gate2(0) * _dot(ya_ref[...], wa_ref[...])
    merged2 = merged2 + gate2(1) * _dot(yb, wb_ref[...])
    merged2 = merged2 + gate2(2) * _dot(yc_scr[...], wc_ref[...])
    o_ref[...] = x_ref[...] + 0.5 * _dot(merged2.astype(BF16), wo_ref[...])


def _merge(ya, ybs, lses, proj, x2d, wa, wb, wc, wo, ws, bsb, gg, *, tm):
    t, d = x2d.shape
    row = lambda c: (lambda i: (i, c))
    full2 = lambda i: (0, 0)
    full3 = lambda i: (0, 0, 0)
    gspec = pl.BlockSpec((tm, ATTN_GW), row(0))
    return pl.pallas_call(
        functools.partial(_merge_kernel, tm=tm),
        grid=(t // tm,),
        in_specs=[pl.BlockSpec((tm, MLSTM_W), row(0)),
                  gspec, gspec, gspec, gspec, gspec, gspec,
                  pl.BlockSpec((tm, GMLP_W), row(OFF_GU // GMLP_W)),
                  pl.BlockSpec((tm, GMLP_W), row(OFF_GV // GMLP_W)),
                  pl.BlockSpec((tm, N_BRANCH * d), row(OFF_GATE // (N_BRANCH * d))),
                  pl.BlockSpec((tm, d), row(0)),
                  pl.BlockSpec(wa.shape, full2), pl.BlockSpec(wb.shape, full2),
                  pl.BlockSpec(wc.shape, full2), pl.BlockSpec(wo.shape, full2),
                  pl.BlockSpec(ws.shape, full3), pl.BlockSpec(bsb.shape, full3),
                  pl.BlockSpec(gg.shape, full2)],
        out_specs=pl.BlockSpec((tm, d), row(0)),
        out_shape=jax.ShapeDtypeStruct((t, d), F32),
        scratch_shapes=[pltpu.VMEM((tm, GMLP_W), BF16)],
        compiler_params=_cparams("parallel"),
        name="merge",
    )(ya, *ybs, *lses, proj, proj, proj, x2d, wa, wb, wc, wo, ws, bsb, gg)


def _memkv(mem_ref, g_ref, w_ref, gk_ref, k_scr, v_scr):
    dh, w = XATTN_DH, XATTN_W
    kv = _dot(_rms(mem_ref[0], g_ref[...]).astype(BF16), w_ref[...])
    for h in range(XATTN_HEADS):
        sl = slice(h * dh, (h + 1) * dh)
        k_scr[:, sl] = _rms(kv[:, sl], gk_ref[...]).astype(k_scr.dtype)
    v_scr[...] = kv[:, w:].astype(v_scr.dtype)


def _route(logits):
    tm = logits.shape[1]
    e = jnp.exp(logits - jnp.max(logits, axis=0, keepdims=True))
    probs = e / jnp.sum(e, axis=0, keepdims=True)
    rowi = lax.broadcasted_iota(jnp.int32, (8, tm), 0)
    real = rowi < EXPERTS_PER_GROUP
    tops = []
    for g in range(N_EXPERT_GROUPS):
        pg = jnp.where(real, probs[8 * g:8 * g + 8, :], -0.5)
        m1 = jnp.max(pg, axis=0, keepdims=True)
        i1 = jnp.min(jnp.where(pg == m1, rowi, 8), axis=0, keepdims=True)
        pg2 = jnp.where(rowi == i1, -1.0, pg)
        m2 = jnp.max(pg2, axis=0, keepdims=True)
        i2 = jnp.min(jnp.where(pg2 == m2, rowi, 8), axis=0, keepdims=True)
        tops.append((m1, i1, m2, i2))
    best = jnp.zeros((1, tm), jnp.int32)
    best_score = tops[0][0] + tops[0][2]
    for g in range(1, N_EXPERT_GROUPS):
        score = tops[g][0] + tops[g][2]
        better = score > best_score
        best = jnp.where(better, g, best)
        best_score = jnp.where(better, score, best_score)
    m1, i1, m2, i2 = tops[0]
    for g in range(1, N_EXPERT_GROUPS):
        m1, i1, m2, i2 = (jnp.where(best == g, new, old) for new, old in zip(tops[g], (m1, i1, m2, i2)))
    tot = m1 + m2
    base = best * EXPERTS_PER_GROUP
    return base + i1, base + i2, m1 / tot, m2 / tot


def _pack_bf16_pairs(x):
    n = x.shape[1] // 2
    hi = lax.bitcast_convert_type(x[:, :n].astype(BF16).astype(F32), jnp.uint32)
    lo = lax.bitcast_convert_type(x[:, n:].astype(BF16).astype(F32), jnp.uint32)
    return hi | (lo >> 16)


def _unpack_bf16_pairs(p):
    hi = lax.bitcast_convert_type(p & jnp.uint32(0xFFFF0000), F32)
    lo = lax.bitcast_convert_type(p << 16, F32)
    return hi, lo


def _store_row_chunks(ref, packed):
    for j in range(ROW_CHUNKS):
        ref[j] = packed[:, j * 128:(j + 1) * 128]


def _load_row_chunks(ref):
    return jnp.concatenate([ref[j] for j in range(ROW_CHUNKS)], axis=-1)


def _xattn_kernel(x_ref, mem_ref, gm_ref, wkv_ref, gk_ref, gx_ref, wq_ref, gq_ref, wo_ref, gf_ref, rw_ref,
                  rb_ref, xo_ref, hf_ref, eidx_ref, wts_ref, k_scr, v_scr, *, sub):
    dh = XATTN_DH

    @pl.when(pl.program_id(1) == 0)
    def _():
        _memkv(mem_ref, gm_ref, wkv_ref, gk_ref, k_scr, v_scr)

    rw = rw_ref[...]
    rw_hi, rw_lo = _split_bf16(rw)
    for s in range(x_ref.shape[0] // sub):
        rows = slice(s * sub, (s + 1) * sub)
        x = x_ref[rows, :]
        q = _dot(_rms(x, gx_ref[...]).astype(BF16), wq_ref[...])
        outs = []
        for h in range(XATTN_HEADS):
            sl = slice(h * dh, (h + 1) * dh)
            q_h = (_rms(q[:, sl], gq_ref[...]) * (dh ** -0.5)).astype(BF16)
            logits = _dot_nt(q_h, k_scr[:, sl])
            p = jnp.exp(logits - jnp.max(logits, axis=-1, keepdims=True))
            o = _dot(p.astype(BF16), v_scr[:, sl]) / jnp.sum(p, axis=-1, keepdims=True)
            outs.append(o.astype(BF16))
        xn = x + _dot(jnp.concatenate(outs, axis=-1), wo_ref[...])
        xo_ref[rows, :] = xn
        hf = _rms(xn, gf_ref[...])
        packed = _pack_bf16_pairs(hf)
        for j in range(ROW_CHUNKS):
            hf_ref[j, rows, :] = packed[:, j * 128:(j + 1) * 128]
        hf_hi, hf_lo = _split_bf16(hf)
        logits_t = _dot_nt(rw_hi, hf_hi) + _dot_nt(rw_hi, hf_lo) + _dot_nt(rw_lo, hf_hi) + rb_ref[...]
        e1, e2, w1, w2 = _route(logits_t)
        eidx_ref[:, rows] = jnp.concatenate([e1, e2, jnp.zeros((6, sub), jnp.int32)], axis=0)
        wts_ref[:, rows] = jnp.concatenate([w1, w2, jnp.zeros((6, sub), F32)], axis=0)


def _xattn(x2d, mem, gm, w_kv, gk, gx, wq, gq, wo, gf, rw_t, rb, *, seq, tm):
    t, d = x2d.shape
    b, m, _ = mem.shape
    per_b = seq // tm
    assert t == b * seq
    full2 = lambda i, j: (0, 0)
    return pl.pallas_call(
        functools.partial(_xattn_kernel, sub=min(tm, XATTN_SUB)),
        grid=(b, per_b),
        in_specs=[pl.BlockSpec((tm, d), lambda i, j: (i * per_b + j, 0)),
                  pl.BlockSpec((1, m, d), lambda i, j: (i, 0, 0)),
                  pl.BlockSpec((1, d), full2), pl.BlockSpec(w_kv.shape, full2),
                  pl.BlockSpec((1, XATTN_DH), full2),
                  pl.BlockSpec((1, d), full2), pl.BlockSpec(wq.shape, full2),
                  pl.BlockSpec((1, XATTN_DH), full2), pl.BlockSpec(wo.shape, full2),
                  pl.BlockSpec((1, d), full2), pl.BlockSpec(rw_t.shape, full2),
                  pl.BlockSpec(rb.shape, full2)],
        out_specs=[pl.BlockSpec((tm, d), lambda i, j: (i * per_b + j, 0)),
                   pl.BlockSpec((ROW_CHUNKS, tm, 128), lambda i, j: (0, i * per_b + j, 0)),
                   pl.BlockSpec((8, tm), lambda i, j: (0, i * per_b + j)),
                   pl.BlockSpec((8, tm), lambda i, j: (0, i * per_b + j))],
        out_shape=[jax.ShapeDtypeStruct((t, d), F32),
                   jax.ShapeDtypeStruct((ROW_CHUNKS, t, 128), jnp.uint32),
                   jax.ShapeDtypeStruct((8, t), jnp.int32),
                   jax.ShapeDtypeStruct((8, t), F32)],
        scratch_shapes=[pltpu.VMEM((m, XATTN_W), BF16), pltpu.VMEM((m, XATTN_W), BF16)],
        compiler_params=_cparams("parallel", "arbitrary"),
        name="xattn_router",
    )(x2d, mem, gm, w_kv, gk, gx, wq, gq, wo, gf, rw_t, rb)


def _moe_plan_kernel(eidx_ref, i1_ref, i2_ref, te_ref, na_ref, cnt_scr, carry_scr, *, tb, tm, plane_rows):
    ne = N_EXPERTS
    hp = lax.Precision.HIGHEST
    phase, j = pl.program_id(0), pl.program_id(1)
    rows = lax.broadcasted_iota(jnp.int32, (ne, tb), 0)
    oh1 = rows == eidx_ref[0:1, :]
    oh2 = rows == eidx_ref[1:2, :]
    a = oh1.astype(F32) + oh2.astype(F32)
    blk_cnt = jnp.broadcast_to(jnp.sum(a, axis=1, keepdims=True), cnt_scr.shape)

    @pl.when((phase == 0) & (j == 0))
    def _():
        cnt_scr[...] = jnp.zeros_like(cnt_scr)

    @pl.when(phase == 0)
    def _():
        cnt_scr[...] += blk_cnt

    @pl.when((phase == 1) & (j == 0))
    def _():
        padded = jnp.ceil(cnt_scr[...] * (1.0 / tm)) * tm
        er = lax.broadcasted_iota(jnp.int32, (ne, ne), 0)
        ec = lax.broadcasted_iota(jnp.int32, (ne, ne), 1)
        off = jnp.dot((ec < er).astype(F32), padded, precision=hp, preferred_element_type=F32)
        carry_scr[...] = off
        seg_end = (off + padded)[:, 0:1]
        tile_start = lax.broadcasted_iota(jnp.int32, (ne, te_ref.shape[1]), 1).astype(F32) * tm
        te = jnp.sum((seg_end <= tile_start).astype(F32), axis=0, keepdims=True)
        te_ref[...] = jnp.broadcast_to(jnp.minimum(te, ne - 1.0), te_ref.shape).astype(jnp.int32)
        total = jnp.sum(padded[:, 0:1], axis=0, keepdims=True)
        na_ref[...] = jnp.broadcast_to(total * (1.0 / tm), na_ref.shape).astype(jnp.int32)

    @pl.when(phase == 1)
    def _():
        before = (lax.broadcasted_iota(jnp.int32, (tb, tb), 0)
                  < lax.broadcasted_iota(jnp.int32, (tb, tb), 1)).astype(BF16)
        rank = carry_scr[:, 0:1] + _dot(a.astype(BF16), before)
        d1 = jnp.sum(jnp.where(oh1, rank, 0.0), axis=0, keepdims=True).astype(jnp.int32)
        d2 = jnp.sum(jnp.where(oh2, rank, 0.0), axis=0, keepdims=True).astype(jnp.int32)
        plane = lax.broadcasted_iota(jnp.int32, (8, tb), 0) * plane_rows
        i1_ref[...] = jnp.where(plane < ROW_CHUNKS * plane_rows, plane + d1, 0)
        i2_ref[...] = jnp.where(plane < ROW_CHUNKS * plane_rows, plane + d2, 0)
        carry_scr[...] += blk_cnt


def _moe_plan(eidx, *, tm, n_tiles, tb=PLAN_TB):
    t = eidx.shape[1]
    ntp = -(-n_tiles // 128) * 128
    return pl.pallas_call(
        functools.partial(_moe_plan_kernel, tb=tb, tm=tm, plane_rows=n_tiles * tm),
        grid=(2, t // tb),
        in_specs=[pl.BlockSpec((8, tb), lambda p, j: (0, j))],
        out_specs=[pl.BlockSpec((8, tb), lambda p, j: (0, j * p)),
                   pl.BlockSpec((8, tb), lambda p, j: (0, j * p)),
                   pl.BlockSpec((8, ntp), lambda p, j: (0, 0)),
                   pl.BlockSpec((8, 128), lambda p, j: (0, 0))],
        out_shape=[jax.ShapeDtypeStruct((8, t), jnp.int32),
                   jax.ShapeDtypeStruct((8, t), jnp.int32),
                   jax.ShapeDtypeStruct((8, ntp), jnp.int32),
                   jax.ShapeDtypeStruct((8, 128), jnp.int32)],
        scratch_shapes=[pltpu.VMEM((N_EXPERTS, 128), F32), pltpu.VMEM((N_EXPERTS, 128), F32)],
        compiler_params=_cparams("arbitrary", "arbitrary", vmem=VMEM_LIMIT_SMALL),
        name="moe_plan",
    )(eidx)


def _sc_mesh():
    return plsc.VectorSubcoreMesh(core_axis_name="c", subcore_axis_name="s",
                                  num_cores=SC_CORES, num_subcores=SC_SUBCORES)


def _sc_index_spec(tokens):
    nb = tokens // SC_WINDOW
    return pl.BlockSpec((1, SC_WINDOW), lambda i: (i // nb, i % nb))


def _sc_dispatch(rows, i1, i2, n_out):
    n = rows.shape[0]
    tokens = i1.shape[1]

    @functools.partial(pl.kernel, out_type=jax.ShapeDtypeStruct((n_out, 128), rows.dtype), mesh=_sc_mesh(),
                       name="moe_dispatch")
    def k(x_hbm, i1_hbm, i2_hbm, o_hbm):
        def body(x_vmem, i1_vmem, i2_vmem):
            pltpu.sync_copy(x_vmem, o_hbm.at[i1_vmem.at[0]])
            pltpu.sync_copy(x_vmem, o_hbm.at[i2_vmem.at[0]])

        pltpu.emit_pipeline(
            body, grid=(n // SC_WINDOW,),
            in_specs=[pl.BlockSpec((SC_WINDOW, 128), lambda i: (i, 0)),
                      _sc_index_spec(tokens), _sc_index_spec(tokens)],
            out_specs=[],
            core_axis_name=("c", "s"), dimension_semantics=(pltpu.PARALLEL,),
        )(x_hbm, i1_hbm, i2_hbm)

    return k(rows, i1, i2)


def _sc_collect(table, i1, i2):
    tokens = i1.shape[1]
    n = ROW_CHUNKS * tokens
    out = jax.ShapeDtypeStruct((n, 128), table.dtype)

    @functools.partial(pl.kernel, out_type=(out, out), mesh=_sc_mesh(), name="moe_collect",
                       scratch_types=[pltpu.SemaphoreType.DMA, pltpu.SemaphoreType.DMA])
    def k(t_hbm, i1_hbm, i2_hbm, o1_hbm, o2_hbm, sem1, sem2):
        def body(i1_vmem, i2_vmem, o1_vmem, o2_vmem):
            first = pltpu.async_copy(t_hbm.at[i1_vmem.at[0]], o1_vmem, sem1)
            second = pltpu.async_copy(t_hbm.at[i2_vmem.at[0]], o2_vmem, sem2)
            first.wait()
            second.wait()

        pltpu.emit_pipeline(
            body, grid=(n // SC_WINDOW,),
            in_specs=[_sc_index_spec(tokens), _sc_index_spec(tokens)],
            out_specs=[pl.BlockSpec((SC_WINDOW, 128), lambda i: (i, 0)),
                       pl.BlockSpec((SC_WINDOW, 128), lambda i: (i, 0))],
            core_axis_name=("c", "s"), dimension_semantics=(pltpu.PARALLEL,),
        )(i1_hbm, i2_hbm, o1_hbm, o2_hbm)

    return k(table, i1, i2)


def _experts_kernel(te_ref, na_ref, xs_ref, wg_ref, wu_ref, wd_ref, y_ref, wg_scr, wu_scr, wd_scr):
    i = pl.program_id(0)
    active = i < na_ref[0]

    @pl.when(active & ((i == 0) | (te_ref[i] != te_ref[jnp.maximum(i - 1, 0)])))
    def _():
        wg_scr[...] = wg_ref[0, 0].astype(BF16)
        wu_scr[...] = wu_ref[0, 0].astype(BF16)
        wd_scr[...] = wd_ref[0, 0].astype(BF16)

    @pl.when(active)
    def _():
        hi, lo = _unpack_bf16_pairs(_load_row_chunks(xs_ref))
        h = jnp.concatenate([hi, lo], axis=-1).astype(BF16)
        up = _dot(h, wg_scr[...])
        act = _silu(up) * _dot(h, wu_scr[...])
        _store_row_chunks(y_ref, _pack_bf16_pairs(_dot(act.astype(BF16), wd_scr[...])))


def _experts(tile_expert, n_active, xs, wg, wu, wd, *, layer, tm):
    n_tiles = tile_expert.shape[0]
    _, _, d, dff = wg.shape
    rows = lambda i, te, na: (0, jnp.minimum(i, na[0] - 1), 0)
    expert = lambda i, te, na: (layer, te[i], 0, 0)
    return pl.pallas_call(
        _experts_kernel,
        grid_spec=pltpu.PrefetchScalarGridSpec(
            num_scalar_prefetch=2,
            grid=(n_tiles,),
            in_specs=[pl.BlockSpec((ROW_CHUNKS, tm, 128), rows),
                      pl.BlockSpec((1, 1, d, dff), expert),
                      pl.BlockSpec((1, 1, d, dff), expert),
                      pl.BlockSpec((1, 1, dff, d), expert)],
            out_specs=pl.BlockSpec((ROW_CHUNKS, tm, 128), rows),
            scratch_shapes=[pltpu.VMEM((d, dff), BF16), pltpu.VMEM((d, dff), BF16), pltpu.VMEM((dff, d), BF16)]),
        out_shape=jax.ShapeDtypeStruct(xs.shape, xs.dtype),
        compiler_params=_cparams("arbitrary"),
        name="moe_experts",
    )(tile_expert, n_active, xs, wg, wu, wd)


def _moe_combine_kernel(x_ref, y1_ref, y2_ref, w_ref, o_ref):
    half = x_ref.shape[1] // 2
    hi1, lo1 = _unpack_bf16_pairs(_load_row_chunks(y1_ref))
    hi2, lo2 = _unpack_bf16_pairs(_load_row_chunks(y2_ref))
    tm = x_ref.shape[0]
    w_cols = jnp.concatenate([w_ref[...], jnp.zeros((128 - w_ref.shape[0], tm), F32)], axis=0).T
    w1, w2 = w_cols[:, 0:1], w_cols[:, 1:2]
    o_ref[:, :half] = x_ref[:, :half] + w1 * hi1 + w2 * hi2
    o_ref[:, half:] = x_ref[:, half:] + w1 * lo1 + w2 * lo2


def _moe_combine(x2d, y1, y2, wts, *, tm):
    t, d = x2d.shape
    chunk_spec = pl.BlockSpec((ROW_CHUNKS, tm, 128), lambda i: (0, i, 0))
    return pl.pallas_call(
        _moe_combine_kernel,
        grid=(t // tm,),
        in_specs=[pl.BlockSpec((tm, d), lambda i: (i, 0)), chunk_spec, chunk_spec,
                  pl.BlockSpec((wts.shape[0], tm), lambda i: (0, i))],
        out_specs=pl.BlockSpec((tm, d), lambda i: (i, 0)),
        out_shape=jax.ShapeDtypeStruct((t, d), F32),
        compiler_params=_cparams("parallel"),
        name="moe_combine",
    )(x2d, y1, y2, wts)


def _moe(x2d, hf_rows, eidx, wts, wg, wu, wd, *, layer):
    t = x2d.shape[0]
    tm = MOE_TM
    n_tiles = 2 * t // tm + N_EXPERTS
    plane = n_tiles * tm
    i1, i2, te, na = _moe_plan(eidx, tm=tm, n_tiles=n_tiles)
    xs = _sc_dispatch(hf_rows.reshape(ROW_CHUNKS * t, 128), i1, i2, ROW_CHUNKS * plane)
    ys = _experts(te[0, :n_tiles], na[0, :1], xs.reshape(ROW_CHUNKS, plane, 128), wg, wu, wd,
                  layer=layer, tm=tm)
    y1, y2 = _sc_collect(ys.reshape(ROW_CHUNKS * plane, 128), i1, i2)
    return _moe_combine(x2d, y1.reshape(ROW_CHUNKS, t, 128), y2.reshape(ROW_CHUNKS, t, 128), wts,
                        tm=COMBINE_TM)


W_ROWS = 256


def _w_rows_kernel(start_ref, valid_ref, w_ref, o_ref):
    del start_ref
    row = lax.broadcasted_iota(jnp.int32, w_ref.shape, 1)
    o_ref[...] = jnp.where(row < valid_ref[pl.program_id(0)], w_ref[...], 0.0).astype(o_ref.dtype)


def _w_rows(w_t, starts, valid):
    depth, _, d = w_t.shape
    nblk = len(starts)
    return pl.pallas_call(
        _w_rows_kernel,
        grid_spec=pltpu.PrefetchScalarGridSpec(
            num_scalar_prefetch=2,
            grid=(nblk,),
            in_specs=[pl.BlockSpec((pl.Element(depth), pl.Element(W_ROWS), pl.Element(d)),
                                   lambda c, st, va: (0, pl.multiple_of(st[c], 8), 0))],
            out_specs=pl.BlockSpec((depth, W_ROWS, d), lambda c, st, va: (0, c, 0))),
        out_shape=jax.ShapeDtypeStruct((depth, nblk * W_ROWS, d), BF16),
        compiler_params=_cparams("arbitrary", vmem=VMEM_LIMIT_SMALL),
        name="w_in_rows",
    )(jnp.asarray(starts, jnp.int32), jnp.asarray(valid, jnp.int32), w_t)


def _w_in_layout(w_in):
    w_t = jnp.swapaxes(w_in, 1, 2)
    src_if = 4 * MLSTM_W
    src_a = src_if + 2 * MLSTM_HEADS
    src_g = src_a + 3 * ATTN_W
    starts = list(range(0, src_if, W_ROWS)) + [src_g + k * W_ROWS for k in range((OFF_IF - OFF_GU) // W_ROWS)]
    valid = [W_ROWS] * len(starts)
    starts.append(src_if)
    valid.append(2 * MLSTM_HEADS)
    assert len(starts) * W_ROWS == N_PROJ and ATTN_GW == W_ROWS
    a_starts = [src_a + j * ATTN_W + g * ATTN_GW for g in range(len(ATTN_PATTERNS)) for j in range(3)]
    return _w_rows(w_t, starts, valid), _w_rows(w_t, a_starts, [W_ROWS] * len(a_starts))


def kernel(x, mem, norm_mix, w_in, mlstm_conv, mlstm_gate_b, mlstm_norm, attn_qk_norm, gmlp_norm, gmlp_ws,
           gmlp_bs, w_branch_a, w_branch_b, w_branch_c, w_out, rel_bias, norm_xattn, norm_mem, w_xq, w_xkv,
           xattn_qk_norm, w_xo, norm_ffn, router_w, router_b, w_expert_gate, w_expert_up, w_expert_down):
    b, s, d = x.shape
    t = b * s
    depth = w_in.shape[0]
    assert d == 2 * ROW_CHUNKS * 128 and OFF_IF == OFF_GATE + N_BRANCH * d
    assert s % ATTN_TILE == 0 and s % MLSTM_BLOCK == 0 and b % MLSTM_GROUP == 0
    assert all(window == dil * ATTN_BLOCK and ATTN_SUB % dil == 0 for window, dil in ATTN_PATTERNS)
    assert t % max(INPROJ_TM, MERGE_TM, XATTN_TM, COMBINE_TM, MOE_TM, PLAN_TB) == 0 and s % XATTN_TM == 0
    x2d = x.reshape(t, d)

    biases = [_attn_bias(rel_bias, g) for g in range(len(ATTN_PATTERNS))]
    rw_t = jnp.zeros((N_EXPERT_GROUPS, 8, d), F32).at[:, :EXPERTS_PER_GROUP].set(
        router_w.T.reshape(N_EXPERT_GROUPS, EXPERTS_PER_GROUP, d)).reshape(ROUTER_ROWS, d)
    rb = jnp.full((N_EXPERT_GROUPS, 8), NEG, F32).at[:, :EXPERTS_PER_GROUP].set(
        router_b.astype(F32).reshape(N_EXPERT_GROUPS, EXPERTS_PER_GROUP)).reshape(ROUTER_ROWS, 1)
    tril = jnp.tril(jnp.ones((GMLP_CHUNK, GMLP_CHUNK), bool))
    head_of = jnp.arange(ATTN_GW) // ATTN_DH
    seg_ones = (head_of[:, None] == head_of[None, :]).astype(BF16)

    w_main, w_attn = _w_in_layout(w_in)

    for l in range(depth):
        proj, h_mix, gates_t = _inproj(x2d, norm_mix[l][None], w_main, layer=l, tm=INPROJ_TM,
                                       tn=INPROJ_TN)
        gq = jnp.tile(attn_qk_norm[l, 0], HEADS_PER_GROUP)[None]
        gk = jnp.tile(attn_qk_norm[l, 1], HEADS_PER_GROUP)[None]

        nh = MLSTM_HEADS
        bias_i = jnp.zeros((8, 1), F32).at[:nh, 0].set(mlstm_gate_b[l, :nh])
        bias_f = jnp.zeros((8, 1), F32).at[:nh, 0].set(mlstm_gate_b[l, nh:])
        ya = _mlstm_rows(proj, gates_t, mlstm_conv[l], bias_i, bias_f, mlstm_norm[l][None],
                         batch=b, seq=s, blk=MLSTM_BLOCK, group=MLSTM_GROUP)

        ybs, lses = [], []
        for g, (_, dilation) in enumerate(ATTN_PATTERNS):
            aproj = _attnproj(h_mix, w_attn, seg_ones, gq, gk, layer=l, group=g, dilation=dilation)
            o, lse = _dattn(aproj, biases[g], seq=s, group=g, dilation=dilation)
            ybs.append(o)
            lses.append(lse)

        ws = jnp.where(tril, gmlp_ws[l], 0.0).astype(BF16)
        bsb = jnp.broadcast_to(gmlp_bs[l][:, :, None], (GMLP_GROUPS, GMLP_CHUNK, GMLP_GC)).astype(F32)
        x2d = _merge(ya, ybs, lses, proj, x2d, w_branch_a[l].astype(BF16), w_branch_b[l].astype(BF16),
                     w_branch_c[l].astype(BF16), w_out[l].astype(BF16), ws, bsb, gmlp_norm[l][None],
                     tm=MERGE_TM)

        x2d, hf_rows, eidx, wts = _xattn(x2d, mem, norm_mem[l][None], w_xkv[l].astype(BF16),
                                         xattn_qk_norm[l, 1][None], norm_xattn[l][None], w_xq[l].astype(BF16),
                                         xattn_qk_norm[l, 0][None], w_xo[l].astype(BF16), norm_ffn[l][None],
                                         rw_t, rb, seq=s, tm=XATTN_TM)

        x2d = _moe(x2d, hf_rows, eidx, wts, w_expert_gate, w_expert_up, w_expert_down, layer=l)

    return x2d.reshape(b, s, d)
```

```python
import functools
import math

import jax
import jax.numpy as jnp
import numpy as np
from jax import lax
from jax.experimental import pallas as pl
from jax.experimental.pallas import tpu as pltpu
from jax.experimental.pallas import tpu_sc as plsc

F32 = jnp.float32
BF16 = jnp.bfloat16

EPS = 1e-6
NEG = -1e30

MLSTM_HEADS = 4
MLSTM_DH = 128
MLSTM_W = MLSTM_HEADS * MLSTM_DH
CONV_WIDTH = 4
MLSTM_BLOCK = 128
MLSTM_GROUP = 4

ATTN_PATTERNS = ((128, 1), (512, 4), (2048, 16))
HEADS_PER_GROUP = 4
ATTN_DH = 64
ATTN_GW = HEADS_PER_GROUP * ATTN_DH
ATTN_W = len(ATTN_PATTERNS) * ATTN_GW
ATTN_BLOCK = 128
REL_BUCKETS = 32
REL_MAX_DIST = 2048

GMLP_GROUPS = 4
GMLP_GC = 128
GMLP_W = GMLP_GROUPS * GMLP_GC
GMLP_CHUNK = 128

XATTN_HEADS = 4
XATTN_DH = 128
XATTN_W = XATTN_HEADS * XATTN_DH
XATTN_SUB = 1024

N_EXPERTS = 16
N_EXPERT_GROUPS = 4
EXPERTS_PER_GROUP = 4
ROUTER_ROWS = 8 * N_EXPERT_GROUPS

N_BRANCH = 3

MOE_TM = 1024
ROW_CHUNKS = 4
SC_CORES, SC_SUBCORES = 2, 16
SC_WINDOW = 128

OFF_MQ, OFF_MK, OFF_MV, OFF_MO = 0, 512, 1024, 1536
OFF_GU, OFF_GV = 2048, 2560
OFF_GATE = 3072
OFF_IF = 6144
IF_PAD = 256
N_PROJ = OFF_IF + IF_PAD

ATTN_TILE = 2048
ATTN_SUB = ATTN_TILE // ATTN_BLOCK
ATTN_SLAB = 2 * ATTN_DH
ATTN_COLS = HEADS_PER_GROUP * ATTN_SLAB + 2 * ATTN_GW

VMEM_LIMIT = 48 * 1024 * 1024
VMEM_LIMIT_INPROJ = 56 * 1024 * 1024
VMEM_LIMIT_SMALL = 24 * 1024 * 1024

INPROJ_TM, INPROJ_TN = 1024, 3072
ATTNPROJ_SUB = 512
MERGE_TM = 512
XATTN_TM = 1024
COMBINE_TM = 1024
PLAN_TB = 1024


def _cparams(*sem, vmem=VMEM_LIMIT):
    return pltpu.CompilerParams(dimension_semantics=sem, vmem_limit_bytes=vmem)


def _rms(x, gain):
    return x * lax.rsqrt(jnp.mean(x * x, axis=-1, keepdims=True) + EPS) * gain


def _sigmoid(x):
    return 0.5 * jnp.tanh(0.5 * x) + 0.5


def _silu(x):
    half = 0.5 * x
    return half + half * jnp.tanh(half)


def _gelu(x):
    c = math.sqrt(2.0 / math.pi)
    half = 0.5 * x
    return half + half * jnp.tanh(x * (c + (c * 0.044715) * (x * x)))


def _dot(a, b):
    return jnp.dot(a, b, preferred_element_type=F32)


def _dot_nt(a, b):
    return lax.dot_general(a, b, (((1,), (1,)), ((), ())), preferred_element_type=F32)


def _inproj_kernel(x_ref, g_ref, w_ref, wg_ref, o_ref, h_ref, gt_ref):
    @pl.when(pl.program_id(1) == 0)
    def _():
        h = _rms(x_ref[...], g_ref[...]).astype(BF16)
        h_ref[...] = h
        gt_ref[...] = _dot_nt(wg_ref[0, 0:128, :], h)[:gt_ref.shape[0], :]

    o_ref[...] = _dot_nt(h_ref[...], w_ref[0]).astype(o_ref.dtype)


def _inproj(x2d, gain, w, *, layer, tm, tn):
    t, d = x2d.shape
    n = OFF_IF
    return pl.pallas_call(
        _inproj_kernel,
        grid=(t // tm, n // tn),
        in_specs=[pl.BlockSpec((tm, d), lambda i, j: (i, 0)),
                  pl.BlockSpec((1, d), lambda i, j: (0, 0)),
                  pl.BlockSpec((1, tn, d), lambda i, j: (layer, j, 0)),
                  pl.BlockSpec((1, IF_PAD, d), lambda i, j: (layer, OFF_IF // IF_PAD, 0))],
        out_specs=[pl.BlockSpec((tm, tn), lambda i, j: (i, j)),
                   pl.BlockSpec((tm, d), lambda i, j: (i, 0)),
                   pl.BlockSpec((8, tm), lambda i, j: (0, i))],
        out_shape=[jax.ShapeDtypeStruct((t, n), BF16), jax.ShapeDtypeStruct((t, d), BF16),
                   jax.ShapeDtypeStruct((8, t), F32)],
        compiler_params=_cparams("parallel", "arbitrary", vmem=VMEM_LIMIT_INPROJ),
        name="inproj",
    )(x2d, gain, w, w)


def _log_sigmoid(x):
    return jnp.minimum(x, 0.0) - jnp.log(1.0 + jnp.exp(-jnp.abs(x)))


def _split_bf16(x):
    hi = x.astype(BF16)
    return hi, (x - hi.astype(F32)).astype(BF16)


def _prefix_max(x):
    n = x.shape[1]
    lane = lax.broadcasted_iota(jnp.int32, x.shape, 1)
    shift = 1
    while shift < n:
        x = jnp.maximum(x, jnp.where(lane >= shift, pltpu.roll(x, shift, 1), NEG))
        shift *= 2
    return x


def _mlstm_rows_kernel(qk_ref, v_ref, og_ref, *rest, blk, group):
    gate_refs = rest[:group]
    cw_ref, bi_ref, bf_ref, ng_ref, y_ref, xe_scr, s_scr, m_scr = rest[group:]
    heads, dh, w = MLSTM_HEADS, MLSTM_DH, MLSTM_W

    @pl.when(pl.program_id(1) == 0)
    def _():
        xe_scr[:, 0:8, :] = jnp.zeros((group, 8, 2 * w), F32)
        s_scr[...] = jnp.zeros_like(s_scr)
        m_scr[...] = jnp.zeros_like(m_scr)

    cw = cw_ref[...]
    causal = lax.broadcasted_iota(jnp.int32, (blk, blk), 0) >= lax.broadcasted_iota(jnp.int32, (blk, blk), 1)
    triu = (lax.broadcasted_iota(jnp.int32, (blk, blk), 0)
            <= lax.broadcasted_iota(jnp.int32, (blk, blk), 1)).astype(BF16)
    ones = jnp.ones((blk, dh), BF16)
    s_in = [[s_scr[g, h] for h in range(heads)] for g in range(group)]
    m_in = [m_scr[g, :, 0:1] for g in range(group)]
    s_out = [[None] * heads for _ in range(group)]
    m_out = [None] * group
    per_seq = []
    for g in range(group):
        xe_scr[g, 8:8 + blk, :] = qk_ref[g].astype(F32)
        conv = cw[CONV_WIDTH - 1:CONV_WIDTH, :] * xe_scr[g, 8:8 + blk, :]
        for j in range(CONV_WIDTH - 1):
            off = 8 - (CONV_WIDTH - 1) + j
            conv = conv + cw[j:j + 1, :] * xe_scr[g, off:off + blk, :]
        xe_scr[g, 0:8, :] = xe_scr[g, blk:blk + 8, :]
        qk = _silu(conv)

        gates = gate_refs[g][...]
        i_r = gates + bi_ref[...]
        lf_hi, lf_lo = _split_bf16(_log_sigmoid(pltpu.roll(gates, heads, 0) + bf_ref[...]))
        b_r = _dot(lf_hi, triu) + _dot(lf_lo, triu)
        m_st = m_in[g]
        a_r = i_r - b_r
        inter = b_r + m_st
        m_t = jnp.maximum(inter, b_r + _prefix_max(a_r))
        b_last = b_r[:, blk - 1:blk]
        dec = b_last - b_r + i_r
        m_new = jnp.maximum(b_last + m_st, jnp.max(dec, axis=1, keepdims=True))
        w_c = jnp.exp(b_last + m_st - m_new)
        m_out[g] = m_new
        pack = jnp.concatenate([b_r - m_t, jnp.exp(inter - m_t), jnp.exp(-m_t), jnp.exp(dec - m_new),
                                jnp.zeros((blk - 32, blk), F32)], axis=0)
        per_seq.append((qk, a_r, pack.T, w_c))

    chains = [(g, h) for h in range(heads) for g in range(group)]
    st = {}
    for g, h in chains:
        qk = per_seq[g][0]
        sl = slice(h * dh, (h + 1) * dh)
        q_b = qk[:, sl].astype(BF16)
        k_f = qk[:, w + h * dh:w + (h + 1) * dh] * (dh ** -0.5)
        v_ext = jnp.concatenate([v_ref[g, :, sl], ones], axis=-1)
        st[g, h] = (q_b, k_f, v_ext, _dot_nt(q_b, k_f.astype(BF16)), _dot(q_b, s_in[g][h].astype(BF16)))
    for g, h in chains:
        q_b, k_f, v_ext, qk_t, q_state = st[g, h]
        _, a_r, cols, _ = per_seq[g]
        u_c, w_inter = cols[:, h:h + 1], cols[:, 8 + h:9 + h]
        w_intra = jnp.exp(jnp.where(causal, u_c + a_r[h:h + 1, :], NEG))
        st[g, h] = (k_f, v_ext, _dot((qk_t * w_intra).astype(BF16), v_ext) + w_inter * q_state)
    for g, h in chains:
        k_f, v_ext, tot = st[g, h]
        _, _, cols, w_c = per_seq[g]
        em_c, w_k = cols[:, 16 + h:17 + h], cols[:, 24 + h:25 + h]
        sl = slice(h * dh, (h + 1) * dh)
        num, den = tot[:, :dh], tot[:, dh:]
        hh = num / jnp.maximum(jnp.abs(den), em_c)
        hn = _rms(hh, ng_ref[:, sl])
        y_ref[g, :, sl] = (hn * _sigmoid(og_ref[g, :, sl].astype(F32))).astype(y_ref.dtype)
        s_out[g][h] = w_c[h:h + 1, :] * s_in[g][h] + _dot((k_f * w_k).T.astype(BF16), v_ext)
    for g in range(group):
        m_scr[g] = jnp.broadcast_to(m_out[g], m_scr.shape[1:])
        for h in range(heads):
            s_scr[g, h] = s_out[g][h]


def _mlstm_rows(proj, gates_t, conv_w, bias_i, bias_f, norm_g, *, batch, seq, blk, group):
    t, npj = proj.shape
    w = MLSTM_W
    proj3 = proj.reshape(batch, seq, npj)
    cols = lambda c: (lambda b, i: (b, i, c))
    const2 = lambda b, i: (0, 0)
    nblk = seq // blk
    gate_specs = [pl.BlockSpec((8, blk), functools.partial(lambda b, i, g: (0, (b * group + g) * nblk + i), g=g))
                  for g in range(group)]
    y = pl.pallas_call(
        functools.partial(_mlstm_rows_kernel, blk=blk, group=group),
        grid=(batch // group, seq // blk),
        in_specs=[pl.BlockSpec((group, blk, 2 * w), cols(OFF_MQ // (2 * w))),
                  pl.BlockSpec((group, blk, w), cols(OFF_MV // w)),
                  pl.BlockSpec((group, blk, w), cols(OFF_MO // w)),
                  *gate_specs,
                  pl.BlockSpec((CONV_WIDTH, 2 * w), const2),
                  pl.BlockSpec((8, 1), const2), pl.BlockSpec((8, 1), const2),
                  pl.BlockSpec((1, w), const2)],
        out_specs=pl.BlockSpec((group, blk, w), cols(0)),
        out_shape=jax.ShapeDtypeStruct((batch, seq, w), BF16),
        scratch_shapes=[pltpu.VMEM((group, blk + 8, 2 * w), F32),
                        pltpu.VMEM((group, MLSTM_HEADS, MLSTM_DH, 2 * MLSTM_DH), F32),
                        pltpu.VMEM((group, 8, 128), F32)],
        compiler_params=_cparams("parallel", "arbitrary", vmem=VMEM_LIMIT_SMALL),
        name="mlstm",
    )(proj3, proj3, proj3, *([gates_t] * group), conv_w, bias_i, bias_f, norm_g)
    return y.reshape(t, w)


def _attnproj_kernel(h_ref, w_ref, seg_ref, gq_ref, gk_ref, o_ref, r_scr, *, dil):
    gw, half = ATTN_GW, ATTN_SLAB // 2
    sub_rows = ATTNPROJ_SUB
    seg, sub_seg = ATTN_TILE // dil, sub_rows // dil

    def head_norm(x, gain):
        ss = _dot((x * x).astype(BF16), seg_ref[...])
        return x * lax.rsqrt(ss * (1.0 / ATTN_DH) + EPS) * gain

    low = lax.broadcasted_iota(jnp.int32, (1, ATTN_SLAB), 1) < half
    for s in range(ATTN_TILE // sub_rows):
        rows = slice(s * sub_rows, (s + 1) * sub_rows)
        res = _dot_nt(h_ref[rows, :], w_ref[0])
        q = head_norm(res[:, :gw], gq_ref[...]) * (ATTN_DH ** -0.5)
        k = head_norm(res[:, gw:2 * gw], gk_ref[...])
        slabs = []
        for pair in range(gw // ATTN_SLAB):
            qp = q[:, pair * ATTN_SLAB:(pair + 1) * ATTN_SLAB]
            slabs += [jnp.where(low, qp, 0.0), jnp.where(low, 0.0, qp)]
        slabs += [k[:, c * 128:(c + 1) * 128] for c in range(gw // 128)]
        slabs += [res[:, 2 * gw + c * 128:2 * gw + (c + 1) * 128] for c in range(gw // 128)]
        pitch = dil + 1 if dil % 16 == 0 else dil
        for c, slab in enumerate(slabs):
            if dil == 1:
                o_ref[rows, c * 128:(c + 1) * 128] = slab.astype(o_ref.dtype)
            elif pitch == dil:
                r_scr[s % 2, c, 0:sub_rows, :] = slab
            else:
                for i in range(sub_seg):
                    r_scr[s % 2, c, pitch * i:pitch * i + dil, :] = slab[dil * i:dil * (i + 1), :]
        if dil > 1:
            for r in range(dil):
                dst = slice(r * seg + s * sub_seg, r * seg + (s + 1) * sub_seg)
                for c in range(r_scr.shape[1]):
                    o_ref[dst, c * 128:(c + 1) * 128] = (
                        r_scr[s % 2, c, pl.ds(r, sub_seg, stride=pitch), :].astype(o_ref.dtype))


def _attnproj(h, w, seg_ones, gq, gk, *, layer, group, dilation):
    t, d = h.shape
    wcols = 3 * ATTN_GW
    const2 = lambda i: (0, 0)
    return pl.pallas_call(
        functools.partial(_attnproj_kernel, dil=dilation),
        grid=(t // ATTN_TILE,),
        in_specs=[pl.BlockSpec((ATTN_TILE, d), lambda i: (i, 0)),
                  pl.BlockSpec((1, wcols, d), lambda i: (layer, group, 0)),
                  pl.BlockSpec((ATTN_GW, ATTN_GW), const2),
                  pl.BlockSpec((1, ATTN_GW), const2), pl.BlockSpec((1, ATTN_GW), const2)],
        out_specs=pl.BlockSpec((ATTN_TILE, ATTN_COLS), lambda i: (i, 0)),
        out_shape=jax.ShapeDtypeStruct((t, ATTN_COLS), BF16),
        scratch_shapes=[pltpu.VMEM((2, ATTN_COLS // 128, ATTNPROJ_SUB + ATTNPROJ_SUB // 16, 128), F32)],
        compiler_params=_cparams("parallel"),
        name=f"attnproj{group}",
    )(h, w, seg_ones, gq, gk)


def _dattn_kernel(q_ref, kc_ref, kp_ref, vc_ref, vp_ref, b0_ref, o_ref, lse_ref,
                  kx_scr, vx_scr, o_scr, l_scr, bias_scr, *, dil):
    blk = ATTN_BLOCK
    per = ATTN_SUB // dil
    pitch = dil + 1 if dil % 16 == 0 else dil
    first_tile = pl.program_id(1) == 0

    @pl.when(first_tile)
    def _():
        for h in range(HEADS_PER_GROUP):
            bias_scr[h] = pltpu.roll(jnp.broadcast_to(b0_ref[h], (blk, 2 * blk)), 0, 1, stride=1, stride_axis=0)

    for r in range(dil):
        base = r * (per + 1) * blk
        last = slice((r * per + per - 1) * blk, (r * per + per) * blk)
        mine = slice(r * per * blk, (r + 1) * per * blk)
        kx_scr[base:base + blk, :] = kp_ref[last, :]
        vx_scr[base:base + blk, :] = vp_ref[last, :]
        kx_scr[base + blk:base + (per + 1) * blk, :] = kc_ref[mine, :]
        vx_scr[base + blk:base + (per + 1) * blk, :] = vc_ref[mine, :]

    low = lax.broadcasted_iota(jnp.int32, (1, ATTN_SLAB), 1) < ATTN_SLAB // 2
    no_prev = lax.broadcasted_iota(jnp.int32, (1, 2 * blk), 1) < blk
    for r in range(dil):
        for sub in range(per):
            u = r * per + sub
            win = slice((r * (per + 1) + sub) * blk, (r * (per + 1) + sub + 2) * blk)
            o_slabs, l_slabs = [], []
            for pair in range(ATTN_GW // ATTN_SLAB):
                cols = slice(pair * ATTN_SLAB, (pair + 1) * ATTN_SLAB)
                kx, vx = kx_scr[win, cols], vx_scr[win, cols]
                o_pair, l_pair = [], []
                for h in (2 * pair, 2 * pair + 1):
                    logits = _dot_nt(q_ref[u * blk:(u + 1) * blk, h * ATTN_SLAB:(h + 1) * ATTN_SLAB], kx)
                    logits = logits + bias_scr[h]
                    if sub == 0:
                        logits = jnp.where(first_tile & no_prev, NEG, logits)
                    m = jnp.max(logits, axis=-1, keepdims=True)
                    p = jnp.exp(logits - m)
                    l = jnp.sum(p, axis=-1, keepdims=True)
                    o_pair.append(_dot(p.astype(BF16), vx) / l)
                    l_pair.append(m + jnp.log(l))
                o_slabs.append(jnp.where(low, o_pair[0], o_pair[1]))
                l_slabs.append(jnp.where(low, l_pair[0], l_pair[1]))
            for c in range(ATTN_GW // ATTN_SLAB):
                cols = slice(c * ATTN_SLAB, (c + 1) * ATTN_SLAB)
                if dil == 1:
                    o_ref[u * blk:(u + 1) * blk, cols] = o_slabs[c].astype(o_ref.dtype)
                    lse_ref[u * blk:(u + 1) * blk, cols] = l_slabs[c]
                else:
                    dst = pl.ds(sub * blk * pitch + r, blk, stride=pitch)
                    o_scr[c, dst, :] = o_slabs[c]
                    l_scr[c, dst, :] = l_slabs[c]
    if dil > 1:
        for c in range(ATTN_GW // ATTN_SLAB):
            cols = slice(c * ATTN_SLAB, (c + 1) * ATTN_SLAB)
            if pitch == dil:
                o_ref[:, cols] = o_scr[c, 0:ATTN_TILE, :].astype(o_ref.dtype)
                lse_ref[:, cols] = l_scr[c, 0:ATTN_TILE, :]
            else:
                for i in range(ATTN_TILE // dil):
                    o_ref[dil * i:dil * (i + 1), cols] = o_scr[c, pitch * i:pitch * i + dil, :].astype(o_ref.dtype)
                    lse_ref[dil * i:dil * (i + 1), cols] = l_scr[c, pitch * i:pitch * i + dil, :]


def _dattn(aproj, bias, *, seq, group, dilation):
    t = aproj.shape[0]
    tiles = seq // ATTN_TILE
    qw = HEADS_PER_GROUP * ATTN_SLAB
    cq, ck, cv = 0, qw // ATTN_GW, qw // ATTN_GW + 1
    blk = (ATTN_TILE, ATTN_GW)
    cur = lambda c: (lambda b, j: (b * tiles + j, c))
    prev = lambda c: (lambda b, j: (b * tiles + jnp.maximum(j - 1, 0), c))
    xrows = ATTN_TILE + dilation * ATTN_BLOCK
    return pl.pallas_call(
        functools.partial(_dattn_kernel, dil=dilation),
        grid=(t // seq, tiles),
        in_specs=[pl.BlockSpec((ATTN_TILE, qw), cur(cq)),
                  pl.BlockSpec(blk, cur(ck)), pl.BlockSpec(blk, prev(ck)),
                  pl.BlockSpec(blk, cur(cv)), pl.BlockSpec(blk, prev(cv)),
                  pl.BlockSpec((HEADS_PER_GROUP, 1, 2 * ATTN_BLOCK), lambda b, j: (0, 0, 0))],
        out_specs=[pl.BlockSpec(blk, cur(0)), pl.BlockSpec(blk, cur(0))],
        out_shape=[jax.ShapeDtypeStruct((t, ATTN_GW), BF16), jax.ShapeDtypeStruct((t, ATTN_GW), F32)],
        scratch_shapes=[pltpu.VMEM((xrows, ATTN_GW), BF16), pltpu.VMEM((xrows, ATTN_GW), BF16),
                        pltpu.VMEM((ATTN_GW // ATTN_SLAB, ATTN_TILE + ATTN_TILE // 16, ATTN_SLAB), F32),
                        pltpu.VMEM((ATTN_GW // ATTN_SLAB, ATTN_TILE + ATTN_TILE // 16, ATTN_SLAB), F32),
                        pltpu.VMEM((HEADS_PER_GROUP, ATTN_BLOCK, 2 * ATTN_BLOCK), F32)],
        compiler_params=_cparams("parallel", "arbitrary"),
        name=f"dattn{group}",
    )(aproj, aproj, aproj, aproj, aproj, bias)


def _rel_bucket(n):
    max_exact = REL_BUCKETS // 2
    nf = jnp.maximum(n, 1).astype(F32)
    log_b = max_exact + (jnp.log(nf / max_exact) / math.log(REL_MAX_DIST / max_exact)
                         * (REL_BUCKETS - max_exact)).astype(jnp.int32)
    return jnp.where(n < max_exact, n, jnp.minimum(log_b, REL_BUCKETS - 1))


def _attn_bias(rel_bias, group):
    window, dilation = ATTN_PATTERNS[group]
    steps = window // dilation
    assert steps == ATTN_BLOCK
    hs = slice(group * HEADS_PER_GROUP, (group + 1) * HEADS_PER_GROUP)
    bucket = _rel_bucket((steps - jnp.arange(steps + 1)) * dilation)
    by_dist = jnp.dot(jax.nn.one_hot(bucket, REL_BUCKETS, dtype=F32), rel_bias[:, hs].astype(F32),
                      precision=lax.Precision.HIGHEST)
    row0 = jnp.concatenate([by_dist, jnp.full((2 * ATTN_BLOCK - steps - 1, HEADS_PER_GROUP), NEG, F32)], axis=0)
    return row0.T[:, None, :]


def _merge_kernel(ya_ref, yb0_ref, yb1_ref, yb2_ref, l0_ref, l1_ref, l2_ref, gu_ref, gv_ref, gate_ref,
                  x_ref, wa_ref, wb_ref, wc_ref, wo_ref, ws_ref, bs_ref, gg_ref, o_ref, yc_scr, *, tm):
    d = x_ref.shape[1]
    l0, l1, l2 = l0_ref[...], l1_ref[...], l2_ref[...]
    mx = jnp.maximum(jnp.maximum(l0, l1), l2)
    e0, e1, e2 = jnp.exp(l0 - mx), jnp.exp(l1 - mx), jnp.exp(l2 - mx)
    inv = 1.0 / (e0 + e1 + e2)
    yb = jnp.concatenate([(yb0_ref[...].astype(F32) * (e0 * inv)).astype(BF16),
                          (yb1_ref[...].astype(F32) * (e1 * inv)).astype(BF16),
                          (yb2_ref[...].astype(F32) * (e2 * inv)).astype(BF16)], axis=-1)

    for j in range(tm // GMLP_CHUNK):
        rows = slice(j * GMLP_CHUNK, (j + 1) * GMLP_CHUNK)
        for g in range(GMLP_GROUPS):
            cols = slice(g * GMLP_GC, (g + 1) * GMLP_GC)
            u = _gelu(gu_ref[rows, cols].astype(F32))
            v = _rms(_gelu(gv_ref[rows, cols].astype(F32)), gg_ref[:, cols])
            mixed = _dot(ws_ref[g], v.astype(BF16)) + bs_ref[g]
            yc_scr[rows, cols] = (u * mixed).astype(BF16)

    def gate2(k):
        return jnp.tanh(0.5 * gate_ref[:, k * d:(k + 1) * d].astype(F32)) + 1.0

    merged2 = gate2(0) * _dot(ya_ref[...], wa_ref[...])
    merged2 = merged2 + gate2(1) * _dot(yb, wb_ref[...])
    merged2 = merged2 + gate2(2) * _dot(yc_scr[...], wc_ref[...])
    o_ref[...] = x_ref[...] + 0.5 * _dot(merged2.astype(BF16), wo_ref[...])


def _merge(ya, ybs, lses, proj, x2d, wa, wb, wc, wo, ws, bsb, gg, *, tm):
    t, d = x2d.shape
    row = lambda c: (lambda i: (i, c))
    full2 = lambda i: (0, 0)
    full3 = lambda i: (0, 0, 0)
    gspec = pl.BlockSpec((tm, ATTN_GW), row(0))
    return pl.pallas_call(
        functools.partial(_merge_kernel, tm=tm),
        grid=(t // tm,),
        in_specs=[pl.BlockSpec((tm, MLSTM_W), row(0)),
                  gspec, gspec, gspec, gspec, gspec, gspec,
                  pl.BlockSpec((tm, GMLP_W), row(OFF_GU // GMLP_W)),
                  pl.BlockSpec((tm, GMLP_W), row(OFF_GV // GMLP_W)),
                  pl.BlockSpec((tm, N_BRANCH * d), row(OFF_GATE // (N_BRANCH * d))),
                  pl.BlockSpec((tm, d), row(0)),
                  pl.BlockSpec(wa.shape, full2), pl.BlockSpec(wb.shape, full2),
                  pl.BlockSpec(wc.shape, full2), pl.BlockSpec(wo.shape, full2),
                  pl.BlockSpec(ws.shape, full3), pl.BlockSpec(bsb.shape, full3),
                  pl.BlockSpec(gg.shape, full2)],
        out_specs=pl.BlockSpec((tm, d), row(0)),
        out_shape=jax.ShapeDtypeStruct((t, d), F32),
        scratch_shapes=[pltpu.VMEM((tm, GMLP_W), BF16)],
        compiler_params=_cparams("parallel"),
        name="merge",
    )(ya, *ybs, *lses, proj, proj, proj, x2d, wa, wb, wc, wo, ws, bsb, gg)


def _memkv_kernel(mem_ref, g_ref, w_ref, gk_ref, k_ref, v_ref):
    dh, w = XATTN_DH, XATTN_W
    kv = _dot(_rms(mem_ref[0], g_ref[...]).astype(BF16), w_ref[...])
    for h in range(XATTN_HEADS):
        sl = slice(h * dh, (h + 1) * dh)
        k_ref[0, :, sl] = _rms(kv[:, sl], gk_ref[...]).astype(k_ref.dtype)
    v_ref[0] = kv[:, w:].astype(v_ref.dtype)


def _memkv(mem, gain, w_kv, gk):
    b, m, d = mem.shape
    full2 = lambda i: (0, 0)
    return pl.pallas_call(
        _memkv_kernel,
        grid=(b,),
        in_specs=[pl.BlockSpec((1, m, d), lambda i: (i, 0, 0)),
                  pl.BlockSpec((1, d), full2),
                  pl.BlockSpec(w_kv.shape, full2),
                  pl.BlockSpec((1, XATTN_DH), full2)],
        out_specs=[pl.BlockSpec((1, m, XATTN_W), lambda i: (i, 0, 0)),
                   pl.BlockSpec((1, m, XATTN_W), lambda i: (i, 0, 0))],
        out_shape=[jax.ShapeDtypeStruct((b, m, XATTN_W), BF16),
                   jax.ShapeDtypeStruct((b, m, XATTN_W), BF16)],
        compiler_params=_cparams("parallel", vmem=VMEM_LIMIT_SMALL),
        name="memkv",
    )(mem, gain, w_kv, gk)


def _route(logits):
    tm = logits.shape[1]
    e = jnp.exp(logits - jnp.max(logits, axis=0, keepdims=True))
    probs = e / jnp.sum(e, axis=0, keepdims=True)
    rowi = lax.broadcasted_iota(jnp.int32, (8, tm), 0)
    real = rowi < EXPERTS_PER_GROUP
    tops = []
    for g in range(N_EXPERT_GROUPS):
        pg = jnp.where(real, probs[8 * g:8 * g + 8, :], -0.5)
        m1 = jnp.max(pg, axis=0, keepdims=True)
        i1 = jnp.min(jnp.where(pg == m1, rowi, 8), axis=0, keepdims=True)
        pg2 = jnp.where(rowi == i1, -1.0, pg)
        m2 = jnp.max(pg2, axis=0, keepdims=True)
        i2 = jnp.min(jnp.where(pg2 == m2, rowi, 8), axis=0, keepdims=True)
        tops.append((m1, i1, m2, i2))
    best = jnp.zeros((1, tm), jnp.int32)
    best_score = tops[0][0] + tops[0][2]
    for g in range(1, N_EXPERT_GROUPS):
        score = tops[g][0] + tops[g][2]
        better = score > best_score
        best = jnp.where(better, g, best)
        best_score = jnp.where(better, score, best_score)
    m1, i1, m2, i2 = tops[0]
    for g in range(1, N_EXPERT_GROUPS):
        m1, i1, m2, i2 = (jnp.where(best == g, new, old) for new, old in zip(tops[g], (m1, i1, m2, i2)))
    tot = m1 + m2
    base = best * EXPERTS_PER_GROUP
    return base + i1, base + i2, m1 / tot, m2 / tot


def _pack_bf16_pairs(x):
    n = x.shape[1] // 2
    hi = lax.bitcast_convert_type(x[:, :n].astype(BF16).astype(F32), jnp.uint32)
    lo = lax.bitcast_convert_type(x[:, n:].astype(BF16).astype(F32), jnp.uint32)
    return hi | (lo >> 16)


def _unpack_bf16_pairs(p):
    hi = lax.bitcast_convert_type(p & jnp.uint32(0xFFFF0000), F32)
    lo = lax.bitcast_convert_type(p << 16, F32)
    return hi, lo


def _store_row_chunks(ref, packed):
    for j in range(ROW_CHUNKS):
        ref[j] = packed[:, j * 128:(j + 1) * 128]


def _load_row_chunks(ref):
    return jnp.concatenate([ref[j] for j in range(ROW_CHUNKS)], axis=-1)


def _xattn_kernel(x_ref, k_ref, v_ref, gx_ref, wq_ref, gq_ref, wo_ref, gf_ref, rw_ref, rb_ref,
                  xo_ref, hf_ref, eidx_ref, wts_ref, *, sub):
    dh = XATTN_DH
    rw = rw_ref[...]
    rw_hi, rw_lo = _split_bf16(rw)
    for s in range(x_ref.shape[0] // sub):
        rows = slice(s * sub, (s + 1) * sub)
        x = x_ref[rows, :]
        q = _dot(_rms(x, gx_ref[...]).astype(BF16), wq_ref[...])
        outs = []
        for h in range(XATTN_HEADS):
            sl = slice(h * dh, (h + 1) * dh)
            q_h = (_rms(q[:, sl], gq_ref[...]) * (dh ** -0.5)).astype(BF16)
            logits = _dot_nt(q_h, k_ref[0, :, sl])
            p = jnp.exp(logits - jnp.max(logits, axis=-1, keepdims=True))
            o = _dot(p.astype(BF16), v_ref[0, :, sl]) / jnp.sum(p, axis=-1, keepdims=True)
            outs.append(o.astype(BF16))
        xn = x + _dot(jnp.concatenate(outs, axis=-1), wo_ref[...])
        xo_ref[rows, :] = xn
        hf = _rms(xn, gf_ref[...])
        packed = _pack_bf16_pairs(hf)
        for j in range(ROW_CHUNKS):
            hf_ref[j, rows, :] = packed[:, j * 128:(j + 1) * 128]
        hf_hi, hf_lo = _split_bf16(hf)
        logits_t = _dot_nt(rw_hi, hf_hi) + _dot_nt(rw_hi, hf_lo) + _dot_nt(rw_lo, hf_hi) + rb_ref[...]
        e1, e2, w1, w2 = _route(logits_t)
        eidx_ref[:, rows] = jnp.concatenate([e1, e2, jnp.zeros((6, sub), jnp.int32)], axis=0)
        wts_ref[:, rows] = jnp.concatenate([w1, w2, jnp.zeros((6, sub), F32)], axis=0)


def _xattn(x2d, k, v, gx, wq, gq, wo, gf, rw_t, rb, *, seq, tm):
    t, d = x2d.shape
    per_b = seq // tm
    full2 = lambda i: (0, 0)
    kv_spec = pl.BlockSpec((1,) + k.shape[1:], lambda i: (i // per_b, 0, 0))
    return pl.pallas_call(
        functools.partial(_xattn_kernel, sub=min(tm, XATTN_SUB)),
        grid=(t // tm,),
        in_specs=[pl.BlockSpec((tm, d), lambda i: (i, 0)), kv_spec, kv_spec,
                  pl.BlockSpec((1, d), full2), pl.BlockSpec(wq.shape, full2),
                  pl.BlockSpec((1, XATTN_DH), full2), pl.BlockSpec(wo.shape, full2),
                  pl.BlockSpec((1, d), full2), pl.BlockSpec(rw_t.shape, full2),
                  pl.BlockSpec(rb.shape, full2)],
        out_specs=[pl.BlockSpec((tm, d), lambda i: (i, 0)),
                   pl.BlockSpec((ROW_CHUNKS, tm, 128), lambda i: (0, i, 0)),
                   pl.BlockSpec((8, tm), lambda i: (0, i)),
                   pl.BlockSpec((8, tm), lambda i: (0, i))],
        out_shape=[jax.ShapeDtypeStruct((t, d), F32),
                   jax.ShapeDtypeStruct((ROW_CHUNKS, t, 128), jnp.uint32),
                   jax.ShapeDtypeStruct((8, t), jnp.int32),
                   jax.ShapeDtypeStruct((8, t), F32)],
        compiler_params=_cparams("parallel"),
        name="xattn_router",
    )(x2d, k, v, gx, wq, gq, wo, gf, rw_t, rb)


def _moe_plan_kernel(eidx_ref, i1_ref, i2_ref, te_ref, na_ref, cnt_scr, carry_scr, *, tb, tm, plane_rows):
    ne = N_EXPERTS
    hp = lax.Precision.HIGHEST
    phase, j = pl.program_id(0), pl.program_id(1)
    rows = lax.broadcasted_iota(jnp.int32, (ne, tb), 0)
    oh1 = rows == eidx_ref[0:1, :]
    oh2 = rows == eidx_ref[1:2, :]
    a = oh1.astype(F32) + oh2.astype(F32)
    blk_cnt = jnp.broadcast_to(jnp.sum(a, axis=1, keepdims=True), cnt_scr.shape)

    @pl.when((phase == 0) & (j == 0))
    def _():
        cnt_scr[...] = jnp.zeros_like(cnt_scr)

    @pl.when(phase == 0)
    def _():
        cnt_scr[...] += blk_cnt

    @pl.when((phase == 1) & (j == 0))
    def _():
        padded = jnp.ceil(cnt_scr[...] * (1.0 / tm)) * tm
        er = lax.broadcasted_iota(jnp.int32, (ne, ne), 0)
        ec = lax.broadcasted_iota(jnp.int32, (ne, ne), 1)
        off = jnp.dot((ec < er).astype(F32), padded, precision=hp, preferred_element_type=F32)
        carry_scr[...] = off
        seg_end = (off + padded)[:, 0:1]
        tile_start = lax.broadcasted_iota(jnp.int32, (ne, te_ref.shape[1]), 1).astype(F32) * tm
        te = jnp.sum((seg_end <= tile_start).astype(F32), axis=0, keepdims=True)
        te_ref[...] = jnp.broadcast_to(jnp.minimum(te, ne - 1.0), te_ref.shape).astype(jnp.int32)
        total = jnp.sum(padded[:, 0:1], axis=0, keepdims=True)
        na_ref[...] = jnp.broadcast_to(total * (1.0 / tm), na_ref.shape).astype(jnp.int32)

    @pl.when(phase == 1)
    def _():
        before = (lax.broadcasted_iota(jnp.int32, (tb, tb), 0)
                  < lax.broadcasted_iota(jnp.int32, (tb, tb), 1)).astype(BF16)
        rank = carry_scr[:, 0:1] + _dot(a.astype(BF16), before)
        d1 = jnp.sum(jnp.where(oh1, rank, 0.0), axis=0, keepdims=True).astype(jnp.int32)
        d2 = jnp.sum(jnp.where(oh2, rank, 0.0), axis=0, keepdims=True).astype(jnp.int32)
        plane = lax.broadcasted_iota(jnp.int32, (8, tb), 0) * plane_rows
        i1_ref[...] = jnp.where(plane < ROW_CHUNKS * plane_rows, plane + d1, 0)
        i2_ref[...] = jnp.where(plane < ROW_CHUNKS * plane_rows, plane + d2, 0)
        carry_scr[...] += blk_cnt


def _moe_plan(eidx, *, tm, n_tiles, tb=PLAN_TB):
    t = eidx.shape[1]
    ntp = -(-n_tiles // 128) * 128
    return pl.pallas_call(
        functools.partial(_moe_plan_kernel, tb=tb, tm=tm, plane_rows=n_tiles * tm),
        grid=(2, t // tb),
        in_specs=[pl.BlockSpec((8, tb), lambda p, j: (0, j))],
        out_specs=[pl.BlockSpec((8, tb), lambda p, j: (0, j * p)),
                   pl.BlockSpec((8, tb), lambda p, j: (0, j * p)),
                   pl.BlockSpec((8, ntp), lambda p, j: (0, 0)),
                   pl.BlockSpec((8, 128), lambda p, j: (0, 0))],
        out_shape=[jax.ShapeDtypeStruct((8, t), jnp.int32),
                   jax.ShapeDtypeStruct((8, t), jnp.int32),
                   jax.ShapeDtypeStruct((8, ntp), jnp.int32),
                   jax.ShapeDtypeStruct((8, 128), jnp.int32)],
        scratch_shapes=[pltpu.VMEM((N_EXPERTS, 128), F32), pltpu.VMEM((N_EXPERTS, 128), F32)],
        compiler_params=_cparams("arbitrary", "arbitrary", vmem=VMEM_LIMIT_SMALL),
        name="moe_plan",
    )(eidx)


def _sc_mesh():
    return plsc.VectorSubcoreMesh(core_axis_name="c", subcore_axis_name="s",
                                  num_cores=SC_CORES, num_subcores=SC_SUBCORES)


def _sc_index_spec(tokens):
    nb = tokens // SC_WINDOW
    return pl.BlockSpec((1, SC_WINDOW), lambda i: (i // nb, i % nb))


def _sc_dispatch(rows, i1, i2, n_out):
    n = rows.shape[0]
    tokens = i1.shape[1]

    @functools.partial(pl.kernel, out_type=jax.ShapeDtypeStruct((n_out, 128), rows.dtype), mesh=_sc_mesh(),
                       name="moe_dispatch")
    def k(x_hbm, i1_hbm, i2_hbm, o_hbm):
        def body(x_vmem, i1_vmem, i2_vmem):
            pltpu.sync_copy(x_vmem, o_hbm.at[i1_vmem.at[0]])
            pltpu.sync_copy(x_vmem, o_hbm.at[i2_vmem.at[0]])

        pltpu.emit_pipeline(
            body, grid=(n // SC_WINDOW,),
            in_specs=[pl.BlockSpec((SC_WINDOW, 128), lambda i: (i, 0)),
                      _sc_index_spec(tokens), _sc_index_spec(tokens)],
            out_specs=[],
            core_axis_name=("c", "s"), dimension_semantics=(pltpu.PARALLEL,),
        )(x_hbm, i1_hbm, i2_hbm)

    return k(rows, i1, i2)


def _sc_collect(table, i1, i2):
    tokens = i1.shape[1]
    n = ROW_CHUNKS * tokens
    out = jax.ShapeDtypeStruct((n, 128), table.dtype)

    @functools.partial(pl.kernel, out_type=(out, out), mesh=_sc_mesh(), name="moe_collect",
                       scratch_types=[pltpu.SemaphoreType.DMA, pltpu.SemaphoreType.DMA])
    def k(t_hbm, i1_hbm, i2_hbm, o1_hbm, o2_hbm, sem1, sem2):
        def body(i1_vmem, i2_vmem, o1_vmem, o2_vmem):
            first = pltpu.async_copy(t_hbm.at[i1_vmem.at[0]], o1_vmem, sem1)
            second = pltpu.async_copy(t_hbm.at[i2_vmem.at[0]], o2_vmem, sem2)
            first.wait()
            second.wait()

        pltpu.emit_pipeline(
            body, grid=(n // SC_WINDOW,),
            in_specs=[_sc_index_spec(tokens), _sc_index_spec(tokens)],
            out_specs=[pl.BlockSpec((SC_WINDOW, 128), lambda i: (i, 0)),
                       pl.BlockSpec((SC_WINDOW, 128), lambda i: (i, 0))],
            core_axis_name=("c", "s"), dimension_semantics=(pltpu.PARALLEL,),
        )(i1_hbm, i2_hbm, o1_hbm, o2_hbm)

    return k(table, i1, i2)


def _experts_kernel(te_ref, na_ref, xs_ref, wg_hbm, wu_hbm, wd_hbm, y_ref, wg_scr, wu_scr, wd_scr,
                    wg_buf, wu_buf, wd_buf, sems, slot_scr, *, layer):
    i = pl.program_id(0)
    n_active = na_ref[0]
    active = i < n_active
    last = te_ref.shape[0] - 1

    def weight_copies(e, slot):
        return [pltpu.make_async_copy(hbm.at[layer, e], buf.at[slot], sems.at[slot, k])
                for k, (hbm, buf) in enumerate(((wg_hbm, wg_buf), (wu_hbm, wu_buf), (wd_hbm, wd_buf)))]

    @pl.when(i == 0)
    def _():
        slot_scr[0] = 1
        for c in weight_copies(te_ref[0], 0):
            c.start()

    @pl.when(active & ((i == 0) | (te_ref[i] != te_ref[jnp.maximum(i - 1, 0)])))
    def _():
        e = te_ref[i]
        slot = 1 - slot_scr[0]
        slot_scr[0] = slot
        for c in weight_copies(e, slot):
            c.wait()
        wg_scr[...] = wg_buf[slot].astype(BF16)
        wu_scr[...] = wu_buf[slot].astype(BF16)
        wd_scr[...] = wd_buf[slot].astype(BF16)
        nxt = lax.while_loop(lambda j: (j < n_active) & (te_ref[jnp.minimum(j, last)] == e),
                             lambda j: j + 1, i + 1)

        @pl.when(nxt < n_active)
        def _():
            for c in weight_copies(te_ref[jnp.minimum(nxt, last)], 1 - slot):
                c.start()

    @pl.when(active)
    def _():
        hi, lo = _unpack_bf16_pairs(_load_row_chunks(xs_ref))
        h = jnp.concatenate([hi, lo], axis=-1).astype(BF16)
        up = _dot(h, wg_scr[...])
        act = _silu(up) * _dot(h, wu_scr[...])
        _store_row_chunks(y_ref, _pack_bf16_pairs(_dot(act.astype(BF16), wd_scr[...])))


def _experts(tile_expert, n_active, xs, wg, wu, wd, *, layer, tm):
    n_tiles = tile_expert.shape[0]
    _, _, d, dff = wg.shape
    rows = lambda i, te, na: (0, jnp.minimum(i, na[0] - 1), 0)
    return pl.pallas_call(
        functools.partial(_experts_kernel, layer=layer),
        grid_spec=pltpu.PrefetchScalarGridSpec(
            num_scalar_prefetch=2,
            grid=(n_tiles,),
            in_specs=[pl.BlockSpec((ROW_CHUNKS, tm, 128), rows),
                      pl.BlockSpec(memory_space=pl.ANY),
                      pl.BlockSpec(memory_space=pl.ANY),
                      pl.BlockSpec(memory_space=pl.ANY)],
            out_specs=pl.BlockSpec((ROW_CHUNKS, tm, 128), rows),
            scratch_shapes=[pltpu.VMEM((d, dff), BF16), pltpu.VMEM((d, dff), BF16), pltpu.VMEM((dff, d), BF16),
                            pltpu.VMEM((2, d, dff), F32), pltpu.VMEM((2, d, dff), F32),
                            pltpu.VMEM((2, dff, d), F32), pltpu.SemaphoreType.DMA((2, 3)),
                            pltpu.SMEM((1,), jnp.int32)]),
        out_shape=jax.ShapeDtypeStruct(xs.shape, xs.dtype),
        compiler_params=_cparams("arbitrary"),
        name="moe_experts",
    )(tile_expert, n_active, xs, wg, wu, wd)


def _moe_combine_kernel(x_ref, y1_ref, y2_ref, w_ref, o_ref):
    half = x_ref.shape[1] // 2
    hi1, lo1 = _unpack_bf16_pairs(_load_row_chunks(y1_ref))
    hi2, lo2 = _unpack_bf16_pairs(_load_row_chunks(y2_ref))
    tm = x_ref.shape[0]
    w_cols = jnp.concatenate([w_ref[...], jnp.zeros((128 - w_ref.shape[0], tm), F32)], axis=0).T
    w1, w2 = w_cols[:, 0:1], w_cols[:, 1:2]
    o_ref[:, :half] = x_ref[:, :half] + w1 * hi1 + w2 * hi2
    o_ref[:, half:] = x_ref[:, half:] + w1 * lo1 + w2 * lo2


def _moe_combine(x2d, y1, y2, wts, *, tm):
    t, d = x2d.shape
    chunk_spec = pl.BlockSpec((ROW_CHUNKS, tm, 128), lambda i: (0, i, 0))
    return pl.pallas_call(
        _moe_combine_kernel,
        grid=(t // tm,),
        in_specs=[pl.BlockSpec((tm, d), lambda i: (i, 0)), chunk_spec, chunk_spec,
                  pl.BlockSpec((wts.shape[0], tm), lambda i: (0, i))],
        out_specs=pl.BlockSpec((tm, d), lambda i: (i, 0)),
        out_shape=jax.ShapeDtypeStruct((t, d), F32),
        compiler_params=_cparams("parallel"),
        name="moe_combine",
    )(x2d, y1, y2, wts)


def _moe(x2d, hf_rows, eidx, wts, wg, wu, wd, *, layer):
    t = x2d.shape[0]
    tm = MOE_TM
    n_tiles = 2 * t // tm + N_EXPERTS
    plane = n_tiles * tm
    i1, i2, te, na = _moe_plan(eidx, tm=tm, n_tiles=n_tiles)
    xs = _sc_dispatch(hf_rows.reshape(ROW_CHUNKS * t, 128), i1, i2, ROW_CHUNKS * plane)
    ys = _experts(te[0, :n_tiles], na[0, :1], xs.reshape(ROW_CHUNKS, plane, 128), wg, wu, wd,
                  layer=layer, tm=tm)
    y1, y2 = _sc_collect(ys.reshape(ROW_CHUNKS * plane, 128), i1, i2)
    return _moe_combine(x2d, y1.reshape(ROW_CHUNKS, t, 128), y2.reshape(ROW_CHUNKS, t, 128), wts,
                        tm=COMBINE_TM)


W_ROWS = 256


def _w_rows_kernel(start_ref, valid_ref, w_ref, o_ref):
    del start_ref
    row = lax.broadcasted_iota(jnp.int32, w_ref.shape, 1)
    o_ref[...] = jnp.where(row < valid_ref[pl.program_id(0)], w_ref[...], 0.0).astype(o_ref.dtype)


def _w_rows(w_t, starts, valid):
    depth, _, d = w_t.shape
    nblk = len(starts)
    return pl.pallas_call(
        _w_rows_kernel,
        grid_spec=pltpu.PrefetchScalarGridSpec(
            num_scalar_prefetch=2,
            grid=(nblk,),
            in_specs=[pl.BlockSpec((pl.Element(depth), pl.Element(W_ROWS), pl.Element(d)),
                                   lambda c, st, va: (0, pl.multiple_of(st[c], 8), 0))],
            out_specs=pl.BlockSpec((depth, W_ROWS, d), lambda c, st, va: (0, c, 0))),
        out_shape=jax.ShapeDtypeStruct((depth, nblk * W_ROWS, d), BF16),
        compiler_params=_cparams("arbitrary", vmem=VMEM_LIMIT_SMALL),
        name="w_in_rows",
    )(jnp.asarray(starts, jnp.int32), jnp.asarray(valid, jnp.int32), w_t)


def _w_in_layout(w_in):
    w_t = jnp.swapaxes(w_in, 1, 2)
    src_if = 4 * MLSTM_W
    src_a = src_if + 2 * MLSTM_HEADS
    src_g = src_a + 3 * ATTN_W
    starts = list(range(0, src_if, W_ROWS)) + [src_g + k * W_ROWS for k in range((OFF_IF - OFF_GU) // W_ROWS)]
    valid = [W_ROWS] * len(starts)
    starts.append(src_if)
    valid.append(2 * MLSTM_HEADS)
    assert len(starts) * W_ROWS == N_PROJ and ATTN_GW == W_ROWS
    a_starts = [src_a + j * ATTN_W + g * ATTN_GW for g in range(len(ATTN_PATTERNS)) for j in range(3)]
    return _w_rows(w_t, starts, valid), _w_rows(w_t, a_starts, [W_ROWS] * len(a_starts))


def kernel(x, mem, norm_mix, w_in, mlstm_conv, mlstm_gate_b, mlstm_norm, attn_qk_norm, gmlp_norm, gmlp_ws,
           gmlp_bs, w_branch_a, w_branch_b, w_branch_c, w_out, rel_bias, norm_xattn, norm_mem, w_xq, w_xkv,
           xattn_qk_norm, w_xo, norm_ffn, router_w, router_b, w_expert_gate, w_expert_up, w_expert_down):
    b, s, d = x.shape
    t = b * s
    depth = w_in.shape[0]
    assert d == 2 * ROW_CHUNKS * 128 and OFF_IF == OFF_GATE + N_BRANCH * d
    assert s % ATTN_TILE == 0 and s % MLSTM_BLOCK == 0 and b % MLSTM_GROUP == 0
    assert all(window == dil * ATTN_BLOCK and ATTN_SUB % dil == 0 for window, dil in ATTN_PATTERNS)
    assert t % max(INPROJ_TM, MERGE_TM, XATTN_TM, COMBINE_TM, MOE_TM, PLAN_TB) == 0 and s % XATTN_TM == 0
    x2d = x.reshape(t, d)

    biases = [_attn_bias(rel_bias, g) for g in range(len(ATTN_PATTERNS))]
    rw_t = jnp.zeros((N_EXPERT_GROUPS, 8, d), F32).at[:, :EXPERTS_PER_GROUP].set(
        router_w.T.reshape(N_EXPERT_GROUPS, EXPERTS_PER_GROUP, d)).reshape(ROUTER_ROWS, d)
    rb = jnp.full((N_EXPERT_GROUPS, 8), NEG, F32).at[:, :EXPERTS_PER_GROUP].set(
        router_b.astype(F32).reshape(N_EXPERT_GROUPS, EXPERTS_PER_GROUP)).reshape(ROUTER_ROWS, 1)
    tril = jnp.tril(jnp.ones((GMLP_CHUNK, GMLP_CHUNK), bool))
    head_of = jnp.arange(ATTN_GW) // ATTN_DH
    seg_ones = (head_of[:, None] == head_of[None, :]).astype(BF16)

    w_main, w_attn = _w_in_layout(w_in)

    for l in range(depth):
        proj, h_mix, gates_t = _inproj(x2d, norm_mix[l][None], w_main, layer=l, tm=INPROJ_TM,
                                       tn=INPROJ_TN)
        gq = jnp.tile(attn_qk_norm[l, 0], HEADS_PER_GROUP)[None]
        gk = jnp.tile(attn_qk_norm[l, 1], HEADS_PER_GROUP)[None]

        nh = MLSTM_HEADS
        bias_i = jnp.zeros((8, 1), F32).at[:nh, 0].set(mlstm_gate_b[l, :nh])
        bias_f = jnp.zeros((8, 1), F32).at[:nh, 0].set(mlstm_gate_b[l, nh:])
        ya = _mlstm_rows(proj, gates_t, mlstm_conv[l], bias_i, bias_f, mlstm_norm[l][None],
                         batch=b, seq=s, blk=MLSTM_BLOCK, group=MLSTM_GROUP)

        ybs, lses = [], []
        for g, (_, dilation) in enumerate(ATTN_PATTERNS):
            aproj = _attnproj(h_mix, w_attn, seg_ones, gq, gk, layer=l, group=g, dilation=dilation)
            o, lse = _dattn(aproj, biases[g], seq=s, group=g, dilation=dilation)
            ybs.append(o)
            lses.append(lse)

        ws = jnp.where(tril, gmlp_ws[l], 0.0).astype(BF16)
        bsb = jnp.broadcast_to(gmlp_bs[l][:, :, None], (GMLP_GROUPS, GMLP_CHUNK, GMLP_GC)).astype(F32)
        x2d = _merge(ya, ybs, lses, proj, x2d, w_branch_a[l].astype(BF16), w_branch_b[l].astype(BF16),
                     w_branch_c[l].astype(BF16), w_out[l].astype(BF16), ws, bsb, gmlp_norm[l][None],
                     tm=MERGE_TM)

        k_mem, v_mem = _memkv(mem, norm_mem[l][None], w_xkv[l].astype(BF16), xattn_qk_norm[l, 1][None])
        x2d, hf_rows, eidx, wts = _xattn(x2d, k_mem, v_mem, norm_xattn[l][None], w_xq[l].astype(BF16),
                                         xattn_qk_norm[l, 0][None], w_xo[l].astype(BF16), norm_ffn[l][None],
                                         rw_t, rb, seq=s, tm=XATTN_TM)

        x2d = _moe(x2d, hf_rows, eidx, wts, w_expert_gate, w_expert_up, w_expert_down, layer=l)

    return x2d.reshape(b, s, d)
```
